```python
import jax, jax.numpy as jnp
from jax import lax
import numpy as np

D_MODEL = 1024
BATCH = 8
SEQ = 4096
DEPTH = 1

D_FF = 2816
POOL_WIDTH = D_MODEL // 2
POOL_WINDOWS = (2, 4, 8, 16)
N_POOL_GROUPS = len(POOL_WINDOWS)
POOL_GROUP = POOL_WIDTH // N_POOL_GROUPS
HEAD_DIM = 64
N_HEADS = D_MODEL // HEAD_DIM
N_KV_HEADS = N_HEADS // 8
GQA_GROUP = N_HEADS // N_KV_HEADS
WINDOW = 128
BLOCK = 128
ATTN_WIDTH = N_HEADS * HEAD_DIM
KV_WIDTH = N_KV_HEADS * HEAD_DIM
N_BRANCHES = 2
IN_WIDTH = POOL_WIDTH + ATTN_WIDTH + 2 * KV_WIDTH + N_BRANCHES * D_MODEL
RMS_EPS = 1e-6

kernel_name = "hybrid_pool_swa_sink_macaron_layer"


def rmsnorm(x, g):
    xf = x.astype(jnp.float32)
    y = xf * lax.rsqrt(jnp.mean(xf * xf, axis=-1, keepdims=True) + RMS_EPS)
    return (y * g.astype(jnp.float32)).astype(x.dtype)


def swiglu(x, w_gate, w_up, w_down):
    return (jax.nn.silu(x @ w_gate) * (x @ w_up)) @ w_down


def causal_pool_mixer(xp, pool_w, pool_scale):
    B, S, P = xp.shape
    xf = xp.astype(jnp.float32)
    csum = jnp.concatenate([jnp.zeros((B, 1, P), jnp.float32), jnp.cumsum(xf, axis=1)], axis=1)
    t = jnp.arange(S)
    pooled = []
    for gi, w in enumerate(POOL_WINDOWS):
        cg = csum[..., gi * POOL_GROUP:(gi + 1) * POOL_GROUP]
        start = jnp.maximum(t + 1 - w, 0)
        window_sum = cg[:, 1:, :] - cg[:, start, :]
        count = jnp.minimum(t + 1, w).astype(jnp.float32)
        pooled.append(window_sum / count[None, :, None])
    pooled = (jnp.concatenate(pooled, axis=-1) - xf).astype(xp.dtype)
    pooled = pooled.reshape(B, S, N_POOL_GROUPS, POOL_GROUP)
    mixed = jnp.einsum('bsgc,gcd->bsgd', pooled, pool_w).reshape(B, S, P)
    return mixed * pool_scale


def sliding_window_sink_attention(q, k, v, q_norm, k_norm, sinks):
    B, S = q.shape[0], q.shape[1]
    nb = S // BLOCK
    q = rmsnorm(q, q_norm)
    k = rmsnorm(k, k_norm)
    qb = q.reshape(B, nb, BLOCK, N_KV_HEADS, GQA_GROUP, HEAD_DIM)
    kb = k.reshape(B, nb, BLOCK, N_KV_HEADS, HEAD_DIM)
    vb = v.reshape(B, nb, BLOCK, N_KV_HEADS, HEAD_DIM)
    pad = ((0, 0), (1, 0), (0, 0), (0, 0), (0, 0))
    kk = jnp.concatenate([jnp.pad(kb, pad)[:, :-1], kb], axis=2)
    vv = jnp.concatenate([jnp.pad(vb, pad)[:, :-1], vb], axis=2)
    scores = jnp.einsum('bnqhgd,bnkhd->bnhgqk', qb, kk).astype(jnp.float32) * (HEAD_DIM ** -0.5)
    qi = jnp.arange(BLOCK)[:, None]
    kj = jnp.arange(2 * BLOCK)[None, :] - BLOCK
    rel = qi - kj
    band = (rel >= 0) & (rel < WINDOW)
    in_seq = (jnp.arange(nb)[:, None, None] > 0) | (kj[None] >= 0)
    mask = band[None] & in_seq
    scores = jnp.where(mask[None, :, None, None], scores, jnp.finfo(jnp.float32).min)
    sink = jnp.broadcast_to(sinks.astype(jnp.float32).reshape(1, 1, N_KV_HEADS, GQA_GROUP, 1, 1),
                            scores.shape[:-1] + (1,))
    probs = jax.nn.softmax(jnp.concatenate([scores, sink], axis=-1), axis=-1)[..., :-1]
    out = jnp.einsum('bnhgqk,bnkhd->bnqhgd', probs.astype(vv.dtype), vv)
    return out.reshape(B, S, ATTN_WIDTH)


def _fwd_setup_inputs(seed: int = 0) -> dict:
    key = jax.random.key(seed)
    ks = jax.random.split(key, 24)
    f32 = jnp.float32

    def nrm(k, shape, fan_in):
        return jax.random.normal(k, shape, f32) * (fan_in ** -0.5)

    def gain(k, shape):
        return jnp.ones(shape, f32) + 0.02 * jax.random.normal(k, shape, f32)

    return {
        "x": jax.random.normal(ks[0], (BATCH, SEQ, D_MODEL), f32),
        "ffn1_norm": gain(ks[1], (D_MODEL,)),
        "ffn1_w_gate": nrm(ks[2], (D_MODEL, D_FF), D_MODEL),
        "ffn1_w_up": nrm(ks[3], (D_MODEL, D_FF), D_MODEL),
        "ffn1_w_down": nrm(ks[4], (D_FF, D_MODEL), D_FF),
        "mix_norm": gain(ks[5], (D_MODEL,)),
        "w_in": nrm(ks[6], (D_MODEL, IN_WIDTH), D_MODEL),
        "pool_w": nrm(ks[7], (N_POOL_GROUPS, POOL_GROUP, POOL_GROUP), POOL_GROUP),
        "pool_scale": gain(ks[8], (POOL_WIDTH,)),
        "w_pool_out": nrm(ks[9], (POOL_WIDTH, D_MODEL), POOL_WIDTH),
        "q_norm": gain(ks[10], (HEAD_DIM,)),
        "k_norm": gain(ks[11], (HEAD_DIM,)),
        "sinks": 0.5 * jax.random.normal(ks[12], (N_HEADS,), f32),
        "w_attn_out": nrm(ks[13], (ATTN_WIDTH, D_MODEL), ATTN_WIDTH),
        "gate_bias": 0.01 * jax.random.normal(ks[14], (N_BRANCHES * D_MODEL,), f32),
        "w_out": nrm(ks[15], (D_MODEL, D_MODEL), D_MODEL),
        "ffn2_norm": gain(ks[16], (D_MODEL,)),
        "ffn2_w_gate": nrm(ks[17], (D_MODEL, D_FF), D_MODEL),
        "ffn2_w_up": nrm(ks[18], (D_MODEL, D_FF), D_MODEL),
        "ffn2_w_down": nrm(ks[19], (D_FF, D_MODEL), D_FF),
    }


def _fwd_reference(x, ffn1_norm, ffn1_w_gate, ffn1_w_up, ffn1_w_down, mix_norm, w_in, pool_w,
              pool_scale, w_pool_out, q_norm, k_norm, sinks, w_attn_out, gate_bias, w_out,
              ffn2_norm, ffn2_w_gate, ffn2_w_up, ffn2_w_down):
    B, S, _ = x.shape
    h = x
    for _layer in range(DEPTH):
        h = h + 0.5 * swiglu(rmsnorm(h, ffn1_norm), ffn1_w_gate, ffn1_w_up, ffn1_w_down)
        u = rmsnorm(h, mix_norm)
        proj = u @ w_in
        o = 0
        xp = proj[..., o:o + POOL_WIDTH]; o += POOL_WIDTH
        q = proj[..., o:o + ATTN_WIDTH].reshape(B, S, N_HEADS, HEAD_DIM); o += ATTN_WIDTH
        k = proj[..., o:o + KV_WIDTH].reshape(B, S, N_KV_HEADS, HEAD_DIM); o += KV_WIDTH
        v = proj[..., o:o + KV_WIDTH].reshape(B, S, N_KV_HEADS, HEAD_DIM); o += KV_WIDTH
        gates = jax.nn.sigmoid(proj[..., o:o + N_BRANCHES * D_MODEL] + gate_bias)
        gate_pool = gates[..., :D_MODEL]
        gate_attn = gates[..., D_MODEL:]
        branch_pool = causal_pool_mixer(xp, pool_w, pool_scale) @ w_pool_out
        branch_attn = sliding_window_sink_attention(q, k, v, q_norm, k_norm, sinks) @ w_attn_out
        merged = gate_pool * branch_pool + gate_attn * branch_attn
        h = h + merged @ w_out
        h = h + 0.5 * swiglu(rmsnorm(h, ffn2_norm), ffn2_w_gate, ffn2_w_up, ffn2_w_down)
    return h


import jax as _jax
import jax.numpy as _jnp

TWIN_FORMAT = 'train_step'
FWD_PARAMS = ['x', 'ffn1_norm', 'ffn1_w_gate', 'ffn1_w_up', 'ffn1_w_down', 'mix_norm', 'w_in', 'pool_w', 'pool_scale', 'w_pool_out', 'q_norm', 'k_norm', 'sinks', 'w_attn_out', 'gate_bias', 'w_out', 'ffn2_norm', 'ffn2_w_gate', 'ffn2_w_up', 'ffn2_w_down']
TWIN_WEIGHTS = ['ffn1_norm', 'ffn1_w_gate', 'ffn1_w_up', 'ffn1_w_down', 'mix_norm', 'w_in', 'pool_w', 'pool_scale', 'w_pool_out', 'q_norm', 'k_norm', 'sinks', 'w_attn_out', 'gate_bias', 'w_out', 'ffn2_norm', 'ffn2_w_gate', 'ffn2_w_up', 'ffn2_w_down']
TWIN_DIFF_INPUT = 'x'
TWIN_INPUTS = ['x', 'ffn1_norm', 'ffn1_w_gate', 'ffn1_w_up', 'ffn1_w_down', 'mix_norm', 'w_in', 'pool_w', 'pool_scale', 'w_pool_out', 'q_norm', 'k_norm', 'sinks', 'w_attn_out', 'gate_bias', 'w_out', 'ffn2_norm', 'ffn2_w_gate', 'ffn2_w_up', 'ffn2_w_down', 'loss_target', 'm_ffn1_norm', 'm_ffn1_w_gate', 'm_ffn1_w_up', 'm_ffn1_w_down', 'm_mix_norm', 'm_w_in', 'm_pool_w', 'm_pool_scale', 'm_w_pool_out', 'm_q_norm', 'm_k_norm', 'm_sinks', 'm_w_attn_out', 'm_gate_bias', 'm_w_out', 'm_ffn2_norm', 'm_ffn2_w_gate', 'm_ffn2_w_up', 'm_ffn2_w_down', 'v_ffn1_norm', 'v_ffn1_w_gate', 'v_ffn1_w_up', 'v_ffn1_w_down', 'v_mix_norm', 'v_w_in', 'v_pool_w', 'v_pool_scale', 'v_w_pool_out', 'v_q_norm', 'v_k_norm', 'v_sinks', 'v_w_attn_out', 'v_gate_bias', 'v_w_out', 'v_ffn2_norm', 'v_ffn2_w_gate', 'v_ffn2_w_up', 'v_ffn2_w_down']
TWIN_OUTPUTS = ['loss', 'grad_x', 'grad_ffn1_norm', 'grad_ffn1_w_gate', 'grad_ffn1_w_up', 'grad_ffn1_w_down', 'grad_mix_norm', 'grad_w_in', 'grad_pool_w', 'grad_pool_scale', 'grad_w_pool_out', 'grad_q_norm', 'grad_k_norm', 'grad_sinks', 'grad_w_attn_out', 'grad_gate_bias', 'grad_w_out', 'grad_ffn2_norm', 'grad_ffn2_w_gate', 'grad_ffn2_w_up', 'grad_ffn2_w_down', 'delta_ffn1_norm', 'delta_ffn1_w_gate', 'delta_ffn1_w_up', 'delta_ffn1_w_down', 'delta_mix_norm', 'delta_w_in', 'delta_pool_w', 'delta_pool_scale', 'delta_w_pool_out', 'delta_q_norm', 'delta_k_norm', 'delta_sinks', 'delta_w_attn_out', 'delta_gate_bias', 'delta_w_out', 'delta_ffn2_norm', 'delta_ffn2_w_gate', 'delta_ffn2_w_up', 'delta_ffn2_w_down', 'new_m_ffn1_norm', 'new_m_ffn1_w_gate', 'new_m_ffn1_w_up', 'new_m_ffn1_w_down', 'new_m_mix_norm', 'new_m_w_in', 'new_m_pool_w', 'new_m_pool_scale', 'new_m_w_pool_out', 'new_m_q_norm', 'new_m_k_norm', 'new_m_sinks', 'new_m_w_attn_out', 'new_m_gate_bias', 'new_m_w_out', 'new_m_ffn2_norm', 'new_m_ffn2_w_gate', 'new_m_ffn2_w_up', 'new_m_ffn2_w_down', 'new_v_ffn1_norm', 'new_v_ffn1_w_gate', 'new_v_ffn1_w_up', 'new_v_ffn1_w_down', 'new_v_mix_norm', 'new_v_w_in', 'new_v_pool_w', 'new_v_pool_scale', 'new_v_w_pool_out', 'new_v_q_norm', 'new_v_k_norm', 'new_v_sinks', 'new_v_w_attn_out', 'new_v_gate_bias', 'new_v_w_out', 'new_v_ffn2_norm', 'new_v_ffn2_w_gate', 'new_v_ffn2_w_up', 'new_v_ffn2_w_down']
TWIN_LEAF_KINDS = {'loss': 'loss', 'grad_x': 'grad_x', 'grad_ffn1_norm': 'grad_w', 'grad_ffn1_w_gate': 'grad_w', 'grad_ffn1_w_up': 'grad_w', 'grad_ffn1_w_down': 'grad_w', 'grad_mix_norm': 'grad_w', 'grad_w_in': 'grad_w', 'grad_pool_w': 'grad_w', 'grad_pool_scale': 'grad_w', 'grad_w_pool_out': 'grad_w', 'grad_q_norm': 'grad_w', 'grad_k_norm': 'grad_w', 'grad_sinks': 'grad_w', 'grad_w_attn_out': 'grad_w', 'grad_gate_bias': 'grad_w', 'grad_w_out': 'grad_w', 'grad_ffn2_norm': 'grad_w', 'grad_ffn2_w_gate': 'grad_w', 'grad_ffn2_w_up': 'grad_w', 'grad_ffn2_w_down': 'grad_w', 'delta_ffn1_norm': 'delta_w', 'delta_ffn1_w_gate': 'delta_w', 'delta_ffn1_w_up': 'delta_w', 'delta_ffn1_w_down': 'delta_w', 'delta_mix_norm': 'delta_w', 'delta_w_in': 'delta_w', 'delta_pool_w': 'delta_w', 'delta_pool_scale': 'delta_w', 'delta_w_pool_out': 'delta_w', 'delta_q_norm': 'delta_w', 'delta_k_norm': 'delta_w', 'delta_sinks': 'delta_w', 'delta_w_attn_out': 'delta_w', 'delta_gate_bias': 'delta_w', 'delta_w_out': 'delta_w', 'delta_ffn2_norm': 'delta_w', 'delta_ffn2_w_gate': 'delta_w', 'delta_ffn2_w_up': 'delta_w', 'delta_ffn2_w_down': 'delta_w', 'new_m_ffn1_norm': 'new_m', 'new_m_ffn1_w_gate': 'new_m', 'new_m_ffn1_w_up': 'new_m', 'new_m_ffn1_w_down': 'new_m', 'new_m_mix_norm': 'new_m', 'new_m_w_in': 'new_m', 'new_m_pool_w': 'new_m', 'new_m_pool_scale': 'new_m', 'new_m_w_pool_out': 'new_m', 'new_m_q_norm': 'new_m', 'new_m_k_norm': 'new_m', 'new_m_sinks': 'new_m', 'new_m_w_attn_out': 'new_m', 'new_m_gate_bias': 'new_m', 'new_m_w_out': 'new_m', 'new_m_ffn2_norm': 'new_m', 'new_m_ffn2_w_gate': 'new_m', 'new_m_ffn2_w_up': 'new_m', 'new_m_ffn2_w_down': 'new_m', 'new_v_ffn1_norm': 'new_v', 'new_v_ffn1_w_gate': 'new_v', 'new_v_ffn1_w_up': 'new_v', 'new_v_ffn1_w_down': 'new_v', 'new_v_mix_norm': 'new_v', 'new_v_w_in': 'new_v', 'new_v_pool_w': 'new_v', 'new_v_pool_scale': 'new_v', 'new_v_w_pool_out': 'new_v', 'new_v_q_norm': 'new_v', 'new_v_k_norm': 'new_v', 'new_v_sinks': 'new_v', 'new_v_w_attn_out': 'new_v', 'new_v_gate_bias': 'new_v', 'new_v_w_out': 'new_v', 'new_v_ffn2_norm': 'new_v', 'new_v_ffn2_w_gate': 'new_v', 'new_v_ffn2_w_up': 'new_v', 'new_v_ffn2_w_down': 'new_v'}


def _forward(args):
    return _fwd_reference(*[args[k] for k in FWD_PARAMS])


def _output_shape():
    def fwd():
        inp = _fwd_setup_inputs(0)
        return _fwd_reference(*[inp[k] for k in FWD_PARAMS])
    out = _jax.eval_shape(fwd)
    return out.shape, out.dtype

N_MICROBATCH = 1
ADAM_LR = 0.001
ADAM_B1 = 0.9
ADAM_B2 = 0.999
ADAM_EPS = 1e-08
ADAM_WD = 0.01
ADAM_STEP = 10
PER_EXAMPLE_BATCH_AXIS = {'x': 0, 'loss_target': 0}
SHARED_INPUTS = []
_WEIGHT_DTYPES = {'ffn1_norm': _jnp.float32, 'ffn1_w_gate': _jnp.float32, 'ffn1_w_up': _jnp.float32, 'ffn1_w_down': _jnp.float32, 'mix_norm': _jnp.float32, 'w_in': _jnp.float32, 'pool_w': _jnp.float32, 'pool_scale': _jnp.float32, 'w_pool_out': _jnp.float32, 'q_norm': _jnp.float32, 'k_norm': _jnp.float32, 'sinks': _jnp.float32, 'w_attn_out': _jnp.float32, 'gate_bias': _jnp.float32, 'w_out': _jnp.float32, 'ffn2_norm': _jnp.float32, 'ffn2_w_gate': _jnp.float32, 'ffn2_w_up': _jnp.float32, 'ffn2_w_down': _jnp.float32}
MOMENT_SCALE = {'ffn1_norm': 6.180956e+00, 'ffn1_w_gate': 7.160844e-02, 'ffn1_w_up': 8.755621e-02, 'ffn1_w_down': 1.458714e-01, 'mix_norm': 8.123698e+00, 'w_in': 3.076749e-01, 'pool_w': 1.586896e+00, 'pool_scale': 1.473375e+01, 'w_pool_out': 7.231841e-01, 'q_norm': 2.048663e+00, 'k_norm': 2.053583e+00, 'sinks': 3.165738e-01, 'w_attn_out': 3.753521e-02, 'gate_bias': 1.800034e+00, 'w_out': 5.784961e-01, 'ffn2_norm': 6.151655e+00, 'ffn2_w_gate': 6.210794e-02, 'ffn2_w_up': 8.482397e-02, 'ffn2_w_down': 1.362981e-01}


def _to_microbatches(a, axis):
    t = _jnp.moveaxis(a, axis, 0)
    t = t.reshape((N_MICROBATCH, t.shape[0] // N_MICROBATCH) + t.shape[1:])
    return _jnp.moveaxis(t, 1, axis + 1)


def setup_inputs(seed: int = 0) -> dict:
    inp = _fwd_setup_inputs(seed)
    key = _jax.random.fold_in(_jax.random.key(seed), 7919)
    shape, _ = _output_shape()
    out = dict(inp)
    out["loss_target"] = _jax.random.normal(_jax.random.fold_in(key, 0), shape, _jnp.float32)
    for i, name in enumerate(TWIN_WEIGHTS):
        w = inp[name].astype(_jnp.float32)
        if MOMENT_SCALE is None:
            s = _jnp.sqrt(_jnp.mean(_jnp.square(w)) + 1e-30)
        else:
            s = MOMENT_SCALE[name]
        km, kv = _jax.random.split(_jax.random.fold_in(key, i + 1))
        out[name] = w
        out["m_" + name] = s * _jax.random.normal(km, w.shape, _jnp.float32)
        out["v_" + name] = (s * s) * _jax.random.uniform(kv, w.shape, _jnp.float32, 0.5, 1.5)
    if N_MICROBATCH > 1:
        for name, axis in PER_EXAMPLE_BATCH_AXIS.items():
            out[name] = _to_microbatches(out[name], axis)
    return {'x': out['x'], 'ffn1_norm': out['ffn1_norm'], 'ffn1_w_gate': out['ffn1_w_gate'], 'ffn1_w_up': out['ffn1_w_up'], 'ffn1_w_down': out['ffn1_w_down'], 'mix_norm': out['mix_norm'], 'w_in': out['w_in'], 'pool_w': out['pool_w'], 'pool_scale': out['pool_scale'], 'w_pool_out': out['w_pool_out'], 'q_norm': out['q_norm'], 'k_norm': out['k_norm'], 'sinks': out['sinks'], 'w_attn_out': out['w_attn_out'], 'gate_bias': out['gate_bias'], 'w_out': out['w_out'], 'ffn2_norm': out['ffn2_norm'], 'ffn2_w_gate': out['ffn2_w_gate'], 'ffn2_w_up': out['ffn2_w_up'], 'ffn2_w_down': out['ffn2_w_down'], 'loss_target': out['loss_target'], 'm_ffn1_norm': out['m_ffn1_norm'], 'm_ffn1_w_gate': out['m_ffn1_w_gate'], 'm_ffn1_w_up': out['m_ffn1_w_up'], 'm_ffn1_w_down': out['m_ffn1_w_down'], 'm_mix_norm': out['m_mix_norm'], 'm_w_in': out['m_w_in'], 'm_pool_w': out['m_pool_w'], 'm_pool_scale': out['m_pool_scale'], 'm_w_pool_out': out['m_w_pool_out'], 'm_q_norm': out['m_q_norm'], 'm_k_norm': out['m_k_norm'], 'm_sinks': out['m_sinks'], 'm_w_attn_out': out['m_w_attn_out'], 'm_gate_bias': out['m_gate_bias'], 'm_w_out': out['m_w_out'], 'm_ffn2_norm': out['m_ffn2_norm'], 'm_ffn2_w_gate': out['m_ffn2_w_gate'], 'm_ffn2_w_up': out['m_ffn2_w_up'], 'm_ffn2_w_down': out['m_ffn2_w_down'], 'v_ffn1_norm': out['v_ffn1_norm'], 'v_ffn1_w_gate': out['v_ffn1_w_gate'], 'v_ffn1_w_up': out['v_ffn1_w_up'], 'v_ffn1_w_down': out['v_ffn1_w_down'], 'v_mix_norm': out['v_mix_norm'], 'v_w_in': out['v_w_in'], 'v_pool_w': out['v_pool_w'], 'v_pool_scale': out['v_pool_scale'], 'v_w_pool_out': out['v_w_pool_out'], 'v_q_norm': out['v_q_norm'], 'v_k_norm': out['v_k_norm'], 'v_sinks': out['v_sinks'], 'v_w_attn_out': out['v_w_attn_out'], 'v_gate_bias': out['v_gate_bias'], 'v_w_out': out['v_w_out'], 'v_ffn2_norm': out['v_ffn2_norm'], 'v_ffn2_w_gate': out['v_ffn2_w_gate'], 'v_ffn2_w_up': out['v_ffn2_w_up'], 'v_ffn2_w_down': out['v_ffn2_w_down']}


def _loss(weights, diff, rest, loss_target):
    with _jax.named_scope("forward"):
        args = {**rest, TWIN_DIFF_INPUT: diff, **{k: w.astype(_WEIGHT_DTYPES[k]) for k, w in weights.items()}}
        y = _forward(args)
    with _jax.named_scope("loss_head"):
        err = _jnp.square(y.astype(_jnp.float32) - loss_target)
        return 0.5 * _jnp.sum(_jnp.mean(err, axis=-1)) if err.ndim else 0.5 * err


def _adamw(w, g, m, v):
    m = ADAM_B1 * m + (1.0 - ADAM_B1) * g
    v = ADAM_B2 * v + (1.0 - ADAM_B2) * _jnp.square(g)
    m_hat = m / (1.0 - ADAM_B1 ** ADAM_STEP)
    v_hat = v / (1.0 - ADAM_B2 ** ADAM_STEP)
    delta = -ADAM_LR * (m_hat / (_jnp.sqrt(v_hat) + ADAM_EPS) + ADAM_WD * w)
    return delta, m, v


def reference(x, ffn1_norm, ffn1_w_gate, ffn1_w_up, ffn1_w_down, mix_norm, w_in, pool_w, pool_scale, w_pool_out, q_norm, k_norm, sinks, w_attn_out, gate_bias, w_out, ffn2_norm, ffn2_w_gate, ffn2_w_up, ffn2_w_down, loss_target, m_ffn1_norm, m_ffn1_w_gate, m_ffn1_w_up, m_ffn1_w_down, m_mix_norm, m_w_in, m_pool_w, m_pool_scale, m_w_pool_out, m_q_norm, m_k_norm, m_sinks, m_w_attn_out, m_gate_bias, m_w_out, m_ffn2_norm, m_ffn2_w_gate, m_ffn2_w_up, m_ffn2_w_down, v_ffn1_norm, v_ffn1_w_gate, v_ffn1_w_up, v_ffn1_w_down, v_mix_norm, v_w_in, v_pool_w, v_pool_scale, v_w_pool_out, v_q_norm, v_k_norm, v_sinks, v_w_attn_out, v_gate_bias, v_w_out, v_ffn2_norm, v_ffn2_w_gate, v_ffn2_w_up, v_ffn2_w_down):
    given = dict(x=x, ffn1_norm=ffn1_norm, ffn1_w_gate=ffn1_w_gate, ffn1_w_up=ffn1_w_up, ffn1_w_down=ffn1_w_down, mix_norm=mix_norm, w_in=w_in, pool_w=pool_w, pool_scale=pool_scale, w_pool_out=w_pool_out, q_norm=q_norm, k_norm=k_norm, sinks=sinks, w_attn_out=w_attn_out, gate_bias=gate_bias, w_out=w_out, ffn2_norm=ffn2_norm, ffn2_w_gate=ffn2_w_gate, ffn2_w_up=ffn2_w_up, ffn2_w_down=ffn2_w_down, loss_target=loss_target, m_ffn1_norm=m_ffn1_norm, m_ffn1_w_gate=m_ffn1_w_gate, m_ffn1_w_up=m_ffn1_w_up, m_ffn1_w_down=m_ffn1_w_down, m_mix_norm=m_mix_norm, m_w_in=m_w_in, m_pool_w=m_pool_w, m_pool_scale=m_pool_scale, m_w_pool_out=m_w_pool_out, m_q_norm=m_q_norm, m_k_norm=m_k_norm, m_sinks=m_sinks, m_w_attn_out=m_w_attn_out, m_gate_bias=m_gate_bias, m_w_out=m_w_out, m_ffn2_norm=m_ffn2_norm, m_ffn2_w_gate=m_ffn2_w_gate, m_ffn2_w_up=m_ffn2_w_up, m_ffn2_w_down=m_ffn2_w_down, v_ffn1_norm=v_ffn1_norm, v_ffn1_w_gate=v_ffn1_w_gate, v_ffn1_w_up=v_ffn1_w_up, v_ffn1_w_down=v_ffn1_w_down, v_mix_norm=v_mix_norm, v_w_in=v_w_in, v_pool_w=v_pool_w, v_pool_scale=v_pool_scale, v_w_pool_out=v_w_pool_out, v_q_norm=v_q_norm, v_k_norm=v_k_norm, v_sinks=v_sinks, v_w_attn_out=v_w_attn_out, v_gate_bias=v_gate_bias, v_w_out=v_w_out, v_ffn2_norm=v_ffn2_norm, v_ffn2_w_gate=v_ffn2_w_gate, v_ffn2_w_up=v_ffn2_w_up, v_ffn2_w_down=v_ffn2_w_down)
    weights = {n: given[n] for n in TWIN_WEIGHTS}
    shared = {n: given[n] for n in SHARED_INPUTS}
    per_example = {n: given[n] for n in ['x']}
    grad_fn = _jax.value_and_grad(_loss, argnums=(0, 1))

    def one_microbatch(ex, loss_target):
        ex = dict(ex)
        diff = ex.pop(TWIN_DIFF_INPUT)
        return grad_fn(weights, diff, {**shared, **ex}, loss_target)

    if N_MICROBATCH == 1:
        loss, (grad_w, grad_x) = one_microbatch(per_example, given["loss_target"])
    else:
        def body(carry, xs):
            loss_sum, grad_sum = carry
            l_k, (gw_k, gx_k) = one_microbatch(xs[0], xs[1])
            with _jax.named_scope("update"):
                return (loss_sum + l_k, _jax.tree.map(_jnp.add, grad_sum, gw_k)), gx_k

        init = (_jnp.zeros((), _jnp.float32), _jax.tree.map(_jnp.zeros_like, weights))
        (loss, grad_w), grad_x = _jax.lax.scan(body, init, (per_example, given["loss_target"]))
    with _jax.named_scope("update"):
        delta_w, new_m, new_v = {}, {}, {}
        for n in TWIN_WEIGHTS:
            delta_w[n], new_m[n], new_v[n] = _adamw(weights[n], grad_w[n], given["m_" + n], given["v_" + n])
    return (loss, grad_x, *[grad_w[n] for n in TWIN_WEIGHTS], *[delta_w[n] for n in TWIN_WEIGHTS],
            *[new_m[n] for n in TWIN_WEIGHTS], *[new_v[n] for n in TWIN_WEIGHTS])
```

```python
import functools

import jax
import jax.numpy as jnp
from jax import lax
from jax.experimental import pallas as pl
from jax.experimental.pallas import tpu as pltpu

BF = jnp.bfloat16
F32 = jnp.float32

D_MODEL = 1024
D_FF = 2816
POOL_WIDTH = 512
POOL_GROUP = 128
N_POOL_GROUPS = 4
HEAD_DIM = 64
N_HEADS = 16
GQA_GROUP = 8
BLOCK = 128
ATTN_WIDTH = 1024
KV_WIDTH = 128
IN_WIDTH = 3840
RMS_EPS = 1e-6
N_DEV = 8
LANES = 128

COL_Q = POOL_WIDTH
COL_K = COL_Q + ATTN_WIDTH
COL_V = COL_K + KV_WIDTH
COL_GP = COL_V + KV_WIDTH
COL_GA = COL_GP + D_MODEL

ADAM_LR = 0.001
ADAM_B1 = 0.9
ADAM_B2 = 0.999
ADAM_EPS = 1e-08
ADAM_WD = 0.01
ADAM_STEP = 10

VMEM_LIMIT_V7X = 56 * 1024 * 1024
MESH = pl.DeviceIdType.MESH
ANY = pl.BlockSpec(memory_space=pl.ANY)


def _params(sem=None):
    return pltpu.CompilerParams(dimension_semantics=sem, vmem_limit_bytes=VMEM_LIMIT_V7X)


_DIMS = {"nt": (((1,), (1,)), ((), ())), "nn": (((1,), (0,)), ((), ())), "tn": (((0,), (0,)), ((), ()))}


def _mm(name, terms, out_dtypes, *, tm, tn, tk, epilogue=None, extras=(), n_colsum=0):
    a0, b0, mode0, _ = terms[0]
    if mode0 == "nt":
        (M, K), N = a0.shape, b0.shape[0]
    elif mode0 == "nn":
        (M, K), N = a0.shape, b0.shape[1]
    else:
        (K, M), N = a0.shape, b0.shape[1]
    tm, tn, tk = min(tm, M), min(tn, N), min(tk, K)
    assert M % tm == 0 and N % tn == 0 and K % tk == 0, (name, M, N, K, tm, tn, tk)
    nI, nJ, nK = M // tm, N // tn, K // tk
    n_terms = len(terms)
    n_acc = max(t[3] for t in terms) + 1
    n_ex = len(extras)
    n_out = len(out_dtypes)
    if epilogue is None:
        epilogue = lambda accs, ex: ([accs[0]], [])

    def body(*refs):
        ab = refs[: 2 * n_terms]
        ex_refs = refs[2 * n_terms: 2 * n_terms + n_ex]
        out_refs = refs[2 * n_terms + n_ex: 2 * n_terms + n_ex + n_out]
        cs_refs = refs[2 * n_terms + n_ex + n_out: 2 * n_terms + n_ex + n_out + n_colsum]
        acc_refs = refs[2 * n_terms + n_ex + n_out + n_colsum:]

        def products():
            accs = [None] * n_acc
            for t, (_, _, mode, ai) in enumerate(terms):
                p = lax.dot_general(ab[2 * t][...], ab[2 * t + 1][...], _DIMS[mode], preferred_element_type=F32)
                accs[ai] = p if accs[ai] is None else accs[ai] + p
            return accs

        def finish(accs):
            outs, colsums = epilogue(accs, [r[...] for r in ex_refs])
            for r, o in zip(out_refs, outs):
                r[...] = o.astype(r.dtype)
            for r, cs in zip(cs_refs, colsums):
                r[...] = jnp.sum(cs, axis=0, keepdims=True).reshape(r.shape)

        if nK == 1:
            finish(products())
        else:
            k = pl.program_id(2)
            accs = products()

            @pl.when(k == 0)
            def _():
                for r, a in zip(acc_refs, accs):
                    r[...] = a

            @pl.when(k > 0)
            def _():
                for r, a in zip(acc_refs, accs):
                    r[...] += a

            @pl.when(k == nK - 1)
            def _():
                finish([r[...] for r in acc_refs])

    in_specs, args = [], []
    for a, b, mode, _ in terms:
        if mode == "nt":
            in_specs += [pl.BlockSpec((tm, tk), lambda i, j, k: (i, k)), pl.BlockSpec((tn, tk), lambda i, j, k: (j, k))]
        elif mode == "nn":
            in_specs += [pl.BlockSpec((tm, tk), lambda i, j, k: (i, k)), pl.BlockSpec((tk, tn), lambda i, j, k: (k, j))]
        else:
            in_specs += [pl.BlockSpec((tk, tm), lambda i, j, k: (k, i)), pl.BlockSpec((tk, tn), lambda i, j, k: (k, j))]
        args += [a, b]
    for arr, kind, off in extras:
        if kind == "tile":
            in_specs.append(pl.BlockSpec((tm, tn), functools.partial(lambda i, j, k, off: (i, j + off), off=off)))
        else:
            in_specs.append(pl.BlockSpec((1, tn), functools.partial(lambda i, j, k, off: (0, j + off), off=off)))
        args.append(arr)
    out_shape = [jax.ShapeDtypeStruct((M, N), dt) for dt in out_dtypes]
    out_specs = [pl.BlockSpec((tm, tn), lambda i, j, k: (i, j)) for _ in out_dtypes]
    out_shape += [jax.ShapeDtypeStruct((nI, 1, N), F32) for _ in range(n_colsum)]
    out_specs += [pl.BlockSpec((1, 1, tn), lambda i, j, k: (i, 0, j)) for _ in range(n_colsum)]
    scratch = [pltpu.VMEM((tm, tn), F32) for _ in range(n_acc)] if nK > 1 else []
    res = pl.pallas_call(
        body, name=name, grid=(nI, nJ, nK), in_specs=in_specs, out_specs=out_specs, out_shape=out_shape,
        scratch_shapes=scratch, compiler_params=_params(("parallel", "parallel", "arbitrary")),
    )(*args)
    return res


ROW_TILE = 512


def _rms_fwd(name, x, g):
    T, D = x.shape

    def body(x_ref, g_ref, o_ref):
        xv = x_ref[...]
        r = lax.rsqrt(jnp.mean(xv * xv, axis=-1, keepdims=True) + RMS_EPS)
        o_ref[...] = (xv * r * g_ref[...]).astype(BF)

    return pl.pallas_call(
        body, name=name, grid=(T // ROW_TILE,),
        in_specs=[pl.BlockSpec((ROW_TILE, D), lambda i: (i, 0)), pl.BlockSpec((1, D), lambda i: (0, 0))],
        out_specs=pl.BlockSpec((ROW_TILE, D), lambda i: (i, 0)),
        out_shape=jax.ShapeDtypeStruct((T, D), BF), compiler_params=_params(("parallel",)),
    )(x, g)


def _rms_bwd(name, dn, x, g, dres):
    T, D = x.shape

    def body(dn_ref, x_ref, g_ref, dres_ref, dx_ref, dxb_ref, dg_ref):
        xv = x_ref[...]
        r = lax.rsqrt(jnp.mean(xv * xv, axis=-1, keepdims=True) + RMS_EPS)
        xhat = xv * r
        dnv = dn_ref[...]
        dxhat = dnv * g_ref[...]
        dx = dres_ref[...] + r * (dxhat - xhat * jnp.mean(dxhat * xhat, axis=-1, keepdims=True))
        dx_ref[...] = dx
        dxb_ref[...] = dx.astype(BF)
        dg_ref[...] = jnp.sum(dnv * xhat, axis=0, keepdims=True).reshape(dg_ref.shape)

    row = pl.BlockSpec((ROW_TILE, D), lambda i: (i, 0))
    return pl.pallas_call(
        body, name=name, grid=(T // ROW_TILE,),
        in_specs=[row, row, pl.BlockSpec((1, D), lambda i: (0, 0)), row],
        out_specs=[row, row, pl.BlockSpec((1, 1, D), lambda i: (i, 0, 0))],
        out_shape=[jax.ShapeDtypeStruct((T, D), F32), jax.ShapeDtypeStruct((T, D), BF),
                   jax.ShapeDtypeStruct((T // ROW_TILE, 1, D), F32)],
        compiler_params=_params(("parallel",)),
    )(dn, x, g, dres)


def _loss_head(name, y, target):
    T, D = y.shape

    def body(y_ref, t_ref, dy_ref, dyb_ref, sq_ref):
        d = y_ref[...] - t_ref[...]
        dy = d * (1.0 / D)
        dy_ref[...] = dy
        dyb_ref[...] = dy.astype(BF)
        sq_ref[...] = jnp.sum(d * d, axis=0, keepdims=True).reshape(sq_ref.shape)

    row = pl.BlockSpec((ROW_TILE, D), lambda i: (i, 0))
    return pl.pallas_call(
        body, name=name, grid=(T // ROW_TILE,), in_specs=[row, row],
        out_specs=[row, row, pl.BlockSpec((1, 1, D), lambda i: (i, 0, 0))],
        out_shape=[jax.ShapeDtypeStruct((T, D), F32), jax.ShapeDtypeStruct((T, D), BF),
                   jax.ShapeDtypeStruct((T // ROW_TILE, 1, D), F32)],
        compiler_params=_params(("parallel",)),
    )(y, target)


HEADNORM_TILE = 1024


def _half_sum_matrix():
    r = lax.broadcasted_iota(jnp.int32, (LANES, LANES), 0) // HEAD_DIM
    c = lax.broadcasted_iota(jnp.int32, (LANES, LANES), 1) // HEAD_DIM
    return (r == c).astype(F32)


def _head_mean(v, ones_blockdiag):
    return jnp.dot(v, ones_blockdiag, precision=lax.Precision.HIGHEST, preferred_element_type=F32) * (1.0 / HEAD_DIM)


def _headnorm_fwd(name, proj, col0, width, g2):
    T = proj.shape[0]
    nb, off = width // LANES, col0 // LANES

    def body(x_ref, g_ref, b_ref, o_ref):
        xv = x_ref[...].astype(F32)
        r = lax.rsqrt(_head_mean(xv * xv, b_ref[...]) + RMS_EPS)
        o_ref[...] = (xv * r * g_ref[...]).astype(BF)

    return pl.pallas_call(
        body, name=name, grid=(T // HEADNORM_TILE, nb),
        in_specs=[pl.BlockSpec((HEADNORM_TILE, LANES), lambda i, j: (i, j + off)),
                  pl.BlockSpec((1, LANES), lambda i, j: (0, 0)), pl.BlockSpec((LANES, LANES), lambda i, j: (0, 0))],
        out_specs=pl.BlockSpec((HEADNORM_TILE, LANES), lambda i, j: (i, j)),
        out_shape=jax.ShapeDtypeStruct((T, width), BF), compiler_params=_params(("parallel", "parallel")),
    )(proj, g2, _half_sum_matrix())


def _headnorm_bwd(name, dy, proj, col0, width, g2):
    T = proj.shape[0]
    nb, off = width // LANES, col0 // LANES

    def body(dy_ref, x_ref, g_ref, b_ref, dx_ref, dg_ref):
        xv = x_ref[...].astype(F32)
        dyv = dy_ref[...].astype(F32)
        r = lax.rsqrt(_head_mean(xv * xv, b_ref[...]) + RMS_EPS)
        xhat = xv * r
        dxhat = dyv * g_ref[...]
        dx_ref[...] = (r * (dxhat - xhat * _head_mean(dxhat * xhat, b_ref[...]))).astype(BF)
        dg_ref[...] = jnp.sum(dyv * xhat, axis=0, keepdims=True).reshape(dg_ref.shape)

    return pl.pallas_call(
        body, name=name, grid=(T // HEADNORM_TILE, nb),
        in_specs=[pl.BlockSpec((HEADNORM_TILE, LANES), lambda i, j: (i, j)),
                  pl.BlockSpec((HEADNORM_TILE, LANES), lambda i, j: (i, j + off)),
                  pl.BlockSpec((1, LANES), lambda i, j: (0, 0)), pl.BlockSpec((LANES, LANES), lambda i, j: (0, 0))],
        out_specs=[pl.BlockSpec((HEADNORM_TILE, LANES), lambda i, j: (i, j)),
                   pl.BlockSpec((1, 1, LANES), lambda i, j: (i, 0, j))],
        out_shape=[jax.ShapeDtypeStruct((T, width), BF), jax.ShapeDtypeStruct((T // HEADNORM_TILE, 1, width), F32)],
        compiler_params=_params(("parallel", "parallel")),
    )(dy, proj, g2, _half_sum_matrix())


def _shift_down(v, k, row):
    return jnp.where(row >= k, pltpu.roll(v, k, axis=0), 0.0)


def _shift_up(v, k, row, T):
    return jnp.where(row < T - k, pltpu.roll(v, T - k, axis=0), 0.0)


def _by_group(g, vals):
    out = vals[-1]
    for i in range(len(vals) - 2, -1, -1):
        out = jnp.where(g == i, vals[i], out)
    return out


def _pool_fwd(name, proj, pool_w, pool_scale):
    T = proj.shape[0]

    def body(x_ref, w_ref, s_ref, pooled_ref, mixed_ref):
        g = pl.program_id(0)
        xv = x_ref[...].astype(F32)
        row = lax.broadcasted_iota(jnp.int32, (T, 1), 0)
        s2 = xv + _shift_down(xv, 1, row)
        s4 = s2 + _shift_down(s2, 2, row)
        s8 = s4 + _shift_down(s4, 4, row)
        s16 = s8 + _shift_down(s8, 8, row)
        wsum = _by_group(g, [s2, s4, s8, s16])
        count = jnp.minimum(row + 1, 2 << g).astype(F32)
        pooled = (wsum / count - xv).astype(BF)
        pooled_ref[...] = pooled
        mixed = jnp.dot(pooled, w_ref[0].astype(BF), preferred_element_type=F32) * s_ref[...]
        mixed_ref[...] = mixed.astype(BF)

    col = pl.BlockSpec((T, POOL_GROUP), lambda g: (0, g))
    return pl.pallas_call(
        body, name=name, grid=(N_POOL_GROUPS,),
        in_specs=[col, pl.BlockSpec((1, POOL_GROUP, POOL_GROUP), lambda g: (g, 0, 0)),
                  pl.BlockSpec((1, POOL_GROUP), lambda g: (0, g))],
        out_specs=[col, col],
        out_shape=[jax.ShapeDtypeStruct((T, POOL_WIDTH), BF), jax.ShapeDtypeStruct((T, POOL_WIDTH), BF)],
        compiler_params=_params(("parallel",)),
    )(proj, pool_w, pool_scale)


def _pool_bwd(name, dmixed, pooled, pool_w, pool_scale):
    T = dmixed.shape[0]

    def body(dm_ref, p_ref, w_ref, s_ref, dx_ref, dw_ref, ds_ref):
        g = pl.program_id(0)
        dm = dm_ref[...].astype(F32)
        pooled = p_ref[...]
        w = w_ref[0].astype(BF)
        pre = jnp.dot(pooled, w, preferred_element_type=F32)
        ds_ref[...] = jnp.sum(dm * pre, axis=0, keepdims=True)
        dms = (dm * s_ref[...]).astype(BF)
        dw_ref[0] = lax.dot_general(pooled, dms, _DIMS["tn"], preferred_element_type=F32)
        dpooled = lax.dot_general(dms, w, _DIMS["nt"], preferred_element_type=F32)
        row = lax.broadcasted_iota(jnp.int32, (T, 1), 0)
        count = jnp.minimum(row + 1, 2 << g).astype(F32)
        z = dpooled / count
        l2 = z + _shift_up(z, 1, row, T)
        l4 = l2 + _shift_up(l2, 2, row, T)
        l8 = l4 + _shift_up(l4, 4, row, T)
        l16 = l8 + _shift_up(l8, 8, row, T)
        dx_ref[...] = (_by_group(g, [l2, l4, l8, l16]) - dpooled).astype(BF)

    col = pl.BlockSpec((T, POOL_GROUP), lambda g: (0, g))
    wspec = pl.BlockSpec((1, POOL_GROUP, POOL_GROUP), lambda g: (g, 0, 0))
    sspec = pl.BlockSpec((1, POOL_GROUP), lambda g: (0, g))
    return pl.pallas_call(
        body, name=name, grid=(N_POOL_GROUPS,), in_specs=[col, col, wspec, sspec], out_specs=[col, wspec, sspec],
        out_shape=[jax.ShapeDtypeStruct((T, POOL_WIDTH), BF),
                   jax.ShapeDtypeStruct((N_POOL_GROUPS, POOL_GROUP, POOL_GROUP), F32),
                   jax.ShapeDtypeStruct((1, POOL_WIDTH), F32)],
        compiler_params=_params(("parallel",)),
    )(dmixed, pooled, pool_w, pool_scale)


ATTN_SCALE = HEAD_DIM ** -0.5
MASKED = float(jnp.finfo(jnp.float32).min)
KV_COL_BLOCK_K = COL_K // LANES
KV_COL_BLOCK_V = COL_V // LANES
GROUP_WIDTH = GQA_GROUP * HEAD_DIM


def _dup_head(v, j):
    half = lax.broadcasted_iota(jnp.int32, (1, LANES), 1) // HEAD_DIM
    return jnp.where(half == j, v, pltpu.roll(v, HEAD_DIM, axis=1))


def _stack_heads(v, low):
    pieces = []
    for p in range(GROUP_WIDTH // LANES):
        vp = v[:, LANES * p: LANES * (p + 1)]
        pieces.append(jnp.where(low, vp, jnp.zeros_like(vp)))
        pieces.append(jnp.where(low, jnp.zeros_like(vp), vp))
    return jnp.concatenate(pieces, axis=0)


def _unstack_heads(st, low):
    pieces = []
    for p in range(GROUP_WIDTH // LANES):
        even = st[BLOCK * (2 * p): BLOCK * (2 * p + 1)]
        odd = st[BLOCK * (2 * p + 1): BLOCK * (2 * p + 2)]
        pieces.append(jnp.where(low, even, odd))
    return jnp.concatenate(pieces, axis=1)


def _band_mask(n):
    row = lax.broadcasted_iota(jnp.int32, (BLOCK, 2 * BLOCK), 0)
    col = lax.broadcasted_iota(jnp.int32, (BLOCK, 2 * BLOCK), 1)
    return (col > row) & (col <= row + BLOCK) & ((n > 0) | (col >= BLOCK))


def _softmax_heads(s, valid, sink_ref, j):
    ps, psinks = [], []
    for h in range(GQA_GROUP):
        sh = jnp.where(valid, s[BLOCK * h: BLOCK * (h + 1)], MASKED)
        sink = sink_ref[j * GQA_GROUP + h]
        m = jnp.maximum(jnp.max(sh, axis=1, keepdims=True), sink)
        e = jnp.exp(sh - m)
        es = jnp.exp(sink - m)
        inv = 1.0 / (jnp.sum(e, axis=1, keepdims=True) + es)
        ps.append(e * inv)
        psinks.append(es * inv)
    return jnp.concatenate(ps, axis=0), jnp.concatenate(psinks, axis=0)


def _attn_fwd(name, qn, kn, proj, sinks):
    T = qn.shape[0]
    nb = T // BLOCK

    def body(sink_ref, q_ref, kp_ref, kc_ref, vp_ref, vc_ref, o_ref):
        n, j = pl.program_id(0), pl.program_id(1)
        low = lax.broadcasted_iota(jnp.int32, (1, LANES), 1) < HEAD_DIM
        k2 = _dup_head(jnp.concatenate([kp_ref[...], kc_ref[...]], axis=0), j)
        v2 = _dup_head(jnp.concatenate([vp_ref[...], vc_ref[...]], axis=0), j)
        q = _stack_heads(q_ref[...], low)
        s = lax.dot_general(q, k2, _DIMS["nt"], preferred_element_type=F32) * ATTN_SCALE
        p, _ = _softmax_heads(s, _band_mask(n), sink_ref, j)
        o = jnp.dot(p.astype(BF), v2, preferred_element_type=F32)
        o_ref[...] = _unstack_heads(o, low).astype(BF)

    prev = lambda n, j: (jnp.maximum(n - 1, 0), 0)
    return pl.pallas_call(
        body, name=name, grid=(nb, 2),
        in_specs=[pl.BlockSpec(memory_space=pltpu.SMEM),
                  pl.BlockSpec((BLOCK, GROUP_WIDTH), lambda n, j: (n, j)),
                  pl.BlockSpec((BLOCK, LANES), prev), pl.BlockSpec((BLOCK, LANES), lambda n, j: (n, 0)),
                  pl.BlockSpec((BLOCK, LANES), lambda n, j: (jnp.maximum(n - 1, 0), KV_COL_BLOCK_V)),
                  pl.BlockSpec((BLOCK, LANES), lambda n, j: (n, KV_COL_BLOCK_V))],
        out_specs=pl.BlockSpec((BLOCK, GROUP_WIDTH), lambda n, j: (n, j)),
        out_shape=jax.ShapeDtypeStruct((T, ATTN_WIDTH), BF), compiler_params=_params(("parallel", "parallel")),
    )(sinks, qn, kn, kn, proj, proj)


def _attn_bwd(name, dout, qn, kn, proj, sinks):
    T = qn.shape[0]
    nb = T // BLOCK

    def body(sink_ref, do_ref, q_ref, kp_ref, kc_ref, vp_ref, vc_ref, dq_ref, dk_ref, dv_ref, dsink_ref,
             carry_k, carry_v, tot_k, tot_v):
        n = pl.program_id(0)
        lane = lax.broadcasted_iota(jnp.int32, (1, LANES), 1)
        low = lane < HEAD_DIM

        @pl.when(n == 0)
        def _():
            carry_k[...] = jnp.zeros_like(carry_k)
            carry_v[...] = jnp.zeros_like(carry_v)
            dsink_ref[...] = jnp.zeros_like(dsink_ref)

        @pl.when(n == nb)
        def _():
            tot_k[...] = jnp.zeros_like(tot_k)
            tot_v[...] = jnp.zeros_like(tot_v)

        @pl.when(n < nb)
        def _():
            kk = jnp.concatenate([kp_ref[...], kc_ref[...]], axis=0)
            vv = jnp.concatenate([vp_ref[...], vc_ref[...]], axis=0)
            valid = _band_mask(n)
            dk_tot = jnp.zeros((2 * BLOCK, LANES), F32)
            dv_tot = jnp.zeros((2 * BLOCK, LANES), F32)
            dsink = jnp.zeros((1, LANES), F32)
            for j in range(2):
                k2 = _dup_head(kk, j)
                v2 = _dup_head(vv, j)
                q = _stack_heads(q_ref[:, GROUP_WIDTH * j: GROUP_WIDTH * (j + 1)], low)
                do = _stack_heads(do_ref[:, GROUP_WIDTH * j: GROUP_WIDTH * (j + 1)], low)
                s = lax.dot_general(q, k2, _DIMS["nt"], preferred_element_type=F32) * ATTN_SCALE
                p, psink = _softmax_heads(s, valid, sink_ref, j)
                dp = lax.dot_general(do, v2, _DIMS["nt"], preferred_element_type=F32)
                delta = jnp.sum(p * dp, axis=1, keepdims=True)
                ds = (p * (dp - delta) * ATTN_SCALE).astype(BF)
                dq_ref[:, GROUP_WIDTH * j: GROUP_WIDTH * (j + 1)] = _unstack_heads(
                    jnp.dot(ds, k2, preferred_element_type=F32), low).astype(BF)
                dk2 = lax.dot_general(ds, q, _DIMS["tn"], preferred_element_type=F32)
                dv2 = lax.dot_general(p.astype(BF), do, _DIMS["tn"], preferred_element_type=F32)
                mine = low if j == 0 else jnp.logical_not(low)
                dk_tot = dk_tot + jnp.where(mine, dk2 + pltpu.roll(dk2, HEAD_DIM, axis=1), 0.0)
                dv_tot = dv_tot + jnp.where(mine, dv2 + pltpu.roll(dv2, HEAD_DIM, axis=1), 0.0)
                sink_term = psink * delta
                for h in range(GQA_GROUP):
                    val = -jnp.sum(sink_term[BLOCK * h: BLOCK * (h + 1)], axis=0, keepdims=True)
                    dsink = dsink + jnp.where(lane == j * GQA_GROUP + h, val, 0.0)
            tot_k[...] = dk_tot
            tot_v[...] = dv_tot
            dsink_ref[0:1, :] += dsink

        dk_ref[...] = (carry_k[...] + tot_k[0:BLOCK]).astype(BF)
        dv_ref[...] = (carry_v[...] + tot_v[0:BLOCK]).astype(BF)
        carry_k[...] = tot_k[BLOCK:]
        carry_v[...] = tot_v[BLOCK:]

    cur = lambda n: (jnp.minimum(n, nb - 1), 0)
    prev = lambda n: (jnp.maximum(n - 1, 0), 0)
    wide = pl.BlockSpec((BLOCK, ATTN_WIDTH), cur)
    return pl.pallas_call(
        body, name=name, grid=(nb + 1,),
        in_specs=[pl.BlockSpec(memory_space=pltpu.SMEM), wide, wide,
                  pl.BlockSpec((BLOCK, LANES), prev), pl.BlockSpec((BLOCK, LANES), cur),
                  pl.BlockSpec((BLOCK, LANES), lambda n: (jnp.maximum(n - 1, 0), KV_COL_BLOCK_V)),
                  pl.BlockSpec((BLOCK, LANES), lambda n: (jnp.minimum(n, nb - 1), KV_COL_BLOCK_V))],
        out_specs=[wide, pl.BlockSpec((BLOCK, LANES), prev), pl.BlockSpec((BLOCK, LANES), prev),
                   pl.BlockSpec((8, LANES), lambda n: (0, 0))],
        out_shape=[jax.ShapeDtypeStruct((T, ATTN_WIDTH), BF), jax.ShapeDtypeStruct((T, KV_WIDTH), BF),
                   jax.ShapeDtypeStruct((T, KV_WIDTH), BF), jax.ShapeDtypeStruct((8, LANES), F32)],
        scratch_shapes=[pltpu.VMEM((BLOCK, LANES), F32), pltpu.VMEM((BLOCK, LANES), F32),
                        pltpu.VMEM((2 * BLOCK, LANES), F32), pltpu.VMEM((2 * BLOCK, LANES), F32)],
        compiler_params=_params(("arbitrary",)),
    )(sinks, dout, qn, kn, kn, proj, proj)


def _swiglu_fwd_epilogue(accs, ex):
    g, u = accs
    return [g, u, g * jax.nn.sigmoid(g) * u], []


def _swiglu_bwd_epilogue(accs, ex):
    (da,) = accs
    g, u = ex[0].astype(F32), ex[1].astype(F32)
    s = jax.nn.sigmoid(g)
    silu = g * s
    return [da * u * (s * (1.0 + g * (1.0 - s))), da * silu, silu * u], []


def _half_residual_epilogue(accs, ex):
    return [ex[0] + 0.5 * accs[0]], []


def _residual_epilogue(accs, ex):
    return [ex[0] + accs[0]], []


def _merge_fwd_epilogue(accs, ex):
    (ba,) = accs
    bp, gp_pre, ga_pre, bias_p, bias_a = ex
    gp = jax.nn.sigmoid(gp_pre.astype(F32) + bias_p)
    ga = jax.nn.sigmoid(ga_pre.astype(F32) + bias_a)
    return [gp * bp.astype(F32) + ga * ba, ba], []


def _merge_bwd_epilogue(accs, ex):
    (dm,) = accs
    bp, ba, gp_pre, ga_pre, bias_p, bias_a = ex
    gp = jax.nn.sigmoid(gp_pre.astype(F32) + bias_p)
    ga = jax.nn.sigmoid(ga_pre.astype(F32) + bias_a)
    dgp = dm * bp.astype(F32) * gp * (1.0 - gp)
    dga = dm * ba.astype(F32) * ga * (1.0 - ga)
    return [dm * gp, dm * ga, dgp, dga], [dgp, dga]


def _prep(name, w, transpose):
    shape = w.shape[::-1] if transpose else w.shape

    def body(w_ref, o_ref):
        v = w_ref[...]
        o_ref[...] = (v.T if transpose else v).astype(BF)

    return pl.pallas_call(body, name=name, out_shape=jax.ShapeDtypeStruct(shape, BF), compiler_params=_params())(w)


def _adam_math(w, g, m, v):
    m = ADAM_B1 * m + (1.0 - ADAM_B1) * g
    v = ADAM_B2 * v + (1.0 - ADAM_B2) * jnp.square(g)
    m_hat = m / (1.0 - ADAM_B1 ** ADAM_STEP)
    v_hat = v / (1.0 - ADAM_B2 ** ADAM_STEP)
    delta = -ADAM_LR * (m_hat / (jnp.sqrt(v_hat) + ADAM_EPS) + ADAM_WD * w)
    return delta, m, v


def _adamw_sharded(name, slots, w, m, v, transpose):
    def body(s_ref, w_ref, m_ref, v_ref, g_out, d_out, m_out, v_out):
        g = s_ref[0].astype(F32)
        for i in range(1, 4):
            g = g + s_ref[i].astype(F32)
        if transpose:
            g = g.T
        delta, mn, vn = _adam_math(w_ref[...], g, m_ref[...], v_ref[...])
        g_out[...] = g
        d_out[...] = delta
        m_out[...] = mn
        v_out[...] = vn

    return pl.pallas_call(
        body, name=name, out_shape=[jax.ShapeDtypeStruct(w.shape, F32)] * 4, compiler_params=_params(),
    )(slots, w, m, v)


def _adamw_small(name, gathered, w, m, v):
    def body(s_ref, w_ref, m_ref, v_ref, g_out, d_out, m_out, v_out):
        g = s_ref[0]
        for i in range(1, N_DEV):
            g = g + s_ref[i]
        delta, mn, vn = _adam_math(w_ref[...], g, m_ref[...], v_ref[...])
        g_out[...] = g
        d_out[...] = delta
        m_out[...] = mn
        v_out[...] = vn

    return pl.pallas_call(
        body, name=name, out_shape=[jax.ShapeDtypeStruct(w.shape, F32)] * 4, compiler_params=_params(),
    )(gathered, w, m, v)


def _place():
    x, y, c = lax.axis_index("x"), lax.axis_index("y"), lax.axis_index("c")
    other_chips = [(1 - x, y), (x, 1 - y), (1 - x, 1 - y)]
    return x, y, c, other_chips


def _all_gather(name, shards):
    n = len(shards)

    def body(*refs):
        ins, outs = refs[:n], refs[n:2 * n]
        send_sems, recv_sems, local_sems = refs[2 * n:]
        x, y, c, chips = _place()
        me, sibling = (x, y, c), (x, y, 1 - c)

        def rows(k, px, py, pc):
            r = shards[k].shape[0]
            return outs[k].at[pl.ds(pl.multiple_of((4 * px + 2 * py + pc) * r, 8), r), :]

        def copy(k, slot, block, to, src=None):
            return pltpu.make_async_remote_copy(
                src_ref=rows(k, *block) if src is None else src, dst_ref=rows(k, *block),
                send_sem=send_sems.at[7 * k + slot], recv_sem=recv_sems.at[7 * k + slot],
                device_id=to, device_id_type=MESH)

        mine = [pltpu.make_async_copy(ins[k], rows(k, *me), local_sems.at[k]) for k in range(n)]
        for cp in mine:
            cp.start()
        first = []
        for j, chip in enumerate(chips):
            first += [copy(k, 1 + j, me, (*chip, c), src=ins[k]) for k in range(n)]
        first += [copy(k, 0, me, sibling, src=ins[k]) for k in range(n)]
        for cp in first:
            cp.start()
        passed = []
        for j, chip in enumerate(chips):
            for k in range(n):
                copy(k, 1 + j, (*chip, c), me).wait_recv()
                fwd = copy(k, 4 + j, (*chip, c), sibling)
                fwd.start()
                passed.append(fwd)
        for k in range(n):
            copy(k, 0, sibling, me).wait_recv()
        for j, chip in enumerate(chips):
            for k in range(n):
                copy(k, 4 + j, (*chip, 1 - c), me).wait_recv()
        for cp in first + passed:
            cp.wait_send()
        for cp in mine:
            cp.wait()

    return pl.pallas_call(
        body, name=name, in_specs=[ANY] * n, out_specs=[ANY] * n,
        out_shape=[jax.ShapeDtypeStruct((N_DEV * s.shape[0], s.shape[1]), s.dtype) for s in shards],
        scratch_shapes=[pltpu.SemaphoreType.DMA((7 * n,)), pltpu.SemaphoreType.DMA((7 * n,)),
                        pltpu.SemaphoreType.DMA((n,))],
        compiler_params=pltpu.CompilerParams(has_side_effects=True),
    )(*shards)


def _pair_exchange(name, parts):
    n = len(parts)

    def body(*refs):
        ins, outs = refs[:n], refs[n:2 * n]
        send_sems, recv_sems = refs[2 * n:]
        x, y, c, _ = _place()
        copies = [pltpu.make_async_remote_copy(
            src_ref=ins[k].at[:, pl.ds(1 - c, 1)], dst_ref=outs[k], send_sem=send_sems.at[k], recv_sem=recv_sems.at[k],
            device_id=(x, y, 1 - c), device_id_type=MESH) for k in range(n)]
        for cp in copies:
            cp.start()
        for cp in copies:
            cp.wait()

    return pl.pallas_call(
        body, name=name, in_specs=[ANY] * n, out_specs=[ANY] * n,
        out_shape=[jax.ShapeDtypeStruct((4, 1) + p.shape[2:], p.dtype) for p in parts],
        scratch_shapes=[pltpu.SemaphoreType.DMA((n,)), pltpu.SemaphoreType.DMA((n,))],
        compiler_params=pltpu.CompilerParams(has_side_effects=True),
    )(*parts)


def _pair_sum(name, part, got, core):
    _, _, r, C = part.shape

    def body(core_ref, p_ref, g_ref, o_ref):
        o_ref[0] = (p_ref[0, 0].astype(F32) + g_ref[0, 0].astype(F32)).astype(o_ref.dtype)

    return pl.pallas_call(
        body, name=name,
        grid_spec=pltpu.PrefetchScalarGridSpec(
            num_scalar_prefetch=1, grid=(4,),
            in_specs=[pl.BlockSpec((1, 1, r, C), lambda i, core_ref: (i, core_ref[0], 0, 0)),
                      pl.BlockSpec((1, 1, r, C), lambda i, core_ref: (i, 0, 0, 0))],
            out_specs=pl.BlockSpec((1, r, C), lambda i, core_ref: (i, 0, 0))),
        out_shape=jax.ShapeDtypeStruct((4, r, C), part.dtype), compiler_params=_params(("parallel",)),
    )(core, part, got)


def _chip_exchange(name, sums):
    n = len(sums)

    def body(*refs):
        ins, outs = refs[:n], refs[n:2 * n]
        send_sems, recv_sems, local_sems = refs[2 * n:]
        x, y, c, chips = _place()
        here = 2 * x + y
        mine = [pltpu.make_async_copy(ins[k].at[pl.ds(here, 1)], outs[k].at[pl.ds(here, 1)], local_sems.at[k])
                for k in range(n)]
        for cp in mine:
            cp.start()
        copies = []
        for j, (px, py) in enumerate(chips):
            copies += [pltpu.make_async_remote_copy(
                src_ref=ins[k].at[pl.ds(2 * px + py, 1)], dst_ref=outs[k].at[pl.ds(here, 1)],
                send_sem=send_sems.at[3 * k + j], recv_sem=recv_sems.at[3 * k + j],
                device_id=(px, py, c), device_id_type=MESH) for k in range(n)]
        for cp in copies:
            cp.start()
        for cp in copies:
            cp.wait()
        for cp in mine:
            cp.wait()

    return pl.pallas_call(
        body, name=name, in_specs=[ANY] * n, out_specs=[ANY] * n,
        out_shape=[jax.ShapeDtypeStruct(s.shape, s.dtype) for s in sums],
        scratch_shapes=[pltpu.SemaphoreType.DMA((3 * n,)), pltpu.SemaphoreType.DMA((3 * n,)),
                        pltpu.SemaphoreType.DMA((n,))],
        compiler_params=pltpu.CompilerParams(has_side_effects=True),
    )(*sums)


def _ffn_fwd(tag, x, gain, wgT, wuT, wd):
    n = _rms_fwd(tag + "_norm", x, gain)
    g, u, a = _mm(tag + "_gate_up", [(n, wgT, "nt", 0), (n, wuT, "nt", 1)], [BF, BF, BF],
                  tm=512, tn=1408, tk=D_MODEL, epilogue=_swiglu_fwd_epilogue)
    (h,) = _mm(tag + "_down", [(a, wd, "nn", 0)], [F32], tm=512, tn=D_MODEL, tk=D_FF,
               epilogue=_half_residual_epilogue, extras=[(x, "tile", 0)])
    return h, (n, g, u)


def _ffn_bwd(tag, dy, dyb, x, gain, wgT, wuT, wd, saved):
    n, g, u = saved
    half = lambda accs, ex: _swiglu_bwd_epilogue([0.5 * accs[0]], ex)
    dg, du, a = _mm(tag + "_d_act", [(dyb, wd, "nt", 0)], [BF, BF, BF], tm=512, tn=1408, tk=D_MODEL,
                    epilogue=half, extras=[(g, "tile", 0), (u, "tile", 0)])
    (dn,) = _mm(tag + "_d_norm", [(dg, wgT, "nn", 0), (du, wuT, "nn", 0)], [F32], tm=512, tn=D_MODEL, tk=1408)
    dx, dxb, dgain = _rms_bwd(tag + "_norm_bwd", dn, x, gain, dy)
    (dwgT,) = _mm(tag + "_dw_gate", [(dg, n, "tn", 0)], [BF], tm=1408, tn=D_MODEL, tk=512)
    (dwuT,) = _mm(tag + "_dw_up", [(du, n, "tn", 0)], [BF], tm=1408, tn=D_MODEL, tk=512)
    (dwd,) = _mm(tag + "_dw_down", [(a, dyb, "tn", 0)], [BF], tm=1408, tn=D_MODEL, tk=512,
                 epilogue=lambda accs, ex: ([0.5 * accs[0]], []))
    return dx, dxb, dgain, dwgT, dwuT, dwd


def _tile_gain(g):
    return jnp.concatenate([g, g]).reshape(1, LANES)


def _fold_heads(partials):
    return jnp.sum(partials.reshape(-1, HEAD_DIM), axis=0)


def _pack_small(norm1, mixn, norm2, pool_w, pool_scale, qn, kn, sinks, gate_bias, last):
    pad = lambda v: jnp.pad(v.reshape(-1), (0, LANES - v.size)).reshape(1, LANES)
    return jnp.concatenate([
        norm1.reshape(-1, LANES), mixn.reshape(-1, LANES), norm2.reshape(-1, LANES), pool_w.reshape(-1, LANES),
        pool_scale.reshape(-1, LANES), pad(qn), pad(kn), pad(sinks), gate_bias.reshape(-1, LANES), pad(last)], axis=0)


def _unpack_small(p):
    o, out = 0, []
    for rows, shape in ((8, (D_MODEL,)), (8, (D_MODEL,)), (8, (D_MODEL,)), (512, (N_POOL_GROUPS, POOL_GROUP, POOL_GROUP)),
                        (4, (POOL_WIDTH,)), (1, (HEAD_DIM,)), (1, (HEAD_DIM,)), (1, (N_HEADS,)), (16, (2 * D_MODEL,)), (1, (1,))):
        flat = p[o:o + rows].reshape(-1)
        size = 1
        for s in shape:
            size *= s
        out.append(flat[:size].reshape(shape))
        o += rows
    return out


def kernel(x, ffn1_norm, ffn1_w_gate, ffn1_w_up, ffn1_w_down, mix_norm, w_in, pool_w, pool_scale, w_pool_out, q_norm, k_norm, sinks, w_attn_out, gate_bias, w_out, ffn2_norm, ffn2_w_gate, ffn2_w_up, ffn2_w_down, loss_target, m_ffn1_norm, m_ffn1_w_gate, m_ffn1_w_up, m_ffn1_w_down, m_mix_norm, m_w_in, m_pool_w, m_pool_scale, m_w_pool_out, m_q_norm, m_k_norm, m_sinks, m_w_attn_out, m_gate_bias, m_w_out, m_ffn2_norm, m_ffn2_w_gate, m_ffn2_w_up, m_ffn2_w_down, v_ffn1_norm, v_ffn1_w_gate, v_ffn1_w_up, v_ffn1_w_down, v_mix_norm, v_w_in, v_pool_w, v_pool_scale, v_w_pool_out, v_q_norm, v_k_norm, v_sinks, v_w_attn_out, v_gate_bias, v_w_out, v_ffn2_norm, v_ffn2_w_gate, v_ffn2_w_up, v_ffn2_w_down):
    T = x.shape[1]
    x2 = x.reshape(T, D_MODEL)
    target = loss_target.reshape(T, D_MODEL)

    big = [
        ("ffn1_w_gate", ffn1_w_gate, m_ffn1_w_gate, v_ffn1_w_gate, True),
        ("ffn1_w_up", ffn1_w_up, m_ffn1_w_up, v_ffn1_w_up, True),
        ("ffn1_w_down", ffn1_w_down, m_ffn1_w_down, v_ffn1_w_down, False),
        ("w_in", w_in, m_w_in, v_w_in, True),
        ("w_pool_out", w_pool_out, m_w_pool_out, v_w_pool_out, True),
        ("w_attn_out", w_attn_out, m_w_attn_out, v_w_attn_out, False),
        ("w_out", w_out, m_w_out, v_w_out, False),
        ("ffn2_w_gate", ffn2_w_gate, m_ffn2_w_gate, v_ffn2_w_gate, True),
        ("ffn2_w_up", ffn2_w_up, m_ffn2_w_up, v_ffn2_w_up, True),
        ("ffn2_w_down", ffn2_w_down, m_ffn2_w_down, v_ffn2_w_down, False),
    ]
    shards = [_prep("prep_" + nm, w, tr) for nm, w, _, _, tr in big]
    full = _all_gather("gather_weights", shards)
    wg1T, wu1T, wd1, w_inT, w_poT, w_ao, w_o, wg2T, wu2T, wd2 = full

    g1 = ffn1_norm.reshape(1, D_MODEL)
    g2 = mix_norm.reshape(1, D_MODEL)
    g3 = ffn2_norm.reshape(1, D_MODEL)
    bias_row = gate_bias.reshape(1, 2 * D_MODEL)
    qg, kg = _tile_gain(q_norm), _tile_gain(k_norm)
    scale_row = pool_scale.reshape(1, POOL_WIDTH)

    h1, saved1 = _ffn_fwd("ffn1", x2, g1, wg1T, wu1T, wd1)
    u = _rms_fwd("mix_norm", h1, g2)
    (proj,) = _mm("in_proj", [(u, w_inT, "nt", 0)], [BF], tm=512, tn=1280, tk=D_MODEL)
    pooled, mixed = _pool_fwd("pool_fwd", proj, pool_w, scale_row)
    qn = _headnorm_fwd("q_norm", proj, COL_Q, ATTN_WIDTH, qg)
    kn = _headnorm_fwd("k_norm", proj, COL_K, KV_WIDTH, kg)
    attn = _attn_fwd("attn_fwd", qn, kn, proj, sinks)
    (bp,) = _mm("pool_out", [(mixed, w_poT, "nt", 0)], [BF], tm=1024, tn=D_MODEL, tk=POOL_WIDTH)
    gate_tn = 256
    gate_extras = [(proj, "tile", COL_GP // gate_tn), (proj, "tile", COL_GA // gate_tn),
                   (bias_row, "row", 0), (bias_row, "row", D_MODEL // gate_tn)]
    merged, ba = _mm("attn_out_merge", [(attn, w_ao, "nn", 0)], [BF, BF], tm=2048, tn=gate_tn, tk=ATTN_WIDTH,
                     epilogue=_merge_fwd_epilogue, extras=[(bp, "tile", 0)] + gate_extras)
    (h2,) = _mm("mix_out", [(merged, w_o, "nn", 0)], [F32], tm=512, tn=D_MODEL, tk=D_MODEL,
                epilogue=_residual_epilogue, extras=[(h1, "tile", 0)])
    y, saved2 = _ffn_fwd("ffn2", h2, g3, wg2T, wu2T, wd2)
    dy, dyb, sq = _loss_head("loss_head", y, target)
    loss_local = 0.5 * jnp.sum(sq) / D_MODEL

    dh2, dh2b, dg3, dwg2T, dwu2T, dwd2 = _ffn_bwd("ffn2", dy, dyb, h2, g3, wg2T, wu2T, wd2, saved2)
    dbp, dba, dgp, dga, cs_gp, cs_ga = _mm(
        "mix_out_bwd", [(dh2b, w_o, "nt", 0)], [BF, BF, BF, BF], tm=2048, tn=gate_tn, tk=D_MODEL,
        epilogue=_merge_bwd_epilogue, extras=[(bp, "tile", 0), (ba, "tile", 0)] + gate_extras, n_colsum=2)
    (dw_o,) = _mm("dw_out", [(merged, dh2b, "tn", 0)], [BF], tm=D_MODEL, tn=D_MODEL, tk=512)
    (dmixed,) = _mm("pool_out_bwd", [(dbp, w_poT, "nn", 0)], [BF], tm=1024, tn=POOL_WIDTH, tk=D_MODEL)
    (dw_poT,) = _mm("dw_pool_out", [(dbp, mixed, "tn", 0)], [BF], tm=D_MODEL, tn=POOL_WIDTH, tk=512)
    (dattn,) = _mm("attn_out_bwd", [(dba, w_ao, "nt", 0)], [BF], tm=1024, tn=ATTN_WIDTH, tk=D_MODEL)
    (dw_ao,) = _mm("dw_attn_out", [(attn, dba, "tn", 0)], [BF], tm=ATTN_WIDTH, tn=D_MODEL, tk=512)
    dxp, dpool_w, dpool_scale = _pool_bwd("pool_bwd", dmixed, pooled, pool_w, scale_row)
    dqn, dkn, dv, dsink_tile = _attn_bwd("attn_bwd", dattn, qn, kn, proj, sinks)
    dq, dqg = _headnorm_bwd("q_norm_bwd", dqn, proj, COL_Q, ATTN_WIDTH, qg)
    dk, dkg = _headnorm_bwd("k_norm_bwd", dkn, proj, COL_K, KV_WIDTH, kg)
    dproj = jnp.concatenate([dxp, dq, dk, dv, dgp, dga], axis=1)
    (du,) = _mm("in_proj_bwd", [(dproj, w_inT, "nn", 0)], [F32], tm=512, tn=D_MODEL, tk=1280)
    (dw_inT,) = _mm("dw_in", [(dproj, u, "tn", 0)], [BF], tm=1280, tn=D_MODEL, tk=512)
    dh1, dh1b, dg2 = _rms_bwd("mix_norm_bwd", du, h1, g2, dh2)
    dx, _, dg1, dwg1T, dwu1T, dwd1 = _ffn_bwd("ffn1", dh1, dh1b, x2, g1, wg1T, wu1T, wd1, saved1)

    partials = [dwg1T, dwu1T, dwd1, dw_inT, dw_poT, dw_ao, dw_o, dwg2T, dwu2T, dwd2]
    parts4 = [p.reshape(4, 2, p.shape[0] // N_DEV, p.shape[1]) for p in partials]
    got = _pair_exchange("pair_exchange", parts4)
    core = lax.axis_index("c").astype(jnp.int32).reshape(1)
    sums = [_pair_sum("pair_sum_" + big[k][0], parts4[k], got[k], core) for k in range(len(big))]
    slots = _chip_exchange("chip_exchange", sums)
    big_out = {}
    for k, (nm, w, m, v, tr) in enumerate(big):
        big_out[nm] = _adamw_sharded("adamw_" + nm, slots[k], w, m, v, tr)

    dgate_bias = jnp.concatenate([jnp.sum(cs_gp, axis=(0, 1)), jnp.sum(cs_ga, axis=(0, 1))])
    small_grad = _pack_small(
        jnp.sum(dg1, axis=(0, 1)), jnp.sum(dg2, axis=(0, 1)), jnp.sum(dg3, axis=(0, 1)), dpool_w, dpool_scale,
        _fold_heads(dqg), _fold_heads(dkg), dsink_tile[0, :N_HEADS], dgate_bias, loss_local.reshape(1))
    zero = jnp.zeros((1,), F32)
    small_w = _pack_small(ffn1_norm, mix_norm, ffn2_norm, pool_w, pool_scale, q_norm, k_norm, sinks, gate_bias, zero)
    small_m = _pack_small(m_ffn1_norm, m_mix_norm, m_ffn2_norm, m_pool_w, m_pool_scale, m_q_norm, m_k_norm, m_sinks,
                          m_gate_bias, zero)
    small_v = _pack_small(v_ffn1_norm, v_mix_norm, v_ffn2_norm, v_pool_w, v_pool_scale, v_q_norm, v_k_norm, v_sinks,
                          v_gate_bias, zero)
    (gathered,) = _all_gather("gather_small_grads", [small_grad])
    packed = _adamw_small("adamw_small", gathered.reshape(N_DEV, -1, LANES), small_w, small_m, small_v)
    small_names = ["ffn1_norm", "mix_norm", "ffn2_norm", "pool_w", "pool_scale", "q_norm", "k_norm", "sinks", "gate_bias"]
    small_out = {}
    unpacked = [_unpack_small(p) for p in packed]
    for i, nm in enumerate(small_names):
        small_out[nm] = tuple(unpacked[j][i] for j in range(4))
    loss = unpacked[0][9].reshape(())

    order = ["ffn1_norm", "ffn1_w_gate", "ffn1_w_up", "ffn1_w_down", "mix_norm", "w_in", "pool_w", "pool_scale",
             "w_pool_out", "q_norm", "k_norm", "sinks", "w_attn_out", "gate_bias", "w_out", "ffn2_norm",
             "ffn2_w_gate", "ffn2_w_up", "ffn2_w_down"]
    every = {**big_out, **small_out}
    outs = [loss, dx.reshape(x.shape)]
    for j in range(4):
        outs += [every[nm][j] for nm in order]
    return tuple(outs)
```

```python
import functools

import jax
import jax.numpy as jnp
from jax import lax
from jax.experimental import pallas as pl
from jax.experimental.pallas import tpu as pltpu

BF = jnp.bfloat16
F32 = jnp.float32

D_MODEL = 1024
D_FF = 2816
POOL_WIDTH = 512
POOL_GROUP = 128
N_POOL_GROUPS = 4
HEAD_DIM = 64
N_HEADS = 16
GQA_GROUP = 8
BLOCK = 128
ATTN_WIDTH = 1024
KV_WIDTH = 128
IN_WIDTH = 3840
RMS_EPS = 1e-6
N_DEV = 8
LANES = 128

COL_Q = POOL_WIDTH
COL_K = COL_Q + ATTN_WIDTH
COL_V = COL_K + KV_WIDTH
COL_GP = COL_V + KV_WIDTH
COL_GA = COL_GP + D_MODEL

ADAM_LR = 0.001
ADAM_B1 = 0.9
ADAM_B2 = 0.999
ADAM_EPS = 1e-08
ADAM_WD = 0.01
ADAM_STEP = 10

VMEM_LIMIT_V7X = 56 * 1024 * 1024
MESH = pl.DeviceIdType.MESH
ANY = pl.BlockSpec(memory_space=pl.ANY)


def _params(sem=None):
    return pltpu.CompilerParams(dimension_semantics=sem, vmem_limit_bytes=VMEM_LIMIT_V7X)


_DIMS = {"nt": (((1,), (1,)), ((), ())), "nn": (((1,), (0,)), ((), ())), "tn": (((0,), (0,)), ((), ()))}


class _Task:
    def __init__(self, inputs, out_shapes, scratch, start, finish, aliases=None):
        self.inputs, self.out_shapes, self.scratch = list(inputs), list(out_shapes), list(scratch)
        self.start, self.finish, self.aliases = start, finish, dict(aliases or {})


class _CommPlumbing:
    def __init__(self, tasks):
        self.tasks = list(tasks or [])
        self.args = [a for t in self.tasks for a in t.inputs]
        self.out_shapes = [o for t in self.tasks for o in t.out_shapes]
        self.scratch = [s for t in self.tasks for s in t.scratch]
        self.n_in, self.n_out = len(self.args), len(self.out_shapes)

    def _slices(self, c_in, c_out, c_scr):
        i = o = s = 0
        for t in self.tasks:
            yield t, c_in[i:i + len(t.inputs)], c_out[o:o + len(t.out_shapes)], c_scr[s:s + len(t.scratch)]
            i, o, s = i + len(t.inputs), o + len(t.out_shapes), s + len(t.scratch)

    def start(self, c_in, c_out, c_scr):
        for t, ins, outs, scr in self._slices(c_in, c_out, c_scr):
            t.start(ins, outs, scr)

    def finish(self, c_in, c_out, c_scr):
        for t, ins, outs, scr in self._slices(c_in, c_out, c_scr):
            t.finish(ins, outs, scr)

    def aliases(self, in_base, out_base):
        res, i, o = {}, in_base, out_base
        for t in self.tasks:
            for a, b in t.aliases.items():
                res[i + a] = o + b
            i, o = i + len(t.inputs), o + len(t.out_shapes)
        return res

    def split_outputs(self, flat):
        res, o = [], 0
        for t in self.tasks:
            res.append(list(flat[o:o + len(t.out_shapes)]))
            o += len(t.out_shapes)
        return res


def _comm_only(name, tasks):
    plumb = _CommPlumbing(tasks)

    def body(*refs):
        c_in, c_out = refs[:plumb.n_in], refs[plumb.n_in: plumb.n_in + plumb.n_out]
        c_scr = refs[plumb.n_in + plumb.n_out:]
        plumb.start(c_in, c_out, c_scr)
        plumb.finish(c_in, c_out, c_scr)

    res = pl.pallas_call(
        body, name=name, in_specs=[ANY] * plumb.n_in, out_specs=[ANY] * plumb.n_out, out_shape=plumb.out_shapes,
        scratch_shapes=plumb.scratch, input_output_aliases=plumb.aliases(0, 0),
        compiler_params=pltpu.CompilerParams(has_side_effects=True),
    )(*plumb.args)
    return plumb.split_outputs(res)


def _mm(name, terms, out_dtypes, *, tm, tn, tk, epilogue=None, extras=(), n_colsum=0, comm=None):
    a0, b0, mode0, _ = terms[0]
    if mode0 == "nt":
        (M, K), N = a0.shape, b0.shape[0]
    elif mode0 == "nn":
        (M, K), N = a0.shape, b0.shape[1]
    else:
        (K, M), N = a0.shape, b0.shape[1]
    tm, tn, tk = min(tm, M), min(tn, N), min(tk, K)
    assert M % tm == 0 and N % tn == 0 and K % tk == 0, (name, M, N, K, tm, tn, tk)
    nI, nJ, nK = M // tm, N // tn, K // tk
    n_terms = len(terms)
    n_acc = max(t[3] for t in terms) + 1
    n_ex = len(extras)
    n_out = len(out_dtypes)
    if epilogue is None:
        epilogue = lambda accs, ex: ([accs[0]], [])
    plumb = _CommPlumbing(comm)
    n_scr = n_acc if nK > 1 else 0

    def body(*refs):
        n_in = 2 * n_terms + n_ex
        ab = refs[: 2 * n_terms]
        ex_refs = refs[2 * n_terms: n_in]
        c_in = refs[n_in: n_in + plumb.n_in]
        o0 = n_in + plumb.n_in
        out_refs = refs[o0: o0 + n_out]
        cs_refs = refs[o0 + n_out: o0 + n_out + n_colsum]
        c_out = refs[o0 + n_out + n_colsum: o0 + n_out + n_colsum + plumb.n_out]
        s0 = o0 + n_out + n_colsum + plumb.n_out
        acc_refs = refs[s0: s0 + n_scr]
        c_scr = refs[s0 + n_scr:]
        if comm:
            i_, j_, k_ = pl.program_id(0), pl.program_id(1), pl.program_id(2)

            @pl.when((i_ == 0) & (j_ == 0) & (k_ == 0))
            def _():
                plumb.start(c_in, c_out, c_scr)

        def products():
            accs = [None] * n_acc
            for t, (_, _, mode, ai) in enumerate(terms):
                p = lax.dot_general(ab[2 * t][...], ab[2 * t + 1][...], _DIMS[mode], preferred_element_type=F32)
                accs[ai] = p if accs[ai] is None else accs[ai] + p
            return accs

        def finish(accs):
            outs, colsums = epilogue(accs, [r[...] for r in ex_refs])
            for r, o in zip(out_refs, outs):
                r[...] = o.astype(r.dtype)
            for r, cs in zip(cs_refs, colsums):
                r[...] = jnp.sum(cs, axis=0, keepdims=True).reshape(r.shape)

        if nK == 1:
            finish(products())
        else:
            k = pl.program_id(2)
            accs = products()

            @pl.when(k == 0)
            def _():
                for r, a in zip(acc_refs, accs):
                    r[...] = a

            @pl.when(k > 0)
            def _():
                for r, a in zip(acc_refs, accs):
                    r[...] += a

            @pl.when(k == nK - 1)
            def _():
                finish([r[...] for r in acc_refs])

        if comm:
            @pl.when((i_ == nI - 1) & (j_ == nJ - 1) & (k_ == nK - 1))
            def _():
                plumb.finish(c_in, c_out, c_scr)

    in_specs, args = [], []
    for a, b, mode, _ in terms:
        if mode == "nt":
            in_specs += [pl.BlockSpec((tm, tk), lambda i, j, k: (i, k)), pl.BlockSpec((tn, tk), lambda i, j, k: (j, k))]
        elif mode == "nn":
            in_specs += [pl.BlockSpec((tm, tk), lambda i, j, k: (i, k)), pl.BlockSpec((tk, tn), lambda i, j, k: (k, j))]
        else:
            in_specs += [pl.BlockSpec((tk, tm), lambda i, j, k: (k, i)), pl.BlockSpec((tk, tn), lambda i, j, k: (k, j))]
        args += [a, b]
    for arr, kind, off in extras:
        if kind == "tile":
            in_specs.append(pl.BlockSpec((tm, tn), functools.partial(lambda i, j, k, off: (i, j + off), off=off)))
        else:
            in_specs.append(pl.BlockSpec((1, tn), functools.partial(lambda i, j, k, off: (0, j + off), off=off)))
        args.append(arr)
    out_shape = [jax.ShapeDtypeStruct((M, N), dt) for dt in out_dtypes]
    out_specs = [pl.BlockSpec((tm, tn), lambda i, j, k: (i, j)) for _ in out_dtypes]
    out_shape += [jax.ShapeDtypeStruct((nI, 1, N), F32) for _ in range(n_colsum)]
    out_specs += [pl.BlockSpec((1, 1, tn), lambda i, j, k: (i, 0, j)) for _ in range(n_colsum)]
    scratch = [pltpu.VMEM((tm, tn), F32) for _ in range(n_scr)]
    aliases = plumb.aliases(len(args), len(out_shape))
    args += plumb.args
    in_specs += [ANY] * plumb.n_in
    out_shape += plumb.out_shapes
    out_specs += [ANY] * plumb.n_out
    sem = ("arbitrary",) * 3 if comm else ("parallel", "parallel", "arbitrary")
    res = pl.pallas_call(
        body, name=name, grid=(nI, nJ, nK), in_specs=in_specs, out_specs=out_specs, out_shape=out_shape,
        scratch_shapes=scratch + plumb.scratch, input_output_aliases=aliases, compiler_params=_params(sem),
    )(*args)
    n_own = n_out + n_colsum
    return (list(res[:n_own]), plumb.split_outputs(res[n_own:])) if comm is not None else res


ROW_TILE = 512


def _rms_fwd(name, x, g):
    T, D = x.shape

    def body(x_ref, g_ref, o_ref):
        xv = x_ref[...]
        r = lax.rsqrt(jnp.mean(xv * xv, axis=-1, keepdims=True) + RMS_EPS)
        o_ref[...] = (xv * r * g_ref[...]).astype(BF)

    return pl.pallas_call(
        body, name=name, grid=(T // ROW_TILE,),
        in_specs=[pl.BlockSpec((ROW_TILE, D), lambda i: (i, 0)), pl.BlockSpec((1, D), lambda i: (0, 0))],
        out_specs=pl.BlockSpec((ROW_TILE, D), lambda i: (i, 0)),
        out_shape=jax.ShapeDtypeStruct((T, D), BF), compiler_params=_params(("parallel",)),
    )(x, g)


HEADNORM_TILE = 1024


def _half_sum_matrix():
    r = lax.broadcasted_iota(jnp.int32, (LANES, LANES), 0) // HEAD_DIM
    c = lax.broadcasted_iota(jnp.int32, (LANES, LANES), 1) // HEAD_DIM
    return (r == c).astype(BF)


def _head_mean(v, ones_blockdiag):
    hi = v.astype(BF)
    lo = (v - hi.astype(F32)).astype(BF)
    s = jnp.dot(hi, ones_blockdiag, preferred_element_type=F32) + jnp.dot(lo, ones_blockdiag, preferred_element_type=F32)
    return s * (1.0 / HEAD_DIM)


def _headnorm_fwd(name, proj, col0, width, g2):
    T = proj.shape[0]
    nb, off = width // LANES, col0 // LANES

    def body(x_ref, g_ref, b_ref, o_ref):
        xv = x_ref[...].astype(F32)
        r = lax.rsqrt(_head_mean(xv * xv, b_ref[...]) + RMS_EPS)
        o_ref[...] = (xv * r * g_ref[...]).astype(BF)

    return pl.pallas_call(
        body, name=name, grid=(T // HEADNORM_TILE, nb),
        in_specs=[pl.BlockSpec((HEADNORM_TILE, LANES), lambda i, j: (i, j + off)),
                  pl.BlockSpec((1, LANES), lambda i, j: (0, 0)), pl.BlockSpec((LANES, LANES), lambda i, j: (0, 0))],
        out_specs=pl.BlockSpec((HEADNORM_TILE, LANES), lambda i, j: (i, j)),
        out_shape=jax.ShapeDtypeStruct((T, width), BF), compiler_params=_params(("parallel", "parallel")),
    )(proj, g2, _half_sum_matrix())


def _headnorm_bwd(name, dy, proj, col0, width, g2):
    T = proj.shape[0]
    nb, off = width // LANES, col0 // LANES

    def body(dy_ref, x_ref, g_ref, b_ref, dx_ref, dg_ref):
        xv = x_ref[...].astype(F32)
        dyv = dy_ref[...].astype(F32)
        r = lax.rsqrt(_head_mean(xv * xv, b_ref[...]) + RMS_EPS)
        xhat = xv * r
        dxhat = dyv * g_ref[...]
        dx_ref[...] = (r * (dxhat - xhat * _head_mean(dxhat * xhat, b_ref[...]))).astype(BF)
        dg_ref[...] = jnp.sum(dyv * xhat, axis=0, keepdims=True).reshape(dg_ref.shape)

    return pl.pallas_call(
        body, name=name, grid=(T // HEADNORM_TILE, nb),
        in_specs=[pl.BlockSpec((HEADNORM_TILE, LANES), lambda i, j: (i, j)),
                  pl.BlockSpec((HEADNORM_TILE, LANES), lambda i, j: (i, j + off)),
                  pl.BlockSpec((1, LANES), lambda i, j: (0, 0)), pl.BlockSpec((LANES, LANES), lambda i, j: (0, 0))],
        out_specs=[pl.BlockSpec((HEADNORM_TILE, LANES), lambda i, j: (i, j)),
                   pl.BlockSpec((1, 1, LANES), lambda i, j: (i, 0, j))],
        out_shape=[jax.ShapeDtypeStruct((T, width), BF), jax.ShapeDtypeStruct((T // HEADNORM_TILE, 1, width), F32)],
        compiler_params=_params(("parallel", "parallel")),
    )(dy, proj, g2, _half_sum_matrix())


def _shift_down(v, k, row):
    return jnp.where(row >= k, pltpu.roll(v, k, axis=0), 0.0)


def _shift_up(v, k, row, T):
    return jnp.where(row < T - k, pltpu.roll(v, T - k, axis=0), 0.0)


def _by_group(g, vals):
    out = vals[-1]
    for i in range(len(vals) - 2, -1, -1):
        out = jnp.where(g == i, vals[i], out)
    return out


def _pool_fwd(name, proj, pool_w, pool_scale):
    T = proj.shape[0]

    def body(x_ref, w_ref, s_ref, pooled_ref, mixed_ref):
        g = pl.program_id(0)
        xv = x_ref[...].astype(F32)
        row = lax.broadcasted_iota(jnp.int32, (T, 1), 0)
        s2 = xv + _shift_down(xv, 1, row)
        s4 = s2 + _shift_down(s2, 2, row)
        s8 = s4 + _shift_down(s4, 4, row)
        s16 = s8 + _shift_down(s8, 8, row)
        wsum = _by_group(g, [s2, s4, s8, s16])
        count = jnp.minimum(row + 1, 2 << g).astype(F32)
        pooled = (wsum / count - xv).astype(BF)
        pooled_ref[...] = pooled
        mixed = jnp.dot(pooled, w_ref[0].astype(BF), preferred_element_type=F32) * s_ref[...]
        mixed_ref[...] = mixed.astype(BF)

    col = pl.BlockSpec((T, POOL_GROUP), lambda g: (0, g))
    return pl.pallas_call(
        body, name=name, grid=(N_POOL_GROUPS,),
        in_specs=[col, pl.BlockSpec((1, POOL_GROUP, POOL_GROUP), lambda g: (g, 0, 0)),
                  pl.BlockSpec((1, POOL_GROUP), lambda g: (0, g))],
        out_specs=[col, col],
        out_shape=[jax.ShapeDtypeStruct((T, POOL_WIDTH), BF), jax.ShapeDtypeStruct((T, POOL_WIDTH), BF)],
        compiler_params=_params(("parallel",)),
    )(proj, pool_w, pool_scale)


def _pool_bwd(name, dmixed, pooled, pool_w, pool_scale):
    T = dmixed.shape[0]

    def body(dm_ref, p_ref, w_ref, s_ref, dx_ref, dw_ref, ds_ref):
        g = pl.program_id(0)
        dm = dm_ref[...].astype(F32)
        pooled = p_ref[...]
        w = w_ref[0].astype(BF)
        pre = jnp.dot(pooled, w, preferred_element_type=F32)
        ds_ref[...] = jnp.sum(dm * pre, axis=0, keepdims=True)
        dms = (dm * s_ref[...]).astype(BF)
        dw_ref[0] = lax.dot_general(pooled, dms, _DIMS["tn"], preferred_element_type=F32)
        dpooled = lax.dot_general(dms, w, _DIMS["nt"], preferred_element_type=F32)
        row = lax.broadcasted_iota(jnp.int32, (T, 1), 0)
        count = jnp.minimum(row + 1, 2 << g).astype(F32)
        z = dpooled / count
        l2 = z + _shift_up(z, 1, row, T)
        l4 = l2 + _shift_up(l2, 2, row, T)
        l8 = l4 + _shift_up(l4, 4, row, T)
        l16 = l8 + _shift_up(l8, 8, row, T)
        dx_ref[...] = (_by_group(g, [l2, l4, l8, l16]) - dpooled).astype(BF)

    col = pl.BlockSpec((T, POOL_GROUP), lambda g: (0, g))
    wspec = pl.BlockSpec((1, POOL_GROUP, POOL_GROUP), lambda g: (g, 0, 0))
    sspec = pl.BlockSpec((1, POOL_GROUP), lambda g: (0, g))
    return pl.pallas_call(
        body, name=name, grid=(N_POOL_GROUPS,), in_specs=[col, col, wspec, sspec], out_specs=[col, wspec, sspec],
        out_shape=[jax.ShapeDtypeStruct((T, POOL_WIDTH), BF),
                   jax.ShapeDtypeStruct((N_POOL_GROUPS, POOL_GROUP, POOL_GROUP), F32),
                   jax.ShapeDtypeStruct((1, POOL_WIDTH), F32)],
        compiler_params=_params(("parallel",)),
    )(dmixed, pooled, pool_w, pool_scale)


ATTN_SCALE = HEAD_DIM ** -0.5
MASKED = float(jnp.finfo(jnp.float32).min)
KV_COL_BLOCK_K = COL_K // LANES
KV_COL_BLOCK_V = COL_V // LANES
GROUP_WIDTH = GQA_GROUP * HEAD_DIM


def _dup_head(v, j):
    half = lax.broadcasted_iota(jnp.int32, (1, LANES), 1) // HEAD_DIM
    return jnp.where(half == j, v, pltpu.roll(v, HEAD_DIM, axis=1))


def _stack_heads(v, low):
    pieces = []
    for p in range(GROUP_WIDTH // LANES):
        vp = v[:, LANES * p: LANES * (p + 1)]
        pieces.append(jnp.where(low, vp, jnp.zeros_like(vp)))
        pieces.append(jnp.where(low, jnp.zeros_like(vp), vp))
    return jnp.concatenate(pieces, axis=0)


def _unstack_heads(st, low):
    pieces = []
    for p in range(GROUP_WIDTH // LANES):
        even = st[BLOCK * (2 * p): BLOCK * (2 * p + 1)]
        odd = st[BLOCK * (2 * p + 1): BLOCK * (2 * p + 2)]
        pieces.append(jnp.where(low, even, odd))
    return jnp.concatenate(pieces, axis=1)


def _band_mask(n):
    row = lax.broadcasted_iota(jnp.int32, (BLOCK, 2 * BLOCK), 0)
    col = lax.broadcasted_iota(jnp.int32, (BLOCK, 2 * BLOCK), 1)
    return (col > row) & (col <= row + BLOCK) & ((n > 0) | (col >= BLOCK))


def _softmax_heads(s, valid, sink_ref, j):
    ps, psinks = [], []
    for h in range(GQA_GROUP):
        sh = jnp.where(valid, s[BLOCK * h: BLOCK * (h + 1)], MASKED)
        sink = sink_ref[j * GQA_GROUP + h]
        m = jnp.maximum(jnp.max(sh, axis=1, keepdims=True), sink)
        e = jnp.exp(sh - m)
        es = jnp.exp(sink - m)
        inv = 1.0 / (jnp.sum(e, axis=1, keepdims=True) + es)
        ps.append(e * inv)
        psinks.append(es * inv)
    return jnp.concatenate(ps, axis=0), jnp.concatenate(psinks, axis=0)


def _attn_fwd(name, qn, kn, proj, sinks):
    T = qn.shape[0]
    nb = T // BLOCK

    def body(sink_ref, q_ref, kp_ref, kc_ref, vp_ref, vc_ref, o_ref):
        n, j = pl.program_id(0), pl.program_id(1)
        low = lax.broadcasted_iota(jnp.int32, (1, LANES), 1) < HEAD_DIM
        k2 = _dup_head(jnp.concatenate([kp_ref[...], kc_ref[...]], axis=0), j)
        v2 = _dup_head(jnp.concatenate([vp_ref[...], vc_ref[...]], axis=0), j)
        q = _stack_heads(q_ref[...], low)
        s = lax.dot_general(q, k2, _DIMS["nt"], preferred_element_type=F32) * ATTN_SCALE
        p, _ = _softmax_heads(s, _band_mask(n), sink_ref, j)
        o = jnp.dot(p.astype(BF), v2, preferred_element_type=F32)
        o_ref[...] = _unstack_heads(o, low).astype(BF)

    prev = lambda n, j: (jnp.maximum(n - 1, 0), 0)
    return pl.pallas_call(
        body, name=name, grid=(nb, 2),
        in_specs=[pl.BlockSpec(memory_space=pltpu.SMEM),
                  pl.BlockSpec((BLOCK, GROUP_WIDTH), lambda n, j: (n, j)),
                  pl.BlockSpec((BLOCK, LANES), prev), pl.BlockSpec((BLOCK, LANES), lambda n, j: (n, 0)),
                  pl.BlockSpec((BLOCK, LANES), lambda n, j: (jnp.maximum(n - 1, 0), KV_COL_BLOCK_V)),
                  pl.BlockSpec((BLOCK, LANES), lambda n, j: (n, KV_COL_BLOCK_V))],
        out_specs=pl.BlockSpec((BLOCK, GROUP_WIDTH), lambda n, j: (n, j)),
        out_shape=jax.ShapeDtypeStruct((T, ATTN_WIDTH), BF), compiler_params=_params(("parallel", "parallel")),
    )(sinks, qn, kn, kn, proj, proj)


def _attn_bwd(name, dout, qn, kn, proj, sinks):
    T = qn.shape[0]
    nb = T // BLOCK

    def body(sink_ref, do_ref, q_ref, kp_ref, kc_ref, vp_ref, vc_ref, dq_ref, dk_ref, dv_ref, dsink_ref,
             carry_k, carry_v, tot_k, tot_v):
        n = pl.program_id(0)
        lane = lax.broadcasted_iota(jnp.int32, (1, LANES), 1)
        low = lane < HEAD_DIM

        @pl.when(n == 0)
        def _():
            carry_k[...] = jnp.zeros_like(carry_k)
            carry_v[...] = jnp.zeros_like(carry_v)
            dsink_ref[...] = jnp.zeros_like(dsink_ref)

        @pl.when(n == nb)
        def _():
            tot_k[...] = jnp.zeros_like(tot_k)
            tot_v[...] = jnp.zeros_like(tot_v)

        @pl.when(n < nb)
        def _():
            kk = jnp.concatenate([kp_ref[...], kc_ref[...]], axis=0)
            vv = jnp.concatenate([vp_ref[...], vc_ref[...]], axis=0)
            valid = _band_mask(n)
            dk_tot = jnp.zeros((2 * BLOCK, LANES), F32)
            dv_tot = jnp.zeros((2 * BLOCK, LANES), F32)
            dsink = jnp.zeros((1, LANES), F32)
            for j in range(2):
                k2 = _dup_head(kk, j)
                v2 = _dup_head(vv, j)
                q = _stack_heads(q_ref[:, GROUP_WIDTH * j: GROUP_WIDTH * (j + 1)], low)
                do = _stack_heads(do_ref[:, GROUP_WIDTH * j: GROUP_WIDTH * (j + 1)], low)
                s = lax.dot_general(q, k2, _DIMS["nt"], preferred_element_type=F32) * ATTN_SCALE
                p, psink = _softmax_heads(s, valid, sink_ref, j)
                dp = lax.dot_general(do, v2, _DIMS["nt"], preferred_element_type=F32)
                delta = jnp.sum(p * dp, axis=1, keepdims=True)
                ds = (p * (dp - delta) * ATTN_SCALE).astype(BF)
                dq_ref[:, GROUP_WIDTH * j: GROUP_WIDTH * (j + 1)] = _unstack_heads(
                    jnp.dot(ds, k2, preferred_element_type=F32), low).astype(BF)
                dk2 = lax.dot_general(ds, q, _DIMS["tn"], preferred_element_type=F32)
                dv2 = lax.dot_general(p.astype(BF), do, _DIMS["tn"], preferred_element_type=F32)
                mine = low if j == 0 else jnp.logical_not(low)
                dk_tot = dk_tot + jnp.where(mine, dk2 + pltpu.roll(dk2, HEAD_DIM, axis=1), 0.0)
                dv_tot = dv_tot + jnp.where(mine, dv2 + pltpu.roll(dv2, HEAD_DIM, axis=1), 0.0)
                sink_term = psink * delta
                for h in range(GQA_GROUP):
                    val = -jnp.sum(sink_term[BLOCK * h: BLOCK * (h + 1)], axis=0, keepdims=True)
                    dsink = dsink + jnp.where(lane == j * GQA_GROUP + h, val, 0.0)
            tot_k[...] = dk_tot
            tot_v[...] = dv_tot
            dsink_ref[0:1, :] += dsink

        dk_ref[...] = (carry_k[...] + tot_k[0:BLOCK]).astype(BF)
        dv_ref[...] = (carry_v[...] + tot_v[0:BLOCK]).astype(BF)
        carry_k[...] = tot_k[BLOCK:]
        carry_v[...] = tot_v[BLOCK:]

    cur = lambda n: (jnp.minimum(n, nb - 1), 0)
    prev = lambda n: (jnp.maximum(n - 1, 0), 0)
    wide = pl.BlockSpec((BLOCK, ATTN_WIDTH), cur)
    return pl.pallas_call(
        body, name=name, grid=(nb + 1,),
        in_specs=[pl.BlockSpec(memory_space=pltpu.SMEM), wide, wide,
                  pl.BlockSpec((BLOCK, LANES), prev), pl.BlockSpec((BLOCK, LANES), cur),
                  pl.BlockSpec((BLOCK, LANES), lambda n: (jnp.maximum(n - 1, 0), KV_COL_BLOCK_V)),
                  pl.BlockSpec((BLOCK, LANES), lambda n: (jnp.minimum(n, nb - 1), KV_COL_BLOCK_V))],
        out_specs=[wide, pl.BlockSpec((BLOCK, LANES), prev), pl.BlockSpec((BLOCK, LANES), prev),
                   pl.BlockSpec((8, LANES), lambda n: (0, 0))],
        out_shape=[jax.ShapeDtypeStruct((T, ATTN_WIDTH), BF), jax.ShapeDtypeStruct((T, KV_WIDTH), BF),
                   jax.ShapeDtypeStruct((T, KV_WIDTH), BF), jax.ShapeDtypeStruct((8, LANES), F32)],
        scratch_shapes=[pltpu.VMEM((BLOCK, LANES), F32), pltpu.VMEM((BLOCK, LANES), F32),
                        pltpu.VMEM((2 * BLOCK, LANES), F32), pltpu.VMEM((2 * BLOCK, LANES), F32)],
        compiler_params=_params(("arbitrary",)),
    )(sinks, dout, qn, kn, kn, proj, proj)


def _swiglu_fwd_epilogue(accs, ex):
    g, u = accs
    return [g, u, g * jax.nn.sigmoid(g) * u], []


def _swiglu_bwd_epilogue(accs, ex):
    (da,) = accs
    g, u = ex[0].astype(F32), ex[1].astype(F32)
    s = jax.nn.sigmoid(g)
    silu = g * s
    return [da * u * (s * (1.0 + g * (1.0 - s))), da * silu, silu * u], []


def _half_residual_epilogue(accs, ex):
    return [ex[0] + 0.5 * accs[0]], []


def _residual_epilogue(accs, ex):
    return [ex[0] + accs[0]], []


def _rms_bwd_epilogue(accs, ex):
    (dn,) = accs
    xv, g, dres = ex
    r = lax.rsqrt(jnp.mean(xv * xv, axis=-1, keepdims=True) + RMS_EPS)
    xhat = xv * r
    dxhat = dn * g
    dx = dres + r * (dxhat - xhat * jnp.mean(dxhat * xhat, axis=-1, keepdims=True))
    return [dx, dx], [dn * xhat]


def _loss_epilogue(accs, ex):
    xv, target = ex
    d = xv + 0.5 * accs[0] - target
    dy = d * (1.0 / D_MODEL)
    return [dy, dy], [d * d]


def _merge_fwd_epilogue(accs, ex):
    (ba,) = accs
    bp, gp_pre, ga_pre, bias_p, bias_a = ex
    gp = jax.nn.sigmoid(gp_pre.astype(F32) + bias_p)
    ga = jax.nn.sigmoid(ga_pre.astype(F32) + bias_a)
    return [gp * bp.astype(F32) + ga * ba, ba], []


def _merge_bwd_epilogue(accs, ex):
    (dm,) = accs
    bp, ba, gp_pre, ga_pre, bias_p, bias_a = ex
    gp = jax.nn.sigmoid(gp_pre.astype(F32) + bias_p)
    ga = jax.nn.sigmoid(ga_pre.astype(F32) + bias_a)
    dgp = dm * bp.astype(F32) * gp * (1.0 - gp)
    dga = dm * ba.astype(F32) * ga * (1.0 - ga)
    return [dm * gp, dm * ga, dgp, dga], [dgp, dga]


def _prep(name, ws, transposes):
    n = len(ws)

    def body(*refs):
        for w_ref, o_ref, tr in zip(refs[:n], refs[n:], transposes):
            v = w_ref[...]
            o_ref[...] = (v.T if tr else v).astype(BF)

    shapes = [jax.ShapeDtypeStruct(w.shape[::-1] if tr else w.shape, BF) for w, tr in zip(ws, transposes)]
    return pl.pallas_call(body, name=name, out_shape=shapes, compiler_params=_params())(*ws)


def _adam_math(w, g, m, v):
    m = ADAM_B1 * m + (1.0 - ADAM_B1) * g
    v = ADAM_B2 * v + (1.0 - ADAM_B2) * jnp.square(g)
    m_hat = m / (1.0 - ADAM_B1 ** ADAM_STEP)
    v_hat = v / (1.0 - ADAM_B2 ** ADAM_STEP)
    delta = -ADAM_LR * (m_hat / (jnp.sqrt(v_hat) + ADAM_EPS) + ADAM_WD * w)
    return delta, m, v


def _adamw_sharded(name, slots, w, m, v, transpose):
    def body(s_ref, w_ref, m_ref, v_ref, g_out, d_out, m_out, v_out):
        g = s_ref[0].astype(F32)
        for i in range(1, 4):
            g = g + s_ref[i].astype(F32)
        if transpose:
            g = g.T
        delta, mn, vn = _adam_math(w_ref[...], g, m_ref[...], v_ref[...])
        g_out[...] = g
        d_out[...] = delta
        m_out[...] = mn
        v_out[...] = vn

    return pl.pallas_call(
        body, name=name, out_shape=[jax.ShapeDtypeStruct(w.shape, F32)] * 4, compiler_params=_params(),
    )(slots, w, m, v)


def _adamw_small(name, gathered, w, m, v):
    def body(s_ref, w_ref, m_ref, v_ref, g_out, d_out, m_out, v_out):
        g = s_ref[0]
        for i in range(1, N_DEV):
            g = g + s_ref[i]
        delta, mn, vn = _adam_math(w_ref[...], g, m_ref[...], v_ref[...])
        g_out[...] = g
        d_out[...] = delta
        m_out[...] = mn
        v_out[...] = vn

    return pl.pallas_call(
        body, name=name, out_shape=[jax.ShapeDtypeStruct(w.shape, F32)] * 4, compiler_params=_params(),
    )(gathered, w, m, v)


def _place():
    x, y, c = lax.axis_index("x"), lax.axis_index("y"), lax.axis_index("c")
    other_chips = [(1 - x, y), (x, 1 - y), (1 - x, 1 - y)]
    return x, y, c, other_chips


def _rows(ref, r, place, natural=False):
    px, py, pc = place
    b = 4 * px + 2 * py + pc if natural else 4 * pc + 2 * px + py
    return ref.at[pl.ds(pl.multiple_of(b * r, 8), r), :]


def _gather_send_task(shards, natural=()):
    n = len(shards)
    rs = [s.shape[0] for s in shards]
    rows_of = lambda ref, k, place: _rows(ref, rs[k], place, k in natural)

    def copies(ins, outs, scr):
        send_sems, recv_sems, local_sems = scr
        x, y, c, chips = _place()
        me = (x, y, c)
        peers = [(x, y, 1 - c)] + [(*chip, c) for chip in chips]
        local = [pltpu.make_async_copy(ins[k], rows_of(outs[k], k, me), local_sems.at[k]) for k in range(n)]
        sends, recvs = [], []
        for s in (1, 2, 3, 0):
            for k in range(n):
                sems = dict(send_sem=send_sems.at[4 * k + s], recv_sem=recv_sems.at[4 * k + s], device_id_type=MESH)
                sends.append(pltpu.make_async_remote_copy(
                    src_ref=ins[k], dst_ref=rows_of(outs[k], k, me), device_id=peers[s], **sems))
                theirs = rows_of(outs[k], k, peers[s])
                recvs.append(pltpu.make_async_remote_copy(src_ref=theirs, dst_ref=theirs, device_id=me, **sems))
        return local, sends, recvs

    def start(ins, outs, scr):
        local, sends, _ = copies(ins, outs, scr)
        for cp in local + sends:
            cp.start()

    def finish(ins, outs, scr):
        local, sends, recvs = copies(ins, outs, scr)
        for cp in recvs:
            cp.wait_recv()
        for cp in sends:
            cp.wait_send()
        for cp in local:
            cp.wait()

    out_shapes = [jax.ShapeDtypeStruct((N_DEV * s.shape[0], s.shape[1]), s.dtype) for s in shards]
    scratch = [pltpu.SemaphoreType.DMA((4 * n,)), pltpu.SemaphoreType.DMA((4 * n,)), pltpu.SemaphoreType.DMA((n,))]
    return _Task(shards, out_shapes, scratch, start, finish)


def _gather_forward_task(fulls, natural=()):
    n = len(fulls)
    rs = [f.shape[0] // N_DEV for f in fulls]
    rows_of = lambda ref, k, place: _rows(ref, rs[k], place, k in natural)

    def copies(outs, scr):
        send_sems, recv_sems = scr
        x, y, c, chips = _place()
        sends, recvs = [], []
        for j, chip in enumerate(chips):
            for k in range(n):
                sems = dict(send_sem=send_sems.at[3 * k + j], recv_sem=recv_sems.at[3 * k + j], device_id_type=MESH)
                got = rows_of(outs[k], k, (*chip, c))
                sends.append(pltpu.make_async_remote_copy(src_ref=got, dst_ref=got, device_id=(x, y, 1 - c), **sems))
                theirs = rows_of(outs[k], k, (*chip, 1 - c))
                recvs.append(pltpu.make_async_remote_copy(src_ref=theirs, dst_ref=theirs, device_id=(x, y, c), **sems))
        return sends, recvs

    def start(ins, outs, scr):
        for cp in copies(outs, scr)[0]:
            cp.start()

    def finish(ins, outs, scr):
        sends, recvs = copies(outs, scr)
        for cp in recvs:
            cp.wait_recv()
        for cp in sends:
            cp.wait_send()

    out_shapes = [jax.ShapeDtypeStruct(f.shape, f.dtype) for f in fulls]
    scratch = [pltpu.SemaphoreType.DMA((3 * n,)), pltpu.SemaphoreType.DMA((3 * n,))]
    return _Task(fulls, out_shapes, scratch, start, finish, aliases={k: k for k in range(n)})


def _chip_task(sums):
    n = len(sums)
    rs = [s.shape[0] // 4 for s in sums]

    def block(ref, k, chip_index):
        return ref.at[pl.ds(pl.multiple_of(chip_index * rs[k], 8), rs[k]), :]

    def copies(ins, outs, scr):
        send_sems, recv_sems, local_sems = scr
        x, y, c, chips = _place()
        here = 2 * x + y
        local = [pltpu.make_async_copy(block(ins[k], k, here), outs[k].at[here], local_sems.at[k]) for k in range(n)]
        remote = []
        for j, (px, py) in enumerate(chips):
            remote += [pltpu.make_async_remote_copy(
                src_ref=block(ins[k], k, 2 * px + py), dst_ref=outs[k].at[here],
                send_sem=send_sems.at[3 * k + j], recv_sem=recv_sems.at[3 * k + j],
                device_id=(px, py, c), device_id_type=MESH) for k in range(n)]
        return local, remote

    def start(ins, outs, scr):
        local, remote = copies(ins, outs, scr)
        for cp in local + remote:
            cp.start()

    def finish(ins, outs, scr):
        local, remote = copies(ins, outs, scr)
        for cp in remote:
            cp.wait()
        for cp in local:
            cp.wait()

    out_shapes = [jax.ShapeDtypeStruct((4, r, s.shape[1]), s.dtype) for r, s in zip(rs, sums)]
    scratch = [pltpu.SemaphoreType.DMA((3 * n,)), pltpu.SemaphoreType.DMA((3 * n,)), pltpu.SemaphoreType.DMA((n,))]
    return _Task(sums, out_shapes, scratch, start, finish)


def _dw_pair(name, a, b, scale, comm=None, blocks=1):
    T, M = a.shape
    N = b.shape[1]
    half = M // 2
    wide = half // blocks
    tk = min(512, T)
    nK = T // tk
    plumb = _CommPlumbing(comm)

    def body(core_ref, *rest):
        a_refs, b_ref, rest = rest[:blocks], rest[blocks], rest[blocks + 1:]
        c_in = rest[:plumb.n_in]
        o_ref = rest[plumb.n_in]
        c_out = rest[plumb.n_in + 1: plumb.n_in + 1 + plumb.n_out]
        acc, stage, land, send_sem, recv_sem = rest[plumb.n_in + 1 + plumb.n_out: plumb.n_in + 6 + plumb.n_out]
        c_scr = rest[plumb.n_in + 6 + plumb.n_out:]
        i, k = pl.program_id(0), pl.program_id(1)
        x, y, c, _ = _place()
        push = pltpu.make_async_remote_copy(src_ref=stage, dst_ref=land, send_sem=send_sem, recv_sem=recv_sem,
                                            device_id=(x, y, 1 - c), device_id_type=MESH)
        if comm:
            @pl.when((i == 0) & (k == 0))
            def _():
                plumb.start(c_in, c_out, c_scr)

        av = a_refs[0][...] if blocks == 1 else jnp.concatenate([r[...] for r in a_refs], axis=1)
        p = lax.dot_general(av, b_ref[...], _DIMS["tn"], preferred_element_type=F32)

        @pl.when(k == 0)
        def _():
            acc[...] = p

        @pl.when(k > 0)
        def _():
            acc[...] += p

        @pl.when((i == 0) & (k == nK - 1))
        def _():
            stage[...] = (scale * acc[...]).astype(BF)
            push.start()

        @pl.when((i == 1) & (k == nK - 1))
        def _():
            push.wait_recv()
            o_ref[...] = (scale * acc[...] + land[...].astype(F32)).astype(BF)
            push.wait_send()
            if comm:
                plumb.finish(c_in, c_out, c_scr)

    grid_spec = pltpu.PrefetchScalarGridSpec(
        num_scalar_prefetch=1, grid=(2, nK),
        in_specs=[pl.BlockSpec((tk, wide), functools.partial(
            lambda i, k, core, j: (k, (2 * j if blocks > 1 else 0) + jnp.where(i == 0, 1 - core[0], core[0])), j=j))
            for j in range(blocks)] + [pl.BlockSpec((tk, N), lambda i, k, core: (k, 0))] + [ANY] * plumb.n_in,
        out_specs=[pl.BlockSpec((half, N), lambda i, k, core: (0, 0))] + [ANY] * plumb.n_out,
        scratch_shapes=[pltpu.VMEM((half, N), F32), pltpu.VMEM((half, N), BF), pltpu.VMEM((half, N), BF),
                        pltpu.SemaphoreType.DMA, pltpu.SemaphoreType.DMA] + plumb.scratch)
    core = lax.axis_index("c").astype(jnp.int32).reshape(1)
    res = pl.pallas_call(
        body, name=name, grid_spec=grid_spec,
        out_shape=[jax.ShapeDtypeStruct((half, N), BF)] + plumb.out_shapes,
        compiler_params=_params(("arbitrary", "arbitrary")),
    )(core, *([a] * blocks), b, *plumb.args)
    return (res[0], plumb.split_outputs(res[1:])) if comm else res[0]


def _all_gather(name, shards):
    n = len(shards)

    def body(*refs):
        ins, outs = refs[:n], refs[n:2 * n]
        send_sems, recv_sems, local_sems = refs[2 * n:]
        x, y, c, chips = _place()
        me, sibling = (x, y, c), (x, y, 1 - c)

        def rows(k, px, py, pc):
            return _rows(outs[k], shards[k].shape[0], (px, py, pc))

        def copy(k, slot, block, to, src=None):
            return pltpu.make_async_remote_copy(
                src_ref=rows(k, *block) if src is None else src, dst_ref=rows(k, *block),
                send_sem=send_sems.at[7 * k + slot], recv_sem=recv_sems.at[7 * k + slot],
                device_id=to, device_id_type=MESH)

        mine = [pltpu.make_async_copy(ins[k], rows(k, *me), local_sems.at[k]) for k in range(n)]
        for cp in mine:
            cp.start()
        first = []
        for j, chip in enumerate(chips):
            first += [copy(k, 1 + j, me, (*chip, c), src=ins[k]) for k in range(n)]
        first += [copy(k, 0, me, sibling, src=ins[k]) for k in range(n)]
        for cp in first:
            cp.start()
        passed = []
        for j, chip in enumerate(chips):
            for k in range(n):
                copy(k, 1 + j, (*chip, c), me).wait_recv()
                fwd = copy(k, 4 + j, (*chip, c), sibling)
                fwd.start()
                passed.append(fwd)
        for k in range(n):
            copy(k, 0, sibling, me).wait_recv()
        for j, chip in enumerate(chips):
            for k in range(n):
                copy(k, 4 + j, (*chip, 1 - c), me).wait_recv()
        for cp in first + passed:
            cp.wait_send()
        for cp in mine:
            cp.wait()

    return pl.pallas_call(
        body, name=name, in_specs=[ANY] * n, out_specs=[ANY] * n,
        out_shape=[jax.ShapeDtypeStruct((N_DEV * s.shape[0], s.shape[1]), s.dtype) for s in shards],
        scratch_shapes=[pltpu.SemaphoreType.DMA((7 * n,)), pltpu.SemaphoreType.DMA((7 * n,)),
                        pltpu.SemaphoreType.DMA((n,))],
        compiler_params=pltpu.CompilerParams(has_side_effects=True),
    )(*shards)


def _pair_exchange(name, parts):
    n = len(parts)

    def body(*refs):
        ins, outs = refs[:n], refs[n:2 * n]
        send_sems, recv_sems = refs[2 * n:]
        x, y, c, _ = _place()
        copies = [pltpu.make_async_remote_copy(
            src_ref=ins[k].at[:, pl.ds(1 - c, 1)], dst_ref=outs[k], send_sem=send_sems.at[k], recv_sem=recv_sems.at[k],
            device_id=(x, y, 1 - c), device_id_type=MESH) for k in range(n)]
        for cp in copies:
            cp.start()
        for cp in copies:
            cp.wait()

    return pl.pallas_call(
        body, name=name, in_specs=[ANY] * n, out_specs=[ANY] * n,
        out_shape=[jax.ShapeDtypeStruct((4, 1) + p.shape[2:], p.dtype) for p in parts],
        scratch_shapes=[pltpu.SemaphoreType.DMA((n,)), pltpu.SemaphoreType.DMA((n,))],
        compiler_params=pltpu.CompilerParams(has_side_effects=True),
    )(*parts)


def _pair_sum(name, part, got, core):
    _, _, r, C = part.shape

    def body(core_ref, p_ref, g_ref, o_ref):
        o_ref[0] = (p_ref[0, 0].astype(F32) + g_ref[0, 0].astype(F32)).astype(o_ref.dtype)

    return pl.pallas_call(
        body, name=name,
        grid_spec=pltpu.PrefetchScalarGridSpec(
            num_scalar_prefetch=1, grid=(4,),
            in_specs=[pl.BlockSpec((1, 1, r, C), lambda i, core_ref: (i, core_ref[0], 0, 0)),
                      pl.BlockSpec((1, 1, r, C), lambda i, core_ref: (i, 0, 0, 0))],
            out_specs=pl.BlockSpec((1, r, C), lambda i, core_ref: (i, 0, 0))),
        out_shape=jax.ShapeDtypeStruct((4, r, C), part.dtype), compiler_params=_params(("parallel",)),
    )(core, part, got)


def _ffn_bwd(tag, dy, dyb, x, gain, wgT, wuT, wd, saved, pending):
    n, g, u = saved
    half = lambda accs, ex: _swiglu_bwd_epilogue([0.5 * accs[0]], ex)
    (dg, du, a), done0 = _mm(tag + "_d_act", [(dyb, wd, "nt", 0)], [BF, BF, BF], tm=512, tn=1408, tk=D_MODEL,
                             epilogue=half, extras=[(g, "tile", 0), (u, "tile", 0)], comm=pending)
    sum_d = _dw_pair(tag + "_dw_down", a, dyb, 0.5)
    sum_g, (slots_d,) = _dw_pair(tag + "_dw_gate", dg, n, 1.0, comm=[_chip_task([sum_d])])
    sum_u, (slots_g,) = _dw_pair(tag + "_dw_up", du, n, 1.0, comm=[_chip_task([sum_g])])
    (dx, dxb, dgain), (slots_u,) = _mm(
        tag + "_d_norm", [(dg, wgT, "nn", 0), (du, wuT, "nn", 0)], [F32, BF], tm=512, tn=D_MODEL, tk=1408,
        epilogue=_rms_bwd_epilogue, extras=[(x, "tile", 0), (gain, "row", 0), (dy, "tile", 0)], n_colsum=1,
        comm=[_chip_task([sum_u])])
    return dx, dxb, dgain, done0, slots_g[0], slots_u[0], slots_d[0]


def _tile_gain(g):
    return jnp.concatenate([g, g]).reshape(1, LANES)


def _fold_heads(partials):
    return jnp.sum(partials.reshape(-1, HEAD_DIM), axis=0)


def _pack_small(norm1, mixn, norm2, pool_w, pool_scale, qn, kn, sinks, gate_bias, last):
    pad = lambda v: jnp.pad(v.reshape(-1), (0, LANES - v.size)).reshape(1, LANES)
    return jnp.concatenate([
        norm1.reshape(-1, LANES), mixn.reshape(-1, LANES), norm2.reshape(-1, LANES), pool_w.reshape(-1, LANES),
        pool_scale.reshape(-1, LANES), pad(qn), pad(kn), pad(sinks), gate_bias.reshape(-1, LANES), pad(last)], axis=0)


def _unpack_small(p):
    o, out = 0, []
    for rows, shape in ((8, (D_MODEL,)), (8, (D_MODEL,)), (8, (D_MODEL,)), (512, (N_POOL_GROUPS, POOL_GROUP, POOL_GROUP)),
                        (4, (POOL_WIDTH,)), (1, (HEAD_DIM,)), (1, (HEAD_DIM,)), (1, (N_HEADS,)), (16, (2 * D_MODEL,)), (1, (1,))):
        flat = p[o:o + rows].reshape(-1)
        size = 1
        for s in shape:
            size *= s
        out.append(flat[:size].reshape(shape))
        o += rows
    return out


def kernel(x, ffn1_norm, ffn1_w_gate, ffn1_w_up, ffn1_w_down, mix_norm, w_in, pool_w, pool_scale, w_pool_out, q_norm, k_norm, sinks, w_attn_out, gate_bias, w_out, ffn2_norm, ffn2_w_gate, ffn2_w_up, ffn2_w_down, loss_target, m_ffn1_norm, m_ffn1_w_gate, m_ffn1_w_up, m_ffn1_w_down, m_mix_norm, m_w_in, m_pool_w, m_pool_scale, m_w_pool_out, m_q_norm, m_k_norm, m_sinks, m_w_attn_out, m_gate_bias, m_w_out, m_ffn2_norm, m_ffn2_w_gate, m_ffn2_w_up, m_ffn2_w_down, v_ffn1_norm, v_ffn1_w_gate, v_ffn1_w_up, v_ffn1_w_down, v_mix_norm, v_w_in, v_pool_w, v_pool_scale, v_w_pool_out, v_q_norm, v_k_norm, v_sinks, v_w_attn_out, v_gate_bias, v_w_out, v_ffn2_norm, v_ffn2_w_gate, v_ffn2_w_up, v_ffn2_w_down):
    T = x.shape[1]
    x2 = x.reshape(T, D_MODEL)
    target = loss_target.reshape(T, D_MODEL)

    big = [
        ("ffn1_w_gate", ffn1_w_gate, m_ffn1_w_gate, v_ffn1_w_gate, True, False),
        ("ffn1_w_up", ffn1_w_up, m_ffn1_w_up, v_ffn1_w_up, True, False),
        ("ffn1_w_down", ffn1_w_down, m_ffn1_w_down, v_ffn1_w_down, False, False),
        ("w_in", w_in, m_w_in, v_w_in, True, False),
        ("w_pool_out", w_pool_out, m_w_pool_out, v_w_pool_out, False, True),
        ("w_attn_out", w_attn_out, m_w_attn_out, v_w_attn_out, False, False),
        ("w_out", w_out, m_w_out, v_w_out, False, False),
        ("ffn2_w_gate", ffn2_w_gate, m_ffn2_w_gate, v_ffn2_w_gate, True, False),
        ("ffn2_w_up", ffn2_w_up, m_ffn2_w_up, v_ffn2_w_up, True, False),
        ("ffn2_w_down", ffn2_w_down, m_ffn2_w_down, v_ffn2_w_down, False, False),
    ]
    view = lambda a, tv: a.T if tv else a
    shards = _prep("prep_weights", [view(w, tv) for _, w, _, _, tv, _ in big], [tk_ for *_, tk_ in big])
    mixer_natural = (0, 1, 2, 3)
    wg1T, wu1T, wd1 = _all_gather("gather_ffn1", shards[0:3])

    g1 = ffn1_norm.reshape(1, D_MODEL)
    g2 = mix_norm.reshape(1, D_MODEL)
    g3 = ffn2_norm.reshape(1, D_MODEL)
    bias_row = gate_bias.reshape(1, 2 * D_MODEL)
    qg, kg = _tile_gain(q_norm), _tile_gain(k_norm)
    scale_row = pool_scale.reshape(1, POOL_WIDTH)

    n1 = _rms_fwd("ffn1_norm", x2, g1)
    (gt1, up1, act1), (mixer_part,) = _mm(
        "ffn1_gate_up", [(n1, wg1T, "nt", 0), (n1, wu1T, "nt", 1)], [BF, BF, BF], tm=512, tn=1408, tk=D_MODEL,
        epilogue=_swiglu_fwd_epilogue, comm=[_gather_send_task(shards[3:7], mixer_natural)])
    (h1,), (mixer_full, wg2_part) = _mm(
        "ffn1_down", [(act1, wd1, "nn", 0)], [F32], tm=512, tn=D_MODEL, tk=D_FF, epilogue=_half_residual_epilogue,
        extras=[(x2, "tile", 0)], comm=[_gather_forward_task(mixer_part, mixer_natural), _gather_send_task(shards[7:8])])
    w_inT, w_poT, w_ao, w_o = mixer_full
    saved1 = (n1, gt1, up1)
    u = _rms_fwd("mix_norm", h1, g2)
    (proj,), ((wg2T,), ud2_part) = _mm(
        "in_proj", [(u, w_inT, "nt", 0)], [BF], tm=512, tn=1280, tk=D_MODEL,
        comm=[_gather_forward_task(wg2_part), _gather_send_task(shards[8:10])])
    pooled, mixed = _pool_fwd("pool_fwd", proj, pool_w, scale_row)
    qn = _headnorm_fwd("q_norm", proj, COL_Q, ATTN_WIDTH, qg)
    kn = _headnorm_fwd("k_norm", proj, COL_K, KV_WIDTH, kg)
    attn = _attn_fwd("attn_fwd", qn, kn, proj, sinks)
    (bp,) = _mm("pool_out", [(mixed, w_poT, "nt", 0)], [BF], tm=1024, tn=D_MODEL, tk=POOL_WIDTH)
    gate_tn = 256
    gate_extras = [(proj, "tile", COL_GP // gate_tn), (proj, "tile", COL_GA // gate_tn),
                   (bias_row, "row", 0), (bias_row, "row", D_MODEL // gate_tn)]
    (merged, ba), ((wu2T, wd2),) = _mm(
        "attn_out_merge", [(attn, w_ao, "nn", 0)], [BF, BF], tm=2048, tn=gate_tn, tk=ATTN_WIDTH,
        epilogue=_merge_fwd_epilogue, extras=[(bp, "tile", 0)] + gate_extras, comm=[_gather_forward_task(ud2_part)])
    (h2,) = _mm("mix_out", [(merged, w_o, "nn", 0)], [F32], tm=512, tn=D_MODEL, tk=D_MODEL,
                epilogue=_residual_epilogue, extras=[(h1, "tile", 0)])
    n2 = _rms_fwd("ffn2_norm", h2, g3)
    gt2, up2, act2 = _mm("ffn2_gate_up", [(n2, wg2T, "nt", 0), (n2, wu2T, "nt", 1)], [BF, BF, BF],
                         tm=512, tn=1408, tk=D_MODEL, epilogue=_swiglu_fwd_epilogue)
    dy, dyb, sq = _mm("ffn2_down_loss", [(act2, wd2, "nn", 0)], [F32, BF], tm=512, tn=D_MODEL, tk=D_FF,
                      epilogue=_loss_epilogue, extras=[(h2, "tile", 0), (target, "tile", 0)], n_colsum=1)
    loss_local = 0.5 * jnp.sum(sq) / D_MODEL

    dh2, dh2b, dg3, _, slots_g2, slots_u2, slots_d2 = _ffn_bwd(
        "ffn2", dy, dyb, h2, g3, wg2T, wu2T, wd2, (n2, gt2, up2), [])
    dbp, dba, dgp, dga, cs_gp, cs_ga = _mm(
        "mix_out_bwd", [(dh2b, w_o, "nt", 0)], [BF, BF, BF, BF], tm=2048, tn=gate_tn, tk=D_MODEL,
        epilogue=_merge_bwd_epilogue, extras=[(bp, "tile", 0), (ba, "tile", 0)] + gate_extras, n_colsum=2)
    sum_o = _dw_pair("dw_out", merged, dh2b, 1.0, blocks=4)
    (dmixed,), ((slots_o,),) = _mm("pool_out_bwd", [(dbp, w_poT, "nn", 0)], [BF], tm=1024, tn=POOL_WIDTH, tk=D_MODEL,
                                   comm=[_chip_task([sum_o])])
    sum_po = _dw_pair("dw_pool_out", dbp, mixed, 1.0, blocks=4)
    (dattn,), ((slots_po,),) = _mm("attn_out_bwd", [(dba, w_ao, "nt", 0)], [BF], tm=1024, tn=ATTN_WIDTH, tk=D_MODEL,
                                   comm=[_chip_task([sum_po])])
    sum_ao = _dw_pair("dw_attn_out", attn, dba, 1.0, blocks=4)
    dxp, dpool_w, dpool_scale = _pool_bwd("pool_bwd", dmixed, pooled, pool_w, scale_row)
    dqn, dkn, dv, dsink_tile = _attn_bwd("attn_bwd", dattn, qn, kn, proj, sinks)
    dq, dqg = _headnorm_bwd("q_norm_bwd", dqn, proj, COL_Q, ATTN_WIDTH, qg)
    dk, dkg = _headnorm_bwd("k_norm_bwd", dkn, proj, COL_K, KV_WIDTH, kg)
    dproj = jnp.concatenate([dxp, dq, dk, dv, dgp, dga], axis=1)
    (dh1, dh1b, dg2), ((slots_ao,),) = _mm(
        "in_proj_bwd", [(dproj, w_inT, "nn", 0)], [F32, BF], tm=512, tn=D_MODEL, tk=1280, epilogue=_rms_bwd_epilogue,
        extras=[(h1, "tile", 0), (g2, "row", 0), (dh2, "tile", 0)], n_colsum=1, comm=[_chip_task([sum_ao])])
    (dw_inT,) = _mm("dw_in", [(dproj, u, "tn", 0)], [BF], tm=1280, tn=D_MODEL, tk=512)
    part_in = dw_inT.reshape(4, 2, IN_WIDTH // N_DEV, D_MODEL)
    (got_in,) = _pair_exchange("pair_exchange_w_in", [part_in])
    core = lax.axis_index("c").astype(jnp.int32).reshape(1)
    sum_in = _pair_sum("pair_sum_w_in", part_in, got_in, core).reshape(IN_WIDTH // 2, D_MODEL)
    dx, _, dg1, ((slots_in,),), slots_g1, slots_u1, slots_d1 = _ffn_bwd(
        "ffn1", dh1, dh1b, x2, g1, wg1T, wu1T, wd1, saved1, [_chip_task([sum_in])])

    slots = [slots_g1, slots_u1, slots_d1, slots_in, slots_po, slots_ao, slots_o, slots_g2, slots_u2, slots_d2]
    big_out = {}
    for k, (nm, w, m, v, tv, tk_) in enumerate(big):
        res = _adamw_sharded("adamw_" + nm, slots[k], view(w, tv), view(m, tv), view(v, tv), tk_)
        big_out[nm] = tuple(view(r, tv) for r in res)

    dgate_bias = jnp.concatenate([jnp.sum(cs_gp, axis=(0, 1)), jnp.sum(cs_ga, axis=(0, 1))])
    small_grad = _pack_small(
        jnp.sum(dg1, axis=(0, 1)), jnp.sum(dg2, axis=(0, 1)), jnp.sum(dg3, axis=(0, 1)), dpool_w, dpool_scale,
        _fold_heads(dqg), _fold_heads(dkg), dsink_tile[0, :N_HEADS], dgate_bias, loss_local.reshape(1))
    zero = jnp.zeros((1,), F32)
    small_w = _pack_small(ffn1_norm, mix_norm, ffn2_norm, pool_w, pool_scale, q_norm, k_norm, sinks, gate_bias, zero)
    small_m = _pack_small(m_ffn1_norm, m_mix_norm, m_ffn2_norm, m_pool_w, m_pool_scale, m_q_norm, m_k_norm, m_sinks,
                          m_gate_bias, zero)
    small_v = _pack_small(v_ffn1_norm, v_mix_norm, v_ffn2_norm, v_pool_w, v_pool_scale, v_q_norm, v_k_norm, v_sinks,
                          v_gate_bias, zero)
    (gathered,) = _all_gather("gather_small_grads", [small_grad])
    packed = _adamw_small("adamw_small", gathered.reshape(N_DEV, -1, LANES), small_w, small_m, small_v)
    small_names = ["ffn1_norm", "mix_norm", "ffn2_norm", "pool_w", "pool_scale", "q_norm", "k_norm", "sinks", "gate_bias"]
    small_out = {}
    unpacked = [_unpack_small(p) for p in packed]
    for i, nm in enumerate(small_names):
        small_out[nm] = tuple(unpacked[j][i] for j in range(4))
    loss = unpacked[0][9].reshape(())

    order = ["ffn1_norm", "ffn1_w_gate", "ffn1_w_up", "ffn1_w_down", "mix_norm", "w_in", "pool_w", "pool_scale",
             "w_pool_out", "q_norm", "k_norm", "sinks", "w_attn_out", "gate_bias", "w_out", "ffn2_norm",
             "ffn2_w_gate", "ffn2_w_up", "ffn2_w_down"]
    every = {**big_out, **small_out}
    outs = [loss, dx.reshape(x.shape)]
    for j in range(4):
        outs += [every[nm][j] for nm in order]
    return tuple(outs)
```

```python
import functools

import jax
import jax.numpy as jnp
from jax import lax
from jax.experimental import pallas as pl
from jax.experimental.pallas import tpu as pltpu

BF = jnp.bfloat16
F32 = jnp.float32

D_MODEL = 1024
D_FF = 2816
POOL_WIDTH = 512
POOL_GROUP = 128
N_POOL_GROUPS = 4
HEAD_DIM = 64
N_HEADS = 16
GQA_GROUP = 8
BLOCK = 128
ATTN_WIDTH = 1024
KV_WIDTH = 128
IN_WIDTH = 3840
RMS_EPS = 1e-6
N_DEV = 8
LANES = 128

COL_Q = POOL_WIDTH
COL_K = COL_Q + ATTN_WIDTH
COL_V = COL_K + KV_WIDTH
COL_GP = COL_V + KV_WIDTH
COL_GA = COL_GP + D_MODEL

ADAM_LR = 0.001
ADAM_B1 = 0.9
ADAM_B2 = 0.999
ADAM_EPS = 1e-08
ADAM_WD = 0.01
ADAM_STEP = 10

VMEM_LIMIT_V7X = 56 * 1024 * 1024
MESH = pl.DeviceIdType.MESH
ANY = pl.BlockSpec(memory_space=pl.ANY)


def _params(sem=None):
    return pltpu.CompilerParams(dimension_semantics=sem, vmem_limit_bytes=VMEM_LIMIT_V7X)


_DIMS = {"nt": (((1,), (1,)), ((), ())), "nn": (((1,), (0,)), ((), ())), "tn": (((0,), (0,)), ((), ()))}


class _Task:
    def __init__(self, inputs, out_shapes, scratch, start, finish, aliases=None):
        self.inputs, self.out_shapes, self.scratch = list(inputs), list(out_shapes), list(scratch)
        self.start, self.finish, self.aliases = start, finish, dict(aliases or {})


class _CommPlumbing:
    def __init__(self, tasks):
        self.tasks = list(tasks or [])
        self.args = [a for t in self.tasks for a in t.inputs]
        self.out_shapes = [o for t in self.tasks for o in t.out_shapes]
        self.scratch = [s for t in self.tasks for s in t.scratch]
        self.n_in, self.n_out = len(self.args), len(self.out_shapes)

    def _slices(self, c_in, c_out, c_scr):
        i = o = s = 0
        for t in self.tasks:
            yield t, c_in[i:i + len(t.inputs)], c_out[o:o + len(t.out_shapes)], c_scr[s:s + len(t.scratch)]
            i, o, s = i + len(t.inputs), o + len(t.out_shapes), s + len(t.scratch)

    def start(self, c_in, c_out, c_scr):
        for t, ins, outs, scr in self._slices(c_in, c_out, c_scr):
            t.start(ins, outs, scr)

    def finish(self, c_in, c_out, c_scr):
        for t, ins, outs, scr in self._slices(c_in, c_out, c_scr):
            t.finish(ins, outs, scr)

    def aliases(self, in_base, out_base):
        res, i, o = {}, in_base, out_base
        for t in self.tasks:
            for a, b in t.aliases.items():
                res[i + a] = o + b
            i, o = i + len(t.inputs), o + len(t.out_shapes)
        return res

    def split_outputs(self, flat):
        res, o = [], 0
        for t in self.tasks:
            res.append(list(flat[o:o + len(t.out_shapes)]))
            o += len(t.out_shapes)
        return res


def _comm_only(name, tasks):
    plumb = _CommPlumbing(tasks)

    def body(*refs):
        c_in, c_out = refs[:plumb.n_in], refs[plumb.n_in: plumb.n_in + plumb.n_out]
        c_scr = refs[plumb.n_in + plumb.n_out:]
        plumb.start(c_in, c_out, c_scr)
        plumb.finish(c_in, c_out, c_scr)

    res = pl.pallas_call(
        body, name=name, in_specs=[ANY] * plumb.n_in, out_specs=[ANY] * plumb.n_out, out_shape=plumb.out_shapes,
        scratch_shapes=plumb.scratch, input_output_aliases=plumb.aliases(0, 0),
        compiler_params=pltpu.CompilerParams(has_side_effects=True),
    )(*plumb.args)
    return plumb.split_outputs(res)


def _mm(name, terms, out_dtypes, *, tm, tn, tk, epilogue=None, extras=(), n_colsum=0, comm=None, cols_outer=False):
    a0, b0, mode0, _ = terms[0]
    if mode0 == "nt":
        (M, K), N = a0.shape, b0.shape[0]
    elif mode0 == "nn":
        (M, K), N = a0.shape, b0.shape[1]
    else:
        (K, M), N = a0.shape, b0.shape[1]
    tm, tn, tk = min(tm, M), min(tn, N), min(tk, K)
    assert M % tm == 0 and N % tn == 0 and K % tk == 0, (name, M, N, K, tm, tn, tk)
    nI, nJ, nK = M // tm, N // tn, K // tk
    n_terms = len(terms)
    n_acc = max(t[3] for t in terms) + 1
    n_ex = len(extras)
    n_out = len(out_dtypes)
    if epilogue is None:
        epilogue = lambda accs, ex: ([accs[0]], [])
    plumb = _CommPlumbing(comm)
    n_scr = n_acc if nK > 1 else 0
    grid = (nJ, nI, nK) if cols_outer else (nI, nJ, nK)

    def body(*refs):
        n_in = 2 * n_terms + n_ex
        ab = refs[: 2 * n_terms]
        ex_refs = refs[2 * n_terms: n_in]
        c_in = refs[n_in: n_in + plumb.n_in]
        o0 = n_in + plumb.n_in
        out_refs = refs[o0: o0 + n_out]
        cs_refs = refs[o0 + n_out: o0 + n_out + n_colsum]
        c_out = refs[o0 + n_out + n_colsum: o0 + n_out + n_colsum + plumb.n_out]
        s0 = o0 + n_out + n_colsum + plumb.n_out
        acc_refs = refs[s0: s0 + n_scr]
        c_scr = refs[s0 + n_scr:]
        if comm:
            i_, j_, k_ = pl.program_id(0), pl.program_id(1), pl.program_id(2)

            @pl.when((i_ == 0) & (j_ == 0) & (k_ == 0))
            def _():
                plumb.start(c_in, c_out, c_scr)

        def products():
            accs = [None] * n_acc
            for t, (_, _, mode, ai) in enumerate(terms):
                p = lax.dot_general(ab[2 * t][...], ab[2 * t + 1][...], _DIMS[mode], preferred_element_type=F32)
                accs[ai] = p if accs[ai] is None else accs[ai] + p
            return accs

        def finish(accs):
            outs, colsums = epilogue(accs, [r[...] for r in ex_refs])
            for r, o in zip(out_refs, outs):
                r[...] = o.astype(r.dtype)
            for r, cs in zip(cs_refs, colsums):
                r[...] = jnp.sum(cs, axis=0, keepdims=True).reshape(r.shape)

        if nK == 1:
            finish(products())
        else:
            k = pl.program_id(2)
            accs = products()

            @pl.when(k == 0)
            def _():
                for r, a in zip(acc_refs, accs):
                    r[...] = a

            @pl.when(k > 0)
            def _():
                for r, a in zip(acc_refs, accs):
                    r[...] += a

            @pl.when(k == nK - 1)
            def _():
                finish([r[...] for r in acc_refs])

        if comm:
            @pl.when((i_ == grid[0] - 1) & (j_ == grid[1] - 1) & (k_ == nK - 1))
            def _():
                plumb.finish(c_in, c_out, c_scr)

    def spec(block, index, fixed=False):
        imap = (lambda q, p, k: index(p, q, k)) if cols_outer else index
        return pl.BlockSpec(block, imap, pipeline_mode=pl.Buffered(1)) if fixed else pl.BlockSpec(block, imap)

    in_specs, args = [], []
    for a, b, mode, _ in terms:
        if mode == "nt":
            in_specs += [spec((tm, tk), lambda i, j, k: (i, k), nI * nK == 1),
                         spec((tn, tk), lambda i, j, k: (j, k), nJ * nK == 1)]
        elif mode == "nn":
            in_specs += [spec((tm, tk), lambda i, j, k: (i, k), nI * nK == 1),
                         spec((tk, tn), lambda i, j, k: (k, j), nJ * nK == 1)]
        else:
            in_specs += [spec((tk, tm), lambda i, j, k: (k, i), nI * nK == 1),
                         spec((tk, tn), lambda i, j, k: (k, j), nJ * nK == 1)]
        args += [a, b]
    for arr, kind, off in extras:
        if kind == "tile":
            in_specs.append(spec((tm, tn), functools.partial(lambda i, j, k, off: (i, j + off), off=off)))
        else:
            in_specs.append(spec((1, tn), functools.partial(lambda i, j, k, off: (0, j + off), off=off)))
        args.append(arr)
    out_shape = [jax.ShapeDtypeStruct((M, N), dt) for dt in out_dtypes]
    out_specs = [spec((tm, tn), lambda i, j, k: (i, j)) for _ in out_dtypes]
    out_shape += [jax.ShapeDtypeStruct((nI, 1, N), F32) for _ in range(n_colsum)]
    out_specs += [spec((1, 1, tn), lambda i, j, k: (i, 0, j)) for _ in range(n_colsum)]
    scratch = [pltpu.VMEM((tm, tn), F32) for _ in range(n_scr)]
    aliases = plumb.aliases(len(args), len(out_shape))
    args += plumb.args
    in_specs += [ANY] * plumb.n_in
    out_shape += plumb.out_shapes
    out_specs += [ANY] * plumb.n_out
    sem = ("arbitrary",) * 3 if comm else ("parallel", "parallel", "arbitrary")
    res = pl.pallas_call(
        body, name=name, grid=grid, in_specs=in_specs, out_specs=out_specs, out_shape=out_shape,
        scratch_shapes=scratch + plumb.scratch, input_output_aliases=aliases, compiler_params=_params(sem),
    )(*args)
    n_own = n_out + n_colsum
    return (list(res[:n_own]), plumb.split_outputs(res[n_own:])) if comm is not None else res


ROW_TILE = 512


def _rms_fwd(name, x, g):
    T, D = x.shape

    def body(x_ref, g_ref, o_ref):
        xv = x_ref[...]
        r = lax.rsqrt(jnp.mean(xv * xv, axis=-1, keepdims=True) + RMS_EPS)
        o_ref[...] = (xv * r * g_ref[...]).astype(BF)

    return pl.pallas_call(
        body, name=name, grid=(T // ROW_TILE,),
        in_specs=[pl.BlockSpec((ROW_TILE, D), lambda i: (i, 0)), pl.BlockSpec((1, D), lambda i: (0, 0))],
        out_specs=pl.BlockSpec((ROW_TILE, D), lambda i: (i, 0)),
        out_shape=jax.ShapeDtypeStruct((T, D), BF), compiler_params=_params(("parallel",)),
    )(x, g)


HEADNORM_TILE = 1024


def _half_sum_matrix():
    r = lax.broadcasted_iota(jnp.int32, (LANES, LANES), 0) // HEAD_DIM
    c = lax.broadcasted_iota(jnp.int32, (LANES, LANES), 1) // HEAD_DIM
    return (r == c).astype(BF)


def _head_mean(v, ones_blockdiag):
    hi = v.astype(BF)
    lo = (v - hi.astype(F32)).astype(BF)
    s = jnp.dot(hi, ones_blockdiag, preferred_element_type=F32) + jnp.dot(lo, ones_blockdiag, preferred_element_type=F32)
    return s * (1.0 / HEAD_DIM)


def _headnorm_fwd(name, proj, col0, width, g2):
    T = proj.shape[0]
    nb, off = width // LANES, col0 // LANES

    def body(x_ref, g_ref, b_ref, o_ref):
        xv = x_ref[...].astype(F32)
        r = lax.rsqrt(_head_mean(xv * xv, b_ref[...]) + RMS_EPS)
        o_ref[...] = (xv * r * g_ref[...]).astype(BF)

    return pl.pallas_call(
        body, name=name, grid=(T // HEADNORM_TILE, nb),
        in_specs=[pl.BlockSpec((HEADNORM_TILE, LANES), lambda i, j: (i, j + off)),
                  pl.BlockSpec((1, LANES), lambda i, j: (0, 0)), pl.BlockSpec((LANES, LANES), lambda i, j: (0, 0))],
        out_specs=pl.BlockSpec((HEADNORM_TILE, LANES), lambda i, j: (i, j)),
        out_shape=jax.ShapeDtypeStruct((T, width), BF), compiler_params=_params(("parallel", "parallel")),
    )(proj, g2, _half_sum_matrix())


def _headnorm_bwd(name, dy, proj, col0, width, g2):
    T = proj.shape[0]
    nb, off = width // LANES, col0 // LANES

    def body(dy_ref, x_ref, g_ref, b_ref, dx_ref, dg_ref):
        xv = x_ref[...].astype(F32)
        dyv = dy_ref[...].astype(F32)
        r = lax.rsqrt(_head_mean(xv * xv, b_ref[...]) + RMS_EPS)
        xhat = xv * r
        dxhat = dyv * g_ref[...]
        dx_ref[...] = (r * (dxhat - xhat * _head_mean(dxhat * xhat, b_ref[...]))).astype(BF)
        dg_ref[...] = jnp.sum(dyv * xhat, axis=0, keepdims=True).reshape(dg_ref.shape)

    return pl.pallas_call(
        body, name=name, grid=(T // HEADNORM_TILE, nb),
        in_specs=[pl.BlockSpec((HEADNORM_TILE, LANES), lambda i, j: (i, j)),
                  pl.BlockSpec((HEADNORM_TILE, LANES), lambda i, j: (i, j + off)),
                  pl.BlockSpec((1, LANES), lambda i, j: (0, 0)), pl.BlockSpec((LANES, LANES), lambda i, j: (0, 0))],
        out_specs=[pl.BlockSpec((HEADNORM_TILE, LANES), lambda i, j: (i, j)),
                   pl.BlockSpec((1, 1, LANES), lambda i, j: (i, 0, j))],
        out_shape=[jax.ShapeDtypeStruct((T, width), BF), jax.ShapeDtypeStruct((T // HEADNORM_TILE, 1, width), F32)],
        compiler_params=_params(("parallel", "parallel")),
    )(dy, proj, g2, _half_sum_matrix())


def _shift_down(v, k, row):
    return jnp.where(row >= k, pltpu.roll(v, k, axis=0), 0.0)


def _shift_up(v, k, row, T):
    return jnp.where(row < T - k, pltpu.roll(v, T - k, axis=0), 0.0)


def _by_group(g, vals):
    out = vals[-1]
    for i in range(len(vals) - 2, -1, -1):
        out = jnp.where(g == i, vals[i], out)
    return out


def _pool_fwd(name, proj, pool_w, pool_scale):
    T = proj.shape[0]

    def body(x_ref, w_ref, s_ref, pooled_ref, mixed_ref):
        g = pl.program_id(0)
        xv = x_ref[...].astype(F32)
        row = lax.broadcasted_iota(jnp.int32, (T, 1), 0)
        s2 = xv + _shift_down(xv, 1, row)
        s4 = s2 + _shift_down(s2, 2, row)
        s8 = s4 + _shift_down(s4, 4, row)
        s16 = s8 + _shift_down(s8, 8, row)
        wsum = _by_group(g, [s2, s4, s8, s16])
        count = jnp.minimum(row + 1, 2 << g).astype(F32)
        pooled = (wsum / count - xv).astype(BF)
        pooled_ref[...] = pooled
        mixed = jnp.dot(pooled, w_ref[0].astype(BF), preferred_element_type=F32) * s_ref[...]
        mixed_ref[...] = mixed.astype(BF)

    col = pl.BlockSpec((T, POOL_GROUP), lambda g: (0, g))
    return pl.pallas_call(
        body, name=name, grid=(N_POOL_GROUPS,),
        in_specs=[col, pl.BlockSpec((1, POOL_GROUP, POOL_GROUP), lambda g: (g, 0, 0)),
                  pl.BlockSpec((1, POOL_GROUP), lambda g: (0, g))],
        out_specs=[col, col],
        out_shape=[jax.ShapeDtypeStruct((T, POOL_WIDTH), BF), jax.ShapeDtypeStruct((T, POOL_WIDTH), BF)],
        compiler_params=_params(("parallel",)),
    )(proj, pool_w, pool_scale)


def _pool_bwd(name, dmixed, pooled, pool_w, pool_scale):
    T = dmixed.shape[0]

    def body(dm_ref, p_ref, w_ref, s_ref, dx_ref, dw_ref, ds_ref):
        g = pl.program_id(0)
        dm = dm_ref[...].astype(F32)
        pooled = p_ref[...]
        w = w_ref[0].astype(BF)
        pre = jnp.dot(pooled, w, preferred_element_type=F32)
        ds_ref[...] = jnp.sum(dm * pre, axis=0, keepdims=True)
        dms = (dm * s_ref[...]).astype(BF)
        dw_ref[0] = lax.dot_general(pooled, dms, _DIMS["tn"], preferred_element_type=F32)
        dpooled = lax.dot_general(dms, w, _DIMS["nt"], preferred_element_type=F32)
        row = lax.broadcasted_iota(jnp.int32, (T, 1), 0)
        count = jnp.minimum(row + 1, 2 << g).astype(F32)
        z = dpooled / count
        l2 = z + _shift_up(z, 1, row, T)
        l4 = l2 + _shift_up(l2, 2, row, T)
        l8 = l4 + _shift_up(l4, 4, row, T)
        l16 = l8 + _shift_up(l8, 8, row, T)
        dx_ref[...] = (_by_group(g, [l2, l4, l8, l16]) - dpooled).astype(BF)

    col = pl.BlockSpec((T, POOL_GROUP), lambda g: (0, g))
    wspec = pl.BlockSpec((1, POOL_GROUP, POOL_GROUP), lambda g: (g, 0, 0))
    sspec = pl.BlockSpec((1, POOL_GROUP), lambda g: (0, g))
    return pl.pallas_call(
        body, name=name, grid=(N_POOL_GROUPS,), in_specs=[col, col, wspec, sspec], out_specs=[col, wspec, sspec],
        out_shape=[jax.ShapeDtypeStruct((T, POOL_WIDTH), BF),
                   jax.ShapeDtypeStruct((N_POOL_GROUPS, POOL_GROUP, POOL_GROUP), F32),
                   jax.ShapeDtypeStruct((1, POOL_WIDTH), F32)],
        compiler_params=_params(("parallel",)),
    )(dmixed, pooled, pool_w, pool_scale)


ATTN_SCALE = HEAD_DIM ** -0.5
MASKED = float(jnp.finfo(jnp.float32).min)
KV_COL_BLOCK_K = COL_K // LANES
KV_COL_BLOCK_V = COL_V // LANES
GROUP_WIDTH = GQA_GROUP * HEAD_DIM


def _dup_head(v, j):
    half = lax.broadcasted_iota(jnp.int32, (1, LANES), 1) // HEAD_DIM
    return jnp.where(half == j, v, pltpu.roll(v, HEAD_DIM, axis=1))


def _stack_heads(v, low):
    pieces = []
    for p in range(GROUP_WIDTH // LANES):
        vp = v[:, LANES * p: LANES * (p + 1)]
        pieces.append(jnp.where(low, vp, jnp.zeros_like(vp)))
        pieces.append(jnp.where(low, jnp.zeros_like(vp), vp))
    return jnp.concatenate(pieces, axis=0)


def _unstack_heads(st, low):
    pieces = []
    for p in range(GROUP_WIDTH // LANES):
        even = st[BLOCK * (2 * p): BLOCK * (2 * p + 1)]
        odd = st[BLOCK * (2 * p + 1): BLOCK * (2 * p + 2)]
        pieces.append(jnp.where(low, even, odd))
    return jnp.concatenate(pieces, axis=1)


def _band_mask(n):
    row = lax.broadcasted_iota(jnp.int32, (BLOCK, 2 * BLOCK), 0)
    col = lax.broadcasted_iota(jnp.int32, (BLOCK, 2 * BLOCK), 1)
    return (col > row) & (col <= row + BLOCK) & ((n > 0) | (col >= BLOCK))


def _softmax_heads(s, valid, sink_ref, j):
    ps, psinks = [], []
    for h in range(GQA_GROUP):
        sh = jnp.where(valid, s[BLOCK * h: BLOCK * (h + 1)], MASKED)
        sink = sink_ref[j * GQA_GROUP + h]
        m = jnp.maximum(jnp.max(sh, axis=1, keepdims=True), sink)
        e = jnp.exp(sh - m)
        es = jnp.exp(sink - m)
        inv = 1.0 / (jnp.sum(e, axis=1, keepdims=True) + es)
        ps.append(e * inv)
        psinks.append(es * inv)
    return jnp.concatenate(ps, axis=0), jnp.concatenate(psinks, axis=0)


def _attn_fwd(name, qn, kn, proj, sinks):
    T = qn.shape[0]
    nb = T // BLOCK

    def body(sink_ref, q_ref, kp_ref, kc_ref, vp_ref, vc_ref, o_ref):
        n, j = pl.program_id(0), pl.program_id(1)
        low = lax.broadcasted_iota(jnp.int32, (1, LANES), 1) < HEAD_DIM
        k2 = _dup_head(jnp.concatenate([kp_ref[...], kc_ref[...]], axis=0), j)
        v2 = _dup_head(jnp.concatenate([vp_ref[...], vc_ref[...]], axis=0), j)
        q = _stack_heads(q_ref[...], low)
        s = lax.dot_general(q, k2, _DIMS["nt"], preferred_element_type=F32) * ATTN_SCALE
        p, _ = _softmax_heads(s, _band_mask(n), sink_ref, j)
        o = jnp.dot(p.astype(BF), v2, preferred_element_type=F32)
        o_ref[...] = _unstack_heads(o, low).astype(BF)

    prev = lambda n, j: (jnp.maximum(n - 1, 0), 0)
    return pl.pallas_call(
        body, name=name, grid=(nb, 2),
        in_specs=[pl.BlockSpec(memory_space=pltpu.SMEM),
                  pl.BlockSpec((BLOCK, GROUP_WIDTH), lambda n, j: (n, j)),
                  pl.BlockSpec((BLOCK, LANES), prev), pl.BlockSpec((BLOCK, LANES), lambda n, j: (n, 0)),
                  pl.BlockSpec((BLOCK, LANES), lambda n, j: (jnp.maximum(n - 1, 0), KV_COL_BLOCK_V)),
                  pl.BlockSpec((BLOCK, LANES), lambda n, j: (n, KV_COL_BLOCK_V))],
        out_specs=pl.BlockSpec((BLOCK, GROUP_WIDTH), lambda n, j: (n, j)),
        out_shape=jax.ShapeDtypeStruct((T, ATTN_WIDTH), BF), compiler_params=_params(("parallel", "parallel")),
    )(sinks, qn, kn, kn, proj, proj)


def _attn_bwd(name, dout, qn, kn, proj, sinks):
    T = qn.shape[0]
    nb = T // BLOCK

    def body(sink_ref, do_ref, q_ref, kp_ref, kc_ref, vp_ref, vc_ref, dq_ref, dk_ref, dv_ref, dsink_ref,
             carry_k, carry_v, tot_k, tot_v):
        n = pl.program_id(0)
        lane = lax.broadcasted_iota(jnp.int32, (1, LANES), 1)
        low = lane < HEAD_DIM

        @pl.when(n == 0)
        def _():
            carry_k[...] = jnp.zeros_like(carry_k)
            carry_v[...] = jnp.zeros_like(carry_v)
            dsink_ref[...] = jnp.zeros_like(dsink_ref)

        @pl.when(n == nb)
        def _():
            tot_k[...] = jnp.zeros_like(tot_k)
            tot_v[...] = jnp.zeros_like(tot_v)

        @pl.when(n < nb)
        def _():
            kk = jnp.concatenate([kp_ref[...], kc_ref[...]], axis=0)
            vv = jnp.concatenate([vp_ref[...], vc_ref[...]], axis=0)
            valid = _band_mask(n)
            dk_tot = jnp.zeros((2 * BLOCK, LANES), F32)
            dv_tot = jnp.zeros((2 * BLOCK, LANES), F32)
            dsink = jnp.zeros((1, LANES), F32)
            for j in range(2):
                k2 = _dup_head(kk, j)
                v2 = _dup_head(vv, j)
                q = _stack_heads(q_ref[:, GROUP_WIDTH * j: GROUP_WIDTH * (j + 1)], low)
                do = _stack_heads(do_ref[:, GROUP_WIDTH * j: GROUP_WIDTH * (j + 1)], low)
                s = lax.dot_general(q, k2, _DIMS["nt"], preferred_element_type=F32) * ATTN_SCALE
                p, psink = _softmax_heads(s, valid, sink_ref, j)
                dp = lax.dot_general(do, v2, _DIMS["nt"], preferred_element_type=F32)
                delta = jnp.sum(p * dp, axis=1, keepdims=True)
                ds = (p * (dp - delta) * ATTN_SCALE).astype(BF)
                dq_ref[:, GROUP_WIDTH * j: GROUP_WIDTH * (j + 1)] = _unstack_heads(
                    jnp.dot(ds, k2, preferred_element_type=F32), low).astype(BF)
                dk2 = lax.dot_general(ds, q, _DIMS["tn"], preferred_element_type=F32)
                dv2 = lax.dot_general(p.astype(BF), do, _DIMS["tn"], preferred_element_type=F32)
                mine = low if j == 0 else jnp.logical_not(low)
                dk_tot = dk_tot + jnp.where(mine, dk2 + pltpu.roll(dk2, HEAD_DIM, axis=1), 0.0)
                dv_tot = dv_tot + jnp.where(mine, dv2 + pltpu.roll(dv2, HEAD_DIM, axis=1), 0.0)
                sink_term = psink * delta
                for h in range(GQA_GROUP):
                    val = -jnp.sum(sink_term[BLOCK * h: BLOCK * (h + 1)], axis=0, keepdims=True)
                    dsink = dsink + jnp.where(lane == j * GQA_GROUP + h, val, 0.0)
            tot_k[...] = dk_tot
            tot_v[...] = dv_tot
            dsink_ref[0:1, :] += dsink

        dk_ref[...] = (carry_k[...] + tot_k[0:BLOCK]).astype(BF)
        dv_ref[...] = (carry_v[...] + tot_v[0:BLOCK]).astype(BF)
        carry_k[...] = tot_k[BLOCK:]
        carry_v[...] = tot_v[BLOCK:]

    cur = lambda n: (jnp.minimum(n, nb - 1), 0)
    prev = lambda n: (jnp.maximum(n - 1, 0), 0)
    wide = pl.BlockSpec((BLOCK, ATTN_WIDTH), cur)
    return pl.pallas_call(
        body, name=name, grid=(nb + 1,),
        in_specs=[pl.BlockSpec(memory_space=pltpu.SMEM), wide, wide,
                  pl.BlockSpec((BLOCK, LANES), prev), pl.BlockSpec((BLOCK, LANES), cur),
                  pl.BlockSpec((BLOCK, LANES), lambda n: (jnp.maximum(n - 1, 0), KV_COL_BLOCK_V)),
                  pl.BlockSpec((BLOCK, LANES), lambda n: (jnp.minimum(n, nb - 1), KV_COL_BLOCK_V))],
        out_specs=[wide, pl.BlockSpec((BLOCK, LANES), prev), pl.BlockSpec((BLOCK, LANES), prev),
                   pl.BlockSpec((8, LANES), lambda n: (0, 0))],
        out_shape=[jax.ShapeDtypeStruct((T, ATTN_WIDTH), BF), jax.ShapeDtypeStruct((T, KV_WIDTH), BF),
                   jax.ShapeDtypeStruct((T, KV_WIDTH), BF), jax.ShapeDtypeStruct((8, LANES), F32)],
        scratch_shapes=[pltpu.VMEM((BLOCK, LANES), F32), pltpu.VMEM((BLOCK, LANES), F32),
                        pltpu.VMEM((2 * BLOCK, LANES), F32), pltpu.VMEM((2 * BLOCK, LANES), F32)],
        compiler_params=_params(("arbitrary",)),
    )(sinks, dout, qn, kn, kn, proj, proj)


def _swiglu_fwd_epilogue(accs, ex):
    g, u = accs
    return [g, u, g * jax.nn.sigmoid(g) * u], []


def _swiglu_bwd_epilogue(accs, ex):
    (da,) = accs
    g, u = ex[0].astype(F32), ex[1].astype(F32)
    s = jax.nn.sigmoid(g)
    silu = g * s
    return [da * u * (s * (1.0 + g * (1.0 - s))), da * silu, silu * u], []


def _half_residual_epilogue(accs, ex):
    return [ex[0] + 0.5 * accs[0]], []


def _residual_epilogue(accs, ex):
    return [ex[0] + accs[0]], []


def _rms_bwd_epilogue(accs, ex):
    (dn,) = accs
    xv, g, dres = ex
    r = lax.rsqrt(jnp.mean(xv * xv, axis=-1, keepdims=True) + RMS_EPS)
    xhat = xv * r
    dxhat = dn * g
    dx = dres + r * (dxhat - xhat * jnp.mean(dxhat * xhat, axis=-1, keepdims=True))
    return [dx, dx], [dn * xhat]


def _loss_epilogue(accs, ex):
    xv, target = ex
    d = xv + 0.5 * accs[0] - target
    dy = d * (1.0 / D_MODEL)
    return [dy, dy], [d * d]


def _merge_fwd_epilogue(accs, ex):
    (ba,) = accs
    bp, gp_pre, ga_pre, bias_p, bias_a = ex
    gp = jax.nn.sigmoid(gp_pre.astype(F32) + bias_p)
    ga = jax.nn.sigmoid(ga_pre.astype(F32) + bias_a)
    return [gp * bp.astype(F32) + ga * ba, ba], []


def _merge_bwd_epilogue(accs, ex):
    (dm,) = accs
    bp, ba, gp_pre, ga_pre, bias_p, bias_a = ex
    gp = jax.nn.sigmoid(gp_pre.astype(F32) + bias_p)
    ga = jax.nn.sigmoid(ga_pre.astype(F32) + bias_a)
    dgp = dm * bp.astype(F32) * gp * (1.0 - gp)
    dga = dm * ba.astype(F32) * ga * (1.0 - ga)
    return [dm * gp, dm * ga, dgp, dga], [dgp, dga]


def _prep(name, ws, transposes):
    n = len(ws)

    def body(*refs):
        for w_ref, o_ref, tr in zip(refs[:n], refs[n:], transposes):
            v = w_ref[...]
            o_ref[...] = (v.T if tr else v).astype(BF)

    shapes = [jax.ShapeDtypeStruct(w.shape[::-1] if tr else w.shape, BF) for w, tr in zip(ws, transposes)]
    return pl.pallas_call(body, name=name, out_shape=shapes, compiler_params=_params())(*ws)


def _adam_math(w, g, m, v):
    m = ADAM_B1 * m + (1.0 - ADAM_B1) * g
    v = ADAM_B2 * v + (1.0 - ADAM_B2) * jnp.square(g)
    m_hat = m / (1.0 - ADAM_B1 ** ADAM_STEP)
    v_hat = v / (1.0 - ADAM_B2 ** ADAM_STEP)
    delta = -ADAM_LR * (m_hat / (jnp.sqrt(v_hat) + ADAM_EPS) + ADAM_WD * w)
    return delta, m, v


def _adamw_sharded(name, slots, w, m, v, transpose):
    def body(s_ref, w_ref, m_ref, v_ref, g_out, d_out, m_out, v_out):
        g = s_ref[0].astype(F32)
        for i in range(1, 4):
            g = g + s_ref[i].astype(F32)
        if transpose:
            g = g.T
        delta, mn, vn = _adam_math(w_ref[...], g, m_ref[...], v_ref[...])
        g_out[...] = g
        d_out[...] = delta
        m_out[...] = mn
        v_out[...] = vn

    return pl.pallas_call(
        body, name=name, out_shape=[jax.ShapeDtypeStruct(w.shape, F32)] * 4, compiler_params=_params(),
    )(slots, w, m, v)


def _adamw_small(name, gathered, w, m, v):
    def body(s_ref, w_ref, m_ref, v_ref, g_out, d_out, m_out, v_out):
        g = s_ref[0]
        for i in range(1, N_DEV):
            g = g + s_ref[i]
        delta, mn, vn = _adam_math(w_ref[...], g, m_ref[...], v_ref[...])
        g_out[...] = g
        d_out[...] = delta
        m_out[...] = mn
        v_out[...] = vn

    return pl.pallas_call(
        body, name=name, out_shape=[jax.ShapeDtypeStruct(w.shape, F32)] * 4, compiler_params=_params(),
    )(gathered, w, m, v)


def _place():
    x, y, c = lax.axis_index("x"), lax.axis_index("y"), lax.axis_index("c")
    other_chips = [(1 - x, y), (x, 1 - y), (1 - x, 1 - y)]
    return x, y, c, other_chips


def _rows(ref, r, place, natural=False):
    px, py, pc = place
    b = 4 * px + 2 * py + pc if natural else 4 * pc + 2 * px + py
    return ref.at[pl.ds(pl.multiple_of(b * r, 8), r), :]


def _gather_send_task(shards, natural=()):
    n = len(shards)
    rs = [s.shape[0] for s in shards]
    rows_of = lambda ref, k, place: _rows(ref, rs[k], place, k in natural)

    def copies(ins, outs, scr):
        send_sems, recv_sems, local_sems = scr
        x, y, c, chips = _place()
        me = (x, y, c)
        peers = [(x, y, 1 - c)] + [(*chip, c) for chip in chips]
        local = [pltpu.make_async_copy(ins[k], rows_of(outs[k], k, me), local_sems.at[k]) for k in range(n)]
        sends, recvs = [], []
        for s in (1, 2, 3, 0):
            for k in range(n):
                sems = dict(send_sem=send_sems.at[4 * k + s], recv_sem=recv_sems.at[4 * k + s], device_id_type=MESH)
                sends.append(pltpu.make_async_remote_copy(
                    src_ref=ins[k], dst_ref=rows_of(outs[k], k, me), device_id=peers[s], **sems))
                theirs = rows_of(outs[k], k, peers[s])
                recvs.append(pltpu.make_async_remote_copy(src_ref=theirs, dst_ref=theirs, device_id=me, **sems))
        return local, sends, recvs

    def start(ins, outs, scr):
        local, sends, _ = copies(ins, outs, scr)
        for cp in local + sends:
            cp.start()

    def finish(ins, outs, scr):
        local, sends, recvs = copies(ins, outs, scr)
        for cp in recvs:
            cp.wait_recv()
        for cp in sends:
            cp.wait_send()
        for cp in local:
            cp.wait()

    out_shapes = [jax.ShapeDtypeStruct((N_DEV * s.shape[0], s.shape[1]), s.dtype) for s in shards]
    scratch = [pltpu.SemaphoreType.DMA((4 * n,)), pltpu.SemaphoreType.DMA((4 * n,)), pltpu.SemaphoreType.DMA((n,))]
    return _Task(shards, out_shapes, scratch, start, finish)


def _gather_forward_task(fulls, natural=()):
    n = len(fulls)
    rs = [f.shape[0] // N_DEV for f in fulls]
    rows_of = lambda ref, k, place: _rows(ref, rs[k], place, k in natural)

    def copies(outs, scr):
        send_sems, recv_sems = scr
        x, y, c, chips = _place()
        sends, recvs = [], []
        for j, chip in enumerate(chips):
            for k in range(n):
                sems = dict(send_sem=send_sems.at[3 * k + j], recv_sem=recv_sems.at[3 * k + j], device_id_type=MESH)
                got = rows_of(outs[k], k, (*chip, c))
                sends.append(pltpu.make_async_remote_copy(src_ref=got, dst_ref=got, device_id=(x, y, 1 - c), **sems))
                theirs = rows_of(outs[k], k, (*chip, 1 - c))
                recvs.append(pltpu.make_async_remote_copy(src_ref=theirs, dst_ref=theirs, device_id=(x, y, c), **sems))
        return sends, recvs

    def start(ins, outs, scr):
        for cp in copies(outs, scr)[0]:
            cp.start()

    def finish(ins, outs, scr):
        sends, recvs = copies(outs, scr)
        for cp in recvs:
            cp.wait_recv()
        for cp in sends:
            cp.wait_send()

    out_shapes = [jax.ShapeDtypeStruct(f.shape, f.dtype) for f in fulls]
    scratch = [pltpu.SemaphoreType.DMA((3 * n,)), pltpu.SemaphoreType.DMA((3 * n,))]
    return _Task(fulls, out_shapes, scratch, start, finish, aliases={k: k for k in range(n)})


def _chip_task(sums):
    n = len(sums)
    rs = [s.shape[0] // 4 for s in sums]

    def block(ref, k, chip_index):
        return ref.at[pl.ds(pl.multiple_of(chip_index * rs[k], 8), rs[k]), :]

    def copies(ins, outs, scr):
        send_sems, recv_sems, local_sems = scr
        x, y, c, chips = _place()
        here = 2 * x + y
        local = [pltpu.make_async_copy(block(ins[k], k, here), outs[k].at[here], local_sems.at[k]) for k in range(n)]
        remote = []
        for j, (px, py) in enumerate(chips):
            remote += [pltpu.make_async_remote_copy(
                src_ref=block(ins[k], k, 2 * px + py), dst_ref=outs[k].at[here],
                send_sem=send_sems.at[3 * k + j], recv_sem=recv_sems.at[3 * k + j],
                device_id=(px, py, c), device_id_type=MESH) for k in range(n)]
        return local, remote

    def start(ins, outs, scr):
        local, remote = copies(ins, outs, scr)
        for cp in local + remote:
            cp.start()

    def finish(ins, outs, scr):
        local, remote = copies(ins, outs, scr)
        for cp in remote:
            cp.wait()
        for cp in local:
            cp.wait()

    out_shapes = [jax.ShapeDtypeStruct((4, r, s.shape[1]), s.dtype) for r, s in zip(rs, sums)]
    scratch = [pltpu.SemaphoreType.DMA((3 * n,)), pltpu.SemaphoreType.DMA((3 * n,)), pltpu.SemaphoreType.DMA((n,))]
    return _Task(sums, out_shapes, scratch, start, finish)


def _dw_pair(name, a, b, scale, comm=None, blocks=1):
    T, M = a.shape
    N = b.shape[1]
    half = M // 2
    wide = half // blocks
    tk = min(2048, T)
    nK = T // tk
    plumb = _CommPlumbing(comm)

    def body(core_ref, *rest):
        a_refs, b_ref, rest = rest[:blocks], rest[blocks], rest[blocks + 1:]
        c_in = rest[:plumb.n_in]
        o_ref = rest[plumb.n_in]
        c_out = rest[plumb.n_in + 1: plumb.n_in + 1 + plumb.n_out]
        acc, stage, land, send_sem, recv_sem = rest[plumb.n_in + 1 + plumb.n_out: plumb.n_in + 6 + plumb.n_out]
        c_scr = rest[plumb.n_in + 6 + plumb.n_out:]
        i, k = pl.program_id(0), pl.program_id(1)
        x, y, c, _ = _place()
        push = pltpu.make_async_remote_copy(src_ref=stage, dst_ref=land, send_sem=send_sem, recv_sem=recv_sem,
                                            device_id=(x, y, 1 - c), device_id_type=MESH)
        if comm:
            @pl.when((i == 0) & (k == 0))
            def _():
                plumb.start(c_in, c_out, c_scr)

        av = a_refs[0][...] if blocks == 1 else jnp.concatenate([r[...] for r in a_refs], axis=1)
        p = lax.dot_general(av, b_ref[...], _DIMS["tn"], preferred_element_type=F32)

        @pl.when(k == 0)
        def _():
            acc[...] = p

        @pl.when(k > 0)
        def _():
            acc[...] += p

        @pl.when((i == 0) & (k == nK - 1))
        def _():
            stage[...] = (scale * acc[...]).astype(BF)
            push.start()

        @pl.when((i == 1) & (k == nK - 1))
        def _():
            push.wait_recv()
            o_ref[...] = (scale * acc[...] + land[...].astype(F32)).astype(BF)
            push.wait_send()
            if comm:
                plumb.finish(c_in, c_out, c_scr)

    grid_spec = pltpu.PrefetchScalarGridSpec(
        num_scalar_prefetch=1, grid=(2, nK),
        in_specs=[pl.BlockSpec((tk, wide), functools.partial(
            lambda i, k, core, j: (k, (2 * j if blocks > 1 else 0) + jnp.where(i == 0, 1 - core[0], core[0])), j=j))
            for j in range(blocks)] + [pl.BlockSpec((tk, N), lambda i, k, core: (k, 0))] + [ANY] * plumb.n_in,
        out_specs=[pl.BlockSpec((half, N), lambda i, k, core: (0, 0))] + [ANY] * plumb.n_out,
        scratch_shapes=[pltpu.VMEM((half, N), F32), pltpu.VMEM((half, N), BF), pltpu.VMEM((half, N), BF),
                        pltpu.SemaphoreType.DMA, pltpu.SemaphoreType.DMA] + plumb.scratch)
    core = lax.axis_index("c").astype(jnp.int32).reshape(1)
    res = pl.pallas_call(
        body, name=name, grid_spec=grid_spec,
        out_shape=[jax.ShapeDtypeStruct((half, N), BF)] + plumb.out_shapes,
        compiler_params=_params(("arbitrary", "arbitrary")),
    )(core, *([a] * blocks), b, *plumb.args)
    return (res[0], plumb.split_outputs(res[1:])) if comm else res[0]


def _all_gather(name, shards):
    n = len(shards)

    def body(*refs):
        ins, outs = refs[:n], refs[n:2 * n]
        send_sems, recv_sems, local_sems = refs[2 * n:]
        x, y, c, chips = _place()
        me, sibling = (x, y, c), (x, y, 1 - c)

        def rows(k, px, py, pc):
            return _rows(outs[k], shards[k].shape[0], (px, py, pc))

        def copy(k, slot, block, to, src=None):
            return pltpu.make_async_remote_copy(
                src_ref=rows(k, *block) if src is None else src, dst_ref=rows(k, *block),
                send_sem=send_sems.at[7 * k + slot], recv_sem=recv_sems.at[7 * k + slot],
                device_id=to, device_id_type=MESH)

        mine = [pltpu.make_async_copy(ins[k], rows(k, *me), local_sems.at[k]) for k in range(n)]
        for cp in mine:
            cp.start()
        first = []
        for j, chip in enumerate(chips):
            first += [copy(k, 1 + j, me, (*chip, c), src=ins[k]) for k in range(n)]
        first += [copy(k, 0, me, sibling, src=ins[k]) for k in range(n)]
        for cp in first:
            cp.start()
        passed = []
        for j, chip in enumerate(chips):
            for k in range(n):
                copy(k, 1 + j, (*chip, c), me).wait_recv()
                fwd = copy(k, 4 + j, (*chip, c), sibling)
                fwd.start()
                passed.append(fwd)
        for k in range(n):
            copy(k, 0, sibling, me).wait_recv()
        for j, chip in enumerate(chips):
            for k in range(n):
                copy(k, 4 + j, (*chip, 1 - c), me).wait_recv()
        for cp in first + passed:
            cp.wait_send()
        for cp in mine:
            cp.wait()

    return pl.pallas_call(
        body, name=name, in_specs=[ANY] * n, out_specs=[ANY] * n,
        out_shape=[jax.ShapeDtypeStruct((N_DEV * s.shape[0], s.shape[1]), s.dtype) for s in shards],
        scratch_shapes=[pltpu.SemaphoreType.DMA((7 * n,)), pltpu.SemaphoreType.DMA((7 * n,)),
                        pltpu.SemaphoreType.DMA((n,))],
        compiler_params=pltpu.CompilerParams(has_side_effects=True),
    )(*shards)


def _pair_exchange(name, parts):
    n = len(parts)

    def body(*refs):
        ins, outs = refs[:n], refs[n:2 * n]
        send_sems, recv_sems = refs[2 * n:]
        x, y, c, _ = _place()
        copies = [pltpu.make_async_remote_copy(
            src_ref=ins[k].at[:, pl.ds(1 - c, 1)], dst_ref=outs[k], send_sem=send_sems.at[k], recv_sem=recv_sems.at[k],
            device_id=(x, y, 1 - c), device_id_type=MESH) for k in range(n)]
        for cp in copies:
            cp.start()
        for cp in copies:
            cp.wait()

    return pl.pallas_call(
        body, name=name, in_specs=[ANY] * n, out_specs=[ANY] * n,
        out_shape=[jax.ShapeDtypeStruct((4, 1) + p.shape[2:], p.dtype) for p in parts],
        scratch_shapes=[pltpu.SemaphoreType.DMA((n,)), pltpu.SemaphoreType.DMA((n,))],
        compiler_params=pltpu.CompilerParams(has_side_effects=True),
    )(*parts)


def _pair_sum(name, part, got, core):
    _, _, r, C = part.shape

    def body(core_ref, p_ref, g_ref, o_ref):
        o_ref[0] = (p_ref[0, 0].astype(F32) + g_ref[0, 0].astype(F32)).astype(o_ref.dtype)

    return pl.pallas_call(
        body, name=name,
        grid_spec=pltpu.PrefetchScalarGridSpec(
            num_scalar_prefetch=1, grid=(4,),
            in_specs=[pl.BlockSpec((1, 1, r, C), lambda i, core_ref: (i, core_ref[0], 0, 0)),
                      pl.BlockSpec((1, 1, r, C), lambda i, core_ref: (i, 0, 0, 0))],
            out_specs=pl.BlockSpec((1, r, C), lambda i, core_ref: (i, 0, 0))),
        out_shape=jax.ShapeDtypeStruct((4, r, C), part.dtype), compiler_params=_params(("parallel",)),
    )(core, part, got)


def _ffn_bwd(tag, dy, dyb, x, gain, wgT, wuT, wd, saved, pending):
    n, g, u = saved
    half = lambda accs, ex: _swiglu_bwd_epilogue([0.5 * accs[0]], ex)
    (dg, du, a), done0 = _mm(tag + "_d_act", [(dyb, wd, "nt", 0)], [BF, BF, BF], tm=512, tn=1408, tk=D_MODEL,
                             epilogue=half, extras=[(g, "tile", 0), (u, "tile", 0)], comm=pending, cols_outer=True)
    sum_d = _dw_pair(tag + "_dw_down", a, dyb, 0.5)
    sum_g, (slots_d,) = _dw_pair(tag + "_dw_gate", dg, n, 1.0, comm=[_chip_task([sum_d])])
    sum_u, (slots_g,) = _dw_pair(tag + "_dw_up", du, n, 1.0, comm=[_chip_task([sum_g])])
    (dx, dxb, dgain), (slots_u,) = _mm(
        tag + "_d_norm", [(dg, wgT, "nn", 0), (du, wuT, "nn", 0)], [F32, BF], tm=512, tn=D_MODEL, tk=D_FF,
        epilogue=_rms_bwd_epilogue, extras=[(x, "tile", 0), (gain, "row", 0), (dy, "tile", 0)], n_colsum=1,
        comm=[_chip_task([sum_u])])
    return dx, dxb, dgain, done0, slots_g[0], slots_u[0], slots_d[0]


def _tile_gain(g):
    return jnp.concatenate([g, g]).reshape(1, LANES)


def _fold_heads(partials):
    return jnp.sum(partials.reshape(-1, HEAD_DIM), axis=0)


def _pack_small(norm1, mixn, norm2, pool_w, pool_scale, qn, kn, sinks, gate_bias, last):
    pad = lambda v: jnp.pad(v.reshape(-1), (0, LANES - v.size)).reshape(1, LANES)
    return jnp.concatenate([
        norm1.reshape(-1, LANES), mixn.reshape(-1, LANES), norm2.reshape(-1, LANES), pool_w.reshape(-1, LANES),
        pool_scale.reshape(-1, LANES), pad(qn), pad(kn), pad(sinks), gate_bias.reshape(-1, LANES), pad(last)], axis=0)


def _unpack_small(p):
    o, out = 0, []
    for rows, shape in ((8, (D_MODEL,)), (8, (D_MODEL,)), (8, (D_MODEL,)), (512, (N_POOL_GROUPS, POOL_GROUP, POOL_GROUP)),
                        (4, (POOL_WIDTH,)), (1, (HEAD_DIM,)), (1, (HEAD_DIM,)), (1, (N_HEADS,)), (16, (2 * D_MODEL,)), (1, (1,))):
        flat = p[o:o + rows].reshape(-1)
        size = 1
        for s in shape:
            size *= s
        out.append(flat[:size].reshape(shape))
        o += rows
    return out


def kernel(x, ffn1_norm, ffn1_w_gate, ffn1_w_up, ffn1_w_down, mix_norm, w_in, pool_w, pool_scale, w_pool_out, q_norm, k_norm, sinks, w_attn_out, gate_bias, w_out, ffn2_norm, ffn2_w_gate, ffn2_w_up, ffn2_w_down, loss_target, m_ffn1_norm, m_ffn1_w_gate, m_ffn1_w_up, m_ffn1_w_down, m_mix_norm, m_w_in, m_pool_w, m_pool_scale, m_w_pool_out, m_q_norm, m_k_norm, m_sinks, m_w_attn_out, m_gate_bias, m_w_out, m_ffn2_norm, m_ffn2_w_gate, m_ffn2_w_up, m_ffn2_w_down, v_ffn1_norm, v_ffn1_w_gate, v_ffn1_w_up, v_ffn1_w_down, v_mix_norm, v_w_in, v_pool_w, v_pool_scale, v_w_pool_out, v_q_norm, v_k_norm, v_sinks, v_w_attn_out, v_gate_bias, v_w_out, v_ffn2_norm, v_ffn2_w_gate, v_ffn2_w_up, v_ffn2_w_down):
    T = x.shape[1]
    x2 = x.reshape(T, D_MODEL)
    target = loss_target.reshape(T, D_MODEL)

    big = [
        ("ffn1_w_gate", ffn1_w_gate, m_ffn1_w_gate, v_ffn1_w_gate, True, False),
        ("ffn1_w_up", ffn1_w_up, m_ffn1_w_up, v_ffn1_w_up, True, False),
        ("ffn1_w_down", ffn1_w_down, m_ffn1_w_down, v_ffn1_w_down, False, False),
        ("w_in", w_in, m_w_in, v_w_in, True, False),
        ("w_pool_out", w_pool_out, m_w_pool_out, v_w_pool_out, False, True),
        ("w_attn_out", w_attn_out, m_w_attn_out, v_w_attn_out, False, False),
        ("w_out", w_out, m_w_out, v_w_out, False, False),
        ("ffn2_w_gate", ffn2_w_gate, m_ffn2_w_gate, v_ffn2_w_gate, True, False),
        ("ffn2_w_up", ffn2_w_up, m_ffn2_w_up, v_ffn2_w_up, True, False),
        ("ffn2_w_down", ffn2_w_down, m_ffn2_w_down, v_ffn2_w_down, False, False),
    ]
    view = lambda a, tv: a.T if tv else a
    shards = _prep("prep_weights", [view(w, tv) for _, w, _, _, tv, _ in big], [tk_ for *_, tk_ in big])
    mixer_natural = (0, 1, 2, 3)
    wg1T, wu1T, wd1 = _all_gather("gather_ffn1", shards[0:3])

    g1 = ffn1_norm.reshape(1, D_MODEL)
    g2 = mix_norm.reshape(1, D_MODEL)
    g3 = ffn2_norm.reshape(1, D_MODEL)
    bias_row = gate_bias.reshape(1, 2 * D_MODEL)
    qg, kg = _tile_gain(q_norm), _tile_gain(k_norm)
    scale_row = pool_scale.reshape(1, POOL_WIDTH)

    n1 = _rms_fwd("ffn1_norm", x2, g1)
    (gt1, up1, act1), (mixer_part,) = _mm(
        "ffn1_gate_up", [(n1, wg1T, "nt", 0), (n1, wu1T, "nt", 1)], [BF, BF, BF], tm=512, tn=1408, tk=D_MODEL,
        epilogue=_swiglu_fwd_epilogue, comm=[_gather_send_task(shards[3:7], mixer_natural)], cols_outer=True)
    (h1,), (mixer_full, wg2_part) = _mm(
        "ffn1_down", [(act1, wd1, "nn", 0)], [F32], tm=512, tn=D_MODEL, tk=D_FF, epilogue=_half_residual_epilogue,
        extras=[(x2, "tile", 0)], comm=[_gather_forward_task(mixer_part, mixer_natural), _gather_send_task(shards[7:8])])
    w_inT, w_poT, w_ao, w_o = mixer_full
    saved1 = (n1, gt1, up1)
    u = _rms_fwd("mix_norm", h1, g2)
    (proj,), ((wg2T,), ud2_part) = _mm(
        "in_proj", [(u, w_inT, "nt", 0)], [BF], tm=512, tn=1280, tk=D_MODEL,
        comm=[_gather_forward_task(wg2_part), _gather_send_task(shards[8:10])], cols_outer=True)
    pooled, mixed = _pool_fwd("pool_fwd", proj, pool_w, scale_row)
    qn = _headnorm_fwd("q_norm", proj, COL_Q, ATTN_WIDTH, qg)
    kn = _headnorm_fwd("k_norm", proj, COL_K, KV_WIDTH, kg)
    attn = _attn_fwd("attn_fwd", qn, kn, proj, sinks)
    (bp,) = _mm("pool_out", [(mixed, w_poT, "nt", 0)], [BF], tm=1024, tn=D_MODEL, tk=POOL_WIDTH)
    gate_tn = 256
    gate_extras = [(proj, "tile", COL_GP // gate_tn), (proj, "tile", COL_GA // gate_tn),
                   (bias_row, "row", 0), (bias_row, "row", D_MODEL // gate_tn)]
    (merged, ba), ((wu2T, wd2),) = _mm(
        "attn_out_merge", [(attn, w_ao, "nn", 0)], [BF, BF], tm=2048, tn=gate_tn, tk=ATTN_WIDTH,
        epilogue=_merge_fwd_epilogue, extras=[(bp, "tile", 0)] + gate_extras, comm=[_gather_forward_task(ud2_part)])
    (h2,) = _mm("mix_out", [(merged, w_o, "nn", 0)], [F32], tm=512, tn=D_MODEL, tk=D_MODEL,
                epilogue=_residual_epilogue, extras=[(h1, "tile", 0)])
    n2 = _rms_fwd("ffn2_norm", h2, g3)
    gt2, up2, act2 = _mm("ffn2_gate_up", [(n2, wg2T, "nt", 0), (n2, wu2T, "nt", 1)], [BF, BF, BF],
                         tm=512, tn=1408, tk=D_MODEL, epilogue=_swiglu_fwd_epilogue, cols_outer=True)
    dy, dyb, sq = _mm("ffn2_down_loss", [(act2, wd2, "nn", 0)], [F32, BF], tm=512, tn=D_MODEL, tk=D_FF,
                      epilogue=_loss_epilogue, extras=[(h2, "tile", 0), (target, "tile", 0)], n_colsum=1)
    loss_local = 0.5 * jnp.sum(sq) / D_MODEL

    dh2, dh2b, dg3, _, slots_g2, slots_u2, slots_d2 = _ffn_bwd(
        "ffn2", dy, dyb, h2, g3, wg2T, wu2T, wd2, (n2, gt2, up2), [])
    dbp, dba, dgp, dga, cs_gp, cs_ga = _mm(
        "mix_out_bwd", [(dh2b, w_o, "nt", 0)], [BF, BF, BF, BF], tm=2048, tn=gate_tn, tk=D_MODEL,
        epilogue=_merge_bwd_epilogue, extras=[(bp, "tile", 0), (ba, "tile", 0)] + gate_extras, n_colsum=2)
    sum_o = _dw_pair("dw_out", merged, dh2b, 1.0, blocks=4)
    (dmixed,), ((slots_o,),) = _mm("pool_out_bwd", [(dbp, w_poT, "nn", 0)], [BF], tm=1024, tn=POOL_WIDTH, tk=D_MODEL,
                                   comm=[_chip_task([sum_o])])
    sum_po = _dw_pair("dw_pool_out", dbp, mixed, 1.0, blocks=4)
    (dattn,), ((slots_po,),) = _mm("attn_out_bwd", [(dba, w_ao, "nt", 0)], [BF], tm=1024, tn=ATTN_WIDTH, tk=D_MODEL,
                                   comm=[_chip_task([sum_po])])
    sum_ao = _dw_pair("dw_attn_out", attn, dba, 1.0, blocks=4)
    dxp, dpool_w, dpool_scale = _pool_bwd("pool_bwd", dmixed, pooled, pool_w, scale_row)
    dqn, dkn, dv, dsink_tile = _attn_bwd("attn_bwd", dattn, qn, kn, proj, sinks)
    dq, dqg = _headnorm_bwd("q_norm_bwd", dqn, proj, COL_Q, ATTN_WIDTH, qg)
    dk, dkg = _headnorm_bwd("k_norm_bwd", dkn, proj, COL_K, KV_WIDTH, kg)
    dproj = jnp.concatenate([dxp, dq, dk, dv, dgp, dga], axis=1)
    (dh1, dh1b, dg2), ((slots_ao,),) = _mm(
        "in_proj_bwd", [(dproj, w_inT, "nn", 0)], [F32, BF], tm=512, tn=D_MODEL, tk=IN_WIDTH, epilogue=_rms_bwd_epilogue,
        extras=[(h1, "tile", 0), (g2, "row", 0), (dh2, "tile", 0)], n_colsum=1, comm=[_chip_task([sum_ao])])
    (dw_inT,) = _mm("dw_in", [(dproj, u, "tn", 0)], [BF], tm=1280, tn=D_MODEL, tk=2048)
    part_in = dw_inT.reshape(4, 2, IN_WIDTH // N_DEV, D_MODEL)
    (got_in,) = _pair_exchange("pair_exchange_w_in", [part_in])
    core = lax.axis_index("c").astype(jnp.int32).reshape(1)
    sum_in = _pair_sum("pair_sum_w_in", part_in, got_in, core).reshape(IN_WIDTH // 2, D_MODEL)
    dx, _, dg1, ((slots_in,),), slots_g1, slots_u1, slots_d1 = _ffn_bwd(
        "ffn1", dh1, dh1b, x2, g1, wg1T, wu1T, wd1, saved1, [_chip_task([sum_in])])

    slots = [slots_g1, slots_u1, slots_d1, slots_in, slots_po, slots_ao, slots_o, slots_g2, slots_u2, slots_d2]
    big_out = {}
    for k, (nm, w, m, v, tv, tk_) in enumerate(big):
        res = _adamw_sharded("adamw_" + nm, slots[k], view(w, tv), view(m, tv), view(v, tv), tk_)
        big_out[nm] = tuple(view(r, tv) for r in res)

    dgate_bias = jnp.concatenate([jnp.sum(cs_gp, axis=(0, 1)), jnp.sum(cs_ga, axis=(0, 1))])
    small_grad = _pack_small(
        jnp.sum(dg1, axis=(0, 1)), jnp.sum(dg2, axis=(0, 1)), jnp.sum(dg3, axis=(0, 1)), dpool_w, dpool_scale,
        _fold_heads(dqg), _fold_heads(dkg), dsink_tile[0, :N_HEADS], dgate_bias, loss_local.reshape(1))
    zero = jnp.zeros((1,), F32)
    small_w = _pack_small(ffn1_norm, mix_norm, ffn2_norm, pool_w, pool_scale, q_norm, k_norm, sinks, gate_bias, zero)
    small_m = _pack_small(m_ffn1_norm, m_mix_norm, m_ffn2_norm, m_pool_w, m_pool_scale, m_q_norm, m_k_norm, m_sinks,
                          m_gate_bias, zero)
    small_v = _pack_small(v_ffn1_norm, v_mix_norm, v_ffn2_norm, v_pool_w, v_pool_scale, v_q_norm, v_k_norm, v_sinks,
                          v_gate_bias, zero)
    (gathered,) = _all_gather("gather_small_grads", [small_grad])
    packed = _adamw_small("adamw_small", gathered.reshape(N_DEV, -1, LANES), small_w, small_m, small_v)
    small_names = ["ffn1_norm", "mix_norm", "ffn2_norm", "pool_w", "pool_scale", "q_norm", "k_norm", "sinks", "gate_bias"]
    small_out = {}
    unpacked = [_unpack_small(p) for p in packed]
    for i, nm in enumerate(small_names):
        small_out[nm] = tuple(unpacked[j][i] for j in range(4))
    loss = unpacked[0][9].reshape(())

    order = ["ffn1_norm", "ffn1_w_gate", "ffn1_w_up", "ffn1_w_down", "mix_norm", "w_in", "pool_w", "pool_scale",
             "w_pool_out", "q_norm", "k_norm", "sinks", "w_attn_out", "gate_bias", "w_out", "ffn2_norm",
             "ffn2_w_gate", "ffn2_w_up", "ffn2_w_down"]
    every = {**big_out, **small_out}
    outs = [loss, dx.reshape(x.shape)]
    for j in range(4):
        outs += [every[nm][j] for nm in order]
    return tuple(outs)
```

```python
import functools

import jax
import jax.numpy as jnp
from jax import lax
from jax.experimental import pallas as pl
from jax.experimental.pallas import tpu as pltpu

BF = jnp.bfloat16
F32 = jnp.float32

D_MODEL = 1024
D_FF = 2816
POOL_WIDTH = 512
POOL_GROUP = 128
N_POOL_GROUPS = 4
HEAD_DIM = 64
N_HEADS = 16
GQA_GROUP = 8
BLOCK = 128
ATTN_WIDTH = 1024
KV_WIDTH = 128
IN_WIDTH = 3840
RMS_EPS = 1e-6
N_DEV = 8
LANES = 128

COL_Q = POOL_WIDTH
COL_K = COL_Q + ATTN_WIDTH
COL_V = COL_K + KV_WIDTH
COL_GP = COL_V + KV_WIDTH
COL_GA = COL_GP + D_MODEL

ADAM_LR = 0.001
ADAM_B1 = 0.9
ADAM_B2 = 0.999
ADAM_EPS = 1e-08
ADAM_WD = 0.01
ADAM_STEP = 10

VMEM_LIMIT_V7X = 56 * 1024 * 1024
MESH = pl.DeviceIdType.MESH
ANY = pl.BlockSpec(memory_space=pl.ANY)


def _params(sem=None):
    return pltpu.CompilerParams(dimension_semantics=sem, vmem_limit_bytes=VMEM_LIMIT_V7X)


_DIMS = {"nt": (((1,), (1,)), ((), ())), "nn": (((1,), (0,)), ((), ())), "tn": (((0,), (0,)), ((), ()))}


class _Task:
    def __init__(self, inputs, out_shapes, scratch, start, finish, aliases=None):
        self.inputs, self.out_shapes, self.scratch = list(inputs), list(out_shapes), list(scratch)
        self.start, self.finish, self.aliases = start, finish, dict(aliases or {})


class _CommPlumbing:
    def __init__(self, tasks):
        self.tasks = list(tasks or [])
        self.args = [a for t in self.tasks for a in t.inputs]
        self.out_shapes = [o for t in self.tasks for o in t.out_shapes]
        self.scratch = [s for t in self.tasks for s in t.scratch]
        self.n_in, self.n_out = len(self.args), len(self.out_shapes)

    def _slices(self, c_in, c_out, c_scr):
        i = o = s = 0
        for t in self.tasks:
            yield t, c_in[i:i + len(t.inputs)], c_out[o:o + len(t.out_shapes)], c_scr[s:s + len(t.scratch)]
            i, o, s = i + len(t.inputs), o + len(t.out_shapes), s + len(t.scratch)

    def start(self, c_in, c_out, c_scr):
        for t, ins, outs, scr in self._slices(c_in, c_out, c_scr):
            t.start(ins, outs, scr)

    def finish(self, c_in, c_out, c_scr):
        for t, ins, outs, scr in self._slices(c_in, c_out, c_scr):
            t.finish(ins, outs, scr)

    def aliases(self, in_base, out_base):
        res, i, o = {}, in_base, out_base
        for t in self.tasks:
            for a, b in t.aliases.items():
                res[i + a] = o + b
            i, o = i + len(t.inputs), o + len(t.out_shapes)
        return res

    def split_outputs(self, flat):
        res, o = [], 0
        for t in self.tasks:
            res.append(list(flat[o:o + len(t.out_shapes)]))
            o += len(t.out_shapes)
        return res


def _comm_only(name, tasks):
    plumb = _CommPlumbing(tasks)

    def body(*refs):
        c_in, c_out = refs[:plumb.n_in], refs[plumb.n_in: plumb.n_in + plumb.n_out]
        c_scr = refs[plumb.n_in + plumb.n_out:]
        plumb.start(c_in, c_out, c_scr)
        plumb.finish(c_in, c_out, c_scr)

    res = pl.pallas_call(
        body, name=name, in_specs=[ANY] * plumb.n_in, out_specs=[ANY] * plumb.n_out, out_shape=plumb.out_shapes,
        scratch_shapes=plumb.scratch, input_output_aliases=plumb.aliases(0, 0),
        compiler_params=pltpu.CompilerParams(has_side_effects=True),
    )(*plumb.args)
    return plumb.split_outputs(res)


def _mm(name, terms, out_dtypes, *, tm, tn, tk, epilogue=None, extras=(), n_colsum=0, comm=None, cols_outer=False):
    a0, b0, mode0, _ = terms[0]
    if mode0 == "nt":
        (M, K), N = a0.shape, b0.shape[0]
    elif mode0 == "nn":
        (M, K), N = a0.shape, b0.shape[1]
    else:
        (K, M), N = a0.shape, b0.shape[1]
    tm, tn, tk = min(tm, M), min(tn, N), min(tk, K)
    assert M % tm == 0 and N % tn == 0 and K % tk == 0, (name, M, N, K, tm, tn, tk)
    nI, nJ, nK = M // tm, N // tn, K // tk
    n_terms = len(terms)
    n_acc = max(t[3] for t in terms) + 1
    n_ex = len(extras)
    n_out = len(out_dtypes)
    if epilogue is None:
        epilogue = lambda accs, ex: ([accs[0]], [])
    plumb = _CommPlumbing(comm)
    n_scr = n_acc if nK > 1 else 0
    grid = (nJ, nI, nK) if cols_outer else (nI, nJ, nK)

    def body(*refs):
        n_in = 2 * n_terms + n_ex
        ab = refs[: 2 * n_terms]
        ex_refs = refs[2 * n_terms: n_in]
        c_in = refs[n_in: n_in + plumb.n_in]
        o0 = n_in + plumb.n_in
        out_refs = refs[o0: o0 + n_out]
        cs_refs = refs[o0 + n_out: o0 + n_out + n_colsum]
        c_out = refs[o0 + n_out + n_colsum: o0 + n_out + n_colsum + plumb.n_out]
        s0 = o0 + n_out + n_colsum + plumb.n_out
        acc_refs = refs[s0: s0 + n_scr]
        c_scr = refs[s0 + n_scr:]
        if comm:
            i_, j_, k_ = pl.program_id(0), pl.program_id(1), pl.program_id(2)

            @pl.when((i_ == 0) & (j_ == 0) & (k_ == 0))
            def _():
                plumb.start(c_in, c_out, c_scr)

        def products():
            accs = [None] * n_acc
            for t, (_, _, mode, ai) in enumerate(terms):
                p = lax.dot_general(ab[2 * t][...], ab[2 * t + 1][...], _DIMS[mode], preferred_element_type=F32)
                accs[ai] = p if accs[ai] is None else accs[ai] + p
            return accs

        def finish(accs):
            outs, colsums = epilogue(accs, [r[...] for r in ex_refs])
            for r, o in zip(out_refs, outs):
                r[...] = o.astype(r.dtype)
            for r, cs in zip(cs_refs, colsums):
                r[...] = jnp.sum(cs, axis=0, keepdims=True).reshape(r.shape)

        if nK == 1:
            finish(products())
        else:
            k = pl.program_id(2)
            accs = products()

            @pl.when(k == 0)
            def _():
                for r, a in zip(acc_refs, accs):
                    r[...] = a

            @pl.when(k > 0)
            def _():
                for r, a in zip(acc_refs, accs):
                    r[...] += a

            @pl.when(k == nK - 1)
            def _():
                finish([r[...] for r in acc_refs])

        if comm:
            @pl.when((i_ == grid[0] - 1) & (j_ == grid[1] - 1) & (k_ == nK - 1))
            def _():
                plumb.finish(c_in, c_out, c_scr)

    def spec(block, index, fixed=False):
        imap = (lambda q, p, k: index(p, q, k)) if cols_outer else index
        return pl.BlockSpec(block, imap, pipeline_mode=pl.Buffered(1)) if fixed else pl.BlockSpec(block, imap)

    in_specs, args = [], []
    for a, b, mode, _ in terms:
        if mode == "nt":
            in_specs += [spec((tm, tk), lambda i, j, k: (i, k), nI * nK == 1),
                         spec((tn, tk), lambda i, j, k: (j, k), nJ * nK == 1)]
        elif mode == "nn":
            in_specs += [spec((tm, tk), lambda i, j, k: (i, k), nI * nK == 1),
                         spec((tk, tn), lambda i, j, k: (k, j), nJ * nK == 1)]
        else:
            in_specs += [spec((tk, tm), lambda i, j, k: (k, i), nI * nK == 1),
                         spec((tk, tn), lambda i, j, k: (k, j), nJ * nK == 1)]
        args += [a, b]
    for arr, kind, off in extras:
        if kind == "tile":
            in_specs.append(spec((tm, tn), functools.partial(lambda i, j, k, off: (i, j + off), off=off)))
        else:
            in_specs.append(spec((1, tn), functools.partial(lambda i, j, k, off: (0, j + off), off=off)))
        args.append(arr)
    out_shape = [jax.ShapeDtypeStruct((M, N), dt) for dt in out_dtypes]
    out_specs = [spec((tm, tn), lambda i, j, k: (i, j)) for _ in out_dtypes]
    out_shape += [jax.ShapeDtypeStruct((nI, 1, N), F32) for _ in range(n_colsum)]
    out_specs += [spec((1, 1, tn), lambda i, j, k: (i, 0, j)) for _ in range(n_colsum)]
    scratch = [pltpu.VMEM((tm, tn), F32) for _ in range(n_scr)]
    aliases = plumb.aliases(len(args), len(out_shape))
    args += plumb.args
    in_specs += [ANY] * plumb.n_in
    out_shape += plumb.out_shapes
    out_specs += [ANY] * plumb.n_out
    sem = ("arbitrary",) * 3 if comm else ("parallel", "parallel", "arbitrary")
    res = pl.pallas_call(
        body, name=name, grid=grid, in_specs=in_specs, out_specs=out_specs, out_shape=out_shape,
        scratch_shapes=scratch + plumb.scratch, input_output_aliases=aliases, compiler_params=_params(sem),
    )(*args)
    n_own = n_out + n_colsum
    return (list(res[:n_own]), plumb.split_outputs(res[n_own:])) if comm is not None else res


ROW_TILE = 512


def _rms_fwd(name, x, g):
    T, D = x.shape

    def body(x_ref, g_ref, o_ref):
        xv = x_ref[...]
        r = lax.rsqrt(jnp.mean(xv * xv, axis=-1, keepdims=True) + RMS_EPS)
        o_ref[...] = (xv * r * g_ref[...]).astype(BF)

    return pl.pallas_call(
        body, name=name, grid=(T // ROW_TILE,),
        in_specs=[pl.BlockSpec((ROW_TILE, D), lambda i: (i, 0)), pl.BlockSpec((1, D), lambda i: (0, 0))],
        out_specs=pl.BlockSpec((ROW_TILE, D), lambda i: (i, 0)),
        out_shape=jax.ShapeDtypeStruct((T, D), BF), compiler_params=_params(("parallel",)),
    )(x, g)


HEADNORM_TILE = 1024


def _half_sum_matrix():
    r = lax.broadcasted_iota(jnp.int32, (LANES, LANES), 0) // HEAD_DIM
    c = lax.broadcasted_iota(jnp.int32, (LANES, LANES), 1) // HEAD_DIM
    return (r == c).astype(BF)


def _head_mean(v, ones_blockdiag):
    hi = v.astype(BF)
    lo = (v - hi.astype(F32)).astype(BF)
    s = jnp.dot(hi, ones_blockdiag, preferred_element_type=F32) + jnp.dot(lo, ones_blockdiag, preferred_element_type=F32)
    return s * (1.0 / HEAD_DIM)


def _headnorm_fwd(name, proj, col0, width, g2):
    T = proj.shape[0]
    wide = min(width, GROUP_WIDTH)
    nb, off = width // wide, col0 // wide

    def body(x_ref, g_ref, b_ref, o_ref):
        for s in range(wide // LANES):
            lanes = slice(LANES * s, LANES * (s + 1))
            xv = x_ref[:, lanes].astype(F32)
            r = lax.rsqrt(_head_mean(xv * xv, b_ref[...]) + RMS_EPS)
            o_ref[:, lanes] = (xv * r * g_ref[...]).astype(BF)

    return pl.pallas_call(
        body, name=name, grid=(T // HEADNORM_TILE, nb),
        in_specs=[pl.BlockSpec((HEADNORM_TILE, wide), lambda i, j: (i, j + off)),
                  pl.BlockSpec((1, LANES), lambda i, j: (0, 0)), pl.BlockSpec((LANES, LANES), lambda i, j: (0, 0))],
        out_specs=pl.BlockSpec((HEADNORM_TILE, wide), lambda i, j: (i, j)),
        out_shape=jax.ShapeDtypeStruct((T, width), BF), compiler_params=_params(("parallel", "parallel")),
    )(proj, g2, _half_sum_matrix())


def _headnorm_bwd(name, dy, proj, col0, width, g2):
    T = proj.shape[0]
    wide = min(width, GROUP_WIDTH)
    nb, off = width // wide, col0 // wide

    def body(dy_ref, x_ref, g_ref, b_ref, dx_ref, dg_ref):
        for s in range(wide // LANES):
            lanes = slice(LANES * s, LANES * (s + 1))
            xv = x_ref[:, lanes].astype(F32)
            dyv = dy_ref[:, lanes].astype(F32)
            r = lax.rsqrt(_head_mean(xv * xv, b_ref[...]) + RMS_EPS)
            xhat = xv * r
            dxhat = dyv * g_ref[...]
            dx_ref[:, lanes] = (r * (dxhat - xhat * _head_mean(dxhat * xhat, b_ref[...]))).astype(BF)
            dg_ref[0, :, lanes] = jnp.sum(dyv * xhat, axis=0, keepdims=True)

    return pl.pallas_call(
        body, name=name, grid=(T // HEADNORM_TILE, nb),
        in_specs=[pl.BlockSpec((HEADNORM_TILE, wide), lambda i, j: (i, j)),
                  pl.BlockSpec((HEADNORM_TILE, wide), lambda i, j: (i, j + off)),
                  pl.BlockSpec((1, LANES), lambda i, j: (0, 0)), pl.BlockSpec((LANES, LANES), lambda i, j: (0, 0))],
        out_specs=[pl.BlockSpec((HEADNORM_TILE, wide), lambda i, j: (i, j)),
                   pl.BlockSpec((1, 1, wide), lambda i, j: (i, 0, j))],
        out_shape=[jax.ShapeDtypeStruct((T, width), BF), jax.ShapeDtypeStruct((T // HEADNORM_TILE, 1, width), F32)],
        compiler_params=_params(("parallel", "parallel")),
    )(dy, proj, g2, _half_sum_matrix())


def _shift_down(v, k, row):
    return jnp.where(row >= k, pltpu.roll(v, k, axis=0), 0.0)


def _shift_up(v, k, row, T):
    return jnp.where(row < T - k, pltpu.roll(v, T - k, axis=0), 0.0)


def _by_group(g, vals):
    out = vals[-1]
    for i in range(len(vals) - 2, -1, -1):
        out = jnp.where(g == i, vals[i], out)
    return out


def _pool_fwd(name, proj, pool_w, pool_scale):
    T = proj.shape[0]

    def body(x_ref, w_ref, s_ref, pooled_ref, mixed_ref):
        g = pl.program_id(0)
        xv = x_ref[...].astype(F32)
        row = lax.broadcasted_iota(jnp.int32, (T, 1), 0)
        s2 = xv + _shift_down(xv, 1, row)
        s4 = s2 + _shift_down(s2, 2, row)
        s8 = s4 + _shift_down(s4, 4, row)
        s16 = s8 + _shift_down(s8, 8, row)
        wsum = _by_group(g, [s2, s4, s8, s16])
        count = jnp.minimum(row + 1, 2 << g).astype(F32)
        pooled = (wsum / count - xv).astype(BF)
        pooled_ref[...] = pooled
        mixed = jnp.dot(pooled, w_ref[0].astype(BF), preferred_element_type=F32) * s_ref[...]
        mixed_ref[...] = mixed.astype(BF)

    col = pl.BlockSpec((T, POOL_GROUP), lambda g: (0, g))
    return pl.pallas_call(
        body, name=name, grid=(N_POOL_GROUPS,),
        in_specs=[col, pl.BlockSpec((1, POOL_GROUP, POOL_GROUP), lambda g: (g, 0, 0)),
                  pl.BlockSpec((1, POOL_GROUP), lambda g: (0, g))],
        out_specs=[col, col],
        out_shape=[jax.ShapeDtypeStruct((T, POOL_WIDTH), BF), jax.ShapeDtypeStruct((T, POOL_WIDTH), BF)],
        compiler_params=_params(("parallel",)),
    )(proj, pool_w, pool_scale)


def _pool_bwd(name, dmixed, pooled, pool_w, pool_scale):
    T = dmixed.shape[0]

    def body(dm_ref, p_ref, w_ref, s_ref, dx_ref, dw_ref, ds_ref):
        g = pl.program_id(0)
        dm = dm_ref[...].astype(F32)
        pooled = p_ref[...]
        w = w_ref[0].astype(BF)
        pre = jnp.dot(pooled, w, preferred_element_type=F32)
        ds_ref[...] = jnp.sum(dm * pre, axis=0, keepdims=True)
        dms = (dm * s_ref[...]).astype(BF)
        dw_ref[0] = lax.dot_general(pooled, dms, _DIMS["tn"], preferred_element_type=F32)
        dpooled = lax.dot_general(dms, w, _DIMS["nt"], preferred_element_type=F32)
        row = lax.broadcasted_iota(jnp.int32, (T, 1), 0)
        count = jnp.minimum(row + 1, 2 << g).astype(F32)
        z = dpooled / count
        l2 = z + _shift_up(z, 1, row, T)
        l4 = l2 + _shift_up(l2, 2, row, T)
        l8 = l4 + _shift_up(l4, 4, row, T)
        l16 = l8 + _shift_up(l8, 8, row, T)
        dx_ref[...] = (_by_group(g, [l2, l4, l8, l16]) - dpooled).astype(BF)

    col = pl.BlockSpec((T, POOL_GROUP), lambda g: (0, g))
    wspec = pl.BlockSpec((1, POOL_GROUP, POOL_GROUP), lambda g: (g, 0, 0))
    sspec = pl.BlockSpec((1, POOL_GROUP), lambda g: (0, g))
    return pl.pallas_call(
        body, name=name, grid=(N_POOL_GROUPS,), in_specs=[col, col, wspec, sspec], out_specs=[col, wspec, sspec],
        out_shape=[jax.ShapeDtypeStruct((T, POOL_WIDTH), BF),
                   jax.ShapeDtypeStruct((N_POOL_GROUPS, POOL_GROUP, POOL_GROUP), F32),
                   jax.ShapeDtypeStruct((1, POOL_WIDTH), F32)],
        compiler_params=_params(("parallel",)),
    )(dmixed, pooled, pool_w, pool_scale)


ATTN_SCALE = HEAD_DIM ** -0.5
MASKED = float(jnp.finfo(jnp.float32).min)
KV_COL_BLOCK_K = COL_K // LANES
KV_COL_BLOCK_V = COL_V // LANES
GROUP_WIDTH = GQA_GROUP * HEAD_DIM


def _dup_head(v, j):
    half = lax.broadcasted_iota(jnp.int32, (1, LANES), 1) // HEAD_DIM
    return jnp.where(half == j, v, pltpu.roll(v, HEAD_DIM, axis=1))


def _stack_heads(v, low):
    pieces = []
    for p in range(GROUP_WIDTH // LANES):
        vp = v[:, LANES * p: LANES * (p + 1)]
        pieces.append(jnp.where(low, vp, jnp.zeros_like(vp)))
        pieces.append(jnp.where(low, jnp.zeros_like(vp), vp))
    return jnp.concatenate(pieces, axis=0)


def _unstack_heads(st, low):
    pieces = []
    for p in range(GROUP_WIDTH // LANES):
        even = st[BLOCK * (2 * p): BLOCK * (2 * p + 1)]
        odd = st[BLOCK * (2 * p + 1): BLOCK * (2 * p + 2)]
        pieces.append(jnp.where(low, even, odd))
    return jnp.concatenate(pieces, axis=1)


def _band_mask(n):
    row = lax.broadcasted_iota(jnp.int32, (BLOCK, 2 * BLOCK), 0)
    col = lax.broadcasted_iota(jnp.int32, (BLOCK, 2 * BLOCK), 1)
    return (col > row) & (col <= row + BLOCK) & ((n > 0) | (col >= BLOCK))


def _softmax_heads(s, valid, sink_ref, j):
    ps, psinks = [], []
    for h in range(GQA_GROUP):
        sh = jnp.where(valid, s[BLOCK * h: BLOCK * (h + 1)], MASKED)
        sink = sink_ref[j * GQA_GROUP + h]
        m = jnp.maximum(jnp.max(sh, axis=1, keepdims=True), sink)
        e = jnp.exp(sh - m)
        es = jnp.exp(sink - m)
        inv = 1.0 / (jnp.sum(e, axis=1, keepdims=True) + es)
        ps.append(e * inv)
        psinks.append(es * inv)
    return jnp.concatenate(ps, axis=0), jnp.concatenate(psinks, axis=0)


def _attn_fwd(name, qn, kn, proj, sinks):
    T = qn.shape[0]
    nb = T // BLOCK

    def body(sink_ref, q_ref, kp_ref, kc_ref, vp_ref, vc_ref, o_ref):
        n, j = pl.program_id(0), pl.program_id(1)
        low = lax.broadcasted_iota(jnp.int32, (1, LANES), 1) < HEAD_DIM
        k2 = _dup_head(jnp.concatenate([kp_ref[...], kc_ref[...]], axis=0), j)
        v2 = _dup_head(jnp.concatenate([vp_ref[...], vc_ref[...]], axis=0), j)
        q = _stack_heads(q_ref[...], low)
        s = lax.dot_general(q, k2, _DIMS["nt"], preferred_element_type=F32) * ATTN_SCALE
        p, _ = _softmax_heads(s, _band_mask(n), sink_ref, j)
        o = jnp.dot(p.astype(BF), v2, preferred_element_type=F32)
        o_ref[...] = _unstack_heads(o, low).astype(BF)

    prev = lambda n, j: (jnp.maximum(n - 1, 0), 0)
    return pl.pallas_call(
        body, name=name, grid=(nb, 2),
        in_specs=[pl.BlockSpec(memory_space=pltpu.SMEM),
                  pl.BlockSpec((BLOCK, GROUP_WIDTH), lambda n, j: (n, j)),
                  pl.BlockSpec((BLOCK, LANES), prev), pl.BlockSpec((BLOCK, LANES), lambda n, j: (n, 0)),
                  pl.BlockSpec((BLOCK, LANES), lambda n, j: (jnp.maximum(n - 1, 0), KV_COL_BLOCK_V)),
                  pl.BlockSpec((BLOCK, LANES), lambda n, j: (n, KV_COL_BLOCK_V))],
        out_specs=pl.BlockSpec((BLOCK, GROUP_WIDTH), lambda n, j: (n, j)),
        out_shape=jax.ShapeDtypeStruct((T, ATTN_WIDTH), BF), compiler_params=_params(("parallel", "parallel")),
    )(sinks, qn, kn, kn, proj, proj)


def _attn_bwd(name, dout, qn, kn, proj, sinks):
    T = qn.shape[0]
    nb = T // BLOCK

    def body(sink_ref, do_ref, q_ref, kp_ref, kc_ref, vp_ref, vc_ref, dq_ref, dk_ref, dv_ref, dsink_ref,
             carry_k, carry_v, tot_k, tot_v):
        n = pl.program_id(0)
        lane = lax.broadcasted_iota(jnp.int32, (1, LANES), 1)
        low = lane < HEAD_DIM

        @pl.when(n == 0)
        def _():
            carry_k[...] = jnp.zeros_like(carry_k)
            carry_v[...] = jnp.zeros_like(carry_v)
            dsink_ref[...] = jnp.zeros_like(dsink_ref)

        @pl.when(n == nb)
        def _():
            tot_k[...] = jnp.zeros_like(tot_k)
            tot_v[...] = jnp.zeros_like(tot_v)

        @pl.when(n < nb)
        def _():
            kk = jnp.concatenate([kp_ref[...], kc_ref[...]], axis=0)
            vv = jnp.concatenate([vp_ref[...], vc_ref[...]], axis=0)
            valid = _band_mask(n)
            dk_tot = jnp.zeros((2 * BLOCK, LANES), F32)
            dv_tot = jnp.zeros((2 * BLOCK, LANES), F32)
            dsink = jnp.zeros((1, LANES), F32)
            for j in range(2):
                k2 = _dup_head(kk, j)
                v2 = _dup_head(vv, j)
                q = _stack_heads(q_ref[:, GROUP_WIDTH * j: GROUP_WIDTH * (j + 1)], low)
                do = _stack_heads(do_ref[:, GROUP_WIDTH * j: GROUP_WIDTH * (j + 1)], low)
                s = lax.dot_general(q, k2, _DIMS["nt"], preferred_element_type=F32) * ATTN_SCALE
                p, psink = _softmax_heads(s, valid, sink_ref, j)
                dp = lax.dot_general(do, v2, _DIMS["nt"], preferred_element_type=F32)
                delta = jnp.sum(p * dp, axis=1, keepdims=True)
                ds = (p * (dp - delta) * ATTN_SCALE).astype(BF)
                dq_ref[:, GROUP_WIDTH * j: GROUP_WIDTH * (j + 1)] = _unstack_heads(
                    jnp.dot(ds, k2, preferred_element_type=F32), low).astype(BF)
                dk2 = lax.dot_general(ds, q, _DIMS["tn"], preferred_element_type=F32)
                dv2 = lax.dot_general(p.astype(BF), do, _DIMS["tn"], preferred_element_type=F32)
                mine = low if j == 0 else jnp.logical_not(low)
                dk_tot = dk_tot + jnp.where(mine, dk2 + pltpu.roll(dk2, HEAD_DIM, axis=1), 0.0)
                dv_tot = dv_tot + jnp.where(mine, dv2 + pltpu.roll(dv2, HEAD_DIM, axis=1), 0.0)
                sink_term = psink * delta
                for h in range(GQA_GROUP):
                    val = -jnp.sum(sink_term[BLOCK * h: BLOCK * (h + 1)], axis=0, keepdims=True)
                    dsink = dsink + jnp.where(lane == j * GQA_GROUP + h, val, 0.0)
            tot_k[...] = dk_tot
            tot_v[...] = dv_tot
            dsink_ref[0:1, :] += dsink

        dk_ref[...] = (carry_k[...] + tot_k[0:BLOCK]).astype(BF)
        dv_ref[...] = (carry_v[...] + tot_v[0:BLOCK]).astype(BF)
        carry_k[...] = tot_k[BLOCK:]
        carry_v[...] = tot_v[BLOCK:]

    cur = lambda n: (jnp.minimum(n, nb - 1), 0)
    prev = lambda n: (jnp.maximum(n - 1, 0), 0)
    wide = pl.BlockSpec((BLOCK, ATTN_WIDTH), cur)
    return pl.pallas_call(
        body, name=name, grid=(nb + 1,),
        in_specs=[pl.BlockSpec(memory_space=pltpu.SMEM), wide, wide,
                  pl.BlockSpec((BLOCK, LANES), prev), pl.BlockSpec((BLOCK, LANES), cur),
                  pl.BlockSpec((BLOCK, LANES), lambda n: (jnp.maximum(n - 1, 0), KV_COL_BLOCK_V)),
                  pl.BlockSpec((BLOCK, LANES), lambda n: (jnp.minimum(n, nb - 1), KV_COL_BLOCK_V))],
        out_specs=[wide, pl.BlockSpec((BLOCK, LANES), prev), pl.BlockSpec((BLOCK, LANES), prev),
                   pl.BlockSpec((8, LANES), lambda n: (0, 0))],
        out_shape=[jax.ShapeDtypeStruct((T, ATTN_WIDTH), BF), jax.ShapeDtypeStruct((T, KV_WIDTH), BF),
                   jax.ShapeDtypeStruct((T, KV_WIDTH), BF), jax.ShapeDtypeStruct((8, LANES), F32)],
        scratch_shapes=[pltpu.VMEM((BLOCK, LANES), F32), pltpu.VMEM((BLOCK, LANES), F32),
                        pltpu.VMEM((2 * BLOCK, LANES), F32), pltpu.VMEM((2 * BLOCK, LANES), F32)],
        compiler_params=_params(("arbitrary",)),
    )(sinks, dout, qn, kn, kn, proj, proj)


def _swiglu_fwd_epilogue(accs, ex):
    g, u = accs
    return [g, u, g * jax.nn.sigmoid(g) * u], []


def _swiglu_bwd_epilogue(accs, ex):
    (da,) = accs
    g, u = ex[0].astype(F32), ex[1].astype(F32)
    s = jax.nn.sigmoid(g)
    silu = g * s
    return [da * u * (s * (1.0 + g * (1.0 - s))), da * silu, silu * u], []


def _half_residual_epilogue(accs, ex):
    return [ex[0] + 0.5 * accs[0]], []


def _residual_epilogue(accs, ex):
    return [ex[0] + accs[0]], []


def _rms_bwd_epilogue(accs, ex):
    (dn,) = accs
    xv, g, dres = ex
    r = lax.rsqrt(jnp.mean(xv * xv, axis=-1, keepdims=True) + RMS_EPS)
    xhat = xv * r
    dxhat = dn * g
    dx = dres + r * (dxhat - xhat * jnp.mean(dxhat * xhat, axis=-1, keepdims=True))
    return [dx, dx], [dn * xhat]


def _loss_epilogue(accs, ex):
    xv, target = ex
    d = xv + 0.5 * accs[0] - target
    dy = d * (1.0 / D_MODEL)
    return [dy, dy], [d * d]


def _merge_fwd_epilogue(accs, ex):
    (ba,) = accs
    bp, gp_pre, ga_pre, bias_p, bias_a = ex
    gp = jax.nn.sigmoid(gp_pre.astype(F32) + bias_p)
    ga = jax.nn.sigmoid(ga_pre.astype(F32) + bias_a)
    return [gp * bp.astype(F32) + ga * ba, ba], []


def _merge_bwd_epilogue(accs, ex):
    (dm,) = accs
    bp, ba, gp_pre, ga_pre, bias_p, bias_a = ex
    gp = jax.nn.sigmoid(gp_pre.astype(F32) + bias_p)
    ga = jax.nn.sigmoid(ga_pre.astype(F32) + bias_a)
    dgp = dm * bp.astype(F32) * gp * (1.0 - gp)
    dga = dm * ba.astype(F32) * ga * (1.0 - ga)
    return [dm * gp, dm * ga, dgp, dga], [dgp, dga]


def _prep(name, ws, transposes):
    n = len(ws)

    def body(*refs):
        for w_ref, o_ref, tr in zip(refs[:n], refs[n:], transposes):
            v = w_ref[...]
            o_ref[...] = (v.T if tr else v).astype(BF)

    shapes = [jax.ShapeDtypeStruct(w.shape[::-1] if tr else w.shape, BF) for w, tr in zip(ws, transposes)]
    return pl.pallas_call(body, name=name, out_shape=shapes, compiler_params=_params())(*ws)


def _adam_math(w, g, m, v):
    m = ADAM_B1 * m + (1.0 - ADAM_B1) * g
    v = ADAM_B2 * v + (1.0 - ADAM_B2) * jnp.square(g)
    m_hat = m / (1.0 - ADAM_B1 ** ADAM_STEP)
    v_hat = v / (1.0 - ADAM_B2 ** ADAM_STEP)
    delta = -ADAM_LR * (m_hat / (jnp.sqrt(v_hat) + ADAM_EPS) + ADAM_WD * w)
    return delta, m, v


def _adamw_sharded(name, slots, w, m, v, transpose):
    def body(s_ref, w_ref, m_ref, v_ref, g_out, d_out, m_out, v_out):
        g = s_ref[0].astype(F32)
        for i in range(1, 4):
            g = g + s_ref[i].astype(F32)
        if transpose:
            g = g.T
        delta, mn, vn = _adam_math(w_ref[...], g, m_ref[...], v_ref[...])
        g_out[...] = g
        d_out[...] = delta
        m_out[...] = mn
        v_out[...] = vn

    out_shape = [jax.ShapeDtypeStruct(w.shape, F32)] * 4
    _, r, C = slots.shape
    rows = r // 4
    if transpose or rows % 8:
        return pl.pallas_call(body, name=name, out_shape=out_shape, compiler_params=_params())(slots, w, m, v)
    tile = pl.BlockSpec((rows, C), lambda i: (i, 0))
    return pl.pallas_call(
        body, name=name, grid=(4,), in_specs=[pl.BlockSpec((4, rows, C), lambda i: (0, i, 0)), tile, tile, tile],
        out_specs=[tile] * 4, out_shape=out_shape, compiler_params=_params(("parallel",)),
    )(slots, w, m, v)


SMALL_LAYOUT = (("ffn1_norm", 0, (8, LANES)), ("mix_norm", 8, (8, LANES)), ("ffn2_norm", 16, (8, LANES)),
                ("gate_bias", 24, (16, LANES)), ("pool_scale", 40, (4, LANES)), ("q_norm", 48, (1, HEAD_DIM)),
                ("k_norm", 56, (1, HEAD_DIM)), ("sinks", 64, (1, N_HEADS)))
LOSS_ROW = 72
SMALL_ROWS = 80


def _adamw_small(name, g_vec, g_pool_w, params):
    n = len(SMALL_LAYOUT) + 1

    def body(vec_ref, pw_ref, *refs):
        ins, outs = refs[:3 * n], refs[3 * n:]
        vec = vec_ref[0]
        pw = pw_ref[0]
        for i in range(1, N_DEV):
            vec = vec + vec_ref[i]
            pw = pw + pw_ref[i]
        grads = [vec[r0:r0 + shape[0], 0:shape[1]] for _, r0, shape in SMALL_LAYOUT] + [pw]
        for p, g in enumerate(grads):
            w_ref, m_ref, v_ref = ins[3 * p: 3 * p + 3]
            delta, mn, vn = _adam_math(w_ref[...], g, m_ref[...], v_ref[...])
            for o_ref, val in zip(outs[4 * p: 4 * p + 4], (g, delta, mn, vn)):
                o_ref[...] = val
        outs[4 * n][...] = vec[LOSS_ROW:LOSS_ROW + 1, :]

    flat = [a for wmv in params for a in wmv]
    out_shape = [jax.ShapeDtypeStruct(wmv[0].shape, F32) for wmv in params for _ in range(4)]
    out_shape.append(jax.ShapeDtypeStruct((1, LANES), F32))
    res = pl.pallas_call(body, name=name, out_shape=out_shape, compiler_params=_params())(g_vec, g_pool_w, *flat)
    return [tuple(res[4 * p: 4 * p + 4]) for p in range(n)], res[4 * n]


def _place():
    x, y, c = lax.axis_index("x"), lax.axis_index("y"), lax.axis_index("c")
    other_chips = [(1 - x, y), (x, 1 - y), (1 - x, 1 - y)]
    return x, y, c, other_chips


def _rows(ref, r, place, natural=False):
    px, py, pc = place
    b = 4 * px + 2 * py + pc if natural else 4 * pc + 2 * px + py
    return ref.at[pl.ds(pl.multiple_of(b * r, 8), r), :]


def _gather_send_task(shards, natural=()):
    n = len(shards)
    rs = [s.shape[0] for s in shards]
    rows_of = lambda ref, k, place: _rows(ref, rs[k], place, k in natural)

    def copies(ins, outs, scr):
        send_sems, recv_sems, local_sems = scr
        x, y, c, chips = _place()
        me = (x, y, c)
        peers = [(x, y, 1 - c)] + [(*chip, c) for chip in chips]
        local = [pltpu.make_async_copy(ins[k], rows_of(outs[k], k, me), local_sems.at[k]) for k in range(n)]
        sends, recvs = [], []
        for s in (1, 2, 3, 0):
            for k in range(n):
                sems = dict(send_sem=send_sems.at[4 * k + s], recv_sem=recv_sems.at[4 * k + s], device_id_type=MESH)
                sends.append(pltpu.make_async_remote_copy(
                    src_ref=ins[k], dst_ref=rows_of(outs[k], k, me), device_id=peers[s], **sems))
                theirs = rows_of(outs[k], k, peers[s])
                recvs.append(pltpu.make_async_remote_copy(src_ref=theirs, dst_ref=theirs, device_id=me, **sems))
        return local, sends, recvs

    def start(ins, outs, scr):
        local, sends, _ = copies(ins, outs, scr)
        for cp in local + sends:
            cp.start()

    def finish(ins, outs, scr):
        local, sends, recvs = copies(ins, outs, scr)
        for cp in recvs:
            cp.wait_recv()
        for cp in sends:
            cp.wait_send()
        for cp in local:
            cp.wait()

    out_shapes = [jax.ShapeDtypeStruct((N_DEV * s.shape[0], s.shape[1]), s.dtype) for s in shards]
    scratch = [pltpu.SemaphoreType.DMA((4 * n,)), pltpu.SemaphoreType.DMA((4 * n,)), pltpu.SemaphoreType.DMA((n,))]
    return _Task(shards, out_shapes, scratch, start, finish)


def _gather_forward_task(fulls, natural=()):
    n = len(fulls)
    rs = [f.shape[0] // N_DEV for f in fulls]
    rows_of = lambda ref, k, place: _rows(ref, rs[k], place, k in natural)

    def copies(outs, scr):
        send_sems, recv_sems = scr
        x, y, c, chips = _place()
        sends, recvs = [], []
        for j, chip in enumerate(chips):
            for k in range(n):
                sems = dict(send_sem=send_sems.at[3 * k + j], recv_sem=recv_sems.at[3 * k + j], device_id_type=MESH)
                got = rows_of(outs[k], k, (*chip, c))
                sends.append(pltpu.make_async_remote_copy(src_ref=got, dst_ref=got, device_id=(x, y, 1 - c), **sems))
                theirs = rows_of(outs[k], k, (*chip, 1 - c))
                recvs.append(pltpu.make_async_remote_copy(src_ref=theirs, dst_ref=theirs, device_id=(x, y, c), **sems))
        return sends, recvs

    def start(ins, outs, scr):
        for cp in copies(outs, scr)[0]:
            cp.start()

    def finish(ins, outs, scr):
        sends, recvs = copies(outs, scr)
        for cp in recvs:
            cp.wait_recv()
        for cp in sends:
            cp.wait_send()

    out_shapes = [jax.ShapeDtypeStruct(f.shape, f.dtype) for f in fulls]
    scratch = [pltpu.SemaphoreType.DMA((3 * n,)), pltpu.SemaphoreType.DMA((3 * n,))]
    return _Task(fulls, out_shapes, scratch, start, finish, aliases={k: k for k in range(n)})


def _chip_task(sums):
    n = len(sums)
    rs = [s.shape[0] // 4 for s in sums]

    def block(ref, k, chip_index):
        return ref.at[pl.ds(pl.multiple_of(chip_index * rs[k], 8), rs[k]), :]

    def copies(ins, outs, scr):
        send_sems, recv_sems, local_sems = scr
        x, y, c, chips = _place()
        here = 2 * x + y
        local = [pltpu.make_async_copy(block(ins[k], k, here), outs[k].at[here], local_sems.at[k]) for k in range(n)]
        remote = []
        for j, (px, py) in enumerate(chips):
            remote += [pltpu.make_async_remote_copy(
                src_ref=block(ins[k], k, 2 * px + py), dst_ref=outs[k].at[here],
                send_sem=send_sems.at[3 * k + j], recv_sem=recv_sems.at[3 * k + j],
                device_id=(px, py, c), device_id_type=MESH) for k in range(n)]
        return local, remote

    def start(ins, outs, scr):
        local, remote = copies(ins, outs, scr)
        for cp in local + remote:
            cp.start()

    def finish(ins, outs, scr):
        local, remote = copies(ins, outs, scr)
        for cp in remote:
            cp.wait()
        for cp in local:
            cp.wait()

    out_shapes = [jax.ShapeDtypeStruct((4, r, s.shape[1]), s.dtype) for r, s in zip(rs, sums)]
    scratch = [pltpu.SemaphoreType.DMA((3 * n,)), pltpu.SemaphoreType.DMA((3 * n,)), pltpu.SemaphoreType.DMA((n,))]
    return _Task(sums, out_shapes, scratch, start, finish)


def _dw_pair(name, a, b, scale, comm=None, blocks=1):
    T, M = a.shape
    N = b.shape[1]
    half = M // 2
    wide = half // blocks
    tk = min(2048, T)
    nK = T // tk
    plumb = _CommPlumbing(comm)

    def body(core_ref, *rest):
        a_refs, b_ref, rest = rest[:blocks], rest[blocks], rest[blocks + 1:]
        c_in = rest[:plumb.n_in]
        o_ref = rest[plumb.n_in]
        c_out = rest[plumb.n_in + 1: plumb.n_in + 1 + plumb.n_out]
        acc, stage, land, send_sem, recv_sem = rest[plumb.n_in + 1 + plumb.n_out: plumb.n_in + 6 + plumb.n_out]
        c_scr = rest[plumb.n_in + 6 + plumb.n_out:]
        i, k = pl.program_id(0), pl.program_id(1)
        x, y, c, _ = _place()
        push = pltpu.make_async_remote_copy(src_ref=stage, dst_ref=land, send_sem=send_sem, recv_sem=recv_sem,
                                            device_id=(x, y, 1 - c), device_id_type=MESH)
        if comm:
            @pl.when((i == 0) & (k == 0))
            def _():
                plumb.start(c_in, c_out, c_scr)

        av = a_refs[0][...] if blocks == 1 else jnp.concatenate([r[...] for r in a_refs], axis=1)
        p = lax.dot_general(av, b_ref[...], _DIMS["tn"], preferred_element_type=F32)

        @pl.when(k == 0)
        def _():
            acc[...] = p

        @pl.when(k > 0)
        def _():
            acc[...] += p

        @pl.when((i == 0) & (k == nK - 1))
        def _():
            stage[...] = (scale * acc[...]).astype(BF)
            push.start()

        @pl.when((i == 1) & (k == nK - 1))
        def _():
            push.wait_recv()
            o_ref[...] = (scale * acc[...] + land[...].astype(F32)).astype(BF)
            push.wait_send()
            if comm:
                plumb.finish(c_in, c_out, c_scr)

    grid_spec = pltpu.PrefetchScalarGridSpec(
        num_scalar_prefetch=1, grid=(2, nK),
        in_specs=[pl.BlockSpec((tk, wide), functools.partial(
            lambda i, k, core, j: (k, (2 * j if blocks > 1 else 0) + jnp.where(i == 0, 1 - core[0], core[0])), j=j))
            for j in range(blocks)] + [pl.BlockSpec((tk, N), lambda i, k, core: (k, 0))] + [ANY] * plumb.n_in,
        out_specs=[pl.BlockSpec((half, N), lambda i, k, core: (0, 0))] + [ANY] * plumb.n_out,
        scratch_shapes=[pltpu.VMEM((half, N), F32), pltpu.VMEM((half, N), BF), pltpu.VMEM((half, N), BF),
                        pltpu.SemaphoreType.DMA, pltpu.SemaphoreType.DMA] + plumb.scratch)
    core = lax.axis_index("c").astype(jnp.int32).reshape(1)
    res = pl.pallas_call(
        body, name=name, grid_spec=grid_spec,
        out_shape=[jax.ShapeDtypeStruct((half, N), BF)] + plumb.out_shapes,
        compiler_params=_params(("arbitrary", "arbitrary")),
    )(core, *([a] * blocks), b, *plumb.args)
    return (res[0], plumb.split_outputs(res[1:])) if comm else res[0]


def _all_gather(name, shards):
    n = len(shards)

    def body(*refs):
        ins, outs = refs[:n], refs[n:2 * n]
        send_sems, recv_sems, local_sems = refs[2 * n:]
        x, y, c, chips = _place()
        me, sibling = (x, y, c), (x, y, 1 - c)

        def rows(k, px, py, pc):
            return _rows(outs[k], shards[k].shape[0], (px, py, pc))

        def copy(k, slot, block, to, src=None):
            return pltpu.make_async_remote_copy(
                src_ref=rows(k, *block) if src is None else src, dst_ref=rows(k, *block),
                send_sem=send_sems.at[7 * k + slot], recv_sem=recv_sems.at[7 * k + slot],
                device_id=to, device_id_type=MESH)

        mine = [pltpu.make_async_copy(ins[k], rows(k, *me), local_sems.at[k]) for k in range(n)]
        for cp in mine:
            cp.start()
        first = []
        for j, chip in enumerate(chips):
            first += [copy(k, 1 + j, me, (*chip, c), src=ins[k]) for k in range(n)]
        first += [copy(k, 0, me, sibling, src=ins[k]) for k in range(n)]
        for cp in first:
            cp.start()
        passed = []
        for j, chip in enumerate(chips):
            for k in range(n):
                copy(k, 1 + j, (*chip, c), me).wait_recv()
                fwd = copy(k, 4 + j, (*chip, c), sibling)
                fwd.start()
                passed.append(fwd)
        for k in range(n):
            copy(k, 0, sibling, me).wait_recv()
        for j, chip in enumerate(chips):
            for k in range(n):
                copy(k, 4 + j, (*chip, 1 - c), me).wait_recv()
        for cp in first + passed:
            cp.wait_send()
        for cp in mine:
            cp.wait()

    return pl.pallas_call(
        body, name=name, in_specs=[ANY] * n, out_specs=[ANY] * n,
        out_shape=[jax.ShapeDtypeStruct((N_DEV * s.shape[0], s.shape[1]), s.dtype) for s in shards],
        scratch_shapes=[pltpu.SemaphoreType.DMA((7 * n,)), pltpu.SemaphoreType.DMA((7 * n,)),
                        pltpu.SemaphoreType.DMA((n,))],
        compiler_params=pltpu.CompilerParams(has_side_effects=True),
    )(*shards)


def _pair_exchange(name, parts):
    n = len(parts)

    def body(*refs):
        ins, outs = refs[:n], refs[n:2 * n]
        send_sems, recv_sems = refs[2 * n:]
        x, y, c, _ = _place()
        copies = [pltpu.make_async_remote_copy(
            src_ref=ins[k].at[:, pl.ds(1 - c, 1)], dst_ref=outs[k], send_sem=send_sems.at[k], recv_sem=recv_sems.at[k],
            device_id=(x, y, 1 - c), device_id_type=MESH) for k in range(n)]
        for cp in copies:
            cp.start()
        for cp in copies:
            cp.wait()

    return pl.pallas_call(
        body, name=name, in_specs=[ANY] * n, out_specs=[ANY] * n,
        out_shape=[jax.ShapeDtypeStruct((4, 1) + p.shape[2:], p.dtype) for p in parts],
        scratch_shapes=[pltpu.SemaphoreType.DMA((n,)), pltpu.SemaphoreType.DMA((n,))],
        compiler_params=pltpu.CompilerParams(has_side_effects=True),
    )(*parts)


def _pair_sum(name, part, got, core):
    _, _, r, C = part.shape

    def body(core_ref, p_ref, g_ref, o_ref):
        o_ref[0] = (p_ref[0, 0].astype(F32) + g_ref[0, 0].astype(F32)).astype(o_ref.dtype)

    return pl.pallas_call(
        body, name=name,
        grid_spec=pltpu.PrefetchScalarGridSpec(
            num_scalar_prefetch=1, grid=(4,),
            in_specs=[pl.BlockSpec((1, 1, r, C), lambda i, core_ref: (i, core_ref[0], 0, 0)),
                      pl.BlockSpec((1, 1, r, C), lambda i, core_ref: (i, 0, 0, 0))],
            out_specs=pl.BlockSpec((1, r, C), lambda i, core_ref: (i, 0, 0))),
        out_shape=jax.ShapeDtypeStruct((4, r, C), part.dtype), compiler_params=_params(("parallel",)),
    )(core, part, got)


def _ffn_bwd(tag, dy, dyb, x, gain, wgT, wuT, wd, saved, pending):
    n, g, u = saved
    half = lambda accs, ex: _swiglu_bwd_epilogue([0.5 * accs[0]], ex)
    (dg, du, a), done0 = _mm(tag + "_d_act", [(dyb, wd, "nt", 0)], [BF, BF, BF], tm=512, tn=1408, tk=D_MODEL,
                             epilogue=half, extras=[(g, "tile", 0), (u, "tile", 0)], comm=pending, cols_outer=True)
    sum_d = _dw_pair(tag + "_dw_down", a, dyb, 0.5)
    sum_g, (slots_d,) = _dw_pair(tag + "_dw_gate", dg, n, 1.0, comm=[_chip_task([sum_d])])
    sum_u, (slots_g,) = _dw_pair(tag + "_dw_up", du, n, 1.0, comm=[_chip_task([sum_g])])
    (dx, dxb, dgain), (slots_u,) = _mm(
        tag + "_d_norm", [(dg, wgT, "nn", 0), (du, wuT, "nn", 0)], [F32, BF], tm=512, tn=D_MODEL, tk=D_FF,
        epilogue=_rms_bwd_epilogue, extras=[(x, "tile", 0), (gain, "row", 0), (dy, "tile", 0)], n_colsum=1,
        comm=[_chip_task([sum_u])])
    return dx, dxb, dgain, done0, slots_g[0], slots_u[0], slots_d[0]


def _tile_gain(g):
    return jnp.concatenate([g, g]).reshape(1, LANES)


def _fold_heads(partials):
    return jnp.sum(partials.reshape(-1, HEAD_DIM), axis=0)


def _pack_small_grads(grads, loss_local):
    pieces, row = [], 0
    for name, r0, _ in SMALL_LAYOUT + (("loss", LOSS_ROW, None),):
        v = (loss_local if name == "loss" else grads[name]).reshape(-1)
        rows = -(-v.size // LANES)
        block = jnp.pad(v, (0, rows * LANES - v.size)).reshape(rows, LANES)
        pieces += [jnp.zeros((r0 - row, LANES), F32)] * (r0 > row) + [block]
        row = r0 + rows
    pieces.append(jnp.zeros((SMALL_ROWS - row, LANES), F32))
    return jnp.concatenate(pieces, axis=0)


def kernel(x, ffn1_norm, ffn1_w_gate, ffn1_w_up, ffn1_w_down, mix_norm, w_in, pool_w, pool_scale, w_pool_out, q_norm, k_norm, sinks, w_attn_out, gate_bias, w_out, ffn2_norm, ffn2_w_gate, ffn2_w_up, ffn2_w_down, loss_target, m_ffn1_norm, m_ffn1_w_gate, m_ffn1_w_up, m_ffn1_w_down, m_mix_norm, m_w_in, m_pool_w, m_pool_scale, m_w_pool_out, m_q_norm, m_k_norm, m_sinks, m_w_attn_out, m_gate_bias, m_w_out, m_ffn2_norm, m_ffn2_w_gate, m_ffn2_w_up, m_ffn2_w_down, v_ffn1_norm, v_ffn1_w_gate, v_ffn1_w_up, v_ffn1_w_down, v_mix_norm, v_w_in, v_pool_w, v_pool_scale, v_w_pool_out, v_q_norm, v_k_norm, v_sinks, v_w_attn_out, v_gate_bias, v_w_out, v_ffn2_norm, v_ffn2_w_gate, v_ffn2_w_up, v_ffn2_w_down):
    T = x.shape[1]
    x2 = x.reshape(T, D_MODEL)
    target = loss_target.reshape(T, D_MODEL)

    big = [
        ("ffn1_w_gate", ffn1_w_gate, m_ffn1_w_gate, v_ffn1_w_gate, True, False),
        ("ffn1_w_up", ffn1_w_up, m_ffn1_w_up, v_ffn1_w_up, True, False),
        ("ffn1_w_down", ffn1_w_down, m_ffn1_w_down, v_ffn1_w_down, False, False),
        ("w_in", w_in, m_w_in, v_w_in, True, False),
        ("w_pool_out", w_pool_out, m_w_pool_out, v_w_pool_out, False, True),
        ("w_attn_out", w_attn_out, m_w_attn_out, v_w_attn_out, False, False),
        ("w_out", w_out, m_w_out, v_w_out, False, False),
        ("ffn2_w_gate", ffn2_w_gate, m_ffn2_w_gate, v_ffn2_w_gate, True, False),
        ("ffn2_w_up", ffn2_w_up, m_ffn2_w_up, v_ffn2_w_up, True, False),
        ("ffn2_w_down", ffn2_w_down, m_ffn2_w_down, v_ffn2_w_down, False, False),
    ]
    view = lambda a, tv: a.T if tv else a
    shards = _prep("prep_weights", [view(w, tv) for _, w, _, _, tv, _ in big], [tk_ for *_, tk_ in big])
    mixer_natural = (0, 1, 2, 3)
    wg1T, wu1T, wd1 = _all_gather("gather_ffn1", shards[0:3])

    g1 = ffn1_norm.reshape(1, D_MODEL)
    g2 = mix_norm.reshape(1, D_MODEL)
    g3 = ffn2_norm.reshape(1, D_MODEL)
    bias_row = gate_bias.reshape(1, 2 * D_MODEL)
    qg, kg = _tile_gain(q_norm), _tile_gain(k_norm)
    scale_row = pool_scale.reshape(1, POOL_WIDTH)

    n1 = _rms_fwd("ffn1_norm", x2, g1)
    (gt1, up1, act1), (mixer_part,) = _mm(
        "ffn1_gate_up", [(n1, wg1T, "nt", 0), (n1, wu1T, "nt", 1)], [BF, BF, BF], tm=512, tn=1408, tk=D_MODEL,
        epilogue=_swiglu_fwd_epilogue, comm=[_gather_send_task(shards[3:7], mixer_natural)], cols_outer=True)
    (h1,), (mixer_full, wg2_part) = _mm(
        "ffn1_down", [(act1, wd1, "nn", 0)], [F32], tm=512, tn=D_MODEL, tk=D_FF, epilogue=_half_residual_epilogue,
        extras=[(x2, "tile", 0)], comm=[_gather_forward_task(mixer_part, mixer_natural), _gather_send_task(shards[7:8])])
    w_inT, w_poT, w_ao, w_o = mixer_full
    saved1 = (n1, gt1, up1)
    u = _rms_fwd("mix_norm", h1, g2)
    (proj,), ((wg2T,), ud2_part) = _mm(
        "in_proj", [(u, w_inT, "nt", 0)], [BF], tm=512, tn=1280, tk=D_MODEL,
        comm=[_gather_forward_task(wg2_part), _gather_send_task(shards[8:10])], cols_outer=True)
    pooled, mixed = _pool_fwd("pool_fwd", proj, pool_w, scale_row)
    qn = _headnorm_fwd("q_norm", proj, COL_Q, ATTN_WIDTH, qg)
    kn = _headnorm_fwd("k_norm", proj, COL_K, KV_WIDTH, kg)
    attn = _attn_fwd("attn_fwd", qn, kn, proj, sinks)
    (bp,) = _mm("pool_out", [(mixed, w_poT, "nt", 0)], [BF], tm=1024, tn=D_MODEL, tk=POOL_WIDTH)
    gate_tn = 256
    gate_extras = [(proj, "tile", COL_GP // gate_tn), (proj, "tile", COL_GA // gate_tn),
                   (bias_row, "row", 0), (bias_row, "row", D_MODEL // gate_tn)]
    (merged, ba), ((wu2T, wd2),) = _mm(
        "attn_out_merge", [(attn, w_ao, "nn", 0)], [BF, BF], tm=2048, tn=gate_tn, tk=ATTN_WIDTH,
        epilogue=_merge_fwd_epilogue, extras=[(bp, "tile", 0)] + gate_extras, comm=[_gather_forward_task(ud2_part)])
    (h2,) = _mm("mix_out", [(merged, w_o, "nn", 0)], [F32], tm=512, tn=D_MODEL, tk=D_MODEL,
                epilogue=_residual_epilogue, extras=[(h1, "tile", 0)])
    n2 = _rms_fwd("ffn2_norm", h2, g3)
    gt2, up2, act2 = _mm("ffn2_gate_up", [(n2, wg2T, "nt", 0), (n2, wu2T, "nt", 1)], [BF, BF, BF],
                         tm=512, tn=1408, tk=D_MODEL, epilogue=_swiglu_fwd_epilogue, cols_outer=True)
    dy, dyb, sq = _mm("ffn2_down_loss", [(act2, wd2, "nn", 0)], [F32, BF], tm=512, tn=D_MODEL, tk=D_FF,
                      epilogue=_loss_epilogue, extras=[(h2, "tile", 0), (target, "tile", 0)], n_colsum=1)
    loss_local = 0.5 * jnp.sum(sq) / D_MODEL

    dh2, dh2b, dg3, _, slots_g2, slots_u2, slots_d2 = _ffn_bwd(
        "ffn2", dy, dyb, h2, g3, wg2T, wu2T, wd2, (n2, gt2, up2), [])
    dbp, dba, dgp, dga, cs_gp, cs_ga = _mm(
        "mix_out_bwd", [(dh2b, w_o, "nt", 0)], [BF, BF, BF, BF], tm=2048, tn=gate_tn, tk=D_MODEL,
        epilogue=_merge_bwd_epilogue, extras=[(bp, "tile", 0), (ba, "tile", 0)] + gate_extras, n_colsum=2)
    sum_o = _dw_pair("dw_out", merged, dh2b, 1.0, blocks=4)
    (dmixed,), ((slots_o,),) = _mm("pool_out_bwd", [(dbp, w_poT, "nn", 0)], [BF], tm=1024, tn=POOL_WIDTH, tk=D_MODEL,
                                   comm=[_chip_task([sum_o])])
    sum_po = _dw_pair("dw_pool_out", dbp, mixed, 1.0, blocks=4)
    (dattn,), ((slots_po,),) = _mm("attn_out_bwd", [(dba, w_ao, "nt", 0)], [BF], tm=1024, tn=ATTN_WIDTH, tk=D_MODEL,
                                   comm=[_chip_task([sum_po])])
    sum_ao = _dw_pair("dw_attn_out", attn, dba, 1.0, blocks=4)
    dxp, dpool_w, dpool_scale = _pool_bwd("pool_bwd", dmixed, pooled, pool_w, scale_row)
    dqn, dkn, dv, dsink_tile = _attn_bwd("attn_bwd", dattn, qn, kn, proj, sinks)
    dq, dqg = _headnorm_bwd("q_norm_bwd", dqn, proj, COL_Q, ATTN_WIDTH, qg)
    dk, dkg = _headnorm_bwd("k_norm_bwd", dkn, proj, COL_K, KV_WIDTH, kg)
    dproj = jnp.concatenate([dxp, dq, dk, dv, dgp, dga], axis=1)
    (dh1, dh1b, dg2), ((slots_ao,),) = _mm(
        "in_proj_bwd", [(dproj, w_inT, "nn", 0)], [F32, BF], tm=512, tn=D_MODEL, tk=IN_WIDTH, epilogue=_rms_bwd_epilogue,
        extras=[(h1, "tile", 0), (g2, "row", 0), (dh2, "tile", 0)], n_colsum=1, comm=[_chip_task([sum_ao])])
    (dw_inT,) = _mm("dw_in", [(dproj, u, "tn", 0)], [BF], tm=1280, tn=D_MODEL, tk=2048)
    part_in = dw_inT.reshape(4, 2, IN_WIDTH // N_DEV, D_MODEL)
    (got_in,) = _pair_exchange("pair_exchange_w_in", [part_in])
    core = lax.axis_index("c").astype(jnp.int32).reshape(1)
    sum_in = _pair_sum("pair_sum_w_in", part_in, got_in, core).reshape(IN_WIDTH // 2, D_MODEL)
    dx, _, dg1, ((slots_in,),), slots_g1, slots_u1, slots_d1 = _ffn_bwd(
        "ffn1", dh1, dh1b, x2, g1, wg1T, wu1T, wd1, saved1, [_chip_task([sum_in])])

    slots = [slots_g1, slots_u1, slots_d1, slots_in, slots_po, slots_ao, slots_o, slots_g2, slots_u2, slots_d2]
    big_out = {}
    for k, (nm, w, m, v, tv, tk_) in enumerate(big):
        res = _adamw_sharded("adamw_" + nm, slots[k], view(w, tv), view(m, tv), view(v, tv), tk_)
        big_out[nm] = tuple(view(r, tv) for r in res)

    small_grads = {
        "ffn1_norm": jnp.sum(dg1, axis=(0, 1)), "mix_norm": jnp.sum(dg2, axis=(0, 1)), "ffn2_norm": jnp.sum(dg3, axis=(0, 1)),
        "gate_bias": jnp.concatenate([jnp.sum(cs_gp, axis=(0, 1)), jnp.sum(cs_ga, axis=(0, 1))]),
        "pool_scale": dpool_scale, "q_norm": _fold_heads(dqg), "k_norm": _fold_heads(dkg),
        "sinks": dsink_tile[0, :N_HEADS]}
    g_vec, g_pool_w = _all_gather("gather_small_grads", [_pack_small_grads(small_grads, loss_local),
                                                         dpool_w.reshape(-1, LANES)])
    given = {"ffn1_norm": (ffn1_norm, m_ffn1_norm, v_ffn1_norm), "mix_norm": (mix_norm, m_mix_norm, v_mix_norm),
             "ffn2_norm": (ffn2_norm, m_ffn2_norm, v_ffn2_norm), "gate_bias": (gate_bias, m_gate_bias, v_gate_bias),
             "pool_scale": (pool_scale, m_pool_scale, v_pool_scale), "q_norm": (q_norm, m_q_norm, v_q_norm),
             "k_norm": (k_norm, m_k_norm, v_k_norm), "sinks": (sinks, m_sinks, v_sinks)}
    params = [tuple(a.reshape(shape) for a in given[nm]) for nm, _, shape in SMALL_LAYOUT]
    params.append(tuple(a.reshape(-1, LANES) for a in (pool_w, m_pool_w, v_pool_w)))
    small_res, loss_row = _adamw_small("adamw_small", g_vec.reshape(N_DEV, SMALL_ROWS, LANES),
                                       g_pool_w.reshape(N_DEV, -1, LANES), params)
    small_out = {nm: tuple(r.reshape(given[nm][0].shape) for r in res)
                 for (nm, _, _), res in zip(SMALL_LAYOUT, small_res)}
    small_out["pool_w"] = tuple(r.reshape(pool_w.shape) for r in small_res[-1])
    loss = loss_row[0, 0]

    order = ["ffn1_norm", "ffn1_w_gate", "ffn1_w_up", "ffn1_w_down", "mix_norm", "w_in", "pool_w", "pool_scale",
             "w_pool_out", "q_norm", "k_norm", "sinks", "w_attn_out", "gate_bias", "w_out", "ffn2_norm",
             "ffn2_w_gate", "ffn2_w_up", "ffn2_w_down"]
    every = {**big_out, **small_out}
    outs = [loss, dx.reshape(x.shape)]
    for j in range(4):
        outs += [every[nm][j] for nm in order]
    return tuple(outs)
```

```python
import functools

import jax
import jax.numpy as jnp
from jax import lax
from jax.experimental import pallas as pl
from jax.experimental.pallas import tpu as pltpu

BF = jnp.bfloat16
F32 = jnp.float32

D_MODEL = 1024
D_FF = 2816
POOL_WIDTH = 512
POOL_GROUP = 128
N_POOL_GROUPS = 4
HEAD_DIM = 64
N_HEADS = 16
GQA_GROUP = 8
BLOCK = 128
ATTN_WIDTH = 1024
KV_WIDTH = 128
IN_WIDTH = 3840
RMS_EPS = 1e-6
N_DEV = 8
LANES = 128

COL_Q = POOL_WIDTH
COL_K = COL_Q + ATTN_WIDTH
COL_V = COL_K + KV_WIDTH
COL_GP = COL_V + KV_WIDTH
COL_GA = COL_GP + D_MODEL

ADAM_LR = 0.001
ADAM_B1 = 0.9
ADAM_B2 = 0.999
ADAM_EPS = 1e-08
ADAM_WD = 0.01
ADAM_STEP = 10

VMEM_LIMIT_V7X = 56 * 1024 * 1024
MESH = pl.DeviceIdType.MESH
ANY = pl.BlockSpec(memory_space=pl.ANY)


def _params(sem=None):
    return pltpu.CompilerParams(dimension_semantics=sem, vmem_limit_bytes=VMEM_LIMIT_V7X)


_DIMS = {"nt": (((1,), (1,)), ((), ())), "nn": (((1,), (0,)), ((), ())), "tn": (((0,), (0,)), ((), ()))}


class _Task:
    def __init__(self, inputs, out_shapes, scratch, phases):
        self.inputs, self.out_shapes, self.scratch = list(inputs), list(out_shapes), list(scratch)
        self.phases = list(phases)


class _CommPlumbing:
    def __init__(self, tasks):
        self.tasks = list(tasks or [])
        self.args = [a for t in self.tasks for a in t.inputs]
        self.out_shapes = [o for t in self.tasks for o in t.out_shapes]
        self.scratch = [s for t in self.tasks for s in t.scratch]
        self.n_in, self.n_out = len(self.args), len(self.out_shapes)

    def _slices(self, c_in, c_out, c_scr):
        i = o = s = 0
        for t in self.tasks:
            yield t, c_in[i:i + len(t.inputs)], c_out[o:o + len(t.out_shapes)], c_scr[s:s + len(t.scratch)]
            i, o, s = i + len(t.inputs), o + len(t.out_shapes), s + len(t.scratch)

    def run(self, step, steps, before, c_in, c_out, c_scr):
        for t, ins, outs, scr in self._slices(c_in, c_out, c_scr):
            for frac, fn in t.phases:
                if step is None:
                    fn(ins, outs, scr)
                elif before == (frac == 0):
                    at = 0 if frac == 0 else max(0, min(steps, -(-int(round(frac * steps * 64)) // 64)) - 1)
                    pl.when(step == at)(functools.partial(fn, ins, outs, scr))

    def split_outputs(self, flat):
        res, o = [], 0
        for t in self.tasks:
            res.append(list(flat[o:o + len(t.out_shapes)]))
            o += len(t.out_shapes)
        return res


def _comm_only(name, tasks):
    plumb = _CommPlumbing(tasks)

    def body(*refs):
        c_in, c_out = refs[:plumb.n_in], refs[plumb.n_in: plumb.n_in + plumb.n_out]
        c_scr = refs[plumb.n_in + plumb.n_out:]
        plumb.run(None, 1, True, c_in, c_out, c_scr)

    res = pl.pallas_call(
        body, name=name, in_specs=[ANY] * plumb.n_in, out_specs=[ANY] * plumb.n_out, out_shape=plumb.out_shapes,
        scratch_shapes=plumb.scratch, compiler_params=pltpu.CompilerParams(has_side_effects=True),
    )(*plumb.args)
    return plumb.split_outputs(res)


def _mm(name, terms, out_dtypes, *, tm, tn, tk, epilogue=None, extras=(), n_colsum=0, comm=None, cols_outer=False):
    a0, b0, mode0, _ = terms[0]
    if mode0 == "nt":
        (M, K), N = a0.shape, b0.shape[0]
    elif mode0 == "nn":
        (M, K), N = a0.shape, b0.shape[1]
    else:
        (K, M), N = a0.shape, b0.shape[1]
    tm, tn, tk = min(tm, M), min(tn, N), min(tk, K)
    assert M % tm == 0 and N % tn == 0 and K % tk == 0, (name, M, N, K, tm, tn, tk)
    nI, nJ, nK = M // tm, N // tn, K // tk
    n_terms = len(terms)
    n_acc = max(t[3] for t in terms) + 1
    n_ex = len(extras)
    n_out = len(out_dtypes)
    if epilogue is None:
        epilogue = lambda accs, ex: ([accs[0]], [])
    plumb = _CommPlumbing(comm)
    n_scr = n_acc if nK > 1 else 0
    grid = (nJ, nI, nK) if cols_outer else (nI, nJ, nK)

    def body(*refs):
        n_in = 2 * n_terms + n_ex
        ab = refs[: 2 * n_terms]
        ex_refs = refs[2 * n_terms: n_in]
        c_in = refs[n_in: n_in + plumb.n_in]
        o0 = n_in + plumb.n_in
        out_refs = refs[o0: o0 + n_out]
        cs_refs = refs[o0 + n_out: o0 + n_out + n_colsum]
        c_out = refs[o0 + n_out + n_colsum: o0 + n_out + n_colsum + plumb.n_out]
        s0 = o0 + n_out + n_colsum + plumb.n_out
        acc_refs = refs[s0: s0 + n_scr]
        c_scr = refs[s0 + n_scr:]
        steps = grid[0] * grid[1] * nK
        if comm:
            step = (pl.program_id(0) * grid[1] + pl.program_id(1)) * nK + pl.program_id(2)
            plumb.run(step, steps, True, c_in, c_out, c_scr)

        def products():
            accs = [None] * n_acc
            for t, (_, _, mode, ai) in enumerate(terms):
                p = lax.dot_general(ab[2 * t][...], ab[2 * t + 1][...], _DIMS[mode], preferred_element_type=F32)
                accs[ai] = p if accs[ai] is None else accs[ai] + p
            return accs

        def finish(accs):
            outs, colsums = epilogue(accs, [r[...] for r in ex_refs])
            for r, o in zip(out_refs, outs):
                r[...] = o.astype(r.dtype)
            for r, cs in zip(cs_refs, colsums):
                r[...] = jnp.sum(cs, axis=0, keepdims=True).reshape(r.shape)

        if nK == 1:
            finish(products())
        else:
            k = pl.program_id(2)
            accs = products()

            @pl.when(k == 0)
            def _():
                for r, a in zip(acc_refs, accs):
                    r[...] = a

            @pl.when(k > 0)
            def _():
                for r, a in zip(acc_refs, accs):
                    r[...] += a

            @pl.when(k == nK - 1)
            def _():
                finish([r[...] for r in acc_refs])

        if comm:
            plumb.run(step, steps, False, c_in, c_out, c_scr)

    def spec(block, index, fixed=False):
        imap = (lambda q, p, k: index(p, q, k)) if cols_outer else index
        return pl.BlockSpec(block, imap, pipeline_mode=pl.Buffered(1)) if fixed else pl.BlockSpec(block, imap)

    in_specs, args = [], []
    for a, b, mode, _ in terms:
        if mode == "nt":
            in_specs += [spec((tm, tk), lambda i, j, k: (i, k), nI * nK == 1),
                         spec((tn, tk), lambda i, j, k: (j, k), nJ * nK == 1)]
        elif mode == "nn":
            in_specs += [spec((tm, tk), lambda i, j, k: (i, k), nI * nK == 1),
                         spec((tk, tn), lambda i, j, k: (k, j), nJ * nK == 1)]
        else:
            in_specs += [spec((tk, tm), lambda i, j, k: (k, i), nI * nK == 1),
                         spec((tk, tn), lambda i, j, k: (k, j), nJ * nK == 1)]
        args += [a, b]
    for arr, kind, off in extras:
        if kind == "tile":
            in_specs.append(spec((tm, tn), functools.partial(lambda i, j, k, off: (i, j + off), off=off)))
        else:
            in_specs.append(spec((1, tn), functools.partial(lambda i, j, k, off: (0, j + off), off=off)))
        args.append(arr)
    out_shape = [jax.ShapeDtypeStruct((M, N), dt) for dt in out_dtypes]
    out_specs = [spec((tm, tn), lambda i, j, k: (i, j)) for _ in out_dtypes]
    out_shape += [jax.ShapeDtypeStruct((nI, 1, N), F32) for _ in range(n_colsum)]
    out_specs += [spec((1, 1, tn), lambda i, j, k: (i, 0, j)) for _ in range(n_colsum)]
    scratch = [pltpu.VMEM((tm, tn), F32) for _ in range(n_scr)]
    args += plumb.args
    in_specs += [ANY] * plumb.n_in
    out_shape += plumb.out_shapes
    out_specs += [ANY] * plumb.n_out
    sem = ("arbitrary",) * 3 if comm else ("parallel", "parallel", "arbitrary")
    res = pl.pallas_call(
        body, name=name, grid=grid, in_specs=in_specs, out_specs=out_specs, out_shape=out_shape,
        scratch_shapes=scratch + plumb.scratch, compiler_params=_params(sem),
    )(*args)
    n_own = n_out + n_colsum
    return (list(res[:n_own]), plumb.split_outputs(res[n_own:])) if comm is not None else res


ROW_TILE = 512


def _rms_fwd(name, x, g):
    T, D = x.shape

    def body(x_ref, g_ref, o_ref):
        xv = x_ref[...]
        r = lax.rsqrt(jnp.mean(xv * xv, axis=-1, keepdims=True) + RMS_EPS)
        o_ref[...] = (xv * r * g_ref[...]).astype(BF)

    return pl.pallas_call(
        body, name=name, grid=(T // ROW_TILE,),
        in_specs=[pl.BlockSpec((ROW_TILE, D), lambda i: (i, 0)), pl.BlockSpec((1, D), lambda i: (0, 0))],
        out_specs=pl.BlockSpec((ROW_TILE, D), lambda i: (i, 0)),
        out_shape=jax.ShapeDtypeStruct((T, D), BF), compiler_params=_params(("parallel",)),
    )(x, g)


HEADNORM_TILE = 1024


def _half_sum_matrix():
    r = lax.broadcasted_iota(jnp.int32, (LANES, LANES), 0) // HEAD_DIM
    c = lax.broadcasted_iota(jnp.int32, (LANES, LANES), 1) // HEAD_DIM
    return (r == c).astype(BF)


def _head_mean(v, ones_blockdiag):
    hi = v.astype(BF)
    lo = (v - hi.astype(F32)).astype(BF)
    s = jnp.dot(hi, ones_blockdiag, preferred_element_type=F32) + jnp.dot(lo, ones_blockdiag, preferred_element_type=F32)
    return s * (1.0 / HEAD_DIM)


def _headnorm_fwd(name, proj, col0, width, g2):
    T = proj.shape[0]
    wide = min(width, GROUP_WIDTH)
    nb, off = width // wide, col0 // wide

    def body(x_ref, g_ref, b_ref, o_ref):
        for s in range(wide // LANES):
            lanes = slice(LANES * s, LANES * (s + 1))
            xv = x_ref[:, lanes].astype(F32)
            r = lax.rsqrt(_head_mean(xv * xv, b_ref[...]) + RMS_EPS)
            o_ref[:, lanes] = (xv * r * g_ref[...]).astype(BF)

    return pl.pallas_call(
        body, name=name, grid=(T // HEADNORM_TILE, nb),
        in_specs=[pl.BlockSpec((HEADNORM_TILE, wide), lambda i, j: (i, j + off)),
                  pl.BlockSpec((1, LANES), lambda i, j: (0, 0)), pl.BlockSpec((LANES, LANES), lambda i, j: (0, 0))],
        out_specs=pl.BlockSpec((HEADNORM_TILE, wide), lambda i, j: (i, j)),
        out_shape=jax.ShapeDtypeStruct((T, width), BF), compiler_params=_params(("parallel", "parallel")),
    )(proj, g2, _half_sum_matrix())


def _headnorm_bwd(name, dy, proj, col0, width, g2):
    T = proj.shape[0]
    wide = min(width, GROUP_WIDTH)
    nb, off = width // wide, col0 // wide

    def body(dy_ref, x_ref, g_ref, b_ref, dx_ref, dg_ref):
        for s in range(wide // LANES):
            lanes = slice(LANES * s, LANES * (s + 1))
            xv = x_ref[:, lanes].astype(F32)
            dyv = dy_ref[:, lanes].astype(F32)
            r = lax.rsqrt(_head_mean(xv * xv, b_ref[...]) + RMS_EPS)
            xhat = xv * r
            dxhat = dyv * g_ref[...]
            dx_ref[:, lanes] = (r * (dxhat - xhat * _head_mean(dxhat * xhat, b_ref[...]))).astype(BF)
            dg_ref[0, :, lanes] = jnp.sum(dyv * xhat, axis=0, keepdims=True)

    return pl.pallas_call(
        body, name=name, grid=(T // HEADNORM_TILE, nb),
        in_specs=[pl.BlockSpec((HEADNORM_TILE, wide), lambda i, j: (i, j)),
                  pl.BlockSpec((HEADNORM_TILE, wide), lambda i, j: (i, j + off)),
                  pl.BlockSpec((1, LANES), lambda i, j: (0, 0)), pl.BlockSpec((LANES, LANES), lambda i, j: (0, 0))],
        out_specs=[pl.BlockSpec((HEADNORM_TILE, wide), lambda i, j: (i, j)),
                   pl.BlockSpec((1, 1, wide), lambda i, j: (i, 0, j))],
        out_shape=[jax.ShapeDtypeStruct((T, width), BF), jax.ShapeDtypeStruct((T // HEADNORM_TILE, 1, width), F32)],
        compiler_params=_params(("parallel", "parallel")),
    )(dy, proj, g2, _half_sum_matrix())


def _shift_down(v, k, row):
    return jnp.where(row >= k, pltpu.roll(v, k, axis=0), 0.0)


def _shift_up(v, k, row, T):
    return jnp.where(row < T - k, pltpu.roll(v, T - k, axis=0), 0.0)


def _by_group(g, vals):
    out = vals[-1]
    for i in range(len(vals) - 2, -1, -1):
        out = jnp.where(g == i, vals[i], out)
    return out


def _pool_fwd(name, proj, pool_w, pool_scale):
    T = proj.shape[0]

    def body(x_ref, w_ref, s_ref, pooled_ref, mixed_ref):
        g = pl.program_id(0)
        xv = x_ref[...].astype(F32)
        row = lax.broadcasted_iota(jnp.int32, (T, 1), 0)
        s2 = xv + _shift_down(xv, 1, row)
        s4 = s2 + _shift_down(s2, 2, row)
        s8 = s4 + _shift_down(s4, 4, row)
        s16 = s8 + _shift_down(s8, 8, row)
        wsum = _by_group(g, [s2, s4, s8, s16])
        count = jnp.minimum(row + 1, 2 << g).astype(F32)
        pooled = (wsum / count - xv).astype(BF)
        pooled_ref[...] = pooled
        mixed = jnp.dot(pooled, w_ref[0].astype(BF), preferred_element_type=F32) * s_ref[...]
        mixed_ref[...] = mixed.astype(BF)

    col = pl.BlockSpec((T, POOL_GROUP), lambda g: (0, g))
    return pl.pallas_call(
        body, name=name, grid=(N_POOL_GROUPS,),
        in_specs=[col, pl.BlockSpec((1, POOL_GROUP, POOL_GROUP), lambda g: (g, 0, 0)),
                  pl.BlockSpec((1, POOL_GROUP), lambda g: (0, g))],
        out_specs=[col, col],
        out_shape=[jax.ShapeDtypeStruct((T, POOL_WIDTH), BF), jax.ShapeDtypeStruct((T, POOL_WIDTH), BF)],
        compiler_params=_params(("parallel",)),
    )(proj, pool_w, pool_scale)


def _pool_bwd(name, dmixed, pooled, pool_w, pool_scale):
    T = dmixed.shape[0]

    def body(dm_ref, p_ref, w_ref, s_ref, dx_ref, dw_ref, ds_ref):
        g = pl.program_id(0)
        dm = dm_ref[...].astype(F32)
        pooled = p_ref[...]
        w = w_ref[0].astype(BF)
        pre = jnp.dot(pooled, w, preferred_element_type=F32)
        ds_ref[...] = jnp.sum(dm * pre, axis=0, keepdims=True)
        dms = (dm * s_ref[...]).astype(BF)
        dw_ref[0] = lax.dot_general(pooled, dms, _DIMS["tn"], preferred_element_type=F32)
        dpooled = lax.dot_general(dms, w, _DIMS["nt"], preferred_element_type=F32)
        row = lax.broadcasted_iota(jnp.int32, (T, 1), 0)
        count = jnp.minimum(row + 1, 2 << g).astype(F32)
        z = dpooled / count
        l2 = z + _shift_up(z, 1, row, T)
        l4 = l2 + _shift_up(l2, 2, row, T)
        l8 = l4 + _shift_up(l4, 4, row, T)
        l16 = l8 + _shift_up(l8, 8, row, T)
        dx_ref[...] = (_by_group(g, [l2, l4, l8, l16]) - dpooled).astype(BF)

    col = pl.BlockSpec((T, POOL_GROUP), lambda g: (0, g))
    wspec = pl.BlockSpec((1, POOL_GROUP, POOL_GROUP), lambda g: (g, 0, 0))
    sspec = pl.BlockSpec((1, POOL_GROUP), lambda g: (0, g))
    return pl.pallas_call(
        body, name=name, grid=(N_POOL_GROUPS,), in_specs=[col, col, wspec, sspec], out_specs=[col, wspec, sspec],
        out_shape=[jax.ShapeDtypeStruct((T, POOL_WIDTH), BF),
                   jax.ShapeDtypeStruct((N_POOL_GROUPS, POOL_GROUP, POOL_GROUP), F32),
                   jax.ShapeDtypeStruct((1, POOL_WIDTH), F32)],
        compiler_params=_params(("parallel",)),
    )(dmixed, pooled, pool_w, pool_scale)


ATTN_SCALE = HEAD_DIM ** -0.5
MASKED = float(jnp.finfo(jnp.float32).min)
KV_COL_BLOCK_K = COL_K // LANES
KV_COL_BLOCK_V = COL_V // LANES
GROUP_WIDTH = GQA_GROUP * HEAD_DIM


def _dup_head(v, j):
    half = lax.broadcasted_iota(jnp.int32, (1, LANES), 1) // HEAD_DIM
    return jnp.where(half == j, v, pltpu.roll(v, HEAD_DIM, axis=1))


def _stack_heads(v, low):
    pieces = []
    for p in range(GROUP_WIDTH // LANES):
        vp = v[:, LANES * p: LANES * (p + 1)]
        pieces.append(jnp.where(low, vp, jnp.zeros_like(vp)))
        pieces.append(jnp.where(low, jnp.zeros_like(vp), vp))
    return jnp.concatenate(pieces, axis=0)


def _unstack_heads(st, low):
    pieces = []
    for p in range(GROUP_WIDTH // LANES):
        even = st[BLOCK * (2 * p): BLOCK * (2 * p + 1)]
        odd = st[BLOCK * (2 * p + 1): BLOCK * (2 * p + 2)]
        pieces.append(jnp.where(low, even, odd))
    return jnp.concatenate(pieces, axis=1)


def _band_mask(n):
    row = lax.broadcasted_iota(jnp.int32, (BLOCK, 2 * BLOCK), 0)
    col = lax.broadcasted_iota(jnp.int32, (BLOCK, 2 * BLOCK), 1)
    return (col > row) & (col <= row + BLOCK) & ((n > 0) | (col >= BLOCK))


def _softmax_heads(s, valid, sink_ref, j):
    ps, psinks = [], []
    for h in range(GQA_GROUP):
        sh = jnp.where(valid, s[BLOCK * h: BLOCK * (h + 1)], MASKED)
        sink = sink_ref[j * GQA_GROUP + h]
        m = jnp.maximum(jnp.max(sh, axis=1, keepdims=True), sink)
        e = jnp.exp(sh - m)
        es = jnp.exp(sink - m)
        inv = 1.0 / (jnp.sum(e, axis=1, keepdims=True) + es)
        ps.append(e * inv)
        psinks.append(es * inv)
    return jnp.concatenate(ps, axis=0), jnp.concatenate(psinks, axis=0)


def _attn_fwd(name, qn, kn, proj, sinks):
    T = qn.shape[0]
    nb = T // BLOCK

    def body(sink_ref, q_ref, kp_ref, kc_ref, vp_ref, vc_ref, o_ref):
        n, j = pl.program_id(0), pl.program_id(1)
        low = lax.broadcasted_iota(jnp.int32, (1, LANES), 1) < HEAD_DIM
        k2 = _dup_head(jnp.concatenate([kp_ref[...], kc_ref[...]], axis=0), j)
        v2 = _dup_head(jnp.concatenate([vp_ref[...], vc_ref[...]], axis=0), j)
        q = _stack_heads(q_ref[...], low)
        s = lax.dot_general(q, k2, _DIMS["nt"], preferred_element_type=F32) * ATTN_SCALE
        p, _ = _softmax_heads(s, _band_mask(n), sink_ref, j)
        o = jnp.dot(p.astype(BF), v2, preferred_element_type=F32)
        o_ref[...] = _unstack_heads(o, low).astype(BF)

    prev = lambda n, j: (jnp.maximum(n - 1, 0), 0)
    return pl.pallas_call(
        body, name=name, grid=(nb, 2),
        in_specs=[pl.BlockSpec(memory_space=pltpu.SMEM),
                  pl.BlockSpec((BLOCK, GROUP_WIDTH), lambda n, j: (n, j)),
                  pl.BlockSpec((BLOCK, LANES), prev), pl.BlockSpec((BLOCK, LANES), lambda n, j: (n, 0)),
                  pl.BlockSpec((BLOCK, LANES), lambda n, j: (jnp.maximum(n - 1, 0), KV_COL_BLOCK_V)),
                  pl.BlockSpec((BLOCK, LANES), lambda n, j: (n, KV_COL_BLOCK_V))],
        out_specs=pl.BlockSpec((BLOCK, GROUP_WIDTH), lambda n, j: (n, j)),
        out_shape=jax.ShapeDtypeStruct((T, ATTN_WIDTH), BF), compiler_params=_params(("parallel", "parallel")),
    )(sinks, qn, kn, kn, proj, proj)


def _attn_bwd(name, dout, qn, kn, proj, sinks):
    T = qn.shape[0]
    nb = T // BLOCK

    def body(sink_ref, do_ref, q_ref, kp_ref, kc_ref, vp_ref, vc_ref, dq_ref, dk_ref, dv_ref, dsink_ref,
             carry_k, carry_v, tot_k, tot_v):
        n = pl.program_id(0)
        lane = lax.broadcasted_iota(jnp.int32, (1, LANES), 1)
        low = lane < HEAD_DIM

        @pl.when(n == 0)
        def _():
            carry_k[...] = jnp.zeros_like(carry_k)
            carry_v[...] = jnp.zeros_like(carry_v)
            dsink_ref[...] = jnp.zeros_like(dsink_ref)

        @pl.when(n == nb)
        def _():
            tot_k[...] = jnp.zeros_like(tot_k)
            tot_v[...] = jnp.zeros_like(tot_v)

        @pl.when(n < nb)
        def _():
            kk = jnp.concatenate([kp_ref[...], kc_ref[...]], axis=0)
            vv = jnp.concatenate([vp_ref[...], vc_ref[...]], axis=0)
            valid = _band_mask(n)
            dk_tot = jnp.zeros((2 * BLOCK, LANES), F32)
            dv_tot = jnp.zeros((2 * BLOCK, LANES), F32)
            dsink = jnp.zeros((1, LANES), F32)
            for j in range(2):
                k2 = _dup_head(kk, j)
                v2 = _dup_head(vv, j)
                q = _stack_heads(q_ref[:, GROUP_WIDTH * j: GROUP_WIDTH * (j + 1)], low)
                do = _stack_heads(do_ref[:, GROUP_WIDTH * j: GROUP_WIDTH * (j + 1)], low)
                s = lax.dot_general(q, k2, _DIMS["nt"], preferred_element_type=F32) * ATTN_SCALE
                p, psink = _softmax_heads(s, valid, sink_ref, j)
                dp = lax.dot_general(do, v2, _DIMS["nt"], preferred_element_type=F32)
                delta = jnp.sum(p * dp, axis=1, keepdims=True)
                ds = (p * (dp - delta) * ATTN_SCALE).astype(BF)
                dq_ref[:, GROUP_WIDTH * j: GROUP_WIDTH * (j + 1)] = _unstack_heads(
                    jnp.dot(ds, k2, preferred_element_type=F32), low).astype(BF)
                dk2 = lax.dot_general(ds, q, _DIMS["tn"], preferred_element_type=F32)
                dv2 = lax.dot_general(p.astype(BF), do, _DIMS["tn"], preferred_element_type=F32)
                mine = low if j == 0 else jnp.logical_not(low)
                dk_tot = dk_tot + jnp.where(mine, dk2 + pltpu.roll(dk2, HEAD_DIM, axis=1), 0.0)
                dv_tot = dv_tot + jnp.where(mine, dv2 + pltpu.roll(dv2, HEAD_DIM, axis=1), 0.0)
                sink_term = psink * delta
                for h in range(GQA_GROUP):
                    val = -jnp.sum(sink_term[BLOCK * h: BLOCK * (h + 1)], axis=0, keepdims=True)
                    dsink = dsink + jnp.where(lane == j * GQA_GROUP + h, val, 0.0)
            tot_k[...] = dk_tot
            tot_v[...] = dv_tot
            dsink_ref[0:1, :] += dsink

        dk_ref[...] = (carry_k[...] + tot_k[0:BLOCK]).astype(BF)
        dv_ref[...] = (carry_v[...] + tot_v[0:BLOCK]).astype(BF)
        carry_k[...] = tot_k[BLOCK:]
        carry_v[...] = tot_v[BLOCK:]

    cur = lambda n: (jnp.minimum(n, nb - 1), 0)
    prev = lambda n: (jnp.maximum(n - 1, 0), 0)
    wide = pl.BlockSpec((BLOCK, ATTN_WIDTH), cur)
    return pl.pallas_call(
        body, name=name, grid=(nb + 1,),
        in_specs=[pl.BlockSpec(memory_space=pltpu.SMEM), wide, wide,
                  pl.BlockSpec((BLOCK, LANES), prev), pl.BlockSpec((BLOCK, LANES), cur),
                  pl.BlockSpec((BLOCK, LANES), lambda n: (jnp.maximum(n - 1, 0), KV_COL_BLOCK_V)),
                  pl.BlockSpec((BLOCK, LANES), lambda n: (jnp.minimum(n, nb - 1), KV_COL_BLOCK_V))],
        out_specs=[wide, pl.BlockSpec((BLOCK, LANES), prev), pl.BlockSpec((BLOCK, LANES), prev),
                   pl.BlockSpec((8, LANES), lambda n: (0, 0))],
        out_shape=[jax.ShapeDtypeStruct((T, ATTN_WIDTH), BF), jax.ShapeDtypeStruct((T, KV_WIDTH), BF),
                   jax.ShapeDtypeStruct((T, KV_WIDTH), BF), jax.ShapeDtypeStruct((8, LANES), F32)],
        scratch_shapes=[pltpu.VMEM((BLOCK, LANES), F32), pltpu.VMEM((BLOCK, LANES), F32),
                        pltpu.VMEM((2 * BLOCK, LANES), F32), pltpu.VMEM((2 * BLOCK, LANES), F32)],
        compiler_params=_params(("arbitrary",)),
    )(sinks, dout, qn, kn, kn, proj, proj)


def _swiglu_fwd_epilogue(accs, ex):
    g, u = accs
    return [g, u, g * jax.nn.sigmoid(g) * u], []


def _swiglu_bwd_epilogue(accs, ex):
    (da,) = accs
    g, u = ex[0].astype(F32), ex[1].astype(F32)
    s = jax.nn.sigmoid(g)
    silu = g * s
    return [da * u * (s * (1.0 + g * (1.0 - s))), da * silu, silu * u], []


def _half_residual_epilogue(accs, ex):
    return [ex[0] + 0.5 * accs[0]], []


def _residual_epilogue(accs, ex):
    return [ex[0] + accs[0]], []


def _rms_bwd_epilogue(accs, ex):
    (dn,) = accs
    xv, g, dres = ex
    r = lax.rsqrt(jnp.mean(xv * xv, axis=-1, keepdims=True) + RMS_EPS)
    xhat = xv * r
    dxhat = dn * g
    dx = dres + r * (dxhat - xhat * jnp.mean(dxhat * xhat, axis=-1, keepdims=True))
    return [dx, dx], [dn * xhat]


def _loss_epilogue(accs, ex):
    xv, target = ex
    d = xv + 0.5 * accs[0] - target
    dy = d * (1.0 / D_MODEL)
    return [dy, dy], [d * d]


def _merge_fwd_epilogue(accs, ex):
    (ba,) = accs
    bp, gp_pre, ga_pre, bias_p, bias_a = ex
    gp = jax.nn.sigmoid(gp_pre.astype(F32) + bias_p)
    ga = jax.nn.sigmoid(ga_pre.astype(F32) + bias_a)
    return [gp * bp.astype(F32) + ga * ba, ba], []


def _merge_bwd_epilogue(accs, ex):
    (dm,) = accs
    bp, ba, gp_pre, ga_pre, bias_p, bias_a = ex
    gp = jax.nn.sigmoid(gp_pre.astype(F32) + bias_p)
    ga = jax.nn.sigmoid(ga_pre.astype(F32) + bias_a)
    dgp = dm * bp.astype(F32) * gp * (1.0 - gp)
    dga = dm * ba.astype(F32) * ga * (1.0 - ga)
    return [dm * gp, dm * ga, dgp, dga], [dgp, dga]


def _prep(name, ws, transposes):
    n = len(ws)

    def body(*refs):
        for w_ref, o_ref, tr in zip(refs[:n], refs[n:], transposes):
            v = w_ref[...]
            o_ref[...] = (v.T if tr else v).astype(BF)

    shapes = [jax.ShapeDtypeStruct(w.shape[::-1] if tr else w.shape, BF) for w, tr in zip(ws, transposes)]
    return pl.pallas_call(body, name=name, out_shape=shapes, compiler_params=_params())(*ws)


def _adam_math(w, g, m, v):
    m = ADAM_B1 * m + (1.0 - ADAM_B1) * g
    v = ADAM_B2 * v + (1.0 - ADAM_B2) * jnp.square(g)
    m_hat = m / (1.0 - ADAM_B1 ** ADAM_STEP)
    v_hat = v / (1.0 - ADAM_B2 ** ADAM_STEP)
    delta = -ADAM_LR * (m_hat / (jnp.sqrt(v_hat) + ADAM_EPS) + ADAM_WD * w)
    return delta, m, v


def _adamw_sharded(name, slots, w, m, v, transpose):
    def body(s_ref, w_ref, m_ref, v_ref, g_out, d_out, m_out, v_out):
        g = s_ref[0].astype(F32)
        for i in range(1, 4):
            g = g + s_ref[i].astype(F32)
        if transpose:
            g = g.T
        delta, mn, vn = _adam_math(w_ref[...], g, m_ref[...], v_ref[...])
        g_out[...] = g
        d_out[...] = delta
        m_out[...] = mn
        v_out[...] = vn

    out_shape = [jax.ShapeDtypeStruct(w.shape, F32)] * 4
    _, r, C = slots.shape
    rows = r // 4
    if transpose or rows % 8:
        return pl.pallas_call(body, name=name, out_shape=out_shape, compiler_params=_params())(slots, w, m, v)
    tile = pl.BlockSpec((rows, C), lambda i: (i, 0))
    return pl.pallas_call(
        body, name=name, grid=(4,), in_specs=[pl.BlockSpec((4, rows, C), lambda i: (0, i, 0)), tile, tile, tile],
        out_specs=[tile] * 4, out_shape=out_shape, compiler_params=_params(("parallel",)),
    )(slots, w, m, v)


SMALL_LAYOUT = (("ffn1_norm", 0, (8, LANES)), ("mix_norm", 8, (8, LANES)), ("ffn2_norm", 16, (8, LANES)),
                ("gate_bias", 24, (16, LANES)), ("pool_scale", 40, (4, LANES)), ("q_norm", 48, (1, HEAD_DIM)),
                ("k_norm", 56, (1, HEAD_DIM)), ("sinks", 64, (1, N_HEADS)))
LOSS_ROW = 72
SMALL_ROWS = 80


def _adamw_small(name, g_vec, g_pool_w, params):
    n = len(SMALL_LAYOUT) + 1

    def body(vec_ref, pw_ref, *refs):
        ins, outs = refs[:3 * n], refs[3 * n:]
        vec = vec_ref[0]
        pw = pw_ref[0]
        for i in range(1, N_DEV):
            vec = vec + vec_ref[i]
            pw = pw + pw_ref[i]
        grads = [vec[r0:r0 + shape[0], 0:shape[1]] for _, r0, shape in SMALL_LAYOUT] + [pw]
        for p, g in enumerate(grads):
            w_ref, m_ref, v_ref = ins[3 * p: 3 * p + 3]
            delta, mn, vn = _adam_math(w_ref[...], g, m_ref[...], v_ref[...])
            for o_ref, val in zip(outs[4 * p: 4 * p + 4], (g, delta, mn, vn)):
                o_ref[...] = val
        outs[4 * n][...] = vec[LOSS_ROW:LOSS_ROW + 1, :]

    flat = [a for wmv in params for a in wmv]
    out_shape = [jax.ShapeDtypeStruct(wmv[0].shape, F32) for wmv in params for _ in range(4)]
    out_shape.append(jax.ShapeDtypeStruct((1, LANES), F32))
    res = pl.pallas_call(body, name=name, out_shape=out_shape, compiler_params=_params())(g_vec, g_pool_w, *flat)
    return [tuple(res[4 * p: 4 * p + 4]) for p in range(n)], res[4 * n]


def _place():
    x, y, c = lax.axis_index("x"), lax.axis_index("y"), lax.axis_index("c")
    other_chips = [(1 - x, y), (x, 1 - y), (1 - x, 1 - y)]
    return x, y, c, other_chips


def _rows(ref, r, place, natural=False):
    px, py, pc = place
    b = 4 * px + 2 * py + pc if natural else 4 * pc + 2 * px + py
    return ref.at[pl.ds(pl.multiple_of(b * r, 8), r), :]


def _gather_task(shards, natural=(), forward_at=0.75):
    n = len(shards)
    rs = [s.shape[0] for s in shards]
    rows_of = lambda ref, k, place: _rows(ref, rs[k], place, k in natural)

    def copy(scr, outs, k, slot, block, to, src=None):
        rows = rows_of(outs[k], k, block)
        return pltpu.make_async_remote_copy(
            src_ref=rows if src is None else src, dst_ref=rows, send_sem=scr[0].at[7 * k + slot],
            recv_sem=scr[1].at[7 * k + slot], device_id=to, device_id_type=MESH)

    def first_sends(ins, outs, scr):
        x, y, c, chips = _place()
        me = (x, y, c)
        cps = [copy(scr, outs, k, 1 + j, me, (*chip, c), src=ins[k]) for j, chip in enumerate(chips) for k in range(n)]
        return cps + [copy(scr, outs, k, 0, me, (x, y, 1 - c), src=ins[k]) for k in range(n)]

    def passed_on(outs, scr):
        x, y, c, chips = _place()
        return [copy(scr, outs, k, 4 + j, (*chip, c), (x, y, 1 - c)) for j, chip in enumerate(chips) for k in range(n)]

    def local(ins, outs, scr):
        x, y, c, _ = _place()
        return [pltpu.make_async_copy(ins[k], rows_of(outs[k], k, (x, y, c)), scr[2].at[k]) for k in range(n)]

    def start(ins, outs, scr):
        for cp in local(ins, outs, scr) + first_sends(ins, outs, scr):
            cp.start()

    def forward(ins, outs, scr):
        x, y, c, chips = _place()
        for j, chip in enumerate(chips):
            for k in range(n):
                copy(scr, outs, k, 1 + j, (*chip, c), (x, y, c)).wait_recv()
        for cp in passed_on(outs, scr):
            cp.start()

    def finish(ins, outs, scr):
        x, y, c, chips = _place()
        for k in range(n):
            copy(scr, outs, k, 0, (x, y, 1 - c), (x, y, c)).wait_recv()
        for j, chip in enumerate(chips):
            for k in range(n):
                copy(scr, outs, k, 4 + j, (*chip, 1 - c), (x, y, c)).wait_recv()
        for cp in first_sends(ins, outs, scr) + passed_on(outs, scr):
            cp.wait_send()
        for cp in local(ins, outs, scr):
            cp.wait()

    out_shapes = [jax.ShapeDtypeStruct((N_DEV * s.shape[0], s.shape[1]), s.dtype) for s in shards]
    scratch = [pltpu.SemaphoreType.DMA((7 * n,)), pltpu.SemaphoreType.DMA((7 * n,)), pltpu.SemaphoreType.DMA((n,))]
    return _Task(shards, out_shapes, scratch, [(0, start), (forward_at, forward), (1.0, finish)])


def _all_gather(name, shards, natural=()):
    return _comm_only(name, [_gather_task(shards, natural)])[0]


def _chip_task(sums):
    n = len(sums)
    rs = [s.shape[0] // 4 for s in sums]

    def block(ref, k, chip_index):
        return ref.at[pl.ds(pl.multiple_of(chip_index * rs[k], 8), rs[k]), :]

    def copies(ins, outs, scr):
        send_sems, recv_sems, local_sems = scr
        x, y, c, chips = _place()
        here = 2 * x + y
        local = [pltpu.make_async_copy(block(ins[k], k, here), outs[k].at[here], local_sems.at[k]) for k in range(n)]
        remote = []
        for j, (px, py) in enumerate(chips):
            remote += [pltpu.make_async_remote_copy(
                src_ref=block(ins[k], k, 2 * px + py), dst_ref=outs[k].at[here],
                send_sem=send_sems.at[3 * k + j], recv_sem=recv_sems.at[3 * k + j],
                device_id=(px, py, c), device_id_type=MESH) for k in range(n)]
        return local, remote

    def start(ins, outs, scr):
        local, remote = copies(ins, outs, scr)
        for cp in local + remote:
            cp.start()

    def finish(ins, outs, scr):
        local, remote = copies(ins, outs, scr)
        for cp in remote:
            cp.wait()
        for cp in local:
            cp.wait()

    out_shapes = [jax.ShapeDtypeStruct((4, r, s.shape[1]), s.dtype) for r, s in zip(rs, sums)]
    scratch = [pltpu.SemaphoreType.DMA((3 * n,)), pltpu.SemaphoreType.DMA((3 * n,)), pltpu.SemaphoreType.DMA((n,))]
    return _Task(sums, out_shapes, scratch, [(0, start), (1.0, finish)])


def _dw_pair(name, a, b, scale, comm=None, blocks=1):
    T, M = a.shape
    N = b.shape[1]
    half = M // 2
    wide = half // blocks
    tk = min(2048, T)
    nK = T // tk
    plumb = _CommPlumbing(comm)

    def body(core_ref, *rest):
        a_refs, b_ref, rest = rest[:blocks], rest[blocks], rest[blocks + 1:]
        c_in = rest[:plumb.n_in]
        o_ref = rest[plumb.n_in]
        c_out = rest[plumb.n_in + 1: plumb.n_in + 1 + plumb.n_out]
        acc, stage, land, send_sem, recv_sem = rest[plumb.n_in + 1 + plumb.n_out: plumb.n_in + 6 + plumb.n_out]
        c_scr = rest[plumb.n_in + 6 + plumb.n_out:]
        i, k = pl.program_id(0), pl.program_id(1)
        x, y, c, _ = _place()
        push = pltpu.make_async_remote_copy(src_ref=stage, dst_ref=land, send_sem=send_sem, recv_sem=recv_sem,
                                            device_id=(x, y, 1 - c), device_id_type=MESH)
        if comm:
            plumb.run(i * nK + k, 2 * nK, True, c_in, c_out, c_scr)

        av = a_refs[0][...] if blocks == 1 else jnp.concatenate([r[...] for r in a_refs], axis=1)
        p = lax.dot_general(av, b_ref[...], _DIMS["tn"], preferred_element_type=F32)

        @pl.when(k == 0)
        def _():
            acc[...] = p

        @pl.when(k > 0)
        def _():
            acc[...] += p

        @pl.when((i == 0) & (k == nK - 1))
        def _():
            stage[...] = (scale * acc[...]).astype(BF)
            push.start()

        @pl.when((i == 1) & (k == nK - 1))
        def _():
            push.wait_recv()
            o_ref[...] = (scale * acc[...] + land[...].astype(F32)).astype(BF)
            push.wait_send()

        if comm:
            plumb.run(i * nK + k, 2 * nK, False, c_in, c_out, c_scr)

    grid_spec = pltpu.PrefetchScalarGridSpec(
        num_scalar_prefetch=1, grid=(2, nK),
        in_specs=[pl.BlockSpec((tk, wide), functools.partial(
            lambda i, k, core, j: (k, (2 * j if blocks > 1 else 0) + jnp.where(i == 0, 1 - core[0], core[0])), j=j))
            for j in range(blocks)] + [pl.BlockSpec((tk, N), lambda i, k, core: (k, 0))] + [ANY] * plumb.n_in,
        out_specs=[pl.BlockSpec((half, N), lambda i, k, core: (0, 0))] + [ANY] * plumb.n_out,
        scratch_shapes=[pltpu.VMEM((half, N), F32), pltpu.VMEM((half, N), BF), pltpu.VMEM((half, N), BF),
                        pltpu.SemaphoreType.DMA, pltpu.SemaphoreType.DMA] + plumb.scratch)
    core = lax.axis_index("c").astype(jnp.int32).reshape(1)
    res = pl.pallas_call(
        body, name=name, grid_spec=grid_spec,
        out_shape=[jax.ShapeDtypeStruct((half, N), BF)] + plumb.out_shapes,
        compiler_params=_params(("arbitrary", "arbitrary")),
    )(core, *([a] * blocks), b, *plumb.args)
    return (res[0], plumb.split_outputs(res[1:])) if comm else res[0]


def _pair_exchange(name, parts):
    n = len(parts)

    def body(*refs):
        ins, outs = refs[:n], refs[n:2 * n]
        send_sems, recv_sems = refs[2 * n:]
        x, y, c, _ = _place()
        copies = [pltpu.make_async_remote_copy(
            src_ref=ins[k].at[:, pl.ds(1 - c, 1)], dst_ref=outs[k], send_sem=send_sems.at[k], recv_sem=recv_sems.at[k],
            device_id=(x, y, 1 - c), device_id_type=MESH) for k in range(n)]
        for cp in copies:
            cp.start()
        for cp in copies:
            cp.wait()

    return pl.pallas_call(
        body, name=name, in_specs=[ANY] * n, out_specs=[ANY] * n,
        out_shape=[jax.ShapeDtypeStruct((4, 1) + p.shape[2:], p.dtype) for p in parts],
        scratch_shapes=[pltpu.SemaphoreType.DMA((n,)), pltpu.SemaphoreType.DMA((n,))],
        compiler_params=pltpu.CompilerParams(has_side_effects=True),
    )(*parts)


def _pair_sum(name, part, got, core):
    _, _, r, C = part.shape

    def body(core_ref, p_ref, g_ref, o_ref):
        o_ref[0] = (p_ref[0, 0].astype(F32) + g_ref[0, 0].astype(F32)).astype(o_ref.dtype)

    return pl.pallas_call(
        body, name=name,
        grid_spec=pltpu.PrefetchScalarGridSpec(
            num_scalar_prefetch=1, grid=(4,),
            in_specs=[pl.BlockSpec((1, 1, r, C), lambda i, core_ref: (i, core_ref[0], 0, 0)),
                      pl.BlockSpec((1, 1, r, C), lambda i, core_ref: (i, 0, 0, 0))],
            out_specs=pl.BlockSpec((1, r, C), lambda i, core_ref: (i, 0, 0))),
        out_shape=jax.ShapeDtypeStruct((4, r, C), part.dtype), compiler_params=_params(("parallel",)),
    )(core, part, got)


def _ffn_bwd(tag, dy, dyb, x, gain, wgT, wuT, wd, saved, pending):
    n, g, u = saved
    half = lambda accs, ex: _swiglu_bwd_epilogue([0.5 * accs[0]], ex)
    (dg, du, a), done0 = _mm(tag + "_d_act", [(dyb, wd, "nt", 0)], [BF, BF, BF], tm=512, tn=1408, tk=D_MODEL,
                             epilogue=half, extras=[(g, "tile", 0), (u, "tile", 0)], comm=pending, cols_outer=True)
    sum_d = _dw_pair(tag + "_dw_down", a, dyb, 0.5)
    sum_g, (slots_d,) = _dw_pair(tag + "_dw_gate", dg, n, 1.0, comm=[_chip_task([sum_d])])
    sum_u, (slots_g,) = _dw_pair(tag + "_dw_up", du, n, 1.0, comm=[_chip_task([sum_g])])
    (dx, dxb, dgain), (slots_u,) = _mm(
        tag + "_d_norm", [(dg, wgT, "nn", 0), (du, wuT, "nn", 0)], [F32, BF], tm=512, tn=D_MODEL, tk=D_FF,
        epilogue=_rms_bwd_epilogue, extras=[(x, "tile", 0), (gain, "row", 0), (dy, "tile", 0)], n_colsum=1,
        comm=[_chip_task([sum_u])])
    return dx, dxb, dgain, done0, slots_g[0], slots_u[0], slots_d[0]


def _tile_gain(g):
    return jnp.concatenate([g, g]).reshape(1, LANES)


def _fold_heads(partials):
    return jnp.sum(partials.reshape(-1, HEAD_DIM), axis=0)


def _pack_small_grads(grads, loss_local):
    pieces, row = [], 0
    for name, r0, _ in SMALL_LAYOUT + (("loss", LOSS_ROW, None),):
        v = (loss_local if name == "loss" else grads[name]).reshape(-1)
        rows = -(-v.size // LANES)
        block = jnp.pad(v, (0, rows * LANES - v.size)).reshape(rows, LANES)
        pieces += [jnp.zeros((r0 - row, LANES), F32)] * (r0 > row) + [block]
        row = r0 + rows
    pieces.append(jnp.zeros((SMALL_ROWS - row, LANES), F32))
    return jnp.concatenate(pieces, axis=0)


def kernel(x, ffn1_norm, ffn1_w_gate, ffn1_w_up, ffn1_w_down, mix_norm, w_in, pool_w, pool_scale, w_pool_out, q_norm, k_norm, sinks, w_attn_out, gate_bias, w_out, ffn2_norm, ffn2_w_gate, ffn2_w_up, ffn2_w_down, loss_target, m_ffn1_norm, m_ffn1_w_gate, m_ffn1_w_up, m_ffn1_w_down, m_mix_norm, m_w_in, m_pool_w, m_pool_scale, m_w_pool_out, m_q_norm, m_k_norm, m_sinks, m_w_attn_out, m_gate_bias, m_w_out, m_ffn2_norm, m_ffn2_w_gate, m_ffn2_w_up, m_ffn2_w_down, v_ffn1_norm, v_ffn1_w_gate, v_ffn1_w_up, v_ffn1_w_down, v_mix_norm, v_w_in, v_pool_w, v_pool_scale, v_w_pool_out, v_q_norm, v_k_norm, v_sinks, v_w_attn_out, v_gate_bias, v_w_out, v_ffn2_norm, v_ffn2_w_gate, v_ffn2_w_up, v_ffn2_w_down):
    T = x.shape[1]
    x2 = x.reshape(T, D_MODEL)
    target = loss_target.reshape(T, D_MODEL)

    big = [
        ("ffn1_w_gate", ffn1_w_gate, m_ffn1_w_gate, v_ffn1_w_gate, True, False),
        ("ffn1_w_up", ffn1_w_up, m_ffn1_w_up, v_ffn1_w_up, True, False),
        ("ffn1_w_down", ffn1_w_down, m_ffn1_w_down, v_ffn1_w_down, False, False),
        ("w_in", w_in, m_w_in, v_w_in, True, False),
        ("w_pool_out", w_pool_out, m_w_pool_out, v_w_pool_out, False, True),
        ("w_attn_out", w_attn_out, m_w_attn_out, v_w_attn_out, False, False),
        ("w_out", w_out, m_w_out, v_w_out, False, False),
        ("ffn2_w_gate", ffn2_w_gate, m_ffn2_w_gate, v_ffn2_w_gate, True, False),
        ("ffn2_w_up", ffn2_w_up, m_ffn2_w_up, v_ffn2_w_up, True, False),
        ("ffn2_w_down", ffn2_w_down, m_ffn2_w_down, v_ffn2_w_down, False, False),
    ]
    view = lambda a, tv: a.T if tv else a
    shards = _prep("prep_weights", [view(w, tv) for _, w, _, _, tv, _ in big], [tk_ for *_, tk_ in big])
    wg1T, wu1T = _all_gather("gather_ffn1_gate_up", shards[0:2])

    g1 = ffn1_norm.reshape(1, D_MODEL)
    g2 = mix_norm.reshape(1, D_MODEL)
    g3 = ffn2_norm.reshape(1, D_MODEL)
    bias_row = gate_bias.reshape(1, 2 * D_MODEL)
    qg, kg = _tile_gain(q_norm), _tile_gain(k_norm)
    scale_row = pool_scale.reshape(1, POOL_WIDTH)

    n1 = _rms_fwd("ffn1_norm", x2, g1)
    (gt1, up1, act1), ((wd1,), (w_poT, w_ao, w_o)) = _mm(
        "ffn1_gate_up", [(n1, wg1T, "nt", 0), (n1, wu1T, "nt", 1)], [BF, BF, BF], tm=512, tn=1408, tk=D_MODEL,
        epilogue=_swiglu_fwd_epilogue, cols_outer=True,
        comm=[_gather_task(shards[2:3], forward_at=0.6), _gather_task(shards[4:7], natural=(0, 1, 2), forward_at=0.9)])
    (h1,), ((w_inT,),) = _mm(
        "ffn1_down", [(act1, wd1, "nn", 0)], [F32], tm=512, tn=D_MODEL, tk=D_FF, epilogue=_half_residual_epilogue,
        extras=[(x2, "tile", 0)], comm=[_gather_task(shards[3:4], natural=(0,), forward_at=0.85)])
    saved1 = (n1, gt1, up1)
    u = _rms_fwd("mix_norm", h1, g2)
    (proj,), ((wg2T, wu2T),) = _mm(
        "in_proj", [(u, w_inT, "nt", 0)], [BF], tm=512, tn=1280, tk=D_MODEL, cols_outer=True,
        comm=[_gather_task(shards[7:9], forward_at=0.85)])
    pooled, mixed = _pool_fwd("pool_fwd", proj, pool_w, scale_row)
    qn = _headnorm_fwd("q_norm", proj, COL_Q, ATTN_WIDTH, qg)
    kn = _headnorm_fwd("k_norm", proj, COL_K, KV_WIDTH, kg)
    attn = _attn_fwd("attn_fwd", qn, kn, proj, sinks)
    (bp,) = _mm("pool_out", [(mixed, w_poT, "nt", 0)], [BF], tm=1024, tn=D_MODEL, tk=POOL_WIDTH)
    gate_tn = 256
    gate_extras = [(proj, "tile", COL_GP // gate_tn), (proj, "tile", COL_GA // gate_tn),
                   (bias_row, "row", 0), (bias_row, "row", D_MODEL // gate_tn)]
    (merged, ba), ((wd2,),) = _mm(
        "attn_out_merge", [(attn, w_ao, "nn", 0)], [BF, BF], tm=2048, tn=gate_tn, tk=ATTN_WIDTH,
        epilogue=_merge_fwd_epilogue, extras=[(bp, "tile", 0)] + gate_extras,
        comm=[_gather_task(shards[9:10], forward_at=0.85)])
    (h2,) = _mm("mix_out", [(merged, w_o, "nn", 0)], [F32], tm=512, tn=D_MODEL, tk=D_MODEL,
                epilogue=_residual_epilogue, extras=[(h1, "tile", 0)])
    n2 = _rms_fwd("ffn2_norm", h2, g3)
    gt2, up2, act2 = _mm("ffn2_gate_up", [(n2, wg2T, "nt", 0), (n2, wu2T, "nt", 1)], [BF, BF, BF],
                         tm=512, tn=1408, tk=D_MODEL, epilogue=_swiglu_fwd_epilogue, cols_outer=True)
    dy, dyb, sq = _mm("ffn2_down_loss", [(act2, wd2, "nn", 0)], [F32, BF], tm=512, tn=D_MODEL, tk=D_FF,
                      epilogue=_loss_epilogue, extras=[(h2, "tile", 0), (target, "tile", 0)], n_colsum=1)
    loss_local = 0.5 * jnp.sum(sq) / D_MODEL

    dh2, dh2b, dg3, _, slots_g2, slots_u2, slots_d2 = _ffn_bwd(
        "ffn2", dy, dyb, h2, g3, wg2T, wu2T, wd2, (n2, gt2, up2), [])
    dbp, dba, dgp, dga, cs_gp, cs_ga = _mm(
        "mix_out_bwd", [(dh2b, w_o, "nt", 0)], [BF, BF, BF, BF], tm=2048, tn=gate_tn, tk=D_MODEL,
        epilogue=_merge_bwd_epilogue, extras=[(bp, "tile", 0), (ba, "tile", 0)] + gate_extras, n_colsum=2)
    sum_o = _dw_pair("dw_out", merged, dh2b, 1.0, blocks=4)
    (dmixed,), ((slots_o,),) = _mm("pool_out_bwd", [(dbp, w_poT, "nn", 0)], [BF], tm=1024, tn=POOL_WIDTH, tk=D_MODEL,
                                   comm=[_chip_task([sum_o])])
    sum_po = _dw_pair("dw_pool_out", dbp, mixed, 1.0, blocks=4)
    (dattn,), ((slots_po,),) = _mm("attn_out_bwd", [(dba, w_ao, "nt", 0)], [BF], tm=1024, tn=ATTN_WIDTH, tk=D_MODEL,
                                   comm=[_chip_task([sum_po])])
    sum_ao = _dw_pair("dw_attn_out", attn, dba, 1.0, blocks=4)
    dxp, dpool_w, dpool_scale = _pool_bwd("pool_bwd", dmixed, pooled, pool_w, scale_row)
    dqn, dkn, dv, dsink_tile = _attn_bwd("attn_bwd", dattn, qn, kn, proj, sinks)
    dq, dqg = _headnorm_bwd("q_norm_bwd", dqn, proj, COL_Q, ATTN_WIDTH, qg)
    dk, dkg = _headnorm_bwd("k_norm_bwd", dkn, proj, COL_K, KV_WIDTH, kg)
    dproj = jnp.concatenate([dxp, dq, dk, dv, dgp, dga], axis=1)
    (dh1, dh1b, dg2), ((slots_ao,),) = _mm(
        "in_proj_bwd", [(dproj, w_inT, "nn", 0)], [F32, BF], tm=512, tn=D_MODEL, tk=IN_WIDTH, epilogue=_rms_bwd_epilogue,
        extras=[(h1, "tile", 0), (g2, "row", 0), (dh2, "tile", 0)], n_colsum=1, comm=[_chip_task([sum_ao])])
    (dw_inT,) = _mm("dw_in", [(dproj, u, "tn", 0)], [BF], tm=1280, tn=D_MODEL, tk=2048)
    part_in = dw_inT.reshape(4, 2, IN_WIDTH // N_DEV, D_MODEL)
    (got_in,) = _pair_exchange("pair_exchange_w_in", [part_in])
    core = lax.axis_index("c").astype(jnp.int32).reshape(1)
    sum_in = _pair_sum("pair_sum_w_in", part_in, got_in, core).reshape(IN_WIDTH // 2, D_MODEL)
    dx, _, dg1, ((slots_in,),), slots_g1, slots_u1, slots_d1 = _ffn_bwd(
        "ffn1", dh1, dh1b, x2, g1, wg1T, wu1T, wd1, saved1, [_chip_task([sum_in])])

    slots = [slots_g1, slots_u1, slots_d1, slots_in, slots_po, slots_ao, slots_o, slots_g2, slots_u2, slots_d2]
    big_out = {}
    for k, (nm, w, m, v, tv, tk_) in enumerate(big):
        res = _adamw_sharded("adamw_" + nm, slots[k], view(w, tv), view(m, tv), view(v, tv), tk_)
        big_out[nm] = tuple(view(r, tv) for r in res)

    small_grads = {
        "ffn1_norm": jnp.sum(dg1, axis=(0, 1)), "mix_norm": jnp.sum(dg2, axis=(0, 1)), "ffn2_norm": jnp.sum(dg3, axis=(0, 1)),
        "gate_bias": jnp.concatenate([jnp.sum(cs_gp, axis=(0, 1)), jnp.sum(cs_ga, axis=(0, 1))]),
        "pool_scale": dpool_scale, "q_norm": _fold_heads(dqg), "k_norm": _fold_heads(dkg),
        "sinks": dsink_tile[0, :N_HEADS]}
    g_vec, g_pool_w = _all_gather("gather_small_grads", [_pack_small_grads(small_grads, loss_local),
                                                         dpool_w.reshape(-1, LANES)])
    given = {"ffn1_norm": (ffn1_norm, m_ffn1_norm, v_ffn1_norm), "mix_norm": (mix_norm, m_mix_norm, v_mix_norm),
             "ffn2_norm": (ffn2_norm, m_ffn2_norm, v_ffn2_norm), "gate_bias": (gate_bias, m_gate_bias, v_gate_bias),
             "pool_scale": (pool_scale, m_pool_scale, v_pool_scale), "q_norm": (q_norm, m_q_norm, v_q_norm),
             "k_norm": (k_norm, m_k_norm, v_k_norm), "sinks": (sinks, m_sinks, v_sinks)}
    params = [tuple(a.reshape(shape) for a in given[nm]) for nm, _, shape in SMALL_LAYOUT]
    params.append(tuple(a.reshape(-1, LANES) for a in (pool_w, m_pool_w, v_pool_w)))
    small_res, loss_row = _adamw_small("adamw_small", g_vec.reshape(N_DEV, SMALL_ROWS, LANES),
                                       g_pool_w.reshape(N_DEV, -1, LANES), params)
    small_out = {nm: tuple(r.reshape(given[nm][0].shape) for r in res)
                 for (nm, _, _), res in zip(SMALL_LAYOUT, small_res)}
    small_out["pool_w"] = tuple(r.reshape(pool_w.shape) for r in small_res[-1])
    loss = loss_row[0, 0]

    order = ["ffn1_norm", "ffn1_w_gate", "ffn1_w_up", "ffn1_w_down", "mix_norm", "w_in", "pool_w", "pool_scale",
             "w_pool_out", "q_norm", "k_norm", "sinks", "w_attn_out", "gate_bias", "w_out", "ffn2_norm",
             "ffn2_w_gate", "ffn2_w_up", "ffn2_w_down"]
    every = {**big_out, **small_out}
    outs = [loss, dx.reshape(x.shape)]
    for j in range(4):
        outs += [every[nm][j] for nm in order]
    return tuple(outs)
```

```python
import functools

import jax
import jax.numpy as jnp
from jax import lax
from jax.experimental import pallas as pl
from jax.experimental.pallas import tpu as pltpu

BF = jnp.bfloat16
F32 = jnp.float32

D_MODEL = 1024
D_FF = 2816
POOL_WIDTH = 512
POOL_GROUP = 128
N_POOL_GROUPS = 4
HEAD_DIM = 64
N_HEADS = 16
GQA_GROUP = 8
BLOCK = 128
ATTN_WIDTH = 1024
KV_WIDTH = 128
IN_WIDTH = 3840
RMS_EPS = 1e-6
N_DEV = 8
LANES = 128

COL_Q = POOL_WIDTH
COL_K = COL_Q + ATTN_WIDTH
COL_V = COL_K + KV_WIDTH
COL_GP = COL_V + KV_WIDTH
COL_GA = COL_GP + D_MODEL

ADAM_LR = 0.001
ADAM_B1 = 0.9
ADAM_B2 = 0.999
ADAM_EPS = 1e-08
ADAM_WD = 0.01
ADAM_STEP = 10

VMEM_LIMIT_V7X = 56 * 1024 * 1024
MESH = pl.DeviceIdType.MESH
ANY = pl.BlockSpec(memory_space=pl.ANY)


def _params(sem=None):
    return pltpu.CompilerParams(dimension_semantics=sem, vmem_limit_bytes=VMEM_LIMIT_V7X)


_DIMS = {"nt": (((1,), (1,)), ((), ())), "nn": (((1,), (0,)), ((), ())), "tn": (((0,), (0,)), ((), ()))}


class _Task:
    def __init__(self, inputs, out_shapes, scratch, phases):
        self.inputs, self.out_shapes, self.scratch = list(inputs), list(out_shapes), list(scratch)
        self.phases = list(phases)


class _CommPlumbing:
    def __init__(self, tasks):
        self.tasks = list(tasks or [])
        self.args = [a for t in self.tasks for a in t.inputs]
        self.out_shapes = [o for t in self.tasks for o in t.out_shapes]
        self.scratch = [s for t in self.tasks for s in t.scratch]
        self.n_in, self.n_out = len(self.args), len(self.out_shapes)

    def _slices(self, c_in, c_out, c_scr):
        i = o = s = 0
        for t in self.tasks:
            yield t, c_in[i:i + len(t.inputs)], c_out[o:o + len(t.out_shapes)], c_scr[s:s + len(t.scratch)]
            i, o, s = i + len(t.inputs), o + len(t.out_shapes), s + len(t.scratch)

    def run(self, step, steps, before, c_in, c_out, c_scr):
        for t, ins, outs, scr in self._slices(c_in, c_out, c_scr):
            for frac, fn in t.phases:
                if step is None:
                    fn(ins, outs, scr)
                elif before == (frac == 0):
                    at = 0 if frac == 0 else max(0, min(steps, -(-int(round(frac * steps * 64)) // 64)) - 1)
                    pl.when(step == at)(functools.partial(fn, ins, outs, scr))

    def split_outputs(self, flat):
        res, o = [], 0
        for t in self.tasks:
            res.append(list(flat[o:o + len(t.out_shapes)]))
            o += len(t.out_shapes)
        return res


def _comm_only(name, tasks):
    plumb = _CommPlumbing(tasks)

    def body(*refs):
        c_in, c_out = refs[:plumb.n_in], refs[plumb.n_in: plumb.n_in + plumb.n_out]
        c_scr = refs[plumb.n_in + plumb.n_out:]
        plumb.run(None, 1, True, c_in, c_out, c_scr)

    res = pl.pallas_call(
        body, name=name, in_specs=[ANY] * plumb.n_in, out_specs=[ANY] * plumb.n_out, out_shape=plumb.out_shapes,
        scratch_shapes=plumb.scratch, compiler_params=pltpu.CompilerParams(has_side_effects=True),
    )(*plumb.args)
    return plumb.split_outputs(res)


def _mm(name, terms, out_dtypes, *, tm, tn, tk, epilogue=None, extras=(), n_colsum=0, comm=None, cols_outer=False):
    a0, b0, mode0, _ = terms[0]
    if mode0 == "nt":
        (M, K), N = a0.shape, b0.shape[0]
    elif mode0 == "nn":
        (M, K), N = a0.shape, b0.shape[1]
    else:
        (K, M), N = a0.shape, b0.shape[1]
    tm, tn, tk = min(tm, M), min(tn, N), min(tk, K)
    assert M % tm == 0 and N % tn == 0 and K % tk == 0, (name, M, N, K, tm, tn, tk)
    nI, nJ, nK = M // tm, N // tn, K // tk
    n_terms = len(terms)
    n_acc = max(t[3] for t in terms) + 1
    n_ex = len(extras)
    n_out = len(out_dtypes)
    if epilogue is None:
        epilogue = lambda accs, ex: ([accs[0]], [])
    plumb = _CommPlumbing(comm)
    n_scr = n_acc if nK > 1 else 0
    grid = (nJ, nI, nK) if cols_outer else (nI, nJ, nK)

    def body(*refs):
        n_in = 2 * n_terms + n_ex
        ab = refs[: 2 * n_terms]
        ex_refs = refs[2 * n_terms: n_in]
        c_in = refs[n_in: n_in + plumb.n_in]
        o0 = n_in + plumb.n_in
        out_refs = refs[o0: o0 + n_out]
        cs_refs = refs[o0 + n_out: o0 + n_out + n_colsum]
        c_out = refs[o0 + n_out + n_colsum: o0 + n_out + n_colsum + plumb.n_out]
        s0 = o0 + n_out + n_colsum + plumb.n_out
        acc_refs = refs[s0: s0 + n_scr]
        c_scr = refs[s0 + n_scr:]
        steps = grid[0] * grid[1] * nK
        if comm:
            step = (pl.program_id(0) * grid[1] + pl.program_id(1)) * nK + pl.program_id(2)
            plumb.run(step, steps, True, c_in, c_out, c_scr)

        def products():
            accs = [None] * n_acc
            for t, (_, _, mode, ai) in enumerate(terms):
                p = lax.dot_general(ab[2 * t][...], ab[2 * t + 1][...], _DIMS[mode], preferred_element_type=F32)
                accs[ai] = p if accs[ai] is None else accs[ai] + p
            return accs

        def finish(accs):
            outs, colsums = epilogue(accs, [r[...] for r in ex_refs])
            for r, o in zip(out_refs, outs):
                r[...] = o.astype(r.dtype)
            for r, cs in zip(cs_refs, colsums):
                r[...] = jnp.sum(cs, axis=0, keepdims=True).reshape(r.shape)

        if nK == 1:
            finish(products())
        else:
            k = pl.program_id(2)
            accs = products()

            @pl.when(k == 0)
            def _():
                for r, a in zip(acc_refs, accs):
                    r[...] = a

            @pl.when(k > 0)
            def _():
                for r, a in zip(acc_refs, accs):
                    r[...] += a

            @pl.when(k == nK - 1)
            def _():
                finish([r[...] for r in acc_refs])

        if comm:
            plumb.run(step, steps, False, c_in, c_out, c_scr)

    def spec(block, index, fixed=False):
        imap = (lambda q, p, k: index(p, q, k)) if cols_outer else index
        return pl.BlockSpec(block, imap, pipeline_mode=pl.Buffered(1)) if fixed else pl.BlockSpec(block, imap)

    in_specs, args = [], []
    for a, b, mode, _ in terms:
        if mode == "nt":
            in_specs += [spec((tm, tk), lambda i, j, k: (i, k), nI * nK == 1),
                         spec((tn, tk), lambda i, j, k: (j, k), nJ * nK == 1)]
        elif mode == "nn":
            in_specs += [spec((tm, tk), lambda i, j, k: (i, k), nI * nK == 1),
                         spec((tk, tn), lambda i, j, k: (k, j), nJ * nK == 1)]
        else:
            in_specs += [spec((tk, tm), lambda i, j, k: (k, i), nI * nK == 1),
                         spec((tk, tn), lambda i, j, k: (k, j), nJ * nK == 1)]
        args += [a, b]
    for arr, kind, off in extras:
        if kind == "tile":
            in_specs.append(spec((tm, tn), functools.partial(lambda i, j, k, off: (i, j + off), off=off)))
        else:
            in_specs.append(spec((1, tn), functools.partial(lambda i, j, k, off: (0, j + off), off=off)))
        args.append(arr)
    out_shape = [jax.ShapeDtypeStruct((M, N), dt) for dt in out_dtypes]
    out_specs = [spec((tm, tn), lambda i, j, k: (i, j)) for _ in out_dtypes]
    out_shape += [jax.ShapeDtypeStruct((nI, 1, N), F32) for _ in range(n_colsum)]
    out_specs += [spec((1, 1, tn), lambda i, j, k: (i, 0, j)) for _ in range(n_colsum)]
    scratch = [pltpu.VMEM((tm, tn), F32) for _ in range(n_scr)]
    args += plumb.args
    in_specs += [ANY] * plumb.n_in
    out_shape += plumb.out_shapes
    out_specs += [ANY] * plumb.n_out
    sem = ("arbitrary",) * 3 if comm else ("parallel", "parallel", "arbitrary")
    res = pl.pallas_call(
        body, name=name, grid=grid, in_specs=in_specs, out_specs=out_specs, out_shape=out_shape,
        scratch_shapes=scratch + plumb.scratch, compiler_params=_params(sem),
    )(*args)
    n_own = n_out + n_colsum
    return (list(res[:n_own]), plumb.split_outputs(res[n_own:])) if comm is not None else res


ROW_TILE = 512


def _rms_fwd(name, x, g):
    T, D = x.shape

    def body(x_ref, g_ref, o_ref):
        xv = x_ref[...]
        r = lax.rsqrt(jnp.mean(xv * xv, axis=-1, keepdims=True) + RMS_EPS)
        o_ref[...] = (xv * r * g_ref[...]).astype(BF)

    return pl.pallas_call(
        body, name=name, grid=(T // ROW_TILE,),
        in_specs=[pl.BlockSpec((ROW_TILE, D), lambda i: (i, 0)), pl.BlockSpec((1, D), lambda i: (0, 0))],
        out_specs=pl.BlockSpec((ROW_TILE, D), lambda i: (i, 0)),
        out_shape=jax.ShapeDtypeStruct((T, D), BF), compiler_params=_params(("parallel",)),
    )(x, g)


HEADNORM_TILE = 1024


def _half_sum_matrix():
    r = lax.broadcasted_iota(jnp.int32, (LANES, LANES), 0) // HEAD_DIM
    c = lax.broadcasted_iota(jnp.int32, (LANES, LANES), 1) // HEAD_DIM
    return (r == c).astype(BF)


def _head_mean(v, ones_blockdiag):
    hi = v.astype(BF)
    lo = (v - hi.astype(F32)).astype(BF)
    s = jnp.dot(hi, ones_blockdiag, preferred_element_type=F32) + jnp.dot(lo, ones_blockdiag, preferred_element_type=F32)
    return s * (1.0 / HEAD_DIM)


def _headnorm_fwd(name, proj, col0, width, g2):
    T = proj.shape[0]
    wide = min(width, GROUP_WIDTH)
    nb, off = width // wide, col0 // wide

    def body(x_ref, g_ref, b_ref, o_ref):
        for s in range(wide // LANES):
            lanes = slice(LANES * s, LANES * (s + 1))
            xv = x_ref[:, lanes].astype(F32)
            r = lax.rsqrt(_head_mean(xv * xv, b_ref[...]) + RMS_EPS)
            o_ref[:, lanes] = (xv * r * g_ref[...]).astype(BF)

    return pl.pallas_call(
        body, name=name, grid=(T // HEADNORM_TILE, nb),
        in_specs=[pl.BlockSpec((HEADNORM_TILE, wide), lambda i, j: (i, j + off)),
                  pl.BlockSpec((1, LANES), lambda i, j: (0, 0)), pl.BlockSpec((LANES, LANES), lambda i, j: (0, 0))],
        out_specs=pl.BlockSpec((HEADNORM_TILE, wide), lambda i, j: (i, j)),
        out_shape=jax.ShapeDtypeStruct((T, width), BF), compiler_params=_params(("parallel", "parallel")),
    )(proj, g2, _half_sum_matrix())


def _headnorm_bwd(name, dy, proj, col0, width, g2):
    T = proj.shape[0]
    wide = min(width, GROUP_WIDTH)
    nb, off = width // wide, col0 // wide

    def body(dy_ref, x_ref, g_ref, b_ref, dx_ref, dg_ref):
        for s in range(wide // LANES):
            lanes = slice(LANES * s, LANES * (s + 1))
            xv = x_ref[:, lanes].astype(F32)
            dyv = dy_ref[:, lanes].astype(F32)
            r = lax.rsqrt(_head_mean(xv * xv, b_ref[...]) + RMS_EPS)
            xhat = xv * r
            dxhat = dyv * g_ref[...]
            dx_ref[:, lanes] = (r * (dxhat - xhat * _head_mean(dxhat * xhat, b_ref[...]))).astype(BF)
            dg_ref[0, :, lanes] = jnp.sum(dyv * xhat, axis=0, keepdims=True)

    return pl.pallas_call(
        body, name=name, grid=(T // HEADNORM_TILE, nb),
        in_specs=[pl.BlockSpec((HEADNORM_TILE, wide), lambda i, j: (i, j)),
                  pl.BlockSpec((HEADNORM_TILE, wide), lambda i, j: (i, j + off)),
                  pl.BlockSpec((1, LANES), lambda i, j: (0, 0)), pl.BlockSpec((LANES, LANES), lambda i, j: (0, 0))],
        out_specs=[pl.BlockSpec((HEADNORM_TILE, wide), lambda i, j: (i, j)),
                   pl.BlockSpec((1, 1, wide), lambda i, j: (i, 0, j))],
        out_shape=[jax.ShapeDtypeStruct((T, width), BF), jax.ShapeDtypeStruct((T // HEADNORM_TILE, 1, width), F32)],
        compiler_params=_params(("parallel", "parallel")),
    )(dy, proj, g2, _half_sum_matrix())


def _shift_down(v, k, row):
    return jnp.where(row >= k, pltpu.roll(v, k, axis=0), 0.0)


def _shift_up(v, k, row, T):
    return jnp.where(row < T - k, pltpu.roll(v, T - k, axis=0), 0.0)


def _by_group(g, vals):
    out = vals[-1]
    for i in range(len(vals) - 2, -1, -1):
        out = jnp.where(g == i, vals[i], out)
    return out


def _pool_fwd(name, proj, pool_w, pool_scale):
    T = proj.shape[0]

    def body(x_ref, w_ref, s_ref, pooled_ref, mixed_ref):
        g = pl.program_id(0)
        xv = x_ref[...].astype(F32)
        row = lax.broadcasted_iota(jnp.int32, (T, 1), 0)
        s2 = xv + _shift_down(xv, 1, row)
        s4 = s2 + _shift_down(s2, 2, row)
        s8 = s4 + _shift_down(s4, 4, row)
        s16 = s8 + _shift_down(s8, 8, row)
        wsum = _by_group(g, [s2, s4, s8, s16])
        count = jnp.minimum(row + 1, 2 << g).astype(F32)
        pooled = (wsum / count - xv).astype(BF)
        pooled_ref[...] = pooled
        mixed = jnp.dot(pooled, w_ref[0].astype(BF), preferred_element_type=F32) * s_ref[...]
        mixed_ref[...] = mixed.astype(BF)

    col = pl.BlockSpec((T, POOL_GROUP), lambda g: (0, g))
    return pl.pallas_call(
        body, name=name, grid=(N_POOL_GROUPS,),
        in_specs=[col, pl.BlockSpec((1, POOL_GROUP, POOL_GROUP), lambda g: (g, 0, 0)),
                  pl.BlockSpec((1, POOL_GROUP), lambda g: (0, g))],
        out_specs=[col, col],
        out_shape=[jax.ShapeDtypeStruct((T, POOL_WIDTH), BF), jax.ShapeDtypeStruct((T, POOL_WIDTH), BF)],
        compiler_params=_params(("parallel",)),
    )(proj, pool_w, pool_scale)


def _pool_bwd(name, dmixed, pooled, pool_w, pool_scale):
    T = dmixed.shape[0]

    def body(dm_ref, p_ref, w_ref, s_ref, dx_ref, dw_ref, ds_ref):
        g = pl.program_id(0)
        dm = dm_ref[...].astype(F32)
        pooled = p_ref[...]
        w = w_ref[0].astype(BF)
        pre = jnp.dot(pooled, w, preferred_element_type=F32)
        ds_ref[...] = jnp.sum(dm * pre, axis=0, keepdims=True)
        dms = (dm * s_ref[...]).astype(BF)
        dw_ref[0] = lax.dot_general(pooled, dms, _DIMS["tn"], preferred_element_type=F32)
        dpooled = lax.dot_general(dms, w, _DIMS["nt"], preferred_element_type=F32)
        row = lax.broadcasted_iota(jnp.int32, (T, 1), 0)
        count = jnp.minimum(row + 1, 2 << g).astype(F32)
        z = dpooled / count
        l2 = z + _shift_up(z, 1, row, T)
        l4 = l2 + _shift_up(l2, 2, row, T)
        l8 = l4 + _shift_up(l4, 4, row, T)
        l16 = l8 + _shift_up(l8, 8, row, T)
        dx_ref[...] = (_by_group(g, [l2, l4, l8, l16]) - dpooled).astype(BF)

    col = pl.BlockSpec((T, POOL_GROUP), lambda g: (0, g))
    wspec = pl.BlockSpec((1, POOL_GROUP, POOL_GROUP), lambda g: (g, 0, 0))
    sspec = pl.BlockSpec((1, POOL_GROUP), lambda g: (0, g))
    return pl.pallas_call(
        body, name=name, grid=(N_POOL_GROUPS,), in_specs=[col, col, wspec, sspec], out_specs=[col, wspec, sspec],
        out_shape=[jax.ShapeDtypeStruct((T, POOL_WIDTH), BF),
                   jax.ShapeDtypeStruct((N_POOL_GROUPS, POOL_GROUP, POOL_GROUP), F32),
                   jax.ShapeDtypeStruct((1, POOL_WIDTH), F32)],
        compiler_params=_params(("parallel",)),
    )(dmixed, pooled, pool_w, pool_scale)


ATTN_SCALE = HEAD_DIM ** -0.5
MASKED = float(jnp.finfo(jnp.float32).min)
KV_COL_BLOCK_K = COL_K // LANES
KV_COL_BLOCK_V = COL_V // LANES
GROUP_WIDTH = GQA_GROUP * HEAD_DIM


def _dup_head(v, j):
    half = lax.broadcasted_iota(jnp.int32, (1, LANES), 1) // HEAD_DIM
    return jnp.where(half == j, v, pltpu.roll(v, HEAD_DIM, axis=1))


def _stack_heads(v, low):
    pieces = []
    for p in range(GROUP_WIDTH // LANES):
        vp = v[:, LANES * p: LANES * (p + 1)]
        pieces.append(jnp.where(low, vp, jnp.zeros_like(vp)))
        pieces.append(jnp.where(low, jnp.zeros_like(vp), vp))
    return jnp.concatenate(pieces, axis=0)


def _unstack_heads(st, low):
    pieces = []
    for p in range(GROUP_WIDTH // LANES):
        even = st[BLOCK * (2 * p): BLOCK * (2 * p + 1)]
        odd = st[BLOCK * (2 * p + 1): BLOCK * (2 * p + 2)]
        pieces.append(jnp.where(low, even, odd))
    return jnp.concatenate(pieces, axis=1)


def _band_mask(n):
    row = lax.broadcasted_iota(jnp.int32, (BLOCK, 2 * BLOCK), 0)
    col = lax.broadcasted_iota(jnp.int32, (BLOCK, 2 * BLOCK), 1)
    return (col > row) & (col <= row + BLOCK) & ((n > 0) | (col >= BLOCK))


def _softmax_parts(s, valid, sink_ref, j):
    es, invs, sinks = [], [], []
    for h in range(GQA_GROUP):
        sh = jnp.where(valid, s[BLOCK * h: BLOCK * (h + 1)], MASKED)
        sink = sink_ref[j * GQA_GROUP + h]
        m = jnp.maximum(jnp.max(sh, axis=1, keepdims=True), sink)
        e = jnp.exp(sh - m)
        e_sink = jnp.exp(sink - m)
        es.append(e)
        invs.append(1.0 / (jnp.sum(e, axis=1, keepdims=True) + e_sink))
        sinks.append(e_sink)
    return jnp.concatenate(es, axis=0), jnp.concatenate(invs, axis=0), jnp.concatenate(sinks, axis=0)


def _attn_fwd(name, qn, kn, proj, sinks, comm=None):
    T = qn.shape[0]
    nb = T // BLOCK
    plumb = _CommPlumbing(comm)

    def body(sink_ref, q_ref, kp_ref, kc_ref, vp_ref, vc_ref, *rest):
        c_in, o_ref = rest[:plumb.n_in], rest[plumb.n_in]
        c_out, c_scr = rest[plumb.n_in + 1: plumb.n_in + 1 + plumb.n_out], rest[plumb.n_in + 1 + plumb.n_out:]
        n = pl.program_id(0)
        plumb.run(n, nb, True, c_in, c_out, c_scr)
        low = lax.broadcasted_iota(jnp.int32, (1, LANES), 1) < HEAD_DIM
        kk = jnp.concatenate([kp_ref[...], kc_ref[...]], axis=0)
        vv = jnp.concatenate([vp_ref[...], vc_ref[...]], axis=0)
        valid = _band_mask(n)
        for j in range(2):
            lanes = slice(GROUP_WIDTH * j, GROUP_WIDTH * (j + 1))
            s = lax.dot_general(_stack_heads(q_ref[:, lanes], low), _dup_head(kk, j), _DIMS["nt"],
                                preferred_element_type=F32)
            e, inv, _ = _softmax_parts(s, valid, sink_ref, j)
            o = jnp.dot(e.astype(BF), _dup_head(vv, j), preferred_element_type=F32) * inv
            o_ref[:, lanes] = _unstack_heads(o, low).astype(BF)
        plumb.run(n, nb, False, c_in, c_out, c_scr)

    wide = pl.BlockSpec((BLOCK, ATTN_WIDTH), lambda n: (n, 0))
    res = pl.pallas_call(
        body, name=name, grid=(nb,),
        in_specs=[pl.BlockSpec(memory_space=pltpu.SMEM), wide,
                  pl.BlockSpec((BLOCK, LANES), lambda n: (jnp.maximum(n - 1, 0), 0)),
                  pl.BlockSpec((BLOCK, LANES), lambda n: (n, 0)),
                  pl.BlockSpec((BLOCK, LANES), lambda n: (jnp.maximum(n - 1, 0), KV_COL_BLOCK_V)),
                  pl.BlockSpec((BLOCK, LANES), lambda n: (n, KV_COL_BLOCK_V))] + [ANY] * plumb.n_in,
        out_specs=[wide] + [ANY] * plumb.n_out,
        out_shape=[jax.ShapeDtypeStruct((T, ATTN_WIDTH), BF)] + plumb.out_shapes, scratch_shapes=plumb.scratch,
        compiler_params=_params(("arbitrary",) if comm else ("parallel",)),
    )(sinks, qn, kn, kn, proj, proj, *plumb.args)
    return (res[0], plumb.split_outputs(res[1:])) if comm is not None else res[0]


def _attn_bwd(name, dout, qn, kn, proj, sinks):
    T = qn.shape[0]
    nb = T // BLOCK

    def body(sink_ref, do_ref, q_ref, kp_ref, kc_ref, vp_ref, vc_ref, dq_ref, dk_ref, dv_ref, dsink_ref,
             carry_k, carry_v, tot_k, tot_v):
        n = pl.program_id(0)
        lane = lax.broadcasted_iota(jnp.int32, (1, LANES), 1)
        low = lane < HEAD_DIM

        @pl.when(n == 0)
        def _():
            carry_k[...] = jnp.zeros_like(carry_k)
            carry_v[...] = jnp.zeros_like(carry_v)
            dsink_ref[...] = jnp.zeros_like(dsink_ref)

        @pl.when(n == nb)
        def _():
            tot_k[...] = jnp.zeros_like(tot_k)
            tot_v[...] = jnp.zeros_like(tot_v)

        @pl.when(n < nb)
        def _():
            kk = jnp.concatenate([kp_ref[...], kc_ref[...]], axis=0)
            vv = jnp.concatenate([vp_ref[...], vc_ref[...]], axis=0)
            valid = _band_mask(n)
            dk_tot = jnp.zeros((2 * BLOCK, LANES), F32)
            dv_tot = jnp.zeros((2 * BLOCK, LANES), F32)
            dsink = jnp.zeros((1, LANES), F32)
            for j in range(2):
                k2 = _dup_head(kk, j)
                v2 = _dup_head(vv, j)
                q = _stack_heads(q_ref[:, GROUP_WIDTH * j: GROUP_WIDTH * (j + 1)], low)
                do = _stack_heads(do_ref[:, GROUP_WIDTH * j: GROUP_WIDTH * (j + 1)], low)
                s = lax.dot_general(q, k2, _DIMS["nt"], preferred_element_type=F32)
                e, inv, e_sink = _softmax_parts(s, valid, sink_ref, j)
                p = e * inv
                psink = e_sink * inv
                dp = lax.dot_general(do, v2, _DIMS["nt"], preferred_element_type=F32)
                delta = jnp.sum(p * dp, axis=1, keepdims=True)
                ds = (p * (dp - delta)).astype(BF)
                dq_ref[:, GROUP_WIDTH * j: GROUP_WIDTH * (j + 1)] = _unstack_heads(
                    jnp.dot(ds, k2, preferred_element_type=F32), low).astype(BF)
                dk2 = lax.dot_general(ds, q, _DIMS["tn"], preferred_element_type=F32)
                dv2 = lax.dot_general(p.astype(BF), do, _DIMS["tn"], preferred_element_type=F32)
                mine = low if j == 0 else jnp.logical_not(low)
                dk_tot = dk_tot + jnp.where(mine, dk2 + pltpu.roll(dk2, HEAD_DIM, axis=1), 0.0)
                dv_tot = dv_tot + jnp.where(mine, dv2 + pltpu.roll(dv2, HEAD_DIM, axis=1), 0.0)
                sink_term = psink * delta
                for h in range(GQA_GROUP):
                    val = -jnp.sum(sink_term[BLOCK * h: BLOCK * (h + 1)], axis=0, keepdims=True)
                    dsink = dsink + jnp.where(lane == j * GQA_GROUP + h, val, 0.0)
            tot_k[...] = dk_tot
            tot_v[...] = dv_tot
            dsink_ref[0:1, :] += dsink

        dk_ref[...] = (carry_k[...] + tot_k[0:BLOCK]).astype(BF)
        dv_ref[...] = (carry_v[...] + tot_v[0:BLOCK]).astype(BF)
        carry_k[...] = tot_k[BLOCK:]
        carry_v[...] = tot_v[BLOCK:]

    cur = lambda n: (jnp.minimum(n, nb - 1), 0)
    prev = lambda n: (jnp.maximum(n - 1, 0), 0)
    wide = pl.BlockSpec((BLOCK, ATTN_WIDTH), cur)
    return pl.pallas_call(
        body, name=name, grid=(nb + 1,),
        in_specs=[pl.BlockSpec(memory_space=pltpu.SMEM), wide, wide,
                  pl.BlockSpec((BLOCK, LANES), prev), pl.BlockSpec((BLOCK, LANES), cur),
                  pl.BlockSpec((BLOCK, LANES), lambda n: (jnp.maximum(n - 1, 0), KV_COL_BLOCK_V)),
                  pl.BlockSpec((BLOCK, LANES), lambda n: (jnp.minimum(n, nb - 1), KV_COL_BLOCK_V))],
        out_specs=[wide, pl.BlockSpec((BLOCK, LANES), prev), pl.BlockSpec((BLOCK, LANES), prev),
                   pl.BlockSpec((8, LANES), lambda n: (0, 0))],
        out_shape=[jax.ShapeDtypeStruct((T, ATTN_WIDTH), BF), jax.ShapeDtypeStruct((T, KV_WIDTH), BF),
                   jax.ShapeDtypeStruct((T, KV_WIDTH), BF), jax.ShapeDtypeStruct((8, LANES), F32)],
        scratch_shapes=[pltpu.VMEM((BLOCK, LANES), F32), pltpu.VMEM((BLOCK, LANES), F32),
                        pltpu.VMEM((2 * BLOCK, LANES), F32), pltpu.VMEM((2 * BLOCK, LANES), F32)],
        compiler_params=_params(("arbitrary",)),
    )(sinks, dout, qn, kn, kn, proj, proj)


def _swiglu_fwd_epilogue(accs, ex):
    g, u = accs
    return [g, u, g * jax.nn.sigmoid(g) * u], []


def _swiglu_bwd_epilogue(accs, ex):
    (da,) = accs
    g, u = ex[0].astype(F32), ex[1].astype(F32)
    s = jax.nn.sigmoid(g)
    silu = g * s
    return [da * u * (s * (1.0 + g * (1.0 - s))), da * silu, silu * u], []


def _half_residual_epilogue(accs, ex):
    return [ex[0] + 0.5 * accs[0]], []


def _residual_epilogue(accs, ex):
    return [ex[0] + accs[0]], []


def _rms_bwd_epilogue(accs, ex):
    (dn,) = accs
    xv, g, dres = ex
    r = lax.rsqrt(jnp.mean(xv * xv, axis=-1, keepdims=True) + RMS_EPS)
    xhat = xv * r
    dxhat = dn * g
    dx = dres + r * (dxhat - xhat * jnp.mean(dxhat * xhat, axis=-1, keepdims=True))
    return [dx, dx], [dn * xhat]


def _loss_epilogue(accs, ex):
    xv, target = ex
    d = xv + 0.5 * accs[0] - target
    dy = d * (1.0 / D_MODEL)
    return [dy, dy], [d * d]


def _merge_fwd_epilogue(accs, ex):
    (ba,) = accs
    bp, gp_pre, ga_pre, bias_p, bias_a = ex
    gp = jax.nn.sigmoid(gp_pre.astype(F32) + bias_p)
    ga = jax.nn.sigmoid(ga_pre.astype(F32) + bias_a)
    return [gp * bp.astype(F32) + ga * ba, ba], []


def _merge_bwd_epilogue(accs, ex):
    (dm,) = accs
    bp, ba, gp_pre, ga_pre, bias_p, bias_a = ex
    gp = jax.nn.sigmoid(gp_pre.astype(F32) + bias_p)
    ga = jax.nn.sigmoid(ga_pre.astype(F32) + bias_a)
    dgp = dm * bp.astype(F32) * gp * (1.0 - gp)
    dga = dm * ba.astype(F32) * ga * (1.0 - ga)
    return [dm * gp, dm * ga, dgp, dga], [dgp, dga]


def _prep(name, ws, transposes):
    n = len(ws)

    def body(*refs):
        for w_ref, o_ref, tr in zip(refs[:n], refs[n:], transposes):
            v = w_ref[...]
            o_ref[...] = (v.T if tr else v).astype(BF)

    shapes = [jax.ShapeDtypeStruct(w.shape[::-1] if tr else w.shape, BF) for w, tr in zip(ws, transposes)]
    return pl.pallas_call(body, name=name, out_shape=shapes, compiler_params=_params())(*ws)


def _adam_math(w, g, m, v):
    m = ADAM_B1 * m + (1.0 - ADAM_B1) * g
    v = ADAM_B2 * v + (1.0 - ADAM_B2) * jnp.square(g)
    m_hat = m / (1.0 - ADAM_B1 ** ADAM_STEP)
    v_hat = v / (1.0 - ADAM_B2 ** ADAM_STEP)
    delta = -ADAM_LR * (m_hat / (jnp.sqrt(v_hat) + ADAM_EPS) + ADAM_WD * w)
    return delta, m, v


def _adamw_sharded(name, slots, w, m, v, transpose):
    def body(s_ref, w_ref, m_ref, v_ref, g_out, d_out, m_out, v_out):
        g = s_ref[0].astype(F32)
        for i in range(1, 4):
            g = g + s_ref[i].astype(F32)
        if transpose:
            g = g.T
        delta, mn, vn = _adam_math(w_ref[...], g, m_ref[...], v_ref[...])
        g_out[...] = g
        d_out[...] = delta
        m_out[...] = mn
        v_out[...] = vn

    out_shape = [jax.ShapeDtypeStruct(w.shape, F32)] * 4
    _, r, C = slots.shape
    rows = r // 4
    if transpose or rows % 8:
        return pl.pallas_call(body, name=name, out_shape=out_shape, compiler_params=_params())(slots, w, m, v)
    tile = pl.BlockSpec((rows, C), lambda i: (i, 0))
    return pl.pallas_call(
        body, name=name, grid=(4,), in_specs=[pl.BlockSpec((4, rows, C), lambda i: (0, i, 0)), tile, tile, tile],
        out_specs=[tile] * 4, out_shape=out_shape, compiler_params=_params(("parallel",)),
    )(slots, w, m, v)


SMALL_LAYOUT = (("ffn1_norm", 0, (8, LANES)), ("mix_norm", 8, (8, LANES)), ("ffn2_norm", 16, (8, LANES)),
                ("gate_bias", 24, (16, LANES)), ("pool_scale", 40, (4, LANES)), ("q_norm", 48, (1, HEAD_DIM)),
                ("k_norm", 56, (1, HEAD_DIM)), ("sinks", 64, (1, N_HEADS)))
LOSS_ROW = 72
SMALL_ROWS = 80


def _adamw_small(name, g_vec, g_pool_w, params):
    n = len(SMALL_LAYOUT) + 1

    def body(vec_ref, pw_ref, *refs):
        ins, outs = refs[:3 * n], refs[3 * n:]
        vec = vec_ref[0]
        pw = pw_ref[0]
        for i in range(1, N_DEV):
            vec = vec + vec_ref[i]
            pw = pw + pw_ref[i]
        grads = [vec[r0:r0 + shape[0], 0:shape[1]] for _, r0, shape in SMALL_LAYOUT] + [pw]
        for p, g in enumerate(grads):
            w_ref, m_ref, v_ref = ins[3 * p: 3 * p + 3]
            delta, mn, vn = _adam_math(w_ref[...], g, m_ref[...], v_ref[...])
            for o_ref, val in zip(outs[4 * p: 4 * p + 4], (g, delta, mn, vn)):
                o_ref[...] = val
        outs[4 * n][...] = vec[LOSS_ROW:LOSS_ROW + 1, :]

    flat = [a for wmv in params for a in wmv]
    out_shape = [jax.ShapeDtypeStruct(wmv[0].shape, F32) for wmv in params for _ in range(4)]
    out_shape.append(jax.ShapeDtypeStruct((1, LANES), F32))
    res = pl.pallas_call(body, name=name, out_shape=out_shape, compiler_params=_params())(g_vec, g_pool_w, *flat)
    return [tuple(res[4 * p: 4 * p + 4]) for p in range(n)], res[4 * n]


def _place():
    x, y, c = lax.axis_index("x"), lax.axis_index("y"), lax.axis_index("c")
    other_chips = [(1 - x, y), (x, 1 - y), (1 - x, 1 - y)]
    return x, y, c, other_chips


def _rows(ref, r, place, natural=False):
    px, py, pc = place
    b = 4 * px + 2 * py + pc if natural else 4 * pc + 2 * px + py
    return ref.at[pl.ds(pl.multiple_of(b * r, 8), r), :]


def _gather_task(shards, natural=(), forward_at=0.75):
    n = len(shards)
    rs = [s.shape[0] for s in shards]
    rows_of = lambda ref, k, place: _rows(ref, rs[k], place, k in natural)

    def copy(scr, outs, k, slot, block, to, src=None):
        rows = rows_of(outs[k], k, block)
        return pltpu.make_async_remote_copy(
            src_ref=rows if src is None else src, dst_ref=rows, send_sem=scr[0].at[7 * k + slot],
            recv_sem=scr[1].at[7 * k + slot], device_id=to, device_id_type=MESH)

    def first_sends(ins, outs, scr):
        x, y, c, chips = _place()
        me = (x, y, c)
        cps = [copy(scr, outs, k, 1 + j, me, (*chip, c), src=ins[k]) for j, chip in enumerate(chips) for k in range(n)]
        return cps + [copy(scr, outs, k, 0, me, (x, y, 1 - c), src=ins[k]) for k in range(n)]

    def passed_on(outs, scr):
        x, y, c, chips = _place()
        return [copy(scr, outs, k, 4 + j, (*chip, c), (x, y, 1 - c)) for j, chip in enumerate(chips) for k in range(n)]

    def local(ins, outs, scr):
        x, y, c, _ = _place()
        return [pltpu.make_async_copy(ins[k], rows_of(outs[k], k, (x, y, c)), scr[2].at[k]) for k in range(n)]

    def start(ins, outs, scr):
        for cp in local(ins, outs, scr) + first_sends(ins, outs, scr):
            cp.start()

    def forward(ins, outs, scr):
        x, y, c, chips = _place()
        for j, chip in enumerate(chips):
            for k in range(n):
                copy(scr, outs, k, 1 + j, (*chip, c), (x, y, c)).wait_recv()
        for cp in passed_on(outs, scr):
            cp.start()

    def finish(ins, outs, scr):
        x, y, c, chips = _place()
        for k in range(n):
            copy(scr, outs, k, 0, (x, y, 1 - c), (x, y, c)).wait_recv()
        for j, chip in enumerate(chips):
            for k in range(n):
                copy(scr, outs, k, 4 + j, (*chip, 1 - c), (x, y, c)).wait_recv()
        for cp in first_sends(ins, outs, scr) + passed_on(outs, scr):
            cp.wait_send()
        for cp in local(ins, outs, scr):
            cp.wait()

    out_shapes = [jax.ShapeDtypeStruct((N_DEV * s.shape[0], s.shape[1]), s.dtype) for s in shards]
    scratch = [pltpu.SemaphoreType.DMA((7 * n,)), pltpu.SemaphoreType.DMA((7 * n,)), pltpu.SemaphoreType.DMA((n,))]
    return _Task(shards, out_shapes, scratch, [(0, start), (forward_at, forward), (1.0, finish)])


def _all_gather(name, shards, natural=()):
    return _comm_only(name, [_gather_task(shards, natural)])[0]


def _chip_task(sums):
    n = len(sums)
    rs = [s.shape[0] // 4 for s in sums]

    def block(ref, k, chip_index):
        return ref.at[pl.ds(pl.multiple_of(chip_index * rs[k], 8), rs[k]), :]

    def copies(ins, outs, scr):
        send_sems, recv_sems, local_sems = scr
        x, y, c, chips = _place()
        here = 2 * x + y
        local = [pltpu.make_async_copy(block(ins[k], k, here), outs[k].at[here], local_sems.at[k]) for k in range(n)]
        remote = []
        for j, (px, py) in enumerate(chips):
            remote += [pltpu.make_async_remote_copy(
                src_ref=block(ins[k], k, 2 * px + py), dst_ref=outs[k].at[here],
                send_sem=send_sems.at[3 * k + j], recv_sem=recv_sems.at[3 * k + j],
                device_id=(px, py, c), device_id_type=MESH) for k in range(n)]
        return local, remote

    def start(ins, outs, scr):
        local, remote = copies(ins, outs, scr)
        for cp in local + remote:
            cp.start()

    def finish(ins, outs, scr):
        local, remote = copies(ins, outs, scr)
        for cp in remote:
            cp.wait()
        for cp in local:
            cp.wait()

    out_shapes = [jax.ShapeDtypeStruct((4, r, s.shape[1]), s.dtype) for r, s in zip(rs, sums)]
    scratch = [pltpu.SemaphoreType.DMA((3 * n,)), pltpu.SemaphoreType.DMA((3 * n,)), pltpu.SemaphoreType.DMA((n,))]
    return _Task(sums, out_shapes, scratch, [(0, start), (1.0, finish)])


def _dw_pair(name, a, b, scale, comm=None, blocks=1):
    T, M = a.shape
    N = b.shape[1]
    half = M // 2
    wide = half // blocks
    tk = min(2048, T)
    nK = T // tk
    plumb = _CommPlumbing(comm)

    def body(core_ref, *rest):
        a_refs, b_ref, rest = rest[:blocks], rest[blocks], rest[blocks + 1:]
        c_in = rest[:plumb.n_in]
        o_ref = rest[plumb.n_in]
        c_out = rest[plumb.n_in + 1: plumb.n_in + 1 + plumb.n_out]
        acc, stage, land, send_sem, recv_sem = rest[plumb.n_in + 1 + plumb.n_out: plumb.n_in + 6 + plumb.n_out]
        c_scr = rest[plumb.n_in + 6 + plumb.n_out:]
        i, k = pl.program_id(0), pl.program_id(1)
        x, y, c, _ = _place()
        push = pltpu.make_async_remote_copy(src_ref=stage, dst_ref=land, send_sem=send_sem, recv_sem=recv_sem,
                                            device_id=(x, y, 1 - c), device_id_type=MESH)
        if comm:
            plumb.run(i * nK + k, 2 * nK, True, c_in, c_out, c_scr)

        av = a_refs[0][...] if blocks == 1 else jnp.concatenate([r[...] for r in a_refs], axis=1)
        p = lax.dot_general(av, b_ref[...], _DIMS["tn"], preferred_element_type=F32)

        @pl.when(k == 0)
        def _():
            acc[...] = p

        @pl.when(k > 0)
        def _():
            acc[...] += p

        @pl.when((i == 0) & (k == nK - 1))
        def _():
            stage[...] = (scale * acc[...]).astype(BF)
            push.start()

        @pl.when((i == 1) & (k == nK - 1))
        def _():
            push.wait_recv()
            o_ref[...] = (scale * acc[...] + land[...].astype(F32)).astype(BF)
            push.wait_send()

        if comm:
            plumb.run(i * nK + k, 2 * nK, False, c_in, c_out, c_scr)

    grid_spec = pltpu.PrefetchScalarGridSpec(
        num_scalar_prefetch=1, grid=(2, nK),
        in_specs=[pl.BlockSpec((tk, wide), functools.partial(
            lambda i, k, core, j: (k, (2 * j if blocks > 1 else 0) + jnp.where(i == 0, 1 - core[0], core[0])), j=j))
            for j in range(blocks)] + [pl.BlockSpec((tk, N), lambda i, k, core: (k, 0))] + [ANY] * plumb.n_in,
        out_specs=[pl.BlockSpec((half, N), lambda i, k, core: (0, 0))] + [ANY] * plumb.n_out,
        scratch_shapes=[pltpu.VMEM((half, N), F32), pltpu.VMEM((half, N), BF), pltpu.VMEM((half, N), BF),
                        pltpu.SemaphoreType.DMA, pltpu.SemaphoreType.DMA] + plumb.scratch)
    core = lax.axis_index("c").astype(jnp.int32).reshape(1)
    res = pl.pallas_call(
        body, name=name, grid_spec=grid_spec,
        out_shape=[jax.ShapeDtypeStruct((half, N), BF)] + plumb.out_shapes,
        compiler_params=_params(("arbitrary", "arbitrary")),
    )(core, *([a] * blocks), b, *plumb.args)
    return (res[0], plumb.split_outputs(res[1:])) if comm else res[0]


def _pair_exchange(name, parts):
    n = len(parts)

    def body(*refs):
        ins, outs = refs[:n], refs[n:2 * n]
        send_sems, recv_sems = refs[2 * n:]
        x, y, c, _ = _place()
        copies = [pltpu.make_async_remote_copy(
            src_ref=ins[k].at[:, pl.ds(1 - c, 1)], dst_ref=outs[k], send_sem=send_sems.at[k], recv_sem=recv_sems.at[k],
            device_id=(x, y, 1 - c), device_id_type=MESH) for k in range(n)]
        for cp in copies:
            cp.start()
        for cp in copies:
            cp.wait()

    return pl.pallas_call(
        body, name=name, in_specs=[ANY] * n, out_specs=[ANY] * n,
        out_shape=[jax.ShapeDtypeStruct((4, 1) + p.shape[2:], p.dtype) for p in parts],
        scratch_shapes=[pltpu.SemaphoreType.DMA((n,)), pltpu.SemaphoreType.DMA((n,))],
        compiler_params=pltpu.CompilerParams(has_side_effects=True),
    )(*parts)


def _pair_sum(name, part, got, core):
    _, _, r, C = part.shape

    def body(core_ref, p_ref, g_ref, o_ref):
        o_ref[0] = (p_ref[0, 0].astype(F32) + g_ref[0, 0].astype(F32)).astype(o_ref.dtype)

    return pl.pallas_call(
        body, name=name,
        grid_spec=pltpu.PrefetchScalarGridSpec(
            num_scalar_prefetch=1, grid=(4,),
            in_specs=[pl.BlockSpec((1, 1, r, C), lambda i, core_ref: (i, core_ref[0], 0, 0)),
                      pl.BlockSpec((1, 1, r, C), lambda i, core_ref: (i, 0, 0, 0))],
            out_specs=pl.BlockSpec((1, r, C), lambda i, core_ref: (i, 0, 0))),
        out_shape=jax.ShapeDtypeStruct((4, r, C), part.dtype), compiler_params=_params(("parallel",)),
    )(core, part, got)


def _ffn_bwd(tag, dy, dyb, x, gain, wgT, wuT, wd, saved, pending):
    n, g, u = saved
    half = lambda accs, ex: _swiglu_bwd_epilogue([0.5 * accs[0]], ex)
    (dg, du, a), done0 = _mm(tag + "_d_act", [(dyb, wd, "nt", 0)], [BF, BF, BF], tm=512, tn=1408, tk=D_MODEL,
                             epilogue=half, extras=[(g, "tile", 0), (u, "tile", 0)], comm=pending, cols_outer=True)
    sum_d = _dw_pair(tag + "_dw_down", a, dyb, 0.5)
    sum_g, (slots_d,) = _dw_pair(tag + "_dw_gate", dg, n, 1.0, comm=[_chip_task([sum_d])])
    sum_u, (slots_g,) = _dw_pair(tag + "_dw_up", du, n, 1.0, comm=[_chip_task([sum_g])])
    (dx, dxb, dgain), (slots_u,) = _mm(
        tag + "_d_norm", [(dg, wgT, "nn", 0), (du, wuT, "nn", 0)], [F32, BF], tm=512, tn=D_MODEL, tk=D_FF,
        epilogue=_rms_bwd_epilogue, extras=[(x, "tile", 0), (gain, "row", 0), (dy, "tile", 0)], n_colsum=1,
        comm=[_chip_task([sum_u])])
    return dx, dxb, dgain, done0, slots_g[0], slots_u[0], slots_d[0]


def _tile_gain(g):
    return jnp.concatenate([g, g]).reshape(1, LANES)


def _fold_heads(partials):
    return jnp.sum(partials.reshape(-1, HEAD_DIM), axis=0)


def _pack_small_grads(grads, loss_local):
    pieces, row = [], 0
    for name, r0, _ in SMALL_LAYOUT + (("loss", LOSS_ROW, None),):
        v = (loss_local if name == "loss" else grads[name]).reshape(-1)
        rows = -(-v.size // LANES)
        block = jnp.pad(v, (0, rows * LANES - v.size)).reshape(rows, LANES)
        pieces += [jnp.zeros((r0 - row, LANES), F32)] * (r0 > row) + [block]
        row = r0 + rows
    pieces.append(jnp.zeros((SMALL_ROWS - row, LANES), F32))
    return jnp.concatenate(pieces, axis=0)


def kernel(x, ffn1_norm, ffn1_w_gate, ffn1_w_up, ffn1_w_down, mix_norm, w_in, pool_w, pool_scale, w_pool_out, q_norm, k_norm, sinks, w_attn_out, gate_bias, w_out, ffn2_norm, ffn2_w_gate, ffn2_w_up, ffn2_w_down, loss_target, m_ffn1_norm, m_ffn1_w_gate, m_ffn1_w_up, m_ffn1_w_down, m_mix_norm, m_w_in, m_pool_w, m_pool_scale, m_w_pool_out, m_q_norm, m_k_norm, m_sinks, m_w_attn_out, m_gate_bias, m_w_out, m_ffn2_norm, m_ffn2_w_gate, m_ffn2_w_up, m_ffn2_w_down, v_ffn1_norm, v_ffn1_w_gate, v_ffn1_w_up, v_ffn1_w_down, v_mix_norm, v_w_in, v_pool_w, v_pool_scale, v_w_pool_out, v_q_norm, v_k_norm, v_sinks, v_w_attn_out, v_gate_bias, v_w_out, v_ffn2_norm, v_ffn2_w_gate, v_ffn2_w_up, v_ffn2_w_down):
    T = x.shape[1]
    x2 = x.reshape(T, D_MODEL)
    target = loss_target.reshape(T, D_MODEL)

    big = [
        ("ffn1_w_gate", ffn1_w_gate, m_ffn1_w_gate, v_ffn1_w_gate, True, False),
        ("ffn1_w_up", ffn1_w_up, m_ffn1_w_up, v_ffn1_w_up, True, False),
        ("ffn1_w_down", ffn1_w_down, m_ffn1_w_down, v_ffn1_w_down, False, False),
        ("w_in", w_in, m_w_in, v_w_in, True, False),
        ("w_pool_out", w_pool_out, m_w_pool_out, v_w_pool_out, False, True),
        ("w_attn_out", w_attn_out, m_w_attn_out, v_w_attn_out, False, False),
        ("w_out", w_out, m_w_out, v_w_out, False, False),
        ("ffn2_w_gate", ffn2_w_gate, m_ffn2_w_gate, v_ffn2_w_gate, True, False),
        ("ffn2_w_up", ffn2_w_up, m_ffn2_w_up, v_ffn2_w_up, True, False),
        ("ffn2_w_down", ffn2_w_down, m_ffn2_w_down, v_ffn2_w_down, False, False),
    ]
    view = lambda a, tv: a.T if tv else a
    shards = _prep("prep_weights", [view(w, tv) for _, w, _, _, tv, _ in big], [tk_ for *_, tk_ in big])
    wg1T, wu1T = _all_gather("gather_ffn1_gate_up", shards[0:2])

    g1 = ffn1_norm.reshape(1, D_MODEL)
    g2 = mix_norm.reshape(1, D_MODEL)
    g3 = ffn2_norm.reshape(1, D_MODEL)
    bias_row = gate_bias.reshape(1, 2 * D_MODEL)
    qg, kg = _tile_gain(q_norm) * ATTN_SCALE, _tile_gain(k_norm)
    scale_row = pool_scale.reshape(1, POOL_WIDTH)

    n1 = _rms_fwd("ffn1_norm", x2, g1)
    (gt1, up1, act1), ((wd1,), (w_inT,)) = _mm(
        "ffn1_gate_up", [(n1, wg1T, "nt", 0), (n1, wu1T, "nt", 1)], [BF, BF, BF], tm=512, tn=1408, tk=D_MODEL,
        epilogue=_swiglu_fwd_epilogue, cols_outer=True,
        comm=[_gather_task(shards[2:3], forward_at=0.5), _gather_task(shards[3:4], natural=(0,), forward_at=0.9)])
    (h1,) = _mm("ffn1_down", [(act1, wd1, "nn", 0)], [F32], tm=512, tn=D_MODEL, tk=D_FF,
                epilogue=_half_residual_epilogue, extras=[(x2, "tile", 0)])
    saved1 = (n1, gt1, up1)
    u = _rms_fwd("mix_norm", h1, g2)
    (proj,), ((w_poT, w_ao, w_o),) = _mm(
        "in_proj", [(u, w_inT, "nt", 0)], [BF], tm=512, tn=1280, tk=D_MODEL, cols_outer=True,
        comm=[_gather_task(shards[4:7], natural=(0, 1, 2), forward_at=0.8)])
    pooled, mixed = _pool_fwd("pool_fwd", proj, pool_w, scale_row)
    qn = _headnorm_fwd("q_norm", proj, COL_Q, ATTN_WIDTH, qg)
    kn = _headnorm_fwd("k_norm", proj, COL_K, KV_WIDTH, kg)
    attn, ((wg2T, wu2T),) = _attn_fwd("attn_fwd", qn, kn, proj, sinks,
                                      comm=[_gather_task(shards[7:9], forward_at=0.85)])
    (bp,) = _mm("pool_out", [(mixed, w_poT, "nt", 0)], [BF], tm=1024, tn=D_MODEL, tk=POOL_WIDTH)
    gate_tn = 256
    gate_extras = [(proj, "tile", COL_GP // gate_tn), (proj, "tile", COL_GA // gate_tn),
                   (bias_row, "row", 0), (bias_row, "row", D_MODEL // gate_tn)]
    merged, ba = _mm("attn_out_merge", [(attn, w_ao, "nn", 0)], [BF, BF], tm=2048, tn=gate_tn, tk=ATTN_WIDTH,
                     epilogue=_merge_fwd_epilogue, extras=[(bp, "tile", 0)] + gate_extras)
    (h2,) = _mm("mix_out", [(merged, w_o, "nn", 0)], [F32], tm=512, tn=D_MODEL, tk=D_MODEL,
                epilogue=_residual_epilogue, extras=[(h1, "tile", 0)])
    n2 = _rms_fwd("ffn2_norm", h2, g3)
    (gt2, up2, act2), ((wd2,),) = _mm(
        "ffn2_gate_up", [(n2, wg2T, "nt", 0), (n2, wu2T, "nt", 1)], [BF, BF, BF], tm=512, tn=1408, tk=D_MODEL,
        epilogue=_swiglu_fwd_epilogue, cols_outer=True, comm=[_gather_task(shards[9:10], forward_at=0.8)])
    dy, dyb, sq = _mm("ffn2_down_loss", [(act2, wd2, "nn", 0)], [F32, BF], tm=512, tn=D_MODEL, tk=D_FF,
                      epilogue=_loss_epilogue, extras=[(h2, "tile", 0), (target, "tile", 0)], n_colsum=1)
    loss_local = 0.5 * jnp.sum(sq) / D_MODEL

    dh2, dh2b, dg3, _, slots_g2, slots_u2, slots_d2 = _ffn_bwd(
        "ffn2", dy, dyb, h2, g3, wg2T, wu2T, wd2, (n2, gt2, up2), [])
    dbp, dba, dgp, dga, cs_gp, cs_ga = _mm(
        "mix_out_bwd", [(dh2b, w_o, "nt", 0)], [BF, BF, BF, BF], tm=2048, tn=gate_tn, tk=D_MODEL,
        epilogue=_merge_bwd_epilogue, extras=[(bp, "tile", 0), (ba, "tile", 0)] + gate_extras, n_colsum=2)
    sum_o = _dw_pair("dw_out", merged, dh2b, 1.0, blocks=4)
    (dmixed,), ((slots_o,),) = _mm("pool_out_bwd", [(dbp, w_poT, "nn", 0)], [BF], tm=1024, tn=POOL_WIDTH, tk=D_MODEL,
                                   comm=[_chip_task([sum_o])])
    sum_po = _dw_pair("dw_pool_out", dbp, mixed, 1.0, blocks=4)
    (dattn,), ((slots_po,),) = _mm("attn_out_bwd", [(dba, w_ao, "nt", 0)], [BF], tm=1024, tn=ATTN_WIDTH, tk=D_MODEL,
                                   comm=[_chip_task([sum_po])])
    sum_ao = _dw_pair("dw_attn_out", attn, dba, 1.0, blocks=4)
    dxp, dpool_w, dpool_scale = _pool_bwd("pool_bwd", dmixed, pooled, pool_w, scale_row)
    dqn, dkn, dv, dsink_tile = _attn_bwd("attn_bwd", dattn, qn, kn, proj, sinks)
    dq, dqg = _headnorm_bwd("q_norm_bwd", dqn, proj, COL_Q, ATTN_WIDTH, qg)
    dk, dkg = _headnorm_bwd("k_norm_bwd", dkn, proj, COL_K, KV_WIDTH, kg)
    dproj = jnp.concatenate([dxp, dq, dk, dv, dgp, dga], axis=1)
    (dh1, dh1b, dg2), ((slots_ao,),) = _mm(
        "in_proj_bwd", [(dproj, w_inT, "nn", 0)], [F32, BF], tm=512, tn=D_MODEL, tk=IN_WIDTH, epilogue=_rms_bwd_epilogue,
        extras=[(h1, "tile", 0), (g2, "row", 0), (dh2, "tile", 0)], n_colsum=1, comm=[_chip_task([sum_ao])])
    (dw_inT,) = _mm("dw_in", [(dproj, u, "tn", 0)], [BF], tm=1280, tn=D_MODEL, tk=2048)
    part_in = dw_inT.reshape(4, 2, IN_WIDTH // N_DEV, D_MODEL)
    (got_in,) = _pair_exchange("pair_exchange_w_in", [part_in])
    core = lax.axis_index("c").astype(jnp.int32).reshape(1)
    sum_in = _pair_sum("pair_sum_w_in", part_in, got_in, core).reshape(IN_WIDTH // 2, D_MODEL)
    dx, _, dg1, ((slots_in,),), slots_g1, slots_u1, slots_d1 = _ffn_bwd(
        "ffn1", dh1, dh1b, x2, g1, wg1T, wu1T, wd1, saved1, [_chip_task([sum_in])])

    slots = [slots_g1, slots_u1, slots_d1, slots_in, slots_po, slots_ao, slots_o, slots_g2, slots_u2, slots_d2]
    big_out = {}
    for k, (nm, w, m, v, tv, tk_) in enumerate(big):
        res = _adamw_sharded("adamw_" + nm, slots[k], view(w, tv), view(m, tv), view(v, tv), tk_)
        big_out[nm] = tuple(view(r, tv) for r in res)

    small_grads = {
        "ffn1_norm": jnp.sum(dg1, axis=(0, 1)), "mix_norm": jnp.sum(dg2, axis=(0, 1)), "ffn2_norm": jnp.sum(dg3, axis=(0, 1)),
        "gate_bias": jnp.concatenate([jnp.sum(cs_gp, axis=(0, 1)), jnp.sum(cs_ga, axis=(0, 1))]),
        "pool_scale": dpool_scale, "q_norm": _fold_heads(dqg) * ATTN_SCALE, "k_norm": _fold_heads(dkg),
        "sinks": dsink_tile[0, :N_HEADS]}
    g_vec, g_pool_w = _all_gather("gather_small_grads", [_pack_small_grads(small_grads, loss_local),
                                                         dpool_w.reshape(-1, LANES)])
    given = {"ffn1_norm": (ffn1_norm, m_ffn1_norm, v_ffn1_norm), "mix_norm": (mix_norm, m_mix_norm, v_mix_norm),
             "ffn2_norm": (ffn2_norm, m_ffn2_norm, v_ffn2_norm), "gate_bias": (gate_bias, m_gate_bias, v_gate_bias),
             "pool_scale": (pool_scale, m_pool_scale, v_pool_scale), "q_norm": (q_norm, m_q_norm, v_q_norm),
             "k_norm": (k_norm, m_k_norm, v_k_norm), "sinks": (sinks, m_sinks, v_sinks)}
    params = [tuple(a.reshape(shape) for a in given[nm]) for nm, _, shape in SMALL_LAYOUT]
    params.append(tuple(a.reshape(-1, LANES) for a in (pool_w, m_pool_w, v_pool_w)))
    small_res, loss_row = _adamw_small("adamw_small", g_vec.reshape(N_DEV, SMALL_ROWS, LANES),
                                       g_pool_w.reshape(N_DEV, -1, LANES), params)
    small_out = {nm: tuple(r.reshape(given[nm][0].shape) for r in res)
                 for (nm, _, _), res in zip(SMALL_LAYOUT, small_res)}
    small_out["pool_w"] = tuple(r.reshape(pool_w.shape) for r in small_res[-1])
    loss = loss_row[0, 0]

    order = ["ffn1_norm", "ffn1_w_gate", "ffn1_w_up", "ffn1_w_down", "mix_norm", "w_in", "pool_w", "pool_scale",
             "w_pool_out", "q_norm", "k_norm", "sinks", "w_attn_out", "gate_bias", "w_out", "ffn2_norm",
             "ffn2_w_gate", "ffn2_w_up", "ffn2_w_down"]
    every = {**big_out, **small_out}
    outs = [loss, dx.reshape(x.shape)]
    for j in range(4):
        outs += [every[nm][j] for nm in order]
    return tuple(outs)
```

```python
import functools

import jax
import jax.numpy as jnp
from jax import lax
from jax.experimental import pallas as pl
from jax.experimental.pallas import tpu as pltpu

BF = jnp.bfloat16
F32 = jnp.float32

D_MODEL = 1024
D_FF = 2816
POOL_WIDTH = 512
POOL_GROUP = 128
N_POOL_GROUPS = 4
HEAD_DIM = 64
N_HEADS = 16
GQA_GROUP = 8
BLOCK = 128
ATTN_WIDTH = 1024
KV_WIDTH = 128
IN_WIDTH = 3840
RMS_EPS = 1e-6
N_DEV = 8
LANES = 128

COL_Q = POOL_WIDTH
COL_K = COL_Q + ATTN_WIDTH
COL_V = COL_K + KV_WIDTH
COL_GP = COL_V + KV_WIDTH
COL_GA = COL_GP + D_MODEL

ADAM_LR = 0.001
ADAM_B1 = 0.9
ADAM_B2 = 0.999
ADAM_EPS = 1e-08
ADAM_WD = 0.01
ADAM_STEP = 10

VMEM_LIMIT_V7X = 56 * 1024 * 1024
MESH = pl.DeviceIdType.MESH
ANY = pl.BlockSpec(memory_space=pl.ANY)


def _params(sem=None):
    return pltpu.CompilerParams(dimension_semantics=sem, vmem_limit_bytes=VMEM_LIMIT_V7X)


_DIMS = {"nt": (((1,), (1,)), ((), ())), "nn": (((1,), (0,)), ((), ())), "tn": (((0,), (0,)), ((), ()))}


class _Task:
    def __init__(self, inputs, out_shapes, scratch, phases):
        self.inputs, self.out_shapes, self.scratch = list(inputs), list(out_shapes), list(scratch)
        self.phases = list(phases)


class _CommPlumbing:
    def __init__(self, tasks):
        self.tasks = list(tasks or [])
        self.args = [a for t in self.tasks for a in t.inputs]
        self.out_shapes = [o for t in self.tasks for o in t.out_shapes]
        self.scratch = [s for t in self.tasks for s in t.scratch]
        self.n_in, self.n_out = len(self.args), len(self.out_shapes)

    def _slices(self, c_in, c_out, c_scr):
        i = o = s = 0
        for t in self.tasks:
            yield t, c_in[i:i + len(t.inputs)], c_out[o:o + len(t.out_shapes)], c_scr[s:s + len(t.scratch)]
            i, o, s = i + len(t.inputs), o + len(t.out_shapes), s + len(t.scratch)

    def run(self, step, steps, before, c_in, c_out, c_scr):
        for t, ins, outs, scr in self._slices(c_in, c_out, c_scr):
            for frac, fn in t.phases:
                if step is None:
                    fn(ins, outs, scr)
                elif before == (frac == 0):
                    at = 0 if frac == 0 else max(0, min(steps, -(-int(round(frac * steps * 64)) // 64)) - 1)
                    pl.when(step == at)(functools.partial(fn, ins, outs, scr))

    def split_outputs(self, flat):
        res, o = [], 0
        for t in self.tasks:
            res.append(list(flat[o:o + len(t.out_shapes)]))
            o += len(t.out_shapes)
        return res


def _comm_only(name, tasks):
    plumb = _CommPlumbing(tasks)

    def body(*refs):
        c_in, c_out = refs[:plumb.n_in], refs[plumb.n_in: plumb.n_in + plumb.n_out]
        c_scr = refs[plumb.n_in + plumb.n_out:]
        plumb.run(None, 1, True, c_in, c_out, c_scr)

    res = pl.pallas_call(
        body, name=name, in_specs=[ANY] * plumb.n_in, out_specs=[ANY] * plumb.n_out, out_shape=plumb.out_shapes,
        scratch_shapes=plumb.scratch, compiler_params=pltpu.CompilerParams(has_side_effects=True),
    )(*plumb.args)
    return plumb.split_outputs(res)


def _mm(name, terms, out_dtypes, *, tm, tn, tk, epilogue=None, extras=(), n_colsum=0, comm=None, cols_outer=False):
    a0, b0, mode0, _ = terms[0]
    if mode0 == "nt":
        (M, K), N = a0.shape, b0.shape[0]
    elif mode0 == "nn":
        (M, K), N = a0.shape, b0.shape[1]
    else:
        (K, M), N = a0.shape, b0.shape[1]
    tm, tn, tk = min(tm, M), min(tn, N), min(tk, K)
    assert M % tm == 0 and N % tn == 0 and K % tk == 0, (name, M, N, K, tm, tn, tk)
    nI, nJ, nK = M // tm, N // tn, K // tk
    n_terms = len(terms)
    n_acc = max(t[3] for t in terms) + 1
    n_ex = len(extras)
    n_out = len(out_dtypes)
    if epilogue is None:
        epilogue = lambda accs, ex: ([accs[0]], [])
    plumb = _CommPlumbing(comm)
    n_scr = n_acc if nK > 1 else 0
    grid = (nJ, nI, nK) if cols_outer else (nI, nJ, nK)

    def body(*refs):
        n_in = 2 * n_terms + n_ex
        ab = refs[: 2 * n_terms]
        ex_refs = refs[2 * n_terms: n_in]
        c_in = refs[n_in: n_in + plumb.n_in]
        o0 = n_in + plumb.n_in
        out_refs = refs[o0: o0 + n_out]
        cs_refs = refs[o0 + n_out: o0 + n_out + n_colsum]
        c_out = refs[o0 + n_out + n_colsum: o0 + n_out + n_colsum + plumb.n_out]
        s0 = o0 + n_out + n_colsum + plumb.n_out
        acc_refs = refs[s0: s0 + n_scr]
        c_scr = refs[s0 + n_scr:]
        steps = grid[0] * grid[1] * nK
        if comm:
            step = (pl.program_id(0) * grid[1] + pl.program_id(1)) * nK + pl.program_id(2)
            plumb.run(step, steps, True, c_in, c_out, c_scr)

        def products():
            accs = [None] * n_acc
            for t, (_, _, mode, ai) in enumerate(terms):
                p = lax.dot_general(ab[2 * t][...], ab[2 * t + 1][...], _DIMS[mode], preferred_element_type=F32)
                accs[ai] = p if accs[ai] is None else accs[ai] + p
            return accs

        def finish(accs):
            outs, colsums = epilogue(accs, [r[...] for r in ex_refs])
            for r, o in zip(out_refs, outs):
                r[...] = o.astype(r.dtype)
            for r, cs in zip(cs_refs, colsums):
                r[...] = jnp.sum(cs, axis=0, keepdims=True).reshape(r.shape)

        if nK == 1:
            finish(products())
        else:
            k = pl.program_id(2)
            accs = products()

            @pl.when(k == 0)
            def _():
                for r, a in zip(acc_refs, accs):
                    r[...] = a

            @pl.when(k > 0)
            def _():
                for r, a in zip(acc_refs, accs):
                    r[...] += a

            @pl.when(k == nK - 1)
            def _():
                finish([r[...] for r in acc_refs])

        if comm:
            plumb.run(step, steps, False, c_in, c_out, c_scr)

    def spec(block, index, fixed=False):
        imap = (lambda q, p, k: index(p, q, k)) if cols_outer else index
        return pl.BlockSpec(block, imap, pipeline_mode=pl.Buffered(1)) if fixed else pl.BlockSpec(block, imap)

    in_specs, args = [], []
    for a, b, mode, _ in terms:
        if mode == "nt":
            in_specs += [spec((tm, tk), lambda i, j, k: (i, k), nI * nK == 1),
                         spec((tn, tk), lambda i, j, k: (j, k), nJ * nK == 1)]
        elif mode == "nn":
            in_specs += [spec((tm, tk), lambda i, j, k: (i, k), nI * nK == 1),
                         spec((tk, tn), lambda i, j, k: (k, j), nJ * nK == 1)]
        else:
            in_specs += [spec((tk, tm), lambda i, j, k: (k, i), nI * nK == 1),
                         spec((tk, tn), lambda i, j, k: (k, j), nJ * nK == 1)]
        args += [a, b]
    for arr, kind, off in extras:
        if kind == "tile":
            in_specs.append(spec((tm, tn), functools.partial(lambda i, j, k, off: (i, j + off), off=off)))
        else:
            in_specs.append(spec((1, tn), functools.partial(lambda i, j, k, off: (0, j + off), off=off)))
        args.append(arr)
    out_shape = [jax.ShapeDtypeStruct((M, N), dt) for dt in out_dtypes]
    out_specs = [spec((tm, tn), lambda i, j, k: (i, j)) for _ in out_dtypes]
    out_shape += [jax.ShapeDtypeStruct((nI, 1, N), F32) for _ in range(n_colsum)]
    out_specs += [spec((1, 1, tn), lambda i, j, k: (i, 0, j)) for _ in range(n_colsum)]
    scratch = [pltpu.VMEM((tm, tn), F32) for _ in range(n_scr)]
    args += plumb.args
    in_specs += [ANY] * plumb.n_in
    out_shape += plumb.out_shapes
    out_specs += [ANY] * plumb.n_out
    sem = ("arbitrary",) * 3 if comm else ("parallel", "parallel", "arbitrary")
    res = pl.pallas_call(
        body, name=name, grid=grid, in_specs=in_specs, out_specs=out_specs, out_shape=out_shape,
        scratch_shapes=scratch + plumb.scratch, compiler_params=_params(sem),
    )(*args)
    n_own = n_out + n_colsum
    return (list(res[:n_own]), plumb.split_outputs(res[n_own:])) if comm is not None else res


ROW_TILE = 512


def _rms_fwd(name, x, g):
    T, D = x.shape

    def body(x_ref, g_ref, o_ref):
        xv = x_ref[...]
        r = lax.rsqrt(jnp.mean(xv * xv, axis=-1, keepdims=True) + RMS_EPS)
        o_ref[...] = (xv * r * g_ref[...]).astype(BF)

    return pl.pallas_call(
        body, name=name, grid=(T // ROW_TILE,),
        in_specs=[pl.BlockSpec((ROW_TILE, D), lambda i: (i, 0)), pl.BlockSpec((1, D), lambda i: (0, 0))],
        out_specs=pl.BlockSpec((ROW_TILE, D), lambda i: (i, 0)),
        out_shape=jax.ShapeDtypeStruct((T, D), BF), compiler_params=_params(("parallel",)),
    )(x, g)


HEADNORM_TILE = 1024


def _half_sum_matrix():
    r = lax.broadcasted_iota(jnp.int32, (LANES, LANES), 0) // HEAD_DIM
    c = lax.broadcasted_iota(jnp.int32, (LANES, LANES), 1) // HEAD_DIM
    return (r == c).astype(BF)


def _head_mean(v, ones_blockdiag):
    hi = v.astype(BF)
    lo = (v - hi.astype(F32)).astype(BF)
    s = jnp.dot(hi, ones_blockdiag, preferred_element_type=F32) + jnp.dot(lo, ones_blockdiag, preferred_element_type=F32)
    return s * (1.0 / HEAD_DIM)


def _headnorm_fwd(name, proj, col0, width, g2):
    T = proj.shape[0]
    wide = min(width, GROUP_WIDTH)
    nb, off = width // wide, col0 // wide

    def body(x_ref, g_ref, b_ref, o_ref):
        for s in range(wide // LANES):
            lanes = slice(LANES * s, LANES * (s + 1))
            xv = x_ref[:, lanes].astype(F32)
            r = lax.rsqrt(_head_mean(xv * xv, b_ref[...]) + RMS_EPS)
            o_ref[:, lanes] = (xv * r * g_ref[...]).astype(BF)

    return pl.pallas_call(
        body, name=name, grid=(T // HEADNORM_TILE, nb),
        in_specs=[pl.BlockSpec((HEADNORM_TILE, wide), lambda i, j: (i, j + off)),
                  pl.BlockSpec((1, LANES), lambda i, j: (0, 0)), pl.BlockSpec((LANES, LANES), lambda i, j: (0, 0))],
        out_specs=pl.BlockSpec((HEADNORM_TILE, wide), lambda i, j: (i, j)),
        out_shape=jax.ShapeDtypeStruct((T, width), BF), compiler_params=_params(("parallel", "parallel")),
    )(proj, g2, _half_sum_matrix())


def _headnorm_bwd(name, dy, proj, col0, width, g2):
    T = proj.shape[0]
    wide = min(width, GROUP_WIDTH)
    nb, off = width // wide, col0 // wide

    def body(dy_ref, x_ref, g_ref, b_ref, dx_ref, dg_ref):
        for s in range(wide // LANES):
            lanes = slice(LANES * s, LANES * (s + 1))
            xv = x_ref[:, lanes].astype(F32)
            dyv = dy_ref[:, lanes].astype(F32)
            r = lax.rsqrt(_head_mean(xv * xv, b_ref[...]) + RMS_EPS)
            xhat = xv * r
            dxhat = dyv * g_ref[...]
            dx_ref[:, lanes] = (r * (dxhat - xhat * _head_mean(dxhat * xhat, b_ref[...]))).astype(BF)
            dg_ref[0, :, lanes] = jnp.sum(dyv * xhat, axis=0, keepdims=True)

    return pl.pallas_call(
        body, name=name, grid=(T // HEADNORM_TILE, nb),
        in_specs=[pl.BlockSpec((HEADNORM_TILE, wide), lambda i, j: (i, j)),
                  pl.BlockSpec((HEADNORM_TILE, wide), lambda i, j: (i, j + off)),
                  pl.BlockSpec((1, LANES), lambda i, j: (0, 0)), pl.BlockSpec((LANES, LANES), lambda i, j: (0, 0))],
        out_specs=[pl.BlockSpec((HEADNORM_TILE, wide), lambda i, j: (i, j)),
                   pl.BlockSpec((1, 1, wide), lambda i, j: (i, 0, j))],
        out_shape=[jax.ShapeDtypeStruct((T, width), BF), jax.ShapeDtypeStruct((T // HEADNORM_TILE, 1, width), F32)],
        compiler_params=_params(("parallel", "parallel")),
    )(dy, proj, g2, _half_sum_matrix())


def _shift_down(v, k, row):
    return jnp.where(row >= k, pltpu.roll(v, k, axis=0), 0.0)


def _shift_up(v, k, row, T):
    return jnp.where(row < T - k, pltpu.roll(v, T - k, axis=0), 0.0)


def _by_group(g, vals):
    out = vals[-1]
    for i in range(len(vals) - 2, -1, -1):
        out = jnp.where(g == i, vals[i], out)
    return out


def _pool_fwd(name, proj, pool_w, pool_scale):
    T = proj.shape[0]

    def body(x_ref, w_ref, s_ref, pooled_ref, mixed_ref):
        g = pl.program_id(0)
        xv = x_ref[...].astype(F32)
        row = lax.broadcasted_iota(jnp.int32, (T, 1), 0)
        s2 = xv + _shift_down(xv, 1, row)
        s4 = s2 + _shift_down(s2, 2, row)
        s8 = s4 + _shift_down(s4, 4, row)
        s16 = s8 + _shift_down(s8, 8, row)
        wsum = _by_group(g, [s2, s4, s8, s16])
        count = jnp.minimum(row + 1, 2 << g).astype(F32)
        pooled = (wsum / count - xv).astype(BF)
        pooled_ref[...] = pooled
        mixed = jnp.dot(pooled, w_ref[0].astype(BF), preferred_element_type=F32) * s_ref[...]
        mixed_ref[...] = mixed.astype(BF)

    col = pl.BlockSpec((T, POOL_GROUP), lambda g: (0, g))
    return pl.pallas_call(
        body, name=name, grid=(N_POOL_GROUPS,),
        in_specs=[col, pl.BlockSpec((1, POOL_GROUP, POOL_GROUP), lambda g: (g, 0, 0)),
                  pl.BlockSpec((1, POOL_GROUP), lambda g: (0, g))],
        out_specs=[col, col],
        out_shape=[jax.ShapeDtypeStruct((T, POOL_WIDTH), BF), jax.ShapeDtypeStruct((T, POOL_WIDTH), BF)],
        compiler_params=_params(("parallel",)),
    )(proj, pool_w, pool_scale)


def _pool_bwd(name, dmixed, pooled, pool_w, pool_scale):
    T = dmixed.shape[0]

    def body(dm_ref, p_ref, w_ref, s_ref, dx_ref, dw_ref, ds_ref):
        g = pl.program_id(0)
        dm = dm_ref[...].astype(F32)
        pooled = p_ref[...]
        w = w_ref[0].astype(BF)
        pre = jnp.dot(pooled, w, preferred_element_type=F32)
        ds_ref[...] = jnp.sum(dm * pre, axis=0, keepdims=True)
        dms = (dm * s_ref[...]).astype(BF)
        dw_ref[0] = lax.dot_general(pooled, dms, _DIMS["tn"], preferred_element_type=F32)
        dpooled = lax.dot_general(dms, w, _DIMS["nt"], preferred_element_type=F32)
        row = lax.broadcasted_iota(jnp.int32, (T, 1), 0)
        count = jnp.minimum(row + 1, 2 << g).astype(F32)
        z = dpooled / count
        l2 = z + _shift_up(z, 1, row, T)
        l4 = l2 + _shift_up(l2, 2, row, T)
        l8 = l4 + _shift_up(l4, 4, row, T)
        l16 = l8 + _shift_up(l8, 8, row, T)
        dx_ref[...] = (_by_group(g, [l2, l4, l8, l16]) - dpooled).astype(BF)

    col = pl.BlockSpec((T, POOL_GROUP), lambda g: (0, g))
    wspec = pl.BlockSpec((1, POOL_GROUP, POOL_GROUP), lambda g: (g, 0, 0))
    sspec = pl.BlockSpec((1, POOL_GROUP), lambda g: (0, g))
    return pl.pallas_call(
        body, name=name, grid=(N_POOL_GROUPS,), in_specs=[col, col, wspec, sspec], out_specs=[col, wspec, sspec],
        out_shape=[jax.ShapeDtypeStruct((T, POOL_WIDTH), BF),
                   jax.ShapeDtypeStruct((N_POOL_GROUPS, POOL_GROUP, POOL_GROUP), F32),
                   jax.ShapeDtypeStruct((1, POOL_WIDTH), F32)],
        compiler_params=_params(("parallel",)),
    )(dmixed, pooled, pool_w, pool_scale)


ATTN_SCALE = HEAD_DIM ** -0.5
MASKED = float(jnp.finfo(jnp.float32).min)
KV_COL_BLOCK_K = COL_K // LANES
KV_COL_BLOCK_V = COL_V // LANES
GROUP_WIDTH = GQA_GROUP * HEAD_DIM


def _dup_head(v, j):
    half = lax.broadcasted_iota(jnp.int32, (1, LANES), 1) // HEAD_DIM
    return jnp.where(half == j, v, pltpu.roll(v, HEAD_DIM, axis=1))


def _stack_heads(v, low):
    pieces = []
    for p in range(GROUP_WIDTH // LANES):
        vp = v[:, LANES * p: LANES * (p + 1)]
        pieces.append(jnp.where(low, vp, jnp.zeros_like(vp)))
        pieces.append(jnp.where(low, jnp.zeros_like(vp), vp))
    return jnp.concatenate(pieces, axis=0)


def _unstack_heads(st, low):
    pieces = []
    for p in range(GROUP_WIDTH // LANES):
        even = st[BLOCK * (2 * p): BLOCK * (2 * p + 1)]
        odd = st[BLOCK * (2 * p + 1): BLOCK * (2 * p + 2)]
        pieces.append(jnp.where(low, even, odd))
    return jnp.concatenate(pieces, axis=1)


def _band_mask(n):
    row = lax.broadcasted_iota(jnp.int32, (BLOCK, 2 * BLOCK), 0)
    col = lax.broadcasted_iota(jnp.int32, (BLOCK, 2 * BLOCK), 1)
    return (col > row) & (col <= row + BLOCK) & ((n > 0) | (col >= BLOCK))


def _softmax_heads(s, valid, sink_ref, j):
    ps, psinks = [], []
    for h in range(GQA_GROUP):
        sh = jnp.where(valid, s[BLOCK * h: BLOCK * (h + 1)], MASKED)
        sink = sink_ref[j * GQA_GROUP + h]
        m = jnp.maximum(jnp.max(sh, axis=1, keepdims=True), sink)
        e = jnp.exp(sh - m)
        es = jnp.exp(sink - m)
        inv = 1.0 / (jnp.sum(e, axis=1, keepdims=True) + es)
        ps.append(e * inv)
        psinks.append(es * inv)
    return jnp.concatenate(ps, axis=0), jnp.concatenate(psinks, axis=0)


def _attn_fwd(name, qn, kn, proj, sinks, comm=None):
    T = qn.shape[0]
    nb = T // BLOCK
    plumb = _CommPlumbing(comm)

    def body(sink_ref, q_ref, kp_ref, kc_ref, vp_ref, vc_ref, *rest):
        c_in, o_ref = rest[:plumb.n_in], rest[plumb.n_in]
        c_out, c_scr = rest[plumb.n_in + 1: plumb.n_in + 1 + plumb.n_out], rest[plumb.n_in + 1 + plumb.n_out:]
        n, j = pl.program_id(0), pl.program_id(1)
        plumb.run(2 * n + j, 2 * nb, True, c_in, c_out, c_scr)
        low = lax.broadcasted_iota(jnp.int32, (1, LANES), 1) < HEAD_DIM
        k2 = _dup_head(jnp.concatenate([kp_ref[...], kc_ref[...]], axis=0), j)
        v2 = _dup_head(jnp.concatenate([vp_ref[...], vc_ref[...]], axis=0), j)
        s = lax.dot_general(_stack_heads(q_ref[...], low), k2, _DIMS["nt"], preferred_element_type=F32)
        p, _ = _softmax_heads(s, _band_mask(n), sink_ref, j)
        o = jnp.dot(p.astype(BF), v2, preferred_element_type=F32)
        o_ref[...] = _unstack_heads(o, low).astype(BF)
        plumb.run(2 * n + j, 2 * nb, False, c_in, c_out, c_scr)

    group = pl.BlockSpec((BLOCK, GROUP_WIDTH), lambda n, j: (n, j))
    res = pl.pallas_call(
        body, name=name, grid=(nb, 2),
        in_specs=[pl.BlockSpec(memory_space=pltpu.SMEM), group,
                  pl.BlockSpec((BLOCK, LANES), lambda n, j: (jnp.maximum(n - 1, 0), 0)),
                  pl.BlockSpec((BLOCK, LANES), lambda n, j: (n, 0)),
                  pl.BlockSpec((BLOCK, LANES), lambda n, j: (jnp.maximum(n - 1, 0), KV_COL_BLOCK_V)),
                  pl.BlockSpec((BLOCK, LANES), lambda n, j: (n, KV_COL_BLOCK_V))] + [ANY] * plumb.n_in,
        out_specs=[group] + [ANY] * plumb.n_out,
        out_shape=[jax.ShapeDtypeStruct((T, ATTN_WIDTH), BF)] + plumb.out_shapes, scratch_shapes=plumb.scratch,
        compiler_params=_params(("arbitrary", "arbitrary") if comm else ("parallel", "parallel")),
    )(sinks, qn, kn, kn, proj, proj, *plumb.args)
    return (res[0], plumb.split_outputs(res[1:])) if comm is not None else res[0]


def _attn_bwd(name, dout, qn, kn, proj, sinks):
    T = qn.shape[0]
    nb = T // BLOCK

    def body(sink_ref, do_ref, q_ref, kp_ref, kc_ref, vp_ref, vc_ref, dq_ref, dk_ref, dv_ref, dsink_ref,
             carry_k, carry_v, tot_k, tot_v):
        n = pl.program_id(0)
        lane = lax.broadcasted_iota(jnp.int32, (1, LANES), 1)
        low = lane < HEAD_DIM

        @pl.when(n == 0)
        def _():
            carry_k[...] = jnp.zeros_like(carry_k)
            carry_v[...] = jnp.zeros_like(carry_v)
            dsink_ref[...] = jnp.zeros_like(dsink_ref)

        @pl.when(n == nb)
        def _():
            tot_k[...] = jnp.zeros_like(tot_k)
            tot_v[...] = jnp.zeros_like(tot_v)

        @pl.when(n < nb)
        def _():
            kk = jnp.concatenate([kp_ref[...], kc_ref[...]], axis=0)
            vv = jnp.concatenate([vp_ref[...], vc_ref[...]], axis=0)
            valid = _band_mask(n)
            dk_tot = jnp.zeros((2 * BLOCK, LANES), F32)
            dv_tot = jnp.zeros((2 * BLOCK, LANES), F32)
            dsink = jnp.zeros((1, LANES), F32)
            for j in range(2):
                k2 = _dup_head(kk, j)
                v2 = _dup_head(vv, j)
                q = _stack_heads(q_ref[:, GROUP_WIDTH * j: GROUP_WIDTH * (j + 1)], low)
                do = _stack_heads(do_ref[:, GROUP_WIDTH * j: GROUP_WIDTH * (j + 1)], low)
                s = lax.dot_general(q, k2, _DIMS["nt"], preferred_element_type=F32)
                p, psink = _softmax_heads(s, valid, sink_ref, j)
                dp =lax.dot_general(do, v2, _DIMS["nt"], preferred_element_type=F32)
                delta = jnp.sum(p * dp, axis=1, keepdims=True)
                ds = (p * (dp - delta)).astype(BF)
                dq_ref[:, GROUP_WIDTH * j: GROUP_WIDTH * (j + 1)] = _unstack_heads(
                    jnp.dot(ds, k2, preferred_element_type=F32), low).astype(BF)
                dk2 = lax.dot_general(ds, q, _DIMS["tn"], preferred_element_type=F32)
                dv2 = lax.dot_general(p.astype(BF), do, _DIMS["tn"], preferred_element_type=F32)
                mine = low if j == 0 else jnp.logical_not(low)
                dk_tot = dk_tot + jnp.where(mine, dk2 + pltpu.roll(dk2, HEAD_DIM, axis=1), 0.0)
                dv_tot = dv_tot + jnp.where(mine, dv2 + pltpu.roll(dv2, HEAD_DIM, axis=1), 0.0)
                sink_term = psink * delta
                for h in range(GQA_GROUP):
                    val = -jnp.sum(sink_term[BLOCK * h: BLOCK * (h + 1)], axis=0, keepdims=True)
                    dsink = dsink + jnp.where(lane == j * GQA_GROUP + h, val, 0.0)
            tot_k[...] = dk_tot
            tot_v[...] = dv_tot
            dsink_ref[0:1, :] += dsink

        dk_ref[...] = (carry_k[...] + tot_k[0:BLOCK]).astype(BF)
        dv_ref[...] = (carry_v[...] + tot_v[0:BLOCK]).astype(BF)
        carry_k[...] = tot_k[BLOCK:]
        carry_v[...] = tot_v[BLOCK:]

    cur = lambda n: (jnp.minimum(n, nb - 1), 0)
    prev = lambda n: (jnp.maximum(n - 1, 0), 0)
    wide = pl.BlockSpec((BLOCK, ATTN_WIDTH), cur)
    return pl.pallas_call(
        body, name=name, grid=(nb + 1,),
        in_specs=[pl.BlockSpec(memory_space=pltpu.SMEM), wide, wide,
                  pl.BlockSpec((BLOCK, LANES), prev), pl.BlockSpec((BLOCK, LANES), cur),
                  pl.BlockSpec((BLOCK, LANES), lambda n: (jnp.maximum(n - 1, 0), KV_COL_BLOCK_V)),
                  pl.BlockSpec((BLOCK, LANES), lambda n: (jnp.minimum(n, nb - 1), KV_COL_BLOCK_V))],
        out_specs=[wide, pl.BlockSpec((BLOCK, LANES), prev), pl.BlockSpec((BLOCK, LANES), prev),
                   pl.BlockSpec((8, LANES), lambda n: (0, 0))],
        out_shape=[jax.ShapeDtypeStruct((T, ATTN_WIDTH), BF), jax.ShapeDtypeStruct((T, KV_WIDTH), BF),
                   jax.ShapeDtypeStruct((T, KV_WIDTH), BF), jax.ShapeDtypeStruct((8, LANES), F32)],
        scratch_shapes=[pltpu.VMEM((BLOCK, LANES), F32), pltpu.VMEM((BLOCK, LANES), F32),
                        pltpu.VMEM((2 * BLOCK, LANES), F32), pltpu.VMEM((2 * BLOCK, LANES), F32)],
        compiler_params=_params(("arbitrary",)),
    )(sinks, dout, qn, kn, kn, proj, proj)


def _swiglu_fwd_epilogue(accs, ex):
    g, u = accs
    return [g, u, g * jax.nn.sigmoid(g) * u], []


def _swiglu_bwd_epilogue(accs, ex):
    (da,) = accs
    g, u = ex[0].astype(F32), ex[1].astype(F32)
    s = jax.nn.sigmoid(g)
    silu = g * s
    return [da * u * (s * (1.0 + g * (1.0 - s))), da * silu, silu * u], []


def _half_residual_epilogue(accs, ex):
    return [ex[0] + 0.5 * accs[0]], []


def _residual_epilogue(accs, ex):
    return [ex[0] + accs[0]], []


def _rms_bwd_epilogue(accs, ex):
    (dn,) = accs
    xv, g, dres = ex
    r = lax.rsqrt(jnp.mean(xv * xv, axis=-1, keepdims=True) + RMS_EPS)
    xhat = xv * r
    dxhat = dn * g
    dx = dres + r * (dxhat - xhat * jnp.mean(dxhat * xhat, axis=-1, keepdims=True))
    return [dx, dx], [dn * xhat]


def _loss_epilogue(accs, ex):
    xv, target = ex
    d = xv + 0.5 * accs[0] - target
    dy = d * (1.0 / D_MODEL)
    return [dy, dy], [d * d]


def _merge_fwd_epilogue(accs, ex):
    (ba,) = accs
    bp, gp_pre, ga_pre, bias_p, bias_a = ex
    gp = jax.nn.sigmoid(gp_pre.astype(F32) + bias_p)
    ga = jax.nn.sigmoid(ga_pre.astype(F32) + bias_a)
    return [gp * bp.astype(F32) + ga * ba, ba], []


def _merge_bwd_epilogue(accs, ex):
    (dm,) = accs
    bp, ba, gp_pre, ga_pre, bias_p, bias_a = ex
    gp = jax.nn.sigmoid(gp_pre.astype(F32) + bias_p)
    ga = jax.nn.sigmoid(ga_pre.astype(F32) + bias_a)
    dgp = dm * bp.astype(F32) * gp * (1.0 - gp)
    dga = dm * ba.astype(F32) * ga * (1.0 - ga)
    return [dm * gp, dm * ga, dgp, dga], [dgp, dga]


def _prep(name, ws, transposes):
    n = len(ws)

    def body(*refs):
        for w_ref, o_ref, tr in zip(refs[:n], refs[n:], transposes):
            v = w_ref[...]
            o_ref[...] = (v.T if tr else v).astype(BF)

    shapes = [jax.ShapeDtypeStruct(w.shape[::-1] if tr else w.shape, BF) for w, tr in zip(ws, transposes)]
    return pl.pallas_call(body, name=name, out_shape=shapes, compiler_params=_params())(*ws)


def _adam_math(w, g, m, v):
    m = ADAM_B1 * m + (1.0 - ADAM_B1) * g
    v = ADAM_B2 * v + (1.0 - ADAM_B2) * jnp.square(g)
    m_hat = m / (1.0 - ADAM_B1 ** ADAM_STEP)
    v_hat = v / (1.0 - ADAM_B2 ** ADAM_STEP)
    delta = -ADAM_LR * (m_hat / (jnp.sqrt(v_hat) + ADAM_EPS) + ADAM_WD * w)
    return delta, m, v


def _adamw_sharded(name, slots, w, m, v, transpose):
    def body(s_ref, w_ref, m_ref, v_ref, g_out, d_out, m_out, v_out):
        g = s_ref[0].astype(F32)
        for i in range(1, 4):
            g = g + s_ref[i].astype(F32)
        if transpose:
            g = g.T
        delta, mn, vn = _adam_math(w_ref[...], g, m_ref[...], v_ref[...])
        g_out[...] = g
        d_out[...] = delta
        m_out[...] = mn
        v_out[...] = vn

    out_shape = [jax.ShapeDtypeStruct(w.shape, F32)] * 4
    _, r, C = slots.shape
    rows = r // 4
    if transpose or rows % 8:
        return pl.pallas_call(body, name=name, out_shape=out_shape, compiler_params=_params())(slots, w, m, v)
    tile = pl.BlockSpec((rows, C), lambda i: (i, 0))
    return pl.pallas_call(
        body, name=name, grid=(4,), in_specs=[pl.BlockSpec((4, rows, C), lambda i: (0, i, 0)), tile, tile, tile],
        out_specs=[tile] * 4, out_shape=out_shape, compiler_params=_params(("parallel",)),
    )(slots, w, m, v)


SMALL_LAYOUT = (("ffn1_norm", 0, (8, LANES)), ("mix_norm", 8, (8, LANES)), ("ffn2_norm", 16, (8, LANES)),
                ("gate_bias", 24, (16, LANES)), ("pool_scale", 40, (4, LANES)), ("q_norm", 48, (1, HEAD_DIM)),
                ("k_norm", 56, (1, HEAD_DIM)), ("sinks", 64, (1, N_HEADS)))
LOSS_ROW = 72
SMALL_ROWS = 80


def _adamw_small(name, g_vec, g_pool_w, params):
    n = len(SMALL_LAYOUT) + 1

    def body(vec_ref, pw_ref, *refs):
        ins, outs = refs[:3 * n], refs[3 * n:]
        vec = vec_ref[0]
        pw = pw_ref[0]
        for i in range(1, N_DEV):
            vec = vec + vec_ref[i]
            pw = pw + pw_ref[i]
        grads = [vec[r0:r0 + shape[0], 0:shape[1]] for _, r0, shape in SMALL_LAYOUT] + [pw]
        for p, g in enumerate(grads):
            w_ref, m_ref, v_ref = ins[3 * p: 3 * p + 3]
            delta, mn, vn = _adam_math(w_ref[...], g, m_ref[...], v_ref[...])
            for o_ref, val in zip(outs[4 * p: 4 * p + 4], (g, delta, mn, vn)):
                o_ref[...] = val
        outs[4 * n][...] = vec[LOSS_ROW:LOSS_ROW + 1, :]

    flat = [a for wmv in params for a in wmv]
    out_shape = [jax.ShapeDtypeStruct(wmv[0].shape, F32) for wmv in params for _ in range(4)]
    out_shape.append(jax.ShapeDtypeStruct((1, LANES), F32))
    res = pl.pallas_call(body, name=name, out_shape=out_shape, compiler_params=_params())(g_vec, g_pool_w, *flat)
    return [tuple(res[4 * p: 4 * p + 4]) for p in range(n)], res[4 * n]


def _place():
    x, y, c = lax.axis_index("x"), lax.axis_index("y"), lax.axis_index("c")
    other_chips = [(1 - x, y), (x, 1 - y), (1 - x, 1 - y)]
    return x, y, c, other_chips


def _rows(ref, r, place, natural=False):
    px, py, pc = place
    b = 4 * px + 2 * py + pc if natural else 4 * pc + 2 * px + py
    return ref.at[pl.ds(pl.multiple_of(b * r, 8), r), :]


def _gather_task(shards, natural=(), forward_at=0.75):
    n = len(shards)
    rs = [s.shape[0] for s in shards]
    rows_of = lambda ref, k, place: _rows(ref, rs[k], place, k in natural)

    def copy(scr, outs, k, slot, block, to, src=None):
        rows = rows_of(outs[k], k, block)
        return pltpu.make_async_remote_copy(
            src_ref=rows if src is None else src, dst_ref=rows, send_sem=scr[0].at[7 * k + slot],
            recv_sem=scr[1].at[7 * k + slot], device_id=to, device_id_type=MESH)

    def first_sends(ins, outs, scr):
        x, y, c, chips = _place()
        me = (x, y, c)
        cps = [copy(scr, outs, k, 1 + j, me, (*chip, c), src=ins[k]) for j, chip in enumerate(chips) for k in range(n)]
        return cps + [copy(scr, outs, k, 0, me, (x, y, 1 - c), src=ins[k]) for k in range(n)]

    def passed_on(outs, scr):
        x, y, c, chips = _place()
        return [copy(scr, outs, k, 4 + j, (*chip, c), (x, y, 1 - c)) for j, chip in enumerate(chips) for k in range(n)]

    def local(ins, outs, scr):
        x, y, c, _ = _place()
        return [pltpu.make_async_copy(ins[k], rows_of(outs[k], k, (x, y, c)), scr[2].at[k]) for k in range(n)]

    def start(ins, outs, scr):
        for cp in local(ins, outs, scr) + first_sends(ins, outs, scr):
            cp.start()

    def forward(ins, outs, scr):
        x, y, c, chips = _place()
        for j, chip in enumerate(chips):
            for k in range(n):
                copy(scr, outs, k, 1 + j, (*chip, c), (x, y, c)).wait_recv()
        for cp in passed_on(outs, scr):
            cp.start()

    def finish(ins, outs, scr):
        x, y, c, chips = _place()
        for k in range(n):
            copy(scr, outs, k, 0, (x, y, 1 - c), (x, y, c)).wait_recv()
        for j, chip in enumerate(chips):
            for k in range(n):
                copy(scr, outs, k, 4 + j, (*chip, 1 - c), (x, y, c)).wait_recv()
        for cp in first_sends(ins, outs, scr) + passed_on(outs, scr):
            cp.wait_send()
        for cp in local(ins, outs, scr):
            cp.wait()

    out_shapes = [jax.ShapeDtypeStruct((N_DEV * s.shape[0], s.shape[1]), s.dtype) for s in shards]
    scratch = [pltpu.SemaphoreType.DMA((7 * n,)), pltpu.SemaphoreType.DMA((7 * n,)), pltpu.SemaphoreType.DMA((n,))]
    return _Task(shards, out_shapes, scratch, [(0, start), (forward_at, forward), (1.0, finish)])


def _all_gather(name, shards, natural=()):
    return _comm_only(name, [_gather_task(shards, natural)])[0]


def _chip_task(sums):
    n = len(sums)
    rs = [s.shape[0] // 4 for s in sums]

    def block(ref, k, chip_index):
        return ref.at[pl.ds(pl.multiple_of(chip_index * rs[k], 8), rs[k]), :]

    def copies(ins, outs, scr):
        send_sems, recv_sems, local_sems = scr
        x, y, c, chips = _place()
        here = 2 * x + y
        local = [pltpu.make_async_copy(block(ins[k], k, here), outs[k].at[here], local_sems.at[k]) for k in range(n)]
        remote = []
        for j, (px, py) in enumerate(chips):
            remote += [pltpu.make_async_remote_copy(
                src_ref=block(ins[k], k, 2 * px + py), dst_ref=outs[k].at[here],
                send_sem=send_sems.at[3 * k + j], recv_sem=recv_sems.at[3 * k + j],
                device_id=(px, py, c), device_id_type=MESH) for k in range(n)]
        return local, remote

    def start(ins, outs, scr):
        local, remote = copies(ins, outs, scr)
        for cp in local + remote:
            cp.start()

    def finish(ins, outs, scr):
        local, remote = copies(ins, outs, scr)
        for cp in remote:
            cp.wait()
        for cp in local:
            cp.wait()

    out_shapes = [jax.ShapeDtypeStruct((4, r, s.shape[1]), s.dtype) for r, s in zip(rs, sums)]
    scratch = [pltpu.SemaphoreType.DMA((3 * n,)), pltpu.SemaphoreType.DMA((3 * n,)), pltpu.SemaphoreType.DMA((n,))]
    return _Task(sums, out_shapes, scratch, [(0, start), (1.0, finish)])


def _dw_pair(name, a, b, scale, comm=None, blocks=1):
    T, M = a.shape
    N = b.shape[1]
    half = M // 2
    wide = half // blocks
    tk = min(2048, T)
    nK = T // tk
    plumb = _CommPlumbing(comm)

    def body(core_ref, *rest):
        a_refs, b_ref, rest = rest[:blocks], rest[blocks], rest[blocks + 1:]
        c_in = rest[:plumb.n_in]
        o_ref = rest[plumb.n_in]
        c_out = rest[plumb.n_in + 1: plumb.n_in + 1 + plumb.n_out]
        acc, stage, land, send_sem, recv_sem = rest[plumb.n_in + 1 + plumb.n_out: plumb.n_in + 6 + plumb.n_out]
        c_scr = rest[plumb.n_in + 6 + plumb.n_out:]
        i, k = pl.program_id(0), pl.program_id(1)
        x, y, c, _ = _place()
        push = pltpu.make_async_remote_copy(src_ref=stage, dst_ref=land, send_sem=send_sem, recv_sem=recv_sem,
                                            device_id=(x, y, 1 - c), device_id_type=MESH)
        if comm:
            plumb.run(i * nK + k, 2 * nK, True, c_in, c_out, c_scr)

        av = a_refs[0][...] if blocks == 1 else jnp.concatenate([r[...] for r in a_refs], axis=1)
        p = lax.dot_general(av, b_ref[...], _DIMS["tn"], preferred_element_type=F32)

        @pl.when(k == 0)
        def _():
            acc[...] = p

        @pl.when(k > 0)
        def _():
            acc[...] += p

        @pl.when((i == 0) & (k == nK - 1))
        def _():
            stage[...] = (scale * acc[...]).astype(BF)
            push.start()

        @pl.when((i == 1) & (k == nK - 1))
        def _():
            push.wait_recv()
            o_ref[...] = (scale * acc[...] + land[...].astype(F32)).astype(BF)
            push.wait_send()

        if comm:
            plumb.run(i * nK + k, 2 * nK, False, c_in, c_out, c_scr)

    grid_spec = pltpu.PrefetchScalarGridSpec(
        num_scalar_prefetch=1, grid=(2, nK),
        in_specs=[pl.BlockSpec((tk, wide), functools.partial(
            lambda i, k, core, j: (k, (2 * j if blocks > 1 else 0) + jnp.where(i == 0, 1 - core[0], core[0])), j=j))
            for j in range(blocks)] + [pl.BlockSpec((tk, N), lambda i, k, core: (k, 0))] + [ANY] * plumb.n_in,
        out_specs=[pl.BlockSpec((half, N), lambda i, k, core: (0, 0))] + [ANY] * plumb.n_out,
        scratch_shapes=[pltpu.VMEM((half, N), F32), pltpu.VMEM((half, N), BF), pltpu.VMEM((half, N), BF),
                        pltpu.SemaphoreType.DMA, pltpu.SemaphoreType.DMA] + plumb.scratch)
    core = lax.axis_index("c").astype(jnp.int32).reshape(1)
    res = pl.pallas_call(
        body, name=name, grid_spec=grid_spec,
        out_shape=[jax.ShapeDtypeStruct((half, N), BF)] + plumb.out_shapes,
        compiler_params=_params(("arbitrary", "arbitrary")),
    )(core, *([a] * blocks), b, *plumb.args)
    return (res[0], plumb.split_outputs(res[1:])) if comm else res[0]


def _pair_exchange(name, parts):
    n = len(parts)

    def body(*refs):
        ins, outs = refs[:n], refs[n:2 * n]
        send_sems, recv_sems = refs[2 * n:]
        x, y, c, _ = _place()
        copies = [pltpu.make_async_remote_copy(
            src_ref=ins[k].at[:, pl.ds(1 - c, 1)], dst_ref=outs[k], send_sem=send_sems.at[k], recv_sem=recv_sems.at[k],
            device_id=(x, y, 1 - c), device_id_type=MESH) for k in range(n)]
        for cp in copies:
            cp.start()
        for cp in copies:
            cp.wait()

    return pl.pallas_call(
        body, name=name, in_specs=[ANY] * n, out_specs=[ANY] * n,
        out_shape=[jax.ShapeDtypeStruct((4, 1) + p.shape[2:], p.dtype) for p in parts],
        scratch_shapes=[pltpu.SemaphoreType.DMA((n,)), pltpu.SemaphoreType.DMA((n,))],
        compiler_params=pltpu.CompilerParams(has_side_effects=True),
    )(*parts)


def _pair_sum(name, part, got, core):
    _, _, r, C = part.shape

    def body(core_ref, p_ref, g_ref, o_ref):
        o_ref[0] = (p_ref[0, 0].astype(F32) + g_ref[0, 0].astype(F32)).astype(o_ref.dtype)

    return pl.pallas_call(
        body, name=name,
        grid_spec=pltpu.PrefetchScalarGridSpec(
            num_scalar_prefetch=1, grid=(4,),
            in_specs=[pl.BlockSpec((1, 1, r, C), lambda i, core_ref: (i, core_ref[0], 0, 0)),
                      pl.BlockSpec((1, 1, r, C), lambda i, core_ref: (i, 0, 0, 0))],
            out_specs=pl.BlockSpec((1, r, C), lambda i, core_ref: (i, 0, 0))),
        out_shape=jax.ShapeDtypeStruct((4, r, C), part.dtype), compiler_params=_params(("parallel",)),
    )(core, part, got)


def _ffn_bwd(tag, dy, dyb, x, gain, wgT, wuT, wd, saved, pending):
    n, g, u = saved
    half = lambda accs, ex: _swiglu_bwd_epilogue([0.5 * accs[0]], ex)
    (dg, du, a), done0 = _mm(tag + "_d_act", [(dyb, wd, "nt", 0)], [BF, BF, BF], tm=512, tn=1408, tk=D_MODEL,
                             epilogue=half, extras=[(g, "tile", 0), (u, "tile", 0)], comm=pending, cols_outer=True)
    sum_d = _dw_pair(tag + "_dw_down", a, dyb, 0.5)
    sum_g, (slots_d,) = _dw_pair(tag + "_dw_gate", dg, n, 1.0, comm=[_chip_task([sum_d])])
    sum_u, (slots_g,) = _dw_pair(tag + "_dw_up", du, n, 1.0, comm=[_chip_task([sum_g])])
    (dx, dxb, dgain), (slots_u,) = _mm(
        tag + "_d_norm", [(dg, wgT, "nn", 0), (du, wuT, "nn", 0)], [F32, BF], tm=512, tn=D_MODEL, tk=D_FF,
        epilogue=_rms_bwd_epilogue, extras=[(x, "tile", 0), (gain, "row", 0), (dy, "tile", 0)], n_colsum=1,
        comm=[_chip_task([sum_u])])
    return dx, dxb, dgain, done0, slots_g[0], slots_u[0], slots_d[0]


def _tile_gain(g):
    return jnp.concatenate([g, g]).reshape(1, LANES)


def _fold_heads(partials):
    return jnp.sum(partials.reshape(-1, HEAD_DIM), axis=0)


def _pack_small_grads(grads, loss_local):
    pieces, row = [], 0
    for name, r0, _ in SMALL_LAYOUT + (("loss", LOSS_ROW, None),):
        v = (loss_local if name == "loss" else grads[name]).reshape(-1)
        rows = -(-v.size // LANES)
        block = jnp.pad(v, (0, rows * LANES - v.size)).reshape(rows, LANES)
        pieces += [jnp.zeros((r0 - row, LANES), F32)] * (r0 > row) + [block]
        row = r0 + rows
    pieces.append(jnp.zeros((SMALL_ROWS - row, LANES), F32))
    return jnp.concatenate(pieces, axis=0)


def kernel(x, ffn1_norm, ffn1_w_gate, ffn1_w_up, ffn1_w_down, mix_norm, w_in, pool_w, pool_scale, w_pool_out, q_norm, k_norm, sinks, w_attn_out, gate_bias, w_out, ffn2_norm, ffn2_w_gate, ffn2_w_up, ffn2_w_down, loss_target, m_ffn1_norm, m_ffn1_w_gate, m_ffn1_w_up, m_ffn1_w_down, m_mix_norm, m_w_in, m_pool_w, m_pool_scale, m_w_pool_out, m_q_norm, m_k_norm, m_sinks, m_w_attn_out, m_gate_bias, m_w_out, m_ffn2_norm, m_ffn2_w_gate, m_ffn2_w_up, m_ffn2_w_down, v_ffn1_norm, v_ffn1_w_gate, v_ffn1_w_up, v_ffn1_w_down, v_mix_norm, v_w_in, v_pool_w, v_pool_scale, v_w_pool_out, v_q_norm, v_k_norm, v_sinks, v_w_attn_out, v_gate_bias, v_w_out, v_ffn2_norm, v_ffn2_w_gate, v_ffn2_w_up, v_ffn2_w_down):
    T = x.shape[1]
    x2 = x.reshape(T, D_MODEL)
    target = loss_target.reshape(T, D_MODEL)

    big = [
        ("ffn1_w_gate", ffn1_w_gate, m_ffn1_w_gate, v_ffn1_w_gate, True, False),
        ("ffn1_w_up", ffn1_w_up, m_ffn1_w_up, v_ffn1_w_up, True, False),
        ("ffn1_w_down", ffn1_w_down, m_ffn1_w_down, v_ffn1_w_down, False, False),
        ("w_in", w_in, m_w_in, v_w_in, True, False),
        ("w_pool_out", w_pool_out, m_w_pool_out, v_w_pool_out, False, True),
        ("w_attn_out", w_attn_out, m_w_attn_out, v_w_attn_out, False, False),
        ("w_out", w_out, m_w_out, v_w_out, False, False),
        ("ffn2_w_gate", ffn2_w_gate, m_ffn2_w_gate, v_ffn2_w_gate, True, False),
        ("ffn2_w_up", ffn2_w_up, m_ffn2_w_up, v_ffn2_w_up, True, False),
        ("ffn2_w_down", ffn2_w_down, m_ffn2_w_down, v_ffn2_w_down, False, False),
    ]
    view = lambda a, tv: a.T if tv else a
    shards = _prep("prep_weights", [view(w, tv) for _, w, _, _, tv, _ in big], [tk_ for *_, tk_ in big])
    wg1T, wu1T = _all_gather("gather_ffn1_gate_up", shards[0:2])

    g1 = ffn1_norm.reshape(1, D_MODEL)
    g2 = mix_norm.reshape(1, D_MODEL)
    g3 = ffn2_norm.reshape(1, D_MODEL)
    bias_row = gate_bias.reshape(1, 2 * D_MODEL)
    qg, kg = _tile_gain(q_norm) * ATTN_SCALE, _tile_gain(k_norm)
    scale_row = pool_scale.reshape(1, POOL_WIDTH)

    n1 = _rms_fwd("ffn1_norm", x2, g1)
    (gt1, up1, act1), ((wd1,), (w_inT,)) = _mm(
        "ffn1_gate_up", [(n1, wg1T, "nt", 0), (n1, wu1T, "nt", 1)], [BF, BF, BF], tm=512, tn=1408, tk=D_MODEL,
        epilogue=_swiglu_fwd_epilogue, cols_outer=True,
        comm=[_gather_task(shards[2:3], forward_at=0.5), _gather_task(shards[3:4], natural=(0,), forward_at=0.9)])
    (h1,) = _mm("ffn1_down", [(act1, wd1, "nn", 0)], [F32], tm=512, tn=D_MODEL, tk=D_FF,
                epilogue=_half_residual_epilogue, extras=[(x2, "tile", 0)])
    saved1 = (n1, gt1, up1)
    u = _rms_fwd("mix_norm", h1, g2)
    (proj,), ((w_poT, w_ao, w_o),) = _mm(
        "in_proj", [(u, w_inT, "nt", 0)], [BF], tm=512, tn=1280, tk=D_MODEL, cols_outer=True,
        comm=[_gather_task(shards[4:7], natural=(0, 1, 2), forward_at=0.8)])
    pooled, mixed = _pool_fwd("pool_fwd", proj, pool_w, scale_row)
    qn = _headnorm_fwd("q_norm", proj, COL_Q, ATTN_WIDTH, qg)
    kn = _headnorm_fwd("k_norm", proj, COL_K, KV_WIDTH, kg)
    attn, ((wg2T, wu2T),) = _attn_fwd("attn_fwd", qn, kn, proj, sinks,
                                      comm=[_gather_task(shards[7:9], forward_at=0.85)])
    (bp,) = _mm("pool_out", [(mixed, w_poT, "nt", 0)], [BF], tm=1024, tn=D_MODEL, tk=POOL_WIDTH)
    gate_tn = 256
    gate_extras = [(proj, "tile", COL_GP // gate_tn), (proj, "tile", COL_GA // gate_tn),
                   (bias_row, "row", 0), (bias_row, "row", D_MODEL // gate_tn)]
    merged, ba = _mm("attn_out_merge", [(attn, w_ao, "nn", 0)], [BF, BF], tm=2048, tn=gate_tn, tk=ATTN_WIDTH,
                     epilogue=_merge_fwd_epilogue, extras=[(bp, "tile", 0)] + gate_extras)
    (h2,) = _mm("mix_out", [(merged, w_o, "nn", 0)], [F32], tm=512, tn=D_MODEL, tk=D_MODEL,
                epilogue=_residual_epilogue, extras=[(h1, "tile", 0)])
    n2 = _rms_fwd("ffn2_norm", h2, g3)
    (gt2, up2, act2), ((wd2,),) = _mm(
        "ffn2_gate_up", [(n2, wg2T, "nt", 0), (n2, wu2T, "nt", 1)], [BF, BF, BF], tm=512, tn=1408, tk=D_MODEL,
        epilogue=_swiglu_fwd_epilogue, cols_outer=True, comm=[_gather_task(shards[9:10], forward_at=0.8)])
    dy, dyb, sq = _mm("ffn2_down_loss", [(act2, wd2, "nn", 0)], [F32, BF], tm=512, tn=D_MODEL, tk=D_FF,
                      epilogue=_loss_epilogue, extras=[(h2, "tile", 0), (target, "tile", 0)], n_colsum=1)
    loss_local = 0.5 * jnp.sum(sq) / D_MODEL

    dh2, dh2b, dg3, _, slots_g2, slots_u2, slots_d2 = _ffn_bwd(
        "ffn2", dy, dyb, h2, g3, wg2T, wu2T, wd2, (n2, gt2, up2), [])
    dbp, dba, dgp, dga, cs_gp, cs_ga = _mm(
        "mix_out_bwd", [(dh2b, w_o, "nt", 0)], [BF, BF, BF, BF], tm=2048, tn=gate_tn, tk=D_MODEL,
        epilogue=_merge_bwd_epilogue, extras=[(bp, "tile", 0), (ba, "tile", 0)] + gate_extras, n_colsum=2)
    sum_o = _dw_pair("dw_out", merged, dh2b, 1.0, blocks=4)
    (dmixed,), ((slots_o,),) = _mm("pool_out_bwd", [(dbp, w_poT, "nn", 0)], [BF], tm=1024, tn=POOL_WIDTH, tk=D_MODEL,
                                   comm=[_chip_task([sum_o])])
    sum_po = _dw_pair("dw_pool_out", dbp, mixed, 1.0, blocks=4)
    (dattn,), ((slots_po,),) = _mm("attn_out_bwd", [(dba, w_ao, "nt", 0)], [BF], tm=1024, tn=ATTN_WIDTH, tk=D_MODEL,
                                   comm=[_chip_task([sum_po])])
    sum_ao = _dw_pair("dw_attn_out", attn, dba, 1.0, blocks=4)
    dxp, dpool_w, dpool_scale = _pool_bwd("pool_bwd", dmixed, pooled, pool_w, scale_row)
    dqn, dkn, dv, dsink_tile = _attn_bwd("attn_bwd", dattn, qn, kn, proj, sinks)
    dq, dqg = _headnorm_bwd("q_norm_bwd", dqn, proj, COL_Q, ATTN_WIDTH, qg)
    dk, dkg = _headnorm_bwd("k_norm_bwd", dkn, proj, COL_K, KV_WIDTH, kg)
    dproj = jnp.concatenate([dxp, dq, dk, dv, dgp, dga], axis=1)
    (dh1, dh1b, dg2), ((slots_ao,),) = _mm(
        "in_proj_bwd", [(dproj, w_inT, "nn", 0)], [F32, BF], tm=512, tn=D_MODEL, tk=IN_WIDTH, epilogue=_rms_bwd_epilogue,
        extras=[(h1, "tile", 0), (g2, "row", 0), (dh2, "tile", 0)], n_colsum=1, comm=[_chip_task([sum_ao])])
    (dw_inT,) = _mm("dw_in", [(dproj, u, "tn", 0)], [BF], tm=1280, tn=D_MODEL, tk=2048)
    part_in = dw_inT.reshape(4, 2, IN_WIDTH // N_DEV, D_MODEL)
    (got_in,) = _pair_exchange("pair_exchange_w_in", [part_in])
    core = lax.axis_index("c").astype(jnp.int32).reshape(1)
    sum_in = _pair_sum("pair_sum_w_in", part_in, got_in, core).reshape(IN_WIDTH // 2, D_MODEL)
    dx, _, dg1, ((slots_in,),), slots_g1, slots_u1, slots_d1 = _ffn_bwd(
        "ffn1", dh1, dh1b, x2, g1, wg1T, wu1T, wd1, saved1, [_chip_task([sum_in])])

    slots = [slots_g1, slots_u1, slots_d1, slots_in, slots_po, slots_ao, slots_o, slots_g2, slots_u2, slots_d2]
    big_out = {}
    for k, (nm, w, m, v, tv, tk_) in enumerate(big):
        res = _adamw_sharded("adamw_" + nm, slots[k], view(w, tv), view(m, tv), view(v, tv), tk_)
        big_out[nm] = tuple(view(r, tv) for r in res)

    small_grads = {
        "ffn1_norm": jnp.sum(dg1, axis=(0, 1)), "mix_norm": jnp.sum(dg2, axis=(0, 1)), "ffn2_norm": jnp.sum(dg3, axis=(0, 1)),
        "gate_bias": jnp.concatenate([jnp.sum(cs_gp, axis=(0, 1)), jnp.sum(cs_ga, axis=(0, 1))]),
        "pool_scale": dpool_scale, "q_norm": _fold_heads(dqg) * ATTN_SCALE, "k_norm": _fold_heads(dkg),
        "sinks": dsink_tile[0, :N_HEADS]}
    g_vec, g_pool_w = _all_gather("gather_small_grads", [_pack_small_grads(small_grads, loss_local),
                                                         dpool_w.reshape(-1, LANES)])
    given = {"ffn1_norm": (ffn1_norm, m_ffn1_norm, v_ffn1_norm), "mix_norm": (mix_norm, m_mix_norm, v_mix_norm),
             "ffn2_norm": (ffn2_norm, m_ffn2_norm, v_ffn2_norm), "gate_bias": (gate_bias, m_gate_bias, v_gate_bias),
             "pool_scale": (pool_scale, m_pool_scale, v_pool_scale), "q_norm": (q_norm, m_q_norm, v_q_norm),
             "k_norm": (k_norm, m_k_norm, v_k_norm), "sinks": (sinks, m_sinks, v_sinks)}
    params = [tuple(a.reshape(shape) for a in given[nm]) for nm, _, shape in SMALL_LAYOUT]
    params.append(tuple(a.reshape(-1, LANES) for a in (pool_w, m_pool_w, v_pool_w)))
    small_res, loss_row = _adamw_small("adamw_small", g_vec.reshape(N_DEV, SMALL_ROWS, LANES),
                                       g_pool_w.reshape(N_DEV, -1, LANES), params)
    small_out = {nm: tuple(r.reshape(given[nm][0].shape) for r in res)
                 for (nm, _, _), res in zip(SMALL_LAYOUT, small_res)}
    small_out["pool_w"] = tuple(r.reshape(pool_w.shape) for r in small_res[-1])
    loss = loss_row[0, 0]

    order = ["ffn1_norm", "ffn1_w_gate", "ffn1_w_up", "ffn1_w_down", "mix_norm", "w_in", "pool_w", "pool_scale",
             "w_pool_out", "q_norm", "k_norm", "sinks", "w_attn_out", "gate_bias", "w_out", "ffn2_norm",
             "ffn2_w_gate", "ffn2_w_up", "ffn2_w_down"]
    every = {**big_out, **small_out}
    outs = [loss, dx.reshape(x.shape)]
    for j in range(4):
        outs += [every[nm][j] for nm in order]
    return tuple(outs)
```

```python
import functools

import jax
import jax.numpy as jnp
from jax import lax
from jax.experimental import pallas as pl
from jax.experimental.pallas import tpu as pltpu

BF = jnp.bfloat16
F32 = jnp.float32

D_MODEL = 1024
D_FF = 2816
POOL_WIDTH = 512
POOL_GROUP = 128
N_POOL_GROUPS = 4
HEAD_DIM = 64
N_HEADS = 16
GQA_GROUP = 8
BLOCK = 128
ATTN_WIDTH = 1024
KV_WIDTH = 128
IN_WIDTH = 3840
RMS_EPS = 1e-6
N_DEV = 8
LANES = 128

COL_Q = POOL_WIDTH
COL_K = COL_Q + ATTN_WIDTH
COL_V = COL_K + KV_WIDTH
COL_GP = COL_V + KV_WIDTH
COL_GA = COL_GP + D_MODEL

ADAM_LR = 0.001
ADAM_B1 = 0.9
ADAM_B2 = 0.999
ADAM_EPS = 1e-08
ADAM_WD = 0.01
ADAM_STEP = 10

VMEM_LIMIT_V7X = 56 * 1024 * 1024
MESH = pl.DeviceIdType.MESH
ANY = pl.BlockSpec(memory_space=pl.ANY)


def _params(sem=None):
    return pltpu.CompilerParams(dimension_semantics=sem, vmem_limit_bytes=VMEM_LIMIT_V7X)


_DIMS = {"nt": (((1,), (1,)), ((), ())), "nn": (((1,), (0,)), ((), ())), "tn": (((0,), (0,)), ((), ()))}


class _Task:
    def __init__(self, inputs, out_shapes, scratch, phases):
        self.inputs, self.out_shapes, self.scratch = list(inputs), list(out_shapes), list(scratch)
        self.phases = list(phases)


class _CommPlumbing:
    def __init__(self, tasks):
        self.tasks = list(tasks or [])
        self.args = [a for t in self.tasks for a in t.inputs]
        self.out_shapes = [o for t in self.tasks for o in t.out_shapes]
        self.scratch = [s for t in self.tasks for s in t.scratch]
        self.n_in, self.n_out = len(self.args), len(self.out_shapes)

    def _slices(self, c_in, c_out, c_scr):
        i = o = s = 0
        for t in self.tasks:
            yield t, c_in[i:i + len(t.inputs)], c_out[o:o + len(t.out_shapes)], c_scr[s:s + len(t.scratch)]
            i, o, s = i + len(t.inputs), o + len(t.out_shapes), s + len(t.scratch)

    def run(self, step, steps, before, c_in, c_out, c_scr):
        for t, ins, outs, scr in self._slices(c_in, c_out, c_scr):
            for frac, fn in t.phases:
                if step is None:
                    fn(ins, outs, scr)
                elif before == (frac == 0):
                    at = 0 if frac == 0 else max(0, min(steps, -(-int(round(frac * steps * 64)) // 64)) - 1)
                    pl.when(step == at)(functools.partial(fn, ins, outs, scr))

    def split_outputs(self, flat):
        res, o = [], 0
        for t in self.tasks:
            res.append(list(flat[o:o + len(t.out_shapes)]))
            o += len(t.out_shapes)
        return res


def _comm_only(name, tasks):
    plumb = _CommPlumbing(tasks)

    def body(*refs):
        c_in, c_out = refs[:plumb.n_in], refs[plumb.n_in: plumb.n_in + plumb.n_out]
        c_scr = refs[plumb.n_in + plumb.n_out:]
        plumb.run(None, 1, True, c_in, c_out, c_scr)

    res = pl.pallas_call(
        body, name=name, in_specs=[ANY] * plumb.n_in, out_specs=[ANY] * plumb.n_out, out_shape=plumb.out_shapes,
        scratch_shapes=plumb.scratch, compiler_params=pltpu.CompilerParams(has_side_effects=True),
    )(*plumb.args)
    return plumb.split_outputs(res)


def _mm(name, terms, out_dtypes, *, tm, tn, tk, epilogue=None, extras=(), n_colsum=0, comm=None, cols_outer=False):
    a0, b0, mode0, _ = terms[0]
    if mode0 == "nt":
        (M, K), N = a0.shape, b0.shape[0]
    elif mode0 == "nn":
        (M, K), N = a0.shape, b0.shape[1]
    else:
        (K, M), N = a0.shape, b0.shape[1]
    tm, tn, tk = min(tm, M), min(tn, N), min(tk, K)
    assert M % tm == 0 and N % tn == 0 and K % tk == 0, (name, M, N, K, tm, tn, tk)
    nI, nJ, nK = M // tm, N // tn, K // tk
    n_terms = len(terms)
    n_acc = max(t[3] for t in terms) + 1
    n_ex = len(extras)
    n_out = len(out_dtypes)
    if epilogue is None:
        epilogue = lambda accs, ex: ([accs[0]], [])
    plumb = _CommPlumbing(comm)
    n_scr = n_acc if nK > 1 else 0
    grid = (nJ, nI, nK) if cols_outer else (nI, nJ, nK)

    def body(*refs):
        n_in = 2 * n_terms + n_ex
        ab = refs[: 2 * n_terms]
        ex_refs = refs[2 * n_terms: n_in]
        c_in = refs[n_in: n_in + plumb.n_in]
        o0 = n_in + plumb.n_in
        out_refs = refs[o0: o0 + n_out]
        cs_refs = refs[o0 + n_out: o0 + n_out + n_colsum]
        c_out = refs[o0 + n_out + n_colsum: o0 + n_out + n_colsum + plumb.n_out]
        s0 = o0 + n_out + n_colsum + plumb.n_out
        acc_refs = refs[s0: s0 + n_scr]
        c_scr = refs[s0 + n_scr:]
        steps = grid[0] * grid[1] * nK
        if comm:
            step = (pl.program_id(0) * grid[1] + pl.program_id(1)) * nK + pl.program_id(2)
            plumb.run(step, steps, True, c_in, c_out, c_scr)

        def products():
            accs = [None] * n_acc
            for t, (_, _, mode, ai) in enumerate(terms):
                p = lax.dot_general(ab[2 * t][...], ab[2 * t + 1][...], _DIMS[mode], preferred_element_type=F32)
                accs[ai] = p if accs[ai] is None else accs[ai] + p
            return accs

        def finish(accs):
            outs, colsums = epilogue(accs, [r[...] for r in ex_refs])
            for r, o in zip(out_refs, outs):
                r[...] = o.astype(r.dtype)
            for r, cs in zip(cs_refs, colsums):
                r[...] = jnp.sum(cs, axis=0, keepdims=True).reshape(r.shape)

        if nK == 1:
            finish(products())
        else:
            k = pl.program_id(2)
            accs = products()

            @pl.when(k == 0)
            def _():
                for r, a in zip(acc_refs, accs):
                    r[...] = a

            @pl.when(k > 0)
            def _():
                for r, a in zip(acc_refs, accs):
                    r[...] += a

            @pl.when(k == nK - 1)
            def _():
                finish([r[...] for r in acc_refs])

        if comm:
            plumb.run(step, steps, False, c_in, c_out, c_scr)

    def spec(block, index, fixed=False):
        imap = (lambda q, p, k: index(p, q, k)) if cols_outer else index
        return pl.BlockSpec(block, imap, pipeline_mode=pl.Buffered(1)) if fixed else pl.BlockSpec(block, imap)

    in_specs, args = [], []
    for a, b, mode, _ in terms:
        if mode == "nt":
            in_specs += [spec((tm, tk), lambda i, j, k: (i, k), nI * nK == 1),
                         spec((tn, tk), lambda i, j, k: (j, k), nJ * nK == 1)]
        elif mode == "nn":
            in_specs += [spec((tm, tk), lambda i, j, k: (i, k), nI * nK == 1),
                         spec((tk, tn), lambda i, j, k: (k, j), nJ * nK == 1)]
        else:
            in_specs += [spec((tk, tm), lambda i, j, k: (k, i), nI * nK == 1),
                         spec((tk, tn), lambda i, j, k: (k, j), nJ * nK == 1)]
        args += [a, b]
    for arr, kind, off in extras:
        if kind == "tile":
            in_specs.append(spec((tm, tn), functools.partial(lambda i, j, k, off: (i, j + off), off=off)))
        else:
            in_specs.append(spec((1, tn), functools.partial(lambda i, j, k, off: (0, j + off), off=off)))
        args.append(arr)
    out_shape = [jax.ShapeDtypeStruct((M, N), dt) for dt in out_dtypes]
    out_specs = [spec((tm, tn), lambda i, j, k: (i, j)) for _ in out_dtypes]
    out_shape += [jax.ShapeDtypeStruct((nI, 1, N), F32) for _ in range(n_colsum)]
    out_specs += [spec((1, 1, tn), lambda i, j, k: (i, 0, j)) for _ in range(n_colsum)]
    scratch = [pltpu.VMEM((tm, tn), F32) for _ in range(n_scr)]
    args += plumb.args
    in_specs += [ANY] * plumb.n_in
    out_shape += plumb.out_shapes
    out_specs += [ANY] * plumb.n_out
    sem = ("arbitrary",) * 3 if comm else ("parallel", "parallel", "arbitrary")
    res = pl.pallas_call(
        body, name=name, grid=grid, in_specs=in_specs, out_specs=out_specs, out_shape=out_shape,
        scratch_shapes=scratch + plumb.scratch, compiler_params=_params(sem),
    )(*args)
    n_own = n_out + n_colsum
    return (list(res[:n_own]), plumb.split_outputs(res[n_own:])) if comm is not None else res


ROW_TILE = 512


def _rms_fwd(name, x, g, comm):
    T, D = x.shape
    steps = T // ROW_TILE
    plumb = _CommPlumbing(comm)

    def body(x_ref, g_ref, *rest):
        c_in, o_ref = rest[:plumb.n_in], rest[plumb.n_in]
        c_out, c_scr = rest[plumb.n_in + 1: plumb.n_in + 1 + plumb.n_out], rest[plumb.n_in + 1 + plumb.n_out:]
        plumb.run(pl.program_id(0), steps, True, c_in, c_out, c_scr)
        xv = x_ref[...]
        r = lax.rsqrt(jnp.mean(xv * xv, axis=-1, keepdims=True) + RMS_EPS)
        o_ref[...] = (xv * r * g_ref[...]).astype(BF)
        plumb.run(pl.program_id(0), steps, False, c_in, c_out, c_scr)

    row = pl.BlockSpec((ROW_TILE, D), lambda i: (i, 0))
    res = pl.pallas_call(
        body, name=name, grid=(steps,),
        in_specs=[row, pl.BlockSpec((1, D), lambda i: (0, 0))] + [ANY] * plumb.n_in,
        out_specs=[row] + [ANY] * plumb.n_out, out_shape=[jax.ShapeDtypeStruct((T, D), BF)] + plumb.out_shapes,
        scratch_shapes=plumb.scratch, compiler_params=_params(("arbitrary",)),
    )(x, g, *plumb.args)
    return res[0], plumb.split_outputs(res[1:])


HEADNORM_TILE = 1024


def _half_sum_matrix():
    r = lax.broadcasted_iota(jnp.int32, (LANES, LANES), 0) // HEAD_DIM
    c = lax.broadcasted_iota(jnp.int32, (LANES, LANES), 1) // HEAD_DIM
    return (r == c).astype(BF)


def _head_mean(v, ones_blockdiag):
    hi = v.astype(BF)
    lo = (v - hi.astype(F32)).astype(BF)
    s = jnp.dot(hi, ones_blockdiag, preferred_element_type=F32) + jnp.dot(lo, ones_blockdiag, preferred_element_type=F32)
    return s * (1.0 / HEAD_DIM)


def _headnorm_fwd(name, proj, col0, width, g2):
    T = proj.shape[0]
    wide = min(width, GROUP_WIDTH)
    nb, off = width // wide, col0 // wide

    def body(x_ref, g_ref, b_ref, o_ref):
        for s in range(wide // LANES):
            lanes = slice(LANES * s, LANES * (s + 1))
            xv = x_ref[:, lanes].astype(F32)
            r = lax.rsqrt(_head_mean(xv * xv, b_ref[...]) + RMS_EPS)
            o_ref[:, lanes] = (xv * r * g_ref[...]).astype(BF)

    return pl.pallas_call(
        body, name=name, grid=(T // HEADNORM_TILE, nb),
        in_specs=[pl.BlockSpec((HEADNORM_TILE, wide), lambda i, j: (i, j + off)),
                  pl.BlockSpec((1, LANES), lambda i, j: (0, 0)), pl.BlockSpec((LANES, LANES), lambda i, j: (0, 0))],
        out_specs=pl.BlockSpec((HEADNORM_TILE, wide), lambda i, j: (i, j)),
        out_shape=jax.ShapeDtypeStruct((T, width), BF), compiler_params=_params(("parallel", "parallel")),
    )(proj, g2, _half_sum_matrix())


def _headnorm_bwd(name, dy, proj, col0, width, g2):
    T = proj.shape[0]
    wide = min(width, GROUP_WIDTH)
    nb, off = width // wide, col0 // wide

    def body(dy_ref, x_ref, g_ref, b_ref, dx_ref, dg_ref):
        for s in range(wide // LANES):
            lanes = slice(LANES * s, LANES * (s + 1))
            xv = x_ref[:, lanes].astype(F32)
            dyv = dy_ref[:, lanes].astype(F32)
            r = lax.rsqrt(_head_mean(xv * xv, b_ref[...]) + RMS_EPS)
            xhat = xv * r
            dxhat = dyv * g_ref[...]
            dx_ref[:, lanes] = (r * (dxhat - xhat * _head_mean(dxhat * xhat, b_ref[...]))).astype(BF)
            dg_ref[0, :, lanes] = jnp.sum(dyv * xhat, axis=0, keepdims=True)

    return pl.pallas_call(
        body, name=name, grid=(T // HEADNORM_TILE, nb),
        in_specs=[pl.BlockSpec((HEADNORM_TILE, wide), lambda i, j: (i, j)),
                  pl.BlockSpec((HEADNORM_TILE, wide), lambda i, j: (i, j + off)),
                  pl.BlockSpec((1, LANES), lambda i, j: (0, 0)), pl.BlockSpec((LANES, LANES), lambda i, j: (0, 0))],
        out_specs=[pl.BlockSpec((HEADNORM_TILE, wide), lambda i, j: (i, j)),
                   pl.BlockSpec((1, 1, wide), lambda i, j: (i, 0, j))],
        out_shape=[jax.ShapeDtypeStruct((T, width), BF), jax.ShapeDtypeStruct((T // HEADNORM_TILE, 1, width), F32)],
        compiler_params=_params(("parallel", "parallel")),
    )(dy, proj, g2, _half_sum_matrix())


def _shift_down(v, k, row):
    return jnp.where(row >= k, pltpu.roll(v, k, axis=0), 0.0)


def _shift_up(v, k, row, T):
    return jnp.where(row < T - k, pltpu.roll(v, T - k, axis=0), 0.0)


def _by_group(g, vals):
    out = vals[-1]
    for i in range(len(vals) - 2, -1, -1):
        out = jnp.where(g == i, vals[i], out)
    return out


def _pool_fwd(name, proj, pool_w, pool_scale):
    T = proj.shape[0]

    def body(x_ref, w_ref, s_ref, pooled_ref, mixed_ref):
        g = pl.program_id(0)
        xv = x_ref[...].astype(F32)
        row = lax.broadcasted_iota(jnp.int32, (T, 1), 0)
        s2 = xv + _shift_down(xv, 1, row)
        s4 = s2 + _shift_down(s2, 2, row)
        s8 = s4 + _shift_down(s4, 4, row)
        s16 = s8 + _shift_down(s8, 8, row)
        wsum = _by_group(g, [s2, s4, s8, s16])
        count = jnp.minimum(row + 1, 2 << g).astype(F32)
        pooled = (wsum / count - xv).astype(BF)
        pooled_ref[...] = pooled
        mixed = jnp.dot(pooled, w_ref[0].astype(BF), preferred_element_type=F32) * s_ref[...]
        mixed_ref[...] = mixed.astype(BF)

    col = pl.BlockSpec((T, POOL_GROUP), lambda g: (0, g))
    return pl.pallas_call(
        body, name=name, grid=(N_POOL_GROUPS,),
        in_specs=[col, pl.BlockSpec((1, POOL_GROUP, POOL_GROUP), lambda g: (g, 0, 0)),
                  pl.BlockSpec((1, POOL_GROUP), lambda g: (0, g))],
        out_specs=[col, col],
        out_shape=[jax.ShapeDtypeStruct((T, POOL_WIDTH), BF), jax.ShapeDtypeStruct((T, POOL_WIDTH), BF)],
        compiler_params=_params(("parallel",)),
    )(proj, pool_w, pool_scale)


def _pool_bwd(name, dmixed, pooled, pool_w, pool_scale):
    T = dmixed.shape[0]

    def body(dm_ref, p_ref, w_ref, s_ref, dx_ref, dw_ref, ds_ref):
        g = pl.program_id(0)
        dm = dm_ref[...].astype(F32)
        pooled = p_ref[...]
        w = w_ref[0].astype(BF)
        pre = jnp.dot(pooled, w, preferred_element_type=F32)
        ds_ref[...] = jnp.sum(dm * pre, axis=0, keepdims=True)
        dms = (dm * s_ref[...]).astype(BF)
        dw_ref[0] = lax.dot_general(pooled, dms, _DIMS["tn"], preferred_element_type=F32)
        dpooled = lax.dot_general(dms, w, _DIMS["nt"], preferred_element_type=F32)
        row = lax.broadcasted_iota(jnp.int32, (T, 1), 0)
        count = jnp.minimum(row + 1, 2 << g).astype(F32)
        z = dpooled / count
        l2 = z + _shift_up(z, 1, row, T)
        l4 = l2 + _shift_up(l2, 2, row, T)
        l8 = l4 + _shift_up(l4, 4, row, T)
        l16 = l8 + _shift_up(l8, 8, row, T)
        dx_ref[...] = (_by_group(g, [l2, l4, l8, l16]) - dpooled).astype(BF)

    col = pl.BlockSpec((T, POOL_GROUP), lambda g: (0, g))
    wspec = pl.BlockSpec((1, POOL_GROUP, POOL_GROUP), lambda g: (g, 0, 0))
    sspec = pl.BlockSpec((1, POOL_GROUP), lambda g: (0, g))
    return pl.pallas_call(
        body, name=name, grid=(N_POOL_GROUPS,), in_specs=[col, col, wspec, sspec], out_specs=[col, wspec, sspec],
        out_shape=[jax.ShapeDtypeStruct((T, POOL_WIDTH), BF),
                   jax.ShapeDtypeStruct((N_POOL_GROUPS, POOL_GROUP, POOL_GROUP), F32),
                   jax.ShapeDtypeStruct((1, POOL_WIDTH), F32)],
        compiler_params=_params(("parallel",)),
    )(dmixed, pooled, pool_w, pool_scale)


ATTN_SCALE = HEAD_DIM ** -0.5
MASKED = float(jnp.finfo(jnp.float32).min)
KV_COL_BLOCK_K = COL_K // LANES
KV_COL_BLOCK_V = COL_V // LANES
GROUP_WIDTH = GQA_GROUP * HEAD_DIM


def _dup_head(v, j):
    half = lax.broadcasted_iota(jnp.int32, (1, LANES), 1) // HEAD_DIM
    return jnp.where(half == j, v, pltpu.roll(v, HEAD_DIM, axis=1))


def _stack_heads(v, low):
    pieces = []
    for p in range(GROUP_WIDTH // LANES):
        vp = v[:, LANES * p: LANES * (p + 1)]
        pieces.append(jnp.where(low, vp, jnp.zeros_like(vp)))
        pieces.append(jnp.where(low, jnp.zeros_like(vp), vp))
    return jnp.concatenate(pieces, axis=0)


def _unstack_heads(st, low):
    pieces = []
    for p in range(GROUP_WIDTH // LANES):
        even = st[BLOCK * (2 * p): BLOCK * (2 * p + 1)]
        odd = st[BLOCK * (2 * p + 1): BLOCK * (2 * p + 2)]
        pieces.append(jnp.where(low, even, odd))
    return jnp.concatenate(pieces, axis=1)


def _band_mask(n):
    row = lax.broadcasted_iota(jnp.int32, (BLOCK, 2 * BLOCK), 0)
    col = lax.broadcasted_iota(jnp.int32, (BLOCK, 2 * BLOCK), 1)
    return (col > row) & (col <= row + BLOCK) & ((n > 0) | (col >= BLOCK))


def _softmax_heads(s, valid, sink_ref, j):
    ps, psinks = [], []
    for h in range(GQA_GROUP):
        sh = jnp.where(valid, s[BLOCK * h: BLOCK * (h + 1)], MASKED)
        sink = sink_ref[j * GQA_GROUP + h]
        m = jnp.maximum(jnp.max(sh, axis=1, keepdims=True), sink)
        e = jnp.exp(sh - m)
        es = jnp.exp(sink - m)
        inv = 1.0 / (jnp.sum(e, axis=1, keepdims=True) + es)
        ps.append(e * inv)
        psinks.append(es * inv)
    return jnp.concatenate(ps, axis=0), jnp.concatenate(psinks, axis=0)


def _attn_fwd(name, qn, kn, proj, sinks, comm=None):
    T = qn.shape[0]
    nb = T // BLOCK
    plumb = _CommPlumbing(comm)

    def body(sink_ref, q_ref, kp_ref, kc_ref, vp_ref, vc_ref, *rest):
        c_in, o_ref = rest[:plumb.n_in], rest[plumb.n_in]
        c_out, c_scr = rest[plumb.n_in + 1: plumb.n_in + 1 + plumb.n_out], rest[plumb.n_in + 1 + plumb.n_out:]
        n, j = pl.program_id(0), pl.program_id(1)
        plumb.run(2 * n + j, 2 * nb, True, c_in, c_out, c_scr)
        low = lax.broadcasted_iota(jnp.int32, (1, LANES), 1) < HEAD_DIM
        k2 = _dup_head(jnp.concatenate([kp_ref[...], kc_ref[...]], axis=0), j)
        v2 = _dup_head(jnp.concatenate([vp_ref[...], vc_ref[...]], axis=0), j)
        s = lax.dot_general(_stack_heads(q_ref[...], low), k2, _DIMS["nt"], preferred_element_type=F32)
        p, _ = _softmax_heads(s, _band_mask(n), sink_ref, j)
        o = jnp.dot(p.astype(BF), v2, preferred_element_type=F32)
        o_ref[...] = _unstack_heads(o, low).astype(BF)
        plumb.run(2 * n + j, 2 * nb, False, c_in, c_out, c_scr)

    group = pl.BlockSpec((BLOCK, GROUP_WIDTH), lambda n, j: (n, j))
    res = pl.pallas_call(
        body, name=name, grid=(nb, 2),
        in_specs=[pl.BlockSpec(memory_space=pltpu.SMEM), group,
                  pl.BlockSpec((BLOCK, LANES), lambda n, j: (jnp.maximum(n - 1, 0), 0)),
                  pl.BlockSpec((BLOCK, LANES), lambda n, j: (n, 0)),
                  pl.BlockSpec((BLOCK, LANES), lambda n, j: (jnp.maximum(n - 1, 0), KV_COL_BLOCK_V)),
                  pl.BlockSpec((BLOCK, LANES), lambda n, j: (n, KV_COL_BLOCK_V))] + [ANY] * plumb.n_in,
        out_specs=[group] + [ANY] * plumb.n_out,
        out_shape=[jax.ShapeDtypeStruct((T, ATTN_WIDTH), BF)] + plumb.out_shapes, scratch_shapes=plumb.scratch,
        compiler_params=_params(("arbitrary", "arbitrary") if comm else ("parallel", "parallel")),
    )(sinks, qn, kn, kn, proj, proj, *plumb.args)
    return (res[0], plumb.split_outputs(res[1:])) if comm is not None else res[0]


def _attn_bwd(name, dout, qn, kn, proj, sinks):
    T = qn.shape[0]
    nb = T // BLOCK

    def body(sink_ref, do_ref, q_ref, kp_ref, kc_ref, vp_ref, vc_ref, dq_ref, dk_ref, dv_ref, dsink_ref,
             carry_k, carry_v, tot_k, tot_v):
        n = pl.program_id(0)
        lane = lax.broadcasted_iota(jnp.int32, (1, LANES), 1)
        low = lane < HEAD_DIM

        @pl.when(n == 0)
        def _():
            carry_k[...] = jnp.zeros_like(carry_k)
            carry_v[...] = jnp.zeros_like(carry_v)
            dsink_ref[...] = jnp.zeros_like(dsink_ref)

        @pl.when(n == nb)
        def _():
            tot_k[...] = jnp.zeros_like(tot_k)
            tot_v[...] = jnp.zeros_like(tot_v)

        @pl.when(n < nb)
        def _():
            kk = jnp.concatenate([kp_ref[...], kc_ref[...]], axis=0)
            vv = jnp.concatenate([vp_ref[...], vc_ref[...]], axis=0)
            valid = _band_mask(n)
            dk_tot = jnp.zeros((2 * BLOCK, LANES), F32)
            dv_tot = jnp.zeros((2 * BLOCK, LANES), F32)
            dsink = jnp.zeros((1, LANES), F32)
            for j in range(2):
                k2 = _dup_head(kk, j)
                v2 = _dup_head(vv, j)
                q = _stack_heads(q_ref[:, GROUP_WIDTH * j: GROUP_WIDTH * (j + 1)], low)
                do = _stack_heads(do_ref[:, GROUP_WIDTH * j: GROUP_WIDTH * (j + 1)], low)
                s = lax.dot_general(q, k2, _DIMS["nt"], preferred_element_type=F32)
                p, psink = _softmax_heads(s, valid, sink_ref, j)
                dp =lax.dot_general(do, v2, _DIMS["nt"], preferred_element_type=F32)
                delta = jnp.sum(p * dp, axis=1, keepdims=True)
                ds = (p * (dp - delta)).astype(BF)
                dq_ref[:, GROUP_WIDTH * j: GROUP_WIDTH * (j + 1)] = _unstack_heads(
                    jnp.dot(ds, k2, preferred_element_type=F32), low).astype(BF)
                dk2 = lax.dot_general(ds, q, _DIMS["tn"], preferred_element_type=F32)
                dv2 = lax.dot_general(p.astype(BF), do, _DIMS["tn"], preferred_element_type=F32)
                mine = low if j == 0 else jnp.logical_not(low)
                dk_tot = dk_tot + jnp.where(mine, dk2 + pltpu.roll(dk2, HEAD_DIM, axis=1), 0.0)
                dv_tot = dv_tot + jnp.where(mine, dv2 + pltpu.roll(dv2, HEAD_DIM, axis=1), 0.0)
                sink_term = psink * delta
                for h in range(GQA_GROUP):
                    val = -jnp.sum(sink_term[BLOCK * h: BLOCK * (h + 1)], axis=0, keepdims=True)
                    dsink = dsink + jnp.where(lane == j * GQA_GROUP + h, val, 0.0)
            tot_k[...] = dk_tot
            tot_v[...] = dv_tot
            dsink_ref[0:1, :] += dsink

        dk_ref[...] = (carry_k[...] + tot_k[0:BLOCK]).astype(BF)
        dv_ref[...] = (carry_v[...] + tot_v[0:BLOCK]).astype(BF)
        carry_k[...] = tot_k[BLOCK:]
        carry_v[...] = tot_v[BLOCK:]

    cur = lambda n: (jnp.minimum(n, nb - 1), 0)
    prev = lambda n: (jnp.maximum(n - 1, 0), 0)
    wide = pl.BlockSpec((BLOCK, ATTN_WIDTH), cur)
    return pl.pallas_call(
        body, name=name, grid=(nb + 1,),
        in_specs=[pl.BlockSpec(memory_space=pltpu.SMEM), wide, wide,
                  pl.BlockSpec((BLOCK, LANES), prev), pl.BlockSpec((BLOCK, LANES), cur),
                  pl.BlockSpec((BLOCK, LANES), lambda n: (jnp.maximum(n - 1, 0), KV_COL_BLOCK_V)),
                  pl.BlockSpec((BLOCK, LANES), lambda n: (jnp.minimum(n, nb - 1), KV_COL_BLOCK_V))],
        out_specs=[wide, pl.BlockSpec((BLOCK, LANES), prev), pl.BlockSpec((BLOCK, LANES), prev),
                   pl.BlockSpec((8, LANES), lambda n: (0, 0))],
        out_shape=[jax.ShapeDtypeStruct((T, ATTN_WIDTH), BF), jax.ShapeDtypeStruct((T, KV_WIDTH), BF),
                   jax.ShapeDtypeStruct((T, KV_WIDTH), BF), jax.ShapeDtypeStruct((8, LANES), F32)],
        scratch_shapes=[pltpu.VMEM((BLOCK, LANES), F32), pltpu.VMEM((BLOCK, LANES), F32),
                        pltpu.VMEM((2 * BLOCK, LANES), F32), pltpu.VMEM((2 * BLOCK, LANES), F32)],
        compiler_params=_params(("arbitrary",)),
    )(sinks, dout, qn, kn, kn, proj, proj)


def _swiglu_fwd_epilogue(accs, ex):
    g, u = accs
    return [g, u, g * jax.nn.sigmoid(g) * u], []


def _swiglu_bwd_epilogue(accs, ex):
    (da,) = accs
    g, u = ex[0].astype(F32), ex[1].astype(F32)
    s = jax.nn.sigmoid(g)
    return [da * u * (s * (1.0 + g * (1.0 - s))), da * (g * s)], []


def _residual_norm_epilogue(scale):
    def epilogue(accs, ex):
        res, gain = ex
        h = res + scale * accs[0]
        r = lax.rsqrt(jnp.mean(h * h, axis=-1, keepdims=True) + RMS_EPS)
        return [h, h * r * gain], []
    return epilogue


def _rms_bwd_epilogue(accs, ex):
    (dn,) = accs
    xv, g, dres = ex
    r = lax.rsqrt(jnp.mean(xv * xv, axis=-1, keepdims=True) + RMS_EPS)
    xhat = xv * r
    dxhat = dn * g
    dx = dres + r * (dxhat - xhat * jnp.mean(dxhat * xhat, axis=-1, keepdims=True))
    return [dx, dx], [dn * xhat]


def _loss_epilogue(accs, ex):
    xv, target = ex
    d = xv + 0.5 * accs[0] - target
    dy = d * (1.0 / D_MODEL)
    return [dy, dy], [d * d]


def _merge_fwd_epilogue(accs, ex):
    (ba,) = accs
    bp, gp_pre, ga_pre, bias_p, bias_a = ex
    gp = jax.nn.sigmoid(gp_pre.astype(F32) + bias_p)
    ga = jax.nn.sigmoid(ga_pre.astype(F32) + bias_a)
    return [gp * bp.astype(F32) + ga * ba, ba], []


def _merge_bwd_epilogue(accs, ex):
    (dm,) = accs
    bp, ba, gp_pre, ga_pre, bias_p, bias_a = ex
    gp = jax.nn.sigmoid(gp_pre.astype(F32) + bias_p)
    ga = jax.nn.sigmoid(ga_pre.astype(F32) + bias_a)
    dgp = dm * bp.astype(F32) * gp * (1.0 - gp)
    dga = dm * ba.astype(F32) * ga * (1.0 - ga)
    return [dm * gp, dm * ga, dgp, dga], [dgp, dga]


def _prep(name, ws, transposes):
    n = len(ws)

    def body(*refs):
        for w_ref, o_ref, tr in zip(refs[:n], refs[n:], transposes):
            v = w_ref[...]
            o_ref[...] = (v.T if tr else v).astype(BF)

    shapes = [jax.ShapeDtypeStruct(w.shape[::-1] if tr else w.shape, BF) for w, tr in zip(ws, transposes)]
    return pl.pallas_call(body, name=name, out_shape=shapes, compiler_params=_params())(*ws)


def _adam_math(w, g, m, v):
    m = ADAM_B1 * m + (1.0 - ADAM_B1) * g
    v = ADAM_B2 * v + (1.0 - ADAM_B2) * jnp.square(g)
    m_hat = m / (1.0 - ADAM_B1 ** ADAM_STEP)
    v_hat = v / (1.0 - ADAM_B2 ** ADAM_STEP)
    delta = -ADAM_LR * (m_hat / (jnp.sqrt(v_hat) + ADAM_EPS) + ADAM_WD * w)
    return delta, m, v


def _adamw_sharded(name, slots, w, m, v, transpose):
    def body(s_ref, w_ref, m_ref, v_ref, g_out, d_out, m_out, v_out):
        g = s_ref[0].astype(F32)
        for i in range(1, 4):
            g = g + s_ref[i].astype(F32)
        if transpose:
            g = g.T
        delta, mn, vn = _adam_math(w_ref[...], g, m_ref[...], v_ref[...])
        g_out[...] = g
        d_out[...] = delta
        m_out[...] = mn
        v_out[...] = vn

    out_shape = [jax.ShapeDtypeStruct(w.shape, F32)] * 4
    _, r, C = slots.shape
    rows = r // 4
    if transpose or rows % 8:
        return pl.pallas_call(body, name=name, out_shape=out_shape, compiler_params=_params())(slots, w, m, v)
    tile = pl.BlockSpec((rows, C), lambda i: (i, 0))
    return pl.pallas_call(
        body, name=name, grid=(4,), in_specs=[pl.BlockSpec((4, rows, C), lambda i: (0, i, 0)), tile, tile, tile],
        out_specs=[tile] * 4, out_shape=out_shape, compiler_params=_params(("parallel",)),
    )(slots, w, m, v)


SMALL_LAYOUT = (("ffn1_norm", 0, (8, LANES)), ("mix_norm", 8, (8, LANES)), ("ffn2_norm", 16, (8, LANES)),
                ("gate_bias", 24, (16, LANES)), ("pool_scale", 40, (4, LANES)), ("q_norm", 48, (1, HEAD_DIM)),
                ("k_norm", 56, (1, HEAD_DIM)), ("sinks", 64, (1, N_HEADS)))
LOSS_ROW = 72
SMALL_ROWS = 80


def _adamw_small(name, g_vec, g_pool_w, params):
    n = len(SMALL_LAYOUT) + 1

    def body(vec_ref, pw_ref, *refs):
        ins, outs = refs[:3 * n], refs[3 * n:]
        vec = vec_ref[0]
        pw = pw_ref[0]
        for i in range(1, N_DEV):
            vec = vec + vec_ref[i]
            pw = pw + pw_ref[i]
        grads = [vec[r0:r0 + shape[0], 0:shape[1]] for _, r0, shape in SMALL_LAYOUT] + [pw]
        for p, g in enumerate(grads):
            w_ref, m_ref, v_ref = ins[3 * p: 3 * p + 3]
            delta, mn, vn = _adam_math(w_ref[...], g, m_ref[...], v_ref[...])
            for o_ref, val in zip(outs[4 * p: 4 * p + 4], (g, delta, mn, vn)):
                o_ref[...] = val
        outs[4 * n][...] = vec[LOSS_ROW:LOSS_ROW + 1, :]

    flat = [a for wmv in params for a in wmv]
    out_shape = [jax.ShapeDtypeStruct(wmv[0].shape, F32) for wmv in params for _ in range(4)]
    out_shape.append(jax.ShapeDtypeStruct((1, LANES), F32))
    res = pl.pallas_call(body, name=name, out_shape=out_shape, compiler_params=_params())(g_vec, g_pool_w, *flat)
    return [tuple(res[4 * p: 4 * p + 4]) for p in range(n)], res[4 * n]


def _place():
    x, y, c = lax.axis_index("x"), lax.axis_index("y"), lax.axis_index("c")
    other_chips = [(1 - x, y), (x, 1 - y), (1 - x, 1 - y)]
    return x, y, c, other_chips


def _rows(ref, r, place, natural=False):
    px, py, pc = place
    b = 4 * px + 2 * py + pc if natural else 4 * pc + 2 * px + py
    return ref.at[pl.ds(pl.multiple_of(b * r, 8), r), :]


def _gather_task(shards, natural=(), forward_at=0.75):
    n = len(shards)
    rs = [s.shape[0] for s in shards]
    rows_of = lambda ref, k, place: _rows(ref, rs[k], place, k in natural)

    def copy(scr, outs, k, slot, block, to, src=None):
        rows = rows_of(outs[k], k, block)
        return pltpu.make_async_remote_copy(
            src_ref=rows if src is None else src, dst_ref=rows, send_sem=scr[0].at[7 * k + slot],
            recv_sem=scr[1].at[7 * k + slot], device_id=to, device_id_type=MESH)

    def first_sends(ins, outs, scr):
        x, y, c, chips = _place()
        me = (x, y, c)
        cps = [copy(scr, outs, k, 1 + j, me, (*chip, c), src=ins[k]) for j, chip in enumerate(chips) for k in range(n)]
        return cps + [copy(scr, outs, k, 0, me, (x, y, 1 - c), src=ins[k]) for k in range(n)]

    def passed_on(outs, scr):
        x, y, c, chips = _place()
        return [copy(scr, outs, k, 4 + j, (*chip, c), (x, y, 1 - c)) for j, chip in enumerate(chips) for k in range(n)]

    def local(ins, outs, scr):
        x, y, c, _ = _place()
        return [pltpu.make_async_copy(ins[k], rows_of(outs[k], k, (x, y, c)), scr[2].at[k]) for k in range(n)]

    def start(ins, outs, scr):
        for cp in local(ins, outs, scr) + first_sends(ins, outs, scr):
            cp.start()

    def forward(ins, outs, scr):
        x, y, c, chips = _place()
        for j, chip in enumerate(chips):
            for k in range(n):
                copy(scr, outs, k, 1 + j, (*chip, c), (x, y, c)).wait_recv()
        for cp in passed_on(outs, scr):
            cp.start()

    def finish(ins, outs, scr):
        x, y, c, chips = _place()
        for k in range(n):
            copy(scr, outs, k, 0, (x, y, 1 - c), (x, y, c)).wait_recv()
        for j, chip in enumerate(chips):
            for k in range(n):
                copy(scr, outs, k, 4 + j, (*chip, 1 - c), (x, y, c)).wait_recv()
        for cp in first_sends(ins, outs, scr) + passed_on(outs, scr):
            cp.wait_send()
        for cp in local(ins, outs, scr):
            cp.wait()

    out_shapes = [jax.ShapeDtypeStruct((N_DEV * s.shape[0], s.shape[1]), s.dtype) for s in shards]
    scratch = [pltpu.SemaphoreType.DMA((7 * n,)), pltpu.SemaphoreType.DMA((7 * n,)), pltpu.SemaphoreType.DMA((n,))]
    return _Task(shards, out_shapes, scratch, [(0, start), (forward_at, forward), (1.0, finish)])


def _all_gather(name, shards, natural=()):
    return _comm_only(name, [_gather_task(shards, natural)])[0]


def _chip_task(sums):
    n = len(sums)
    rs = [s.shape[0] // 4 for s in sums]

    def block(ref, k, chip_index):
        return ref.at[pl.ds(pl.multiple_of(chip_index * rs[k], 8), rs[k]), :]

    def copies(ins, outs, scr):
        send_sems, recv_sems, local_sems = scr
        x, y, c, chips = _place()
        here = 2 * x + y
        local = [pltpu.make_async_copy(block(ins[k], k, here), outs[k].at[here], local_sems.at[k]) for k in range(n)]
        remote = []
        for j, (px, py) in enumerate(chips):
            remote += [pltpu.make_async_remote_copy(
                src_ref=block(ins[k], k, 2 * px + py), dst_ref=outs[k].at[here],
                send_sem=send_sems.at[3 * k + j], recv_sem=recv_sems.at[3 * k + j],
                device_id=(px, py, c), device_id_type=MESH) for k in range(n)]
        return local, remote

    def start(ins, outs, scr):
        local, remote = copies(ins, outs, scr)
        for cp in local + remote:
            cp.start()

    def finish(ins, outs, scr):
        local, remote = copies(ins, outs, scr)
        for cp in remote:
            cp.wait()
        for cp in local:
            cp.wait()

    out_shapes = [jax.ShapeDtypeStruct((4, r, s.shape[1]), s.dtype) for r, s in zip(rs, sums)]
    scratch = [pltpu.SemaphoreType.DMA((3 * n,)), pltpu.SemaphoreType.DMA((3 * n,)), pltpu.SemaphoreType.DMA((n,))]
    return _Task(sums, out_shapes, scratch, [(0, start), (1.0, finish)])


def _dw_pair(name, a, b, scale, comm=None, blocks=1):
    T, M = a.shape
    N = b.shape[1]
    half = M // 2
    wide = half // blocks
    tk = min(2048, T)
    nK = T // tk
    plumb = _CommPlumbing(comm)

    def body(core_ref, *rest):
        a_refs, b_ref, rest = rest[:blocks], rest[blocks], rest[blocks + 1:]
        c_in = rest[:plumb.n_in]
        o_ref = rest[plumb.n_in]
        c_out = rest[plumb.n_in + 1: plumb.n_in + 1 + plumb.n_out]
        acc, stage, land, send_sem, recv_sem = rest[plumb.n_in + 1 + plumb.n_out: plumb.n_in + 6 + plumb.n_out]
        c_scr = rest[plumb.n_in + 6 + plumb.n_out:]
        i, k = pl.program_id(0), pl.program_id(1)
        x, y, c, _ = _place()
        push = pltpu.make_async_remote_copy(src_ref=stage, dst_ref=land, send_sem=send_sem, recv_sem=recv_sem,
                                            device_id=(x, y, 1 - c), device_id_type=MESH)
        if comm:
            plumb.run(i * nK + k, 2 * nK, True, c_in, c_out, c_scr)

        av = a_refs[0][...] if blocks == 1 else jnp.concatenate([r[...] for r in a_refs], axis=1)
        p = lax.dot_general(av, b_ref[...], _DIMS["tn"], preferred_element_type=F32)

        @pl.when(k == 0)
        def _():
            acc[...] = p

        @pl.when(k > 0)
        def _():
            acc[...] += p

        @pl.when((i == 0) & (k == nK - 1))
        def _():
            stage[...] = (scale * acc[...]).astype(BF)
            push.start()

        @pl.when((i == 1) & (k == nK - 1))
        def _():
            push.wait_recv()
            o_ref[...] = (scale * acc[...] + land[...].astype(F32)).astype(BF)
            push.wait_send()

        if comm:
            plumb.run(i * nK + k, 2 * nK, False, c_in, c_out, c_scr)

    grid_spec = pltpu.PrefetchScalarGridSpec(
        num_scalar_prefetch=1, grid=(2, nK),
        in_specs=[pl.BlockSpec((tk, wide), functools.partial(
            lambda i, k, core, j: (k, (2 * j if blocks > 1 else 0) + jnp.where(i == 0, 1 - core[0], core[0])), j=j))
            for j in range(blocks)] + [pl.BlockSpec((tk, N), lambda i, k, core: (k, 0))] + [ANY] * plumb.n_in,
        out_specs=[pl.BlockSpec((half, N), lambda i, k, core: (0, 0))] + [ANY] * plumb.n_out,
        scratch_shapes=[pltpu.VMEM((half, N), F32), pltpu.VMEM((half, N), BF), pltpu.VMEM((half, N), BF),
                        pltpu.SemaphoreType.DMA, pltpu.SemaphoreType.DMA] + plumb.scratch)
    core = lax.axis_index("c").astype(jnp.int32).reshape(1)
    res = pl.pallas_call(
        body, name=name, grid_spec=grid_spec,
        out_shape=[jax.ShapeDtypeStruct((half, N), BF)] + plumb.out_shapes,
        compiler_params=_params(("arbitrary", "arbitrary")),
    )(core, *([a] * blocks), b, *plumb.args)
    return (res[0], plumb.split_outputs(res[1:])) if comm else res[0]


def _pair_exchange(name, parts):
    n = len(parts)

    def body(*refs):
        ins, outs = refs[:n], refs[n:2 * n]
        send_sems, recv_sems = refs[2 * n:]
        x, y, c, _ = _place()
        copies = [pltpu.make_async_remote_copy(
            src_ref=ins[k].at[:, pl.ds(1 - c, 1)], dst_ref=outs[k], send_sem=send_sems.at[k], recv_sem=recv_sems.at[k],
            device_id=(x, y, 1 - c), device_id_type=MESH) for k in range(n)]
        for cp in copies:
            cp.start()
        for cp in copies:
            cp.wait()

    return pl.pallas_call(
        body, name=name, in_specs=[ANY] * n, out_specs=[ANY] * n,
        out_shape=[jax.ShapeDtypeStruct((4, 1) + p.shape[2:], p.dtype) for p in parts],
        scratch_shapes=[pltpu.SemaphoreType.DMA((n,)), pltpu.SemaphoreType.DMA((n,))],
        compiler_params=pltpu.CompilerParams(has_side_effects=True),
    )(*parts)


def _pair_sum(name, part, got, core):
    _, _, r, C = part.shape

    def body(core_ref, p_ref, g_ref, o_ref):
        o_ref[0] = (p_ref[0, 0].astype(F32) + g_ref[0, 0].astype(F32)).astype(o_ref.dtype)

    return pl.pallas_call(
        body, name=name,
        grid_spec=pltpu.PrefetchScalarGridSpec(
            num_scalar_prefetch=1, grid=(4,),
            in_specs=[pl.BlockSpec((1, 1, r, C), lambda i, core_ref: (i, core_ref[0], 0, 0)),
                      pl.BlockSpec((1, 1, r, C), lambda i, core_ref: (i, 0, 0, 0))],
            out_specs=pl.BlockSpec((1, r, C), lambda i, core_ref: (i, 0, 0))),
        out_shape=jax.ShapeDtypeStruct((4, r, C), part.dtype), compiler_params=_params(("parallel",)),
    )(core, part, got)


def _ffn_bwd(tag, dy, dyb, x, gain, wgT, wuT, wd, saved, pending):
    n, g, u, a = saved
    half = lambda accs, ex: _swiglu_bwd_epilogue([0.5 * accs[0]], ex)
    sum_d = _dw_pair(tag + "_dw_down", a, dyb, 0.5)
    early = not pending
    (dg, du), done0 = _mm(tag + "_d_act", [(dyb, wd, "nt", 0)], [BF, BF], tm=512, tn=1408, tk=D_MODEL, epilogue=half,
                          extras=[(g, "tile", 0), (u, "tile", 0)], cols_outer=True,
                          comm=[_chip_task([sum_d])] if early else pending)
    if early:
        (slots_d,), done0 = done0[0], []
        sum_g = _dw_pair(tag + "_dw_gate", dg, n, 1.0)
    else:
        sum_g, ((slots_d,),) = _dw_pair(tag + "_dw_gate", dg, n, 1.0, comm=[_chip_task([sum_d])])
    sum_u, ((slots_g,),) = _dw_pair(tag + "_dw_up", du, n, 1.0, comm=[_chip_task([sum_g])])
    (dx, dxb, dgain), ((slots_u,),) = _mm(
        tag + "_d_norm", [(dg, wgT, "nn", 0), (du, wuT, "nn", 0)], [F32, BF], tm=512, tn=D_MODEL, tk=D_FF,
        epilogue=_rms_bwd_epilogue, extras=[(x, "tile", 0), (gain, "row", 0), (dy, "tile", 0)], n_colsum=1,
        comm=[_chip_task([sum_u])])
    return dx, dxb, dgain, done0, slots_g, slots_u, slots_d


def _tile_gain(g):
    return jnp.concatenate([g, g]).reshape(1, LANES)


def _fold_heads(partials):
    return jnp.sum(partials.reshape(-1, HEAD_DIM), axis=0)


def _pack_small_grads(grads, loss_local):
    pieces, row = [], 0
    for name, r0, _ in SMALL_LAYOUT + (("loss", LOSS_ROW, None),):
        v = (loss_local if name == "loss" else grads[name]).reshape(-1)
        rows = -(-v.size // LANES)
        block = jnp.pad(v, (0, rows * LANES - v.size)).reshape(rows, LANES)
        pieces += [jnp.zeros((r0 - row, LANES), F32)] * (r0 > row) + [block]
        row = r0 + rows
    pieces.append(jnp.zeros((SMALL_ROWS - row, LANES), F32))
    return jnp.concatenate(pieces, axis=0)


def kernel(x, ffn1_norm, ffn1_w_gate, ffn1_w_up, ffn1_w_down, mix_norm, w_in, pool_w, pool_scale, w_pool_out, q_norm, k_norm, sinks, w_attn_out, gate_bias, w_out, ffn2_norm, ffn2_w_gate, ffn2_w_up, ffn2_w_down, loss_target, m_ffn1_norm, m_ffn1_w_gate, m_ffn1_w_up, m_ffn1_w_down, m_mix_norm, m_w_in, m_pool_w, m_pool_scale, m_w_pool_out, m_q_norm, m_k_norm, m_sinks, m_w_attn_out, m_gate_bias, m_w_out, m_ffn2_norm, m_ffn2_w_gate, m_ffn2_w_up, m_ffn2_w_down, v_ffn1_norm, v_ffn1_w_gate, v_ffn1_w_up, v_ffn1_w_down, v_mix_norm, v_w_in, v_pool_w, v_pool_scale, v_w_pool_out, v_q_norm, v_k_norm, v_sinks, v_w_attn_out, v_gate_bias, v_w_out, v_ffn2_norm, v_ffn2_w_gate, v_ffn2_w_up, v_ffn2_w_down):
    T = x.shape[1]
    x2 = x.reshape(T, D_MODEL)
    target = loss_target.reshape(T, D_MODEL)

    big = [
        ("ffn1_w_gate", ffn1_w_gate, m_ffn1_w_gate, v_ffn1_w_gate, True, False),
        ("ffn1_w_up", ffn1_w_up, m_ffn1_w_up, v_ffn1_w_up, True, False),
        ("ffn1_w_down", ffn1_w_down, m_ffn1_w_down, v_ffn1_w_down, False, False),
        ("w_in", w_in, m_w_in, v_w_in, True, False),
        ("w_pool_out", w_pool_out, m_w_pool_out, v_w_pool_out, False, True),
        ("w_attn_out", w_attn_out, m_w_attn_out, v_w_attn_out, False, False),
        ("w_out", w_out, m_w_out, v_w_out, False, False),
        ("ffn2_w_gate", ffn2_w_gate, m_ffn2_w_gate, v_ffn2_w_gate, True, False),
        ("ffn2_w_up", ffn2_w_up, m_ffn2_w_up, v_ffn2_w_up, True, False),
        ("ffn2_w_down", ffn2_w_down, m_ffn2_w_down, v_ffn2_w_down, False, False),
    ]
    view = lambda a, tv: a.T if tv else a
    shards = _prep("prep_weights", [view(w, tv) for _, w, _, _, tv, _ in big], [tk_ for *_, tk_ in big])
    g1 =ffn1_norm.reshape(1, D_MODEL)
    g2 = mix_norm.reshape(1, D_MODEL)
    g3 = ffn2_norm.reshape(1, D_MODEL)
    bias_row = gate_bias.reshape(1, 2 * D_MODEL)
    qg, kg = _tile_gain(q_norm) * ATTN_SCALE, _tile_gain(k_norm)
    scale_row = pool_scale.reshape(1, POOL_WIDTH)

    n1, ((wg1T, wu1T),) = _rms_fwd("ffn1_norm", x2, g1, [_gather_task(shards[0:2], forward_at=0.9)])
    (gt1, up1, act1), ((wd1,), (w_inT,)) = _mm(
        "ffn1_gate_up", [(n1, wg1T, "nt", 0), (n1, wu1T, "nt", 1)], [BF, BF, BF], tm=512, tn=1408, tk=D_MODEL,
        epilogue=_swiglu_fwd_epilogue, cols_outer=True,
        comm=[_gather_task(shards[2:3], forward_at=0.5), _gather_task(shards[3:4], natural=(0,), forward_at=0.9)])
    h1, u = _mm("ffn1_down", [(act1, wd1, "nn", 0)], [F32, BF], tm=512, tn=D_MODEL, tk=D_FF,
                epilogue=_residual_norm_epilogue(0.5), extras=[(x2, "tile", 0), (g2, "row", 0)])
    saved1 = (n1, gt1, up1, act1)
    (proj,), ((w_poT, w_ao, w_o),) = _mm(
        "in_proj", [(u, w_inT, "nt", 0)], [BF], tm=512, tn=1280, tk=D_MODEL, cols_outer=True,
        comm=[_gather_task(shards[4:7], natural=(0, 1, 2), forward_at=0.8)])
    pooled, mixed = _pool_fwd("pool_fwd", proj, pool_w, scale_row)
    qn = _headnorm_fwd("q_norm", proj, COL_Q, ATTN_WIDTH, qg)
    kn = _headnorm_fwd("k_norm", proj, COL_K, KV_WIDTH, kg)
    attn, ((wg2T, wu2T),) = _attn_fwd("attn_fwd", qn, kn, proj, sinks,
                                      comm=[_gather_task(shards[7:9], forward_at=0.85)])
    (bp,) = _mm("pool_out", [(mixed, w_poT, "nt", 0)], [BF], tm=1024, tn=D_MODEL, tk=POOL_WIDTH)
    gate_tn = 256
    gate_extras = [(proj, "tile", COL_GP // gate_tn), (proj, "tile", COL_GA // gate_tn),
                   (bias_row, "row", 0), (bias_row, "row", D_MODEL // gate_tn)]
    merged, ba = _mm("attn_out_merge", [(attn, w_ao, "nn", 0)], [BF, BF], tm=2048, tn=gate_tn, tk=ATTN_WIDTH,
                     epilogue=_merge_fwd_epilogue, extras=[(bp, "tile", 0)] + gate_extras)
    h2, n2 = _mm("mix_out", [(merged, w_o, "nn", 0)], [F32, BF], tm=512, tn=D_MODEL, tk=D_MODEL,
                 epilogue=_residual_norm_epilogue(1.0), extras=[(h1, "tile", 0), (g3, "row", 0)])
    (gt2, up2, act2), ((wd2,),) = _mm(
        "ffn2_gate_up", [(n2, wg2T, "nt", 0), (n2, wu2T, "nt", 1)], [BF, BF, BF], tm=512, tn=1408, tk=D_MODEL,
        epilogue=_swiglu_fwd_epilogue, cols_outer=True, comm=[_gather_task(shards[9:10], forward_at=0.8)])
    dy, dyb, sq = _mm("ffn2_down_loss", [(act2, wd2, "nn", 0)], [F32, BF], tm=512, tn=D_MODEL, tk=D_FF,
                      epilogue=_loss_epilogue, extras=[(h2, "tile", 0), (target, "tile", 0)], n_colsum=1)
    loss_local = 0.5 * jnp.sum(sq) / D_MODEL

    dh2, dh2b, dg3, _, slots_g2, slots_u2, slots_d2 = _ffn_bwd(
        "ffn2", dy, dyb, h2, g3, wg2T, wu2T, wd2, (n2, gt2, up2, act2), [])
    dbp, dba, dgp, dga, cs_gp, cs_ga = _mm(
        "mix_out_bwd", [(dh2b, w_o, "nt", 0)], [BF, BF, BF, BF], tm=2048, tn=gate_tn, tk=D_MODEL,
        epilogue=_merge_bwd_epilogue, extras=[(bp, "tile", 0), (ba, "tile", 0)] + gate_extras, n_colsum=2)
    sum_o = _dw_pair("dw_out", merged, dh2b, 1.0, blocks=4)
    (dmixed,), ((slots_o,),) = _mm("pool_out_bwd", [(dbp, w_poT, "nn", 0)], [BF], tm=1024, tn=POOL_WIDTH, tk=D_MODEL,
                                   comm=[_chip_task([sum_o])])
    sum_po = _dw_pair("dw_pool_out", dbp, mixed, 1.0, blocks=4)
    (dattn,), ((slots_po,),) = _mm("attn_out_bwd", [(dba, w_ao, "nt", 0)], [BF], tm=1024, tn=ATTN_WIDTH, tk=D_MODEL,
                                   comm=[_chip_task([sum_po])])
    sum_ao = _dw_pair("dw_attn_out", attn, dba, 1.0, blocks=4)
    dxp, dpool_w, dpool_scale = _pool_bwd("pool_bwd", dmixed, pooled, pool_w, scale_row)
    dqn, dkn, dv, dsink_tile = _attn_bwd("attn_bwd", dattn, qn, kn, proj, sinks)
    dq, dqg = _headnorm_bwd("q_norm_bwd", dqn, proj, COL_Q, ATTN_WIDTH, qg)
    dk, dkg = _headnorm_bwd("k_norm_bwd", dkn, proj, COL_K, KV_WIDTH, kg)
    dproj = jnp.concatenate([dxp, dq, dk, dv, dgp, dga], axis=1)
    (dh1, dh1b, dg2), ((slots_ao,),) = _mm(
        "in_proj_bwd", [(dproj, w_inT, "nn", 0)], [F32, BF], tm=512, tn=D_MODEL, tk=IN_WIDTH, epilogue=_rms_bwd_epilogue,
        extras=[(h1, "tile", 0), (g2, "row", 0), (dh2, "tile", 0)], n_colsum=1, comm=[_chip_task([sum_ao])])
    (dw_inT,) = _mm("dw_in", [(dproj, u, "tn", 0)], [BF], tm=1280, tn=D_MODEL, tk=2048)
    part_in = dw_inT.reshape(4, 2, IN_WIDTH // N_DEV, D_MODEL)
    (got_in,) = _pair_exchange("pair_exchange_w_in", [part_in])
    core = lax.axis_index("c").astype(jnp.int32).reshape(1)
    sum_in = _pair_sum("pair_sum_w_in", part_in, got_in, core).reshape(IN_WIDTH // 2, D_MODEL)
    dx, _, dg1, ((slots_in,),), slots_g1, slots_u1, slots_d1 = _ffn_bwd(
        "ffn1", dh1, dh1b, x2, g1, wg1T, wu1T, wd1, saved1, [_chip_task([sum_in])])

    slots = [slots_g1, slots_u1, slots_d1, slots_in, slots_po, slots_ao, slots_o, slots_g2, slots_u2, slots_d2]
    big_out = {}
    for k, (nm, w, m, v, tv, tk_) in enumerate(big):
        res = _adamw_sharded("adamw_" + nm, slots[k], view(w, tv), view(m, tv), view(v, tv), tk_)
        big_out[nm] = tuple(view(r, tv) for r in res)

    small_grads = {
        "ffn1_norm": jnp.sum(dg1, axis=(0, 1)), "mix_norm": jnp.sum(dg2, axis=(0, 1)), "ffn2_norm": jnp.sum(dg3, axis=(0, 1)),
        "gate_bias": jnp.concatenate([jnp.sum(cs_gp, axis=(0, 1)), jnp.sum(cs_ga, axis=(0, 1))]),
        "pool_scale": dpool_scale, "q_norm": _fold_heads(dqg) * ATTN_SCALE, "k_norm": _fold_heads(dkg),
        "sinks": dsink_tile[0, :N_HEADS]}
    g_vec, g_pool_w = _all_gather("gather_small_grads", [_pack_small_grads(small_grads, loss_local),
                                                         dpool_w.reshape(-1, LANES)])
    given = {"ffn1_norm": (ffn1_norm, m_ffn1_norm, v_ffn1_norm), "mix_norm": (mix_norm, m_mix_norm, v_mix_norm),
             "ffn2_norm": (ffn2_norm, m_ffn2_norm, v_ffn2_norm), "gate_bias": (gate_bias, m_gate_bias, v_gate_bias),
             "pool_scale": (pool_scale, m_pool_scale, v_pool_scale), "q_norm": (q_norm, m_q_norm, v_q_norm),
             "k_norm": (k_norm, m_k_norm, v_k_norm), "sinks": (sinks, m_sinks, v_sinks)}
    params = [tuple(a.reshape(shape) for a in given[nm]) for nm, _, shape in SMALL_LAYOUT]
    params.append(tuple(a.reshape(-1, LANES) for a in (pool_w, m_pool_w, v_pool_w)))
    small_res, loss_row = _adamw_small("adamw_small", g_vec.reshape(N_DEV, SMALL_ROWS, LANES),
                                       g_pool_w.reshape(N_DEV, -1, LANES), params)
    small_out = {nm: tuple(r.reshape(given[nm][0].shape) for r in res)
                 for (nm, _, _), res in zip(SMALL_LAYOUT, small_res)}
    small_out["pool_w"] = tuple(r.reshape(pool_w.shape) for r in small_res[-1])
    loss = loss_row[0, 0]

    order = ["ffn1_norm", "ffn1_w_gate", "ffn1_w_up", "ffn1_w_down", "mix_norm", "w_in", "pool_w", "pool_scale",
             "w_pool_out", "q_norm", "k_norm", "sinks", "w_attn_out", "gate_bias", "w_out", "ffn2_norm",
             "ffn2_w_gate", "ffn2_w_up", "ffn2_w_down"]
    every = {**big_out, **small_out}
    outs = [loss, dx.reshape(x.shape)]
    for j in range(4):
        outs += [every[nm][j] for nm in order]
    return tuple(outs)
```

```python
import functools

import jax
import jax.numpy as jnp
from jax import lax
from jax.experimental import pallas as pl
from jax.experimental.pallas import tpu as pltpu

BF = jnp.bfloat16
F32 = jnp.float32

D_MODEL = 1024
D_FF = 2816
POOL_WIDTH = 512
POOL_GROUP = 128
N_POOL_GROUPS = 4
HEAD_DIM = 64
N_HEADS = 16
GQA_GROUP = 8
BLOCK = 128
ATTN_WIDTH = 1024
KV_WIDTH = 128
IN_WIDTH = 3840
RMS_EPS = 1e-6
N_DEV = 8
LANES = 128

COL_Q = POOL_WIDTH
COL_K = COL_Q + ATTN_WIDTH
COL_V = COL_K + KV_WIDTH
COL_GP = COL_V + KV_WIDTH
COL_GA = COL_GP + D_MODEL

ADAM_LR = 0.001
ADAM_B1 = 0.9
ADAM_B2 = 0.999
ADAM_EPS = 1e-08
ADAM_WD = 0.01
ADAM_STEP = 10

VMEM_LIMIT_V7X = 56 * 1024 * 1024
MESH = pl.DeviceIdType.MESH
ANY = pl.BlockSpec(memory_space=pl.ANY)


def _params(sem=None):
    return pltpu.CompilerParams(dimension_semantics=sem, vmem_limit_bytes=VMEM_LIMIT_V7X)


_DIMS = {"nt": (((1,), (1,)), ((), ())), "nn": (((1,), (0,)), ((), ())), "tn": (((0,), (0,)), ((), ()))}


class _Task:
    def __init__(self, inputs, out_shapes, scratch, phases):
        self.inputs, self.out_shapes, self.scratch = list(inputs), list(out_shapes), list(scratch)
        self.phases = list(phases)


class _CommPlumbing:
    def __init__(self, tasks):
        self.tasks = list(tasks or [])
        self.args = [a for t in self.tasks for a in t.inputs]
        self.out_shapes = [o for t in self.tasks for o in t.out_shapes]
        self.scratch = [s for t in self.tasks for s in t.scratch]
        self.n_in, self.n_out = len(self.args), len(self.out_shapes)

    def _slices(self, c_in, c_out, c_scr):
        i = o = s = 0
        for t in self.tasks:
            yield t, c_in[i:i + len(t.inputs)], c_out[o:o + len(t.out_shapes)], c_scr[s:s + len(t.scratch)]
            i, o, s = i + len(t.inputs), o + len(t.out_shapes), s + len(t.scratch)

    def run(self, step, steps, before, c_in, c_out, c_scr):
        for t, ins, outs, scr in self._slices(c_in, c_out, c_scr):
            for frac, fn in t.phases:
                if step is None:
                    fn(ins, outs, scr)
                elif before == (frac == 0):
                    at = 0 if frac == 0 else max(0, min(steps, -(-int(round(frac * steps * 64)) // 64)) - 1)
                    pl.when(step == at)(functools.partial(fn, ins, outs, scr))

    def split_outputs(self, flat):
        res, o = [], 0
        for t in self.tasks:
            res.append(list(flat[o:o + len(t.out_shapes)]))
            o += len(t.out_shapes)
        return res


def _comm_only(name, tasks):
    plumb = _CommPlumbing(tasks)

    def body(*refs):
        c_in, c_out = refs[:plumb.n_in], refs[plumb.n_in: plumb.n_in + plumb.n_out]
        c_scr = refs[plumb.n_in + plumb.n_out:]
        plumb.run(None, 1, True, c_in, c_out, c_scr)

    res = pl.pallas_call(
        body, name=name, in_specs=[ANY] * plumb.n_in, out_specs=[ANY] * plumb.n_out, out_shape=plumb.out_shapes,
        scratch_shapes=plumb.scratch, compiler_params=pltpu.CompilerParams(has_side_effects=True),
    )(*plumb.args)
    return plumb.split_outputs(res)


def _mm(name, terms, out_dtypes, *, tm, tn, tk, epilogue=None, extras=(), n_colsum=0, comm=None, cols_outer=False):
    a0, b0, mode0, _ = terms[0]
    if mode0 == "nt":
        (M, K), N = a0.shape, b0.shape[0]
    elif mode0 == "nn":
        (M, K), N = a0.shape, b0.shape[1]
    else:
        (K, M), N = a0.shape, b0.shape[1]
    tm, tn, tk = min(tm, M), min(tn, N), min(tk, K)
    assert M % tm == 0 and N % tn == 0 and K % tk == 0, (name, M, N, K, tm, tn, tk)
    nI, nJ, nK = M // tm, N // tn, K // tk
    n_terms = len(terms)
    n_acc = max(t[3] for t in terms) + 1
    n_ex = len(extras)
    n_out = len(out_dtypes)
    if epilogue is None:
        epilogue = lambda accs, ex: ([accs[0]], [])
    plumb = _CommPlumbing(comm)
    n_scr = n_acc if nK > 1 else 0
    grid = (nJ, nI, nK) if cols_outer else (nI, nJ, nK)

    def body(*refs):
        n_in = 2 * n_terms + n_ex
        ab = refs[: 2 * n_terms]
        ex_refs = refs[2 * n_terms: n_in]
        c_in = refs[n_in: n_in + plumb.n_in]
        o0 = n_in + plumb.n_in
        out_refs = refs[o0: o0 + n_out]
        cs_refs = refs[o0 + n_out: o0 + n_out + n_colsum]
        c_out = refs[o0 + n_out + n_colsum: o0 + n_out + n_colsum + plumb.n_out]
        s0 = o0 + n_out + n_colsum + plumb.n_out
        acc_refs = refs[s0: s0 + n_scr]
        c_scr = refs[s0 + n_scr:]
        steps = grid[0] * grid[1] * nK
        if comm:
            step = (pl.program_id(0) * grid[1] + pl.program_id(1)) * nK + pl.program_id(2)
            plumb.run(step, steps, True, c_in, c_out, c_scr)

        def products():
            accs = [None] * n_acc
            for t, (_, _, mode, ai) in enumerate(terms):
                p = lax.dot_general(ab[2 * t][...], ab[2 * t + 1][...], _DIMS[mode], preferred_element_type=F32)
                accs[ai] = p if accs[ai] is None else accs[ai] + p
            return accs

        def finish(accs):
            outs, colsums = epilogue(accs, [r[...] for r in ex_refs])
            for r, o in zip(out_refs, outs):
                r[...] = o.astype(r.dtype)
            for r, cs in zip(cs_refs, colsums):
                r[...] = jnp.sum(cs, axis=0, keepdims=True).reshape(r.shape)

        if nK == 1:
            finish(products())
        else:
            k = pl.program_id(2)
            accs = products()

            @pl.when(k == 0)
            def _():
                for r, a in zip(acc_refs, accs):
                    r[...] = a

            @pl.when(k > 0)
            def _():
                for r, a in zip(acc_refs, accs):
                    r[...] += a

            @pl.when(k == nK - 1)
            def _():
                finish([r[...] for r in acc_refs])

        if comm:
            plumb.run(step, steps, False, c_in, c_out, c_scr)

    def spec(block, index, fixed=False):
        imap = (lambda q, p, k: index(p, q, k)) if cols_outer else index
        return pl.BlockSpec(block, imap, pipeline_mode=pl.Buffered(1)) if fixed else pl.BlockSpec(block, imap)

    in_specs, args = [], []
    for a, b, mode, _ in terms:
        if mode == "nt":
            in_specs += [spec((tm, tk), lambda i, j, k: (i, k), nI * nK == 1),
                         spec((tn, tk), lambda i, j, k: (j, k), nJ * nK == 1)]
        elif mode == "nn":
            in_specs += [spec((tm, tk), lambda i, j, k: (i, k), nI * nK == 1),
                         spec((tk, tn), lambda i, j, k: (k, j), nJ * nK == 1)]
        else:
            in_specs += [spec((tk, tm), lambda i, j, k: (k, i), nI * nK == 1),
                         spec((tk, tn), lambda i, j, k: (k, j), nJ * nK == 1)]
        args += [a, b]
    for arr, kind, off in extras:
        if kind == "tile":
            in_specs.append(spec((tm, tn), functools.partial(lambda i, j, k, off: (i, j + off), off=off)))
        else:
            in_specs.append(spec((1, tn), functools.partial(lambda i, j, k, off: (0, j + off), off=off)))
        args.append(arr)
    out_shape = [jax.ShapeDtypeStruct((M, N), dt) for dt in out_dtypes]
    out_specs = [spec((tm, tn), lambda i, j, k: (i, j)) for _ in out_dtypes]
    out_shape += [jax.ShapeDtypeStruct((nI, 1, N), F32) for _ in range(n_colsum)]
    out_specs += [spec((1, 1, tn), lambda i, j, k: (i, 0, j)) for _ in range(n_colsum)]
    scratch = [pltpu.VMEM((tm, tn), F32) for _ in range(n_scr)]
    args += plumb.args
    in_specs += [ANY] * plumb.n_in
    out_shape += plumb.out_shapes
    out_specs += [ANY] * plumb.n_out
    sem = ("arbitrary",) * 3 if comm else ("parallel", "parallel", "arbitrary")
    res = pl.pallas_call(
        body, name=name, grid=grid, in_specs=in_specs, out_specs=out_specs, out_shape=out_shape,
        scratch_shapes=scratch + plumb.scratch, compiler_params=_params(sem),
    )(*args)
    n_own = n_out + n_colsum
    return (list(res[:n_own]), plumb.split_outputs(res[n_own:])) if comm is not None else res


ROW_TILE = 512


def _rms_fwd(name, x, g, comm):
    T, D = x.shape
    steps = T // ROW_TILE
    plumb = _CommPlumbing(comm)

    def body(x_ref, g_ref, *rest):
        c_in, o_ref = rest[:plumb.n_in], rest[plumb.n_in]
        c_out, c_scr = rest[plumb.n_in + 1: plumb.n_in + 1 + plumb.n_out], rest[plumb.n_in + 1 + plumb.n_out:]
        plumb.run(pl.program_id(0), steps, True, c_in, c_out, c_scr)
        xv = x_ref[...]
        r = lax.rsqrt(jnp.mean(xv * xv, axis=-1, keepdims=True) + RMS_EPS)
        o_ref[...] = (xv * r * g_ref[...]).astype(BF)
        plumb.run(pl.program_id(0), steps, False, c_in, c_out, c_scr)

    row = pl.BlockSpec((ROW_TILE, D), lambda i: (i, 0))
    res = pl.pallas_call(
        body, name=name, grid=(steps,),
        in_specs=[row, pl.BlockSpec((1, D), lambda i: (0, 0))] + [ANY] * plumb.n_in,
        out_specs=[row] + [ANY] * plumb.n_out, out_shape=[jax.ShapeDtypeStruct((T, D), BF)] + plumb.out_shapes,
        scratch_shapes=plumb.scratch, compiler_params=_params(("arbitrary",)),
    )(x, g, *plumb.args)
    return res[0], plumb.split_outputs(res[1:])


HEADNORM_TILE = 1024


def _half_sum_matrix():
    r = lax.broadcasted_iota(jnp.int32, (LANES, LANES), 0) // HEAD_DIM
    c = lax.broadcasted_iota(jnp.int32, (LANES, LANES), 1) // HEAD_DIM
    return (r == c).astype(BF)


def _head_mean(v, ones_blockdiag):
    hi = v.astype(BF)
    lo = (v - hi.astype(F32)).astype(BF)
    s = jnp.dot(hi, ones_blockdiag, preferred_element_type=F32) + jnp.dot(lo, ones_blockdiag, preferred_element_type=F32)
    return s * (1.0 / HEAD_DIM)


def _headnorm_fwd(name, proj, col0, width, g2):
    T = proj.shape[0]
    wide = min(width, GROUP_WIDTH)
    nb, off = width // wide, col0 // wide

    def body(x_ref, g_ref, b_ref, o_ref):
        for s in range(wide // LANES):
            lanes = slice(LANES * s, LANES * (s + 1))
            xv = x_ref[:, lanes].astype(F32)
            r = lax.rsqrt(_head_mean(xv * xv, b_ref[...]) + RMS_EPS)
            o_ref[:, lanes] = (xv * r * g_ref[...]).astype(BF)

    return pl.pallas_call(
        body, name=name, grid=(T // HEADNORM_TILE, nb),
        in_specs=[pl.BlockSpec((HEADNORM_TILE, wide), lambda i, j: (i, j + off)),
                  pl.BlockSpec((1, LANES), lambda i, j: (0, 0)), pl.BlockSpec((LANES, LANES), lambda i, j: (0, 0))],
        out_specs=pl.BlockSpec((HEADNORM_TILE, wide), lambda i, j: (i, j)),
        out_shape=jax.ShapeDtypeStruct((T, width), BF), compiler_params=_params(("parallel", "parallel")),
    )(proj, g2, _half_sum_matrix())


def _headnorm_bwd(name, dy, proj, col0, width, g2):
    T = proj.shape[0]
    wide = min(width, GROUP_WIDTH)
    nb, off = width // wide, col0 // wide

    def body(dy_ref, x_ref, g_ref, b_ref, dx_ref, dg_ref):
        for s in range(wide // LANES):
            lanes = slice(LANES * s, LANES * (s + 1))
            xv = x_ref[:, lanes].astype(F32)
            dyv = dy_ref[:, lanes].astype(F32)
            r = lax.rsqrt(_head_mean(xv * xv, b_ref[...]) + RMS_EPS)
            xhat = xv * r
            dxhat = dyv * g_ref[...]
            dx_ref[:, lanes] = (r * (dxhat - xhat * _head_mean(dxhat * xhat, b_ref[...]))).astype(BF)
            dg_ref[0, :, lanes] = jnp.sum(dyv * xhat, axis=0, keepdims=True)

    return pl.pallas_call(
        body, name=name, grid=(T // HEADNORM_TILE, nb),
        in_specs=[pl.BlockSpec((HEADNORM_TILE, wide), lambda i, j: (i, j)),
                  pl.BlockSpec((HEADNORM_TILE, wide), lambda i, j: (i, j + off)),
                  pl.BlockSpec((1, LANES), lambda i, j: (0, 0)), pl.BlockSpec((LANES, LANES), lambda i, j: (0, 0))],
        out_specs=[pl.BlockSpec((HEADNORM_TILE, wide), lambda i, j: (i, j)),
                   pl.BlockSpec((1, 1, wide), lambda i, j: (i, 0, j))],
        out_shape=[jax.ShapeDtypeStruct((T, width), BF), jax.ShapeDtypeStruct((T // HEADNORM_TILE, 1, width), F32)],
        compiler_params=_params(("parallel", "parallel")),
    )(dy, proj, g2, _half_sum_matrix())


def _shift_down(v, k, row):
    return jnp.where(row >= k, pltpu.roll(v, k, axis=0), 0.0)


def _shift_up(v, k, row, T):
    return jnp.where(row < T - k, pltpu.roll(v, T - k, axis=0), 0.0)


def _by_group(g, vals):
    out = vals[-1]
    for i in range(len(vals) - 2, -1, -1):
        out = jnp.where(g == i, vals[i], out)
    return out


def _pool_fwd(name, proj, pool_w, pool_scale):
    T = proj.shape[0]

    def body(x_ref, w_ref, s_ref, pooled_ref, mixed_ref):
        g = pl.program_id(0)
        xv = x_ref[...].astype(F32)
        row = lax.broadcasted_iota(jnp.int32, (T, 1), 0)
        s2 = xv + _shift_down(xv, 1, row)
        s4 = s2 + _shift_down(s2, 2, row)
        s8 = s4 + _shift_down(s4, 4, row)
        s16 = s8 + _shift_down(s8, 8, row)
        wsum = _by_group(g, [s2, s4, s8, s16])
        count = jnp.minimum(row + 1, 2 << g).astype(F32)
        pooled = (wsum / count - xv).astype(BF)
        pooled_ref[...] = pooled
        mixed = jnp.dot(pooled, w_ref[0].astype(BF), preferred_element_type=F32) * s_ref[...]
        mixed_ref[...] = mixed.astype(BF)

    col = pl.BlockSpec((T, POOL_GROUP), lambda g: (0, g))
    return pl.pallas_call(
        body, name=name, grid=(N_POOL_GROUPS,),
        in_specs=[col, pl.BlockSpec((1, POOL_GROUP, POOL_GROUP), lambda g: (g, 0, 0)),
                  pl.BlockSpec((1, POOL_GROUP), lambda g: (0, g))],
        out_specs=[col, col],
        out_shape=[jax.ShapeDtypeStruct((T, POOL_WIDTH), BF), jax.ShapeDtypeStruct((T, POOL_WIDTH), BF)],
        compiler_params=_params(("parallel",)),
    )(proj, pool_w, pool_scale)


def _pool_bwd(name, dmixed, pooled, pool_w, pool_scale):
    T = dmixed.shape[0]

    def body(dm_ref, p_ref, w_ref, s_ref, dx_ref, dw_ref, ds_ref):
        g = pl.program_id(0)
        dm = dm_ref[...].astype(F32)
        pooled = p_ref[...]
        w = w_ref[0].astype(BF)
        pre = jnp.dot(pooled, w, preferred_element_type=F32)
        ds_ref[...] = jnp.sum(dm * pre, axis=0, keepdims=True)
        dms = (dm * s_ref[...]).astype(BF)
        dw_ref[0] = lax.dot_general(pooled, dms, _DIMS["tn"], preferred_element_type=F32)
        dpooled = lax.dot_general(dms, w, _DIMS["nt"], preferred_element_type=F32)
        row = lax.broadcasted_iota(jnp.int32, (T, 1), 0)
        count = jnp.minimum(row + 1, 2 << g).astype(F32)
        z = dpooled / count
        l2 = z + _shift_up(z, 1, row, T)
        l4 = l2 + _shift_up(l2, 2, row, T)
        l8 = l4 + _shift_up(l4, 4, row, T)
        l16 = l8 + _shift_up(l8, 8, row, T)
        dx_ref[...] = (_by_group(g, [l2, l4, l8, l16]) - dpooled).astype(BF)

    col = pl.BlockSpec((T, POOL_GROUP), lambda g: (0, g))
    wspec = pl.BlockSpec((1, POOL_GROUP, POOL_GROUP), lambda g: (g, 0, 0))
    sspec = pl.BlockSpec((1, POOL_GROUP), lambda g: (0, g))
    return pl.pallas_call(
        body, name=name, grid=(N_POOL_GROUPS,), in_specs=[col, col, wspec, sspec], out_specs=[col, wspec, sspec],
        out_shape=[jax.ShapeDtypeStruct((T, POOL_WIDTH), BF),
                   jax.ShapeDtypeStruct((N_POOL_GROUPS, POOL_GROUP, POOL_GROUP), F32),
                   jax.ShapeDtypeStruct((1, POOL_WIDTH), F32)],
        compiler_params=_params(("parallel",)),
    )(dmixed, pooled, pool_w, pool_scale)


ATTN_SCALE = HEAD_DIM ** -0.5
MASKED = float(jnp.finfo(jnp.float32).min)
KV_COL_BLOCK_K = COL_K // LANES
KV_COL_BLOCK_V = COL_V // LANES
GROUP_WIDTH = GQA_GROUP * HEAD_DIM


def _dup_head(v, j):
    half = lax.broadcasted_iota(jnp.int32, (1, LANES), 1) // HEAD_DIM
    return jnp.where(half == j, v, pltpu.roll(v, HEAD_DIM, axis=1))


def _stack_heads(v, low):
    pieces = []
    for p in range(GROUP_WIDTH // LANES):
        vp = v[:, LANES * p: LANES * (p + 1)]
        pieces.append(jnp.where(low, vp, jnp.zeros_like(vp)))
        pieces.append(jnp.where(low, jnp.zeros_like(vp), vp))
    return jnp.concatenate(pieces, axis=0)


def _unstack_transposed(t, low):
    pairs = []
    for p in range(GROUP_WIDTH // LANES):
        even = t[:, BLOCK * (2 * p): BLOCK * (2 * p + 1)].T
        odd = t[:, BLOCK * (2 * p + 1): BLOCK * (2 * p + 2)].T
        pairs.append(jnp.where(low, even, odd))
    return pairs


def _softmax_keys_on_sublanes(k2, q, n, sink_ref, j):
    stacked = GQA_GROUP * BLOCK
    key = lax.broadcasted_iota(jnp.int32, (2 * BLOCK, stacked), 0)
    qry = lax.broadcasted_iota(jnp.int32, (2 * BLOCK, stacked), 1) % BLOCK
    valid = (key > qry) & (key <= qry + BLOCK) & ((n > 0) | (key >= BLOCK))
    head_of_lane = lax.broadcasted_iota(jnp.int32, (1, stacked), 1) // BLOCK
    sink = jnp.zeros((1, stacked), F32)
    for h in range(GQA_GROUP):
        sink = jnp.where(head_of_lane == h, sink_ref[j * GQA_GROUP + h], sink)
    s = jnp.where(valid, lax.dot_general(k2, q, _DIMS["nt"], preferred_element_type=F32), MASKED)
    m = jnp.maximum(jnp.max(s, axis=0, keepdims=True), sink)
    e = jnp.exp(s - m)
    e_sink = jnp.exp(sink - m)
    inv = 1.0 / (jnp.sum(e, axis=0, keepdims=True) + e_sink)
    return e * inv, e_sink * inv


def _attn_fwd(name, qn, kn, proj, sinks, comm=None):
    T = qn.shape[0]
    nb = T // BLOCK
    plumb = _CommPlumbing(comm)

    def body(sink_ref, q_ref, kp_ref, kc_ref, vp_ref, vc_ref, *rest):
        c_in, o_ref = rest[:plumb.n_in], rest[plumb.n_in]
        c_out, c_scr = rest[plumb.n_in + 1: plumb.n_in + 1 + plumb.n_out], rest[plumb.n_in + 1 + plumb.n_out:]
        n, j = pl.program_id(0), pl.program_id(1)
        plumb.run(2 * n + j, 2 * nb, True, c_in, c_out, c_scr)
        low = lax.broadcasted_iota(jnp.int32, (1, LANES), 1) < HEAD_DIM
        k2 = _dup_head(jnp.concatenate([kp_ref[...], kc_ref[...]], axis=0), j)
        v2 = _dup_head(jnp.concatenate([vp_ref[...], vc_ref[...]], axis=0), j)
        p, _ = _softmax_keys_on_sublanes(k2, _stack_heads(q_ref[...], low), n, sink_ref, j)
        o_t = lax.dot_general(v2, p.astype(BF), _DIMS["tn"], preferred_element_type=F32)
        for pair, o in enumerate(_unstack_transposed(o_t, low)):
            o_ref[:, LANES * pair: LANES * (pair + 1)] = o.astype(BF)
        plumb.run(2 * n + j, 2 * nb, False, c_in, c_out, c_scr)

    group = pl.BlockSpec((BLOCK, GROUP_WIDTH), lambda n, j: (n, j))
    res = pl.pallas_call(
        body, name=name, grid=(nb, 2),
        in_specs=[pl.BlockSpec(memory_space=pltpu.SMEM), group,
                  pl.BlockSpec((BLOCK, LANES), lambda n, j: (jnp.maximum(n - 1, 0), 0)),
                  pl.BlockSpec((BLOCK, LANES), lambda n, j: (n, 0)),
                  pl.BlockSpec((BLOCK, LANES), lambda n, j: (jnp.maximum(n - 1, 0), KV_COL_BLOCK_V)),
                  pl.BlockSpec((BLOCK, LANES), lambda n, j: (n, KV_COL_BLOCK_V))] + [ANY] * plumb.n_in,
        out_specs=[group] + [ANY] * plumb.n_out,
        out_shape=[jax.ShapeDtypeStruct((T, ATTN_WIDTH), BF)] + plumb.out_shapes, scratch_shapes=plumb.scratch,
        compiler_params=_params(("arbitrary", "arbitrary") if comm else ("parallel", "parallel")),
    )(sinks, qn, kn, kn, proj, proj, *plumb.args)
    return (res[0], plumb.split_outputs(res[1:])) if comm is not None else res[0]


def _attn_bwd(name, dout, qn, kn, proj, sinks):
    T = qn.shape[0]
    nb = T // BLOCK

    def body(sink_ref, do_ref, q_ref, kp_ref, kc_ref, vp_ref, vc_ref, dq_ref, dk_ref, dv_ref, dsink_ref,
             carry_k, carry_v, tot_k, tot_v):
        n = pl.program_id(0)
        lane = lax.broadcasted_iota(jnp.int32, (1, LANES), 1)
        low = lane < HEAD_DIM

        @pl.when(n == 0)
        def _():
            carry_k[...] = jnp.zeros_like(carry_k)
            carry_v[...] = jnp.zeros_like(carry_v)
            dsink_ref[...] = jnp.zeros_like(dsink_ref)

        @pl.when(n == nb)
        def _():
            tot_k[...] = jnp.zeros_like(tot_k)
            tot_v[...] = jnp.zeros_like(tot_v)

        @pl.when(n < nb)
        def _():
            kk = jnp.concatenate([kp_ref[...], kc_ref[...]], axis=0)
            vv = jnp.concatenate([vp_ref[...], vc_ref[...]], axis=0)
            dk_tot = jnp.zeros((2 * BLOCK, LANES), F32)
            dv_tot = jnp.zeros((2 * BLOCK, LANES), F32)
            dsink = jnp.zeros((1, LANES), F32)
            for j in range(2):
                k2 = _dup_head(kk, j)
                v2 = _dup_head(vv, j)
                q = _stack_heads(q_ref[:, GROUP_WIDTH * j: GROUP_WIDTH * (j + 1)], low)
                do = _stack_heads(do_ref[:, GROUP_WIDTH * j: GROUP_WIDTH * (j + 1)], low)
                p, psink = _softmax_keys_on_sublanes(k2, q, n, sink_ref, j)
                dp =lax.dot_general(v2, do, _DIMS["nt"], preferred_element_type=F32)
                delta = jnp.sum(p * dp, axis=0, keepdims=True)
                ds = (p * (dp - delta)).astype(BF)
                dk2 = jnp.dot(ds, q, preferred_element_type=F32)
                dv2 = jnp.dot(p.astype(BF), do, preferred_element_type=F32)
                dq_t = lax.dot_general(k2, ds, _DIMS["tn"], preferred_element_type=F32)
                for pair, dq in enumerate(_unstack_transposed(dq_t, low)):
                    lanes = slice(GROUP_WIDTH * j + LANES * pair, GROUP_WIDTH * j + LANES * (pair + 1))
                    dq_ref[:, lanes] = dq.astype(BF)
                mine = low if j == 0 else jnp.logical_not(low)
                dk_tot = dk_tot + jnp.where(mine, dk2 + pltpu.roll(dk2, HEAD_DIM, axis=1), 0.0)
                dv_tot = dv_tot + jnp.where(mine, dv2 + pltpu.roll(dv2, HEAD_DIM, axis=1), 0.0)
                sink_term = psink * delta
                for h in range(GQA_GROUP):
                    val = -jnp.sum(sink_term[:, BLOCK * h: BLOCK * (h + 1)], axis=1, keepdims=True)
                    dsink = dsink + jnp.where(lane == j * GQA_GROUP + h, val, 0.0)
            tot_k[...] = dk_tot
            tot_v[...] = dv_tot
            dsink_ref[0:1, :] += dsink

        dk_ref[...] = (carry_k[...] + tot_k[0:BLOCK]).astype(BF)
        dv_ref[...] = (carry_v[...] + tot_v[0:BLOCK]).astype(BF)
        carry_k[...] = tot_k[BLOCK:]
        carry_v[...] = tot_v[BLOCK:]

    cur = lambda n: (jnp.minimum(n, nb - 1), 0)
    prev = lambda n: (jnp.maximum(n - 1, 0), 0)
    wide = pl.BlockSpec((BLOCK, ATTN_WIDTH), cur)
    return pl.pallas_call(
        body, name=name, grid=(nb + 1,),
        in_specs=[pl.BlockSpec(memory_space=pltpu.SMEM), wide, wide,
                  pl.BlockSpec((BLOCK, LANES), prev), pl.BlockSpec((BLOCK, LANES), cur),
                  pl.BlockSpec((BLOCK, LANES), lambda n: (jnp.maximum(n - 1, 0), KV_COL_BLOCK_V)),
                  pl.BlockSpec((BLOCK, LANES), lambda n: (jnp.minimum(n, nb - 1), KV_COL_BLOCK_V))],
        out_specs=[wide, pl.BlockSpec((BLOCK, LANES), prev), pl.BlockSpec((BLOCK, LANES), prev),
                   pl.BlockSpec((8, LANES), lambda n: (0, 0))],
        out_shape=[jax.ShapeDtypeStruct((T, ATTN_WIDTH), BF), jax.ShapeDtypeStruct((T, KV_WIDTH), BF),
                   jax.ShapeDtypeStruct((T, KV_WIDTH), BF), jax.ShapeDtypeStruct((8, LANES), F32)],
        scratch_shapes=[pltpu.VMEM((BLOCK, LANES), F32), pltpu.VMEM((BLOCK, LANES), F32),
                        pltpu.VMEM((2 * BLOCK, LANES), F32), pltpu.VMEM((2 * BLOCK, LANES), F32)],
        compiler_params=_params(("arbitrary",)),
    )(sinks, dout, qn, kn, kn, proj, proj)


def _swiglu_fwd_epilogue(accs, ex):
    g, u = accs
    return [g, u, g * jax.nn.sigmoid(g) * u], []


def _swiglu_bwd_epilogue(accs, ex):
    (da,) = accs
    g, u = ex[0].astype(F32), ex[1].astype(F32)
    s = jax.nn.sigmoid(g)
    return [da * u * (s * (1.0 + g * (1.0 - s))), da * (g * s)], []


def _residual_norm_epilogue(scale):
    def epilogue(accs, ex):
        res, gain = ex
        h = res + scale * accs[0]
        r = lax.rsqrt(jnp.mean(h * h, axis=-1, keepdims=True) + RMS_EPS)
        return [h, h * r * gain], []
    return epilogue


def _rms_bwd_epilogue(accs, ex):
    (dn,) = accs
    xv, g, dres = ex
    r = lax.rsqrt(jnp.mean(xv * xv, axis=-1, keepdims=True) + RMS_EPS)
    xhat = xv * r
    dxhat = dn * g
    dx = dres + r * (dxhat - xhat * jnp.mean(dxhat * xhat, axis=-1, keepdims=True))
    return [dx, dx], [dn * xhat]


def _loss_epilogue(accs, ex):
    xv, target = ex
    d = xv + 0.5 * accs[0] - target
    dy = d * (1.0 / D_MODEL)
    return [dy, dy], [d * d]


def _merge_fwd_epilogue(accs, ex):
    (ba,) = accs
    bp, gp_pre, ga_pre, bias_p, bias_a = ex
    gp = jax.nn.sigmoid(gp_pre.astype(F32) + bias_p)
    ga = jax.nn.sigmoid(ga_pre.astype(F32) + bias_a)
    return [gp * bp.astype(F32) + ga * ba, ba], []


def _merge_bwd_epilogue(accs, ex):
    (dm,) = accs
    bp, ba, gp_pre, ga_pre, bias_p, bias_a = ex
    gp = jax.nn.sigmoid(gp_pre.astype(F32) + bias_p)
    ga = jax.nn.sigmoid(ga_pre.astype(F32) + bias_a)
    dgp = dm * bp.astype(F32) * gp * (1.0 - gp)
    dga = dm * ba.astype(F32) * ga * (1.0 - ga)
    return [dm * gp, dm * ga, dgp, dga], [dgp, dga]


def _prep(name, ws, transposes):
    n = len(ws)

    def body(*refs):
        for w_ref, o_ref, tr in zip(refs[:n], refs[n:], transposes):
            v = w_ref[...]
            o_ref[...] = (v.T if tr else v).astype(BF)

    shapes = [jax.ShapeDtypeStruct(w.shape[::-1] if tr else w.shape, BF) for w, tr in zip(ws, transposes)]
    return pl.pallas_call(body, name=name, out_shape=shapes, compiler_params=_params())(*ws)


def _adam_math(w, g, m, v):
    m = ADAM_B1 * m + (1.0 - ADAM_B1) * g
    v = ADAM_B2 * v + (1.0 - ADAM_B2) * jnp.square(g)
    m_hat = m / (1.0 - ADAM_B1 ** ADAM_STEP)
    v_hat = v / (1.0 - ADAM_B2 ** ADAM_STEP)
    delta = -ADAM_LR * (m_hat / (jnp.sqrt(v_hat) + ADAM_EPS) + ADAM_WD * w)
    return delta, m, v


def _adamw_sharded(name, slots, w, m, v, transpose):
    def body(s_ref, w_ref, m_ref, v_ref, g_out, d_out, m_out, v_out):
        g = s_ref[0].astype(F32)
        for i in range(1, 4):
            g = g + s_ref[i].astype(F32)
        if transpose:
            g = g.T
        delta, mn, vn = _adam_math(w_ref[...], g, m_ref[...], v_ref[...])
        g_out[...] = g
        d_out[...] = delta
        m_out[...] = mn
        v_out[...] = vn

    out_shape = [jax.ShapeDtypeStruct(w.shape, F32)] * 4
    _, r, C = slots.shape
    rows = r // 4
    if transpose or rows % 8:
        return pl.pallas_call(body, name=name, out_shape=out_shape, compiler_params=_params())(slots, w, m, v)
    tile = pl.BlockSpec((rows, C), lambda i: (i, 0))
    return pl.pallas_call(
        body, name=name, grid=(4,), in_specs=[pl.BlockSpec((4, rows, C), lambda i: (0, i, 0)), tile, tile, tile],
        out_specs=[tile] * 4, out_shape=out_shape, compiler_params=_params(("parallel",)),
    )(slots, w, m, v)


SMALL_LAYOUT = (("ffn1_norm", 0, (8, LANES)), ("mix_norm", 8, (8, LANES)), ("ffn2_norm", 16, (8, LANES)),
                ("gate_bias", 24, (16, LANES)), ("pool_scale", 40, (4, LANES)), ("q_norm", 48, (1, HEAD_DIM)),
                ("k_norm", 56, (1, HEAD_DIM)), ("sinks", 64, (1, N_HEADS)))
LOSS_ROW = 72
SMALL_ROWS = 80


def _adamw_small(name, g_vec, g_pool_w, params):
    n = len(SMALL_LAYOUT) + 1

    def body(vec_ref, pw_ref, *refs):
        ins, outs = refs[:3 * n], refs[3 * n:]
        vec = vec_ref[0]
        pw = pw_ref[0]
        for i in range(1, N_DEV):
            vec = vec + vec_ref[i]
            pw = pw + pw_ref[i]
        grads = [vec[r0:r0 + shape[0], 0:shape[1]] for _, r0, shape in SMALL_LAYOUT] + [pw]
        for p, g in enumerate(grads):
            w_ref, m_ref, v_ref = ins[3 * p: 3 * p + 3]
            delta, mn, vn = _adam_math(w_ref[...], g, m_ref[...], v_ref[...])
            for o_ref, val in zip(outs[4 * p: 4 * p + 4], (g, delta, mn, vn)):
                o_ref[...] = val
        outs[4 * n][...] = vec[LOSS_ROW:LOSS_ROW + 1, :]

    flat = [a for wmv in params for a in wmv]
    out_shape = [jax.ShapeDtypeStruct(wmv[0].shape, F32) for wmv in params for _ in range(4)]
    out_shape.append(jax.ShapeDtypeStruct((1, LANES), F32))
    res = pl.pallas_call(body, name=name, out_shape=out_shape, compiler_params=_params())(g_vec, g_pool_w, *flat)
    return [tuple(res[4 * p: 4 * p + 4]) for p in range(n)], res[4 * n]


def _place():
    x, y, c = lax.axis_index("x"), lax.axis_index("y"), lax.axis_index("c")
    other_chips = [(1 - x, y), (x, 1 - y), (1 - x, 1 - y)]
    return x, y, c, other_chips


def _rows(ref, r, place, natural=False):
    px, py, pc = place
    b = 4 * px + 2 * py + pc if natural else 4 * pc + 2 * px + py
    return ref.at[pl.ds(pl.multiple_of(b * r, 8), r), :]


def _gather_task(shards, natural=(), forward_at=0.75):
    n = len(shards)
    rs = [s.shape[0] for s in shards]
    rows_of = lambda ref, k, place: _rows(ref, rs[k], place, k in natural)

    def copy(scr, outs, k, slot, block, to, src=None):
        rows = rows_of(outs[k], k, block)
        return pltpu.make_async_remote_copy(
            src_ref=rows if src is None else src, dst_ref=rows, send_sem=scr[0].at[7 * k + slot],
            recv_sem=scr[1].at[7 * k + slot], device_id=to, device_id_type=MESH)

    def first_sends(ins, outs, scr):
        x, y, c, chips = _place()
        me = (x, y, c)
        cps = [copy(scr, outs, k, 1 + j, me, (*chip, c), src=ins[k]) for j, chip in enumerate(chips) for k in range(n)]
        return cps + [copy(scr, outs, k, 0, me, (x, y, 1 - c), src=ins[k]) for k in range(n)]

    def passed_on(outs, scr):
        x, y, c, chips = _place()
        return [copy(scr, outs, k, 4 + j, (*chip, c), (x, y, 1 - c)) for j, chip in enumerate(chips) for k in range(n)]

    def local(ins, outs, scr):
        x, y, c, _ = _place()
        return [pltpu.make_async_copy(ins[k], rows_of(outs[k], k, (x, y, c)), scr[2].at[k]) for k in range(n)]

    def start(ins, outs, scr):
        for cp in local(ins, outs, scr) + first_sends(ins, outs, scr):
            cp.start()

    def forward(ins, outs, scr):
        x, y, c, chips = _place()
        for j, chip in enumerate(chips):
            for k in range(n):
                copy(scr, outs, k, 1 + j, (*chip, c), (x, y, c)).wait_recv()
        for cp in passed_on(outs, scr):
            cp.start()

    def finish(ins, outs, scr):
        x, y, c, chips = _place()
        for k in range(n):
            copy(scr, outs, k, 0, (x, y, 1 - c), (x, y, c)).wait_recv()
        for j, chip in enumerate(chips):
            for k in range(n):
                copy(scr, outs, k, 4 + j, (*chip, 1 - c), (x, y, c)).wait_recv()
        for cp in first_sends(ins, outs, scr) + passed_on(outs, scr):
            cp.wait_send()
        for cp in local(ins, outs, scr):
            cp.wait()

    out_shapes = [jax.ShapeDtypeStruct((N_DEV * s.shape[0], s.shape[1]), s.dtype) for s in shards]
    scratch = [pltpu.SemaphoreType.DMA((7 * n,)), pltpu.SemaphoreType.DMA((7 * n,)), pltpu.SemaphoreType.DMA((n,))]
    return _Task(shards, out_shapes, scratch, [(0, start), (forward_at, forward), (1.0, finish)])


def _all_gather(name, shards, natural=()):
    return _comm_only(name, [_gather_task(shards, natural)])[0]


def _chip_task(sums):
    n = len(sums)
    rs = [s.shape[0] // 4 for s in sums]

    def block(ref, k, chip_index):
        return ref.at[pl.ds(pl.multiple_of(chip_index * rs[k], 8), rs[k]), :]

    def copies(ins, outs, scr):
        send_sems, recv_sems, local_sems = scr
        x, y, c, chips = _place()
        here = 2 * x + y
        local = [pltpu.make_async_copy(block(ins[k], k, here), outs[k].at[here], local_sems.at[k]) for k in range(n)]
        remote = []
        for j, (px, py) in enumerate(chips):
            remote += [pltpu.make_async_remote_copy(
                src_ref=block(ins[k], k, 2 * px + py), dst_ref=outs[k].at[here],
                send_sem=send_sems.at[3 * k + j], recv_sem=recv_sems.at[3 * k + j],
                device_id=(px, py, c), device_id_type=MESH) for k in range(n)]
        return local, remote

    def start(ins, outs, scr):
        local, remote = copies(ins, outs, scr)
        for cp in local + remote:
            cp.start()

    def finish(ins, outs, scr):
        local, remote = copies(ins, outs, scr)
        for cp in remote:
            cp.wait()
        for cp in local:
            cp.wait()

    out_shapes = [jax.ShapeDtypeStruct((4, r, s.shape[1]), s.dtype) for r, s in zip(rs, sums)]
    scratch = [pltpu.SemaphoreType.DMA((3 * n,)), pltpu.SemaphoreType.DMA((3 * n,)), pltpu.SemaphoreType.DMA((n,))]
    return _Task(sums, out_shapes, scratch, [(0, start), (1.0, finish)])


def _dw_pair(name, a, b, scale, comm=None, blocks=1):
    T, M = a.shape
    N = b.shape[1]
    half = M // 2
    wide = half // blocks
    tk = min(2048, T)
    nK = T // tk
    plumb = _CommPlumbing(comm)

    def body(core_ref, *rest):
        a_refs, b_ref, rest = rest[:blocks], rest[blocks], rest[blocks + 1:]
        c_in = rest[:plumb.n_in]
        o_ref = rest[plumb.n_in]
        c_out = rest[plumb.n_in + 1: plumb.n_in + 1 + plumb.n_out]
        acc, stage, land, send_sem, recv_sem = rest[plumb.n_in + 1 + plumb.n_out: plumb.n_in + 6 + plumb.n_out]
        c_scr = rest[plumb.n_in + 6 + plumb.n_out:]
        i, k = pl.program_id(0), pl.program_id(1)
        x, y, c, _ = _place()
        push = pltpu.make_async_remote_copy(src_ref=stage, dst_ref=land, send_sem=send_sem, recv_sem=recv_sem,
                                            device_id=(x, y, 1 - c), device_id_type=MESH)
        if comm:
            plumb.run(i * nK + k, 2 * nK, True, c_in, c_out, c_scr)

        av = a_refs[0][...] if blocks == 1 else jnp.concatenate([r[...] for r in a_refs], axis=1)
        p = lax.dot_general(av, b_ref[...], _DIMS["tn"], preferred_element_type=F32)

        @pl.when(k == 0)
        def _():
            acc[...] = p

        @pl.when(k > 0)
        def _():
            acc[...] += p

        @pl.when((i == 0) & (k == nK - 1))
        def _():
            stage[...] = (scale * acc[...]).astype(BF)
            push.start()

        @pl.when((i == 1) & (k == nK - 1))
        def _():
            push.wait_recv()
            o_ref[...] = (scale * acc[...] + land[...].astype(F32)).astype(BF)
            push.wait_send()

        if comm:
            plumb.run(i * nK + k, 2 * nK, False, c_in, c_out, c_scr)

    grid_spec = pltpu.PrefetchScalarGridSpec(
        num_scalar_prefetch=1, grid=(2, nK),
        in_specs=[pl.BlockSpec((tk, wide), functools.partial(
            lambda i, k, core, j: (k, (2 * j if blocks > 1 else 0) + jnp.where(i == 0, 1 - core[0], core[0])), j=j))
            for j in range(blocks)] + [pl.BlockSpec((tk, N), lambda i, k, core: (k, 0))] + [ANY] * plumb.n_in,
        out_specs=[pl.BlockSpec((half, N), lambda i, k, core: (0, 0))] + [ANY] * plumb.n_out,
        scratch_shapes=[pltpu.VMEM((half, N), F32), pltpu.VMEM((half, N), BF), pltpu.VMEM((half, N), BF),
                        pltpu.SemaphoreType.DMA, pltpu.SemaphoreType.DMA] + plumb.scratch)
    core = lax.axis_index("c").astype(jnp.int32).reshape(1)
    res = pl.pallas_call(
        body, name=name, grid_spec=grid_spec,
        out_shape=[jax.ShapeDtypeStruct((half, N), BF)] + plumb.out_shapes,
        compiler_params=_params(("arbitrary", "arbitrary")),
    )(core, *([a] * blocks), b, *plumb.args)
    return (res[0], plumb.split_outputs(res[1:])) if comm else res[0]


def _pair_exchange(name, parts):
    n = len(parts)

    def body(*refs):
        ins, outs = refs[:n], refs[n:2 * n]
        send_sems, recv_sems = refs[2 * n:]
        x, y, c, _ = _place()
        copies = [pltpu.make_async_remote_copy(
            src_ref=ins[k].at[:, pl.ds(1 - c, 1)], dst_ref=outs[k], send_sem=send_sems.at[k], recv_sem=recv_sems.at[k],
            device_id=(x, y, 1 - c), device_id_type=MESH) for k in range(n)]
        for cp in copies:
            cp.start()
        for cp in copies:
            cp.wait()

    return pl.pallas_call(
        body, name=name, in_specs=[ANY] * n, out_specs=[ANY] * n,
        out_shape=[jax.ShapeDtypeStruct((4, 1) + p.shape[2:], p.dtype) for p in parts],
        scratch_shapes=[pltpu.SemaphoreType.DMA((n,)), pltpu.SemaphoreType.DMA((n,))],
        compiler_params=pltpu.CompilerParams(has_side_effects=True),
    )(*parts)


def _pair_sum(name, part, got, core):
    _, _, r, C = part.shape

    def body(core_ref, p_ref, g_ref, o_ref):
        o_ref[0] = (p_ref[0, 0].astype(F32) + g_ref[0, 0].astype(F32)).astype(o_ref.dtype)

    return pl.pallas_call(
        body, name=name,
        grid_spec=pltpu.PrefetchScalarGridSpec(
            num_scalar_prefetch=1, grid=(4,),
            in_specs=[pl.BlockSpec((1, 1, r, C), lambda i, core_ref: (i, core_ref[0], 0, 0)),
                      pl.BlockSpec((1, 1, r, C), lambda i, core_ref: (i, 0, 0, 0))],
            out_specs=pl.BlockSpec((1, r, C), lambda i, core_ref: (i, 0, 0))),
        out_shape=jax.ShapeDtypeStruct((4, r, C), part.dtype), compiler_params=_params(("parallel",)),
    )(core, part, got)


def _ffn_bwd(tag, dy, dyb, x, gain, wgT, wuT, wd, saved, pending):
    n, g, u, a = saved
    half = lambda accs, ex: _swiglu_bwd_epilogue([0.5 * accs[0]], ex)
    sum_d = _dw_pair(tag + "_dw_down", a, dyb, 0.5)
    early = not pending
    (dg, du), done0 = _mm(tag + "_d_act", [(dyb, wd, "nt", 0)], [BF, BF], tm=512, tn=1408, tk=D_MODEL, epilogue=half,
                          extras=[(g, "tile", 0), (u, "tile", 0)], cols_outer=True,
                          comm=[_chip_task([sum_d])] if early else pending)
    if early:
        (slots_d,), done0 = done0[0], []
        sum_g = _dw_pair(tag + "_dw_gate", dg, n, 1.0)
    else:
        sum_g, ((slots_d,),) = _dw_pair(tag + "_dw_gate", dg, n, 1.0, comm=[_chip_task([sum_d])])
    sum_u, ((slots_g,),) = _dw_pair(tag + "_dw_up", du, n, 1.0, comm=[_chip_task([sum_g])])
    (dx, dxb, dgain), ((slots_u,),) = _mm(
        tag + "_d_norm", [(dg, wgT, "nn", 0), (du, wuT, "nn", 0)], [F32, BF], tm=512, tn=D_MODEL, tk=D_FF,
        epilogue=_rms_bwd_epilogue, extras=[(x, "tile", 0), (gain, "row", 0), (dy, "tile", 0)], n_colsum=1,
        comm=[_chip_task([sum_u])])
    return dx, dxb, dgain, done0, slots_g, slots_u, slots_d


def _tile_gain(g):
    return jnp.concatenate([g, g]).reshape(1, LANES)


def _fold_heads(partials):
    return jnp.sum(partials.reshape(-1, HEAD_DIM), axis=0)


def _pack_small_grads(grads, loss_local):
    pieces, row = [], 0
    for name, r0, _ in SMALL_LAYOUT + (("loss", LOSS_ROW, None),):
        v = (loss_local if name == "loss" else grads[name]).reshape(-1)
        rows = -(-v.size // LANES)
        block = jnp.pad(v, (0, rows * LANES - v.size)).reshape(rows, LANES)
        pieces += [jnp.zeros((r0 - row, LANES), F32)] * (r0 > row) + [block]
        row = r0 + rows
    pieces.append(jnp.zeros((SMALL_ROWS - row, LANES), F32))
    return jnp.concatenate(pieces, axis=0)


def kernel(x, ffn1_norm, ffn1_w_gate, ffn1_w_up, ffn1_w_down, mix_norm, w_in, pool_w, pool_scale, w_pool_out, q_norm, k_norm, sinks, w_attn_out, gate_bias, w_out, ffn2_norm, ffn2_w_gate, ffn2_w_up, ffn2_w_down, loss_target, m_ffn1_norm, m_ffn1_w_gate, m_ffn1_w_up, m_ffn1_w_down, m_mix_norm, m_w_in, m_pool_w, m_pool_scale, m_w_pool_out, m_q_norm, m_k_norm, m_sinks, m_w_attn_out, m_gate_bias, m_w_out, m_ffn2_norm, m_ffn2_w_gate, m_ffn2_w_up, m_ffn2_w_down, v_ffn1_norm, v_ffn1_w_gate, v_ffn1_w_up, v_ffn1_w_down, v_mix_norm, v_w_in, v_pool_w, v_pool_scale, v_w_pool_out, v_q_norm, v_k_norm, v_sinks, v_w_attn_out, v_gate_bias, v_w_out, v_ffn2_norm, v_ffn2_w_gate, v_ffn2_w_up, v_ffn2_w_down):
    T = x.shape[1]
    x2 = x.reshape(T, D_MODEL)
    target = loss_target.reshape(T, D_MODEL)

    big = [
        ("ffn1_w_gate", ffn1_w_gate, m_ffn1_w_gate, v_ffn1_w_gate, True, False),
        ("ffn1_w_up", ffn1_w_up, m_ffn1_w_up, v_ffn1_w_up, True, False),
        ("ffn1_w_down", ffn1_w_down, m_ffn1_w_down, v_ffn1_w_down, False, False),
        ("w_in", w_in, m_w_in, v_w_in, True, False),
        ("w_pool_out", w_pool_out, m_w_pool_out, v_w_pool_out, False, True),
        ("w_attn_out", w_attn_out, m_w_attn_out, v_w_attn_out, False, False),
        ("w_out", w_out, m_w_out, v_w_out, False, False),
        ("ffn2_w_gate", ffn2_w_gate, m_ffn2_w_gate, v_ffn2_w_gate, True, False),
        ("ffn2_w_up", ffn2_w_up, m_ffn2_w_up, v_ffn2_w_up, True, False),
        ("ffn2_w_down", ffn2_w_down, m_ffn2_w_down, v_ffn2_w_down, False, False),
    ]
    view = lambda a, tv: a.T if tv else a
    shards = _prep("prep_weights", [view(w, tv) for _, w, _, _, tv, _ in big], [tk_ for *_, tk_ in big])
    g1 =ffn1_norm.reshape(1, D_MODEL)
    g2 = mix_norm.reshape(1, D_MODEL)
    g3 = ffn2_norm.reshape(1, D_MODEL)
    bias_row = gate_bias.reshape(1, 2 * D_MODEL)
    qg, kg = _tile_gain(q_norm) * ATTN_SCALE, _tile_gain(k_norm)
    scale_row = pool_scale.reshape(1, POOL_WIDTH)

    n1, ((wg1T, wu1T),) = _rms_fwd("ffn1_norm", x2, g1, [_gather_task(shards[0:2], forward_at=0.9)])
    (gt1, up1, act1), ((wd1,), (w_inT,)) = _mm(
        "ffn1_gate_up", [(n1, wg1T, "nt", 0), (n1, wu1T, "nt", 1)], [BF, BF, BF], tm=512, tn=1408, tk=D_MODEL,
        epilogue=_swiglu_fwd_epilogue, cols_outer=True,
        comm=[_gather_task(shards[2:3], forward_at=0.5), _gather_task(shards[3:4], natural=(0,), forward_at=0.9)])
    h1, u = _mm("ffn1_down", [(act1, wd1, "nn", 0)], [F32, BF], tm=512, tn=D_MODEL, tk=D_FF,
                epilogue=_residual_norm_epilogue(0.5), extras=[(x2, "tile", 0), (g2, "row", 0)])
    saved1 = (n1, gt1, up1, act1)
    (proj,), ((w_poT, w_ao, w_o),) = _mm(
        "in_proj", [(u, w_inT, "nt", 0)], [BF], tm=512, tn=1280, tk=D_MODEL, cols_outer=True,
        comm=[_gather_task(shards[4:7], natural=(0, 1, 2), forward_at=0.8)])
    pooled, mixed = _pool_fwd("pool_fwd", proj, pool_w, scale_row)
    qn = _headnorm_fwd("q_norm", proj, COL_Q, ATTN_WIDTH, qg)
    kn = _headnorm_fwd("k_norm", proj, COL_K, KV_WIDTH, kg)
    attn, ((wg2T, wu2T),) = _attn_fwd("attn_fwd", qn, kn, proj, sinks,
                                      comm=[_gather_task(shards[7:9], forward_at=0.85)])
    (bp,) = _mm("pool_out", [(mixed, w_poT, "nt", 0)], [BF], tm=1024, tn=D_MODEL, tk=POOL_WIDTH)
    gate_tn = 256
    gate_extras = [(proj, "tile", COL_GP // gate_tn), (proj, "tile", COL_GA // gate_tn),
                   (bias_row, "row", 0), (bias_row, "row", D_MODEL // gate_tn)]
    merged, ba = _mm("attn_out_merge", [(attn, w_ao, "nn", 0)], [BF, BF], tm=2048, tn=gate_tn, tk=ATTN_WIDTH,
                     epilogue=_merge_fwd_epilogue, extras=[(bp, "tile", 0)] + gate_extras)
    h2, n2 = _mm("mix_out", [(merged, w_o, "nn", 0)], [F32, BF], tm=512, tn=D_MODEL, tk=D_MODEL,
                 epilogue=_residual_norm_epilogue(1.0), extras=[(h1, "tile", 0), (g3, "row", 0)])
    (gt2, up2, act2), ((wd2,),) = _mm(
        "ffn2_gate_up", [(n2, wg2T, "nt", 0), (n2, wu2T, "nt", 1)], [BF, BF, BF], tm=512, tn=1408, tk=D_MODEL,
        epilogue=_swiglu_fwd_epilogue, cols_outer=True, comm=[_gather_task(shards[9:10], forward_at=0.8)])
    dy, dyb, sq = _mm("ffn2_down_loss", [(act2, wd2, "nn", 0)], [F32, BF], tm=512, tn=D_MODEL, tk=D_FF,
                      epilogue=_loss_epilogue, extras=[(h2, "tile", 0), (target, "tile", 0)], n_colsum=1)
    loss_local = 0.5 * jnp.sum(sq) / D_MODEL

    dh2, dh2b, dg3, _, slots_g2, slots_u2, slots_d2 = _ffn_bwd(
        "ffn2", dy, dyb, h2, g3, wg2T, wu2T, wd2, (n2, gt2, up2, act2), [])
    dbp, dba, dgp, dga, cs_gp, cs_ga = _mm(
        "mix_out_bwd", [(dh2b, w_o, "nt", 0)], [BF, BF, BF, BF], tm=2048, tn=gate_tn, tk=D_MODEL,
        epilogue=_merge_bwd_epilogue, extras=[(bp, "tile", 0), (ba, "tile", 0)] + gate_extras, n_colsum=2)
    sum_o = _dw_pair("dw_out", merged, dh2b, 1.0, blocks=4)
    (dmixed,), ((slots_o,),) = _mm("pool_out_bwd", [(dbp, w_poT, "nn", 0)], [BF], tm=1024, tn=POOL_WIDTH, tk=D_MODEL,
                                   comm=[_chip_task([sum_o])])
    sum_po = _dw_pair("dw_pool_out", dbp, mixed, 1.0, blocks=4)
    (dattn,), ((slots_po,),) = _mm("attn_out_bwd", [(dba, w_ao, "nt", 0)], [BF], tm=1024, tn=ATTN_WIDTH, tk=D_MODEL,
                                   comm=[_chip_task([sum_po])])
    sum_ao = _dw_pair("dw_attn_out", attn, dba, 1.0, blocks=4)
    dxp, dpool_w, dpool_scale = _pool_bwd("pool_bwd", dmixed, pooled, pool_w, scale_row)
    dqn, dkn, dv, dsink_tile = _attn_bwd("attn_bwd", dattn, qn, kn, proj, sinks)
    dq, dqg = _headnorm_bwd("q_norm_bwd", dqn, proj, COL_Q, ATTN_WIDTH, qg)
    dk, dkg = _headnorm_bwd("k_norm_bwd", dkn, proj, COL_K, KV_WIDTH, kg)
    dproj = jnp.concatenate([dxp, dq, dk, dv, dgp, dga], axis=1)
    (dh1, dh1b, dg2), ((slots_ao,),) = _mm(
        "in_proj_bwd", [(dproj, w_inT, "nn", 0)], [F32, BF], tm=512, tn=D_MODEL, tk=IN_WIDTH, epilogue=_rms_bwd_epilogue,
        extras=[(h1, "tile", 0), (g2, "row", 0), (dh2, "tile", 0)], n_colsum=1, comm=[_chip_task([sum_ao])])
    (dw_inT,) = _mm("dw_in", [(dproj, u, "tn", 0)], [BF], tm=1280, tn=D_MODEL, tk=2048)
    part_in = dw_inT.reshape(4, 2, IN_WIDTH // N_DEV, D_MODEL)
    (got_in,) = _pair_exchange("pair_exchange_w_in", [part_in])
    core = lax.axis_index("c").astype(jnp.int32).reshape(1)
    sum_in = _pair_sum("pair_sum_w_in", part_in, got_in, core).reshape(IN_WIDTH // 2, D_MODEL)
    dx, _, dg1, ((slots_in,),), slots_g1, slots_u1, slots_d1 = _ffn_bwd(
        "ffn1", dh1, dh1b, x2, g1, wg1T, wu1T, wd1, saved1, [_chip_task([sum_in])])

    slots = [slots_g1, slots_u1, slots_d1, slots_in, slots_po, slots_ao, slots_o, slots_g2, slots_u2, slots_d2]
    big_out = {}
    for k, (nm, w, m, v, tv, tk_) in enumerate(big):
        res = _adamw_sharded("adamw_" + nm, slots[k], view(w, tv), view(m, tv), view(v, tv), tk_)
        big_out[nm] = tuple(view(r, tv) for r in res)

    small_grads = {
        "ffn1_norm": jnp.sum(dg1, axis=(0, 1)), "mix_norm": jnp.sum(dg2, axis=(0, 1)), "ffn2_norm": jnp.sum(dg3, axis=(0, 1)),
        "gate_bias": jnp.concatenate([jnp.sum(cs_gp, axis=(0, 1)), jnp.sum(cs_ga, axis=(0, 1))]),
        "pool_scale": dpool_scale, "q_norm": _fold_heads(dqg) * ATTN_SCALE, "k_norm": _fold_heads(dkg),
        "sinks": dsink_tile[0, :N_HEADS]}
    g_vec, g_pool_w = _all_gather("gather_small_grads", [_pack_small_grads(small_grads, loss_local),
                                                         dpool_w.reshape(-1, LANES)])
    given = {"ffn1_norm": (ffn1_norm, m_ffn1_norm, v_ffn1_norm), "mix_norm": (mix_norm, m_mix_norm, v_mix_norm),
             "ffn2_norm": (ffn2_norm, m_ffn2_norm, v_ffn2_norm), "gate_bias": (gate_bias, m_gate_bias, v_gate_bias),
             "pool_scale": (pool_scale, m_pool_scale, v_pool_scale), "q_norm": (q_norm, m_q_norm, v_q_norm),
             "k_norm": (k_norm, m_k_norm, v_k_norm), "sinks": (sinks, m_sinks, v_sinks)}
    params = [tuple(a.reshape(shape) for a in given[nm]) for nm, _, shape in SMALL_LAYOUT]
    params.append(tuple(a.reshape(-1, LANES) for a in (pool_w, m_pool_w, v_pool_w)))
    small_res, loss_row = _adamw_small("adamw_small", g_vec.reshape(N_DEV, SMALL_ROWS, LANES),
                                       g_pool_w.reshape(N_DEV, -1, LANES), params)
    small_out = {nm: tuple(r.reshape(given[nm][0].shape) for r in res)
                 for (nm, _, _), res in zip(SMALL_LAYOUT, small_res)}
    small_out["pool_w"] = tuple(r.reshape(pool_w.shape) for r in small_res[-1])
    loss = loss_row[0, 0]

    order = ["ffn1_norm", "ffn1_w_gate", "ffn1_w_up", "ffn1_w_down", "mix_norm", "w_in", "pool_w", "pool_scale",
             "w_pool_out", "q_norm", "k_norm", "sinks", "w_attn_out", "gate_bias", "w_out", "ffn2_norm",
             "ffn2_w_gate", "ffn2_w_up", "ffn2_w_down"]
    every = {**big_out, **small_out}
    outs = [loss, dx.reshape(x.shape)]
    for j in range(4):
        outs += [every[nm][j] for nm in order]
    return tuple(outs)
```

```python
import functools

import jax
import jax.numpy as jnp
from jax import lax
from jax.experimental import pallas as pl
from jax.experimental.pallas import tpu as pltpu

BF = jnp.bfloat16
F32 = jnp.float32

D_MODEL = 1024
D_FF = 2816
POOL_WIDTH = 512
POOL_GROUP = 128
N_POOL_GROUPS = 4
HEAD_DIM = 64
N_HEADS = 16
GQA_GROUP = 8
BLOCK = 128
ATTN_WIDTH = 1024
KV_WIDTH = 128
IN_WIDTH = 3840
RMS_EPS = 1e-6
N_DEV = 8
LANES = 128

COL_Q = POOL_WIDTH
COL_K = COL_Q + ATTN_WIDTH
COL_V = COL_K + KV_WIDTH
COL_GP = COL_V + KV_WIDTH
COL_GA = COL_GP + D_MODEL

ADAM_LR = 0.001
ADAM_B1 = 0.9
ADAM_B2 = 0.999
ADAM_EPS = 1e-08
ADAM_WD = 0.01
ADAM_STEP = 10

VMEM_LIMIT_V7X = 56 * 1024 * 1024
MESH = pl.DeviceIdType.MESH
ANY = pl.BlockSpec(memory_space=pl.ANY)


def _params(sem=None):
    return pltpu.CompilerParams(dimension_semantics=sem, vmem_limit_bytes=VMEM_LIMIT_V7X)


_DIMS = {"nt": (((1,), (1,)), ((), ())), "nn": (((1,), (0,)), ((), ())), "tn": (((0,), (0,)), ((), ()))}


class _Task:
    def __init__(self, inputs, out_shapes, scratch, phases):
        self.inputs, self.out_shapes, self.scratch = list(inputs), list(out_shapes), list(scratch)
        self.phases = list(phases)


class _CommPlumbing:
    def __init__(self, tasks):
        self.tasks = list(tasks or [])
        self.args = [a for t in self.tasks for a in t.inputs]
        self.out_shapes = [o for t in self.tasks for o in t.out_shapes]
        self.scratch = [s for t in self.tasks for s in t.scratch]
        self.n_in, self.n_out = len(self.args), len(self.out_shapes)

    def _slices(self, c_in, c_out, c_scr):
        i = o = s = 0
        for t in self.tasks:
            yield t, c_in[i:i + len(t.inputs)], c_out[o:o + len(t.out_shapes)], c_scr[s:s + len(t.scratch)]
            i, o, s = i + len(t.inputs), o + len(t.out_shapes), s + len(t.scratch)

    def run(self, step, steps, before, c_in, c_out, c_scr):
        for t, ins, outs, scr in self._slices(c_in, c_out, c_scr):
            for frac, fn in t.phases:
                if step is None:
                    fn(ins, outs, scr)
                elif before == (frac == 0):
                    at = 0 if frac == 0 else max(0, min(steps, -(-int(round(frac * steps * 64)) // 64)) - 1)
                    pl.when(step == at)(functools.partial(fn, ins, outs, scr))

    def split_outputs(self, flat):
        res, o = [], 0
        for t in self.tasks:
            res.append(list(flat[o:o + len(t.out_shapes)]))
            o += len(t.out_shapes)
        return res


def _comm_only(name, tasks):
    plumb = _CommPlumbing(tasks)

    def body(*refs):
        c_in, c_out = refs[:plumb.n_in], refs[plumb.n_in: plumb.n_in + plumb.n_out]
        c_scr = refs[plumb.n_in + plumb.n_out:]
        plumb.run(None, 1, True, c_in, c_out, c_scr)

    res = pl.pallas_call(
        body, name=name, in_specs=[ANY] * plumb.n_in, out_specs=[ANY] * plumb.n_out, out_shape=plumb.out_shapes,
        scratch_shapes=plumb.scratch, compiler_params=pltpu.CompilerParams(has_side_effects=True),
    )(*plumb.args)
    return plumb.split_outputs(res)


def _mm(name, terms, out_dtypes, *, tm, tn, tk, epilogue=None, extras=(), n_colsum=0, comm=None, cols_outer=False):
    a0, b0, mode0, _ = terms[0]
    if mode0 == "nt":
        (M, K), N = a0.shape, b0.shape[0]
    elif mode0 == "nn":
        (M, K), N = a0.shape, b0.shape[1]
    else:
        (K, M), N = a0.shape, b0.shape[1]
    tm, tn, tk = min(tm, M), min(tn, N), min(tk, K)
    assert M % tm == 0 and N % tn == 0 and K % tk == 0, (name, M, N, K, tm, tn, tk)
    nI, nJ, nK = M // tm, N // tn, K // tk
    n_terms = len(terms)
    n_acc = max(t[3] for t in terms) + 1
    n_ex = len(extras)
    n_out = len(out_dtypes)
    if epilogue is None:
        epilogue = lambda accs, ex: ([accs[0]], [])
    plumb = _CommPlumbing(comm)
    n_scr = n_acc if nK > 1 else 0
    grid = (nJ, nI, nK) if cols_outer else (nI, nJ, nK)

    def body(*refs):
        n_in = 2 * n_terms + n_ex
        ab = refs[: 2 * n_terms]
        ex_refs = refs[2 * n_terms: n_in]
        c_in = refs[n_in: n_in + plumb.n_in]
        o0 = n_in + plumb.n_in
        out_refs = refs[o0: o0 + n_out]
        cs_refs = refs[o0 + n_out: o0 + n_out + n_colsum]
        c_out = refs[o0 + n_out + n_colsum: o0 + n_out + n_colsum + plumb.n_out]
        s0 = o0 + n_out + n_colsum + plumb.n_out
        acc_refs = refs[s0: s0 + n_scr]
        c_scr = refs[s0 + n_scr:]
        steps = grid[0] * grid[1] * nK
        if comm:
            step = (pl.program_id(0) * grid[1] + pl.program_id(1)) * nK + pl.program_id(2)
            plumb.run(step, steps, True, c_in, c_out, c_scr)

        def products():
            accs = [None] * n_acc
            for t, (_, _, mode, ai) in enumerate(terms):
                p = lax.dot_general(ab[2 * t][...], ab[2 * t + 1][...], _DIMS[mode], preferred_element_type=F32)
                accs[ai] = p if accs[ai] is None else accs[ai] + p
            return accs

        def finish(accs):
            outs, colsums = epilogue(accs, [r[...] for r in ex_refs])
            for r, o in zip(out_refs, outs):
                r[...] = o.astype(r.dtype)
            for r, cs in zip(cs_refs, colsums):
                r[...] = jnp.sum(cs, axis=0, keepdims=True).reshape(r.shape)

        if nK == 1:
            finish(products())
        else:
            k = pl.program_id(2)
            accs = products()

            @pl.when(k == 0)
            def _():
                for r, a in zip(acc_refs, accs):
                    r[...] = a

            @pl.when(k > 0)
            def _():
                for r, a in zip(acc_refs, accs):
                    r[...] += a

            @pl.when(k == nK - 1)
            def _():
                finish([r[...] for r in acc_refs])

        if comm:
            plumb.run(step, steps, False, c_in, c_out, c_scr)

    def spec(block, index, fixed=False):
        imap = (lambda q, p, k: index(p, q, k)) if cols_outer else index
        return pl.BlockSpec(block, imap, pipeline_mode=pl.Buffered(1)) if fixed else pl.BlockSpec(block, imap)

    in_specs, args = [], []
    for a, b, mode, _ in terms:
        if mode == "nt":
            in_specs += [spec((tm, tk), lambda i, j, k: (i, k), nI * nK == 1),
                         spec((tn, tk), lambda i, j, k: (j, k), nJ * nK == 1)]
        elif mode == "nn":
            in_specs += [spec((tm, tk), lambda i, j, k: (i, k), nI * nK == 1),
                         spec((tk, tn), lambda i, j, k: (k, j), nJ * nK == 1)]
        else:
            in_specs += [spec((tk, tm), lambda i, j, k: (k, i), nI * nK == 1),
                         spec((tk, tn), lambda i, j, k: (k, j), nJ * nK == 1)]
        args += [a, b]
    for arr, kind, off in extras:
        if kind == "tile":
            in_specs.append(spec((tm, tn), functools.partial(lambda i, j, k, off: (i, j + off), off=off)))
        else:
            in_specs.append(spec((1, tn), functools.partial(lambda i, j, k, off: (0, j + off), off=off)))
        args.append(arr)
    out_shape = [jax.ShapeDtypeStruct((M, N), dt) for dt in out_dtypes]
    out_specs = [spec((tm, tn), lambda i, j, k: (i, j)) for _ in out_dtypes]
    out_shape += [jax.ShapeDtypeStruct((nI, 1, N), F32) for _ in range(n_colsum)]
    out_specs += [spec((1, 1, tn), lambda i, j, k: (i, 0, j)) for _ in range(n_colsum)]
    scratch = [pltpu.VMEM((tm, tn), F32) for _ in range(n_scr)]
    args += plumb.args
    in_specs += [ANY] * plumb.n_in
    out_shape += plumb.out_shapes
    out_specs += [ANY] * plumb.n_out
    sem = ("arbitrary",) * 3 if comm else ("parallel", "parallel", "arbitrary")
    res = pl.pallas_call(
        body, name=name, grid=grid, in_specs=in_specs, out_specs=out_specs, out_shape=out_shape,
        scratch_shapes=scratch + plumb.scratch, compiler_params=_params(sem),
    )(*args)
    n_own = n_out + n_colsum
    return (list(res[:n_own]), plumb.split_outputs(res[n_own:])) if comm is not None else res


ROW_TILE = 512


def _rms_fwd(name, x, g, comm):
    T, D = x.shape
    steps = T // ROW_TILE
    plumb = _CommPlumbing(comm)

    def body(x_ref, g_ref, *rest):
        c_in, o_ref = rest[:plumb.n_in], rest[plumb.n_in]
        c_out, c_scr = rest[plumb.n_in + 1: plumb.n_in + 1 + plumb.n_out], rest[plumb.n_in + 1 + plumb.n_out:]
        plumb.run(pl.program_id(0), steps, True, c_in, c_out, c_scr)
        xv = x_ref[...]
        r = lax.rsqrt(jnp.mean(xv * xv, axis=-1, keepdims=True) + RMS_EPS)
        o_ref[...] = (xv * r * g_ref[...]).astype(BF)
        plumb.run(pl.program_id(0), steps, False, c_in, c_out, c_scr)

    row = pl.BlockSpec((ROW_TILE, D), lambda i: (i, 0))
    res = pl.pallas_call(
        body, name=name, grid=(steps,),
        in_specs=[row, pl.BlockSpec((1, D), lambda i: (0, 0))] + [ANY] * plumb.n_in,
        out_specs=[row] + [ANY] * plumb.n_out, out_shape=[jax.ShapeDtypeStruct((T, D), BF)] + plumb.out_shapes,
        scratch_shapes=plumb.scratch, compiler_params=_params(("arbitrary",)),
    )(x, g, *plumb.args)
    return res[0], plumb.split_outputs(res[1:])


HEADNORM_TILE = 1024


def _half_sum_matrix():
    r = lax.broadcasted_iota(jnp.int32, (LANES, LANES), 0) // HEAD_DIM
    c = lax.broadcasted_iota(jnp.int32, (LANES, LANES), 1) // HEAD_DIM
    return (r == c).astype(BF)


def _head_mean(v, ones_blockdiag):
    hi = v.astype(BF)
    lo = (v - hi.astype(F32)).astype(BF)
    s = jnp.dot(hi, ones_blockdiag, preferred_element_type=F32) + jnp.dot(lo, ones_blockdiag, preferred_element_type=F32)
    return s * (1.0 / HEAD_DIM)


def _headnorm_fwd(name, proj, col0, width, g2):
    T = proj.shape[0]
    wide = min(width, GROUP_WIDTH)
    nb, off = width // wide, col0 // wide

    def body(x_ref, g_ref, b_ref, o_ref):
        for s in range(wide // LANES):
            lanes = slice(LANES * s, LANES * (s + 1))
            xv = x_ref[:, lanes].astype(F32)
            r = lax.rsqrt(_head_mean(xv * xv, b_ref[...]) + RMS_EPS)
            o_ref[:, lanes] = (xv * r * g_ref[...]).astype(BF)

    return pl.pallas_call(
        body, name=name, grid=(T // HEADNORM_TILE, nb),
        in_specs=[pl.BlockSpec((HEADNORM_TILE, wide), lambda i, j: (i, j + off)),
                  pl.BlockSpec((1, LANES), lambda i, j: (0, 0)), pl.BlockSpec((LANES, LANES), lambda i, j: (0, 0))],
        out_specs=pl.BlockSpec((HEADNORM_TILE, wide), lambda i, j: (i, j)),
        out_shape=jax.ShapeDtypeStruct((T, width), BF), compiler_params=_params(("parallel", "parallel")),
    )(proj, g2, _half_sum_matrix())


def _headnorm_bwd(name, dy, proj, col0, width, g2):
    T = proj.shape[0]
    wide = min(width, GROUP_WIDTH)
    nb, off = width // wide, col0 // wide

    def body(dy_ref, x_ref, g_ref, b_ref, dx_ref, dg_ref):
        for s in range(wide // LANES):
            lanes = slice(LANES * s, LANES * (s + 1))
            xv = x_ref[:, lanes].astype(F32)
            dyv = dy_ref[:, lanes].astype(F32)
            r = lax.rsqrt(_head_mean(xv * xv, b_ref[...]) + RMS_EPS)
            xhat = xv * r
            dxhat = dyv * g_ref[...]
            dx_ref[:, lanes] = (r * (dxhat - xhat * _head_mean(dxhat * xhat, b_ref[...]))).astype(BF)
            dg_ref[0, :, lanes] = jnp.sum(dyv * xhat, axis=0, keepdims=True)

    return pl.pallas_call(
        body, name=name, grid=(T // HEADNORM_TILE, nb),
        in_specs=[pl.BlockSpec((HEADNORM_TILE, wide), lambda i, j: (i, j)),
                  pl.BlockSpec((HEADNORM_TILE, wide), lambda i, j: (i, j + off)),
                  pl.BlockSpec((1, LANES), lambda i, j: (0, 0)), pl.BlockSpec((LANES, LANES), lambda i, j: (0, 0))],
        out_specs=[pl.BlockSpec((HEADNORM_TILE, wide), lambda i, j: (i, j)),
                   pl.BlockSpec((1, 1, wide), lambda i, j: (i, 0, j))],
        out_shape=[jax.ShapeDtypeStruct((T, width), BF), jax.ShapeDtypeStruct((T // HEADNORM_TILE, 1, width), F32)],
        compiler_params=_params(("parallel", "parallel")),
    )(dy, proj, g2, _half_sum_matrix())


def _shift_down(v, k, row):
    return jnp.where(row >= k, pltpu.roll(v, k, axis=0), 0.0)


def _shift_up(v, k, row, T):
    return jnp.where(row < T - k, pltpu.roll(v, T - k, axis=0), 0.0)


def _by_group(g, vals):
    out = vals[-1]
    for i in range(len(vals) - 2, -1, -1):
        out = jnp.where(g == i, vals[i], out)
    return out


def _pool_fwd(name, proj, pool_w, pool_scale):
    T = proj.shape[0]

    def body(x_ref, w_ref, s_ref, pooled_ref, mixed_ref):
        g = pl.program_id(0)
        xv = x_ref[...].astype(F32)
        row = lax.broadcasted_iota(jnp.int32, (T, 1), 0)
        s2 = xv + _shift_down(xv, 1, row)
        s4 = s2 + _shift_down(s2, 2, row)
        s8 = s4 + _shift_down(s4, 4, row)
        s16 = s8 + _shift_down(s8, 8, row)
        wsum = _by_group(g, [s2, s4, s8, s16])
        count = jnp.minimum(row + 1, 2 << g).astype(F32)
        pooled = (wsum / count - xv).astype(BF)
        pooled_ref[...] = pooled
        mixed = jnp.dot(pooled, w_ref[0].astype(BF), preferred_element_type=F32) * s_ref[...]
        mixed_ref[...] = mixed.astype(BF)

    col = pl.BlockSpec((T, POOL_GROUP), lambda g: (0, g))
    return pl.pallas_call(
        body, name=name, grid=(N_POOL_GROUPS,),
        in_specs=[col, pl.BlockSpec((1, POOL_GROUP, POOL_GROUP), lambda g: (g, 0, 0)),
                  pl.BlockSpec((1, POOL_GROUP), lambda g: (0, g))],
        out_specs=[col, col],
        out_shape=[jax.ShapeDtypeStruct((T, POOL_WIDTH), BF), jax.ShapeDtypeStruct((T, POOL_WIDTH), BF)],
        compiler_params=_params(("parallel",)),
    )(proj, pool_w, pool_scale)


def _pool_bwd(name, dmixed, pooled, pool_w, pool_scale):
    T = dmixed.shape[0]

    def body(dm_ref, p_ref, w_ref, s_ref, dx_ref, dw_ref, ds_ref):
        g = pl.program_id(0)
        dm = dm_ref[...].astype(F32)
        pooled = p_ref[...]
        w = w_ref[0].astype(BF)
        pre = jnp.dot(pooled, w, preferred_element_type=F32)
        ds_ref[...] = jnp.sum(dm * pre, axis=0, keepdims=True)
        dms = (dm * s_ref[...]).astype(BF)
        dw_ref[0] = lax.dot_general(pooled, dms, _DIMS["tn"], preferred_element_type=F32)
        dpooled = lax.dot_general(dms, w, _DIMS["nt"], preferred_element_type=F32)
        row = lax.broadcasted_iota(jnp.int32, (T, 1), 0)
        count = jnp.minimum(row + 1, 2 << g).astype(F32)
        z = dpooled / count
        l2 = z + _shift_up(z, 1, row, T)
        l4 = l2 + _shift_up(l2, 2, row, T)
        l8 = l4 + _shift_up(l4, 4, row, T)
        l16 = l8 + _shift_up(l8, 8, row, T)
        dx_ref[...] = (_by_group(g, [l2, l4, l8, l16]) - dpooled).astype(BF)

    col = pl.BlockSpec((T, POOL_GROUP), lambda g: (0, g))
    wspec = pl.BlockSpec((1, POOL_GROUP, POOL_GROUP), lambda g: (g, 0, 0))
    sspec = pl.BlockSpec((1, POOL_GROUP), lambda g: (0, g))
    return pl.pallas_call(
        body, name=name, grid=(N_POOL_GROUPS,), in_specs=[col, col, wspec, sspec], out_specs=[col, wspec, sspec],
        out_shape=[jax.ShapeDtypeStruct((T, POOL_WIDTH), BF),
                   jax.ShapeDtypeStruct((N_POOL_GROUPS, POOL_GROUP, POOL_GROUP), F32),
                   jax.ShapeDtypeStruct((1, POOL_WIDTH), F32)],
        compiler_params=_params(("parallel",)),
    )(dmixed, pooled, pool_w, pool_scale)


ATTN_SCALE = HEAD_DIM ** -0.5
MASKED = float(jnp.finfo(jnp.float32).min)
KV_COL_BLOCK_K = COL_K // LANES
KV_COL_BLOCK_V = COL_V // LANES
GROUP_WIDTH = GQA_GROUP * HEAD_DIM


def _dup_head(v, j):
    half = lax.broadcasted_iota(jnp.int32, (1, LANES), 1) // HEAD_DIM
    return jnp.where(half == j, v, pltpu.roll(v, HEAD_DIM, axis=1))


def _stack_heads(v, low):
    pieces = []
    for p in range(GROUP_WIDTH // LANES):
        vp = v[:, LANES * p: LANES * (p + 1)]
        pieces.append(jnp.where(low, vp, jnp.zeros_like(vp)))
        pieces.append(jnp.where(low, jnp.zeros_like(vp), vp))
    return jnp.concatenate(pieces, axis=0)


def _unstack_transposed(t, low):
    pairs = []
    for p in range(GROUP_WIDTH // LANES):
        even = t[:, BLOCK * (2 * p): BLOCK * (2 * p + 1)].T
        odd = t[:, BLOCK * (2 * p + 1): BLOCK * (2 * p + 2)].T
        pairs.append(jnp.where(low, even, odd))
    return pairs


def _softmax_keys_on_sublanes(k2, q, n, sink_ref, j):
    stacked = GQA_GROUP * BLOCK
    key = lax.broadcasted_iota(jnp.int32, (2 * BLOCK, stacked), 0)
    qry = lax.broadcasted_iota(jnp.int32, (2 * BLOCK, stacked), 1) % BLOCK
    valid = (key > qry) & (key <= qry + BLOCK) & ((n > 0) | (key >= BLOCK))
    head_of_lane = lax.broadcasted_iota(jnp.int32, (1, stacked), 1) // BLOCK
    sink = jnp.zeros((1, stacked), F32)
    for h in range(GQA_GROUP):
        sink = jnp.where(head_of_lane == h, sink_ref[j * GQA_GROUP + h], sink)
    s = jnp.where(valid, lax.dot_general(k2, q, _DIMS["nt"], preferred_element_type=F32), MASKED)
    m = jnp.maximum(jnp.max(s, axis=0, keepdims=True), sink)
    e = jnp.exp(s - m)
    e_sink = jnp.exp(sink - m)
    inv = 1.0 / (jnp.sum(e, axis=0, keepdims=True) + e_sink)
    return e * inv, e_sink * inv


def _attn_fwd(name, qn, kn, proj, sinks, comm=None):
    T = qn.shape[0]
    nb = T // BLOCK
    plumb = _CommPlumbing(comm)

    def body(sink_ref, q_ref, kp_ref, kc_ref, vp_ref, vc_ref, *rest):
        c_in, o_ref = rest[:plumb.n_in], rest[plumb.n_in]
        c_out, c_scr = rest[plumb.n_in + 1: plumb.n_in + 1 + plumb.n_out], rest[plumb.n_in + 1 + plumb.n_out:]
        n = pl.program_id(0)
        plumb.run(n, nb, True, c_in, c_out, c_scr)
        low = lax.broadcasted_iota(jnp.int32, (1, LANES), 1) < HEAD_DIM
        kk = jnp.concatenate([kp_ref[...], kc_ref[...]], axis=0)
        vv = jnp.concatenate([vp_ref[...], vc_ref[...]], axis=0)
        for j in range(2):
            q = _stack_heads(q_ref[:, GROUP_WIDTH * j: GROUP_WIDTH * (j + 1)], low)
            p, _ = _softmax_keys_on_sublanes(_dup_head(kk, j), q, n, sink_ref, j)
            o_t = lax.dot_general(_dup_head(vv, j), p.astype(BF), _DIMS["tn"], preferred_element_type=F32)
            for pair, o in enumerate(_unstack_transposed(o_t, low)):
                lanes = slice(GROUP_WIDTH * j + LANES * pair, GROUP_WIDTH * j + LANES * (pair + 1))
                o_ref[:, lanes] = o.astype(BF)
        plumb.run(n, nb, False, c_in, c_out, c_scr)

    wide = pl.BlockSpec((BLOCK, ATTN_WIDTH), lambda n: (n, 0))
    res = pl.pallas_call(
        body, name=name, grid=(nb,),
        in_specs=[pl.BlockSpec(memory_space=pltpu.SMEM), wide,
                  pl.BlockSpec((BLOCK, LANES), lambda n: (jnp.maximum(n - 1, 0), 0)),
                  pl.BlockSpec((BLOCK, LANES), lambda n: (n, 0)),
                  pl.BlockSpec((BLOCK, LANES), lambda n: (jnp.maximum(n - 1, 0), KV_COL_BLOCK_V)),
                  pl.BlockSpec((BLOCK, LANES), lambda n: (n, KV_COL_BLOCK_V))] + [ANY] * plumb.n_in,
        out_specs=[wide] + [ANY] * plumb.n_out,
        out_shape=[jax.ShapeDtypeStruct((T, ATTN_WIDTH), BF)] + plumb.out_shapes, scratch_shapes=plumb.scratch,
        compiler_params=_params(("arbitrary",) if comm else ("parallel",)),
    )(sinks, qn, kn, kn, proj, proj, *plumb.args)
    return (res[0], plumb.split_outputs(res[1:])) if comm is not None else res[0]


def _attn_bwd(name, dout, qn, kn, proj, sinks):
    T = qn.shape[0]
    nb = T // BLOCK

    def body(sink_ref, do_ref, q_ref, kp_ref, kc_ref, vp_ref, vc_ref, dq_ref, dk_ref, dv_ref, dsink_ref,
             carry_k, carry_v, tot_k, tot_v):
        n = pl.program_id(0)
        lane = lax.broadcasted_iota(jnp.int32, (1, LANES), 1)
        low = lane < HEAD_DIM

        @pl.when(n == 0)
        def _():
            carry_k[...] = jnp.zeros_like(carry_k)
            carry_v[...] = jnp.zeros_like(carry_v)
            dsink_ref[...] = jnp.zeros_like(dsink_ref)

        @pl.when(n == nb)
        def _():
            tot_k[...] = jnp.zeros_like(tot_k)
            tot_v[...] = jnp.zeros_like(tot_v)

        @pl.when(n < nb)
        def _():
            kk = jnp.concatenate([kp_ref[...], kc_ref[...]], axis=0)
            vv = jnp.concatenate([vp_ref[...], vc_ref[...]], axis=0)
            dk_tot = jnp.zeros((2 * BLOCK, LANES), F32)
            dv_tot = jnp.zeros((2 * BLOCK, LANES), F32)
            dsink = jnp.zeros((1, LANES), F32)
            for j in range(2):
                k2 = _dup_head(kk, j)
                v2 = _dup_head(vv, j)
                q = _stack_heads(q_ref[:, GROUP_WIDTH * j: GROUP_WIDTH * (j + 1)], low)
                do = _stack_heads(do_ref[:, GROUP_WIDTH * j: GROUP_WIDTH * (j + 1)], low)
                p, psink = _softmax_keys_on_sublanes(k2, q, n, sink_ref, j)
                dp =lax.dot_general(v2, do, _DIMS["nt"], preferred_element_type=F32)
                delta = jnp.sum(p * dp, axis=0, keepdims=True)
                ds = (p * (dp - delta)).astype(BF)
                dk2 = jnp.dot(ds, q, preferred_element_type=F32)
                dv2 = jnp.dot(p.astype(BF), do, preferred_element_type=F32)
                dq_t = lax.dot_general(k2, ds, _DIMS["tn"], preferred_element_type=F32)
                for pair, dq in enumerate(_unstack_transposed(dq_t, low)):
                    lanes = slice(GROUP_WIDTH * j + LANES * pair, GROUP_WIDTH * j + LANES * (pair + 1))
                    dq_ref[:, lanes] = dq.astype(BF)
                mine = low if j == 0 else jnp.logical_not(low)
                dk_tot = dk_tot + jnp.where(mine, dk2 + pltpu.roll(dk2, HEAD_DIM, axis=1), 0.0)
                dv_tot = dv_tot + jnp.where(mine, dv2 + pltpu.roll(dv2, HEAD_DIM, axis=1), 0.0)
                sink_term = psink * delta
                for h in range(GQA_GROUP):
                    val = -jnp.sum(sink_term[:, BLOCK * h: BLOCK * (h + 1)], axis=1, keepdims=True)
                    dsink = dsink + jnp.where(lane == j * GQA_GROUP + h, val, 0.0)
            tot_k[...] = dk_tot
            tot_v[...] = dv_tot
            dsink_ref[0:1, :] += dsink

        dk_ref[...] = (carry_k[...] + tot_k[0:BLOCK]).astype(BF)
        dv_ref[...] = (carry_v[...] + tot_v[0:BLOCK]).astype(BF)
        carry_k[...] = tot_k[BLOCK:]
        carry_v[...] = tot_v[BLOCK:]

    cur = lambda n: (jnp.minimum(n, nb - 1), 0)
    prev = lambda n: (jnp.maximum(n - 1, 0), 0)
    wide = pl.BlockSpec((BLOCK, ATTN_WIDTH), cur)
    return pl.pallas_call(
        body, name=name, grid=(nb + 1,),
        in_specs=[pl.BlockSpec(memory_space=pltpu.SMEM), wide, wide,
                  pl.BlockSpec((BLOCK, LANES), prev), pl.BlockSpec((BLOCK, LANES), cur),
                  pl.BlockSpec((BLOCK, LANES), lambda n: (jnp.maximum(n - 1, 0), KV_COL_BLOCK_V)),
                  pl.BlockSpec((BLOCK, LANES), lambda n: (jnp.minimum(n, nb - 1), KV_COL_BLOCK_V))],
        out_specs=[wide, pl.BlockSpec((BLOCK, LANES), prev), pl.BlockSpec((BLOCK, LANES), prev),
                   pl.BlockSpec((8, LANES), lambda n: (0, 0))],
        out_shape=[jax.ShapeDtypeStruct((T, ATTN_WIDTH), BF), jax.ShapeDtypeStruct((T, KV_WIDTH), BF),
                   jax.ShapeDtypeStruct((T, KV_WIDTH), BF), jax.ShapeDtypeStruct((8, LANES), F32)],
        scratch_shapes=[pltpu.VMEM((BLOCK, LANES), F32), pltpu.VMEM((BLOCK, LANES), F32),
                        pltpu.VMEM((2 * BLOCK, LANES), F32), pltpu.VMEM((2 * BLOCK, LANES), F32)],
        compiler_params=_params(("arbitrary",)),
    )(sinks, dout, qn, kn, kn, proj, proj)


def _swiglu_fwd_epilogue(accs, ex):
    g, u = accs
    return [g, u, g * jax.nn.sigmoid(g) * u], []


def _swiglu_bwd_epilogue(accs, ex):
    (da,) = accs
    g, u = ex[0].astype(F32), ex[1].astype(F32)
    s = jax.nn.sigmoid(g)
    return [da * u * (s * (1.0 + g * (1.0 - s))), da * (g * s)], []


def _residual_norm_epilogue(scale):
    def epilogue(accs, ex):
        res, gain = ex
        h = res + scale * accs[0]
        r = lax.rsqrt(jnp.mean(h * h, axis=-1, keepdims=True) + RMS_EPS)
        return [h, h * r * gain], []
    return epilogue


def _rms_bwd_epilogue(accs, ex):
    (dn,) = accs
    xv, g, dres = ex
    r = lax.rsqrt(jnp.mean(xv * xv, axis=-1, keepdims=True) + RMS_EPS)
    xhat = xv * r
    dxhat = dn * g
    dx = dres + r * (dxhat - xhat * jnp.mean(dxhat * xhat, axis=-1, keepdims=True))
    return [dx, dx], [dn * xhat]


def _loss_epilogue(accs, ex):
    xv, target = ex
    d = xv + 0.5 * accs[0] - target
    dy = d * (1.0 / D_MODEL)
    return [dy, dy], [d * d]


def _merge_fwd_epilogue(accs, ex):
    (ba,) = accs
    bp, gp_pre, ga_pre, bias_p, bias_a = ex
    gp = jax.nn.sigmoid(gp_pre.astype(F32) + bias_p)
    ga = jax.nn.sigmoid(ga_pre.astype(F32) + bias_a)
    return [gp * bp.astype(F32) + ga * ba, ba], []


def _merge_bwd_epilogue(accs, ex):
    (dm,) = accs
    bp, ba, gp_pre, ga_pre, bias_p, bias_a = ex
    gp = jax.nn.sigmoid(gp_pre.astype(F32) + bias_p)
    ga = jax.nn.sigmoid(ga_pre.astype(F32) + bias_a)
    dgp = dm * bp.astype(F32) * gp * (1.0 - gp)
    dga = dm * ba.astype(F32) * ga * (1.0 - ga)
    return [dm * gp, dm * ga, dgp, dga], [dgp, dga]


def _prep(name, ws, transposes):
    n = len(ws)

    def body(*refs):
        for w_ref, o_ref, tr in zip(refs[:n], refs[n:], transposes):
            v = w_ref[...]
            o_ref[...] = (v.T if tr else v).astype(BF)

    shapes = [jax.ShapeDtypeStruct(w.shape[::-1] if tr else w.shape, BF) for w, tr in zip(ws, transposes)]
    return pl.pallas_call(body, name=name, out_shape=shapes, compiler_params=_params())(*ws)


def _adam_math(w, g, m, v):
    m = ADAM_B1 * m + (1.0 - ADAM_B1) * g
    v = ADAM_B2 * v + (1.0 - ADAM_B2) * jnp.square(g)
    m_hat = m / (1.0 - ADAM_B1 ** ADAM_STEP)
    v_hat = v / (1.0 - ADAM_B2 ** ADAM_STEP)
    delta = -ADAM_LR * (m_hat / (jnp.sqrt(v_hat) + ADAM_EPS) + ADAM_WD * w)
    return delta, m, v


def _adamw_sharded(name, items, transpose=False):
    n = len(items)

    def body(*refs):
        ins, outs = refs[:4 * n], refs[4 * n:]
        for k in range(n):
            s_ref, w_ref, m_ref, v_ref = ins[4 * k: 4 * k + 4]
            g = s_ref[0].astype(F32)
            for i in range(1, 4):
                g = g + s_ref[i].astype(F32)
            if transpose:
                g = g.T
            delta, mn, vn = _adam_math(w_ref[...], g, m_ref[...], v_ref[...])
            for o_ref, val in zip(outs[4 * k: 4 * k + 4], (g, delta, mn, vn)):
                o_ref[...] = val

    flat = [a for item in items for a in item]
    out_shape = [jax.ShapeDtypeStruct(item[1].shape, F32) for item in items for _ in range(4)]
    _, r, C = items[0][0].shape
    rows = r // 4
    if transpose or rows % 8:
        res = pl.pallas_call(body, name=name, out_shape=out_shape, compiler_params=_params())(*flat)
    else:
        tile = pl.BlockSpec((rows, C), lambda i: (i, 0))
        res = pl.pallas_call(
            body, name=name, grid=(4,), in_specs=[pl.BlockSpec((4, rows, C), lambda i: (0, i, 0)), tile, tile, tile] * n,
            out_specs=[tile] * (4 * n), out_shape=out_shape, compiler_params=_params(("parallel",)),
        )(*flat)
    return [tuple(res[4 * k: 4 * k + 4]) for k in range(n)]


SMALL_LAYOUT = (("ffn1_norm", 0, (8, LANES)), ("mix_norm", 8, (8, LANES)), ("ffn2_norm", 16, (8, LANES)),
                ("gate_bias", 24, (16, LANES)), ("pool_scale", 40, (4, LANES)), ("q_norm", 48, (1, HEAD_DIM)),
                ("k_norm", 56, (1, HEAD_DIM)), ("sinks", 64, (1, N_HEADS)))
LOSS_ROW = 72
SMALL_ROWS = 80


def _adamw_small(name, g_vec, g_pool_w, params):
    n = len(SMALL_LAYOUT) + 1

    def body(vec_ref, pw_ref, *refs):
        ins, outs = refs[:3 * n], refs[3 * n:]
        vec = vec_ref[0]
        pw = pw_ref[0]
        for i in range(1, N_DEV):
            vec = vec + vec_ref[i]
            pw = pw + pw_ref[i]
        grads = [vec[r0:r0 + shape[0], 0:shape[1]] for _, r0, shape in SMALL_LAYOUT] + [pw]
        for p, g in enumerate(grads):
            w_ref, m_ref, v_ref = ins[3 * p: 3 * p + 3]
            delta, mn, vn = _adam_math(w_ref[...], g, m_ref[...], v_ref[...])
            for o_ref, val in zip(outs[4 * p: 4 * p + 4], (g, delta, mn, vn)):
                o_ref[...] = val
        outs[4 * n][...] = vec[LOSS_ROW:LOSS_ROW + 1, :]

    flat = [a for wmv in params for a in wmv]
    out_shape = [jax.ShapeDtypeStruct(wmv[0].shape, F32) for wmv in params for _ in range(4)]
    out_shape.append(jax.ShapeDtypeStruct((1, LANES), F32))
    res = pl.pallas_call(body, name=name, out_shape=out_shape, compiler_params=_params())(g_vec, g_pool_w, *flat)
    return [tuple(res[4 * p: 4 * p + 4]) for p in range(n)], res[4 * n]


def _place():
    x, y, c = lax.axis_index("x"), lax.axis_index("y"), lax.axis_index("c")
    other_chips = [(1 - x, y), (x, 1 - y), (1 - x, 1 - y)]
    return x, y, c, other_chips


def _rows(ref, r, place, natural=False):
    px, py, pc = place
    b = 4 * px + 2 * py + pc if natural else 4 * pc + 2 * px + py
    return ref.at[pl.ds(pl.multiple_of(b * r, 8), r), :]


def _gather_task(shards, natural=(), forward_at=0.75):
    n = len(shards)
    rs = [s.shape[0] for s in shards]
    rows_of = lambda ref, k, place: _rows(ref, rs[k], place, k in natural)

    def copy(scr, outs, k, slot, block, to, src=None):
        rows = rows_of(outs[k], k, block)
        return pltpu.make_async_remote_copy(
            src_ref=rows if src is None else src, dst_ref=rows, send_sem=scr[0].at[7 * k + slot],
            recv_sem=scr[1].at[7 * k + slot], device_id=to, device_id_type=MESH)

    def first_sends(ins, outs, scr):
        x, y, c, chips = _place()
        me = (x, y, c)
        cps = [copy(scr, outs, k, 1 + j, me, (*chip, c), src=ins[k]) for j, chip in enumerate(chips) for k in range(n)]
        return cps + [copy(scr, outs, k, 0, me, (x, y, 1 - c), src=ins[k]) for k in range(n)]

    def passed_on(outs, scr):
        x, y, c, chips = _place()
        return [copy(scr, outs, k, 4 + j, (*chip, c), (x, y, 1 - c)) for j, chip in enumerate(chips) for k in range(n)]

    def local(ins, outs, scr):
        x, y, c, _ = _place()
        return [pltpu.make_async_copy(ins[k], rows_of(outs[k], k, (x, y, c)), scr[2].at[k]) for k in range(n)]

    def start(ins, outs, scr):
        for cp in local(ins, outs, scr) + first_sends(ins, outs, scr):
            cp.start()

    def forward(ins, outs, scr):
        x, y, c, chips = _place()
        for j, chip in enumerate(chips):
            for k in range(n):
                copy(scr, outs, k, 1 + j, (*chip, c), (x, y, c)).wait_recv()
        for cp in passed_on(outs, scr):
            cp.start()

    def finish(ins, outs, scr):
        x, y, c, chips = _place()
        for k in range(n):
            copy(scr, outs, k, 0, (x, y, 1 - c), (x, y, c)).wait_recv()
        for j, chip in enumerate(chips):
            for k in range(n):
                copy(scr, outs, k, 4 + j, (*chip, 1 - c), (x, y, c)).wait_recv()
        for cp in first_sends(ins, outs, scr) + passed_on(outs, scr):
            cp.wait_send()
        for cp in local(ins, outs, scr):
            cp.wait()

    out_shapes = [jax.ShapeDtypeStruct((N_DEV * s.shape[0], s.shape[1]), s.dtype) for s in shards]
    scratch = [pltpu.SemaphoreType.DMA((7 * n,)), pltpu.SemaphoreType.DMA((7 * n,)), pltpu.SemaphoreType.DMA((n,))]
    return _Task(shards, out_shapes, scratch, [(0, start), (forward_at, forward), (1.0, finish)])


def _all_gather(name, shards, natural=()):
    return _comm_only(name, [_gather_task(shards, natural)])[0]


def _chip_task(sums):
    n = len(sums)
    rs = [s.shape[0] // 4 for s in sums]

    def block(ref, k, chip_index):
        return ref.at[pl.ds(pl.multiple_of(chip_index * rs[k], 8), rs[k]), :]

    def copies(ins, outs, scr):
        send_sems, recv_sems, local_sems = scr
        x, y, c, chips = _place()
        here = 2 * x + y
        local = [pltpu.make_async_copy(block(ins[k], k, here), outs[k].at[here], local_sems.at[k]) for k in range(n)]
        remote = []
        for j, (px, py) in enumerate(chips):
            remote += [pltpu.make_async_remote_copy(
                src_ref=block(ins[k], k, 2 * px + py), dst_ref=outs[k].at[here],
                send_sem=send_sems.at[3 * k + j], recv_sem=recv_sems.at[3 * k + j],
                device_id=(px, py, c), device_id_type=MESH) for k in range(n)]
        return local, remote

    def start(ins, outs, scr):
        local, remote = copies(ins, outs, scr)
        for cp in local + remote:
            cp.start()

    def finish(ins, outs, scr):
        local, remote = copies(ins, outs, scr)
        for cp in remote:
            cp.wait()
        for cp in local:
            cp.wait()

    out_shapes = [jax.ShapeDtypeStruct((4, r, s.shape[1]), s.dtype) for r, s in zip(rs, sums)]
    scratch = [pltpu.SemaphoreType.DMA((3 * n,)), pltpu.SemaphoreType.DMA((3 * n,)), pltpu.SemaphoreType.DMA((n,))]
    return _Task(sums, out_shapes, scratch, [(0, start), (1.0, finish)])


def _dw_pair(name, a, b, scale, comm=None, blocks=1):
    T, M = a.shape
    N = b.shape[1]
    half = M // 2
    wide = half // blocks
    tk = min(2048, T)
    nK = T // tk
    plumb = _CommPlumbing(comm)

    def body(core_ref, *rest):
        a_refs, b_ref, rest = rest[:blocks], rest[blocks], rest[blocks + 1:]
        c_in = rest[:plumb.n_in]
        o_ref = rest[plumb.n_in]
        c_out = rest[plumb.n_in + 1: plumb.n_in + 1 + plumb.n_out]
        acc, stage, land, send_sem, recv_sem = rest[plumb.n_in + 1 + plumb.n_out: plumb.n_in + 6 + plumb.n_out]
        c_scr = rest[plumb.n_in + 6 + plumb.n_out:]
        i, k = pl.program_id(0), pl.program_id(1)
        x, y, c, _ = _place()
        push = pltpu.make_async_remote_copy(src_ref=stage, dst_ref=land, send_sem=send_sem, recv_sem=recv_sem,
                                            device_id=(x, y, 1 - c), device_id_type=MESH)
        if comm:
            plumb.run(i * nK + k, 2 * nK, True, c_in, c_out, c_scr)

        av = a_refs[0][...] if blocks == 1 else jnp.concatenate([r[...] for r in a_refs], axis=1)
        p = lax.dot_general(av, b_ref[...], _DIMS["tn"], preferred_element_type=F32)

        @pl.when(k == 0)
        def _():
            acc[...] = p

        @pl.when(k > 0)
        def _():
            acc[...] += p

        @pl.when((i == 0) & (k == nK - 1))
        def _():
            stage[...] = (scale * acc[...]).astype(BF)
            push.start()

        @pl.when((i == 1) & (k == nK - 1))
        def _():
            push.wait_recv()
            o_ref[...] = (scale * acc[...] + land[...].astype(F32)).astype(BF)
            push.wait_send()

        if comm:
            plumb.run(i * nK + k, 2 * nK, False, c_in, c_out, c_scr)

    grid_spec = pltpu.PrefetchScalarGridSpec(
        num_scalar_prefetch=1, grid=(2, nK),
        in_specs=[pl.BlockSpec((tk, wide), functools.partial(
            lambda i, k, core, j: (k, (2 * j if blocks > 1 else 0) + jnp.where(i == 0, 1 - core[0], core[0])), j=j))
            for j in range(blocks)] + [pl.BlockSpec((tk, N), lambda i, k, core: (k, 0))] + [ANY] * plumb.n_in,
        out_specs=[pl.BlockSpec((half, N), lambda i, k, core: (0, 0))] + [ANY] * plumb.n_out,
        scratch_shapes=[pltpu.VMEM((half, N), F32), pltpu.VMEM((half, N), BF), pltpu.VMEM((half, N), BF),
                        pltpu.SemaphoreType.DMA, pltpu.SemaphoreType.DMA] + plumb.scratch)
    core = lax.axis_index("c").astype(jnp.int32).reshape(1)
    res = pl.pallas_call(
        body, name=name, grid_spec=grid_spec,
        out_shape=[jax.ShapeDtypeStruct((half, N), BF)] + plumb.out_shapes,
        compiler_params=_params(("arbitrary", "arbitrary")),
    )(core, *([a] * blocks), b, *plumb.args)
    return (res[0], plumb.split_outputs(res[1:])) if comm else res[0]


def _pair_task(parts):
    n = len(parts)

    def copies(ins, outs, scr):
        x, y, c, _ = _place()
        return [pltpu.make_async_remote_copy(
            src_ref=ins[k].at[:, pl.ds(1 - c, 1)], dst_ref=outs[k], send_sem=scr[0].at[k], recv_sem=scr[1].at[k],
            device_id=(x, y, 1 - c), device_id_type=MESH) for k in range(n)]

    def start(ins, outs, scr):
        for cp in copies(ins, outs, scr):
            cp.start()

    def finish(ins, outs, scr):
        for cp in copies(ins, outs, scr):
            cp.wait()

    out_shapes = [jax.ShapeDtypeStruct((4, 1) + p.shape[2:], p.dtype) for p in parts]
    scratch = [pltpu.SemaphoreType.DMA((n,)), pltpu.SemaphoreType.DMA((n,))]
    return _Task(parts, out_shapes, scratch, [(0, start), (1.0, finish)])


def _pair_sum(name, part, got, core):
    _, _, r, C = part.shape

    def body(core_ref, p_ref, g_ref, o_ref):
        o_ref[0] = (p_ref[0, 0].astype(F32) + g_ref[0, 0].astype(F32)).astype(o_ref.dtype)

    return pl.pallas_call(
        body, name=name,
        grid_spec=pltpu.PrefetchScalarGridSpec(
            num_scalar_prefetch=1, grid=(4,),
            in_specs=[pl.BlockSpec((1, 1, r, C), lambda i, core_ref: (i, core_ref[0], 0, 0)),
                      pl.BlockSpec((1, 1, r, C), lambda i, core_ref: (i, 0, 0, 0))],
            out_specs=pl.BlockSpec((1, r, C), lambda i, core_ref: (i, 0, 0))),
        out_shape=jax.ShapeDtypeStruct((4, r, C), part.dtype), compiler_params=_params(("parallel",)),
    )(core, part, got)


def _ffn_bwd(tag, dy, dyb, x, gain, wgT, wuT, wd, saved, earlier=None):
    n, g, u, a = saved
    half = lambda accs, ex: _swiglu_bwd_epilogue([0.5 * accs[0]], ex)
    act_args = dict(tm=512, tn=1408, tk=D_MODEL, epilogue=half, extras=[(g, "tile", 0), (u, "tile", 0)], cols_outer=True)
    if earlier is None:
        sum_d = _dw_pair(tag + "_dw_down", a, dyb, 0.5)
        (dg, du), ((slots_d,),) = _mm(tag + "_d_act", [(dyb, wd, "nt", 0)], [BF, BF], comm=[_chip_task([sum_d])], **act_args)
        slots_e = None
        sum_g = _dw_pair(tag + "_dw_gate", dg, n, 1.0)
    else:
        sum_d, ((got,),) = _dw_pair(tag + "_dw_down", a, dyb, 0.5, comm=[_pair_task([earlier])])
        core = lax.axis_index("c").astype(jnp.int32).reshape(1)
        sum_e = _pair_sum(tag + "_pair_sum_earlier", earlier, got, core)
        sum_e = sum_e.reshape(4 * sum_e.shape[1], sum_e.shape[2])
        (dg, du), ((slots_e,),) = _mm(tag + "_d_act", [(dyb, wd, "nt", 0)], [BF, BF], comm=[_chip_task([sum_e])], **act_args)
        sum_g, ((slots_d,),) = _dw_pair(tag + "_dw_gate", dg, n, 1.0, comm=[_chip_task([sum_d])])
    sum_u, ((slots_g,),) = _dw_pair(tag + "_dw_up", du, n, 1.0, comm=[_chip_task([sum_g])])
    (dx, dxb, dgain), ((slots_u,),) = _mm(
        tag + "_d_norm", [(dg, wgT, "nn", 0), (du, wuT, "nn", 0)], [F32, BF], tm=512, tn=D_MODEL, tk=D_FF,
        epilogue=_rms_bwd_epilogue, extras=[(x, "tile", 0), (gain, "row", 0), (dy, "tile", 0)], n_colsum=1,
        comm=[_chip_task([sum_u])])
    return dx, dxb, dgain, slots_e, slots_g, slots_u, slots_d


def _tile_gain(g):
    return jnp.concatenate([g, g]).reshape(1, LANES)


def _fold_heads(partials):
    return jnp.sum(partials.reshape(-1, HEAD_DIM), axis=0)


def _pack_small_grads(grads, loss_local):
    pieces, row = [], 0
    for name, r0, _ in SMALL_LAYOUT + (("loss", LOSS_ROW, None),):
        v = (loss_local if name == "loss" else grads[name]).reshape(-1)
        rows = -(-v.size // LANES)
        block = jnp.pad(v, (0, rows * LANES - v.size)).reshape(rows, LANES)
        pieces += [jnp.zeros((r0 - row, LANES), F32)] * (r0 > row) + [block]
        row = r0 + rows
    pieces.append(jnp.zeros((SMALL_ROWS - row, LANES), F32))
    return jnp.concatenate(pieces, axis=0)


def kernel(x, ffn1_norm, ffn1_w_gate, ffn1_w_up, ffn1_w_down, mix_norm, w_in, pool_w, pool_scale, w_pool_out, q_norm, k_norm, sinks, w_attn_out, gate_bias, w_out, ffn2_norm, ffn2_w_gate, ffn2_w_up, ffn2_w_down, loss_target, m_ffn1_norm, m_ffn1_w_gate, m_ffn1_w_up, m_ffn1_w_down, m_mix_norm, m_w_in, m_pool_w, m_pool_scale, m_w_pool_out, m_q_norm, m_k_norm, m_sinks, m_w_attn_out, m_gate_bias, m_w_out, m_ffn2_norm, m_ffn2_w_gate, m_ffn2_w_up, m_ffn2_w_down, v_ffn1_norm, v_ffn1_w_gate, v_ffn1_w_up, v_ffn1_w_down, v_mix_norm, v_w_in, v_pool_w, v_pool_scale, v_w_pool_out, v_q_norm, v_k_norm, v_sinks, v_w_attn_out, v_gate_bias, v_w_out, v_ffn2_norm, v_ffn2_w_gate, v_ffn2_w_up, v_ffn2_w_down):
    T = x.shape[1]
    x2 = x.reshape(T, D_MODEL)
    target = loss_target.reshape(T, D_MODEL)

    big = [
        ("ffn1_w_gate", ffn1_w_gate, m_ffn1_w_gate, v_ffn1_w_gate, True, False),
        ("ffn1_w_up", ffn1_w_up, m_ffn1_w_up, v_ffn1_w_up, True, False),
        ("ffn1_w_down", ffn1_w_down, m_ffn1_w_down, v_ffn1_w_down, False, False),
        ("w_in", w_in, m_w_in, v_w_in, True, False),
        ("w_pool_out", w_pool_out, m_w_pool_out, v_w_pool_out, False, True),
        ("w_attn_out", w_attn_out, m_w_attn_out, v_w_attn_out, False, False),
        ("w_out", w_out, m_w_out, v_w_out, False, False),
        ("ffn2_w_gate", ffn2_w_gate, m_ffn2_w_gate, v_ffn2_w_gate, True, False),
        ("ffn2_w_up", ffn2_w_up, m_ffn2_w_up, v_ffn2_w_up, True, False),
        ("ffn2_w_down", ffn2_w_down, m_ffn2_w_down, v_ffn2_w_down, False, False),
    ]
    view = lambda a, tv: a.T if tv else a
    shards = _prep("prep_weights", [view(w, tv) for _, w, _, _, tv, _ in big], [tk_ for *_, tk_ in big])
    g1 =ffn1_norm.reshape(1, D_MODEL)
    g2 = mix_norm.reshape(1, D_MODEL)
    g3 = ffn2_norm.reshape(1, D_MODEL)
    bias_row = gate_bias.reshape(1, 2 * D_MODEL)
    qg, kg = _tile_gain(q_norm) * ATTN_SCALE, _tile_gain(k_norm)
    scale_row = pool_scale.reshape(1, POOL_WIDTH)

    n1, ((wg1T, wu1T),) = _rms_fwd("ffn1_norm", x2, g1, [_gather_task(shards[0:2], forward_at=0.9)])
    (gt1, up1, act1), ((wd1,), (w_inT,)) = _mm(
        "ffn1_gate_up", [(n1, wg1T, "nt", 0), (n1, wu1T, "nt", 1)], [BF, BF, BF], tm=512, tn=1408, tk=D_MODEL,
        epilogue=_swiglu_fwd_epilogue, cols_outer=True,
        comm=[_gather_task(shards[2:3], forward_at=0.5), _gather_task(shards[3:4], natural=(0,), forward_at=0.9)])
    h1, u = _mm("ffn1_down", [(act1, wd1, "nn", 0)], [F32, BF], tm=512, tn=D_MODEL, tk=D_FF,
                epilogue=_residual_norm_epilogue(0.5), extras=[(x2, "tile", 0), (g2, "row", 0)])
    saved1 = (n1, gt1, up1, act1)
    (proj,), ((w_poT, w_ao, w_o),) = _mm(
        "in_proj", [(u, w_inT, "nt", 0)], [BF], tm=512, tn=1280, tk=D_MODEL, cols_outer=True,
        comm=[_gather_task(shards[4:7], natural=(0, 1, 2), forward_at=0.8)])
    pooled, mixed = _pool_fwd("pool_fwd", proj, pool_w, scale_row)
    qn = _headnorm_fwd("q_norm", proj, COL_Q, ATTN_WIDTH, qg)
    kn = _headnorm_fwd("k_norm", proj, COL_K, KV_WIDTH, kg)
    attn, ((wg2T, wu2T),) = _attn_fwd("attn_fwd", qn, kn, proj, sinks,
                                      comm=[_gather_task(shards[7:9], forward_at=0.85)])
    (bp,) = _mm("pool_out", [(mixed, w_poT, "nt", 0)], [BF], tm=1024, tn=D_MODEL, tk=POOL_WIDTH)
    gate_tn = 256
    gate_extras = [(proj, "tile", COL_GP // gate_tn), (proj, "tile", COL_GA // gate_tn),
                   (bias_row, "row", 0), (bias_row, "row", D_MODEL // gate_tn)]
    merged, ba = _mm("attn_out_merge", [(attn, w_ao, "nn", 0)], [BF, BF], tm=2048, tn=gate_tn, tk=ATTN_WIDTH,
                     epilogue=_merge_fwd_epilogue, extras=[(bp, "tile", 0)] + gate_extras)
    h2, n2 = _mm("mix_out", [(merged, w_o, "nn", 0)], [F32, BF], tm=512, tn=D_MODEL, tk=D_MODEL,
                 epilogue=_residual_norm_epilogue(1.0), extras=[(h1, "tile", 0), (g3, "row", 0)])
    (gt2, up2, act2), ((wd2,),) = _mm(
        "ffn2_gate_up", [(n2, wg2T, "nt", 0), (n2, wu2T, "nt", 1)], [BF, BF, BF], tm=512, tn=1408, tk=D_MODEL,
        epilogue=_swiglu_fwd_epilogue, cols_outer=True, comm=[_gather_task(shards[9:10], forward_at=0.8)])
    dy, dyb, sq = _mm("ffn2_down_loss", [(act2, wd2, "nn", 0)], [F32, BF], tm=512, tn=D_MODEL, tk=D_FF,
                      epilogue=_loss_epilogue, extras=[(h2, "tile", 0), (target, "tile", 0)], n_colsum=1)
    loss_local = 0.5 * jnp.sum(sq) / D_MODEL

    dh2, dh2b, dg3, _, slots_g2, slots_u2, slots_d2 = _ffn_bwd(
        "ffn2", dy, dyb, h2, g3, wg2T, wu2T, wd2, (n2, gt2, up2, act2))
    dbp, dba, dgp, dga, cs_gp, cs_ga = _mm(
        "mix_out_bwd", [(dh2b, w_o, "nt", 0)], [BF, BF, BF, BF], tm=2048, tn=gate_tn, tk=D_MODEL,
        epilogue=_merge_bwd_epilogue, extras=[(bp, "tile", 0), (ba, "tile", 0)] + gate_extras, n_colsum=2)
    sum_o = _dw_pair("dw_out", merged, dh2b, 1.0, blocks=4)
    (dmixed,), ((slots_o,),) = _mm("pool_out_bwd", [(dbp, w_poT, "nn", 0)], [BF], tm=1024, tn=POOL_WIDTH, tk=D_MODEL,
                                   comm=[_chip_task([sum_o])])
    sum_po = _dw_pair("dw_pool_out", dbp, mixed, 1.0, blocks=4)
    (dattn,), ((slots_po,),) = _mm("attn_out_bwd", [(dba, w_ao, "nt", 0)], [BF], tm=1024, tn=ATTN_WIDTH, tk=D_MODEL,
                                   comm=[_chip_task([sum_po])])
    sum_ao = _dw_pair("dw_attn_out", attn, dba, 1.0, blocks=4)
    dxp, dpool_w, dpool_scale = _pool_bwd("pool_bwd", dmixed, pooled, pool_w, scale_row)
    dqn, dkn, dv, dsink_tile = _attn_bwd("attn_bwd", dattn, qn, kn, proj, sinks)
    dq, dqg = _headnorm_bwd("q_norm_bwd", dqn, proj, COL_Q, ATTN_WIDTH, qg)
    dk, dkg = _headnorm_bwd("k_norm_bwd", dkn, proj, COL_K, KV_WIDTH, kg)
    dproj = jnp.concatenate([dxp, dq, dk, dv, dgp, dga], axis=1)
    (dh1, dh1b, dg2), ((slots_ao,),) = _mm(
        "in_proj_bwd", [(dproj, w_inT, "nn", 0)], [F32, BF], tm=512, tn=D_MODEL, tk=IN_WIDTH, epilogue=_rms_bwd_epilogue,
        extras=[(h1, "tile", 0), (g2, "row", 0), (dh2, "tile", 0)], n_colsum=1, comm=[_chip_task([sum_ao])])
    (dw_inT,) = _mm("dw_in", [(dproj, u, "tn", 0)], [BF], tm=1280, tn=D_MODEL, tk=2048)
    dx, _, dg1, slots_in, slots_g1, slots_u1, slots_d1 = _ffn_bwd(
        "ffn1", dh1, dh1b, x2, g1, wg1T, wu1T, wd1, saved1, dw_inT.reshape(4, 2, IN_WIDTH // N_DEV, D_MODEL))

    slots = [slots_g1, slots_u1, slots_d1, slots_in, slots_po, slots_ao, slots_o, slots_g2, slots_u2, slots_d2]
    big_out = {}
    for label, group in (("ffn", (0, 1, 2, 7, 8, 9)), ("w_in", (3,)), ("w_pool_out", (4,)), ("attn_out_and_out", (5, 6))):
        items = [(slots[k], view(big[k][1], big[k][4]), view(big[k][2], big[k][4]), view(big[k][3], big[k][4]))
                 for k in group]
        for k, res in zip(group, _adamw_sharded("adamw_" + label, items, transpose=big[group[0]][5])):
            big_out[big[k][0]] = tuple(view(r, big[k][4]) for r in res)

    small_grads = {
        "ffn1_norm": jnp.sum(dg1, axis=(0, 1)), "mix_norm": jnp.sum(dg2, axis=(0, 1)), "ffn2_norm": jnp.sum(dg3, axis=(0, 1)),
        "gate_bias": jnp.concatenate([jnp.sum(cs_gp, axis=(0, 1)), jnp.sum(cs_ga, axis=(0, 1))]),
        "pool_scale": dpool_scale, "q_norm": _fold_heads(dqg) * ATTN_SCALE, "k_norm": _fold_heads(dkg),
        "sinks": dsink_tile[0, :N_HEADS]}
    g_vec, g_pool_w = _all_gather("gather_small_grads", [_pack_small_grads(small_grads, loss_local),
                                                         dpool_w.reshape(-1, LANES)])
    given = {"ffn1_norm": (ffn1_norm, m_ffn1_norm, v_ffn1_norm), "mix_norm": (mix_norm, m_mix_norm, v_mix_norm),
             "ffn2_norm": (ffn2_norm, m_ffn2_norm, v_ffn2_norm), "gate_bias": (gate_bias, m_gate_bias, v_gate_bias),
             "pool_scale": (pool_scale, m_pool_scale, v_pool_scale), "q_norm": (q_norm, m_q_norm, v_q_norm),
             "k_norm": (k_norm, m_k_norm, v_k_norm), "sinks": (sinks, m_sinks, v_sinks)}
    params = [tuple(a.reshape(shape) for a in given[nm]) for nm, _, shape in SMALL_LAYOUT]
    params.append(tuple(a.reshape(-1, LANES) for a in (pool_w, m_pool_w, v_pool_w)))
    small_res, loss_row = _adamw_small("adamw_small", g_vec.reshape(N_DEV, SMALL_ROWS, LANES),
                                       g_pool_w.reshape(N_DEV, -1, LANES), params)
    small_out = {nm: tuple(r.reshape(given[nm][0].shape) for r in res)
                 for (nm, _, _), res in zip(SMALL_LAYOUT, small_res)}
    small_out["pool_w"] = tuple(r.reshape(pool_w.shape) for r in small_res[-1])
    loss = loss_row[0, 0]

    order = ["ffn1_norm", "ffn1_w_gate", "ffn1_w_up", "ffn1_w_down", "mix_norm", "w_in", "pool_w", "pool_scale",
             "w_pool_out", "q_norm", "k_norm", "sinks", "w_attn_out", "gate_bias", "w_out", "ffn2_norm",
             "ffn2_w_gate", "ffn2_w_up", "ffn2_w_down"]
    every = {**big_out, **small_out}
    outs = [loss, dx.reshape(x.shape)]
    for j in range(4):
        outs += [every[nm][j] for nm in order]
    return tuple(outs)
```

```python
import functools

import jax
import jax.numpy as jnp
from jax import lax
from jax.experimental import pallas as pl
from jax.experimental.pallas import tpu as pltpu

BF = jnp.bfloat16
F32 = jnp.float32

D_MODEL = 1024
D_FF = 2816
POOL_WIDTH = 512
POOL_GROUP = 128
N_POOL_GROUPS = 4
HEAD_DIM = 64
N_HEADS = 16
GQA_GROUP = 8
BLOCK = 128
ATTN_WIDTH = 1024
KV_WIDTH = 128
IN_WIDTH = 3840
RMS_EPS = 1e-6
N_DEV = 8
LANES = 128

COL_Q = POOL_WIDTH
COL_K = COL_Q + ATTN_WIDTH
COL_V = COL_K + KV_WIDTH
COL_GP = COL_V + KV_WIDTH
COL_GA = COL_GP + D_MODEL

ADAM_LR = 0.001
ADAM_B1 = 0.9
ADAM_B2 = 0.999
ADAM_EPS = 1e-08
ADAM_WD = 0.01
ADAM_STEP = 10

VMEM_LIMIT_V7X = 56 * 1024 * 1024
MESH = pl.DeviceIdType.MESH
ANY = pl.BlockSpec(memory_space=pl.ANY)


def _params(sem=None):
    return pltpu.CompilerParams(dimension_semantics=sem, vmem_limit_bytes=VMEM_LIMIT_V7X)


_DIMS = {"nt": (((1,), (1,)), ((), ())), "nn": (((1,), (0,)), ((), ())), "tn": (((0,), (0,)), ((), ()))}


class _Task:
    def __init__(self, inputs, out_shapes, scratch, phases):
        self.inputs, self.out_shapes, self.scratch = list(inputs), list(out_shapes), list(scratch)
        self.phases = list(phases)


class _CommPlumbing:
    def __init__(self, tasks):
        self.tasks = list(tasks or [])
        self.args = [a for t in self.tasks for a in t.inputs]
        self.out_shapes = [o for t in self.tasks for o in t.out_shapes]
        self.scratch = [s for t in self.tasks for s in t.scratch]
        self.n_in, self.n_out = len(self.args), len(self.out_shapes)

    def _slices(self, c_in, c_out, c_scr):
        i = o = s = 0
        for t in self.tasks:
            yield t, c_in[i:i + len(t.inputs)], c_out[o:o + len(t.out_shapes)], c_scr[s:s + len(t.scratch)]
            i, o, s = i + len(t.inputs), o + len(t.out_shapes), s + len(t.scratch)

    def run(self, step, steps, before, c_in, c_out, c_scr):
        for t, ins, outs, scr in self._slices(c_in, c_out, c_scr):
            for frac, fn in t.phases:
                if step is None:
                    fn(ins, outs, scr)
                elif before == (frac == 0):
                    at = 0 if frac == 0 else max(0, min(steps, -(-int(round(frac * steps * 64)) // 64)) - 1)
                    pl.when(step == at)(functools.partial(fn, ins, outs, scr))

    def split_outputs(self, flat):
        res, o = [], 0
        for t in self.tasks:
            res.append(list(flat[o:o + len(t.out_shapes)]))
            o += len(t.out_shapes)
        return res


def _comm_only(name, tasks):
    plumb = _CommPlumbing(tasks)

    def body(*refs):
        c_in, c_out = refs[:plumb.n_in], refs[plumb.n_in: plumb.n_in + plumb.n_out]
        c_scr = refs[plumb.n_in + plumb.n_out:]
        plumb.run(None, 1, True, c_in, c_out, c_scr)

    res = pl.pallas_call(
        body, name=name, in_specs=[ANY] * plumb.n_in, out_specs=[ANY] * plumb.n_out, out_shape=plumb.out_shapes,
        scratch_shapes=plumb.scratch, compiler_params=pltpu.CompilerParams(has_side_effects=True),
    )(*plumb.args)
    return plumb.split_outputs(res)


def _mm(name, terms, out_dtypes, *, tm, tn, tk, epilogue=None, extras=(), n_colsum=0, comm=None, cols_outer=False):
    a0, b0, mode0, _ = terms[0]
    if mode0 == "nt":
        (M, K), N = a0.shape, b0.shape[0]
    elif mode0 == "nn":
        (M, K), N = a0.shape, b0.shape[1]
    else:
        (K, M), N = a0.shape, b0.shape[1]
    tm, tn, tk = min(tm, M), min(tn, N), min(tk, K)
    assert M % tm == 0 and N % tn == 0 and K % tk == 0, (name, M, N, K, tm, tn, tk)
    nI, nJ, nK = M // tm, N // tn, K // tk
    n_terms = len(terms)
    n_acc = max(t[3] for t in terms) + 1
    n_ex = len(extras)
    n_out = len(out_dtypes)
    if epilogue is None:
        epilogue = lambda accs, ex: ([accs[0]], [])
    plumb = _CommPlumbing(comm)
    n_scr = n_acc if nK > 1 else 0
    grid = (nJ, nI, nK) if cols_outer else (nI, nJ, nK)

    def body(*refs):
        n_in = 2 * n_terms + n_ex
        ab = refs[: 2 * n_terms]
        ex_refs = refs[2 * n_terms: n_in]
        c_in = refs[n_in: n_in + plumb.n_in]
        o0 = n_in + plumb.n_in
        out_refs = refs[o0: o0 + n_out]
        cs_refs = refs[o0 + n_out: o0 + n_out + n_colsum]
        c_out = refs[o0 + n_out + n_colsum: o0 + n_out + n_colsum + plumb.n_out]
        s0 = o0 + n_out + n_colsum + plumb.n_out
        acc_refs = refs[s0: s0 + n_scr]
        c_scr = refs[s0 + n_scr:]
        steps = grid[0] * grid[1] * nK
        if comm:
            step = (pl.program_id(0) * grid[1] + pl.program_id(1)) * nK + pl.program_id(2)
            plumb.run(step, steps, True, c_in, c_out, c_scr)

        def products():
            accs = [None] * n_acc
            for t, (_, _, mode, ai) in enumerate(terms):
                p = lax.dot_general(ab[2 * t][...], ab[2 * t + 1][...], _DIMS[mode], preferred_element_type=F32)
                accs[ai] = p if accs[ai] is None else accs[ai] + p
            return accs

        def finish(accs):
            outs, colsums = epilogue(accs, [r[...] for r in ex_refs])
            for r, o in zip(out_refs, outs):
                r[...] = o.astype(r.dtype)
            for r, cs in zip(cs_refs, colsums):
                r[...] = jnp.sum(cs, axis=0, keepdims=True).reshape(r.shape)

        if nK == 1:
            finish(products())
        else:
            k = pl.program_id(2)
            accs = products()

            @pl.when(k == 0)
            def _():
                for r, a in zip(acc_refs, accs):
                    r[...] = a

            @pl.when(k > 0)
            def _():
                for r, a in zip(acc_refs, accs):
                    r[...] += a

            @pl.when(k == nK - 1)
            def _():
                finish([r[...] for r in acc_refs])

        if comm:
            plumb.run(step, steps, False, c_in, c_out, c_scr)

    def spec(block, index, fixed=False):
        imap = (lambda q, p, k: index(p, q, k)) if cols_outer else index
        return pl.BlockSpec(block, imap, pipeline_mode=pl.Buffered(1)) if fixed else pl.BlockSpec(block, imap)

    in_specs, args = [], []
    for a, b, mode, _ in terms:
        if mode == "nt":
            in_specs += [spec((tm, tk), lambda i, j, k: (i, k), nI * nK == 1),
                         spec((tn, tk), lambda i, j, k: (j, k), nJ * nK == 1)]
        elif mode == "nn":
            in_specs += [spec((tm, tk), lambda i, j, k: (i, k), nI * nK == 1),
                         spec((tk, tn), lambda i, j, k: (k, j), nJ * nK == 1)]
        else:
            in_specs += [spec((tk, tm), lambda i, j, k: (k, i), nI * nK == 1),
                         spec((tk, tn), lambda i, j, k: (k, j), nJ * nK == 1)]
        args += [a, b]
    for arr, kind, off in extras:
        if kind == "tile":
            in_specs.append(spec((tm, tn), functools.partial(lambda i, j, k, off: (i, j + off), off=off)))
        else:
            in_specs.append(spec((1, tn), functools.partial(lambda i, j, k, off: (0, j + off), off=off)))
        args.append(arr)
    out_shape = [jax.ShapeDtypeStruct((M, N), dt) for dt in out_dtypes]
    out_specs = [spec((tm, tn), lambda i, j, k: (i, j)) for _ in out_dtypes]
    out_shape += [jax.ShapeDtypeStruct((nI, 1, N), F32) for _ in range(n_colsum)]
    out_specs += [spec((1, 1, tn), lambda i, j, k: (i, 0, j)) for _ in range(n_colsum)]
    scratch = [pltpu.VMEM((tm, tn), F32) for _ in range(n_scr)]
    args += plumb.args
    in_specs += [ANY] * plumb.n_in
    out_shape += plumb.out_shapes
    out_specs += [ANY] * plumb.n_out
    sem = ("arbitrary",) * 3 if comm else ("parallel", "parallel", "arbitrary")
    res = pl.pallas_call(
        body, name=name, grid=grid, in_specs=in_specs, out_specs=out_specs, out_shape=out_shape,
        scratch_shapes=scratch + plumb.scratch, compiler_params=_params(sem),
    )(*args)
    n_own = n_out + n_colsum
    return (list(res[:n_own]), plumb.split_outputs(res[n_own:])) if comm is not None else res


ROW_TILE = 512


def _rms_fwd(name, x, g, comm):
    T, D = x.shape
    steps = T // ROW_TILE
    plumb = _CommPlumbing(comm)

    def body(x_ref, g_ref, *rest):
        c_in, o_ref = rest[:plumb.n_in], rest[plumb.n_in]
        c_out, c_scr = rest[plumb.n_in + 1: plumb.n_in + 1 + plumb.n_out], rest[plumb.n_in + 1 + plumb.n_out:]
        plumb.run(pl.program_id(0), steps, True, c_in, c_out, c_scr)
        xv = x_ref[...]
        r = lax.rsqrt(jnp.mean(xv * xv, axis=-1, keepdims=True) + RMS_EPS)
        o_ref[...] = (xv * r * g_ref[...]).astype(BF)
        plumb.run(pl.program_id(0), steps, False, c_in, c_out, c_scr)

    row = pl.BlockSpec((ROW_TILE, D), lambda i: (i, 0))
    res = pl.pallas_call(
        body, name=name, grid=(steps,),
        in_specs=[row, pl.BlockSpec((1, D), lambda i: (0, 0))] + [ANY] * plumb.n_in,
        out_specs=[row] + [ANY] * plumb.n_out, out_shape=[jax.ShapeDtypeStruct((T, D), BF)] + plumb.out_shapes,
        scratch_shapes=plumb.scratch, compiler_params=_params(("arbitrary",)),
    )(x, g, *plumb.args)
    return res[0], plumb.split_outputs(res[1:])


HEADNORM_TILE = 1024


def _half_sum_matrix():
    r = lax.broadcasted_iota(jnp.int32, (LANES, LANES), 0) // HEAD_DIM
    c = lax.broadcasted_iota(jnp.int32, (LANES, LANES), 1) // HEAD_DIM
    return (r == c).astype(BF)


def _head_mean(v, ones_blockdiag):
    hi = v.astype(BF)
    lo = (v - hi.astype(F32)).astype(BF)
    s = jnp.dot(hi, ones_blockdiag, preferred_element_type=F32) + jnp.dot(lo, ones_blockdiag, preferred_element_type=F32)
    return s * (1.0 / HEAD_DIM)


def _headnorm_fwd(name, proj, col0, width, g2):
    T = proj.shape[0]
    wide = min(width, GROUP_WIDTH)
    nb, off = width // wide, col0 // wide

    def body(x_ref, g_ref, b_ref, o_ref):
        for s in range(wide // LANES):
            lanes = slice(LANES * s, LANES * (s + 1))
            xv = x_ref[:, lanes].astype(F32)
            r = lax.rsqrt(_head_mean(xv * xv, b_ref[...]) + RMS_EPS)
            o_ref[:, lanes] = (xv * r * g_ref[...]).astype(BF)

    return pl.pallas_call(
        body, name=name, grid=(T // HEADNORM_TILE, nb),
        in_specs=[pl.BlockSpec((HEADNORM_TILE, wide), lambda i, j: (i, j + off)),
                  pl.BlockSpec((1, LANES), lambda i, j: (0, 0)), pl.BlockSpec((LANES, LANES), lambda i, j: (0, 0))],
        out_specs=pl.BlockSpec((HEADNORM_TILE, wide), lambda i, j: (i, j)),
        out_shape=jax.ShapeDtypeStruct((T, width), BF), compiler_params=_params(("parallel", "parallel")),
    )(proj, g2, _half_sum_matrix())


def _headnorm_bwd(name, dy, proj, col0, width, g2):
    T = proj.shape[0]
    wide = min(width, GROUP_WIDTH)
    nb, off = width // wide, col0 // wide

    def body(dy_ref, x_ref, g_ref, b_ref, dx_ref, dg_ref):
        for s in range(wide // LANES):
            lanes = slice(LANES * s, LANES * (s + 1))
            xv = x_ref[:, lanes].astype(F32)
            dyv = dy_ref[:, lanes].astype(F32)
            r = lax.rsqrt(_head_mean(xv * xv, b_ref[...]) + RMS_EPS)
            xhat = xv * r
            dxhat = dyv * g_ref[...]
            dx_ref[:, lanes] = (r * (dxhat - xhat * _head_mean(dxhat * xhat, b_ref[...]))).astype(BF)
            dg_ref[0, :, lanes] = jnp.sum(dyv * xhat, axis=0, keepdims=True)

    return pl.pallas_call(
        body, name=name, grid=(T // HEADNORM_TILE, nb),
        in_specs=[pl.BlockSpec((HEADNORM_TILE, wide), lambda i, j: (i, j)),
                  pl.BlockSpec((HEADNORM_TILE, wide), lambda i, j: (i, j + off)),
                  pl.BlockSpec((1, LANES), lambda i, j: (0, 0)), pl.BlockSpec((LANES, LANES), lambda i, j: (0, 0))],
        out_specs=[pl.BlockSpec((HEADNORM_TILE, wide), lambda i, j: (i, j)),
                   pl.BlockSpec((1, 1, wide), lambda i, j: (i, 0, j))],
        out_shape=[jax.ShapeDtypeStruct((T, width), BF), jax.ShapeDtypeStruct((T // HEADNORM_TILE, 1, width), F32)],
        compiler_params=_params(("parallel", "parallel")),
    )(dy, proj, g2, _half_sum_matrix())


def _shift_down(v, k, row):
    return jnp.where(row >= k, pltpu.roll(v, k, axis=0), 0.0)


def _shift_up(v, k, row, T):
    return jnp.where(row < T - k, pltpu.roll(v, T - k, axis=0), 0.0)


def _by_group(g, vals):
    out = vals[-1]
    for i in range(len(vals) - 2, -1, -1):
        out = jnp.where(g == i, vals[i], out)
    return out


def _pool_fwd(name, proj, pool_w, pool_scale):
    T = proj.shape[0]

    def body(x_ref, w_ref, s_ref, pooled_ref, mixed_ref):
        g = pl.program_id(0)
        xv = x_ref[...].astype(F32)
        row = lax.broadcasted_iota(jnp.int32, (T, 1), 0)
        s2 = xv + _shift_down(xv, 1, row)
        s4 = s2 + _shift_down(s2, 2, row)
        s8 = s4 + _shift_down(s4, 4, row)
        s16 = s8 + _shift_down(s8, 8, row)
        wsum = _by_group(g, [s2, s4, s8, s16])
        count = jnp.minimum(row + 1, 2 << g).astype(F32)
        pooled = (wsum / count - xv).astype(BF)
        pooled_ref[...] = pooled
        mixed = jnp.dot(pooled, w_ref[0].astype(BF), preferred_element_type=F32) * s_ref[...]
        mixed_ref[...] = mixed.astype(BF)

    col = pl.BlockSpec((T, POOL_GROUP), lambda g: (0, g))
    return pl.pallas_call(
        body, name=name, grid=(N_POOL_GROUPS,),
        in_specs=[col, pl.BlockSpec((1, POOL_GROUP, POOL_GROUP), lambda g: (g, 0, 0)),
                  pl.BlockSpec((1, POOL_GROUP), lambda g: (0, g))],
        out_specs=[col, col],
        out_shape=[jax.ShapeDtypeStruct((T, POOL_WIDTH), BF), jax.ShapeDtypeStruct((T, POOL_WIDTH), BF)],
        compiler_params=_params(("parallel",)),
    )(proj, pool_w, pool_scale)


def _pool_bwd(name, dmixed, pooled, pool_w, pool_scale):
    T = dmixed.shape[0]

    def body(dm_ref, p_ref, w_ref, s_ref, dx_ref, dw_ref, ds_ref):
        g = pl.program_id(0)
        dm = dm_ref[...].astype(F32)
        pooled = p_ref[...]
        w = w_ref[0].astype(BF)
        pre = jnp.dot(pooled, w, preferred_element_type=F32)
        ds_ref[...] = jnp.sum(dm * pre, axis=0, keepdims=True)
        dms = (dm * s_ref[...]).astype(BF)
        dw_ref[0] = lax.dot_general(pooled, dms, _DIMS["tn"], preferred_element_type=F32)
        dpooled = lax.dot_general(dms, w, _DIMS["nt"], preferred_element_type=F32)
        row = lax.broadcasted_iota(jnp.int32, (T, 1), 0)
        count = jnp.minimum(row + 1, 2 << g).astype(F32)
        z = dpooled / count
        l2 = z + _shift_up(z, 1, row, T)
        l4 = l2 + _shift_up(l2, 2, row, T)
        l8 = l4 + _shift_up(l4, 4, row, T)
        l16 = l8 + _shift_up(l8, 8, row, T)
        dx_ref[...] = (_by_group(g, [l2, l4, l8, l16]) - dpooled).astype(BF)

    col = pl.BlockSpec((T, POOL_GROUP), lambda g: (0, g))
    wspec = pl.BlockSpec((1, POOL_GROUP, POOL_GROUP), lambda g: (g, 0, 0))
    sspec = pl.BlockSpec((1, POOL_GROUP), lambda g: (0, g))
    return pl.pallas_call(
        body, name=name, grid=(N_POOL_GROUPS,), in_specs=[col, col, wspec, sspec], out_specs=[col, wspec, sspec],
        out_shape=[jax.ShapeDtypeStruct((T, POOL_WIDTH), BF),
                   jax.ShapeDtypeStruct((N_POOL_GROUPS, POOL_GROUP, POOL_GROUP), F32),
                   jax.ShapeDtypeStruct((1, POOL_WIDTH), F32)],
        compiler_params=_params(("parallel",)),
    )(dmixed, pooled, pool_w, pool_scale)


ATTN_SCALE = HEAD_DIM ** -0.5
MASKED = float(jnp.finfo(jnp.float32).min)
KV_COL_BLOCK_K = COL_K // LANES
KV_COL_BLOCK_V = COL_V // LANES
GROUP_WIDTH = GQA_GROUP * HEAD_DIM


def _dup_head(v, j):
    half = lax.broadcasted_iota(jnp.int32, (1, LANES), 1) // HEAD_DIM
    return jnp.where(half == j, v, pltpu.roll(v, HEAD_DIM, axis=1))


def _stack_heads(v, low):
    pieces = []
    for p in range(GROUP_WIDTH // LANES):
        vp = v[:, LANES * p: LANES * (p + 1)]
        pieces.append(jnp.where(low, vp, jnp.zeros_like(vp)))
        pieces.append(jnp.where(low, jnp.zeros_like(vp), vp))
    return jnp.concatenate(pieces, axis=0)


def _unstack_transposed(t, low):
    pairs = []
    for p in range(GROUP_WIDTH // LANES):
        even = t[:, BLOCK * (2 * p): BLOCK * (2 * p + 1)].T
        odd = t[:, BLOCK * (2 * p + 1): BLOCK * (2 * p + 2)].T
        pairs.append(jnp.where(low, even, odd))
    return pairs


def _softmax_keys_on_sublanes(k2, q, n, sink_ref, j):
    stacked = GQA_GROUP * BLOCK
    key = lax.broadcasted_iota(jnp.int32, (2 * BLOCK, stacked), 0)
    qry = lax.broadcasted_iota(jnp.int32, (2 * BLOCK, stacked), 1) % BLOCK
    valid = (key > qry) & (key <= qry + BLOCK) & ((n > 0) | (key >= BLOCK))
    head_of_lane = lax.broadcasted_iota(jnp.int32, (1, stacked), 1) // BLOCK
    sink = jnp.zeros((1, stacked), F32)
    for h in range(GQA_GROUP):
        sink = jnp.where(head_of_lane == h, sink_ref[j * GQA_GROUP + h], sink)
    s = jnp.where(valid, lax.dot_general(k2, q, _DIMS["nt"], preferred_element_type=F32), MASKED)
    m = jnp.maximum(jnp.max(s, axis=0, keepdims=True), sink)
    e = jnp.exp(s - m)
    e_sink = jnp.exp(sink - m)
    inv = 1.0 / (jnp.sum(e, axis=0, keepdims=True) + e_sink)
    return e * inv, e_sink * inv


def _attn_fwd(name, qn, kn, proj, sinks, comm=None):
    T = qn.shape[0]
    nb = T // BLOCK
    plumb = _CommPlumbing(comm)

    def body(sink_ref, q_ref, kp_ref, kc_ref, vp_ref, vc_ref, *rest):
        c_in, o_ref = rest[:plumb.n_in], rest[plumb.n_in]
        c_out, c_scr = rest[plumb.n_in + 1: plumb.n_in + 1 + plumb.n_out], rest[plumb.n_in + 1 + plumb.n_out:]
        n = pl.program_id(0)
        plumb.run(n, nb, True, c_in, c_out, c_scr)
        low = lax.broadcasted_iota(jnp.int32, (1, LANES), 1) < HEAD_DIM
        kk = jnp.concatenate([kp_ref[...], kc_ref[...]], axis=0)
        vv = jnp.concatenate([vp_ref[...], vc_ref[...]], axis=0)
        for j in range(2):
            q = _stack_heads(q_ref[:, GROUP_WIDTH * j: GROUP_WIDTH * (j + 1)], low)
            p, _ = _softmax_keys_on_sublanes(_dup_head(kk, j), q, n, sink_ref, j)
            o_t = lax.dot_general(_dup_head(vv, j), p.astype(BF), _DIMS["tn"], preferred_element_type=F32)
            for pair, o in enumerate(_unstack_transposed(o_t, low)):
                lanes = slice(GROUP_WIDTH * j + LANES * pair, GROUP_WIDTH * j + LANES * (pair + 1))
                o_ref[:, lanes] = o.astype(BF)
        plumb.run(n, nb, False, c_in, c_out, c_scr)

    wide = pl.BlockSpec((BLOCK, ATTN_WIDTH), lambda n: (n, 0))
    res = pl.pallas_call(
        body, name=name, grid=(nb,),
        in_specs=[pl.BlockSpec(memory_space=pltpu.SMEM), wide,
                  pl.BlockSpec((BLOCK, LANES), lambda n: (jnp.maximum(n - 1, 0), 0)),
                  pl.BlockSpec((BLOCK, LANES), lambda n: (n, 0)),
                  pl.BlockSpec((BLOCK, LANES), lambda n: (jnp.maximum(n - 1, 0), KV_COL_BLOCK_V)),
                  pl.BlockSpec((BLOCK, LANES), lambda n: (n, KV_COL_BLOCK_V))] + [ANY] * plumb.n_in,
        out_specs=[wide] + [ANY] * plumb.n_out,
        out_shape=[jax.ShapeDtypeStruct((T, ATTN_WIDTH), BF)] + plumb.out_shapes, scratch_shapes=plumb.scratch,
        compiler_params=_params(("arbitrary",) if comm else ("parallel",)),
    )(sinks, qn, kn, kn, proj, proj, *plumb.args)
    return (res[0], plumb.split_outputs(res[1:])) if comm is not None else res[0]


def _attn_bwd(name, dout, qn, kn, proj, sinks, comm):
    T = qn.shape[0]
    nb = T // BLOCK
    plumb = _CommPlumbing(comm)

    def body(sink_ref, do_ref, q_ref, kp_ref, kc_ref, vp_ref, vc_ref, *rest):
        c_in, (dq_ref, dk_ref, dv_ref, dsink_ref) = rest[:plumb.n_in], rest[plumb.n_in: plumb.n_in + 4]
        c_out = rest[plumb.n_in + 4: plumb.n_in + 4 + plumb.n_out]
        carry_k, carry_v, tot_k, tot_v = rest[plumb.n_in + 4 + plumb.n_out: plumb.n_in + 8 + plumb.n_out]
        c_scr = rest[plumb.n_in + 8 + plumb.n_out:]
        n = pl.program_id(0)
        plumb.run(n, nb + 1, True, c_in, c_out, c_scr)
        lane = lax.broadcasted_iota(jnp.int32, (1, LANES), 1)
        low = lane < HEAD_DIM

        @pl.when(n == 0)
        def _():
            carry_k[...] = jnp.zeros_like(carry_k)
            carry_v[...] = jnp.zeros_like(carry_v)
            dsink_ref[...] = jnp.zeros_like(dsink_ref)

        @pl.when(n == nb)
        def _():
            tot_k[...] = jnp.zeros_like(tot_k)
            tot_v[...] = jnp.zeros_like(tot_v)

        @pl.when(n < nb)
        def _():
            kk = jnp.concatenate([kp_ref[...], kc_ref[...]], axis=0)
            vv = jnp.concatenate([vp_ref[...], vc_ref[...]], axis=0)
            dk_tot = jnp.zeros((2 * BLOCK, LANES), F32)
            dv_tot = jnp.zeros((2 * BLOCK, LANES), F32)
            dsink = jnp.zeros((1, LANES), F32)
            for j in range(2):
                k2 = _dup_head(kk, j)
                v2 = _dup_head(vv, j)
                q = _stack_heads(q_ref[:, GROUP_WIDTH * j: GROUP_WIDTH * (j + 1)], low)
                do = _stack_heads(do_ref[:, GROUP_WIDTH * j: GROUP_WIDTH * (j + 1)], low)
                p, psink = _softmax_keys_on_sublanes(k2, q, n, sink_ref, j)
                dp =lax.dot_general(v2, do, _DIMS["nt"], preferred_element_type=F32)
                delta = jnp.sum(p * dp, axis=0, keepdims=True)
                ds = (p * (dp - delta)).astype(BF)
                dk2 = jnp.dot(ds, q, preferred_element_type=F32)
                dv2 = jnp.dot(p.astype(BF), do, preferred_element_type=F32)
                dq_t = lax.dot_general(k2, ds, _DIMS["tn"], preferred_element_type=F32)
                for pair, dq in enumerate(_unstack_transposed(dq_t, low)):
                    lanes = slice(GROUP_WIDTH * j + LANES * pair, GROUP_WIDTH * j + LANES * (pair + 1))
                    dq_ref[:, lanes] = dq.astype(BF)
                mine = low if j == 0 else jnp.logical_not(low)
                dk_tot = dk_tot + jnp.where(mine, dk2 + pltpu.roll(dk2, HEAD_DIM, axis=1), 0.0)
                dv_tot = dv_tot + jnp.where(mine, dv2 + pltpu.roll(dv2, HEAD_DIM, axis=1), 0.0)
                sink_term = psink * delta
                for h in range(GQA_GROUP):
                    val = -jnp.sum(sink_term[:, BLOCK * h: BLOCK * (h + 1)], axis=1, keepdims=True)
                    dsink = dsink + jnp.where(lane == j * GQA_GROUP + h, val, 0.0)
            tot_k[...] = dk_tot
            tot_v[...] = dv_tot
            dsink_ref[0:1, :] += dsink

        dk_ref[...] = (carry_k[...] + tot_k[0:BLOCK]).astype(BF)
        dv_ref[...] = (carry_v[...] + tot_v[0:BLOCK]).astype(BF)
        carry_k[...] = tot_k[BLOCK:]
        carry_v[...] = tot_v[BLOCK:]
        plumb.run(n, nb + 1, False, c_in, c_out, c_scr)

    cur = lambda n: (jnp.minimum(n, nb - 1), 0)
    prev = lambda n: (jnp.maximum(n - 1, 0), 0)
    wide = pl.BlockSpec((BLOCK, ATTN_WIDTH), cur)
    res = pl.pallas_call(
        body, name=name, grid=(nb + 1,),
        in_specs=[pl.BlockSpec(memory_space=pltpu.SMEM), wide, wide,
                  pl.BlockSpec((BLOCK, LANES), prev), pl.BlockSpec((BLOCK, LANES), cur),
                  pl.BlockSpec((BLOCK, LANES), lambda n: (jnp.maximum(n - 1, 0), KV_COL_BLOCK_V)),
                  pl.BlockSpec((BLOCK, LANES), lambda n: (jnp.minimum(n, nb - 1), KV_COL_BLOCK_V))] + [ANY] * plumb.n_in,
        out_specs=[wide, pl.BlockSpec((BLOCK, LANES), prev), pl.BlockSpec((BLOCK, LANES), prev),
                   pl.BlockSpec((8, LANES), lambda n: (0, 0))] + [ANY] * plumb.n_out,
        out_shape=[jax.ShapeDtypeStruct((T, ATTN_WIDTH), BF), jax.ShapeDtypeStruct((T, KV_WIDTH), BF),
                   jax.ShapeDtypeStruct((T, KV_WIDTH), BF), jax.ShapeDtypeStruct((8, LANES), F32)] + plumb.out_shapes,
        scratch_shapes=[pltpu.VMEM((BLOCK, LANES), F32), pltpu.VMEM((BLOCK, LANES), F32),
                        pltpu.VMEM((2 * BLOCK, LANES), F32), pltpu.VMEM((2 * BLOCK, LANES), F32)] + plumb.scratch,
        compiler_params=_params(("arbitrary",)),
    )(sinks, dout, qn, kn, kn, proj, proj, *plumb.args)
    return list(res[:4]), plumb.split_outputs(res[4:])


def _swiglu_fwd_epilogue(accs, ex):
    g, u = accs
    return [g, u, g * jax.nn.sigmoid(g) * u], []


def _swiglu_bwd_epilogue(accs, ex):
    (da,) = accs
    g, u = ex[0].astype(F32), ex[1].astype(F32)
    s = jax.nn.sigmoid(g)
    return [da * u * (s * (1.0 + g * (1.0 - s))), da * (g * s)], []


def _residual_norm_epilogue(scale):
    def epilogue(accs, ex):
        res, gain = ex
        h = res + scale * accs[0]
        r = lax.rsqrt(jnp.mean(h * h, axis=-1, keepdims=True) + RMS_EPS)
        return [h, h * r * gain], []
    return epilogue


def _rms_bwd_epilogue(accs, ex):
    (dn,) = accs
    xv, g, dres = ex
    r = lax.rsqrt(jnp.mean(xv * xv, axis=-1, keepdims=True) + RMS_EPS)
    xhat = xv * r
    dxhat = dn * g
    dx = dres + r * (dxhat - xhat * jnp.mean(dxhat * xhat, axis=-1, keepdims=True))
    return [dx, dx], [dn * xhat]


def _loss_epilogue(accs, ex):
    xv, target = ex
    d = xv + 0.5 * accs[0] - target
    dy = d * (1.0 / D_MODEL)
    return [dy, dy], [d * d]


def _merge_fwd_epilogue(accs, ex):
    (ba,) = accs
    bp, gp_pre, ga_pre, bias_p, bias_a = ex
    gp = jax.nn.sigmoid(gp_pre.astype(F32) + bias_p)
    ga = jax.nn.sigmoid(ga_pre.astype(F32) + bias_a)
    return [gp * bp.astype(F32) + ga * ba, ba], []


def _merge_bwd_epilogue(accs, ex):
    (dm,) = accs
    bp, ba, gp_pre, ga_pre, bias_p, bias_a = ex
    gp = jax.nn.sigmoid(gp_pre.astype(F32) + bias_p)
    ga = jax.nn.sigmoid(ga_pre.astype(F32) + bias_a)
    dgp = dm * bp.astype(F32) * gp * (1.0 - gp)
    dga = dm * ba.astype(F32) * ga * (1.0 - ga)
    return [dm * gp, dm * ga, dgp, dga], [dgp, dga]


def _prep(name, ws, transposes):
    n = len(ws)

    def body(*refs):
        for w_ref, o_ref, tr in zip(refs[:n], refs[n:], transposes):
            v = w_ref[...]
            o_ref[...] = (v.T if tr else v).astype(BF)

    shapes = [jax.ShapeDtypeStruct(w.shape[::-1] if tr else w.shape, BF) for w, tr in zip(ws, transposes)]
    return pl.pallas_call(body, name=name, out_shape=shapes, compiler_params=_params())(*ws)


def _adam_math(w, g, m, v):
    m = ADAM_B1 * m + (1.0 - ADAM_B1) * g
    v = ADAM_B2 * v + (1.0 - ADAM_B2) * jnp.square(g)
    m_hat = m / (1.0 - ADAM_B1 ** ADAM_STEP)
    v_hat = v / (1.0 - ADAM_B2 ** ADAM_STEP)
    delta = -ADAM_LR * (m_hat / (jnp.sqrt(v_hat) + ADAM_EPS) + ADAM_WD * w)
    return delta, m, v


def _adamw_sharded(name, items, transpose=False):
    n = len(items)

    def body(*refs):
        ins, outs = refs[:4 * n], refs[4 * n:]
        for k in range(n):
            s_ref, w_ref, m_ref, v_ref = ins[4 * k: 4 * k + 4]
            g = s_ref[0].astype(F32)
            for i in range(1, 4):
                g = g + s_ref[i].astype(F32)
            if transpose:
                g = g.T
            delta, mn, vn = _adam_math(w_ref[...], g, m_ref[...], v_ref[...])
            for o_ref, val in zip(outs[4 * k: 4 * k + 4], (g, delta, mn, vn)):
                o_ref[...] = val

    flat = [a for item in items for a in item]
    out_shape = [jax.ShapeDtypeStruct(item[1].shape, F32) for item in items for _ in range(4)]
    _, r, C = items[0][0].shape
    rows = r // 4
    if transpose or rows % 8:
        res = pl.pallas_call(body, name=name, out_shape=out_shape, compiler_params=_params())(*flat)
    else:
        tile = pl.BlockSpec((rows, C), lambda i: (i, 0))
        res = pl.pallas_call(
            body, name=name, grid=(4,), in_specs=[pl.BlockSpec((4, rows, C), lambda i: (0, i, 0)), tile, tile, tile] * n,
            out_specs=[tile] * (4 * n), out_shape=out_shape, compiler_params=_params(("parallel",)),
        )(*flat)
    return [tuple(res[4 * k: 4 * k + 4]) for k in range(n)]


SMALL_LAYOUT = (("ffn1_norm", 0, (8, LANES)), ("mix_norm", 8, (8, LANES)), ("ffn2_norm", 16, (8, LANES)),
                ("gate_bias", 24, (16, LANES)), ("pool_scale", 40, (4, LANES)), ("q_norm", 48, (1, HEAD_DIM)),
                ("k_norm", 56, (1, HEAD_DIM)), ("sinks", 64, (1, N_HEADS)))
LOSS_ROW = 72
SMALL_ROWS = 80


def _adamw_small(name, g_vec, g_pool_w, params):
    n = len(SMALL_LAYOUT) + 1

    def body(vec_ref, pw_ref, *refs):
        ins, outs = refs[:3 * n], refs[3 * n:]
        vec = vec_ref[0]
        pw = pw_ref[0]
        for i in range(1, N_DEV):
            vec = vec + vec_ref[i]
            pw = pw + pw_ref[i]
        grads = [vec[r0:r0 + shape[0], 0:shape[1]] for _, r0, shape in SMALL_LAYOUT] + [pw]
        for p, g in enumerate(grads):
            w_ref, m_ref, v_ref = ins[3 * p: 3 * p + 3]
            delta, mn, vn = _adam_math(w_ref[...], g, m_ref[...], v_ref[...])
            for o_ref, val in zip(outs[4 * p: 4 * p + 4], (g, delta, mn, vn)):
                o_ref[...] = val
        outs[4 * n][...] = vec[LOSS_ROW:LOSS_ROW + 1, :]

    flat = [a for wmv in params for a in wmv]
    out_shape = [jax.ShapeDtypeStruct(wmv[0].shape, F32) for wmv in params for _ in range(4)]
    out_shape.append(jax.ShapeDtypeStruct((1, LANES), F32))
    res = pl.pallas_call(body, name=name, out_shape=out_shape, compiler_params=_params())(g_vec, g_pool_w, *flat)
    return [tuple(res[4 * p: 4 * p + 4]) for p in range(n)], res[4 * n]


def _place():
    x, y, c = lax.axis_index("x"), lax.axis_index("y"), lax.axis_index("c")
    other_chips = [(1 - x, y), (x, 1 - y), (1 - x, 1 - y)]
    return x, y, c, other_chips


def _rows(ref, r, place, natural=False):
    px, py, pc = place
    b = 4 * px + 2 * py + pc if natural else 4 * pc + 2 * px + py
    return ref.at[pl.ds(pl.multiple_of(b * r, 8), r), :]


def _gather_task(shards, natural=(), forward_at=0.75):
    n = len(shards)
    rs = [s.shape[0] for s in shards]
    rows_of = lambda ref, k, place: _rows(ref, rs[k], place, k in natural)

    def copy(scr, outs, k, slot, block, to, src=None):
        rows = rows_of(outs[k], k, block)
        return pltpu.make_async_remote_copy(
            src_ref=rows if src is None else src, dst_ref=rows, send_sem=scr[0].at[7 * k + slot],
            recv_sem=scr[1].at[7 * k + slot], device_id=to, device_id_type=MESH)

    def first_sends(ins, outs, scr):
        x, y, c, chips = _place()
        me = (x, y, c)
        cps = [copy(scr, outs, k, 1 + j, me, (*chip, c), src=ins[k]) for j, chip in enumerate(chips) for k in range(n)]
        return cps + [copy(scr, outs, k, 0, me, (x, y, 1 - c), src=ins[k]) for k in range(n)]

    def passed_on(outs, scr):
        x, y, c, chips = _place()
        return [copy(scr, outs, k, 4 + j, (*chip, c), (x, y, 1 - c)) for j, chip in enumerate(chips) for k in range(n)]

    def local(ins, outs, scr):
        x, y, c, _ = _place()
        return [pltpu.make_async_copy(ins[k], rows_of(outs[k], k, (x, y, c)), scr[2].at[k]) for k in range(n)]

    def start(ins, outs, scr):
        for cp in local(ins, outs, scr) + first_sends(ins, outs, scr):
            cp.start()

    def forward(ins, outs, scr):
        x, y, c, chips = _place()
        for j, chip in enumerate(chips):
            for k in range(n):
                copy(scr, outs, k, 1 + j, (*chip, c), (x, y, c)).wait_recv()
        for cp in passed_on(outs, scr):
            cp.start()

    def finish(ins, outs, scr):
        x, y, c, chips = _place()
        for k in range(n):
            copy(scr, outs, k, 0, (x, y, 1 - c), (x, y, c)).wait_recv()
        for j, chip in enumerate(chips):
            for k in range(n):
                copy(scr, outs, k, 4 + j, (*chip, 1 - c), (x, y, c)).wait_recv()
        for cp in first_sends(ins, outs, scr) + passed_on(outs, scr):
            cp.wait_send()
        for cp in local(ins, outs, scr):
            cp.wait()

    out_shapes = [jax.ShapeDtypeStruct((N_DEV * s.shape[0], s.shape[1]), s.dtype) for s in shards]
    scratch = [pltpu.SemaphoreType.DMA((7 * n,)), pltpu.SemaphoreType.DMA((7 * n,)), pltpu.SemaphoreType.DMA((n,))]
    return _Task(shards, out_shapes, scratch, [(0, start), (forward_at, forward), (1.0, finish)])


def _all_gather(name, shards, natural=()):
    return _comm_only(name, [_gather_task(shards, natural)])[0]


def _chip_task(sums):
    n = len(sums)
    rs = [s.shape[0] // 4 for s in sums]

    def block(ref, k, chip_index):
        return ref.at[pl.ds(pl.multiple_of(chip_index * rs[k], 8), rs[k]), :]

    def copies(ins, outs, scr):
        send_sems, recv_sems, local_sems = scr
        x, y, c, chips = _place()
        here = 2 * x + y
        local = [pltpu.make_async_copy(block(ins[k], k, here), outs[k].at[here], local_sems.at[k]) for k in range(n)]
        remote = []
        for j, (px, py) in enumerate(chips):
            remote += [pltpu.make_async_remote_copy(
                src_ref=block(ins[k], k, 2 * px + py), dst_ref=outs[k].at[here],
                send_sem=send_sems.at[3 * k + j], recv_sem=recv_sems.at[3 * k + j],
                device_id=(px, py, c), device_id_type=MESH) for k in range(n)]
        return local, remote

    def start(ins, outs, scr):
        local, remote = copies(ins, outs, scr)
        for cp in local + remote:
            cp.start()

    def finish(ins, outs, scr):
        local, remote = copies(ins, outs, scr)
        for cp in remote:
            cp.wait()
        for cp in local:
            cp.wait()

    out_shapes = [jax.ShapeDtypeStruct((4, r, s.shape[1]), s.dtype) for r, s in zip(rs, sums)]
    scratch = [pltpu.SemaphoreType.DMA((3 * n,)), pltpu.SemaphoreType.DMA((3 * n,)), pltpu.SemaphoreType.DMA((n,))]
    return _Task(sums, out_shapes, scratch, [(0, start), (1.0, finish)])


def _dw_pair(name, a, b, scale, comm=None, blocks=1):
    T, M = a.shape
    N = b.shape[1]
    half = M // 2
    wide = half // blocks
    tk = min(2048, T)
    nK = T // tk
    plumb = _CommPlumbing(comm)

    def body(core_ref, *rest):
        a_refs, b_ref, rest = rest[:blocks], rest[blocks], rest[blocks + 1:]
        c_in = rest[:plumb.n_in]
        o_ref = rest[plumb.n_in]
        c_out = rest[plumb.n_in + 1: plumb.n_in + 1 + plumb.n_out]
        acc, stage, land, send_sem, recv_sem = rest[plumb.n_in + 1 + plumb.n_out: plumb.n_in + 6 + plumb.n_out]
        c_scr = rest[plumb.n_in + 6 + plumb.n_out:]
        i, k = pl.program_id(0), pl.program_id(1)
        x, y, c, _ = _place()
        push = pltpu.make_async_remote_copy(src_ref=stage, dst_ref=land, send_sem=send_sem, recv_sem=recv_sem,
                                            device_id=(x, y, 1 - c), device_id_type=MESH)
        if comm:
            plumb.run(i * nK + k, 2 * nK, True, c_in, c_out, c_scr)

        av = a_refs[0][...] if blocks == 1 else jnp.concatenate([r[...] for r in a_refs], axis=1)
        p = lax.dot_general(av, b_ref[...], _DIMS["tn"], preferred_element_type=F32)

        @pl.when(k == 0)
        def _():
            acc[...] = p

        @pl.when(k > 0)
        def _():
            acc[...] += p

        @pl.when((i == 0) & (k == nK - 1))
        def _():
            stage[...] = (scale * acc[...]).astype(BF)
            push.start()

        @pl.when((i == 1) & (k == nK - 1))
        def _():
            push.wait_recv()
            o_ref[...] = (scale * acc[...] + land[...].astype(F32)).astype(BF)
            push.wait_send()

        if comm:
            plumb.run(i * nK + k, 2 * nK, False, c_in, c_out, c_scr)

    grid_spec = pltpu.PrefetchScalarGridSpec(
        num_scalar_prefetch=1, grid=(2, nK),
        in_specs=[pl.BlockSpec((tk, wide), functools.partial(
            lambda i, k, core, j: (k, (2 * j if blocks > 1 else 0) + jnp.where(i == 0, 1 - core[0], core[0])), j=j))
            for j in range(blocks)] + [pl.BlockSpec((tk, N), lambda i, k, core: (k, 0))] + [ANY] * plumb.n_in,
        out_specs=[pl.BlockSpec((half, N), lambda i, k, core: (0, 0))] + [ANY] * plumb.n_out,
        scratch_shapes=[pltpu.VMEM((half, N), F32), pltpu.VMEM((half, N), BF), pltpu.VMEM((half, N), BF),
                        pltpu.SemaphoreType.DMA, pltpu.SemaphoreType.DMA] + plumb.scratch)
    core = lax.axis_index("c").astype(jnp.int32).reshape(1)
    res = pl.pallas_call(
        body, name=name, grid_spec=grid_spec,
        out_shape=[jax.ShapeDtypeStruct((half, N), BF)] + plumb.out_shapes,
        compiler_params=_params(("arbitrary", "arbitrary")),
    )(core, *([a] * blocks), b, *plumb.args)
    return (res[0], plumb.split_outputs(res[1:])) if comm else res[0]


def _pair_task(parts):
    n = len(parts)

    def copies(ins, outs, scr):
        x, y, c, _ = _place()
        return [pltpu.make_async_remote_copy(
            src_ref=ins[k].at[:, pl.ds(1 - c, 1)], dst_ref=outs[k], send_sem=scr[0].at[k], recv_sem=scr[1].at[k],
            device_id=(x, y, 1 - c), device_id_type=MESH) for k in range(n)]

    def start(ins, outs, scr):
        for cp in copies(ins, outs, scr):
            cp.start()

    def finish(ins, outs, scr):
        for cp in copies(ins, outs, scr):
            cp.wait()

    out_shapes = [jax.ShapeDtypeStruct((4, 1) + p.shape[2:], p.dtype) for p in parts]
    scratch = [pltpu.SemaphoreType.DMA((n,)), pltpu.SemaphoreType.DMA((n,))]
    return _Task(parts, out_shapes, scratch, [(0, start), (1.0, finish)])


def _pair_sum(name, part, got, core):
    _, _, r, C = part.shape

    def body(core_ref, p_ref, g_ref, o_ref):
        o_ref[0] = (p_ref[0, 0].astype(F32) + g_ref[0, 0].astype(F32)).astype(o_ref.dtype)

    return pl.pallas_call(
        body, name=name,
        grid_spec=pltpu.PrefetchScalarGridSpec(
            num_scalar_prefetch=1, grid=(4,),
            in_specs=[pl.BlockSpec((1, 1, r, C), lambda i, core_ref: (i, core_ref[0], 0, 0)),
                      pl.BlockSpec((1, 1, r, C), lambda i, core_ref: (i, 0, 0, 0))],
            out_specs=pl.BlockSpec((1, r, C), lambda i, core_ref: (i, 0, 0))),
        out_shape=jax.ShapeDtypeStruct((4, r, C), part.dtype), compiler_params=_params(("parallel",)),
    )(core, part, got)


def _ffn_bwd(tag, dy, dyb, x, gain, wgT, wuT, wd, saved, earlier=None):
    n, g, u, a = saved
    half = lambda accs, ex: _swiglu_bwd_epilogue([0.5 * accs[0]], ex)
    act_args = dict(tm=512, tn=1408, tk=D_MODEL, epilogue=half, extras=[(g, "tile", 0), (u, "tile", 0)], cols_outer=True)
    if earlier is None:
        sum_d = _dw_pair(tag + "_dw_down", a, dyb, 0.5)
        (dg, du), ((slots_d,),) = _mm(tag + "_d_act", [(dyb, wd, "nt", 0)], [BF, BF], comm=[_chip_task([sum_d])], **act_args)
        slots_e = None
        sum_g = _dw_pair(tag + "_dw_gate", dg, n, 1.0)
    else:
        sum_d, ((got,),) = _dw_pair(tag + "_dw_down", a, dyb, 0.5, comm=[_pair_task([earlier])])
        core = lax.axis_index("c").astype(jnp.int32).reshape(1)
        sum_e = _pair_sum(tag + "_pair_sum_earlier", earlier, got, core)
        sum_e = sum_e.reshape(4 * sum_e.shape[1], sum_e.shape[2])
        (dg, du), ((slots_e,),) = _mm(tag + "_d_act", [(dyb, wd, "nt", 0)], [BF, BF], comm=[_chip_task([sum_e])], **act_args)
        sum_g, ((slots_d,),) = _dw_pair(tag + "_dw_gate", dg, n, 1.0, comm=[_chip_task([sum_d])])
    sum_u, ((slots_g,),) = _dw_pair(tag + "_dw_up", du, n, 1.0, comm=[_chip_task([sum_g])])
    (dx, dxb, dgain), ((slots_u,),) = _mm(
        tag + "_d_norm", [(dg, wgT, "nn", 0), (du, wuT, "nn", 0)], [F32, BF], tm=512, tn=D_MODEL, tk=D_FF,
        epilogue=_rms_bwd_epilogue, extras=[(x, "tile", 0), (gain, "row", 0), (dy, "tile", 0)], n_colsum=1,
        comm=[_chip_task([sum_u])])
    return dx, dxb, dgain, slots_e, slots_g, slots_u, slots_d


def _tile_gain(g):
    return jnp.concatenate([g, g]).reshape(1, LANES)


def _fold_heads(partials):
    return jnp.sum(partials.reshape(-1, HEAD_DIM), axis=0)


def _pack_small_grads(grads, loss_local):
    pieces, row = [], 0
    for name, r0, _ in SMALL_LAYOUT + (("loss", LOSS_ROW, None),):
        v = (loss_local if name == "loss" else grads[name]).reshape(-1)
        rows = -(-v.size // LANES)
        block = jnp.pad(v, (0, rows * LANES - v.size)).reshape(rows, LANES)
        pieces += [jnp.zeros((r0 - row, LANES), F32)] * (r0 > row) + [block]
        row = r0 + rows
    pieces.append(jnp.zeros((SMALL_ROWS - row, LANES), F32))
    return jnp.concatenate(pieces, axis=0)


def kernel(x, ffn1_norm, ffn1_w_gate, ffn1_w_up, ffn1_w_down, mix_norm, w_in, pool_w, pool_scale, w_pool_out, q_norm, k_norm, sinks, w_attn_out, gate_bias, w_out, ffn2_norm, ffn2_w_gate, ffn2_w_up, ffn2_w_down, loss_target, m_ffn1_norm, m_ffn1_w_gate, m_ffn1_w_up, m_ffn1_w_down, m_mix_norm, m_w_in, m_pool_w, m_pool_scale, m_w_pool_out, m_q_norm, m_k_norm, m_sinks, m_w_attn_out, m_gate_bias, m_w_out, m_ffn2_norm, m_ffn2_w_gate, m_ffn2_w_up, m_ffn2_w_down, v_ffn1_norm, v_ffn1_w_gate, v_ffn1_w_up, v_ffn1_w_down, v_mix_norm, v_w_in, v_pool_w, v_pool_scale, v_w_pool_out, v_q_norm, v_k_norm, v_sinks, v_w_attn_out, v_gate_bias, v_w_out, v_ffn2_norm, v_ffn2_w_gate, v_ffn2_w_up, v_ffn2_w_down):
    T = x.shape[1]
    x2 = x.reshape(T, D_MODEL)
    target = loss_target.reshape(T, D_MODEL)

    big = [
        ("ffn1_w_gate", ffn1_w_gate, m_ffn1_w_gate, v_ffn1_w_gate, True, False),
        ("ffn1_w_up", ffn1_w_up, m_ffn1_w_up, v_ffn1_w_up, True, False),
        ("ffn1_w_down", ffn1_w_down, m_ffn1_w_down, v_ffn1_w_down, False, False),
        ("w_in", w_in, m_w_in, v_w_in, True, False),
        ("w_pool_out", w_pool_out, m_w_pool_out, v_w_pool_out, False, True),
        ("w_attn_out", w_attn_out, m_w_attn_out, v_w_attn_out, False, False),
        ("w_out", w_out, m_w_out, v_w_out, False, False),
        ("ffn2_w_gate", ffn2_w_gate, m_ffn2_w_gate, v_ffn2_w_gate, True, False),
        ("ffn2_w_up", ffn2_w_up, m_ffn2_w_up, v_ffn2_w_up, True, False),
        ("ffn2_w_down", ffn2_w_down, m_ffn2_w_down, v_ffn2_w_down, False, False),
    ]
    view = lambda a, tv: a.T if tv else a
    shards = _prep("prep_weights", [view(w, tv) for _, w, _, _, tv, _ in big], [tk_ for *_, tk_ in big])
    g1 =ffn1_norm.reshape(1, D_MODEL)
    g2 = mix_norm.reshape(1, D_MODEL)
    g3 = ffn2_norm.reshape(1, D_MODEL)
    bias_row = gate_bias.reshape(1, 2 * D_MODEL)
    qg, kg = _tile_gain(q_norm) * ATTN_SCALE, _tile_gain(k_norm)
    scale_row = pool_scale.reshape(1, POOL_WIDTH)

    n1, ((wg1T, wu1T),) = _rms_fwd("ffn1_norm", x2, g1, [_gather_task(shards[0:2], forward_at=0.9)])
    (gt1, up1, act1), ((wd1,), (w_inT,)) = _mm(
        "ffn1_gate_up", [(n1, wg1T, "nt", 0), (n1, wu1T, "nt", 1)], [BF, BF, BF], tm=512, tn=1408, tk=D_MODEL,
        epilogue=_swiglu_fwd_epilogue, cols_outer=True,
        comm=[_gather_task(shards[2:3], forward_at=0.5), _gather_task(shards[3:4], natural=(0,), forward_at=0.9)])
    (h1, u), ((w_poT, w_ao, w_o),) = _mm(
        "ffn1_down", [(act1, wd1, "nn", 0)], [F32, BF], tm=512, tn=D_MODEL, tk=D_FF,
        epilogue=_residual_norm_epilogue(0.5), extras=[(x2, "tile", 0), (g2, "row", 0)],
        comm=[_gather_task(shards[4:7], natural=(0, 1, 2), forward_at=0.8)])
    saved1 = (n1, gt1, up1, act1)
    (proj,), ((wg2T,),) = _mm(
        "in_proj", [(u, w_inT, "nt", 0)], [BF], tm=512, tn=1280, tk=D_MODEL, cols_outer=True,
        comm=[_gather_task(shards[7:8], forward_at=0.8)])
    pooled, mixed = _pool_fwd("pool_fwd", proj, pool_w, scale_row)
    qn = _headnorm_fwd("q_norm", proj, COL_Q, ATTN_WIDTH, qg)
    kn = _headnorm_fwd("k_norm", proj, COL_K, KV_WIDTH, kg)
    attn, ((wu2T,),) = _attn_fwd("attn_fwd", qn, kn, proj, sinks, comm=[_gather_task(shards[8:9], forward_at=0.8)])
    (bp,) = _mm("pool_out", [(mixed, w_poT, "nt", 0)], [BF], tm=1024, tn=D_MODEL, tk=POOL_WIDTH)
    gate_tn = 256
    gate_extras = [(proj, "tile", COL_GP // gate_tn), (proj, "tile", COL_GA // gate_tn),
                   (bias_row, "row", 0), (bias_row, "row", D_MODEL // gate_tn)]
    merged, ba = _mm("attn_out_merge", [(attn, w_ao, "nn", 0)], [BF, BF], tm=2048, tn=gate_tn, tk=ATTN_WIDTH,
                     epilogue=_merge_fwd_epilogue, extras=[(bp, "tile", 0)] + gate_extras)
    h2, n2 = _mm("mix_out", [(merged, w_o, "nn", 0)], [F32, BF], tm=512, tn=D_MODEL, tk=D_MODEL,
                 epilogue=_residual_norm_epilogue(1.0), extras=[(h1, "tile", 0), (g3, "row", 0)])
    (gt2, up2, act2), ((wd2,),) = _mm(
        "ffn2_gate_up", [(n2, wg2T, "nt", 0), (n2, wu2T, "nt", 1)], [BF, BF, BF], tm=512, tn=1408, tk=D_MODEL,
        epilogue=_swiglu_fwd_epilogue, cols_outer=True, comm=[_gather_task(shards[9:10], forward_at=0.8)])
    dy, dyb, sq = _mm("ffn2_down_loss", [(act2, wd2, "nn", 0)], [F32, BF], tm=512, tn=D_MODEL, tk=D_FF,
                      epilogue=_loss_epilogue, extras=[(h2, "tile", 0), (target, "tile", 0)], n_colsum=1)
    loss_local = 0.5 * jnp.sum(sq) / D_MODEL

    dh2, dh2b, dg3, _, slots_g2, slots_u2, slots_d2 = _ffn_bwd(
        "ffn2", dy, dyb, h2, g3, wg2T, wu2T, wd2, (n2, gt2, up2, act2))
    dbp, dba, dgp, dga, cs_gp, cs_ga = _mm(
        "mix_out_bwd", [(dh2b, w_o, "nt", 0)], [BF, BF, BF, BF], tm=2048, tn=gate_tn, tk=D_MODEL,
        epilogue=_merge_bwd_epilogue, extras=[(bp, "tile", 0), (ba, "tile", 0)] + gate_extras, n_colsum=2)
    sum_o = _dw_pair("dw_out", merged, dh2b, 1.0, blocks=4)
    (dmixed,) = _mm("pool_out_bwd", [(dbp, w_poT, "nn", 0)], [BF], tm=1024, tn=POOL_WIDTH, tk=D_MODEL)
    sum_po = _dw_pair("dw_pool_out", dbp, mixed, 1.0, blocks=4)
    (dattn,) = _mm("attn_out_bwd", [(dba, w_ao, "nt", 0)], [BF], tm=1024, tn=ATTN_WIDTH, tk=D_MODEL)
    sum_ao = _dw_pair("dw_attn_out", attn, dba, 1.0, blocks=4)
    dxp, dpool_w, dpool_scale = _pool_bwd("pool_bwd", dmixed, pooled, pool_w, scale_row)
    (dqn, dkn, dv, dsink_tile), ((slots_o, slots_po, slots_ao),) = _attn_bwd(
        "attn_bwd", dattn, qn, kn, proj, sinks, [_chip_task([sum_o, sum_po, sum_ao])])
    dq, dqg = _headnorm_bwd("q_norm_bwd", dqn, proj, COL_Q, ATTN_WIDTH, qg)
    dk, dkg = _headnorm_bwd("k_norm_bwd", dkn, proj, COL_K, KV_WIDTH, kg)
    dproj = jnp.concatenate([dxp, dq, dk, dv, dgp, dga], axis=1)
    dh1, dh1b, dg2 = _mm(
        "in_proj_bwd", [(dproj, w_inT, "nn", 0)], [F32, BF], tm=512, tn=D_MODEL, tk=IN_WIDTH, epilogue=_rms_bwd_epilogue,
        extras=[(h1, "tile", 0), (g2, "row", 0), (dh2, "tile", 0)], n_colsum=1)
    (dw_inT,) = _mm("dw_in", [(dproj, u, "tn", 0)], [BF], tm=1280, tn=D_MODEL, tk=2048)
    dx, _, dg1, slots_in, slots_g1, slots_u1, slots_d1 = _ffn_bwd(
        "ffn1", dh1, dh1b, x2, g1, wg1T, wu1T, wd1, saved1, dw_inT.reshape(4, 2, IN_WIDTH // N_DEV, D_MODEL))

    slots = [slots_g1, slots_u1, slots_d1, slots_in, slots_po, slots_ao, slots_o, slots_g2, slots_u2, slots_d2]
    big_out = {}
    for label, group in (("ffn", (0, 1, 2, 7, 8, 9)), ("w_in", (3,)), ("w_pool_out", (4,)), ("attn_out_and_out", (5, 6))):
        items = [(slots[k], view(big[k][1], big[k][4]), view(big[k][2], big[k][4]), view(big[k][3], big[k][4]))
                 for k in group]
        for k, res in zip(group, _adamw_sharded("adamw_" + label, items, transpose=big[group[0]][5])):
            big_out[big[k][0]] = tuple(view(r, big[k][4]) for r in res)

    small_grads = {
        "ffn1_norm": jnp.sum(dg1, axis=(0, 1)), "mix_norm": jnp.sum(dg2, axis=(0, 1)), "ffn2_norm": jnp.sum(dg3, axis=(0, 1)),
        "gate_bias": jnp.concatenate([jnp.sum(cs_gp, axis=(0, 1)), jnp.sum(cs_ga, axis=(0, 1))]),
        "pool_scale": dpool_scale, "q_norm": _fold_heads(dqg) * ATTN_SCALE, "k_norm": _fold_heads(dkg),
        "sinks": dsink_tile[0, :N_HEADS]}
    g_vec, g_pool_w = _all_gather("gather_small_grads", [_pack_small_grads(small_grads, loss_local),
                                                         dpool_w.reshape(-1, LANES)])
    given = {"ffn1_norm": (ffn1_norm, m_ffn1_norm, v_ffn1_norm), "mix_norm": (mix_norm, m_mix_norm, v_mix_norm),
             "ffn2_norm": (ffn2_norm, m_ffn2_norm, v_ffn2_norm), "gate_bias": (gate_bias, m_gate_bias, v_gate_bias),
             "pool_scale": (pool_scale, m_pool_scale, v_pool_scale), "q_norm": (q_norm, m_q_norm, v_q_norm),
             "k_norm": (k_norm, m_k_norm, v_k_norm), "sinks": (sinks, m_sinks, v_sinks)}
    params = [tuple(a.reshape(shape) for a in given[nm]) for nm, _, shape in SMALL_LAYOUT]
    params.append(tuple(a.reshape(-1, LANES) for a in (pool_w, m_pool_w, v_pool_w)))
    small_res, loss_row = _adamw_small("adamw_small", g_vec.reshape(N_DEV, SMALL_ROWS, LANES),
                                       g_pool_w.reshape(N_DEV, -1, LANES), params)
    small_out = {nm: tuple(r.reshape(given[nm][0].shape) for r in res)
                 for (nm, _, _), res in zip(SMALL_LAYOUT, small_res)}
    small_out["pool_w"] = tuple(r.reshape(pool_w.shape) for r in small_res[-1])
    loss = loss_row[0, 0]

    order = ["ffn1_norm", "ffn1_w_gate", "ffn1_w_up", "ffn1_w_down", "mix_norm", "w_in", "pool_w", "pool_scale",
             "w_pool_out", "q_norm", "k_norm", "sinks", "w_attn_out", "gate_bias", "w_out", "ffn2_norm",
             "ffn2_w_gate", "ffn2_w_up", "ffn2_w_down"]
    every = {**big_out, **small_out}
    outs = [loss, dx.reshape(x.shape)]
    for j in range(4):
        outs += [every[nm][j] for nm in order]
    return tuple(outs)
```

```python
import functools

import jax
import jax.numpy as jnp
from jax import lax
from jax.experimental import pallas as pl
from jax.experimental.pallas import tpu as pltpu

BF = jnp.bfloat16
F32 = jnp.float32

D_MODEL = 1024
D_FF = 2816
POOL_WIDTH = 512
POOL_GROUP = 128
N_POOL_GROUPS = 4
HEAD_DIM = 64
N_HEADS = 16
GQA_GROUP = 8
BLOCK = 128
ATTN_WIDTH = 1024
KV_WIDTH = 128
IN_WIDTH = 3840
RMS_EPS = 1e-6
N_DEV = 8
LANES = 128

COL_Q = POOL_WIDTH
COL_K = COL_Q + ATTN_WIDTH
COL_V = COL_K + KV_WIDTH
COL_GP = COL_V + KV_WIDTH
COL_GA = COL_GP + D_MODEL

ADAM_LR = 0.001
ADAM_B1 = 0.9
ADAM_B2 = 0.999
ADAM_EPS = 1e-08
ADAM_WD = 0.01
ADAM_STEP = 10

VMEM_LIMIT_V7X = 56 * 1024 * 1024
MESH = pl.DeviceIdType.MESH
ANY = pl.BlockSpec(memory_space=pl.ANY)


def _params(sem=None):
    return pltpu.CompilerParams(dimension_semantics=sem, vmem_limit_bytes=VMEM_LIMIT_V7X)


_DIMS = {"nt": (((1,), (1,)), ((), ())), "nn": (((1,), (0,)), ((), ())), "tn": (((0,), (0,)), ((), ()))}


class _Task:
    def __init__(self, inputs, out_shapes, scratch, phases):
        self.inputs, self.out_shapes, self.scratch = list(inputs), list(out_shapes), list(scratch)
        self.phases = list(phases)


class _CommPlumbing:
    def __init__(self, tasks):
        self.tasks = list(tasks or [])
        self.args = [a for t in self.tasks for a in t.inputs]
        self.out_shapes = [o for t in self.tasks for o in t.out_shapes]
        self.scratch = [s for t in self.tasks for s in t.scratch]
        self.n_in, self.n_out = len(self.args), len(self.out_shapes)

    def _slices(self, c_in, c_out, c_scr):
        i = o = s = 0
        for t in self.tasks:
            yield t, c_in[i:i + len(t.inputs)], c_out[o:o + len(t.out_shapes)], c_scr[s:s + len(t.scratch)]
            i, o, s = i + len(t.inputs), o + len(t.out_shapes), s + len(t.scratch)

    def run(self, step, steps, before, c_in, c_out, c_scr):
        for t, ins, outs, scr in self._slices(c_in, c_out, c_scr):
            for frac, fn in t.phases:
                if step is None:
                    fn(ins, outs, scr)
                elif before == (frac == 0):
                    at = 0 if frac == 0 else max(0, min(steps, -(-int(round(frac * steps * 64)) // 64)) - 1)
                    pl.when(step == at)(functools.partial(fn, ins, outs, scr))

    def split_outputs(self, flat):
        res, o = [], 0
        for t in self.tasks:
            res.append(list(flat[o:o + len(t.out_shapes)]))
            o += len(t.out_shapes)
        return res


def _comm_only(name, tasks):
    plumb = _CommPlumbing(tasks)

    def body(*refs):
        c_in, c_out = refs[:plumb.n_in], refs[plumb.n_in: plumb.n_in + plumb.n_out]
        c_scr = refs[plumb.n_in + plumb.n_out:]
        plumb.run(None, 1, True, c_in, c_out, c_scr)

    res = pl.pallas_call(
        body, name=name, in_specs=[ANY] * plumb.n_in, out_specs=[ANY] * plumb.n_out, out_shape=plumb.out_shapes,
        scratch_shapes=plumb.scratch, compiler_params=pltpu.CompilerParams(has_side_effects=True),
    )(*plumb.args)
    return plumb.split_outputs(res)


def _mm(name, terms, out_dtypes, *, tm, tn, tk, epilogue=None, extras=(), n_colsum=0, comm=None, cols_outer=False):
    a0, b0, mode0, _ = terms[0]
    if mode0 == "nt":
        (M, K), N = a0.shape, b0.shape[0]
    elif mode0 == "nn":
        (M, K), N = a0.shape, b0.shape[1]
    else:
        (K, M), N = a0.shape, b0.shape[1]
    tm, tn, tk = min(tm, M), min(tn, N), min(tk, K)
    assert M % tm == 0 and N % tn == 0 and K % tk == 0, (name, M, N, K, tm, tn, tk)
    nI, nJ, nK = M // tm, N // tn, K // tk
    n_terms = len(terms)
    n_acc = max(t[3] for t in terms) + 1
    n_ex = len(extras)
    n_out = len(out_dtypes)
    if epilogue is None:
        epilogue = lambda accs, ex: ([accs[0]], [])
    plumb = _CommPlumbing(comm)
    n_scr = n_acc if nK > 1 else 0
    grid = (nJ, nI, nK) if cols_outer else (nI, nJ, nK)

    def body(*refs):
        n_in = 2 * n_terms + n_ex
        ab = refs[: 2 * n_terms]
        ex_refs = refs[2 * n_terms: n_in]
        c_in = refs[n_in: n_in + plumb.n_in]
        o0 = n_in + plumb.n_in
        out_refs = refs[o0: o0 + n_out]
        cs_refs = refs[o0 + n_out: o0 + n_out + n_colsum]
        c_out = refs[o0 + n_out + n_colsum: o0 + n_out + n_colsum + plumb.n_out]
        s0 = o0 + n_out + n_colsum + plumb.n_out
        acc_refs = refs[s0: s0 + n_scr]
        c_scr = refs[s0 + n_scr:]
        steps = grid[0] * grid[1] * nK
        if comm:
            step = (pl.program_id(0) * grid[1] + pl.program_id(1)) * nK + pl.program_id(2)
            plumb.run(step, steps, True, c_in, c_out, c_scr)

        def products():
            accs = [None] * n_acc
            for t, (_, _, mode, ai) in enumerate(terms):
                p = lax.dot_general(ab[2 * t][...], ab[2 * t + 1][...], _DIMS[mode], preferred_element_type=F32)
                accs[ai] = p if accs[ai] is None else accs[ai] + p
            return accs

        def finish(accs):
            outs, colsums = epilogue(accs, [r[...] for r in ex_refs])
            for r, o in zip(out_refs, outs):
                r[...] = o.astype(r.dtype)
            for r, cs in zip(cs_refs, colsums):
                r[...] = jnp.sum(cs, axis=0, keepdims=True).reshape(r.shape)

        if nK == 1:
            finish(products())
        else:
            k = pl.program_id(2)
            accs = products()

            @pl.when(k == 0)
            def _():
                for r, a in zip(acc_refs, accs):
                    r[...] = a

            @pl.when(k > 0)
            def _():
                for r, a in zip(acc_refs, accs):
                    r[...] += a

            @pl.when(k == nK - 1)
            def _():
                finish([r[...] for r in acc_refs])

        if comm:
            plumb.run(step, steps, False, c_in, c_out, c_scr)

    def spec(block, index, fixed=False):
        imap = (lambda q, p, k: index(p, q, k)) if cols_outer else index
        return pl.BlockSpec(block, imap, pipeline_mode=pl.Buffered(1)) if fixed else pl.BlockSpec(block, imap)

    in_specs, args = [], []
    for a, b, mode, _ in terms:
        if mode == "nt":
            in_specs += [spec((tm, tk), lambda i, j, k: (i, k), nI * nK == 1),
                         spec((tn, tk), lambda i, j, k: (j, k), nJ * nK == 1)]
        elif mode == "nn":
            in_specs += [spec((tm, tk), lambda i, j, k: (i, k), nI * nK == 1),
                         spec((tk, tn), lambda i, j, k: (k, j), nJ * nK == 1)]
        else:
            in_specs += [spec((tk, tm), lambda i, j, k: (k, i), nI * nK == 1),
                         spec((tk, tn), lambda i, j, k: (k, j), nJ * nK == 1)]
        args += [a, b]
    for arr, kind, off in extras:
        if kind == "tile":
            in_specs.append(spec((tm, tn), functools.partial(lambda i, j, k, off: (i, j + off), off=off)))
        else:
            in_specs.append(spec((1, tn), functools.partial(lambda i, j, k, off: (0, j + off), off=off)))
        args.append(arr)
    out_shape = [jax.ShapeDtypeStruct((M, N), dt) for dt in out_dtypes]
    out_specs = [spec((tm, tn), lambda i, j, k: (i, j)) for _ in out_dtypes]
    out_shape += [jax.ShapeDtypeStruct((nI, 1, N), F32) for _ in range(n_colsum)]
    out_specs += [spec((1, 1, tn), lambda i, j, k: (i, 0, j)) for _ in range(n_colsum)]
    scratch = [pltpu.VMEM((tm, tn), F32) for _ in range(n_scr)]
    args += plumb.args
    in_specs += [ANY] * plumb.n_in
    out_shape += plumb.out_shapes
    out_specs += [ANY] * plumb.n_out
    sem = ("arbitrary",) * 3 if comm else ("parallel", "parallel", "arbitrary")
    res = pl.pallas_call(
        body, name=name, grid=grid, in_specs=in_specs, out_specs=out_specs, out_shape=out_shape,
        scratch_shapes=scratch + plumb.scratch, compiler_params=_params(sem),
    )(*args)
    n_own = n_out + n_colsum
    return (list(res[:n_own]), plumb.split_outputs(res[n_own:])) if comm is not None else res


ROW_TILE = 512


def _rms_fwd(name, x, g, comm):
    T, D = x.shape
    steps = T // ROW_TILE
    plumb = _CommPlumbing(comm)

    def body(x_ref, g_ref, *rest):
        c_in, o_ref = rest[:plumb.n_in], rest[plumb.n_in]
        c_out, c_scr = rest[plumb.n_in + 1: plumb.n_in + 1 + plumb.n_out], rest[plumb.n_in + 1 + plumb.n_out:]
        plumb.run(pl.program_id(0), steps, True, c_in, c_out, c_scr)
        xv = x_ref[...]
        r = lax.rsqrt(jnp.mean(xv * xv, axis=-1, keepdims=True) + RMS_EPS)
        o_ref[...] = (xv * r * g_ref[...]).astype(BF)
        plumb.run(pl.program_id(0), steps, False, c_in, c_out, c_scr)

    row = pl.BlockSpec((ROW_TILE, D), lambda i: (i, 0))
    res = pl.pallas_call(
        body, name=name, grid=(steps,),
        in_specs=[row, pl.BlockSpec((1, D), lambda i: (0, 0))] + [ANY] * plumb.n_in,
        out_specs=[row] + [ANY] * plumb.n_out, out_shape=[jax.ShapeDtypeStruct((T, D), BF)] + plumb.out_shapes,
        scratch_shapes=plumb.scratch, compiler_params=_params(("arbitrary",)),
    )(x, g, *plumb.args)
    return res[0], plumb.split_outputs(res[1:])


HEADNORM_TILE = 1024


def _half_sum_matrix():
    r = lax.broadcasted_iota(jnp.int32, (LANES, LANES), 0) // HEAD_DIM
    c = lax.broadcasted_iota(jnp.int32, (LANES, LANES), 1) // HEAD_DIM
    return (r == c).astype(BF)


def _head_mean(v, ones_blockdiag):
    hi = v.astype(BF)
    lo = (v - hi.astype(F32)).astype(BF)
    s = jnp.dot(hi, ones_blockdiag, preferred_element_type=F32) + jnp.dot(lo, ones_blockdiag, preferred_element_type=F32)
    return s * (1.0 / HEAD_DIM)


def _headnorm_fwd(name, proj, col0, width, g2):
    T = proj.shape[0]
    wide = min(width, GROUP_WIDTH)
    nb, off = width // wide, col0 // wide

    def body(x_ref, g_ref, b_ref, o_ref):
        for s in range(wide // LANES):
            lanes = slice(LANES * s, LANES * (s + 1))
            xv = x_ref[:, lanes].astype(F32)
            r = lax.rsqrt(_head_mean(xv * xv, b_ref[...]) + RMS_EPS)
            o_ref[:, lanes] = (xv * r * g_ref[...]).astype(BF)

    return pl.pallas_call(
        body, name=name, grid=(T // HEADNORM_TILE, nb),
        in_specs=[pl.BlockSpec((HEADNORM_TILE, wide), lambda i, j: (i, j + off)),
                  pl.BlockSpec((1, LANES), lambda i, j: (0, 0)), pl.BlockSpec((LANES, LANES), lambda i, j: (0, 0))],
        out_specs=pl.BlockSpec((HEADNORM_TILE, wide), lambda i, j: (i, j)),
        out_shape=jax.ShapeDtypeStruct((T, width), BF), compiler_params=_params(("parallel", "parallel")),
    )(proj, g2, _half_sum_matrix())


def _headnorm_bwd(name, dy, proj, col0, width, g2):
    T = proj.shape[0]
    wide = min(width, GROUP_WIDTH)
    nb, off = width // wide, col0 // wide

    def body(dy_ref, x_ref, g_ref, b_ref, dx_ref, dg_ref):
        for s in range(wide // LANES):
            lanes = slice(LANES * s, LANES * (s + 1))
            xv = x_ref[:, lanes].astype(F32)
            dyv = dy_ref[:, lanes].astype(F32)
            r = lax.rsqrt(_head_mean(xv * xv, b_ref[...]) + RMS_EPS)
            xhat = xv * r
            dxhat = dyv * g_ref[...]
            dx_ref[:, lanes] = (r * (dxhat - xhat * _head_mean(dxhat * xhat, b_ref[...]))).astype(BF)
            dg_ref[0, :, lanes] = jnp.sum(dyv * xhat, axis=0, keepdims=True)

    return pl.pallas_call(
        body, name=name, grid=(T // HEADNORM_TILE, nb),
        in_specs=[pl.BlockSpec((HEADNORM_TILE, wide), lambda i, j: (i, j)),
                  pl.BlockSpec((HEADNORM_TILE, wide), lambda i, j: (i, j + off)),
                  pl.BlockSpec((1, LANES), lambda i, j: (0, 0)), pl.BlockSpec((LANES, LANES), lambda i, j: (0, 0))],
        out_specs=[pl.BlockSpec((HEADNORM_TILE, wide), lambda i, j: (i, j)),
                   pl.BlockSpec((1, 1, wide), lambda i, j: (i, 0, j))],
        out_shape=[jax.ShapeDtypeStruct((T, width), BF), jax.ShapeDtypeStruct((T // HEADNORM_TILE, 1, width), F32)],
        compiler_params=_params(("parallel", "parallel")),
    )(dy, proj, g2, _half_sum_matrix())


def _shift_down(v, k, row):
    return jnp.where(row >= k, pltpu.roll(v, k, axis=0), 0.0)


def _shift_up(v, k, row, T):
    return jnp.where(row < T - k, pltpu.roll(v, T - k, axis=0), 0.0)


def _by_group(g, vals):
    out = vals[-1]
    for i in range(len(vals) - 2, -1, -1):
        out = jnp.where(g == i, vals[i], out)
    return out


def _pool_fwd(name, proj, pool_w, pool_scale):
    T = proj.shape[0]

    def body(x_ref, w_ref, s_ref, pooled_ref, mixed_ref):
        g = pl.program_id(0)
        xv = x_ref[...].astype(F32)
        row = lax.broadcasted_iota(jnp.int32, (T, 1), 0)
        s2 = xv + _shift_down(xv, 1, row)
        s4 = s2 + _shift_down(s2, 2, row)
        s8 = s4 + _shift_down(s4, 4, row)
        s16 = s8 + _shift_down(s8, 8, row)
        wsum = _by_group(g, [s2, s4, s8, s16])
        count = jnp.minimum(row + 1, 2 << g).astype(F32)
        pooled = (wsum / count - xv).astype(BF)
        pooled_ref[...] = pooled
        mixed = jnp.dot(pooled, w_ref[0].astype(BF), preferred_element_type=F32) * s_ref[...]
        mixed_ref[...] = mixed.astype(BF)

    col = pl.BlockSpec((T, POOL_GROUP), lambda g: (0, g))
    return pl.pallas_call(
        body, name=name, grid=(N_POOL_GROUPS,),
        in_specs=[col, pl.BlockSpec((1, POOL_GROUP, POOL_GROUP), lambda g: (g, 0, 0)),
                  pl.BlockSpec((1, POOL_GROUP), lambda g: (0, g))],
        out_specs=[col, col],
        out_shape=[jax.ShapeDtypeStruct((T, POOL_WIDTH), BF), jax.ShapeDtypeStruct((T, POOL_WIDTH), BF)],
        compiler_params=_params(("parallel",)),
    )(proj, pool_w, pool_scale)


def _pool_bwd(name, dmixed, pooled, pool_w, pool_scale):
    T = dmixed.shape[0]

    def body(dm_ref, p_ref, w_ref, s_ref, dx_ref, dw_ref, ds_ref):
        g = pl.program_id(0)
        dm = dm_ref[...].astype(F32)
        pooled = p_ref[...]
        w = w_ref[0].astype(BF)
        pre = jnp.dot(pooled, w, preferred_element_type=F32)
        ds_ref[...] = jnp.sum(dm * pre, axis=0, keepdims=True)
        dms = (dm * s_ref[...]).astype(BF)
        dw_ref[0] = lax.dot_general(pooled, dms, _DIMS["tn"], preferred_element_type=F32)
        dpooled = lax.dot_general(dms, w, _DIMS["nt"], preferred_element_type=F32)
        row = lax.broadcasted_iota(jnp.int32, (T, 1), 0)
        count = jnp.minimum(row + 1, 2 << g).astype(F32)
        z = dpooled / count
        l2 = z + _shift_up(z, 1, row, T)
        l4 = l2 + _shift_up(l2, 2, row, T)
        l8 = l4 + _shift_up(l4, 4, row, T)
        l16 = l8 + _shift_up(l8, 8, row, T)
        dx_ref[...] = (_by_group(g, [l2, l4, l8, l16]) - dpooled).astype(BF)

    col = pl.BlockSpec((T, POOL_GROUP), lambda g: (0, g))
    wspec = pl.BlockSpec((1, POOL_GROUP, POOL_GROUP), lambda g: (g, 0, 0))
    sspec = pl.BlockSpec((1, POOL_GROUP), lambda g: (0, g))
    return pl.pallas_call(
        body, name=name, grid=(N_POOL_GROUPS,), in_specs=[col, col, wspec, sspec], out_specs=[col, wspec, sspec],
        out_shape=[jax.ShapeDtypeStruct((T, POOL_WIDTH), BF),
                   jax.ShapeDtypeStruct((N_POOL_GROUPS, POOL_GROUP, POOL_GROUP), F32),
                   jax.ShapeDtypeStruct((1, POOL_WIDTH), F32)],
        compiler_params=_params(("parallel",)),
    )(dmixed, pooled, pool_w, pool_scale)


ATTN_SCALE = HEAD_DIM ** -0.5
MASKED = float(jnp.finfo(jnp.float32).min)
KV_COL_BLOCK_K = COL_K // LANES
KV_COL_BLOCK_V = COL_V // LANES
GROUP_WIDTH = GQA_GROUP * HEAD_DIM


def _dup_head(v, j):
    half = lax.broadcasted_iota(jnp.int32, (1, LANES), 1) // HEAD_DIM
    return jnp.where(half == j, v, pltpu.roll(v, HEAD_DIM, axis=1))


def _stack_heads(v, low):
    pieces = []
    for p in range(GROUP_WIDTH // LANES):
        vp = v[:, LANES * p: LANES * (p + 1)]
        pieces.append(jnp.where(low, vp, jnp.zeros_like(vp)))
        pieces.append(jnp.where(low, jnp.zeros_like(vp), vp))
    return jnp.concatenate(pieces, axis=0)


def _unstack_transposed(t, low):
    pairs = []
    for p in range(GROUP_WIDTH // LANES):
        even = t[:, BLOCK * (2 * p): BLOCK * (2 * p + 1)].T
        odd = t[:, BLOCK * (2 * p + 1): BLOCK * (2 * p + 2)].T
        pairs.append(jnp.where(low, even, odd))
    return pairs


STACKED = GQA_GROUP * BLOCK


def _band_bias():
    key = lax.broadcasted_iota(jnp.int32, (2, 2 * BLOCK, STACKED), 1)
    qry = lax.broadcasted_iota(jnp.int32, (2, 2 * BLOCK, STACKED), 2) % BLOCK
    first = lax.broadcasted_iota(jnp.int32, (2, 2 * BLOCK, STACKED), 0) == 0
    valid = (key > qry) & (key <= qry + BLOCK) & (jnp.logical_not(first) | (key >= BLOCK))
    return jnp.where(valid, 0.0, MASKED).astype(F32)


BIAS_SPEC = pl.BlockSpec((1, 2 * BLOCK, STACKED), lambda n: (jnp.minimum(n, 1), 0, 0))


def _softmax_keys_on_sublanes(k2, q, bias, sink_ref, j):
    head_of_lane = lax.broadcasted_iota(jnp.int32, (1, STACKED), 1) // BLOCK
    sink = jnp.zeros((1, STACKED), F32)
    for h in range(GQA_GROUP):
        sink = jnp.where(head_of_lane == h, sink_ref[j * GQA_GROUP + h], sink)
    s = lax.dot_general(k2, q, _DIMS["nt"], preferred_element_type=F32) + bias
    m = jnp.maximum(jnp.max(s, axis=0, keepdims=True), sink)
    e = jnp.exp(s - m)
    e_sink = jnp.exp(sink - m)
    inv = 1.0 / (jnp.sum(e, axis=0, keepdims=True) + e_sink)
    return e * inv, e_sink * inv


def _attn_fwd(name, qn, kn, proj, sinks, comm=None):
    T = qn.shape[0]
    nb = T // BLOCK
    plumb = _CommPlumbing(comm)

    def body(sink_ref, bias_ref, q_ref, kp_ref, kc_ref, vp_ref, vc_ref, *rest):
        c_in, o_ref = rest[:plumb.n_in], rest[plumb.n_in]
        c_out, c_scr = rest[plumb.n_in + 1: plumb.n_in + 1 + plumb.n_out], rest[plumb.n_in + 1 + plumb.n_out:]
        n = pl.program_id(0)
        plumb.run(n, nb, True, c_in, c_out, c_scr)
        low = lax.broadcasted_iota(jnp.int32, (1, LANES), 1) < HEAD_DIM
        kk = jnp.concatenate([kp_ref[...], kc_ref[...]], axis=0)
        vv = jnp.concatenate([vp_ref[...], vc_ref[...]], axis=0)
        for j in range(2):
            q = _stack_heads(q_ref[:, GROUP_WIDTH * j: GROUP_WIDTH * (j + 1)], low)
            p, _ = _softmax_keys_on_sublanes(_dup_head(kk, j), q, bias_ref[0], sink_ref, j)
            o_t = lax.dot_general(_dup_head(vv, j), p.astype(BF), _DIMS["tn"], preferred_element_type=F32)
            for pair, o in enumerate(_unstack_transposed(o_t, low)):
                lanes = slice(GROUP_WIDTH * j + LANES * pair, GROUP_WIDTH * j + LANES * (pair + 1))
                o_ref[:, lanes] = o.astype(BF)
        plumb.run(n, nb, False, c_in, c_out, c_scr)

    wide = pl.BlockSpec((BLOCK, ATTN_WIDTH), lambda n: (n, 0))
    res = pl.pallas_call(
        body, name=name, grid=(nb,),
        in_specs=[pl.BlockSpec(memory_space=pltpu.SMEM), BIAS_SPEC, wide,
                  pl.BlockSpec((BLOCK, LANES), lambda n: (jnp.maximum(n - 1, 0), 0)),
                  pl.BlockSpec((BLOCK, LANES), lambda n: (n, 0)),
                  pl.BlockSpec((BLOCK, LANES), lambda n: (jnp.maximum(n - 1, 0), KV_COL_BLOCK_V)),
                  pl.BlockSpec((BLOCK, LANES), lambda n: (n, KV_COL_BLOCK_V))] + [ANY] * plumb.n_in,
        out_specs=[wide] + [ANY] * plumb.n_out,
        out_shape=[jax.ShapeDtypeStruct((T, ATTN_WIDTH), BF)] + plumb.out_shapes, scratch_shapes=plumb.scratch,
        compiler_params=_params(("arbitrary",) if comm else ("parallel",)),
    )(sinks, _band_bias(), qn, kn, kn, proj, proj, *plumb.args)
    return (res[0], plumb.split_outputs(res[1:])) if comm is not None else res[0]


def _attn_bwd(name, dout, qn, kn, proj, sinks, comm):
    T = qn.shape[0]
    nb = T // BLOCK
    plumb = _CommPlumbing(comm)

    def body(sink_ref, bias_ref, do_ref, q_ref, kp_ref, kc_ref, vp_ref, vc_ref, *rest):
        c_in, (dq_ref, dk_ref, dv_ref, dsink_ref) = rest[:plumb.n_in], rest[plumb.n_in: plumb.n_in + 4]
        c_out = rest[plumb.n_in + 4: plumb.n_in + 4 + plumb.n_out]
        carry_k, carry_v, tot_k, tot_v = rest[plumb.n_in + 4 + plumb.n_out: plumb.n_in + 8 + plumb.n_out]
        c_scr = rest[plumb.n_in + 8 + plumb.n_out:]
        n = pl.program_id(0)
        plumb.run(n, nb + 1, True, c_in, c_out, c_scr)
        lane = lax.broadcasted_iota(jnp.int32, (1, LANES), 1)
        low = lane < HEAD_DIM

        @pl.when(n == 0)
        def _():
            carry_k[...] = jnp.zeros_like(carry_k)
            carry_v[...] = jnp.zeros_like(carry_v)
            dsink_ref[...] = jnp.zeros_like(dsink_ref)

        @pl.when(n == nb)
        def _():
            tot_k[...] = jnp.zeros_like(tot_k)
            tot_v[...] = jnp.zeros_like(tot_v)

        @pl.when(n < nb)
        def _():
            kk = jnp.concatenate([kp_ref[...], kc_ref[...]], axis=0)
            vv = jnp.concatenate([vp_ref[...], vc_ref[...]], axis=0)
            dk_tot = jnp.zeros((2 * BLOCK, LANES), F32)
            dv_tot = jnp.zeros((2 * BLOCK, LANES), F32)
            dsink = jnp.zeros((1, LANES), F32)
            for j in range(2):
                k2 = _dup_head(kk, j)
                v2 = _dup_head(vv, j)
                q = _stack_heads(q_ref[:, GROUP_WIDTH * j: GROUP_WIDTH * (j + 1)], low)
                do = _stack_heads(do_ref[:, GROUP_WIDTH * j: GROUP_WIDTH * (j + 1)], low)
                p, psink = _softmax_keys_on_sublanes(k2, q, bias_ref[0], sink_ref, j)
                dp =lax.dot_general(v2, do, _DIMS["nt"], preferred_element_type=F32)
                delta = jnp.sum(p * dp, axis=0, keepdims=True)
                ds = (p * (dp - delta)).astype(BF)
                dk2 = jnp.dot(ds, q, preferred_element_type=F32)
                dv2 = jnp.dot(p.astype(BF), do, preferred_element_type=F32)
                dq_t = lax.dot_general(k2, ds, _DIMS["tn"], preferred_element_type=F32)
                for pair, dq in enumerate(_unstack_transposed(dq_t, low)):
                    lanes = slice(GROUP_WIDTH * j + LANES * pair, GROUP_WIDTH * j + LANES * (pair + 1))
                    dq_ref[:, lanes] = dq.astype(BF)
                mine = low if j == 0 else jnp.logical_not(low)
                dk_tot = dk_tot + jnp.where(mine, dk2 + pltpu.roll(dk2, HEAD_DIM, axis=1), 0.0)
                dv_tot = dv_tot + jnp.where(mine, dv2 + pltpu.roll(dv2, HEAD_DIM, axis=1), 0.0)
                sink_term = psink * delta
                for h in range(GQA_GROUP):
                    val = -jnp.sum(sink_term[:, BLOCK * h: BLOCK * (h + 1)], axis=1, keepdims=True)
                    dsink = dsink + jnp.where(lane == j * GQA_GROUP + h, val, 0.0)
            tot_k[...] = dk_tot
            tot_v[...] = dv_tot
            dsink_ref[0:1, :] += dsink

        dk_ref[...] = (carry_k[...] + tot_k[0:BLOCK]).astype(BF)
        dv_ref[...] = (carry_v[...] + tot_v[0:BLOCK]).astype(BF)
        carry_k[...] = tot_k[BLOCK:]
        carry_v[...] = tot_v[BLOCK:]
        plumb.run(n, nb + 1, False, c_in, c_out, c_scr)

    cur = lambda n: (jnp.minimum(n, nb - 1), 0)
    prev = lambda n: (jnp.maximum(n - 1, 0), 0)
    wide = pl.BlockSpec((BLOCK, ATTN_WIDTH), cur)
    res = pl.pallas_call(
        body, name=name, grid=(nb + 1,),
        in_specs=[pl.BlockSpec(memory_space=pltpu.SMEM), BIAS_SPEC, wide, wide,
                  pl.BlockSpec((BLOCK, LANES), prev), pl.BlockSpec((BLOCK, LANES), cur),
                  pl.BlockSpec((BLOCK, LANES), lambda n: (jnp.maximum(n - 1, 0), KV_COL_BLOCK_V)),
                  pl.BlockSpec((BLOCK, LANES), lambda n: (jnp.minimum(n, nb - 1), KV_COL_BLOCK_V))] + [ANY] * plumb.n_in,
        out_specs=[wide, pl.BlockSpec((BLOCK, LANES), prev), pl.BlockSpec((BLOCK, LANES), prev),
                   pl.BlockSpec((8, LANES), lambda n: (0, 0))] + [ANY] * plumb.n_out,
        out_shape=[jax.ShapeDtypeStruct((T, ATTN_WIDTH), BF), jax.ShapeDtypeStruct((T, KV_WIDTH), BF),
                   jax.ShapeDtypeStruct((T, KV_WIDTH), BF), jax.ShapeDtypeStruct((8, LANES), F32)] + plumb.out_shapes,
        scratch_shapes=[pltpu.VMEM((BLOCK, LANES), F32), pltpu.VMEM((BLOCK, LANES), F32),
                        pltpu.VMEM((2 * BLOCK, LANES), F32), pltpu.VMEM((2 * BLOCK, LANES), F32)] + plumb.scratch,
        compiler_params=_params(("arbitrary",)),
    )(sinks, _band_bias(), dout, qn, kn, kn, proj, proj, *plumb.args)
    return list(res[:4]), plumb.split_outputs(res[4:])


def _swiglu_fwd_epilogue(accs, ex):
    g, u = accs
    return [g, u, g * jax.nn.sigmoid(g) * u], []


def _swiglu_bwd_epilogue(accs, ex):
    (da,) = accs
    g, u = ex[0].astype(F32), ex[1].astype(F32)
    s = jax.nn.sigmoid(g)
    return [da * u * (s * (1.0 + g * (1.0 - s))), da * (g * s)], []


def _residual_norm_epilogue(scale):
    def epilogue(accs, ex):
        res, gain = ex
        h = res + scale * accs[0]
        r = lax.rsqrt(jnp.mean(h * h, axis=-1, keepdims=True) + RMS_EPS)
        return [h, h * r * gain], []
    return epilogue


def _rms_bwd_epilogue(accs, ex):
    (dn,) = accs
    xv, g, dres = ex
    r = lax.rsqrt(jnp.mean(xv * xv, axis=-1, keepdims=True) + RMS_EPS)
    xhat = xv * r
    dxhat = dn * g
    dx = dres + r * (dxhat - xhat * jnp.mean(dxhat * xhat, axis=-1, keepdims=True))
    return [dx, dx], [dn * xhat]


def _loss_epilogue(accs, ex):
    xv, target = ex
    d = xv + 0.5 * accs[0] - target
    dy = d * (1.0 / D_MODEL)
    return [dy, dy], [d * d]


def _merge_fwd_epilogue(accs, ex):
    (ba,) = accs
    bp, gp_pre, ga_pre, bias_p, bias_a = ex
    gp = jax.nn.sigmoid(gp_pre.astype(F32) + bias_p)
    ga = jax.nn.sigmoid(ga_pre.astype(F32) + bias_a)
    return [gp * bp.astype(F32) + ga * ba, ba], []


def _merge_bwd_epilogue(accs, ex):
    (dm,) = accs
    bp, ba, gp_pre, ga_pre, bias_p, bias_a = ex
    gp = jax.nn.sigmoid(gp_pre.astype(F32) + bias_p)
    ga = jax.nn.sigmoid(ga_pre.astype(F32) + bias_a)
    dgp = dm * bp.astype(F32) * gp * (1.0 - gp)
    dga = dm * ba.astype(F32) * ga * (1.0 - ga)
    return [dm * gp, dm * ga, dgp, dga], [dgp, dga]


def _prep(name, ws, transposes):
    n = len(ws)

    def body(*refs):
        for w_ref, o_ref, tr in zip(refs[:n], refs[n:], transposes):
            v = w_ref[...]
            o_ref[...] = (v.T if tr else v).astype(BF)

    shapes = [jax.ShapeDtypeStruct(w.shape[::-1] if tr else w.shape, BF) for w, tr in zip(ws, transposes)]
    return pl.pallas_call(body, name=name, out_shape=shapes, compiler_params=_params())(*ws)


def _adam_math(w, g, m, v):
    m = ADAM_B1 * m + (1.0 - ADAM_B1) * g
    v = ADAM_B2 * v + (1.0 - ADAM_B2) * jnp.square(g)
    m_hat = m / (1.0 - ADAM_B1 ** ADAM_STEP)
    v_hat = v / (1.0 - ADAM_B2 ** ADAM_STEP)
    delta = -ADAM_LR * (m_hat / (jnp.sqrt(v_hat) + ADAM_EPS) + ADAM_WD * w)
    return delta, m, v


def _adamw_sharded(name, items, transpose=False):
    n = len(items)

    def body(*refs):
        ins, outs = refs[:4 * n], refs[4 * n:]
        for k in range(n):
            s_ref, w_ref, m_ref, v_ref = ins[4 * k: 4 * k + 4]
            g = s_ref[0].astype(F32)
            for i in range(1, 4):
                g = g + s_ref[i].astype(F32)
            if transpose:
                g = g.T
            delta, mn, vn = _adam_math(w_ref[...], g, m_ref[...], v_ref[...])
            for o_ref, val in zip(outs[4 * k: 4 * k + 4], (g, delta, mn, vn)):
                o_ref[...] = val

    flat = [a for item in items for a in item]
    out_shape = [jax.ShapeDtypeStruct(item[1].shape, F32) for item in items for _ in range(4)]
    _, r, C = items[0][0].shape
    rows = r // 4
    if transpose or rows % 8:
        res = pl.pallas_call(body, name=name, out_shape=out_shape, compiler_params=_params())(*flat)
    else:
        tile = pl.BlockSpec((rows, C), lambda i: (i, 0))
        res = pl.pallas_call(
            body, name=name, grid=(4,), in_specs=[pl.BlockSpec((4, rows, C), lambda i: (0, i, 0)), tile, tile, tile] * n,
            out_specs=[tile] * (4 * n), out_shape=out_shape, compiler_params=_params(("parallel",)),
        )(*flat)
    return [tuple(res[4 * k: 4 * k + 4]) for k in range(n)]


SMALL_LAYOUT = (("ffn1_norm", 0, (8, LANES)), ("mix_norm", 8, (8, LANES)), ("ffn2_norm", 16, (8, LANES)),
                ("gate_bias", 24, (16, LANES)), ("pool_scale", 40, (4, LANES)), ("q_norm", 48, (1, HEAD_DIM)),
                ("k_norm", 56, (1, HEAD_DIM)), ("sinks", 64, (1, N_HEADS)))
LOSS_ROW = 72
SMALL_ROWS = 80


def _adamw_small(name, g_vec, g_pool_w, params):
    n = len(SMALL_LAYOUT) + 1

    def body(vec_ref, pw_ref, *refs):
        ins, outs = refs[:3 * n], refs[3 * n:]
        vec = vec_ref[0]
        pw = pw_ref[0]
        for i in range(1, N_DEV):
            vec = vec + vec_ref[i]
            pw = pw + pw_ref[i]
        grads = [vec[r0:r0 + shape[0], 0:shape[1]] for _, r0, shape in SMALL_LAYOUT] + [pw]
        for p, g in enumerate(grads):
            w_ref, m_ref, v_ref = ins[3 * p: 3 * p + 3]
            delta, mn, vn = _adam_math(w_ref[...], g, m_ref[...], v_ref[...])
            for o_ref, val in zip(outs[4 * p: 4 * p + 4], (g, delta, mn, vn)):
                o_ref[...] = val
        outs[4 * n][...] = vec[LOSS_ROW:LOSS_ROW + 1, :]

    flat = [a for wmv in params for a in wmv]
    out_shape = [jax.ShapeDtypeStruct(wmv[0].shape, F32) for wmv in params for _ in range(4)]
    out_shape.append(jax.ShapeDtypeStruct((1, LANES), F32))
    res = pl.pallas_call(body, name=name, out_shape=out_shape, compiler_params=_params())(g_vec, g_pool_w, *flat)
    return [tuple(res[4 * p: 4 * p + 4]) for p in range(n)], res[4 * n]


def _place():
    x, y, c = lax.axis_index("x"), lax.axis_index("y"), lax.axis_index("c")
    other_chips = [(1 - x, y), (x, 1 - y), (1 - x, 1 - y)]
    return x, y, c, other_chips


def _rows(ref, r, place, natural=False):
    px, py, pc = place
    b = 4 * px + 2 * py + pc if natural else 4 * pc + 2 * px + py
    return ref.at[pl.ds(pl.multiple_of(b * r, 8), r), :]


def _gather_task(shards, natural=(), forward_at=0.75):
    n = len(shards)
    rs = [s.shape[0] for s in shards]
    rows_of = lambda ref, k, place: _rows(ref, rs[k], place, k in natural)

    def copy(scr, outs, k, slot, block, to, src=None):
        rows = rows_of(outs[k], k, block)
        return pltpu.make_async_remote_copy(
            src_ref=rows if src is None else src, dst_ref=rows, send_sem=scr[0].at[7 * k + slot],
            recv_sem=scr[1].at[7 * k + slot], device_id=to, device_id_type=MESH)

    def first_sends(ins, outs, scr):
        x, y, c, chips = _place()
        me = (x, y, c)
        cps = [copy(scr, outs, k, 1 + j, me, (*chip, c), src=ins[k]) for j, chip in enumerate(chips) for k in range(n)]
        return cps + [copy(scr, outs, k, 0, me, (x, y, 1 - c), src=ins[k]) for k in range(n)]

    def passed_on(outs, scr):
        x, y, c, chips = _place()
        return [copy(scr, outs, k, 4 + j, (*chip, c), (x, y, 1 - c)) for j, chip in enumerate(chips) for k in range(n)]

    def local(ins, outs, scr):
        x, y, c, _ = _place()
        return [pltpu.make_async_copy(ins[k], rows_of(outs[k], k, (x, y, c)), scr[2].at[k]) for k in range(n)]

    def start(ins, outs, scr):
        for cp in local(ins, outs, scr) + first_sends(ins, outs, scr):
            cp.start()

    def forward(ins, outs, scr):
        x, y, c, chips = _place()
        for j, chip in enumerate(chips):
            for k in range(n):
                copy(scr, outs, k, 1 + j, (*chip, c), (x, y, c)).wait_recv()
        for cp in passed_on(outs, scr):
            cp.start()

    def finish(ins, outs, scr):
        x, y, c, chips = _place()
        for k in range(n):
            copy(scr, outs, k, 0, (x, y, 1 - c), (x, y, c)).wait_recv()
        for j, chip in enumerate(chips):
            for k in range(n):
                copy(scr, outs, k, 4 + j, (*chip, 1 - c), (x, y, c)).wait_recv()
        for cp in first_sends(ins, outs, scr) + passed_on(outs, scr):
            cp.wait_send()
        for cp in local(ins, outs, scr):
            cp.wait()

    out_shapes = [jax.ShapeDtypeStruct((N_DEV * s.shape[0], s.shape[1]), s.dtype) for s in shards]
    scratch = [pltpu.SemaphoreType.DMA((7 * n,)), pltpu.SemaphoreType.DMA((7 * n,)), pltpu.SemaphoreType.DMA((n,))]
    return _Task(shards, out_shapes, scratch, [(0, start), (forward_at, forward), (1.0, finish)])


def _all_gather(name, shards, natural=()):
    return _comm_only(name, [_gather_task(shards, natural)])[0]


def _chip_task(sums):
    n = len(sums)
    rs = [s.shape[0] // 4 for s in sums]

    def block(ref, k, chip_index):
        return ref.at[pl.ds(pl.multiple_of(chip_index * rs[k], 8), rs[k]), :]

    def copies(ins, outs, scr):
        send_sems, recv_sems, local_sems = scr
        x, y, c, chips = _place()
        here = 2 * x + y
        local = [pltpu.make_async_copy(block(ins[k], k, here), outs[k].at[here], local_sems.at[k]) for k in range(n)]
        remote = []
        for j, (px, py) in enumerate(chips):
            remote += [pltpu.make_async_remote_copy(
                src_ref=block(ins[k], k, 2 * px + py), dst_ref=outs[k].at[here],
                send_sem=send_sems.at[3 * k + j], recv_sem=recv_sems.at[3 * k + j],
                device_id=(px, py, c), device_id_type=MESH) for k in range(n)]
        return local, remote

    def start(ins, outs, scr):
        local, remote = copies(ins, outs, scr)
        for cp in local + remote:
            cp.start()

    def finish(ins, outs, scr):
        local, remote = copies(ins, outs, scr)
        for cp in remote:
            cp.wait()
        for cp in local:
            cp.wait()

    out_shapes = [jax.ShapeDtypeStruct((4, r, s.shape[1]), s.dtype) for r, s in zip(rs, sums)]
    scratch = [pltpu.SemaphoreType.DMA((3 * n,)), pltpu.SemaphoreType.DMA((3 * n,)), pltpu.SemaphoreType.DMA((n,))]
    return _Task(sums, out_shapes, scratch, [(0, start), (1.0, finish)])


def _dw_pair(name, a, b, scale, comm=None, blocks=1):
    T, M = a.shape
    N = b.shape[1]
    half = M // 2
    wide = half // blocks
    tk = min(2048, T)
    nK = T // tk
    plumb = _CommPlumbing(comm)

    def body(core_ref, *rest):
        a_refs, b_ref, rest = rest[:blocks], rest[blocks], rest[blocks + 1:]
        c_in = rest[:plumb.n_in]
        o_ref = rest[plumb.n_in]
        c_out = rest[plumb.n_in + 1: plumb.n_in + 1 + plumb.n_out]
        acc, stage, land, send_sem, recv_sem = rest[plumb.n_in + 1 + plumb.n_out: plumb.n_in + 6 + plumb.n_out]
        c_scr = rest[plumb.n_in + 6 + plumb.n_out:]
        i, k = pl.program_id(0), pl.program_id(1)
        x, y, c, _ = _place()
        push = pltpu.make_async_remote_copy(src_ref=stage, dst_ref=land, send_sem=send_sem, recv_sem=recv_sem,
                                            device_id=(x, y, 1 - c), device_id_type=MESH)
        if comm:
            plumb.run(i * nK + k, 2 * nK, True, c_in, c_out, c_scr)

        av = a_refs[0][...] if blocks == 1 else jnp.concatenate([r[...] for r in a_refs], axis=1)
        p = lax.dot_general(av, b_ref[...], _DIMS["tn"], preferred_element_type=F32)

        @pl.when(k == 0)
        def _():
            acc[...] = p

        @pl.when(k > 0)
        def _():
            acc[...] += p

        @pl.when((i == 0) & (k == nK - 1))
        def _():
            stage[...] = (scale * acc[...]).astype(BF)
            push.start()

        @pl.when((i == 1) & (k == nK - 1))
        def _():
            push.wait_recv()
            o_ref[...] = (scale * acc[...] + land[...].astype(F32)).astype(BF)
            push.wait_send()

        if comm:
            plumb.run(i * nK + k, 2 * nK, False, c_in, c_out, c_scr)

    grid_spec = pltpu.PrefetchScalarGridSpec(
        num_scalar_prefetch=1, grid=(2, nK),
        in_specs=[pl.BlockSpec((tk, wide), functools.partial(
            lambda i, k, core, j: (k, (2 * j if blocks > 1 else 0) + jnp.where(i == 0, 1 - core[0], core[0])), j=j))
            for j in range(blocks)] + [pl.BlockSpec((tk, N), lambda i, k, core: (k, 0))] + [ANY] * plumb.n_in,
        out_specs=[pl.BlockSpec((half, N), lambda i, k, core: (0, 0))] + [ANY] * plumb.n_out,
        scratch_shapes=[pltpu.VMEM((half, N), F32), pltpu.VMEM((half, N), BF), pltpu.VMEM((half, N), BF),
                        pltpu.SemaphoreType.DMA, pltpu.SemaphoreType.DMA] + plumb.scratch)
    core = lax.axis_index("c").astype(jnp.int32).reshape(1)
    res = pl.pallas_call(
        body, name=name, grid_spec=grid_spec,
        out_shape=[jax.ShapeDtypeStruct((half, N), BF)] + plumb.out_shapes,
        compiler_params=_params(("arbitrary", "arbitrary")),
    )(core, *([a] * blocks), b, *plumb.args)
    return (res[0], plumb.split_outputs(res[1:])) if comm else res[0]


def _pair_task(parts):
    n = len(parts)

    def copies(ins, outs, scr):
        x, y, c, _ = _place()
        return [pltpu.make_async_remote_copy(
            src_ref=ins[k].at[:, pl.ds(1 - c, 1)], dst_ref=outs[k], send_sem=scr[0].at[k], recv_sem=scr[1].at[k],
            device_id=(x, y, 1 - c), device_id_type=MESH) for k in range(n)]

    def start(ins, outs, scr):
        for cp in copies(ins, outs, scr):
            cp.start()

    def finish(ins, outs, scr):
        for cp in copies(ins, outs, scr):
            cp.wait()

    out_shapes = [jax.ShapeDtypeStruct((4, 1) + p.shape[2:], p.dtype) for p in parts]
    scratch = [pltpu.SemaphoreType.DMA((n,)), pltpu.SemaphoreType.DMA((n,))]
    return _Task(parts, out_shapes, scratch, [(0, start), (1.0, finish)])


def _pair_sum(name, part, got, core):
    _, _, r, C = part.shape

    def body(core_ref, p_ref, g_ref, o_ref):
        o_ref[0] = (p_ref[0, 0].astype(F32) + g_ref[0, 0].astype(F32)).astype(o_ref.dtype)

    return pl.pallas_call(
        body, name=name,
        grid_spec=pltpu.PrefetchScalarGridSpec(
            num_scalar_prefetch=1, grid=(4,),
            in_specs=[pl.BlockSpec((1, 1, r, C), lambda i, core_ref: (i, core_ref[0], 0, 0)),
                      pl.BlockSpec((1, 1, r, C), lambda i, core_ref: (i, 0, 0, 0))],
            out_specs=pl.BlockSpec((1, r, C), lambda i, core_ref: (i, 0, 0))),
        out_shape=jax.ShapeDtypeStruct((4, r, C), part.dtype), compiler_params=_params(("parallel",)),
    )(core, part, got)


def _ffn_bwd(tag, dy, dyb, x, gain, wgT, wuT, wd, saved, earlier=None):
    n, g, u, a = saved
    half = lambda accs, ex: _swiglu_bwd_epilogue([0.5 * accs[0]], ex)
    act_args = dict(tm=512, tn=1408, tk=D_MODEL, epilogue=half, extras=[(g, "tile", 0), (u, "tile", 0)], cols_outer=True)
    if earlier is None:
        sum_d = _dw_pair(tag + "_dw_down", a, dyb, 0.5)
        (dg, du), ((slots_d,),) = _mm(tag + "_d_act", [(dyb, wd, "nt", 0)], [BF, BF], comm=[_chip_task([sum_d])], **act_args)
        slots_e = None
        sum_g = _dw_pair(tag + "_dw_gate", dg, n, 1.0)
    else:
        sum_d, ((got,),) = _dw_pair(tag + "_dw_down", a, dyb, 0.5, comm=[_pair_task([earlier])])
        core = lax.axis_index("c").astype(jnp.int32).reshape(1)
        sum_e = _pair_sum(tag + "_pair_sum_earlier", earlier, got, core)
        sum_e = sum_e.reshape(4 * sum_e.shape[1], sum_e.shape[2])
        (dg, du), ((slots_e,),) = _mm(tag + "_d_act", [(dyb, wd, "nt", 0)], [BF, BF], comm=[_chip_task([sum_e])], **act_args)
        sum_g, ((slots_d,),) = _dw_pair(tag + "_dw_gate", dg, n, 1.0, comm=[_chip_task([sum_d])])
    sum_u, ((slots_g,),) = _dw_pair(tag + "_dw_up", du, n, 1.0, comm=[_chip_task([sum_g])])
    (dx, dxb, dgain), ((slots_u,),) = _mm(
        tag + "_d_norm", [(dg, wgT, "nn", 0), (du, wuT, "nn", 0)], [F32, BF], tm=512, tn=D_MODEL, tk=D_FF,
        epilogue=_rms_bwd_epilogue, extras=[(x, "tile", 0), (gain, "row", 0), (dy, "tile", 0)], n_colsum=1,
        comm=[_chip_task([sum_u])])
    return dx, dxb, dgain, slots_e, slots_g, slots_u, slots_d


def _tile_gain(g):
    return jnp.concatenate([g, g]).reshape(1, LANES)


def _fold_heads(partials):
    return jnp.sum(partials.reshape(-1, HEAD_DIM), axis=0)


def _pack_small_grads(grads, loss_local):
    pieces, row = [], 0
    for name, r0, _ in SMALL_LAYOUT + (("loss", LOSS_ROW, None),):
        v = (loss_local if name == "loss" else grads[name]).reshape(-1)
        rows = -(-v.size // LANES)
        block = jnp.pad(v, (0, rows * LANES - v.size)).reshape(rows, LANES)
        pieces += [jnp.zeros((r0 - row, LANES), F32)] * (r0 > row) + [block]
        row = r0 + rows
    pieces.append(jnp.zeros((SMALL_ROWS - row, LANES), F32))
    return jnp.concatenate(pieces, axis=0)


def kernel(x, ffn1_norm, ffn1_w_gate, ffn1_w_up, ffn1_w_down, mix_norm, w_in, pool_w, pool_scale, w_pool_out, q_norm, k_norm, sinks, w_attn_out, gate_bias, w_out, ffn2_norm, ffn2_w_gate, ffn2_w_up, ffn2_w_down, loss_target, m_ffn1_norm, m_ffn1_w_gate, m_ffn1_w_up, m_ffn1_w_down, m_mix_norm, m_w_in, m_pool_w, m_pool_scale, m_w_pool_out, m_q_norm, m_k_norm, m_sinks, m_w_attn_out, m_gate_bias, m_w_out, m_ffn2_norm, m_ffn2_w_gate, m_ffn2_w_up, m_ffn2_w_down, v_ffn1_norm, v_ffn1_w_gate, v_ffn1_w_up, v_ffn1_w_down, v_mix_norm, v_w_in, v_pool_w, v_pool_scale, v_w_pool_out, v_q_norm, v_k_norm, v_sinks, v_w_attn_out, v_gate_bias, v_w_out, v_ffn2_norm, v_ffn2_w_gate, v_ffn2_w_up, v_ffn2_w_down):
    T = x.shape[1]
    x2 = x.reshape(T, D_MODEL)
    target = loss_target.reshape(T, D_MODEL)

    big = [
        ("ffn1_w_gate", ffn1_w_gate, m_ffn1_w_gate, v_ffn1_w_gate, True, False),
        ("ffn1_w_up", ffn1_w_up, m_ffn1_w_up, v_ffn1_w_up, True, False),
        ("ffn1_w_down", ffn1_w_down, m_ffn1_w_down, v_ffn1_w_down, False, False),
        ("w_in", w_in, m_w_in, v_w_in, True, False),
        ("w_pool_out", w_pool_out, m_w_pool_out, v_w_pool_out, False, True),
        ("w_attn_out", w_attn_out, m_w_attn_out, v_w_attn_out, False, False),
        ("w_out", w_out, m_w_out, v_w_out, False, False),
        ("ffn2_w_gate", ffn2_w_gate, m_ffn2_w_gate, v_ffn2_w_gate, True, False),
        ("ffn2_w_up", ffn2_w_up, m_ffn2_w_up, v_ffn2_w_up, True, False),
        ("ffn2_w_down", ffn2_w_down, m_ffn2_w_down, v_ffn2_w_down, False, False),
    ]
    view = lambda a, tv: a.T if tv else a
    shards = _prep("prep_weights", [view(w, tv) for _, w, _, _, tv, _ in big], [tk_ for *_, tk_ in big])
    g1 =ffn1_norm.reshape(1, D_MODEL)
    g2 = mix_norm.reshape(1, D_MODEL)
    g3 = ffn2_norm.reshape(1, D_MODEL)
    bias_row = gate_bias.reshape(1, 2 * D_MODEL)
    qg, kg = _tile_gain(q_norm) * ATTN_SCALE, _tile_gain(k_norm)
    scale_row = pool_scale.reshape(1, POOL_WIDTH)

    n1, ((wg1T, wu1T),) = _rms_fwd("ffn1_norm", x2, g1, [_gather_task(shards[0:2], forward_at=0.9)])
    (gt1, up1, act1), ((wd1,), (w_inT,)) = _mm(
        "ffn1_gate_up", [(n1, wg1T, "nt", 0), (n1, wu1T, "nt", 1)], [BF, BF, BF], tm=512, tn=1408, tk=D_MODEL,
        epilogue=_swiglu_fwd_epilogue, cols_outer=True,
        comm=[_gather_task(shards[2:3], forward_at=0.5), _gather_task(shards[3:4], natural=(0,), forward_at=0.9)])
    (h1, u), ((w_poT, w_ao, w_o),) = _mm(
        "ffn1_down", [(act1, wd1, "nn", 0)], [F32, BF], tm=512, tn=D_MODEL, tk=D_FF,
        epilogue=_residual_norm_epilogue(0.5), extras=[(x2, "tile", 0), (g2, "row", 0)],
        comm=[_gather_task(shards[4:7], natural=(0, 1, 2), forward_at=0.8)])
    saved1 = (n1, gt1, up1, act1)
    (proj,), ((wg2T,),) = _mm(
        "in_proj", [(u, w_inT, "nt", 0)], [BF], tm=512, tn=1280, tk=D_MODEL, cols_outer=True,
        comm=[_gather_task(shards[7:8], forward_at=0.8)])
    pooled, mixed = _pool_fwd("pool_fwd", proj, pool_w, scale_row)
    qn = _headnorm_fwd("q_norm", proj, COL_Q, ATTN_WIDTH, qg)
    kn = _headnorm_fwd("k_norm", proj, COL_K, KV_WIDTH, kg)
    attn, ((wu2T,),) = _attn_fwd("attn_fwd", qn, kn, proj, sinks, comm=[_gather_task(shards[8:9], forward_at=0.8)])
    (bp,) = _mm("pool_out", [(mixed, w_poT, "nt", 0)], [BF], tm=1024, tn=D_MODEL, tk=POOL_WIDTH)
    gate_tn = 256
    gate_extras = [(proj, "tile", COL_GP // gate_tn), (proj, "tile", COL_GA // gate_tn),
                   (bias_row, "row", 0), (bias_row, "row", D_MODEL // gate_tn)]
    merged, ba = _mm("attn_out_merge", [(attn, w_ao, "nn", 0)], [BF, BF], tm=2048, tn=gate_tn, tk=ATTN_WIDTH,
                     epilogue=_merge_fwd_epilogue, extras=[(bp, "tile", 0)] + gate_extras)
    h2, n2 = _mm("mix_out", [(merged, w_o, "nn", 0)], [F32, BF], tm=512, tn=D_MODEL, tk=D_MODEL,
                 epilogue=_residual_norm_epilogue(1.0), extras=[(h1, "tile", 0), (g3, "row", 0)])
    (gt2, up2, act2), ((wd2,),) = _mm(
        "ffn2_gate_up", [(n2, wg2T, "nt", 0), (n2, wu2T, "nt", 1)], [BF, BF, BF], tm=512, tn=1408, tk=D_MODEL,
        epilogue=_swiglu_fwd_epilogue, cols_outer=True, comm=[_gather_task(shards[9:10], forward_at=0.8)])
    dy, dyb, sq = _mm("ffn2_down_loss", [(act2, wd2, "nn", 0)], [F32, BF], tm=512, tn=D_MODEL, tk=D_FF,
                      epilogue=_loss_epilogue, extras=[(h2, "tile", 0), (target, "tile", 0)], n_colsum=1)
    loss_local = 0.5 * jnp.sum(sq) / D_MODEL

    dh2, dh2b, dg3, _, slots_g2, slots_u2, slots_d2 = _ffn_bwd(
        "ffn2", dy, dyb, h2, g3, wg2T, wu2T, wd2, (n2, gt2, up2, act2))
    dbp, dba, dgp, dga, cs_gp, cs_ga = _mm(
        "mix_out_bwd", [(dh2b, w_o, "nt", 0)], [BF, BF, BF, BF], tm=2048, tn=gate_tn, tk=D_MODEL,
        epilogue=_merge_bwd_epilogue, extras=[(bp, "tile", 0), (ba, "tile", 0)] + gate_extras, n_colsum=2)
    sum_o = _dw_pair("dw_out", merged, dh2b, 1.0, blocks=4)
    (dmixed,) = _mm("pool_out_bwd", [(dbp, w_poT, "nn", 0)], [BF], tm=1024, tn=POOL_WIDTH, tk=D_MODEL)
    sum_po = _dw_pair("dw_pool_out", dbp, mixed, 1.0, blocks=4)
    (dattn,) = _mm("attn_out_bwd", [(dba, w_ao, "nt", 0)], [BF], tm=1024, tn=ATTN_WIDTH, tk=D_MODEL)
    sum_ao = _dw_pair("dw_attn_out", attn, dba, 1.0, blocks=4)
    dxp, dpool_w, dpool_scale = _pool_bwd("pool_bwd", dmixed, pooled, pool_w, scale_row)
    (dqn, dkn, dv, dsink_tile), ((slots_o, slots_po, slots_ao),) = _attn_bwd(
        "attn_bwd", dattn, qn, kn, proj, sinks, [_chip_task([sum_o, sum_po, sum_ao])])
    dq, dqg = _headnorm_bwd("q_norm_bwd", dqn, proj, COL_Q, ATTN_WIDTH, qg)
    dk, dkg = _headnorm_bwd("k_norm_bwd", dkn, proj, COL_K, KV_WIDTH, kg)
    dproj = jnp.concatenate([dxp, dq, dk, dv, dgp, dga], axis=1)
    dh1, dh1b, dg2 = _mm(
        "in_proj_bwd", [(dproj, w_inT, "nn", 0)], [F32, BF], tm=512, tn=D_MODEL, tk=IN_WIDTH, epilogue=_rms_bwd_epilogue,
        extras=[(h1, "tile", 0), (g2, "row", 0), (dh2, "tile", 0)], n_colsum=1)
    (dw_inT,) = _mm("dw_in", [(dproj, u, "tn", 0)], [BF], tm=1280, tn=D_MODEL, tk=2048)
    dx, _, dg1, slots_in, slots_g1, slots_u1, slots_d1 = _ffn_bwd(
        "ffn1", dh1, dh1b, x2, g1, wg1T, wu1T, wd1, saved1, dw_inT.reshape(4, 2, IN_WIDTH // N_DEV, D_MODEL))

    slots = [slots_g1, slots_u1, slots_d1, slots_in, slots_po, slots_ao, slots_o, slots_g2, slots_u2, slots_d2]
    big_out = {}
    for label, group in (("ffn", (0, 1, 2, 7, 8, 9)), ("w_in", (3,)), ("w_pool_out", (4,)), ("attn_out_and_out", (5, 6))):
        items = [(slots[k], view(big[k][1], big[k][4]), view(big[k][2], big[k][4]), view(big[k][3], big[k][4]))
                 for k in group]
        for k, res in zip(group, _adamw_sharded("adamw_" + label, items, transpose=big[group[0]][5])):
            big_out[big[k][0]] = tuple(view(r, big[k][4]) for r in res)

    small_grads = {
        "ffn1_norm": jnp.sum(dg1, axis=(0, 1)), "mix_norm": jnp.sum(dg2, axis=(0, 1)), "ffn2_norm": jnp.sum(dg3, axis=(0, 1)),
        "gate_bias": jnp.concatenate([jnp.sum(cs_gp, axis=(0, 1)), jnp.sum(cs_ga, axis=(0, 1))]),
        "pool_scale": dpool_scale, "q_norm": _fold_heads(dqg) * ATTN_SCALE, "k_norm": _fold_heads(dkg),
        "sinks": dsink_tile[0, :N_HEADS]}
    g_vec, g_pool_w = _all_gather("gather_small_grads", [_pack_small_grads(small_grads, loss_local),
                                                         dpool_w.reshape(-1, LANES)])
    given = {"ffn1_norm": (ffn1_norm, m_ffn1_norm, v_ffn1_norm), "mix_norm": (mix_norm, m_mix_norm, v_mix_norm),
             "ffn2_norm": (ffn2_norm, m_ffn2_norm, v_ffn2_norm), "gate_bias": (gate_bias, m_gate_bias, v_gate_bias),
             "pool_scale": (pool_scale, m_pool_scale, v_pool_scale), "q_norm": (q_norm, m_q_norm, v_q_norm),
             "k_norm": (k_norm, m_k_norm, v_k_norm), "sinks": (sinks, m_sinks, v_sinks)}
    params = [tuple(a.reshape(shape) for a in given[nm]) for nm, _, shape in SMALL_LAYOUT]
    params.append(tuple(a.reshape(-1, LANES) for a in (pool_w, m_pool_w, v_pool_w)))
    small_res, loss_row = _adamw_small("adamw_small", g_vec.reshape(N_DEV, SMALL_ROWS, LANES),
                                       g_pool_w.reshape(N_DEV, -1, LANES), params)
    small_out = {nm: tuple(r.reshape(given[nm][0].shape) for r in res)
                 for (nm, _, _), res in zip(SMALL_LAYOUT, small_res)}
    small_out["pool_w"] = tuple(r.reshape(pool_w.shape) for r in small_res[-1])
    loss = loss_row[0, 0]

    order = ["ffn1_norm", "ffn1_w_gate", "ffn1_w_up", "ffn1_w_down", "mix_norm", "w_in", "pool_w", "pool_scale",
             "w_pool_out", "q_norm", "k_norm", "sinks", "w_attn_out", "gate_bias", "w_out", "ffn2_norm",
             "ffn2_w_gate", "ffn2_w_up", "ffn2_w_down"]
    every = {**big_out, **small_out}
    outs = [loss, dx.reshape(x.shape)]
    for j in range(4):
        outs += [every[nm][j] for nm in order]
    return tuple(outs)
```

```python
import functools

import jax
import jax.numpy as jnp
from jax import lax
from jax.experimental import pallas as pl
from jax.experimental.pallas import tpu as pltpu

BF = jnp.bfloat16
F32 = jnp.float32

D_MODEL = 1024
D_FF = 2816
POOL_WIDTH = 512
POOL_GROUP = 128
N_POOL_GROUPS = 4
HEAD_DIM = 64
N_HEADS = 16
GQA_GROUP = 8
BLOCK = 128
ATTN_WIDTH = 1024
KV_WIDTH = 128
IN_WIDTH = 3840
RMS_EPS = 1e-6
N_DEV = 8
LANES = 128

COL_Q = POOL_WIDTH
COL_K = COL_Q + ATTN_WIDTH
COL_V = COL_K + KV_WIDTH
COL_GP = COL_V + KV_WIDTH
COL_GA = COL_GP + D_MODEL

ADAM_LR = 0.001
ADAM_B1 = 0.9
ADAM_B2 = 0.999
ADAM_EPS = 1e-08
ADAM_WD = 0.01
ADAM_STEP = 10

VMEM_LIMIT_V7X = 56 * 1024 * 1024
MESH = pl.DeviceIdType.MESH
ANY = pl.BlockSpec(memory_space=pl.ANY)


def _params(sem=None):
    return pltpu.CompilerParams(dimension_semantics=sem, vmem_limit_bytes=VMEM_LIMIT_V7X)


_DIMS = {"nt": (((1,), (1,)), ((), ())), "nn": (((1,), (0,)), ((), ())), "tn": (((0,), (0,)), ((), ()))}


class _Task:
    def __init__(self, inputs, out_shapes, scratch, phases):
        self.inputs, self.out_shapes, self.scratch = list(inputs), list(out_shapes), list(scratch)
        self.phases = list(phases)


class _CommPlumbing:
    def __init__(self, tasks):
        self.tasks = list(tasks or [])
        self.args = [a for t in self.tasks for a in t.inputs]
        self.out_shapes = [o for t in self.tasks for o in t.out_shapes]
        self.scratch = [s for t in self.tasks for s in t.scratch]
        self.n_in, self.n_out = len(self.args), len(self.out_shapes)

    def _slices(self, c_in, c_out, c_scr):
        i = o = s = 0
        for t in self.tasks:
            yield t, c_in[i:i + len(t.inputs)], c_out[o:o + len(t.out_shapes)], c_scr[s:s + len(t.scratch)]
            i, o, s = i + len(t.inputs), o + len(t.out_shapes), s + len(t.scratch)

    def run(self, step, steps, before, c_in, c_out, c_scr):
        for t, ins, outs, scr in self._slices(c_in, c_out, c_scr):
            for frac, fn in t.phases:
                if step is None:
                    fn(ins, outs, scr)
                elif before == (frac == 0):
                    at = 0 if frac == 0 else max(0, min(steps, -(-int(round(frac * steps * 64)) // 64)) - 1)
                    pl.when(step == at)(functools.partial(fn, ins, outs, scr))

    def split_outputs(self, flat):
        res, o = [], 0
        for t in self.tasks:
            res.append(list(flat[o:o + len(t.out_shapes)]))
            o += len(t.out_shapes)
        return res


def _comm_only(name, tasks):
    plumb = _CommPlumbing(tasks)

    def body(*refs):
        c_in, c_out = refs[:plumb.n_in], refs[plumb.n_in: plumb.n_in + plumb.n_out]
        c_scr = refs[plumb.n_in + plumb.n_out:]
        plumb.run(None, 1, True, c_in, c_out, c_scr)

    res = pl.pallas_call(
        body, name=name, in_specs=[ANY] * plumb.n_in, out_specs=[ANY] * plumb.n_out, out_shape=plumb.out_shapes,
        scratch_shapes=plumb.scratch, compiler_params=pltpu.CompilerParams(has_side_effects=True),
    )(*plumb.args)
    return plumb.split_outputs(res)


def _mm(name, terms, out_dtypes, *, tm, tn, tk, epilogue=None, extras=(), n_colsum=0, comm=None, cols_outer=False):
    a0, b0, mode0, _ = terms[0]
    if mode0 == "nt":
        (M, K), N = a0.shape, b0.shape[0]
    elif mode0 == "nn":
        (M, K), N = a0.shape, b0.shape[1]
    else:
        (K, M), N = a0.shape, b0.shape[1]
    tm, tn, tk = min(tm, M), min(tn, N), min(tk, K)
    assert M % tm == 0 and N % tn == 0 and K % tk == 0, (name, M, N, K, tm, tn, tk)
    nI, nJ, nK = M // tm, N // tn, K // tk
    n_terms = len(terms)
    n_acc = max(t[3] for t in terms) + 1
    n_ex = len(extras)
    n_out = len(out_dtypes)
    if epilogue is None:
        epilogue = lambda accs, ex: ([accs[0]], [])
    plumb = _CommPlumbing(comm)
    n_scr = n_acc if nK > 1 else 0
    grid = (nJ, nI, nK) if cols_outer else (nI, nJ, nK)

    def body(*refs):
        n_in = 2 * n_terms + n_ex
        ab = refs[: 2 * n_terms]
        ex_refs = refs[2 * n_terms: n_in]
        c_in = refs[n_in: n_in + plumb.n_in]
        o0 = n_in + plumb.n_in
        out_refs = refs[o0: o0 + n_out]
        cs_refs = refs[o0 + n_out: o0 + n_out + n_colsum]
        c_out = refs[o0 + n_out + n_colsum: o0 + n_out + n_colsum + plumb.n_out]
        s0 = o0 + n_out + n_colsum + plumb.n_out
        acc_refs = refs[s0: s0 + n_scr]
        c_scr = refs[s0 + n_scr:]
        steps = grid[0] * grid[1] * nK
        if comm:
            step = (pl.program_id(0) * grid[1] + pl.program_id(1)) * nK + pl.program_id(2)
            plumb.run(step, steps, True, c_in, c_out, c_scr)

        def products():
            accs = [None] * n_acc
            for t, (_, _, mode, ai) in enumerate(terms):
                p = lax.dot_general(ab[2 * t][...], ab[2 * t + 1][...], _DIMS[mode], preferred_element_type=F32)
                accs[ai] = p if accs[ai] is None else accs[ai] + p
            return accs

        def finish(accs):
            outs, colsums = epilogue(accs, [r[...] for r in ex_refs])
            for r, o in zip(out_refs, outs):
                r[...] = o.astype(r.dtype)
            for r, cs in zip(cs_refs, colsums):
                r[...] = jnp.sum(cs, axis=0, keepdims=True).reshape(r.shape)

        if nK == 1:
            finish(products())
        else:
            k = pl.program_id(2)
            accs = products()

            @pl.when(k == 0)
            def _():
                for r, a in zip(acc_refs, accs):
                    r[...] = a

            @pl.when(k > 0)
            def _():
                for r, a in zip(acc_refs, accs):
                    r[...] += a

            @pl.when(k == nK - 1)
            def _():
                finish([r[...] for r in acc_refs])

        if comm:
            plumb.run(step, steps, False, c_in, c_out, c_scr)

    def spec(block, index, fixed=False):
        imap = (lambda q, p, k: index(p, q, k)) if cols_outer else index
        return pl.BlockSpec(block, imap, pipeline_mode=pl.Buffered(1)) if fixed else pl.BlockSpec(block, imap)

    in_specs, args = [], []
    for a, b, mode, _ in terms:
        if mode == "nt":
            in_specs += [spec((tm, tk), lambda i, j, k: (i, k), nI * nK == 1),
                         spec((tn, tk), lambda i, j, k: (j, k), nJ * nK == 1)]
        elif mode == "nn":
            in_specs += [spec((tm, tk), lambda i, j, k: (i, k), nI * nK == 1),
                         spec((tk, tn), lambda i, j, k: (k, j), nJ * nK == 1)]
        else:
            in_specs += [spec((tk, tm), lambda i, j, k: (k, i), nI * nK == 1),
                         spec((tk, tn), lambda i, j, k: (k, j), nJ * nK == 1)]
        args += [a, b]
    for arr, kind, off in extras:
        if kind == "tile":
            in_specs.append(spec((tm, tn), functools.partial(lambda i, j, k, off: (i, j + off), off=off)))
        else:
            in_specs.append(spec((1, tn), functools.partial(lambda i, j, k, off: (0, j + off), off=off)))
        args.append(arr)
    out_shape = [jax.ShapeDtypeStruct((M, N), dt) for dt in out_dtypes]
    out_specs = [spec((tm, tn), lambda i, j, k: (i, j)) for _ in out_dtypes]
    out_shape += [jax.ShapeDtypeStruct((nI, 1, N), F32) for _ in range(n_colsum)]
    out_specs += [spec((1, 1, tn), lambda i, j, k: (i, 0, j)) for _ in range(n_colsum)]
    scratch = [pltpu.VMEM((tm, tn), F32) for _ in range(n_scr)]
    args += plumb.args
    in_specs += [ANY] * plumb.n_in
    out_shape += plumb.out_shapes
    out_specs += [ANY] * plumb.n_out
    sem = ("arbitrary",) * 3 if comm else ("parallel", "parallel", "arbitrary")
    res = pl.pallas_call(
        body, name=name, grid=grid, in_specs=in_specs, out_specs=out_specs, out_shape=out_shape,
        scratch_shapes=scratch + plumb.scratch, compiler_params=_params(sem),
    )(*args)
    n_own = n_out + n_colsum
    return (list(res[:n_own]), plumb.split_outputs(res[n_own:])) if comm is not None else res


ROW_TILE = 512


def _rms_fwd(name, x, g, comm):
    T, D = x.shape
    steps = T // ROW_TILE
    plumb = _CommPlumbing(comm)

    def body(x_ref, g_ref, *rest):
        c_in, o_ref = rest[:plumb.n_in], rest[plumb.n_in]
        c_out, c_scr = rest[plumb.n_in + 1: plumb.n_in + 1 + plumb.n_out], rest[plumb.n_in + 1 + plumb.n_out:]
        plumb.run(pl.program_id(0), steps, True, c_in, c_out, c_scr)
        xv = x_ref[...]
        r = lax.rsqrt(jnp.mean(xv * xv, axis=-1, keepdims=True) + RMS_EPS)
        o_ref[...] = (xv * r * g_ref[...]).astype(BF)
        plumb.run(pl.program_id(0), steps, False, c_in, c_out, c_scr)

    row = pl.BlockSpec((ROW_TILE, D), lambda i: (i, 0))
    res = pl.pallas_call(
        body, name=name, grid=(steps,),
        in_specs=[row, pl.BlockSpec((1, D), lambda i: (0, 0))] + [ANY] * plumb.n_in,
        out_specs=[row] + [ANY] * plumb.n_out, out_shape=[jax.ShapeDtypeStruct((T, D), BF)] + plumb.out_shapes,
        scratch_shapes=plumb.scratch, compiler_params=_params(("arbitrary",)),
    )(x, g, *plumb.args)
    return res[0], plumb.split_outputs(res[1:])


HEADNORM_TILE = 1024


def _half_sum_matrix():
    r = lax.broadcasted_iota(jnp.int32, (LANES, LANES), 0) // HEAD_DIM
    c = lax.broadcasted_iota(jnp.int32, (LANES, LANES), 1) // HEAD_DIM
    return (r == c).astype(BF)


def _head_mean(v, ones_blockdiag):
    hi = v.astype(BF)
    lo = (v - hi.astype(F32)).astype(BF)
    s = jnp.dot(hi, ones_blockdiag, preferred_element_type=F32) + jnp.dot(lo, ones_blockdiag, preferred_element_type=F32)
    return s * (1.0 / HEAD_DIM)


def _headnorm_fwd(name, proj, col0, width, g2):
    T = proj.shape[0]
    wide = min(width, GROUP_WIDTH)
    nb, off = width // wide, col0 // wide

    def body(x_ref, g_ref, b_ref, o_ref):
        for s in range(wide // LANES):
            lanes = slice(LANES * s, LANES * (s + 1))
            xv = x_ref[:, lanes].astype(F32)
            r = lax.rsqrt(_head_mean(xv * xv, b_ref[...]) + RMS_EPS)
            o_ref[:, lanes] = (xv * r * g_ref[...]).astype(BF)

    return pl.pallas_call(
        body, name=name, grid=(T // HEADNORM_TILE, nb),
        in_specs=[pl.BlockSpec((HEADNORM_TILE, wide), lambda i, j: (i, j + off)),
                  pl.BlockSpec((1, LANES), lambda i, j: (0, 0)), pl.BlockSpec((LANES, LANES), lambda i, j: (0, 0))],
        out_specs=pl.BlockSpec((HEADNORM_TILE, wide), lambda i, j: (i, j)),
        out_shape=jax.ShapeDtypeStruct((T, width), BF), compiler_params=_params(("parallel", "parallel")),
    )(proj, g2, _half_sum_matrix())


def _headnorm_bwd(name, dy, proj, col0, width, g2):
    T = proj.shape[0]
    wide = min(width, GROUP_WIDTH)
    nb, off = width // wide, col0 // wide

    def body(dy_ref, x_ref, g_ref, b_ref, dx_ref, dg_ref):
        for s in range(wide // LANES):
            lanes = slice(LANES * s, LANES * (s + 1))
            xv = x_ref[:, lanes].astype(F32)
            dyv = dy_ref[:, lanes].astype(F32)
            r = lax.rsqrt(_head_mean(xv * xv, b_ref[...]) + RMS_EPS)
            xhat = xv * r
            dxhat = dyv * g_ref[...]
            dx_ref[:, lanes] = (r * (dxhat - xhat * _head_mean(dxhat * xhat, b_ref[...]))).astype(BF)
            dg_ref[0, :, lanes] = jnp.sum(dyv * xhat, axis=0, keepdims=True)

    return pl.pallas_call(
        body, name=name, grid=(T // HEADNORM_TILE, nb),
        in_specs=[pl.BlockSpec((HEADNORM_TILE, wide), lambda i, j: (i, j)),
                  pl.BlockSpec((HEADNORM_TILE, wide), lambda i, j: (i, j + off)),
                  pl.BlockSpec((1, LANES), lambda i, j: (0, 0)), pl.BlockSpec((LANES, LANES), lambda i, j: (0, 0))],
        out_specs=[pl.BlockSpec((HEADNORM_TILE, wide), lambda i, j: (i, j)),
                   pl.BlockSpec((1, 1, wide), lambda i, j: (i, 0, j))],
        out_shape=[jax.ShapeDtypeStruct((T, width), BF), jax.ShapeDtypeStruct((T // HEADNORM_TILE, 1, width), F32)],
        compiler_params=_params(("parallel", "parallel")),
    )(dy, proj, g2, _half_sum_matrix())


def _shift_down(v, k, row):
    return jnp.where(row >= k, pltpu.roll(v, k, axis=0), 0.0)


def _shift_up(v, k, row, T):
    return jnp.where(row < T - k, pltpu.roll(v, T - k, axis=0), 0.0)


def _by_group(g, vals):
    out = vals[-1]
    for i in range(len(vals) - 2, -1, -1):
        out = jnp.where(g == i, vals[i], out)
    return out


def _pool_fwd(name, proj, pool_w, pool_scale):
    T = proj.shape[0]

    def body(x_ref, w_ref, s_ref, pooled_ref, mixed_ref):
        g = pl.program_id(0)
        xv = x_ref[...].astype(F32)
        row = lax.broadcasted_iota(jnp.int32, (T, 1), 0)
        s2 = xv + _shift_down(xv, 1, row)
        s4 = s2 + _shift_down(s2, 2, row)
        s8 = s4 + _shift_down(s4, 4, row)
        s16 = s8 + _shift_down(s8, 8, row)
        wsum = _by_group(g, [s2, s4, s8, s16])
        count = jnp.minimum(row + 1, 2 << g).astype(F32)
        pooled = (wsum / count - xv).astype(BF)
        pooled_ref[...] = pooled
        mixed = jnp.dot(pooled, w_ref[0].astype(BF), preferred_element_type=F32) * s_ref[...]
        mixed_ref[...] = mixed.astype(BF)

    col = pl.BlockSpec((T, POOL_GROUP), lambda g: (0, g))
    return pl.pallas_call(
        body, name=name, grid=(N_POOL_GROUPS,),
        in_specs=[col, pl.BlockSpec((1, POOL_GROUP, POOL_GROUP), lambda g: (g, 0, 0)),
                  pl.BlockSpec((1, POOL_GROUP), lambda g: (0, g))],
        out_specs=[col, col],
        out_shape=[jax.ShapeDtypeStruct((T, POOL_WIDTH), BF), jax.ShapeDtypeStruct((T, POOL_WIDTH), BF)],
        compiler_params=_params(("parallel",)),
    )(proj, pool_w, pool_scale)


def _pool_bwd(name, dmixed, pooled, pool_w, pool_scale):
    T = dmixed.shape[0]

    def body(dm_ref, p_ref, w_ref, s_ref, dx_ref, dw_ref, ds_ref):
        g = pl.program_id(0)
        dm = dm_ref[...].astype(F32)
        pooled = p_ref[...]
        w = w_ref[0].astype(BF)
        pre = jnp.dot(pooled, w, preferred_element_type=F32)
        ds_ref[...] = jnp.sum(dm * pre, axis=0, keepdims=True)
        dms = (dm * s_ref[...]).astype(BF)
        dw_ref[0] = lax.dot_general(pooled, dms, _DIMS["tn"], preferred_element_type=F32)
        dpooled = lax.dot_general(dms, w, _DIMS["nt"], preferred_element_type=F32)
        row = lax.broadcasted_iota(jnp.int32, (T, 1), 0)
        count = jnp.minimum(row + 1, 2 << g).astype(F32)
        z = dpooled / count
        l2 = z + _shift_up(z, 1, row, T)
        l4 = l2 + _shift_up(l2, 2, row, T)
        l8 = l4 + _shift_up(l4, 4, row, T)
        l16 = l8 + _shift_up(l8, 8, row, T)
        dx_ref[...] = (_by_group(g, [l2, l4, l8, l16]) - dpooled).astype(BF)

    col = pl.BlockSpec((T, POOL_GROUP), lambda g: (0, g))
    wspec = pl.BlockSpec((1, POOL_GROUP, POOL_GROUP), lambda g: (g, 0, 0))
    sspec = pl.BlockSpec((1, POOL_GROUP), lambda g: (0, g))
    return pl.pallas_call(
        body, name=name, grid=(N_POOL_GROUPS,), in_specs=[col, col, wspec, sspec], out_specs=[col, wspec, sspec],
        out_shape=[jax.ShapeDtypeStruct((T, POOL_WIDTH), BF),
                   jax.ShapeDtypeStruct((N_POOL_GROUPS, POOL_GROUP, POOL_GROUP), F32),
                   jax.ShapeDtypeStruct((1, POOL_WIDTH), F32)],
        compiler_params=_params(("parallel",)),
    )(dmixed, pooled, pool_w, pool_scale)


ATTN_SCALE = HEAD_DIM ** -0.5
MASKED = float(jnp.finfo(jnp.float32).min)
KV_COL_BLOCK_V = COL_V // LANES
GROUP_WIDTH = GQA_GROUP * HEAD_DIM


def _dup_head(v, j):
    half = lax.broadcasted_iota(jnp.int32, (1, LANES), 1) // HEAD_DIM
    return jnp.where(half == j, v, pltpu.roll(v, HEAD_DIM, axis=1))


def _stack_heads(v, low):
    pieces = []
    for p in range(GROUP_WIDTH // LANES):
        vp = v[:, LANES * p: LANES * (p + 1)]
        pieces.append(jnp.where(low, vp, jnp.zeros_like(vp)))
        pieces.append(jnp.where(low, jnp.zeros_like(vp), vp))
    return jnp.concatenate(pieces, axis=0)


def _unstack_transposed(t, low):
    pairs = []
    for p in range(GROUP_WIDTH // LANES):
        even = t[:, BLOCK * (2 * p): BLOCK * (2 * p + 1)].T
        odd = t[:, BLOCK * (2 * p + 1): BLOCK * (2 * p + 2)].T
        pairs.append(jnp.where(low, even, odd))
    return pairs


STACKED = GQA_GROUP * BLOCK


def _band_bias():
    key = lax.broadcasted_iota(jnp.int32, (2, 2 * BLOCK, STACKED), 1)
    qry = lax.broadcasted_iota(jnp.int32, (2, 2 * BLOCK, STACKED), 2) % BLOCK
    first = lax.broadcasted_iota(jnp.int32, (2, 2 * BLOCK, STACKED), 0) == 0
    valid = (key > qry) & (key <= qry + BLOCK) & (jnp.logical_not(first) | (key >= BLOCK))
    return jnp.where(valid, 0.0, MASKED).astype(F32)


BIAS_SPEC = pl.BlockSpec((1, 2 * BLOCK, STACKED), lambda n: (jnp.minimum(n, 1), 0, 0))


def _softmax_keys_on_sublanes(k2, q, bias, sink_ref, j):
    head_of_lane = lax.broadcasted_iota(jnp.int32, (1, STACKED), 1) // BLOCK
    sink = jnp.zeros((1, STACKED), F32)
    for h in range(GQA_GROUP):
        sink = jnp.where(head_of_lane == h, sink_ref[j * GQA_GROUP + h], sink)
    s = lax.dot_general(k2, q, _DIMS["nt"], preferred_element_type=F32) + bias
    m = jnp.maximum(jnp.max(s, axis=0, keepdims=True), sink)
    e = jnp.exp(s - m)
    e_sink = jnp.exp(sink - m)
    inv = 1.0 / (jnp.sum(e, axis=0, keepdims=True) + e_sink)
    return e * inv, e_sink * inv


def _attn_fwd(name, qn, kn, proj, sinks, comm=None):
    T = qn.shape[0]
    nb = T // BLOCK
    plumb = _CommPlumbing(comm)

    def body(sink_ref, bias_ref, q_ref, kp_ref, kc_ref, vp_ref, vc_ref, *rest):
        c_in, o_ref = rest[:plumb.n_in], rest[plumb.n_in]
        c_out, c_scr = rest[plumb.n_in + 1: plumb.n_in + 1 + plumb.n_out], rest[plumb.n_in + 1 + plumb.n_out:]
        n = pl.program_id(0)
        plumb.run(n, nb, True, c_in, c_out, c_scr)
        low = lax.broadcasted_iota(jnp.int32, (1, LANES), 1) < HEAD_DIM
        kk = jnp.concatenate([kp_ref[...], kc_ref[...]], axis=0)
        vv = jnp.concatenate([vp_ref[...], vc_ref[...]], axis=0)
        for j in range(2):
            q = _stack_heads(q_ref[:, GROUP_WIDTH * j: GROUP_WIDTH * (j + 1)], low)
            p, _ = _softmax_keys_on_sublanes(_dup_head(kk, j), q, bias_ref[0], sink_ref, j)
            o_t = lax.dot_general(_dup_head(vv, j), p.astype(BF), _DIMS["tn"], preferred_element_type=F32)
            for pair, o in enumerate(_unstack_transposed(o_t, low)):
                lanes = slice(GROUP_WIDTH * j + LANES * pair, GROUP_WIDTH * j + LANES * (pair + 1))
                o_ref[:, lanes] = o.astype(BF)
        plumb.run(n, nb, False, c_in, c_out, c_scr)

    wide = pl.BlockSpec((BLOCK, ATTN_WIDTH), lambda n: (n, 0))
    res = pl.pallas_call(
        body, name=name, grid=(nb,),
        in_specs=[pl.BlockSpec(memory_space=pltpu.SMEM), BIAS_SPEC, wide,
                  pl.BlockSpec((BLOCK, LANES), lambda n: (jnp.maximum(n - 1, 0), 0)),
                  pl.BlockSpec((BLOCK, LANES), lambda n: (n, 0)),
                  pl.BlockSpec((BLOCK, LANES), lambda n: (jnp.maximum(n - 1, 0), KV_COL_BLOCK_V)),
                  pl.BlockSpec((BLOCK, LANES), lambda n: (n, KV_COL_BLOCK_V))] + [ANY] * plumb.n_in,
        out_specs=[wide] + [ANY] * plumb.n_out,
        out_shape=[jax.ShapeDtypeStruct((T, ATTN_WIDTH), BF)] + plumb.out_shapes, scratch_shapes=plumb.scratch,
        compiler_params=_params(("arbitrary",) if comm else ("parallel",)),
    )(sinks, _band_bias(), qn, kn, kn, proj, proj, *plumb.args)
    return (res[0], plumb.split_outputs(res[1:])) if comm is not None else res[0]


def _attn_bwd(name, dout, qn, kn, proj, sinks, comm):
    T = qn.shape[0]
    nb = T // BLOCK
    plumb = _CommPlumbing(comm)

    def body(sink_ref, bias_ref, do_ref, q_ref, kp_ref, kc_ref, vp_ref, vc_ref, *rest):
        c_in, (dq_ref, dk_ref, dv_ref, dsink_ref) = rest[:plumb.n_in], rest[plumb.n_in: plumb.n_in + 4]
        c_out = rest[plumb.n_in + 4: plumb.n_in + 4 + plumb.n_out]
        carry_k, carry_v, tot_k, tot_v = rest[plumb.n_in + 4 + plumb.n_out: plumb.n_in + 8 + plumb.n_out]
        c_scr = rest[plumb.n_in + 8 + plumb.n_out:]
        n = pl.program_id(0)
        plumb.run(n, nb + 1, True, c_in, c_out, c_scr)
        lane = lax.broadcasted_iota(jnp.int32, (1, LANES), 1)
        low = lane < HEAD_DIM

        @pl.when(n == 0)
        def _():
            carry_k[...] = jnp.zeros_like(carry_k)
            carry_v[...] = jnp.zeros_like(carry_v)
            dsink_ref[...] = jnp.zeros_like(dsink_ref)

        @pl.when(n == nb)
        def _():
            tot_k[...] = jnp.zeros_like(tot_k)
            tot_v[...] = jnp.zeros_like(tot_v)

        @pl.when(n < nb)
        def _():
            kk = jnp.concatenate([kp_ref[...], kc_ref[...]], axis=0)
            vv = jnp.concatenate([vp_ref[...], vc_ref[...]], axis=0)
            dk_tot = jnp.zeros((2 * BLOCK, LANES), F32)
            dv_tot = jnp.zeros((2 * BLOCK, LANES), F32)
            dsink = jnp.zeros((1, LANES), F32)
            for j in range(2):
                k2 = _dup_head(kk, j)
                v2 = _dup_head(vv, j)
                q = _stack_heads(q_ref[:, GROUP_WIDTH * j: GROUP_WIDTH * (j + 1)], low)
                do = _stack_heads(do_ref[:, GROUP_WIDTH * j: GROUP_WIDTH * (j + 1)], low)
                p, psink = _softmax_keys_on_sublanes(k2, q, bias_ref[0], sink_ref, j)
                dp =lax.dot_general(v2, do, _DIMS["nt"], preferred_element_type=F32)
                delta = jnp.sum(p * dp, axis=0, keepdims=True)
                ds = (p * (dp - delta)).astype(BF)
                dk2 = jnp.dot(ds, q, preferred_element_type=F32)
                dv2 = jnp.dot(p.astype(BF), do, preferred_element_type=F32)
                dq_t = lax.dot_general(k2, ds, _DIMS["tn"], preferred_element_type=F32)
                for pair, dq in enumerate(_unstack_transposed(dq_t, low)):
                    lanes = slice(GROUP_WIDTH * j + LANES * pair, GROUP_WIDTH * j + LANES * (pair + 1))
                    dq_ref[:, lanes] = dq.astype(BF)
                mine = low if j == 0 else jnp.logical_not(low)
                dk_tot = dk_tot + jnp.where(mine, dk2 + pltpu.roll(dk2, HEAD_DIM, axis=1), 0.0)
                dv_tot = dv_tot + jnp.where(mine, dv2 + pltpu.roll(dv2, HEAD_DIM, axis=1), 0.0)
                sink_term = psink * delta
                for h in range(GQA_GROUP):
                    val = -jnp.sum(sink_term[:, BLOCK * h: BLOCK * (h + 1)], axis=1, keepdims=True)
                    dsink = dsink + jnp.where(lane == j * GQA_GROUP + h, val, 0.0)
            tot_k[...] = dk_tot
            tot_v[...] = dv_tot
            dsink_ref[0:1, :] += dsink

        dk_ref[...] = (carry_k[...] + tot_k[0:BLOCK]).astype(BF)
        dv_ref[...] = (carry_v[...] + tot_v[0:BLOCK]).astype(BF)
        carry_k[...] = tot_k[BLOCK:]
        carry_v[...] = tot_v[BLOCK:]
        plumb.run(n, nb + 1, False, c_in, c_out, c_scr)

    cur = lambda n: (jnp.minimum(n, nb - 1), 0)
    prev = lambda n: (jnp.maximum(n - 1, 0), 0)
    wide = pl.BlockSpec((BLOCK, ATTN_WIDTH), cur)
    res = pl.pallas_call(
        body, name=name, grid=(nb + 1,),
        in_specs=[pl.BlockSpec(memory_space=pltpu.SMEM), BIAS_SPEC, wide, wide,
                  pl.BlockSpec((BLOCK, LANES), prev), pl.BlockSpec((BLOCK, LANES), cur),
                  pl.BlockSpec((BLOCK, LANES), lambda n: (jnp.maximum(n - 1, 0), KV_COL_BLOCK_V)),
                  pl.BlockSpec((BLOCK, LANES), lambda n: (jnp.minimum(n, nb - 1), KV_COL_BLOCK_V))] + [ANY] * plumb.n_in,
        out_specs=[wide, pl.BlockSpec((BLOCK, LANES), prev), pl.BlockSpec((BLOCK, LANES), prev),
                   pl.BlockSpec((8, LANES), lambda n: (0, 0))] + [ANY] * plumb.n_out,
        out_shape=[jax.ShapeDtypeStruct((T, ATTN_WIDTH), BF), jax.ShapeDtypeStruct((T, KV_WIDTH), BF),
                   jax.ShapeDtypeStruct((T, KV_WIDTH), BF), jax.ShapeDtypeStruct((8, LANES), F32)] + plumb.out_shapes,
        scratch_shapes=[pltpu.VMEM((BLOCK, LANES), F32), pltpu.VMEM((BLOCK, LANES), F32),
                        pltpu.VMEM((2 * BLOCK, LANES), F32), pltpu.VMEM((2 * BLOCK, LANES), F32)] + plumb.scratch,
        compiler_params=_params(("arbitrary",)),
    )(sinks, _band_bias(), dout, qn, kn, kn, proj, proj, *plumb.args)
    return list(res[:4]), plumb.split_outputs(res[4:])


def _swiglu_fwd_epilogue(accs, ex):
    g, u = accs
    return [g, u, g * jax.nn.sigmoid(g) * u], []


def _swiglu_bwd_epilogue(accs, ex):
    (da,) = accs
    g, u = ex[0].astype(F32), ex[1].astype(F32)
    s = jax.nn.sigmoid(g)
    return [da * u * (s * (1.0 + g * (1.0 - s))), da * (g * s)], []


def _residual_norm_epilogue(scale):
    def epilogue(accs, ex):
        res, gain = ex
        h = res + scale * accs[0]
        r = lax.rsqrt(jnp.mean(h * h, axis=-1, keepdims=True) + RMS_EPS)
        return [h, h * r * gain], []
    return epilogue


def _rms_bwd_epilogue(accs, ex):
    (dn,) = accs
    xv, g, dres = ex
    r = lax.rsqrt(jnp.mean(xv * xv, axis=-1, keepdims=True) + RMS_EPS)
    xhat = xv * r
    dxhat = dn * g
    dx = dres + r * (dxhat - xhat * jnp.mean(dxhat * xhat, axis=-1, keepdims=True))
    return [dx, dx], [dn * xhat]


def _loss_epilogue(accs, ex):
    xv, target = ex
    d = xv + 0.5 * accs[0] - target
    dy = d * (1.0 / D_MODEL)
    return [dy, dy], [d * d]


def _merge_fwd_epilogue(accs, ex):
    (ba,) = accs
    bp, gp_pre, ga_pre, bias_p, bias_a = ex
    gp = jax.nn.sigmoid(gp_pre.astype(F32) + bias_p)
    ga = jax.nn.sigmoid(ga_pre.astype(F32) + bias_a)
    return [gp * bp.astype(F32) + ga * ba, ba], []


def _merge_bwd_epilogue(accs, ex):
    (dm,) = accs
    bp, ba, gp_pre, ga_pre, bias_p, bias_a = ex
    gp = jax.nn.sigmoid(gp_pre.astype(F32) + bias_p)
    ga = jax.nn.sigmoid(ga_pre.astype(F32) + bias_a)
    dgp = dm * bp.astype(F32) * gp * (1.0 - gp)
    dga = dm * ba.astype(F32) * ga * (1.0 - ga)
    return [dm * gp, dm * ga, dgp, dga], [dgp, dga]


def _prep(name, ws, transposes):
    n = len(ws)

    def body(*refs):
        for w_ref, o_ref, tr in zip(refs[:n], refs[n:], transposes):
            v = w_ref[...]
            o_ref[...] = (v.T if tr else v).astype(BF)

    shapes = [jax.ShapeDtypeStruct(w.shape[::-1] if tr else w.shape, BF) for w, tr in zip(ws, transposes)]
    return pl.pallas_call(body, name=name, out_shape=shapes, compiler_params=_params())(*ws)


def _adam_math(w, g, m, v):
    m = ADAM_B1 * m + (1.0 - ADAM_B1) * g
    v = ADAM_B2 * v + (1.0 - ADAM_B2) * jnp.square(g)
    m_hat = m / (1.0 - ADAM_B1 ** ADAM_STEP)
    v_hat = v / (1.0 - ADAM_B2 ** ADAM_STEP)
    delta = -ADAM_LR * (m_hat / (jnp.sqrt(v_hat) + ADAM_EPS) + ADAM_WD * w)
    return delta, m, v


def _adamw_sharded(name, items, transpose=False):
    n = len(items)

    def body(*refs):
        ins, outs = refs[:4 * n], refs[4 * n:]
        for k in range(n):
            s_ref, w_ref, m_ref, v_ref = ins[4 * k: 4 * k + 4]
            g = s_ref[0].astype(F32)
            for i in range(1, 4):
                g = g + s_ref[i].astype(F32)
            if transpose:
                g = g.T
            delta, mn, vn = _adam_math(w_ref[...], g, m_ref[...], v_ref[...])
            for o_ref, val in zip(outs[4 * k: 4 * k + 4], (g, delta, mn, vn)):
                o_ref[...] = val

    flat = [a for item in items for a in item]
    out_shape = [jax.ShapeDtypeStruct(item[1].shape, F32) for item in items for _ in range(4)]
    _, r, C = items[0][0].shape
    rows = r // 4
    if transpose or rows % 8:
        res = pl.pallas_call(body, name=name, out_shape=out_shape, compiler_params=_params())(*flat)
    else:
        tile = pl.BlockSpec((rows, C), lambda i: (i, 0))
        res = pl.pallas_call(
            body, name=name, grid=(4,), in_specs=[pl.BlockSpec((4, rows, C), lambda i: (0, i, 0)), tile, tile, tile] * n,
            out_specs=[tile] * (4 * n), out_shape=out_shape, compiler_params=_params(("parallel",)),
        )(*flat)
    return [tuple(res[4 * k: 4 * k + 4]) for k in range(n)]


SMALL_LAYOUT = (("ffn1_norm", 0, (8, LANES)), ("mix_norm", 8, (8, LANES)), ("ffn2_norm", 16, (8, LANES)),
                ("gate_bias", 24, (16, LANES)), ("pool_scale", 40, (4, LANES)), ("q_norm", 48, (1, HEAD_DIM)),
                ("k_norm", 56, (1, HEAD_DIM)), ("sinks", 64, (1, N_HEADS)))
LOSS_ROW = 72
SMALL_ROWS = 80


def _adamw_small(name, g_vec, g_pool_w, params):
    n = len(SMALL_LAYOUT) + 1

    def body(vec_ref, pw_ref, *refs):
        ins, outs = refs[:3 * n], refs[3 * n:]
        vec = vec_ref[0]
        pw = pw_ref[0]
        for i in range(1, N_DEV):
            vec = vec + vec_ref[i]
            pw = pw + pw_ref[i]
        grads = [vec[r0:r0 + shape[0], 0:shape[1]] for _, r0, shape in SMALL_LAYOUT] + [pw]
        for p, g in enumerate(grads):
            w_ref, m_ref, v_ref = ins[3 * p: 3 * p + 3]
            delta, mn, vn = _adam_math(w_ref[...], g, m_ref[...], v_ref[...])
            for o_ref, val in zip(outs[4 * p: 4 * p + 4], (g, delta, mn, vn)):
                o_ref[...] = val
        outs[4 * n][...] = vec[LOSS_ROW:LOSS_ROW + 1, :]

    flat = [a for wmv in params for a in wmv]
    out_shape = [jax.ShapeDtypeStruct(wmv[0].shape, F32) for wmv in params for _ in range(4)]
    out_shape.append(jax.ShapeDtypeStruct((1, LANES), F32))
    res = pl.pallas_call(body, name=name, out_shape=out_shape, compiler_params=_params())(g_vec, g_pool_w, *flat)
    return [tuple(res[4 * p: 4 * p + 4]) for p in range(n)], res[4 * n]


def _place():
    x, y, c = lax.axis_index("x"), lax.axis_index("y"), lax.axis_index("c")
    other_chips = [(1 - x, y), (x, 1 - y), (1 - x, 1 - y)]
    return x, y, c, other_chips


def _rows(ref, r, place, natural=False):
    px, py, pc = place
    b = 4 * px + 2 * py + pc if natural else 4 * pc + 2 * px + py
    return ref.at[pl.ds(pl.multiple_of(b * r, 8), r), :]


def _gather_task(shards, natural=(), forward_at=0.75):
    n = len(shards)
    rs = [s.shape[0] for s in shards]
    rows_of = lambda ref, k, place: _rows(ref, rs[k], place, k in natural)

    def copy(scr, outs, k, slot, block, to, src=None):
        rows = rows_of(outs[k], k, block)
        return pltpu.make_async_remote_copy(
            src_ref=rows if src is None else src, dst_ref=rows, send_sem=scr[0].at[7 * k + slot],
            recv_sem=scr[1].at[7 * k + slot], device_id=to, device_id_type=MESH)

    def first_sends(ins, outs, scr):
        x, y, c, chips = _place()
        me = (x, y, c)
        cps = [copy(scr, outs, k, 1 + j, me, (*chip, c), src=ins[k]) for j, chip in enumerate(chips) for k in range(n)]
        return cps + [copy(scr, outs, k, 0, me, (x, y, 1 - c), src=ins[k]) for k in range(n)]

    def passed_on(outs, scr):
        x, y, c, chips = _place()
        return [copy(scr, outs, k, 4 + j, (*chip, c), (x, y, 1 - c)) for j, chip in enumerate(chips) for k in range(n)]

    def local(ins, outs, scr):
        x, y, c, _ = _place()
        return [pltpu.make_async_copy(ins[k], rows_of(outs[k], k, (x, y, c)), scr[2].at[k]) for k in range(n)]

    def start(ins, outs, scr):
        for cp in local(ins, outs, scr) + first_sends(ins, outs, scr):
            cp.start()

    def forward(ins, outs, scr):
        x, y, c, chips = _place()
        for j, chip in enumerate(chips):
            for k in range(n):
                copy(scr, outs, k, 1 + j, (*chip, c), (x, y, c)).wait_recv()
                copy(scr, outs, k, 4 + j, (*chip, c), (x, y, 1 - c)).start()

    def finish(ins, outs, scr):
        x, y, c, chips = _place()
        for k in range(n):
            copy(scr, outs, k, 0, (x, y, 1 - c), (x, y, c)).wait_recv()
        for j, chip in enumerate(chips):
            for k in range(n):
                copy(scr, outs, k, 4 + j, (*chip, 1 - c), (x, y, c)).wait_recv()
        for cp in first_sends(ins, outs, scr) + passed_on(outs, scr):
            cp.wait_send()
        for cp in local(ins, outs, scr):
            cp.wait()

    out_shapes = [jax.ShapeDtypeStruct((N_DEV * s.shape[0], s.shape[1]), s.dtype) for s in shards]
    scratch = [pltpu.SemaphoreType.DMA((7 * n,)), pltpu.SemaphoreType.DMA((7 * n,)), pltpu.SemaphoreType.DMA((n,))]
    return _Task(shards, out_shapes, scratch, [(0, start), (forward_at, forward), (1.0, finish)])


def _direct_gather_task(shards):
    n = len(shards)
    rs = [s.shape[0] for s in shards]

    def peers():
        x, y, c, _ = _place()
        flip = lambda v, bit: 1 - v if bit else v
        return (x, y, c), [(flip(x, (s >> 2) & 1), flip(y, (s >> 1) & 1), flip(c, s & 1)) for s in range(1, N_DEV)]

    def copies(ins, outs, scr):
        me, others = peers()
        local = [pltpu.make_async_copy(ins[k], _rows(outs[k], rs[k], me), scr[2].at[k]) for k in range(n)]
        sems = lambda k, s: dict(send_sem=scr[0].at[7 * k + s], recv_sem=scr[1].at[7 * k + s], device_id_type=MESH)
        sends = [pltpu.make_async_remote_copy(src_ref=ins[k], dst_ref=_rows(outs[k], rs[k], me), device_id=to, **sems(k, s))
                 for s, to in enumerate(others) for k in range(n)]
        recvs = [pltpu.make_async_remote_copy(src_ref=_rows(outs[k], rs[k], frm), dst_ref=_rows(outs[k], rs[k], frm),
                                              device_id=me, **sems(k, s))
                 for s, frm in enumerate(others) for k in range(n)]
        return local, sends, recvs

    def start(ins, outs, scr):
        local, sends, _ = copies(ins, outs, scr)
        for cp in local + sends:
            cp.start()

    def finish(ins, outs, scr):
        local, sends, recvs = copies(ins, outs, scr)
        for cp in recvs:
            cp.wait_recv()
        for cp in sends:
            cp.wait_send()
        for cp in local:
            cp.wait()

    out_shapes = [jax.ShapeDtypeStruct((N_DEV * s.shape[0], s.shape[1]), s.dtype) for s in shards]
    scratch = [pltpu.SemaphoreType.DMA((7 * n,)), pltpu.SemaphoreType.DMA((7 * n,)), pltpu.SemaphoreType.DMA((n,))]
    return _Task(shards, out_shapes, scratch, [(0, start), (1.0, finish)])


def _chip_task(sums):
    n = len(sums)
    rs = [s.shape[0] // 4 for s in sums]

    def block(ref, k, chip_index):
        return ref.at[pl.ds(pl.multiple_of(chip_index * rs[k], 8), rs[k]), :]

    def copies(ins, outs, scr):
        send_sems, recv_sems, local_sems = scr
        x, y, c, chips = _place()
        here = 2 * x + y
        local = [pltpu.make_async_copy(block(ins[k], k, here), outs[k].at[here], local_sems.at[k]) for k in range(n)]
        remote = []
        for j, (px, py) in enumerate(chips):
            remote += [pltpu.make_async_remote_copy(
                src_ref=block(ins[k], k, 2 * px + py), dst_ref=outs[k].at[here],
                send_sem=send_sems.at[3 * k + j], recv_sem=recv_sems.at[3 * k + j],
                device_id=(px, py, c), device_id_type=MESH) for k in range(n)]
        return local, remote

    def start(ins, outs, scr):
        local, remote = copies(ins, outs, scr)
        for cp in local + remote:
            cp.start()

    def finish(ins, outs, scr):
        local, remote = copies(ins, outs, scr)
        for cp in remote:
            cp.wait()
        for cp in local:
            cp.wait()

    out_shapes = [jax.ShapeDtypeStruct((4, r, s.shape[1]), s.dtype) for r, s in zip(rs, sums)]
    scratch = [pltpu.SemaphoreType.DMA((3 * n,)), pltpu.SemaphoreType.DMA((3 * n,)), pltpu.SemaphoreType.DMA((n,))]
    return _Task(sums, out_shapes, scratch, [(0, start), (1.0, finish)])


def _dw_pair(name, a, b, scale, comm=None, blocks=1):
    T, M = a.shape
    N = b.shape[1]
    half = M // 2
    wide = half // blocks
    tk = min(2048, T)
    nK = T // tk
    plumb = _CommPlumbing(comm)

    def body(core_ref, *rest):
        a_refs, b_ref, rest = rest[:blocks], rest[blocks], rest[blocks + 1:]
        c_in = rest[:plumb.n_in]
        o_ref = rest[plumb.n_in]
        c_out = rest[plumb.n_in + 1: plumb.n_in + 1 + plumb.n_out]
        acc, stage, land, send_sem, recv_sem = rest[plumb.n_in + 1 + plumb.n_out: plumb.n_in + 6 + plumb.n_out]
        c_scr = rest[plumb.n_in + 6 + plumb.n_out:]
        i, k = pl.program_id(0), pl.program_id(1)
        x, y, c, _ = _place()
        push = pltpu.make_async_remote_copy(src_ref=stage, dst_ref=land, send_sem=send_sem, recv_sem=recv_sem,
                                            device_id=(x, y, 1 - c), device_id_type=MESH)
        if comm:
            plumb.run(i * nK + k, 2 * nK, True, c_in, c_out, c_scr)

        av = a_refs[0][...] if blocks == 1 else jnp.concatenate([r[...] for r in a_refs], axis=1)
        p = lax.dot_general(av, b_ref[...], _DIMS["tn"], preferred_element_type=F32)

        @pl.when(k == 0)
        def _():
            acc[...] = p

        @pl.when(k > 0)
        def _():
            acc[...] += p

        @pl.when((i == 0) & (k == nK - 1))
        def _():
            stage[...] = (scale * acc[...]).astype(BF)
            push.start()

        @pl.when((i == 1) & (k == nK - 1))
        def _():
            push.wait_recv()
            o_ref[...] = (scale * acc[...] + land[...].astype(F32)).astype(BF)
            push.wait_send()

        if comm:
            plumb.run(i * nK + k, 2 * nK, False, c_in, c_out, c_scr)

    grid_spec = pltpu.PrefetchScalarGridSpec(
        num_scalar_prefetch=1, grid=(2, nK),
        in_specs=[pl.BlockSpec((tk, wide), functools.partial(
            lambda i, k, core, j: (k, (2 * j if blocks > 1 else 0) + jnp.where(i == 0, 1 - core[0], core[0])), j=j))
            for j in range(blocks)] + [pl.BlockSpec((tk, N), lambda i, k, core: (k, 0))] + [ANY] * plumb.n_in,
        out_specs=[pl.BlockSpec((half, N), lambda i, k, core: (0, 0))] + [ANY] * plumb.n_out,
        scratch_shapes=[pltpu.VMEM((half, N), F32), pltpu.VMEM((half, N), BF), pltpu.VMEM((half, N), BF),
                        pltpu.SemaphoreType.DMA, pltpu.SemaphoreType.DMA] + plumb.scratch)
    core = lax.axis_index("c").astype(jnp.int32).reshape(1)
    res = pl.pallas_call(
        body, name=name, grid_spec=grid_spec,
        out_shape=[jax.ShapeDtypeStruct((half, N), BF)] + plumb.out_shapes,
        compiler_params=_params(("arbitrary", "arbitrary")),
    )(core, *([a] * blocks), b, *plumb.args)
    return (res[0], plumb.split_outputs(res[1:])) if comm else res[0]


def _pair_task(parts):
    n = len(parts)

    def copies(ins, outs, scr):
        x, y, c, _ = _place()
        return [pltpu.make_async_remote_copy(
            src_ref=ins[k].at[:, pl.ds(1 - c, 1)], dst_ref=outs[k], send_sem=scr[0].at[k], recv_sem=scr[1].at[k],
            device_id=(x, y, 1 - c), device_id_type=MESH) for k in range(n)]

    def start(ins, outs, scr):
        for cp in copies(ins, outs, scr):
            cp.start()

    def finish(ins, outs, scr):
        for cp in copies(ins, outs, scr):
            cp.wait()

    out_shapes = [jax.ShapeDtypeStruct((4, 1) + p.shape[2:], p.dtype) for p in parts]
    scratch = [pltpu.SemaphoreType.DMA((n,)), pltpu.SemaphoreType.DMA((n,))]
    return _Task(parts, out_shapes, scratch, [(0, start), (1.0, finish)])


def _pair_sum(name, part, got, core):
    _, _, r, C = part.shape

    def body(core_ref, p_ref, g_ref, o_ref):
        o_ref[0] = (p_ref[0, 0].astype(F32) + g_ref[0, 0].astype(F32)).astype(o_ref.dtype)

    return pl.pallas_call(
        body, name=name,
        grid_spec=pltpu.PrefetchScalarGridSpec(
            num_scalar_prefetch=1, grid=(4,),
            in_specs=[pl.BlockSpec((1, 1, r, C), lambda i, core_ref: (i, core_ref[0], 0, 0)),
                      pl.BlockSpec((1, 1, r, C), lambda i, core_ref: (i, 0, 0, 0))],
            out_specs=pl.BlockSpec((1, r, C), lambda i, core_ref: (i, 0, 0))),
        out_shape=jax.ShapeDtypeStruct((4, r, C), part.dtype), compiler_params=_params(("parallel",)),
    )(core, part, got)


def _ffn_bwd(tag, dy, dyb, x, gain, wgT, wuT, wd, saved, earlier=None):
    n, g, u, a = saved
    half = lambda accs, ex: _swiglu_bwd_epilogue([0.5 * accs[0]], ex)
    act_args = dict(tm=512, tn=1408, tk=D_MODEL, epilogue=half, extras=[(g, "tile", 0), (u, "tile", 0)], cols_outer=True)
    if earlier is None:
        sum_d = _dw_pair(tag + "_dw_down", a, dyb, 0.5)
        (dg, du), ((slots_d,),) = _mm(tag + "_d_act", [(dyb, wd, "nt", 0)], [BF, BF], comm=[_chip_task([sum_d])], **act_args)
        slots_e = None
        sum_g = _dw_pair(tag + "_dw_gate", dg, n, 1.0)
    else:
        sum_d, ((got,),) = _dw_pair(tag + "_dw_down", a, dyb, 0.5, comm=[_pair_task([earlier])])
        core = lax.axis_index("c").astype(jnp.int32).reshape(1)
        sum_e = _pair_sum(tag + "_pair_sum_earlier", earlier, got, core)
        sum_e = sum_e.reshape(4 * sum_e.shape[1], sum_e.shape[2])
        (dg, du), ((slots_e,),) = _mm(tag + "_d_act", [(dyb, wd, "nt", 0)], [BF, BF], comm=[_chip_task([sum_e])], **act_args)
        sum_g, ((slots_d,),) = _dw_pair(tag + "_dw_gate", dg, n, 1.0, comm=[_chip_task([sum_d])])
    sum_u, ((slots_g,),) = _dw_pair(tag + "_dw_up", du, n, 1.0, comm=[_chip_task([sum_g])])
    (dx, dxb, dgain), ((slots_u,),) = _mm(
        tag + "_d_norm", [(dg, wgT, "nn", 0), (du, wuT, "nn", 0)], [F32, BF], tm=512, tn=D_MODEL, tk=D_FF,
        epilogue=_rms_bwd_epilogue, extras=[(x, "tile", 0), (gain, "row", 0), (dy, "tile", 0)], n_colsum=1,
        comm=[_chip_task([sum_u])])
    return dx, dxb, dgain, slots_e, slots_g, slots_u, slots_d


def _tile_gain(g):
    return jnp.concatenate([g, g]).reshape(1, LANES)


def _fold_heads(partials):
    return jnp.sum(partials.reshape(-1, HEAD_DIM), axis=0)


def _pack_small_grads(grads, loss_local):
    pieces, row = [], 0
    for name, r0, _ in SMALL_LAYOUT + (("loss", LOSS_ROW, None),):
        v = (loss_local if name == "loss" else grads[name]).reshape(-1)
        rows = -(-v.size // LANES)
        block = jnp.pad(v, (0, rows * LANES - v.size)).reshape(rows, LANES)
        pieces += [jnp.zeros((r0 - row, LANES), F32)] * (r0 > row) + [block]
        row = r0 + rows
    pieces.append(jnp.zeros((SMALL_ROWS - row, LANES), F32))
    return jnp.concatenate(pieces, axis=0)


def kernel(x, ffn1_norm, ffn1_w_gate, ffn1_w_up, ffn1_w_down, mix_norm, w_in, pool_w, pool_scale, w_pool_out, q_norm, k_norm, sinks, w_attn_out, gate_bias, w_out, ffn2_norm, ffn2_w_gate, ffn2_w_up, ffn2_w_down, loss_target, m_ffn1_norm, m_ffn1_w_gate, m_ffn1_w_up, m_ffn1_w_down, m_mix_norm, m_w_in, m_pool_w, m_pool_scale, m_w_pool_out, m_q_norm, m_k_norm, m_sinks, m_w_attn_out, m_gate_bias, m_w_out, m_ffn2_norm, m_ffn2_w_gate, m_ffn2_w_up, m_ffn2_w_down, v_ffn1_norm, v_ffn1_w_gate, v_ffn1_w_up, v_ffn1_w_down, v_mix_norm, v_w_in, v_pool_w, v_pool_scale, v_w_pool_out, v_q_norm, v_k_norm, v_sinks, v_w_attn_out, v_gate_bias, v_w_out, v_ffn2_norm, v_ffn2_w_gate, v_ffn2_w_up, v_ffn2_w_down):
    T = x.shape[1]
    x2 = x.reshape(T, D_MODEL)
    target = loss_target.reshape(T, D_MODEL)

    big = [
        ("ffn1_w_gate", ffn1_w_gate, m_ffn1_w_gate, v_ffn1_w_gate, True, False),
        ("ffn1_w_up", ffn1_w_up, m_ffn1_w_up, v_ffn1_w_up, True, False),
        ("ffn1_w_down", ffn1_w_down, m_ffn1_w_down, v_ffn1_w_down, False, False),
        ("w_in", w_in, m_w_in, v_w_in, True, False),
        ("w_pool_out", w_pool_out, m_w_pool_out, v_w_pool_out, False, True),
        ("w_attn_out", w_attn_out, m_w_attn_out, v_w_attn_out, False, False),
        ("w_out", w_out, m_w_out, v_w_out, False, False),
        ("ffn2_w_gate", ffn2_w_gate, m_ffn2_w_gate, v_ffn2_w_gate, True, False),
        ("ffn2_w_up", ffn2_w_up, m_ffn2_w_up, v_ffn2_w_up, True, False),
        ("ffn2_w_down", ffn2_w_down, m_ffn2_w_down, v_ffn2_w_down, False, False),
    ]
    view = lambda a, tv: a.T if tv else a
    shards = _prep("prep_weights", [view(w, tv) for _, w, _, _, tv, _ in big], [tk_ for *_, tk_ in big])
    g1 =ffn1_norm.reshape(1, D_MODEL)
    g2 = mix_norm.reshape(1, D_MODEL)
    g3 = ffn2_norm.reshape(1, D_MODEL)
    bias_row = gate_bias.reshape(1, 2 * D_MODEL)
    qg, kg = _tile_gain(q_norm) * ATTN_SCALE, _tile_gain(k_norm)
    scale_row = pool_scale.reshape(1, POOL_WIDTH)

    n1, ((wg1T, wu1T),) = _rms_fwd("ffn1_norm", x2, g1, [_gather_task(shards[0:2], forward_at=0.9)])
    (gt1, up1, act1), ((wd1,), (w_inT,)) = _mm(
        "ffn1_gate_up", [(n1, wg1T, "nt", 0), (n1, wu1T, "nt", 1)], [BF, BF, BF], tm=512, tn=1408, tk=D_MODEL,
        epilogue=_swiglu_fwd_epilogue, cols_outer=True,
        comm=[_gather_task(shards[2:3], forward_at=0.5), _gather_task(shards[3:4], natural=(0,), forward_at=0.9)])
    (h1, u), ((w_poT, w_ao, w_o),) = _mm(
        "ffn1_down", [(act1, wd1, "nn", 0)], [F32, BF], tm=512, tn=D_MODEL, tk=D_FF,
        epilogue=_residual_norm_epilogue(0.5), extras=[(x2, "tile", 0), (g2, "row", 0)],
        comm=[_gather_task(shards[4:7], natural=(0, 1, 2), forward_at=0.8)])
    saved1 = (n1, gt1, up1, act1)
    (proj,), ((wg2T,),) = _mm(
        "in_proj", [(u, w_inT, "nt", 0)], [BF], tm=512, tn=1280, tk=D_MODEL, cols_outer=True,
        comm=[_gather_task(shards[7:8], forward_at=0.8)])
    pooled, mixed = _pool_fwd("pool_fwd", proj, pool_w, scale_row)
    qn = _headnorm_fwd("q_norm", proj, COL_Q, ATTN_WIDTH, qg)
    kn = _headnorm_fwd("k_norm", proj, COL_K, KV_WIDTH, kg)
    attn, ((wu2T,),) = _attn_fwd("attn_fwd", qn, kn, proj, sinks, comm=[_gather_task(shards[8:9], forward_at=0.8)])
    (bp,) = _mm("pool_out", [(mixed, w_poT, "nt", 0)], [BF], tm=1024, tn=D_MODEL, tk=POOL_WIDTH)
    gate_tn = 256
    gate_extras = [(proj, "tile", COL_GP // gate_tn), (proj, "tile", COL_GA // gate_tn),
                   (bias_row, "row", 0), (bias_row, "row", D_MODEL // gate_tn)]
    merged, ba = _mm("attn_out_merge", [(attn, w_ao, "nn", 0)], [BF, BF], tm=2048, tn=gate_tn, tk=ATTN_WIDTH,
                     epilogue=_merge_fwd_epilogue, extras=[(bp, "tile", 0)] + gate_extras)
    h2, n2 = _mm("mix_out", [(merged, w_o, "nn", 0)], [F32, BF], tm=512, tn=D_MODEL, tk=D_MODEL,
                 epilogue=_residual_norm_epilogue(1.0), extras=[(h1, "tile", 0), (g3, "row", 0)])
    (gt2, up2, act2), ((wd2,),) = _mm(
        "ffn2_gate_up", [(n2, wg2T, "nt", 0), (n2, wu2T, "nt", 1)], [BF, BF, BF], tm=512, tn=1408, tk=D_MODEL,
        epilogue=_swiglu_fwd_epilogue, cols_outer=True, comm=[_gather_task(shards[9:10], forward_at=0.8)])
    dy, dyb, sq = _mm("ffn2_down_loss", [(act2, wd2, "nn", 0)], [F32, BF], tm=512, tn=D_MODEL, tk=D_FF,
                      epilogue=_loss_epilogue, extras=[(h2, "tile", 0), (target, "tile", 0)], n_colsum=1)
    loss_local = 0.5 * jnp.sum(sq) / D_MODEL

    dh2, dh2b, dg3, _, slots_g2, slots_u2, slots_d2 = _ffn_bwd(
        "ffn2", dy, dyb, h2, g3, wg2T, wu2T, wd2, (n2, gt2, up2, act2))
    dbp, dba, dgp, dga, cs_gp, cs_ga = _mm(
        "mix_out_bwd", [(dh2b, w_o, "nt", 0)], [BF, BF, BF, BF], tm=2048, tn=gate_tn, tk=D_MODEL,
        epilogue=_merge_bwd_epilogue, extras=[(bp, "tile", 0), (ba, "tile", 0)] + gate_extras, n_colsum=2)
    sum_o = _dw_pair("dw_out", merged, dh2b, 1.0, blocks=4)
    (dmixed,) = _mm("pool_out_bwd", [(dbp, w_poT, "nn", 0)], [BF], tm=1024, tn=POOL_WIDTH, tk=D_MODEL)
    sum_po = _dw_pair("dw_pool_out", dbp, mixed, 1.0, blocks=4)
    (dattn,) = _mm("attn_out_bwd", [(dba, w_ao, "nt", 0)], [BF], tm=1024, tn=ATTN_WIDTH, tk=D_MODEL)
    sum_ao = _dw_pair("dw_attn_out", attn, dba, 1.0, blocks=4)
    dxp, dpool_w, dpool_scale = _pool_bwd("pool_bwd", dmixed, pooled, pool_w, scale_row)
    (dqn, dkn, dv, dsink_tile), ((slots_o, slots_po, slots_ao),) = _attn_bwd(
        "attn_bwd", dattn, qn, kn, proj, sinks, [_chip_task([sum_o, sum_po, sum_ao])])
    dq, dqg = _headnorm_bwd("q_norm_bwd", dqn, proj, COL_Q, ATTN_WIDTH, qg)
    dk, dkg = _headnorm_bwd("k_norm_bwd", dkn, proj, COL_K, KV_WIDTH, kg)
    dproj = jnp.concatenate([dxp, dq, dk, dv, dgp, dga], axis=1)
    (dh1, dh1b, dg2), ((g_pool_w,),) = _mm(
        "in_proj_bwd", [(dproj, w_inT, "nn", 0)], [F32, BF], tm=512, tn=D_MODEL, tk=IN_WIDTH, epilogue=_rms_bwd_epilogue,
        extras=[(h1, "tile", 0), (g2, "row", 0), (dh2, "tile", 0)], n_colsum=1,
        comm=[_gather_task([dpool_w.reshape(-1, LANES)])])
    (dw_inT,) = _mm("dw_in", [(dproj, u, "tn", 0)], [BF], tm=1280, tn=D_MODEL, tk=2048)
    dx, _, dg1, slots_in, slots_g1, slots_u1, slots_d1 = _ffn_bwd(
        "ffn1", dh1, dh1b, x2, g1, wg1T, wu1T, wd1, saved1, dw_inT.reshape(4, 2, IN_WIDTH // N_DEV, D_MODEL))

    slots = [slots_g1, slots_u1, slots_d1, slots_in, slots_po, slots_ao, slots_o, slots_g2, slots_u2, slots_d2]
    big_out = {}
    for label, group in (("ffn", (0, 1, 2, 7, 8, 9)), ("w_in", (3,)), ("w_pool_out", (4,)), ("attn_out_and_out", (5, 6))):
        items = [(slots[k], view(big[k][1], big[k][4]), view(big[k][2], big[k][4]), view(big[k][3], big[k][4]))
                 for k in group]
        for k, res in zip(group, _adamw_sharded("adamw_" + label, items, transpose=big[group[0]][5])):
            big_out[big[k][0]] = tuple(view(r, big[k][4]) for r in res)

    small_grads = {
        "ffn1_norm": jnp.sum(dg1, axis=(0, 1)), "mix_norm": jnp.sum(dg2, axis=(0, 1)), "ffn2_norm": jnp.sum(dg3, axis=(0, 1)),
        "gate_bias": jnp.concatenate([jnp.sum(cs_gp, axis=(0, 1)), jnp.sum(cs_ga, axis=(0, 1))]),
        "pool_scale": dpool_scale, "q_norm": _fold_heads(dqg) * ATTN_SCALE, "k_norm": _fold_heads(dkg),
        "sinks": dsink_tile[0, :N_HEADS]}
    ((g_vec,),) = _comm_only("gather_small_grads", [_direct_gather_task([_pack_small_grads(small_grads, loss_local)])])
    given = {"ffn1_norm": (ffn1_norm, m_ffn1_norm, v_ffn1_norm), "mix_norm": (mix_norm, m_mix_norm, v_mix_norm),
             "ffn2_norm": (ffn2_norm, m_ffn2_norm, v_ffn2_norm), "gate_bias": (gate_bias, m_gate_bias, v_gate_bias),
             "pool_scale": (pool_scale, m_pool_scale, v_pool_scale), "q_norm": (q_norm, m_q_norm, v_q_norm),
             "k_norm": (k_norm, m_k_norm, v_k_norm), "sinks": (sinks, m_sinks, v_sinks)}
    params = [tuple(a.reshape(shape) for a in given[nm]) for nm, _, shape in SMALL_LAYOUT]
    params.append(tuple(a.reshape(-1, LANES) for a in (pool_w, m_pool_w, v_pool_w)))
    small_res, loss_row = _adamw_small("adamw_small", g_vec.reshape(N_DEV, SMALL_ROWS, LANES),
                                       g_pool_w.reshape(N_DEV, -1, LANES), params)
    small_out = {nm: tuple(r.reshape(given[nm][0].shape) for r in res)
                 for (nm, _, _), res in zip(SMALL_LAYOUT, small_res)}
    small_out["pool_w"] = tuple(r.reshape(pool_w.shape) for r in small_res[-1])
    loss = loss_row[0, 0]

    order = ["ffn1_norm", "ffn1_w_gate", "ffn1_w_up", "ffn1_w_down", "mix_norm", "w_in", "pool_w", "pool_scale",
             "w_pool_out", "q_norm", "k_norm", "sinks", "w_attn_out", "gate_bias", "w_out", "ffn2_norm",
             "ffn2_w_gate", "ffn2_w_up", "ffn2_w_down"]
    every = {**big_out, **small_out}
    outs = [loss, dx.reshape(x.shape)]
    for j in range(4):
        outs += [every[nm][j] for nm in order]
    return tuple(outs)
```

```python
import functools

import jax
import jax.numpy as jnp
from jax import lax
from jax.experimental import pallas as pl
from jax.experimental.pallas import tpu as pltpu

BF = jnp.bfloat16
F32 = jnp.float32

D_MODEL = 1024
D_FF = 2816
POOL_WIDTH = 512
POOL_GROUP = 128
N_POOL_GROUPS = 4
HEAD_DIM = 64
N_HEADS = 16
GQA_GROUP = 8
BLOCK = 128
ATTN_WIDTH = 1024
KV_WIDTH = 128
IN_WIDTH = 3840
RMS_EPS = 1e-6
N_DEV = 8
LANES = 128

COL_Q = POOL_WIDTH
COL_K = COL_Q + ATTN_WIDTH
COL_V = COL_K + KV_WIDTH
COL_GP = COL_V + KV_WIDTH
COL_GA = COL_GP + D_MODEL

ADAM_LR = 0.001
ADAM_B1 = 0.9
ADAM_B2 = 0.999
ADAM_EPS = 1e-08
ADAM_WD = 0.01
ADAM_STEP = 10

VMEM_LIMIT_V7X = 56 * 1024 * 1024
MESH = pl.DeviceIdType.MESH
ANY = pl.BlockSpec(memory_space=pl.ANY)


def _params(sem=None):
    return pltpu.CompilerParams(dimension_semantics=sem, vmem_limit_bytes=VMEM_LIMIT_V7X)


_DIMS = {"nt": (((1,), (1,)), ((), ())), "nn": (((1,), (0,)), ((), ())), "tn": (((0,), (0,)), ((), ()))}


class _Task:
    def __init__(self, inputs, out_shapes, scratch, phases):
        self.inputs, self.out_shapes, self.scratch = list(inputs), list(out_shapes), list(scratch)
        self.phases = list(phases)


class _CommPlumbing:
    def __init__(self, tasks):
        self.tasks = list(tasks or [])
        self.args = [a for t in self.tasks for a in t.inputs]
        self.out_shapes = [o for t in self.tasks for o in t.out_shapes]
        self.scratch = [s for t in self.tasks for s in t.scratch]
        self.n_in, self.n_out = len(self.args), len(self.out_shapes)

    def _slices(self, c_in, c_out, c_scr):
        i = o = s = 0
        for t in self.tasks:
            yield t, c_in[i:i + len(t.inputs)], c_out[o:o + len(t.out_shapes)], c_scr[s:s + len(t.scratch)]
            i, o, s = i + len(t.inputs), o + len(t.out_shapes), s + len(t.scratch)

    def run(self, step, steps, before, c_in, c_out, c_scr):
        for t, ins, outs, scr in self._slices(c_in, c_out, c_scr):
            for frac, fn in t.phases:
                if step is None:
                    fn(ins, outs, scr)
                elif before == (frac == 0):
                    at = 0 if frac == 0 else max(0, min(steps, -(-int(round(frac * steps * 64)) // 64)) - 1)
                    pl.when(step == at)(functools.partial(fn, ins, outs, scr))

    def split_outputs(self, flat):
        res, o = [], 0
        for t in self.tasks:
            res.append(list(flat[o:o + len(t.out_shapes)]))
            o += len(t.out_shapes)
        return res


def _comm_only(name, tasks):
    plumb = _CommPlumbing(tasks)

    def body(*refs):
        c_in, c_out = refs[:plumb.n_in], refs[plumb.n_in: plumb.n_in + plumb.n_out]
        c_scr = refs[plumb.n_in + plumb.n_out:]
        plumb.run(None, 1, True, c_in, c_out, c_scr)

    res = pl.pallas_call(
        body, name=name, in_specs=[ANY] * plumb.n_in, out_specs=[ANY] * plumb.n_out, out_shape=plumb.out_shapes,
        scratch_shapes=plumb.scratch, compiler_params=pltpu.CompilerParams(has_side_effects=True),
    )(*plumb.args)
    return plumb.split_outputs(res)


def _mm(name, terms, out_dtypes, *, tm, tn, tk, epilogue=None, extras=(), n_colsum=0, comm=None, cols_outer=False):
    a0, b0, mode0, _ = terms[0]
    if mode0 == "nt":
        (M, K), N = a0.shape, b0.shape[0]
    elif mode0 == "nn":
        (M, K), N = a0.shape, b0.shape[1]
    else:
        (K, M), N = a0.shape, b0.shape[1]
    tm, tn, tk = min(tm, M), min(tn, N), min(tk, K)
    assert M % tm == 0 and N % tn == 0 and K % tk == 0, (name, M, N, K, tm, tn, tk)
    nI, nJ, nK = M // tm, N // tn, K // tk
    n_terms = len(terms)
    n_acc = max(t[3] for t in terms) + 1
    n_ex = len(extras)
    n_out = len(out_dtypes)
    if epilogue is None:
        epilogue = lambda accs, ex: ([accs[0]], [])
    plumb = _CommPlumbing(comm)
    n_scr = n_acc if nK > 1 else 0
    grid = (nJ, nI, nK) if cols_outer else (nI, nJ, nK)

    def body(*refs):
        n_in = 2 * n_terms + n_ex
        ab = refs[: 2 * n_terms]
        ex_refs = refs[2 * n_terms: n_in]
        c_in = refs[n_in: n_in + plumb.n_in]
        o0 = n_in + plumb.n_in
        out_refs = refs[o0: o0 + n_out]
        cs_refs = refs[o0 + n_out: o0 + n_out + n_colsum]
        c_out = refs[o0 + n_out + n_colsum: o0 + n_out + n_colsum + plumb.n_out]
        s0 = o0 + n_out + n_colsum + plumb.n_out
        acc_refs = refs[s0: s0 + n_scr]
        c_scr = refs[s0 + n_scr:]
        steps = grid[0] * grid[1] * nK
        if comm:
            step = (pl.program_id(0) * grid[1] + pl.program_id(1)) * nK + pl.program_id(2)
            plumb.run(step, steps, True, c_in, c_out, c_scr)

        def products():
            accs = [None] * n_acc
            for t, (_, _, mode, ai) in enumerate(terms):
                p = lax.dot_general(ab[2 * t][...], ab[2 * t + 1][...], _DIMS[mode], preferred_element_type=F32)
                accs[ai] = p if accs[ai] is None else accs[ai] + p
            return accs

        def finish(accs):
            outs, colsums = epilogue(accs, [r[...] for r in ex_refs])
            for r, o in zip(out_refs, outs):
                r[...] = o.astype(r.dtype)
            for r, cs in zip(cs_refs, colsums):
                r[...] = jnp.sum(cs, axis=0, keepdims=True).reshape(r.shape)

        if nK == 1:
            finish(products())
        else:
            k = pl.program_id(2)
            accs = products()

            @pl.when(k == 0)
            def _():
                for r, a in zip(acc_refs, accs):
                    r[...] = a

            @pl.when(k > 0)
            def _():
                for r, a in zip(acc_refs, accs):
                    r[...] += a

            @pl.when(k == nK - 1)
            def _():
                finish([r[...] for r in acc_refs])

        if comm:
            plumb.run(step, steps, False, c_in, c_out, c_scr)

    def spec(block, index, fixed=False):
        imap = (lambda q, p, k: index(p, q, k)) if cols_outer else index
        return pl.BlockSpec(block, imap, pipeline_mode=pl.Buffered(1)) if fixed else pl.BlockSpec(block, imap)

    in_specs, args = [], []
    for a, b, mode, _ in terms:
        if mode == "nt":
            in_specs += [spec((tm, tk), lambda i, j, k: (i, k), nI * nK == 1),
                         spec((tn, tk), lambda i, j, k: (j, k), nJ * nK == 1)]
        elif mode == "nn":
            in_specs += [spec((tm, tk), lambda i, j, k: (i, k), nI * nK == 1),
                         spec((tk, tn), lambda i, j, k: (k, j), nJ * nK == 1)]
        else:
            in_specs += [spec((tk, tm), lambda i, j, k: (k, i), nI * nK == 1),
                         spec((tk, tn), lambda i, j, k: (k, j), nJ * nK == 1)]
        args += [a, b]
    for arr, kind, off in extras:
        if kind == "tile":
            in_specs.append(spec((tm, tn), functools.partial(lambda i, j, k, off: (i, j + off), off=off)))
        else:
            in_specs.append(spec((1, tn), functools.partial(lambda i, j, k, off: (0, j + off), off=off)))
        args.append(arr)
    out_shape = [jax.ShapeDtypeStruct((M, N), dt) for dt in out_dtypes]
    out_specs = [spec((tm, tn), lambda i, j, k: (i, j)) for _ in out_dtypes]
    out_shape += [jax.ShapeDtypeStruct((nI, 1, N), F32) for _ in range(n_colsum)]
    out_specs += [spec((1, 1, tn), lambda i, j, k: (i, 0, j)) for _ in range(n_colsum)]
    scratch = [pltpu.VMEM((tm, tn), F32) for _ in range(n_scr)]
    args += plumb.args
    in_specs += [ANY] * plumb.n_in
    out_shape += plumb.out_shapes
    out_specs += [ANY] * plumb.n_out
    sem = ("arbitrary",) * 3 if comm else ("parallel", "parallel", "arbitrary")
    res = pl.pallas_call(
        body, name=name, grid=grid, in_specs=in_specs, out_specs=out_specs, out_shape=out_shape,
        scratch_shapes=scratch + plumb.scratch, compiler_params=_params(sem),
    )(*args)
    n_own = n_out + n_colsum
    return (list(res[:n_own]), plumb.split_outputs(res[n_own:])) if comm is not None else res


ROW_TILE = 512


def _rms_fwd(name, x, g, comm):
    T, D = x.shape
    steps = T // ROW_TILE
    plumb = _CommPlumbing(comm)

    def body(x_ref, g_ref, *rest):
        c_in, o_ref = rest[:plumb.n_in], rest[plumb.n_in]
        c_out, c_scr = rest[plumb.n_in + 1: plumb.n_in + 1 + plumb.n_out], rest[plumb.n_in + 1 + plumb.n_out:]
        plumb.run(pl.program_id(0), steps, True, c_in, c_out, c_scr)
        xv = x_ref[...]
        r = lax.rsqrt(jnp.mean(xv * xv, axis=-1, keepdims=True) + RMS_EPS)
        o_ref[...] = (xv * r * g_ref[...]).astype(BF)
        plumb.run(pl.program_id(0), steps, False, c_in, c_out, c_scr)

    row = pl.BlockSpec((ROW_TILE, D), lambda i: (i, 0))
    res = pl.pallas_call(
        body, name=name, grid=(steps,),
        in_specs=[row, pl.BlockSpec((1, D), lambda i: (0, 0))] + [ANY] * plumb.n_in,
        out_specs=[row] + [ANY] * plumb.n_out, out_shape=[jax.ShapeDtypeStruct((T, D), BF)] + plumb.out_shapes,
        scratch_shapes=plumb.scratch, compiler_params=_params(("arbitrary",)),
    )(x, g, *plumb.args)
    return res[0], plumb.split_outputs(res[1:])


HEADNORM_TILE = 1024


def _half_sum_matrix():
    r = lax.broadcasted_iota(jnp.int32, (LANES, LANES), 0) // HEAD_DIM
    c = lax.broadcasted_iota(jnp.int32, (LANES, LANES), 1) // HEAD_DIM
    return (r == c).astype(BF)


def _head_mean(v, ones_blockdiag):
    hi = v.astype(BF)
    lo = (v - hi.astype(F32)).astype(BF)
    s = jnp.dot(hi, ones_blockdiag, preferred_element_type=F32) + jnp.dot(lo, ones_blockdiag, preferred_element_type=F32)
    return s * (1.0 / HEAD_DIM)


def _headnorm_fwd(name, proj, col0, width, g2):
    T = proj.shape[0]
    wide = min(width, GROUP_WIDTH)
    nb, off = width // wide, col0 // wide

    def body(x_ref, g_ref, b_ref, o_ref):
        for s in range(wide // LANES):
            lanes = slice(LANES * s, LANES * (s + 1))
            xv = x_ref[:, lanes].astype(F32)
            r = lax.rsqrt(_head_mean(xv * xv, b_ref[...]) + RMS_EPS)
            o_ref[:, lanes] = (xv * r * g_ref[...]).astype(BF)

    return pl.pallas_call(
        body, name=name, grid=(T // HEADNORM_TILE, nb),
        in_specs=[pl.BlockSpec((HEADNORM_TILE, wide), lambda i, j: (i, j + off)),
                  pl.BlockSpec((1, LANES), lambda i, j: (0, 0)), pl.BlockSpec((LANES, LANES), lambda i, j: (0, 0))],
        out_specs=pl.BlockSpec((HEADNORM_TILE, wide), lambda i, j: (i, j)),
        out_shape=jax.ShapeDtypeStruct((T, width), BF), compiler_params=_params(("parallel", "parallel")),
    )(proj, g2, _half_sum_matrix())


def _headnorm_bwd(name, dy, proj, col0, width, g2):
    T = proj.shape[0]
    wide = min(width, GROUP_WIDTH)
    nb, off = width // wide, col0 // wide

    def body(dy_ref, x_ref, g_ref, b_ref, dx_ref, dg_ref):
        for s in range(wide // LANES):
            lanes = slice(LANES * s, LANES * (s + 1))
            xv = x_ref[:, lanes].astype(F32)
            dyv = dy_ref[:, lanes].astype(F32)
            r = lax.rsqrt(_head_mean(xv * xv, b_ref[...]) + RMS_EPS)
            xhat = xv * r
            dxhat = dyv * g_ref[...]
            dx_ref[:, lanes] = (r * (dxhat - xhat * _head_mean(dxhat * xhat, b_ref[...]))).astype(BF)
            dg_ref[0, :, lanes] = jnp.sum(dyv * xhat, axis=0, keepdims=True)

    return pl.pallas_call(
        body, name=name, grid=(T // HEADNORM_TILE, nb),
        in_specs=[pl.BlockSpec((HEADNORM_TILE, wide), lambda i, j: (i, j)),
                  pl.BlockSpec((HEADNORM_TILE, wide), lambda i, j: (i, j + off)),
                  pl.BlockSpec((1, LANES), lambda i, j: (0, 0)), pl.BlockSpec((LANES, LANES), lambda i, j: (0, 0))],
        out_specs=[pl.BlockSpec((HEADNORM_TILE, wide), lambda i, j: (i, j)),
                   pl.BlockSpec((1, 1, wide), lambda i, j: (i, 0, j))],
        out_shape=[jax.ShapeDtypeStruct((T, width), BF), jax.ShapeDtypeStruct((T // HEADNORM_TILE, 1, width), F32)],
        compiler_params=_params(("parallel", "parallel")),
    )(dy, proj, g2, _half_sum_matrix())


def _shift_down(v, k, row):
    return jnp.where(row >= k, pltpu.roll(v, k, axis=0), 0.0)


def _shift_up(v, k, row, T):
    return jnp.where(row < T - k, pltpu.roll(v, T - k, axis=0), 0.0)


def _by_group(g, vals):
    out = vals[-1]
    for i in range(len(vals) - 2, -1, -1):
        out = jnp.where(g == i, vals[i], out)
    return out


def _pool_fwd(name, proj, pool_w, pool_scale):
    T = proj.shape[0]

    def body(x_ref, w_ref, s_ref, pooled_ref, mixed_ref):
        g = pl.program_id(0)
        xv = x_ref[...].astype(F32)
        row = lax.broadcasted_iota(jnp.int32, (T, 1), 0)
        s2 = xv + _shift_down(xv, 1, row)
        s4 = s2 + _shift_down(s2, 2, row)
        s8 = s4 + _shift_down(s4, 4, row)
        s16 = s8 + _shift_down(s8, 8, row)
        wsum = _by_group(g, [s2, s4, s8, s16])
        count = jnp.minimum(row + 1, 2 << g).astype(F32)
        pooled = (wsum / count - xv).astype(BF)
        pooled_ref[...] = pooled
        mixed = jnp.dot(pooled, w_ref[0].astype(BF), preferred_element_type=F32) * s_ref[...]
        mixed_ref[...] = mixed.astype(BF)

    col = pl.BlockSpec((T, POOL_GROUP), lambda g: (0, g))
    return pl.pallas_call(
        body, name=name, grid=(N_POOL_GROUPS,),
        in_specs=[col, pl.BlockSpec((1, POOL_GROUP, POOL_GROUP), lambda g: (g, 0, 0)),
                  pl.BlockSpec((1, POOL_GROUP), lambda g: (0, g))],
        out_specs=[col, col],
        out_shape=[jax.ShapeDtypeStruct((T, POOL_WIDTH), BF), jax.ShapeDtypeStruct((T, POOL_WIDTH), BF)],
        compiler_params=_params(("parallel",)),
    )(proj, pool_w, pool_scale)


def _pool_bwd(name, dmixed, pooled, pool_w, pool_scale):
    T = dmixed.shape[0]

    def body(dm_ref, p_ref, w_ref, s_ref, dx_ref, dw_ref, ds_ref):
        g = pl.program_id(0)
        dm = dm_ref[...].astype(F32)
        pooled = p_ref[...]
        w = w_ref[0].astype(BF)
        pre = jnp.dot(pooled, w, preferred_element_type=F32)
        ds_ref[...] = jnp.sum(dm * pre, axis=0, keepdims=True)
        dms = (dm * s_ref[...]).astype(BF)
        dw_ref[0] = lax.dot_general(pooled, dms, _DIMS["tn"], preferred_element_type=F32)
        dpooled = lax.dot_general(dms, w, _DIMS["nt"], preferred_element_type=F32)
        row = lax.broadcasted_iota(jnp.int32, (T, 1), 0)
        count = jnp.minimum(row + 1, 2 << g).astype(F32)
        z = dpooled / count
        l2 = z + _shift_up(z, 1, row, T)
        l4 = l2 + _shift_up(l2, 2, row, T)
        l8 = l4 + _shift_up(l4, 4, row, T)
        l16 = l8 + _shift_up(l8, 8, row, T)
        dx_ref[...] = (_by_group(g, [l2, l4, l8, l16]) - dpooled).astype(BF)

    col = pl.BlockSpec((T, POOL_GROUP), lambda g: (0, g))
    wspec = pl.BlockSpec((1, POOL_GROUP, POOL_GROUP), lambda g: (g, 0, 0))
    sspec = pl.BlockSpec((1, POOL_GROUP), lambda g: (0, g))
    return pl.pallas_call(
        body, name=name, grid=(N_POOL_GROUPS,), in_specs=[col, col, wspec, sspec], out_specs=[col, wspec, sspec],
        out_shape=[jax.ShapeDtypeStruct((T, POOL_WIDTH), BF),
                   jax.ShapeDtypeStruct((N_POOL_GROUPS, POOL_GROUP, POOL_GROUP), F32),
                   jax.ShapeDtypeStruct((1, POOL_WIDTH), F32)],
        compiler_params=_params(("parallel",)),
    )(dmixed, pooled, pool_w, pool_scale)


ATTN_SCALE = HEAD_DIM ** -0.5
MASKED = float(jnp.finfo(jnp.float32).min)
KV_COL_BLOCK_V = COL_V // LANES
GROUP_WIDTH = GQA_GROUP * HEAD_DIM


def _dup_head(v, j):
    half = lax.broadcasted_iota(jnp.int32, (1, LANES), 1) // HEAD_DIM
    return jnp.where(half == j, v, pltpu.roll(v, HEAD_DIM, axis=1))


def _stack_heads(v, low):
    pieces = []
    for p in range(GROUP_WIDTH // LANES):
        vp = v[:, LANES * p: LANES * (p + 1)]
        pieces.append(jnp.where(low, vp, jnp.zeros_like(vp)))
        pieces.append(jnp.where(low, jnp.zeros_like(vp), vp))
    return jnp.concatenate(pieces, axis=0)


def _unstack_transposed(t, low):
    pairs = []
    for p in range(GROUP_WIDTH // LANES):
        even = t[:, BLOCK * (2 * p): BLOCK * (2 * p + 1)].T
        odd = t[:, BLOCK * (2 * p + 1): BLOCK * (2 * p + 2)].T
        pairs.append(jnp.where(low, even, odd))
    return pairs


STACKED = GQA_GROUP * BLOCK


def _band_bias():
    key = lax.broadcasted_iota(jnp.int32, (2, 2 * BLOCK, STACKED), 1)
    qry = lax.broadcasted_iota(jnp.int32, (2, 2 * BLOCK, STACKED), 2) % BLOCK
    first = lax.broadcasted_iota(jnp.int32, (2, 2 * BLOCK, STACKED), 0) == 0
    valid = (key > qry) & (key <= qry + BLOCK) & (jnp.logical_not(first) | (key >= BLOCK))
    return jnp.where(valid, 0.0, MASKED).astype(F32)


BIAS_SPEC = pl.BlockSpec((1, 2 * BLOCK, STACKED), lambda n: (jnp.minimum(n, 1), 0, 0))


def _softmax_keys_on_sublanes(k2, q, bias, sink_ref, j):
    head_of_lane = lax.broadcasted_iota(jnp.int32, (1, STACKED), 1) // BLOCK
    sink = jnp.zeros((1, STACKED), F32)
    for h in range(GQA_GROUP):
        sink = jnp.where(head_of_lane == h, sink_ref[j * GQA_GROUP + h], sink)
    s = lax.dot_general(k2, q, _DIMS["nt"], preferred_element_type=F32) + bias
    m = jnp.maximum(jnp.max(s, axis=0, keepdims=True), sink)
    e = jnp.exp(s - m)
    e_sink = jnp.exp(sink - m)
    inv = 1.0 / (jnp.sum(e, axis=0, keepdims=True) + e_sink)
    return e * inv, e_sink * inv


def _attn_fwd(name, qn, kn, proj, sinks, comm=None):
    T = qn.shape[0]
    nb = T // BLOCK
    plumb = _CommPlumbing(comm)

    def body(sink_ref, bias_ref, q_ref, kp_ref, kc_ref, vp_ref, vc_ref, *rest):
        c_in, o_ref = rest[:plumb.n_in], rest[plumb.n_in]
        c_out, c_scr = rest[plumb.n_in + 1: plumb.n_in + 1 + plumb.n_out], rest[plumb.n_in + 1 + plumb.n_out:]
        n = pl.program_id(0)
        plumb.run(n, nb, True, c_in, c_out, c_scr)
        low = lax.broadcasted_iota(jnp.int32, (1, LANES), 1) < HEAD_DIM
        kk = jnp.concatenate([kp_ref[...], kc_ref[...]], axis=0)
        vv = jnp.concatenate([vp_ref[...], vc_ref[...]], axis=0)
        for j in range(2):
            q = _stack_heads(q_ref[:, GROUP_WIDTH * j: GROUP_WIDTH * (j + 1)], low)
            p, _ = _softmax_keys_on_sublanes(_dup_head(kk, j), q, bias_ref[0], sink_ref, j)
            o_t = lax.dot_general(_dup_head(vv, j), p.astype(BF), _DIMS["tn"], preferred_element_type=F32)
            for pair, o in enumerate(_unstack_transposed(o_t, low)):
                lanes = slice(GROUP_WIDTH * j + LANES * pair, GROUP_WIDTH * j + LANES * (pair + 1))
                o_ref[:, lanes] = o.astype(BF)
        plumb.run(n, nb, False, c_in, c_out, c_scr)

    wide = pl.BlockSpec((BLOCK, ATTN_WIDTH), lambda n: (n, 0))
    res = pl.pallas_call(
        body, name=name, grid=(nb,),
        in_specs=[pl.BlockSpec(memory_space=pltpu.SMEM), BIAS_SPEC, wide,
                  pl.BlockSpec((BLOCK, LANES), lambda n: (jnp.maximum(n - 1, 0), 0)),
                  pl.BlockSpec((BLOCK, LANES), lambda n: (n, 0)),
                  pl.BlockSpec((BLOCK, LANES), lambda n: (jnp.maximum(n - 1, 0), KV_COL_BLOCK_V)),
                  pl.BlockSpec((BLOCK, LANES), lambda n: (n, KV_COL_BLOCK_V))] + [ANY] * plumb.n_in,
        out_specs=[wide] + [ANY] * plumb.n_out,
        out_shape=[jax.ShapeDtypeStruct((T, ATTN_WIDTH), BF)] + plumb.out_shapes, scratch_shapes=plumb.scratch,
        compiler_params=_params(("arbitrary",) if comm else ("parallel",)),
    )(sinks, _band_bias(), qn, kn, kn, proj, proj, *plumb.args)
    return (res[0], plumb.split_outputs(res[1:])) if comm is not None else res[0]


def _attn_bwd(name, dout, qn, kn, proj, sinks, comm):
    T = qn.shape[0]
    nb = T // BLOCK
    plumb = _CommPlumbing(comm)

    def body(sink_ref, bias_ref, do_ref, q_ref, kp_ref, kc_ref, vp_ref, vc_ref, *rest):
        c_in, (dq_ref, dk_ref, dv_ref, dsink_ref) = rest[:plumb.n_in], rest[plumb.n_in: plumb.n_in + 4]
        c_out = rest[plumb.n_in + 4: plumb.n_in + 4 + plumb.n_out]
        carry_k, carry_v, tot_k, tot_v = rest[plumb.n_in + 4 + plumb.n_out: plumb.n_in + 8 + plumb.n_out]
        c_scr = rest[plumb.n_in + 8 + plumb.n_out:]
        n = pl.program_id(0)
        plumb.run(n, nb + 1, True, c_in, c_out, c_scr)
        lane = lax.broadcasted_iota(jnp.int32, (1, LANES), 1)
        low = lane < HEAD_DIM

        @pl.when(n == 0)
        def _():
            carry_k[...] = jnp.zeros_like(carry_k)
            carry_v[...] = jnp.zeros_like(carry_v)
            dsink_ref[...] = jnp.zeros_like(dsink_ref)

        @pl.when(n == nb)
        def _():
            tot_k[...] = jnp.zeros_like(tot_k)
            tot_v[...] = jnp.zeros_like(tot_v)

        @pl.when(n < nb)
        def _():
            kk = jnp.concatenate([kp_ref[...], kc_ref[...]], axis=0)
            vv = jnp.concatenate([vp_ref[...], vc_ref[...]], axis=0)
            dk_tot = jnp.zeros((2 * BLOCK, LANES), F32)
            dv_tot = jnp.zeros((2 * BLOCK, LANES), F32)
            dsink = jnp.zeros((1, LANES), F32)
            for j in range(2):
                k2 = _dup_head(kk, j)
                v2 = _dup_head(vv, j)
                q = _stack_heads(q_ref[:, GROUP_WIDTH * j: GROUP_WIDTH * (j + 1)], low)
                do = _stack_heads(do_ref[:, GROUP_WIDTH * j: GROUP_WIDTH * (j + 1)], low)
                p, psink = _softmax_keys_on_sublanes(k2, q, bias_ref[0], sink_ref, j)
                dp =lax.dot_general(v2, do, _DIMS["nt"], preferred_element_type=F32)
                delta = jnp.sum(p * dp, axis=0, keepdims=True)
                ds = (p * (dp - delta)).astype(BF)
                dk2 = jnp.dot(ds, q, preferred_element_type=F32)
                dv2 = jnp.dot(p.astype(BF), do, preferred_element_type=F32)
                dq_t = lax.dot_general(k2, ds, _DIMS["tn"], preferred_element_type=F32)
                for pair, dq in enumerate(_unstack_transposed(dq_t, low)):
                    lanes = slice(GROUP_WIDTH * j + LANES * pair, GROUP_WIDTH * j + LANES * (pair + 1))
                    dq_ref[:, lanes] = dq.astype(BF)
                mine = low if j == 0 else jnp.logical_not(low)
                dk_tot = dk_tot + jnp.where(mine, dk2 + pltpu.roll(dk2, HEAD_DIM, axis=1), 0.0)
                dv_tot = dv_tot + jnp.where(mine, dv2 + pltpu.roll(dv2, HEAD_DIM, axis=1), 0.0)
                sink_term = psink * delta
                for h in range(GQA_GROUP):
                    val = -jnp.sum(sink_term[:, BLOCK * h: BLOCK * (h + 1)], axis=1, keepdims=True)
                    dsink = dsink + jnp.where(lane == j * GQA_GROUP + h, val, 0.0)
            tot_k[...] = dk_tot
            tot_v[...] = dv_tot
            dsink_ref[0:1, :] += dsink

        dk_ref[...] = (carry_k[...] + tot_k[0:BLOCK]).astype(BF)
        dv_ref[...] = (carry_v[...] + tot_v[0:BLOCK]).astype(BF)
        carry_k[...] = tot_k[BLOCK:]
        carry_v[...] = tot_v[BLOCK:]
        plumb.run(n, nb + 1, False, c_in, c_out, c_scr)

    cur = lambda n: (jnp.minimum(n, nb - 1), 0)
    prev = lambda n: (jnp.maximum(n - 1, 0), 0)
    wide = pl.BlockSpec((BLOCK, ATTN_WIDTH), cur)
    res = pl.pallas_call(
        body, name=name, grid=(nb + 1,),
        in_specs=[pl.BlockSpec(memory_space=pltpu.SMEM), BIAS_SPEC, wide, wide,
                  pl.BlockSpec((BLOCK, LANES), prev), pl.BlockSpec((BLOCK, LANES), cur),
                  pl.BlockSpec((BLOCK, LANES), lambda n: (jnp.maximum(n - 1, 0), KV_COL_BLOCK_V)),
                  pl.BlockSpec((BLOCK, LANES), lambda n: (jnp.minimum(n, nb - 1), KV_COL_BLOCK_V))] + [ANY] * plumb.n_in,
        out_specs=[wide, pl.BlockSpec((BLOCK, LANES), prev), pl.BlockSpec((BLOCK, LANES), prev),
                   pl.BlockSpec((8, LANES), lambda n: (0, 0))] + [ANY] * plumb.n_out,
        out_shape=[jax.ShapeDtypeStruct((T, ATTN_WIDTH), BF), jax.ShapeDtypeStruct((T, KV_WIDTH), BF),
                   jax.ShapeDtypeStruct((T, KV_WIDTH), BF), jax.ShapeDtypeStruct((8, LANES), F32)] + plumb.out_shapes,
        scratch_shapes=[pltpu.VMEM((BLOCK, LANES), F32), pltpu.VMEM((BLOCK, LANES), F32),
                        pltpu.VMEM((2 * BLOCK, LANES), F32), pltpu.VMEM((2 * BLOCK, LANES), F32)] + plumb.scratch,
        compiler_params=_params(("arbitrary",)),
    )(sinks, _band_bias(), dout, qn, kn, kn, proj, proj, *plumb.args)
    return list(res[:4]), plumb.split_outputs(res[4:])


def _swiglu_fwd_epilogue(accs, ex):
    g, u = accs
    return [g, u, g * jax.nn.sigmoid(g) * u], []


def _swiglu_bwd_epilogue(accs, ex):
    (da,) = accs
    g, u = ex[0].astype(F32), ex[1].astype(F32)
    s = jax.nn.sigmoid(g)
    return [da * u * (s * (1.0 + g * (1.0 - s))), da * (g * s)], []


def _residual_norm_epilogue(scale):
    def epilogue(accs, ex):
        res, gain = ex
        h = res + scale * accs[0]
        r = lax.rsqrt(jnp.mean(h * h, axis=-1, keepdims=True) + RMS_EPS)
        return [h, h * r * gain], []
    return epilogue


def _rms_bwd_epilogue(accs, ex):
    (dn,) = accs
    xv, g, dres = ex
    r = lax.rsqrt(jnp.mean(xv * xv, axis=-1, keepdims=True) + RMS_EPS)
    xhat = xv * r
    dxhat = dn * g
    dx = dres + r * (dxhat - xhat * jnp.mean(dxhat * xhat, axis=-1, keepdims=True))
    return [dx, dx], [dn * xhat]


def _loss_epilogue(accs, ex):
    xv, target = ex
    d = xv + 0.5 * accs[0] - target
    dy = d * (1.0 / D_MODEL)
    return [dy, dy], [d * d]


def _merge_fwd_epilogue(accs, ex):
    (ba,) = accs
    bp, gp_pre, ga_pre, bias_p, bias_a = ex
    gp = jax.nn.sigmoid(gp_pre.astype(F32) + bias_p)
    ga = jax.nn.sigmoid(ga_pre.astype(F32) + bias_a)
    return [gp * bp.astype(F32) + ga * ba, ba], []


def _merge_bwd_epilogue(accs, ex):
    (dm,) = accs
    bp, ba, gp_pre, ga_pre, bias_p, bias_a = ex
    gp = jax.nn.sigmoid(gp_pre.astype(F32) + bias_p)
    ga = jax.nn.sigmoid(ga_pre.astype(F32) + bias_a)
    dgp = dm * bp.astype(F32) * gp * (1.0 - gp)
    dga = dm * ba.astype(F32) * ga * (1.0 - ga)
    return [dm * gp, dm * ga, dgp, dga], [dgp, dga]


def _prep(name, ws, transposes):
    n = len(ws)

    def body(*refs):
        for w_ref, o_ref, tr in zip(refs[:n], refs[n:], transposes):
            v = w_ref[...]
            o_ref[...] = (v.T if tr else v).astype(BF)

    shapes = [jax.ShapeDtypeStruct(w.shape[::-1] if tr else w.shape, BF) for w, tr in zip(ws, transposes)]
    return pl.pallas_call(body, name=name, out_shape=shapes, compiler_params=_params())(*ws)


def _adam_math(w, g, m, v):
    m = ADAM_B1 * m + (1.0 - ADAM_B1) * g
    v = ADAM_B2 * v + (1.0 - ADAM_B2) * jnp.square(g)
    m_hat = m / (1.0 - ADAM_B1 ** ADAM_STEP)
    v_hat = v / (1.0 - ADAM_B2 ** ADAM_STEP)
    delta = -ADAM_LR * (m_hat / (jnp.sqrt(v_hat) + ADAM_EPS) + ADAM_WD * w)
    return delta, m, v


def _adamw_sharded(name, items, transpose=False):
    n = len(items)

    def body(*refs):
        ins, outs = refs[:4 * n], refs[4 * n:]
        for k in range(n):
            s_ref, w_ref, m_ref, v_ref = ins[4 * k: 4 * k + 4]
            g = s_ref[0].astype(F32)
            for i in range(1, 4):
                g = g + s_ref[i].astype(F32)
            if transpose:
                g = g.T
            delta, mn, vn = _adam_math(w_ref[...], g, m_ref[...], v_ref[...])
            for o_ref, val in zip(outs[4 * k: 4 * k + 4], (g, delta, mn, vn)):
                o_ref[...] = val

    flat = [a for item in items for a in item]
    out_shape = [jax.ShapeDtypeStruct(item[1].shape, F32) for item in items for _ in range(4)]
    _, r, C = items[0][0].shape
    rows = r // 4
    if transpose or rows % 8:
        res = pl.pallas_call(body, name=name, out_shape=out_shape, compiler_params=_params())(*flat)
    else:
        tile = pl.BlockSpec((rows, C), lambda i: (i, 0))
        res = pl.pallas_call(
            body, name=name, grid=(4,), in_specs=[pl.BlockSpec((4, rows, C), lambda i: (0, i, 0)), tile, tile, tile] * n,
            out_specs=[tile] * (4 * n), out_shape=out_shape, compiler_params=_params(("parallel",)),
        )(*flat)
    return [tuple(res[4 * k: 4 * k + 4]) for k in range(n)]


SMALL_LAYOUT = (("ffn1_norm", 0, (8, LANES)), ("mix_norm", 8, (8, LANES)), ("ffn2_norm", 16, (8, LANES)),
                ("gate_bias", 24, (16, LANES)), ("pool_scale", 40, (4, LANES)), ("q_norm", 48, (1, HEAD_DIM)),
                ("k_norm", 56, (1, HEAD_DIM)), ("sinks", 64, (1, N_HEADS)))
LOSS_ROW = 72
SMALL_ROWS = 80


def _adamw_small(name, g_vec, g_pool_w, params):
    n = len(SMALL_LAYOUT) + 1

    def body(vec_ref, pw_ref, *refs):
        ins, outs = refs[:3 * n], refs[3 * n:]
        vec = vec_ref[0]
        pw = pw_ref[0]
        for i in range(1, N_DEV):
            vec = vec + vec_ref[i]
            pw = pw + pw_ref[i]
        grads = [vec[r0:r0 + shape[0], 0:shape[1]] for _, r0, shape in SMALL_LAYOUT] + [pw]
        for p, g in enumerate(grads):
            w_ref, m_ref, v_ref = ins[3 * p: 3 * p + 3]
            delta, mn, vn = _adam_math(w_ref[...], g, m_ref[...], v_ref[...])
            for o_ref, val in zip(outs[4 * p: 4 * p + 4], (g, delta, mn, vn)):
                o_ref[...] = val
        outs[4 * n][...] = vec[LOSS_ROW:LOSS_ROW + 1, :]

    flat = [a for wmv in params for a in wmv]
    out_shape = [jax.ShapeDtypeStruct(wmv[0].shape, F32) for wmv in params for _ in range(4)]
    out_shape.append(jax.ShapeDtypeStruct((1, LANES), F32))
    res = pl.pallas_call(body, name=name, out_shape=out_shape, compiler_params=_params())(g_vec, g_pool_w, *flat)
    return [tuple(res[4 * p: 4 * p + 4]) for p in range(n)], res[4 * n]


def _place():
    x, y, c = lax.axis_index("x"), lax.axis_index("y"), lax.axis_index("c")
    other_chips = [(1 - x, y), (x, 1 - y), (1 - x, 1 - y)]
    return x, y, c, other_chips


def _rows(ref, r, place, natural=False):
    px, py, pc = place
    b = 4 * px + 2 * py + pc if natural else 4 * pc + 2 * px + py
    return ref.at[pl.ds(pl.multiple_of(b * r, 8), r), :]


def _gather_task(shards, natural=(), forward_at=0.75):
    n = len(shards)
    rs = [s.shape[0] for s in shards]
    rows_of = lambda ref, k, place: _rows(ref, rs[k], place, k in natural)

    def copy(scr, outs, k, slot, block, to, src=None):
        rows = rows_of(outs[k], k, block)
        return pltpu.make_async_remote_copy(
            src_ref=rows if src is None else src, dst_ref=rows, send_sem=scr[0].at[7 * k + slot],
            recv_sem=scr[1].at[7 * k + slot], device_id=to, device_id_type=MESH)

    def first_sends(ins, outs, scr):
        x, y, c, chips = _place()
        me = (x, y, c)
        cps = [copy(scr, outs, k, 1 + j, me, (*chip, c), src=ins[k]) for j, chip in enumerate(chips) for k in range(n)]
        return cps + [copy(scr, outs, k, 0, me, (x, y, 1 - c), src=ins[k]) for k in range(n)]

    def passed_on(outs, scr):
        x, y, c, chips = _place()
        return [copy(scr, outs, k, 4 + j, (*chip, c), (x, y, 1 - c)) for j, chip in enumerate(chips) for k in range(n)]

    def local(ins, outs, scr):
        x, y, c, _ = _place()
        return [pltpu.make_async_copy(ins[k], rows_of(outs[k], k, (x, y, c)), scr[2].at[k]) for k in range(n)]

    def start(ins, outs, scr):
        for cp in local(ins, outs, scr) + first_sends(ins, outs, scr):
            cp.start()

    def forward(ins, outs, scr):
        x, y, c, chips = _place()
        for j, chip in enumerate(chips):
            for k in range(n):
                copy(scr, outs, k, 1 + j, (*chip, c), (x, y, c)).wait_recv()
                copy(scr, outs, k, 4 + j, (*chip, c), (x, y, 1 - c)).start()

    def finish(ins, outs, scr):
        x, y, c, chips = _place()
        for k in range(n):
            copy(scr, outs, k, 0, (x, y, 1 - c), (x, y, c)).wait_recv()
        for j, chip in enumerate(chips):
            for k in range(n):
                copy(scr, outs, k, 4 + j, (*chip, 1 - c), (x, y, c)).wait_recv()
        for cp in first_sends(ins, outs, scr) + passed_on(outs, scr):
            cp.wait_send()
        for cp in local(ins, outs, scr):
            cp.wait()

    out_shapes = [jax.ShapeDtypeStruct((N_DEV * s.shape[0], s.shape[1]), s.dtype) for s in shards]
    scratch = [pltpu.SemaphoreType.DMA((7 * n,)), pltpu.SemaphoreType.DMA((7 * n,)), pltpu.SemaphoreType.DMA((n,))]
    return _Task(shards, out_shapes, scratch, [(0, start), (forward_at, forward), (1.0, finish)])


def _direct_gather_task(shards):
    n = len(shards)
    rs = [s.shape[0] for s in shards]

    def peers():
        x, y, c, _ = _place()
        flip = lambda v, bit: 1 - v if bit else v
        return (x, y, c), [(flip(x, (s >> 2) & 1), flip(y, (s >> 1) & 1), flip(c, s & 1)) for s in range(1, N_DEV)]

    def copies(ins, outs, scr):
        me, others = peers()
        local = [pltpu.make_async_copy(ins[k], _rows(outs[k], rs[k], me), scr[2].at[k]) for k in range(n)]
        sems = lambda k, s: dict(send_sem=scr[0].at[7 * k + s], recv_sem=scr[1].at[7 * k + s], device_id_type=MESH)
        sends = [pltpu.make_async_remote_copy(src_ref=ins[k], dst_ref=_rows(outs[k], rs[k], me), device_id=to, **sems(k, s))
                 for s, to in enumerate(others) for k in range(n)]
        recvs = [pltpu.make_async_remote_copy(src_ref=_rows(outs[k], rs[k], frm), dst_ref=_rows(outs[k], rs[k], frm),
                                              device_id=me, **sems(k, s))
                 for s, frm in enumerate(others) for k in range(n)]
        return local, sends, recvs

    def start(ins, outs, scr):
        local, sends, _ = copies(ins, outs, scr)
        for cp in local + sends:
            cp.start()

    def finish(ins, outs, scr):
        local, sends, recvs = copies(ins, outs, scr)
        for cp in recvs:
            cp.wait_recv()
        for cp in sends:
            cp.wait_send()
        for cp in local:
            cp.wait()

    out_shapes = [jax.ShapeDtypeStruct((N_DEV * s.shape[0], s.shape[1]), s.dtype) for s in shards]
    scratch = [pltpu.SemaphoreType.DMA((7 * n,)), pltpu.SemaphoreType.DMA((7 * n,)), pltpu.SemaphoreType.DMA((n,))]
    return _Task(shards, out_shapes, scratch, [(0, start), (1.0, finish)])


def _chip_task(sums):
    n = len(sums)
    rs = [s.shape[0] // 4 for s in sums]

    def block(ref, k, chip_index):
        return ref.at[pl.ds(pl.multiple_of(chip_index * rs[k], 8), rs[k]), :]

    def copies(ins, outs, scr):
        send_sems, recv_sems, local_sems = scr
        x, y, c, chips = _place()
        here = 2 * x + y
        local = [pltpu.make_async_copy(block(ins[k], k, here), outs[k].at[here], local_sems.at[k]) for k in range(n)]
        remote = []
        for j, (px, py) in enumerate(chips):
            remote += [pltpu.make_async_remote_copy(
                src_ref=block(ins[k], k, 2 * px + py), dst_ref=outs[k].at[here],
                send_sem=send_sems.at[3 * k + j], recv_sem=recv_sems.at[3 * k + j],
                device_id=(px, py, c), device_id_type=MESH) for k in range(n)]
        return local, remote

    def start(ins, outs, scr):
        local, remote = copies(ins, outs, scr)
        for cp in local + remote:
            cp.start()

    def finish(ins, outs, scr):
        local, remote = copies(ins, outs, scr)
        for cp in remote:
            cp.wait()
        for cp in local:
            cp.wait()

    out_shapes = [jax.ShapeDtypeStruct((4, r, s.shape[1]), s.dtype) for r, s in zip(rs, sums)]
    scratch = [pltpu.SemaphoreType.DMA((3 * n,)), pltpu.SemaphoreType.DMA((3 * n,)), pltpu.SemaphoreType.DMA((n,))]
    return _Task(sums, out_shapes, scratch, [(0, start), (1.0, finish)])


SIBLING_COLLECTIVE_ID = 0


def _dw_pair(name, a, b, scale, comm=None, blocks=1, sibling_only=False):
    T, M = a.shape
    N = b.shape[1]
    half = M // 2
    wide = half // blocks
    tk = min(2048, T)
    nK = T // tk
    plumb = _CommPlumbing(comm)

    def body(core_ref, *rest):
        a_refs, b_ref, rest = rest[:blocks], rest[blocks], rest[blocks + 1:]
        c_in = rest[:plumb.n_in]
        o_ref = rest[plumb.n_in]
        c_out = rest[plumb.n_in + 1: plumb.n_in + 1 + plumb.n_out]
        acc, stage, land, send_sem, recv_sem = rest[plumb.n_in + 1 + plumb.n_out: plumb.n_in + 6 + plumb.n_out]
        c_scr = rest[plumb.n_in + 6 + plumb.n_out:]
        i, k = pl.program_id(0), pl.program_id(1)
        x, y, c, _ = _place()
        push = pltpu.make_async_remote_copy(src_ref=stage, dst_ref=land, send_sem=send_sem, recv_sem=recv_sem,
                                            device_id=(x, y, 1 - c), device_id_type=MESH)
        if sibling_only:
            @pl.when((i == 0) & (k == 0))
            def _():
                barrier = pltpu.get_barrier_semaphore()
                pl.semaphore_signal(barrier, inc=1, device_id=(x, y, 1 - c), device_id_type=MESH)
                pl.semaphore_wait(barrier, 1)

        if comm:
            plumb.run(i * nK + k, 2 * nK, True, c_in, c_out, c_scr)

        av = a_refs[0][...] if blocks == 1 else jnp.concatenate([r[...] for r in a_refs], axis=1)
        p = lax.dot_general(av, b_ref[...], _DIMS["tn"], preferred_element_type=F32)

        @pl.when(k == 0)
        def _():
            acc[...] = p

        @pl.when(k > 0)
        def _():
            acc[...] += p

        @pl.when((i == 0) & (k == nK - 1))
        def _():
            stage[...] = (scale * acc[...]).astype(BF)
            push.start()

        @pl.when((i == 1) & (k == nK - 1))
        def _():
            push.wait_recv()
            o_ref[...] = (scale * acc[...] + land[...].astype(F32)).astype(BF)
            push.wait_send()

        if comm:
            plumb.run(i * nK + k, 2 * nK, False, c_in, c_out, c_scr)

    grid_spec = pltpu.PrefetchScalarGridSpec(
        num_scalar_prefetch=1, grid=(2, nK),
        in_specs=[pl.BlockSpec((tk, wide), functools.partial(
            lambda i, k, core, j: (k, (2 * j if blocks > 1 else 0) + jnp.where(i == 0, 1 - core[0], core[0])), j=j))
            for j in range(blocks)] + [pl.BlockSpec((tk, N), lambda i, k, core: (k, 0))] + [ANY] * plumb.n_in,
        out_specs=[pl.BlockSpec((half, N), lambda i, k, core: (0, 0))] + [ANY] * plumb.n_out,
        scratch_shapes=[pltpu.VMEM((half, N), F32), pltpu.VMEM((half, N), BF), pltpu.VMEM((half, N), BF),
                        pltpu.SemaphoreType.DMA, pltpu.SemaphoreType.DMA] + plumb.scratch)
    core = lax.axis_index("c").astype(jnp.int32).reshape(1)
    res = pl.pallas_call(
        body, name=name, grid_spec=grid_spec,
        out_shape=[jax.ShapeDtypeStruct((half, N), BF)] + plumb.out_shapes,
        compiler_params=pltpu.CompilerParams(
            dimension_semantics=("arbitrary", "arbitrary"), vmem_limit_bytes=VMEM_LIMIT_V7X,
            collective_id=SIBLING_COLLECTIVE_ID if sibling_only else None),
    )(core, *([a] * blocks), b, *plumb.args)
    return (res[0], plumb.split_outputs(res[1:])) if comm else res[0]


def _pair_task(parts):
    n = len(parts)

    def copies(ins, outs, scr):
        x, y, c, _ = _place()
        return [pltpu.make_async_remote_copy(
            src_ref=ins[k].at[:, pl.ds(1 - c, 1)], dst_ref=outs[k], send_sem=scr[0].at[k], recv_sem=scr[1].at[k],
            device_id=(x, y, 1 - c), device_id_type=MESH) for k in range(n)]

    def start(ins, outs, scr):
        for cp in copies(ins, outs, scr):
            cp.start()

    def finish(ins, outs, scr):
        for cp in copies(ins, outs, scr):
            cp.wait()

    out_shapes = [jax.ShapeDtypeStruct((4, 1) + p.shape[2:], p.dtype) for p in parts]
    scratch = [pltpu.SemaphoreType.DMA((n,)), pltpu.SemaphoreType.DMA((n,))]
    return _Task(parts, out_shapes, scratch, [(0, start), (1.0, finish)])


def _pair_sum(name, part, got, core):
    _, _, r, C = part.shape

    def body(core_ref, p_ref, g_ref, o_ref):
        o_ref[0] = (p_ref[0, 0].astype(F32) + g_ref[0, 0].astype(F32)).astype(o_ref.dtype)

    return pl.pallas_call(
        body, name=name,
        grid_spec=pltpu.PrefetchScalarGridSpec(
            num_scalar_prefetch=1, grid=(4,),
            in_specs=[pl.BlockSpec((1, 1, r, C), lambda i, core_ref: (i, core_ref[0], 0, 0)),
                      pl.BlockSpec((1, 1, r, C), lambda i, core_ref: (i, 0, 0, 0))],
            out_specs=pl.BlockSpec((1, r, C), lambda i, core_ref: (i, 0, 0))),
        out_shape=jax.ShapeDtypeStruct((4, r, C), part.dtype), compiler_params=_params(("parallel",)),
    )(core, part, got)


def _ffn_bwd(tag, dy, dyb, x, gain, wgT, wuT, wd, saved, earlier=None):
    n, g, u, a = saved
    half = lambda accs, ex: _swiglu_bwd_epilogue([0.5 * accs[0]], ex)
    act_args = dict(tm=512, tn=1408, tk=D_MODEL, epilogue=half, extras=[(g, "tile", 0), (u, "tile", 0)], cols_outer=True)
    if earlier is None:
        sum_d = _dw_pair(tag + "_dw_down", a, dyb, 0.5, sibling_only=True)
        (dg, du), ((slots_d,),) = _mm(tag + "_d_act", [(dyb, wd, "nt", 0)], [BF, BF], comm=[_chip_task([sum_d])], **act_args)
        slots_e = None
        sum_g = _dw_pair(tag + "_dw_gate", dg, n, 1.0, sibling_only=True)
    else:
        sum_d, ((got,),) = _dw_pair(tag + "_dw_down", a, dyb, 0.5, comm=[_pair_task([earlier])], sibling_only=True)
        core = lax.axis_index("c").astype(jnp.int32).reshape(1)
        sum_e = _pair_sum(tag + "_pair_sum_earlier", earlier, got, core)
        sum_e = sum_e.reshape(4 * sum_e.shape[1], sum_e.shape[2])
        (dg, du), ((slots_e,),) = _mm(tag + "_d_act", [(dyb, wd, "nt", 0)], [BF, BF], comm=[_chip_task([sum_e])], **act_args)
        sum_g, ((slots_d,),) = _dw_pair(tag + "_dw_gate", dg, n, 1.0, comm=[_chip_task([sum_d])])
    sum_u, ((slots_g,),) = _dw_pair(tag + "_dw_up", du, n, 1.0, comm=[_chip_task([sum_g])])
    (dx, dxb, dgain), ((slots_u,),) = _mm(
        tag + "_d_norm", [(dg, wgT, "nn", 0), (du, wuT, "nn", 0)], [F32, BF], tm=512, tn=D_MODEL, tk=D_FF,
        epilogue=_rms_bwd_epilogue, extras=[(x, "tile", 0), (gain, "row", 0), (dy, "tile", 0)], n_colsum=1,
        comm=[_chip_task([sum_u])])
    return dx, dxb, dgain, slots_e, slots_g, slots_u, slots_d


def _tile_gain(g):
    return jnp.concatenate([g, g]).reshape(1, LANES)


def _fold_heads(partials):
    return jnp.sum(partials.reshape(-1, HEAD_DIM), axis=0)


def _pack_small_grads(grads, loss_local):
    pieces, row = [], 0
    for name, r0, _ in SMALL_LAYOUT + (("loss", LOSS_ROW, None),):
        v = (loss_local if name == "loss" else grads[name]).reshape(-1)
        rows = -(-v.size // LANES)
        block = jnp.pad(v, (0, rows * LANES - v.size)).reshape(rows, LANES)
        pieces += [jnp.zeros((r0 - row, LANES), F32)] * (r0 > row) + [block]
        row = r0 + rows
    pieces.append(jnp.zeros((SMALL_ROWS - row, LANES), F32))
    return jnp.concatenate(pieces, axis=0)


def kernel(x, ffn1_norm, ffn1_w_gate, ffn1_w_up, ffn1_w_down, mix_norm, w_in, pool_w, pool_scale, w_pool_out, q_norm, k_norm, sinks, w_attn_out, gate_bias, w_out, ffn2_norm, ffn2_w_gate, ffn2_w_up, ffn2_w_down, loss_target, m_ffn1_norm, m_ffn1_w_gate, m_ffn1_w_up, m_ffn1_w_down, m_mix_norm, m_w_in, m_pool_w, m_pool_scale, m_w_pool_out, m_q_norm, m_k_norm, m_sinks, m_w_attn_out, m_gate_bias, m_w_out, m_ffn2_norm, m_ffn2_w_gate, m_ffn2_w_up, m_ffn2_w_down, v_ffn1_norm, v_ffn1_w_gate, v_ffn1_w_up, v_ffn1_w_down, v_mix_norm, v_w_in, v_pool_w, v_pool_scale, v_w_pool_out, v_q_norm, v_k_norm, v_sinks, v_w_attn_out, v_gate_bias, v_w_out, v_ffn2_norm, v_ffn2_w_gate, v_ffn2_w_up, v_ffn2_w_down):
    T = x.shape[1]
    x2 = x.reshape(T, D_MODEL)
    target = loss_target.reshape(T, D_MODEL)

    big = [
        ("ffn1_w_gate", ffn1_w_gate, m_ffn1_w_gate, v_ffn1_w_gate, True, False),
        ("ffn1_w_up", ffn1_w_up, m_ffn1_w_up, v_ffn1_w_up, True, False),
        ("ffn1_w_down", ffn1_w_down, m_ffn1_w_down, v_ffn1_w_down, False, False),
        ("w_in", w_in, m_w_in, v_w_in, True, False),
        ("w_pool_out", w_pool_out, m_w_pool_out, v_w_pool_out, False, True),
        ("w_attn_out", w_attn_out, m_w_attn_out, v_w_attn_out, False, False),
        ("w_out", w_out, m_w_out, v_w_out, False, False),
        ("ffn2_w_gate", ffn2_w_gate, m_ffn2_w_gate, v_ffn2_w_gate, True, False),
        ("ffn2_w_up", ffn2_w_up, m_ffn2_w_up, v_ffn2_w_up, True, False),
        ("ffn2_w_down", ffn2_w_down, m_ffn2_w_down, v_ffn2_w_down, False, False),
    ]
    view = lambda a, tv: a.T if tv else a
    shards = _prep("prep_weights", [view(w, tv) for _, w, _, _, tv, _ in big], [tk_ for *_, tk_ in big])
    g1 =ffn1_norm.reshape(1, D_MODEL)
    g2 = mix_norm.reshape(1, D_MODEL)
    g3 = ffn2_norm.reshape(1, D_MODEL)
    bias_row = gate_bias.reshape(1, 2 * D_MODEL)
    qg, kg = _tile_gain(q_norm) * ATTN_SCALE, _tile_gain(k_norm)
    scale_row = pool_scale.reshape(1, POOL_WIDTH)

    n1, ((wg1T, wu1T),) = _rms_fwd("ffn1_norm", x2, g1, [_gather_task(shards[0:2], forward_at=0.9)])
    (gt1, up1, act1), ((wd1,), (w_inT,)) = _mm(
        "ffn1_gate_up", [(n1, wg1T, "nt", 0), (n1, wu1T, "nt", 1)], [BF, BF, BF], tm=512, tn=1408, tk=D_MODEL,
        epilogue=_swiglu_fwd_epilogue, cols_outer=True,
        comm=[_gather_task(shards[2:3], forward_at=0.5), _gather_task(shards[3:4], natural=(0,), forward_at=0.9)])
    (h1, u), ((w_poT, w_ao, w_o),) = _mm(
        "ffn1_down", [(act1, wd1, "nn", 0)], [F32, BF], tm=512, tn=D_MODEL, tk=D_FF,
        epilogue=_residual_norm_epilogue(0.5), extras=[(x2, "tile", 0), (g2, "row", 0)],
        comm=[_gather_task(shards[4:7], natural=(0, 1, 2), forward_at=0.8)])
    saved1 = (n1, gt1, up1, act1)
    (proj,), ((wg2T,),) = _mm(
        "in_proj", [(u, w_inT, "nt", 0)], [BF], tm=512, tn=1280, tk=D_MODEL, cols_outer=True,
        comm=[_gather_task(shards[7:8], forward_at=0.8)])
    pooled, mixed = _pool_fwd("pool_fwd", proj, pool_w, scale_row)
    qn = _headnorm_fwd("q_norm", proj, COL_Q, ATTN_WIDTH, qg)
    kn = _headnorm_fwd("k_norm", proj, COL_K, KV_WIDTH, kg)
    attn, ((wu2T,),) = _attn_fwd("attn_fwd", qn, kn, proj, sinks, comm=[_gather_task(shards[8:9], forward_at=0.8)])
    (bp,) = _mm("pool_out", [(mixed, w_poT, "nt", 0)], [BF], tm=1024, tn=D_MODEL, tk=POOL_WIDTH)
    gate_tn = 256
    gate_extras = [(proj, "tile", COL_GP // gate_tn), (proj, "tile", COL_GA // gate_tn),
                   (bias_row, "row", 0), (bias_row, "row", D_MODEL // gate_tn)]
    merged, ba = _mm("attn_out_merge", [(attn, w_ao, "nn", 0)], [BF, BF], tm=2048, tn=gate_tn, tk=ATTN_WIDTH,
                     epilogue=_merge_fwd_epilogue, extras=[(bp, "tile", 0)] + gate_extras)
    h2, n2 = _mm("mix_out", [(merged, w_o, "nn", 0)], [F32, BF], tm=512, tn=D_MODEL, tk=D_MODEL,
                 epilogue=_residual_norm_epilogue(1.0), extras=[(h1, "tile", 0), (g3, "row", 0)])
    (gt2, up2, act2), ((wd2,),) = _mm(
        "ffn2_gate_up", [(n2, wg2T, "nt", 0), (n2, wu2T, "nt", 1)], [BF, BF, BF], tm=512, tn=1408, tk=D_MODEL,
        epilogue=_swiglu_fwd_epilogue, cols_outer=True, comm=[_gather_task(shards[9:10], forward_at=0.8)])
    dy, dyb, sq = _mm("ffn2_down_loss", [(act2, wd2, "nn", 0)], [F32, BF], tm=512, tn=D_MODEL, tk=D_FF,
                      epilogue=_loss_epilogue, extras=[(h2, "tile", 0), (target, "tile", 0)], n_colsum=1)
    loss_local = 0.5 * jnp.sum(sq) / D_MODEL

    dh2, dh2b, dg3, _, slots_g2, slots_u2, slots_d2 = _ffn_bwd(
        "ffn2", dy, dyb, h2, g3, wg2T, wu2T, wd2, (n2, gt2, up2, act2))
    dbp, dba, dgp, dga, cs_gp, cs_ga = _mm(
        "mix_out_bwd", [(dh2b, w_o, "nt", 0)], [BF, BF, BF, BF], tm=2048, tn=gate_tn, tk=D_MODEL,
        epilogue=_merge_bwd_epilogue, extras=[(bp, "tile", 0), (ba, "tile", 0)] + gate_extras, n_colsum=2)
    sum_o = _dw_pair("dw_out", merged, dh2b, 1.0, blocks=4, sibling_only=True)
    (dmixed,) = _mm("pool_out_bwd", [(dbp, w_poT, "nn", 0)], [BF], tm=1024, tn=POOL_WIDTH, tk=D_MODEL)
    sum_po = _dw_pair("dw_pool_out", dbp, mixed, 1.0, blocks=4, sibling_only=True)
    (dattn,) = _mm("attn_out_bwd", [(dba, w_ao, "nt", 0)], [BF], tm=1024, tn=ATTN_WIDTH, tk=D_MODEL)
    sum_ao = _dw_pair("dw_attn_out", attn, dba, 1.0, blocks=4, sibling_only=True)
    dxp, dpool_w, dpool_scale = _pool_bwd("pool_bwd", dmixed, pooled, pool_w, scale_row)
    (dqn, dkn, dv, dsink_tile), ((slots_o, slots_po, slots_ao),) = _attn_bwd(
        "attn_bwd", dattn, qn, kn, proj, sinks, [_chip_task([sum_o, sum_po, sum_ao])])
    dq, dqg = _headnorm_bwd("q_norm_bwd", dqn, proj, COL_Q, ATTN_WIDTH, qg)
    dk, dkg = _headnorm_bwd("k_norm_bwd", dkn, proj, COL_K, KV_WIDTH, kg)
    dproj = jnp.concatenate([dxp, dq, dk, dv, dgp, dga], axis=1)
    (dh1, dh1b, dg2), ((g_pool_w,),) = _mm(
        "in_proj_bwd", [(dproj, w_inT, "nn", 0)], [F32, BF], tm=512, tn=D_MODEL, tk=IN_WIDTH, epilogue=_rms_bwd_epilogue,
        extras=[(h1, "tile", 0), (g2, "row", 0), (dh2, "tile", 0)], n_colsum=1,
        comm=[_gather_task([dpool_w.reshape(-1, LANES)])])
    (dw_inT,) = _mm("dw_in", [(dproj, u, "tn", 0)], [BF], tm=1280, tn=D_MODEL, tk=2048)
    dx, _, dg1, slots_in, slots_g1, slots_u1, slots_d1 = _ffn_bwd(
        "ffn1", dh1, dh1b, x2, g1, wg1T, wu1T, wd1, saved1, dw_inT.reshape(4, 2, IN_WIDTH // N_DEV, D_MODEL))

    slots = [slots_g1, slots_u1, slots_d1, slots_in, slots_po, slots_ao, slots_o, slots_g2, slots_u2, slots_d2]
    big_out = {}
    for label, group in (("ffn", (0, 1, 2, 7, 8, 9)), ("w_in", (3,)), ("w_pool_out", (4,)), ("attn_out_and_out", (5, 6))):
        items = [(slots[k], view(big[k][1], big[k][4]), view(big[k][2], big[k][4]), view(big[k][3], big[k][4]))
                 for k in group]
        for k, res in zip(group, _adamw_sharded("adamw_" + label, items, transpose=big[group[0]][5])):
            big_out[big[k][0]] = tuple(view(r, big[k][4]) for r in res)

    small_grads = {
        "ffn1_norm": jnp.sum(dg1, axis=(0, 1)), "mix_norm": jnp.sum(dg2, axis=(0, 1)), "ffn2_norm": jnp.sum(dg3, axis=(0, 1)),
        "gate_bias": jnp.concatenate([jnp.sum(cs_gp, axis=(0, 1)), jnp.sum(cs_ga, axis=(0, 1))]),
        "pool_scale": dpool_scale, "q_norm": _fold_heads(dqg) * ATTN_SCALE, "k_norm": _fold_heads(dkg),
        "sinks": dsink_tile[0, :N_HEADS]}
    ((g_vec,),) = _comm_only("gather_small_grads", [_direct_gather_task([_pack_small_grads(small_grads, loss_local)])])
    given = {"ffn1_norm": (ffn1_norm, m_ffn1_norm, v_ffn1_norm), "mix_norm": (mix_norm, m_mix_norm, v_mix_norm),
             "ffn2_norm": (ffn2_norm, m_ffn2_norm, v_ffn2_norm), "gate_bias": (gate_bias, m_gate_bias, v_gate_bias),
             "pool_scale": (pool_scale, m_pool_scale, v_pool_scale), "q_norm": (q_norm, m_q_norm, v_q_norm),
             "k_norm": (k_norm, m_k_norm, v_k_norm), "sinks": (sinks, m_sinks, v_sinks)}
    params = [tuple(a.reshape(shape) for a in given[nm]) for nm, _, shape in SMALL_LAYOUT]
    params.append(tuple(a.reshape(-1, LANES) for a in (pool_w, m_pool_w, v_pool_w)))
    small_res, loss_row = _adamw_small("adamw_small", g_vec.reshape(N_DEV, SMALL_ROWS, LANES),
                                       g_pool_w.reshape(N_DEV, -1, LANES), params)
    small_out = {nm: tuple(r.reshape(given[nm][0].shape) for r in res)
                 for (nm, _, _), res in zip(SMALL_LAYOUT, small_res)}
    small_out["pool_w"] = tuple(r.reshape(pool_w.shape) for r in small_res[-1])
    loss = loss_row[0, 0]

    order = ["ffn1_norm", "ffn1_w_gate", "ffn1_w_up", "ffn1_w_down", "mix_norm", "w_in", "pool_w", "pool_scale",
             "w_pool_out", "q_norm", "k_norm", "sinks", "w_attn_out", "gate_bias", "w_out", "ffn2_norm",
             "ffn2_w_gate", "ffn2_w_up", "ffn2_w_down"]
    every = {**big_out, **small_out}
    outs = [loss, dx.reshape(x.shape)]
    for j in range(4):
        outs += [every[nm][j] for nm in order]
    return tuple(outs)
```

```python
import functools

import jax
import jax.numpy as jnp
from jax import lax
from jax.experimental import pallas as pl
from jax.experimental.pallas import tpu as pltpu

BF = jnp.bfloat16
F32 = jnp.float32

D_MODEL = 1024
D_FF = 2816
POOL_WIDTH = 512
POOL_GROUP = 128
N_POOL_GROUPS = 4
HEAD_DIM = 64
N_HEADS = 16
GQA_GROUP = 8
BLOCK = 128
ATTN_WIDTH = 1024
KV_WIDTH = 128
IN_WIDTH = 3840
RMS_EPS = 1e-6
N_DEV = 8
LANES = 128

COL_Q = POOL_WIDTH
COL_K = COL_Q + ATTN_WIDTH
COL_V = COL_K + KV_WIDTH
COL_GP = COL_V + KV_WIDTH
COL_GA = COL_GP + D_MODEL

ADAM_LR = 0.001
ADAM_B1 = 0.9
ADAM_B2 = 0.999
ADAM_EPS = 1e-08
ADAM_WD = 0.01
ADAM_STEP = 10

VMEM_LIMIT_V7X = 56 * 1024 * 1024
MESH = pl.DeviceIdType.MESH
ANY = pl.BlockSpec(memory_space=pl.ANY)


def _params(sem=None, collective_id=None):
    return pltpu.CompilerParams(dimension_semantics=sem, vmem_limit_bytes=VMEM_LIMIT_V7X, collective_id=collective_id)


COLLECTIVE_IDS = {frozenset(["sibling"]): 0, frozenset(["chips"]): 1, frozenset(["sibling", "chips"]): 2}


def _handshake(peer_kinds):
    x, y, c, chips = _place()
    peers = ([(x, y, 1 - c)] if "sibling" in peer_kinds else []) + ([(*chip, c) for chip in chips] if "chips" in peer_kinds else [])
    barrier = pltpu.get_barrier_semaphore()
    for peer in peers:
        pl.semaphore_signal(barrier, inc=1, device_id=peer, device_id_type=MESH)
    pl.semaphore_wait(barrier, len(peers))


_DIMS = {"nt": (((1,), (1,)), ((), ())), "nn": (((1,), (0,)), ((), ())), "tn": (((0,), (0,)), ((), ()))}


class _Task:
    def __init__(self, inputs, out_shapes, scratch, phases, peers):
        self.inputs, self.out_shapes, self.scratch = list(inputs), list(out_shapes), list(scratch)
        self.phases = list(phases)
        self.peers = frozenset(peers)


class _CommPlumbing:
    def __init__(self, tasks):
        self.tasks = list(tasks or [])
        self.args = [a for t in self.tasks for a in t.inputs]
        self.out_shapes = [o for t in self.tasks for o in t.out_shapes]
        self.scratch = [s for t in self.tasks for s in t.scratch]
        self.n_in, self.n_out = len(self.args), len(self.out_shapes)

    def peer_kinds(self, own=()):
        kinds = frozenset(own).union(*[t.peers for t in self.tasks])
        return None if "all" in kinds or not kinds else kinds

    def collective_id(self, own=()):
        kinds = self.peer_kinds(own)
        return None if kinds is None else COLLECTIVE_IDS[kinds]

    def handshake(self, first, own=()):
        kinds = self.peer_kinds(own)
        if kinds is not None:
            pl.when(first)(functools.partial(_handshake, kinds))

    def _slices(self, c_in, c_out, c_scr):
        i = o = s = 0
        for t in self.tasks:
            yield t, c_in[i:i + len(t.inputs)], c_out[o:o + len(t.out_shapes)], c_scr[s:s + len(t.scratch)]
            i, o, s = i + len(t.inputs), o + len(t.out_shapes), s + len(t.scratch)

    def run(self, step, steps, before, c_in, c_out, c_scr):
        for t, ins, outs, scr in self._slices(c_in, c_out, c_scr):
            for frac, fn in t.phases:
                if step is None:
                    fn(ins, outs, scr)
                elif before == (frac == 0):
                    at = 0 if frac == 0 else max(0, min(steps, -(-int(round(frac * steps * 64)) // 64)) - 1)
                    pl.when(step == at)(functools.partial(fn, ins, outs, scr))

    def split_outputs(self, flat):
        res, o = [], 0
        for t in self.tasks:
            res.append(list(flat[o:o + len(t.out_shapes)]))
            o += len(t.out_shapes)
        return res


def _comm_only(name, tasks):
    plumb = _CommPlumbing(tasks)

    def body(*refs):
        c_in, c_out = refs[:plumb.n_in], refs[plumb.n_in: plumb.n_in + plumb.n_out]
        c_scr = refs[plumb.n_in + plumb.n_out:]
        plumb.run(None, 1, True, c_in, c_out, c_scr)

    res = pl.pallas_call(
        body, name=name, in_specs=[ANY] * plumb.n_in, out_specs=[ANY] * plumb.n_out, out_shape=plumb.out_shapes,
        scratch_shapes=plumb.scratch, compiler_params=pltpu.CompilerParams(has_side_effects=True),
    )(*plumb.args)
    return plumb.split_outputs(res)


def _mm(name, terms, out_dtypes, *, tm, tn, tk, epilogue=None, extras=(), n_colsum=0, comm=None, cols_outer=False):
    a0, b0, mode0, _ = terms[0]
    if mode0 == "nt":
        (M, K), N = a0.shape, b0.shape[0]
    elif mode0 == "nn":
        (M, K), N = a0.shape, b0.shape[1]
    else:
        (K, M), N = a0.shape, b0.shape[1]
    tm, tn, tk = min(tm, M), min(tn, N), min(tk, K)
    assert M % tm == 0 and N % tn == 0 and K % tk == 0, (name, M, N, K, tm, tn, tk)
    nI, nJ, nK = M // tm, N // tn, K // tk
    n_terms = len(terms)
    n_acc = max(t[3] for t in terms) + 1
    n_ex = len(extras)
    n_out = len(out_dtypes)
    if epilogue is None:
        epilogue = lambda accs, ex: ([accs[0]], [])
    plumb = _CommPlumbing(comm)
    n_scr = n_acc if nK > 1 else 0
    grid = (nJ, nI, nK) if cols_outer else (nI, nJ, nK)

    def body(*refs):
        n_in = 2 * n_terms + n_ex
        ab = refs[: 2 * n_terms]
        ex_refs = refs[2 * n_terms: n_in]
        c_in = refs[n_in: n_in + plumb.n_in]
        o0 = n_in + plumb.n_in
        out_refs = refs[o0: o0 + n_out]
        cs_refs = refs[o0 + n_out: o0 + n_out + n_colsum]
        c_out = refs[o0 + n_out + n_colsum: o0 + n_out + n_colsum + plumb.n_out]
        s0 = o0 + n_out + n_colsum + plumb.n_out
        acc_refs = refs[s0: s0 + n_scr]
        c_scr = refs[s0 + n_scr:]
        steps = grid[0] * grid[1] * nK
        if comm:
            step = (pl.program_id(0) * grid[1] + pl.program_id(1)) * nK + pl.program_id(2)
            plumb.handshake(step == 0)
            plumb.run(step, steps, True, c_in, c_out, c_scr)

        def products():
            accs = [None] * n_acc
            for t, (_, _, mode, ai) in enumerate(terms):
                p = lax.dot_general(ab[2 * t][...], ab[2 * t + 1][...], _DIMS[mode], preferred_element_type=F32)
                accs[ai] = p if accs[ai] is None else accs[ai] + p
            return accs

        def finish(accs):
            outs, colsums = epilogue(accs, [r[...] for r in ex_refs])
            for r, o in zip(out_refs, outs):
                r[...] = o.astype(r.dtype)
            for r, cs in zip(cs_refs, colsums):
                r[...] = jnp.sum(cs, axis=0, keepdims=True).reshape(r.shape)

        if nK == 1:
            finish(products())
        else:
            k = pl.program_id(2)
            accs = products()

            @pl.when(k == 0)
            def _():
                for r, a in zip(acc_refs, accs):
                    r[...] = a

            @pl.when(k > 0)
            def _():
                for r, a in zip(acc_refs, accs):
                    r[...] += a

            @pl.when(k == nK - 1)
            def _():
                finish([r[...] for r in acc_refs])

        if comm:
            plumb.run(step, steps, False, c_in, c_out, c_scr)

    def spec(block, index, fixed=False):
        imap = (lambda q, p, k: index(p, q, k)) if cols_outer else index
        return pl.BlockSpec(block, imap, pipeline_mode=pl.Buffered(1)) if fixed else pl.BlockSpec(block, imap)

    in_specs, args = [], []
    for a, b, mode, _ in terms:
        if mode == "nt":
            in_specs += [spec((tm, tk), lambda i, j, k: (i, k), nI * nK == 1),
                         spec((tn, tk), lambda i, j, k: (j, k), nJ * nK == 1)]
        elif mode == "nn":
            in_specs += [spec((tm, tk), lambda i, j, k: (i, k), nI * nK == 1),
                         spec((tk, tn), lambda i, j, k: (k, j), nJ * nK == 1)]
        else:
            in_specs += [spec((tk, tm), lambda i, j, k: (k, i), nI * nK == 1),
                         spec((tk, tn), lambda i, j, k: (k, j), nJ * nK == 1)]
        args += [a, b]
    for arr, kind, off in extras:
        if kind == "tile":
            in_specs.append(spec((tm, tn), functools.partial(lambda i, j, k, off: (i, j + off), off=off)))
        else:
            in_specs.append(spec((1, tn), functools.partial(lambda i, j, k, off: (0, j + off), off=off)))
        args.append(arr)
    out_shape = [jax.ShapeDtypeStruct((M, N), dt) for dt in out_dtypes]
    out_specs = [spec((tm, tn), lambda i, j, k: (i, j)) for _ in out_dtypes]
    out_shape += [jax.ShapeDtypeStruct((nI, 1, N), F32) for _ in range(n_colsum)]
    out_specs += [spec((1, 1, tn), lambda i, j, k: (i, 0, j)) for _ in range(n_colsum)]
    scratch = [pltpu.VMEM((tm, tn), F32) for _ in range(n_scr)]
    args += plumb.args
    in_specs += [ANY] * plumb.n_in
    out_shape += plumb.out_shapes
    out_specs += [ANY] * plumb.n_out
    sem = ("arbitrary",) * 3 if comm else ("parallel", "parallel", "arbitrary")
    res = pl.pallas_call(
        body, name=name, grid=grid, in_specs=in_specs, out_specs=out_specs, out_shape=out_shape,
        scratch_shapes=scratch + plumb.scratch, compiler_params=_params(sem, plumb.collective_id()),
    )(*args)
    n_own = n_out + n_colsum
    return (list(res[:n_own]), plumb.split_outputs(res[n_own:])) if comm is not None else res


ROW_TILE = 512


def _rms_fwd(name, x, g, comm):
    T, D = x.shape
    steps = T // ROW_TILE
    plumb = _CommPlumbing(comm)

    def body(x_ref, g_ref, *rest):
        c_in, o_ref = rest[:plumb.n_in], rest[plumb.n_in]
        c_out, c_scr = rest[plumb.n_in + 1: plumb.n_in + 1 + plumb.n_out], rest[plumb.n_in + 1 + plumb.n_out:]
        plumb.handshake(pl.program_id(0) == 0)
        plumb.run(pl.program_id(0), steps, True, c_in, c_out, c_scr)
        xv = x_ref[...]
        r = lax.rsqrt(jnp.mean(xv * xv, axis=-1, keepdims=True) + RMS_EPS)
        o_ref[...] = (xv * r * g_ref[...]).astype(BF)
        plumb.run(pl.program_id(0), steps, False, c_in, c_out, c_scr)

    row = pl.BlockSpec((ROW_TILE, D), lambda i: (i, 0))
    res = pl.pallas_call(
        body, name=name, grid=(steps,),
        in_specs=[row, pl.BlockSpec((1, D), lambda i: (0, 0))] + [ANY] * plumb.n_in,
        out_specs=[row] + [ANY] * plumb.n_out, out_shape=[jax.ShapeDtypeStruct((T, D), BF)] + plumb.out_shapes,
        scratch_shapes=plumb.scratch, compiler_params=_params(("arbitrary",), plumb.collective_id()),
    )(x, g, *plumb.args)
    return res[0], plumb.split_outputs(res[1:])


HEADNORM_TILE = 1024


def _half_sum_matrix():
    r = lax.broadcasted_iota(jnp.int32, (LANES, LANES), 0) // HEAD_DIM
    c = lax.broadcasted_iota(jnp.int32, (LANES, LANES), 1) // HEAD_DIM
    return (r == c).astype(BF)


def _head_mean(v, ones_blockdiag):
    hi = v.astype(BF)
    lo = (v - hi.astype(F32)).astype(BF)
    s = jnp.dot(hi, ones_blockdiag, preferred_element_type=F32) + jnp.dot(lo, ones_blockdiag, preferred_element_type=F32)
    return s * (1.0 / HEAD_DIM)


def _headnorm_fwd(name, proj, col0, width, g2):
    T = proj.shape[0]
    wide = min(width, GROUP_WIDTH)
    nb, off = width // wide, col0 // wide

    def body(x_ref, g_ref, b_ref, o_ref):
        for s in range(wide // LANES):
            lanes = slice(LANES * s, LANES * (s + 1))
            xv = x_ref[:, lanes].astype(F32)
            r = lax.rsqrt(_head_mean(xv * xv, b_ref[...]) + RMS_EPS)
            o_ref[:, lanes] = (xv * r * g_ref[...]).astype(BF)

    return pl.pallas_call(
        body, name=name, grid=(T // HEADNORM_TILE, nb),
        in_specs=[pl.BlockSpec((HEADNORM_TILE, wide), lambda i, j: (i, j + off)),
                  pl.BlockSpec((1, LANES), lambda i, j: (0, 0)), pl.BlockSpec((LANES, LANES), lambda i, j: (0, 0))],
        out_specs=pl.BlockSpec((HEADNORM_TILE, wide), lambda i, j: (i, j)),
        out_shape=jax.ShapeDtypeStruct((T, width), BF), compiler_params=_params(("parallel", "parallel")),
    )(proj, g2, _half_sum_matrix())


def _headnorm_bwd(name, dy, proj, col0, width, g2):
    T = proj.shape[0]
    wide = min(width, GROUP_WIDTH)
    nb, off = width // wide, col0 // wide

    def body(dy_ref, x_ref, g_ref, b_ref, dx_ref, dg_ref):
        for s in range(wide // LANES):
            lanes = slice(LANES * s, LANES * (s + 1))
            xv = x_ref[:, lanes].astype(F32)
            dyv = dy_ref[:, lanes].astype(F32)
            r = lax.rsqrt(_head_mean(xv * xv, b_ref[...]) + RMS_EPS)
            xhat = xv * r
            dxhat = dyv * g_ref[...]
            dx_ref[:, lanes] = (r * (dxhat - xhat * _head_mean(dxhat * xhat, b_ref[...]))).astype(BF)
            dg_ref[0, :, lanes] = jnp.sum(dyv * xhat, axis=0, keepdims=True)

    return pl.pallas_call(
        body, name=name, grid=(T // HEADNORM_TILE, nb),
        in_specs=[pl.BlockSpec((HEADNORM_TILE, wide), lambda i, j: (i, j)),
                  pl.BlockSpec((HEADNORM_TILE, wide), lambda i, j: (i, j + off)),
                  pl.BlockSpec((1, LANES), lambda i, j: (0, 0)), pl.BlockSpec((LANES, LANES), lambda i, j: (0, 0))],
        out_specs=[pl.BlockSpec((HEADNORM_TILE, wide), lambda i, j: (i, j)),
                   pl.BlockSpec((1, 1, wide), lambda i, j: (i, 0, j))],
        out_shape=[jax.ShapeDtypeStruct((T, width), BF), jax.ShapeDtypeStruct((T // HEADNORM_TILE, 1, width), F32)],
        compiler_params=_params(("parallel", "parallel")),
    )(dy, proj, g2, _half_sum_matrix())


def _shift_down(v, k, row):
    return jnp.where(row >= k, pltpu.roll(v, k, axis=0), 0.0)


def _shift_up(v, k, row, T):
    return jnp.where(row < T - k, pltpu.roll(v, T - k, axis=0), 0.0)


def _by_group(g, vals):
    out = vals[-1]
    for i in range(len(vals) - 2, -1, -1):
        out = jnp.where(g == i, vals[i], out)
    return out


def _pool_fwd(name, proj, pool_w, pool_scale):
    T = proj.shape[0]

    def body(x_ref, w_ref, s_ref, pooled_ref, mixed_ref):
        g = pl.program_id(0)
        xv = x_ref[...].astype(F32)
        row = lax.broadcasted_iota(jnp.int32, (T, 1), 0)
        s2 = xv + _shift_down(xv, 1, row)
        s4 = s2 + _shift_down(s2, 2, row)
        s8 = s4 + _shift_down(s4, 4, row)
        s16 = s8 + _shift_down(s8, 8, row)
        wsum = _by_group(g, [s2, s4, s8, s16])
        count = jnp.minimum(row + 1, 2 << g).astype(F32)
        pooled = (wsum / count - xv).astype(BF)
        pooled_ref[...] = pooled
        mixed = jnp.dot(pooled, w_ref[0].astype(BF), preferred_element_type=F32) * s_ref[...]
        mixed_ref[...] = mixed.astype(BF)

    col = pl.BlockSpec((T, POOL_GROUP), lambda g: (0, g))
    return pl.pallas_call(
        body, name=name, grid=(N_POOL_GROUPS,),
        in_specs=[col, pl.BlockSpec((1, POOL_GROUP, POOL_GROUP), lambda g: (g, 0, 0)),
                  pl.BlockSpec((1, POOL_GROUP), lambda g: (0, g))],
        out_specs=[col, col],
        out_shape=[jax.ShapeDtypeStruct((T, POOL_WIDTH), BF), jax.ShapeDtypeStruct((T, POOL_WIDTH), BF)],
        compiler_params=_params(("parallel",)),
    )(proj, pool_w, pool_scale)


def _pool_bwd(name, dmixed, pooled, pool_w, pool_scale):
    T = dmixed.shape[0]

    def body(dm_ref, p_ref, w_ref, s_ref, dx_ref, dw_ref, ds_ref):
        g = pl.program_id(0)
        dm = dm_ref[...].astype(F32)
        pooled = p_ref[...]
        w = w_ref[0].astype(BF)
        pre = jnp.dot(pooled, w, preferred_element_type=F32)
        ds_ref[...] = jnp.sum(dm * pre, axis=0, keepdims=True)
        dms = (dm * s_ref[...]).astype(BF)
        dw_ref[0] = lax.dot_general(pooled, dms, _DIMS["tn"], preferred_element_type=F32)
        dpooled = lax.dot_general(dms, w, _DIMS["nt"], preferred_element_type=F32)
        row = lax.broadcasted_iota(jnp.int32, (T, 1), 0)
        count = jnp.minimum(row + 1, 2 << g).astype(F32)
        z = dpooled / count
        l2 = z + _shift_up(z, 1, row, T)
        l4 = l2 + _shift_up(l2, 2, row, T)
        l8 = l4 + _shift_up(l4, 4, row, T)
        l16 = l8 + _shift_up(l8, 8, row, T)
        dx_ref[...] = (_by_group(g, [l2, l4, l8, l16]) - dpooled).astype(BF)

    col = pl.BlockSpec((T, POOL_GROUP), lambda g: (0, g))
    wspec = pl.BlockSpec((1, POOL_GROUP, POOL_GROUP), lambda g: (g, 0, 0))
    sspec = pl.BlockSpec((1, POOL_GROUP), lambda g: (0, g))
    return pl.pallas_call(
        body, name=name, grid=(N_POOL_GROUPS,), in_specs=[col, col, wspec, sspec], out_specs=[col, wspec, sspec],
        out_shape=[jax.ShapeDtypeStruct((T, POOL_WIDTH), BF),
                   jax.ShapeDtypeStruct((N_POOL_GROUPS, POOL_GROUP, POOL_GROUP), F32),
                   jax.ShapeDtypeStruct((1, POOL_WIDTH), F32)],
        compiler_params=_params(("parallel",)),
    )(dmixed, pooled, pool_w, pool_scale)


ATTN_SCALE = HEAD_DIM ** -0.5
MASKED = float(jnp.finfo(jnp.float32).min)
KV_COL_BLOCK_V = COL_V // LANES
GROUP_WIDTH = GQA_GROUP * HEAD_DIM


def _dup_head(v, j):
    half = lax.broadcasted_iota(jnp.int32, (1, LANES), 1) // HEAD_DIM
    return jnp.where(half == j, v, pltpu.roll(v, HEAD_DIM, axis=1))


def _stack_heads(v, low):
    pieces = []
    for p in range(GROUP_WIDTH // LANES):
        vp = v[:, LANES * p: LANES * (p + 1)]
        pieces.append(jnp.where(low, vp, jnp.zeros_like(vp)))
        pieces.append(jnp.where(low, jnp.zeros_like(vp), vp))
    return jnp.concatenate(pieces, axis=0)


def _unstack_transposed(t, low):
    pairs = []
    for p in range(GROUP_WIDTH // LANES):
        even = t[:, BLOCK * (2 * p): BLOCK * (2 * p + 1)].T
        odd = t[:, BLOCK * (2 * p + 1): BLOCK * (2 * p + 2)].T
        pairs.append(jnp.where(low, even, odd))
    return pairs


STACKED = GQA_GROUP * BLOCK


def _band_bias():
    key = lax.broadcasted_iota(jnp.int32, (2, 2 * BLOCK, STACKED), 1)
    qry = lax.broadcasted_iota(jnp.int32, (2, 2 * BLOCK, STACKED), 2) % BLOCK
    first = lax.broadcasted_iota(jnp.int32, (2, 2 * BLOCK, STACKED), 0) == 0
    valid = (key > qry) & (key <= qry + BLOCK) & (jnp.logical_not(first) | (key >= BLOCK))
    return jnp.where(valid, 0.0, MASKED).astype(F32)


BIAS_SPEC = pl.BlockSpec((1, 2 * BLOCK, STACKED), lambda n: (jnp.minimum(n, 1), 0, 0))


def _softmax_keys_on_sublanes(k2, q, bias, sink_ref, j):
    head_of_lane = lax.broadcasted_iota(jnp.int32, (1, STACKED), 1) // BLOCK
    sink = jnp.zeros((1, STACKED), F32)
    for h in range(GQA_GROUP):
        sink = jnp.where(head_of_lane == h, sink_ref[j * GQA_GROUP + h], sink)
    s = lax.dot_general(k2, q, _DIMS["nt"], preferred_element_type=F32) + bias
    m = jnp.maximum(jnp.max(s, axis=0, keepdims=True), sink)
    e = jnp.exp(s - m)
    e_sink = jnp.exp(sink - m)
    inv = 1.0 / (jnp.sum(e, axis=0, keepdims=True) + e_sink)
    return e * inv, e_sink * inv


def _attn_fwd(name, qn, kn, proj, sinks, comm=None):
    T = qn.shape[0]
    nb = T // BLOCK
    plumb = _CommPlumbing(comm)

    def body(sink_ref, bias_ref, q_ref, kp_ref, kc_ref, vp_ref, vc_ref, *rest):
        c_in, o_ref = rest[:plumb.n_in], rest[plumb.n_in]
        c_out, c_scr = rest[plumb.n_in + 1: plumb.n_in + 1 + plumb.n_out], rest[plumb.n_in + 1 + plumb.n_out:]
        n = pl.program_id(0)
        plumb.handshake(n == 0)
        plumb.run(n, nb, True, c_in, c_out, c_scr)
        low = lax.broadcasted_iota(jnp.int32, (1, LANES), 1) < HEAD_DIM
        kk = jnp.concatenate([kp_ref[...], kc_ref[...]], axis=0)
        vv = jnp.concatenate([vp_ref[...], vc_ref[...]], axis=0)
        for j in range(2):
            q = _stack_heads(q_ref[:, GROUP_WIDTH * j: GROUP_WIDTH * (j + 1)], low)
            p, _ = _softmax_keys_on_sublanes(_dup_head(kk, j), q, bias_ref[0], sink_ref, j)
            o_t = lax.dot_general(_dup_head(vv, j), p.astype(BF), _DIMS["tn"], preferred_element_type=F32)
            for pair, o in enumerate(_unstack_transposed(o_t, low)):
                lanes = slice(GROUP_WIDTH * j + LANES * pair, GROUP_WIDTH * j + LANES * (pair + 1))
                o_ref[:, lanes] = o.astype(BF)
        plumb.run(n, nb, False, c_in, c_out, c_scr)

    wide = pl.BlockSpec((BLOCK, ATTN_WIDTH), lambda n: (n, 0))
    res = pl.pallas_call(
        body, name=name, grid=(nb,),
        in_specs=[pl.BlockSpec(memory_space=pltpu.SMEM), BIAS_SPEC, wide,
                  pl.BlockSpec((BLOCK, LANES), lambda n: (jnp.maximum(n - 1, 0), 0)),
                  pl.BlockSpec((BLOCK, LANES), lambda n: (n, 0)),
                  pl.BlockSpec((BLOCK, LANES), lambda n: (jnp.maximum(n - 1, 0), KV_COL_BLOCK_V)),
                  pl.BlockSpec((BLOCK, LANES), lambda n: (n, KV_COL_BLOCK_V))] + [ANY] * plumb.n_in,
        out_specs=[wide] + [ANY] * plumb.n_out,
        out_shape=[jax.ShapeDtypeStruct((T, ATTN_WIDTH), BF)] + plumb.out_shapes, scratch_shapes=plumb.scratch,
        compiler_params=_params(("arbitrary",) if comm else ("parallel",), plumb.collective_id()),
    )(sinks, _band_bias(), qn, kn, kn, proj, proj, *plumb.args)
    return (res[0], plumb.split_outputs(res[1:])) if comm is not None else res[0]


def _attn_bwd(name, dout, qn, kn, proj, sinks, comm):
    T = qn.shape[0]
    nb = T // BLOCK
    plumb = _CommPlumbing(comm)

    def body(sink_ref, bias_ref, do_ref, q_ref, kp_ref, kc_ref, vp_ref, vc_ref, *rest):
        c_in, (dq_ref, dk_ref, dv_ref, dsink_ref) = rest[:plumb.n_in], rest[plumb.n_in: plumb.n_in + 4]
        c_out = rest[plumb.n_in + 4: plumb.n_in + 4 + plumb.n_out]
        carry_k, carry_v, tot_k, tot_v = rest[plumb.n_in + 4 + plumb.n_out: plumb.n_in + 8 + plumb.n_out]
        c_scr = rest[plumb.n_in + 8 + plumb.n_out:]
        n = pl.program_id(0)
        plumb.handshake(n == 0)
        plumb.run(n, nb + 1, True, c_in, c_out, c_scr)
        lane = lax.broadcasted_iota(jnp.int32, (1, LANES), 1)
        low = lane < HEAD_DIM

        @pl.when(n == 0)
        def _():
            carry_k[...] = jnp.zeros_like(carry_k)
            carry_v[...] = jnp.zeros_like(carry_v)
            dsink_ref[...] = jnp.zeros_like(dsink_ref)

        @pl.when(n == nb)
        def _():
            tot_k[...] = jnp.zeros_like(tot_k)
            tot_v[...] = jnp.zeros_like(tot_v)

        @pl.when(n < nb)
        def _():
            kk = jnp.concatenate([kp_ref[...], kc_ref[...]], axis=0)
            vv = jnp.concatenate([vp_ref[...], vc_ref[...]], axis=0)
            dk_tot = jnp.zeros((2 * BLOCK, LANES), F32)
            dv_tot = jnp.zeros((2 * BLOCK, LANES), F32)
            dsink = jnp.zeros((1, LANES), F32)
            for j in range(2):
                k2 = _dup_head(kk, j)
                v2 = _dup_head(vv, j)
                q = _stack_heads(q_ref[:, GROUP_WIDTH * j: GROUP_WIDTH * (j + 1)], low)
                do = _stack_heads(do_ref[:, GROUP_WIDTH * j: GROUP_WIDTH * (j + 1)], low)
                p, psink = _softmax_keys_on_sublanes(k2, q, bias_ref[0], sink_ref, j)
                dp =lax.dot_general(v2, do, _DIMS["nt"], preferred_element_type=F32)
                delta = jnp.sum(p * dp, axis=0, keepdims=True)
                ds = (p * (dp - delta)).astype(BF)
                dk2 = jnp.dot(ds, q, preferred_element_type=F32)
                dv2 = jnp.dot(p.astype(BF), do, preferred_element_type=F32)
                dq_t = lax.dot_general(k2, ds, _DIMS["tn"], preferred_element_type=F32)
                for pair, dq in enumerate(_unstack_transposed(dq_t, low)):
                    lanes = slice(GROUP_WIDTH * j + LANES * pair, GROUP_WIDTH * j + LANES * (pair + 1))
                    dq_ref[:, lanes] = dq.astype(BF)
                mine = low if j == 0 else jnp.logical_not(low)
                dk_tot = dk_tot + jnp.where(mine, dk2 + pltpu.roll(dk2, HEAD_DIM, axis=1), 0.0)
                dv_tot = dv_tot + jnp.where(mine, dv2 + pltpu.roll(dv2, HEAD_DIM, axis=1), 0.0)
                sink_term = psink * delta
                for h in range(GQA_GROUP):
                    val = -jnp.sum(sink_term[:, BLOCK * h: BLOCK * (h + 1)], axis=1, keepdims=True)
                    dsink = dsink + jnp.where(lane == j * GQA_GROUP + h, val, 0.0)
            tot_k[...] = dk_tot
            tot_v[...] = dv_tot
            dsink_ref[0:1, :] += dsink

        dk_ref[...] = (carry_k[...] + tot_k[0:BLOCK]).astype(BF)
        dv_ref[...] = (carry_v[...] + tot_v[0:BLOCK]).astype(BF)
        carry_k[...] = tot_k[BLOCK:]
        carry_v[...] = tot_v[BLOCK:]
        plumb.run(n, nb + 1, False, c_in, c_out, c_scr)

    cur = lambda n: (jnp.minimum(n, nb - 1), 0)
    prev = lambda n: (jnp.maximum(n - 1, 0), 0)
    wide = pl.BlockSpec((BLOCK, ATTN_WIDTH), cur)
    res = pl.pallas_call(
        body, name=name, grid=(nb + 1,),
        in_specs=[pl.BlockSpec(memory_space=pltpu.SMEM), BIAS_SPEC, wide, wide,
                  pl.BlockSpec((BLOCK, LANES), prev), pl.BlockSpec((BLOCK, LANES), cur),
                  pl.BlockSpec((BLOCK, LANES), lambda n: (jnp.maximum(n - 1, 0), KV_COL_BLOCK_V)),
                  pl.BlockSpec((BLOCK, LANES), lambda n: (jnp.minimum(n, nb - 1), KV_COL_BLOCK_V))] + [ANY] * plumb.n_in,
        out_specs=[wide, pl.BlockSpec((BLOCK, LANES), prev), pl.BlockSpec((BLOCK, LANES), prev),
                   pl.BlockSpec((8, LANES), lambda n: (0, 0))] + [ANY] * plumb.n_out,
        out_shape=[jax.ShapeDtypeStruct((T, ATTN_WIDTH), BF), jax.ShapeDtypeStruct((T, KV_WIDTH), BF),
                   jax.ShapeDtypeStruct((T, KV_WIDTH), BF), jax.ShapeDtypeStruct((8, LANES), F32)] + plumb.out_shapes,
        scratch_shapes=[pltpu.VMEM((BLOCK, LANES), F32), pltpu.VMEM((BLOCK, LANES), F32),
                        pltpu.VMEM((2 * BLOCK, LANES), F32), pltpu.VMEM((2 * BLOCK, LANES), F32)] + plumb.scratch,
        compiler_params=_params(("arbitrary",), plumb.collective_id()),
    )(sinks, _band_bias(), dout, qn, kn, kn, proj, proj, *plumb.args)
    return list(res[:4]), plumb.split_outputs(res[4:])


def _swiglu_fwd_epilogue(accs, ex):
    g, u = accs
    return [g, u, g * jax.nn.sigmoid(g) * u], []


def _swiglu_bwd_epilogue(accs, ex):
    (da,) = accs
    g, u = ex[0].astype(F32), ex[1].astype(F32)
    s = jax.nn.sigmoid(g)
    return [da * u * (s * (1.0 + g * (1.0 - s))), da * (g * s)], []


def _residual_norm_epilogue(scale):
    def epilogue(accs, ex):
        res, gain = ex
        h = res + scale * accs[0]
        r = lax.rsqrt(jnp.mean(h * h, axis=-1, keepdims=True) + RMS_EPS)
        return [h, h * r * gain], []
    return epilogue


def _rms_bwd_epilogue(accs, ex):
    (dn,) = accs
    xv, g, dres = ex
    r = lax.rsqrt(jnp.mean(xv * xv, axis=-1, keepdims=True) + RMS_EPS)
    xhat = xv * r
    dxhat = dn * g
    dx = dres + r * (dxhat - xhat * jnp.mean(dxhat * xhat, axis=-1, keepdims=True))
    return [dx, dx], [dn * xhat]


def _loss_epilogue(accs, ex):
    xv, target = ex
    d = xv + 0.5 * accs[0] - target
    dy = d * (1.0 / D_MODEL)
    return [dy, dy], [d * d]


def _merge_fwd_epilogue(accs, ex):
    (ba,) = accs
    bp, gp_pre, ga_pre, bias_p, bias_a = ex
    gp = jax.nn.sigmoid(gp_pre.astype(F32) + bias_p)
    ga = jax.nn.sigmoid(ga_pre.astype(F32) + bias_a)
    return [gp * bp.astype(F32) + ga * ba, ba], []


def _merge_bwd_epilogue(accs, ex):
    (dm,) = accs
    bp, ba, gp_pre, ga_pre, bias_p, bias_a = ex
    gp = jax.nn.sigmoid(gp_pre.astype(F32) + bias_p)
    ga = jax.nn.sigmoid(ga_pre.astype(F32) + bias_a)
    dgp = dm * bp.astype(F32) * gp * (1.0 - gp)
    dga = dm * ba.astype(F32) * ga * (1.0 - ga)
    return [dm * gp, dm * ga, dgp, dga], [dgp, dga]


def _prep(name, ws, transposes):
    n = len(ws)

    def body(*refs):
        for w_ref, o_ref, tr in zip(refs[:n], refs[n:], transposes):
            v = w_ref[...]
            o_ref[...] = (v.T if tr else v).astype(BF)

    shapes = [jax.ShapeDtypeStruct(w.shape[::-1] if tr else w.shape, BF) for w, tr in zip(ws, transposes)]
    return pl.pallas_call(body, name=name, out_shape=shapes, compiler_params=_params())(*ws)


def _adam_math(w, g, m, v):
    m = ADAM_B1 * m + (1.0 - ADAM_B1) * g
    v = ADAM_B2 * v + (1.0 - ADAM_B2) * jnp.square(g)
    m_hat = m / (1.0 - ADAM_B1 ** ADAM_STEP)
    v_hat = v / (1.0 - ADAM_B2 ** ADAM_STEP)
    delta = -ADAM_LR * (m_hat / (jnp.sqrt(v_hat) + ADAM_EPS) + ADAM_WD * w)
    return delta, m, v


def _adamw_sharded(name, items, transpose=False):
    n = len(items)

    def body(*refs):
        ins, outs = refs[:4 * n], refs[4 * n:]
        for k in range(n):
            s_ref, w_ref, m_ref, v_ref = ins[4 * k: 4 * k + 4]
            g = s_ref[0].astype(F32)
            for i in range(1, 4):
                g = g + s_ref[i].astype(F32)
            if transpose:
                g = g.T
            delta, mn, vn = _adam_math(w_ref[...], g, m_ref[...], v_ref[...])
            for o_ref, val in zip(outs[4 * k: 4 * k + 4], (g, delta, mn, vn)):
                o_ref[...] = val

    flat = [a for item in items for a in item]
    out_shape = [jax.ShapeDtypeStruct(item[1].shape, F32) for item in items for _ in range(4)]
    _, r, C = items[0][0].shape
    rows = r // 4
    if transpose or rows % 8:
        res = pl.pallas_call(body, name=name, out_shape=out_shape, compiler_params=_params())(*flat)
    else:
        tile = pl.BlockSpec((rows, C), lambda i: (i, 0))
        res = pl.pallas_call(
            body, name=name, grid=(4,), in_specs=[pl.BlockSpec((4, rows, C), lambda i: (0, i, 0)), tile, tile, tile] * n,
            out_specs=[tile] * (4 * n), out_shape=out_shape, compiler_params=_params(("parallel",)),
        )(*flat)
    return [tuple(res[4 * k: 4 * k + 4]) for k in range(n)]


SMALL_LAYOUT = (("ffn1_norm", 0, (8, LANES)), ("mix_norm", 8, (8, LANES)), ("ffn2_norm", 16, (8, LANES)),
                ("gate_bias", 24, (16, LANES)), ("pool_scale", 40, (4, LANES)), ("q_norm", 48, (1, HEAD_DIM)),
                ("k_norm", 56, (1, HEAD_DIM)), ("sinks", 64, (1, N_HEADS)))
LOSS_ROW = 72
SMALL_ROWS = 80


def _adamw_small(name, g_vec, g_pool_w, params):
    n = len(SMALL_LAYOUT) + 1

    def body(vec_ref, pw_ref, *refs):
        ins, outs = refs[:3 * n], refs[3 * n:]
        vec = vec_ref[0]
        pw = pw_ref[0]
        for i in range(1, N_DEV):
            vec = vec + vec_ref[i]
            pw = pw + pw_ref[i]
        grads = [vec[r0:r0 + shape[0], 0:shape[1]] for _, r0, shape in SMALL_LAYOUT] + [pw]
        for p, g in enumerate(grads):
            w_ref, m_ref, v_ref = ins[3 * p: 3 * p + 3]
            delta, mn, vn = _adam_math(w_ref[...], g, m_ref[...], v_ref[...])
            for o_ref, val in zip(outs[4 * p: 4 * p + 4], (g, delta, mn, vn)):
                o_ref[...] = val
        outs[4 * n][...] = vec[LOSS_ROW:LOSS_ROW + 1, :]

    flat = [a for wmv in params for a in wmv]
    out_shape = [jax.ShapeDtypeStruct(wmv[0].shape, F32) for wmv in params for _ in range(4)]
    out_shape.append(jax.ShapeDtypeStruct((1, LANES), F32))
    res = pl.pallas_call(body, name=name, out_shape=out_shape, compiler_params=_params())(g_vec, g_pool_w, *flat)
    return [tuple(res[4 * p: 4 * p + 4]) for p in range(n)], res[4 * n]


def _place():
    x, y, c = lax.axis_index("x"), lax.axis_index("y"), lax.axis_index("c")
    other_chips = [(1 - x, y), (x, 1 - y), (1 - x, 1 - y)]
    return x, y, c, other_chips


def _rows(ref, r, place, natural=False):
    px, py, pc = place
    b = 4 * px + 2 * py + pc if natural else 4 * pc + 2 * px + py
    return ref.at[pl.ds(pl.multiple_of(b * r, 8), r), :]


def _gather_task(shards, natural=(), forward_at=0.75):
    n = len(shards)
    rs = [s.shape[0] for s in shards]
    rows_of = lambda ref, k, place: _rows(ref, rs[k], place, k in natural)

    def copy(scr, outs, k, slot, block, to, src=None):
        rows = rows_of(outs[k], k, block)
        return pltpu.make_async_remote_copy(
            src_ref=rows if src is None else src, dst_ref=rows, send_sem=scr[0].at[7 * k + slot],
            recv_sem=scr[1].at[7 * k + slot], device_id=to, device_id_type=MESH)

    def first_sends(ins, outs, scr):
        x, y, c, chips = _place()
        me = (x, y, c)
        cps = [copy(scr, outs, k, 1 + j, me, (*chip, c), src=ins[k]) for j, chip in enumerate(chips) for k in range(n)]
        return cps + [copy(scr, outs, k, 0, me, (x, y, 1 - c), src=ins[k]) for k in range(n)]

    def passed_on(outs, scr):
        x, y, c, chips = _place()
        return [copy(scr, outs, k, 4 + j, (*chip, c), (x, y, 1 - c)) for j, chip in enumerate(chips) for k in range(n)]

    def local(ins, outs, scr):
        x, y, c, _ = _place()
        return [pltpu.make_async_copy(ins[k], rows_of(outs[k], k, (x, y, c)), scr[2].at[k]) for k in range(n)]

    def start(ins, outs, scr):
        for cp in local(ins, outs, scr) + first_sends(ins, outs, scr):
            cp.start()

    def forward(ins, outs, scr):
        x, y, c, chips = _place()
        for j, chip in enumerate(chips):
            for k in range(n):
                copy(scr, outs, k, 1 + j, (*chip, c), (x, y, c)).wait_recv()
                copy(scr, outs, k, 4 + j, (*chip, c), (x, y, 1 - c)).start()

    def finish(ins, outs, scr):
        x, y, c, chips = _place()
        for k in range(n):
            copy(scr, outs, k, 0, (x, y, 1 - c), (x, y, c)).wait_recv()
        for j, chip in enumerate(chips):
            for k in range(n):
                copy(scr, outs, k, 4 + j, (*chip, 1 - c), (x, y, c)).wait_recv()
        for cp in first_sends(ins, outs, scr) + passed_on(outs, scr):
            cp.wait_send()
        for cp in local(ins, outs, scr):
            cp.wait()

    out_shapes = [jax.ShapeDtypeStruct((N_DEV * s.shape[0], s.shape[1]), s.dtype) for s in shards]
    scratch = [pltpu.SemaphoreType.DMA((7 * n,)), pltpu.SemaphoreType.DMA((7 * n,)), pltpu.SemaphoreType.DMA((n,))]
    return _Task(shards, out_shapes, scratch, [(0, start), (forward_at, forward), (1.0, finish)], ("sibling", "chips"))


def _direct_gather_task(shards):
    n = len(shards)
    rs = [s.shape[0] for s in shards]

    def peers():
        x, y, c, _ = _place()
        flip = lambda v, bit: 1 - v if bit else v
        return (x, y, c), [(flip(x, (s >> 2) & 1), flip(y, (s >> 1) & 1), flip(c, s & 1)) for s in range(1, N_DEV)]

    def copies(ins, outs, scr):
        me, others = peers()
        local = [pltpu.make_async_copy(ins[k], _rows(outs[k], rs[k], me), scr[2].at[k]) for k in range(n)]
        sems = lambda k, s: dict(send_sem=scr[0].at[7 * k + s], recv_sem=scr[1].at[7 * k + s], device_id_type=MESH)
        sends = [pltpu.make_async_remote_copy(src_ref=ins[k], dst_ref=_rows(outs[k], rs[k], me), device_id=to, **sems(k, s))
                 for s, to in enumerate(others) for k in range(n)]
        recvs = [pltpu.make_async_remote_copy(src_ref=_rows(outs[k], rs[k], frm), dst_ref=_rows(outs[k], rs[k], frm),
                                              device_id=me, **sems(k, s))
                 for s, frm in enumerate(others) for k in range(n)]
        return local, sends, recvs

    def start(ins, outs, scr):
        local, sends, _ = copies(ins, outs, scr)
        for cp in local + sends:
            cp.start()

    def finish(ins, outs, scr):
        local, sends, recvs = copies(ins, outs, scr)
        for cp in recvs:
            cp.wait_recv()
        for cp in sends:
            cp.wait_send()
        for cp in local:
            cp.wait()

    out_shapes = [jax.ShapeDtypeStruct((N_DEV * s.shape[0], s.shape[1]), s.dtype) for s in shards]
    scratch = [pltpu.SemaphoreType.DMA((7 * n,)), pltpu.SemaphoreType.DMA((7 * n,)), pltpu.SemaphoreType.DMA((n,))]
    return _Task(shards, out_shapes, scratch, [(0, start), (1.0, finish)], ("all",))


def _chip_task(sums):
    n = len(sums)
    rs = [s.shape[0] // 4 for s in sums]

    def block(ref, k, chip_index):
        return ref.at[pl.ds(pl.multiple_of(chip_index * rs[k], 8), rs[k]), :]

    def copies(ins, outs, scr):
        send_sems, recv_sems, local_sems = scr
        x, y, c, chips = _place()
        here = 2 * x + y
        local = [pltpu.make_async_copy(block(ins[k], k, here), outs[k].at[here], local_sems.at[k]) for k in range(n)]
        remote = []
        for j, (px, py) in enumerate(chips):
            remote += [pltpu.make_async_remote_copy(
                src_ref=block(ins[k], k, 2 * px + py), dst_ref=outs[k].at[here],
                send_sem=send_sems.at[3 * k + j], recv_sem=recv_sems.at[3 * k + j],
                device_id=(px, py, c), device_id_type=MESH) for k in range(n)]
        return local, remote

    def start(ins, outs, scr):
        local, remote = copies(ins, outs, scr)
        for cp in local + remote:
            cp.start()

    def finish(ins, outs, scr):
        local, remote = copies(ins, outs, scr)
        for cp in remote:
            cp.wait()
        for cp in local:
            cp.wait()

    out_shapes = [jax.ShapeDtypeStruct((4, r, s.shape[1]), s.dtype) for r, s in zip(rs, sums)]
    scratch = [pltpu.SemaphoreType.DMA((3 * n,)), pltpu.SemaphoreType.DMA((3 * n,)), pltpu.SemaphoreType.DMA((n,))]
    return _Task(sums, out_shapes, scratch, [(0, start), (1.0, finish)], ("chips",))


def _dw_pair(name, a, b, scale, comm=None, blocks=1):
    T, M = a.shape
    N = b.shape[1]
    half = M // 2
    wide = half // blocks
    tk = min(2048, T)
    nK = T // tk
    plumb = _CommPlumbing(comm)

    def body(core_ref, *rest):
        a_refs, b_ref, rest = rest[:blocks], rest[blocks], rest[blocks + 1:]
        c_in = rest[:plumb.n_in]
        o_ref = rest[plumb.n_in]
        c_out = rest[plumb.n_in + 1: plumb.n_in + 1 + plumb.n_out]
        acc, stage, land, send_sem, recv_sem = rest[plumb.n_in + 1 + plumb.n_out: plumb.n_in + 6 + plumb.n_out]
        c_scr = rest[plumb.n_in + 6 + plumb.n_out:]
        i, k = pl.program_id(0), pl.program_id(1)
        x, y, c, _ = _place()
        push = pltpu.make_async_remote_copy(src_ref=stage, dst_ref=land, send_sem=send_sem, recv_sem=recv_sem,
                                            device_id=(x, y, 1 - c), device_id_type=MESH)
        plumb.handshake((i == 0) & (k == 0), own=("sibling",))
        if comm:
            plumb.run(i * nK + k, 2 * nK, True, c_in, c_out, c_scr)

        av = a_refs[0][...] if blocks == 1 else jnp.concatenate([r[...] for r in a_refs], axis=1)
        p = lax.dot_general(av, b_ref[...], _DIMS["tn"], preferred_element_type=F32)

        @pl.when(k == 0)
        def _():
            acc[...] = p

        @pl.when(k > 0)
        def _():
            acc[...] += p

        @pl.when((i == 0) & (k == nK - 1))
        def _():
            stage[...] = (scale * acc[...]).astype(BF)
            push.start()

        @pl.when((i == 1) & (k == nK - 1))
        def _():
            push.wait_recv()
            o_ref[...] = (scale * acc[...] + land[...].astype(F32)).astype(BF)
            push.wait_send()

        if comm:
            plumb.run(i * nK + k, 2 * nK, False, c_in, c_out, c_scr)

    grid_spec = pltpu.PrefetchScalarGridSpec(
        num_scalar_prefetch=1, grid=(2, nK),
        in_specs=[pl.BlockSpec((tk, wide), functools.partial(
            lambda i, k, core, j: (k, (2 * j if blocks > 1 else 0) + jnp.where(i == 0, 1 - core[0], core[0])), j=j))
            for j in range(blocks)] + [pl.BlockSpec((tk, N), lambda i, k, core: (k, 0))] + [ANY] * plumb.n_in,
        out_specs=[pl.BlockSpec((half, N), lambda i, k, core: (0, 0))] + [ANY] * plumb.n_out,
        scratch_shapes=[pltpu.VMEM((half, N), F32), pltpu.VMEM((half, N), BF), pltpu.VMEM((half, N), BF),
                        pltpu.SemaphoreType.DMA, pltpu.SemaphoreType.DMA] + plumb.scratch)
    core = lax.axis_index("c").astype(jnp.int32).reshape(1)
    res = pl.pallas_call(
        body, name=name, grid_spec=grid_spec,
        out_shape=[jax.ShapeDtypeStruct((half, N), BF)] + plumb.out_shapes,
        compiler_params=_params(("arbitrary", "arbitrary"), plumb.collective_id(own=("sibling",))),
    )(core, *([a] * blocks), b, *plumb.args)
    return (res[0], plumb.split_outputs(res[1:])) if comm else res[0]


def _pair_task(parts):
    n = len(parts)

    def copies(ins, outs, scr):
        x, y, c, _ = _place()
        return [pltpu.make_async_remote_copy(
            src_ref=ins[k].at[:, pl.ds(1 - c, 1)], dst_ref=outs[k], send_sem=scr[0].at[k], recv_sem=scr[1].at[k],
            device_id=(x, y, 1 - c), device_id_type=MESH) for k in range(n)]

    def start(ins, outs, scr):
        for cp in copies(ins, outs, scr):
            cp.start()

    def finish(ins, outs, scr):
        for cp in copies(ins, outs, scr):
            cp.wait()

    out_shapes = [jax.ShapeDtypeStruct((4, 1) + p.shape[2:], p.dtype) for p in parts]
    scratch = [pltpu.SemaphoreType.DMA((n,)), pltpu.SemaphoreType.DMA((n,))]
    return _Task(parts, out_shapes, scratch, [(0, start), (1.0, finish)], ("sibling",))


def _pair_sum(name, part, got, core):
    _, _, r, C = part.shape

    def body(core_ref, p_ref, g_ref, o_ref):
        o_ref[0] = (p_ref[0, 0].astype(F32) + g_ref[0, 0].astype(F32)).astype(o_ref.dtype)

    return pl.pallas_call(
        body, name=name,
        grid_spec=pltpu.PrefetchScalarGridSpec(
            num_scalar_prefetch=1, grid=(4,),
            in_specs=[pl.BlockSpec((1, 1, r, C), lambda i, core_ref: (i, core_ref[0], 0, 0)),
                      pl.BlockSpec((1, 1, r, C), lambda i, core_ref: (i, 0, 0, 0))],
            out_specs=pl.BlockSpec((1, r, C), lambda i, core_ref: (i, 0, 0))),
        out_shape=jax.ShapeDtypeStruct((4, r, C), part.dtype), compiler_params=_params(("parallel",)),
    )(core, part, got)


def _ffn_bwd(tag, dy, dyb, x, gain, wgT, wuT, wd, saved, earlier=None):
    n, g, u, a = saved
    half = lambda accs, ex: _swiglu_bwd_epilogue([0.5 * accs[0]], ex)
    act_args = dict(tm=512, tn=1408, tk=D_MODEL, epilogue=half, extras=[(g, "tile", 0), (u, "tile", 0)], cols_outer=True)
    if earlier is None:
        sum_d = _dw_pair(tag + "_dw_down", a, dyb, 0.5)
        (dg, du), ((slots_d,),) = _mm(tag + "_d_act", [(dyb, wd, "nt", 0)], [BF, BF], comm=[_chip_task([sum_d])], **act_args)
        slots_e = None
        sum_g = _dw_pair(tag + "_dw_gate", dg, n, 1.0)
    else:
        sum_d, ((got,),) = _dw_pair(tag + "_dw_down", a, dyb, 0.5, comm=[_pair_task([earlier])])
        core = lax.axis_index("c").astype(jnp.int32).reshape(1)
        sum_e = _pair_sum(tag + "_pair_sum_earlier", earlier, got, core)
        sum_e = sum_e.reshape(4 * sum_e.shape[1], sum_e.shape[2])
        (dg, du), ((slots_e,),) = _mm(tag + "_d_act", [(dyb, wd, "nt", 0)], [BF, BF], comm=[_chip_task([sum_e])], **act_args)
        sum_g, ((slots_d,),) = _dw_pair(tag + "_dw_gate", dg, n, 1.0, comm=[_chip_task([sum_d])])
    sum_u, ((slots_g,),) = _dw_pair(tag + "_dw_up", du, n, 1.0, comm=[_chip_task([sum_g])])
    (dx, dxb, dgain), ((slots_u,),) = _mm(
        tag + "_d_norm", [(dg, wgT, "nn", 0), (du, wuT, "nn", 0)], [F32, BF], tm=512, tn=D_MODEL, tk=D_FF,
        epilogue=_rms_bwd_epilogue, extras=[(x, "tile", 0), (gain, "row", 0), (dy, "tile", 0)], n_colsum=1,
        comm=[_chip_task([sum_u])])
    return dx, dxb, dgain, slots_e, slots_g, slots_u, slots_d


def _tile_gain(g):
    return jnp.concatenate([g, g]).reshape(1, LANES)


def _fold_heads(partials):
    return jnp.sum(partials.reshape(-1, HEAD_DIM), axis=0)


def _pack_small_grads(grads, loss_local):
    pieces, row = [], 0
    for name, r0, _ in SMALL_LAYOUT + (("loss", LOSS_ROW, None),):
        v = (loss_local if name == "loss" else grads[name]).reshape(-1)
        rows = -(-v.size // LANES)
        block = jnp.pad(v, (0, rows * LANES - v.size)).reshape(rows, LANES)
        pieces += [jnp.zeros((r0 - row, LANES), F32)] * (r0 > row) + [block]
        row = r0 + rows
    pieces.append(jnp.zeros((SMALL_ROWS - row, LANES), F32))
    return jnp.concatenate(pieces, axis=0)


def kernel(x, ffn1_norm, ffn1_w_gate, ffn1_w_up, ffn1_w_down, mix_norm, w_in, pool_w, pool_scale, w_pool_out, q_norm, k_norm, sinks, w_attn_out, gate_bias, w_out, ffn2_norm, ffn2_w_gate, ffn2_w_up, ffn2_w_down, loss_target, m_ffn1_norm, m_ffn1_w_gate, m_ffn1_w_up, m_ffn1_w_down, m_mix_norm, m_w_in, m_pool_w, m_pool_scale, m_w_pool_out, m_q_norm, m_k_norm, m_sinks, m_w_attn_out, m_gate_bias, m_w_out, m_ffn2_norm, m_ffn2_w_gate, m_ffn2_w_up, m_ffn2_w_down, v_ffn1_norm, v_ffn1_w_gate, v_ffn1_w_up, v_ffn1_w_down, v_mix_norm, v_w_in, v_pool_w, v_pool_scale, v_w_pool_out, v_q_norm, v_k_norm, v_sinks, v_w_attn_out, v_gate_bias, v_w_out, v_ffn2_norm, v_ffn2_w_gate, v_ffn2_w_up, v_ffn2_w_down):
    T = x.shape[1]
    x2 = x.reshape(T, D_MODEL)
    target = loss_target.reshape(T, D_MODEL)

    big = [
        ("ffn1_w_gate", ffn1_w_gate, m_ffn1_w_gate, v_ffn1_w_gate, True, False),
        ("ffn1_w_up", ffn1_w_up, m_ffn1_w_up, v_ffn1_w_up, True, False),
        ("ffn1_w_down", ffn1_w_down, m_ffn1_w_down, v_ffn1_w_down, False, False),
        ("w_in", w_in, m_w_in, v_w_in, True, False),
        ("w_pool_out", w_pool_out, m_w_pool_out, v_w_pool_out, False, True),
        ("w_attn_out", w_attn_out, m_w_attn_out, v_w_attn_out, False, False),
        ("w_out", w_out, m_w_out, v_w_out, False, False),
        ("ffn2_w_gate", ffn2_w_gate, m_ffn2_w_gate, v_ffn2_w_gate, True, False),
        ("ffn2_w_up", ffn2_w_up, m_ffn2_w_up, v_ffn2_w_up, True, False),
        ("ffn2_w_down", ffn2_w_down, m_ffn2_w_down, v_ffn2_w_down, False, False),
    ]
    view = lambda a, tv: a.T if tv else a
    shards = _prep("prep_weights", [view(w, tv) for _, w, _, _, tv, _ in big], [tk_ for *_, tk_ in big])
    g1 =ffn1_norm.reshape(1, D_MODEL)
    g2 = mix_norm.reshape(1, D_MODEL)
    g3 = ffn2_norm.reshape(1, D_MODEL)
    bias_row = gate_bias.reshape(1, 2 * D_MODEL)
    qg, kg = _tile_gain(q_norm) * ATTN_SCALE, _tile_gain(k_norm)
    scale_row = pool_scale.reshape(1, POOL_WIDTH)

    n1, ((wg1T, wu1T),) = _rms_fwd("ffn1_norm", x2, g1, [_gather_task(shards[0:2], forward_at=0.9)])
    (gt1, up1, act1), ((wd1,), (w_inT,)) = _mm(
        "ffn1_gate_up", [(n1, wg1T, "nt", 0), (n1, wu1T, "nt", 1)], [BF, BF, BF], tm=512, tn=1408, tk=D_MODEL,
        epilogue=_swiglu_fwd_epilogue, cols_outer=True,
        comm=[_gather_task(shards[2:3], forward_at=0.5), _gather_task(shards[3:4], natural=(0,), forward_at=0.9)])
    (h1, u), ((w_poT, w_ao, w_o),) = _mm(
        "ffn1_down", [(act1, wd1, "nn", 0)], [F32, BF], tm=512, tn=D_MODEL, tk=D_FF,
        epilogue=_residual_norm_epilogue(0.5), extras=[(x2, "tile", 0), (g2, "row", 0)],
        comm=[_gather_task(shards[4:7], natural=(0, 1, 2), forward_at=0.8)])
    saved1 = (n1, gt1, up1, act1)
    (proj,), ((wg2T,),) = _mm(
        "in_proj", [(u, w_inT, "nt", 0)], [BF], tm=512, tn=1280, tk=D_MODEL, cols_outer=True,
        comm=[_gather_task(shards[7:8], forward_at=0.8)])
    pooled, mixed = _pool_fwd("pool_fwd", proj, pool_w, scale_row)
    qn = _headnorm_fwd("q_norm", proj, COL_Q, ATTN_WIDTH, qg)
    kn = _headnorm_fwd("k_norm", proj, COL_K, KV_WIDTH, kg)
    attn, ((wu2T,),) = _attn_fwd("attn_fwd", qn, kn, proj, sinks, comm=[_gather_task(shards[8:9], forward_at=0.8)])
    (bp,) = _mm("pool_out", [(mixed, w_poT, "nt", 0)], [BF], tm=1024, tn=D_MODEL, tk=POOL_WIDTH)
    gate_tn = 256
    gate_extras = [(proj, "tile", COL_GP // gate_tn), (proj, "tile", COL_GA // gate_tn),
                   (bias_row, "row", 0), (bias_row, "row", D_MODEL // gate_tn)]
    merged, ba = _mm("attn_out_merge", [(attn, w_ao, "nn", 0)], [BF, BF], tm=2048, tn=gate_tn, tk=ATTN_WIDTH,
                     epilogue=_merge_fwd_epilogue, extras=[(bp, "tile", 0)] + gate_extras)
    h2, n2 = _mm("mix_out", [(merged, w_o, "nn", 0)], [F32, BF], tm=512, tn=D_MODEL, tk=D_MODEL,
                 epilogue=_residual_norm_epilogue(1.0), extras=[(h1, "tile", 0), (g3, "row", 0)])
    (gt2, up2, act2), ((wd2,),) = _mm(
        "ffn2_gate_up", [(n2, wg2T, "nt", 0), (n2, wu2T, "nt", 1)], [BF, BF, BF], tm=512, tn=1408, tk=D_MODEL,
        epilogue=_swiglu_fwd_epilogue, cols_outer=True, comm=[_gather_task(shards[9:10], forward_at=0.8)])
    dy, dyb, sq = _mm("ffn2_down_loss", [(act2, wd2, "nn", 0)], [F32, BF], tm=512, tn=D_MODEL, tk=D_FF,
                      epilogue=_loss_epilogue, extras=[(h2, "tile", 0), (target, "tile", 0)], n_colsum=1)
    loss_local = 0.5 * jnp.sum(sq) / D_MODEL

    dh2, dh2b, dg3, _, slots_g2, slots_u2, slots_d2 = _ffn_bwd(
        "ffn2", dy, dyb, h2, g3, wg2T, wu2T, wd2, (n2, gt2, up2, act2))
    dbp, dba, dgp, dga, cs_gp, cs_ga = _mm(
        "mix_out_bwd", [(dh2b, w_o, "nt", 0)], [BF, BF, BF, BF], tm=2048, tn=gate_tn, tk=D_MODEL,
        epilogue=_merge_bwd_epilogue, extras=[(bp, "tile", 0), (ba, "tile", 0)] + gate_extras, n_colsum=2)
    sum_o = _dw_pair("dw_out", merged, dh2b, 1.0, blocks=4)
    (dmixed,) = _mm("pool_out_bwd", [(dbp, w_poT, "nn", 0)], [BF], tm=1024, tn=POOL_WIDTH, tk=D_MODEL)
    sum_po = _dw_pair("dw_pool_out", dbp, mixed, 1.0, blocks=4)
    (dattn,) = _mm("attn_out_bwd", [(dba, w_ao, "nt", 0)], [BF], tm=1024, tn=ATTN_WIDTH, tk=D_MODEL)
    sum_ao = _dw_pair("dw_attn_out", attn, dba, 1.0, blocks=4)
    dxp, dpool_w, dpool_scale = _pool_bwd("pool_bwd", dmixed, pooled, pool_w, scale_row)
    (dqn, dkn, dv, dsink_tile), ((slots_o, slots_po, slots_ao),) = _attn_bwd(
        "attn_bwd", dattn, qn, kn, proj, sinks, [_chip_task([sum_o, sum_po, sum_ao])])
    dq, dqg = _headnorm_bwd("q_norm_bwd", dqn, proj, COL_Q, ATTN_WIDTH, qg)
    dk, dkg = _headnorm_bwd("k_norm_bwd", dkn, proj, COL_K, KV_WIDTH, kg)
    dproj = jnp.concatenate([dxp, dq, dk, dv, dgp, dga], axis=1)
    (dh1, dh1b, dg2), ((g_pool_w,),) = _mm(
        "in_proj_bwd", [(dproj, w_inT, "nn", 0)], [F32, BF], tm=512, tn=D_MODEL, tk=IN_WIDTH, epilogue=_rms_bwd_epilogue,
        extras=[(h1, "tile", 0), (g2, "row", 0), (dh2, "tile", 0)], n_colsum=1,
        comm=[_gather_task([dpool_w.reshape(-1, LANES)])])
    (dw_inT,) = _mm("dw_in", [(dproj, u, "tn", 0)], [BF], tm=1280, tn=D_MODEL, tk=2048)
    dx, _, dg1, slots_in, slots_g1, slots_u1, slots_d1 = _ffn_bwd(
        "ffn1", dh1, dh1b, x2, g1, wg1T, wu1T, wd1, saved1, dw_inT.reshape(4, 2, IN_WIDTH // N_DEV, D_MODEL))

    slots = [slots_g1, slots_u1, slots_d1, slots_in, slots_po, slots_ao, slots_o, slots_g2, slots_u2, slots_d2]
    big_out = {}
    for label, group in (("ffn", (0, 1, 2, 7, 8, 9)), ("w_in", (3,)), ("w_pool_out", (4,)), ("attn_out_and_out", (5, 6))):
        items = [(slots[k], view(big[k][1], big[k][4]), view(big[k][2], big[k][4]), view(big[k][3], big[k][4]))
                 for k in group]
        for k, res in zip(group, _adamw_sharded("adamw_" + label, items, transpose=big[group[0]][5])):
            big_out[big[k][0]] = tuple(view(r, big[k][4]) for r in res)

    small_grads = {
        "ffn1_norm": jnp.sum(dg1, axis=(0, 1)), "mix_norm": jnp.sum(dg2, axis=(0, 1)), "ffn2_norm": jnp.sum(dg3, axis=(0, 1)),
        "gate_bias": jnp.concatenate([jnp.sum(cs_gp, axis=(0, 1)), jnp.sum(cs_ga, axis=(0, 1))]),
        "pool_scale": dpool_scale, "q_norm": _fold_heads(dqg) * ATTN_SCALE, "k_norm": _fold_heads(dkg),
        "sinks": dsink_tile[0, :N_HEADS]}
    ((g_vec,),) = _comm_only("gather_small_grads", [_direct_gather_task([_pack_small_grads(small_grads, loss_local)])])
    given = {"ffn1_norm": (ffn1_norm, m_ffn1_norm, v_ffn1_norm), "mix_norm": (mix_norm, m_mix_norm, v_mix_norm),
             "ffn2_norm": (ffn2_norm, m_ffn2_norm, v_ffn2_norm), "gate_bias": (gate_bias, m_gate_bias, v_gate_bias),
             "pool_scale": (pool_scale, m_pool_scale, v_pool_scale), "q_norm": (q_norm, m_q_norm, v_q_norm),
             "k_norm": (k_norm, m_k_norm, v_k_norm), "sinks": (sinks, m_sinks, v_sinks)}
    params = [tuple(a.reshape(shape) for a in given[nm]) for nm, _, shape in SMALL_LAYOUT]
    params.append(tuple(a.reshape(-1, LANES) for a in (pool_w, m_pool_w, v_pool_w)))
    small_res, loss_row = _adamw_small("adamw_small", g_vec.reshape(N_DEV, SMALL_ROWS, LANES),
                                       g_pool_w.reshape(N_DEV, -1, LANES), params)
    small_out = {nm: tuple(r.reshape(given[nm][0].shape) for r in res)
                 for (nm, _, _), res in zip(SMALL_LAYOUT, small_res)}
    small_out["pool_w"] = tuple(r.reshape(pool_w.shape) for r in small_res[-1])
    loss = loss_row[0, 0]

    order = ["ffn1_norm", "ffn1_w_gate", "ffn1_w_up", "ffn1_w_down", "mix_norm", "w_in", "pool_w", "pool_scale",
             "w_pool_out", "q_norm", "k_norm", "sinks", "w_attn_out", "gate_bias", "w_out", "ffn2_norm",
             "ffn2_w_gate", "ffn2_w_up", "ffn2_w_down"]
    every = {**big_out, **small_out}
    outs = [loss, dx.reshape(x.shape)]
    for j in range(4):
        outs += [every[nm][j] for nm in order]
    return tuple(outs)
```

```python
import functools

import jax
import jax.numpy as jnp
from jax import lax
from jax.experimental import pallas as pl
from jax.experimental.pallas import tpu as pltpu

BF = jnp.bfloat16
F32 = jnp.float32

D_MODEL = 1024
D_FF = 2816
POOL_WIDTH = 512
POOL_GROUP = 128
N_POOL_GROUPS = 4
HEAD_DIM = 64
N_HEADS = 16
GQA_GROUP = 8
BLOCK = 128
ATTN_WIDTH = 1024
KV_WIDTH = 128
IN_WIDTH = 3840
RMS_EPS = 1e-6
N_DEV = 8
LANES = 128

COL_Q = POOL_WIDTH
COL_K = COL_Q + ATTN_WIDTH
COL_V = COL_K + KV_WIDTH
COL_GP = COL_V + KV_WIDTH
COL_GA = COL_GP + D_MODEL

ADAM_LR = 0.001
ADAM_B1 = 0.9
ADAM_B2 = 0.999
ADAM_EPS = 1e-08
ADAM_WD = 0.01
ADAM_STEP = 10

VMEM_LIMIT_V7X = 56 * 1024 * 1024
MESH = pl.DeviceIdType.MESH
ANY = pl.BlockSpec(memory_space=pl.ANY)


def _params(sem=None, collective_id=None):
    return pltpu.CompilerParams(dimension_semantics=sem, vmem_limit_bytes=VMEM_LIMIT_V7X, collective_id=collective_id)


COLLECTIVE_IDS = {frozenset(["sibling"]): 0, frozenset(["chips"]): 1, frozenset(["sibling", "chips"]): 2}


def _handshake(peer_kinds):
    x, y, c, chips = _place()
    peers = ([(x, y, 1 - c)] if "sibling" in peer_kinds else []) + ([(*chip, c) for chip in chips] if "chips" in peer_kinds else [])
    barrier = pltpu.get_barrier_semaphore()
    for peer in peers:
        pl.semaphore_signal(barrier, inc=1, device_id=peer, device_id_type=MESH)
    pl.semaphore_wait(barrier, len(peers))


_DIMS = {"nt": (((1,), (1,)), ((), ())), "nn": (((1,), (0,)), ((), ())), "tn": (((0,), (0,)), ((), ()))}


class _Task:
    def __init__(self, inputs, out_shapes, scratch, phases, peers):
        self.inputs, self.out_shapes, self.scratch = list(inputs), list(out_shapes), list(scratch)
        self.phases = list(phases)
        self.peers = frozenset(peers)


class _CommPlumbing:
    def __init__(self, tasks):
        self.tasks = list(tasks or [])
        self.args = [a for t in self.tasks for a in t.inputs]
        self.out_shapes = [o for t in self.tasks for o in t.out_shapes]
        self.scratch = [s for t in self.tasks for s in t.scratch]
        self.n_in, self.n_out = len(self.args), len(self.out_shapes)

    def peer_kinds(self, own=()):
        kinds = frozenset(own).union(*[t.peers for t in self.tasks])
        return None if "all" in kinds or not kinds else kinds

    def collective_id(self, own=()):
        kinds = self.peer_kinds(own)
        return None if kinds is None else COLLECTIVE_IDS[kinds]

    def handshake(self, first, own=()):
        kinds = self.peer_kinds(own)
        if kinds is not None:
            pl.when(first)(functools.partial(_handshake, kinds))

    def _slices(self, c_in, c_out, c_scr):
        i = o = s = 0
        for t in self.tasks:
            yield t, c_in[i:i + len(t.inputs)], c_out[o:o + len(t.out_shapes)], c_scr[s:s + len(t.scratch)]
            i, o, s = i + len(t.inputs), o + len(t.out_shapes), s + len(t.scratch)

    def run(self, step, steps, before, c_in, c_out, c_scr):
        for t, ins, outs, scr in self._slices(c_in, c_out, c_scr):
            for frac, fn in t.phases:
                if step is None:
                    fn(ins, outs, scr)
                elif before == (frac == 0):
                    at = 0 if frac == 0 else max(0, min(steps, -(-int(round(frac * steps * 64)) // 64)) - 1)
                    pl.when(step == at)(functools.partial(fn, ins, outs, scr))

    def split_outputs(self, flat):
        res, o = [], 0
        for t in self.tasks:
            res.append(list(flat[o:o + len(t.out_shapes)]))
            o += len(t.out_shapes)
        return res


def _comm_only(name, tasks):
    plumb = _CommPlumbing(tasks)

    def body(*refs):
        c_in, c_out = refs[:plumb.n_in], refs[plumb.n_in: plumb.n_in + plumb.n_out]
        c_scr = refs[plumb.n_in + plumb.n_out:]
        plumb.run(None, 1, True, c_in, c_out, c_scr)

    res = pl.pallas_call(
        body, name=name, in_specs=[ANY] * plumb.n_in, out_specs=[ANY] * plumb.n_out, out_shape=plumb.out_shapes,
        scratch_shapes=plumb.scratch, compiler_params=pltpu.CompilerParams(has_side_effects=True),
    )(*plumb.args)
    return plumb.split_outputs(res)


def _mm(name, terms, out_dtypes, *, tm, tn, tk, epilogue=None, extras=(), n_colsum=0, comm=None, cols_outer=False):
    a0, b0, mode0, _ = terms[0]
    if mode0 == "nt":
        (M, K), N = a0.shape, b0.shape[0]
    elif mode0 == "nn":
        (M, K), N = a0.shape, b0.shape[1]
    else:
        (K, M), N = a0.shape, b0.shape[1]
    tm, tn, tk = min(tm, M), min(tn, N), min(tk, K)
    assert M % tm == 0 and N % tn == 0 and K % tk == 0, (name, M, N, K, tm, tn, tk)
    nI, nJ, nK = M // tm, N // tn, K // tk
    n_terms = len(terms)
    n_acc = max(t[3] for t in terms) + 1
    n_ex = len(extras)
    n_out = len(out_dtypes)
    if epilogue is None:
        epilogue = lambda accs, ex: ([accs[0]], [])
    plumb = _CommPlumbing(comm)
    n_scr = n_acc if nK > 1 else 0
    grid = (nJ, nI, nK) if cols_outer else (nI, nJ, nK)

    def body(*refs):
        n_in = 2 * n_terms + n_ex
        ab = refs[: 2 * n_terms]
        ex_refs = refs[2 * n_terms: n_in]
        c_in = refs[n_in: n_in + plumb.n_in]
        o0 = n_in + plumb.n_in
        out_refs = refs[o0: o0 + n_out]
        cs_refs = refs[o0 + n_out: o0 + n_out + n_colsum]
        c_out = refs[o0 + n_out + n_colsum: o0 + n_out + n_colsum + plumb.n_out]
        s0 = o0 + n_out + n_colsum + plumb.n_out
        acc_refs = refs[s0: s0 + n_scr]
        c_scr = refs[s0 + n_scr:]
        steps = grid[0] * grid[1] * nK
        if comm:
            step = (pl.program_id(0) * grid[1] + pl.program_id(1)) * nK + pl.program_id(2)
            plumb.handshake(step == 0)
            plumb.run(step, steps, True, c_in, c_out, c_scr)

        def products():
            accs = [None] * n_acc
            for t, (_, _, mode, ai) in enumerate(terms):
                p = lax.dot_general(ab[2 * t][...], ab[2 * t + 1][...], _DIMS[mode], preferred_element_type=F32)
                accs[ai] = p if accs[ai] is None else accs[ai] + p
            return accs

        def finish(accs):
            outs, colsums = epilogue(accs, [r[...] for r in ex_refs])
            for r, o in zip(out_refs, outs):
                r[...] = o.astype(r.dtype)
            for r, cs in zip(cs_refs, colsums):
                r[...] = jnp.sum(cs, axis=0, keepdims=True).reshape(r.shape)

        if nK == 1:
            finish(products())
        else:
            k = pl.program_id(2)
            accs = products()

            @pl.when(k == 0)
            def _():
                for r, a in zip(acc_refs, accs):
                    r[...] = a

            @pl.when(k > 0)
            def _():
                for r, a in zip(acc_refs, accs):
                    r[...] += a

            @pl.when(k == nK - 1)
            def _():
                finish([r[...] for r in acc_refs])

        if comm:
            plumb.run(step, steps, False, c_in, c_out, c_scr)

    def spec(block, index, fixed=False):
        imap = (lambda q, p, k: index(p, q, k)) if cols_outer else index
        return pl.BlockSpec(block, imap, pipeline_mode=pl.Buffered(1)) if fixed else pl.BlockSpec(block, imap)

    in_specs, args = [], []
    for a, b, mode, _ in terms:
        if mode == "nt":
            in_specs += [spec((tm, tk), lambda i, j, k: (i, k), nI * nK == 1),
                         spec((tn, tk), lambda i, j, k: (j, k), nJ * nK == 1)]
        elif mode == "nn":
            in_specs += [spec((tm, tk), lambda i, j, k: (i, k), nI * nK == 1),
                         spec((tk, tn), lambda i, j, k: (k, j), nJ * nK == 1)]
        else:
            in_specs += [spec((tk, tm), lambda i, j, k: (k, i), nI * nK == 1),
                         spec((tk, tn), lambda i, j, k: (k, j), nJ * nK == 1)]
        args += [a, b]
    for arr, kind, off in extras:
        if kind == "tile":
            in_specs.append(spec((tm, tn), functools.partial(lambda i, j, k, off: (i, j + off), off=off)))
        else:
            in_specs.append(spec((1, tn), functools.partial(lambda i, j, k, off: (0, j + off), off=off)))
        args.append(arr)
    out_shape = [jax.ShapeDtypeStruct((M, N), dt) for dt in out_dtypes]
    out_specs = [spec((tm, tn), lambda i, j, k: (i, j)) for _ in out_dtypes]
    out_shape += [jax.ShapeDtypeStruct((nI, 1, N), F32) for _ in range(n_colsum)]
    out_specs += [spec((1, 1, tn), lambda i, j, k: (i, 0, j)) for _ in range(n_colsum)]
    scratch = [pltpu.VMEM((tm, tn), F32) for _ in range(n_scr)]
    args += plumb.args
    in_specs += [ANY] * plumb.n_in
    out_shape += plumb.out_shapes
    out_specs += [ANY] * plumb.n_out
    sem = ("arbitrary",) * 3 if comm else ("parallel", "parallel", "arbitrary")
    res = pl.pallas_call(
        body, name=name, grid=grid, in_specs=in_specs, out_specs=out_specs, out_shape=out_shape,
        scratch_shapes=scratch + plumb.scratch, compiler_params=_params(sem, plumb.collective_id()),
    )(*args)
    n_own = n_out + n_colsum
    return (list(res[:n_own]), plumb.split_outputs(res[n_own:])) if comm is not None else res


ROW_TILE = 512


def _rms_fwd(name, x, g, comm):
    T, D = x.shape
    steps = T // ROW_TILE
    plumb = _CommPlumbing(comm)

    def body(x_ref, g_ref, *rest):
        c_in, o_ref = rest[:plumb.n_in], rest[plumb.n_in]
        c_out, c_scr = rest[plumb.n_in + 1: plumb.n_in + 1 + plumb.n_out], rest[plumb.n_in + 1 + plumb.n_out:]
        plumb.handshake(pl.program_id(0) == 0)
        plumb.run(pl.program_id(0), steps, True, c_in, c_out, c_scr)
        xv = x_ref[...]
        r = lax.rsqrt(jnp.mean(xv * xv, axis=-1, keepdims=True) + RMS_EPS)
        o_ref[...] = (xv * r * g_ref[...]).astype(BF)
        plumb.run(pl.program_id(0), steps, False, c_in, c_out, c_scr)

    row = pl.BlockSpec((ROW_TILE, D), lambda i: (i, 0))
    res = pl.pallas_call(
        body, name=name, grid=(steps,),
        in_specs=[row, pl.BlockSpec((1, D), lambda i: (0, 0))] + [ANY] * plumb.n_in,
        out_specs=[row] + [ANY] * plumb.n_out, out_shape=[jax.ShapeDtypeStruct((T, D), BF)] + plumb.out_shapes,
        scratch_shapes=plumb.scratch, compiler_params=_params(("arbitrary",), plumb.collective_id()),
    )(x, g, *plumb.args)
    return res[0], plumb.split_outputs(res[1:])


HEADNORM_TILE = 1024


def _half_sum_matrix():
    r = lax.broadcasted_iota(jnp.int32, (LANES, LANES), 0) // HEAD_DIM
    c = lax.broadcasted_iota(jnp.int32, (LANES, LANES), 1) // HEAD_DIM
    return (r == c).astype(BF)


def _head_mean(v, ones_blockdiag):
    hi = v.astype(BF)
    lo = (v - hi.astype(F32)).astype(BF)
    s = jnp.dot(hi, ones_blockdiag, preferred_element_type=F32) + jnp.dot(lo, ones_blockdiag, preferred_element_type=F32)
    return s * (1.0 / HEAD_DIM)


def _headnorm_fwd(name, proj, col0, width, g2):
    T = proj.shape[0]
    wide = min(width, GROUP_WIDTH)
    nb, off = width // wide, col0 // wide

    def body(x_ref, g_ref, b_ref, o_ref):
        for s in range(wide // LANES):
            lanes = slice(LANES * s, LANES * (s + 1))
            xv = x_ref[:, lanes].astype(F32)
            r = lax.rsqrt(_head_mean(xv * xv, b_ref[...]) + RMS_EPS)
            o_ref[:, lanes] = (xv * r * g_ref[...]).astype(BF)

    return pl.pallas_call(
        body, name=name, grid=(T // HEADNORM_TILE, nb),
        in_specs=[pl.BlockSpec((HEADNORM_TILE, wide), lambda i, j: (i, j + off)),
                  pl.BlockSpec((1, LANES), lambda i, j: (0, 0)), pl.BlockSpec((LANES, LANES), lambda i, j: (0, 0))],
        out_specs=pl.BlockSpec((HEADNORM_TILE, wide), lambda i, j: (i, j)),
        out_shape=jax.ShapeDtypeStruct((T, width), BF), compiler_params=_params(("parallel", "parallel")),
    )(proj, g2, _half_sum_matrix())


def _headnorm_bwd(name, dy, proj, col0, width, g2):
    T = proj.shape[0]
    wide = min(width, GROUP_WIDTH)
    nb, off = width // wide, col0 // wide

    def body(dy_ref, x_ref, g_ref, b_ref, dx_ref, dg_ref):
        for s in range(wide // LANES):
            lanes = slice(LANES * s, LANES * (s + 1))
            xv = x_ref[:, lanes].astype(F32)
            dyv = dy_ref[:, lanes].astype(F32)
            r = lax.rsqrt(_head_mean(xv * xv, b_ref[...]) + RMS_EPS)
            xhat = xv * r
            dxhat = dyv * g_ref[...]
            dx_ref[:, lanes] = (r * (dxhat - xhat * _head_mean(dxhat * xhat, b_ref[...]))).astype(BF)
            dg_ref[0, :, lanes] = jnp.sum(dyv * xhat, axis=0, keepdims=True)

    return pl.pallas_call(
        body, name=name, grid=(T // HEADNORM_TILE, nb),
        in_specs=[pl.BlockSpec((HEADNORM_TILE, wide), lambda i, j: (i, j)),
                  pl.BlockSpec((HEADNORM_TILE, wide), lambda i, j: (i, j + off)),
                  pl.BlockSpec((1, LANES), lambda i, j: (0, 0)), pl.BlockSpec((LANES, LANES), lambda i, j: (0, 0))],
        out_specs=[pl.BlockSpec((HEADNORM_TILE, wide), lambda i, j: (i, j)),
                   pl.BlockSpec((1, 1, wide), lambda i, j: (i, 0, j))],
        out_shape=[jax.ShapeDtypeStruct((T, width), BF), jax.ShapeDtypeStruct((T // HEADNORM_TILE, 1, width), F32)],
        compiler_params=_params(("parallel", "parallel")),
    )(dy, proj, g2, _half_sum_matrix())


def _shift_down(v, k, row):
    return jnp.where(row >= k, pltpu.roll(v, k, axis=0), 0.0)


def _shift_up(v, k, row, T):
    return jnp.where(row < T - k, pltpu.roll(v, T - k, axis=0), 0.0)


def _by_group(g, vals):
    out = vals[-1]
    for i in range(len(vals) - 2, -1, -1):
        out = jnp.where(g == i, vals[i], out)
    return out


def _pool_fwd(name, proj, pool_w, pool_scale):
    T = proj.shape[0]

    def body(x_ref, w_ref, s_ref, pooled_ref, mixed_ref):
        g = pl.program_id(0)
        xv = x_ref[...].astype(F32)
        row = lax.broadcasted_iota(jnp.int32, (T, 1), 0)
        s2 = xv + _shift_down(xv, 1, row)
        s4 = s2 + _shift_down(s2, 2, row)
        s8 = s4 + _shift_down(s4, 4, row)
        s16 = s8 + _shift_down(s8, 8, row)
        wsum = _by_group(g, [s2, s4, s8, s16])
        count = jnp.minimum(row + 1, 2 << g).astype(F32)
        pooled = (wsum / count - xv).astype(BF)
        pooled_ref[...] = pooled
        mixed = jnp.dot(pooled, w_ref[0].astype(BF), preferred_element_type=F32) * s_ref[...]
        mixed_ref[...] = mixed.astype(BF)

    col = pl.BlockSpec((T, POOL_GROUP), lambda g: (0, g))
    return pl.pallas_call(
        body, name=name, grid=(N_POOL_GROUPS,),
        in_specs=[col, pl.BlockSpec((1, POOL_GROUP, POOL_GROUP), lambda g: (g, 0, 0)),
                  pl.BlockSpec((1, POOL_GROUP), lambda g: (0, g))],
        out_specs=[col, col],
        out_shape=[jax.ShapeDtypeStruct((T, POOL_WIDTH), BF), jax.ShapeDtypeStruct((T, POOL_WIDTH), BF)],
        compiler_params=_params(("parallel",)),
    )(proj, pool_w, pool_scale)


def _pool_bwd(name, dmixed, pooled, pool_w, pool_scale):
    T = dmixed.shape[0]

    def body(dm_ref, p_ref, w_ref, s_ref, dx_ref, dw_ref, ds_ref):
        g = pl.program_id(0)
        dm = dm_ref[...].astype(F32)
        pooled = p_ref[...]
        w = w_ref[0].astype(BF)
        pre = jnp.dot(pooled, w, preferred_element_type=F32)
        ds_ref[...] = jnp.sum(dm * pre, axis=0, keepdims=True)
        dms = (dm * s_ref[...]).astype(BF)
        dw_ref[0] = lax.dot_general(pooled, dms, _DIMS["tn"], preferred_element_type=F32)
        dpooled = lax.dot_general(dms, w, _DIMS["nt"], preferred_element_type=F32)
        row = lax.broadcasted_iota(jnp.int32, (T, 1), 0)
        count = jnp.minimum(row + 1, 2 << g).astype(F32)
        z = dpooled / count
        l2 = z + _shift_up(z, 1, row, T)
        l4 = l2 + _shift_up(l2, 2, row, T)
        l8 = l4 + _shift_up(l4, 4, row, T)
        l16 = l8 + _shift_up(l8, 8, row, T)
        dx_ref[...] = (_by_group(g, [l2, l4, l8, l16]) - dpooled).astype(BF)

    col = pl.BlockSpec((T, POOL_GROUP), lambda g: (0, g))
    wspec = pl.BlockSpec((1, POOL_GROUP, POOL_GROUP), lambda g: (g, 0, 0))
    sspec = pl.BlockSpec((1, POOL_GROUP), lambda g: (0, g))
    return pl.pallas_call(
        body, name=name, grid=(N_POOL_GROUPS,), in_specs=[col, col, wspec, sspec], out_specs=[col, wspec, sspec],
        out_shape=[jax.ShapeDtypeStruct((T, POOL_WIDTH), BF),
                   jax.ShapeDtypeStruct((N_POOL_GROUPS, POOL_GROUP, POOL_GROUP), F32),
                   jax.ShapeDtypeStruct((1, POOL_WIDTH), F32)],
        compiler_params=_params(("parallel",)),
    )(dmixed, pooled, pool_w, pool_scale)


ATTN_SCALE = HEAD_DIM ** -0.5
MASKED = float(jnp.finfo(jnp.float32).min)
KV_COL_BLOCK_V = COL_V // LANES
GROUP_WIDTH = GQA_GROUP * HEAD_DIM


def _dup_head(v, j):
    half = lax.broadcasted_iota(jnp.int32, (1, LANES), 1) // HEAD_DIM
    return jnp.where(half == j, v, pltpu.roll(v, HEAD_DIM, axis=1))


def _stack_heads(v, low):
    pieces = []
    for p in range(GROUP_WIDTH // LANES):
        vp = v[:, LANES * p: LANES * (p + 1)]
        pieces.append(jnp.where(low, vp, jnp.zeros_like(vp)))
        pieces.append(jnp.where(low, jnp.zeros_like(vp), vp))
    return jnp.concatenate(pieces, axis=0)


def _unstack_transposed(t, low):
    pairs = []
    for p in range(GROUP_WIDTH // LANES):
        even = t[:, BLOCK * (2 * p): BLOCK * (2 * p + 1)].T
        odd = t[:, BLOCK * (2 * p + 1): BLOCK * (2 * p + 2)].T
        pairs.append(jnp.where(low, even, odd))
    return pairs


STACKED = GQA_GROUP * BLOCK


def _band_bias():
    key = lax.broadcasted_iota(jnp.int32, (2, 2 * BLOCK, STACKED), 1)
    qry = lax.broadcasted_iota(jnp.int32, (2, 2 * BLOCK, STACKED), 2) % BLOCK
    first = lax.broadcasted_iota(jnp.int32, (2, 2 * BLOCK, STACKED), 0) == 0
    valid = (key > qry) & (key <= qry + BLOCK) & (jnp.logical_not(first) | (key >= BLOCK))
    return jnp.where(valid, 0.0, MASKED).astype(F32)


BIAS_SPEC = pl.BlockSpec((1, 2 * BLOCK, STACKED), lambda n: (jnp.minimum(n, 1), 0, 0))


def _softmax_keys_on_sublanes(k2, q, bias, sink_ref, j):
    head_of_lane = lax.broadcasted_iota(jnp.int32, (1, STACKED), 1) // BLOCK
    sink = jnp.zeros((1, STACKED), F32)
    for h in range(GQA_GROUP):
        sink = jnp.where(head_of_lane == h, sink_ref[j * GQA_GROUP + h], sink)
    s = lax.dot_general(k2, q, _DIMS["nt"], preferred_element_type=F32) + bias
    m = jnp.maximum(jnp.max(s, axis=0, keepdims=True), sink)
    e = jnp.exp(s - m)
    e_sink = jnp.exp(sink - m)
    inv = 1.0 / (jnp.sum(e, axis=0, keepdims=True) + e_sink)
    return e * inv, e_sink * inv


def _attn_fwd(name, qn, kn, proj, sinks, comm=None):
    T = qn.shape[0]
    nb = T // BLOCK
    plumb = _CommPlumbing(comm)

    def body(sink_ref, bias_ref, q_ref, kp_ref, kc_ref, vp_ref, vc_ref, *rest):
        c_in, o_ref = rest[:plumb.n_in], rest[plumb.n_in]
        c_out, c_scr = rest[plumb.n_in + 1: plumb.n_in + 1 + plumb.n_out], rest[plumb.n_in + 1 + plumb.n_out:]
        n = pl.program_id(0)
        plumb.handshake(n == 0)
        plumb.run(n, nb, True, c_in, c_out, c_scr)
        low = lax.broadcasted_iota(jnp.int32, (1, LANES), 1) < HEAD_DIM
        kk = jnp.concatenate([kp_ref[...], kc_ref[...]], axis=0)
        vv = jnp.concatenate([vp_ref[...], vc_ref[...]], axis=0)
        for j in range(2):
            q = _stack_heads(q_ref[:, GROUP_WIDTH * j: GROUP_WIDTH * (j + 1)], low)
            p, _ = _softmax_keys_on_sublanes(_dup_head(kk, j), q, bias_ref[0], sink_ref, j)
            o_t = lax.dot_general(_dup_head(vv, j), p.astype(BF), _DIMS["tn"], preferred_element_type=F32)
            for pair, o in enumerate(_unstack_transposed(o_t, low)):
                lanes = slice(GROUP_WIDTH * j + LANES * pair, GROUP_WIDTH * j + LANES * (pair + 1))
                o_ref[:, lanes] = o.astype(BF)
        plumb.run(n, nb, False, c_in, c_out, c_scr)

    wide = pl.BlockSpec((BLOCK, ATTN_WIDTH), lambda n: (n, 0))
    res = pl.pallas_call(
        body, name=name, grid=(nb,),
        in_specs=[pl.BlockSpec(memory_space=pltpu.SMEM), BIAS_SPEC, wide,
                  pl.BlockSpec((BLOCK, LANES), lambda n: (jnp.maximum(n - 1, 0), 0)),
                  pl.BlockSpec((BLOCK, LANES), lambda n: (n, 0)),
                  pl.BlockSpec((BLOCK, LANES), lambda n: (jnp.maximum(n - 1, 0), KV_COL_BLOCK_V)),
                  pl.BlockSpec((BLOCK, LANES), lambda n: (n, KV_COL_BLOCK_V))] + [ANY] * plumb.n_in,
        out_specs=[wide] + [ANY] * plumb.n_out,
        out_shape=[jax.ShapeDtypeStruct((T, ATTN_WIDTH), BF)] + plumb.out_shapes, scratch_shapes=plumb.scratch,
        compiler_params=_params(("arbitrary",) if comm else ("parallel",), plumb.collective_id()),
    )(sinks, _band_bias(), qn, kn, kn, proj, proj, *plumb.args)
    return (res[0], plumb.split_outputs(res[1:])) if comm is not None else res[0]


def _attn_bwd(name, dout, qn, kn, proj, sinks, comm):
    T = qn.shape[0]
    nb = T // BLOCK
    plumb = _CommPlumbing(comm)

    def body(sink_ref, bias_ref, do_ref, q_ref, kp_ref, kc_ref, vp_ref, vc_ref, *rest):
        c_in, (dq_ref, dk_ref, dv_ref, dsink_ref) = rest[:plumb.n_in], rest[plumb.n_in: plumb.n_in + 4]
        c_out = rest[plumb.n_in + 4: plumb.n_in + 4 + plumb.n_out]
        carry_k, carry_v, tot_k, tot_v = rest[plumb.n_in + 4 + plumb.n_out: plumb.n_in + 8 + plumb.n_out]
        c_scr = rest[plumb.n_in + 8 + plumb.n_out:]
        n = pl.program_id(0)
        plumb.handshake(n == 0)
        plumb.run(n, nb + 1, True, c_in, c_out, c_scr)
        lane = lax.broadcasted_iota(jnp.int32, (1, LANES), 1)
        low = lane < HEAD_DIM

        @pl.when(n == 0)
        def _():
            carry_k[...] = jnp.zeros_like(carry_k)
            carry_v[...] = jnp.zeros_like(carry_v)
            dsink_ref[...] = jnp.zeros_like(dsink_ref)

        @pl.when(n == nb)
        def _():
            tot_k[...] = jnp.zeros_like(tot_k)
            tot_v[...] = jnp.zeros_like(tot_v)

        @pl.when(n < nb)
        def _():
            kk = jnp.concatenate([kp_ref[...], kc_ref[...]], axis=0)
            vv = jnp.concatenate([vp_ref[...], vc_ref[...]], axis=0)
            dk_tot = jnp.zeros((2 * BLOCK, LANES), F32)
            dv_tot = jnp.zeros((2 * BLOCK, LANES), F32)
            dsink = jnp.zeros((1, LANES), F32)
            for j in range(2):
                k2 = _dup_head(kk, j)
                v2 = _dup_head(vv, j)
                q = _stack_heads(q_ref[:, GROUP_WIDTH * j: GROUP_WIDTH * (j + 1)], low)
                do = _stack_heads(do_ref[:, GROUP_WIDTH * j: GROUP_WIDTH * (j + 1)], low)
                p, psink = _softmax_keys_on_sublanes(k2, q, bias_ref[0], sink_ref, j)
                dp =lax.dot_general(v2, do, _DIMS["nt"], preferred_element_type=F32)
                delta = jnp.sum(p * dp, axis=0, keepdims=True)
                ds = (p * (dp - delta)).astype(BF)
                dk2 = jnp.dot(ds, q, preferred_element_type=F32)
                dv2 = jnp.dot(p.astype(BF), do, preferred_element_type=F32)
                dq_t = lax.dot_general(k2, ds, _DIMS["tn"], preferred_element_type=F32)
                for pair, dq in enumerate(_unstack_transposed(dq_t, low)):
                    lanes = slice(GROUP_WIDTH * j + LANES * pair, GROUP_WIDTH * j + LANES * (pair + 1))
                    dq_ref[:, lanes] = dq.astype(BF)
                mine = low if j == 0 else jnp.logical_not(low)
                dk_tot = dk_tot + jnp.where(mine, dk2 + pltpu.roll(dk2, HEAD_DIM, axis=1), 0.0)
                dv_tot = dv_tot + jnp.where(mine, dv2 + pltpu.roll(dv2, HEAD_DIM, axis=1), 0.0)
                sink_term = psink * delta
                for h in range(GQA_GROUP):
                    val = -jnp.sum(sink_term[:, BLOCK * h: BLOCK * (h + 1)], axis=1, keepdims=True)
                    dsink = dsink + jnp.where(lane == j * GQA_GROUP + h, val, 0.0)
            tot_k[...] = dk_tot
            tot_v[...] = dv_tot
            dsink_ref[0:1, :] += dsink

        dk_ref[...] = (carry_k[...] + tot_k[0:BLOCK]).astype(BF)
        dv_ref[...] = (carry_v[...] + tot_v[0:BLOCK]).astype(BF)
        carry_k[...] = tot_k[BLOCK:]
        carry_v[...] = tot_v[BLOCK:]
        plumb.run(n, nb + 1, False, c_in, c_out, c_scr)

    cur = lambda n: (jnp.minimum(n, nb - 1), 0)
    prev = lambda n: (jnp.maximum(n - 1, 0), 0)
    wide = pl.BlockSpec((BLOCK, ATTN_WIDTH), cur)
    res = pl.pallas_call(
        body, name=name, grid=(nb + 1,),
        in_specs=[pl.BlockSpec(memory_space=pltpu.SMEM), BIAS_SPEC, wide, wide,
                  pl.BlockSpec((BLOCK, LANES), prev), pl.BlockSpec((BLOCK, LANES), cur),
                  pl.BlockSpec((BLOCK, LANES), lambda n: (jnp.maximum(n - 1, 0), KV_COL_BLOCK_V)),
                  pl.BlockSpec((BLOCK, LANES), lambda n: (jnp.minimum(n, nb - 1), KV_COL_BLOCK_V))] + [ANY] * plumb.n_in,
        out_specs=[wide, pl.BlockSpec((BLOCK, LANES), prev), pl.BlockSpec((BLOCK, LANES), prev),
                   pl.BlockSpec((8, LANES), lambda n: (0, 0))] + [ANY] * plumb.n_out,
        out_shape=[jax.ShapeDtypeStruct((T, ATTN_WIDTH), BF), jax.ShapeDtypeStruct((T, KV_WIDTH), BF),
                   jax.ShapeDtypeStruct((T, KV_WIDTH), BF), jax.ShapeDtypeStruct((8, LANES), F32)] + plumb.out_shapes,
        scratch_shapes=[pltpu.VMEM((BLOCK, LANES), F32), pltpu.VMEM((BLOCK, LANES), F32),
                        pltpu.VMEM((2 * BLOCK, LANES), F32), pltpu.VMEM((2 * BLOCK, LANES), F32)] + plumb.scratch,
        compiler_params=_params(("arbitrary",), plumb.collective_id()),
    )(sinks, _band_bias(), dout, qn, kn, kn, proj, proj, *plumb.args)
    return list(res[:4]), plumb.split_outputs(res[4:])


def _swiglu_fwd_epilogue(accs, ex):
    g, u = accs
    return [g, u, g * jax.nn.sigmoid(g) * u], []


def _swiglu_bwd_epilogue(accs, ex):
    (da,) = accs
    g, u = ex[0].astype(F32), ex[1].astype(F32)
    s = jax.nn.sigmoid(g)
    gs = g * s
    return [da * u * (s + gs - gs * s), da * gs], []


def _residual_norm_epilogue(scale):
    def epilogue(accs, ex):
        res, gain = ex
        h = res + scale * accs[0]
        r = lax.rsqrt(jnp.mean(h * h, axis=-1, keepdims=True) + RMS_EPS)
        return [h, h * r * gain], []
    return epilogue


def _rms_bwd_epilogue(accs, ex):
    (dn,) = accs
    xv, g, dres = ex
    r = lax.rsqrt(jnp.mean(xv * xv, axis=-1, keepdims=True) + RMS_EPS)
    xhat = xv * r
    dxhat = dn * g
    dx = dres + r * (dxhat - xhat * jnp.mean(dxhat * xhat, axis=-1, keepdims=True))
    return [dx, dx], [dn * xhat]


def _loss_epilogue(accs, ex):
    xv, target = ex
    d = xv + 0.5 * accs[0] - target
    dy = d * (1.0 / D_MODEL)
    return [dy, dy], [d * d]


def _merge_fwd_epilogue(accs, ex):
    (ba,) = accs
    bp, gp_pre, ga_pre, bias_p, bias_a = ex
    gp = jax.nn.sigmoid(gp_pre.astype(F32) + bias_p)
    ga = jax.nn.sigmoid(ga_pre.astype(F32) + bias_a)
    return [gp * bp.astype(F32) + ga * ba, ba], []


def _merge_bwd_epilogue(accs, ex):
    (dm,) = accs
    bp, ba, gp_pre, ga_pre, bias_p, bias_a = ex
    gp = jax.nn.sigmoid(gp_pre.astype(F32) + bias_p)
    ga = jax.nn.sigmoid(ga_pre.astype(F32) + bias_a)
    dbp, dba = dm * gp, dm * ga
    dgp = dbp * bp.astype(F32) * (1.0 - gp)
    dga = dba * ba.astype(F32) * (1.0 - ga)
    return [dbp, dba, dgp, dga], [dgp, dga]


def _prep(name, ws, transposes):
    n = len(ws)

    def body(*refs):
        for w_ref, o_ref, tr in zip(refs[:n], refs[n:], transposes):
            v = w_ref[...]
            o_ref[...] = (v.T if tr else v).astype(BF)

    shapes = [jax.ShapeDtypeStruct(w.shape[::-1] if tr else w.shape, BF) for w, tr in zip(ws, transposes)]
    return pl.pallas_call(body, name=name, out_shape=shapes, compiler_params=_params())(*ws)


def _adam_math(w, g, m, v):
    m = ADAM_B1 * m + (1.0 - ADAM_B1) * g
    v = ADAM_B2 * v + (1.0 - ADAM_B2) * jnp.square(g)
    m_hat = m / (1.0 - ADAM_B1 ** ADAM_STEP)
    v_hat = v / (1.0 - ADAM_B2 ** ADAM_STEP)
    delta = -ADAM_LR * (m_hat / (jnp.sqrt(v_hat) + ADAM_EPS) + ADAM_WD * w)
    return delta, m, v


def _adamw_sharded(name, items, transpose=False):
    n = len(items)

    def body(*refs):
        ins, outs = refs[:4 * n], refs[4 * n:]
        for k in range(n):
            s_ref, w_ref, m_ref, v_ref = ins[4 * k: 4 * k + 4]
            g = s_ref[0].astype(F32)
            for i in range(1, 4):
                g = g + s_ref[i].astype(F32)
            if transpose:
                g = g.T
            delta, mn, vn = _adam_math(w_ref[...], g, m_ref[...], v_ref[...])
            for o_ref, val in zip(outs[4 * k: 4 * k + 4], (g, delta, mn, vn)):
                o_ref[...] = val

    flat = [a for item in items for a in item]
    out_shape = [jax.ShapeDtypeStruct(item[1].shape, F32) for item in items for _ in range(4)]
    _, r, C = items[0][0].shape
    rows = r // 4
    if transpose or rows % 8:
        res = pl.pallas_call(body, name=name, out_shape=out_shape, compiler_params=_params())(*flat)
    else:
        tile = pl.BlockSpec((rows, C), lambda i: (i, 0))
        res = pl.pallas_call(
            body, name=name, grid=(4,), in_specs=[pl.BlockSpec((4, rows, C), lambda i: (0, i, 0)), tile, tile, tile] * n,
            out_specs=[tile] * (4 * n), out_shape=out_shape, compiler_params=_params(("parallel",)),
        )(*flat)
    return [tuple(res[4 * k: 4 * k + 4]) for k in range(n)]


SMALL_LAYOUT = (("ffn1_norm", 0, (8, LANES)), ("mix_norm", 8, (8, LANES)), ("ffn2_norm", 16, (8, LANES)),
                ("gate_bias", 24, (16, LANES)), ("pool_scale", 40, (4, LANES)), ("q_norm", 48, (1, HEAD_DIM)),
                ("k_norm", 56, (1, HEAD_DIM)), ("sinks", 64, (1, N_HEADS)))
LOSS_ROW = 72
SMALL_ROWS = 80


def _adamw_small(name, g_vec, g_pool_w, params):
    n = len(SMALL_LAYOUT) + 1

    def body(vec_ref, pw_ref, *refs):
        ins, outs = refs[:3 * n], refs[3 * n:]
        vec = vec_ref[0]
        pw = pw_ref[0]
        for i in range(1, N_DEV):
            vec = vec + vec_ref[i]
            pw = pw + pw_ref[i]
        grads = [vec[r0:r0 + shape[0], 0:shape[1]] for _, r0, shape in SMALL_LAYOUT] + [pw]
        for p, g in enumerate(grads):
            w_ref, m_ref, v_ref = ins[3 * p: 3 * p + 3]
            delta, mn, vn = _adam_math(w_ref[...], g, m_ref[...], v_ref[...])
            for o_ref, val in zip(outs[4 * p: 4 * p + 4], (g, delta, mn, vn)):
                o_ref[...] = val
        outs[4 * n][...] = vec[LOSS_ROW:LOSS_ROW + 1, :]

    flat = [a for wmv in params for a in wmv]
    out_shape = [jax.ShapeDtypeStruct(wmv[0].shape, F32) for wmv in params for _ in range(4)]
    out_shape.append(jax.ShapeDtypeStruct((1, LANES), F32))
    res = pl.pallas_call(body, name=name, out_shape=out_shape, compiler_params=_params())(g_vec, g_pool_w, *flat)
    return [tuple(res[4 * p: 4 * p + 4]) for p in range(n)], res[4 * n]


def _place():
    x, y, c = lax.axis_index("x"), lax.axis_index("y"), lax.axis_index("c")
    other_chips = [(1 - x, y), (x, 1 - y), (1 - x, 1 - y)]
    return x, y, c, other_chips


def _rows(ref, r, place, natural=False):
    px, py, pc = place
    b = 4 * px + 2 * py + pc if natural else 4 * pc + 2 * px + py
    return ref.at[pl.ds(pl.multiple_of(b * r, 8), r), :]


def _gather_task(shards, natural=(), forward_at=0.75):
    n = len(shards)
    rs = [s.shape[0] for s in shards]
    rows_of = lambda ref, k, place: _rows(ref, rs[k], place, k in natural)

    def copy(scr, outs, k, slot, block, to, src=None):
        rows = rows_of(outs[k], k, block)
        return pltpu.make_async_remote_copy(
            src_ref=rows if src is None else src, dst_ref=rows, send_sem=scr[0].at[7 * k + slot],
            recv_sem=scr[1].at[7 * k + slot], device_id=to, device_id_type=MESH)

    def first_sends(ins, outs, scr):
        x, y, c, chips = _place()
        me = (x, y, c)
        cps = [copy(scr, outs, k, 1 + j, me, (*chip, c), src=ins[k]) for j, chip in enumerate(chips) for k in range(n)]
        return cps + [copy(scr, outs, k, 0, me, (x, y, 1 - c), src=ins[k]) for k in range(n)]

    def passed_on(outs, scr):
        x, y, c, chips = _place()
        return [copy(scr, outs, k, 4 + j, (*chip, c), (x, y, 1 - c)) for j, chip in enumerate(chips) for k in range(n)]

    def local(ins, outs, scr):
        x, y, c, _ = _place()
        return [pltpu.make_async_copy(ins[k], rows_of(outs[k], k, (x, y, c)), scr[2].at[k]) for k in range(n)]

    def start(ins, outs, scr):
        for cp in local(ins, outs, scr) + first_sends(ins, outs, scr):
            cp.start()

    def forward(ins, outs, scr):
        x, y, c, chips = _place()
        for j, chip in enumerate(chips):
            for k in range(n):
                copy(scr, outs, k, 1 + j, (*chip, c), (x, y, c)).wait_recv()
                copy(scr, outs, k, 4 + j, (*chip, c), (x, y, 1 - c)).start()

    def finish(ins, outs, scr):
        x, y, c, chips = _place()
        for k in range(n):
            copy(scr, outs, k, 0, (x, y, 1 - c), (x, y, c)).wait_recv()
        for j, chip in enumerate(chips):
            for k in range(n):
                copy(scr, outs, k, 4 + j, (*chip, 1 - c), (x, y, c)).wait_recv()
        for cp in first_sends(ins, outs, scr) + passed_on(outs, scr):
            cp.wait_send()
        for cp in local(ins, outs, scr):
            cp.wait()

    out_shapes = [jax.ShapeDtypeStruct((N_DEV * s.shape[0], s.shape[1]), s.dtype) for s in shards]
    scratch = [pltpu.SemaphoreType.DMA((7 * n,)), pltpu.SemaphoreType.DMA((7 * n,)), pltpu.SemaphoreType.DMA((n,))]
    return _Task(shards, out_shapes, scratch, [(0, start), (forward_at, forward), (1.0, finish)], ("sibling", "chips"))


def _direct_gather_task(shards):
    n = len(shards)
    rs = [s.shape[0] for s in shards]

    def peers():
        x, y, c, _ = _place()
        flip = lambda v, bit: 1 - v if bit else v
        return (x, y, c), [(flip(x, (s >> 2) & 1), flip(y, (s >> 1) & 1), flip(c, s & 1)) for s in range(1, N_DEV)]

    def copies(ins, outs, scr):
        me, others = peers()
        local = [pltpu.make_async_copy(ins[k], _rows(outs[k], rs[k], me), scr[2].at[k]) for k in range(n)]
        sems = lambda k, s: dict(send_sem=scr[0].at[7 * k + s], recv_sem=scr[1].at[7 * k + s], device_id_type=MESH)
        sends = [pltpu.make_async_remote_copy(src_ref=ins[k], dst_ref=_rows(outs[k], rs[k], me), device_id=to, **sems(k, s))
                 for s, to in enumerate(others) for k in range(n)]
        recvs = [pltpu.make_async_remote_copy(src_ref=_rows(outs[k], rs[k], frm), dst_ref=_rows(outs[k], rs[k], frm),
                                              device_id=me, **sems(k, s))
                 for s, frm in enumerate(others) for k in range(n)]
        return local, sends, recvs

    def start(ins, outs, scr):
        local, sends, _ = copies(ins, outs, scr)
        for cp in local + sends:
            cp.start()

    def finish(ins, outs, scr):
        local, sends, recvs = copies(ins, outs, scr)
        for cp in recvs:
            cp.wait_recv()
        for cp in sends:
            cp.wait_send()
        for cp in local:
            cp.wait()

    out_shapes = [jax.ShapeDtypeStruct((N_DEV * s.shape[0], s.shape[1]), s.dtype) for s in shards]
    scratch = [pltpu.SemaphoreType.DMA((7 * n,)), pltpu.SemaphoreType.DMA((7 * n,)), pltpu.SemaphoreType.DMA((n,))]
    return _Task(shards, out_shapes, scratch, [(0, start), (1.0, finish)], ("all",))


def _chip_task(sums):
    n = len(sums)
    rs = [s.shape[0] // 4 for s in sums]

    def block(ref, k, chip_index):
        return ref.at[pl.ds(pl.multiple_of(chip_index * rs[k], 8), rs[k]), :]

    def copies(ins, outs, scr):
        send_sems, recv_sems, local_sems = scr
        x, y, c, chips = _place()
        here = 2 * x + y
        local = [pltpu.make_async_copy(block(ins[k], k, here), outs[k].at[here], local_sems.at[k]) for k in range(n)]
        remote = []
        for j, (px, py) in enumerate(chips):
            remote += [pltpu.make_async_remote_copy(
                src_ref=block(ins[k], k, 2 * px + py), dst_ref=outs[k].at[here],
                send_sem=send_sems.at[3 * k + j], recv_sem=recv_sems.at[3 * k + j],
                device_id=(px, py, c), device_id_type=MESH) for k in range(n)]
        return local, remote

    def start(ins, outs, scr):
        local, remote = copies(ins, outs, scr)
        for cp in local + remote:
            cp.start()

    def finish(ins, outs, scr):
        local, remote = copies(ins, outs, scr)
        for cp in remote:
            cp.wait()
        for cp in local:
            cp.wait()

    out_shapes = [jax.ShapeDtypeStruct((4, r, s.shape[1]), s.dtype) for r, s in zip(rs, sums)]
    scratch = [pltpu.SemaphoreType.DMA((3 * n,)), pltpu.SemaphoreType.DMA((3 * n,)), pltpu.SemaphoreType.DMA((n,))]
    return _Task(sums, out_shapes, scratch, [(0, start), (1.0, finish)], ("chips",))


def _dw_pair(name, a, b, scale, comm=None, blocks=1):
    T, M = a.shape
    N = b.shape[1]
    half = M // 2
    wide = half // blocks
    tk = min(2048, T)
    nK = T // tk
    plumb = _CommPlumbing(comm)

    def body(core_ref, *rest):
        a_refs, b_ref, rest = rest[:blocks], rest[blocks], rest[blocks + 1:]
        c_in = rest[:plumb.n_in]
        o_ref = rest[plumb.n_in]
        c_out = rest[plumb.n_in + 1: plumb.n_in + 1 + plumb.n_out]
        acc, stage, land, send_sem, recv_sem = rest[plumb.n_in + 1 + plumb.n_out: plumb.n_in + 6 + plumb.n_out]
        c_scr = rest[plumb.n_in + 6 + plumb.n_out:]
        i, k = pl.program_id(0), pl.program_id(1)
        x, y, c, _ = _place()
        push = pltpu.make_async_remote_copy(src_ref=stage, dst_ref=land, send_sem=send_sem, recv_sem=recv_sem,
                                            device_id=(x, y, 1 - c), device_id_type=MESH)
        plumb.handshake((i == 0) & (k == 0), own=("sibling",))
        if comm:
            plumb.run(i * nK + k, 2 * nK, True, c_in, c_out, c_scr)

        av = a_refs[0][...] if blocks == 1 else jnp.concatenate([r[...] for r in a_refs], axis=1)
        p = lax.dot_general(av, b_ref[...], _DIMS["tn"], preferred_element_type=F32)

        @pl.when(k == 0)
        def _():
            acc[...] = p

        @pl.when(k > 0)
        def _():
            acc[...] += p

        @pl.when((i == 0) & (k == nK - 1))
        def _():
            stage[...] = (scale * acc[...]).astype(BF)
            push.start()

        @pl.when((i == 1) & (k == nK - 1))
        def _():
            push.wait_recv()
            o_ref[...] = (scale * acc[...] + land[...].astype(F32)).astype(BF)
            push.wait_send()

        if comm:
            plumb.run(i * nK + k, 2 * nK, False, c_in, c_out, c_scr)

    grid_spec = pltpu.PrefetchScalarGridSpec(
        num_scalar_prefetch=1, grid=(2, nK),
        in_specs=[pl.BlockSpec((tk, wide), functools.partial(
            lambda i, k, core, j: (k, (2 * j if blocks > 1 else 0) + jnp.where(i == 0, 1 - core[0], core[0])), j=j))
            for j in range(blocks)] + [pl.BlockSpec((tk, N), lambda i, k, core: (k, 0))] + [ANY] * plumb.n_in,
        out_specs=[pl.BlockSpec((half, N), lambda i, k, core: (0, 0))] + [ANY] * plumb.n_out,
        scratch_shapes=[pltpu.VMEM((half, N), F32), pltpu.VMEM((half, N), BF), pltpu.VMEM((half, N), BF),
                        pltpu.SemaphoreType.DMA, pltpu.SemaphoreType.DMA] + plumb.scratch)
    core = lax.axis_index("c").astype(jnp.int32).reshape(1)
    res = pl.pallas_call(
        body, name=name, grid_spec=grid_spec,
        out_shape=[jax.ShapeDtypeStruct((half, N), BF)] + plumb.out_shapes,
        compiler_params=_params(("arbitrary", "arbitrary"), plumb.collective_id(own=("sibling",))),
    )(core, *([a] * blocks), b, *plumb.args)
    return (res[0], plumb.split_outputs(res[1:])) if comm else res[0]


def _pair_task(parts):
    n = len(parts)

    def copies(ins, outs, scr):
        x, y, c, _ = _place()
        return [pltpu.make_async_remote_copy(
            src_ref=ins[k].at[:, pl.ds(1 - c, 1)], dst_ref=outs[k], send_sem=scr[0].at[k], recv_sem=scr[1].at[k],
            device_id=(x, y, 1 - c), device_id_type=MESH) for k in range(n)]

    def start(ins, outs, scr):
        for cp in copies(ins, outs, scr):
            cp.start()

    def finish(ins, outs, scr):
        for cp in copies(ins, outs, scr):
            cp.wait()

    out_shapes = [jax.ShapeDtypeStruct((4, 1) + p.shape[2:], p.dtype) for p in parts]
    scratch = [pltpu.SemaphoreType.DMA((n,)), pltpu.SemaphoreType.DMA((n,))]
    return _Task(parts, out_shapes, scratch, [(0, start), (1.0, finish)], ("sibling",))


def _pair_sum(name, part, got, core):
    _, _, r, C = part.shape

    def body(core_ref, p_ref, g_ref, o_ref):
        o_ref[0] = (p_ref[0, 0].astype(F32) + g_ref[0, 0].astype(F32)).astype(o_ref.dtype)

    return pl.pallas_call(
        body, name=name,
        grid_spec=pltpu.PrefetchScalarGridSpec(
            num_scalar_prefetch=1, grid=(4,),
            in_specs=[pl.BlockSpec((1, 1, r, C), lambda i, core_ref: (i, core_ref[0], 0, 0)),
                      pl.BlockSpec((1, 1, r, C), lambda i, core_ref: (i, 0, 0, 0))],
            out_specs=pl.BlockSpec((1, r, C), lambda i, core_ref: (i, 0, 0))),
        out_shape=jax.ShapeDtypeStruct((4, r, C), part.dtype), compiler_params=_params(("parallel",)),
    )(core, part, got)


def _ffn_bwd(tag, dy, dyb, x, gain, wgT, wuT, wd, saved, earlier=None):
    n, g, u, a = saved
    half = lambda accs, ex: _swiglu_bwd_epilogue([0.5 * accs[0]], ex)
    act_args = dict(tm=1024, tn=1408, tk=D_MODEL, epilogue=half, extras=[(g, "tile", 0), (u, "tile", 0)], cols_outer=True)
    if earlier is None:
        sum_d = _dw_pair(tag + "_dw_down", a, dyb, 0.5)
        (dg, du), ((slots_d,),) = _mm(tag + "_d_act", [(dyb, wd, "nt", 0)], [BF, BF], comm=[_chip_task([sum_d])], **act_args)
        slots_e = None
        sum_g = _dw_pair(tag + "_dw_gate", dg, n, 1.0)
    else:
        sum_d, ((got,),) = _dw_pair(tag + "_dw_down", a, dyb, 0.5, comm=[_pair_task([earlier])])
        core = lax.axis_index("c").astype(jnp.int32).reshape(1)
        sum_e = _pair_sum(tag + "_pair_sum_earlier", earlier, got, core)
        sum_e = sum_e.reshape(4 * sum_e.shape[1], sum_e.shape[2])
        (dg, du), ((slots_e,),) = _mm(tag + "_d_act", [(dyb, wd, "nt", 0)], [BF, BF], comm=[_chip_task([sum_e])], **act_args)
        sum_g, ((slots_d,),) = _dw_pair(tag + "_dw_gate", dg, n, 1.0, comm=[_chip_task([sum_d])])
    sum_u, ((slots_g,),) = _dw_pair(tag + "_dw_up", du, n, 1.0, comm=[_chip_task([sum_g])])
    (dx, dxb, dgain), ((slots_u,),) = _mm(
        tag + "_d_norm", [(dg, wgT, "nn", 0), (du, wuT, "nn", 0)], [F32, BF], tm=512, tn=D_MODEL, tk=D_FF,
        epilogue=_rms_bwd_epilogue, extras=[(x, "tile", 0), (gain, "row", 0), (dy, "tile", 0)], n_colsum=1,
        comm=[_chip_task([sum_u])])
    return dx, dxb, dgain, slots_e, slots_g, slots_u, slots_d


def _tile_gain(g):
    return jnp.concatenate([g, g]).reshape(1, LANES)


def _fold_heads(partials):
    return jnp.sum(partials.reshape(-1, HEAD_DIM), axis=0)


def _pack_small_grads(grads, loss_local):
    pieces, row = [], 0
    for name, r0, _ in SMALL_LAYOUT + (("loss", LOSS_ROW, None),):
        v = (loss_local if name == "loss" else grads[name]).reshape(-1)
        rows = -(-v.size // LANES)
        block = jnp.pad(v, (0, rows * LANES - v.size)).reshape(rows, LANES)
        pieces += [jnp.zeros((r0 - row, LANES), F32)] * (r0 > row) + [block]
        row = r0 + rows
    pieces.append(jnp.zeros((SMALL_ROWS - row, LANES), F32))
    return jnp.concatenate(pieces, axis=0)


def kernel(x, ffn1_norm, ffn1_w_gate, ffn1_w_up, ffn1_w_down, mix_norm, w_in, pool_w, pool_scale, w_pool_out, q_norm, k_norm, sinks, w_attn_out, gate_bias, w_out, ffn2_norm, ffn2_w_gate, ffn2_w_up, ffn2_w_down, loss_target, m_ffn1_norm, m_ffn1_w_gate, m_ffn1_w_up, m_ffn1_w_down, m_mix_norm, m_w_in, m_pool_w, m_pool_scale, m_w_pool_out, m_q_norm, m_k_norm, m_sinks, m_w_attn_out, m_gate_bias, m_w_out, m_ffn2_norm, m_ffn2_w_gate, m_ffn2_w_up, m_ffn2_w_down, v_ffn1_norm, v_ffn1_w_gate, v_ffn1_w_up, v_ffn1_w_down, v_mix_norm, v_w_in, v_pool_w, v_pool_scale, v_w_pool_out, v_q_norm, v_k_norm, v_sinks, v_w_attn_out, v_gate_bias, v_w_out, v_ffn2_norm, v_ffn2_w_gate, v_ffn2_w_up, v_ffn2_w_down):
    T = x.shape[1]
    x2 = x.reshape(T, D_MODEL)
    target = loss_target.reshape(T, D_MODEL)

    big = [
        ("ffn1_w_gate", ffn1_w_gate, m_ffn1_w_gate, v_ffn1_w_gate, True, False),
        ("ffn1_w_up", ffn1_w_up, m_ffn1_w_up, v_ffn1_w_up, True, False),
        ("ffn1_w_down", ffn1_w_down, m_ffn1_w_down, v_ffn1_w_down, False, False),
        ("w_in", w_in, m_w_in, v_w_in, True, False),
        ("w_pool_out", w_pool_out, m_w_pool_out, v_w_pool_out, False, True),
        ("w_attn_out", w_attn_out, m_w_attn_out, v_w_attn_out, False, False),
        ("w_out", w_out, m_w_out, v_w_out, False, False),
        ("ffn2_w_gate", ffn2_w_gate, m_ffn2_w_gate, v_ffn2_w_gate, True, False),
        ("ffn2_w_up", ffn2_w_up, m_ffn2_w_up, v_ffn2_w_up, True, False),
        ("ffn2_w_down", ffn2_w_down, m_ffn2_w_down, v_ffn2_w_down, False, False),
    ]
    view = lambda a, tv: a.T if tv else a
    shards = _prep("prep_weights", [view(w, tv) for _, w, _, _, tv, _ in big], [tk_ for *_, tk_ in big])
    g1 =ffn1_norm.reshape(1, D_MODEL)
    g2 = mix_norm.reshape(1, D_MODEL)
    g3 = ffn2_norm.reshape(1, D_MODEL)
    bias_row = gate_bias.reshape(1, 2 * D_MODEL)
    qg, kg = _tile_gain(q_norm) * ATTN_SCALE, _tile_gain(k_norm)
    scale_row = pool_scale.reshape(1, POOL_WIDTH)

    n1, ((wg1T, wu1T),) = _rms_fwd("ffn1_norm", x2, g1, [_gather_task(shards[0:2], forward_at=0.9)])
    (gt1, up1, act1), ((wd1,), (w_inT,)) = _mm(
        "ffn1_gate_up", [(n1, wg1T, "nt", 0), (n1, wu1T, "nt", 1)], [BF, BF, BF], tm=1024, tn=1408, tk=D_MODEL,
        epilogue=_swiglu_fwd_epilogue, cols_outer=True,
        comm=[_gather_task(shards[2:3], forward_at=0.5), _gather_task(shards[3:4], natural=(0,), forward_at=0.9)])
    (h1, u), ((w_poT, w_ao, w_o),) = _mm(
        "ffn1_down", [(act1, wd1, "nn", 0)], [F32, BF], tm=512, tn=D_MODEL, tk=D_FF,
        epilogue=_residual_norm_epilogue(0.5), extras=[(x2, "tile", 0), (g2, "row", 0)],
        comm=[_gather_task(shards[4:7], natural=(0, 1, 2), forward_at=0.8)])
    saved1 = (n1, gt1, up1, act1)
    (proj,), ((wg2T,),) = _mm(
        "in_proj", [(u, w_inT, "nt", 0)], [BF], tm=1024, tn=1280, tk=D_MODEL, cols_outer=True,
        comm=[_gather_task(shards[7:8], forward_at=0.8)])
    pooled, mixed = _pool_fwd("pool_fwd", proj, pool_w, scale_row)
    qn = _headnorm_fwd("q_norm", proj, COL_Q, ATTN_WIDTH, qg)
    kn = _headnorm_fwd("k_norm", proj, COL_K, KV_WIDTH, kg)
    attn, ((wu2T,),) = _attn_fwd("attn_fwd", qn, kn, proj, sinks, comm=[_gather_task(shards[8:9], forward_at=0.8)])
    (bp,) = _mm("pool_out", [(mixed, w_poT, "nt", 0)], [BF], tm=1024, tn=D_MODEL, tk=POOL_WIDTH)
    gate_tn = 256
    gate_extras = [(proj, "tile", COL_GP // gate_tn), (proj, "tile", COL_GA // gate_tn),
                   (bias_row, "row", 0), (bias_row, "row", D_MODEL // gate_tn)]
    merged, ba = _mm("attn_out_merge", [(attn, w_ao, "nn", 0)], [BF, BF], tm=2048, tn=gate_tn, tk=ATTN_WIDTH,
                     epilogue=_merge_fwd_epilogue, extras=[(bp, "tile", 0)] + gate_extras)
    h2, n2 = _mm("mix_out", [(merged, w_o, "nn", 0)], [F32, BF], tm=512, tn=D_MODEL, tk=D_MODEL,
                 epilogue=_residual_norm_epilogue(1.0), extras=[(h1, "tile", 0), (g3, "row", 0)])
    (gt2, up2, act2), ((wd2,),) = _mm(
        "ffn2_gate_up", [(n2, wg2T, "nt", 0), (n2, wu2T, "nt", 1)], [BF, BF, BF], tm=1024, tn=1408, tk=D_MODEL,
        epilogue=_swiglu_fwd_epilogue, cols_outer=True, comm=[_gather_task(shards[9:10], forward_at=0.8)])
    dy, dyb, sq = _mm("ffn2_down_loss", [(act2, wd2, "nn", 0)], [F32, BF], tm=512, tn=D_MODEL, tk=D_FF,
                      epilogue=_loss_epilogue, extras=[(h2, "tile", 0), (target, "tile", 0)], n_colsum=1)
    loss_local = 0.5 * jnp.sum(sq) / D_MODEL

    dh2, dh2b, dg3, _, slots_g2, slots_u2, slots_d2 = _ffn_bwd(
        "ffn2", dy, dyb, h2, g3, wg2T, wu2T, wd2, (n2, gt2, up2, act2))
    dbp, dba, dgp, dga, cs_gp, cs_ga = _mm(
        "mix_out_bwd", [(dh2b, w_o, "nt", 0)], [BF, BF, BF, BF], tm=2048, tn=gate_tn, tk=D_MODEL,
        epilogue=_merge_bwd_epilogue, extras=[(bp, "tile", 0), (ba, "tile", 0)] + gate_extras, n_colsum=2)
    sum_o = _dw_pair("dw_out", merged, dh2b, 1.0, blocks=4)
    (dmixed,) = _mm("pool_out_bwd", [(dbp, w_poT, "nn", 0)], [BF], tm=1024, tn=POOL_WIDTH, tk=D_MODEL)
    sum_po = _dw_pair("dw_pool_out", dbp, mixed, 1.0, blocks=4)
    (dattn,) = _mm("attn_out_bwd", [(dba, w_ao, "nt", 0)], [BF], tm=1024, tn=ATTN_WIDTH, tk=D_MODEL)
    sum_ao = _dw_pair("dw_attn_out", attn, dba, 1.0, blocks=4)
    dxp, dpool_w, dpool_scale = _pool_bwd("pool_bwd", dmixed, pooled, pool_w, scale_row)
    (dqn, dkn, dv, dsink_tile), ((slots_o, slots_po, slots_ao),) = _attn_bwd(
        "attn_bwd", dattn, qn, kn, proj, sinks, [_chip_task([sum_o, sum_po, sum_ao])])
    dq, dqg = _headnorm_bwd("q_norm_bwd", dqn, proj, COL_Q, ATTN_WIDTH, qg)
    dk, dkg = _headnorm_bwd("k_norm_bwd", dkn, proj, COL_K, KV_WIDTH, kg)
    dproj = jnp.concatenate([dxp, dq, dk, dv, dgp, dga], axis=1)
    (dh1, dh1b, dg2), ((g_pool_w,),) = _mm(
        "in_proj_bwd", [(dproj, w_inT, "nn", 0)], [F32, BF], tm=512, tn=D_MODEL, tk=IN_WIDTH, epilogue=_rms_bwd_epilogue,
        extras=[(h1, "tile", 0), (g2, "row", 0), (dh2, "tile", 0)], n_colsum=1,
        comm=[_gather_task([dpool_w.reshape(-1, LANES)])])
    (dw_inT,) = _mm("dw_in", [(dproj, u, "tn", 0)], [BF], tm=1280, tn=D_MODEL, tk=2048)
    dx, _, dg1, slots_in, slots_g1, slots_u1, slots_d1 = _ffn_bwd(
        "ffn1", dh1, dh1b, x2, g1, wg1T, wu1T, wd1, saved1, dw_inT.reshape(4, 2, IN_WIDTH // N_DEV, D_MODEL))

    slots = [slots_g1, slots_u1, slots_d1, slots_in, slots_po, slots_ao, slots_o, slots_g2, slots_u2, slots_d2]
    big_out = {}
    for label, group in (("ffn", (0, 1, 2, 7, 8, 9)), ("w_in", (3,)), ("w_pool_out", (4,)), ("attn_out_and_out", (5, 6))):
        items = [(slots[k], view(big[k][1], big[k][4]), view(big[k][2], big[k][4]), view(big[k][3], big[k][4]))
                 for k in group]
        for k, res in zip(group, _adamw_sharded("adamw_" + label, items, transpose=big[group[0]][5])):
            big_out[big[k][0]] = tuple(view(r, big[k][4]) for r in res)

    small_grads = {
        "ffn1_norm": jnp.sum(dg1, axis=(0, 1)), "mix_norm": jnp.sum(dg2, axis=(0, 1)), "ffn2_norm": jnp.sum(dg3, axis=(0, 1)),
        "gate_bias": jnp.concatenate([jnp.sum(cs_gp, axis=(0, 1)), jnp.sum(cs_ga, axis=(0, 1))]),
        "pool_scale": dpool_scale, "q_norm": _fold_heads(dqg) * ATTN_SCALE, "k_norm": _fold_heads(dkg),
        "sinks": dsink_tile[0, :N_HEADS]}
    ((g_vec,),) = _comm_only("gather_small_grads", [_direct_gather_task([_pack_small_grads(small_grads, loss_local)])])
    given = {"ffn1_norm": (ffn1_norm, m_ffn1_norm, v_ffn1_norm), "mix_norm": (mix_norm, m_mix_norm, v_mix_norm),
             "ffn2_norm": (ffn2_norm, m_ffn2_norm, v_ffn2_norm), "gate_bias": (gate_bias, m_gate_bias, v_gate_bias),
             "pool_scale": (pool_scale, m_pool_scale, v_pool_scale), "q_norm": (q_norm, m_q_norm, v_q_norm),
             "k_norm": (k_norm, m_k_norm, v_k_norm), "sinks": (sinks, m_sinks, v_sinks)}
    params = [tuple(a.reshape(shape) for a in given[nm]) for nm, _, shape in SMALL_LAYOUT]
    params.append(tuple(a.reshape(-1, LANES) for a in (pool_w, m_pool_w, v_pool_w)))
    small_res, loss_row = _adamw_small("adamw_small", g_vec.reshape(N_DEV, SMALL_ROWS, LANES),
                                       g_pool_w.reshape(N_DEV, -1, LANES), params)
    small_out = {nm: tuple(r.reshape(given[nm][0].shape) for r in res)
                 for (nm, _, _), res in zip(SMALL_LAYOUT, small_res)}
    small_out["pool_w"] = tuple(r.reshape(pool_w.shape) for r in small_res[-1])
    loss = loss_row[0, 0]

    order = ["ffn1_norm", "ffn1_w_gate", "ffn1_w_up", "ffn1_w_down", "mix_norm", "w_in", "pool_w", "pool_scale",
             "w_pool_out", "q_norm", "k_norm", "sinks", "w_attn_out", "gate_bias", "w_out", "ffn2_norm",
             "ffn2_w_gate", "ffn2_w_up", "ffn2_w_down"]
    every = {**big_out, **small_out}
    outs = [loss, dx.reshape(x.shape)]
    for j in range(4):
        outs += [every[nm][j] for nm in order]
    return tuple(outs)
```

```python
import functools

import jax
import jax.numpy as jnp
from jax import lax
from jax.experimental import pallas as pl
from jax.experimental.pallas import tpu as pltpu

BF = jnp.bfloat16
F32 = jnp.float32

D_MODEL = 1024
D_FF = 2816
POOL_WIDTH = 512
POOL_GROUP = 128
N_POOL_GROUPS = 4
HEAD_DIM = 64
N_HEADS = 16
GQA_GROUP = 8
BLOCK = 128
ATTN_WIDTH = 1024
KV_WIDTH = 128
IN_WIDTH = 3840
RMS_EPS = 1e-6
N_DEV = 8
LANES = 128

COL_Q = POOL_WIDTH
COL_K = COL_Q + ATTN_WIDTH
COL_V = COL_K + KV_WIDTH
COL_GP = COL_V + KV_WIDTH
COL_GA = COL_GP + D_MODEL

ADAM_LR = 0.001
ADAM_B1 = 0.9
ADAM_B2 = 0.999
ADAM_EPS = 1e-08
ADAM_WD = 0.01
ADAM_STEP = 10

VMEM_LIMIT_V7X = 56 * 1024 * 1024
MESH = pl.DeviceIdType.MESH
ANY = pl.BlockSpec(memory_space=pl.ANY)


def _params(sem=None, collective_id=None):
    return pltpu.CompilerParams(dimension_semantics=sem, vmem_limit_bytes=VMEM_LIMIT_V7X, collective_id=collective_id)


COLLECTIVE_IDS = {frozenset(["sibling"]): 0, frozenset(["chips"]): 1, frozenset(["sibling", "chips"]): 2}


def _handshake(peer_kinds):
    x, y, c, chips = _place()
    peers = ([(x, y, 1 - c)] if "sibling" in peer_kinds else []) + ([(*chip, c) for chip in chips] if "chips" in peer_kinds else [])
    barrier = pltpu.get_barrier_semaphore()
    for peer in peers:
        pl.semaphore_signal(barrier, inc=1, device_id=peer, device_id_type=MESH)
    pl.semaphore_wait(barrier, len(peers))


_DIMS = {"nt": (((1,), (1,)), ((), ())), "nn": (((1,), (0,)), ((), ())), "tn": (((0,), (0,)), ((), ()))}


class _Task:
    def __init__(self, inputs, out_shapes, scratch, phases, peers):
        self.inputs, self.out_shapes, self.scratch = list(inputs), list(out_shapes), list(scratch)
        self.phases = list(phases)
        self.peers = frozenset(peers)


class _CommPlumbing:
    def __init__(self, tasks):
        self.tasks = list(tasks or [])
        self.args = [a for t in self.tasks for a in t.inputs]
        self.out_shapes = [o for t in self.tasks for o in t.out_shapes]
        self.scratch = [s for t in self.tasks for s in t.scratch]
        self.n_in, self.n_out = len(self.args), len(self.out_shapes)

    def peer_kinds(self, own=()):
        kinds = frozenset(own).union(*[t.peers for t in self.tasks])
        return None if "all" in kinds or not kinds else kinds

    def collective_id(self, own=()):
        kinds = self.peer_kinds(own)
        return None if kinds is None else COLLECTIVE_IDS[kinds]

    def handshake(self, first, own=()):
        kinds = self.peer_kinds(own)
        if kinds is not None:
            pl.when(first)(functools.partial(_handshake, kinds))

    def _slices(self, c_in, c_out, c_scr):
        i = o = s = 0
        for t in self.tasks:
            yield t, c_in[i:i + len(t.inputs)], c_out[o:o + len(t.out_shapes)], c_scr[s:s + len(t.scratch)]
            i, o, s = i + len(t.inputs), o + len(t.out_shapes), s + len(t.scratch)

    def run(self, step, steps, before, c_in, c_out, c_scr):
        for t, ins, outs, scr in self._slices(c_in, c_out, c_scr):
            for frac, fn in t.phases:
                if step is None:
                    fn(ins, outs, scr)
                elif before == (frac == 0):
                    at = 0 if frac == 0 else max(0, min(steps, -(-int(round(frac * steps * 64)) // 64)) - 1)
                    pl.when(step == at)(functools.partial(fn, ins, outs, scr))

    def split_outputs(self, flat):
        res, o = [], 0
        for t in self.tasks:
            res.append(list(flat[o:o + len(t.out_shapes)]))
            o += len(t.out_shapes)
        return res


def _comm_only(name, tasks):
    plumb = _CommPlumbing(tasks)

    def body(*refs):
        c_in, c_out = refs[:plumb.n_in], refs[plumb.n_in: plumb.n_in + plumb.n_out]
        c_scr = refs[plumb.n_in + plumb.n_out:]
        plumb.run(None, 1, True, c_in, c_out, c_scr)

    res = pl.pallas_call(
        body, name=name, in_specs=[ANY] * plumb.n_in, out_specs=[ANY] * plumb.n_out, out_shape=plumb.out_shapes,
        scratch_shapes=plumb.scratch, compiler_params=pltpu.CompilerParams(has_side_effects=True),
    )(*plumb.args)
    return plumb.split_outputs(res)


def _mm(name, terms, out_dtypes, *, tm, tn, tk, epilogue=None, extras=(), n_colsum=0, comm=None, cols_outer=False):
    a0, b0, mode0, _ = terms[0]
    if mode0 == "nt":
        (M, K), N = a0.shape, b0.shape[0]
    elif mode0 == "nn":
        (M, K), N = a0.shape, b0.shape[1]
    else:
        (K, M), N = a0.shape, b0.shape[1]
    tm, tn, tk = min(tm, M), min(tn, N), min(tk, K)
    assert M % tm == 0 and N % tn == 0 and K % tk == 0, (name, M, N, K, tm, tn, tk)
    nI, nJ, nK = M // tm, N // tn, K // tk
    n_terms = len(terms)
    n_acc = max(t[3] for t in terms) + 1
    n_ex = len(extras)
    n_out = len(out_dtypes)
    if epilogue is None:
        epilogue = lambda accs, ex: ([accs[0]], [])
    plumb = _CommPlumbing(comm)
    n_scr = n_acc if nK > 1 else 0
    grid = (nJ, nI, nK) if cols_outer else (nI, nJ, nK)

    def body(*refs):
        n_in = 2 * n_terms + n_ex
        ab = refs[: 2 * n_terms]
        ex_refs = refs[2 * n_terms: n_in]
        c_in = refs[n_in: n_in + plumb.n_in]
        o0 = n_in + plumb.n_in
        out_refs = refs[o0: o0 + n_out]
        cs_refs = refs[o0 + n_out: o0 + n_out + n_colsum]
        c_out = refs[o0 + n_out + n_colsum: o0 + n_out + n_colsum + plumb.n_out]
        s0 = o0 + n_out + n_colsum + plumb.n_out
        acc_refs = refs[s0: s0 + n_scr]
        c_scr = refs[s0 + n_scr:]
        steps = grid[0] * grid[1] * nK
        if comm:
            step = (pl.program_id(0) * grid[1] + pl.program_id(1)) * nK + pl.program_id(2)
            plumb.handshake(step == 0)
            plumb.run(step, steps, True, c_in, c_out, c_scr)

        def products():
            accs = [None] * n_acc
            for t, (_, _, mode, ai) in enumerate(terms):
                p = lax.dot_general(ab[2 * t][...], ab[2 * t + 1][...], _DIMS[mode], preferred_element_type=F32)
                accs[ai] = p if accs[ai] is None else accs[ai] + p
            return accs

        def finish(accs):
            outs, colsums = epilogue(accs, [r[...] for r in ex_refs])
            for r, o in zip(out_refs, outs):
                r[...] = o.astype(r.dtype)
            for r, cs in zip(cs_refs, colsums):
                r[...] = jnp.sum(cs, axis=0, keepdims=True).reshape(r.shape)

        if nK == 1:
            finish(products())
        else:
            k = pl.program_id(2)
            accs = products()

            @pl.when(k == 0)
            def _():
                for r, a in zip(acc_refs, accs):
                    r[...] = a

            @pl.when(k > 0)
            def _():
                for r, a in zip(acc_refs, accs):
                    r[...] += a

            @pl.when(k == nK - 1)
            def _():
                finish([r[...] for r in acc_refs])

        if comm:
            plumb.run(step, steps, False, c_in, c_out, c_scr)

    def spec(block, index, fixed=False):
        imap = (lambda q, p, k: index(p, q, k)) if cols_outer else index
        return pl.BlockSpec(block, imap, pipeline_mode=pl.Buffered(1)) if fixed else pl.BlockSpec(block, imap)

    in_specs, args = [], []
    for a, b, mode, _ in terms:
        if mode == "nt":
            in_specs += [spec((tm, tk), lambda i, j, k: (i, k), nI * nK == 1),
                         spec((tn, tk), lambda i, j, k: (j, k), nJ * nK == 1)]
        elif mode == "nn":
            in_specs += [spec((tm, tk), lambda i, j, k: (i, k), nI * nK == 1),
                         spec((tk, tn), lambda i, j, k: (k, j), nJ * nK == 1)]
        else:
            in_specs += [spec((tk, tm), lambda i, j, k: (k, i), nI * nK == 1),
                         spec((tk, tn), lambda i, j, k: (k, j), nJ * nK == 1)]
        args += [a, b]
    for arr, kind, off in extras:
        if kind == "tile":
            in_specs.append(spec((tm, tn), functools.partial(lambda i, j, k, off: (i, j + off), off=off)))
        else:
            in_specs.append(spec((1, tn), functools.partial(lambda i, j, k, off: (0, j + off), off=off)))
        args.append(arr)
    out_shape = [jax.ShapeDtypeStruct((M, N), dt) for dt in out_dtypes]
    out_specs = [spec((tm, tn), lambda i, j, k: (i, j)) for _ in out_dtypes]
    out_shape += [jax.ShapeDtypeStruct((nI, 1, N), F32) for _ in range(n_colsum)]
    out_specs += [spec((1, 1, tn), lambda i, j, k: (i, 0, j)) for _ in range(n_colsum)]
    scratch = [pltpu.VMEM((tm, tn), F32) for _ in range(n_scr)]
    args += plumb.args
    in_specs += [ANY] * plumb.n_in
    out_shape += plumb.out_shapes
    out_specs += [ANY] * plumb.n_out
    sem = ("arbitrary",) * 3 if comm else ("parallel", "parallel", "arbitrary")
    res = pl.pallas_call(
        body, name=name, grid=grid, in_specs=in_specs, out_specs=out_specs, out_shape=out_shape,
        scratch_shapes=scratch + plumb.scratch, compiler_params=_params(sem, plumb.collective_id()),
    )(*args)
    n_own = n_out + n_colsum
    return (list(res[:n_own]), plumb.split_outputs(res[n_own:])) if comm is not None else res


ROW_TILE = 512


def _rms_fwd(name, x, g, comm):
    T, D = x.shape
    steps = T // ROW_TILE
    plumb = _CommPlumbing(comm)

    def body(x_ref, g_ref, *rest):
        c_in, o_ref = rest[:plumb.n_in], rest[plumb.n_in]
        c_out, c_scr = rest[plumb.n_in + 1: plumb.n_in + 1 + plumb.n_out], rest[plumb.n_in + 1 + plumb.n_out:]
        plumb.handshake(pl.program_id(0) == 0)
        plumb.run(pl.program_id(0), steps, True, c_in, c_out, c_scr)
        xv = x_ref[...]
        r = lax.rsqrt(jnp.mean(xv * xv, axis=-1, keepdims=True) + RMS_EPS)
        o_ref[...] = (xv * r * g_ref[...]).astype(BF)
        plumb.run(pl.program_id(0), steps, False, c_in, c_out, c_scr)

    row = pl.BlockSpec((ROW_TILE, D), lambda i: (i, 0))
    res = pl.pallas_call(
        body, name=name, grid=(steps,),
        in_specs=[row, pl.BlockSpec((1, D), lambda i: (0, 0))] + [ANY] * plumb.n_in,
        out_specs=[row] + [ANY] * plumb.n_out, out_shape=[jax.ShapeDtypeStruct((T, D), BF)] + plumb.out_shapes,
        scratch_shapes=plumb.scratch, compiler_params=_params(("arbitrary",), plumb.collective_id()),
    )(x, g, *plumb.args)
    return res[0], plumb.split_outputs(res[1:])


HEADNORM_TILE = 1024


def _half_sum_matrix():
    r = lax.broadcasted_iota(jnp.int32, (LANES, LANES), 0) // HEAD_DIM
    c = lax.broadcasted_iota(jnp.int32, (LANES, LANES), 1) // HEAD_DIM
    return (r == c).astype(BF)


def _head_mean(v, ones_blockdiag):
    hi = v.astype(BF)
    lo = (v - hi.astype(F32)).astype(BF)
    s = jnp.dot(hi, ones_blockdiag, preferred_element_type=F32) + jnp.dot(lo, ones_blockdiag, preferred_element_type=F32)
    return s * (1.0 / HEAD_DIM)


def _headnorm_fwd(name, proj, col0, width, g2):
    T = proj.shape[0]
    wide = min(width, GROUP_WIDTH)
    nb, off = width // wide, col0 // wide

    def body(x_ref, g_ref, b_ref, o_ref):
        for s in range(wide // LANES):
            lanes = slice(LANES * s, LANES * (s + 1))
            xv = x_ref[:, lanes].astype(F32)
            r = lax.rsqrt(_head_mean(xv * xv, b_ref[...]) + RMS_EPS)
            o_ref[:, lanes] = (xv * r * g_ref[...]).astype(BF)

    return pl.pallas_call(
        body, name=name, grid=(T // HEADNORM_TILE, nb),
        in_specs=[pl.BlockSpec((HEADNORM_TILE, wide), lambda i, j: (i, j + off)),
                  pl.BlockSpec((1, LANES), lambda i, j: (0, 0)), pl.BlockSpec((LANES, LANES), lambda i, j: (0, 0))],
        out_specs=pl.BlockSpec((HEADNORM_TILE, wide), lambda i, j: (i, j)),
        out_shape=jax.ShapeDtypeStruct((T, width), BF), compiler_params=_params(("parallel", "parallel")),
    )(proj, g2, _half_sum_matrix())


def _headnorm_bwd(name, dy, proj, col0, width, g2):
    T = proj.shape[0]
    wide = min(width, GROUP_WIDTH)
    nb, off = width // wide, col0 // wide

    def body(dy_ref, x_ref, g_ref, b_ref, dx_ref, dg_ref):
        for s in range(wide // LANES):
            lanes = slice(LANES * s, LANES * (s + 1))
            xv = x_ref[:, lanes].astype(F32)
            dyv = dy_ref[:, lanes].astype(F32)
            r = lax.rsqrt(_head_mean(xv * xv, b_ref[...]) + RMS_EPS)
            xhat = xv * r
            dxhat = dyv * g_ref[...]
            dx_ref[:, lanes] = (r * (dxhat - xhat * _head_mean(dxhat * xhat, b_ref[...]))).astype(BF)
            dg_ref[0, :, lanes] = jnp.sum(dyv * xhat, axis=0, keepdims=True)

    return pl.pallas_call(
        body, name=name, grid=(T // HEADNORM_TILE, nb),
        in_specs=[pl.BlockSpec((HEADNORM_TILE, wide), lambda i, j: (i, j)),
                  pl.BlockSpec((HEADNORM_TILE, wide), lambda i, j: (i, j + off)),
                  pl.BlockSpec((1, LANES), lambda i, j: (0, 0)), pl.BlockSpec((LANES, LANES), lambda i, j: (0, 0))],
        out_specs=[pl.BlockSpec((HEADNORM_TILE, wide), lambda i, j: (i, j)),
                   pl.BlockSpec((1, 1, wide), lambda i, j: (i, 0, j))],
        out_shape=[jax.ShapeDtypeStruct((T, width), BF), jax.ShapeDtypeStruct((T // HEADNORM_TILE, 1, width), F32)],
        compiler_params=_params(("parallel", "parallel")),
    )(dy, proj, g2, _half_sum_matrix())


def _shift_down(v, k, row):
    return jnp.where(row >= k, pltpu.roll(v, k, axis=0), 0.0)


def _shift_up(v, k, row, T):
    return jnp.where(row < T - k, pltpu.roll(v, T - k, axis=0), 0.0)


def _by_group(g, vals):
    out = vals[-1]
    for i in range(len(vals) - 2, -1, -1):
        out = jnp.where(g == i, vals[i], out)
    return out


def _pool_fwd(name, proj, pool_w, pool_scale):
    T = proj.shape[0]

    def body(x_ref, w_ref, s_ref, pooled_ref, mixed_ref):
        g = pl.program_id(0)
        xv = x_ref[...].astype(F32)
        row = lax.broadcasted_iota(jnp.int32, (T, 1), 0)
        s2 = xv + _shift_down(xv, 1, row)
        s4 = s2 + _shift_down(s2, 2, row)
        s8 = s4 + _shift_down(s4, 4, row)
        s16 = s8 + _shift_down(s8, 8, row)
        wsum = _by_group(g, [s2, s4, s8, s16])
        count = jnp.minimum(row + 1, 2 << g).astype(F32)
        pooled = (wsum / count - xv).astype(BF)
        pooled_ref[...] = pooled
        mixed = jnp.dot(pooled, w_ref[0].astype(BF), preferred_element_type=F32) * s_ref[...]
        mixed_ref[...] = mixed.astype(BF)

    col = pl.BlockSpec((T, POOL_GROUP), lambda g: (0, g))
    return pl.pallas_call(
        body, name=name, grid=(N_POOL_GROUPS,),
        in_specs=[col, pl.BlockSpec((1, POOL_GROUP, POOL_GROUP), lambda g: (g, 0, 0)),
                  pl.BlockSpec((1, POOL_GROUP), lambda g: (0, g))],
        out_specs=[col, col],
        out_shape=[jax.ShapeDtypeStruct((T, POOL_WIDTH), BF), jax.ShapeDtypeStruct((T, POOL_WIDTH), BF)],
        compiler_params=_params(("parallel",)),
    )(proj, pool_w, pool_scale)


def _pool_bwd(name, dmixed, pooled, pool_w, pool_scale):
    T = dmixed.shape[0]

    def body(dm_ref, p_ref, w_ref, s_ref, dx_ref, dw_ref, ds_ref):
        g = pl.program_id(0)
        dm = dm_ref[...].astype(F32)
        pooled = p_ref[...]
        w = w_ref[0].astype(BF)
        pre = jnp.dot(pooled, w, preferred_element_type=F32)
        ds_ref[...] = jnp.sum(dm * pre, axis=0, keepdims=True)
        dms = (dm * s_ref[...]).astype(BF)
        dw_ref[0] = lax.dot_general(pooled, dms, _DIMS["tn"], preferred_element_type=F32)
        dpooled = lax.dot_general(dms, w, _DIMS["nt"], preferred_element_type=F32)
        row = lax.broadcasted_iota(jnp.int32, (T, 1), 0)
        count = jnp.minimum(row + 1, 2 << g).astype(F32)
        z = dpooled / count
        l2 = z + _shift_up(z, 1, row, T)
        l4 = l2 + _shift_up(l2, 2, row, T)
        l8 = l4 + _shift_up(l4, 4, row, T)
        l16 = l8 + _shift_up(l8, 8, row, T)
        dx_ref[...] = (_by_group(g, [l2, l4, l8, l16]) - dpooled).astype(BF)

    col = pl.BlockSpec((T, POOL_GROUP), lambda g: (0, g))
    wspec = pl.BlockSpec((1, POOL_GROUP, POOL_GROUP), lambda g: (g, 0, 0))
    sspec = pl.BlockSpec((1, POOL_GROUP), lambda g: (0, g))
    return pl.pallas_call(
        body, name=name, grid=(N_POOL_GROUPS,), in_specs=[col, col, wspec, sspec], out_specs=[col, wspec, sspec],
        out_shape=[jax.ShapeDtypeStruct((T, POOL_WIDTH), BF),
                   jax.ShapeDtypeStruct((N_POOL_GROUPS, POOL_GROUP, POOL_GROUP), F32),
                   jax.ShapeDtypeStruct((1, POOL_WIDTH), F32)],
        compiler_params=_params(("parallel",)),
    )(dmixed, pooled, pool_w, pool_scale)


ATTN_SCALE = HEAD_DIM ** -0.5
MASKED = float(jnp.finfo(jnp.float32).min)
KV_COL_BLOCK_V = COL_V // LANES
GROUP_WIDTH = GQA_GROUP * HEAD_DIM


def _dup_head(v, j):
    half = lax.broadcasted_iota(jnp.int32, (1, LANES), 1) // HEAD_DIM
    return jnp.where(half == j, v, pltpu.roll(v, HEAD_DIM, axis=1))


def _stack_heads(v, low):
    pieces = []
    for p in range(GROUP_WIDTH // LANES):
        vp = v[:, LANES * p: LANES * (p + 1)]
        pieces.append(jnp.where(low, vp, jnp.zeros_like(vp)))
        pieces.append(jnp.where(low, jnp.zeros_like(vp), vp))
    return jnp.concatenate(pieces, axis=0)


def _unstack_transposed(t, low):
    pairs = []
    for p in range(GROUP_WIDTH // LANES):
        even = t[:, BLOCK * (2 * p): BLOCK * (2 * p + 1)].T
        odd = t[:, BLOCK * (2 * p + 1): BLOCK * (2 * p + 2)].T
        pairs.append(jnp.where(low, even, odd))
    return pairs


STACKED = GQA_GROUP * BLOCK


def _band_bias():
    key = lax.broadcasted_iota(jnp.int32, (2, 2 * BLOCK, STACKED), 1)
    qry = lax.broadcasted_iota(jnp.int32, (2, 2 * BLOCK, STACKED), 2) % BLOCK
    first = lax.broadcasted_iota(jnp.int32, (2, 2 * BLOCK, STACKED), 0) == 0
    valid = (key > qry) & (key <= qry + BLOCK) & (jnp.logical_not(first) | (key >= BLOCK))
    return jnp.where(valid, 0.0, MASKED).astype(F32)


BIAS_SPEC = pl.BlockSpec((1, 2 * BLOCK, STACKED), lambda n: (jnp.minimum(n, 1), 0, 0))


def _softmax_keys_on_sublanes(k2, q, bias, sink_ref, j):
    head_of_lane = lax.broadcasted_iota(jnp.int32, (1, STACKED), 1) // BLOCK
    sink = jnp.zeros((1, STACKED), F32)
    for h in range(GQA_GROUP):
        sink = jnp.where(head_of_lane == h, sink_ref[j * GQA_GROUP + h], sink)
    s = lax.dot_general(k2, q, _DIMS["nt"], preferred_element_type=F32) + bias
    m = jnp.maximum(jnp.max(s, axis=0, keepdims=True), sink)
    e = jnp.exp(s - m)
    e_sink = jnp.exp(sink - m)
    inv = 1.0 / (jnp.sum(e, axis=0, keepdims=True) + e_sink)
    return e * inv, e_sink * inv


def _attn_fwd(name, qn, kn, proj, sinks, comm=None):
    T = qn.shape[0]
    nb = T // BLOCK
    plumb = _CommPlumbing(comm)

    def body(sink_ref, bias_ref, q_ref, kp_ref, kc_ref, vp_ref, vc_ref, *rest):
        c_in, o_ref = rest[:plumb.n_in], rest[plumb.n_in]
        c_out, c_scr = rest[plumb.n_in + 1: plumb.n_in + 1 + plumb.n_out], rest[plumb.n_in + 1 + plumb.n_out:]
        n = pl.program_id(0)
        plumb.handshake(n == 0)
        plumb.run(n, nb, True, c_in, c_out, c_scr)
        low = lax.broadcasted_iota(jnp.int32, (1, LANES), 1) < HEAD_DIM
        kk = jnp.concatenate([kp_ref[...], kc_ref[...]], axis=0)
        vv = jnp.concatenate([vp_ref[...], vc_ref[...]], axis=0)
        for j in range(2):
            q = _stack_heads(q_ref[:, GROUP_WIDTH * j: GROUP_WIDTH * (j + 1)], low)
            p, _ = _softmax_keys_on_sublanes(_dup_head(kk, j), q, bias_ref[0], sink_ref, j)
            o_t = lax.dot_general(_dup_head(vv, j), p.astype(BF), _DIMS["tn"], preferred_element_type=F32)
            for pair, o in enumerate(_unstack_transposed(o_t, low)):
                lanes = slice(GROUP_WIDTH * j + LANES * pair, GROUP_WIDTH * j + LANES * (pair + 1))
                o_ref[:, lanes] = o.astype(BF)
        plumb.run(n, nb, False, c_in, c_out, c_scr)

    wide = pl.BlockSpec((BLOCK, ATTN_WIDTH), lambda n: (n, 0))
    res = pl.pallas_call(
        body, name=name, grid=(nb,),
        in_specs=[pl.BlockSpec(memory_space=pltpu.SMEM), BIAS_SPEC, wide,
                  pl.BlockSpec((BLOCK, LANES), lambda n: (jnp.maximum(n - 1, 0), 0)),
                  pl.BlockSpec((BLOCK, LANES), lambda n: (n, 0)),
                  pl.BlockSpec((BLOCK, LANES), lambda n: (jnp.maximum(n - 1, 0), KV_COL_BLOCK_V)),
                  pl.BlockSpec((BLOCK, LANES), lambda n: (n, KV_COL_BLOCK_V))] + [ANY] * plumb.n_in,
        out_specs=[wide] + [ANY] * plumb.n_out,
        out_shape=[jax.ShapeDtypeStruct((T, ATTN_WIDTH), BF)] + plumb.out_shapes, scratch_shapes=plumb.scratch,
        compiler_params=_params(("arbitrary",) if comm else ("parallel",), plumb.collective_id()),
    )(sinks, _band_bias(), qn, kn, kn, proj, proj, *plumb.args)
    return (res[0], plumb.split_outputs(res[1:])) if comm is not None else res[0]


def _attn_bwd(name, dout, qn, kn, proj, sinks, comm):
    T = qn.shape[0]
    nb = T // BLOCK
    plumb = _CommPlumbing(comm)

    def body(sink_ref, bias_ref, do_ref, q_ref, kp_ref, kc_ref, vp_ref, vc_ref, *rest):
        c_in, (dq_ref, dk_ref, dv_ref, dsink_ref) = rest[:plumb.n_in], rest[plumb.n_in: plumb.n_in + 4]
        c_out = rest[plumb.n_in + 4: plumb.n_in + 4 + plumb.n_out]
        carry_k, carry_v, tot_k, tot_v = rest[plumb.n_in + 4 + plumb.n_out: plumb.n_in + 8 + plumb.n_out]
        c_scr = rest[plumb.n_in + 8 + plumb.n_out:]
        n = pl.program_id(0)
        plumb.handshake(n == 0)
        plumb.run(n, nb + 1, True, c_in, c_out, c_scr)
        lane = lax.broadcasted_iota(jnp.int32, (1, LANES), 1)
        low = lane < HEAD_DIM

        @pl.when(n == 0)
        def _():
            carry_k[...] = jnp.zeros_like(carry_k)
            carry_v[...] = jnp.zeros_like(carry_v)
            dsink_ref[...] = jnp.zeros_like(dsink_ref)

        @pl.when(n == nb)
        def _():
            tot_k[...] = jnp.zeros_like(tot_k)
            tot_v[...] = jnp.zeros_like(tot_v)

        @pl.when(n < nb)
        def _():
            kk = jnp.concatenate([kp_ref[...], kc_ref[...]], axis=0)
            vv = jnp.concatenate([vp_ref[...], vc_ref[...]], axis=0)
            dk_tot = jnp.zeros((2 * BLOCK, LANES), F32)
            dv_tot = jnp.zeros((2 * BLOCK, LANES), F32)
            dsink = jnp.zeros((1, LANES), F32)
            for j in range(2):
                k2 = _dup_head(kk, j)
                v2 = _dup_head(vv, j)
                q = _stack_heads(q_ref[:, GROUP_WIDTH * j: GROUP_WIDTH * (j + 1)], low)
                do = _stack_heads(do_ref[:, GROUP_WIDTH * j: GROUP_WIDTH * (j + 1)], low)
                p, psink = _softmax_keys_on_sublanes(k2, q, bias_ref[0], sink_ref, j)
                dp =lax.dot_general(v2, do, _DIMS["nt"], preferred_element_type=F32)
                delta = jnp.sum(p * dp, axis=0, keepdims=True)
                ds = (p * (dp - delta)).astype(BF)
                dk2 = jnp.dot(ds, q, preferred_element_type=F32)
                dv2 = jnp.dot(p.astype(BF), do, preferred_element_type=F32)
                dq_t = lax.dot_general(k2, ds, _DIMS["tn"], preferred_element_type=F32)
                for pair, dq in enumerate(_unstack_transposed(dq_t, low)):
                    lanes = slice(GROUP_WIDTH * j + LANES * pair, GROUP_WIDTH * j + LANES * (pair + 1))
                    dq_ref[:, lanes] = dq.astype(BF)
                mine = low if j == 0 else jnp.logical_not(low)
                dk_tot = dk_tot + jnp.where(mine, dk2 + pltpu.roll(dk2, HEAD_DIM, axis=1), 0.0)
                dv_tot = dv_tot + jnp.where(mine, dv2 + pltpu.roll(dv2, HEAD_DIM, axis=1), 0.0)
                sink_term = psink * delta
                for h in range(GQA_GROUP):
                    val = -jnp.sum(sink_term[:, BLOCK * h: BLOCK * (h + 1)], axis=1, keepdims=True)
                    dsink = dsink + jnp.where(lane == j * GQA_GROUP + h, val, 0.0)
            tot_k[...] = dk_tot
            tot_v[...] = dv_tot
            dsink_ref[0:1, :] += dsink

        dk_ref[...] = (carry_k[...] + tot_k[0:BLOCK]).astype(BF)
        dv_ref[...] = (carry_v[...] + tot_v[0:BLOCK]).astype(BF)
        carry_k[...] = tot_k[BLOCK:]
        carry_v[...] = tot_v[BLOCK:]
        plumb.run(n, nb + 1, False, c_in, c_out, c_scr)

    cur = lambda n: (jnp.minimum(n, nb - 1), 0)
    prev = lambda n: (jnp.maximum(n - 1, 0), 0)
    wide = pl.BlockSpec((BLOCK, ATTN_WIDTH), cur)
    res = pl.pallas_call(
        body, name=name, grid=(nb + 1,),
        in_specs=[pl.BlockSpec(memory_space=pltpu.SMEM), BIAS_SPEC, wide, wide,
                  pl.BlockSpec((BLOCK, LANES), prev), pl.BlockSpec((BLOCK, LANES), cur),
                  pl.BlockSpec((BLOCK, LANES), lambda n: (jnp.maximum(n - 1, 0), KV_COL_BLOCK_V)),
                  pl.BlockSpec((BLOCK, LANES), lambda n: (jnp.minimum(n, nb - 1), KV_COL_BLOCK_V))] + [ANY] * plumb.n_in,
        out_specs=[wide, pl.BlockSpec((BLOCK, LANES), prev), pl.BlockSpec((BLOCK, LANES), prev),
                   pl.BlockSpec((8, LANES), lambda n: (0, 0))] + [ANY] * plumb.n_out,
        out_shape=[jax.ShapeDtypeStruct((T, ATTN_WIDTH), BF), jax.ShapeDtypeStruct((T, KV_WIDTH), BF),
                   jax.ShapeDtypeStruct((T, KV_WIDTH), BF), jax.ShapeDtypeStruct((8, LANES), F32)] + plumb.out_shapes,
        scratch_shapes=[pltpu.VMEM((BLOCK, LANES), F32), pltpu.VMEM((BLOCK, LANES), F32),
                        pltpu.VMEM((2 * BLOCK, LANES), F32), pltpu.VMEM((2 * BLOCK, LANES), F32)] + plumb.scratch,
        compiler_params=_params(("arbitrary",), plumb.collective_id()),
    )(sinks, _band_bias(), dout, qn, kn, kn, proj, proj, *plumb.args)
    return list(res[:4]), plumb.split_outputs(res[4:])


def _swiglu_fwd_epilogue(accs, ex):
    g, u = accs
    return [g, u, g * jax.nn.sigmoid(g) * u], []


def _swiglu_bwd_epilogue(accs, ex):
    (da,) = accs
    g, u = ex[0].astype(F32), ex[1].astype(F32)
    s = jax.nn.sigmoid(g)
    gs = g * s
    return [da * u * (s + gs - gs * s), da * gs], []


def _residual_norm_epilogue(scale):
    def epilogue(accs, ex):
        res, gain = ex
        h = res + scale * accs[0]
        r = lax.rsqrt(jnp.mean(h * h, axis=-1, keepdims=True) + RMS_EPS)
        return [h, h * r * gain], []
    return epilogue


def _rms_bwd_epilogue(accs, ex):
    (dn,) = accs
    xv, g, dres = ex
    r = lax.rsqrt(jnp.mean(xv * xv, axis=-1, keepdims=True) + RMS_EPS)
    xhat = xv * r
    dxhat = dn * g
    dx = dres + r * (dxhat - xhat * jnp.mean(dxhat * xhat, axis=-1, keepdims=True))
    return [dx, dx], [dn * xhat]


def _loss_epilogue(accs, ex):
    xv, target = ex
    d = xv + 0.5 * accs[0] - target
    dy = d * (1.0 / D_MODEL)
    return [dy, dy], [d * d]


def _merge_fwd_epilogue(accs, ex):
    (ba,) = accs
    bp, gp_pre, ga_pre, bias_p, bias_a = ex
    gp = jax.nn.sigmoid(gp_pre.astype(F32) + bias_p)
    ga = jax.nn.sigmoid(ga_pre.astype(F32) + bias_a)
    return [gp * bp.astype(F32) + ga * ba, ba], []


def _merge_bwd_epilogue(accs, ex):
    (dm,) = accs
    bp, ba, gp_pre, ga_pre, bias_p, bias_a = ex
    gp = jax.nn.sigmoid(gp_pre.astype(F32) + bias_p)
    ga = jax.nn.sigmoid(ga_pre.astype(F32) + bias_a)
    dbp, dba = dm * gp, dm * ga
    dgp = dbp * bp.astype(F32) * (1.0 - gp)
    dga = dba * ba.astype(F32) * (1.0 - ga)
    return [dbp, dba, dgp, dga], [dgp, dga]


def _prep(name, ws, transposes):
    n = len(ws)

    def body(*refs):
        for w_ref, o_ref, tr in zip(refs[:n], refs[n:], transposes):
            v = w_ref[...]
            o_ref[...] = (v.T if tr else v).astype(BF)

    shapes = [jax.ShapeDtypeStruct(w.shape[::-1] if tr else w.shape, BF) for w, tr in zip(ws, transposes)]
    return pl.pallas_call(body, name=name, out_shape=shapes, compiler_params=_params())(*ws)


def _adam_math(w, g, m, v):
    m = ADAM_B1 * m + (1.0 - ADAM_B1) * g
    v = ADAM_B2 * v + (1.0 - ADAM_B2) * jnp.square(g)
    m_hat = m / (1.0 - ADAM_B1 ** ADAM_STEP)
    v_hat = v / (1.0 - ADAM_B2 ** ADAM_STEP)
    delta = -ADAM_LR * (m_hat / (jnp.sqrt(v_hat) + ADAM_EPS) + ADAM_WD * w)
    return delta, m, v


def _adamw_sharded(name, items, transpose=False):
    n = len(items)

    def body(*refs):
        ins, outs = refs[:4 * n], refs[4 * n:]
        for k in range(n):
            s_ref, w_ref, m_ref, v_ref = ins[4 * k: 4 * k + 4]
            g = s_ref[0].astype(F32)
            for i in range(1, 4):
                g = g + s_ref[i].astype(F32)
            if transpose:
                g = g.T
            delta, mn, vn = _adam_math(w_ref[...], g, m_ref[...], v_ref[...])
            for o_ref, val in zip(outs[4 * k: 4 * k + 4], (g, delta, mn, vn)):
                o_ref[...] = val

    flat = [a for item in items for a in item]
    out_shape = [jax.ShapeDtypeStruct(item[1].shape, F32) for item in items for _ in range(4)]
    _, r, C = items[0][0].shape
    rows = r // 4
    if transpose or rows % 8:
        res = pl.pallas_call(body, name=name, out_shape=out_shape, compiler_params=_params())(*flat)
    else:
        tile = pl.BlockSpec((rows, C), lambda i: (i, 0))
        res = pl.pallas_call(
            body, name=name, grid=(4,), in_specs=[pl.BlockSpec((4, rows, C), lambda i: (0, i, 0)), tile, tile, tile] * n,
            out_specs=[tile] * (4 * n), out_shape=out_shape, compiler_params=_params(("parallel",)),
        )(*flat)
    return [tuple(res[4 * k: 4 * k + 4]) for k in range(n)]


SMALL_LAYOUT = (("ffn1_norm", 0, (8, LANES)), ("mix_norm", 8, (8, LANES)), ("ffn2_norm", 16, (8, LANES)),
                ("gate_bias", 24, (16, LANES)), ("pool_scale", 40, (4, LANES)), ("q_norm", 48, (1, HEAD_DIM)),
                ("k_norm", 56, (1, HEAD_DIM)), ("sinks", 64, (1, N_HEADS)))
LOSS_ROW = 72
SMALL_ROWS = 80


def _adamw_small(name, g_vec, g_pool_w, params):
    n = len(SMALL_LAYOUT) + 1

    def body(vec_ref, pw_ref, *refs):
        ins, outs = refs[:3 * n], refs[3 * n:]
        vec = vec_ref[0]
        pw = pw_ref[0]
        for i in range(1, N_DEV):
            vec = vec + vec_ref[i]
            pw = pw + pw_ref[i]
        grads = [vec[r0:r0 + shape[0], 0:shape[1]] for _, r0, shape in SMALL_LAYOUT] + [pw]
        for p, g in enumerate(grads):
            w_ref, m_ref, v_ref = ins[3 * p: 3 * p + 3]
            delta, mn, vn = _adam_math(w_ref[...], g, m_ref[...], v_ref[...])
            for o_ref, val in zip(outs[4 * p: 4 * p + 4], (g, delta, mn, vn)):
                o_ref[...] = val
        outs[4 * n][...] = vec[LOSS_ROW:LOSS_ROW + 1, :]

    flat = [a for wmv in params for a in wmv]
    out_shape = [jax.ShapeDtypeStruct(wmv[0].shape, F32) for wmv in params for _ in range(4)]
    out_shape.append(jax.ShapeDtypeStruct((1, LANES), F32))
    res = pl.pallas_call(body, name=name, out_shape=out_shape, compiler_params=_params())(g_vec, g_pool_w, *flat)
    return [tuple(res[4 * p: 4 * p + 4]) for p in range(n)], res[4 * n]


def _place():
    x, y, c = lax.axis_index("x"), lax.axis_index("y"), lax.axis_index("c")
    other_chips = [(1 - x, y), (x, 1 - y), (1 - x, 1 - y)]
    return x, y, c, other_chips


def _rows(ref, r, place, natural=False):
    px, py, pc = place
    b = 4 * px + 2 * py + pc if natural else 4 * pc + 2 * px + py
    return ref.at[pl.ds(pl.multiple_of(b * r, 8), r), :]


def _gather_task(shards, natural=(), forward_at=0.75):
    n = len(shards)
    rs = [s.shape[0] for s in shards]
    rows_of = lambda ref, k, place: _rows(ref, rs[k], place, k in natural)

    def copy(scr, outs, k, slot, block, to, src=None):
        rows = rows_of(outs[k], k, block)
        return pltpu.make_async_remote_copy(
            src_ref=rows if src is None else src, dst_ref=rows, send_sem=scr[0].at[7 * k + slot],
            recv_sem=scr[1].at[7 * k + slot], device_id=to, device_id_type=MESH)

    def first_sends(ins, outs, scr):
        x, y, c, chips = _place()
        me = (x, y, c)
        cps = [copy(scr, outs, k, 1 + j, me, (*chip, c), src=ins[k]) for j, chip in enumerate(chips) for k in range(n)]
        return cps + [copy(scr, outs, k, 0, me, (x, y, 1 - c), src=ins[k]) for k in range(n)]

    def passed_on(outs, scr):
        x, y, c, chips = _place()
        return [copy(scr, outs, k, 4 + j, (*chip, c), (x, y, 1 - c)) for j, chip in enumerate(chips) for k in range(n)]

    def local(ins, outs, scr):
        x, y, c, _ = _place()
        return [pltpu.make_async_copy(ins[k], rows_of(outs[k], k, (x, y, c)), scr[2].at[k]) for k in range(n)]

    def start(ins, outs, scr):
        for cp in local(ins, outs, scr) + first_sends(ins, outs, scr):
            cp.start()

    def forward(ins, outs, scr):
        x, y, c, chips = _place()
        for j, chip in enumerate(chips):
            for k in range(n):
                copy(scr, outs, k, 1 + j, (*chip, c), (x, y, c)).wait_recv()
                copy(scr, outs, k, 4 + j, (*chip, c), (x, y, 1 - c)).start()

    def finish(ins, outs, scr):
        x, y, c, chips = _place()
        for k in range(n):
            copy(scr, outs, k, 0, (x, y, 1 - c), (x, y, c)).wait_recv()
        for j, chip in enumerate(chips):
            for k in range(n):
                copy(scr, outs, k, 4 + j, (*chip, 1 - c), (x, y, c)).wait_recv()
        for cp in first_sends(ins, outs, scr) + passed_on(outs, scr):
            cp.wait_send()
        for cp in local(ins, outs, scr):
            cp.wait()

    out_shapes = [jax.ShapeDtypeStruct((N_DEV * s.shape[0], s.shape[1]), s.dtype) for s in shards]
    scratch = [pltpu.SemaphoreType.DMA((7 * n,)), pltpu.SemaphoreType.DMA((7 * n,)), pltpu.SemaphoreType.DMA((n,))]
    return _Task(shards, out_shapes, scratch, [(0, start), (forward_at, forward), (1.0, finish)], ("sibling", "chips"))


def _direct_gather_task(shards):
    n = len(shards)
    rs = [s.shape[0] for s in shards]

    def peers():
        x, y, c, _ = _place()
        flip = lambda v, bit: 1 - v if bit else v
        return (x, y, c), [(flip(x, (s >> 2) & 1), flip(y, (s >> 1) & 1), flip(c, s & 1)) for s in range(1, N_DEV)]

    def copies(ins, outs, scr):
        me, others = peers()
        local = [pltpu.make_async_copy(ins[k], _rows(outs[k], rs[k], me), scr[2].at[k]) for k in range(n)]
        sems = lambda k, s: dict(send_sem=scr[0].at[7 * k + s], recv_sem=scr[1].at[7 * k + s], device_id_type=MESH)
        sends = [pltpu.make_async_remote_copy(src_ref=ins[k], dst_ref=_rows(outs[k], rs[k], me), device_id=to, **sems(k, s))
                 for s, to in enumerate(others) for k in range(n)]
        recvs = [pltpu.make_async_remote_copy(src_ref=_rows(outs[k], rs[k], frm), dst_ref=_rows(outs[k], rs[k], frm),
                                              device_id=me, **sems(k, s))
                 for s, frm in enumerate(others) for k in range(n)]
        return local, sends, recvs

    def start(ins, outs, scr):
        local, sends, _ = copies(ins, outs, scr)
        for cp in local + sends:
            cp.start()

    def finish(ins, outs, scr):
        local, sends, recvs = copies(ins, outs, scr)
        for cp in recvs:
            cp.wait_recv()
        for cp in sends:
            cp.wait_send()
        for cp in local:
            cp.wait()

    out_shapes = [jax.ShapeDtypeStruct((N_DEV * s.shape[0], s.shape[1]), s.dtype) for s in shards]
    scratch = [pltpu.SemaphoreType.DMA((7 * n,)), pltpu.SemaphoreType.DMA((7 * n,)), pltpu.SemaphoreType.DMA((n,))]
    return _Task(shards, out_shapes, scratch, [(0, start), (1.0, finish)], ("all",))


def _chip_task(sums):
    n = len(sums)
    rs = [s.shape[0] // 4 for s in sums]

    def block(ref, k, chip_index):
        return ref.at[pl.ds(pl.multiple_of(chip_index * rs[k], 8), rs[k]), :]

    def copies(ins, outs, scr):
        send_sems, recv_sems, local_sems = scr
        x, y, c, chips = _place()
        here = 2 * x + y
        local = [pltpu.make_async_copy(block(ins[k], k, here), outs[k].at[here], local_sems.at[k]) for k in range(n)]
        remote = []
        for j, (px, py) in enumerate(chips):
            remote += [pltpu.make_async_remote_copy(
                src_ref=block(ins[k], k, 2 * px + py), dst_ref=outs[k].at[here],
                send_sem=send_sems.at[3 * k + j], recv_sem=recv_sems.at[3 * k + j],
                device_id=(px, py, c), device_id_type=MESH) for k in range(n)]
        return local, remote

    def start(ins, outs, scr):
        local, remote = copies(ins, outs, scr)
        for cp in local + remote:
            cp.start()

    def finish(ins, outs, scr):
        local, remote = copies(ins, outs, scr)
        for cp in remote:
            cp.wait()
        for cp in local:
            cp.wait()

    out_shapes = [jax.ShapeDtypeStruct((4, r, s.shape[1]), s.dtype) for r, s in zip(rs, sums)]
    scratch = [pltpu.SemaphoreType.DMA((3 * n,)), pltpu.SemaphoreType.DMA((3 * n,)), pltpu.SemaphoreType.DMA((n,))]
    return _Task(sums, out_shapes, scratch, [(0, start), (1.0, finish)], ("chips",))


def _dw_pair(name, a, b, scale, comm=None, blocks=1):
    T, M = a.shape
    N = b.shape[1]
    half = M // 2
    wide = half // blocks
    tk = min(2048, T)
    nK = T // tk
    plumb = _CommPlumbing(comm)

    def body(core_ref, *rest):
        a_refs, b_ref, rest = rest[:blocks], rest[blocks], rest[blocks + 1:]
        c_in = rest[:plumb.n_in]
        o_ref = rest[plumb.n_in]
        c_out = rest[plumb.n_in + 1: plumb.n_in + 1 + plumb.n_out]
        acc, stage, land, send_sem, recv_sem = rest[plumb.n_in + 1 + plumb.n_out: plumb.n_in + 6 + plumb.n_out]
        c_scr = rest[plumb.n_in + 6 + plumb.n_out:]
        i, k = pl.program_id(0), pl.program_id(1)
        x, y, c, _ = _place()
        push = pltpu.make_async_remote_copy(src_ref=stage, dst_ref=land, send_sem=send_sem, recv_sem=recv_sem,
                                            device_id=(x, y, 1 - c), device_id_type=MESH)
        plumb.handshake((i == 0) & (k == 0), own=("sibling",))
        if comm:
            plumb.run(i * nK + k, 2 * nK, True, c_in, c_out, c_scr)

        av = a_refs[0][...] if blocks == 1 else jnp.concatenate([r[...] for r in a_refs], axis=1)
        p = lax.dot_general(av, b_ref[...], _DIMS["tn"], preferred_element_type=F32)

        @pl.when(k == 0)
        def _():
            acc[...] = p

        @pl.when(k > 0)
        def _():
            acc[...] += p

        @pl.when((i == 0) & (k == nK - 1))
        def _():
            stage[...] = (scale * acc[...]).astype(BF)
            push.start()

        @pl.when((i == 1) & (k == nK - 1))
        def _():
            push.wait_recv()
            o_ref[...] = (scale * acc[...] + land[...].astype(F32)).astype(BF)
            push.wait_send()

        if comm:
            plumb.run(i * nK + k, 2 * nK, False, c_in, c_out, c_scr)

    grid_spec = pltpu.PrefetchScalarGridSpec(
        num_scalar_prefetch=1, grid=(2, nK),
        in_specs=[pl.BlockSpec((tk, wide), functools.partial(
            lambda i, k, core, j: (k, (2 * j if blocks > 1 else 0) + jnp.where(i == 0, 1 - core[0], core[0])), j=j))
            for j in range(blocks)] + [pl.BlockSpec((tk, N), lambda i, k, core: (k, 0))] + [ANY] * plumb.n_in,
        out_specs=[pl.BlockSpec((half, N), lambda i, k, core: (0, 0))] + [ANY] * plumb.n_out,
        scratch_shapes=[pltpu.VMEM((half, N), F32), pltpu.VMEM((half, N), BF), pltpu.VMEM((half, N), BF),
                        pltpu.SemaphoreType.DMA, pltpu.SemaphoreType.DMA] + plumb.scratch)
    core = lax.axis_index("c").astype(jnp.int32).reshape(1)
    res = pl.pallas_call(
        body, name=name, grid_spec=grid_spec,
        out_shape=[jax.ShapeDtypeStruct((half, N), BF)] + plumb.out_shapes,
        compiler_params=_params(("arbitrary", "arbitrary"), plumb.collective_id(own=("sibling",))),
    )(core, *([a] * blocks), b, *plumb.args)
    return (res[0], plumb.split_outputs(res[1:])) if comm else res[0]


def _pair_task(parts):
    n = len(parts)

    def copies(ins, outs, scr):
        x, y, c, _ = _place()
        return [pltpu.make_async_remote_copy(
            src_ref=ins[k].at[:, pl.ds(1 - c, 1)], dst_ref=outs[k], send_sem=scr[0].at[k], recv_sem=scr[1].at[k],
            device_id=(x, y, 1 - c), device_id_type=MESH) for k in range(n)]

    def start(ins, outs, scr):
        for cp in copies(ins, outs, scr):
            cp.start()

    def finish(ins, outs, scr):
        for cp in copies(ins, outs, scr):
            cp.wait()

    out_shapes = [jax.ShapeDtypeStruct((4, 1) + p.shape[2:], p.dtype) for p in parts]
    scratch = [pltpu.SemaphoreType.DMA((n,)), pltpu.SemaphoreType.DMA((n,))]
    return _Task(parts, out_shapes, scratch, [(0, start), (1.0, finish)], ("sibling",))


def _pair_sum(name, part, got, core):
    _, _, r, C = part.shape

    def body(core_ref, p_ref, g_ref, o_ref):
        o_ref[0] = (p_ref[0, 0].astype(F32) + g_ref[0, 0].astype(F32)).astype(o_ref.dtype)

    return pl.pallas_call(
        body, name=name,
        grid_spec=pltpu.PrefetchScalarGridSpec(
            num_scalar_prefetch=1, grid=(4,),
            in_specs=[pl.BlockSpec((1, 1, r, C), lambda i, core_ref: (i, core_ref[0], 0, 0)),
                      pl.BlockSpec((1, 1, r, C), lambda i, core_ref: (i, 0, 0, 0))],
            out_specs=pl.BlockSpec((1, r, C), lambda i, core_ref: (i, 0, 0))),
        out_shape=jax.ShapeDtypeStruct((4, r, C), part.dtype), compiler_params=_params(("parallel",)),
    )(core, part, got)


def _ffn_bwd(tag, dy, dyb, x, gain, wgT, wuT, wd, saved, earlier=None):
    n, g, u, a = saved
    half = lambda accs, ex: _swiglu_bwd_epilogue([0.5 * accs[0]], ex)
    act_args = dict(tm=1024, tn=1408, tk=D_MODEL, epilogue=half, extras=[(g, "tile", 0), (u, "tile", 0)], cols_outer=True)
    if earlier is None:
        sum_d = _dw_pair(tag + "_dw_down", a, dyb, 0.5)
        (dg, du), ((slots_d,),) = _mm(tag + "_d_act", [(dyb, wd, "nt", 0)], [BF, BF], comm=[_chip_task([sum_d])], **act_args)
        slots_e = None
        sum_g = _dw_pair(tag + "_dw_gate", dg, n, 1.0)
    else:
        sum_d, ((got,),) = _dw_pair(tag + "_dw_down", a, dyb, 0.5, comm=[_pair_task([earlier])])
        core = lax.axis_index("c").astype(jnp.int32).reshape(1)
        sum_e = _pair_sum(tag + "_pair_sum_earlier", earlier, got, core)
        sum_e = sum_e.reshape(4 * sum_e.shape[1], sum_e.shape[2])
        (dg, du), ((slots_e,),) = _mm(tag + "_d_act", [(dyb, wd, "nt", 0)], [BF, BF], comm=[_chip_task([sum_e])], **act_args)
        sum_g, ((slots_d,),) = _dw_pair(tag + "_dw_gate", dg, n, 1.0, comm=[_chip_task([sum_d])])
    sum_u, ((slots_g,),) = _dw_pair(tag + "_dw_up", du, n, 1.0, comm=[_chip_task([sum_g])])
    (dx, dxb, dgain), ((slots_u,),) = _mm(
        tag + "_d_norm", [(dg, wgT, "nn", 0), (du, wuT, "nn", 0)], [F32, BF], tm=512, tn=D_MODEL, tk=D_FF,
        epilogue=_rms_bwd_epilogue, extras=[(x, "tile", 0), (gain, "row", 0), (dy, "tile", 0)], n_colsum=1,
        comm=[_chip_task([sum_u])])
    return dx, dxb, dgain, slots_e, slots_g, slots_u, slots_d


def _tile_gain(g):
    return jnp.concatenate([g, g]).reshape(1, LANES)


def _fold_heads(partials):
    return jnp.sum(partials.reshape(-1, HEAD_DIM), axis=0)


def _pack_small_grads(grads, loss_local):
    pieces, row = [], 0
    for name, r0, _ in SMALL_LAYOUT + (("loss", LOSS_ROW, None),):
        v = (loss_local if name == "loss" else grads[name]).reshape(-1)
        rows = -(-v.size // LANES)
        block = jnp.pad(v, (0, rows * LANES - v.size)).reshape(rows, LANES)
        pieces += [jnp.zeros((r0 - row, LANES), F32)] * (r0 > row) + [block]
        row = r0 + rows
    pieces.append(jnp.zeros((SMALL_ROWS - row, LANES), F32))
    return jnp.concatenate(pieces, axis=0)


def kernel(x, ffn1_norm, ffn1_w_gate, ffn1_w_up, ffn1_w_down, mix_norm, w_in, pool_w, pool_scale, w_pool_out, q_norm, k_norm, sinks, w_attn_out, gate_bias, w_out, ffn2_norm, ffn2_w_gate, ffn2_w_up, ffn2_w_down, loss_target, m_ffn1_norm, m_ffn1_w_gate, m_ffn1_w_up, m_ffn1_w_down, m_mix_norm, m_w_in, m_pool_w, m_pool_scale, m_w_pool_out, m_q_norm, m_k_norm, m_sinks, m_w_attn_out, m_gate_bias, m_w_out, m_ffn2_norm, m_ffn2_w_gate, m_ffn2_w_up, m_ffn2_w_down, v_ffn1_norm, v_ffn1_w_gate, v_ffn1_w_up, v_ffn1_w_down, v_mix_norm, v_w_in, v_pool_w, v_pool_scale, v_w_pool_out, v_q_norm, v_k_norm, v_sinks, v_w_attn_out, v_gate_bias, v_w_out, v_ffn2_norm, v_ffn2_w_gate, v_ffn2_w_up, v_ffn2_w_down):
    T = x.shape[1]
    x2 = x.reshape(T, D_MODEL)
    target = loss_target.reshape(T, D_MODEL)

    big = [
        ("ffn1_w_gate", ffn1_w_gate, m_ffn1_w_gate, v_ffn1_w_gate, True, False),
        ("ffn1_w_up", ffn1_w_up, m_ffn1_w_up, v_ffn1_w_up, True, False),
        ("ffn1_w_down", ffn1_w_down, m_ffn1_w_down, v_ffn1_w_down, False, False),
        ("w_in", w_in, m_w_in, v_w_in, True, False),
        ("w_pool_out", w_pool_out, m_w_pool_out, v_w_pool_out, False, True),
        ("w_attn_out", w_attn_out, m_w_attn_out, v_w_attn_out, False, False),
        ("w_out", w_out, m_w_out, v_w_out, False, False),
        ("ffn2_w_gate", ffn2_w_gate, m_ffn2_w_gate, v_ffn2_w_gate, True, False),
        ("ffn2_w_up", ffn2_w_up, m_ffn2_w_up, v_ffn2_w_up, True, False),
        ("ffn2_w_down", ffn2_w_down, m_ffn2_w_down, v_ffn2_w_down, False, False),
    ]
    view = lambda a, tv: a.T if tv else a
    shards = _prep("prep_weights", [view(w, tv) for _, w, _, _, tv, _ in big], [tk_ for *_, tk_ in big])
    g1 =ffn1_norm.reshape(1, D_MODEL)
    g2 = mix_norm.reshape(1, D_MODEL)
    g3 = ffn2_norm.reshape(1, D_MODEL)
    bias_row = gate_bias.reshape(1, 2 * D_MODEL)
    qg, kg = _tile_gain(q_norm) * ATTN_SCALE, _tile_gain(k_norm)
    scale_row = pool_scale.reshape(1, POOL_WIDTH)

    n1, ((wg1T, wu1T),) = _rms_fwd("ffn1_norm", x2, g1, [_gather_task(shards[0:2], forward_at=0.9)])
    (gt1, up1, act1), ((wd1,), (w_inT,)) = _mm(
        "ffn1_gate_up", [(n1, wg1T, "nt", 0), (n1, wu1T, "nt", 1)], [BF, BF, BF], tm=1024, tn=1408, tk=D_MODEL,
        epilogue=_swiglu_fwd_epilogue, cols_outer=True,
        comm=[_gather_task(shards[2:3], forward_at=0.5), _gather_task(shards[3:4], natural=(0,), forward_at=0.9)])
    (h1, u), ((w_poT, w_ao, w_o),) = _mm(
        "ffn1_down", [(act1, wd1, "nn", 0)], [F32, BF], tm=1024, tn=D_MODEL, tk=D_FF,
        epilogue=_residual_norm_epilogue(0.5), extras=[(x2, "tile", 0), (g2, "row", 0)],
        comm=[_gather_task(shards[4:7], natural=(0, 1, 2), forward_at=0.8)])
    saved1 = (n1, gt1, up1, act1)
    (proj,), ((wg2T,),) = _mm(
        "in_proj", [(u, w_inT, "nt", 0)], [BF], tm=1024, tn=1280, tk=D_MODEL, cols_outer=True,
        comm=[_gather_task(shards[7:8], forward_at=0.8)])
    pooled, mixed = _pool_fwd("pool_fwd", proj, pool_w, scale_row)
    qn = _headnorm_fwd("q_norm", proj, COL_Q, ATTN_WIDTH, qg)
    kn = _headnorm_fwd("k_norm", proj, COL_K, KV_WIDTH, kg)
    attn, ((wu2T,),) = _attn_fwd("attn_fwd", qn, kn, proj, sinks, comm=[_gather_task(shards[8:9], forward_at=0.8)])
    (bp,) = _mm("pool_out", [(mixed, w_poT, "nt", 0)], [BF], tm=1024, tn=D_MODEL, tk=POOL_WIDTH)
    gate_tn = 256
    gate_extras = [(proj, "tile", COL_GP // gate_tn), (proj, "tile", COL_GA // gate_tn),
                   (bias_row, "row", 0), (bias_row, "row", D_MODEL // gate_tn)]
    merged, ba = _mm("attn_out_merge", [(attn, w_ao, "nn", 0)], [BF, BF], tm=2048, tn=gate_tn, tk=ATTN_WIDTH,
                     epilogue=_merge_fwd_epilogue, extras=[(bp, "tile", 0)] + gate_extras)
    h2, n2 = _mm("mix_out", [(merged, w_o, "nn", 0)], [F32, BF], tm=1024, tn=D_MODEL, tk=D_MODEL,
                 epilogue=_residual_norm_epilogue(1.0), extras=[(h1, "tile", 0), (g3, "row", 0)])
    (gt2, up2, act2), ((wd2,),) = _mm(
        "ffn2_gate_up", [(n2, wg2T, "nt", 0), (n2, wu2T, "nt", 1)], [BF, BF, BF], tm=1024, tn=1408, tk=D_MODEL,
        epilogue=_swiglu_fwd_epilogue, cols_outer=True, comm=[_gather_task(shards[9:10], forward_at=0.8)])
    dy, dyb, sq = _mm("ffn2_down_loss", [(act2, wd2, "nn", 0)], [F32, BF], tm=1024, tn=D_MODEL, tk=D_FF,
                      epilogue=_loss_epilogue, extras=[(h2, "tile", 0), (target, "tile", 0)], n_colsum=1)
    loss_local = 0.5 * jnp.sum(sq) / D_MODEL

    dh2, dh2b, dg3, _, slots_g2, slots_u2, slots_d2 = _ffn_bwd(
        "ffn2", dy, dyb, h2, g3, wg2T, wu2T, wd2, (n2, gt2, up2, act2))
    dbp, dba, dgp, dga, cs_gp, cs_ga = _mm(
        "mix_out_bwd", [(dh2b, w_o, "nt", 0)], [BF, BF, BF, BF], tm=2048, tn=gate_tn, tk=D_MODEL,
        epilogue=_merge_bwd_epilogue, extras=[(bp, "tile", 0), (ba, "tile", 0)] + gate_extras, n_colsum=2)
    sum_o = _dw_pair("dw_out", merged, dh2b, 1.0, blocks=4)
    (dmixed,) = _mm("pool_out_bwd", [(dbp, w_poT, "nn", 0)], [BF], tm=1024, tn=POOL_WIDTH, tk=D_MODEL)
    sum_po = _dw_pair("dw_pool_out", dbp, mixed, 1.0, blocks=4)
    (dattn,) = _mm("attn_out_bwd", [(dba, w_ao, "nt", 0)], [BF], tm=1024, tn=ATTN_WIDTH, tk=D_MODEL)
    sum_ao = _dw_pair("dw_attn_out", attn, dba, 1.0, blocks=4)
    dxp, dpool_w, dpool_scale = _pool_bwd("pool_bwd", dmixed, pooled, pool_w, scale_row)
    (dqn, dkn, dv, dsink_tile), ((slots_o, slots_po, slots_ao),) = _attn_bwd(
        "attn_bwd", dattn, qn, kn, proj, sinks, [_chip_task([sum_o, sum_po, sum_ao])])
    dq, dqg = _headnorm_bwd("q_norm_bwd", dqn, proj, COL_Q, ATTN_WIDTH, qg)
    dk, dkg = _headnorm_bwd("k_norm_bwd", dkn, proj, COL_K, KV_WIDTH, kg)
    dproj = jnp.concatenate([dxp, dq, dk, dv, dgp, dga], axis=1)
    (dh1, dh1b, dg2), ((g_pool_w,),) = _mm(
        "in_proj_bwd", [(dproj, w_inT, "nn", 0)], [F32, BF], tm=512, tn=D_MODEL, tk=IN_WIDTH, epilogue=_rms_bwd_epilogue,
        extras=[(h1, "tile", 0), (g2, "row", 0), (dh2, "tile", 0)], n_colsum=1,
        comm=[_gather_task([dpool_w.reshape(-1, LANES)])])
    (dw_inT,) = _mm("dw_in", [(dproj, u, "tn", 0)], [BF], tm=1280, tn=D_MODEL, tk=2048)
    dx, _, dg1, slots_in, slots_g1, slots_u1, slots_d1 = _ffn_bwd(
        "ffn1", dh1, dh1b, x2, g1, wg1T, wu1T, wd1, saved1, dw_inT.reshape(4, 2, IN_WIDTH // N_DEV, D_MODEL))

    slots = [slots_g1, slots_u1, slots_d1, slots_in, slots_po, slots_ao, slots_o, slots_g2, slots_u2, slots_d2]
    big_out = {}
    for label, group in (("ffn", (0, 1, 2, 7, 8, 9)), ("w_in", (3,)), ("w_pool_out", (4,)), ("attn_out_and_out", (5, 6))):
        items = [(slots[k], view(big[k][1], big[k][4]), view(big[k][2], big[k][4]), view(big[k][3], big[k][4]))
                 for k in group]
        for k, res in zip(group, _adamw_sharded("adamw_" + label, items, transpose=big[group[0]][5])):
            big_out[big[k][0]] = tuple(view(r, big[k][4]) for r in res)

    small_grads = {
        "ffn1_norm": jnp.sum(dg1, axis=(0, 1)), "mix_norm": jnp.sum(dg2, axis=(0, 1)), "ffn2_norm": jnp.sum(dg3, axis=(0, 1)),
        "gate_bias": jnp.concatenate([jnp.sum(cs_gp, axis=(0, 1)), jnp.sum(cs_ga, axis=(0, 1))]),
        "pool_scale": dpool_scale, "q_norm": _fold_heads(dqg) * ATTN_SCALE, "k_norm": _fold_heads(dkg),
        "sinks": dsink_tile[0, :N_HEADS]}
    ((g_vec,),) = _comm_only("gather_small_grads", [_direct_gather_task([_pack_small_grads(small_grads, loss_local)])])
    given = {"ffn1_norm": (ffn1_norm, m_ffn1_norm, v_ffn1_norm), "mix_norm": (mix_norm, m_mix_norm, v_mix_norm),
             "ffn2_norm": (ffn2_norm, m_ffn2_norm, v_ffn2_norm), "gate_bias": (gate_bias, m_gate_bias, v_gate_bias),
             "pool_scale": (pool_scale, m_pool_scale, v_pool_scale), "q_norm": (q_norm, m_q_norm, v_q_norm),
             "k_norm": (k_norm, m_k_norm, v_k_norm), "sinks": (sinks, m_sinks, v_sinks)}
    params = [tuple(a.reshape(shape) for a in given[nm]) for nm, _, shape in SMALL_LAYOUT]
    params.append(tuple(a.reshape(-1, LANES) for a in (pool_w, m_pool_w, v_pool_w)))
    small_res, loss_row = _adamw_small("adamw_small", g_vec.reshape(N_DEV, SMALL_ROWS, LANES),
                                       g_pool_w.reshape(N_DEV, -1, LANES), params)
    small_out = {nm: tuple(r.reshape(given[nm][0].shape) for r in res)
                 for (nm, _, _), res in zip(SMALL_LAYOUT, small_res)}
    small_out["pool_w"] = tuple(r.reshape(pool_w.shape) for r in small_res[-1])
    loss = loss_row[0, 0]

    order = ["ffn1_norm", "ffn1_w_gate", "ffn1_w_up", "ffn1_w_down", "mix_norm", "w_in", "pool_w", "pool_scale",
             "w_pool_out", "q_norm", "k_norm", "sinks", "w_attn_out", "gate_bias", "w_out", "ffn2_norm",
             "ffn2_w_gate", "ffn2_w_up", "ffn2_w_down"]
    every = {**big_out, **small_out}
    outs = [loss, dx.reshape(x.shape)]
    for j in range(4):
        outs += [every[nm][j] for nm in order]
    return tuple(outs)
```

```python
import functools

import jax
import jax.numpy as jnp
from jax import lax
from jax.experimental import pallas as pl
from jax.experimental.pallas import tpu as pltpu

BF = jnp.bfloat16
F32 = jnp.float32

D_MODEL = 1024
D_FF = 2816
POOL_WIDTH = 512
POOL_GROUP = 128
N_POOL_GROUPS = 4
HEAD_DIM = 64
N_HEADS = 16
GQA_GROUP = 8
BLOCK = 128
ATTN_WIDTH = 1024
KV_WIDTH = 128
IN_WIDTH = 3840
RMS_EPS = 1e-6
N_DEV = 8
LANES = 128

COL_Q = POOL_WIDTH
COL_K = COL_Q + ATTN_WIDTH
COL_V = COL_K + KV_WIDTH
COL_GP = COL_V + KV_WIDTH
COL_GA = COL_GP + D_MODEL

ADAM_LR = 0.001
ADAM_B1 = 0.9
ADAM_B2 = 0.999
ADAM_EPS = 1e-08
ADAM_WD = 0.01
ADAM_STEP = 10

VMEM_LIMIT_V7X = 56 * 1024 * 1024
MESH = pl.DeviceIdType.MESH
ANY = pl.BlockSpec(memory_space=pl.ANY)


def _params(sem=None, collective_id=None):
    return pltpu.CompilerParams(dimension_semantics=sem, vmem_limit_bytes=VMEM_LIMIT_V7X, collective_id=collective_id)


COLLECTIVE_IDS = {frozenset(["sibling"]): 0, frozenset(["chips"]): 1, frozenset(["sibling", "chips"]): 2}


def _handshake(peer_kinds):
    x, y, c, chips = _place()
    peers = ([(x, y, 1 - c)] if "sibling" in peer_kinds else []) + ([(*chip, c) for chip in chips] if "chips" in peer_kinds else [])
    barrier = pltpu.get_barrier_semaphore()
    for peer in peers:
        pl.semaphore_signal(barrier, inc=1, device_id=peer, device_id_type=MESH)
    pl.semaphore_wait(barrier, len(peers))


_DIMS = {"nt": (((1,), (1,)), ((), ())), "nn": (((1,), (0,)), ((), ())), "tn": (((0,), (0,)), ((), ()))}


class _Task:
    def __init__(self, inputs, out_shapes, scratch, phases, peers):
        self.inputs, self.out_shapes, self.scratch = list(inputs), list(out_shapes), list(scratch)
        self.phases = list(phases)
        self.peers = frozenset(peers)


class _CommPlumbing:
    def __init__(self, tasks):
        self.tasks = list(tasks or [])
        self.args = [a for t in self.tasks for a in t.inputs]
        self.out_shapes = [o for t in self.tasks for o in t.out_shapes]
        self.scratch = [s for t in self.tasks for s in t.scratch]
        self.n_in, self.n_out = len(self.args), len(self.out_shapes)

    def peer_kinds(self, own=()):
        kinds = frozenset(own).union(*[t.peers for t in self.tasks])
        return None if "all" in kinds or not kinds else kinds

    def collective_id(self, own=()):
        kinds = self.peer_kinds(own)
        return None if kinds is None else COLLECTIVE_IDS[kinds]

    def handshake(self, first, own=()):
        kinds = self.peer_kinds(own)
        if kinds is not None:
            pl.when(first)(functools.partial(_handshake, kinds))

    def _slices(self, c_in, c_out, c_scr):
        i = o = s = 0
        for t in self.tasks:
            yield t, c_in[i:i + len(t.inputs)], c_out[o:o + len(t.out_shapes)], c_scr[s:s + len(t.scratch)]
            i, o, s = i + len(t.inputs), o + len(t.out_shapes), s + len(t.scratch)

    def run(self, step, steps, before, c_in, c_out, c_scr):
        for t, ins, outs, scr in self._slices(c_in, c_out, c_scr):
            for frac, fn in t.phases:
                if step is None:
                    fn(ins, outs, scr)
                elif before == (frac == 0):
                    at = 0 if frac == 0 else max(0, min(steps, -(-int(round(frac * steps * 64)) // 64)) - 1)
                    pl.when(step == at)(functools.partial(fn, ins, outs, scr))

    def split_outputs(self, flat):
        res, o = [], 0
        for t in self.tasks:
            res.append(list(flat[o:o + len(t.out_shapes)]))
            o += len(t.out_shapes)
        return res


def _comm_only(name, tasks):
    plumb = _CommPlumbing(tasks)

    def body(*refs):
        c_in, c_out = refs[:plumb.n_in], refs[plumb.n_in: plumb.n_in + plumb.n_out]
        c_scr = refs[plumb.n_in + plumb.n_out:]
        plumb.run(None, 1, True, c_in, c_out, c_scr)

    res = pl.pallas_call(
        body, name=name, in_specs=[ANY] * plumb.n_in, out_specs=[ANY] * plumb.n_out, out_shape=plumb.out_shapes,
        scratch_shapes=plumb.scratch, compiler_params=pltpu.CompilerParams(has_side_effects=True),
    )(*plumb.args)
    return plumb.split_outputs(res)


def _mm(name, terms, out_dtypes, *, tm, tn, tk, epilogue=None, extras=(), n_colsum=0, comm=None, cols_outer=False):
    a0, b0, mode0, _ = terms[0]
    if mode0 == "nt":
        (M, K), N = a0.shape, b0.shape[0]
    elif mode0 == "nn":
        (M, K), N = a0.shape, b0.shape[1]
    else:
        (K, M), N = a0.shape, b0.shape[1]
    tm, tn, tk = min(tm, M), min(tn, N), min(tk, K)
    assert M % tm == 0 and N % tn == 0 and K % tk == 0, (name, M, N, K, tm, tn, tk)
    nI, nJ, nK = M // tm, N // tn, K // tk
    n_terms = len(terms)
    n_acc = max(t[3] for t in terms) + 1
    n_ex = len(extras)
    n_out = len(out_dtypes)
    if epilogue is None:
        epilogue = lambda accs, ex: ([accs[0]], [])
    plumb = _CommPlumbing(comm)
    n_scr = n_acc if nK > 1 else 0
    grid = (nJ, nI, nK) if cols_outer else (nI, nJ, nK)

    def body(*refs):
        n_in = 2 * n_terms + n_ex
        ab = refs[: 2 * n_terms]
        ex_refs = refs[2 * n_terms: n_in]
        c_in = refs[n_in: n_in + plumb.n_in]
        o0 = n_in + plumb.n_in
        out_refs = refs[o0: o0 + n_out]
        cs_refs = refs[o0 + n_out: o0 + n_out + n_colsum]
        c_out = refs[o0 + n_out + n_colsum: o0 + n_out + n_colsum + plumb.n_out]
        s0 = o0 + n_out + n_colsum + plumb.n_out
        acc_refs = refs[s0: s0 + n_scr]
        c_scr = refs[s0 + n_scr:]
        steps = grid[0] * grid[1] * nK
        if comm:
            step = (pl.program_id(0) * grid[1] + pl.program_id(1)) * nK + pl.program_id(2)
            plumb.handshake(step == 0)
            plumb.run(step, steps, True, c_in, c_out, c_scr)

        def products():
            accs = [None] * n_acc
            for t, (_, _, mode, ai) in enumerate(terms):
                p = lax.dot_general(ab[2 * t][...], ab[2 * t + 1][...], _DIMS[mode], preferred_element_type=F32)
                accs[ai] = p if accs[ai] is None else accs[ai] + p
            return accs

        def finish(accs):
            outs, colsums = epilogue(accs, [r[...] for r in ex_refs])
            for r, o in zip(out_refs, outs):
                r[...] = o.astype(r.dtype)
            for r, cs in zip(cs_refs, colsums):
                r[...] = jnp.sum(cs, axis=0, keepdims=True).reshape(r.shape)

        if nK == 1:
            finish(products())
        else:
            k = pl.program_id(2)
            accs = products()

            @pl.when(k == 0)
            def _():
                for r, a in zip(acc_refs, accs):
                    r[...] = a

            @pl.when(k > 0)
            def _():
                for r, a in zip(acc_refs, accs):
                    r[...] += a

            @pl.when(k == nK - 1)
            def _():
                finish([r[...] for r in acc_refs])

        if comm:
            plumb.run(step, steps, False, c_in, c_out, c_scr)

    def spec(block, index, fixed=False):
        imap = (lambda q, p, k: index(p, q, k)) if cols_outer else index
        return pl.BlockSpec(block, imap, pipeline_mode=pl.Buffered(1)) if fixed else pl.BlockSpec(block, imap)

    in_specs, args = [], []
    for a, b, mode, _ in terms:
        if mode == "nt":
            in_specs += [spec((tm, tk), lambda i, j, k: (i, k), nI * nK == 1),
                         spec((tn, tk), lambda i, j, k: (j, k), nJ * nK == 1)]
        elif mode == "nn":
            in_specs += [spec((tm, tk), lambda i, j, k: (i, k), nI * nK == 1),
                         spec((tk, tn), lambda i, j, k: (k, j), nJ * nK == 1)]
        else:
            in_specs += [spec((tk, tm), lambda i, j, k: (k, i), nI * nK == 1),
                         spec((tk, tn), lambda i, j, k: (k, j), nJ * nK == 1)]
        args += [a, b]
    for arr, kind, off in extras:
        if kind == "tile":
            in_specs.append(spec((tm, tn), functools.partial(lambda i, j, k, off: (i, j + off), off=off)))
        else:
            in_specs.append(spec((1, tn), functools.partial(lambda i, j, k, off: (0, j + off), off=off)))
        args.append(arr)
    out_shape = [jax.ShapeDtypeStruct((M, N), dt) for dt in out_dtypes]
    out_specs = [spec((tm, tn), lambda i, j, k: (i, j)) for _ in out_dtypes]
    out_shape += [jax.ShapeDtypeStruct((nI, 1, N), F32) for _ in range(n_colsum)]
    out_specs += [spec((1, 1, tn), lambda i, j, k: (i, 0, j)) for _ in range(n_colsum)]
    scratch = [pltpu.VMEM((tm, tn), F32) for _ in range(n_scr)]
    args += plumb.args
    in_specs += [ANY] * plumb.n_in
    out_shape += plumb.out_shapes
    out_specs += [ANY] * plumb.n_out
    sem = ("arbitrary",) * 3 if comm else ("parallel", "parallel", "arbitrary")
    res = pl.pallas_call(
        body, name=name, grid=grid, in_specs=in_specs, out_specs=out_specs, out_shape=out_shape,
        scratch_shapes=scratch + plumb.scratch, compiler_params=_params(sem, plumb.collective_id()),
    )(*args)
    n_own = n_out + n_colsum
    return (list(res[:n_own]), plumb.split_outputs(res[n_own:])) if comm is not None else res


ROW_TILE = 512


def _rms_fwd(name, x, g, comm, weights, transposes):
    T, D = x.shape
    steps = T // ROW_TILE
    plumb = _CommPlumbing(comm)
    nw = len(weights)

    def body(x_ref, g_ref, *rest):
        w_refs, c_in = rest[:nw], rest[nw: nw + plumb.n_in]
        o_ref, shard_refs = rest[nw + plumb.n_in], rest[nw + plumb.n_in + 1: 2 * nw + plumb.n_in + 1]
        c_out = rest[2 * nw + plumb.n_in + 1: 2 * nw + plumb.n_in + 1 + plumb.n_out]
        c_scr = rest[2 * nw + plumb.n_in + 1 + plumb.n_out:]
        plumb.handshake(pl.program_id(0) == 0)
        plumb.run(pl.program_id(0), steps, True, c_in, c_out, c_scr)

        @pl.when(pl.program_id(0) == 0)
        def _():
            for w_ref, s_ref, tr in zip(w_refs, shard_refs, transposes):
                v = w_ref[...]
                s_ref[...] = (v.T if tr else v).astype(BF)

        xv = x_ref[...]
        r = lax.rsqrt(jnp.mean(xv * xv, axis=-1, keepdims=True) + RMS_EPS)
        o_ref[...] = (xv * r * g_ref[...]).astype(BF)
        plumb.run(pl.program_id(0), steps, False, c_in, c_out, c_scr)

    row = pl.BlockSpec((ROW_TILE, D), lambda i: (i, 0))
    whole = lambda shape: pl.BlockSpec(shape, lambda i: (0, 0), pipeline_mode=pl.Buffered(1))
    shard_shapes = [w.shape[::-1] if tr else w.shape for w, tr in zip(weights, transposes)]
    res = pl.pallas_call(
        body, name=name, grid=(steps,),
        in_specs=[row, pl.BlockSpec((1, D), lambda i: (0, 0))] + [whole(w.shape) for w in weights] + [ANY] * plumb.n_in,
        out_specs=[row] + [whole(s) for s in shard_shapes] + [ANY] * plumb.n_out,
        out_shape=[jax.ShapeDtypeStruct((T, D), BF)] + [jax.ShapeDtypeStruct(s, BF) for s in shard_shapes] + plumb.out_shapes,
        scratch_shapes=plumb.scratch, compiler_params=_params(("arbitrary",), plumb.collective_id()),
    )(x, g, *weights, *plumb.args)
    return res[0], list(res[1: nw + 1]), plumb.split_outputs(res[nw + 1:])


HEADNORM_TILE = 1024


def _half_sum_matrix():
    r = lax.broadcasted_iota(jnp.int32, (LANES, LANES), 0) // HEAD_DIM
    c = lax.broadcasted_iota(jnp.int32, (LANES, LANES), 1) // HEAD_DIM
    return (r == c).astype(BF)


def _head_mean(v, ones_blockdiag):
    hi = v.astype(BF)
    lo = (v - hi.astype(F32)).astype(BF)
    s = jnp.dot(hi, ones_blockdiag, preferred_element_type=F32) + jnp.dot(lo, ones_blockdiag, preferred_element_type=F32)
    return s * (1.0 / HEAD_DIM)


def _headnorm_fwd(name, proj, col0, width, g2):
    T = proj.shape[0]
    wide = min(width, GROUP_WIDTH)
    nb, off = width // wide, col0 // wide

    def body(x_ref, g_ref, b_ref, o_ref):
        for s in range(wide // LANES):
            lanes = slice(LANES * s, LANES * (s + 1))
            xv = x_ref[:, lanes].astype(F32)
            r = lax.rsqrt(_head_mean(xv * xv, b_ref[...]) + RMS_EPS)
            o_ref[:, lanes] = (xv * r * g_ref[...]).astype(BF)

    return pl.pallas_call(
        body, name=name, grid=(T // HEADNORM_TILE, nb),
        in_specs=[pl.BlockSpec((HEADNORM_TILE, wide), lambda i, j: (i, j + off)),
                  pl.BlockSpec((1, LANES), lambda i, j: (0, 0)), pl.BlockSpec((LANES, LANES), lambda i, j: (0, 0))],
        out_specs=pl.BlockSpec((HEADNORM_TILE, wide), lambda i, j: (i, j)),
        out_shape=jax.ShapeDtypeStruct((T, width), BF), compiler_params=_params(("parallel", "parallel")),
    )(proj, g2, _half_sum_matrix())


def _headnorm_bwd(name, dy, proj, col0, width, g2):
    T = proj.shape[0]
    wide = min(width, GROUP_WIDTH)
    nb, off = width // wide, col0 // wide

    def body(dy_ref, x_ref, g_ref, b_ref, dx_ref, dg_ref):
        for s in range(wide // LANES):
            lanes = slice(LANES * s, LANES * (s + 1))
            xv = x_ref[:, lanes].astype(F32)
            dyv = dy_ref[:, lanes].astype(F32)
            r = lax.rsqrt(_head_mean(xv * xv, b_ref[...]) + RMS_EPS)
            xhat = xv * r
            dxhat = dyv * g_ref[...]
            dx_ref[:, lanes] = (r * (dxhat - xhat * _head_mean(dxhat * xhat, b_ref[...]))).astype(BF)
            dg_ref[0, :, lanes] = jnp.sum(dyv * xhat, axis=0, keepdims=True)

    return pl.pallas_call(
        body, name=name, grid=(T // HEADNORM_TILE, nb),
        in_specs=[pl.BlockSpec((HEADNORM_TILE, wide), lambda i, j: (i, j)),
                  pl.BlockSpec((HEADNORM_TILE, wide), lambda i, j: (i, j + off)),
                  pl.BlockSpec((1, LANES), lambda i, j: (0, 0)), pl.BlockSpec((LANES, LANES), lambda i, j: (0, 0))],
        out_specs=[pl.BlockSpec((HEADNORM_TILE, wide), lambda i, j: (i, j)),
                   pl.BlockSpec((1, 1, wide), lambda i, j: (i, 0, j))],
        out_shape=[jax.ShapeDtypeStruct((T, width), BF), jax.ShapeDtypeStruct((T // HEADNORM_TILE, 1, width), F32)],
        compiler_params=_params(("parallel", "parallel")),
    )(dy, proj, g2, _half_sum_matrix())


def _shift_down(v, k, row):
    return jnp.where(row >= k, pltpu.roll(v, k, axis=0), 0.0)


def _shift_up(v, k, row, T):
    return jnp.where(row < T - k, pltpu.roll(v, T - k, axis=0), 0.0)


def _by_group(g, vals):
    out = vals[-1]
    for i in range(len(vals) - 2, -1, -1):
        out = jnp.where(g == i, vals[i], out)
    return out


def _pool_fwd(name, proj, pool_w, pool_scale):
    T = proj.shape[0]

    def body(x_ref, w_ref, s_ref, pooled_ref, mixed_ref):
        g = pl.program_id(0)
        xv = x_ref[...].astype(F32)
        row = lax.broadcasted_iota(jnp.int32, (T, 1), 0)
        s2 = xv + _shift_down(xv, 1, row)
        s4 = s2 + _shift_down(s2, 2, row)
        s8 = s4 + _shift_down(s4, 4, row)
        s16 = s8 + _shift_down(s8, 8, row)
        wsum = _by_group(g, [s2, s4, s8, s16])
        count = jnp.minimum(row + 1, 2 << g).astype(F32)
        pooled = (wsum / count - xv).astype(BF)
        pooled_ref[...] = pooled
        mixed = jnp.dot(pooled, w_ref[0].astype(BF), preferred_element_type=F32) * s_ref[...]
        mixed_ref[...] = mixed.astype(BF)

    col = pl.BlockSpec((T, POOL_GROUP), lambda g: (0, g))
    return pl.pallas_call(
        body, name=name, grid=(N_POOL_GROUPS,),
        in_specs=[col, pl.BlockSpec((1, POOL_GROUP, POOL_GROUP), lambda g: (g, 0, 0)),
                  pl.BlockSpec((1, POOL_GROUP), lambda g: (0, g))],
        out_specs=[col, col],
        out_shape=[jax.ShapeDtypeStruct((T, POOL_WIDTH), BF), jax.ShapeDtypeStruct((T, POOL_WIDTH), BF)],
        compiler_params=_params(("parallel",)),
    )(proj, pool_w, pool_scale)


def _pool_bwd(name, dmixed, pooled, pool_w, pool_scale):
    T = dmixed.shape[0]

    def body(dm_ref, p_ref, w_ref, s_ref, dx_ref, dw_ref, ds_ref):
        g = pl.program_id(0)
        dm = dm_ref[...].astype(F32)
        pooled = p_ref[...]
        w = w_ref[0].astype(BF)
        pre = jnp.dot(pooled, w, preferred_element_type=F32)
        ds_ref[...] = jnp.sum(dm * pre, axis=0, keepdims=True)
        dms = (dm * s_ref[...]).astype(BF)
        dw_ref[0] = lax.dot_general(pooled, dms, _DIMS["tn"], preferred_element_type=F32)
        dpooled = lax.dot_general(dms, w, _DIMS["nt"], preferred_element_type=F32)
        row = lax.broadcasted_iota(jnp.int32, (T, 1), 0)
        count = jnp.minimum(row + 1, 2 << g).astype(F32)
        z = dpooled / count
        l2 = z + _shift_up(z, 1, row, T)
        l4 = l2 + _shift_up(l2, 2, row, T)
        l8 = l4 + _shift_up(l4, 4, row, T)
        l16 = l8 + _shift_up(l8, 8, row, T)
        dx_ref[...] = (_by_group(g, [l2, l4, l8, l16]) - dpooled).astype(BF)

    col = pl.BlockSpec((T, POOL_GROUP), lambda g: (0, g))
    wspec = pl.BlockSpec((1, POOL_GROUP, POOL_GROUP), lambda g: (g, 0, 0))
    sspec = pl.BlockSpec((1, POOL_GROUP), lambda g: (0, g))
    return pl.pallas_call(
        body, name=name, grid=(N_POOL_GROUPS,), in_specs=[col, col, wspec, sspec], out_specs=[col, wspec, sspec],
        out_shape=[jax.ShapeDtypeStruct((T, POOL_WIDTH), BF),
                   jax.ShapeDtypeStruct((N_POOL_GROUPS, POOL_GROUP, POOL_GROUP), F32),
                   jax.ShapeDtypeStruct((1, POOL_WIDTH), F32)],
        compiler_params=_params(("parallel",)),
    )(dmixed, pooled, pool_w, pool_scale)


ATTN_SCALE = HEAD_DIM ** -0.5
MASKED = float(jnp.finfo(jnp.float32).min)
KV_COL_BLOCK_V = COL_V // LANES
GROUP_WIDTH = GQA_GROUP * HEAD_DIM


def _dup_head(v, j):
    half = lax.broadcasted_iota(jnp.int32, (1, LANES), 1) // HEAD_DIM
    return jnp.where(half == j, v, pltpu.roll(v, HEAD_DIM, axis=1))


def _stack_heads(v, low):
    pieces = []
    for p in range(GROUP_WIDTH // LANES):
        vp = v[:, LANES * p: LANES * (p + 1)]
        pieces.append(jnp.where(low, vp, jnp.zeros_like(vp)))
        pieces.append(jnp.where(low, jnp.zeros_like(vp), vp))
    return jnp.concatenate(pieces, axis=0)


def _unstack_transposed(t, low):
    pairs = []
    for p in range(GROUP_WIDTH // LANES):
        even = t[:, BLOCK * (2 * p): BLOCK * (2 * p + 1)].T
        odd = t[:, BLOCK * (2 * p + 1): BLOCK * (2 * p + 2)].T
        pairs.append(jnp.where(low, even, odd))
    return pairs


STACKED = GQA_GROUP * BLOCK


def _band_bias():
    key = lax.broadcasted_iota(jnp.int32, (2, 2 * BLOCK, STACKED), 1)
    qry = lax.broadcasted_iota(jnp.int32, (2, 2 * BLOCK, STACKED), 2) % BLOCK
    first = lax.broadcasted_iota(jnp.int32, (2, 2 * BLOCK, STACKED), 0) == 0
    valid = (key > qry) & (key <= qry + BLOCK) & (jnp.logical_not(first) | (key >= BLOCK))
    return jnp.where(valid, 0.0, MASKED).astype(F32)


BIAS_SPEC = pl.BlockSpec((1, 2 * BLOCK, STACKED), lambda n: (jnp.minimum(n, 1), 0, 0))


def _softmax_keys_on_sublanes(k2, q, bias, sink_ref, j):
    head_of_lane = lax.broadcasted_iota(jnp.int32, (1, STACKED), 1) // BLOCK
    sink = jnp.zeros((1, STACKED), F32)
    for h in range(GQA_GROUP):
        sink = jnp.where(head_of_lane == h, sink_ref[j * GQA_GROUP + h], sink)
    s = lax.dot_general(k2, q, _DIMS["nt"], preferred_element_type=F32) + bias
    m = jnp.maximum(jnp.max(s, axis=0, keepdims=True), sink)
    e = jnp.exp(s - m)
    e_sink = jnp.exp(sink - m)
    inv = 1.0 / (jnp.sum(e, axis=0, keepdims=True) + e_sink)
    return e * inv, e_sink * inv


def _attn_fwd(name, qn, kn, proj, sinks, comm=None):
    T = qn.shape[0]
    nb = T // BLOCK
    plumb = _CommPlumbing(comm)

    def body(sink_ref, bias_ref, q_ref, kp_ref, kc_ref, vp_ref, vc_ref, *rest):
        c_in, o_ref = rest[:plumb.n_in], rest[plumb.n_in]
        c_out, c_scr = rest[plumb.n_in + 1: plumb.n_in + 1 + plumb.n_out], rest[plumb.n_in + 1 + plumb.n_out:]
        n = pl.program_id(0)
        plumb.handshake(n == 0)
        plumb.run(n, nb, True, c_in, c_out, c_scr)
        low = lax.broadcasted_iota(jnp.int32, (1, LANES), 1) < HEAD_DIM
        kk = jnp.concatenate([kp_ref[...], kc_ref[...]], axis=0)
        vv = jnp.concatenate([vp_ref[...], vc_ref[...]], axis=0)
        for j in range(2):
            q = _stack_heads(q_ref[:, GROUP_WIDTH * j: GROUP_WIDTH * (j + 1)], low)
            p, _ = _softmax_keys_on_sublanes(_dup_head(kk, j), q, bias_ref[0], sink_ref, j)
            o_t = lax.dot_general(_dup_head(vv, j), p.astype(BF), _DIMS["tn"], preferred_element_type=F32)
            for pair, o in enumerate(_unstack_transposed(o_t, low)):
                lanes = slice(GROUP_WIDTH * j + LANES * pair, GROUP_WIDTH * j + LANES * (pair + 1))
                o_ref[:, lanes] = o.astype(BF)
        plumb.run(n, nb, False, c_in, c_out, c_scr)

    wide = pl.BlockSpec((BLOCK, ATTN_WIDTH), lambda n: (n, 0))
    res = pl.pallas_call(
        body, name=name, grid=(nb,),
        in_specs=[pl.BlockSpec(memory_space=pltpu.SMEM), BIAS_SPEC, wide,
                  pl.BlockSpec((BLOCK, LANES), lambda n: (jnp.maximum(n - 1, 0), 0)),
                  pl.BlockSpec((BLOCK, LANES), lambda n: (n, 0)),
                  pl.BlockSpec((BLOCK, LANES), lambda n: (jnp.maximum(n - 1, 0), KV_COL_BLOCK_V)),
                  pl.BlockSpec((BLOCK, LANES), lambda n: (n, KV_COL_BLOCK_V))] + [ANY] * plumb.n_in,
        out_specs=[wide] + [ANY] * plumb.n_out,
        out_shape=[jax.ShapeDtypeStruct((T, ATTN_WIDTH), BF)] + plumb.out_shapes, scratch_shapes=plumb.scratch,
        compiler_params=_params(("arbitrary",) if comm else ("parallel",), plumb.collective_id()),
    )(sinks, _band_bias(), qn, kn, kn, proj, proj, *plumb.args)
    return (res[0], plumb.split_outputs(res[1:])) if comm is not None else res[0]


def _attn_bwd(name, dout, qn, kn, proj, sinks, comm):
    T = qn.shape[0]
    nb = T // BLOCK
    plumb = _CommPlumbing(comm)

    def body(sink_ref, bias_ref, do_ref, q_ref, kp_ref, kc_ref, vp_ref, vc_ref, *rest):
        c_in, (dq_ref, dk_ref, dv_ref, dsink_ref) = rest[:plumb.n_in], rest[plumb.n_in: plumb.n_in + 4]
        c_out = rest[plumb.n_in + 4: plumb.n_in + 4 + plumb.n_out]
        carry_k, carry_v, tot_k, tot_v = rest[plumb.n_in + 4 + plumb.n_out: plumb.n_in + 8 + plumb.n_out]
        c_scr = rest[plumb.n_in + 8 + plumb.n_out:]
        n = pl.program_id(0)
        plumb.handshake(n == 0)
        plumb.run(n, nb + 1, True, c_in, c_out, c_scr)
        lane = lax.broadcasted_iota(jnp.int32, (1, LANES), 1)
        low = lane < HEAD_DIM

        @pl.when(n == 0)
        def _():
            carry_k[...] = jnp.zeros_like(carry_k)
            carry_v[...] = jnp.zeros_like(carry_v)
            dsink_ref[...] = jnp.zeros_like(dsink_ref)

        @pl.when(n == nb)
        def _():
            tot_k[...] = jnp.zeros_like(tot_k)
            tot_v[...] = jnp.zeros_like(tot_v)

        @pl.when(n < nb)
        def _():
            kk = jnp.concatenate([kp_ref[...], kc_ref[...]], axis=0)
            vv = jnp.concatenate([vp_ref[...], vc_ref[...]], axis=0)
            dk_tot = jnp.zeros((2 * BLOCK, LANES), F32)
            dv_tot = jnp.zeros((2 * BLOCK, LANES), F32)
            dsink = jnp.zeros((1, LANES), F32)
            for j in range(2):
                k2 = _dup_head(kk, j)
                v2 = _dup_head(vv, j)
                q = _stack_heads(q_ref[:, GROUP_WIDTH * j: GROUP_WIDTH * (j + 1)], low)
                do = _stack_heads(do_ref[:, GROUP_WIDTH * j: GROUP_WIDTH * (j + 1)], low)
                p, psink = _softmax_keys_on_sublanes(k2, q, bias_ref[0], sink_ref, j)
                dp =lax.dot_general(v2, do, _DIMS["nt"], preferred_element_type=F32)
                delta = jnp.sum(p * dp, axis=0, keepdims=True)
                ds = (p * (dp - delta)).astype(BF)
                dk2 = jnp.dot(ds, q, preferred_element_type=F32)
                dv2 = jnp.dot(p.astype(BF), do, preferred_element_type=F32)
                dq_t = lax.dot_general(k2, ds, _DIMS["tn"], preferred_element_type=F32)
                for pair, dq in enumerate(_unstack_transposed(dq_t, low)):
                    lanes = slice(GROUP_WIDTH * j + LANES * pair, GROUP_WIDTH * j + LANES * (pair + 1))
                    dq_ref[:, lanes] = dq.astype(BF)
                mine = low if j == 0 else jnp.logical_not(low)
                dk_tot = dk_tot + jnp.where(mine, dk2 + pltpu.roll(dk2, HEAD_DIM, axis=1), 0.0)
                dv_tot = dv_tot + jnp.where(mine, dv2 + pltpu.roll(dv2, HEAD_DIM, axis=1), 0.0)
                sink_term = psink * delta
                for h in range(GQA_GROUP):
                    val = -jnp.sum(sink_term[:, BLOCK * h: BLOCK * (h + 1)], axis=1, keepdims=True)
                    dsink = dsink + jnp.where(lane == j * GQA_GROUP + h, val, 0.0)
            tot_k[...] = dk_tot
            tot_v[...] = dv_tot
            dsink_ref[0:1, :] += dsink

        dk_ref[...] = (carry_k[...] + tot_k[0:BLOCK]).astype(BF)
        dv_ref[...] = (carry_v[...] + tot_v[0:BLOCK]).astype(BF)
        carry_k[...] = tot_k[BLOCK:]
        carry_v[...] = tot_v[BLOCK:]
        plumb.run(n, nb + 1, False, c_in, c_out, c_scr)

    cur = lambda n: (jnp.minimum(n, nb - 1), 0)
    prev = lambda n: (jnp.maximum(n - 1, 0), 0)
    wide = pl.BlockSpec((BLOCK, ATTN_WIDTH), cur)
    res = pl.pallas_call(
        body, name=name, grid=(nb + 1,),
        in_specs=[pl.BlockSpec(memory_space=pltpu.SMEM), BIAS_SPEC, wide, wide,
                  pl.BlockSpec((BLOCK, LANES), prev), pl.BlockSpec((BLOCK, LANES), cur),
                  pl.BlockSpec((BLOCK, LANES), lambda n: (jnp.maximum(n - 1, 0), KV_COL_BLOCK_V)),
                  pl.BlockSpec((BLOCK, LANES), lambda n: (jnp.minimum(n, nb - 1), KV_COL_BLOCK_V))] + [ANY] * plumb.n_in,
        out_specs=[wide, pl.BlockSpec((BLOCK, LANES), prev), pl.BlockSpec((BLOCK, LANES), prev),
                   pl.BlockSpec((8, LANES), lambda n: (0, 0))] + [ANY] * plumb.n_out,
        out_shape=[jax.ShapeDtypeStruct((T, ATTN_WIDTH), BF), jax.ShapeDtypeStruct((T, KV_WIDTH), BF),
                   jax.ShapeDtypeStruct((T, KV_WIDTH), BF), jax.ShapeDtypeStruct((8, LANES), F32)] + plumb.out_shapes,
        scratch_shapes=[pltpu.VMEM((BLOCK, LANES), F32), pltpu.VMEM((BLOCK, LANES), F32),
                        pltpu.VMEM((2 * BLOCK, LANES), F32), pltpu.VMEM((2 * BLOCK, LANES), F32)] + plumb.scratch,
        compiler_params=_params(("arbitrary",), plumb.collective_id()),
    )(sinks, _band_bias(), dout, qn, kn, kn, proj, proj, *plumb.args)
    return list(res[:4]), plumb.split_outputs(res[4:])


def _swiglu_fwd_epilogue(accs, ex):
    g, u = accs
    return [g, u, g * jax.nn.sigmoid(g) * u], []


def _swiglu_bwd_epilogue(accs, ex):
    (da,) = accs
    g, u = ex[0].astype(F32), ex[1].astype(F32)
    s = jax.nn.sigmoid(g)
    gs = g * s
    return [da * u * (s + gs - gs * s), da * gs], []


def _residual_norm_epilogue(scale):
    def epilogue(accs, ex):
        res, gain = ex
        h = res + scale * accs[0]
        r = lax.rsqrt(jnp.mean(h * h, axis=-1, keepdims=True) + RMS_EPS)
        return [h, h * r * gain], []
    return epilogue


def _rms_bwd_epilogue(accs, ex):
    (dn,) = accs
    xv, g, dres = ex
    r = lax.rsqrt(jnp.mean(xv * xv, axis=-1, keepdims=True) + RMS_EPS)
    xhat = xv * r
    dxhat = dn * g
    dx = dres + r * (dxhat - xhat * jnp.mean(dxhat * xhat, axis=-1, keepdims=True))
    return [dx, dx], [dn * xhat]


def _loss_epilogue(accs, ex):
    xv, target = ex
    d = xv + 0.5 * accs[0] - target
    dy = d * (1.0 / D_MODEL)
    return [dy, dy], [d * d]


def _merge_fwd_epilogue(accs, ex):
    (ba,) = accs
    bp, gp_pre, ga_pre, bias_p, bias_a = ex
    gp = jax.nn.sigmoid(gp_pre.astype(F32) + bias_p)
    ga = jax.nn.sigmoid(ga_pre.astype(F32) + bias_a)
    return [gp * bp.astype(F32) + ga * ba, ba], []


def _merge_bwd_epilogue(accs, ex):
    (dm,) = accs
    bp, ba, gp_pre, ga_pre, bias_p, bias_a = ex
    gp = jax.nn.sigmoid(gp_pre.astype(F32) + bias_p)
    ga = jax.nn.sigmoid(ga_pre.astype(F32) + bias_a)
    dbp, dba = dm * gp, dm * ga
    dgp = dbp * bp.astype(F32) * (1.0 - gp)
    dga = dba * ba.astype(F32) * (1.0 - ga)
    return [dbp, dba, dgp, dga], [dgp, dga]


def _prep(name, ws, transposes):
    n = len(ws)

    def body(*refs):
        for w_ref, o_ref, tr in zip(refs[:n], refs[n:], transposes):
            v = w_ref[...]
            o_ref[...] = (v.T if tr else v).astype(BF)

    shapes = [jax.ShapeDtypeStruct(w.shape[::-1] if tr else w.shape, BF) for w, tr in zip(ws, transposes)]
    return pl.pallas_call(body, name=name, out_shape=shapes, compiler_params=_params())(*ws)


def _adam_math(w, g, m, v):
    m = ADAM_B1 * m + (1.0 - ADAM_B1) * g
    v = ADAM_B2 * v + (1.0 - ADAM_B2) * jnp.square(g)
    m_hat = m / (1.0 - ADAM_B1 ** ADAM_STEP)
    v_hat = v / (1.0 - ADAM_B2 ** ADAM_STEP)
    delta = -ADAM_LR * (m_hat / (jnp.sqrt(v_hat) + ADAM_EPS) + ADAM_WD * w)
    return delta, m, v


def _adamw_sharded(name, items, transpose=False):
    n = len(items)

    def body(*refs):
        ins, outs = refs[:4 * n], refs[4 * n:]
        for k in range(n):
            s_ref, w_ref, m_ref, v_ref = ins[4 * k: 4 * k + 4]
            g = s_ref[0].astype(F32)
            for i in range(1, 4):
                g = g + s_ref[i].astype(F32)
            if transpose:
                g = g.T
            delta, mn, vn = _adam_math(w_ref[...], g, m_ref[...], v_ref[...])
            for o_ref, val in zip(outs[4 * k: 4 * k + 4], (g, delta, mn, vn)):
                o_ref[...] = val

    flat = [a for item in items for a in item]
    out_shape = [jax.ShapeDtypeStruct(item[1].shape, F32) for item in items for _ in range(4)]
    _, r, C = items[0][0].shape
    rows = r // 4
    if transpose or rows % 8:
        res = pl.pallas_call(body, name=name, out_shape=out_shape, compiler_params=_params())(*flat)
    else:
        tile = pl.BlockSpec((rows, C), lambda i: (i, 0))
        res = pl.pallas_call(
            body, name=name, grid=(4,), in_specs=[pl.BlockSpec((4, rows, C), lambda i: (0, i, 0)), tile, tile, tile] * n,
            out_specs=[tile] * (4 * n), out_shape=out_shape, compiler_params=_params(("parallel",)),
        )(*flat)
    return [tuple(res[4 * k: 4 * k + 4]) for k in range(n)]


SMALL_LAYOUT = (("ffn1_norm", 0, (8, LANES)), ("mix_norm", 8, (8, LANES)), ("ffn2_norm", 16, (8, LANES)),
                ("gate_bias", 24, (16, LANES)), ("pool_scale", 40, (4, LANES)), ("q_norm", 48, (1, HEAD_DIM)),
                ("k_norm", 56, (1, HEAD_DIM)), ("sinks", 64, (1, N_HEADS)))
LOSS_ROW = 72
SMALL_ROWS = 80


def _adamw_small(name, g_vec, g_pool_w, params):
    n = len(SMALL_LAYOUT) + 1

    def body(vec_ref, pw_ref, *refs):
        ins, outs = refs[:3 * n], refs[3 * n:]
        vec = vec_ref[0]
        pw = pw_ref[0]
        for i in range(1, N_DEV):
            vec = vec + vec_ref[i]
            pw = pw + pw_ref[i]
        grads = [vec[r0:r0 + shape[0], 0:shape[1]] for _, r0, shape in SMALL_LAYOUT] + [pw]
        for p, g in enumerate(grads):
            w_ref, m_ref, v_ref = ins[3 * p: 3 * p + 3]
            delta, mn, vn = _adam_math(w_ref[...], g, m_ref[...], v_ref[...])
            for o_ref, val in zip(outs[4 * p: 4 * p + 4], (g, delta, mn, vn)):
                o_ref[...] = val
        outs[4 * n][...] = vec[LOSS_ROW:LOSS_ROW + 1, :]

    flat = [a for wmv in params for a in wmv]
    out_shape = [jax.ShapeDtypeStruct(wmv[0].shape, F32) for wmv in params for _ in range(4)]
    out_shape.append(jax.ShapeDtypeStruct((1, LANES), F32))
    res = pl.pallas_call(body, name=name, out_shape=out_shape, compiler_params=_params())(g_vec, g_pool_w, *flat)
    return [tuple(res[4 * p: 4 * p + 4]) for p in range(n)], res[4 * n]


def _place():
    x, y, c = lax.axis_index("x"), lax.axis_index("y"), lax.axis_index("c")
    other_chips = [(1 - x, y), (x, 1 - y), (1 - x, 1 - y)]
    return x, y, c, other_chips


def _rows(ref, r, place, natural=False):
    px, py, pc = place
    b = 4 * px + 2 * py + pc if natural else 4 * pc + 2 * px + py
    return ref.at[pl.ds(pl.multiple_of(b * r, 8), r), :]


def _gather_task(shards, natural=(), forward_at=0.75):
    n = len(shards)
    rs = [s.shape[0] for s in shards]
    rows_of = lambda ref, k, place: _rows(ref, rs[k], place, k in natural)

    def copy(scr, outs, k, slot, block, to, src=None):
        rows = rows_of(outs[k], k, block)
        return pltpu.make_async_remote_copy(
            src_ref=rows if src is None else src, dst_ref=rows, send_sem=scr[0].at[7 * k + slot],
            recv_sem=scr[1].at[7 * k + slot], device_id=to, device_id_type=MESH)

    def first_sends(ins, outs, scr):
        x, y, c, chips = _place()
        me = (x, y, c)
        cps = [copy(scr, outs, k, 1 + j, me, (*chip, c), src=ins[k]) for j, chip in enumerate(chips) for k in range(n)]
        return cps + [copy(scr, outs, k, 0, me, (x, y, 1 - c), src=ins[k]) for k in range(n)]

    def passed_on(outs, scr):
        x, y, c, chips = _place()
        return [copy(scr, outs, k, 4 + j, (*chip, c), (x, y, 1 - c)) for j, chip in enumerate(chips) for k in range(n)]

    def local(ins, outs, scr):
        x, y, c, _ = _place()
        return [pltpu.make_async_copy(ins[k], rows_of(outs[k], k, (x, y, c)), scr[2].at[k]) for k in range(n)]

    def start(ins, outs, scr):
        for cp in local(ins, outs, scr) + first_sends(ins, outs, scr):
            cp.start()

    def forward(ins, outs, scr):
        x, y, c, chips = _place()
        for j, chip in enumerate(chips):
            for k in range(n):
                copy(scr, outs, k, 1 + j, (*chip, c), (x, y, c)).wait_recv()
                copy(scr, outs, k, 4 + j, (*chip, c), (x, y, 1 - c)).start()

    def finish(ins, outs, scr):
        x, y, c, chips = _place()
        for k in range(n):
            copy(scr, outs, k, 0, (x, y, 1 - c), (x, y, c)).wait_recv()
        for j, chip in enumerate(chips):
            for k in range(n):
                copy(scr, outs, k, 4 + j, (*chip, 1 - c), (x, y, c)).wait_recv()
        for cp in first_sends(ins, outs, scr) + passed_on(outs, scr):
            cp.wait_send()
        for cp in local(ins, outs, scr):
            cp.wait()

    out_shapes = [jax.ShapeDtypeStruct((N_DEV * s.shape[0], s.shape[1]), s.dtype) for s in shards]
    scratch = [pltpu.SemaphoreType.DMA((7 * n,)), pltpu.SemaphoreType.DMA((7 * n,)), pltpu.SemaphoreType.DMA((n,))]
    return _Task(shards, out_shapes, scratch, [(0, start), (forward_at, forward), (1.0, finish)], ("sibling", "chips"))


def _direct_gather_task(shards):
    n = len(shards)
    rs = [s.shape[0] for s in shards]

    def peers():
        x, y, c, _ = _place()
        flip = lambda v, bit: 1 - v if bit else v
        return (x, y, c), [(flip(x, (s >> 2) & 1), flip(y, (s >> 1) & 1), flip(c, s & 1)) for s in range(1, N_DEV)]

    def copies(ins, outs, scr):
        me, others = peers()
        local = [pltpu.make_async_copy(ins[k], _rows(outs[k], rs[k], me), scr[2].at[k]) for k in range(n)]
        sems = lambda k, s: dict(send_sem=scr[0].at[7 * k + s], recv_sem=scr[1].at[7 * k + s], device_id_type=MESH)
        sends = [pltpu.make_async_remote_copy(src_ref=ins[k], dst_ref=_rows(outs[k], rs[k], me), device_id=to, **sems(k, s))
                 for s, to in enumerate(others) for k in range(n)]
        recvs = [pltpu.make_async_remote_copy(src_ref=_rows(outs[k], rs[k], frm), dst_ref=_rows(outs[k], rs[k], frm),
                                              device_id=me, **sems(k, s))
                 for s, frm in enumerate(others) for k in range(n)]
        return local, sends, recvs

    def start(ins, outs, scr):
        local, sends, _ = copies(ins, outs, scr)
        for cp in local + sends:
            cp.start()

    def finish(ins, outs, scr):
        local, sends, recvs = copies(ins, outs, scr)
        for cp in recvs:
            cp.wait_recv()
        for cp in sends:
            cp.wait_send()
        for cp in local:
            cp.wait()

    out_shapes = [jax.ShapeDtypeStruct((N_DEV * s.shape[0], s.shape[1]), s.dtype) for s in shards]
    scratch = [pltpu.SemaphoreType.DMA((7 * n,)), pltpu.SemaphoreType.DMA((7 * n,)), pltpu.SemaphoreType.DMA((n,))]
    return _Task(shards, out_shapes, scratch, [(0, start), (1.0, finish)], ("all",))


def _chip_task(sums):
    n = len(sums)
    rs = [s.shape[0] // 4 for s in sums]

    def block(ref, k, chip_index):
        return ref.at[pl.ds(pl.multiple_of(chip_index * rs[k], 8), rs[k]), :]

    def copies(ins, outs, scr):
        send_sems, recv_sems, local_sems = scr
        x, y, c, chips = _place()
        here = 2 * x + y
        local = [pltpu.make_async_copy(block(ins[k], k, here), outs[k].at[here], local_sems.at[k]) for k in range(n)]
        remote = []
        for j, (px, py) in enumerate(chips):
            remote += [pltpu.make_async_remote_copy(
                src_ref=block(ins[k], k, 2 * px + py), dst_ref=outs[k].at[here],
                send_sem=send_sems.at[3 * k + j], recv_sem=recv_sems.at[3 * k + j],
                device_id=(px, py, c), device_id_type=MESH) for k in range(n)]
        return local, remote

    def start(ins, outs, scr):
        local, remote = copies(ins, outs, scr)
        for cp in local + remote:
            cp.start()

    def finish(ins, outs, scr):
        local, remote = copies(ins, outs, scr)
        for cp in remote:
            cp.wait()
        for cp in local:
            cp.wait()

    out_shapes = [jax.ShapeDtypeStruct((4, r, s.shape[1]), s.dtype) for r, s in zip(rs, sums)]
    scratch = [pltpu.SemaphoreType.DMA((3 * n,)), pltpu.SemaphoreType.DMA((3 * n,)), pltpu.SemaphoreType.DMA((n,))]
    return _Task(sums, out_shapes, scratch, [(0, start), (1.0, finish)], ("chips",))


def _dw_pair(name, a, b, scale, comm=None, blocks=1):
    T, M = a.shape
    N = b.shape[1]
    half = M // 2
    wide = half // blocks
    tk = min(2048, T)
    nK = T // tk
    plumb = _CommPlumbing(comm)

    def body(core_ref, *rest):
        a_refs, b_ref, rest = rest[:blocks], rest[blocks], rest[blocks + 1:]
        c_in = rest[:plumb.n_in]
        o_ref = rest[plumb.n_in]
        c_out = rest[plumb.n_in + 1: plumb.n_in + 1 + plumb.n_out]
        acc, stage, land, send_sem, recv_sem = rest[plumb.n_in + 1 + plumb.n_out: plumb.n_in + 6 + plumb.n_out]
        c_scr = rest[plumb.n_in + 6 + plumb.n_out:]
        i, k = pl.program_id(0), pl.program_id(1)
        x, y, c, _ = _place()
        push = pltpu.make_async_remote_copy(src_ref=stage, dst_ref=land, send_sem=send_sem, recv_sem=recv_sem,
                                            device_id=(x, y, 1 - c), device_id_type=MESH)
        plumb.handshake((i == 0) & (k == 0), own=("sibling",))
        if comm:
            plumb.run(i * nK + k, 2 * nK, True, c_in, c_out, c_scr)

        av = a_refs[0][...] if blocks == 1 else jnp.concatenate([r[...] for r in a_refs], axis=1)
        p = lax.dot_general(av, b_ref[...], _DIMS["tn"], preferred_element_type=F32)

        @pl.when(k == 0)
        def _():
            acc[...] = p

        @pl.when(k > 0)
        def _():
            acc[...] += p

        @pl.when((i == 0) & (k == nK - 1))
        def _():
            stage[...] = (scale * acc[...]).astype(BF)
            push.start()

        @pl.when((i == 1) & (k == nK - 1))
        def _():
            push.wait_recv()
            o_ref[...] = (scale * acc[...] + land[...].astype(F32)).astype(BF)
            push.wait_send()

        if comm:
            plumb.run(i * nK + k, 2 * nK, False, c_in, c_out, c_scr)

    grid_spec = pltpu.PrefetchScalarGridSpec(
        num_scalar_prefetch=1, grid=(2, nK),
        in_specs=[pl.BlockSpec((tk, wide), functools.partial(
            lambda i, k, core, j: (k, (2 * j if blocks > 1 else 0) + jnp.where(i == 0, 1 - core[0], core[0])), j=j))
            for j in range(blocks)] + [pl.BlockSpec((tk, N), lambda i, k, core: (k, 0))] + [ANY] * plumb.n_in,
        out_specs=[pl.BlockSpec((half, N), lambda i, k, core: (0, 0))] + [ANY] * plumb.n_out,
        scratch_shapes=[pltpu.VMEM((half, N), F32), pltpu.VMEM((half, N), BF), pltpu.VMEM((half, N), BF),
                        pltpu.SemaphoreType.DMA, pltpu.SemaphoreType.DMA] + plumb.scratch)
    core = lax.axis_index("c").astype(jnp.int32).reshape(1)
    res = pl.pallas_call(
        body, name=name, grid_spec=grid_spec,
        out_shape=[jax.ShapeDtypeStruct((half, N), BF)] + plumb.out_shapes,
        compiler_params=_params(("arbitrary", "arbitrary"), plumb.collective_id(own=("sibling",))),
    )(core, *([a] * blocks), b, *plumb.args)
    return (res[0], plumb.split_outputs(res[1:])) if comm else res[0]


def _pair_task(parts):
    n = len(parts)

    def copies(ins, outs, scr):
        x, y, c, _ = _place()
        return [pltpu.make_async_remote_copy(
            src_ref=ins[k].at[:, pl.ds(1 - c, 1)], dst_ref=outs[k], send_sem=scr[0].at[k], recv_sem=scr[1].at[k],
            device_id=(x, y, 1 - c), device_id_type=MESH) for k in range(n)]

    def start(ins, outs, scr):
        for cp in copies(ins, outs, scr):
            cp.start()

    def finish(ins, outs, scr):
        for cp in copies(ins, outs, scr):
            cp.wait()

    out_shapes = [jax.ShapeDtypeStruct((4, 1) + p.shape[2:], p.dtype) for p in parts]
    scratch = [pltpu.SemaphoreType.DMA((n,)), pltpu.SemaphoreType.DMA((n,))]
    return _Task(parts, out_shapes, scratch, [(0, start), (1.0, finish)], ("sibling",))


def _pair_sum(name, part, got, core):
    _, _, r, C = part.shape

    def body(core_ref, p_ref, g_ref, o_ref):
        o_ref[0] = (p_ref[0, 0].astype(F32) + g_ref[0, 0].astype(F32)).astype(o_ref.dtype)

    return pl.pallas_call(
        body, name=name,
        grid_spec=pltpu.PrefetchScalarGridSpec(
            num_scalar_prefetch=1, grid=(4,),
            in_specs=[pl.BlockSpec((1, 1, r, C), lambda i, core_ref: (i, core_ref[0], 0, 0)),
                      pl.BlockSpec((1, 1, r, C), lambda i, core_ref: (i, 0, 0, 0))],
            out_specs=pl.BlockSpec((1, r, C), lambda i, core_ref: (i, 0, 0))),
        out_shape=jax.ShapeDtypeStruct((4, r, C), part.dtype), compiler_params=_params(("parallel",)),
    )(core, part, got)


def _ffn_bwd(tag, dy, dyb, x, gain, wgT, wuT, wd, saved, earlier=None):
    n, g, u, a = saved
    half = lambda accs, ex: _swiglu_bwd_epilogue([0.5 * accs[0]], ex)
    act_args = dict(tm=1024, tn=1408, tk=D_MODEL, epilogue=half, extras=[(g, "tile", 0), (u, "tile", 0)], cols_outer=True)
    if earlier is None:
        sum_d = _dw_pair(tag + "_dw_down", a, dyb, 0.5)
        (dg, du), ((slots_d,),) = _mm(tag + "_d_act", [(dyb, wd, "nt", 0)], [BF, BF], comm=[_chip_task([sum_d])], **act_args)
        slots_e = None
        sum_g = _dw_pair(tag + "_dw_gate", dg, n, 1.0)
    else:
        sum_d, ((got,),) = _dw_pair(tag + "_dw_down", a, dyb, 0.5, comm=[_pair_task([earlier])])
        core = lax.axis_index("c").astype(jnp.int32).reshape(1)
        sum_e = _pair_sum(tag + "_pair_sum_earlier", earlier, got, core)
        sum_e = sum_e.reshape(4 * sum_e.shape[1], sum_e.shape[2])
        (dg, du), ((slots_e,),) = _mm(tag + "_d_act", [(dyb, wd, "nt", 0)], [BF, BF], comm=[_chip_task([sum_e])], **act_args)
        sum_g, ((slots_d,),) = _dw_pair(tag + "_dw_gate", dg, n, 1.0, comm=[_chip_task([sum_d])])
    sum_u, ((slots_g,),) = _dw_pair(tag + "_dw_up", du, n, 1.0, comm=[_chip_task([sum_g])])
    (dx, dxb, dgain), ((slots_u,),) = _mm(
        tag + "_d_norm", [(dg, wgT, "nn", 0), (du, wuT, "nn", 0)], [F32, BF], tm=512, tn=D_MODEL, tk=D_FF,
        epilogue=_rms_bwd_epilogue, extras=[(x, "tile", 0), (gain, "row", 0), (dy, "tile", 0)], n_colsum=1,
        comm=[_chip_task([sum_u])])
    return dx, dxb, dgain, slots_e, slots_g, slots_u, slots_d


def _tile_gain(g):
    return jnp.concatenate([g, g]).reshape(1, LANES)


def _fold_heads(partials):
    return jnp.sum(partials.reshape(-1, HEAD_DIM), axis=0)


def _pack_small_grads(grads, loss_local):
    pieces, row = [], 0
    for name, r0, _ in SMALL_LAYOUT + (("loss", LOSS_ROW, None),):
        v = (loss_local if name == "loss" else grads[name]).reshape(-1)
        rows = -(-v.size // LANES)
        block = jnp.pad(v, (0, rows * LANES - v.size)).reshape(rows, LANES)
        pieces += [jnp.zeros((r0 - row, LANES), F32)] * (r0 > row) + [block]
        row = r0 + rows
    pieces.append(jnp.zeros((SMALL_ROWS - row, LANES), F32))
    return jnp.concatenate(pieces, axis=0)


def kernel(x, ffn1_norm, ffn1_w_gate, ffn1_w_up, ffn1_w_down, mix_norm, w_in, pool_w, pool_scale, w_pool_out, q_norm, k_norm, sinks, w_attn_out, gate_bias, w_out, ffn2_norm, ffn2_w_gate, ffn2_w_up, ffn2_w_down, loss_target, m_ffn1_norm, m_ffn1_w_gate, m_ffn1_w_up, m_ffn1_w_down, m_mix_norm, m_w_in, m_pool_w, m_pool_scale, m_w_pool_out, m_q_norm, m_k_norm, m_sinks, m_w_attn_out, m_gate_bias, m_w_out, m_ffn2_norm, m_ffn2_w_gate, m_ffn2_w_up, m_ffn2_w_down, v_ffn1_norm, v_ffn1_w_gate, v_ffn1_w_up, v_ffn1_w_down, v_mix_norm, v_w_in, v_pool_w, v_pool_scale, v_w_pool_out, v_q_norm, v_k_norm, v_sinks, v_w_attn_out, v_gate_bias, v_w_out, v_ffn2_norm, v_ffn2_w_gate, v_ffn2_w_up, v_ffn2_w_down):
    T = x.shape[1]
    x2 = x.reshape(T, D_MODEL)
    target = loss_target.reshape(T, D_MODEL)

    big = [
        ("ffn1_w_gate", ffn1_w_gate, m_ffn1_w_gate, v_ffn1_w_gate, True, False),
        ("ffn1_w_up", ffn1_w_up, m_ffn1_w_up, v_ffn1_w_up, True, False),
        ("ffn1_w_down", ffn1_w_down, m_ffn1_w_down, v_ffn1_w_down, False, False),
        ("w_in", w_in, m_w_in, v_w_in, True, False),
        ("w_pool_out", w_pool_out, m_w_pool_out, v_w_pool_out, False, True),
        ("w_attn_out", w_attn_out, m_w_attn_out, v_w_attn_out, False, False),
        ("w_out", w_out, m_w_out, v_w_out, False, False),
        ("ffn2_w_gate", ffn2_w_gate, m_ffn2_w_gate, v_ffn2_w_gate, True, False),
        ("ffn2_w_up", ffn2_w_up, m_ffn2_w_up, v_ffn2_w_up, True, False),
        ("ffn2_w_down", ffn2_w_down, m_ffn2_w_down, v_ffn2_w_down, False, False),
    ]
    view = lambda a, tv: a.T if tv else a
    views = [view(w, tv) for _, w, _, _, tv, _ in big]
    in_kernel_t = [tk_ for *_, tk_ in big]
    first_shards = _prep("prep_ffn1_gate_up", views[0:2], in_kernel_t[0:2])
    g1 = ffn1_norm.reshape(1, D_MODEL)
    g2 = mix_norm.reshape(1, D_MODEL)
    g3 = ffn2_norm.reshape(1, D_MODEL)
    bias_row = gate_bias.reshape(1, 2 * D_MODEL)
    qg, kg = _tile_gain(q_norm) * ATTN_SCALE, _tile_gain(k_norm)
    scale_row = pool_scale.reshape(1, POOL_WIDTH)

    n1, later_shards, ((wg1T, wu1T),) = _rms_fwd(
        "ffn1_norm", x2, g1, [_gather_task(first_shards, forward_at=0.9)], views[2:], in_kernel_t[2:])
    shards = list(first_shards) + later_shards
    (gt1, up1, act1), ((wd1,), (w_inT,)) = _mm(
        "ffn1_gate_up", [(n1, wg1T, "nt", 0), (n1, wu1T, "nt", 1)], [BF, BF, BF], tm=1024, tn=1408, tk=D_MODEL,
        epilogue=_swiglu_fwd_epilogue, cols_outer=True,
        comm=[_gather_task(shards[2:3], forward_at=0.5), _gather_task(shards[3:4], natural=(0,), forward_at=0.9)])
    (h1, u), ((w_poT, w_ao, w_o),) = _mm(
        "ffn1_down", [(act1, wd1, "nn", 0)], [F32, BF], tm=512, tn=D_MODEL, tk=D_FF,
        epilogue=_residual_norm_epilogue(0.5), extras=[(x2, "tile", 0), (g2, "row", 0)],
        comm=[_gather_task(shards[4:7], natural=(0, 1, 2), forward_at=0.8)])
    saved1 = (n1, gt1, up1, act1)
    (proj,), ((wg2T,),) = _mm(
        "in_proj", [(u, w_inT, "nt", 0)], [BF], tm=1024, tn=1280, tk=D_MODEL, cols_outer=True,
        comm=[_gather_task(shards[7:8], forward_at=0.8)])
    pooled, mixed = _pool_fwd("pool_fwd", proj, pool_w, scale_row)
    qn = _headnorm_fwd("q_norm", proj, COL_Q, ATTN_WIDTH, qg)
    kn = _headnorm_fwd("k_norm", proj, COL_K, KV_WIDTH, kg)
    attn, ((wu2T,),) = _attn_fwd("attn_fwd", qn, kn, proj, sinks, comm=[_gather_task(shards[8:9], forward_at=0.8)])
    (bp,) = _mm("pool_out", [(mixed, w_poT, "nt", 0)], [BF], tm=1024, tn=D_MODEL, tk=POOL_WIDTH)
    gate_tn = 256
    gate_extras = [(proj, "tile", COL_GP // gate_tn), (proj, "tile", COL_GA // gate_tn),
                   (bias_row, "row", 0), (bias_row, "row", D_MODEL // gate_tn)]
    merged, ba = _mm("attn_out_merge", [(attn, w_ao, "nn", 0)], [BF, BF], tm=2048, tn=gate_tn, tk=ATTN_WIDTH,
                     epilogue=_merge_fwd_epilogue, extras=[(bp, "tile", 0)] + gate_extras)
    h2, n2 = _mm("mix_out", [(merged, w_o, "nn", 0)], [F32, BF], tm=1024, tn=D_MODEL, tk=D_MODEL,
                 epilogue=_residual_norm_epilogue(1.0), extras=[(h1, "tile", 0), (g3, "row", 0)])
    (gt2, up2, act2), ((wd2,),) = _mm(
        "ffn2_gate_up", [(n2, wg2T, "nt", 0), (n2, wu2T, "nt", 1)], [BF, BF, BF], tm=1024, tn=1408, tk=D_MODEL,
        epilogue=_swiglu_fwd_epilogue, cols_outer=True, comm=[_gather_task(shards[9:10], forward_at=0.8)])
    dy, dyb, sq = _mm("ffn2_down_loss", [(act2, wd2, "nn", 0)], [F32, BF], tm=512, tn=D_MODEL, tk=D_FF,
                      epilogue=_loss_epilogue, extras=[(h2, "tile", 0), (target, "tile", 0)], n_colsum=1)
    loss_local = 0.5 * jnp.sum(sq) / D_MODEL

    dh2, dh2b, dg3, _, slots_g2, slots_u2, slots_d2 = _ffn_bwd(
        "ffn2", dy, dyb, h2, g3, wg2T, wu2T, wd2, (n2, gt2, up2, act2))
    dbp, dba, dgp, dga, cs_gp, cs_ga = _mm(
        "mix_out_bwd", [(dh2b, w_o, "nt", 0)], [BF, BF, BF, BF], tm=2048, tn=gate_tn, tk=D_MODEL,
        epilogue=_merge_bwd_epilogue, extras=[(bp, "tile", 0), (ba, "tile", 0)] + gate_extras, n_colsum=2)
    sum_o = _dw_pair("dw_out", merged, dh2b, 1.0, blocks=4)
    (dmixed,) = _mm("pool_out_bwd", [(dbp, w_poT, "nn", 0)], [BF], tm=1024, tn=POOL_WIDTH, tk=D_MODEL)
    sum_po = _dw_pair("dw_pool_out", dbp, mixed, 1.0, blocks=4)
    (dattn,) = _mm("attn_out_bwd", [(dba, w_ao, "nt", 0)], [BF], tm=1024, tn=ATTN_WIDTH, tk=D_MODEL)
    sum_ao = _dw_pair("dw_attn_out", attn, dba, 1.0, blocks=4)
    dxp, dpool_w, dpool_scale = _pool_bwd("pool_bwd", dmixed, pooled, pool_w, scale_row)
    (dqn, dkn, dv, dsink_tile), ((slots_o, slots_po, slots_ao),) = _attn_bwd(
        "attn_bwd", dattn, qn, kn, proj, sinks, [_chip_task([sum_o, sum_po, sum_ao])])
    dq, dqg = _headnorm_bwd("q_norm_bwd", dqn, proj, COL_Q, ATTN_WIDTH, qg)
    dk, dkg = _headnorm_bwd("k_norm_bwd", dkn, proj, COL_K, KV_WIDTH, kg)
    dproj = jnp.concatenate([dxp, dq, dk, dv, dgp, dga], axis=1)
    (dh1, dh1b, dg2), ((g_pool_w,),) = _mm(
        "in_proj_bwd", [(dproj, w_inT, "nn", 0)], [F32, BF], tm=512, tn=D_MODEL, tk=IN_WIDTH, epilogue=_rms_bwd_epilogue,
        extras=[(h1, "tile", 0), (g2, "row", 0), (dh2, "tile", 0)], n_colsum=1,
        comm=[_gather_task([dpool_w.reshape(-1, LANES)])])
    (dw_inT,) = _mm("dw_in", [(dproj, u, "tn", 0)], [BF], tm=1280, tn=D_MODEL, tk=2048)
    dx, _, dg1, slots_in, slots_g1, slots_u1, slots_d1 = _ffn_bwd(
        "ffn1", dh1, dh1b, x2, g1, wg1T, wu1T, wd1, saved1, dw_inT.reshape(4, 2, IN_WIDTH // N_DEV, D_MODEL))

    slots = [slots_g1, slots_u1, slots_d1, slots_in, slots_po, slots_ao, slots_o, slots_g2, slots_u2, slots_d2]
    big_out = {}
    for label, group in (("ffn", (0, 1, 2, 7, 8, 9)), ("w_in", (3,)), ("w_pool_out", (4,)), ("attn_out_and_out", (5, 6))):
        items = [(slots[k], view(big[k][1], big[k][4]), view(big[k][2], big[k][4]), view(big[k][3], big[k][4]))
                 for k in group]
        for k, res in zip(group, _adamw_sharded("adamw_" + label, items, transpose=big[group[0]][5])):
            big_out[big[k][0]] = tuple(view(r, big[k][4]) for r in res)

    small_grads = {
        "ffn1_norm": jnp.sum(dg1, axis=(0, 1)), "mix_norm": jnp.sum(dg2, axis=(0, 1)), "ffn2_norm": jnp.sum(dg3, axis=(0, 1)),
        "gate_bias": jnp.concatenate([jnp.sum(cs_gp, axis=(0, 1)), jnp.sum(cs_ga, axis=(0, 1))]),
        "pool_scale": dpool_scale, "q_norm": _fold_heads(dqg) * ATTN_SCALE, "k_norm": _fold_heads(dkg),
        "sinks": dsink_tile[0, :N_HEADS]}
    ((g_vec,),) = _comm_only("gather_small_grads", [_direct_gather_task([_pack_small_grads(small_grads, loss_local)])])
    given = {"ffn1_norm": (ffn1_norm, m_ffn1_norm, v_ffn1_norm), "mix_norm": (mix_norm, m_mix_norm, v_mix_norm),
             "ffn2_norm": (ffn2_norm, m_ffn2_norm, v_ffn2_norm), "gate_bias": (gate_bias, m_gate_bias, v_gate_bias),
             "pool_scale": (pool_scale, m_pool_scale, v_pool_scale), "q_norm": (q_norm, m_q_norm, v_q_norm),
             "k_norm": (k_norm, m_k_norm, v_k_norm), "sinks": (sinks, m_sinks, v_sinks)}
    params = [tuple(a.reshape(shape) for a in given[nm]) for nm, _, shape in SMALL_LAYOUT]
    params.append(tuple(a.reshape(-1, LANES) for a in (pool_w, m_pool_w, v_pool_w)))
    small_res, loss_row = _adamw_small("adamw_small", g_vec.reshape(N_DEV, SMALL_ROWS, LANES),
                                       g_pool_w.reshape(N_DEV, -1, LANES), params)
    small_out = {nm: tuple(r.reshape(given[nm][0].shape) for r in res)
                 for (nm, _, _), res in zip(SMALL_LAYOUT, small_res)}
    small_out["pool_w"] = tuple(r.reshape(pool_w.shape) for r in small_res[-1])
    loss = loss_row[0, 0]

    order = ["ffn1_norm", "ffn1_w_gate", "ffn1_w_up", "ffn1_w_down", "mix_norm", "w_in", "pool_w", "pool_scale",
             "w_pool_out", "q_norm", "k_norm", "sinks", "w_attn_out", "gate_bias", "w_out", "ffn2_norm",
             "ffn2_w_gate", "ffn2_w_up", "ffn2_w_down"]
    every = {**big_out, **small_out}
    outs = [loss, dx.reshape(x.shape)]
    for j in range(4):
        outs += [every[nm][j] for nm in order]
    return tuple(outs)
```

```python
import functools

import jax
import jax.numpy as jnp
from jax import lax
from jax.experimental import pallas as pl
from jax.experimental.pallas import tpu as pltpu

BF = jnp.bfloat16
F32 = jnp.float32

D_MODEL = 1024
D_FF = 2816
POOL_WIDTH = 512
POOL_GROUP = 128
N_POOL_GROUPS = 4
HEAD_DIM = 64
N_HEADS = 16
GQA_GROUP = 8
BLOCK = 128
ATTN_WIDTH = 1024
KV_WIDTH = 128
IN_WIDTH = 3840
RMS_EPS = 1e-6
N_DEV = 8
LANES = 128

COL_Q = POOL_WIDTH
COL_K = COL_Q + ATTN_WIDTH
COL_V = COL_K + KV_WIDTH
COL_GP = COL_V + KV_WIDTH
COL_GA = COL_GP + D_MODEL

ADAM_LR = 0.001
ADAM_B1 = 0.9
ADAM_B2 = 0.999
ADAM_EPS = 1e-08
ADAM_WD = 0.01
ADAM_STEP = 10

VMEM_LIMIT_V7X = 56 * 1024 * 1024
MESH = pl.DeviceIdType.MESH
ANY = pl.BlockSpec(memory_space=pl.ANY)


def _params(sem=None, collective_id=None):
    return pltpu.CompilerParams(dimension_semantics=sem, vmem_limit_bytes=VMEM_LIMIT_V7X, collective_id=collective_id)


COLLECTIVE_IDS = {frozenset(["sibling"]): 0, frozenset(["chips"]): 1, frozenset(["sibling", "chips"]): 2}


def _handshake(peer_kinds):
    x, y, c, chips = _place()
    peers = ([(x, y, 1 - c)] if "sibling" in peer_kinds else []) + ([(*chip, c) for chip in chips] if "chips" in peer_kinds else [])
    barrier = pltpu.get_barrier_semaphore()
    for peer in peers:
        pl.semaphore_signal(barrier, inc=1, device_id=peer, device_id_type=MESH)
    pl.semaphore_wait(barrier, len(peers))


_DIMS = {"nt": (((1,), (1,)), ((), ())), "nn": (((1,), (0,)), ((), ())), "tn": (((0,), (0,)), ((), ()))}


class _Task:
    def __init__(self, inputs, out_shapes, scratch, phases, peers):
        self.inputs, self.out_shapes, self.scratch = list(inputs), list(out_shapes), list(scratch)
        self.phases = list(phases)
        self.peers = frozenset(peers)


class _CommPlumbing:
    def __init__(self, tasks):
        self.tasks = list(tasks or [])
        self.args = [a for t in self.tasks for a in t.inputs]
        self.out_shapes = [o for t in self.tasks for o in t.out_shapes]
        self.scratch = [s for t in self.tasks for s in t.scratch]
        self.n_in, self.n_out = len(self.args), len(self.out_shapes)

    def peer_kinds(self, own=()):
        kinds = frozenset(own).union(*[t.peers for t in self.tasks])
        return None if "all" in kinds or not kinds else kinds

    def collective_id(self, own=()):
        kinds = self.peer_kinds(own)
        return None if kinds is None else COLLECTIVE_IDS[kinds]

    def handshake(self, first, own=()):
        kinds = self.peer_kinds(own)
        if kinds is not None:
            pl.when(first)(functools.partial(_handshake, kinds))

    def _slices(self, c_in, c_out, c_scr):
        i = o = s = 0
        for t in self.tasks:
            yield t, c_in[i:i + len(t.inputs)], c_out[o:o + len(t.out_shapes)], c_scr[s:s + len(t.scratch)]
            i, o, s = i + len(t.inputs), o + len(t.out_shapes), s + len(t.scratch)

    def run(self, step, steps, before, c_in, c_out, c_scr):
        for t, ins, outs, scr in self._slices(c_in, c_out, c_scr):
            for frac, fn in t.phases:
                if step is None:
                    fn(ins, outs, scr)
                elif before == (frac == 0):
                    at = 0 if frac == 0 else max(0, min(steps, -(-int(round(frac * steps * 64)) // 64)) - 1)
                    pl.when(step == at)(functools.partial(fn, ins, outs, scr))

    def split_outputs(self, flat):
        res, o = [], 0
        for t in self.tasks:
            res.append(list(flat[o:o + len(t.out_shapes)]))
            o += len(t.out_shapes)
        return res


def _comm_only(name, tasks):
    plumb = _CommPlumbing(tasks)

    def body(*refs):
        c_in, c_out = refs[:plumb.n_in], refs[plumb.n_in: plumb.n_in + plumb.n_out]
        c_scr = refs[plumb.n_in + plumb.n_out:]
        plumb.run(None, 1, True, c_in, c_out, c_scr)

    res = pl.pallas_call(
        body, name=name, in_specs=[ANY] * plumb.n_in, out_specs=[ANY] * plumb.n_out, out_shape=plumb.out_shapes,
        scratch_shapes=plumb.scratch, compiler_params=pltpu.CompilerParams(has_side_effects=True),
    )(*plumb.args)
    return plumb.split_outputs(res)


def _mm(name, terms, out_dtypes, *, tm, tn, tk, epilogue=None, extras=(), n_colsum=0, comm=None, cols_outer=False):
    a0, b0, mode0, _ = terms[0]
    if mode0 == "nt":
        (M, K), N = a0.shape, b0.shape[0]
    elif mode0 == "nn":
        (M, K), N = a0.shape, b0.shape[1]
    else:
        (K, M), N = a0.shape, b0.shape[1]
    tm, tn, tk = min(tm, M), min(tn, N), min(tk, K)
    assert M % tm == 0 and N % tn == 0 and K % tk == 0, (name, M, N, K, tm, tn, tk)
    nI, nJ, nK = M // tm, N // tn, K // tk
    n_terms = len(terms)
    n_acc = max(t[3] for t in terms) + 1
    n_ex = len(extras)
    n_out = len(out_dtypes)
    if epilogue is None:
        epilogue = lambda accs, ex: ([accs[0]], [])
    plumb = _CommPlumbing(comm)
    n_scr = n_acc if nK > 1 else 0
    grid = (nJ, nI, nK) if cols_outer else (nI, nJ, nK)

    def body(*refs):
        n_in = 2 * n_terms + n_ex
        ab = refs[: 2 * n_terms]
        ex_refs = refs[2 * n_terms: n_in]
        c_in = refs[n_in: n_in + plumb.n_in]
        o0 = n_in + plumb.n_in
        out_refs = refs[o0: o0 + n_out]
        cs_refs = refs[o0 + n_out: o0 + n_out + n_colsum]
        c_out = refs[o0 + n_out + n_colsum: o0 + n_out + n_colsum + plumb.n_out]
        s0 = o0 + n_out + n_colsum + plumb.n_out
        acc_refs = refs[s0: s0 + n_scr]
        c_scr = refs[s0 + n_scr:]
        steps = grid[0] * grid[1] * nK
        if comm:
            step = (pl.program_id(0) * grid[1] + pl.program_id(1)) * nK + pl.program_id(2)
            plumb.handshake(step == 0)
            plumb.run(step, steps, True, c_in, c_out, c_scr)

        def products():
            accs = [None] * n_acc
            for t, (_, _, mode, ai) in enumerate(terms):
                p = lax.dot_general(ab[2 * t][...], ab[2 * t + 1][...], _DIMS[mode], preferred_element_type=F32)
                accs[ai] = p if accs[ai] is None else accs[ai] + p
            return accs

        def finish(accs):
            outs, colsums = epilogue(accs, [r[...] for r in ex_refs])
            for r, o in zip(out_refs, outs):
                r[...] = o.astype(r.dtype)
            for r, cs in zip(cs_refs, colsums):
                r[...] = jnp.sum(cs, axis=0, keepdims=True).reshape(r.shape)

        if nK == 1:
            finish(products())
        else:
            k = pl.program_id(2)
            accs = products()

            @pl.when(k == 0)
            def _():
                for r, a in zip(acc_refs, accs):
                    r[...] = a

            @pl.when(k > 0)
            def _():
                for r, a in zip(acc_refs, accs):
                    r[...] += a

            @pl.when(k == nK - 1)
            def _():
                finish([r[...] for r in acc_refs])

        if comm:
            plumb.run(step, steps, False, c_in, c_out, c_scr)

    def spec(block, index, fixed=False):
        imap = (lambda q, p, k: index(p, q, k)) if cols_outer else index
        return pl.BlockSpec(block, imap, pipeline_mode=pl.Buffered(1)) if fixed else pl.BlockSpec(block, imap)

    in_specs, args = [], []
    for a, b, mode, _ in terms:
        if mode == "nt":
            in_specs += [spec((tm, tk), lambda i, j, k: (i, k), nI * nK == 1),
                         spec((tn, tk), lambda i, j, k: (j, k), nJ * nK == 1)]
        elif mode == "nn":
            in_specs += [spec((tm, tk), lambda i, j, k: (i, k), nI * nK == 1),
                         spec((tk, tn), lambda i, j, k: (k, j), nJ * nK == 1)]
        else:
            in_specs += [spec((tk, tm), lambda i, j, k: (k, i), nI * nK == 1),
                         spec((tk, tn), lambda i, j, k: (k, j), nJ * nK == 1)]
        args += [a, b]
    for arr, kind, off in extras:
        if kind == "tile":
            in_specs.append(spec((tm, tn), functools.partial(lambda i, j, k, off: (i, j + off), off=off)))
        else:
            in_specs.append(spec((1, tn), functools.partial(lambda i, j, k, off: (0, j + off), off=off)))
        args.append(arr)
    out_shape = [jax.ShapeDtypeStruct((M, N), dt) for dt in out_dtypes]
    out_specs = [spec((tm, tn), lambda i, j, k: (i, j)) for _ in out_dtypes]
    out_shape += [jax.ShapeDtypeStruct((nI, 1, N), F32) for _ in range(n_colsum)]
    out_specs += [spec((1, 1, tn), lambda i, j, k: (i, 0, j)) for _ in range(n_colsum)]
    scratch = [pltpu.VMEM((tm, tn), F32) for _ in range(n_scr)]
    args += plumb.args
    in_specs += [ANY] * plumb.n_in
    out_shape += plumb.out_shapes
    out_specs += [ANY] * plumb.n_out
    sem = ("arbitrary",) * 3 if comm else ("parallel", "parallel", "arbitrary")
    res = pl.pallas_call(
        body, name=name, grid=grid, in_specs=in_specs, out_specs=out_specs, out_shape=out_shape,
        scratch_shapes=scratch + plumb.scratch, compiler_params=_params(sem, plumb.collective_id()),
    )(*args)
    n_own = n_out + n_colsum
    return (list(res[:n_own]), plumb.split_outputs(res[n_own:])) if comm is not None else res


ROW_TILE = 512


def _rms_fwd(name, x, g, comm, weights, transposes):
    T, D = x.shape
    steps = T // ROW_TILE
    plumb = _CommPlumbing(comm)
    nw = len(weights)

    def body(x_ref, g_ref, *rest):
        w_refs, c_in = rest[:nw], rest[nw: nw + plumb.n_in]
        o_ref, shard_refs = rest[nw + plumb.n_in], rest[nw + plumb.n_in + 1: 2 * nw + plumb.n_in + 1]
        c_out = rest[2 * nw + plumb.n_in + 1: 2 * nw + plumb.n_in + 1 + plumb.n_out]
        c_scr = rest[2 * nw + plumb.n_in + 1 + plumb.n_out:]
        plumb.handshake(pl.program_id(0) == 0)
        plumb.run(pl.program_id(0), steps, True, c_in, c_out, c_scr)

        @pl.when(pl.program_id(0) == 0)
        def _():
            for w_ref, s_ref, tr in zip(w_refs, shard_refs, transposes):
                v = w_ref[...]
                s_ref[...] = (v.T if tr else v).astype(BF)

        xv = x_ref[...]
        r = lax.rsqrt(jnp.mean(xv * xv, axis=-1, keepdims=True) + RMS_EPS)
        o_ref[...] = (xv * r * g_ref[...]).astype(BF)
        plumb.run(pl.program_id(0), steps, False, c_in, c_out, c_scr)

    row = pl.BlockSpec((ROW_TILE, D), lambda i: (i, 0))
    whole = lambda shape: pl.BlockSpec(shape, lambda i: (0, 0), pipeline_mode=pl.Buffered(1))
    shard_shapes = [w.shape[::-1] if tr else w.shape for w, tr in zip(weights, transposes)]
    res = pl.pallas_call(
        body, name=name, grid=(steps,),
        in_specs=[row, pl.BlockSpec((1, D), lambda i: (0, 0))] + [whole(w.shape) for w in weights] + [ANY] * plumb.n_in,
        out_specs=[row] + [whole(s) for s in shard_shapes] + [ANY] * plumb.n_out,
        out_shape=[jax.ShapeDtypeStruct((T, D), BF)] + [jax.ShapeDtypeStruct(s, BF) for s in shard_shapes] + plumb.out_shapes,
        scratch_shapes=plumb.scratch, compiler_params=_params(("arbitrary",), plumb.collective_id()),
    )(x, g, *weights, *plumb.args)
    return res[0], list(res[1: nw + 1]), plumb.split_outputs(res[nw + 1:])


HEADNORM_TILE = 1024


def _half_sum_matrix():
    r = lax.broadcasted_iota(jnp.int32, (LANES, LANES), 0) // HEAD_DIM
    c = lax.broadcasted_iota(jnp.int32, (LANES, LANES), 1) // HEAD_DIM
    return (r == c).astype(BF)


def _head_mean(v, ones_blockdiag):
    hi = v.astype(BF)
    lo = (v - hi.astype(F32)).astype(BF)
    s = jnp.dot(hi, ones_blockdiag, preferred_element_type=F32) + jnp.dot(lo, ones_blockdiag, preferred_element_type=F32)
    return s * (1.0 / HEAD_DIM)


def _headnorm_fwd(name, proj, col0, width, g2):
    T = proj.shape[0]
    wide = min(width, GROUP_WIDTH)
    nb, off = width // wide, col0 // wide

    def body(x_ref, g_ref, b_ref, o_ref):
        for s in range(wide // LANES):
            lanes = slice(LANES * s, LANES * (s + 1))
            xv = x_ref[:, lanes].astype(F32)
            r = lax.rsqrt(_head_mean(xv * xv, b_ref[...]) + RMS_EPS)
            o_ref[:, lanes] = (xv * r * g_ref[...]).astype(BF)

    return pl.pallas_call(
        body, name=name, grid=(T // HEADNORM_TILE, nb),
        in_specs=[pl.BlockSpec((HEADNORM_TILE, wide), lambda i, j: (i, j + off)),
                  pl.BlockSpec((1, LANES), lambda i, j: (0, 0)), pl.BlockSpec((LANES, LANES), lambda i, j: (0, 0))],
        out_specs=pl.BlockSpec((HEADNORM_TILE, wide), lambda i, j: (i, j)),
        out_shape=jax.ShapeDtypeStruct((T, width), BF), compiler_params=_params(("parallel", "parallel")),
    )(proj, g2, _half_sum_matrix())


def _headnorm_bwd(name, dy, proj, col0, width, g2, into=None):
    T = proj.shape[0]
    wide = min(width, GROUP_WIDTH)
    nb, off = width // wide, col0 // wide

    def body(dy_ref, x_ref, g_ref, b_ref, *rest):
        dx_ref, dg_ref = rest[-2:]
        for s in range(wide // LANES):
            lanes = slice(LANES * s, LANES * (s + 1))
            xv = x_ref[:, lanes].astype(F32)
            dyv = dy_ref[:, lanes].astype(F32)
            r = lax.rsqrt(_head_mean(xv * xv, b_ref[...]) + RMS_EPS)
            xhat = xv * r
            dxhat = dyv * g_ref[...]
            dx_ref[:, lanes] = (r * (dxhat - xhat * _head_mean(dxhat * xhat, b_ref[...]))).astype(BF)
            dg_ref[0, :, lanes] = jnp.sum(dyv * xhat, axis=0, keepdims=True)

    return pl.pallas_call(
        body, name=name, grid=(T // HEADNORM_TILE, nb),
        in_specs=[pl.BlockSpec((HEADNORM_TILE, wide), lambda i, j: (i, j)),
                  pl.BlockSpec((HEADNORM_TILE, wide), lambda i, j: (i, j + off)),
                  pl.BlockSpec((1, LANES), lambda i, j: (0, 0)), pl.BlockSpec((LANES, LANES), lambda i, j: (0, 0))]
        + ([] if into is None else [ANY]),
        out_specs=[pl.BlockSpec((HEADNORM_TILE, wide), lambda i, j: (i, j + off)),
                   pl.BlockSpec((1, 1, wide), lambda i, j: (i, 0, j))],
        out_shape=[jax.ShapeDtypeStruct((T, IN_WIDTH), BF), jax.ShapeDtypeStruct((T // HEADNORM_TILE, 1, width), F32)],
        input_output_aliases={} if into is None else {4: 0},
        compiler_params=_params(("parallel", "parallel")),
    )(dy, proj, g2, _half_sum_matrix(), *([] if into is None else [into]))


def _shift_down(v, k, row):
    return jnp.where(row >= k, pltpu.roll(v, k, axis=0), 0.0)


def _shift_up(v, k, row, T):
    return jnp.where(row < T - k, pltpu.roll(v, T - k, axis=0), 0.0)


def _by_group(g, vals):
    out = vals[-1]
    for i in range(len(vals) - 2, -1, -1):
        out = jnp.where(g == i, vals[i], out)
    return out


def _pool_fwd(name, proj, pool_w, pool_scale):
    T = proj.shape[0]

    def body(x_ref, w_ref, s_ref, pooled_ref, mixed_ref):
        g = pl.program_id(0)
        xv = x_ref[...].astype(F32)
        row = lax.broadcasted_iota(jnp.int32, (T, 1), 0)
        s2 = xv + _shift_down(xv, 1, row)
        s4 = s2 + _shift_down(s2, 2, row)
        s8 = s4 + _shift_down(s4, 4, row)
        s16 = s8 + _shift_down(s8, 8, row)
        wsum = _by_group(g, [s2, s4, s8, s16])
        count = jnp.minimum(row + 1, 2 << g).astype(F32)
        pooled = (wsum / count - xv).astype(BF)
        pooled_ref[...] = pooled
        mixed = jnp.dot(pooled, w_ref[0].astype(BF), preferred_element_type=F32) * s_ref[...]
        mixed_ref[...] = mixed.astype(BF)

    col = pl.BlockSpec((T, POOL_GROUP), lambda g: (0, g))
    return pl.pallas_call(
        body, name=name, grid=(N_POOL_GROUPS,),
        in_specs=[col, pl.BlockSpec((1, POOL_GROUP, POOL_GROUP), lambda g: (g, 0, 0)),
                  pl.BlockSpec((1, POOL_GROUP), lambda g: (0, g))],
        out_specs=[col, col],
        out_shape=[jax.ShapeDtypeStruct((T, POOL_WIDTH), BF), jax.ShapeDtypeStruct((T, POOL_WIDTH), BF)],
        compiler_params=_params(("parallel",)),
    )(proj, pool_w, pool_scale)


def _pool_bwd(name, dmixed, pooled, pool_w, pool_scale, into):
    T = dmixed.shape[0]

    def body(dm_ref, p_ref, w_ref, s_ref, into_ref, dx_ref, dw_ref, ds_ref):
        g = pl.program_id(0)
        dm = dm_ref[...].astype(F32)
        pooled = p_ref[...]
        w = w_ref[0].astype(BF)
        pre = jnp.dot(pooled, w, preferred_element_type=F32)
        ds_ref[...] = jnp.sum(dm * pre, axis=0, keepdims=True)
        dms = (dm * s_ref[...]).astype(BF)
        dw_ref[0] = lax.dot_general(pooled, dms, _DIMS["tn"], preferred_element_type=F32)
        dpooled = lax.dot_general(dms, w, _DIMS["nt"], preferred_element_type=F32)
        row = lax.broadcasted_iota(jnp.int32, (T, 1), 0)
        count = jnp.minimum(row + 1, 2 << g).astype(F32)
        z = dpooled / count
        l2 = z + _shift_up(z, 1, row, T)
        l4 = l2 + _shift_up(l2, 2, row, T)
        l8 = l4 + _shift_up(l4, 4, row, T)
        l16 = l8 + _shift_up(l8, 8, row, T)
        dx_ref[...] = (_by_group(g, [l2, l4, l8, l16]) - dpooled).astype(BF)

    col = pl.BlockSpec((T, POOL_GROUP), lambda g: (0, g))
    wspec = pl.BlockSpec((1, POOL_GROUP, POOL_GROUP), lambda g: (g, 0, 0))
    sspec = pl.BlockSpec((1, POOL_GROUP), lambda g: (0, g))
    return pl.pallas_call(
        body, name=name, grid=(N_POOL_GROUPS,), in_specs=[col, col, wspec, sspec, ANY], out_specs=[col, wspec, sspec],
        out_shape=[jax.ShapeDtypeStruct(into.shape, BF),
                   jax.ShapeDtypeStruct((N_POOL_GROUPS, POOL_GROUP, POOL_GROUP), F32),
                   jax.ShapeDtypeStruct((1, POOL_WIDTH), F32)],
        input_output_aliases={4: 0}, compiler_params=_params(("parallel",)),
    )(dmixed, pooled, pool_w, pool_scale, into)


ATTN_SCALE = HEAD_DIM ** -0.5
MASKED = float(jnp.finfo(jnp.float32).min)
KV_COL_BLOCK_V = COL_V // LANES
GROUP_WIDTH = GQA_GROUP * HEAD_DIM


def _dup_head(v, j):
    half = lax.broadcasted_iota(jnp.int32, (1, LANES), 1) // HEAD_DIM
    return jnp.where(half == j, v, pltpu.roll(v, HEAD_DIM, axis=1))


def _stack_heads(v, low):
    pieces = []
    for p in range(GROUP_WIDTH // LANES):
        vp = v[:, LANES * p: LANES * (p + 1)]
        pieces.append(jnp.where(low, vp, jnp.zeros_like(vp)))
        pieces.append(jnp.where(low, jnp.zeros_like(vp), vp))
    return jnp.concatenate(pieces, axis=0)


def _unstack_transposed(t, low):
    pairs = []
    for p in range(GROUP_WIDTH // LANES):
        even = t[:, BLOCK * (2 * p): BLOCK * (2 * p + 1)].T
        odd = t[:, BLOCK * (2 * p + 1): BLOCK * (2 * p + 2)].T
        pairs.append(jnp.where(low, even, odd))
    return pairs


STACKED = GQA_GROUP * BLOCK


def _band_bias():
    key = lax.broadcasted_iota(jnp.int32, (2, 2 * BLOCK, STACKED), 1)
    qry = lax.broadcasted_iota(jnp.int32, (2, 2 * BLOCK, STACKED), 2) % BLOCK
    first = lax.broadcasted_iota(jnp.int32, (2, 2 * BLOCK, STACKED), 0) == 0
    valid = (key > qry) & (key <= qry + BLOCK) & (jnp.logical_not(first) | (key >= BLOCK))
    return jnp.where(valid, 0.0, MASKED).astype(F32)


BIAS_SPEC = pl.BlockSpec((1, 2 * BLOCK, STACKED), lambda n: (jnp.minimum(n, 1), 0, 0))


def _softmax_keys_on_sublanes(k2, q, bias, sink_ref, j):
    head_of_lane = lax.broadcasted_iota(jnp.int32, (1, STACKED), 1) // BLOCK
    sink = jnp.zeros((1, STACKED), F32)
    for h in range(GQA_GROUP):
        sink = jnp.where(head_of_lane == h, sink_ref[j * GQA_GROUP + h], sink)
    s = lax.dot_general(k2, q, _DIMS["nt"], preferred_element_type=F32) + bias
    m = jnp.maximum(jnp.max(s, axis=0, keepdims=True), sink)
    e = jnp.exp(s - m)
    e_sink = jnp.exp(sink - m)
    inv = 1.0 / (jnp.sum(e, axis=0, keepdims=True) + e_sink)
    return e * inv, e_sink * inv


def _attn_fwd(name, qn, kn, proj, sinks, comm=None):
    T = qn.shape[0]
    nb = T // BLOCK
    plumb = _CommPlumbing(comm)

    def body(sink_ref, bias_ref, q_ref, kp_ref, kc_ref, vp_ref, vc_ref, *rest):
        c_in, o_ref = rest[:plumb.n_in], rest[plumb.n_in]
        c_out, c_scr = rest[plumb.n_in + 1: plumb.n_in + 1 + plumb.n_out], rest[plumb.n_in + 1 + plumb.n_out:]
        n = pl.program_id(0)
        plumb.handshake(n == 0)
        plumb.run(n, nb, True, c_in, c_out, c_scr)
        low = lax.broadcasted_iota(jnp.int32, (1, LANES), 1) < HEAD_DIM
        kk = jnp.concatenate([kp_ref[...], kc_ref[...]], axis=0)
        vv = jnp.concatenate([vp_ref[...], vc_ref[...]], axis=0)
        for j in range(2):
            q = _stack_heads(q_ref[:, GROUP_WIDTH * j: GROUP_WIDTH * (j + 1)], low)
            p, _ = _softmax_keys_on_sublanes(_dup_head(kk, j), q, bias_ref[0], sink_ref, j)
            o_t = lax.dot_general(_dup_head(vv, j), p.astype(BF), _DIMS["tn"], preferred_element_type=F32)
            for pair, o in enumerate(_unstack_transposed(o_t, low)):
                lanes = slice(GROUP_WIDTH * j + LANES * pair, GROUP_WIDTH * j + LANES * (pair + 1))
                o_ref[:, lanes] = o.astype(BF)
        plumb.run(n, nb, False, c_in, c_out, c_scr)

    wide = pl.BlockSpec((BLOCK, ATTN_WIDTH), lambda n: (n, 0))
    res = pl.pallas_call(
        body, name=name, grid=(nb,),
        in_specs=[pl.BlockSpec(memory_space=pltpu.SMEM), BIAS_SPEC, wide,
                  pl.BlockSpec((BLOCK, LANES), lambda n: (jnp.maximum(n - 1, 0), 0)),
                  pl.BlockSpec((BLOCK, LANES), lambda n: (n, 0)),
                  pl.BlockSpec((BLOCK, LANES), lambda n: (jnp.maximum(n - 1, 0), KV_COL_BLOCK_V)),
                  pl.BlockSpec((BLOCK, LANES), lambda n: (n, KV_COL_BLOCK_V))] + [ANY] * plumb.n_in,
        out_specs=[wide] + [ANY] * plumb.n_out,
        out_shape=[jax.ShapeDtypeStruct((T, ATTN_WIDTH), BF)] + plumb.out_shapes, scratch_shapes=plumb.scratch,
        compiler_params=_params(("arbitrary",) if comm else ("parallel",), plumb.collective_id()),
    )(sinks, _band_bias(), qn, kn, kn, proj, proj, *plumb.args)
    return (res[0], plumb.split_outputs(res[1:])) if comm is not None else res[0]


def _attn_bwd(name, dout, qn, kn, proj, sinks, comm):
    T = qn.shape[0]
    nb = T // BLOCK
    plumb = _CommPlumbing(comm)

    def body(sink_ref, bias_ref, do_ref, q_ref, kp_ref, kc_ref, vp_ref, vc_ref, *rest):
        c_in, (dq_ref, dk_ref, dv_ref, dsink_ref) = rest[:plumb.n_in], rest[plumb.n_in: plumb.n_in + 4]
        c_out = rest[plumb.n_in + 4: plumb.n_in + 4 + plumb.n_out]
        carry_k, carry_v, tot_k, tot_v = rest[plumb.n_in + 4 + plumb.n_out: plumb.n_in + 8 + plumb.n_out]
        c_scr = rest[plumb.n_in + 8 + plumb.n_out:]
        n = pl.program_id(0)
        plumb.handshake(n == 0)
        plumb.run(n, nb + 1, True, c_in, c_out, c_scr)
        lane = lax.broadcasted_iota(jnp.int32, (1, LANES), 1)
        low = lane < HEAD_DIM

        @pl.when(n == 0)
        def _():
            carry_k[...] = jnp.zeros_like(carry_k)
            carry_v[...] = jnp.zeros_like(carry_v)
            dsink_ref[...] = jnp.zeros_like(dsink_ref)

        @pl.when(n == nb)
        def _():
            tot_k[...] = jnp.zeros_like(tot_k)
            tot_v[...] = jnp.zeros_like(tot_v)

        @pl.when(n < nb)
        def _():
            kk = jnp.concatenate([kp_ref[...], kc_ref[...]], axis=0)
            vv = jnp.concatenate([vp_ref[...], vc_ref[...]], axis=0)
            dk_tot = jnp.zeros((2 * BLOCK, LANES), F32)
            dv_tot = jnp.zeros((2 * BLOCK, LANES), F32)
            dsink = jnp.zeros((1, LANES), F32)
            for j in range(2):
                k2 = _dup_head(kk, j)
                v2 = _dup_head(vv, j)
                q = _stack_heads(q_ref[:, GROUP_WIDTH * j: GROUP_WIDTH * (j + 1)], low)
                do = _stack_heads(do_ref[:, GROUP_WIDTH * j: GROUP_WIDTH * (j + 1)], low)
                p, psink = _softmax_keys_on_sublanes(k2, q, bias_ref[0], sink_ref, j)
                dp =lax.dot_general(v2, do, _DIMS["nt"], preferred_element_type=F32)
                delta = jnp.sum(p * dp, axis=0, keepdims=True)
                ds = (p * (dp - delta)).astype(BF)
                dk2 = jnp.dot(ds, q, preferred_element_type=F32)
                dv2 = jnp.dot(p.astype(BF), do, preferred_element_type=F32)
                dq_t = lax.dot_general(k2, ds, _DIMS["tn"], preferred_element_type=F32)
                for pair, dq in enumerate(_unstack_transposed(dq_t, low)):
                    lanes = slice(GROUP_WIDTH * j + LANES * pair, GROUP_WIDTH * j + LANES * (pair + 1))
                    dq_ref[:, lanes] = dq.astype(BF)
                mine = low if j == 0 else jnp.logical_not(low)
                dk_tot = dk_tot + jnp.where(mine, dk2 + pltpu.roll(dk2, HEAD_DIM, axis=1), 0.0)
                dv_tot = dv_tot + jnp.where(mine, dv2 + pltpu.roll(dv2, HEAD_DIM, axis=1), 0.0)
                sink_term = psink * delta
                for h in range(GQA_GROUP):
                    val = -jnp.sum(sink_term[:, BLOCK * h: BLOCK * (h + 1)], axis=1, keepdims=True)
                    dsink = dsink + jnp.where(lane == j * GQA_GROUP + h, val, 0.0)
            tot_k[...] = dk_tot
            tot_v[...] = dv_tot
            dsink_ref[0:1, :] += dsink

        dk_ref[...] = (carry_k[...] + tot_k[0:BLOCK]).astype(BF)
        dv_ref[...] = (carry_v[...] + tot_v[0:BLOCK]).astype(BF)
        carry_k[...] = tot_k[BLOCK:]
        carry_v[...] = tot_v[BLOCK:]
        plumb.run(n, nb + 1, False, c_in, c_out, c_scr)

    cur = lambda n: (jnp.minimum(n, nb - 1), 0)
    prev = lambda n: (jnp.maximum(n - 1, 0), 0)
    wide = pl.BlockSpec((BLOCK, ATTN_WIDTH), cur)
    res = pl.pallas_call(
        body, name=name, grid=(nb + 1,),
        in_specs=[pl.BlockSpec(memory_space=pltpu.SMEM), BIAS_SPEC, wide, wide,
                  pl.BlockSpec((BLOCK, LANES), prev), pl.BlockSpec((BLOCK, LANES), cur),
                  pl.BlockSpec((BLOCK, LANES), lambda n: (jnp.maximum(n - 1, 0), KV_COL_BLOCK_V)),
                  pl.BlockSpec((BLOCK, LANES), lambda n: (jnp.minimum(n, nb - 1), KV_COL_BLOCK_V))] + [ANY] * plumb.n_in,
        out_specs=[wide, pl.BlockSpec((BLOCK, LANES), prev), pl.BlockSpec((BLOCK, LANES), prev),
                   pl.BlockSpec((8, LANES), lambda n: (0, 0))] + [ANY] * plumb.n_out,
        out_shape=[jax.ShapeDtypeStruct((T, ATTN_WIDTH), BF), jax.ShapeDtypeStruct((T, KV_WIDTH), BF),
                   jax.ShapeDtypeStruct((T, KV_WIDTH), BF), jax.ShapeDtypeStruct((8, LANES), F32)] + plumb.out_shapes,
        scratch_shapes=[pltpu.VMEM((BLOCK, LANES), F32), pltpu.VMEM((BLOCK, LANES), F32),
                        pltpu.VMEM((2 * BLOCK, LANES), F32), pltpu.VMEM((2 * BLOCK, LANES), F32)] + plumb.scratch,
        compiler_params=_params(("arbitrary",), plumb.collective_id()),
    )(sinks, _band_bias(), dout, qn, kn, kn, proj, proj, *plumb.args)
    return list(res[:4]), plumb.split_outputs(res[4:])


def _swiglu_fwd_epilogue(accs, ex):
    g, u = accs
    return [g, u, g * jax.nn.sigmoid(g) * u], []


def _swiglu_bwd_epilogue(accs, ex):
    (da,) = accs
    g, u = ex[0].astype(F32), ex[1].astype(F32)
    s = jax.nn.sigmoid(g)
    gs = g * s
    return [da * u * (s + gs - gs * s), da * gs], []


def _residual_norm_epilogue(scale):
    def epilogue(accs, ex):
        res, gain = ex
        h = res + scale * accs[0]
        r = lax.rsqrt(jnp.mean(h * h, axis=-1, keepdims=True) + RMS_EPS)
        return [h, h * r * gain], []
    return epilogue


def _rms_bwd_epilogue(accs, ex):
    (dn,) = accs
    xv, g, dres = ex
    r = lax.rsqrt(jnp.mean(xv * xv, axis=-1, keepdims=True) + RMS_EPS)
    xhat = xv * r
    dxhat = dn * g
    dx = dres + r * (dxhat - xhat * jnp.mean(dxhat * xhat, axis=-1, keepdims=True))
    return [dx, dx], [dn * xhat]


def _loss_epilogue(accs, ex):
    xv, target = ex
    d = xv + 0.5 * accs[0] - target
    dy = d * (1.0 / D_MODEL)
    return [dy, dy], [d * d]


def _merge_fwd_epilogue(accs, ex):
    (ba,) = accs
    bp, gp_pre, ga_pre, bias_p, bias_a = ex
    gp = jax.nn.sigmoid(gp_pre.astype(F32) + bias_p)
    ga = jax.nn.sigmoid(ga_pre.astype(F32) + bias_a)
    return [gp * bp.astype(F32) + ga * ba, ba], []


def _merge_bwd_epilogue(accs, ex):
    (dm,) = accs
    bp, ba, gp_pre, ga_pre, bias_p, bias_a = ex
    gp = jax.nn.sigmoid(gp_pre.astype(F32) + bias_p)
    ga = jax.nn.sigmoid(ga_pre.astype(F32) + bias_a)
    dbp, dba = dm * gp, dm * ga
    dgp = dbp * bp.astype(F32) * (1.0 - gp)
    dga = dba * ba.astype(F32) * (1.0 - ga)
    return [dbp, dba, dgp, dga], [dgp, dga]


def _prep(name, ws, transposes):
    n = len(ws)

    def body(*refs):
        for w_ref, o_ref, tr in zip(refs[:n], refs[n:], transposes):
            v = w_ref[...]
            o_ref[...] = (v.T if tr else v).astype(BF)

    shapes = [jax.ShapeDtypeStruct(w.shape[::-1] if tr else w.shape, BF) for w, tr in zip(ws, transposes)]
    return pl.pallas_call(body, name=name, out_shape=shapes, compiler_params=_params())(*ws)


def _adam_math(w, g, m, v):
    m = ADAM_B1 * m + (1.0 - ADAM_B1) * g
    v = ADAM_B2 * v + (1.0 - ADAM_B2) * jnp.square(g)
    m_hat = m / (1.0 - ADAM_B1 ** ADAM_STEP)
    v_hat = v / (1.0 - ADAM_B2 ** ADAM_STEP)
    delta = -ADAM_LR * (m_hat / (jnp.sqrt(v_hat) + ADAM_EPS) + ADAM_WD * w)
    return delta, m, v


def _adamw_sharded(name, items, transpose=False):
    n = len(items)

    def body(*refs):
        ins, outs = refs[:4 * n], refs[4 * n:]
        for k in range(n):
            s_ref, w_ref, m_ref, v_ref = ins[4 * k: 4 * k + 4]
            g = s_ref[0].astype(F32)
            for i in range(1, 4):
                g = g + s_ref[i].astype(F32)
            if transpose:
                g = g.T
            delta, mn, vn = _adam_math(w_ref[...], g, m_ref[...], v_ref[...])
            for o_ref, val in zip(outs[4 * k: 4 * k + 4], (g, delta, mn, vn)):
                o_ref[...] = val

    flat = [a for item in items for a in item]
    out_shape = [jax.ShapeDtypeStruct(item[1].shape, F32) for item in items for _ in range(4)]
    _, r, C = items[0][0].shape
    rows = r // 4
    if transpose or rows % 8:
        res = pl.pallas_call(body, name=name, out_shape=out_shape, compiler_params=_params())(*flat)
    else:
        tile = pl.BlockSpec((rows, C), lambda i: (i, 0))
        res = pl.pallas_call(
            body, name=name, grid=(4,), in_specs=[pl.BlockSpec((4, rows, C), lambda i: (0, i, 0)), tile, tile, tile] * n,
            out_specs=[tile] * (4 * n), out_shape=out_shape, compiler_params=_params(("parallel",)),
        )(*flat)
    return [tuple(res[4 * k: 4 * k + 4]) for k in range(n)]


SMALL_LAYOUT = (("ffn1_norm", 0, (8, LANES)), ("mix_norm", 8, (8, LANES)), ("ffn2_norm", 16, (8, LANES)),
                ("gate_bias", 24, (16, LANES)), ("pool_scale", 40, (4, LANES)), ("q_norm", 48, (1, HEAD_DIM)),
                ("k_norm", 56, (1, HEAD_DIM)), ("sinks", 64, (1, N_HEADS)))
LOSS_ROW = 72
SMALL_ROWS = 80


def _adamw_small(name, g_vec, g_pool_w, params):
    n = len(SMALL_LAYOUT) + 1

    def body(vec_ref, pw_ref, *refs):
        ins, outs = refs[:3 * n], refs[3 * n:]
        vec = vec_ref[0]
        pw = pw_ref[0]
        for i in range(1, N_DEV):
            vec = vec + vec_ref[i]
            pw = pw + pw_ref[i]
        grads = [vec[r0:r0 + shape[0], 0:shape[1]] for _, r0, shape in SMALL_LAYOUT] + [pw]
        for p, g in enumerate(grads):
            w_ref, m_ref, v_ref = ins[3 * p: 3 * p + 3]
            delta, mn, vn = _adam_math(w_ref[...], g, m_ref[...], v_ref[...])
            for o_ref, val in zip(outs[4 * p: 4 * p + 4], (g, delta, mn, vn)):
                o_ref[...] = val
        outs[4 * n][...] = vec[LOSS_ROW:LOSS_ROW + 1, :]

    flat = [a for wmv in params for a in wmv]
    out_shape = [jax.ShapeDtypeStruct(wmv[0].shape, F32) for wmv in params for _ in range(4)]
    out_shape.append(jax.ShapeDtypeStruct((1, LANES), F32))
    res = pl.pallas_call(body, name=name, out_shape=out_shape, compiler_params=_params())(g_vec, g_pool_w, *flat)
    return [tuple(res[4 * p: 4 * p + 4]) for p in range(n)], res[4 * n]


def _place():
    x, y, c = lax.axis_index("x"), lax.axis_index("y"), lax.axis_index("c")
    other_chips = [(1 - x, y), (x, 1 - y), (1 - x, 1 - y)]
    return x, y, c, other_chips


def _rows(ref, r, place, natural=False):
    px, py, pc = place
    b = 4 * px + 2 * py + pc if natural else 4 * pc + 2 * px + py
    return ref.at[pl.ds(pl.multiple_of(b * r, 8), r), :]


def _gather_task(shards, natural=(), forward_at=0.75):
    n = len(shards)
    rs = [s.shape[0] for s in shards]
    rows_of = lambda ref, k, place: _rows(ref, rs[k], place, k in natural)

    def copy(scr, outs, k, slot, block, to, src=None):
        rows = rows_of(outs[k], k, block)
        return pltpu.make_async_remote_copy(
            src_ref=rows if src is None else src, dst_ref=rows, send_sem=scr[0].at[7 * k + slot],
            recv_sem=scr[1].at[7 * k + slot], device_id=to, device_id_type=MESH)

    def first_sends(ins, outs, scr):
        x, y, c, chips = _place()
        me = (x, y, c)
        cps = [copy(scr, outs, k, 1 + j, me, (*chip, c), src=ins[k]) for j, chip in enumerate(chips) for k in range(n)]
        return cps + [copy(scr, outs, k, 0, me, (x, y, 1 - c), src=ins[k]) for k in range(n)]

    def passed_on(outs, scr):
        x, y, c, chips = _place()
        return [copy(scr, outs, k, 4 + j, (*chip, c), (x, y, 1 - c)) for j, chip in enumerate(chips) for k in range(n)]

    def local(ins, outs, scr):
        x, y, c, _ = _place()
        return [pltpu.make_async_copy(ins[k], rows_of(outs[k], k, (x, y, c)), scr[2].at[k]) for k in range(n)]

    def start(ins, outs, scr):
        for cp in local(ins, outs, scr) + first_sends(ins, outs, scr):
            cp.start()

    def forward(ins, outs, scr):
        x, y, c, chips = _place()
        for j, chip in enumerate(chips):
            for k in range(n):
                copy(scr, outs, k, 1 + j, (*chip, c), (x, y, c)).wait_recv()
                copy(scr, outs, k, 4 + j, (*chip, c), (x, y, 1 - c)).start()

    def finish(ins, outs, scr):
        x, y, c, chips = _place()
        for k in range(n):
            copy(scr, outs, k, 0, (x, y, 1 - c), (x, y, c)).wait_recv()
        for j, chip in enumerate(chips):
            for k in range(n):
                copy(scr, outs, k, 4 + j, (*chip, 1 - c), (x, y, c)).wait_recv()
        for cp in first_sends(ins, outs, scr) + passed_on(outs, scr):
            cp.wait_send()
        for cp in local(ins, outs, scr):
            cp.wait()

    out_shapes = [jax.ShapeDtypeStruct((N_DEV * s.shape[0], s.shape[1]), s.dtype) for s in shards]
    scratch = [pltpu.SemaphoreType.DMA((7 * n,)), pltpu.SemaphoreType.DMA((7 * n,)), pltpu.SemaphoreType.DMA((n,))]
    return _Task(shards, out_shapes, scratch, [(0, start), (forward_at, forward), (1.0, finish)], ("sibling", "chips"))


def _direct_gather_task(shards):
    n = len(shards)
    rs = [s.shape[0] for s in shards]

    def peers():
        x, y, c, _ = _place()
        flip = lambda v, bit: 1 - v if bit else v
        return (x, y, c), [(flip(x, (s >> 2) & 1), flip(y, (s >> 1) & 1), flip(c, s & 1)) for s in range(1, N_DEV)]

    def copies(ins, outs, scr):
        me, others = peers()
        local = [pltpu.make_async_copy(ins[k], _rows(outs[k], rs[k], me), scr[2].at[k]) for k in range(n)]
        sems = lambda k, s: dict(send_sem=scr[0].at[7 * k + s], recv_sem=scr[1].at[7 * k + s], device_id_type=MESH)
        sends = [pltpu.make_async_remote_copy(src_ref=ins[k], dst_ref=_rows(outs[k], rs[k], me), device_id=to, **sems(k, s))
                 for s, to in enumerate(others) for k in range(n)]
        recvs = [pltpu.make_async_remote_copy(src_ref=_rows(outs[k], rs[k], frm), dst_ref=_rows(outs[k], rs[k], frm),
                                              device_id=me, **sems(k, s))
                 for s, frm in enumerate(others) for k in range(n)]
        return local, sends, recvs

    def start(ins, outs, scr):
        local, sends, _ = copies(ins, outs, scr)
        for cp in local + sends:
            cp.start()

    def finish(ins, outs, scr):
        local, sends, recvs = copies(ins, outs, scr)
        for cp in recvs:
            cp.wait_recv()
        for cp in sends:
            cp.wait_send()
        for cp in local:
            cp.wait()

    out_shapes = [jax.ShapeDtypeStruct((N_DEV * s.shape[0], s.shape[1]), s.dtype) for s in shards]
    scratch = [pltpu.SemaphoreType.DMA((7 * n,)), pltpu.SemaphoreType.DMA((7 * n,)), pltpu.SemaphoreType.DMA((n,))]
    return _Task(shards, out_shapes, scratch, [(0, start), (1.0, finish)], ("all",))


def _chip_task(sums):
    n = len(sums)
    rs = [s.shape[0] // 4 for s in sums]

    def block(ref, k, chip_index):
        return ref.at[pl.ds(pl.multiple_of(chip_index * rs[k], 8), rs[k]), :]

    def copies(ins, outs, scr):
        send_sems, recv_sems, local_sems = scr
        x, y, c, chips = _place()
        here = 2 * x + y
        local = [pltpu.make_async_copy(block(ins[k], k, here), outs[k].at[here], local_sems.at[k]) for k in range(n)]
        remote = []
        for j, (px, py) in enumerate(chips):
            remote += [pltpu.make_async_remote_copy(
                src_ref=block(ins[k], k, 2 * px + py), dst_ref=outs[k].at[here],
                send_sem=send_sems.at[3 * k + j], recv_sem=recv_sems.at[3 * k + j],
                device_id=(px, py, c), device_id_type=MESH) for k in range(n)]
        return local, remote

    def start(ins, outs, scr):
        local, remote = copies(ins, outs, scr)
        for cp in local + remote:
            cp.start()

    def finish(ins, outs, scr):
        local, remote = copies(ins, outs, scr)
        for cp in remote:
            cp.wait()
        for cp in local:
            cp.wait()

    out_shapes = [jax.ShapeDtypeStruct((4, r, s.shape[1]), s.dtype) for r, s in zip(rs, sums)]
    scratch = [pltpu.SemaphoreType.DMA((3 * n,)), pltpu.SemaphoreType.DMA((3 * n,)), pltpu.SemaphoreType.DMA((n,))]
    return _Task(sums, out_shapes, scratch, [(0, start), (1.0, finish)], ("chips",))


def _dw_pair(name, a, b, scale, comm=None, blocks=1):
    T, M = a.shape
    N = b.shape[1]
    half = M // 2
    wide = half // blocks
    tk = min(2048, T)
    nK = T // tk
    plumb = _CommPlumbing(comm)

    def body(core_ref, *rest):
        a_refs, b_ref, rest = rest[:blocks], rest[blocks], rest[blocks + 1:]
        c_in = rest[:plumb.n_in]
        o_ref = rest[plumb.n_in]
        c_out = rest[plumb.n_in + 1: plumb.n_in + 1 + plumb.n_out]
        acc, stage, land, send_sem, recv_sem = rest[plumb.n_in + 1 + plumb.n_out: plumb.n_in + 6 + plumb.n_out]
        c_scr = rest[plumb.n_in + 6 + plumb.n_out:]
        i, k = pl.program_id(0), pl.program_id(1)
        x, y, c, _ = _place()
        push = pltpu.make_async_remote_copy(src_ref=stage, dst_ref=land, send_sem=send_sem, recv_sem=recv_sem,
                                            device_id=(x, y, 1 - c), device_id_type=MESH)
        plumb.handshake((i == 0) & (k == 0), own=("sibling",))
        if comm:
            plumb.run(i * nK + k, 2 * nK, True, c_in, c_out, c_scr)

        av = a_refs[0][...] if blocks == 1 else jnp.concatenate([r[...] for r in a_refs], axis=1)
        p = lax.dot_general(av, b_ref[...], _DIMS["tn"], preferred_element_type=F32)

        @pl.when(k == 0)
        def _():
            acc[...] = p

        @pl.when(k > 0)
        def _():
            acc[...] += p

        @pl.when((i == 0) & (k == nK - 1))
        def _():
            stage[...] = (scale * acc[...]).astype(BF)
            push.start()

        @pl.when((i == 1) & (k == nK - 1))
        def _():
            push.wait_recv()
            o_ref[...] = (scale * acc[...] + land[...].astype(F32)).astype(BF)
            push.wait_send()

        if comm:
            plumb.run(i * nK + k, 2 * nK, False, c_in, c_out, c_scr)

    grid_spec = pltpu.PrefetchScalarGridSpec(
        num_scalar_prefetch=1, grid=(2, nK),
        in_specs=[pl.BlockSpec((tk, wide), functools.partial(
            lambda i, k, core, j: (k, (2 * j if blocks > 1 else 0) + jnp.where(i == 0, 1 - core[0], core[0])), j=j))
            for j in range(blocks)] + [pl.BlockSpec((tk, N), lambda i, k, core: (k, 0))] + [ANY] * plumb.n_in,
        out_specs=[pl.BlockSpec((half, N), lambda i, k, core: (0, 0))] + [ANY] * plumb.n_out,
        scratch_shapes=[pltpu.VMEM((half, N), F32), pltpu.VMEM((half, N), BF), pltpu.VMEM((half, N), BF),
                        pltpu.SemaphoreType.DMA, pltpu.SemaphoreType.DMA] + plumb.scratch)
    core = lax.axis_index("c").astype(jnp.int32).reshape(1)
    res = pl.pallas_call(
        body, name=name, grid_spec=grid_spec,
        out_shape=[jax.ShapeDtypeStruct((half, N), BF)] + plumb.out_shapes,
        compiler_params=_params(("arbitrary", "arbitrary"), plumb.collective_id(own=("sibling",))),
    )(core, *([a] * blocks), b, *plumb.args)
    return (res[0], plumb.split_outputs(res[1:])) if comm else res[0]


def _pair_task(parts):
    n = len(parts)

    def copies(ins, outs, scr):
        x, y, c, _ = _place()
        return [pltpu.make_async_remote_copy(
            src_ref=ins[k].at[:, pl.ds(1 - c, 1)], dst_ref=outs[k], send_sem=scr[0].at[k], recv_sem=scr[1].at[k],
            device_id=(x, y, 1 - c), device_id_type=MESH) for k in range(n)]

    def start(ins, outs, scr):
        for cp in copies(ins, outs, scr):
            cp.start()

    def finish(ins, outs, scr):
        for cp in copies(ins, outs, scr):
            cp.wait()

    out_shapes = [jax.ShapeDtypeStruct((4, 1) + p.shape[2:], p.dtype) for p in parts]
    scratch = [pltpu.SemaphoreType.DMA((n,)), pltpu.SemaphoreType.DMA((n,))]
    return _Task(parts, out_shapes, scratch, [(0, start), (1.0, finish)], ("sibling",))


def _pair_sum(name, part, got, core):
    _, _, r, C = part.shape

    def body(core_ref, p_ref, g_ref, o_ref):
        o_ref[0] = (p_ref[0, 0].astype(F32) + g_ref[0, 0].astype(F32)).astype(o_ref.dtype)

    return pl.pallas_call(
        body, name=name,
        grid_spec=pltpu.PrefetchScalarGridSpec(
            num_scalar_prefetch=1, grid=(4,),
            in_specs=[pl.BlockSpec((1, 1, r, C), lambda i, core_ref: (i, core_ref[0], 0, 0)),
                      pl.BlockSpec((1, 1, r, C), lambda i, core_ref: (i, 0, 0, 0))],
            out_specs=pl.BlockSpec((1, r, C), lambda i, core_ref: (i, 0, 0))),
        out_shape=jax.ShapeDtypeStruct((4, r, C), part.dtype), compiler_params=_params(("parallel",)),
    )(core, part, got)


def _ffn_bwd(tag, dy, dyb, x, gain, wgT, wuT, wd, saved, earlier=None):
    n, g, u, a = saved
    half = lambda accs, ex: _swiglu_bwd_epilogue([0.5 * accs[0]], ex)
    act_args = dict(tm=1024, tn=1408, tk=D_MODEL, epilogue=half, extras=[(g, "tile", 0), (u, "tile", 0)], cols_outer=True)
    if earlier is None:
        sum_d = _dw_pair(tag + "_dw_down", a, dyb, 0.5)
        (dg, du), ((slots_d,),) = _mm(tag + "_d_act", [(dyb, wd, "nt", 0)], [BF, BF], comm=[_chip_task([sum_d])], **act_args)
        slots_e = None
        sum_g = _dw_pair(tag + "_dw_gate", dg, n, 1.0)
    else:
        sum_d, ((got,),) = _dw_pair(tag + "_dw_down", a, dyb, 0.5, comm=[_pair_task([earlier])])
        core = lax.axis_index("c").astype(jnp.int32).reshape(1)
        sum_e = _pair_sum(tag + "_pair_sum_earlier", earlier, got, core)
        sum_e = sum_e.reshape(4 * sum_e.shape[1], sum_e.shape[2])
        (dg, du), ((slots_e,),) = _mm(tag + "_d_act", [(dyb, wd, "nt", 0)], [BF, BF], comm=[_chip_task([sum_e])], **act_args)
        sum_g, ((slots_d,),) = _dw_pair(tag + "_dw_gate", dg, n, 1.0, comm=[_chip_task([sum_d])])
    sum_u, ((slots_g,),) = _dw_pair(tag + "_dw_up", du, n, 1.0, comm=[_chip_task([sum_g])])
    (dx, dxb, dgain), ((slots_u,),) = _mm(
        tag + "_d_norm", [(dg, wgT, "nn", 0), (du, wuT, "nn", 0)], [F32, BF], tm=512, tn=D_MODEL, tk=D_FF,
        epilogue=_rms_bwd_epilogue, extras=[(x, "tile", 0), (gain, "row", 0), (dy, "tile", 0)], n_colsum=1,
        comm=[_chip_task([sum_u])])
    return dx, dxb, dgain, slots_e, slots_g, slots_u, slots_d


def _tile_gain(g):
    return jnp.concatenate([g, g]).reshape(1, LANES)


def _fold_heads(partials):
    return jnp.sum(partials.reshape(-1, HEAD_DIM), axis=0)


def _pack_small_grads(grads, loss_local):
    pieces, row = [], 0
    for name, r0, _ in SMALL_LAYOUT + (("loss", LOSS_ROW, None),):
        v = (loss_local if name == "loss" else grads[name]).reshape(-1)
        rows = -(-v.size // LANES)
        block = jnp.pad(v, (0, rows * LANES - v.size)).reshape(rows, LANES)
        pieces += [jnp.zeros((r0 - row, LANES), F32)] * (r0 > row) + [block]
        row = r0 + rows
    pieces.append(jnp.zeros((SMALL_ROWS - row, LANES), F32))
    return jnp.concatenate(pieces, axis=0)


def kernel(x, ffn1_norm, ffn1_w_gate, ffn1_w_up, ffn1_w_down, mix_norm, w_in, pool_w, pool_scale, w_pool_out, q_norm, k_norm, sinks, w_attn_out, gate_bias, w_out, ffn2_norm, ffn2_w_gate, ffn2_w_up, ffn2_w_down, loss_target, m_ffn1_norm, m_ffn1_w_gate, m_ffn1_w_up, m_ffn1_w_down, m_mix_norm, m_w_in, m_pool_w, m_pool_scale, m_w_pool_out, m_q_norm, m_k_norm, m_sinks, m_w_attn_out, m_gate_bias, m_w_out, m_ffn2_norm, m_ffn2_w_gate, m_ffn2_w_up, m_ffn2_w_down, v_ffn1_norm, v_ffn1_w_gate, v_ffn1_w_up, v_ffn1_w_down, v_mix_norm, v_w_in, v_pool_w, v_pool_scale, v_w_pool_out, v_q_norm, v_k_norm, v_sinks, v_w_attn_out, v_gate_bias, v_w_out, v_ffn2_norm, v_ffn2_w_gate, v_ffn2_w_up, v_ffn2_w_down):
    T = x.shape[1]
    x2 = x.reshape(T, D_MODEL)
    target = loss_target.reshape(T, D_MODEL)

    big = [
        ("ffn1_w_gate", ffn1_w_gate, m_ffn1_w_gate, v_ffn1_w_gate, True, False),
        ("ffn1_w_up", ffn1_w_up, m_ffn1_w_up, v_ffn1_w_up, True, False),
        ("ffn1_w_down", ffn1_w_down, m_ffn1_w_down, v_ffn1_w_down, False, False),
        ("w_in", w_in, m_w_in, v_w_in, True, False),
        ("w_pool_out", w_pool_out, m_w_pool_out, v_w_pool_out, False, True),
        ("w_attn_out", w_attn_out, m_w_attn_out, v_w_attn_out, False, False),
        ("w_out", w_out, m_w_out, v_w_out, False, False),
        ("ffn2_w_gate", ffn2_w_gate, m_ffn2_w_gate, v_ffn2_w_gate, True, False),
        ("ffn2_w_up", ffn2_w_up, m_ffn2_w_up, v_ffn2_w_up, True, False),
        ("ffn2_w_down", ffn2_w_down, m_ffn2_w_down, v_ffn2_w_down, False, False),
    ]
    view = lambda a, tv: a.T if tv else a
    views = [view(w, tv) for _, w, _, _, tv, _ in big]
    in_kernel_t = [tk_ for *_, tk_ in big]
    first_shards = _prep("prep_ffn1_gate_up", views[0:2], in_kernel_t[0:2])
    g1 = ffn1_norm.reshape(1, D_MODEL)
    g2 = mix_norm.reshape(1, D_MODEL)
    g3 = ffn2_norm.reshape(1, D_MODEL)
    bias_row = gate_bias.reshape(1, 2 * D_MODEL)
    qg, kg = _tile_gain(q_norm) * ATTN_SCALE, _tile_gain(k_norm)
    scale_row = pool_scale.reshape(1, POOL_WIDTH)

    n1, later_shards, ((wg1T, wu1T),) = _rms_fwd(
        "ffn1_norm", x2, g1, [_gather_task(first_shards, forward_at=0.9)], views[2:], in_kernel_t[2:])
    shards = list(first_shards) + later_shards
    (gt1, up1, act1), ((wd1,), (w_inT,)) = _mm(
        "ffn1_gate_up", [(n1, wg1T, "nt", 0), (n1, wu1T, "nt", 1)], [BF, BF, BF], tm=1024, tn=1408, tk=D_MODEL,
        epilogue=_swiglu_fwd_epilogue, cols_outer=True,
        comm=[_gather_task(shards[2:3], forward_at=0.5), _gather_task(shards[3:4], natural=(0,), forward_at=0.9)])
    (h1, u), ((w_poT, w_ao, w_o),) = _mm(
        "ffn1_down", [(act1, wd1, "nn", 0)], [F32, BF], tm=512, tn=D_MODEL, tk=D_FF,
        epilogue=_residual_norm_epilogue(0.5), extras=[(x2, "tile", 0), (g2, "row", 0)],
        comm=[_gather_task(shards[4:7], natural=(0, 1, 2), forward_at=0.8)])
    saved1 = (n1, gt1, up1, act1)
    (proj,), ((wg2T,),) = _mm(
        "in_proj", [(u, w_inT, "nt", 0)], [BF], tm=1024, tn=1280, tk=D_MODEL, cols_outer=True,
        comm=[_gather_task(shards[7:8], forward_at=0.8)])
    pooled, mixed = _pool_fwd("pool_fwd", proj, pool_w, scale_row)
    qn = _headnorm_fwd("q_norm", proj, COL_Q, ATTN_WIDTH, qg)
    kn = _headnorm_fwd("k_norm", proj, COL_K, KV_WIDTH, kg)
    attn, ((wu2T,),) = _attn_fwd("attn_fwd", qn, kn, proj, sinks, comm=[_gather_task(shards[8:9], forward_at=0.8)])
    (bp,) = _mm("pool_out", [(mixed, w_poT, "nt", 0)], [BF], tm=1024, tn=D_MODEL, tk=POOL_WIDTH)
    gate_tn = 256
    gate_extras = [(proj, "tile", COL_GP // gate_tn), (proj, "tile", COL_GA // gate_tn),
                   (bias_row, "row", 0), (bias_row, "row", D_MODEL // gate_tn)]
    merged, ba = _mm("attn_out_merge", [(attn, w_ao, "nn", 0)], [BF, BF], tm=2048, tn=gate_tn, tk=ATTN_WIDTH,
                     epilogue=_merge_fwd_epilogue, extras=[(bp, "tile", 0)] + gate_extras)
    h2, n2 = _mm("mix_out", [(merged, w_o, "nn", 0)], [F32, BF], tm=1024, tn=D_MODEL, tk=D_MODEL,
                 epilogue=_residual_norm_epilogue(1.0), extras=[(h1, "tile", 0), (g3, "row", 0)])
    (gt2, up2, act2), ((wd2,),) = _mm(
        "ffn2_gate_up", [(n2, wg2T, "nt", 0), (n2, wu2T, "nt", 1)], [BF, BF, BF], tm=1024, tn=1408, tk=D_MODEL,
        epilogue=_swiglu_fwd_epilogue, cols_outer=True, comm=[_gather_task(shards[9:10], forward_at=0.8)])
    dy, dyb, sq = _mm("ffn2_down_loss", [(act2, wd2, "nn", 0)], [F32, BF], tm=512, tn=D_MODEL, tk=D_FF,
                      epilogue=_loss_epilogue, extras=[(h2, "tile", 0), (target, "tile", 0)], n_colsum=1)
    loss_local = 0.5 * jnp.sum(sq) / D_MODEL

    dh2, dh2b, dg3, _, slots_g2, slots_u2, slots_d2 = _ffn_bwd(
        "ffn2", dy, dyb, h2, g3, wg2T, wu2T, wd2, (n2, gt2, up2, act2))
    dbp, dba, dgp, dga, cs_gp, cs_ga = _mm(
        "mix_out_bwd", [(dh2b, w_o, "nt", 0)], [BF, BF, BF, BF], tm=2048, tn=gate_tn, tk=D_MODEL,
        epilogue=_merge_bwd_epilogue, extras=[(bp, "tile", 0), (ba, "tile", 0)] + gate_extras, n_colsum=2)
    sum_o = _dw_pair("dw_out", merged, dh2b, 1.0, blocks=4)
    (dmixed,) = _mm("pool_out_bwd", [(dbp, w_poT, "nn", 0)], [BF], tm=1024, tn=POOL_WIDTH, tk=D_MODEL)
    sum_po = _dw_pair("dw_pool_out", dbp, mixed, 1.0, blocks=4)
    (dattn,) = _mm("attn_out_bwd", [(dba, w_ao, "nt", 0)], [BF], tm=1024, tn=ATTN_WIDTH, tk=D_MODEL)
    sum_ao = _dw_pair("dw_attn_out", attn, dba, 1.0, blocks=4)
    (dqn, dkn, dv, dsink_tile), ((slots_o, slots_po, slots_ao),) = _attn_bwd(
        "attn_bwd", dattn, qn, kn, proj, sinks, [_chip_task([sum_o, sum_po, sum_ao])])
    dproj, dqg = _headnorm_bwd("q_norm_bwd", dqn, proj, COL_Q, ATTN_WIDTH, qg)
    dproj, dkg = _headnorm_bwd("k_norm_bwd", dkn, proj, COL_K, KV_WIDTH, kg, into=dproj)
    dproj, dpool_w, dpool_scale = _pool_bwd("pool_bwd", dmixed, pooled, pool_w, scale_row, dproj)
    for piece, col in ((dv, COL_V), (dgp, COL_GP), (dga, COL_GA)):
        dproj = lax.dynamic_update_slice(dproj, piece, (0, col))
    (dh1, dh1b, dg2), ((g_pool_w,),) = _mm(
        "in_proj_bwd", [(dproj, w_inT, "nn", 0)], [F32, BF], tm=512, tn=D_MODEL, tk=IN_WIDTH, epilogue=_rms_bwd_epilogue,
        extras=[(h1, "tile", 0), (g2, "row", 0), (dh2, "tile", 0)], n_colsum=1,
        comm=[_gather_task([dpool_w.reshape(-1, LANES)])])
    (dw_inT,) = _mm("dw_in", [(dproj, u, "tn", 0)], [BF], tm=1280, tn=D_MODEL, tk=2048)
    dx, _, dg1, slots_in, slots_g1, slots_u1, slots_d1 = _ffn_bwd(
        "ffn1", dh1, dh1b, x2, g1, wg1T, wu1T, wd1, saved1, dw_inT.reshape(4, 2, IN_WIDTH // N_DEV, D_MODEL))

    slots = [slots_g1, slots_u1, slots_d1, slots_in, slots_po, slots_ao, slots_o, slots_g2, slots_u2, slots_d2]
    big_out = {}
    for label, group in (("ffn", (0, 1, 2, 7, 8, 9)), ("w_in", (3,)), ("w_pool_out", (4,)), ("attn_out_and_out", (5, 6))):
        items = [(slots[k], view(big[k][1], big[k][4]), view(big[k][2], big[k][4]), view(big[k][3], big[k][4]))
                 for k in group]
        for k, res in zip(group, _adamw_sharded("adamw_" + label, items, transpose=big[group[0]][5])):
            big_out[big[k][0]] = tuple(view(r, big[k][4]) for r in res)

    small_grads = {
        "ffn1_norm": jnp.sum(dg1, axis=(0, 1)), "mix_norm": jnp.sum(dg2, axis=(0, 1)), "ffn2_norm": jnp.sum(dg3, axis=(0, 1)),
        "gate_bias": jnp.concatenate([jnp.sum(cs_gp, axis=(0, 1)), jnp.sum(cs_ga, axis=(0, 1))]),
        "pool_scale": dpool_scale, "q_norm": _fold_heads(dqg) * ATTN_SCALE, "k_norm": _fold_heads(dkg),
        "sinks": dsink_tile[0, :N_HEADS]}
    ((g_vec,),) = _comm_only("gather_small_grads", [_direct_gather_task([_pack_small_grads(small_grads, loss_local)])])
    given = {"ffn1_norm": (ffn1_norm, m_ffn1_norm, v_ffn1_norm), "mix_norm": (mix_norm, m_mix_norm, v_mix_norm),
             "ffn2_norm": (ffn2_norm, m_ffn2_norm, v_ffn2_norm), "gate_bias": (gate_bias, m_gate_bias, v_gate_bias),
             "pool_scale": (pool_scale, m_pool_scale, v_pool_scale), "q_norm": (q_norm, m_q_norm, v_q_norm),
             "k_norm": (k_norm, m_k_norm, v_k_norm), "sinks": (sinks, m_sinks, v_sinks)}
    params = [tuple(a.reshape(shape) for a in given[nm]) for nm, _, shape in SMALL_LAYOUT]
    params.append(tuple(a.reshape(-1, LANES) for a in (pool_w, m_pool_w, v_pool_w)))
    small_res, loss_row = _adamw_small("adamw_small", g_vec.reshape(N_DEV, SMALL_ROWS, LANES),
                                       g_pool_w.reshape(N_DEV, -1, LANES), params)
    small_out = {nm: tuple(r.reshape(given[nm][0].shape) for r in res)
                 for (nm, _, _), res in zip(SMALL_LAYOUT, small_res)}
    small_out["pool_w"] = tuple(r.reshape(pool_w.shape) for r in small_res[-1])
    loss = loss_row[0, 0]

    order = ["ffn1_norm", "ffn1_w_gate", "ffn1_w_up", "ffn1_w_down", "mix_norm", "w_in", "pool_w", "pool_scale",
             "w_pool_out", "q_norm", "k_norm", "sinks", "w_attn_out", "gate_bias", "w_out", "ffn2_norm",
             "ffn2_w_gate", "ffn2_w_up", "ffn2_w_down"]
    every = {**big_out, **small_out}
    outs = [loss, dx.reshape(x.shape)]
    for j in range(4):
        outs += [every[nm][j] for nm in order]
    return tuple(outs)
```

```python
import functools

import jax
import jax.numpy as jnp
from jax import lax
from jax.experimental import pallas as pl
from jax.experimental.pallas import tpu as pltpu

BF = jnp.bfloat16
F32 = jnp.float32

D_MODEL = 1024
D_FF = 2816
POOL_WIDTH = 512
POOL_GROUP = 128
N_POOL_GROUPS = 4
HEAD_DIM = 64
N_HEADS = 16
GQA_GROUP = 8
BLOCK = 128
ATTN_WIDTH = 1024
KV_WIDTH = 128
IN_WIDTH = 3840
RMS_EPS = 1e-6
N_DEV = 8
LANES = 128

COL_Q = POOL_WIDTH
COL_K = COL_Q + ATTN_WIDTH
COL_V = COL_K + KV_WIDTH
COL_GP = COL_V + KV_WIDTH
COL_GA = COL_GP + D_MODEL

ADAM_LR = 0.001
ADAM_B1 = 0.9
ADAM_B2 = 0.999
ADAM_EPS = 1e-08
ADAM_WD = 0.01
ADAM_STEP = 10

VMEM_LIMIT_V7X = 56 * 1024 * 1024
MESH = pl.DeviceIdType.MESH
ANY = pl.BlockSpec(memory_space=pl.ANY)


def _params(sem=None, collective_id=None):
    return pltpu.CompilerParams(dimension_semantics=sem, vmem_limit_bytes=VMEM_LIMIT_V7X, collective_id=collective_id)


COLLECTIVE_IDS = {frozenset(["sibling"]): 0, frozenset(["chips"]): 1, frozenset(["sibling", "chips"]): 2}


def _handshake(peer_kinds):
    x, y, c, chips = _place()
    peers = ([(x, y, 1 - c)] if "sibling" in peer_kinds else []) + ([(*chip, c) for chip in chips] if "chips" in peer_kinds else [])
    barrier = pltpu.get_barrier_semaphore()
    for peer in peers:
        pl.semaphore_signal(barrier, inc=1, device_id=peer, device_id_type=MESH)
    pl.semaphore_wait(barrier, len(peers))


_DIMS = {"nt": (((1,), (1,)), ((), ())), "nn": (((1,), (0,)), ((), ())), "tn": (((0,), (0,)), ((), ()))}


class _Task:
    def __init__(self, inputs, out_shapes, scratch, phases, peers):
        self.inputs, self.out_shapes, self.scratch = list(inputs), list(out_shapes), list(scratch)
        self.phases = list(phases)
        self.peers = frozenset(peers)


class _CommPlumbing:
    def __init__(self, tasks):
        self.tasks = list(tasks or [])
        self.args = [a for t in self.tasks for a in t.inputs]
        self.out_shapes = [o for t in self.tasks for o in t.out_shapes]
        self.scratch = [s for t in self.tasks for s in t.scratch]
        self.n_in, self.n_out = len(self.args), len(self.out_shapes)

    def peer_kinds(self, own=()):
        kinds = frozenset(own).union(*[t.peers for t in self.tasks])
        return None if "all" in kinds or not kinds else kinds

    def collective_id(self, own=()):
        kinds = self.peer_kinds(own)
        return None if kinds is None else COLLECTIVE_IDS[kinds]

    def handshake(self, first, own=()):
        kinds = self.peer_kinds(own)
        if kinds is not None:
            pl.when(first)(functools.partial(_handshake, kinds))

    def _slices(self, c_in, c_out, c_scr):
        i = o = s = 0
        for t in self.tasks:
            yield t, c_in[i:i + len(t.inputs)], c_out[o:o + len(t.out_shapes)], c_scr[s:s + len(t.scratch)]
            i, o, s = i + len(t.inputs), o + len(t.out_shapes), s + len(t.scratch)

    def run(self, step, steps, before, c_in, c_out, c_scr):
        for t, ins, outs, scr in self._slices(c_in, c_out, c_scr):
            for frac, fn in t.phases:
                if step is None:
                    fn(ins, outs, scr)
                elif before == (frac == 0):
                    at = 0 if frac == 0 else max(0, min(steps, -(-int(round(frac * steps * 64)) // 64)) - 1)
                    pl.when(step == at)(functools.partial(fn, ins, outs, scr))

    def split_outputs(self, flat):
        res, o = [], 0
        for t in self.tasks:
            res.append(list(flat[o:o + len(t.out_shapes)]))
            o += len(t.out_shapes)
        return res


def _comm_only(name, tasks):
    plumb = _CommPlumbing(tasks)

    def body(*refs):
        c_in, c_out = refs[:plumb.n_in], refs[plumb.n_in: plumb.n_in + plumb.n_out]
        c_scr = refs[plumb.n_in + plumb.n_out:]
        plumb.run(None, 1, True, c_in, c_out, c_scr)

    res = pl.pallas_call(
        body, name=name, in_specs=[ANY] * plumb.n_in, out_specs=[ANY] * plumb.n_out, out_shape=plumb.out_shapes,
        scratch_shapes=plumb.scratch, compiler_params=pltpu.CompilerParams(has_side_effects=True),
    )(*plumb.args)
    return plumb.split_outputs(res)


def _mm(name, terms, out_dtypes, *, tm, tn, tk, epilogue=None, extras=(), n_colsum=0, comm=None, cols_outer=False,
        out_placement=None):
    a0, b0, mode0, _ = terms[0]
    if mode0 == "nt":
        (M, K), N = a0.shape, b0.shape[0]
    elif mode0 == "nn":
        (M, K), N = a0.shape, b0.shape[1]
    else:
        (K, M), N = a0.shape, b0.shape[1]
    tm, tn, tk = min(tm, M), min(tn, N), min(tk, K)
    assert M % tm == 0 and N % tn == 0 and K % tk == 0, (name, M, N, K, tm, tn, tk)
    nI, nJ, nK = M // tm, N // tn, K // tk
    n_terms = len(terms)
    n_acc = max(t[3] for t in terms) + 1
    n_ex = len(extras)
    n_out = len(out_dtypes)
    if epilogue is None:
        epilogue = lambda accs, ex: ([accs[0]], [])
    plumb = _CommPlumbing(comm)
    n_scr = n_acc if nK > 1 else 0
    grid = (nJ, nI, nK) if cols_outer else (nI, nJ, nK)

    def body(*refs):
        n_in = 2 * n_terms + n_ex
        ab = refs[: 2 * n_terms]
        ex_refs = refs[2 * n_terms: n_in]
        c_in = refs[n_in: n_in + plumb.n_in]
        o0 = n_in + plumb.n_in
        out_refs = refs[o0: o0 + n_out]
        cs_refs = refs[o0 + n_out: o0 + n_out + n_colsum]
        c_out = refs[o0 + n_out + n_colsum: o0 + n_out + n_colsum + plumb.n_out]
        s0 = o0 + n_out + n_colsum + plumb.n_out
        acc_refs = refs[s0: s0 + n_scr]
        c_scr = refs[s0 + n_scr:]
        steps = grid[0] * grid[1] * nK
        if comm:
            step = (pl.program_id(0) * grid[1] + pl.program_id(1)) * nK + pl.program_id(2)
            plumb.handshake(step == 0)
            plumb.run(step, steps, True, c_in, c_out, c_scr)

        def products():
            accs = [None] * n_acc
            for t, (_, _, mode, ai) in enumerate(terms):
                p = lax.dot_general(ab[2 * t][...], ab[2 * t + 1][...], _DIMS[mode], preferred_element_type=F32)
                accs[ai] = p if accs[ai] is None else accs[ai] + p
            return accs

        def finish(accs):
            outs, colsums = epilogue(accs, [r[...] for r in ex_refs])
            for r, o in zip(out_refs, outs):
                r[...] = o.astype(r.dtype)
            for r, cs in zip(cs_refs, colsums):
                r[...] = jnp.sum(cs, axis=0, keepdims=True).reshape(r.shape)

        if nK == 1:
            finish(products())
        else:
            k = pl.program_id(2)
            accs = products()

            @pl.when(k == 0)
            def _():
                for r, a in zip(acc_refs, accs):
                    r[...] = a

            @pl.when(k > 0)
            def _():
                for r, a in zip(acc_refs, accs):
                    r[...] += a

            @pl.when(k == nK - 1)
            def _():
                finish([r[...] for r in acc_refs])

        if comm:
            plumb.run(step, steps, False, c_in, c_out, c_scr)

    def spec(block, index, fixed=False):
        imap = (lambda q, p, k: index(p, q, k)) if cols_outer else index
        return pl.BlockSpec(block, imap, pipeline_mode=pl.Buffered(1)) if fixed else pl.BlockSpec(block, imap)

    in_specs, args = [], []
    for a, b, mode, _ in terms:
        if mode == "nt":
            in_specs += [spec((tm, tk), lambda i, j, k: (i, k), nI * nK == 1),
                         spec((tn, tk), lambda i, j, k: (j, k), nJ * nK == 1)]
        elif mode == "nn":
            in_specs += [spec((tm, tk), lambda i, j, k: (i, k), nI * nK == 1),
                         spec((tk, tn), lambda i, j, k: (k, j), nJ * nK == 1)]
        else:
            in_specs += [spec((tk, tm), lambda i, j, k: (k, i), nI * nK == 1),
                         spec((tk, tn), lambda i, j, k: (k, j), nJ * nK == 1)]
        args += [a, b]
    for arr, kind, off in extras:
        if kind == "tile":
            in_specs.append(spec((tm, tn), functools.partial(lambda i, j, k, off: (i, j + off), off=off)))
        else:
            in_specs.append(spec((1, tn), functools.partial(lambda i, j, k, off: (0, j + off), off=off)))
        args.append(arr)
    placed = dict(out_placement or {})
    out_shape = [jax.ShapeDtypeStruct((M, placed.get(o, (N, 0))[0]), dt) for o, dt in enumerate(out_dtypes)]
    out_specs = [spec((tm, tn), functools.partial(lambda i, j, k, off: (i, j + off), off=placed.get(o, (N, 0))[1] // tn))
                 for o in range(n_out)]
    out_shape += [jax.ShapeDtypeStruct((nI, 1, N), F32) for _ in range(n_colsum)]
    out_specs += [spec((1, 1, tn), lambda i, j, k: (i, 0, j)) for _ in range(n_colsum)]
    scratch = [pltpu.VMEM((tm, tn), F32) for _ in range(n_scr)]
    args += plumb.args
    in_specs += [ANY] * plumb.n_in
    out_shape += plumb.out_shapes
    out_specs += [ANY] * plumb.n_out
    sem = ("arbitrary",) * 3 if comm else ("parallel", "parallel", "arbitrary")
    res = pl.pallas_call(
        body, name=name, grid=grid, in_specs=in_specs, out_specs=out_specs, out_shape=out_shape,
        scratch_shapes=scratch + plumb.scratch, compiler_params=_params(sem, plumb.collective_id()),
    )(*args)
    n_own = n_out + n_colsum
    return (list(res[:n_own]), plumb.split_outputs(res[n_own:])) if comm is not None else res


ROW_TILE = 512


def _rms_fwd(name, x, g, comm, weights, transposes):
    T, D = x.shape
    steps = T // ROW_TILE
    plumb = _CommPlumbing(comm)
    nw = len(weights)

    def body(x_ref, g_ref, *rest):
        w_refs, c_in = rest[:nw], rest[nw: nw + plumb.n_in]
        o_ref, shard_refs = rest[nw + plumb.n_in], rest[nw + plumb.n_in + 1: 2 * nw + plumb.n_in + 1]
        c_out = rest[2 * nw + plumb.n_in + 1: 2 * nw + plumb.n_in + 1 + plumb.n_out]
        c_scr = rest[2 * nw + plumb.n_in + 1 + plumb.n_out:]
        plumb.handshake(pl.program_id(0) == 0)
        plumb.run(pl.program_id(0), steps, True, c_in, c_out, c_scr)

        @pl.when(pl.program_id(0) == 0)
        def _():
            for w_ref, s_ref, tr in zip(w_refs, shard_refs, transposes):
                v = w_ref[...]
                s_ref[...] = (v.T if tr else v).astype(BF)

        xv = x_ref[...]
        r = lax.rsqrt(jnp.mean(xv * xv, axis=-1, keepdims=True) + RMS_EPS)
        o_ref[...] = (xv * r * g_ref[...]).astype(BF)
        plumb.run(pl.program_id(0), steps, False, c_in, c_out, c_scr)

    row = pl.BlockSpec((ROW_TILE, D), lambda i: (i, 0))
    whole = lambda shape: pl.BlockSpec(shape, lambda i: (0, 0), pipeline_mode=pl.Buffered(1))
    shard_shapes = [w.shape[::-1] if tr else w.shape for w, tr in zip(weights, transposes)]
    res = pl.pallas_call(
        body, name=name, grid=(steps,),
        in_specs=[row, pl.BlockSpec((1, D), lambda i: (0, 0))] + [whole(w.shape) for w in weights] + [ANY] * plumb.n_in,
        out_specs=[row] + [whole(s) for s in shard_shapes] + [ANY] * plumb.n_out,
        out_shape=[jax.ShapeDtypeStruct((T, D), BF)] + [jax.ShapeDtypeStruct(s, BF) for s in shard_shapes] + plumb.out_shapes,
        scratch_shapes=plumb.scratch, compiler_params=_params(("arbitrary",), plumb.collective_id()),
    )(x, g, *weights, *plumb.args)
    return res[0], list(res[1: nw + 1]), plumb.split_outputs(res[nw + 1:])


HEADNORM_TILE = 1024


def _half_sum_matrix():
    r = lax.broadcasted_iota(jnp.int32, (LANES, LANES), 0) // HEAD_DIM
    c = lax.broadcasted_iota(jnp.int32, (LANES, LANES), 1) // HEAD_DIM
    return (r == c).astype(BF)


def _head_mean(v, ones_blockdiag):
    hi = v.astype(BF)
    lo = (v - hi.astype(F32)).astype(BF)
    s = jnp.dot(hi, ones_blockdiag, preferred_element_type=F32) + jnp.dot(lo, ones_blockdiag, preferred_element_type=F32)
    return s * (1.0 / HEAD_DIM)


def _headnorm_fwd(name, proj, col0, width, g2):
    T = proj.shape[0]
    wide = min(width, GROUP_WIDTH)
    nb, off = width // wide, col0 // wide

    def body(x_ref, g_ref, b_ref, o_ref):
        for s in range(wide // LANES):
            lanes = slice(LANES * s, LANES * (s + 1))
            xv = x_ref[:, lanes].astype(F32)
            r = lax.rsqrt(_head_mean(xv * xv, b_ref[...]) + RMS_EPS)
            o_ref[:, lanes] = (xv * r * g_ref[...]).astype(BF)

    return pl.pallas_call(
        body, name=name, grid=(T // HEADNORM_TILE, nb),
        in_specs=[pl.BlockSpec((HEADNORM_TILE, wide), lambda i, j: (i, j + off)),
                  pl.BlockSpec((1, LANES), lambda i, j: (0, 0)), pl.BlockSpec((LANES, LANES), lambda i, j: (0, 0))],
        out_specs=pl.BlockSpec((HEADNORM_TILE, wide), lambda i, j: (i, j)),
        out_shape=jax.ShapeDtypeStruct((T, width), BF), compiler_params=_params(("parallel", "parallel")),
    )(proj, g2, _half_sum_matrix())


def _headnorm_bwd(name, dy, proj, col0, width, g2, into):
    T = proj.shape[0]
    wide = min(width, GROUP_WIDTH)
    nb, off = width // wide, col0 // wide

    def body(dy_ref, x_ref, g_ref, b_ref, into_ref, dx_ref, dg_ref):
        for s in range(wide // LANES):
            lanes = slice(LANES * s, LANES * (s + 1))
            xv = x_ref[:, lanes].astype(F32)
            dyv = dy_ref[:, lanes].astype(F32)
            r = lax.rsqrt(_head_mean(xv * xv, b_ref[...]) + RMS_EPS)
            xhat = xv * r
            dxhat = dyv * g_ref[...]
            dx_ref[:, lanes] = (r * (dxhat - xhat * _head_mean(dxhat * xhat, b_ref[...]))).astype(BF)
            dg_ref[0, :, lanes] = jnp.sum(dyv * xhat, axis=0, keepdims=True)

    return pl.pallas_call(
        body, name=name, grid=(T // HEADNORM_TILE, nb),
        in_specs=[pl.BlockSpec((HEADNORM_TILE, wide), lambda i, j: (i, j)),
                  pl.BlockSpec((HEADNORM_TILE, wide), lambda i, j: (i, j + off)),
                  pl.BlockSpec((1, LANES), lambda i, j: (0, 0)), pl.BlockSpec((LANES, LANES), lambda i, j: (0, 0)), ANY],
        out_specs=[pl.BlockSpec((HEADNORM_TILE, wide), lambda i, j: (i, j + off)),
                   pl.BlockSpec((1, 1, wide), lambda i, j: (i, 0, j))],
        out_shape=[jax.ShapeDtypeStruct(into.shape, BF), jax.ShapeDtypeStruct((T // HEADNORM_TILE, 1, width), F32)],
        input_output_aliases={4: 0}, compiler_params=_params(("parallel", "parallel")),
    )(dy, proj, g2, _half_sum_matrix(), into)


def _shift_down(v, k, row):
    return jnp.where(row >= k, pltpu.roll(v, k, axis=0), 0.0)


def _shift_up(v, k, row, T):
    return jnp.where(row < T - k, pltpu.roll(v, T - k, axis=0), 0.0)


def _by_group(g, vals):
    out = vals[-1]
    for i in range(len(vals) - 2, -1, -1):
        out = jnp.where(g == i, vals[i], out)
    return out


def _pool_fwd(name, proj, pool_w, pool_scale):
    T = proj.shape[0]

    def body(x_ref, w_ref, s_ref, pooled_ref, mixed_ref):
        g = pl.program_id(0)
        xv = x_ref[...].astype(F32)
        row = lax.broadcasted_iota(jnp.int32, (T, 1), 0)
        s2 = xv + _shift_down(xv, 1, row)
        s4 = s2 + _shift_down(s2, 2, row)
        s8 = s4 + _shift_down(s4, 4, row)
        s16 = s8 + _shift_down(s8, 8, row)
        wsum = _by_group(g, [s2, s4, s8, s16])
        count = jnp.minimum(row + 1, 2 << g).astype(F32)
        pooled = (wsum / count - xv).astype(BF)
        pooled_ref[...] = pooled
        mixed = jnp.dot(pooled, w_ref[0].astype(BF), preferred_element_type=F32) * s_ref[...]
        mixed_ref[...] = mixed.astype(BF)

    col = pl.BlockSpec((T, POOL_GROUP), lambda g: (0, g))
    return pl.pallas_call(
        body, name=name, grid=(N_POOL_GROUPS,),
        in_specs=[col, pl.BlockSpec((1, POOL_GROUP, POOL_GROUP), lambda g: (g, 0, 0)),
                  pl.BlockSpec((1, POOL_GROUP), lambda g: (0, g))],
        out_specs=[col, col],
        out_shape=[jax.ShapeDtypeStruct((T, POOL_WIDTH), BF), jax.ShapeDtypeStruct((T, POOL_WIDTH), BF)],
        compiler_params=_params(("parallel",)),
    )(proj, pool_w, pool_scale)


def _pool_bwd(name, dmixed, pooled, pool_w, pool_scale, into):
    T = dmixed.shape[0]

    def body(dm_ref, p_ref, w_ref, s_ref, into_ref, dx_ref, dw_ref, ds_ref):
        g = pl.program_id(0)
        dm = dm_ref[...].astype(F32)
        pooled = p_ref[...]
        w = w_ref[0].astype(BF)
        pre = jnp.dot(pooled, w, preferred_element_type=F32)
        ds_ref[...] = jnp.sum(dm * pre, axis=0, keepdims=True)
        dms = (dm * s_ref[...]).astype(BF)
        dw_ref[0] = lax.dot_general(pooled, dms, _DIMS["tn"], preferred_element_type=F32)
        dpooled = lax.dot_general(dms, w, _DIMS["nt"], preferred_element_type=F32)
        row = lax.broadcasted_iota(jnp.int32, (T, 1), 0)
        count = jnp.minimum(row + 1, 2 << g).astype(F32)
        z = dpooled / count
        l2 = z + _shift_up(z, 1, row, T)
        l4 = l2 + _shift_up(l2, 2, row, T)
        l8 = l4 + _shift_up(l4, 4, row, T)
        l16 = l8 + _shift_up(l8, 8, row, T)
        dx_ref[...] = (_by_group(g, [l2, l4, l8, l16]) - dpooled).astype(BF)

    col = pl.BlockSpec((T, POOL_GROUP), lambda g: (0, g))
    wspec = pl.BlockSpec((1, POOL_GROUP, POOL_GROUP), lambda g: (g, 0, 0))
    sspec = pl.BlockSpec((1, POOL_GROUP), lambda g: (0, g))
    return pl.pallas_call(
        body, name=name, grid=(N_POOL_GROUPS,), in_specs=[col, col, wspec, sspec, ANY], out_specs=[col, wspec, sspec],
        out_shape=[jax.ShapeDtypeStruct(into.shape, BF),
                   jax.ShapeDtypeStruct((N_POOL_GROUPS, POOL_GROUP, POOL_GROUP), F32),
                   jax.ShapeDtypeStruct((1, POOL_WIDTH), F32)],
        input_output_aliases={4: 0}, compiler_params=_params(("parallel",)),
    )(dmixed, pooled, pool_w, pool_scale, into)


ATTN_SCALE = HEAD_DIM ** -0.5
MASKED = float(jnp.finfo(jnp.float32).min)
KV_COL_BLOCK_V = COL_V // LANES
GROUP_WIDTH = GQA_GROUP * HEAD_DIM


def _dup_head(v, j):
    half = lax.broadcasted_iota(jnp.int32, (1, LANES), 1) // HEAD_DIM
    return jnp.where(half == j, v, pltpu.roll(v, HEAD_DIM, axis=1))


def _stack_heads(v, low):
    pieces = []
    for p in range(GROUP_WIDTH // LANES):
        vp = v[:, LANES * p: LANES * (p + 1)]
        pieces.append(jnp.where(low, vp, jnp.zeros_like(vp)))
        pieces.append(jnp.where(low, jnp.zeros_like(vp), vp))
    return jnp.concatenate(pieces, axis=0)


def _unstack_transposed(t, low):
    pairs = []
    for p in range(GROUP_WIDTH // LANES):
        even = t[:, BLOCK * (2 * p): BLOCK * (2 * p + 1)].T
        odd = t[:, BLOCK * (2 * p + 1): BLOCK * (2 * p + 2)].T
        pairs.append(jnp.where(low, even, odd))
    return pairs


STACKED = GQA_GROUP * BLOCK


def _band_bias():
    key = lax.broadcasted_iota(jnp.int32, (2, 2 * BLOCK, STACKED), 1)
    qry = lax.broadcasted_iota(jnp.int32, (2, 2 * BLOCK, STACKED), 2) % BLOCK
    first = lax.broadcasted_iota(jnp.int32, (2, 2 * BLOCK, STACKED), 0) == 0
    valid = (key > qry) & (key <= qry + BLOCK) & (jnp.logical_not(first) | (key >= BLOCK))
    return jnp.where(valid, 0.0, MASKED).astype(F32)


BIAS_SPEC = pl.BlockSpec((1, 2 * BLOCK, STACKED), lambda n: (jnp.minimum(n, 1), 0, 0))


def _softmax_keys_on_sublanes(k2, q, bias, sink_ref, j):
    head_of_lane = lax.broadcasted_iota(jnp.int32, (1, STACKED), 1) // BLOCK
    sink = jnp.zeros((1, STACKED), F32)
    for h in range(GQA_GROUP):
        sink = jnp.where(head_of_lane == h, sink_ref[j * GQA_GROUP + h], sink)
    s = lax.dot_general(k2, q, _DIMS["nt"], preferred_element_type=F32) + bias
    m = jnp.maximum(jnp.max(s, axis=0, keepdims=True), sink)
    e = jnp.exp(s - m)
    e_sink = jnp.exp(sink - m)
    inv = 1.0 / (jnp.sum(e, axis=0, keepdims=True) + e_sink)
    return e * inv, e_sink * inv


def _attn_fwd(name, qn, kn, proj, sinks, comm=None):
    T = qn.shape[0]
    nb = T // BLOCK
    plumb = _CommPlumbing(comm)

    def body(sink_ref, bias_ref, q_ref, kp_ref, kc_ref, vp_ref, vc_ref, *rest):
        c_in, o_ref = rest[:plumb.n_in], rest[plumb.n_in]
        c_out, c_scr = rest[plumb.n_in + 1: plumb.n_in + 1 + plumb.n_out], rest[plumb.n_in + 1 + plumb.n_out:]
        n = pl.program_id(0)
        plumb.handshake(n == 0)
        plumb.run(n, nb, True, c_in, c_out, c_scr)
        low = lax.broadcasted_iota(jnp.int32, (1, LANES), 1) < HEAD_DIM
        kk = jnp.concatenate([kp_ref[...], kc_ref[...]], axis=0)
        vv = jnp.concatenate([vp_ref[...], vc_ref[...]], axis=0)
        for j in range(2):
            q = _stack_heads(q_ref[:, GROUP_WIDTH * j: GROUP_WIDTH * (j + 1)], low)
            p, _ = _softmax_keys_on_sublanes(_dup_head(kk, j), q, bias_ref[0], sink_ref, j)
            o_t = lax.dot_general(_dup_head(vv, j), p.astype(BF), _DIMS["tn"], preferred_element_type=F32)
            for pair, o in enumerate(_unstack_transposed(o_t, low)):
                lanes = slice(GROUP_WIDTH * j + LANES * pair, GROUP_WIDTH * j + LANES * (pair + 1))
                o_ref[:, lanes] = o.astype(BF)
        plumb.run(n, nb, False, c_in, c_out, c_scr)

    wide = pl.BlockSpec((BLOCK, ATTN_WIDTH), lambda n: (n, 0))
    res = pl.pallas_call(
        body, name=name, grid=(nb,),
        in_specs=[pl.BlockSpec(memory_space=pltpu.SMEM), BIAS_SPEC, wide,
                  pl.BlockSpec((BLOCK, LANES), lambda n: (jnp.maximum(n - 1, 0), 0)),
                  pl.BlockSpec((BLOCK, LANES), lambda n: (n, 0)),
                  pl.BlockSpec((BLOCK, LANES), lambda n: (jnp.maximum(n - 1, 0), KV_COL_BLOCK_V)),
                  pl.BlockSpec((BLOCK, LANES), lambda n: (n, KV_COL_BLOCK_V))] + [ANY] * plumb.n_in,
        out_specs=[wide] + [ANY] * plumb.n_out,
        out_shape=[jax.ShapeDtypeStruct((T, ATTN_WIDTH), BF)] + plumb.out_shapes, scratch_shapes=plumb.scratch,
        compiler_params=_params(("arbitrary",) if comm else ("parallel",), plumb.collective_id()),
    )(sinks, _band_bias(), qn, kn, kn, proj, proj, *plumb.args)
    return (res[0], plumb.split_outputs(res[1:])) if comm is not None else res[0]


def _attn_bwd(name, dout, qn, kn, proj, sinks, comm):
    T = qn.shape[0]
    nb = T // BLOCK
    plumb = _CommPlumbing(comm)

    def body(sink_ref, bias_ref, do_ref, q_ref, kp_ref, kc_ref, vp_ref, vc_ref, *rest):
        c_in, (dq_ref, dk_ref, dv_ref, dsink_ref) = rest[:plumb.n_in], rest[plumb.n_in: plumb.n_in + 4]
        c_out = rest[plumb.n_in + 4: plumb.n_in + 4 + plumb.n_out]
        carry_k, carry_v, tot_k, tot_v = rest[plumb.n_in + 4 + plumb.n_out: plumb.n_in + 8 + plumb.n_out]
        c_scr = rest[plumb.n_in + 8 + plumb.n_out:]
        n = pl.program_id(0)
        plumb.handshake(n == 0)
        plumb.run(n, nb + 1, True, c_in, c_out, c_scr)
        lane = lax.broadcasted_iota(jnp.int32, (1, LANES), 1)
        low = lane < HEAD_DIM

        @pl.when(n == 0)
        def _():
            carry_k[...] = jnp.zeros_like(carry_k)
            carry_v[...] = jnp.zeros_like(carry_v)
            dsink_ref[...] = jnp.zeros_like(dsink_ref)

        @pl.when(n == nb)
        def _():
            tot_k[...] = jnp.zeros_like(tot_k)
            tot_v[...] = jnp.zeros_like(tot_v)

        @pl.when(n < nb)
        def _():
            kk = jnp.concatenate([kp_ref[...], kc_ref[...]], axis=0)
            vv = jnp.concatenate([vp_ref[...], vc_ref[...]], axis=0)
            dk_tot = jnp.zeros((2 * BLOCK, LANES), F32)
            dv_tot = jnp.zeros((2 * BLOCK, LANES), F32)
            dsink = jnp.zeros((1, LANES), F32)
            for j in range(2):
                k2 = _dup_head(kk, j)
                v2 = _dup_head(vv, j)
                q = _stack_heads(q_ref[:, GROUP_WIDTH * j: GROUP_WIDTH * (j + 1)], low)
                do = _stack_heads(do_ref[:, GROUP_WIDTH * j: GROUP_WIDTH * (j + 1)], low)
                p, psink = _softmax_keys_on_sublanes(k2, q, bias_ref[0], sink_ref, j)
                dp =lax.dot_general(v2, do, _DIMS["nt"], preferred_element_type=F32)
                delta = jnp.sum(p * dp, axis=0, keepdims=True)
                ds = (p * (dp - delta)).astype(BF)
                dk2 = jnp.dot(ds, q, preferred_element_type=F32)
                dv2 = jnp.dot(p.astype(BF), do, preferred_element_type=F32)
                dq_t = lax.dot_general(k2, ds, _DIMS["tn"], preferred_element_type=F32)
                for pair, dq in enumerate(_unstack_transposed(dq_t, low)):
                    lanes = slice(GROUP_WIDTH * j + LANES * pair, GROUP_WIDTH * j + LANES * (pair + 1))
                    dq_ref[:, lanes] = dq.astype(BF)
                mine = low if j == 0 else jnp.logical_not(low)
                dk_tot = dk_tot + jnp.where(mine, dk2 + pltpu.roll(dk2, HEAD_DIM, axis=1), 0.0)
                dv_tot = dv_tot + jnp.where(mine, dv2 + pltpu.roll(dv2, HEAD_DIM, axis=1), 0.0)
                sink_term = psink * delta
                for h in range(GQA_GROUP):
                    val = -jnp.sum(sink_term[:, BLOCK * h: BLOCK * (h + 1)], axis=1, keepdims=True)
                    dsink = dsink + jnp.where(lane == j * GQA_GROUP + h, val, 0.0)
            tot_k[...] = dk_tot
            tot_v[...] = dv_tot
            dsink_ref[0:1, :] += dsink

        dk_ref[...] = (carry_k[...] + tot_k[0:BLOCK]).astype(BF)
        dv_ref[...] = (carry_v[...] + tot_v[0:BLOCK]).astype(BF)
        carry_k[...] = tot_k[BLOCK:]
        carry_v[...] = tot_v[BLOCK:]
        plumb.run(n, nb + 1, False, c_in, c_out, c_scr)

    cur = lambda n: (jnp.minimum(n, nb - 1), 0)
    prev = lambda n: (jnp.maximum(n - 1, 0), 0)
    wide = pl.BlockSpec((BLOCK, ATTN_WIDTH), cur)
    res = pl.pallas_call(
        body, name=name, grid=(nb + 1,),
        in_specs=[pl.BlockSpec(memory_space=pltpu.SMEM), BIAS_SPEC, wide, wide,
                  pl.BlockSpec((BLOCK, LANES), prev), pl.BlockSpec((BLOCK, LANES), cur),
                  pl.BlockSpec((BLOCK, LANES), lambda n: (jnp.maximum(n - 1, 0), KV_COL_BLOCK_V)),
                  pl.BlockSpec((BLOCK, LANES), lambda n: (jnp.minimum(n, nb - 1), KV_COL_BLOCK_V))] + [ANY] * plumb.n_in,
        out_specs=[wide, pl.BlockSpec((BLOCK, LANES), prev), pl.BlockSpec((BLOCK, LANES), prev),
                   pl.BlockSpec((8, LANES), lambda n: (0, 0))] + [ANY] * plumb.n_out,
        out_shape=[jax.ShapeDtypeStruct((T, ATTN_WIDTH), BF), jax.ShapeDtypeStruct((T, KV_WIDTH), BF),
                   jax.ShapeDtypeStruct((T, KV_WIDTH), BF), jax.ShapeDtypeStruct((8, LANES), F32)] + plumb.out_shapes,
        scratch_shapes=[pltpu.VMEM((BLOCK, LANES), F32), pltpu.VMEM((BLOCK, LANES), F32),
                        pltpu.VMEM((2 * BLOCK, LANES), F32), pltpu.VMEM((2 * BLOCK, LANES), F32)] + plumb.scratch,
        compiler_params=_params(("arbitrary",), plumb.collective_id()),
    )(sinks, _band_bias(), dout, qn, kn, kn, proj, proj, *plumb.args)
    return list(res[:4]), plumb.split_outputs(res[4:])


def _swiglu_fwd_epilogue(accs, ex):
    g, u = accs
    return [g, u, g * jax.nn.sigmoid(g) * u], []


def _swiglu_bwd_epilogue(accs, ex):
    (da,) = accs
    g, u = ex[0].astype(F32), ex[1].astype(F32)
    s = jax.nn.sigmoid(g)
    gs = g * s
    return [da * u * (s + gs - gs * s), da * gs], []


def _residual_norm_epilogue(scale):
    def epilogue(accs, ex):
        res, gain = ex
        h = res + scale * accs[0]
        r = lax.rsqrt(jnp.mean(h * h, axis=-1, keepdims=True) + RMS_EPS)
        return [h, h * r * gain], []
    return epilogue


def _rms_bwd_epilogue(accs, ex):
    (dn,) = accs
    xv, g, dres = ex
    r = lax.rsqrt(jnp.mean(xv * xv, axis=-1, keepdims=True) + RMS_EPS)
    xhat = xv * r
    dxhat = dn * g
    dx = dres + r * (dxhat - xhat * jnp.mean(dxhat * xhat, axis=-1, keepdims=True))
    return [dx, dx], [dn * xhat]


def _loss_epilogue(accs, ex):
    xv, target = ex
    d = xv + 0.5 * accs[0] - target
    dy = d * (1.0 / D_MODEL)
    return [dy, dy], [d * d]


def _merge_fwd_epilogue(accs, ex):
    (ba,) = accs
    bp, gp_pre, ga_pre, bias_p, bias_a = ex
    gp = jax.nn.sigmoid(gp_pre.astype(F32) + bias_p)
    ga = jax.nn.sigmoid(ga_pre.astype(F32) + bias_a)
    return [gp * bp.astype(F32) + ga * ba, ba], []


def _merge_bwd_epilogue(accs, ex):
    (dm,) = accs
    bp, ba, gp_pre, ga_pre, bias_p, bias_a = ex
    gp = jax.nn.sigmoid(gp_pre.astype(F32) + bias_p)
    ga = jax.nn.sigmoid(ga_pre.astype(F32) + bias_a)
    dbp, dba = dm * gp, dm * ga
    dgp = dbp * bp.astype(F32) * (1.0 - gp)
    dga = dba * ba.astype(F32) * (1.0 - ga)
    return [dbp, dba, dgp, dga], [dgp, dga]


def _prep(name, ws, transposes):
    n = len(ws)

    def body(*refs):
        for w_ref, o_ref, tr in zip(refs[:n], refs[n:], transposes):
            v = w_ref[...]
            o_ref[...] = (v.T if tr else v).astype(BF)

    shapes = [jax.ShapeDtypeStruct(w.shape[::-1] if tr else w.shape, BF) for w, tr in zip(ws, transposes)]
    return pl.pallas_call(body, name=name, out_shape=shapes, compiler_params=_params())(*ws)


def _adam_math(w, g, m, v):
    m = ADAM_B1 * m + (1.0 - ADAM_B1) * g
    v = ADAM_B2 * v + (1.0 - ADAM_B2) * jnp.square(g)
    m_hat = m / (1.0 - ADAM_B1 ** ADAM_STEP)
    v_hat = v / (1.0 - ADAM_B2 ** ADAM_STEP)
    delta = -ADAM_LR * (m_hat / (jnp.sqrt(v_hat) + ADAM_EPS) + ADAM_WD * w)
    return delta, m, v


def _adamw_sharded(name, items, transpose=False):
    n = len(items)

    def body(*refs):
        ins, outs = refs[:4 * n], refs[4 * n:]
        for k in range(n):
            s_ref, w_ref, m_ref, v_ref = ins[4 * k: 4 * k + 4]
            g = s_ref[0].astype(F32)
            for i in range(1, 4):
                g = g + s_ref[i].astype(F32)
            if transpose:
                g = g.T
            delta, mn, vn = _adam_math(w_ref[...], g, m_ref[...], v_ref[...])
            for o_ref, val in zip(outs[4 * k: 4 * k + 4], (g, delta, mn, vn)):
                o_ref[...] = val

    flat = [a for item in items for a in item]
    out_shape = [jax.ShapeDtypeStruct(item[1].shape, F32) for item in items for _ in range(4)]
    _, r, C = items[0][0].shape
    rows = r // 4
    if transpose or rows % 8:
        res = pl.pallas_call(body, name=name, out_shape=out_shape, compiler_params=_params())(*flat)
    else:
        tile = pl.BlockSpec((rows, C), lambda i: (i, 0))
        res = pl.pallas_call(
            body, name=name, grid=(4,), in_specs=[pl.BlockSpec((4, rows, C), lambda i: (0, i, 0)), tile, tile, tile] * n,
            out_specs=[tile] * (4 * n), out_shape=out_shape, compiler_params=_params(("parallel",)),
        )(*flat)
    return [tuple(res[4 * k: 4 * k + 4]) for k in range(n)]


SMALL_LAYOUT = (("ffn1_norm", 0, (8, LANES)), ("mix_norm", 8, (8, LANES)), ("ffn2_norm", 16, (8, LANES)),
                ("gate_bias", 24, (16, LANES)), ("pool_scale", 40, (4, LANES)), ("q_norm", 48, (1, HEAD_DIM)),
                ("k_norm", 56, (1, HEAD_DIM)), ("sinks", 64, (1, N_HEADS)))
LOSS_ROW = 72
SMALL_ROWS = 80


def _adamw_small(name, g_vec, g_pool_w, params):
    n = len(SMALL_LAYOUT) + 1

    def body(vec_ref, pw_ref, *refs):
        ins, outs = refs[:3 * n], refs[3 * n:]
        vec = vec_ref[0]
        pw = pw_ref[0]
        for i in range(1, N_DEV):
            vec = vec + vec_ref[i]
            pw = pw + pw_ref[i]
        grads = [vec[r0:r0 + shape[0], 0:shape[1]] for _, r0, shape in SMALL_LAYOUT] + [pw]
        for p, g in enumerate(grads):
            w_ref, m_ref, v_ref = ins[3 * p: 3 * p + 3]
            delta, mn, vn = _adam_math(w_ref[...], g, m_ref[...], v_ref[...])
            for o_ref, val in zip(outs[4 * p: 4 * p + 4], (g, delta, mn, vn)):
                o_ref[...] = val
        outs[4 * n][...] = vec[LOSS_ROW:LOSS_ROW + 1, :]

    flat = [a for wmv in params for a in wmv]
    out_shape = [jax.ShapeDtypeStruct(wmv[0].shape, F32) for wmv in params for _ in range(4)]
    out_shape.append(jax.ShapeDtypeStruct((1, LANES), F32))
    res = pl.pallas_call(body, name=name, out_shape=out_shape, compiler_params=_params())(g_vec, g_pool_w, *flat)
    return [tuple(res[4 * p: 4 * p + 4]) for p in range(n)], res[4 * n]


def _place():
    x, y, c = lax.axis_index("x"), lax.axis_index("y"), lax.axis_index("c")
    other_chips = [(1 - x, y), (x, 1 - y), (1 - x, 1 - y)]
    return x, y, c, other_chips


def _rows(ref, r, place, natural=False):
    px, py, pc = place
    b = 4 * px + 2 * py + pc if natural else 4 * pc + 2 * px + py
    return ref.at[pl.ds(pl.multiple_of(b * r, 8), r), :]


def _gather_task(shards, natural=(), forward_at=0.75):
    n = len(shards)
    rs = [s.shape[0] for s in shards]
    rows_of = lambda ref, k, place: _rows(ref, rs[k], place, k in natural)

    def copy(scr, outs, k, slot, block, to, src=None):
        rows = rows_of(outs[k], k, block)
        return pltpu.make_async_remote_copy(
            src_ref=rows if src is None else src, dst_ref=rows, send_sem=scr[0].at[7 * k + slot],
            recv_sem=scr[1].at[7 * k + slot], device_id=to, device_id_type=MESH)

    def first_sends(ins, outs, scr):
        x, y, c, chips = _place()
        me = (x, y, c)
        cps = [copy(scr, outs, k, 1 + j, me, (*chip, c), src=ins[k]) for j, chip in enumerate(chips) for k in range(n)]
        return cps + [copy(scr, outs, k, 0, me, (x, y, 1 - c), src=ins[k]) for k in range(n)]

    def passed_on(outs, scr):
        x, y, c, chips = _place()
        return [copy(scr, outs, k, 4 + j, (*chip, c), (x, y, 1 - c)) for j, chip in enumerate(chips) for k in range(n)]

    def local(ins, outs, scr):
        x, y, c, _ = _place()
        return [pltpu.make_async_copy(ins[k], rows_of(outs[k], k, (x, y, c)), scr[2].at[k]) for k in range(n)]

    def start(ins, outs, scr):
        for cp in local(ins, outs, scr) + first_sends(ins, outs, scr):
            cp.start()

    def forward(ins, outs, scr):
        x, y, c, chips = _place()
        for j, chip in enumerate(chips):
            for k in range(n):
                copy(scr, outs, k, 1 + j, (*chip, c), (x, y, c)).wait_recv()
                copy(scr, outs, k, 4 + j, (*chip, c), (x, y, 1 - c)).start()

    def finish(ins, outs, scr):
        x, y, c, chips = _place()
        for k in range(n):
            copy(scr, outs, k, 0, (x, y, 1 - c), (x, y, c)).wait_recv()
        for j, chip in enumerate(chips):
            for k in range(n):
                copy(scr, outs, k, 4 + j, (*chip, 1 - c), (x, y, c)).wait_recv()
        for cp in first_sends(ins, outs, scr) + passed_on(outs, scr):
            cp.wait_send()
        for cp in local(ins, outs, scr):
            cp.wait()

    out_shapes = [jax.ShapeDtypeStruct((N_DEV * s.shape[0], s.shape[1]), s.dtype) for s in shards]
    scratch = [pltpu.SemaphoreType.DMA((7 * n,)), pltpu.SemaphoreType.DMA((7 * n,)), pltpu.SemaphoreType.DMA((n,))]
    return _Task(shards, out_shapes, scratch, [(0, start), (forward_at, forward), (1.0, finish)], ("sibling", "chips"))


def _direct_gather_task(shards):
    n = len(shards)
    rs = [s.shape[0] for s in shards]

    def peers():
        x, y, c, _ = _place()
        flip = lambda v, bit: 1 - v if bit else v
        return (x, y, c), [(flip(x, (s >> 2) & 1), flip(y, (s >> 1) & 1), flip(c, s & 1)) for s in range(1, N_DEV)]

    def copies(ins, outs, scr):
        me, others = peers()
        local = [pltpu.make_async_copy(ins[k], _rows(outs[k], rs[k], me), scr[2].at[k]) for k in range(n)]
        sems = lambda k, s: dict(send_sem=scr[0].at[7 * k + s], recv_sem=scr[1].at[7 * k + s], device_id_type=MESH)
        sends = [pltpu.make_async_remote_copy(src_ref=ins[k], dst_ref=_rows(outs[k], rs[k], me), device_id=to, **sems(k, s))
                 for s, to in enumerate(others) for k in range(n)]
        recvs = [pltpu.make_async_remote_copy(src_ref=_rows(outs[k], rs[k], frm), dst_ref=_rows(outs[k], rs[k], frm),
                                              device_id=me, **sems(k, s))
                 for s, frm in enumerate(others) for k in range(n)]
        return local, sends, recvs

    def start(ins, outs, scr):
        local, sends, _ = copies(ins, outs, scr)
        for cp in local + sends:
            cp.start()

    def finish(ins, outs, scr):
        local, sends, recvs = copies(ins, outs, scr)
        for cp in recvs:
            cp.wait_recv()
        for cp in sends:
            cp.wait_send()
        for cp in local:
            cp.wait()

    out_shapes = [jax.ShapeDtypeStruct((N_DEV * s.shape[0], s.shape[1]), s.dtype) for s in shards]
    scratch = [pltpu.SemaphoreType.DMA((7 * n,)), pltpu.SemaphoreType.DMA((7 * n,)), pltpu.SemaphoreType.DMA((n,))]
    return _Task(shards, out_shapes, scratch, [(0, start), (1.0, finish)], ("all",))


def _chip_task(sums):
    n = len(sums)
    rs = [s.shape[0] // 4 for s in sums]

    def block(ref, k, chip_index):
        return ref.at[pl.ds(pl.multiple_of(chip_index * rs[k], 8), rs[k]), :]

    def copies(ins, outs, scr):
        send_sems, recv_sems, local_sems = scr
        x, y, c, chips = _place()
        here = 2 * x + y
        local = [pltpu.make_async_copy(block(ins[k], k, here), outs[k].at[here], local_sems.at[k]) for k in range(n)]
        remote = []
        for j, (px, py) in enumerate(chips):
            remote += [pltpu.make_async_remote_copy(
                src_ref=block(ins[k], k, 2 * px + py), dst_ref=outs[k].at[here],
                send_sem=send_sems.at[3 * k + j], recv_sem=recv_sems.at[3 * k + j],
                device_id=(px, py, c), device_id_type=MESH) for k in range(n)]
        return local, remote

    def start(ins, outs, scr):
        local, remote = copies(ins, outs, scr)
        for cp in local + remote:
            cp.start()

    def finish(ins, outs, scr):
        local, remote = copies(ins, outs, scr)
        for cp in remote:
            cp.wait()
        for cp in local:
            cp.wait()

    out_shapes = [jax.ShapeDtypeStruct((4, r, s.shape[1]), s.dtype) for r, s in zip(rs, sums)]
    scratch = [pltpu.SemaphoreType.DMA((3 * n,)), pltpu.SemaphoreType.DMA((3 * n,)), pltpu.SemaphoreType.DMA((n,))]
    return _Task(sums, out_shapes, scratch, [(0, start), (1.0, finish)], ("chips",))


def _dw_pair(name, a, b, scale, comm=None, blocks=1):
    T, M = a.shape
    N = b.shape[1]
    half = M // 2
    wide = half // blocks
    tk = min(2048, T)
    nK = T // tk
    plumb = _CommPlumbing(comm)

    def body(core_ref, *rest):
        a_refs, b_ref, rest = rest[:blocks], rest[blocks], rest[blocks + 1:]
        c_in = rest[:plumb.n_in]
        o_ref = rest[plumb.n_in]
        c_out = rest[plumb.n_in + 1: plumb.n_in + 1 + plumb.n_out]
        acc, stage, land, send_sem, recv_sem = rest[plumb.n_in + 1 + plumb.n_out: plumb.n_in + 6 + plumb.n_out]
        c_scr = rest[plumb.n_in + 6 + plumb.n_out:]
        i, k = pl.program_id(0), pl.program_id(1)
        x, y, c, _ = _place()
        push = pltpu.make_async_remote_copy(src_ref=stage, dst_ref=land, send_sem=send_sem, recv_sem=recv_sem,
                                            device_id=(x, y, 1 - c), device_id_type=MESH)
        plumb.handshake((i == 0) & (k == 0), own=("sibling",))
        if comm:
            plumb.run(i * nK + k, 2 * nK, True, c_in, c_out, c_scr)

        av = a_refs[0][...] if blocks == 1 else jnp.concatenate([r[...] for r in a_refs], axis=1)
        p = lax.dot_general(av, b_ref[...], _DIMS["tn"], preferred_element_type=F32)

        @pl.when(k == 0)
        def _():
            acc[...] = p

        @pl.when(k > 0)
        def _():
            acc[...] += p

        @pl.when((i == 0) & (k == nK - 1))
        def _():
            stage[...] = (scale * acc[...]).astype(BF)
            push.start()

        @pl.when((i == 1) & (k == nK - 1))
        def _():
            push.wait_recv()
            o_ref[...] = (scale * acc[...] + land[...].astype(F32)).astype(BF)
            push.wait_send()

        if comm:
            plumb.run(i * nK + k, 2 * nK, False, c_in, c_out, c_scr)

    grid_spec = pltpu.PrefetchScalarGridSpec(
        num_scalar_prefetch=1, grid=(2, nK),
        in_specs=[pl.BlockSpec((tk, wide), functools.partial(
            lambda i, k, core, j: (k, (2 * j if blocks > 1 else 0) + jnp.where(i == 0, 1 - core[0], core[0])), j=j))
            for j in range(blocks)] + [pl.BlockSpec((tk, N), lambda i, k, core: (k, 0))] + [ANY] * plumb.n_in,
        out_specs=[pl.BlockSpec((half, N), lambda i, k, core: (0, 0))] + [ANY] * plumb.n_out,
        scratch_shapes=[pltpu.VMEM((half, N), F32), pltpu.VMEM((half, N), BF), pltpu.VMEM((half, N), BF),
                        pltpu.SemaphoreType.DMA, pltpu.SemaphoreType.DMA] + plumb.scratch)
    core = lax.axis_index("c").astype(jnp.int32).reshape(1)
    res = pl.pallas_call(
        body, name=name, grid_spec=grid_spec,
        out_shape=[jax.ShapeDtypeStruct((half, N), BF)] + plumb.out_shapes,
        compiler_params=_params(("arbitrary", "arbitrary"), plumb.collective_id(own=("sibling",))),
    )(core, *([a] * blocks), b, *plumb.args)
    return (res[0], plumb.split_outputs(res[1:])) if comm else res[0]


def _pair_task(parts):
    n = len(parts)

    def copies(ins, outs, scr):
        x, y, c, _ = _place()
        return [pltpu.make_async_remote_copy(
            src_ref=ins[k].at[:, pl.ds(1 - c, 1)], dst_ref=outs[k], send_sem=scr[0].at[k], recv_sem=scr[1].at[k],
            device_id=(x, y, 1 - c), device_id_type=MESH) for k in range(n)]

    def start(ins, outs, scr):
        for cp in copies(ins, outs, scr):
            cp.start()

    def finish(ins, outs, scr):
        for cp in copies(ins, outs, scr):
            cp.wait()

    out_shapes = [jax.ShapeDtypeStruct((4, 1) + p.shape[2:], p.dtype) for p in parts]
    scratch = [pltpu.SemaphoreType.DMA((n,)), pltpu.SemaphoreType.DMA((n,))]
    return _Task(parts, out_shapes, scratch, [(0, start), (1.0, finish)], ("sibling",))


def _pair_sum(name, part, got, core):
    _, _, r, C = part.shape

    def body(core_ref, p_ref, g_ref, o_ref):
        o_ref[0] = (p_ref[0, 0].astype(F32) + g_ref[0, 0].astype(F32)).astype(o_ref.dtype)

    return pl.pallas_call(
        body, name=name,
        grid_spec=pltpu.PrefetchScalarGridSpec(
            num_scalar_prefetch=1, grid=(4,),
            in_specs=[pl.BlockSpec((1, 1, r, C), lambda i, core_ref: (i, core_ref[0], 0, 0)),
                      pl.BlockSpec((1, 1, r, C), lambda i, core_ref: (i, 0, 0, 0))],
            out_specs=pl.BlockSpec((1, r, C), lambda i, core_ref: (i, 0, 0))),
        out_shape=jax.ShapeDtypeStruct((4, r, C), part.dtype), compiler_params=_params(("parallel",)),
    )(core, part, got)


def _ffn_bwd(tag, dy, dyb, x, gain, wgT, wuT, wd, saved, earlier=None):
    n, g, u, a = saved
    half = lambda accs, ex: _swiglu_bwd_epilogue([0.5 * accs[0]], ex)
    act_args = dict(tm=1024, tn=1408, tk=D_MODEL, epilogue=half, extras=[(g, "tile", 0), (u, "tile", 0)], cols_outer=True)
    if earlier is None:
        sum_d = _dw_pair(tag + "_dw_down", a, dyb, 0.5)
        (dg, du), ((slots_d,),) = _mm(tag + "_d_act", [(dyb, wd, "nt", 0)], [BF, BF], comm=[_chip_task([sum_d])], **act_args)
        slots_e = None
        sum_g = _dw_pair(tag + "_dw_gate", dg, n, 1.0)
    else:
        sum_d, ((got,),) = _dw_pair(tag + "_dw_down", a, dyb, 0.5, comm=[_pair_task([earlier])])
        core = lax.axis_index("c").astype(jnp.int32).reshape(1)
        sum_e = _pair_sum(tag + "_pair_sum_earlier", earlier, got, core)
        sum_e = sum_e.reshape(4 * sum_e.shape[1], sum_e.shape[2])
        (dg, du), ((slots_e,),) = _mm(tag + "_d_act", [(dyb, wd, "nt", 0)], [BF, BF], comm=[_chip_task([sum_e])], **act_args)
        sum_g, ((slots_d,),) = _dw_pair(tag + "_dw_gate", dg, n, 1.0, comm=[_chip_task([sum_d])])
    sum_u, ((slots_g,),) = _dw_pair(tag + "_dw_up", du, n, 1.0, comm=[_chip_task([sum_g])])
    (dx, dxb, dgain), ((slots_u,),) = _mm(
        tag + "_d_norm", [(dg, wgT, "nn", 0), (du, wuT, "nn", 0)], [F32, BF], tm=512, tn=D_MODEL, tk=D_FF,
        epilogue=_rms_bwd_epilogue, extras=[(x, "tile", 0), (gain, "row", 0), (dy, "tile", 0)], n_colsum=1,
        comm=[_chip_task([sum_u])])
    return dx, dxb, dgain, slots_e, slots_g, slots_u, slots_d


def _tile_gain(g):
    return jnp.concatenate([g, g]).reshape(1, LANES)


def _fold_heads(partials):
    return jnp.sum(partials.reshape(-1, HEAD_DIM), axis=0)


def _pack_small_grads(grads, loss_local):
    pieces, row = [], 0
    for name, r0, _ in SMALL_LAYOUT + (("loss", LOSS_ROW, None),):
        v = (loss_local if name == "loss" else grads[name]).reshape(-1)
        rows = -(-v.size // LANES)
        block = jnp.pad(v, (0, rows * LANES - v.size)).reshape(rows, LANES)
        pieces += [jnp.zeros((r0 - row, LANES), F32)] * (r0 > row) + [block]
        row = r0 + rows
    pieces.append(jnp.zeros((SMALL_ROWS - row, LANES), F32))
    return jnp.concatenate(pieces, axis=0)


def kernel(x, ffn1_norm, ffn1_w_gate, ffn1_w_up, ffn1_w_down, mix_norm, w_in, pool_w, pool_scale, w_pool_out, q_norm, k_norm, sinks, w_attn_out, gate_bias, w_out, ffn2_norm, ffn2_w_gate, ffn2_w_up, ffn2_w_down, loss_target, m_ffn1_norm, m_ffn1_w_gate, m_ffn1_w_up, m_ffn1_w_down, m_mix_norm, m_w_in, m_pool_w, m_pool_scale, m_w_pool_out, m_q_norm, m_k_norm, m_sinks, m_w_attn_out, m_gate_bias, m_w_out, m_ffn2_norm, m_ffn2_w_gate, m_ffn2_w_up, m_ffn2_w_down, v_ffn1_norm, v_ffn1_w_gate, v_ffn1_w_up, v_ffn1_w_down, v_mix_norm, v_w_in, v_pool_w, v_pool_scale, v_w_pool_out, v_q_norm, v_k_norm, v_sinks, v_w_attn_out, v_gate_bias, v_w_out, v_ffn2_norm, v_ffn2_w_gate, v_ffn2_w_up, v_ffn2_w_down):
    T = x.shape[1]
    x2 = x.reshape(T, D_MODEL)
    target = loss_target.reshape(T, D_MODEL)

    big = [
        ("ffn1_w_gate", ffn1_w_gate, m_ffn1_w_gate, v_ffn1_w_gate, True, False),
        ("ffn1_w_up", ffn1_w_up, m_ffn1_w_up, v_ffn1_w_up, True, False),
        ("ffn1_w_down", ffn1_w_down, m_ffn1_w_down, v_ffn1_w_down, False, False),
        ("w_in", w_in, m_w_in, v_w_in, True, False),
        ("w_pool_out", w_pool_out, m_w_pool_out, v_w_pool_out, False, True),
        ("w_attn_out", w_attn_out, m_w_attn_out, v_w_attn_out, False, False),
        ("w_out", w_out, m_w_out, v_w_out, False, False),
        ("ffn2_w_gate", ffn2_w_gate, m_ffn2_w_gate, v_ffn2_w_gate, True, False),
        ("ffn2_w_up", ffn2_w_up, m_ffn2_w_up, v_ffn2_w_up, True, False),
        ("ffn2_w_down", ffn2_w_down, m_ffn2_w_down, v_ffn2_w_down, False, False),
    ]
    view = lambda a, tv: a.T if tv else a
    views = [view(w, tv) for _, w, _, _, tv, _ in big]
    in_kernel_t = [tk_ for *_, tk_ in big]
    first_shards = _prep("prep_ffn1_gate_up", views[0:2], in_kernel_t[0:2])
    g1 = ffn1_norm.reshape(1, D_MODEL)
    g2 = mix_norm.reshape(1, D_MODEL)
    g3 = ffn2_norm.reshape(1, D_MODEL)
    bias_row = gate_bias.reshape(1, 2 * D_MODEL)
    qg, kg = _tile_gain(q_norm) * ATTN_SCALE, _tile_gain(k_norm)
    scale_row = pool_scale.reshape(1, POOL_WIDTH)

    n1, later_shards, ((wg1T, wu1T),) = _rms_fwd(
        "ffn1_norm", x2, g1, [_gather_task(first_shards, forward_at=0.9)], views[2:], in_kernel_t[2:])
    shards = list(first_shards) + later_shards
    (gt1, up1, act1), ((wd1,), (w_inT,)) = _mm(
        "ffn1_gate_up", [(n1, wg1T, "nt", 0), (n1, wu1T, "nt", 1)], [BF, BF, BF], tm=1024, tn=1408, tk=D_MODEL,
        epilogue=_swiglu_fwd_epilogue, cols_outer=True,
        comm=[_gather_task(shards[2:3], forward_at=0.5), _gather_task(shards[3:4], natural=(0,), forward_at=0.9)])
    (h1, u), ((w_poT, w_ao, w_o),) = _mm(
        "ffn1_down", [(act1, wd1, "nn", 0)], [F32, BF], tm=512, tn=D_MODEL, tk=D_FF,
        epilogue=_residual_norm_epilogue(0.5), extras=[(x2, "tile", 0), (g2, "row", 0)],
        comm=[_gather_task(shards[4:7], natural=(0, 1, 2), forward_at=0.8)])
    saved1 = (n1, gt1, up1, act1)
    (proj,), ((wg2T,),) = _mm(
        "in_proj", [(u, w_inT, "nt", 0)], [BF], tm=1024, tn=1280, tk=D_MODEL, cols_outer=True,
        comm=[_gather_task(shards[7:8], forward_at=0.8)])
    pooled, mixed = _pool_fwd("pool_fwd", proj, pool_w, scale_row)
    qn = _headnorm_fwd("q_norm", proj, COL_Q, ATTN_WIDTH, qg)
    kn = _headnorm_fwd("k_norm", proj, COL_K, KV_WIDTH, kg)
    attn, ((wu2T,),) = _attn_fwd("attn_fwd", qn, kn, proj, sinks, comm=[_gather_task(shards[8:9], forward_at=0.8)])
    (bp,) = _mm("pool_out", [(mixed, w_poT, "nt", 0)], [BF], tm=1024, tn=D_MODEL, tk=POOL_WIDTH)
    gate_tn = 256
    gate_extras = [(proj, "tile", COL_GP // gate_tn), (proj, "tile", COL_GA // gate_tn),
                   (bias_row, "row", 0), (bias_row, "row", D_MODEL // gate_tn)]
    merged, ba = _mm("attn_out_merge", [(attn, w_ao, "nn", 0)], [BF, BF], tm=2048, tn=gate_tn, tk=ATTN_WIDTH,
                     epilogue=_merge_fwd_epilogue, extras=[(bp, "tile", 0)] + gate_extras)
    h2, n2 = _mm("mix_out", [(merged, w_o, "nn", 0)], [F32, BF], tm=1024, tn=D_MODEL, tk=D_MODEL,
                 epilogue=_residual_norm_epilogue(1.0), extras=[(h1, "tile", 0), (g3, "row", 0)])
    (gt2, up2, act2), ((wd2,),) = _mm(
        "ffn2_gate_up", [(n2, wg2T, "nt", 0), (n2, wu2T, "nt", 1)], [BF, BF, BF], tm=1024, tn=1408, tk=D_MODEL,
        epilogue=_swiglu_fwd_epilogue, cols_outer=True, comm=[_gather_task(shards[9:10], forward_at=0.8)])
    dy, dyb, sq = _mm("ffn2_down_loss", [(act2, wd2, "nn", 0)], [F32, BF], tm=512, tn=D_MODEL, tk=D_FF,
                      epilogue=_loss_epilogue, extras=[(h2, "tile", 0), (target, "tile", 0)], n_colsum=1)
    loss_local = 0.5 * jnp.sum(sq) / D_MODEL

    dh2, dh2b, dg3, _, slots_g2, slots_u2, slots_d2 = _ffn_bwd(
        "ffn2", dy, dyb, h2, g3, wg2T, wu2T, wd2, (n2, gt2, up2, act2))
    dbp, dba, dproj, dga, cs_gp, cs_ga = _mm(
        "mix_out_bwd", [(dh2b, w_o, "nt", 0)], [BF, BF, BF, BF], tm=2048, tn=gate_tn, tk=D_MODEL,
        epilogue=_merge_bwd_epilogue, extras=[(bp, "tile", 0), (ba, "tile", 0)] + gate_extras, n_colsum=2,
        out_placement={2: (IN_WIDTH, COL_GP)})
    sum_o = _dw_pair("dw_out", merged, dh2b, 1.0, blocks=4)
    (dmixed,) = _mm("pool_out_bwd", [(dbp, w_poT, "nn", 0)], [BF], tm=1024, tn=POOL_WIDTH, tk=D_MODEL)
    sum_po = _dw_pair("dw_pool_out", dbp, mixed, 1.0, blocks=4)
    (dattn,) = _mm("attn_out_bwd", [(dba, w_ao, "nt", 0)], [BF], tm=1024, tn=ATTN_WIDTH, tk=D_MODEL)
    sum_ao = _dw_pair("dw_attn_out", attn, dba, 1.0, blocks=4)
    (dqn, dkn, dv, dsink_tile), ((slots_o, slots_po, slots_ao),) = _attn_bwd(
        "attn_bwd", dattn, qn, kn, proj, sinks, [_chip_task([sum_o, sum_po, sum_ao])])
    dproj, dqg = _headnorm_bwd("q_norm_bwd", dqn, proj, COL_Q, ATTN_WIDTH, qg, dproj)
    dproj, dkg = _headnorm_bwd("k_norm_bwd", dkn, proj, COL_K, KV_WIDTH, kg, dproj)
    dproj, dpool_w, dpool_scale = _pool_bwd("pool_bwd", dmixed, pooled, pool_w, scale_row, dproj)
    for piece, col in ((dv, COL_V), (dga, COL_GA)):
        dproj = lax.dynamic_update_slice(dproj, piece, (0, col))
    (dh1, dh1b, dg2), ((g_pool_w,),) = _mm(
        "in_proj_bwd", [(dproj, w_inT, "nn", 0)], [F32, BF], tm=512, tn=D_MODEL, tk=IN_WIDTH, epilogue=_rms_bwd_epilogue,
        extras=[(h1, "tile", 0), (g2, "row", 0), (dh2, "tile", 0)], n_colsum=1,
        comm=[_gather_task([dpool_w.reshape(-1, LANES)])])
    (dw_inT,) = _mm("dw_in", [(dproj, u, "tn", 0)], [BF], tm=1280, tn=D_MODEL, tk=2048)
    dx, _, dg1, slots_in, slots_g1, slots_u1, slots_d1 = _ffn_bwd(
        "ffn1", dh1, dh1b, x2, g1, wg1T, wu1T, wd1, saved1, dw_inT.reshape(4, 2, IN_WIDTH // N_DEV, D_MODEL))

    slots = [slots_g1, slots_u1, slots_d1, slots_in, slots_po, slots_ao, slots_o, slots_g2, slots_u2, slots_d2]
    big_out = {}
    for label, group in (("ffn", (0, 1, 2, 7, 8, 9)), ("w_in", (3,)), ("w_pool_out", (4,)), ("attn_out_and_out", (5, 6))):
        items = [(slots[k], view(big[k][1], big[k][4]), view(big[k][2], big[k][4]), view(big[k][3], big[k][4]))
                 for k in group]
        for k, res in zip(group, _adamw_sharded("adamw_" + label, items, transpose=big[group[0]][5])):
            big_out[big[k][0]] = tuple(view(r, big[k][4]) for r in res)

    small_grads = {
        "ffn1_norm": jnp.sum(dg1, axis=(0, 1)), "mix_norm": jnp.sum(dg2, axis=(0, 1)), "ffn2_norm": jnp.sum(dg3, axis=(0, 1)),
        "gate_bias": jnp.concatenate([jnp.sum(cs_gp, axis=(0, 1)), jnp.sum(cs_ga, axis=(0, 1))]),
        "pool_scale": dpool_scale, "q_norm": _fold_heads(dqg) * ATTN_SCALE, "k_norm": _fold_heads(dkg),
        "sinks": dsink_tile[0, :N_HEADS]}
    ((g_vec,),) = _comm_only("gather_small_grads", [_direct_gather_task([_pack_small_grads(small_grads, loss_local)])])
    given = {"ffn1_norm": (ffn1_norm, m_ffn1_norm, v_ffn1_norm), "mix_norm": (mix_norm, m_mix_norm, v_mix_norm),
             "ffn2_norm": (ffn2_norm, m_ffn2_norm, v_ffn2_norm), "gate_bias": (gate_bias, m_gate_bias, v_gate_bias),
             "pool_scale": (pool_scale, m_pool_scale, v_pool_scale), "q_norm": (q_norm, m_q_norm, v_q_norm),
             "k_norm": (k_norm, m_k_norm, v_k_norm), "sinks": (sinks, m_sinks, v_sinks)}
    params = [tuple(a.reshape(shape) for a in given[nm]) for nm, _, shape in SMALL_LAYOUT]
    params.append(tuple(a.reshape(-1, LANES) for a in (pool_w, m_pool_w, v_pool_w)))
    small_res, loss_row = _adamw_small("adamw_small", g_vec.reshape(N_DEV, SMALL_ROWS, LANES),
                                       g_pool_w.reshape(N_DEV, -1, LANES), params)
    small_out = {nm: tuple(r.reshape(given[nm][0].shape) for r in res)
                 for (nm, _, _), res in zip(SMALL_LAYOUT, small_res)}
    small_out["pool_w"] = tuple(r.reshape(pool_w.shape) for r in small_res[-1])
    loss = loss_row[0, 0]

    order = ["ffn1_norm", "ffn1_w_gate", "ffn1_w_up", "ffn1_w_down", "mix_norm", "w_in", "pool_w", "pool_scale",
             "w_pool_out", "q_norm", "k_norm", "sinks", "w_attn_out", "gate_bias", "w_out", "ffn2_norm",
             "ffn2_w_gate", "ffn2_w_up", "ffn2_w_down"]
    every = {**big_out, **small_out}
    outs = [loss, dx.reshape(x.shape)]
    for j in range(4):
        outs += [every[nm][j] for nm in order]
    return tuple(outs)
```

```python
import functools

import jax
import jax.numpy as jnp
from jax import lax
from jax.experimental import pallas as pl
from jax.experimental.pallas import tpu as pltpu

BF = jnp.bfloat16
F32 = jnp.float32

D_MODEL = 1024
D_FF = 2816
POOL_WIDTH = 512
POOL_GROUP = 128
N_POOL_GROUPS = 4
HEAD_DIM = 64
N_HEADS = 16
GQA_GROUP = 8
BLOCK = 128
ATTN_WIDTH = 1024
KV_WIDTH = 128
IN_WIDTH = 3840
RMS_EPS = 1e-6
N_DEV = 8
LANES = 128

COL_Q = POOL_WIDTH
COL_K = COL_Q + ATTN_WIDTH
COL_V = COL_K + KV_WIDTH
COL_GP = COL_V + KV_WIDTH
COL_GA = COL_GP + D_MODEL

ADAM_LR = 0.001
ADAM_B1 = 0.9
ADAM_B2 = 0.999
ADAM_EPS = 1e-08
ADAM_WD = 0.01
ADAM_STEP = 10

VMEM_LIMIT_V7X = 56 * 1024 * 1024
MESH = pl.DeviceIdType.MESH
ANY = pl.BlockSpec(memory_space=pl.ANY)


def _params(sem=None, collective_id=None):
    return pltpu.CompilerParams(dimension_semantics=sem, vmem_limit_bytes=VMEM_LIMIT_V7X, collective_id=collective_id)


COLLECTIVE_IDS = {frozenset(["sibling"]): 0, frozenset(["chips"]): 1, frozenset(["sibling", "chips"]): 2}


def _handshake(peer_kinds):
    x, y, c, chips = _place()
    peers = ([(x, y, 1 - c)] if "sibling" in peer_kinds else []) + ([(*chip, c) for chip in chips] if "chips" in peer_kinds else [])
    barrier = pltpu.get_barrier_semaphore()
    for peer in peers:
        pl.semaphore_signal(barrier, inc=1, device_id=peer, device_id_type=MESH)
    pl.semaphore_wait(barrier, len(peers))


_DIMS = {"nt": (((1,), (1,)), ((), ())), "nn": (((1,), (0,)), ((), ())), "tn": (((0,), (0,)), ((), ()))}


class _Task:
    def __init__(self, inputs, out_shapes, scratch, phases, peers):
        self.inputs, self.out_shapes, self.scratch = list(inputs), list(out_shapes), list(scratch)
        self.phases = list(phases)
        self.peers = frozenset(peers)


class _CommPlumbing:
    def __init__(self, tasks):
        self.tasks = list(tasks or [])
        self.args = [a for t in self.tasks for a in t.inputs]
        self.out_shapes = [o for t in self.tasks for o in t.out_shapes]
        self.scratch = [s for t in self.tasks for s in t.scratch]
        self.n_in, self.n_out = len(self.args), len(self.out_shapes)

    def peer_kinds(self, own=()):
        kinds = frozenset(own).union(*[t.peers for t in self.tasks])
        return None if "all" in kinds or not kinds else kinds

    def collective_id(self, own=()):
        kinds = self.peer_kinds(own)
        return None if kinds is None else COLLECTIVE_IDS[kinds]

    def handshake(self, first, own=()):
        kinds = self.peer_kinds(own)
        if kinds is not None:
            pl.when(first)(functools.partial(_handshake, kinds))

    def _slices(self, c_in, c_out, c_scr):
        i = o = s = 0
        for t in self.tasks:
            yield t, c_in[i:i + len(t.inputs)], c_out[o:o + len(t.out_shapes)], c_scr[s:s + len(t.scratch)]
            i, o, s = i + len(t.inputs), o + len(t.out_shapes), s + len(t.scratch)

    def run(self, step, steps, before, c_in, c_out, c_scr):
        for t, ins, outs, scr in self._slices(c_in, c_out, c_scr):
            for frac, fn in t.phases:
                if step is None:
                    fn(ins, outs, scr)
                elif before == (frac == 0):
                    at = 0 if frac == 0 else max(0, min(steps, -(-int(round(frac * steps * 64)) // 64)) - 1)
                    pl.when(step == at)(functools.partial(fn, ins, outs, scr))

    def split_outputs(self, flat):
        res, o = [], 0
        for t in self.tasks:
            res.append(list(flat[o:o + len(t.out_shapes)]))
            o += len(t.out_shapes)
        return res


def _comm_only(name, tasks):
    plumb = _CommPlumbing(tasks)

    def body(*refs):
        c_in, c_out = refs[:plumb.n_in], refs[plumb.n_in: plumb.n_in + plumb.n_out]
        c_scr = refs[plumb.n_in + plumb.n_out:]
        plumb.run(None, 1, True, c_in, c_out, c_scr)

    res = pl.pallas_call(
        body, name=name, in_specs=[ANY] * plumb.n_in, out_specs=[ANY] * plumb.n_out, out_shape=plumb.out_shapes,
        scratch_shapes=plumb.scratch, compiler_params=pltpu.CompilerParams(has_side_effects=True),
    )(*plumb.args)
    return plumb.split_outputs(res)


def _mm(name, terms, out_dtypes, *, tm, tn, tk, epilogue=None, extras=(), n_colsum=0, comm=None, cols_outer=False,
        out_placement=None):
    a0, b0, mode0, _ = terms[0]
    if mode0 == "nt":
        (M, K), N = a0.shape, b0.shape[0]
    elif mode0 == "nn":
        (M, K), N = a0.shape, b0.shape[1]
    else:
        (K, M), N = a0.shape, b0.shape[1]
    tm, tn, tk = min(tm, M), min(tn, N), min(tk, K)
    assert M % tm == 0 and N % tn == 0 and K % tk == 0, (name, M, N, K, tm, tn, tk)
    nI, nJ, nK = M // tm, N // tn, K // tk
    n_terms = len(terms)
    n_acc = max(t[3] for t in terms) + 1
    n_ex = len(extras)
    n_out = len(out_dtypes)
    if epilogue is None:
        epilogue = lambda accs, ex: ([accs[0]], [])
    plumb = _CommPlumbing(comm)
    n_scr = n_acc if nK > 1 else 0
    grid = (nJ, nI, nK) if cols_outer else (nI, nJ, nK)

    def body(*refs):
        n_in = 2 * n_terms + n_ex
        ab = refs[: 2 * n_terms]
        ex_refs = refs[2 * n_terms: n_in]
        c_in = refs[n_in: n_in + plumb.n_in]
        o0 = n_in + plumb.n_in
        out_refs = refs[o0: o0 + n_out]
        cs_refs = refs[o0 + n_out: o0 + n_out + n_colsum]
        c_out = refs[o0 + n_out + n_colsum: o0 + n_out + n_colsum + plumb.n_out]
        s0 = o0 + n_out + n_colsum + plumb.n_out
        acc_refs = refs[s0: s0 + n_scr]
        c_scr = refs[s0 + n_scr:]
        steps = grid[0] * grid[1] * nK
        if comm:
            step = (pl.program_id(0) * grid[1] + pl.program_id(1)) * nK + pl.program_id(2)
            plumb.handshake(step == 0)
            plumb.run(step, steps, True, c_in, c_out, c_scr)

        def products():
            accs = [None] * n_acc
            for t, (_, _, mode, ai) in enumerate(terms):
                p = lax.dot_general(ab[2 * t][...], ab[2 * t + 1][...], _DIMS[mode], preferred_element_type=F32)
                accs[ai] = p if accs[ai] is None else accs[ai] + p
            return accs

        def finish(accs):
            outs, colsums = epilogue(accs, [r[...] for r in ex_refs])
            for r, o in zip(out_refs, outs):
                r[...] = o.astype(r.dtype)
            for r, cs in zip(cs_refs, colsums):
                r[...] = jnp.sum(cs, axis=0, keepdims=True).reshape(r.shape)

        if nK == 1:
            finish(products())
        else:
            k = pl.program_id(2)
            accs = products()

            @pl.when(k == 0)
            def _():
                for r, a in zip(acc_refs, accs):
                    r[...] = a

            @pl.when(k > 0)
            def _():
                for r, a in zip(acc_refs, accs):
                    r[...] += a

            @pl.when(k == nK - 1)
            def _():
                finish([r[...] for r in acc_refs])

        if comm:
            plumb.run(step, steps, False, c_in, c_out, c_scr)

    def spec(block, index, fixed=False):
        imap = (lambda q, p, k: index(p, q, k)) if cols_outer else index
        return pl.BlockSpec(block, imap, pipeline_mode=pl.Buffered(1)) if fixed else pl.BlockSpec(block, imap)

    in_specs, args = [], []
    for a, b, mode, _ in terms:
        if mode == "nt":
            in_specs += [spec((tm, tk), lambda i, j, k: (i, k), nI * nK == 1),
                         spec((tn, tk), lambda i, j, k: (j, k), nJ * nK == 1)]
        elif mode == "nn":
            in_specs += [spec((tm, tk), lambda i, j, k: (i, k), nI * nK == 1),
                         spec((tk, tn), lambda i, j, k: (k, j), nJ * nK == 1)]
        else:
            in_specs += [spec((tk, tm), lambda i, j, k: (k, i), nI * nK == 1),
                         spec((tk, tn), lambda i, j, k: (k, j), nJ * nK == 1)]
        args += [a, b]
    for arr, kind, off in extras:
        if kind == "tile":
            in_specs.append(spec((tm, tn), functools.partial(lambda i, j, k, off: (i, j + off), off=off)))
        else:
            in_specs.append(spec((1, tn), functools.partial(lambda i, j, k, off: (0, j + off), off=off)))
        args.append(arr)
    placed = dict(out_placement or {})
    out_shape = [jax.ShapeDtypeStruct((M, placed.get(o, (N, 0))[0]), dt) for o, dt in enumerate(out_dtypes)]
    out_specs = [spec((tm, tn), functools.partial(lambda i, j, k, off: (i, j + off), off=placed.get(o, (N, 0))[1] // tn))
                 for o in range(n_out)]
    out_shape += [jax.ShapeDtypeStruct((nI, 1, N), F32) for _ in range(n_colsum)]
    out_specs += [spec((1, 1, tn), lambda i, j, k: (i, 0, j)) for _ in range(n_colsum)]
    scratch = [pltpu.VMEM((tm, tn), F32) for _ in range(n_scr)]
    args += plumb.args
    in_specs += [ANY] * plumb.n_in
    out_shape += plumb.out_shapes
    out_specs += [ANY] * plumb.n_out
    sem = ("arbitrary",) * 3 if comm else ("parallel", "parallel", "arbitrary")
    res = pl.pallas_call(
        body, name=name, grid=grid, in_specs=in_specs, out_specs=out_specs, out_shape=out_shape,
        scratch_shapes=scratch + plumb.scratch, compiler_params=_params(sem, plumb.collective_id()),
    )(*args)
    n_own = n_out + n_colsum
    return (list(res[:n_own]), plumb.split_outputs(res[n_own:])) if comm is not None else res


ROW_TILE = 512


def _rms_fwd(name, x, g, comm, weights, transposes):
    T, D = x.shape
    steps = T // ROW_TILE
    plumb = _CommPlumbing(comm)
    nw = len(weights)

    def body(x_ref, g_ref, *rest):
        w_refs, c_in = rest[:nw], rest[nw: nw + plumb.n_in]
        o_ref, shard_refs = rest[nw + plumb.n_in], rest[nw + plumb.n_in + 1: 2 * nw + plumb.n_in + 1]
        c_out = rest[2 * nw + plumb.n_in + 1: 2 * nw + plumb.n_in + 1 + plumb.n_out]
        c_scr = rest[2 * nw + plumb.n_in + 1 + plumb.n_out:]
        plumb.handshake(pl.program_id(0) == 0)
        plumb.run(pl.program_id(0), steps, True, c_in, c_out, c_scr)

        @pl.when(pl.program_id(0) == 0)
        def _():
            for w_ref, s_ref, tr in zip(w_refs, shard_refs, transposes):
                v = w_ref[...]
                s_ref[...] = (v.T if tr else v).astype(BF)

        xv = x_ref[...]
        r = lax.rsqrt(jnp.mean(xv * xv, axis=-1, keepdims=True) + RMS_EPS)
        o_ref[...] = (xv * r * g_ref[...]).astype(BF)
        plumb.run(pl.program_id(0), steps, False, c_in, c_out, c_scr)

    row = pl.BlockSpec((ROW_TILE, D), lambda i: (i, 0))
    whole = lambda shape: pl.BlockSpec(shape, lambda i: (0, 0), pipeline_mode=pl.Buffered(1))
    shard_shapes = [w.shape[::-1] if tr else w.shape for w, tr in zip(weights, transposes)]
    res = pl.pallas_call(
        body, name=name, grid=(steps,),
        in_specs=[row, pl.BlockSpec((1, D), lambda i: (0, 0))] + [whole(w.shape) for w in weights] + [ANY] * plumb.n_in,
        out_specs=[row] + [whole(s) for s in shard_shapes] + [ANY] * plumb.n_out,
        out_shape=[jax.ShapeDtypeStruct((T, D), BF)] + [jax.ShapeDtypeStruct(s, BF) for s in shard_shapes] + plumb.out_shapes,
        scratch_shapes=plumb.scratch, compiler_params=_params(("arbitrary",), plumb.collective_id()),
    )(x, g, *weights, *plumb.args)
    return res[0], list(res[1: nw + 1]), plumb.split_outputs(res[nw + 1:])


HEADNORM_TILE = 1024


def _half_sum_matrix():
    r = lax.broadcasted_iota(jnp.int32, (LANES, LANES), 0) // HEAD_DIM
    c = lax.broadcasted_iota(jnp.int32, (LANES, LANES), 1) // HEAD_DIM
    return (r == c).astype(BF)


def _head_mean(v, ones_blockdiag):
    hi = v.astype(BF)
    lo = (v - hi.astype(F32)).astype(BF)
    s = jnp.dot(hi, ones_blockdiag, preferred_element_type=F32) + jnp.dot(lo, ones_blockdiag, preferred_element_type=F32)
    return s * (1.0 / HEAD_DIM)


def _headnorm_fwd(name, proj, col0, width, g2):
    T = proj.shape[0]
    wide = min(width, GROUP_WIDTH)
    nb, off = width // wide, col0 // wide

    def body(x_ref, g_ref, b_ref, o_ref):
        for s in range(wide // LANES):
            lanes = slice(LANES * s, LANES * (s + 1))
            xv = x_ref[:, lanes].astype(F32)
            r = lax.rsqrt(_head_mean(xv * xv, b_ref[...]) + RMS_EPS)
            o_ref[:, lanes] = (xv * r * g_ref[...]).astype(BF)

    return pl.pallas_call(
        body, name=name, grid=(T // HEADNORM_TILE, nb),
        in_specs=[pl.BlockSpec((HEADNORM_TILE, wide), lambda i, j: (i, j + off)),
                  pl.BlockSpec((1, LANES), lambda i, j: (0, 0)), pl.BlockSpec((LANES, LANES), lambda i, j: (0, 0))],
        out_specs=pl.BlockSpec((HEADNORM_TILE, wide), lambda i, j: (i, j)),
        out_shape=jax.ShapeDtypeStruct((T, width), BF), compiler_params=_params(("parallel", "parallel")),
    )(proj, g2, _half_sum_matrix())


def _headnorm_bwd(name, dy, proj, col0, width, g2, into):
    T = proj.shape[0]
    wide = min(width, GROUP_WIDTH)
    nb, off = width // wide, col0 // wide

    def body(dy_ref, x_ref, g_ref, b_ref, into_ref, dx_ref, dg_ref):
        for s in range(wide // LANES):
            lanes = slice(LANES * s, LANES * (s + 1))
            xv = x_ref[:, lanes].astype(F32)
            dyv = dy_ref[:, lanes].astype(F32)
            r = lax.rsqrt(_head_mean(xv * xv, b_ref[...]) + RMS_EPS)
            xhat = xv * r
            dxhat = dyv * g_ref[...]
            dx_ref[:, lanes] = (r * (dxhat - xhat * _head_mean(dxhat * xhat, b_ref[...]))).astype(BF)
            dg_ref[0, :, lanes] = jnp.sum(dyv * xhat, axis=0, keepdims=True)

    return pl.pallas_call(
        body, name=name, grid=(T // HEADNORM_TILE, nb),
        in_specs=[pl.BlockSpec((HEADNORM_TILE, wide), lambda i, j: (i, j)),
                  pl.BlockSpec((HEADNORM_TILE, wide), lambda i, j: (i, j + off)),
                  pl.BlockSpec((1, LANES), lambda i, j: (0, 0)), pl.BlockSpec((LANES, LANES), lambda i, j: (0, 0)), ANY],
        out_specs=[pl.BlockSpec((HEADNORM_TILE, wide), lambda i, j: (i, j + off)),
                   pl.BlockSpec((1, 1, wide), lambda i, j: (i, 0, j))],
        out_shape=[jax.ShapeDtypeStruct(into.shape, BF), jax.ShapeDtypeStruct((T // HEADNORM_TILE, 1, width), F32)],
        input_output_aliases={4: 0}, compiler_params=_params(("parallel", "parallel")),
    )(dy, proj, g2, _half_sum_matrix(), into)


def _shift_down(v, k, row):
    return jnp.where(row >= k, pltpu.roll(v, k, axis=0), 0.0)


def _shift_up(v, k, row, T):
    return jnp.where(row < T - k, pltpu.roll(v, T - k, axis=0), 0.0)


def _by_group(g, vals):
    out = vals[-1]
    for i in range(len(vals) - 2, -1, -1):
        out = jnp.where(g == i, vals[i], out)
    return out


def _pool_fwd(name, proj, pool_w, pool_scale):
    T = proj.shape[0]

    def body(x_ref, w_ref, s_ref, pooled_ref, mixed_ref):
        g = pl.program_id(0)
        xv = x_ref[...].astype(F32)
        row = lax.broadcasted_iota(jnp.int32, (T, 1), 0)
        s2 = xv + _shift_down(xv, 1, row)
        s4 = s2 + _shift_down(s2, 2, row)
        s8 = s4 + _shift_down(s4, 4, row)
        s16 = s8 + _shift_down(s8, 8, row)
        wsum = _by_group(g, [s2, s4, s8, s16])
        count = jnp.minimum(row + 1, 2 << g).astype(F32)
        pooled = (wsum / count - xv).astype(BF)
        pooled_ref[...] = pooled
        mixed = jnp.dot(pooled, w_ref[0].astype(BF), preferred_element_type=F32) * s_ref[...]
        mixed_ref[...] = mixed.astype(BF)

    col = pl.BlockSpec((T, POOL_GROUP), lambda g: (0, g))
    return pl.pallas_call(
        body, name=name, grid=(N_POOL_GROUPS,),
        in_specs=[col, pl.BlockSpec((1, POOL_GROUP, POOL_GROUP), lambda g: (g, 0, 0)),
                  pl.BlockSpec((1, POOL_GROUP), lambda g: (0, g))],
        out_specs=[col, col],
        out_shape=[jax.ShapeDtypeStruct((T, POOL_WIDTH), BF), jax.ShapeDtypeStruct((T, POOL_WIDTH), BF)],
        compiler_params=_params(("parallel",)),
    )(proj, pool_w, pool_scale)


def _pool_bwd(name, dmixed, pooled, pool_w, pool_scale, into):
    T = dmixed.shape[0]

    def body(dm_ref, p_ref, w_ref, s_ref, into_ref, dx_ref, dw_ref, ds_ref):
        g = pl.program_id(0)
        dm = dm_ref[...].astype(F32)
        pooled = p_ref[...]
        w = w_ref[0].astype(BF)
        pre = jnp.dot(pooled, w, preferred_element_type=F32)
        ds_ref[...] = jnp.sum(dm * pre, axis=0, keepdims=True)
        dms = (dm * s_ref[...]).astype(BF)
        dw_ref[0] = lax.dot_general(pooled, dms, _DIMS["tn"], preferred_element_type=F32)
        dpooled = lax.dot_general(dms, w, _DIMS["nt"], preferred_element_type=F32)
        row = lax.broadcasted_iota(jnp.int32, (T, 1), 0)
        count = jnp.minimum(row + 1, 2 << g).astype(F32)
        z = dpooled / count
        l2 = z + _shift_up(z, 1, row, T)
        l4 = l2 + _shift_up(l2, 2, row, T)
        l8 = l4 + _shift_up(l4, 4, row, T)
        l16 = l8 + _shift_up(l8, 8, row, T)
        dx_ref[...] = (_by_group(g, [l2, l4, l8, l16]) - dpooled).astype(BF)

    col = pl.BlockSpec((T, POOL_GROUP), lambda g: (0, g))
    wspec = pl.BlockSpec((1, POOL_GROUP, POOL_GROUP), lambda g: (g, 0, 0))
    sspec = pl.BlockSpec((1, POOL_GROUP), lambda g: (0, g))
    return pl.pallas_call(
        body, name=name, grid=(N_POOL_GROUPS,), in_specs=[col, col, wspec, sspec, ANY], out_specs=[col, wspec, sspec],
        out_shape=[jax.ShapeDtypeStruct(into.shape, BF),
                   jax.ShapeDtypeStruct((N_POOL_GROUPS, POOL_GROUP, POOL_GROUP), F32),
                   jax.ShapeDtypeStruct((1, POOL_WIDTH), F32)],
        input_output_aliases={4: 0}, compiler_params=_params(("parallel",)),
    )(dmixed, pooled, pool_w, pool_scale, into)


ATTN_SCALE = HEAD_DIM ** -0.5
MASKED = float(jnp.finfo(jnp.float32).min)
KV_COL_BLOCK_V = COL_V // LANES
GROUP_WIDTH = GQA_GROUP * HEAD_DIM


def _dup_head(v, j):
    half = lax.broadcasted_iota(jnp.int32, (1, LANES), 1) // HEAD_DIM
    return jnp.where(half == j, v, pltpu.roll(v, HEAD_DIM, axis=1))


def _stack_heads(v, low):
    pieces = []
    for p in range(GROUP_WIDTH // LANES):
        vp = v[:, LANES * p: LANES * (p + 1)]
        pieces.append(jnp.where(low, vp, jnp.zeros_like(vp)))
        pieces.append(jnp.where(low, jnp.zeros_like(vp), vp))
    return jnp.concatenate(pieces, axis=0)


def _unstack_transposed(t, low):
    pairs = []
    for p in range(GROUP_WIDTH // LANES):
        even = t[:, BLOCK * (2 * p): BLOCK * (2 * p + 1)].T
        odd = t[:, BLOCK * (2 * p + 1): BLOCK * (2 * p + 2)].T
        pairs.append(jnp.where(low, even, odd))
    return pairs


STACKED = GQA_GROUP * BLOCK


def _band_bias():
    key = lax.broadcasted_iota(jnp.int32, (2, 2 * BLOCK, STACKED), 1)
    qry = lax.broadcasted_iota(jnp.int32, (2, 2 * BLOCK, STACKED), 2) % BLOCK
    first = lax.broadcasted_iota(jnp.int32, (2, 2 * BLOCK, STACKED), 0) == 0
    valid = (key > qry) & (key <= qry + BLOCK) & (jnp.logical_not(first) | (key >= BLOCK))
    return jnp.where(valid, 0.0, MASKED).astype(F32)


BIAS_SPEC = pl.BlockSpec((1, 2 * BLOCK, STACKED), lambda n: (jnp.minimum(n, 1), 0, 0))


def _softmax_keys_on_sublanes(k2, q, bias, sink_ref, j):
    head_of_lane = lax.broadcasted_iota(jnp.int32, (1, STACKED), 1) // BLOCK
    sink = jnp.zeros((1, STACKED), F32)
    for h in range(GQA_GROUP):
        sink = jnp.where(head_of_lane == h, sink_ref[j * GQA_GROUP + h], sink)
    s = lax.dot_general(k2, q, _DIMS["nt"], preferred_element_type=F32) + bias
    m = jnp.maximum(jnp.max(s, axis=0, keepdims=True), sink)
    e = jnp.exp(s - m)
    e_sink = jnp.exp(sink - m)
    inv = 1.0 / (jnp.sum(e, axis=0, keepdims=True) + e_sink)
    return e * inv, e_sink * inv


def _attn_fwd(name, qn, kn, proj, sinks, comm=None):
    T = qn.shape[0]
    nb = T // BLOCK
    plumb = _CommPlumbing(comm)

    def body(sink_ref, bias_ref, q_ref, kp_ref, kc_ref, vp_ref, vc_ref, *rest):
        c_in, o_ref = rest[:plumb.n_in], rest[plumb.n_in]
        c_out, c_scr = rest[plumb.n_in + 1: plumb.n_in + 1 + plumb.n_out], rest[plumb.n_in + 1 + plumb.n_out:]
        m = pl.program_id(0)
        plumb.handshake(m == 0)
        plumb.run(m, nb // 2, True, c_in, c_out, c_scr)
        low = lax.broadcasted_iota(jnp.int32, (1, LANES), 1) < HEAD_DIM
        k_pair, v_pair = kc_ref[...], vc_ref[...]
        for b in range(2):
            rows = slice(BLOCK * b, BLOCK * (b + 1))
            kk = k_pair if b else jnp.concatenate([kp_ref[...], k_pair[0:BLOCK]], axis=0)
            vv = v_pair if b else jnp.concatenate([vp_ref[...], v_pair[0:BLOCK]], axis=0)
            bias = bias_ref[1] if b else bias_ref[jnp.minimum(m, 1)]
            for j in range(2):
                q = _stack_heads(q_ref[rows, GROUP_WIDTH * j: GROUP_WIDTH * (j + 1)], low)
                p, _ = _softmax_keys_on_sublanes(_dup_head(kk, j), q, bias, sink_ref, j)
                o_t = lax.dot_general(_dup_head(vv, j), p.astype(BF), _DIMS["tn"], preferred_element_type=F32)
                for pair, o in enumerate(_unstack_transposed(o_t, low)):
                    lanes = slice(GROUP_WIDTH * j + LANES * pair, GROUP_WIDTH * j + LANES * (pair + 1))
                    o_ref[rows, lanes] = o.astype(BF)
        plumb.run(m, nb // 2, False, c_in, c_out, c_scr)

    wide = pl.BlockSpec((2 * BLOCK, ATTN_WIDTH), lambda m: (m, 0))
    before = lambda m: jnp.maximum(2 * m - 1, 0)
    res = pl.pallas_call(
        body, name=name, grid=(nb // 2,),
        in_specs=[pl.BlockSpec(memory_space=pltpu.SMEM),
                  pl.BlockSpec((2, 2 * BLOCK, STACKED), lambda m: (0, 0, 0)), wide,
                  pl.BlockSpec((BLOCK, LANES), lambda m: (before(m), 0)),
                  pl.BlockSpec((2 * BLOCK, LANES), lambda m: (m, 0)),
                  pl.BlockSpec((BLOCK, LANES), lambda m: (before(m), KV_COL_BLOCK_V)),
                  pl.BlockSpec((2 * BLOCK, LANES), lambda m: (m, KV_COL_BLOCK_V))] + [ANY] * plumb.n_in,
        out_specs=[wide] + [ANY] * plumb.n_out,
        out_shape=[jax.ShapeDtypeStruct((T, ATTN_WIDTH), BF)] + plumb.out_shapes, scratch_shapes=plumb.scratch,
        compiler_params=_params(("arbitrary",) if comm else ("parallel",), plumb.collective_id()),
    )(sinks, _band_bias(), qn, kn, kn, proj, proj, *plumb.args)
    return (res[0], plumb.split_outputs(res[1:])) if comm is not None else res[0]


def _attn_bwd(name, dout, qn, kn, proj, sinks, comm):
    T = qn.shape[0]
    nb = T // BLOCK
    plumb = _CommPlumbing(comm)

    def body(sink_ref, bias_ref, do_ref, q_ref, kp_ref, kc_ref, vp_ref, vc_ref, *rest):
        c_in, (dq_ref, dk_ref, dv_ref, dsink_ref) = rest[:plumb.n_in], rest[plumb.n_in: plumb.n_in + 4]
        c_out = rest[plumb.n_in + 4: plumb.n_in + 4 + plumb.n_out]
        carry_k, carry_v, tot_k, tot_v = rest[plumb.n_in + 4 + plumb.n_out: plumb.n_in + 8 + plumb.n_out]
        c_scr = rest[plumb.n_in + 8 + plumb.n_out:]
        n = pl.program_id(0)
        plumb.handshake(n == 0)
        plumb.run(n, nb + 1, True, c_in, c_out, c_scr)
        lane = lax.broadcasted_iota(jnp.int32, (1, LANES), 1)
        low = lane < HEAD_DIM

        @pl.when(n == 0)
        def _():
            carry_k[...] = jnp.zeros_like(carry_k)
            carry_v[...] = jnp.zeros_like(carry_v)
            dsink_ref[...] = jnp.zeros_like(dsink_ref)

        @pl.when(n == nb)
        def _():
            tot_k[...] = jnp.zeros_like(tot_k)
            tot_v[...] = jnp.zeros_like(tot_v)

        @pl.when(n < nb)
        def _():
            kk = jnp.concatenate([kp_ref[...], kc_ref[...]], axis=0)
            vv = jnp.concatenate([vp_ref[...], vc_ref[...]], axis=0)
            dk_tot = jnp.zeros((2 * BLOCK, LANES), F32)
            dv_tot = jnp.zeros((2 * BLOCK, LANES), F32)
            dsink = jnp.zeros((1, LANES), F32)
            for j in range(2):
                k2 = _dup_head(kk, j)
                v2 = _dup_head(vv, j)
                q = _stack_heads(q_ref[:, GROUP_WIDTH * j: GROUP_WIDTH * (j + 1)], low)
                do = _stack_heads(do_ref[:, GROUP_WIDTH * j: GROUP_WIDTH * (j + 1)], low)
                p, psink = _softmax_keys_on_sublanes(k2, q, bias_ref[0], sink_ref, j)
                dp =lax.dot_general(v2, do, _DIMS["nt"], preferred_element_type=F32)
                delta = jnp.sum(p * dp, axis=0, keepdims=True)
                ds = (p * (dp - delta)).astype(BF)
                dk2 = jnp.dot(ds, q, preferred_element_type=F32)
                dv2 = jnp.dot(p.astype(BF), do, preferred_element_type=F32)
                dq_t = lax.dot_general(k2, ds, _DIMS["tn"], preferred_element_type=F32)
                for pair, dq in enumerate(_unstack_transposed(dq_t, low)):
                    lanes = slice(GROUP_WIDTH * j + LANES * pair, GROUP_WIDTH * j + LANES * (pair + 1))
                    dq_ref[:, lanes] = dq.astype(BF)
                mine = low if j == 0 else jnp.logical_not(low)
                dk_tot = dk_tot + jnp.where(mine, dk2 + pltpu.roll(dk2, HEAD_DIM, axis=1), 0.0)
                dv_tot = dv_tot + jnp.where(mine, dv2 + pltpu.roll(dv2, HEAD_DIM, axis=1), 0.0)
                sink_term = psink * delta
                for h in range(GQA_GROUP):
                    val = -jnp.sum(sink_term[:, BLOCK * h: BLOCK * (h + 1)], axis=1, keepdims=True)
                    dsink = dsink + jnp.where(lane == j * GQA_GROUP + h, val, 0.0)
            tot_k[...] = dk_tot
            tot_v[...] = dv_tot
            dsink_ref[0:1, :] += dsink

        dk_ref[...] = (carry_k[...] + tot_k[0:BLOCK]).astype(BF)
        dv_ref[...] = (carry_v[...] + tot_v[0:BLOCK]).astype(BF)
        carry_k[...] = tot_k[BLOCK:]
        carry_v[...] = tot_v[BLOCK:]
        plumb.run(n, nb + 1, False, c_in, c_out, c_scr)

    cur = lambda n: (jnp.minimum(n, nb - 1), 0)
    prev = lambda n: (jnp.maximum(n - 1, 0), 0)
    wide = pl.BlockSpec((BLOCK, ATTN_WIDTH), cur)
    res = pl.pallas_call(
        body, name=name, grid=(nb + 1,),
        in_specs=[pl.BlockSpec(memory_space=pltpu.SMEM), BIAS_SPEC, wide, wide,
                  pl.BlockSpec((BLOCK, LANES), prev), pl.BlockSpec((BLOCK, LANES), cur),
                  pl.BlockSpec((BLOCK, LANES), lambda n: (jnp.maximum(n - 1, 0), KV_COL_BLOCK_V)),
                  pl.BlockSpec((BLOCK, LANES), lambda n: (jnp.minimum(n, nb - 1), KV_COL_BLOCK_V))] + [ANY] * plumb.n_in,
        out_specs=[wide, pl.BlockSpec((BLOCK, LANES), prev), pl.BlockSpec((BLOCK, LANES), prev),
                   pl.BlockSpec((8, LANES), lambda n: (0, 0))] + [ANY] * plumb.n_out,
        out_shape=[jax.ShapeDtypeStruct((T, ATTN_WIDTH), BF), jax.ShapeDtypeStruct((T, KV_WIDTH), BF),
                   jax.ShapeDtypeStruct((T, KV_WIDTH), BF), jax.ShapeDtypeStruct((8, LANES), F32)] + plumb.out_shapes,
        scratch_shapes=[pltpu.VMEM((BLOCK, LANES), F32), pltpu.VMEM((BLOCK, LANES), F32),
                        pltpu.VMEM((2 * BLOCK, LANES), F32), pltpu.VMEM((2 * BLOCK, LANES), F32)] + plumb.scratch,
        compiler_params=_params(("arbitrary",), plumb.collective_id()),
    )(sinks, _band_bias(), dout, qn, kn, kn, proj, proj, *plumb.args)
    return list(res[:4]), plumb.split_outputs(res[4:])


def _swiglu_fwd_epilogue(accs, ex):
    g, u = accs
    return [g, u, g * jax.nn.sigmoid(g) * u], []


def _swiglu_bwd_epilogue(accs, ex):
    (da,) = accs
    g, u = ex[0].astype(F32), ex[1].astype(F32)
    s = jax.nn.sigmoid(g)
    gs = g * s
    return [da * u * (s + gs - gs * s), da * gs], []


def _residual_norm_epilogue(scale):
    def epilogue(accs, ex):
        res, gain = ex
        h = res + scale * accs[0]
        r = lax.rsqrt(jnp.mean(h * h, axis=-1, keepdims=True) + RMS_EPS)
        return [h, h * r * gain], []
    return epilogue


def _rms_bwd_epilogue(accs, ex):
    (dn,) = accs
    xv, g, dres = ex
    r = lax.rsqrt(jnp.mean(xv * xv, axis=-1, keepdims=True) + RMS_EPS)
    xhat = xv * r
    dxhat = dn * g
    dx = dres + r * (dxhat - xhat * jnp.mean(dxhat * xhat, axis=-1, keepdims=True))
    return [dx, dx], [dn * xhat]


def _loss_epilogue(accs, ex):
    xv, target = ex
    d = xv + 0.5 * accs[0] - target
    dy = d * (1.0 / D_MODEL)
    return [dy, dy], [d * d]


def _merge_fwd_epilogue(accs, ex):
    (ba,) = accs
    bp, gp_pre, ga_pre, bias_p, bias_a = ex
    gp = jax.nn.sigmoid(gp_pre.astype(F32) + bias_p)
    ga = jax.nn.sigmoid(ga_pre.astype(F32) + bias_a)
    return [gp * bp.astype(F32) + ga * ba, ba], []


def _merge_bwd_epilogue(accs, ex):
    (dm,) = accs
    bp, ba, gp_pre, ga_pre, bias_p, bias_a = ex
    gp = jax.nn.sigmoid(gp_pre.astype(F32) + bias_p)
    ga = jax.nn.sigmoid(ga_pre.astype(F32) + bias_a)
    dbp, dba = dm * gp, dm * ga
    dgp = dbp * bp.astype(F32) * (1.0 - gp)
    dga = dba * ba.astype(F32) * (1.0 - ga)
    return [dbp, dba, dgp, dga], [dgp, dga]


def _prep(name, ws, transposes):
    n = len(ws)

    def body(*refs):
        for w_ref, o_ref, tr in zip(refs[:n], refs[n:], transposes):
            v = w_ref[...]
            o_ref[...] = (v.T if tr else v).astype(BF)

    shapes = [jax.ShapeDtypeStruct(w.shape[::-1] if tr else w.shape, BF) for w, tr in zip(ws, transposes)]
    return pl.pallas_call(body, name=name, out_shape=shapes, compiler_params=_params())(*ws)


def _adam_math(w, g, m, v):
    m = ADAM_B1 * m + (1.0 - ADAM_B1) * g
    v = ADAM_B2 * v + (1.0 - ADAM_B2) * jnp.square(g)
    m_hat = m / (1.0 - ADAM_B1 ** ADAM_STEP)
    v_hat = v / (1.0 - ADAM_B2 ** ADAM_STEP)
    delta = -ADAM_LR * (m_hat / (jnp.sqrt(v_hat) + ADAM_EPS) + ADAM_WD * w)
    return delta, m, v


def _adamw_sharded(name, items, transpose=False):
    n = len(items)

    def body(*refs):
        ins, outs = refs[:4 * n], refs[4 * n:]
        for k in range(n):
            s_ref, w_ref, m_ref, v_ref = ins[4 * k: 4 * k + 4]
            g = s_ref[0].astype(F32)
            for i in range(1, 4):
                g = g + s_ref[i].astype(F32)
            if transpose:
                g = g.T
            delta, mn, vn = _adam_math(w_ref[...], g, m_ref[...], v_ref[...])
            for o_ref, val in zip(outs[4 * k: 4 * k + 4], (g, delta, mn, vn)):
                o_ref[...] = val

    flat = [a for item in items for a in item]
    out_shape = [jax.ShapeDtypeStruct(item[1].shape, F32) for item in items for _ in range(4)]
    _, r, C = items[0][0].shape
    rows = r // 4
    if transpose or rows % 8:
        res = pl.pallas_call(body, name=name, out_shape=out_shape, compiler_params=_params())(*flat)
    else:
        tile = pl.BlockSpec((rows, C), lambda i: (i, 0))
        res = pl.pallas_call(
            body, name=name, grid=(4,), in_specs=[pl.BlockSpec((4, rows, C), lambda i: (0, i, 0)), tile, tile, tile] * n,
            out_specs=[tile] * (4 * n), out_shape=out_shape, compiler_params=_params(("parallel",)),
        )(*flat)
    return [tuple(res[4 * k: 4 * k + 4]) for k in range(n)]


SMALL_LAYOUT = (("ffn1_norm", 0, (8, LANES)), ("mix_norm", 8, (8, LANES)), ("ffn2_norm", 16, (8, LANES)),
                ("gate_bias", 24, (16, LANES)), ("pool_scale", 40, (4, LANES)), ("q_norm", 48, (1, HEAD_DIM)),
                ("k_norm", 56, (1, HEAD_DIM)), ("sinks", 64, (1, N_HEADS)))
LOSS_ROW = 72
SMALL_ROWS = 80


def _adamw_small(name, g_vec, g_pool_w, params):
    n = len(SMALL_LAYOUT) + 1

    def body(vec_ref, pw_ref, *refs):
        ins, outs = refs[:3 * n], refs[3 * n:]
        vec = vec_ref[0]
        pw = pw_ref[0]
        for i in range(1, N_DEV):
            vec = vec + vec_ref[i]
            pw = pw + pw_ref[i]
        grads = [vec[r0:r0 + shape[0], 0:shape[1]] for _, r0, shape in SMALL_LAYOUT] + [pw]
        for p, g in enumerate(grads):
            w_ref, m_ref, v_ref = ins[3 * p: 3 * p + 3]
            delta, mn, vn = _adam_math(w_ref[...], g, m_ref[...], v_ref[...])
            for o_ref, val in zip(outs[4 * p: 4 * p + 4], (g, delta, mn, vn)):
                o_ref[...] = val
        outs[4 * n][...] = vec[LOSS_ROW:LOSS_ROW + 1, :]

    flat = [a for wmv in params for a in wmv]
    out_shape = [jax.ShapeDtypeStruct(wmv[0].shape, F32) for wmv in params for _ in range(4)]
    out_shape.append(jax.ShapeDtypeStruct((1, LANES), F32))
    res = pl.pallas_call(body, name=name, out_shape=out_shape, compiler_params=_params())(g_vec, g_pool_w, *flat)
    return [tuple(res[4 * p: 4 * p + 4]) for p in range(n)], res[4 * n]


def _place():
    x, y, c = lax.axis_index("x"), lax.axis_index("y"), lax.axis_index("c")
    other_chips = [(1 - x, y), (x, 1 - y), (1 - x, 1 - y)]
    return x, y, c, other_chips


def _rows(ref, r, place, natural=False):
    px, py, pc = place
    b = 4 * px + 2 * py + pc if natural else 4 * pc + 2 * px + py
    return ref.at[pl.ds(pl.multiple_of(b * r, 8), r), :]


def _gather_task(shards, natural=(), forward_at=0.75):
    n = len(shards)
    rs = [s.shape[0] for s in shards]
    rows_of = lambda ref, k, place: _rows(ref, rs[k], place, k in natural)

    def copy(scr, outs, k, slot, block, to, src=None):
        rows = rows_of(outs[k], k, block)
        return pltpu.make_async_remote_copy(
            src_ref=rows if src is None else src, dst_ref=rows, send_sem=scr[0].at[7 * k + slot],
            recv_sem=scr[1].at[7 * k + slot], device_id=to, device_id_type=MESH)

    def first_sends(ins, outs, scr):
        x, y, c, chips = _place()
        me = (x, y, c)
        cps = [copy(scr, outs, k, 1 + j, me, (*chip, c), src=ins[k]) for j, chip in enumerate(chips) for k in range(n)]
        return cps + [copy(scr, outs, k, 0, me, (x, y, 1 - c), src=ins[k]) for k in range(n)]

    def passed_on(outs, scr):
        x, y, c, chips = _place()
        return [copy(scr, outs, k, 4 + j, (*chip, c), (x, y, 1 - c)) for j, chip in enumerate(chips) for k in range(n)]

    def local(ins, outs, scr):
        x, y, c, _ = _place()
        return [pltpu.make_async_copy(ins[k], rows_of(outs[k], k, (x, y, c)), scr[2].at[k]) for k in range(n)]

    def start(ins, outs, scr):
        for cp in local(ins, outs, scr) + first_sends(ins, outs, scr):
            cp.start()

    def forward(ins, outs, scr):
        x, y, c, chips = _place()
        for j, chip in enumerate(chips):
            for k in range(n):
                copy(scr, outs, k, 1 + j, (*chip, c), (x, y, c)).wait_recv()
                copy(scr, outs, k, 4 + j, (*chip, c), (x, y, 1 - c)).start()

    def finish(ins, outs, scr):
        x, y, c, chips = _place()
        for k in range(n):
            copy(scr, outs, k, 0, (x, y, 1 - c), (x, y, c)).wait_recv()
        for j, chip in enumerate(chips):
            for k in range(n):
                copy(scr, outs, k, 4 + j, (*chip, 1 - c), (x, y, c)).wait_recv()
        for cp in first_sends(ins, outs, scr) + passed_on(outs, scr):
            cp.wait_send()
        for cp in local(ins, outs, scr):
            cp.wait()

    out_shapes = [jax.ShapeDtypeStruct((N_DEV * s.shape[0], s.shape[1]), s.dtype) for s in shards]
    scratch = [pltpu.SemaphoreType.DMA((7 * n,)), pltpu.SemaphoreType.DMA((7 * n,)), pltpu.SemaphoreType.DMA((n,))]
    return _Task(shards, out_shapes, scratch, [(0, start), (forward_at, forward), (1.0, finish)], ("sibling", "chips"))


def _direct_gather_task(shards):
    n = len(shards)
    rs = [s.shape[0] for s in shards]

    def peers():
        x, y, c, _ = _place()
        flip = lambda v, bit: 1 - v if bit else v
        return (x, y, c), [(flip(x, (s >> 2) & 1), flip(y, (s >> 1) & 1), flip(c, s & 1)) for s in range(1, N_DEV)]

    def copies(ins, outs, scr):
        me, others = peers()
        local = [pltpu.make_async_copy(ins[k], _rows(outs[k], rs[k], me), scr[2].at[k]) for k in range(n)]
        sems = lambda k, s: dict(send_sem=scr[0].at[7 * k + s], recv_sem=scr[1].at[7 * k + s], device_id_type=MESH)
        sends = [pltpu.make_async_remote_copy(src_ref=ins[k], dst_ref=_rows(outs[k], rs[k], me), device_id=to, **sems(k, s))
                 for s, to in enumerate(others) for k in range(n)]
        recvs = [pltpu.make_async_remote_copy(src_ref=_rows(outs[k], rs[k], frm), dst_ref=_rows(outs[k], rs[k], frm),
                                              device_id=me, **sems(k, s))
                 for s, frm in enumerate(others) for k in range(n)]
        return local, sends, recvs

    def start(ins, outs, scr):
        local, sends, _ = copies(ins, outs, scr)
        for cp in local + sends:
            cp.start()

    def finish(ins, outs, scr):
        local, sends, recvs = copies(ins, outs, scr)
        for cp in recvs:
            cp.wait_recv()
        for cp in sends:
            cp.wait_send()
        for cp in local:
            cp.wait()

    out_shapes = [jax.ShapeDtypeStruct((N_DEV * s.shape[0], s.shape[1]), s.dtype) for s in shards]
    scratch = [pltpu.SemaphoreType.DMA((7 * n,)), pltpu.SemaphoreType.DMA((7 * n,)), pltpu.SemaphoreType.DMA((n,))]
    return _Task(shards, out_shapes, scratch, [(0, start), (1.0, finish)], ("all",))


def _chip_task(sums):
    n = len(sums)
    rs = [s.shape[0] // 4 for s in sums]

    def block(ref, k, chip_index):
        return ref.at[pl.ds(pl.multiple_of(chip_index * rs[k], 8), rs[k]), :]

    def copies(ins, outs, scr):
        send_sems, recv_sems, local_sems = scr
        x, y, c, chips = _place()
        here = 2 * x + y
        local = [pltpu.make_async_copy(block(ins[k], k, here), outs[k].at[here], local_sems.at[k]) for k in range(n)]
        remote = []
        for j, (px, py) in enumerate(chips):
            remote += [pltpu.make_async_remote_copy(
                src_ref=block(ins[k], k, 2 * px + py), dst_ref=outs[k].at[here],
                send_sem=send_sems.at[3 * k + j], recv_sem=recv_sems.at[3 * k + j],
                device_id=(px, py, c), device_id_type=MESH) for k in range(n)]
        return local, remote

    def start(ins, outs, scr):
        local, remote = copies(ins, outs, scr)
        for cp in local + remote:
            cp.start()

    def finish(ins, outs, scr):
        local, remote = copies(ins, outs, scr)
        for cp in remote:
            cp.wait()
        for cp in local:
            cp.wait()

    out_shapes = [jax.ShapeDtypeStruct((4, r, s.shape[1]), s.dtype) for r, s in zip(rs, sums)]
    scratch = [pltpu.SemaphoreType.DMA((3 * n,)), pltpu.SemaphoreType.DMA((3 * n,)), pltpu.SemaphoreType.DMA((n,))]
    return _Task(sums, out_shapes, scratch, [(0, start), (1.0, finish)], ("chips",))


def _dw_pair(name, a, b, scale, comm=None, blocks=1):
    T, M = a.shape
    N = b.shape[1]
    half = M // 2
    wide = half // blocks
    tk = min(2048, T)
    nK = T // tk
    plumb = _CommPlumbing(comm)

    def body(core_ref, *rest):
        a_refs, b_ref, rest = rest[:blocks], rest[blocks], rest[blocks + 1:]
        c_in = rest[:plumb.n_in]
        o_ref = rest[plumb.n_in]
        c_out = rest[plumb.n_in + 1: plumb.n_in + 1 + plumb.n_out]
        acc, stage, land, send_sem, recv_sem = rest[plumb.n_in + 1 + plumb.n_out: plumb.n_in + 6 + plumb.n_out]
        c_scr = rest[plumb.n_in + 6 + plumb.n_out:]
        i, k = pl.program_id(0), pl.program_id(1)
        x, y, c, _ = _place()
        push = pltpu.make_async_remote_copy(src_ref=stage, dst_ref=land, send_sem=send_sem, recv_sem=recv_sem,
                                            device_id=(x, y, 1 - c), device_id_type=MESH)
        plumb.handshake((i == 0) & (k == 0), own=("sibling",))
        if comm:
            plumb.run(i * nK + k, 2 * nK, True, c_in, c_out, c_scr)

        av = a_refs[0][...] if blocks == 1 else jnp.concatenate([r[...] for r in a_refs], axis=1)
        p = lax.dot_general(av, b_ref[...], _DIMS["tn"], preferred_element_type=F32)

        @pl.when(k == 0)
        def _():
            acc[...] = p

        @pl.when(k > 0)
        def _():
            acc[...] += p

        @pl.when((i == 0) & (k == nK - 1))
        def _():
            stage[...] = (scale * acc[...]).astype(BF)
            push.start()

        @pl.when((i == 1) & (k == nK - 1))
        def _():
            push.wait_recv()
            o_ref[...] = (scale * acc[...] + land[...].astype(F32)).astype(BF)
            push.wait_send()

        if comm:
            plumb.run(i * nK + k, 2 * nK, False, c_in, c_out, c_scr)

    grid_spec = pltpu.PrefetchScalarGridSpec(
        num_scalar_prefetch=1, grid=(2, nK),
        in_specs=[pl.BlockSpec((tk, wide), functools.partial(
            lambda i, k, core, j: (k, (2 * j if blocks > 1 else 0) + jnp.where(i == 0, 1 - core[0], core[0])), j=j))
            for j in range(blocks)] + [pl.BlockSpec((tk, N), lambda i, k, core: (k, 0))] + [ANY] * plumb.n_in,
        out_specs=[pl.BlockSpec((half, N), lambda i, k, core: (0, 0))] + [ANY] * plumb.n_out,
        scratch_shapes=[pltpu.VMEM((half, N), F32), pltpu.VMEM((half, N), BF), pltpu.VMEM((half, N), BF),
                        pltpu.SemaphoreType.DMA, pltpu.SemaphoreType.DMA] + plumb.scratch)
    core = lax.axis_index("c").astype(jnp.int32).reshape(1)
    res = pl.pallas_call(
        body, name=name, grid_spec=grid_spec,
        out_shape=[jax.ShapeDtypeStruct((half, N), BF)] + plumb.out_shapes,
        compiler_params=_params(("arbitrary", "arbitrary"), plumb.collective_id(own=("sibling",))),
    )(core, *([a] * blocks), b, *plumb.args)
    return (res[0], plumb.split_outputs(res[1:])) if comm else res[0]


def _pair_task(parts):
    n = len(parts)

    def copies(ins, outs, scr):
        x, y, c, _ = _place()
        return [pltpu.make_async_remote_copy(
            src_ref=ins[k].at[:, pl.ds(1 - c, 1)], dst_ref=outs[k], send_sem=scr[0].at[k], recv_sem=scr[1].at[k],
            device_id=(x, y, 1 - c), device_id_type=MESH) for k in range(n)]

    def start(ins, outs, scr):
        for cp in copies(ins, outs, scr):
            cp.start()

    def finish(ins, outs, scr):
        for cp in copies(ins, outs, scr):
            cp.wait()

    out_shapes = [jax.ShapeDtypeStruct((4, 1) + p.shape[2:], p.dtype) for p in parts]
    scratch = [pltpu.SemaphoreType.DMA((n,)), pltpu.SemaphoreType.DMA((n,))]
    return _Task(parts, out_shapes, scratch, [(0, start), (1.0, finish)], ("sibling",))


def _pair_sum(name, part, got, core):
    _, _, r, C = part.shape

    def body(core_ref, p_ref, g_ref, o_ref):
        o_ref[0] = (p_ref[0, 0].astype(F32) + g_ref[0, 0].astype(F32)).astype(o_ref.dtype)

    return pl.pallas_call(
        body, name=name,
        grid_spec=pltpu.PrefetchScalarGridSpec(
            num_scalar_prefetch=1, grid=(4,),
            in_specs=[pl.BlockSpec((1, 1, r, C), lambda i, core_ref: (i, core_ref[0], 0, 0)),
                      pl.BlockSpec((1, 1, r, C), lambda i, core_ref: (i, 0, 0, 0))],
            out_specs=pl.BlockSpec((1, r, C), lambda i, core_ref: (i, 0, 0))),
        out_shape=jax.ShapeDtypeStruct((4, r, C), part.dtype), compiler_params=_params(("parallel",)),
    )(core, part, got)


def _ffn_bwd(tag, dy, dyb, x, gain, wgT, wuT, wd, saved, earlier=None):
    n, g, u, a = saved
    half = lambda accs, ex: _swiglu_bwd_epilogue([0.5 * accs[0]], ex)
    act_args = dict(tm=1024, tn=1408, tk=D_MODEL, epilogue=half, extras=[(g, "tile", 0), (u, "tile", 0)], cols_outer=True)
    if earlier is None:
        sum_d = _dw_pair(tag + "_dw_down", a, dyb, 0.5)
        (dg, du), ((slots_d,),) = _mm(tag + "_d_act", [(dyb, wd, "nt", 0)], [BF, BF], comm=[_chip_task([sum_d])], **act_args)
        slots_e = None
        sum_g = _dw_pair(tag + "_dw_gate", dg, n, 1.0)
    else:
        sum_d, ((got,),) = _dw_pair(tag + "_dw_down", a, dyb, 0.5, comm=[_pair_task([earlier])])
        core = lax.axis_index("c").astype(jnp.int32).reshape(1)
        sum_e = _pair_sum(tag + "_pair_sum_earlier", earlier, got, core)
        sum_e = sum_e.reshape(4 * sum_e.shape[1], sum_e.shape[2])
        (dg, du), ((slots_e,),) = _mm(tag + "_d_act", [(dyb, wd, "nt", 0)], [BF, BF], comm=[_chip_task([sum_e])], **act_args)
        sum_g, ((slots_d,),) = _dw_pair(tag + "_dw_gate", dg, n, 1.0, comm=[_chip_task([sum_d])])
    sum_u, ((slots_g,),) = _dw_pair(tag + "_dw_up", du, n, 1.0, comm=[_chip_task([sum_g])])
    (dx, dxb, dgain), ((slots_u,),) = _mm(
        tag + "_d_norm", [(dg, wgT, "nn", 0), (du, wuT, "nn", 0)], [F32, BF], tm=512, tn=D_MODEL, tk=D_FF,
        epilogue=_rms_bwd_epilogue, extras=[(x, "tile", 0), (gain, "row", 0), (dy, "tile", 0)], n_colsum=1,
        comm=[_chip_task([sum_u])])
    return dx, dxb, dgain, slots_e, slots_g, slots_u, slots_d


def _tile_gain(g):
    return jnp.concatenate([g, g]).reshape(1, LANES)


def _fold_heads(partials):
    return jnp.sum(partials.reshape(-1, HEAD_DIM), axis=0)


def _pack_small_grads(grads, loss_local):
    pieces, row = [], 0
    for name, r0, _ in SMALL_LAYOUT + (("loss", LOSS_ROW, None),):
        v = (loss_local if name == "loss" else grads[name]).reshape(-1)
        rows = -(-v.size // LANES)
        block = jnp.pad(v, (0, rows * LANES - v.size)).reshape(rows, LANES)
        pieces += [jnp.zeros((r0 - row, LANES), F32)] * (r0 > row) + [block]
        row = r0 + rows
    pieces.append(jnp.zeros((SMALL_ROWS - row, LANES), F32))
    return jnp.concatenate(pieces, axis=0)


def kernel(x, ffn1_norm, ffn1_w_gate, ffn1_w_up, ffn1_w_down, mix_norm, w_in, pool_w, pool_scale, w_pool_out, q_norm, k_norm, sinks, w_attn_out, gate_bias, w_out, ffn2_norm, ffn2_w_gate, ffn2_w_up, ffn2_w_down, loss_target, m_ffn1_norm, m_ffn1_w_gate, m_ffn1_w_up, m_ffn1_w_down, m_mix_norm, m_w_in, m_pool_w, m_pool_scale, m_w_pool_out, m_q_norm, m_k_norm, m_sinks, m_w_attn_out, m_gate_bias, m_w_out, m_ffn2_norm, m_ffn2_w_gate, m_ffn2_w_up, m_ffn2_w_down, v_ffn1_norm, v_ffn1_w_gate, v_ffn1_w_up, v_ffn1_w_down, v_mix_norm, v_w_in, v_pool_w, v_pool_scale, v_w_pool_out, v_q_norm, v_k_norm, v_sinks, v_w_attn_out, v_gate_bias, v_w_out, v_ffn2_norm, v_ffn2_w_gate, v_ffn2_w_up, v_ffn2_w_down):
    T = x.shape[1]
    x2 = x.reshape(T, D_MODEL)
    target = loss_target.reshape(T, D_MODEL)

    big = [
        ("ffn1_w_gate", ffn1_w_gate, m_ffn1_w_gate, v_ffn1_w_gate, True, False),
        ("ffn1_w_up", ffn1_w_up, m_ffn1_w_up, v_ffn1_w_up, True, False),
        ("ffn1_w_down", ffn1_w_down, m_ffn1_w_down, v_ffn1_w_down, False, False),
        ("w_in", w_in, m_w_in, v_w_in, True, False),
        ("w_pool_out", w_pool_out, m_w_pool_out, v_w_pool_out, False, True),
        ("w_attn_out", w_attn_out, m_w_attn_out, v_w_attn_out, False, False),
        ("w_out", w_out, m_w_out, v_w_out, False, False),
        ("ffn2_w_gate", ffn2_w_gate, m_ffn2_w_gate, v_ffn2_w_gate, True, False),
        ("ffn2_w_up", ffn2_w_up, m_ffn2_w_up, v_ffn2_w_up, True, False),
        ("ffn2_w_down", ffn2_w_down, m_ffn2_w_down, v_ffn2_w_down, False, False),
    ]
    view = lambda a, tv: a.T if tv else a
    views = [view(w, tv) for _, w, _, _, tv, _ in big]
    in_kernel_t = [tk_ for *_, tk_ in big]
    first_shards = _prep("prep_ffn1_gate_up", views[0:2], in_kernel_t[0:2])
    g1 = ffn1_norm.reshape(1, D_MODEL)
    g2 = mix_norm.reshape(1, D_MODEL)
    g3 = ffn2_norm.reshape(1, D_MODEL)
    bias_row = gate_bias.reshape(1, 2 * D_MODEL)
    qg, kg = _tile_gain(q_norm) * ATTN_SCALE, _tile_gain(k_norm)
    scale_row = pool_scale.reshape(1, POOL_WIDTH)

    n1, later_shards, ((wg1T, wu1T),) = _rms_fwd(
        "ffn1_norm", x2, g1, [_gather_task(first_shards, forward_at=0.9)], views[2:], in_kernel_t[2:])
    shards = list(first_shards) + later_shards
    (gt1, up1, act1), ((wd1,), (w_inT,)) = _mm(
        "ffn1_gate_up", [(n1, wg1T, "nt", 0), (n1, wu1T, "nt", 1)], [BF, BF, BF], tm=1024, tn=1408, tk=D_MODEL,
        epilogue=_swiglu_fwd_epilogue, cols_outer=True,
        comm=[_gather_task(shards[2:3], forward_at=0.5), _gather_task(shards[3:4], natural=(0,), forward_at=0.9)])
    (h1, u), ((w_poT, w_ao, w_o),) = _mm(
        "ffn1_down", [(act1, wd1, "nn", 0)], [F32, BF], tm=512, tn=D_MODEL, tk=D_FF,
        epilogue=_residual_norm_epilogue(0.5), extras=[(x2, "tile", 0), (g2, "row", 0)],
        comm=[_gather_task(shards[4:7], natural=(0, 1, 2), forward_at=0.8)])
    saved1 = (n1, gt1, up1, act1)
    (proj,), ((wg2T,),) = _mm(
        "in_proj", [(u, w_inT, "nt", 0)], [BF], tm=1024, tn=1280, tk=D_MODEL, cols_outer=True,
        comm=[_gather_task(shards[7:8], forward_at=0.8)])
    pooled, mixed = _pool_fwd("pool_fwd", proj, pool_w, scale_row)
    qn = _headnorm_fwd("q_norm", proj, COL_Q, ATTN_WIDTH, qg)
    kn = _headnorm_fwd("k_norm", proj, COL_K, KV_WIDTH, kg)
    attn, ((wu2T,),) = _attn_fwd("attn_fwd", qn, kn, proj, sinks, comm=[_gather_task(shards[8:9], forward_at=0.8)])
    (bp,) = _mm("pool_out", [(mixed, w_poT, "nt", 0)], [BF], tm=1024, tn=D_MODEL, tk=POOL_WIDTH)
    gate_tn = 256
    gate_extras = [(proj, "tile", COL_GP // gate_tn), (proj, "tile", COL_GA // gate_tn),
                   (bias_row, "row", 0), (bias_row, "row", D_MODEL // gate_tn)]
    merged, ba = _mm("attn_out_merge", [(attn, w_ao, "nn", 0)], [BF, BF], tm=2048, tn=gate_tn, tk=ATTN_WIDTH,
                     epilogue=_merge_fwd_epilogue, extras=[(bp, "tile", 0)] + gate_extras)
    h2, n2 = _mm("mix_out", [(merged, w_o, "nn", 0)], [F32, BF], tm=1024, tn=D_MODEL, tk=D_MODEL,
                 epilogue=_residual_norm_epilogue(1.0), extras=[(h1, "tile", 0), (g3, "row", 0)])
    (gt2, up2, act2), ((wd2,),) = _mm(
        "ffn2_gate_up", [(n2, wg2T, "nt", 0), (n2, wu2T, "nt", 1)], [BF, BF, BF], tm=1024, tn=1408, tk=D_MODEL,
        epilogue=_swiglu_fwd_epilogue, cols_outer=True, comm=[_gather_task(shards[9:10], forward_at=0.8)])
    dy, dyb, sq = _mm("ffn2_down_loss", [(act2, wd2, "nn", 0)], [F32, BF], tm=512, tn=D_MODEL, tk=D_FF,
                      epilogue=_loss_epilogue, extras=[(h2, "tile", 0), (target, "tile", 0)], n_colsum=1)
    loss_local = 0.5 * jnp.sum(sq) / D_MODEL

    dh2, dh2b, dg3, _, slots_g2, slots_u2, slots_d2 = _ffn_bwd(
        "ffn2", dy, dyb, h2, g3, wg2T, wu2T, wd2, (n2, gt2, up2, act2))
    dbp, dba, dproj, dga, cs_gp, cs_ga = _mm(
        "mix_out_bwd", [(dh2b, w_o, "nt", 0)], [BF, BF, BF, BF], tm=2048, tn=gate_tn, tk=D_MODEL,
        epilogue=_merge_bwd_epilogue, extras=[(bp, "tile", 0), (ba, "tile", 0)] + gate_extras, n_colsum=2,
        out_placement={2: (IN_WIDTH, COL_GP)})
    sum_o = _dw_pair("dw_out", merged, dh2b, 1.0, blocks=4)
    (dmixed,) = _mm("pool_out_bwd", [(dbp, w_poT, "nn", 0)], [BF], tm=1024, tn=POOL_WIDTH, tk=D_MODEL)
    sum_po = _dw_pair("dw_pool_out", dbp, mixed, 1.0, blocks=4)
    (dattn,) = _mm("attn_out_bwd", [(dba, w_ao, "nt", 0)], [BF], tm=1024, tn=ATTN_WIDTH, tk=D_MODEL)
    sum_ao = _dw_pair("dw_attn_out", attn, dba, 1.0, blocks=4)
    (dqn, dkn, dv, dsink_tile), ((slots_o, slots_po, slots_ao),) = _attn_bwd(
        "attn_bwd", dattn, qn, kn, proj, sinks, [_chip_task([sum_o, sum_po, sum_ao])])
    dproj, dqg = _headnorm_bwd("q_norm_bwd", dqn, proj, COL_Q, ATTN_WIDTH, qg, dproj)
    dproj, dkg = _headnorm_bwd("k_norm_bwd", dkn, proj, COL_K, KV_WIDTH, kg, dproj)
    dproj, dpool_w, dpool_scale = _pool_bwd("pool_bwd", dmixed, pooled, pool_w, scale_row, dproj)
    for piece, col in ((dv, COL_V), (dga, COL_GA)):
        dproj = lax.dynamic_update_slice(dproj, piece, (0, col))
    (dh1, dh1b, dg2), ((g_pool_w,),) = _mm(
        "in_proj_bwd", [(dproj, w_inT, "nn", 0)], [F32, BF], tm=512, tn=D_MODEL, tk=IN_WIDTH, epilogue=_rms_bwd_epilogue,
        extras=[(h1, "tile", 0), (g2, "row", 0), (dh2, "tile", 0)], n_colsum=1,
        comm=[_gather_task([dpool_w.reshape(-1, LANES)])])
    (dw_inT,) = _mm("dw_in", [(dproj, u, "tn", 0)], [BF], tm=1280, tn=D_MODEL, tk=2048)
    dx, _, dg1, slots_in, slots_g1, slots_u1, slots_d1 = _ffn_bwd(
        "ffn1", dh1, dh1b, x2, g1, wg1T, wu1T, wd1, saved1, dw_inT.reshape(4, 2, IN_WIDTH // N_DEV, D_MODEL))

    slots = [slots_g1, slots_u1, slots_d1, slots_in, slots_po, slots_ao, slots_o, slots_g2, slots_u2, slots_d2]
    big_out = {}
    for label, group in (("ffn", (0, 1, 2, 7, 8, 9)), ("w_in", (3,)), ("w_pool_out", (4,)), ("attn_out_and_out", (5, 6))):
        items = [(slots[k], view(big[k][1], big[k][4]), view(big[k][2], big[k][4]), view(big[k][3], big[k][4]))
                 for k in group]
        for k, res in zip(group, _adamw_sharded("adamw_" + label, items, transpose=big[group[0]][5])):
            big_out[big[k][0]] = tuple(view(r, big[k][4]) for r in res)

    small_grads = {
        "ffn1_norm": jnp.sum(dg1, axis=(0, 1)), "mix_norm": jnp.sum(dg2, axis=(0, 1)), "ffn2_norm": jnp.sum(dg3, axis=(0, 1)),
        "gate_bias": jnp.concatenate([jnp.sum(cs_gp, axis=(0, 1)), jnp.sum(cs_ga, axis=(0, 1))]),
        "pool_scale": dpool_scale, "q_norm": _fold_heads(dqg) * ATTN_SCALE, "k_norm": _fold_heads(dkg),
        "sinks": dsink_tile[0, :N_HEADS]}
    ((g_vec,),) = _comm_only("gather_small_grads", [_direct_gather_task([_pack_small_grads(small_grads, loss_local)])])
    given = {"ffn1_norm": (ffn1_norm, m_ffn1_norm, v_ffn1_norm), "mix_norm": (mix_norm, m_mix_norm, v_mix_norm),
             "ffn2_norm": (ffn2_norm, m_ffn2_norm, v_ffn2_norm), "gate_bias": (gate_bias, m_gate_bias, v_gate_bias),
             "pool_scale": (pool_scale, m_pool_scale, v_pool_scale), "q_norm": (q_norm, m_q_norm, v_q_norm),
             "k_norm": (k_norm, m_k_norm, v_k_norm), "sinks": (sinks, m_sinks, v_sinks)}
    params = [tuple(a.reshape(shape) for a in given[nm]) for nm, _, shape in SMALL_LAYOUT]
    params.append(tuple(a.reshape(-1, LANES) for a in (pool_w, m_pool_w, v_pool_w)))
    small_res, loss_row = _adamw_small("adamw_small", g_vec.reshape(N_DEV, SMALL_ROWS, LANES),
                                       g_pool_w.reshape(N_DEV, -1, LANES), params)
    small_out = {nm: tuple(r.reshape(given[nm][0].shape) for r in res)
                 for (nm, _, _), res in zip(SMALL_LAYOUT, small_res)}
    small_out["pool_w"] = tuple(r.reshape(pool_w.shape) for r in small_res[-1])
    loss = loss_row[0, 0]

    order = ["ffn1_norm", "ffn1_w_gate", "ffn1_w_up", "ffn1_w_down", "mix_norm", "w_in", "pool_w", "pool_scale",
             "w_pool_out", "q_norm", "k_norm", "sinks", "w_attn_out", "gate_bias", "w_out", "ffn2_norm",
             "ffn2_w_gate", "ffn2_w_up", "ffn2_w_down"]
    every = {**big_out, **small_out}
    outs = [loss, dx.reshape(x.shape)]
    for j in range(4):
        outs += [every[nm][j] for nm in order]
    return tuple(outs)
```

```python
import functools

import jax
import jax.numpy as jnp
from jax import lax
from jax.experimental import pallas as pl
from jax.experimental.pallas import tpu as pltpu

BF = jnp.bfloat16
F32 = jnp.float32

D_MODEL = 1024
D_FF = 2816
POOL_WIDTH = 512
POOL_GROUP = 128
N_POOL_GROUPS = 4
HEAD_DIM = 64
N_HEADS = 16
GQA_GROUP = 8
BLOCK = 128
ATTN_WIDTH = 1024
KV_WIDTH = 128
IN_WIDTH = 3840
RMS_EPS = 1e-6
N_DEV = 8
LANES = 128

COL_Q = POOL_WIDTH
COL_K = COL_Q + ATTN_WIDTH
COL_V = COL_K + KV_WIDTH
COL_GP = COL_V + KV_WIDTH
COL_GA = COL_GP + D_MODEL

ADAM_LR = 0.001
ADAM_B1 = 0.9
ADAM_B2 = 0.999
ADAM_EPS = 1e-08
ADAM_WD = 0.01
ADAM_STEP = 10

VMEM_LIMIT_V7X = 56 * 1024 * 1024
MESH = pl.DeviceIdType.MESH
ANY = pl.BlockSpec(memory_space=pl.ANY)


def _params(sem=None, collective_id=None):
    return pltpu.CompilerParams(dimension_semantics=sem, vmem_limit_bytes=VMEM_LIMIT_V7X, collective_id=collective_id)


COLLECTIVE_IDS = {frozenset(["sibling"]): 0, frozenset(["chips"]): 1, frozenset(["sibling", "chips"]): 2}


def _handshake(peer_kinds):
    x, y, c, chips = _place()
    peers = ([(x, y, 1 - c)] if "sibling" in peer_kinds else []) + ([(*chip, c) for chip in chips] if "chips" in peer_kinds else [])
    barrier = pltpu.get_barrier_semaphore()
    for peer in peers:
        pl.semaphore_signal(barrier, inc=1, device_id=peer, device_id_type=MESH)
    pl.semaphore_wait(barrier, len(peers))


_DIMS = {"nt": (((1,), (1,)), ((), ())), "nn": (((1,), (0,)), ((), ())), "tn": (((0,), (0,)), ((), ()))}


class _Task:
    def __init__(self, inputs, out_shapes, scratch, phases, peers):
        self.inputs, self.out_shapes, self.scratch = list(inputs), list(out_shapes), list(scratch)
        self.phases = list(phases)
        self.peers = frozenset(peers)


class _CommPlumbing:
    def __init__(self, tasks):
        self.tasks = list(tasks or [])
        self.args = [a for t in self.tasks for a in t.inputs]
        self.out_shapes = [o for t in self.tasks for o in t.out_shapes]
        self.scratch = [s for t in self.tasks for s in t.scratch]
        self.n_in, self.n_out = len(self.args), len(self.out_shapes)

    def peer_kinds(self, own=()):
        kinds = frozenset(own).union(*[t.peers for t in self.tasks])
        return None if "all" in kinds or not kinds else kinds

    def collective_id(self, own=()):
        kinds = self.peer_kinds(own)
        return None if kinds is None else COLLECTIVE_IDS[kinds]

    def handshake(self, first, own=()):
        kinds = self.peer_kinds(own)
        if kinds is not None:
            pl.when(first)(functools.partial(_handshake, kinds))

    def _slices(self, c_in, c_out, c_scr):
        i = o = s = 0
        for t in self.tasks:
            yield t, c_in[i:i + len(t.inputs)], c_out[o:o + len(t.out_shapes)], c_scr[s:s + len(t.scratch)]
            i, o, s = i + len(t.inputs), o + len(t.out_shapes), s + len(t.scratch)

    def run(self, step, steps, before, c_in, c_out, c_scr):
        for t, ins, outs, scr in self._slices(c_in, c_out, c_scr):
            for frac, fn in t.phases:
                if step is None:
                    fn(ins, outs, scr)
                elif before == (frac == 0):
                    at = 0 if frac == 0 else max(0, min(steps, -(-int(round(frac * steps * 64)) // 64)) - 1)
                    pl.when(step == at)(functools.partial(fn, ins, outs, scr))

    def split_outputs(self, flat):
        res, o = [], 0
        for t in self.tasks:
            res.append(list(flat[o:o + len(t.out_shapes)]))
            o += len(t.out_shapes)
        return res


def _comm_only(name, tasks):
    plumb = _CommPlumbing(tasks)

    def body(*refs):
        c_in, c_out = refs[:plumb.n_in], refs[plumb.n_in: plumb.n_in + plumb.n_out]
        c_scr = refs[plumb.n_in + plumb.n_out:]
        plumb.run(None, 1, True, c_in, c_out, c_scr)

    res = pl.pallas_call(
        body, name=name, in_specs=[ANY] * plumb.n_in, out_specs=[ANY] * plumb.n_out, out_shape=plumb.out_shapes,
        scratch_shapes=plumb.scratch, compiler_params=pltpu.CompilerParams(has_side_effects=True),
    )(*plumb.args)
    return plumb.split_outputs(res)


def _mm(name, terms, out_dtypes, *, tm, tn, tk, epilogue=None, extras=(), n_colsum=0, comm=None, cols_outer=False,
        out_placement=None):
    a0, b0, mode0, _ = terms[0]
    if mode0 == "nt":
        (M, K), N = a0.shape, b0.shape[0]
    elif mode0 == "nn":
        (M, K), N = a0.shape, b0.shape[1]
    else:
        (K, M), N = a0.shape, b0.shape[1]
    tm, tn, tk = min(tm, M), min(tn, N), min(tk, K)
    assert M % tm == 0 and N % tn == 0 and K % tk == 0, (name, M, N, K, tm, tn, tk)
    nI, nJ, nK = M // tm, N // tn, K // tk
    n_terms = len(terms)
    n_acc = max(t[3] for t in terms) + 1
    n_ex = len(extras)
    n_out = len(out_dtypes)
    if epilogue is None:
        epilogue = lambda accs, ex: ([accs[0]], [])
    plumb = _CommPlumbing(comm)
    n_scr = n_acc if nK > 1 else 0
    grid = (nJ, nI, nK) if cols_outer else (nI, nJ, nK)

    def body(*refs):
        n_in = 2 * n_terms + n_ex
        ab = refs[: 2 * n_terms]
        ex_refs = refs[2 * n_terms: n_in]
        c_in = refs[n_in: n_in + plumb.n_in]
        o0 = n_in + plumb.n_in
        out_refs = refs[o0: o0 + n_out]
        cs_refs = refs[o0 + n_out: o0 + n_out + n_colsum]
        c_out = refs[o0 + n_out + n_colsum: o0 + n_out + n_colsum + plumb.n_out]
        s0 = o0 + n_out + n_colsum + plumb.n_out
        acc_refs = refs[s0: s0 + n_scr]
        c_scr = refs[s0 + n_scr:]
        steps = grid[0] * grid[1] * nK
        if comm:
            step = (pl.program_id(0) * grid[1] + pl.program_id(1)) * nK + pl.program_id(2)
            plumb.handshake(step == 0)
            plumb.run(step, steps, True, c_in, c_out, c_scr)

        def products():
            accs = [None] * n_acc
            for t, (_, _, mode, ai) in enumerate(terms):
                p = lax.dot_general(ab[2 * t][...], ab[2 * t + 1][...], _DIMS[mode], preferred_element_type=F32)
                accs[ai] = p if accs[ai] is None else accs[ai] + p
            return accs

        def finish(accs):
            outs, colsums = epilogue(accs, [r[...] for r in ex_refs])
            for r, o in zip(out_refs, outs):
                r[...] = o.astype(r.dtype)
            for r, cs in zip(cs_refs, colsums):
                r[...] = jnp.sum(cs, axis=0, keepdims=True).reshape(r.shape)

        if nK == 1:
            finish(products())
        else:
            k = pl.program_id(2)
            accs = products()

            @pl.when(k == 0)
            def _():
                for r, a in zip(acc_refs, accs):
                    r[...] = a

            @pl.when(k > 0)
            def _():
                for r, a in zip(acc_refs, accs):
                    r[...] += a

            @pl.when(k == nK - 1)
            def _():
                finish([r[...] for r in acc_refs])

        if comm:
            plumb.run(step, steps, False, c_in, c_out, c_scr)

    def spec(block, index, fixed=False):
        imap = (lambda q, p, k: index(p, q, k)) if cols_outer else index
        return pl.BlockSpec(block, imap, pipeline_mode=pl.Buffered(1)) if fixed else pl.BlockSpec(block, imap)

    in_specs, args = [], []
    for a, b, mode, _ in terms:
        if mode == "nt":
            in_specs += [spec((tm, tk), lambda i, j, k: (i, k), nI * nK == 1),
                         spec((tn, tk), lambda i, j, k: (j, k), nJ * nK == 1)]
        elif mode == "nn":
            in_specs += [spec((tm, tk), lambda i, j, k: (i, k), nI * nK == 1),
                         spec((tk, tn), lambda i, j, k: (k, j), nJ * nK == 1)]
        else:
            in_specs += [spec((tk, tm), lambda i, j, k: (k, i), nI * nK == 1),
                         spec((tk, tn), lambda i, j, k: (k, j), nJ * nK == 1)]
        args += [a, b]
    for arr, kind, off in extras:
        if kind == "tile":
            in_specs.append(spec((tm, tn), functools.partial(lambda i, j, k, off: (i, j + off), off=off)))
        else:
            in_specs.append(spec((1, tn), functools.partial(lambda i, j, k, off: (0, j + off), off=off)))
        args.append(arr)
    placed = dict(out_placement or {})
    out_shape = [jax.ShapeDtypeStruct((M, placed.get(o, (N, 0))[0]), dt) for o, dt in enumerate(out_dtypes)]
    out_specs = [spec((tm, tn), functools.partial(lambda i, j, k, off: (i, j + off), off=placed.get(o, (N, 0))[1] // tn))
                 for o in range(n_out)]
    out_shape += [jax.ShapeDtypeStruct((nI, 1, N), F32) for _ in range(n_colsum)]
    out_specs += [spec((1, 1, tn), lambda i, j, k: (i, 0, j)) for _ in range(n_colsum)]
    scratch = [pltpu.VMEM((tm, tn), F32) for _ in range(n_scr)]
    args += plumb.args
    in_specs += [ANY] * plumb.n_in
    out_shape += plumb.out_shapes
    out_specs += [ANY] * plumb.n_out
    sem = ("arbitrary",) * 3 if comm else ("parallel", "parallel", "arbitrary")
    res = pl.pallas_call(
        body, name=name, grid=grid, in_specs=in_specs, out_specs=out_specs, out_shape=out_shape,
        scratch_shapes=scratch + plumb.scratch, compiler_params=_params(sem, plumb.collective_id()),
    )(*args)
    n_own = n_out + n_colsum
    return (list(res[:n_own]), plumb.split_outputs(res[n_own:])) if comm is not None else res


ROW_TILE = 512


def _rms_fwd(name, x, g, comm, weights, transposes):
    T, D = x.shape
    steps = T // ROW_TILE
    plumb = _CommPlumbing(comm)
    nw = len(weights)

    def body(x_ref, g_ref, *rest):
        w_refs, c_in = rest[:nw], rest[nw: nw + plumb.n_in]
        o_ref, shard_refs = rest[nw + plumb.n_in], rest[nw + plumb.n_in + 1: 2 * nw + plumb.n_in + 1]
        c_out = rest[2 * nw + plumb.n_in + 1: 2 * nw + plumb.n_in + 1 + plumb.n_out]
        c_scr = rest[2 * nw + plumb.n_in + 1 + plumb.n_out:]
        plumb.handshake(pl.program_id(0) == 0)
        plumb.run(pl.program_id(0), steps, True, c_in, c_out, c_scr)

        @pl.when(pl.program_id(0) == 0)
        def _():
            for w_ref, s_ref, tr in zip(w_refs, shard_refs, transposes):
                v = w_ref[...]
                s_ref[...] = (v.T if tr else v).astype(BF)

        xv = x_ref[...]
        r = lax.rsqrt(jnp.mean(xv * xv, axis=-1, keepdims=True) + RMS_EPS)
        o_ref[...] = (xv * r * g_ref[...]).astype(BF)
        plumb.run(pl.program_id(0), steps, False, c_in, c_out, c_scr)

    row = pl.BlockSpec((ROW_TILE, D), lambda i: (i, 0))
    whole = lambda shape: pl.BlockSpec(shape, lambda i: (0, 0), pipeline_mode=pl.Buffered(1))
    shard_shapes = [w.shape[::-1] if tr else w.shape for w, tr in zip(weights, transposes)]
    res = pl.pallas_call(
        body, name=name, grid=(steps,),
        in_specs=[row, pl.BlockSpec((1, D), lambda i: (0, 0))] + [whole(w.shape) for w in weights] + [ANY] * plumb.n_in,
        out_specs=[row] + [whole(s) for s in shard_shapes] + [ANY] * plumb.n_out,
        out_shape=[jax.ShapeDtypeStruct((T, D), BF)] + [jax.ShapeDtypeStruct(s, BF) for s in shard_shapes] + plumb.out_shapes,
        scratch_shapes=plumb.scratch, compiler_params=_params(("arbitrary",), plumb.collective_id()),
    )(x, g, *weights, *plumb.args)
    return res[0], list(res[1: nw + 1]), plumb.split_outputs(res[nw + 1:])


HEADNORM_TILE = 2048


def _half_sum_matrix():
    r = lax.broadcasted_iota(jnp.int32, (LANES, LANES), 0) // HEAD_DIM
    c = lax.broadcasted_iota(jnp.int32, (LANES, LANES), 1) // HEAD_DIM
    return (r == c).astype(BF)


def _head_mean(v, ones_blockdiag):
    hi = v.astype(BF)
    lo = (v - hi.astype(F32)).astype(BF)
    s = jnp.dot(hi, ones_blockdiag, preferred_element_type=F32) + jnp.dot(lo, ones_blockdiag, preferred_element_type=F32)
    return s * (1.0 / HEAD_DIM)


def _headnorm_fwd(name, proj, col0, width, g2):
    T = proj.shape[0]
    wide = min(width, GROUP_WIDTH)
    nb, off = width // wide, col0 // wide

    def body(x_ref, g_ref, b_ref, o_ref):
        for s in range(wide // LANES):
            lanes = slice(LANES * s, LANES * (s + 1))
            xv = x_ref[:, lanes].astype(F32)
            r = lax.rsqrt(_head_mean(xv * xv, b_ref[...]) + RMS_EPS)
            o_ref[:, lanes] = (xv * r * g_ref[...]).astype(BF)

    return pl.pallas_call(
        body, name=name, grid=(T // HEADNORM_TILE, nb),
        in_specs=[pl.BlockSpec((HEADNORM_TILE, wide), lambda i, j: (i, j + off)),
                  pl.BlockSpec((1, LANES), lambda i, j: (0, 0)), pl.BlockSpec((LANES, LANES), lambda i, j: (0, 0))],
        out_specs=pl.BlockSpec((HEADNORM_TILE, wide), lambda i, j: (i, j)),
        out_shape=jax.ShapeDtypeStruct((T, width), BF), compiler_params=_params(("parallel", "parallel")),
    )(proj, g2, _half_sum_matrix())


def _headnorm_bwd(name, dy, proj, col0, width, g2, into):
    T = proj.shape[0]
    wide = min(width, GROUP_WIDTH)
    nb, off = width // wide, col0 // wide

    def body(dy_ref, x_ref, g_ref, b_ref, into_ref, dx_ref, dg_ref):
        for s in range(wide // LANES):
            lanes = slice(LANES * s, LANES * (s + 1))
            xv = x_ref[:, lanes].astype(F32)
            dyv = dy_ref[:, lanes].astype(F32)
            r = lax.rsqrt(_head_mean(xv * xv, b_ref[...]) + RMS_EPS)
            xhat = xv * r
            dxhat = dyv * g_ref[...]
            dx_ref[:, lanes] = (r * (dxhat - xhat * _head_mean(dxhat * xhat, b_ref[...]))).astype(BF)
            dg_ref[0, :, lanes] = jnp.sum(dyv * xhat, axis=0, keepdims=True)

    return pl.pallas_call(
        body, name=name, grid=(T // HEADNORM_TILE, nb),
        in_specs=[pl.BlockSpec((HEADNORM_TILE, wide), lambda i, j: (i, j)),
                  pl.BlockSpec((HEADNORM_TILE, wide), lambda i, j: (i, j + off)),
                  pl.BlockSpec((1, LANES), lambda i, j: (0, 0)), pl.BlockSpec((LANES, LANES), lambda i, j: (0, 0)), ANY],
        out_specs=[pl.BlockSpec((HEADNORM_TILE, wide), lambda i, j: (i, j + off)),
                   pl.BlockSpec((1, 1, wide), lambda i, j: (i, 0, j))],
        out_shape=[jax.ShapeDtypeStruct(into.shape, BF), jax.ShapeDtypeStruct((T // HEADNORM_TILE, 1, width), F32)],
        input_output_aliases={4: 0}, compiler_params=_params(("parallel", "parallel")),
    )(dy, proj, g2, _half_sum_matrix(), into)


def _shift_down(v, k, row):
    return jnp.where(row >= k, pltpu.roll(v, k, axis=0), 0.0)


def _shift_up(v, k, row, T):
    return jnp.where(row < T - k, pltpu.roll(v, T - k, axis=0), 0.0)


def _by_group(g, vals):
    out = vals[-1]
    for i in range(len(vals) - 2, -1, -1):
        out = jnp.where(g == i, vals[i], out)
    return out


def _pool_fwd(name, proj, pool_w, pool_scale):
    T = proj.shape[0]

    def body(x_ref, w_ref, s_ref, pooled_ref, mixed_ref):
        g = pl.program_id(0)
        xv = x_ref[...].astype(F32)
        row = lax.broadcasted_iota(jnp.int32, (T, 1), 0)
        s2 = xv + _shift_down(xv, 1, row)
        s4 = s2 + _shift_down(s2, 2, row)
        s8 = s4 + _shift_down(s4, 4, row)
        s16 = s8 + _shift_down(s8, 8, row)
        wsum = _by_group(g, [s2, s4, s8, s16])
        count = jnp.minimum(row + 1, 2 << g).astype(F32)
        pooled = (wsum / count - xv).astype(BF)
        pooled_ref[...] = pooled
        mixed = jnp.dot(pooled, w_ref[0].astype(BF), preferred_element_type=F32) * s_ref[...]
        mixed_ref[...] = mixed.astype(BF)

    col = pl.BlockSpec((T, POOL_GROUP), lambda g: (0, g))
    return pl.pallas_call(
        body, name=name, grid=(N_POOL_GROUPS,),
        in_specs=[col, pl.BlockSpec((1, POOL_GROUP, POOL_GROUP), lambda g: (g, 0, 0)),
                  pl.BlockSpec((1, POOL_GROUP), lambda g: (0, g))],
        out_specs=[col, col],
        out_shape=[jax.ShapeDtypeStruct((T, POOL_WIDTH), BF), jax.ShapeDtypeStruct((T, POOL_WIDTH), BF)],
        compiler_params=_params(("parallel",)),
    )(proj, pool_w, pool_scale)


def _pool_bwd(name, dmixed, pooled, pool_w, pool_scale, into):
    T = dmixed.shape[0]

    def body(dm_ref, p_ref, w_ref, s_ref, into_ref, dx_ref, dw_ref, ds_ref):
        g = pl.program_id(0)
        dm = dm_ref[...].astype(F32)
        pooled = p_ref[...]
        w = w_ref[0].astype(BF)
        pre = jnp.dot(pooled, w, preferred_element_type=F32)
        ds_ref[...] = jnp.sum(dm * pre, axis=0, keepdims=True)
        dms = (dm * s_ref[...]).astype(BF)
        dw_ref[0] = lax.dot_general(pooled, dms, _DIMS["tn"], preferred_element_type=F32)
        dpooled = lax.dot_general(dms, w, _DIMS["nt"], preferred_element_type=F32)
        row = lax.broadcasted_iota(jnp.int32, (T, 1), 0)
        count = jnp.minimum(row + 1, 2 << g).astype(F32)
        z = dpooled / count
        l2 = z + _shift_up(z, 1, row, T)
        l4 = l2 + _shift_up(l2, 2, row, T)
        l8 = l4 + _shift_up(l4, 4, row, T)
        l16 = l8 + _shift_up(l8, 8, row, T)
        dx_ref[...] = (_by_group(g, [l2, l4, l8, l16]) - dpooled).astype(BF)

    col = pl.BlockSpec((T, POOL_GROUP), lambda g: (0, g))
    wspec = pl.BlockSpec((1, POOL_GROUP, POOL_GROUP), lambda g: (g, 0, 0))
    sspec = pl.BlockSpec((1, POOL_GROUP), lambda g: (0, g))
    return pl.pallas_call(
        body, name=name, grid=(N_POOL_GROUPS,), in_specs=[col, col, wspec, sspec, ANY], out_specs=[col, wspec, sspec],
        out_shape=[jax.ShapeDtypeStruct(into.shape, BF),
                   jax.ShapeDtypeStruct((N_POOL_GROUPS, POOL_GROUP, POOL_GROUP), F32),
                   jax.ShapeDtypeStruct((1, POOL_WIDTH), F32)],
        input_output_aliases={4: 0}, compiler_params=_params(("parallel",)),
    )(dmixed, pooled, pool_w, pool_scale, into)


ATTN_SCALE = HEAD_DIM ** -0.5
MASKED = float(jnp.finfo(jnp.float32).min)
KV_COL_BLOCK_V = COL_V // LANES
GROUP_WIDTH = GQA_GROUP * HEAD_DIM


def _dup_head(v, j):
    half = lax.broadcasted_iota(jnp.int32, (1, LANES), 1) // HEAD_DIM
    return jnp.where(half == j, v, pltpu.roll(v, HEAD_DIM, axis=1))


def _stack_heads(v, low):
    pieces = []
    for p in range(GROUP_WIDTH // LANES):
        vp = v[:, LANES * p: LANES * (p + 1)]
        pieces.append(jnp.where(low, vp, jnp.zeros_like(vp)))
        pieces.append(jnp.where(low, jnp.zeros_like(vp), vp))
    return jnp.concatenate(pieces, axis=0)


def _unstack_transposed(t, low):
    pairs = []
    for p in range(GROUP_WIDTH // LANES):
        even = t[:, BLOCK * (2 * p): BLOCK * (2 * p + 1)].T
        odd = t[:, BLOCK * (2 * p + 1): BLOCK * (2 * p + 2)].T
        pairs.append(jnp.where(low, even, odd))
    return pairs


STACKED = GQA_GROUP * BLOCK


def _band_bias():
    key = lax.broadcasted_iota(jnp.int32, (2, 2 * BLOCK, STACKED), 1)
    qry = lax.broadcasted_iota(jnp.int32, (2, 2 * BLOCK, STACKED), 2) % BLOCK
    first = lax.broadcasted_iota(jnp.int32, (2, 2 * BLOCK, STACKED), 0) == 0
    valid = (key > qry) & (key <= qry + BLOCK) & (jnp.logical_not(first) | (key >= BLOCK))
    return jnp.where(valid, 0.0, MASKED).astype(F32)


BIAS_SPEC = pl.BlockSpec((1, 2 * BLOCK, STACKED), lambda n: (jnp.minimum(n, 1), 0, 0))


def _softmax_keys_on_sublanes(k2, q, bias, sink_ref, j):
    head_of_lane = lax.broadcasted_iota(jnp.int32, (1, STACKED), 1) // BLOCK
    sink = jnp.zeros((1, STACKED), F32)
    for h in range(GQA_GROUP):
        sink = jnp.where(head_of_lane == h, sink_ref[j * GQA_GROUP + h], sink)
    s = lax.dot_general(k2, q, _DIMS["nt"], preferred_element_type=F32) + bias
    m = jnp.maximum(jnp.max(s, axis=0, keepdims=True), sink)
    e = jnp.exp(s - m)
    e_sink = jnp.exp(sink - m)
    inv = 1.0 / (jnp.sum(e, axis=0, keepdims=True) + e_sink)
    return e * inv, e_sink * inv


def _attn_fwd(name, qn, kn, proj, sinks, comm=None):
    T = qn.shape[0]
    nb = T // BLOCK
    plumb = _CommPlumbing(comm)

    def body(sink_ref, bias_ref, q_ref, kp_ref, kc_ref, vp_ref, vc_ref, *rest):
        c_in, o_ref = rest[:plumb.n_in], rest[plumb.n_in]
        c_out, c_scr = rest[plumb.n_in + 1: plumb.n_in + 1 + plumb.n_out], rest[plumb.n_in + 1 + plumb.n_out:]
        m = pl.program_id(0)
        plumb.handshake(m == 0)
        plumb.run(m, nb // 2, True, c_in, c_out, c_scr)
        low = lax.broadcasted_iota(jnp.int32, (1, LANES), 1) < HEAD_DIM
        k_pair, v_pair = kc_ref[...], vc_ref[...]
        for b in range(2):
            rows = slice(BLOCK * b, BLOCK * (b + 1))
            kk = k_pair if b else jnp.concatenate([kp_ref[...], k_pair[0:BLOCK]], axis=0)
            vv = v_pair if b else jnp.concatenate([vp_ref[...], v_pair[0:BLOCK]], axis=0)
            bias = bias_ref[1] if b else bias_ref[jnp.minimum(m, 1)]
            for j in range(2):
                q = _stack_heads(q_ref[rows, GROUP_WIDTH * j: GROUP_WIDTH * (j + 1)], low)
                p, _ = _softmax_keys_on_sublanes(_dup_head(kk, j), q, bias, sink_ref, j)
                o_t = lax.dot_general(_dup_head(vv, j), p.astype(BF), _DIMS["tn"], preferred_element_type=F32)
                for pair, o in enumerate(_unstack_transposed(o_t, low)):
                    lanes = slice(GROUP_WIDTH * j + LANES * pair, GROUP_WIDTH * j + LANES * (pair + 1))
                    o_ref[rows, lanes] = o.astype(BF)
        plumb.run(m, nb // 2, False, c_in, c_out, c_scr)

    wide = pl.BlockSpec((2 * BLOCK, ATTN_WIDTH), lambda m: (m, 0))
    before = lambda m: jnp.maximum(2 * m - 1, 0)
    res = pl.pallas_call(
        body, name=name, grid=(nb // 2,),
        in_specs=[pl.BlockSpec(memory_space=pltpu.SMEM),
                  pl.BlockSpec((2, 2 * BLOCK, STACKED), lambda m: (0, 0, 0)), wide,
                  pl.BlockSpec((BLOCK, LANES), lambda m: (before(m), 0)),
                  pl.BlockSpec((2 * BLOCK, LANES), lambda m: (m, 0)),
                  pl.BlockSpec((BLOCK, LANES), lambda m: (before(m), KV_COL_BLOCK_V)),
                  pl.BlockSpec((2 * BLOCK, LANES), lambda m: (m, KV_COL_BLOCK_V))] + [ANY] * plumb.n_in,
        out_specs=[wide] + [ANY] * plumb.n_out,
        out_shape=[jax.ShapeDtypeStruct((T, ATTN_WIDTH), BF)] + plumb.out_shapes, scratch_shapes=plumb.scratch,
        compiler_params=_params(("arbitrary",) if comm else ("parallel",), plumb.collective_id()),
    )(sinks, _band_bias(), qn, kn, kn, proj, proj, *plumb.args)
    return (res[0], plumb.split_outputs(res[1:])) if comm is not None else res[0]


def _attn_bwd(name, dout, qn, kn, proj, sinks, comm):
    T = qn.shape[0]
    nb = T // BLOCK
    plumb = _CommPlumbing(comm)

    def body(sink_ref, bias_ref, do_ref, q_ref, kp_ref, kc_ref, vp_ref, vc_ref, *rest):
        c_in, (dq_ref, dk_ref, dv_ref, dsink_ref) = rest[:plumb.n_in], rest[plumb.n_in: plumb.n_in + 4]
        c_out = rest[plumb.n_in + 4: plumb.n_in + 4 + plumb.n_out]
        carry_k, carry_v, tot_k, tot_v = rest[plumb.n_in + 4 + plumb.n_out: plumb.n_in + 8 + plumb.n_out]
        c_scr = rest[plumb.n_in + 8 + plumb.n_out:]
        n = pl.program_id(0)
        plumb.handshake(n == 0)
        plumb.run(n, nb + 1, True, c_in, c_out, c_scr)
        lane = lax.broadcasted_iota(jnp.int32, (1, LANES), 1)
        low = lane < HEAD_DIM

        @pl.when(n == 0)
        def _():
            carry_k[...] = jnp.zeros_like(carry_k)
            carry_v[...] = jnp.zeros_like(carry_v)
            dsink_ref[...] = jnp.zeros_like(dsink_ref)

        @pl.when(n == nb)
        def _():
            tot_k[...] = jnp.zeros_like(tot_k)
            tot_v[...] = jnp.zeros_like(tot_v)

        @pl.when(n < nb)
        def _():
            kk = jnp.concatenate([kp_ref[...], kc_ref[...]], axis=0)
            vv = jnp.concatenate([vp_ref[...], vc_ref[...]], axis=0)
            dk_tot = jnp.zeros((2 * BLOCK, LANES), F32)
            dv_tot = jnp.zeros((2 * BLOCK, LANES), F32)
            dsink = jnp.zeros((1, LANES), F32)
            for j in range(2):
                k2 = _dup_head(kk, j)
                v2 = _dup_head(vv, j)
                q = _stack_heads(q_ref[:, GROUP_WIDTH * j: GROUP_WIDTH * (j + 1)], low)
                do = _stack_heads(do_ref[:, GROUP_WIDTH * j: GROUP_WIDTH * (j + 1)], low)
                p, psink = _softmax_keys_on_sublanes(k2, q, bias_ref[0], sink_ref, j)
                dp =lax.dot_general(v2, do, _DIMS["nt"], preferred_element_type=F32)
                delta = jnp.sum(p * dp, axis=0, keepdims=True)
                ds = (p * (dp - delta)).astype(BF)
                dk2 = jnp.dot(ds, q, preferred_element_type=F32)
                dv2 = jnp.dot(p.astype(BF), do, preferred_element_type=F32)
                dq_t = lax.dot_general(k2, ds, _DIMS["tn"], preferred_element_type=F32)
                for pair, dq in enumerate(_unstack_transposed(dq_t, low)):
                    lanes = slice(GROUP_WIDTH * j + LANES * pair, GROUP_WIDTH * j + LANES * (pair + 1))
                    dq_ref[:, lanes] = dq.astype(BF)
                mine = low if j == 0 else jnp.logical_not(low)
                dk_tot = dk_tot + jnp.where(mine, dk2 + pltpu.roll(dk2, HEAD_DIM, axis=1), 0.0)
                dv_tot = dv_tot + jnp.where(mine, dv2 + pltpu.roll(dv2, HEAD_DIM, axis=1), 0.0)
                sink_term = psink * delta
                for h in range(GQA_GROUP):
                    val = -jnp.sum(sink_term[:, BLOCK * h: BLOCK * (h + 1)], axis=1, keepdims=True)
                    dsink = dsink + jnp.where(lane == j * GQA_GROUP + h, val, 0.0)
            tot_k[...] = dk_tot
            tot_v[...] = dv_tot
            dsink_ref[0:1, :] += dsink

        dk_ref[...] = (carry_k[...] + tot_k[0:BLOCK]).astype(BF)
        dv_ref[...] = (carry_v[...] + tot_v[0:BLOCK]).astype(BF)
        carry_k[...] = tot_k[BLOCK:]
        carry_v[...] = tot_v[BLOCK:]
        plumb.run(n, nb + 1, False, c_in, c_out, c_scr)

    cur = lambda n: (jnp.minimum(n, nb - 1), 0)
    prev = lambda n: (jnp.maximum(n - 1, 0), 0)
    wide = pl.BlockSpec((BLOCK, ATTN_WIDTH), cur)
    res = pl.pallas_call(
        body, name=name, grid=(nb + 1,),
        in_specs=[pl.BlockSpec(memory_space=pltpu.SMEM), BIAS_SPEC, wide, wide,
                  pl.BlockSpec((BLOCK, LANES), prev), pl.BlockSpec((BLOCK, LANES), cur),
                  pl.BlockSpec((BLOCK, LANES), lambda n: (jnp.maximum(n - 1, 0), KV_COL_BLOCK_V)),
                  pl.BlockSpec((BLOCK, LANES), lambda n: (jnp.minimum(n, nb - 1), KV_COL_BLOCK_V))] + [ANY] * plumb.n_in,
        out_specs=[wide, pl.BlockSpec((BLOCK, LANES), prev), pl.BlockSpec((BLOCK, LANES), prev),
                   pl.BlockSpec((8, LANES), lambda n: (0, 0))] + [ANY] * plumb.n_out,
        out_shape=[jax.ShapeDtypeStruct((T, ATTN_WIDTH), BF), jax.ShapeDtypeStruct((T, KV_WIDTH), BF),
                   jax.ShapeDtypeStruct((T, KV_WIDTH), BF), jax.ShapeDtypeStruct((8, LANES), F32)] + plumb.out_shapes,
        scratch_shapes=[pltpu.VMEM((BLOCK, LANES), F32), pltpu.VMEM((BLOCK, LANES), F32),
                        pltpu.VMEM((2 * BLOCK, LANES), F32), pltpu.VMEM((2 * BLOCK, LANES), F32)] + plumb.scratch,
        compiler_params=_params(("arbitrary",), plumb.collective_id()),
    )(sinks, _band_bias(), dout, qn, kn, kn, proj, proj, *plumb.args)
    return list(res[:4]), plumb.split_outputs(res[4:])


def _swiglu_fwd_epilogue(accs, ex):
    g, u = accs
    return [g, u, g * jax.nn.sigmoid(g) * u], []


def _swiglu_bwd_epilogue(accs, ex):
    (da,) = accs
    g, u = ex[0].astype(F32), ex[1].astype(F32)
    s = jax.nn.sigmoid(g)
    gs = g * s
    return [da * u * (s + gs - gs * s), da * gs], []


def _residual_norm_epilogue(scale):
    def epilogue(accs, ex):
        res, gain = ex
        h = res + scale * accs[0]
        r = lax.rsqrt(jnp.mean(h * h, axis=-1, keepdims=True) + RMS_EPS)
        return [h, h * r * gain], []
    return epilogue


def _rms_bwd_epilogue(accs, ex):
    (dn,) = accs
    xv, g, dres = ex
    r = lax.rsqrt(jnp.mean(xv * xv, axis=-1, keepdims=True) + RMS_EPS)
    xhat = xv * r
    dxhat = dn * g
    dx = dres + r * (dxhat - xhat * jnp.mean(dxhat * xhat, axis=-1, keepdims=True))
    return [dx, dx], [dn * xhat]


def _loss_epilogue(accs, ex):
    xv, target = ex
    d = xv + 0.5 * accs[0] - target
    dy = d * (1.0 / D_MODEL)
    return [dy, dy], [d * d]


def _merge_fwd_epilogue(accs, ex):
    (ba,) = accs
    bp, gp_pre, ga_pre, bias_p, bias_a = ex
    gp = jax.nn.sigmoid(gp_pre.astype(F32) + bias_p)
    ga = jax.nn.sigmoid(ga_pre.astype(F32) + bias_a)
    return [gp * bp.astype(F32) + ga * ba, ba], []


def _merge_bwd_epilogue(accs, ex):
    (dm,) = accs
    bp, ba, gp_pre, ga_pre, bias_p, bias_a = ex
    gp = jax.nn.sigmoid(gp_pre.astype(F32) + bias_p)
    ga = jax.nn.sigmoid(ga_pre.astype(F32) + bias_a)
    dbp, dba = dm * gp, dm * ga
    dgp = dbp * bp.astype(F32) * (1.0 - gp)
    dga = dba * ba.astype(F32) * (1.0 - ga)
    return [dbp, dba, dgp, dga], [dgp, dga]


def _prep(name, ws, transposes):
    n = len(ws)

    def body(*refs):
        for w_ref, o_ref, tr in zip(refs[:n], refs[n:], transposes):
            v = w_ref[...]
            o_ref[...] = (v.T if tr else v).astype(BF)

    shapes = [jax.ShapeDtypeStruct(w.shape[::-1] if tr else w.shape, BF) for w, tr in zip(ws, transposes)]
    return pl.pallas_call(body, name=name, out_shape=shapes, compiler_params=_params())(*ws)


def _adam_math(w, g, m, v):
    m = ADAM_B1 * m + (1.0 - ADAM_B1) * g
    v = ADAM_B2 * v + (1.0 - ADAM_B2) * jnp.square(g)
    m_hat = m / (1.0 - ADAM_B1 ** ADAM_STEP)
    v_hat = v / (1.0 - ADAM_B2 ** ADAM_STEP)
    delta = -ADAM_LR * (m_hat / (jnp.sqrt(v_hat) + ADAM_EPS) + ADAM_WD * w)
    return delta, m, v


def _adamw_sharded(name, items, transpose=False):
    n = len(items)

    def body(*refs):
        ins, outs = refs[:4 * n], refs[4 * n:]
        for k in range(n):
            s_ref, w_ref, m_ref, v_ref = ins[4 * k: 4 * k + 4]
            g = s_ref[0].astype(F32)
            for i in range(1, 4):
                g = g + s_ref[i].astype(F32)
            if transpose:
                g = g.T
            delta, mn, vn = _adam_math(w_ref[...], g, m_ref[...], v_ref[...])
            for o_ref, val in zip(outs[4 * k: 4 * k + 4], (g, delta, mn, vn)):
                o_ref[...] = val

    flat = [a for item in items for a in item]
    out_shape = [jax.ShapeDtypeStruct(item[1].shape, F32) for item in items for _ in range(4)]
    _, r, C = items[0][0].shape
    rows = r // 4
    if transpose or rows % 8:
        res = pl.pallas_call(body, name=name, out_shape=out_shape, compiler_params=_params())(*flat)
    else:
        tile = pl.BlockSpec((rows, C), lambda i: (i, 0))
        res = pl.pallas_call(
            body, name=name, grid=(4,), in_specs=[pl.BlockSpec((4, rows, C), lambda i: (0, i, 0)), tile, tile, tile] * n,
            out_specs=[tile] * (4 * n), out_shape=out_shape, compiler_params=_params(("parallel",)),
        )(*flat)
    return [tuple(res[4 * k: 4 * k + 4]) for k in range(n)]


SMALL_LAYOUT = (("ffn1_norm", 0, (8, LANES)), ("mix_norm", 8, (8, LANES)), ("ffn2_norm", 16, (8, LANES)),
                ("gate_bias", 24, (16, LANES)), ("pool_scale", 40, (4, LANES)), ("q_norm", 48, (1, HEAD_DIM)),
                ("k_norm", 56, (1, HEAD_DIM)), ("sinks", 64, (1, N_HEADS)))
LOSS_ROW = 72
SMALL_ROWS = 80


def _adamw_small(name, g_vec, g_pool_w, params):
    n = len(SMALL_LAYOUT) + 1

    def body(vec_ref, pw_ref, *refs):
        ins, outs = refs[:3 * n], refs[3 * n:]
        vec = vec_ref[0]
        pw = pw_ref[0]
        for i in range(1, N_DEV):
            vec = vec + vec_ref[i]
            pw = pw + pw_ref[i]
        grads = [vec[r0:r0 + shape[0], 0:shape[1]] for _, r0, shape in SMALL_LAYOUT] + [pw]
        for p, g in enumerate(grads):
            w_ref, m_ref, v_ref = ins[3 * p: 3 * p + 3]
            delta, mn, vn = _adam_math(w_ref[...], g, m_ref[...], v_ref[...])
            for o_ref, val in zip(outs[4 * p: 4 * p + 4], (g, delta, mn, vn)):
                o_ref[...] = val
        outs[4 * n][...] = vec[LOSS_ROW:LOSS_ROW + 1, :]

    flat = [a for wmv in params for a in wmv]
    out_shape = [jax.ShapeDtypeStruct(wmv[0].shape, F32) for wmv in params for _ in range(4)]
    out_shape.append(jax.ShapeDtypeStruct((1, LANES), F32))
    res = pl.pallas_call(body, name=name, out_shape=out_shape, compiler_params=_params())(g_vec, g_pool_w, *flat)
    return [tuple(res[4 * p: 4 * p + 4]) for p in range(n)], res[4 * n]


def _place():
    x, y, c = lax.axis_index("x"), lax.axis_index("y"), lax.axis_index("c")
    other_chips = [(1 - x, y), (x, 1 - y), (1 - x, 1 - y)]
    return x, y, c, other_chips


def _rows(ref, r, place, natural=False):
    px, py, pc = place
    b = 4 * px + 2 * py + pc if natural else 4 * pc + 2 * px + py
    return ref.at[pl.ds(pl.multiple_of(b * r, 8), r), :]


def _gather_task(shards, natural=(), forward_at=0.75):
    n = len(shards)
    rs = [s.shape[0] for s in shards]
    rows_of = lambda ref, k, place: _rows(ref, rs[k], place, k in natural)

    def copy(scr, outs, k, slot, block, to, src=None):
        rows = rows_of(outs[k], k, block)
        return pltpu.make_async_remote_copy(
            src_ref=rows if src is None else src, dst_ref=rows, send_sem=scr[0].at[7 * k + slot],
            recv_sem=scr[1].at[7 * k + slot], device_id=to, device_id_type=MESH)

    def first_sends(ins, outs, scr):
        x, y, c, chips = _place()
        me = (x, y, c)
        cps = [copy(scr, outs, k, 1 + j, me, (*chip, c), src=ins[k]) for j, chip in enumerate(chips) for k in range(n)]
        return cps + [copy(scr, outs, k, 0, me, (x, y, 1 - c), src=ins[k]) for k in range(n)]

    def passed_on(outs, scr):
        x, y, c, chips = _place()
        return [copy(scr, outs, k, 4 + j, (*chip, c), (x, y, 1 - c)) for j, chip in enumerate(chips) for k in range(n)]

    def local(ins, outs, scr):
        x, y, c, _ = _place()
        return [pltpu.make_async_copy(ins[k], rows_of(outs[k], k, (x, y, c)), scr[2].at[k]) for k in range(n)]

    def start(ins, outs, scr):
        for cp in local(ins, outs, scr) + first_sends(ins, outs, scr):
            cp.start()

    def forward(ins, outs, scr):
        x, y, c, chips = _place()
        for j, chip in enumerate(chips):
            for k in range(n):
                copy(scr, outs, k, 1 + j, (*chip, c), (x, y, c)).wait_recv()
                copy(scr, outs, k, 4 + j, (*chip, c), (x, y, 1 - c)).start()

    def finish(ins, outs, scr):
        x, y, c, chips = _place()
        for k in range(n):
            copy(scr, outs, k, 0, (x, y, 1 - c), (x, y, c)).wait_recv()
        for j, chip in enumerate(chips):
            for k in range(n):
                copy(scr, outs, k, 4 + j, (*chip, 1 - c), (x, y, c)).wait_recv()
        for cp in first_sends(ins, outs, scr) + passed_on(outs, scr):
            cp.wait_send()
        for cp in local(ins, outs, scr):
            cp.wait()

    out_shapes = [jax.ShapeDtypeStruct((N_DEV * s.shape[0], s.shape[1]), s.dtype) for s in shards]
    scratch = [pltpu.SemaphoreType.DMA((7 * n,)), pltpu.SemaphoreType.DMA((7 * n,)), pltpu.SemaphoreType.DMA((n,))]
    return _Task(shards, out_shapes, scratch, [(0, start), (forward_at, forward), (1.0, finish)], ("sibling", "chips"))


def _direct_gather_task(shards):
    n = len(shards)
    rs = [s.shape[0] for s in shards]

    def peers():
        x, y, c, _ = _place()
        flip = lambda v, bit: 1 - v if bit else v
        return (x, y, c), [(flip(x, (s >> 2) & 1), flip(y, (s >> 1) & 1), flip(c, s & 1)) for s in range(1, N_DEV)]

    def copies(ins, outs, scr):
        me, others = peers()
        local = [pltpu.make_async_copy(ins[k], _rows(outs[k], rs[k], me), scr[2].at[k]) for k in range(n)]
        sems = lambda k, s: dict(send_sem=scr[0].at[7 * k + s], recv_sem=scr[1].at[7 * k + s], device_id_type=MESH)
        sends = [pltpu.make_async_remote_copy(src_ref=ins[k], dst_ref=_rows(outs[k], rs[k], me), device_id=to, **sems(k, s))
                 for s, to in enumerate(others) for k in range(n)]
        recvs = [pltpu.make_async_remote_copy(src_ref=_rows(outs[k], rs[k], frm), dst_ref=_rows(outs[k], rs[k], frm),
                                              device_id=me, **sems(k, s))
                 for s, frm in enumerate(others) for k in range(n)]
        return local, sends, recvs

    def start(ins, outs, scr):
        local, sends, _ = copies(ins, outs, scr)
        for cp in local + sends:
            cp.start()

    def finish(ins, outs, scr):
        local, sends, recvs = copies(ins, outs, scr)
        for cp in recvs:
            cp.wait_recv()
        for cp in sends:
            cp.wait_send()
        for cp in local:
            cp.wait()

    out_shapes = [jax.ShapeDtypeStruct((N_DEV * s.shape[0], s.shape[1]), s.dtype) for s in shards]
    scratch = [pltpu.SemaphoreType.DMA((7 * n,)), pltpu.SemaphoreType.DMA((7 * n,)), pltpu.SemaphoreType.DMA((n,))]
    return _Task(shards, out_shapes, scratch, [(0, start), (1.0, finish)], ("all",))


def _chip_task(sums):
    n = len(sums)
    rs = [s.shape[0] // 4 for s in sums]

    def block(ref, k, chip_index):
        return ref.at[pl.ds(pl.multiple_of(chip_index * rs[k], 8), rs[k]), :]

    def copies(ins, outs, scr):
        send_sems, recv_sems, local_sems = scr
        x, y, c, chips = _place()
        here = 2 * x + y
        local = [pltpu.make_async_copy(block(ins[k], k, here), outs[k].at[here], local_sems.at[k]) for k in range(n)]
        remote = []
        for j, (px, py) in enumerate(chips):
            remote += [pltpu.make_async_remote_copy(
                src_ref=block(ins[k], k, 2 * px + py), dst_ref=outs[k].at[here],
                send_sem=send_sems.at[3 * k + j], recv_sem=recv_sems.at[3 * k + j],
                device_id=(px, py, c), device_id_type=MESH) for k in range(n)]
        return local, remote

    def start(ins, outs, scr):
        local, remote = copies(ins, outs, scr)
        for cp in local + remote:
            cp.start()

    def finish(ins, outs, scr):
        local, remote = copies(ins, outs, scr)
        for cp in remote:
            cp.wait()
        for cp in local:
            cp.wait()

    out_shapes = [jax.ShapeDtypeStruct((4, r, s.shape[1]), s.dtype) for r, s in zip(rs, sums)]
    scratch = [pltpu.SemaphoreType.DMA((3 * n,)), pltpu.SemaphoreType.DMA((3 * n,)), pltpu.SemaphoreType.DMA((n,))]
    return _Task(sums, out_shapes, scratch, [(0, start), (1.0, finish)], ("chips",))


def _dw_pair(name, a, b, scale, comm=None, blocks=1):
    T, M = a.shape
    N = b.shape[1]
    half = M // 2
    wide = half // blocks
    tk = min(2048, T)
    nK = T // tk
    plumb = _CommPlumbing(comm)

    def body(core_ref, *rest):
        a_refs, b_ref, rest = rest[:blocks], rest[blocks], rest[blocks + 1:]
        c_in = rest[:plumb.n_in]
        o_ref = rest[plumb.n_in]
        c_out = rest[plumb.n_in + 1: plumb.n_in + 1 + plumb.n_out]
        acc, stage, land, send_sem, recv_sem = rest[plumb.n_in + 1 + plumb.n_out: plumb.n_in + 6 + plumb.n_out]
        c_scr = rest[plumb.n_in + 6 + plumb.n_out:]
        i, k = pl.program_id(0), pl.program_id(1)
        x, y, c, _ = _place()
        push = pltpu.make_async_remote_copy(src_ref=stage, dst_ref=land, send_sem=send_sem, recv_sem=recv_sem,
                                            device_id=(x, y, 1 - c), device_id_type=MESH)
        plumb.handshake((i == 0) & (k == 0), own=("sibling",))
        if comm:
            plumb.run(i * nK + k, 2 * nK, True, c_in, c_out, c_scr)

        av = a_refs[0][...] if blocks == 1 else jnp.concatenate([r[...] for r in a_refs], axis=1)
        p = lax.dot_general(av, b_ref[...], _DIMS["tn"], preferred_element_type=F32)

        @pl.when(k == 0)
        def _():
            acc[...] = p

        @pl.when(k > 0)
        def _():
            acc[...] += p

        @pl.when((i == 0) & (k == nK - 1))
        def _():
            stage[...] = (scale * acc[...]).astype(BF)
            push.start()

        @pl.when((i == 1) & (k == nK - 1))
        def _():
            push.wait_recv()
            o_ref[...] = (scale * acc[...] + land[...].astype(F32)).astype(BF)
            push.wait_send()

        if comm:
            plumb.run(i * nK + k, 2 * nK, False, c_in, c_out, c_scr)

    grid_spec = pltpu.PrefetchScalarGridSpec(
        num_scalar_prefetch=1, grid=(2, nK),
        in_specs=[pl.BlockSpec((tk, wide), functools.partial(
            lambda i, k, core, j: (k, (2 * j if blocks > 1 else 0) + jnp.where(i == 0, 1 - core[0], core[0])), j=j))
            for j in range(blocks)] + [pl.BlockSpec((tk, N), lambda i, k, core: (k, 0))] + [ANY] * plumb.n_in,
        out_specs=[pl.BlockSpec((half, N), lambda i, k, core: (0, 0))] + [ANY] * plumb.n_out,
        scratch_shapes=[pltpu.VMEM((half, N), F32), pltpu.VMEM((half, N), BF), pltpu.VMEM((half, N), BF),
                        pltpu.SemaphoreType.DMA, pltpu.SemaphoreType.DMA] + plumb.scratch)
    core = lax.axis_index("c").astype(jnp.int32).reshape(1)
    res = pl.pallas_call(
        body, name=name, grid_spec=grid_spec,
        out_shape=[jax.ShapeDtypeStruct((half, N), BF)] + plumb.out_shapes,
        compiler_params=_params(("arbitrary", "arbitrary"), plumb.collective_id(own=("sibling",))),
    )(core, *([a] * blocks), b, *plumb.args)
    return (res[0], plumb.split_outputs(res[1:])) if comm else res[0]


def _pair_task(parts):
    n = len(parts)

    def copies(ins, outs, scr):
        x, y, c, _ = _place()
        return [pltpu.make_async_remote_copy(
            src_ref=ins[k].at[:, pl.ds(1 - c, 1)], dst_ref=outs[k], send_sem=scr[0].at[k], recv_sem=scr[1].at[k],
            device_id=(x, y, 1 - c), device_id_type=MESH) for k in range(n)]

    def start(ins, outs, scr):
        for cp in copies(ins, outs, scr):
            cp.start()

    def finish(ins, outs, scr):
        for cp in copies(ins, outs, scr):
            cp.wait()

    out_shapes = [jax.ShapeDtypeStruct((4, 1) + p.shape[2:], p.dtype) for p in parts]
    scratch = [pltpu.SemaphoreType.DMA((n,)), pltpu.SemaphoreType.DMA((n,))]
    return _Task(parts, out_shapes, scratch, [(0, start), (1.0, finish)], ("sibling",))


def _pair_sum(name, part, got, core):
    _, _, r, C = part.shape

    def body(core_ref, p_ref, g_ref, o_ref):
        o_ref[0] = (p_ref[0, 0].astype(F32) + g_ref[0, 0].astype(F32)).astype(o_ref.dtype)

    return pl.pallas_call(
        body, name=name,
        grid_spec=pltpu.PrefetchScalarGridSpec(
            num_scalar_prefetch=1, grid=(4,),
            in_specs=[pl.BlockSpec((1, 1, r, C), lambda i, core_ref: (i, core_ref[0], 0, 0)),
                      pl.BlockSpec((1, 1, r, C), lambda i, core_ref: (i, 0, 0, 0))],
            out_specs=pl.BlockSpec((1, r, C), lambda i, core_ref: (i, 0, 0))),
        out_shape=jax.ShapeDtypeStruct((4, r, C), part.dtype), compiler_params=_params(("parallel",)),
    )(core, part, got)


def _ffn_bwd(tag, dy, dyb, x, gain, wgT, wuT, wd, saved, earlier=None):
    n, g, u, a = saved
    half = lambda accs, ex: _swiglu_bwd_epilogue([0.5 * accs[0]], ex)
    act_args = dict(tm=1024, tn=1408, tk=D_MODEL, epilogue=half, extras=[(g, "tile", 0), (u, "tile", 0)], cols_outer=True)
    if earlier is None:
        sum_d = _dw_pair(tag + "_dw_down", a, dyb, 0.5)
        (dg, du), ((slots_d,),) = _mm(tag + "_d_act", [(dyb, wd, "nt", 0)], [BF, BF], comm=[_chip_task([sum_d])], **act_args)
        slots_e = None
        sum_g = _dw_pair(tag + "_dw_gate", dg, n, 1.0)
    else:
        sum_d, ((got,),) = _dw_pair(tag + "_dw_down", a, dyb, 0.5, comm=[_pair_task([earlier])])
        core = lax.axis_index("c").astype(jnp.int32).reshape(1)
        sum_e = _pair_sum(tag + "_pair_sum_earlier", earlier, got, core)
        sum_e = sum_e.reshape(4 * sum_e.shape[1], sum_e.shape[2])
        (dg, du), ((slots_e,),) = _mm(tag + "_d_act", [(dyb, wd, "nt", 0)], [BF, BF], comm=[_chip_task([sum_e])], **act_args)
        sum_g, ((slots_d,),) = _dw_pair(tag + "_dw_gate", dg, n, 1.0, comm=[_chip_task([sum_d])])
    sum_u, ((slots_g,),) = _dw_pair(tag + "_dw_up", du, n, 1.0, comm=[_chip_task([sum_g])])
    (dx, dxb, dgain), ((slots_u,),) = _mm(
        tag + "_d_norm", [(dg, wgT, "nn", 0), (du, wuT, "nn", 0)], [F32, BF], tm=512, tn=D_MODEL, tk=D_FF,
        epilogue=_rms_bwd_epilogue, extras=[(x, "tile", 0), (gain, "row", 0), (dy, "tile", 0)], n_colsum=1,
        comm=[_chip_task([sum_u])])
    return dx, dxb, dgain, slots_e, slots_g, slots_u, slots_d


def _tile_gain(g):
    return jnp.concatenate([g, g]).reshape(1, LANES)


def _fold_heads(partials):
    return jnp.sum(partials.reshape(-1, HEAD_DIM), axis=0)


def _pack_small_grads(grads, loss_local):
    pieces, row = [], 0
    for name, r0, _ in SMALL_LAYOUT + (("loss", LOSS_ROW, None),):
        v = (loss_local if name == "loss" else grads[name]).reshape(-1)
        rows = -(-v.size // LANES)
        block = jnp.pad(v, (0, rows * LANES - v.size)).reshape(rows, LANES)
        pieces += [jnp.zeros((r0 - row, LANES), F32)] * (r0 > row) + [block]
        row = r0 + rows
    pieces.append(jnp.zeros((SMALL_ROWS - row, LANES), F32))
    return jnp.concatenate(pieces, axis=0)


def kernel(x, ffn1_norm, ffn1_w_gate, ffn1_w_up, ffn1_w_down, mix_norm, w_in, pool_w, pool_scale, w_pool_out, q_norm, k_norm, sinks, w_attn_out, gate_bias, w_out, ffn2_norm, ffn2_w_gate, ffn2_w_up, ffn2_w_down, loss_target, m_ffn1_norm, m_ffn1_w_gate, m_ffn1_w_up, m_ffn1_w_down, m_mix_norm, m_w_in, m_pool_w, m_pool_scale, m_w_pool_out, m_q_norm, m_k_norm, m_sinks, m_w_attn_out, m_gate_bias, m_w_out, m_ffn2_norm, m_ffn2_w_gate, m_ffn2_w_up, m_ffn2_w_down, v_ffn1_norm, v_ffn1_w_gate, v_ffn1_w_up, v_ffn1_w_down, v_mix_norm, v_w_in, v_pool_w, v_pool_scale, v_w_pool_out, v_q_norm, v_k_norm, v_sinks, v_w_attn_out, v_gate_bias, v_w_out, v_ffn2_norm, v_ffn2_w_gate, v_ffn2_w_up, v_ffn2_w_down):
    T = x.shape[1]
    x2 = x.reshape(T, D_MODEL)
    target = loss_target.reshape(T, D_MODEL)

    big = [
        ("ffn1_w_gate", ffn1_w_gate, m_ffn1_w_gate, v_ffn1_w_gate, True, False),
        ("ffn1_w_up", ffn1_w_up, m_ffn1_w_up, v_ffn1_w_up, True, False),
        ("ffn1_w_down", ffn1_w_down, m_ffn1_w_down, v_ffn1_w_down, False, False),
        ("w_in", w_in, m_w_in, v_w_in, True, False),
        ("w_pool_out", w_pool_out, m_w_pool_out, v_w_pool_out, False, True),
        ("w_attn_out", w_attn_out, m_w_attn_out, v_w_attn_out, False, False),
        ("w_out", w_out, m_w_out, v_w_out, False, False),
        ("ffn2_w_gate", ffn2_w_gate, m_ffn2_w_gate, v_ffn2_w_gate, True, False),
        ("ffn2_w_up", ffn2_w_up, m_ffn2_w_up, v_ffn2_w_up, True, False),
        ("ffn2_w_down", ffn2_w_down, m_ffn2_w_down, v_ffn2_w_down, False, False),
    ]
    view = lambda a, tv: a.T if tv else a
    views = [view(w, tv) for _, w, _, _, tv, _ in big]
    in_kernel_t = [tk_ for *_, tk_ in big]
    first_shards = _prep("prep_ffn1_gate_up", views[0:2], in_kernel_t[0:2])
    g1 = ffn1_norm.reshape(1, D_MODEL)
    g2 = mix_norm.reshape(1, D_MODEL)
    g3 = ffn2_norm.reshape(1, D_MODEL)
    bias_row = gate_bias.reshape(1, 2 * D_MODEL)
    qg, kg = _tile_gain(q_norm) * ATTN_SCALE, _tile_gain(k_norm)
    scale_row = pool_scale.reshape(1, POOL_WIDTH)

    n1, later_shards, ((wg1T, wu1T),) = _rms_fwd(
        "ffn1_norm", x2, g1, [_gather_task(first_shards, forward_at=0.9)], views[2:], in_kernel_t[2:])
    shards = list(first_shards) + later_shards
    (gt1, up1, act1), ((wd1,), (w_inT,)) = _mm(
        "ffn1_gate_up", [(n1, wg1T, "nt", 0), (n1, wu1T, "nt", 1)], [BF, BF, BF], tm=1024, tn=1408, tk=D_MODEL,
        epilogue=_swiglu_fwd_epilogue, cols_outer=True,
        comm=[_gather_task(shards[2:3], forward_at=0.5), _gather_task(shards[3:4], natural=(0,), forward_at=0.9)])
    (h1, u), ((w_poT, w_ao, w_o),) = _mm(
        "ffn1_down", [(act1, wd1, "nn", 0)], [F32, BF], tm=512, tn=D_MODEL, tk=D_FF,
        epilogue=_residual_norm_epilogue(0.5), extras=[(x2, "tile", 0), (g2, "row", 0)],
        comm=[_gather_task(shards[4:7], natural=(0, 1, 2), forward_at=0.8)])
    saved1 = (n1, gt1, up1, act1)
    (proj,), ((wg2T,),) = _mm(
        "in_proj", [(u, w_inT, "nt", 0)], [BF], tm=1024, tn=1280, tk=D_MODEL, cols_outer=True,
        comm=[_gather_task(shards[7:8], forward_at=0.8)])
    pooled, mixed = _pool_fwd("pool_fwd", proj, pool_w, scale_row)
    qn = _headnorm_fwd("q_norm", proj, COL_Q, ATTN_WIDTH, qg)
    kn = _headnorm_fwd("k_norm", proj, COL_K, KV_WIDTH, kg)
    attn, ((wu2T,),) = _attn_fwd("attn_fwd", qn, kn, proj, sinks, comm=[_gather_task(shards[8:9], forward_at=0.8)])
    (bp,) = _mm("pool_out", [(mixed, w_poT, "nt", 0)], [BF], tm=1024, tn=D_MODEL, tk=POOL_WIDTH)
    gate_tn = 256
    gate_extras = [(proj, "tile", COL_GP // gate_tn), (proj, "tile", COL_GA // gate_tn),
                   (bias_row, "row", 0), (bias_row, "row", D_MODEL // gate_tn)]
    merged, ba = _mm("attn_out_merge", [(attn, w_ao, "nn", 0)], [BF, BF], tm=2048, tn=gate_tn, tk=ATTN_WIDTH,
                     epilogue=_merge_fwd_epilogue, extras=[(bp, "tile", 0)] + gate_extras)
    h2, n2 = _mm("mix_out", [(merged, w_o, "nn", 0)], [F32, BF], tm=1024, tn=D_MODEL, tk=D_MODEL,
                 epilogue=_residual_norm_epilogue(1.0), extras=[(h1, "tile", 0), (g3, "row", 0)])
    (gt2, up2, act2), ((wd2,),) = _mm(
        "ffn2_gate_up", [(n2, wg2T, "nt", 0), (n2, wu2T, "nt", 1)], [BF, BF, BF], tm=1024, tn=1408, tk=D_MODEL,
        epilogue=_swiglu_fwd_epilogue, cols_outer=True, comm=[_gather_task(shards[9:10], forward_at=0.8)])
    dy, dyb, sq = _mm("ffn2_down_loss", [(act2, wd2, "nn", 0)], [F32, BF], tm=512, tn=D_MODEL, tk=D_FF,
                      epilogue=_loss_epilogue, extras=[(h2, "tile", 0), (target, "tile", 0)], n_colsum=1)
    loss_local = 0.5 * jnp.sum(sq) / D_MODEL

    dh2, dh2b, dg3, _, slots_g2, slots_u2, slots_d2 = _ffn_bwd(
        "ffn2", dy, dyb, h2, g3, wg2T, wu2T, wd2, (n2, gt2, up2, act2))
    dbp, dba, dproj, dga, cs_gp, cs_ga = _mm(
        "mix_out_bwd", [(dh2b, w_o, "nt", 0)], [BF, BF, BF, BF], tm=2048, tn=gate_tn, tk=D_MODEL,
        epilogue=_merge_bwd_epilogue, extras=[(bp, "tile", 0), (ba, "tile", 0)] + gate_extras, n_colsum=2,
        out_placement={2: (IN_WIDTH, COL_GP)})
    sum_o = _dw_pair("dw_out", merged, dh2b, 1.0, blocks=4)
    (dmixed,) = _mm("pool_out_bwd", [(dbp, w_poT, "nn", 0)], [BF], tm=1024, tn=POOL_WIDTH, tk=D_MODEL)
    sum_po = _dw_pair("dw_pool_out", dbp, mixed, 1.0, blocks=4)
    (dattn,) = _mm("attn_out_bwd", [(dba, w_ao, "nt", 0)], [BF], tm=1024, tn=ATTN_WIDTH, tk=D_MODEL)
    sum_ao = _dw_pair("dw_attn_out", attn, dba, 1.0, blocks=4)
    (dqn, dkn, dv, dsink_tile), ((slots_o, slots_po, slots_ao),) = _attn_bwd(
        "attn_bwd", dattn, qn, kn, proj, sinks, [_chip_task([sum_o, sum_po, sum_ao])])
    dproj, dqg = _headnorm_bwd("q_norm_bwd", dqn, proj, COL_Q, ATTN_WIDTH, qg, dproj)
    dproj, dkg = _headnorm_bwd("k_norm_bwd", dkn, proj, COL_K, KV_WIDTH, kg, dproj)
    dproj, dpool_w, dpool_scale = _pool_bwd("pool_bwd", dmixed, pooled, pool_w, scale_row, dproj)
    for piece, col in ((dv, COL_V), (dga, COL_GA)):
        dproj = lax.dynamic_update_slice(dproj, piece, (0, col))
    (dh1, dh1b, dg2), ((g_pool_w,),) = _mm(
        "in_proj_bwd", [(dproj, w_inT, "nn", 0)], [F32, BF], tm=512, tn=D_MODEL, tk=IN_WIDTH, epilogue=_rms_bwd_epilogue,
        extras=[(h1, "tile", 0), (g2, "row", 0), (dh2, "tile", 0)], n_colsum=1,
        comm=[_gather_task([dpool_w.reshape(-1, LANES)])])
    (dw_inT,) = _mm("dw_in", [(dproj, u, "tn", 0)], [BF], tm=1920, tn=D_MODEL, tk=2048)
    dx, _, dg1, slots_in, slots_g1, slots_u1, slots_d1 = _ffn_bwd(
        "ffn1", dh1, dh1b, x2, g1, wg1T, wu1T, wd1, saved1, dw_inT.reshape(4, 2, IN_WIDTH // N_DEV, D_MODEL))

    slots = [slots_g1, slots_u1, slots_d1, slots_in, slots_po, slots_ao, slots_o, slots_g2, slots_u2, slots_d2]
    big_out = {}
    for label, group in (("ffn", (0, 1, 2, 7, 8, 9)), ("w_in", (3,)), ("w_pool_out", (4,)), ("attn_out_and_out", (5, 6))):
        items = [(slots[k], view(big[k][1], big[k][4]), view(big[k][2], big[k][4]), view(big[k][3], big[k][4]))
                 for k in group]
        for k, res in zip(group, _adamw_sharded("adamw_" + label, items, transpose=big[group[0]][5])):
            big_out[big[k][0]] = tuple(view(r, big[k][4]) for r in res)

    small_grads = {
        "ffn1_norm": jnp.sum(dg1, axis=(0, 1)), "mix_norm": jnp.sum(dg2, axis=(0, 1)), "ffn2_norm": jnp.sum(dg3, axis=(0, 1)),
        "gate_bias": jnp.concatenate([jnp.sum(cs_gp, axis=(0, 1)), jnp.sum(cs_ga, axis=(0, 1))]),
        "pool_scale": dpool_scale, "q_norm": _fold_heads(dqg) * ATTN_SCALE, "k_norm": _fold_heads(dkg),
        "sinks": dsink_tile[0, :N_HEADS]}
    ((g_vec,),) = _comm_only("gather_small_grads", [_direct_gather_task([_pack_small_grads(small_grads, loss_local)])])
    given = {"ffn1_norm": (ffn1_norm, m_ffn1_norm, v_ffn1_norm), "mix_norm": (mix_norm, m_mix_norm, v_mix_norm),
             "ffn2_norm": (ffn2_norm, m_ffn2_norm, v_ffn2_norm), "gate_bias": (gate_bias, m_gate_bias, v_gate_bias),
             "pool_scale": (pool_scale, m_pool_scale, v_pool_scale), "q_norm": (q_norm, m_q_norm, v_q_norm),
             "k_norm": (k_norm, m_k_norm, v_k_norm), "sinks": (sinks, m_sinks, v_sinks)}
    params = [tuple(a.reshape(shape) for a in given[nm]) for nm, _, shape in SMALL_LAYOUT]
    params.append(tuple(a.reshape(-1, LANES) for a in (pool_w, m_pool_w, v_pool_w)))
    small_res, loss_row = _adamw_small("adamw_small", g_vec.reshape(N_DEV, SMALL_ROWS, LANES),
                                       g_pool_w.reshape(N_DEV, -1, LANES), params)
    small_out = {nm: tuple(r.reshape(given[nm][0].shape) for r in res)
                 for (nm, _, _), res in zip(SMALL_LAYOUT, small_res)}
    small_out["pool_w"] = tuple(r.reshape(pool_w.shape) for r in small_res[-1])
    loss = loss_row[0, 0]

    order = ["ffn1_norm", "ffn1_w_gate", "ffn1_w_up", "ffn1_w_down", "mix_norm", "w_in", "pool_w", "pool_scale",
             "w_pool_out", "q_norm", "k_norm", "sinks", "w_attn_out", "gate_bias", "w_out", "ffn2_norm",
             "ffn2_w_gate", "ffn2_w_up", "ffn2_w_down"]
    every = {**big_out, **small_out}
    outs = [loss, dx.reshape(x.shape)]
    for j in range(4):
        outs += [every[nm][j] for nm in order]
    return tuple(outs)
```

```python
import functools

import jax
import jax.numpy as jnp
from jax import lax
from jax.experimental import pallas as pl
from jax.experimental.pallas import tpu as pltpu

BF = jnp.bfloat16
F32 = jnp.float32

D_MODEL = 1024
D_FF = 2816
POOL_WIDTH = 512
POOL_GROUP = 128
N_POOL_GROUPS = 4
HEAD_DIM = 64
N_HEADS = 16
GQA_GROUP = 8
BLOCK = 128
ATTN_WIDTH = 1024
KV_WIDTH = 128
IN_WIDTH = 3840
RMS_EPS = 1e-6
N_DEV = 8
LANES = 128

COL_Q = POOL_WIDTH
COL_K = COL_Q + ATTN_WIDTH
COL_V = COL_K + KV_WIDTH
COL_GP = COL_V + KV_WIDTH
COL_GA = COL_GP + D_MODEL

ADAM_LR = 0.001
ADAM_B1 = 0.9
ADAM_B2 = 0.999
ADAM_EPS = 1e-08
ADAM_WD = 0.01
ADAM_STEP = 10

VMEM_LIMIT_V7X = 56 * 1024 * 1024
MESH = pl.DeviceIdType.MESH
ANY = pl.BlockSpec(memory_space=pl.ANY)


def _params(sem=None, collective_id=None):
    return pltpu.CompilerParams(dimension_semantics=sem, vmem_limit_bytes=VMEM_LIMIT_V7X, collective_id=collective_id)


COLLECTIVE_IDS = {frozenset(["sibling"]): 0, frozenset(["chips"]): 1, frozenset(["sibling", "chips"]): 2}


def _handshake(peer_kinds):
    x, y, c, chips = _place()
    peers = ([(x, y, 1 - c)] if "sibling" in peer_kinds else []) + ([(*chip, c) for chip in chips] if "chips" in peer_kinds else [])
    barrier = pltpu.get_barrier_semaphore()
    for peer in peers:
        pl.semaphore_signal(barrier, inc=1, device_id=peer, device_id_type=MESH)
    pl.semaphore_wait(barrier, len(peers))


_DIMS = {"nt": (((1,), (1,)), ((), ())), "nn": (((1,), (0,)), ((), ())), "tn": (((0,), (0,)), ((), ()))}


class _Task:
    def __init__(self, inputs, out_shapes, scratch, phases, peers):
        self.inputs, self.out_shapes, self.scratch = list(inputs), list(out_shapes), list(scratch)
        self.phases = list(phases)
        self.peers = frozenset(peers)


class _CommPlumbing:
    def __init__(self, tasks):
        self.tasks = list(tasks or [])
        self.args = [a for t in self.tasks for a in t.inputs]
        self.out_shapes = [o for t in self.tasks for o in t.out_shapes]
        self.scratch = [s for t in self.tasks for s in t.scratch]
        self.n_in, self.n_out = len(self.args), len(self.out_shapes)

    def peer_kinds(self, own=()):
        kinds = frozenset(own).union(*[t.peers for t in self.tasks])
        return None if "all" in kinds or not kinds else kinds

    def collective_id(self, own=()):
        kinds = self.peer_kinds(own)
        return None if kinds is None else COLLECTIVE_IDS[kinds]

    def handshake(self, first, own=()):
        kinds = self.peer_kinds(own)
        if kinds is not None:
            pl.when(first)(functools.partial(_handshake, kinds))

    def _slices(self, c_in, c_out, c_scr):
        i = o = s = 0
        for t in self.tasks:
            yield t, c_in[i:i + len(t.inputs)], c_out[o:o + len(t.out_shapes)], c_scr[s:s + len(t.scratch)]
            i, o, s = i + len(t.inputs), o + len(t.out_shapes), s + len(t.scratch)

    def run(self, step, steps, before, c_in, c_out, c_scr):
        for t, ins, outs, scr in self._slices(c_in, c_out, c_scr):
            for frac, fn in t.phases:
                if step is None:
                    fn(ins, outs, scr)
                elif before == (frac == 0):
                    at = 0 if frac == 0 else max(0, min(steps, -(-int(round(frac * steps * 64)) // 64)) - 1)
                    pl.when(step == at)(functools.partial(fn, ins, outs, scr))

    def split_outputs(self, flat):
        res, o = [], 0
        for t in self.tasks:
            res.append(list(flat[o:o + len(t.out_shapes)]))
            o += len(t.out_shapes)
        return res


def _comm_only(name, tasks):
    plumb = _CommPlumbing(tasks)

    def body(*refs):
        c_in, c_out = refs[:plumb.n_in], refs[plumb.n_in: plumb.n_in + plumb.n_out]
        c_scr = refs[plumb.n_in + plumb.n_out:]
        plumb.run(None, 1, True, c_in, c_out, c_scr)

    res = pl.pallas_call(
        body, name=name, in_specs=[ANY] * plumb.n_in, out_specs=[ANY] * plumb.n_out, out_shape=plumb.out_shapes,
        scratch_shapes=plumb.scratch, compiler_params=pltpu.CompilerParams(has_side_effects=True),
    )(*plumb.args)
    return plumb.split_outputs(res)


def _mm(name, terms, out_dtypes, *, tm, tn, tk, epilogue=None, extras=(), n_colsum=0, comm=None, cols_outer=False,
        out_placement=None):
    a0, b0, mode0, _ = terms[0]
    if mode0 == "nt":
        (M, K), N = a0.shape, b0.shape[0]
    elif mode0 == "nn":
        (M, K), N = a0.shape, b0.shape[1]
    else:
        (K, M), N = a0.shape, b0.shape[1]
    tm, tn, tk = min(tm, M), min(tn, N), min(tk, K)
    assert M % tm == 0 and N % tn == 0 and K % tk == 0, (name, M, N, K, tm, tn, tk)
    nI, nJ, nK = M // tm, N // tn, K // tk
    n_terms = len(terms)
    n_acc = max(t[3] for t in terms) + 1
    n_ex = len(extras)
    n_out = len(out_dtypes)
    if epilogue is None:
        epilogue = lambda accs, ex: ([accs[0]], [])
    plumb = _CommPlumbing(comm)
    n_scr = n_acc if nK > 1 else 0
    grid = (nJ, nI, nK) if cols_outer else (nI, nJ, nK)

    def body(*refs):
        n_in = 2 * n_terms + n_ex
        ab = refs[: 2 * n_terms]
        ex_refs = refs[2 * n_terms: n_in]
        c_in = refs[n_in: n_in + plumb.n_in]
        o0 = n_in + plumb.n_in
        out_refs = refs[o0: o0 + n_out]
        cs_refs = refs[o0 + n_out: o0 + n_out + n_colsum]
        c_out = refs[o0 + n_out + n_colsum: o0 + n_out + n_colsum + plumb.n_out]
        s0 = o0 + n_out + n_colsum + plumb.n_out
        acc_refs = refs[s0: s0 + n_scr]
        c_scr = refs[s0 + n_scr:]
        steps = grid[0] * grid[1] * nK
        if comm:
            step = (pl.program_id(0) * grid[1] + pl.program_id(1)) * nK + pl.program_id(2)
            plumb.handshake(step == 0)
            plumb.run(step, steps, True, c_in, c_out, c_scr)

        def products():
            accs = [None] * n_acc
            for t, (_, _, mode, ai) in enumerate(terms):
                p = lax.dot_general(ab[2 * t][...], ab[2 * t + 1][...], _DIMS[mode], preferred_element_type=F32)
                accs[ai] = p if accs[ai] is None else accs[ai] + p
            return accs

        def finish(accs):
            outs, colsums = epilogue(accs, [r[...] for r in ex_refs])
            for r, o in zip(out_refs, outs):
                r[...] = o.astype(r.dtype)
            for r, cs in zip(cs_refs, colsums):
                r[...] = jnp.sum(cs, axis=0, keepdims=True).reshape(r.shape)

        if nK == 1:
            finish(products())
        else:
            k = pl.program_id(2)
            accs = products()

            @pl.when(k == 0)
            def _():
                for r, a in zip(acc_refs, accs):
                    r[...] = a

            @pl.when(k > 0)
            def _():
                for r, a in zip(acc_refs, accs):
                    r[...] += a

            @pl.when(k == nK - 1)
            def _():
                finish([r[...] for r in acc_refs])

        if comm:
            plumb.run(step, steps, False, c_in, c_out, c_scr)

    def spec(block, index, fixed=False):
        imap = (lambda q, p, k: index(p, q, k)) if cols_outer else index
        return pl.BlockSpec(block, imap, pipeline_mode=pl.Buffered(1)) if fixed else pl.BlockSpec(block, imap)

    in_specs, args = [], []
    for a, b, mode, _ in terms:
        if mode == "nt":
            in_specs += [spec((tm, tk), lambda i, j, k: (i, k), nI * nK == 1),
                         spec((tn, tk), lambda i, j, k: (j, k), nJ * nK == 1)]
        elif mode == "nn":
            in_specs += [spec((tm, tk), lambda i, j, k: (i, k), nI * nK == 1),
                         spec((tk, tn), lambda i, j, k: (k, j), nJ * nK == 1)]
        else:
            in_specs += [spec((tk, tm), lambda i, j, k: (k, i), nI * nK == 1),
                         spec((tk, tn), lambda i, j, k: (k, j), nJ * nK == 1)]
        args += [a, b]
    for arr, kind, off in extras:
        if kind == "tile":
            in_specs.append(spec((tm, tn), functools.partial(lambda i, j, k, off: (i, j + off), off=off)))
        else:
            in_specs.append(spec((1, tn), functools.partial(lambda i, j, k, off: (0, j + off), off=off)))
        args.append(arr)
    placed = dict(out_placement or {})
    out_shape = [jax.ShapeDtypeStruct((M, placed.get(o, (N, 0))[0]), dt) for o, dt in enumerate(out_dtypes)]
    out_specs = [spec((tm, tn), functools.partial(lambda i, j, k, off: (i, j + off), off=placed.get(o, (N, 0))[1] // tn))
                 for o in range(n_out)]
    out_shape += [jax.ShapeDtypeStruct((nI, 1, N), F32) for _ in range(n_colsum)]
    out_specs += [spec((1, 1, tn), lambda i, j, k: (i, 0, j)) for _ in range(n_colsum)]
    scratch = [pltpu.VMEM((tm, tn), F32) for _ in range(n_scr)]
    args += plumb.args
    in_specs += [ANY] * plumb.n_in
    out_shape += plumb.out_shapes
    out_specs += [ANY] * plumb.n_out
    sem = ("arbitrary",) * 3 if comm else ("parallel", "parallel", "arbitrary")
    res = pl.pallas_call(
        body, name=name, grid=grid, in_specs=in_specs, out_specs=out_specs, out_shape=out_shape,
        scratch_shapes=scratch + plumb.scratch, compiler_params=_params(sem, plumb.collective_id()),
    )(*args)
    n_own = n_out + n_colsum
    return (list(res[:n_own]), plumb.split_outputs(res[n_own:])) if comm is not None else res


ROW_TILE = 512


def _rms_fwd(name, x, g, comm, weights, transposes):
    T, D = x.shape
    steps = T // ROW_TILE
    plumb = _CommPlumbing(comm)
    nw = len(weights)

    def body(x_ref, g_ref, *rest):
        w_refs, c_in = rest[:nw], rest[nw: nw + plumb.n_in]
        o_ref, shard_refs = rest[nw + plumb.n_in], rest[nw + plumb.n_in + 1: 2 * nw + plumb.n_in + 1]
        c_out = rest[2 * nw + plumb.n_in + 1: 2 * nw + plumb.n_in + 1 + plumb.n_out]
        c_scr = rest[2 * nw + plumb.n_in + 1 + plumb.n_out:]
        plumb.handshake(pl.program_id(0) == 0)
        plumb.run(pl.program_id(0), steps, True, c_in, c_out, c_scr)

        @pl.when(pl.program_id(0) == 0)
        def _():
            for w_ref, s_ref, tr in zip(w_refs, shard_refs, transposes):
                v = w_ref[...]
                s_ref[...] = (v.T if tr else v).astype(BF)

        xv = x_ref[...]
        r = lax.rsqrt(jnp.mean(xv * xv, axis=-1, keepdims=True) + RMS_EPS)
        o_ref[...] = (xv * r * g_ref[...]).astype(BF)
        plumb.run(pl.program_id(0), steps, False, c_in, c_out, c_scr)

    row = pl.BlockSpec((ROW_TILE, D), lambda i: (i, 0))
    whole = lambda shape: pl.BlockSpec(shape, lambda i: (0, 0), pipeline_mode=pl.Buffered(1))
    shard_shapes = [w.shape[::-1] if tr else w.shape for w, tr in zip(weights, transposes)]
    res = pl.pallas_call(
        body, name=name, grid=(steps,),
        in_specs=[row, pl.BlockSpec((1, D), lambda i: (0, 0))] + [whole(w.shape) for w in weights] + [ANY] * plumb.n_in,
        out_specs=[row] + [whole(s) for s in shard_shapes] + [ANY] * plumb.n_out,
        out_shape=[jax.ShapeDtypeStruct((T, D), BF)] + [jax.ShapeDtypeStruct(s, BF) for s in shard_shapes] + plumb.out_shapes,
        scratch_shapes=plumb.scratch, compiler_params=_params(("arbitrary",), plumb.collective_id()),
    )(x, g, *weights, *plumb.args)
    return res[0], list(res[1: nw + 1]), plumb.split_outputs(res[nw + 1:])


HEADNORM_TILE = 2048


def _half_sum_matrix():
    r = lax.broadcasted_iota(jnp.int32, (LANES, LANES), 0) // HEAD_DIM
    c = lax.broadcasted_iota(jnp.int32, (LANES, LANES), 1) // HEAD_DIM
    return (r == c).astype(BF)


def _head_mean(v, ones_blockdiag):
    hi = v.astype(BF)
    lo = (v - hi.astype(F32)).astype(BF)
    s = jnp.dot(hi, ones_blockdiag, preferred_element_type=F32) + jnp.dot(lo, ones_blockdiag, preferred_element_type=F32)
    return s * (1.0 / HEAD_DIM)


def _headnorm_fwd(name, proj, col0, width, g2):
    T = proj.shape[0]
    wide = min(width, GROUP_WIDTH)
    nb, off = width // wide, col0 // wide

    def body(x_ref, g_ref, b_ref, o_ref):
        for s in range(wide // LANES):
            lanes = slice(LANES * s, LANES * (s + 1))
            xv = x_ref[:, lanes].astype(F32)
            r = lax.rsqrt(_head_mean(xv * xv, b_ref[...]) + RMS_EPS)
            o_ref[:, lanes] = (xv * r * g_ref[...]).astype(BF)

    return pl.pallas_call(
        body, name=name, grid=(T // HEADNORM_TILE, nb),
        in_specs=[pl.BlockSpec((HEADNORM_TILE, wide), lambda i, j: (i, j + off)),
                  pl.BlockSpec((1, LANES), lambda i, j: (0, 0)), pl.BlockSpec((LANES, LANES), lambda i, j: (0, 0))],
        out_specs=pl.BlockSpec((HEADNORM_TILE, wide), lambda i, j: (i, j)),
        out_shape=jax.ShapeDtypeStruct((T, width), BF), compiler_params=_params(("parallel", "parallel")),
    )(proj, g2, _half_sum_matrix())


def _headnorm_bwd(name, dy, proj, col0, width, g2, into):
    T = proj.shape[0]
    wide = min(width, GROUP_WIDTH)
    nb, off = width // wide, col0 // wide

    def body(dy_ref, x_ref, g_ref, b_ref, into_ref, dx_ref, dg_ref):
        for s in range(wide // LANES):
            lanes = slice(LANES * s, LANES * (s + 1))
            xv = x_ref[:, lanes].astype(F32)
            dyv = dy_ref[:, lanes].astype(F32)
            r = lax.rsqrt(_head_mean(xv * xv, b_ref[...]) + RMS_EPS)
            xhat = xv * r
            dxhat = dyv * g_ref[...]
            dx_ref[:, lanes] = (r * (dxhat - xhat * _head_mean(dxhat * xhat, b_ref[...]))).astype(BF)
            dg_ref[0, :, lanes] = jnp.sum(dyv * xhat, axis=0, keepdims=True)

    return pl.pallas_call(
        body, name=name, grid=(T // HEADNORM_TILE, nb),
        in_specs=[pl.BlockSpec((HEADNORM_TILE, wide), lambda i, j: (i, j)),
                  pl.BlockSpec((HEADNORM_TILE, wide), lambda i, j: (i, j + off)),
                  pl.BlockSpec((1, LANES), lambda i, j: (0, 0)), pl.BlockSpec((LANES, LANES), lambda i, j: (0, 0)), ANY],
        out_specs=[pl.BlockSpec((HEADNORM_TILE, wide), lambda i, j: (i, j + off)),
                   pl.BlockSpec((1, 1, wide), lambda i, j: (i, 0, j))],
        out_shape=[jax.ShapeDtypeStruct(into.shape, BF), jax.ShapeDtypeStruct((T // HEADNORM_TILE, 1, width), F32)],
        input_output_aliases={4: 0}, compiler_params=_params(("parallel", "parallel")),
    )(dy, proj, g2, _half_sum_matrix(), into)


def _shift_down(v, k, row):
    return jnp.where(row >= k, pltpu.roll(v, k, axis=0), 0.0)


def _shift_up(v, k, row, T):
    return jnp.where(row < T - k, pltpu.roll(v, T - k, axis=0), 0.0)


def _by_group(g, vals):
    out = vals[-1]
    for i in range(len(vals) - 2, -1, -1):
        out = jnp.where(g == i, vals[i], out)
    return out


def _pool_fwd(name, proj, pool_w, pool_scale):
    T = proj.shape[0]

    def body(x_ref, w_ref, s_ref, pooled_ref, mixed_ref):
        g = pl.program_id(0)
        xv = x_ref[...].astype(F32)
        row = lax.broadcasted_iota(jnp.int32, (T, 1), 0)
        s2 = xv + _shift_down(xv, 1, row)
        s4 = s2 + _shift_down(s2, 2, row)
        s8 = s4 + _shift_down(s4, 4, row)
        s16 = s8 + _shift_down(s8, 8, row)
        wsum = _by_group(g, [s2, s4, s8, s16])
        count = jnp.minimum(row + 1, 2 << g).astype(F32)
        pooled = (wsum / count - xv).astype(BF)
        pooled_ref[...] = pooled
        mixed = jnp.dot(pooled, w_ref[0].astype(BF), preferred_element_type=F32) * s_ref[...]
        mixed_ref[...] = mixed.astype(BF)

    col = pl.BlockSpec((T, POOL_GROUP), lambda g: (0, g))
    return pl.pallas_call(
        body, name=name, grid=(N_POOL_GROUPS,),
        in_specs=[col, pl.BlockSpec((1, POOL_GROUP, POOL_GROUP), lambda g: (g, 0, 0)),
                  pl.BlockSpec((1, POOL_GROUP), lambda g: (0, g))],
        out_specs=[col, col],
        out_shape=[jax.ShapeDtypeStruct((T, POOL_WIDTH), BF), jax.ShapeDtypeStruct((T, POOL_WIDTH), BF)],
        compiler_params=_params(("parallel",)),
    )(proj, pool_w, pool_scale)


def _pool_bwd(name, dmixed, pooled, pool_w, pool_scale, into):
    T = dmixed.shape[0]

    def body(dm_ref, p_ref, w_ref, s_ref, into_ref, dx_ref, dw_ref, ds_ref):
        g = pl.program_id(0)
        dm = dm_ref[...].astype(F32)
        pooled = p_ref[...]
        w = w_ref[0].astype(BF)
        pre = jnp.dot(pooled, w, preferred_element_type=F32)
        ds_ref[...] = jnp.sum(dm * pre, axis=0, keepdims=True)
        dms = (dm * s_ref[...]).astype(BF)
        dw_ref[0] = lax.dot_general(pooled, dms, _DIMS["tn"], preferred_element_type=F32)
        dpooled = lax.dot_general(dms, w, _DIMS["nt"], preferred_element_type=F32)
        row = lax.broadcasted_iota(jnp.int32, (T, 1), 0)
        count = jnp.minimum(row + 1, 2 << g).astype(F32)
        z = dpooled / count
        l2 = z + _shift_up(z, 1, row, T)
        l4 = l2 + _shift_up(l2, 2, row, T)
        l8 = l4 + _shift_up(l4, 4, row, T)
        l16 = l8 + _shift_up(l8, 8, row, T)
        dx_ref[...] = (_by_group(g, [l2, l4, l8, l16]) - dpooled).astype(BF)

    col = pl.BlockSpec((T, POOL_GROUP), lambda g: (0, g))
    wspec = pl.BlockSpec((1, POOL_GROUP, POOL_GROUP), lambda g: (g, 0, 0))
    sspec = pl.BlockSpec((1, POOL_GROUP), lambda g: (0, g))
    return pl.pallas_call(
        body, name=name, grid=(N_POOL_GROUPS,), in_specs=[col, col, wspec, sspec, ANY], out_specs=[col, wspec, sspec],
        out_shape=[jax.ShapeDtypeStruct(into.shape, BF),
                   jax.ShapeDtypeStruct((N_POOL_GROUPS, POOL_GROUP, POOL_GROUP), F32),
                   jax.ShapeDtypeStruct((1, POOL_WIDTH), F32)],
        input_output_aliases={4: 0}, compiler_params=_params(("parallel",)),
    )(dmixed, pooled, pool_w, pool_scale, into)


ATTN_SCALE = HEAD_DIM ** -0.5
MASKED = float(jnp.finfo(jnp.float32).min)
KV_COL_BLOCK_V = COL_V // LANES
GROUP_WIDTH = GQA_GROUP * HEAD_DIM


def _dup_head(v, j):
    half = lax.broadcasted_iota(jnp.int32, (1, LANES), 1) // HEAD_DIM
    return jnp.where(half == j, v, pltpu.roll(v, HEAD_DIM, axis=1))


def _stack_heads(v, low):
    pieces = []
    for p in range(GROUP_WIDTH // LANES):
        vp = v[:, LANES * p: LANES * (p + 1)]
        pieces.append(jnp.where(low, vp, jnp.zeros_like(vp)))
        pieces.append(jnp.where(low, jnp.zeros_like(vp), vp))
    return jnp.concatenate(pieces, axis=0)


def _unstack_transposed(t, low):
    pairs = []
    for p in range(GROUP_WIDTH // LANES):
        even = t[:, BLOCK * (2 * p): BLOCK * (2 * p + 1)].T
        odd = t[:, BLOCK * (2 * p + 1): BLOCK * (2 * p + 2)].T
        pairs.append(jnp.where(low, even, odd))
    return pairs


STACKED = GQA_GROUP * BLOCK


def _band_bias():
    key = lax.broadcasted_iota(jnp.int32, (2, 2 * BLOCK, STACKED), 1)
    qry = lax.broadcasted_iota(jnp.int32, (2, 2 * BLOCK, STACKED), 2) % BLOCK
    first = lax.broadcasted_iota(jnp.int32, (2, 2 * BLOCK, STACKED), 0) == 0
    valid = (key > qry) & (key <= qry + BLOCK) & (jnp.logical_not(first) | (key >= BLOCK))
    return jnp.where(valid, 0.0, MASKED).astype(F32)


def _softmax_keys_on_sublanes(k2, q, bias, sink_ref, j):
    head_of_lane = lax.broadcasted_iota(jnp.int32, (1, STACKED), 1) // BLOCK
    sink = jnp.zeros((1, STACKED), F32)
    for h in range(GQA_GROUP):
        sink = jnp.where(head_of_lane == h, sink_ref[j * GQA_GROUP + h], sink)
    s = lax.dot_general(k2, q, _DIMS["nt"], preferred_element_type=F32) + bias
    m = jnp.maximum(jnp.max(s, axis=0, keepdims=True), sink)
    e = jnp.exp(s - m)
    e_sink = jnp.exp(sink - m)
    inv = 1.0 / (jnp.sum(e, axis=0, keepdims=True) + e_sink)
    return e * inv, e_sink * inv


def _attn_fwd(name, qn, kn, proj, sinks, comm=None):
    T = qn.shape[0]
    nb = T // BLOCK
    plumb = _CommPlumbing(comm)

    def body(sink_ref, bias_ref, q_ref, kp_ref, kc_ref, vp_ref, vc_ref, *rest):
        c_in, o_ref = rest[:plumb.n_in], rest[plumb.n_in]
        c_out, c_scr = rest[plumb.n_in + 1: plumb.n_in + 1 + plumb.n_out], rest[plumb.n_in + 1 + plumb.n_out:]
        m = pl.program_id(0)
        plumb.handshake(m == 0)
        plumb.run(m, nb // 2, True, c_in, c_out, c_scr)
        low = lax.broadcasted_iota(jnp.int32, (1, LANES), 1) < HEAD_DIM
        k_pair, v_pair = kc_ref[...], vc_ref[...]
        for b in range(2):
            rows = slice(BLOCK * b, BLOCK * (b + 1))
            kk = k_pair if b else jnp.concatenate([kp_ref[...], k_pair[0:BLOCK]], axis=0)
            vv = v_pair if b else jnp.concatenate([vp_ref[...], v_pair[0:BLOCK]], axis=0)
            bias = bias_ref[1] if b else bias_ref[jnp.minimum(m, 1)]
            for j in range(2):
                q = _stack_heads(q_ref[rows, GROUP_WIDTH * j: GROUP_WIDTH * (j + 1)], low)
                p, _ = _softmax_keys_on_sublanes(_dup_head(kk, j), q, bias, sink_ref, j)
                o_t = lax.dot_general(_dup_head(vv, j), p.astype(BF), _DIMS["tn"], preferred_element_type=F32)
                for pair, o in enumerate(_unstack_transposed(o_t, low)):
                    lanes = slice(GROUP_WIDTH * j + LANES * pair, GROUP_WIDTH * j + LANES * (pair + 1))
                    o_ref[rows, lanes] = o.astype(BF)
        plumb.run(m, nb // 2, False, c_in, c_out, c_scr)

    wide = pl.BlockSpec((2 * BLOCK, ATTN_WIDTH), lambda m: (m, 0))
    before = lambda m: jnp.maximum(2 * m - 1, 0)
    res = pl.pallas_call(
        body, name=name, grid=(nb // 2,),
        in_specs=[pl.BlockSpec(memory_space=pltpu.SMEM),
                  pl.BlockSpec((2, 2 * BLOCK, STACKED), lambda m: (0, 0, 0)), wide,
                  pl.BlockSpec((BLOCK, LANES), lambda m: (before(m), 0)),
                  pl.BlockSpec((2 * BLOCK, LANES), lambda m: (m, 0)),
                  pl.BlockSpec((BLOCK, LANES), lambda m: (before(m), KV_COL_BLOCK_V)),
                  pl.BlockSpec((2 * BLOCK, LANES), lambda m: (m, KV_COL_BLOCK_V))] + [ANY] * plumb.n_in,
        out_specs=[wide] + [ANY] * plumb.n_out,
        out_shape=[jax.ShapeDtypeStruct((T, ATTN_WIDTH), BF)] + plumb.out_shapes, scratch_shapes=plumb.scratch,
        compiler_params=_params(("arbitrary",) if comm else ("parallel",), plumb.collective_id()),
    )(sinks, _band_bias(), qn, kn, kn, proj, proj, *plumb.args)
    return (res[0], plumb.split_outputs(res[1:])) if comm is not None else res[0]


def _attn_bwd(name, dout, qn, kn, proj, sinks, comm):
    T = qn.shape[0]
    nb = T // BLOCK
    plumb = _CommPlumbing(comm)

    def body(sink_ref, bias_ref, do_ref, q_ref, kp_ref, kc_ref, vp_ref, vc_ref, *rest):
        c_in = rest[:plumb.n_in]
        dq_ref, k_own, k_before, v_own, v_before, dsink_ref = rest[plumb.n_in: plumb.n_in + 6]
        c_out, c_scr = rest[plumb.n_in + 6: plumb.n_in + 6 + plumb.n_out], rest[plumb.n_in + 6 + plumb.n_out:]
        m = pl.program_id(0)
        plumb.handshake(m == 0)
        plumb.run(m, nb // 2, True, c_in, c_out, c_scr)
        lane = lax.broadcasted_iota(jnp.int32, (1, LANES), 1)
        low = lane < HEAD_DIM

        @pl.when(m == 0)
        def _():
            dsink_ref[...] = jnp.zeros_like(dsink_ref)

        k_pair, v_pair = kc_ref[...], vc_ref[...]
        dsink = jnp.zeros((1, LANES), F32)
        for b in range(2):
            rows = slice(BLOCK * b, BLOCK * (b + 1))
            kk = k_pair if b else jnp.concatenate([kp_ref[...], k_pair[0:BLOCK]], axis=0)
            vv = v_pair if b else jnp.concatenate([vp_ref[...], v_pair[0:BLOCK]], axis=0)
            bias = bias_ref[1] if b else bias_ref[jnp.minimum(m, 1)]
            dk_tot = jnp.zeros((2 * BLOCK, LANES), F32)
            dv_tot = jnp.zeros((2 * BLOCK, LANES), F32)
            for j in range(2):
                k2 = _dup_head(kk, j)
                v2 = _dup_head(vv, j)
                q = _stack_heads(q_ref[rows, GROUP_WIDTH * j: GROUP_WIDTH * (j + 1)], low)
                do = _stack_heads(do_ref[rows, GROUP_WIDTH * j: GROUP_WIDTH * (j + 1)], low)
                p, psink = _softmax_keys_on_sublanes(k2, q, bias, sink_ref, j)
                dp =lax.dot_general(v2, do, _DIMS["nt"], preferred_element_type=F32)
                delta = jnp.sum(p * dp, axis=0, keepdims=True)
                ds = (p * (dp - delta)).astype(BF)
                dk2 = jnp.dot(ds, q, preferred_element_type=F32)
                dv2 = jnp.dot(p.astype(BF), do, preferred_element_type=F32)
                dq_t = lax.dot_general(k2, ds, _DIMS["tn"], preferred_element_type=F32)
                for pair, dq in enumerate(_unstack_transposed(dq_t, low)):
                    lanes = slice(GROUP_WIDTH * j + LANES * pair, GROUP_WIDTH * j + LANES * (pair + 1))
                    dq_ref[rows, lanes] = dq.astype(BF)
                mine = low if j == 0 else jnp.logical_not(low)
                dk_tot = dk_tot + jnp.where(mine, dk2 + pltpu.roll(dk2, HEAD_DIM, axis=1), 0.0)
                dv_tot = dv_tot + jnp.where(mine, dv2 + pltpu.roll(dv2, HEAD_DIM, axis=1), 0.0)
                sink_term = psink * delta
                for h in range(GQA_GROUP):
                    val = -jnp.sum(sink_term[:, BLOCK * h: BLOCK * (h + 1)], axis=1, keepdims=True)
                    dsink = dsink + jnp.where(lane == j * GQA_GROUP + h, val, 0.0)
            k_before[rows, :], k_own[rows, :] = dk_tot[0:BLOCK], dk_tot[BLOCK:]
            v_before[rows, :], v_own[rows, :] = dv_tot[0:BLOCK], dv_tot[BLOCK:]
        dsink_ref[0:1, :] += dsink
        plumb.run(m, nb // 2, False, c_in, c_out, c_scr)

    wide = pl.BlockSpec((2 * BLOCK, ATTN_WIDTH), lambda m: (m, 0))
    pair = pl.BlockSpec((2 * BLOCK, LANES), lambda m: (m, 0))
    before = lambda m: jnp.maximum(2 * m - 1, 0)
    res = pl.pallas_call(
        body, name=name, grid=(nb // 2,),
        in_specs=[pl.BlockSpec(memory_space=pltpu.SMEM),
                  pl.BlockSpec((2, 2 * BLOCK, STACKED), lambda m: (0, 0, 0)), wide, wide,
                  pl.BlockSpec((BLOCK, LANES), lambda m: (before(m), 0)), pair,
                  pl.BlockSpec((BLOCK, LANES), lambda m: (before(m), KV_COL_BLOCK_V)),
                  pl.BlockSpec((2 * BLOCK, LANES), lambda m: (m, KV_COL_BLOCK_V))] + [ANY] * plumb.n_in,
        out_specs=[wide, pair, pair, pair, pair, pl.BlockSpec((8, LANES), lambda m: (0, 0))] + [ANY] * plumb.n_out,
        out_shape=[jax.ShapeDtypeStruct((T, ATTN_WIDTH), BF)] + [jax.ShapeDtypeStruct((T, KV_WIDTH), F32)] * 4
        + [jax.ShapeDtypeStruct((8, LANES), F32)] + plumb.out_shapes,
        scratch_shapes=plumb.scratch, compiler_params=_params(("arbitrary",), plumb.collective_id()),
    )(sinks, _band_bias(), dout, qn, kn, kn, proj, proj, *plumb.args)
    return list(res[:6]), plumb.split_outputs(res[6:])


def _swiglu_fwd_epilogue(accs, ex):
    g, u = accs
    return [g, u, g * jax.nn.sigmoid(g) * u], []


def _swiglu_bwd_epilogue(accs, ex):
    (da,) = accs
    g, u = ex[0].astype(F32), ex[1].astype(F32)
    s = jax.nn.sigmoid(g)
    gs = g * s
    return [da * u * (s + gs - gs * s), da * gs], []


def _residual_norm_epilogue(scale):
    def epilogue(accs, ex):
        res, gain = ex
        h = res + scale * accs[0]
        r = lax.rsqrt(jnp.mean(h * h, axis=-1, keepdims=True) + RMS_EPS)
        return [h, h * r * gain], []
    return epilogue


def _rms_bwd_epilogue(accs, ex):
    (dn,) = accs
    xv, g, dres = ex
    r = lax.rsqrt(jnp.mean(xv * xv, axis=-1, keepdims=True) + RMS_EPS)
    xhat = xv * r
    dxhat = dn * g
    dx = dres + r * (dxhat - xhat * jnp.mean(dxhat * xhat, axis=-1, keepdims=True))
    return [dx, dx], [dn * xhat]


def _loss_epilogue(accs, ex):
    xv, target = ex
    d = xv + 0.5 * accs[0] - target
    dy = d * (1.0 / D_MODEL)
    return [dy, dy], [d * d]


def _merge_fwd_epilogue(accs, ex):
    (ba,) = accs
    bp, gp_pre, ga_pre, bias_p, bias_a = ex
    gp = jax.nn.sigmoid(gp_pre.astype(F32) + bias_p)
    ga = jax.nn.sigmoid(ga_pre.astype(F32) + bias_a)
    return [gp * bp.astype(F32) + ga * ba, ba], []


def _merge_bwd_epilogue(accs, ex):
    (dm,) = accs
    bp, ba, gp_pre, ga_pre, bias_p, bias_a = ex
    gp = jax.nn.sigmoid(gp_pre.astype(F32) + bias_p)
    ga = jax.nn.sigmoid(ga_pre.astype(F32) + bias_a)
    dbp, dba = dm * gp, dm * ga
    dgp = dbp * bp.astype(F32) * (1.0 - gp)
    dga = dba * ba.astype(F32) * (1.0 - ga)
    return [dbp, dba, dgp, dga], [dgp, dga]


def _prep(name, ws, transposes):
    n = len(ws)

    def body(*refs):
        for w_ref, o_ref, tr in zip(refs[:n], refs[n:], transposes):
            v = w_ref[...]
            o_ref[...] = (v.T if tr else v).astype(BF)

    shapes = [jax.ShapeDtypeStruct(w.shape[::-1] if tr else w.shape, BF) for w, tr in zip(ws, transposes)]
    return pl.pallas_call(body, name=name, out_shape=shapes, compiler_params=_params())(*ws)


def _adam_math(w, g, m, v):
    m = ADAM_B1 * m + (1.0 - ADAM_B1) * g
    v = ADAM_B2 * v + (1.0 - ADAM_B2) * jnp.square(g)
    m_hat = m / (1.0 - ADAM_B1 ** ADAM_STEP)
    v_hat = v / (1.0 - ADAM_B2 ** ADAM_STEP)
    delta = -ADAM_LR * (m_hat / (jnp.sqrt(v_hat) + ADAM_EPS) + ADAM_WD * w)
    return delta, m, v


def _adamw_sharded(name, items, transpose=False):
    n = len(items)

    def body(*refs):
        ins, outs = refs[:4 * n], refs[4 * n:]
        for k in range(n):
            s_ref, w_ref, m_ref, v_ref = ins[4 * k: 4 * k + 4]
            g = s_ref[0].astype(F32)
            for i in range(1, 4):
                g = g + s_ref[i].astype(F32)
            if transpose:
                g = g.T
            delta, mn, vn = _adam_math(w_ref[...], g, m_ref[...], v_ref[...])
            for o_ref, val in zip(outs[4 * k: 4 * k + 4], (g, delta, mn, vn)):
                o_ref[...] = val

    flat = [a for item in items for a in item]
    out_shape = [jax.ShapeDtypeStruct(item[1].shape, F32) for item in items for _ in range(4)]
    _, r, C = items[0][0].shape
    rows = r // 4
    if transpose or rows % 8:
        res = pl.pallas_call(body, name=name, out_shape=out_shape, compiler_params=_params())(*flat)
    else:
        tile = pl.BlockSpec((rows, C), lambda i: (i, 0))
        res = pl.pallas_call(
            body, name=name, grid=(4,), in_specs=[pl.BlockSpec((4, rows, C), lambda i: (0, i, 0)), tile, tile, tile] * n,
            out_specs=[tile] * (4 * n), out_shape=out_shape, compiler_params=_params(("parallel",)),
        )(*flat)
    return [tuple(res[4 * k: 4 * k + 4]) for k in range(n)]


SMALL_LAYOUT = (("ffn1_norm", 0, (8, LANES)), ("mix_norm", 8, (8, LANES)), ("ffn2_norm", 16, (8, LANES)),
                ("gate_bias", 24, (16, LANES)), ("pool_scale", 40, (4, LANES)), ("q_norm", 48, (1, HEAD_DIM)),
                ("k_norm", 56, (1, HEAD_DIM)), ("sinks", 64, (1, N_HEADS)))
LOSS_ROW = 72
SMALL_ROWS = 80


def _adamw_small(name, g_vec, g_pool_w, params):
    n = len(SMALL_LAYOUT) + 1

    def body(vec_ref, pw_ref, *refs):
        ins, outs = refs[:3 * n], refs[3 * n:]
        vec = vec_ref[0]
        pw = pw_ref[0]
        for i in range(1, N_DEV):
            vec = vec + vec_ref[i]
            pw = pw + pw_ref[i]
        grads = [vec[r0:r0 + shape[0], 0:shape[1]] for _, r0, shape in SMALL_LAYOUT] + [pw]
        for p, g in enumerate(grads):
            w_ref, m_ref, v_ref = ins[3 * p: 3 * p + 3]
            delta, mn, vn = _adam_math(w_ref[...], g, m_ref[...], v_ref[...])
            for o_ref, val in zip(outs[4 * p: 4 * p + 4], (g, delta, mn, vn)):
                o_ref[...] = val
        outs[4 * n][...] = vec[LOSS_ROW:LOSS_ROW + 1, :]

    flat = [a for wmv in params for a in wmv]
    out_shape = [jax.ShapeDtypeStruct(wmv[0].shape, F32) for wmv in params for _ in range(4)]
    out_shape.append(jax.ShapeDtypeStruct((1, LANES), F32))
    res = pl.pallas_call(body, name=name, out_shape=out_shape, compiler_params=_params())(g_vec, g_pool_w, *flat)
    return [tuple(res[4 * p: 4 * p + 4]) for p in range(n)], res[4 * n]


def _place():
    x, y, c = lax.axis_index("x"), lax.axis_index("y"), lax.axis_index("c")
    other_chips = [(1 - x, y), (x, 1 - y), (1 - x, 1 - y)]
    return x, y, c, other_chips


def _rows(ref, r, place, natural=False):
    px, py, pc = place
    b = 4 * px + 2 * py + pc if natural else 4 * pc + 2 * px + py
    return ref.at[pl.ds(pl.multiple_of(b * r, 8), r), :]


def _gather_task(shards, natural=(), forward_at=0.75):
    n = len(shards)
    rs = [s.shape[0] for s in shards]
    rows_of = lambda ref, k, place: _rows(ref, rs[k], place, k in natural)

    def copy(scr, outs, k, slot, block, to, src=None):
        rows = rows_of(outs[k], k, block)
        return pltpu.make_async_remote_copy(
            src_ref=rows if src is None else src, dst_ref=rows, send_sem=scr[0].at[7 * k + slot],
            recv_sem=scr[1].at[7 * k + slot], device_id=to, device_id_type=MESH)

    def first_sends(ins, outs, scr):
        x, y, c, chips = _place()
        me = (x, y, c)
        cps = [copy(scr, outs, k, 1 + j, me, (*chip, c), src=ins[k]) for j, chip in enumerate(chips) for k in range(n)]
        return cps + [copy(scr, outs, k, 0, me, (x, y, 1 - c), src=ins[k]) for k in range(n)]

    def passed_on(outs, scr):
        x, y, c, chips = _place()
        return [copy(scr, outs, k, 4 + j, (*chip, c), (x, y, 1 - c)) for j, chip in enumerate(chips) for k in range(n)]

    def local(ins, outs, scr):
        x, y, c, _ = _place()
        return [pltpu.make_async_copy(ins[k], rows_of(outs[k], k, (x, y, c)), scr[2].at[k]) for k in range(n)]

    def start(ins, outs, scr):
        for cp in local(ins, outs, scr) + first_sends(ins, outs, scr):
            cp.start()

    def forward(ins, outs, scr):
        x, y, c, chips = _place()
        for j, chip in enumerate(chips):
            for k in range(n):
                copy(scr, outs, k, 1 + j, (*chip, c), (x, y, c)).wait_recv()
                copy(scr, outs, k, 4 + j, (*chip, c), (x, y, 1 - c)).start()

    def finish(ins, outs, scr):
        x, y, c, chips = _place()
        for k in range(n):
            copy(scr, outs, k, 0, (x, y, 1 - c), (x, y, c)).wait_recv()
        for j, chip in enumerate(chips):
            for k in range(n):
                copy(scr, outs, k, 4 + j, (*chip, 1 - c), (x, y, c)).wait_recv()
        for cp in first_sends(ins, outs, scr) + passed_on(outs, scr):
            cp.wait_send()
        for cp in local(ins, outs, scr):
            cp.wait()

    out_shapes = [jax.ShapeDtypeStruct((N_DEV * s.shape[0], s.shape[1]), s.dtype) for s in shards]
    scratch = [pltpu.SemaphoreType.DMA((7 * n,)), pltpu.SemaphoreType.DMA((7 * n,)), pltpu.SemaphoreType.DMA((n,))]
    return _Task(shards, out_shapes, scratch, [(0, start), (forward_at, forward), (1.0, finish)], ("sibling", "chips"))


def _direct_gather_task(shards):
    n = len(shards)
    rs = [s.shape[0] for s in shards]

    def peers():
        x, y, c, _ = _place()
        flip = lambda v, bit: 1 - v if bit else v
        return (x, y, c), [(flip(x, (s >> 2) & 1), flip(y, (s >> 1) & 1), flip(c, s & 1)) for s in range(1, N_DEV)]

    def copies(ins, outs, scr):
        me, others = peers()
        local = [pltpu.make_async_copy(ins[k], _rows(outs[k], rs[k], me), scr[2].at[k]) for k in range(n)]
        sems = lambda k, s: dict(send_sem=scr[0].at[7 * k + s], recv_sem=scr[1].at[7 * k + s], device_id_type=MESH)
        sends = [pltpu.make_async_remote_copy(src_ref=ins[k], dst_ref=_rows(outs[k], rs[k], me), device_id=to, **sems(k, s))
                 for s, to in enumerate(others) for k in range(n)]
        recvs = [pltpu.make_async_remote_copy(src_ref=_rows(outs[k], rs[k], frm), dst_ref=_rows(outs[k], rs[k], frm),
                                              device_id=me, **sems(k, s))
                 for s, frm in enumerate(others) for k in range(n)]
        return local, sends, recvs

    def start(ins, outs, scr):
        local, sends, _ = copies(ins, outs, scr)
        for cp in local + sends:
            cp.start()

    def finish(ins, outs, scr):
        local, sends, recvs = copies(ins, outs, scr)
        for cp in recvs:
            cp.wait_recv()
        for cp in sends:
            cp.wait_send()
        for cp in local:
            cp.wait()

    out_shapes = [jax.ShapeDtypeStruct((N_DEV * s.shape[0], s.shape[1]), s.dtype) for s in shards]
    scratch = [pltpu.SemaphoreType.DMA((7 * n,)), pltpu.SemaphoreType.DMA((7 * n,)), pltpu.SemaphoreType.DMA((n,))]
    return _Task(shards, out_shapes, scratch, [(0, start), (1.0, finish)], ("all",))


def _chip_task(sums):
    n = len(sums)
    rs = [s.shape[0] // 4 for s in sums]

    def block(ref, k, chip_index):
        return ref.at[pl.ds(pl.multiple_of(chip_index * rs[k], 8), rs[k]), :]

    def copies(ins, outs, scr):
        send_sems, recv_sems, local_sems = scr
        x, y, c, chips = _place()
        here = 2 * x + y
        local = [pltpu.make_async_copy(block(ins[k], k, here), outs[k].at[here], local_sems.at[k]) for k in range(n)]
        remote = []
        for j, (px, py) in enumerate(chips):
            remote += [pltpu.make_async_remote_copy(
                src_ref=block(ins[k], k, 2 * px + py), dst_ref=outs[k].at[here],
                send_sem=send_sems.at[3 * k + j], recv_sem=recv_sems.at[3 * k + j],
                device_id=(px, py, c), device_id_type=MESH) for k in range(n)]
        return local, remote

    def start(ins, outs, scr):
        local, remote = copies(ins, outs, scr)
        for cp in local + remote:
            cp.start()

    def finish(ins, outs, scr):
        local, remote = copies(ins, outs, scr)
        for cp in remote:
            cp.wait()
        for cp in local:
            cp.wait()

    out_shapes = [jax.ShapeDtypeStruct((4, r, s.shape[1]), s.dtype) for r, s in zip(rs, sums)]
    scratch = [pltpu.SemaphoreType.DMA((3 * n,)), pltpu.SemaphoreType.DMA((3 * n,)), pltpu.SemaphoreType.DMA((n,))]
    return _Task(sums, out_shapes, scratch, [(0, start), (1.0, finish)], ("chips",))


def _dw_pair(name, a, b, scale, comm=None, blocks=1):
    T, M = a.shape
    N = b.shape[1]
    half = M // 2
    wide = half // blocks
    tk = min(2048, T)
    nK = T // tk
    plumb = _CommPlumbing(comm)

    def body(core_ref, *rest):
        a_refs, b_ref, rest = rest[:blocks], rest[blocks], rest[blocks + 1:]
        c_in = rest[:plumb.n_in]
        o_ref = rest[plumb.n_in]
        c_out = rest[plumb.n_in + 1: plumb.n_in + 1 + plumb.n_out]
        acc, stage, land, send_sem, recv_sem = rest[plumb.n_in + 1 + plumb.n_out: plumb.n_in + 6 + plumb.n_out]
        c_scr = rest[plumb.n_in + 6 + plumb.n_out:]
        i, k = pl.program_id(0), pl.program_id(1)
        x, y, c, _ = _place()
        push = pltpu.make_async_remote_copy(src_ref=stage, dst_ref=land, send_sem=send_sem, recv_sem=recv_sem,
                                            device_id=(x, y, 1 - c), device_id_type=MESH)
        plumb.handshake((i == 0) & (k == 0), own=("sibling",))
        if comm:
            plumb.run(i * nK + k, 2 * nK, True, c_in, c_out, c_scr)

        av = a_refs[0][...] if blocks == 1 else jnp.concatenate([r[...] for r in a_refs], axis=1)
        p = lax.dot_general(av, b_ref[...], _DIMS["tn"], preferred_element_type=F32)

        @pl.when(k == 0)
        def _():
            acc[...] = p

        @pl.when(k > 0)
        def _():
            acc[...] += p

        @pl.when((i == 0) & (k == nK - 1))
        def _():
            stage[...] = (scale * acc[...]).astype(BF)
            push.start()

        @pl.when((i == 1) & (k == nK - 1))
        def _():
            push.wait_recv()
            o_ref[...] = (scale * acc[...] + land[...].astype(F32)).astype(BF)
            push.wait_send()

        if comm:
            plumb.run(i * nK + k, 2 * nK, False, c_in, c_out, c_scr)

    grid_spec = pltpu.PrefetchScalarGridSpec(
        num_scalar_prefetch=1, grid=(2, nK),
        in_specs=[pl.BlockSpec((tk, wide), functools.partial(
            lambda i, k, core, j: (k, (2 * j if blocks > 1 else 0) + jnp.where(i == 0, 1 - core[0], core[0])), j=j))
            for j in range(blocks)] + [pl.BlockSpec((tk, N), lambda i, k, core: (k, 0))] + [ANY] * plumb.n_in,
        out_specs=[pl.BlockSpec((half, N), lambda i, k, core: (0, 0))] + [ANY] * plumb.n_out,
        scratch_shapes=[pltpu.VMEM((half, N), F32), pltpu.VMEM((half, N), BF), pltpu.VMEM((half, N), BF),
                        pltpu.SemaphoreType.DMA, pltpu.SemaphoreType.DMA] + plumb.scratch)
    core = lax.axis_index("c").astype(jnp.int32).reshape(1)
    res = pl.pallas_call(
        body, name=name, grid_spec=grid_spec,
        out_shape=[jax.ShapeDtypeStruct((half, N), BF)] + plumb.out_shapes,
        compiler_params=_params(("arbitrary", "arbitrary"), plumb.collective_id(own=("sibling",))),
    )(core, *([a] * blocks), b, *plumb.args)
    return (res[0], plumb.split_outputs(res[1:])) if comm else res[0]


def _pair_task(parts):
    n = len(parts)

    def copies(ins, outs, scr):
        x, y, c, _ = _place()
        return [pltpu.make_async_remote_copy(
            src_ref=ins[k].at[:, pl.ds(1 - c, 1)], dst_ref=outs[k], send_sem=scr[0].at[k], recv_sem=scr[1].at[k],
            device_id=(x, y, 1 - c), device_id_type=MESH) for k in range(n)]

    def start(ins, outs, scr):
        for cp in copies(ins, outs, scr):
            cp.start()

    def finish(ins, outs, scr):
        for cp in copies(ins, outs, scr):
            cp.wait()

    out_shapes = [jax.ShapeDtypeStruct((4, 1) + p.shape[2:], p.dtype) for p in parts]
    scratch = [pltpu.SemaphoreType.DMA((n,)), pltpu.SemaphoreType.DMA((n,))]
    return _Task(parts, out_shapes, scratch, [(0, start), (1.0, finish)], ("sibling",))


def _pair_sum(name, part, got, core):
    _, _, r, C = part.shape

    def body(core_ref, p_ref, g_ref, o_ref):
        o_ref[0] = (p_ref[0, 0].astype(F32) + g_ref[0, 0].astype(F32)).astype(o_ref.dtype)

    return pl.pallas_call(
        body, name=name,
        grid_spec=pltpu.PrefetchScalarGridSpec(
            num_scalar_prefetch=1, grid=(4,),
            in_specs=[pl.BlockSpec((1, 1, r, C), lambda i, core_ref: (i, core_ref[0], 0, 0)),
                      pl.BlockSpec((1, 1, r, C), lambda i, core_ref: (i, 0, 0, 0))],
            out_specs=pl.BlockSpec((1, r, C), lambda i, core_ref: (i, 0, 0))),
        out_shape=jax.ShapeDtypeStruct((4, r, C), part.dtype), compiler_params=_params(("parallel",)),
    )(core, part, got)


def _ffn_bwd(tag, dy, dyb, x, gain, wgT, wuT, wd, saved, earlier=None):
    n, g, u, a = saved
    half = lambda accs, ex: _swiglu_bwd_epilogue([0.5 * accs[0]], ex)
    act_args = dict(tm=1024, tn=1408, tk=D_MODEL, epilogue=half, extras=[(g, "tile", 0), (u, "tile", 0)], cols_outer=True)
    if earlier is None:
        sum_d = _dw_pair(tag + "_dw_down", a, dyb, 0.5)
        (dg, du), ((slots_d,),) = _mm(tag + "_d_act", [(dyb, wd, "nt", 0)], [BF, BF], comm=[_chip_task([sum_d])], **act_args)
        slots_e = None
        sum_g = _dw_pair(tag + "_dw_gate", dg, n, 1.0)
    else:
        sum_d, ((got,),) = _dw_pair(tag + "_dw_down", a, dyb, 0.5, comm=[_pair_task([earlier])])
        core = lax.axis_index("c").astype(jnp.int32).reshape(1)
        sum_e = _pair_sum(tag + "_pair_sum_earlier", earlier, got, core)
        sum_e = sum_e.reshape(4 * sum_e.shape[1], sum_e.shape[2])
        (dg, du), ((slots_e,),) = _mm(tag + "_d_act", [(dyb, wd, "nt", 0)], [BF, BF], comm=[_chip_task([sum_e])], **act_args)
        sum_g, ((slots_d,),) = _dw_pair(tag + "_dw_gate", dg, n, 1.0, comm=[_chip_task([sum_d])])
    sum_u, ((slots_g,),) = _dw_pair(tag + "_dw_up", du, n, 1.0, comm=[_chip_task([sum_g])])
    (dx, dxb, dgain), ((slots_u,),) = _mm(
        tag + "_d_norm", [(dg, wgT, "nn", 0), (du, wuT, "nn", 0)], [F32, BF], tm=512, tn=D_MODEL, tk=D_FF,
        epilogue=_rms_bwd_epilogue, extras=[(x, "tile", 0), (gain, "row", 0), (dy, "tile", 0)], n_colsum=1,
        comm=[_chip_task([sum_u])])
    return dx, dxb, dgain, slots_e, slots_g, slots_u, slots_d


def _tile_gain(g):
    return jnp.concatenate([g, g]).reshape(1, LANES)


def _fold_heads(partials):
    return jnp.sum(partials.reshape(-1, HEAD_DIM), axis=0)


def _pack_small_grads(grads, loss_local):
    pieces, row = [], 0
    for name, r0, _ in SMALL_LAYOUT + (("loss", LOSS_ROW, None),):
        v = (loss_local if name == "loss" else grads[name]).reshape(-1)
        rows = -(-v.size // LANES)
        block = jnp.pad(v, (0, rows * LANES - v.size)).reshape(rows, LANES)
        pieces += [jnp.zeros((r0 - row, LANES), F32)] * (r0 > row) + [block]
        row = r0 + rows
    pieces.append(jnp.zeros((SMALL_ROWS - row, LANES), F32))
    return jnp.concatenate(pieces, axis=0)


def kernel(x, ffn1_norm, ffn1_w_gate, ffn1_w_up, ffn1_w_down, mix_norm, w_in, pool_w, pool_scale, w_pool_out, q_norm, k_norm, sinks, w_attn_out, gate_bias, w_out, ffn2_norm, ffn2_w_gate, ffn2_w_up, ffn2_w_down, loss_target, m_ffn1_norm, m_ffn1_w_gate, m_ffn1_w_up, m_ffn1_w_down, m_mix_norm, m_w_in, m_pool_w, m_pool_scale, m_w_pool_out, m_q_norm, m_k_norm, m_sinks, m_w_attn_out, m_gate_bias, m_w_out, m_ffn2_norm, m_ffn2_w_gate, m_ffn2_w_up, m_ffn2_w_down, v_ffn1_norm, v_ffn1_w_gate, v_ffn1_w_up, v_ffn1_w_down, v_mix_norm, v_w_in, v_pool_w, v_pool_scale, v_w_pool_out, v_q_norm, v_k_norm, v_sinks, v_w_attn_out, v_gate_bias, v_w_out, v_ffn2_norm, v_ffn2_w_gate, v_ffn2_w_up, v_ffn2_w_down):
    T = x.shape[1]
    x2 = x.reshape(T, D_MODEL)
    target = loss_target.reshape(T, D_MODEL)

    big = [
        ("ffn1_w_gate", ffn1_w_gate, m_ffn1_w_gate, v_ffn1_w_gate, True, False),
        ("ffn1_w_up", ffn1_w_up, m_ffn1_w_up, v_ffn1_w_up, True, False),
        ("ffn1_w_down", ffn1_w_down, m_ffn1_w_down, v_ffn1_w_down, False, False),
        ("w_in", w_in, m_w_in, v_w_in, True, False),
        ("w_pool_out", w_pool_out, m_w_pool_out, v_w_pool_out, False, True),
        ("w_attn_out", w_attn_out, m_w_attn_out, v_w_attn_out, False, False),
        ("w_out", w_out, m_w_out, v_w_out, False, False),
        ("ffn2_w_gate", ffn2_w_gate, m_ffn2_w_gate, v_ffn2_w_gate, True, False),
        ("ffn2_w_up", ffn2_w_up, m_ffn2_w_up, v_ffn2_w_up, True, False),
        ("ffn2_w_down", ffn2_w_down, m_ffn2_w_down, v_ffn2_w_down, False, False),
    ]
    view = lambda a, tv: a.T if tv else a
    views = [view(w, tv) for _, w, _, _, tv, _ in big]
    in_kernel_t = [tk_ for *_, tk_ in big]
    first_shards = _prep("prep_ffn1_gate_up", views[0:2], in_kernel_t[0:2])
    g1 = ffn1_norm.reshape(1, D_MODEL)
    g2 = mix_norm.reshape(1, D_MODEL)
    g3 = ffn2_norm.reshape(1, D_MODEL)
    bias_row = gate_bias.reshape(1, 2 * D_MODEL)
    qg, kg = _tile_gain(q_norm) * ATTN_SCALE, _tile_gain(k_norm)
    scale_row = pool_scale.reshape(1, POOL_WIDTH)

    n1, later_shards, ((wg1T, wu1T),) = _rms_fwd(
        "ffn1_norm", x2, g1, [_gather_task(first_shards, forward_at=0.9)], views[2:], in_kernel_t[2:])
    shards = list(first_shards) + later_shards
    (gt1, up1, act1), ((wd1,), (w_inT,)) = _mm(
        "ffn1_gate_up", [(n1, wg1T, "nt", 0), (n1, wu1T, "nt", 1)], [BF, BF, BF], tm=1024, tn=1408, tk=D_MODEL,
        epilogue=_swiglu_fwd_epilogue, cols_outer=True,
        comm=[_gather_task(shards[2:3], forward_at=0.5), _gather_task(shards[3:4], natural=(0,), forward_at=0.9)])
    (h1, u), ((w_poT, w_ao, w_o),) = _mm(
        "ffn1_down", [(act1, wd1, "nn", 0)], [F32, BF], tm=512, tn=D_MODEL, tk=D_FF,
        epilogue=_residual_norm_epilogue(0.5), extras=[(x2, "tile", 0), (g2, "row", 0)],
        comm=[_gather_task(shards[4:7], natural=(0, 1, 2), forward_at=0.8)])
    saved1 = (n1, gt1, up1, act1)
    (proj,), ((wg2T,),) = _mm(
        "in_proj", [(u, w_inT, "nt", 0)], [BF], tm=1024, tn=1280, tk=D_MODEL, cols_outer=True,
        comm=[_gather_task(shards[7:8], forward_at=0.8)])
    pooled, mixed = _pool_fwd("pool_fwd", proj, pool_w, scale_row)
    qn = _headnorm_fwd("q_norm", proj, COL_Q, ATTN_WIDTH, qg)
    kn = _headnorm_fwd("k_norm", proj, COL_K, KV_WIDTH, kg)
    attn, ((wu2T,),) = _attn_fwd("attn_fwd", qn, kn, proj, sinks, comm=[_gather_task(shards[8:9], forward_at=0.8)])
    (bp,) = _mm("pool_out", [(mixed, w_poT, "nt", 0)], [BF], tm=1024, tn=D_MODEL, tk=POOL_WIDTH)
    gate_tn = 256
    gate_extras = [(proj, "tile", COL_GP // gate_tn), (proj, "tile", COL_GA // gate_tn),
                   (bias_row, "row", 0), (bias_row, "row", D_MODEL // gate_tn)]
    merged, ba = _mm("attn_out_merge", [(attn, w_ao, "nn", 0)], [BF, BF], tm=2048, tn=gate_tn, tk=ATTN_WIDTH,
                     epilogue=_merge_fwd_epilogue, extras=[(bp, "tile", 0)] + gate_extras)
    h2, n2 = _mm("mix_out", [(merged, w_o, "nn", 0)], [F32, BF], tm=1024, tn=D_MODEL, tk=D_MODEL,
                 epilogue=_residual_norm_epilogue(1.0), extras=[(h1, "tile", 0), (g3, "row", 0)])
    (gt2, up2, act2), ((wd2,),) = _mm(
        "ffn2_gate_up", [(n2, wg2T, "nt", 0), (n2, wu2T, "nt", 1)], [BF, BF, BF], tm=1024, tn=1408, tk=D_MODEL,
        epilogue=_swiglu_fwd_epilogue, cols_outer=True, comm=[_gather_task(shards[9:10], forward_at=0.8)])
    dy, dyb, sq = _mm("ffn2_down_loss", [(act2, wd2, "nn", 0)], [F32, BF], tm=512, tn=D_MODEL, tk=D_FF,
                      epilogue=_loss_epilogue, extras=[(h2, "tile", 0), (target, "tile", 0)], n_colsum=1)
    loss_local = 0.5 * jnp.sum(sq) / D_MODEL

    dh2, dh2b, dg3, _, slots_g2, slots_u2, slots_d2 = _ffn_bwd(
        "ffn2", dy, dyb, h2, g3, wg2T, wu2T, wd2, (n2, gt2, up2, act2))
    dbp, dba, dproj, dga, cs_gp, cs_ga = _mm(
        "mix_out_bwd", [(dh2b, w_o, "nt", 0)], [BF, BF, BF, BF], tm=2048, tn=gate_tn, tk=D_MODEL,
        epilogue=_merge_bwd_epilogue, extras=[(bp, "tile", 0), (ba, "tile", 0)] + gate_extras, n_colsum=2,
        out_placement={2: (IN_WIDTH, COL_GP)})
    sum_o = _dw_pair("dw_out", merged, dh2b, 1.0, blocks=4)
    (dmixed,) = _mm("pool_out_bwd", [(dbp, w_poT, "nn", 0)], [BF], tm=1024, tn=POOL_WIDTH, tk=D_MODEL)
    sum_po = _dw_pair("dw_pool_out", dbp, mixed, 1.0, blocks=4)
    (dattn,) = _mm("attn_out_bwd", [(dba, w_ao, "nt", 0)], [BF], tm=1024, tn=ATTN_WIDTH, tk=D_MODEL)
    sum_ao = _dw_pair("dw_attn_out", attn, dba, 1.0, blocks=4)
    (dqn, k_own, k_before, v_own, v_before, dsink_tile), ((slots_o, slots_po, slots_ao),) = _attn_bwd(
        "attn_bwd", dattn, qn, kn, proj, sinks, [_chip_task([sum_o, sum_po, sum_ao])])
    next_block = lambda a: jnp.concatenate([a[BLOCK:], jnp.zeros((BLOCK, KV_WIDTH), F32)], axis=0)
    dkn = (k_own + next_block(k_before)).astype(BF)
    dv = (v_own + next_block(v_before)).astype(BF)
    dproj, dqg = _headnorm_bwd("q_norm_bwd", dqn, proj, COL_Q, ATTN_WIDTH, qg, dproj)
    dproj, dkg = _headnorm_bwd("k_norm_bwd", dkn, proj, COL_K, KV_WIDTH, kg, dproj)
    dproj, dpool_w, dpool_scale = _pool_bwd("pool_bwd", dmixed, pooled, pool_w, scale_row, dproj)
    for piece, col in ((dv, COL_V), (dga, COL_GA)):
        dproj = lax.dynamic_update_slice(dproj, piece, (0, col))
    (dh1, dh1b, dg2), ((g_pool_w,),) = _mm(
        "in_proj_bwd", [(dproj, w_inT, "nn", 0)], [F32, BF], tm=512, tn=D_MODEL, tk=IN_WIDTH, epilogue=_rms_bwd_epilogue,
        extras=[(h1, "tile", 0), (g2, "row", 0), (dh2, "tile", 0)], n_colsum=1,
        comm=[_gather_task([dpool_w.reshape(-1, LANES)])])
    (dw_inT,) = _mm("dw_in", [(dproj, u, "tn", 0)], [BF], tm=1920, tn=D_MODEL, tk=2048)
    dx, _, dg1, slots_in, slots_g1, slots_u1, slots_d1 = _ffn_bwd(
        "ffn1", dh1, dh1b, x2, g1, wg1T, wu1T, wd1, saved1, dw_inT.reshape(4, 2, IN_WIDTH // N_DEV, D_MODEL))

    slots = [slots_g1, slots_u1, slots_d1, slots_in, slots_po, slots_ao, slots_o, slots_g2, slots_u2, slots_d2]
    big_out = {}
    for label, group in (("ffn", (0, 1, 2, 7, 8, 9)), ("w_in", (3,)), ("w_pool_out", (4,)), ("attn_out_and_out", (5, 6))):
        items = [(slots[k], view(big[k][1], big[k][4]), view(big[k][2], big[k][4]), view(big[k][3], big[k][4]))
                 for k in group]
        for k, res in zip(group, _adamw_sharded("adamw_" + label, items, transpose=big[group[0]][5])):
            big_out[big[k][0]] = tuple(view(r, big[k][4]) for r in res)

    small_grads = {
        "ffn1_norm": jnp.sum(dg1, axis=(0, 1)), "mix_norm": jnp.sum(dg2, axis=(0, 1)), "ffn2_norm": jnp.sum(dg3, axis=(0, 1)),
        "gate_bias": jnp.concatenate([jnp.sum(cs_gp, axis=(0, 1)), jnp.sum(cs_ga, axis=(0, 1))]),
        "pool_scale": dpool_scale, "q_norm": _fold_heads(dqg) * ATTN_SCALE, "k_norm": _fold_heads(dkg),
        "sinks": dsink_tile[0, :N_HEADS]}
    ((g_vec,),) = _comm_only("gather_small_grads", [_direct_gather_task([_pack_small_grads(small_grads, loss_local)])])
    given = {"ffn1_norm": (ffn1_norm, m_ffn1_norm, v_ffn1_norm), "mix_norm": (mix_norm, m_mix_norm, v_mix_norm),
             "ffn2_norm": (ffn2_norm, m_ffn2_norm, v_ffn2_norm), "gate_bias": (gate_bias, m_gate_bias, v_gate_bias),
             "pool_scale": (pool_scale, m_pool_scale, v_pool_scale), "q_norm": (q_norm, m_q_norm, v_q_norm),
             "k_norm": (k_norm, m_k_norm, v_k_norm), "sinks": (sinks, m_sinks, v_sinks)}
    params = [tuple(a.reshape(shape) for a in given[nm]) for nm, _, shape in SMALL_LAYOUT]
    params.append(tuple(a.reshape(-1, LANES) for a in (pool_w, m_pool_w, v_pool_w)))
    small_res, loss_row = _adamw_small("adamw_small", g_vec.reshape(N_DEV, SMALL_ROWS, LANES),
                                       g_pool_w.reshape(N_DEV, -1, LANES), params)
    small_out = {nm: tuple(r.reshape(given[nm][0].shape) for r in res)
                 for (nm, _, _), res in zip(SMALL_LAYOUT, small_res)}
    small_out["pool_w"] = tuple(r.reshape(pool_w.shape) for r in small_res[-1])
    loss = loss_row[0, 0]

    order = ["ffn1_norm", "ffn1_w_gate", "ffn1_w_up", "ffn1_w_down", "mix_norm", "w_in", "pool_w", "pool_scale",
             "w_pool_out", "q_norm", "k_norm", "sinks", "w_attn_out", "gate_bias", "w_out", "ffn2_norm",
             "ffn2_w_gate", "ffn2_w_up", "ffn2_w_down"]
    every = {**big_out, **small_out}
    outs = [loss, dx.reshape(x.shape)]
    for j in range(4):
        outs += [every[nm][j] for nm in order]
    return tuple(outs)
```

```python
import functools

import jax
import jax.numpy as jnp
from jax import lax
from jax.experimental import pallas as pl
from jax.experimental.pallas import tpu as pltpu

BF = jnp.bfloat16
F32 = jnp.float32

D_MODEL = 1024
D_FF = 2816
POOL_WIDTH = 512
POOL_GROUP = 128
N_POOL_GROUPS = 4
HEAD_DIM = 64
N_HEADS = 16
GQA_GROUP = 8
BLOCK = 128
ATTN_WIDTH = 1024
KV_WIDTH = 128
IN_WIDTH = 3840
RMS_EPS = 1e-6
N_DEV = 8
LANES = 128

COL_Q = POOL_WIDTH
COL_K = COL_Q + ATTN_WIDTH
COL_V = COL_K + KV_WIDTH
COL_GP = COL_V + KV_WIDTH
COL_GA = COL_GP + D_MODEL

ADAM_LR = 0.001
ADAM_B1 = 0.9
ADAM_B2 = 0.999
ADAM_EPS = 1e-08
ADAM_WD = 0.01
ADAM_STEP = 10

VMEM_LIMIT_V7X = 60 * 1024 * 1024
MESH = pl.DeviceIdType.MESH
ANY = pl.BlockSpec(memory_space=pl.ANY)


def _params(sem=None, collective_id=None):
    return pltpu.CompilerParams(dimension_semantics=sem, vmem_limit_bytes=VMEM_LIMIT_V7X, collective_id=collective_id)


COLLECTIVE_IDS = {frozenset(["sibling"]): 0, frozenset(["chips"]): 1, frozenset(["sibling", "chips"]): 2}


def _handshake(peer_kinds):
    x, y, c, chips = _place()
    peers = ([(x, y, 1 - c)] if "sibling" in peer_kinds else []) + ([(*chip, c) for chip in chips] if "chips" in peer_kinds else [])
    barrier = pltpu.get_barrier_semaphore()
    for peer in peers:
        pl.semaphore_signal(barrier, inc=1, device_id=peer, device_id_type=MESH)
    pl.semaphore_wait(barrier, len(peers))


_DIMS = {"nt": (((1,), (1,)), ((), ())), "nn": (((1,), (0,)), ((), ())), "tn": (((0,), (0,)), ((), ()))}


class _Task:
    def __init__(self, inputs, out_shapes, scratch, phases, peers):
        self.inputs, self.out_shapes, self.scratch = list(inputs), list(out_shapes), list(scratch)
        self.phases = list(phases)
        self.peers = frozenset(peers)


class _CommPlumbing:
    def __init__(self, tasks):
        self.tasks = list(tasks or [])
        self.args = [a for t in self.tasks for a in t.inputs]
        self.out_shapes = [o for t in self.tasks for o in t.out_shapes]
        self.scratch = [s for t in self.tasks for s in t.scratch]
        self.n_in, self.n_out = len(self.args), len(self.out_shapes)

    def peer_kinds(self, own=()):
        kinds = frozenset(own).union(*[t.peers for t in self.tasks])
        return None if "all" in kinds or not kinds else kinds

    def collective_id(self, own=()):
        kinds = self.peer_kinds(own)
        return None if kinds is None else COLLECTIVE_IDS[kinds]

    def handshake(self, first, own=()):
        kinds = self.peer_kinds(own)
        if kinds is not None:
            pl.when(first)(functools.partial(_handshake, kinds))

    def _slices(self, c_in, c_out, c_scr):
        i = o = s = 0
        for t in self.tasks:
            yield t, c_in[i:i + len(t.inputs)], c_out[o:o + len(t.out_shapes)], c_scr[s:s + len(t.scratch)]
            i, o, s = i + len(t.inputs), o + len(t.out_shapes), s + len(t.scratch)

    def run(self, step, steps, before, c_in, c_out, c_scr):
        for t, ins, outs, scr in self._slices(c_in, c_out, c_scr):
            for frac, fn in t.phases:
                if step is None:
                    fn(ins, outs, scr)
                elif before == (frac == 0):
                    at = 0 if frac == 0 else max(0, min(steps, -(-int(round(frac * steps * 64)) // 64)) - 1)
                    pl.when(step == at)(functools.partial(fn, ins, outs, scr))

    def split_outputs(self, flat):
        res, o = [], 0
        for t in self.tasks:
            res.append(list(flat[o:o + len(t.out_shapes)]))
            o += len(t.out_shapes)
        return res


def _comm_only(name, tasks):
    plumb = _CommPlumbing(tasks)

    def body(*refs):
        c_in, c_out = refs[:plumb.n_in], refs[plumb.n_in: plumb.n_in + plumb.n_out]
        c_scr = refs[plumb.n_in + plumb.n_out:]
        plumb.run(None, 1, True, c_in, c_out, c_scr)

    res = pl.pallas_call(
        body, name=name, in_specs=[ANY] * plumb.n_in, out_specs=[ANY] * plumb.n_out, out_shape=plumb.out_shapes,
        scratch_shapes=plumb.scratch, compiler_params=pltpu.CompilerParams(has_side_effects=True),
    )(*plumb.args)
    return plumb.split_outputs(res)


def _mm(name, terms, out_dtypes, *, tm, tn, tk, epilogue=None, extras=(), n_colsum=0, comm=None, cols_outer=False,
        out_placement=None):
    a0, b0, mode0, _ = terms[0]
    if mode0 == "nt":
        (M, K), N = a0.shape, b0.shape[0]
    elif mode0 == "nn":
        (M, K), N = a0.shape, b0.shape[1]
    else:
        (K, M), N = a0.shape, b0.shape[1]
    tm, tn, tk = min(tm, M), min(tn, N), min(tk, K)
    assert M % tm == 0 and N % tn == 0 and K % tk == 0, (name, M, N, K, tm, tn, tk)
    nI, nJ, nK = M // tm, N // tn, K // tk
    n_terms = len(terms)
    n_acc = max(t[3] for t in terms) + 1
    n_ex = len(extras)
    n_out = len(out_dtypes)
    if epilogue is None:
        epilogue = lambda accs, ex: ([accs[0]], [])
    plumb = _CommPlumbing(comm)
    n_scr = n_acc if nK > 1 else 0
    grid = (nJ, nI, nK) if cols_outer else (nI, nJ, nK)

    def body(*refs):
        n_in = 2 * n_terms + n_ex
        ab = refs[: 2 * n_terms]
        ex_refs = refs[2 * n_terms: n_in]
        c_in = refs[n_in: n_in + plumb.n_in]
        o0 = n_in + plumb.n_in
        out_refs = refs[o0: o0 + n_out]
        cs_refs = refs[o0 + n_out: o0 + n_out + n_colsum]
        c_out = refs[o0 + n_out + n_colsum: o0 + n_out + n_colsum + plumb.n_out]
        s0 = o0 + n_out + n_colsum + plumb.n_out
        acc_refs = refs[s0: s0 + n_scr]
        c_scr = refs[s0 + n_scr:]
        steps = grid[0] * grid[1] * nK
        if comm:
            step = (pl.program_id(0) * grid[1] + pl.program_id(1)) * nK + pl.program_id(2)
            plumb.handshake(step == 0)
            plumb.run(step, steps, True, c_in, c_out, c_scr)

        def products():
            accs = [None] * n_acc
            for t, (_, _, mode, ai) in enumerate(terms):
                p = lax.dot_general(ab[2 * t][...], ab[2 * t + 1][...], _DIMS[mode], preferred_element_type=F32)
                accs[ai] = p if accs[ai] is None else accs[ai] + p
            return accs

        def finish(accs):
            outs, colsums = epilogue(accs, [r[...] for r in ex_refs])
            for r, o in zip(out_refs, outs):
                r[...] = o.astype(r.dtype)
            for r, cs in zip(cs_refs, colsums):
                r[...] = jnp.sum(cs, axis=0, keepdims=True).reshape(r.shape)

        if nK == 1:
            finish(products())
        else:
            k = pl.program_id(2)
            accs = products()

            @pl.when(k == 0)
            def _():
                for r, a in zip(acc_refs, accs):
                    r[...] = a

            @pl.when(k > 0)
            def _():
                for r, a in zip(acc_refs, accs):
                    r[...] += a

            @pl.when(k == nK - 1)
            def _():
                finish([r[...] for r in acc_refs])

        if comm:
            plumb.run(step, steps, False, c_in, c_out, c_scr)

    def spec(block, index, fixed=False):
        imap = (lambda q, p, k: index(p, q, k)) if cols_outer else index
        return pl.BlockSpec(block, imap, pipeline_mode=pl.Buffered(1)) if fixed else pl.BlockSpec(block, imap)

    in_specs, args = [], []
    for a, b, mode, _ in terms:
        if mode == "nt":
            in_specs += [spec((tm, tk), lambda i, j, k: (i, k), nI * nK == 1),
                         spec((tn, tk), lambda i, j, k: (j, k), nJ * nK == 1)]
        elif mode == "nn":
            in_specs += [spec((tm, tk), lambda i, j, k: (i, k), nI * nK == 1),
                         spec((tk, tn), lambda i, j, k: (k, j), nJ * nK == 1)]
        else:
            in_specs += [spec((tk, tm), lambda i, j, k: (k, i), nI * nK == 1),
                         spec((tk, tn), lambda i, j, k: (k, j), nJ * nK == 1)]
        args += [a, b]
    for arr, kind, off in extras:
        if kind == "tile":
            in_specs.append(spec((tm, tn), functools.partial(lambda i, j, k, off: (i, j + off), off=off)))
        else:
            in_specs.append(spec((1, tn), functools.partial(lambda i, j, k, off: (0, j + off), off=off)))
        args.append(arr)
    placed = dict(out_placement or {})
    out_shape = [jax.ShapeDtypeStruct((M, placed.get(o, (N, 0))[0]), dt) for o, dt in enumerate(out_dtypes)]
    out_specs = [spec((tm, tn), functools.partial(lambda i, j, k, off: (i, j + off), off=placed.get(o, (N, 0))[1] // tn))
                 for o in range(n_out)]
    out_shape += [jax.ShapeDtypeStruct((nI, 1, N), F32) for _ in range(n_colsum)]
    out_specs += [spec((1, 1, tn), lambda i, j, k: (i, 0, j)) for _ in range(n_colsum)]
    scratch = [pltpu.VMEM((tm, tn), F32) for _ in range(n_scr)]
    args += plumb.args
    in_specs += [ANY] * plumb.n_in
    out_shape += plumb.out_shapes
    out_specs += [ANY] * plumb.n_out
    sem = ("arbitrary",) * 3 if comm else ("parallel", "parallel", "arbitrary")
    res = pl.pallas_call(
        body, name=name, grid=grid, in_specs=in_specs, out_specs=out_specs, out_shape=out_shape,
        scratch_shapes=scratch + plumb.scratch, compiler_params=_params(sem, plumb.collective_id()),
    )(*args)
    n_own = n_out + n_colsum
    return (list(res[:n_own]), plumb.split_outputs(res[n_own:])) if comm is not None else res


ROW_TILE = 512


def _rms_fwd(name, x, g, comm, weights, transposes):
    T, D = x.shape
    steps = T // ROW_TILE
    plumb = _CommPlumbing(comm)
    nw = len(weights)

    def body(x_ref, g_ref, *rest):
        w_refs, c_in = rest[:nw], rest[nw: nw + plumb.n_in]
        o_ref, shard_refs = rest[nw + plumb.n_in], rest[nw + plumb.n_in + 1: 2 * nw + plumb.n_in + 1]
        c_out = rest[2 * nw + plumb.n_in + 1: 2 * nw + plumb.n_in + 1 + plumb.n_out]
        c_scr = rest[2 * nw + plumb.n_in + 1 + plumb.n_out:]
        plumb.handshake(pl.program_id(0) == 0)
        plumb.run(pl.program_id(0), steps, True, c_in, c_out, c_scr)

        @pl.when(pl.program_id(0) == 0)
        def _():
            for w_ref, s_ref, tr in zip(w_refs, shard_refs, transposes):
                v = w_ref[...]
                s_ref[...] = (v.T if tr else v).astype(BF)

        xv = x_ref[...]
        r = lax.rsqrt(jnp.mean(xv * xv, axis=-1, keepdims=True) + RMS_EPS)
        o_ref[...] = (xv * r * g_ref[...]).astype(BF)
        plumb.run(pl.program_id(0), steps, False, c_in, c_out, c_scr)

    row = pl.BlockSpec((ROW_TILE, D), lambda i: (i, 0))
    whole = lambda shape: pl.BlockSpec(shape, lambda i: (0, 0), pipeline_mode=pl.Buffered(1))
    shard_shapes = [w.shape[::-1] if tr else w.shape for w, tr in zip(weights, transposes)]
    res = pl.pallas_call(
        body, name=name, grid=(steps,),
        in_specs=[row, pl.BlockSpec((1, D), lambda i: (0, 0))] + [whole(w.shape) for w in weights] + [ANY] * plumb.n_in,
        out_specs=[row] + [whole(s) for s in shard_shapes] + [ANY] * plumb.n_out,
        out_shape=[jax.ShapeDtypeStruct((T, D), BF)] + [jax.ShapeDtypeStruct(s, BF) for s in shard_shapes] + plumb.out_shapes,
        scratch_shapes=plumb.scratch, compiler_params=_params(("arbitrary",), plumb.collective_id()),
    )(x, g, *weights, *plumb.args)
    return res[0], list(res[1: nw + 1]), plumb.split_outputs(res[nw + 1:])


HEADNORM_TILE = 2048


def _half_sum_matrix():
    r = lax.broadcasted_iota(jnp.int32, (LANES, LANES), 0) // HEAD_DIM
    c = lax.broadcasted_iota(jnp.int32, (LANES, LANES), 1) // HEAD_DIM
    return (r == c).astype(BF)


def _head_mean(v, ones_blockdiag):
    hi = v.astype(BF)
    lo = (v - hi.astype(F32)).astype(BF)
    s = jnp.dot(hi, ones_blockdiag, preferred_element_type=F32) + jnp.dot(lo, ones_blockdiag, preferred_element_type=F32)
    return s * (1.0 / HEAD_DIM)


def _headnorm_fwd(name, proj, col0, width, g2):
    T = proj.shape[0]
    wide = min(width, GROUP_WIDTH)
    nb, off = width // wide, col0 // wide

    def body(x_ref, g_ref, b_ref, o_ref):
        for s in range(wide // LANES):
            lanes = slice(LANES * s, LANES * (s + 1))
            xv = x_ref[:, lanes].astype(F32)
            r = lax.rsqrt(_head_mean(xv * xv, b_ref[...]) + RMS_EPS)
            o_ref[:, lanes] = (xv * r * g_ref[...]).astype(BF)

    return pl.pallas_call(
        body, name=name, grid=(T // HEADNORM_TILE, nb),
        in_specs=[pl.BlockSpec((HEADNORM_TILE, wide), lambda i, j: (i, j + off)),
                  pl.BlockSpec((1, LANES), lambda i, j: (0, 0)), pl.BlockSpec((LANES, LANES), lambda i, j: (0, 0))],
        out_specs=pl.BlockSpec((HEADNORM_TILE, wide), lambda i, j: (i, j)),
        out_shape=jax.ShapeDtypeStruct((T, width), BF), compiler_params=_params(("parallel", "parallel")),
    )(proj, g2, _half_sum_matrix())


def _headnorm_bwd(name, dy, proj, col0, width, g2, into):
    T = proj.shape[0]
    wide = min(width, GROUP_WIDTH)
    nb, off = width // wide, col0 // wide

    def body(dy_ref, x_ref, g_ref, b_ref, into_ref, dx_ref, dg_ref):
        for s in range(wide // LANES):
            lanes = slice(LANES * s, LANES * (s + 1))
            xv = x_ref[:, lanes].astype(F32)
            dyv = dy_ref[:, lanes].astype(F32)
            r = lax.rsqrt(_head_mean(xv * xv, b_ref[...]) + RMS_EPS)
            xhat = xv * r
            dxhat = dyv * g_ref[...]
            dx_ref[:, lanes] = (r * (dxhat - xhat * _head_mean(dxhat * xhat, b_ref[...]))).astype(BF)
            dg_ref[0, :, lanes] = jnp.sum(dyv * xhat, axis=0, keepdims=True)

    return pl.pallas_call(
        body, name=name, grid=(T // HEADNORM_TILE, nb),
        in_specs=[pl.BlockSpec((HEADNORM_TILE, wide), lambda i, j: (i, j)),
                  pl.BlockSpec((HEADNORM_TILE, wide), lambda i, j: (i, j + off)),
                  pl.BlockSpec((1, LANES), lambda i, j: (0, 0)), pl.BlockSpec((LANES, LANES), lambda i, j: (0, 0)), ANY],
        out_specs=[pl.BlockSpec((HEADNORM_TILE, wide), lambda i, j: (i, j + off)),
                   pl.BlockSpec((1, 1, wide), lambda i, j: (i, 0, j))],
        out_shape=[jax.ShapeDtypeStruct(into.shape, BF), jax.ShapeDtypeStruct((T // HEADNORM_TILE, 1, width), F32)],
        input_output_aliases={4: 0}, compiler_params=_params(("parallel", "parallel")),
    )(dy, proj, g2, _half_sum_matrix(), into)


def _shift_down(v, k, row):
    return jnp.where(row >= k, pltpu.roll(v, k, axis=0), 0.0)


def _shift_up(v, k, row, T):
    return jnp.where(row < T - k, pltpu.roll(v, T - k, axis=0), 0.0)


def _by_group(g, vals):
    out = vals[-1]
    for i in range(len(vals) - 2, -1, -1):
        out = jnp.where(g == i, vals[i], out)
    return out


def _pool_fwd(name, proj, pool_w, pool_scale):
    T = proj.shape[0]

    def body(x_ref, w_ref, s_ref, pooled_ref, mixed_ref):
        g = pl.program_id(0)
        xv = x_ref[...].astype(F32)
        row = lax.broadcasted_iota(jnp.int32, (T, 1), 0)
        s2 = xv + _shift_down(xv, 1, row)
        s4 = s2 + _shift_down(s2, 2, row)
        s8 = s4 + _shift_down(s4, 4, row)
        s16 = s8 + _shift_down(s8, 8, row)
        wsum = _by_group(g, [s2, s4, s8, s16])
        count = jnp.minimum(row + 1, 2 << g).astype(F32)
        pooled = (wsum / count - xv).astype(BF)
        pooled_ref[...] = pooled
        mixed = jnp.dot(pooled, w_ref[0].astype(BF), preferred_element_type=F32) * s_ref[...]
        mixed_ref[...] = mixed.astype(BF)

    col = pl.BlockSpec((T, POOL_GROUP), lambda g: (0, g))
    return pl.pallas_call(
        body, name=name, grid=(N_POOL_GROUPS,),
        in_specs=[col, pl.BlockSpec((1, POOL_GROUP, POOL_GROUP), lambda g: (g, 0, 0)),
                  pl.BlockSpec((1, POOL_GROUP), lambda g: (0, g))],
        out_specs=[col, col],
        out_shape=[jax.ShapeDtypeStruct((T, POOL_WIDTH), BF), jax.ShapeDtypeStruct((T, POOL_WIDTH), BF)],
        compiler_params=_params(("parallel",)),
    )(proj, pool_w, pool_scale)


def _pool_bwd(name, dmixed, pooled, pool_w, pool_scale, into):
    T = dmixed.shape[0]

    def body(dm_ref, p_ref, w_ref, s_ref, into_ref, dx_ref, dw_ref, ds_ref):
        g = pl.program_id(0)
        dm = dm_ref[...].astype(F32)
        pooled = p_ref[...]
        w = w_ref[0].astype(BF)
        pre = jnp.dot(pooled, w, preferred_element_type=F32)
        ds_ref[...] = jnp.sum(dm * pre, axis=0, keepdims=True)
        dms = (dm * s_ref[...]).astype(BF)
        dw_ref[0] = lax.dot_general(pooled, dms, _DIMS["tn"], preferred_element_type=F32)
        dpooled = lax.dot_general(dms, w, _DIMS["nt"], preferred_element_type=F32)
        row = lax.broadcasted_iota(jnp.int32, (T, 1), 0)
        count = jnp.minimum(row + 1, 2 << g).astype(F32)
        z = dpooled / count
        l2 = z + _shift_up(z, 1, row, T)
        l4 = l2 + _shift_up(l2, 2, row, T)
        l8 = l4 + _shift_up(l4, 4, row, T)
        l16 = l8 + _shift_up(l8, 8, row, T)
        dx_ref[...] = (_by_group(g, [l2, l4, l8, l16]) - dpooled).astype(BF)

    col = pl.BlockSpec((T, POOL_GROUP), lambda g: (0, g))
    wspec = pl.BlockSpec((1, POOL_GROUP, POOL_GROUP), lambda g: (g, 0, 0))
    sspec = pl.BlockSpec((1, POOL_GROUP), lambda g: (0, g))
    return pl.pallas_call(
        body, name=name, grid=(N_POOL_GROUPS,), in_specs=[col, col, wspec, sspec, ANY], out_specs=[col, wspec, sspec],
        out_shape=[jax.ShapeDtypeStruct(into.shape, BF),
                   jax.ShapeDtypeStruct((N_POOL_GROUPS, POOL_GROUP, POOL_GROUP), F32),
                   jax.ShapeDtypeStruct((1, POOL_WIDTH), F32)],
        input_output_aliases={4: 0}, compiler_params=_params(("parallel",)),
    )(dmixed, pooled, pool_w, pool_scale, into)


ATTN_SCALE = HEAD_DIM ** -0.5
MASKED = float(jnp.finfo(jnp.float32).min)
KV_COL_BLOCK_V = COL_V // LANES
GROUP_WIDTH = GQA_GROUP * HEAD_DIM


def _dup_head(v, j):
    half = lax.broadcasted_iota(jnp.int32, (1, LANES), 1) // HEAD_DIM
    return jnp.where(half == j, v, pltpu.roll(v, HEAD_DIM, axis=1))


def _stack_heads(v, low):
    pieces = []
    for p in range(GROUP_WIDTH // LANES):
        vp = v[:, LANES * p: LANES * (p + 1)]
        pieces.append(jnp.where(low, vp, jnp.zeros_like(vp)))
        pieces.append(jnp.where(low, jnp.zeros_like(vp), vp))
    return jnp.concatenate(pieces, axis=0)


def _unstack_transposed(t, low):
    pairs = []
    for p in range(GROUP_WIDTH // LANES):
        even = t[:, BLOCK * (2 * p): BLOCK * (2 * p + 1)].T
        odd = t[:, BLOCK * (2 * p + 1): BLOCK * (2 * p + 2)].T
        pairs.append(jnp.where(low, even, odd))
    return pairs


STACKED = GQA_GROUP * BLOCK


def _band_bias():
    key = lax.broadcasted_iota(jnp.int32, (2, 2 * BLOCK, STACKED), 1)
    qry = lax.broadcasted_iota(jnp.int32, (2, 2 * BLOCK, STACKED), 2) % BLOCK
    first = lax.broadcasted_iota(jnp.int32, (2, 2 * BLOCK, STACKED), 0) == 0
    valid = (key > qry) & (key <= qry + BLOCK) & (jnp.logical_not(first) | (key >= BLOCK))
    return jnp.where(valid, 0.0, MASKED).astype(F32)


def _softmax_keys_on_sublanes(k2, q, bias, sink_ref, j):
    head_of_lane = lax.broadcasted_iota(jnp.int32, (1, STACKED), 1) // BLOCK
    sink = jnp.zeros((1, STACKED), F32)
    for h in range(GQA_GROUP):
        sink = jnp.where(head_of_lane == h, sink_ref[j * GQA_GROUP + h], sink)
    s = lax.dot_general(k2, q, _DIMS["nt"], preferred_element_type=F32) + bias
    m = jnp.maximum(jnp.max(s, axis=0, keepdims=True), sink)
    e = jnp.exp(s - m)
    e_sink = jnp.exp(sink - m)
    inv = 1.0 / (jnp.sum(e, axis=0, keepdims=True) + e_sink)
    return e * inv, e_sink * inv


def _attn_fwd(name, qn, kn, proj, sinks, comm=None):
    T = qn.shape[0]
    nb = T // BLOCK
    plumb = _CommPlumbing(comm)

    def body(sink_ref, bias_ref, q_ref, kp_ref, kc_ref, vp_ref, vc_ref, *rest):
        c_in, o_ref = rest[:plumb.n_in], rest[plumb.n_in]
        c_out, c_scr = rest[plumb.n_in + 1: plumb.n_in + 1 + plumb.n_out], rest[plumb.n_in + 1 + plumb.n_out:]
        m = pl.program_id(0)
        plumb.handshake(m == 0)
        plumb.run(m, nb // 2, True, c_in, c_out, c_scr)
        low = lax.broadcasted_iota(jnp.int32, (1, LANES), 1) < HEAD_DIM
        k_pair, v_pair = kc_ref[...], vc_ref[...]
        for b in range(2):
            rows = slice(BLOCK * b, BLOCK * (b + 1))
            kk = k_pair if b else jnp.concatenate([kp_ref[...], k_pair[0:BLOCK]], axis=0)
            vv = v_pair if b else jnp.concatenate([vp_ref[...], v_pair[0:BLOCK]], axis=0)
            bias = bias_ref[1] if b else bias_ref[jnp.minimum(m, 1)]
            for j in range(2):
                q = _stack_heads(q_ref[rows, GROUP_WIDTH * j: GROUP_WIDTH * (j + 1)], low)
                p, _ = _softmax_keys_on_sublanes(_dup_head(kk, j), q, bias, sink_ref, j)
                o_t = lax.dot_general(_dup_head(vv, j), p.astype(BF), _DIMS["tn"], preferred_element_type=F32)
                for pair, o in enumerate(_unstack_transposed(o_t, low)):
                    lanes = slice(GROUP_WIDTH * j + LANES * pair, GROUP_WIDTH * j + LANES * (pair + 1))
                    o_ref[rows, lanes] = o.astype(BF)
        plumb.run(m, nb // 2, False, c_in, c_out, c_scr)

    wide = pl.BlockSpec((2 * BLOCK, ATTN_WIDTH), lambda m: (m, 0))
    before = lambda m: jnp.maximum(2 * m - 1, 0)
    res = pl.pallas_call(
        body, name=name, grid=(nb // 2,),
        in_specs=[pl.BlockSpec(memory_space=pltpu.SMEM),
                  pl.BlockSpec((2, 2 * BLOCK, STACKED), lambda m: (0, 0, 0)), wide,
                  pl.BlockSpec((BLOCK, LANES), lambda m: (before(m), 0)),
                  pl.BlockSpec((2 * BLOCK, LANES), lambda m: (m, 0)),
                  pl.BlockSpec((BLOCK, LANES), lambda m: (before(m), KV_COL_BLOCK_V)),
                  pl.BlockSpec((2 * BLOCK, LANES), lambda m: (m, KV_COL_BLOCK_V))] + [ANY] * plumb.n_in,
        out_specs=[wide] + [ANY] * plumb.n_out,
        out_shape=[jax.ShapeDtypeStruct((T, ATTN_WIDTH), BF)] + plumb.out_shapes, scratch_shapes=plumb.scratch,
        compiler_params=_params(("arbitrary",) if comm else ("parallel",), plumb.collective_id()),
    )(sinks, _band_bias(), qn, kn, kn, proj, proj, *plumb.args)
    return (res[0], plumb.split_outputs(res[1:])) if comm is not None else res[0]


def _attn_bwd(name, dout, qn, kn, proj, sinks, comm):
    T = qn.shape[0]
    nb = T // BLOCK
    plumb = _CommPlumbing(comm)

    def body(sink_ref, bias_ref, do_ref, q_ref, kp_ref, kc_ref, vp_ref, vc_ref, *rest):
        c_in = rest[:plumb.n_in]
        dq_ref, k_own, k_before, v_own, v_before, dsink_ref = rest[plumb.n_in: plumb.n_in + 6]
        c_out, c_scr = rest[plumb.n_in + 6: plumb.n_in + 6 + plumb.n_out], rest[plumb.n_in + 6 + plumb.n_out:]
        m = pl.program_id(0)
        plumb.handshake(m == 0)
        plumb.run(m, nb // 2, True, c_in, c_out, c_scr)
        lane = lax.broadcasted_iota(jnp.int32, (1, LANES), 1)
        low = lane < HEAD_DIM

        @pl.when(m == 0)
        def _():
            dsink_ref[...] = jnp.zeros_like(dsink_ref)

        k_pair, v_pair = kc_ref[...], vc_ref[...]
        dsink = jnp.zeros((1, LANES), F32)
        for b in range(2):
            rows = slice(BLOCK * b, BLOCK * (b + 1))
            kk = k_pair if b else jnp.concatenate([kp_ref[...], k_pair[0:BLOCK]], axis=0)
            vv = v_pair if b else jnp.concatenate([vp_ref[...], v_pair[0:BLOCK]], axis=0)
            bias = bias_ref[1] if b else bias_ref[jnp.minimum(m, 1)]
            dk_tot = jnp.zeros((2 * BLOCK, LANES), F32)
            dv_tot = jnp.zeros((2 * BLOCK, LANES), F32)
            for j in range(2):
                k2 = _dup_head(kk, j)
                v2 = _dup_head(vv, j)
                q = _stack_heads(q_ref[rows, GROUP_WIDTH * j: GROUP_WIDTH * (j + 1)], low)
                do = _stack_heads(do_ref[rows, GROUP_WIDTH * j: GROUP_WIDTH * (j + 1)], low)
                p, psink = _softmax_keys_on_sublanes(k2, q, bias, sink_ref, j)
                dp =lax.dot_general(v2, do, _DIMS["nt"], preferred_element_type=F32)
                delta = jnp.sum(p * dp, axis=0, keepdims=True)
                ds = (p * (dp - delta)).astype(BF)
                dk2 = jnp.dot(ds, q, preferred_element_type=F32)
                dv2 = jnp.dot(p.astype(BF), do, preferred_element_type=F32)
                dq_t = lax.dot_general(k2, ds, _DIMS["tn"], preferred_element_type=F32)
                for pair, dq in enumerate(_unstack_transposed(dq_t, low)):
                    lanes = slice(GROUP_WIDTH * j + LANES * pair, GROUP_WIDTH * j + LANES * (pair + 1))
                    dq_ref[rows, lanes] = dq.astype(BF)
                mine = low if j == 0 else jnp.logical_not(low)
                dk_tot = dk_tot + jnp.where(mine, dk2 + pltpu.roll(dk2, HEAD_DIM, axis=1), 0.0)
                dv_tot = dv_tot + jnp.where(mine, dv2 + pltpu.roll(dv2, HEAD_DIM, axis=1), 0.0)
                sink_term = psink * delta
                for h in range(GQA_GROUP):
                    val = -jnp.sum(sink_term[:, BLOCK * h: BLOCK * (h + 1)], axis=1, keepdims=True)
                    dsink = dsink + jnp.where(lane == j * GQA_GROUP + h, val, 0.0)
            k_before[rows, :], k_own[rows, :] = dk_tot[0:BLOCK], dk_tot[BLOCK:]
            v_before[rows, :], v_own[rows, :] = dv_tot[0:BLOCK], dv_tot[BLOCK:]
        dsink_ref[0:1, :] += dsink
        plumb.run(m, nb // 2, False, c_in, c_out, c_scr)

    wide = pl.BlockSpec((2 * BLOCK, ATTN_WIDTH), lambda m: (m, 0))
    pair = pl.BlockSpec((2 * BLOCK, LANES), lambda m: (m, 0))
    before = lambda m: jnp.maximum(2 * m - 1, 0)
    res = pl.pallas_call(
        body, name=name, grid=(nb // 2,),
        in_specs=[pl.BlockSpec(memory_space=pltpu.SMEM),
                  pl.BlockSpec((2, 2 * BLOCK, STACKED), lambda m: (0, 0, 0)), wide, wide,
                  pl.BlockSpec((BLOCK, LANES), lambda m: (before(m), 0)), pair,
                  pl.BlockSpec((BLOCK, LANES), lambda m: (before(m), KV_COL_BLOCK_V)),
                  pl.BlockSpec((2 * BLOCK, LANES), lambda m: (m, KV_COL_BLOCK_V))] + [ANY] * plumb.n_in,
        out_specs=[wide, pair, pair, pair, pair, pl.BlockSpec((8, LANES), lambda m: (0, 0))] + [ANY] * plumb.n_out,
        out_shape=[jax.ShapeDtypeStruct((T, ATTN_WIDTH), BF)] + [jax.ShapeDtypeStruct((T, KV_WIDTH), F32)] * 4
        + [jax.ShapeDtypeStruct((8, LANES), F32)] + plumb.out_shapes,
        scratch_shapes=plumb.scratch, compiler_params=_params(("arbitrary",), plumb.collective_id()),
    )(sinks, _band_bias(), dout, qn, kn, kn, proj, proj, *plumb.args)
    return list(res[:6]), plumb.split_outputs(res[6:])


def _swiglu_fwd_epilogue(accs, ex):
    g, u = accs
    return [g, u, g * jax.nn.sigmoid(g) * u], []


def _swiglu_bwd_epilogue(accs, ex):
    (da,) = accs
    g, u = ex[0].astype(F32), ex[1].astype(F32)
    s = jax.nn.sigmoid(g)
    gs = g * s
    return [da * u * (s + gs - gs * s), da * gs], []


def _residual_norm_epilogue(scale):
    def epilogue(accs, ex):
        res, gain = ex
        h = res + scale * accs[0]
        r = lax.rsqrt(jnp.mean(h * h, axis=-1, keepdims=True) + RMS_EPS)
        return [h, h * r * gain], []
    return epilogue


def _rms_bwd_epilogue(accs, ex):
    (dn,) = accs
    xv, g, dres = ex
    r = lax.rsqrt(jnp.mean(xv * xv, axis=-1, keepdims=True) + RMS_EPS)
    xhat = xv * r
    dxhat = dn * g
    dx = dres + r * (dxhat - xhat * jnp.mean(dxhat * xhat, axis=-1, keepdims=True))
    return [dx, dx], [dn * xhat]


def _loss_epilogue(accs, ex):
    xv, target = ex
    d = xv + 0.5 * accs[0] - target
    dy = d * (1.0 / D_MODEL)
    return [dy, dy], [d * d]


def _merge_fwd_epilogue(accs, ex):
    (ba,) = accs
    bp, gp_pre, ga_pre, bias_p, bias_a = ex
    gp = jax.nn.sigmoid(gp_pre.astype(F32) + bias_p)
    ga = jax.nn.sigmoid(ga_pre.astype(F32) + bias_a)
    return [gp * bp.astype(F32) + ga * ba, ba], []


def _merge_bwd_epilogue(accs, ex):
    (dm,) = accs
    bp, ba, gp_pre, ga_pre, bias_p, bias_a = ex
    gp = jax.nn.sigmoid(gp_pre.astype(F32) + bias_p)
    ga = jax.nn.sigmoid(ga_pre.astype(F32) + bias_a)
    dbp, dba = dm * gp, dm * ga
    dgp = dbp * bp.astype(F32) * (1.0 - gp)
    dga = dba * ba.astype(F32) * (1.0 - ga)
    return [dbp, dba, dgp, dga], [dgp, dga]


def _prep(name, ws, transposes):
    n = len(ws)

    def body(*refs):
        for w_ref, o_ref, tr in zip(refs[:n], refs[n:], transposes):
            v = w_ref[...]
            o_ref[...] = (v.T if tr else v).astype(BF)

    shapes = [jax.ShapeDtypeStruct(w.shape[::-1] if tr else w.shape, BF) for w, tr in zip(ws, transposes)]
    return pl.pallas_call(body, name=name, out_shape=shapes, compiler_params=_params())(*ws)


def _adam_math(w, g, m, v):
    m = ADAM_B1 * m + (1.0 - ADAM_B1) * g
    v = ADAM_B2 * v + (1.0 - ADAM_B2) * jnp.square(g)
    m_hat = m / (1.0 - ADAM_B1 ** ADAM_STEP)
    v_hat = v / (1.0 - ADAM_B2 ** ADAM_STEP)
    delta = -ADAM_LR * (m_hat / (jnp.sqrt(v_hat) + ADAM_EPS) + ADAM_WD * w)
    return delta, m, v


def _adamw_sharded(name, items, transpose=False):
    n = len(items)

    def body(*refs):
        ins, outs = refs[:4 * n], refs[4 * n:]
        for k in range(n):
            s_ref, w_ref, m_ref, v_ref = ins[4 * k: 4 * k + 4]
            g = s_ref[0].astype(F32)
            for i in range(1, 4):
                g = g + s_ref[i].astype(F32)
            if transpose:
                g = g.T
            delta, mn, vn = _adam_math(w_ref[...], g, m_ref[...], v_ref[...])
            for o_ref, val in zip(outs[4 * k: 4 * k + 4], (g, delta, mn, vn)):
                o_ref[...] = val

    flat = [a for item in items for a in item]
    out_shape = [jax.ShapeDtypeStruct(item[1].shape, F32) for item in items for _ in range(4)]
    _, r, C = items[0][0].shape
    rows = r // 4
    if transpose or rows % 8:
        res = pl.pallas_call(body, name=name, out_shape=out_shape, compiler_params=_params())(*flat)
    else:
        tile = pl.BlockSpec((rows, C), lambda i: (i, 0))
        res = pl.pallas_call(
            body, name=name, grid=(4,), in_specs=[pl.BlockSpec((4, rows, C), lambda i: (0, i, 0)), tile, tile, tile] * n,
            out_specs=[tile] * (4 * n), out_shape=out_shape, compiler_params=_params(("parallel",)),
        )(*flat)
    return [tuple(res[4 * k: 4 * k + 4]) for k in range(n)]


SMALL_LAYOUT = (("ffn1_norm", 0, (8, LANES)), ("mix_norm", 8, (8, LANES)), ("ffn2_norm", 16, (8, LANES)),
                ("gate_bias", 24, (16, LANES)), ("pool_scale", 40, (4, LANES)), ("q_norm", 48, (1, HEAD_DIM)),
                ("k_norm", 56, (1, HEAD_DIM)), ("sinks", 64, (1, N_HEADS)))
LOSS_ROW = 72
SMALL_ROWS = 80


def _adamw_small(name, g_vec, g_pool_w, params):
    n = len(SMALL_LAYOUT) + 1

    def body(vec_ref, pw_ref, *refs):
        ins, outs = refs[:3 * n], refs[3 * n:]
        vec = vec_ref[0]
        pw = pw_ref[0]
        for i in range(1, N_DEV):
            vec = vec + vec_ref[i]
            pw = pw + pw_ref[i]
        grads = [vec[r0:r0 + shape[0], 0:shape[1]] for _, r0, shape in SMALL_LAYOUT] + [pw]
        for p, g in enumerate(grads):
            w_ref, m_ref, v_ref = ins[3 * p: 3 * p + 3]
            delta, mn, vn = _adam_math(w_ref[...], g, m_ref[...], v_ref[...])
            for o_ref, val in zip(outs[4 * p: 4 * p + 4], (g, delta, mn, vn)):
                o_ref[...] = val
        outs[4 * n][...] = vec[LOSS_ROW:LOSS_ROW + 1, :]

    flat = [a for wmv in params for a in wmv]
    out_shape = [jax.ShapeDtypeStruct(wmv[0].shape, F32) for wmv in params for _ in range(4)]
    out_shape.append(jax.ShapeDtypeStruct((1, LANES), F32))
    res = pl.pallas_call(body, name=name, out_shape=out_shape, compiler_params=_params())(g_vec, g_pool_w, *flat)
    return [tuple(res[4 * p: 4 * p + 4]) for p in range(n)], res[4 * n]


def _place():
    x, y, c = lax.axis_index("x"), lax.axis_index("y"), lax.axis_index("c")
    other_chips = [(1 - x, y), (x, 1 - y), (1 - x, 1 - y)]
    return x, y, c, other_chips


def _rows(ref, r, place, natural=False):
    px, py, pc = place
    b = 4 * px + 2 * py + pc if natural else 4 * pc + 2 * px + py
    return ref.at[pl.ds(pl.multiple_of(b * r, 8), r), :]


def _gather_task(shards, natural=(), forward_at=0.75):
    n = len(shards)
    rs = [s.shape[0] for s in shards]
    rows_of = lambda ref, k, place: _rows(ref, rs[k], place, k in natural)

    def copy(scr, outs, k, slot, block, to, src=None):
        rows = rows_of(outs[k], k, block)
        return pltpu.make_async_remote_copy(
            src_ref=rows if src is None else src, dst_ref=rows, send_sem=scr[0].at[7 * k + slot],
            recv_sem=scr[1].at[7 * k + slot], device_id=to, device_id_type=MESH)

    def first_sends(ins, outs, scr):
        x, y, c, chips = _place()
        me = (x, y, c)
        cps = [copy(scr, outs, k, 1 + j, me, (*chip, c), src=ins[k]) for j, chip in enumerate(chips) for k in range(n)]
        return cps + [copy(scr, outs, k, 0, me, (x, y, 1 - c), src=ins[k]) for k in range(n)]

    def passed_on(outs, scr):
        x, y, c, chips = _place()
        return [copy(scr, outs, k, 4 + j, (*chip, c), (x, y, 1 - c)) for j, chip in enumerate(chips) for k in range(n)]

    def local(ins, outs, scr):
        x, y, c, _ = _place()
        return [pltpu.make_async_copy(ins[k], rows_of(outs[k], k, (x, y, c)), scr[2].at[k]) for k in range(n)]

    def start(ins, outs, scr):
        for cp in local(ins, outs, scr) + first_sends(ins, outs, scr):
            cp.start()

    def forward(ins, outs, scr):
        x, y, c, chips = _place()
        for j, chip in enumerate(chips):
            for k in range(n):
                copy(scr, outs, k, 1 + j, (*chip, c), (x, y, c)).wait_recv()
                copy(scr, outs, k, 4 + j, (*chip, c), (x, y, 1 - c)).start()

    def finish(ins, outs, scr):
        x, y, c, chips = _place()
        for k in range(n):
            copy(scr, outs, k, 0, (x, y, 1 - c), (x, y, c)).wait_recv()
        for j, chip in enumerate(chips):
            for k in range(n):
                copy(scr, outs, k, 4 + j, (*chip, 1 - c), (x, y, c)).wait_recv()
        for cp in first_sends(ins, outs, scr) + passed_on(outs, scr):
            cp.wait_send()
        for cp in local(ins, outs, scr):
            cp.wait()

    out_shapes = [jax.ShapeDtypeStruct((N_DEV * s.shape[0], s.shape[1]), s.dtype) for s in shards]
    scratch = [pltpu.SemaphoreType.DMA((7 * n,)), pltpu.SemaphoreType.DMA((7 * n,)), pltpu.SemaphoreType.DMA((n,))]
    return _Task(shards, out_shapes, scratch, [(0, start), (forward_at, forward), (1.0, finish)], ("sibling", "chips"))


def _direct_gather_task(shards):
    n = len(shards)
    rs = [s.shape[0] for s in shards]

    def peers():
        x, y, c, _ = _place()
        flip = lambda v, bit: 1 - v if bit else v
        return (x, y, c), [(flip(x, (s >> 2) & 1), flip(y, (s >> 1) & 1), flip(c, s & 1)) for s in range(1, N_DEV)]

    def copies(ins, outs, scr):
        me, others = peers()
        local = [pltpu.make_async_copy(ins[k], _rows(outs[k], rs[k], me), scr[2].at[k]) for k in range(n)]
        sems = lambda k, s: dict(send_sem=scr[0].at[7 * k + s], recv_sem=scr[1].at[7 * k + s], device_id_type=MESH)
        sends = [pltpu.make_async_remote_copy(src_ref=ins[k], dst_ref=_rows(outs[k], rs[k], me), device_id=to, **sems(k, s))
                 for s, to in enumerate(others) for k in range(n)]
        recvs = [pltpu.make_async_remote_copy(src_ref=_rows(outs[k], rs[k], frm), dst_ref=_rows(outs[k], rs[k], frm),
                                              device_id=me, **sems(k, s))
                 for s, frm in enumerate(others) for k in range(n)]
        return local, sends, recvs

    def start(ins, outs, scr):
        local, sends, _ = copies(ins, outs, scr)
        for cp in local + sends:
            cp.start()

    def finish(ins, outs, scr):
        local, sends, recvs = copies(ins, outs, scr)
        for cp in recvs:
            cp.wait_recv()
        for cp in sends:
            cp.wait_send()
        for cp in local:
            cp.wait()

    out_shapes = [jax.ShapeDtypeStruct((N_DEV * s.shape[0], s.shape[1]), s.dtype) for s in shards]
    scratch = [pltpu.SemaphoreType.DMA((7 * n,)), pltpu.SemaphoreType.DMA((7 * n,)), pltpu.SemaphoreType.DMA((n,))]
    return _Task(shards, out_shapes, scratch, [(0, start), (1.0, finish)], ("all",))


def _chip_task(sums):
    n = len(sums)
    rs = [s.shape[0] // 4 for s in sums]

    def block(ref, k, chip_index):
        return ref.at[pl.ds(pl.multiple_of(chip_index * rs[k], 8), rs[k]), :]

    def copies(ins, outs, scr):
        send_sems, recv_sems, local_sems = scr
        x, y, c, chips = _place()
        here = 2 * x + y
        local = [pltpu.make_async_copy(block(ins[k], k, here), outs[k].at[here], local_sems.at[k]) for k in range(n)]
        remote = []
        for j, (px, py) in enumerate(chips):
            remote += [pltpu.make_async_remote_copy(
                src_ref=block(ins[k], k, 2 * px + py), dst_ref=outs[k].at[here],
                send_sem=send_sems.at[3 * k + j], recv_sem=recv_sems.at[3 * k + j],
                device_id=(px, py, c), device_id_type=MESH) for k in range(n)]
        return local, remote

    def start(ins, outs, scr):
        local, remote = copies(ins, outs, scr)
        for cp in local + remote:
            cp.start()

    def finish(ins, outs, scr):
        local, remote = copies(ins, outs, scr)
        for cp in remote:
            cp.wait()
        for cp in local:
            cp.wait()

    out_shapes = [jax.ShapeDtypeStruct((4, r, s.shape[1]), s.dtype) for r, s in zip(rs, sums)]
    scratch = [pltpu.SemaphoreType.DMA((3 * n,)), pltpu.SemaphoreType.DMA((3 * n,)), pltpu.SemaphoreType.DMA((n,))]
    return _Task(sums, out_shapes, scratch, [(0, start), (1.0, finish)], ("chips",))


def _dw_pair(name, a, b, scale, comm=None, blocks=1):
    T, M = a.shape
    N = b.shape[1]
    half = M // 2
    wide = half // blocks
    tk = min(4096 if blocks > 1 else 2048, T)
    nK = T // tk
    plumb = _CommPlumbing(comm)

    def body(core_ref, *rest):
        a_refs, b_ref, rest = rest[:blocks], rest[blocks], rest[blocks + 1:]
        c_in = rest[:plumb.n_in]
        o_ref = rest[plumb.n_in]
        c_out = rest[plumb.n_in + 1: plumb.n_in + 1 + plumb.n_out]
        acc, stage, land, send_sem, recv_sem = rest[plumb.n_in + 1 + plumb.n_out: plumb.n_in + 6 + plumb.n_out]
        c_scr = rest[plumb.n_in + 6 + plumb.n_out:]
        i, k = pl.program_id(0), pl.program_id(1)
        x, y, c, _ = _place()
        push = pltpu.make_async_remote_copy(src_ref=stage, dst_ref=land, send_sem=send_sem, recv_sem=recv_sem,
                                            device_id=(x, y, 1 - c), device_id_type=MESH)
        plumb.handshake((i == 0) & (k == 0), own=("sibling",))
        if comm:
            plumb.run(i * nK + k, 2 * nK, True, c_in, c_out, c_scr)

        av = a_refs[0][...] if blocks == 1 else jnp.concatenate([r[...] for r in a_refs], axis=1)
        p = lax.dot_general(av, b_ref[...], _DIMS["tn"], preferred_element_type=F32)

        @pl.when(k == 0)
        def _():
            acc[...] = p

        @pl.when(k > 0)
        def _():
            acc[...] += p

        @pl.when((i == 0) & (k == nK - 1))
        def _():
            stage[...] = (scale * acc[...]).astype(BF)
            push.start()

        @pl.when((i == 1) & (k == nK - 1))
        def _():
            push.wait_recv()
            o_ref[...] = (scale * acc[...] + land[...].astype(F32)).astype(BF)
            push.wait_send()

        if comm:
            plumb.run(i * nK + k, 2 * nK, False, c_in, c_out, c_scr)

    grid_spec = pltpu.PrefetchScalarGridSpec(
        num_scalar_prefetch=1, grid=(2, nK),
        in_specs=[pl.BlockSpec((tk, wide), functools.partial(
            lambda i, k, core, j: (k, (2 * j if blocks > 1 else 0) + jnp.where(i == 0, 1 - core[0], core[0])), j=j))
            for j in range(blocks)] + [pl.BlockSpec((tk, N), lambda i, k, core: (k, 0))] + [ANY] * plumb.n_in,
        out_specs=[pl.BlockSpec((half, N), lambda i, k, core: (0, 0))] + [ANY] * plumb.n_out,
        scratch_shapes=[pltpu.VMEM((half, N), F32), pltpu.VMEM((half, N), BF), pltpu.VMEM((half, N), BF),
                        pltpu.SemaphoreType.DMA, pltpu.SemaphoreType.DMA] + plumb.scratch)
    core = lax.axis_index("c").astype(jnp.int32).reshape(1)
    res = pl.pallas_call(
        body, name=name, grid_spec=grid_spec,
        out_shape=[jax.ShapeDtypeStruct((half, N), BF)] + plumb.out_shapes,
        compiler_params=_params(("arbitrary", "arbitrary"), plumb.collective_id(own=("sibling",))),
    )(core, *([a] * blocks), b, *plumb.args)
    return (res[0], plumb.split_outputs(res[1:])) if comm else res[0]


def _pair_task(parts):
    n = len(parts)

    def copies(ins, outs, scr):
        x, y, c, _ = _place()
        return [pltpu.make_async_remote_copy(
            src_ref=ins[k].at[:, pl.ds(1 - c, 1)], dst_ref=outs[k], send_sem=scr[0].at[k], recv_sem=scr[1].at[k],
            device_id=(x, y, 1 - c), device_id_type=MESH) for k in range(n)]

    def start(ins, outs, scr):
        for cp in copies(ins, outs, scr):
            cp.start()

    def finish(ins, outs, scr):
        for cp in copies(ins, outs, scr):
            cp.wait()

    out_shapes = [jax.ShapeDtypeStruct((4, 1) + p.shape[2:], p.dtype) for p in parts]
    scratch = [pltpu.SemaphoreType.DMA((n,)), pltpu.SemaphoreType.DMA((n,))]
    return _Task(parts, out_shapes, scratch, [(0, start), (1.0, finish)], ("sibling",))


def _pair_sum(name, part, got, core):
    _, _, r, C = part.shape

    def body(core_ref, p_ref, g_ref, o_ref):
        o_ref[0] = (p_ref[0, 0].astype(F32) + g_ref[0, 0].astype(F32)).astype(o_ref.dtype)

    return pl.pallas_call(
        body, name=name,
        grid_spec=pltpu.PrefetchScalarGridSpec(
            num_scalar_prefetch=1, grid=(4,),
            in_specs=[pl.BlockSpec((1, 1, r, C), lambda i, core_ref: (i, core_ref[0], 0, 0)),
                      pl.BlockSpec((1, 1, r, C), lambda i, core_ref: (i, 0, 0, 0))],
            out_specs=pl.BlockSpec((1, r, C), lambda i, core_ref: (i, 0, 0))),
        out_shape=jax.ShapeDtypeStruct((4, r, C), part.dtype), compiler_params=_params(("parallel",)),
    )(core, part, got)


def _ffn_bwd(tag, dy, dyb, x, gain, wgT, wuT, wd, saved, earlier=None):
    n, g, u, a = saved
    half = lambda accs, ex: _swiglu_bwd_epilogue([0.5 * accs[0]], ex)
    act_args = dict(tm=1024, tn=1408, tk=D_MODEL, epilogue=half, extras=[(g, "tile", 0), (u, "tile", 0)], cols_outer=True)
    if earlier is None:
        sum_d = _dw_pair(tag + "_dw_down", a, dyb, 0.5)
        (dg, du), ((slots_d,),) = _mm(tag + "_d_act", [(dyb, wd, "nt", 0)], [BF, BF], comm=[_chip_task([sum_d])], **act_args)
        slots_e = None
        sum_g = _dw_pair(tag + "_dw_gate", dg, n, 1.0)
    else:
        sum_d, ((got,),) = _dw_pair(tag + "_dw_down", a, dyb, 0.5, comm=[_pair_task([earlier])])
        core = lax.axis_index("c").astype(jnp.int32).reshape(1)
        sum_e = _pair_sum(tag + "_pair_sum_earlier", earlier, got, core)
        sum_e = sum_e.reshape(4 * sum_e.shape[1], sum_e.shape[2])
        (dg, du), ((slots_e,),) = _mm(tag + "_d_act", [(dyb, wd, "nt", 0)], [BF, BF], comm=[_chip_task([sum_e])], **act_args)
        sum_g, ((slots_d,),) = _dw_pair(tag + "_dw_gate", dg, n, 1.0, comm=[_chip_task([sum_d])])
    sum_u, ((slots_g,),) = _dw_pair(tag + "_dw_up", du, n, 1.0, comm=[_chip_task([sum_g])])
    (dx, dxb, dgain), ((slots_u,),) = _mm(
        tag + "_d_norm", [(dg, wgT, "nn", 0), (du, wuT, "nn", 0)], [F32, BF], tm=512, tn=D_MODEL, tk=D_FF,
        epilogue=_rms_bwd_epilogue, extras=[(x, "tile", 0), (gain, "row", 0), (dy, "tile", 0)], n_colsum=1,
        comm=[_chip_task([sum_u])])
    return dx, dxb, dgain, slots_e, slots_g, slots_u, slots_d


def _tile_gain(g):
    return jnp.concatenate([g, g]).reshape(1, LANES)


def _fold_heads(partials):
    return jnp.sum(partials.reshape(-1, HEAD_DIM), axis=0)


def _pack_small_grads(grads, loss_local):
    pieces, row = [], 0
    for name, r0, _ in SMALL_LAYOUT + (("loss", LOSS_ROW, None),):
        v = (loss_local if name == "loss" else grads[name]).reshape(-1)
        rows = -(-v.size // LANES)
        block = jnp.pad(v, (0, rows * LANES - v.size)).reshape(rows, LANES)
        pieces += [jnp.zeros((r0 - row, LANES), F32)] * (r0 > row) + [block]
        row = r0 + rows
    pieces.append(jnp.zeros((SMALL_ROWS - row, LANES), F32))
    return jnp.concatenate(pieces, axis=0)


def kernel(x, ffn1_norm, ffn1_w_gate, ffn1_w_up, ffn1_w_down, mix_norm, w_in, pool_w, pool_scale, w_pool_out, q_norm, k_norm, sinks, w_attn_out, gate_bias, w_out, ffn2_norm, ffn2_w_gate, ffn2_w_up, ffn2_w_down, loss_target, m_ffn1_norm, m_ffn1_w_gate, m_ffn1_w_up, m_ffn1_w_down, m_mix_norm, m_w_in, m_pool_w, m_pool_scale, m_w_pool_out, m_q_norm, m_k_norm, m_sinks, m_w_attn_out, m_gate_bias, m_w_out, m_ffn2_norm, m_ffn2_w_gate, m_ffn2_w_up, m_ffn2_w_down, v_ffn1_norm, v_ffn1_w_gate, v_ffn1_w_up, v_ffn1_w_down, v_mix_norm, v_w_in, v_pool_w, v_pool_scale, v_w_pool_out, v_q_norm, v_k_norm, v_sinks, v_w_attn_out, v_gate_bias, v_w_out, v_ffn2_norm, v_ffn2_w_gate, v_ffn2_w_up, v_ffn2_w_down):
    T = x.shape[1]
    x2 = x.reshape(T, D_MODEL)
    target = loss_target.reshape(T, D_MODEL)

    big = [
        ("ffn1_w_gate", ffn1_w_gate, m_ffn1_w_gate, v_ffn1_w_gate, True, False),
        ("ffn1_w_up", ffn1_w_up, m_ffn1_w_up, v_ffn1_w_up, True, False),
        ("ffn1_w_down", ffn1_w_down, m_ffn1_w_down, v_ffn1_w_down, False, False),
        ("w_in", w_in, m_w_in, v_w_in, True, False),
        ("w_pool_out", w_pool_out, m_w_pool_out, v_w_pool_out, False, True),
        ("w_attn_out", w_attn_out, m_w_attn_out, v_w_attn_out, False, False),
        ("w_out", w_out, m_w_out, v_w_out, False, False),
        ("ffn2_w_gate", ffn2_w_gate, m_ffn2_w_gate, v_ffn2_w_gate, True, False),
        ("ffn2_w_up", ffn2_w_up, m_ffn2_w_up, v_ffn2_w_up, True, False),
        ("ffn2_w_down", ffn2_w_down, m_ffn2_w_down, v_ffn2_w_down, False, False),
    ]
    view = lambda a, tv: a.T if tv else a
    views = [view(w, tv) for _, w, _, _, tv, _ in big]
    in_kernel_t = [tk_ for *_, tk_ in big]
    first_shards = _prep("prep_ffn1_gate_up", views[0:2], in_kernel_t[0:2])
    g1 = ffn1_norm.reshape(1, D_MODEL)
    g2 = mix_norm.reshape(1, D_MODEL)
    g3 = ffn2_norm.reshape(1, D_MODEL)
    bias_row = gate_bias.reshape(1, 2 * D_MODEL)
    qg, kg = _tile_gain(q_norm) * ATTN_SCALE, _tile_gain(k_norm)
    scale_row = pool_scale.reshape(1, POOL_WIDTH)

    n1, later_shards, ((wg1T, wu1T),) = _rms_fwd(
        "ffn1_norm", x2, g1, [_gather_task(first_shards, forward_at=0.9)], views[2:], in_kernel_t[2:])
    shards = list(first_shards) + later_shards
    (gt1, up1, act1), ((wd1,), (w_inT,)) = _mm(
        "ffn1_gate_up", [(n1, wg1T, "nt", 0), (n1, wu1T, "nt", 1)], [BF, BF, BF], tm=1024, tn=1408, tk=D_MODEL,
        epilogue=_swiglu_fwd_epilogue, cols_outer=True,
        comm=[_gather_task(shards[2:3], forward_at=0.5), _gather_task(shards[3:4], natural=(0,), forward_at=0.9)])
    (h1, u), ((w_poT, w_ao, w_o),) = _mm(
        "ffn1_down", [(act1, wd1, "nn", 0)], [F32, BF], tm=512, tn=D_MODEL, tk=D_FF,
        epilogue=_residual_norm_epilogue(0.5), extras=[(x2, "tile", 0), (g2, "row", 0)],
        comm=[_gather_task(shards[4:7], natural=(0, 1, 2), forward_at=0.8)])
    saved1 = (n1, gt1, up1, act1)
    (proj,), ((wg2T,),) = _mm(
        "in_proj", [(u, w_inT, "nt", 0)], [BF], tm=1024, tn=1280, tk=D_MODEL, cols_outer=True,
        comm=[_gather_task(shards[7:8], forward_at=0.8)])
    pooled, mixed = _pool_fwd("pool_fwd", proj, pool_w, scale_row)
    qn = _headnorm_fwd("q_norm", proj, COL_Q, ATTN_WIDTH, qg)
    kn = _headnorm_fwd("k_norm", proj, COL_K, KV_WIDTH, kg)
    attn, ((wu2T,),) = _attn_fwd("attn_fwd", qn, kn, proj, sinks, comm=[_gather_task(shards[8:9], forward_at=0.8)])
    (bp,) = _mm("pool_out", [(mixed, w_poT, "nt", 0)], [BF], tm=1024, tn=D_MODEL, tk=POOL_WIDTH)
    gate_tn = 256
    gate_extras = [(proj, "tile", COL_GP // gate_tn), (proj, "tile", COL_GA // gate_tn),
                   (bias_row, "row", 0), (bias_row, "row", D_MODEL // gate_tn)]
    merged, ba = _mm("attn_out_merge", [(attn, w_ao, "nn", 0)], [BF, BF], tm=2048, tn=gate_tn, tk=ATTN_WIDTH,
                     epilogue=_merge_fwd_epilogue, extras=[(bp, "tile", 0)] + gate_extras)
    h2, n2 = _mm("mix_out", [(merged, w_o, "nn", 0)], [F32, BF], tm=1024, tn=D_MODEL, tk=D_MODEL,
                 epilogue=_residual_norm_epilogue(1.0), extras=[(h1, "tile", 0), (g3, "row", 0)])
    (gt2, up2, act2), ((wd2,),) = _mm(
        "ffn2_gate_up", [(n2, wg2T, "nt", 0), (n2, wu2T, "nt", 1)], [BF, BF, BF], tm=1024, tn=1408, tk=D_MODEL,
        epilogue=_swiglu_fwd_epilogue, cols_outer=True, comm=[_gather_task(shards[9:10], forward_at=0.8)])
    dy, dyb, sq = _mm("ffn2_down_loss", [(act2, wd2, "nn", 0)], [F32, BF], tm=512, tn=D_MODEL, tk=D_FF,
                      epilogue=_loss_epilogue, extras=[(h2, "tile", 0), (target, "tile", 0)], n_colsum=1)
    loss_local = 0.5 * jnp.sum(sq) / D_MODEL

    dh2, dh2b, dg3, _, slots_g2, slots_u2, slots_d2 = _ffn_bwd(
        "ffn2", dy, dyb, h2, g3, wg2T, wu2T, wd2, (n2, gt2, up2, act2))
    dbp, dba, dproj, dga, cs_gp, cs_ga = _mm(
        "mix_out_bwd", [(dh2b, w_o, "nt", 0)], [BF, BF, BF, BF], tm=2048, tn=gate_tn, tk=D_MODEL,
        epilogue=_merge_bwd_epilogue, extras=[(bp, "tile", 0), (ba, "tile", 0)] + gate_extras, n_colsum=2,
        out_placement={2: (IN_WIDTH, COL_GP)})
    sum_o = _dw_pair("dw_out", merged, dh2b, 1.0, blocks=4)
    (dmixed,) = _mm("pool_out_bwd", [(dbp, w_poT, "nn", 0)], [BF], tm=1024, tn=POOL_WIDTH, tk=D_MODEL)
    sum_po = _dw_pair("dw_pool_out", dbp, mixed, 1.0, blocks=4)
    (dattn,) = _mm("attn_out_bwd", [(dba, w_ao, "nt", 0)], [BF], tm=1024, tn=ATTN_WIDTH, tk=D_MODEL)
    sum_ao = _dw_pair("dw_attn_out", attn, dba, 1.0, blocks=4)
    (dqn, k_own, k_before, v_own, v_before, dsink_tile), ((slots_o, slots_po, slots_ao),) = _attn_bwd(
        "attn_bwd", dattn, qn, kn, proj, sinks, [_chip_task([sum_o, sum_po, sum_ao])])
    next_block = lambda a: jnp.concatenate([a[BLOCK:], jnp.zeros((BLOCK, KV_WIDTH), F32)], axis=0)
    dkn = (k_own + next_block(k_before)).astype(BF)
    dv = (v_own + next_block(v_before)).astype(BF)
    dproj, dqg = _headnorm_bwd("q_norm_bwd", dqn, proj, COL_Q, ATTN_WIDTH, qg, dproj)
    dproj, dkg = _headnorm_bwd("k_norm_bwd", dkn, proj, COL_K, KV_WIDTH, kg, dproj)
    dproj, dpool_w, dpool_scale = _pool_bwd("pool_bwd", dmixed, pooled, pool_w, scale_row, dproj)
    for piece, col in ((dv, COL_V), (dga, COL_GA)):
        dproj = lax.dynamic_update_slice(dproj, piece, (0, col))
    (dh1, dh1b, dg2), ((g_pool_w,),) = _mm(
        "in_proj_bwd", [(dproj, w_inT, "nn", 0)], [F32, BF], tm=1024, tn=D_MODEL, tk=IN_WIDTH, epilogue=_rms_bwd_epilogue,
        extras=[(h1, "tile", 0), (g2, "row", 0), (dh2, "tile", 0)], n_colsum=1,
        comm=[_gather_task([dpool_w.reshape(-1, LANES)])])
    (dw_inT,) = _mm("dw_in", [(dproj, u, "tn", 0)], [BF], tm=1920, tn=D_MODEL, tk=2048)
    dx, _, dg1, slots_in, slots_g1, slots_u1, slots_d1 = _ffn_bwd(
        "ffn1", dh1, dh1b, x2, g1, wg1T, wu1T, wd1, saved1, dw_inT.reshape(4, 2, IN_WIDTH // N_DEV, D_MODEL))

    slots = [slots_g1, slots_u1, slots_d1, slots_in, slots_po, slots_ao, slots_o, slots_g2, slots_u2, slots_d2]
    big_out = {}
    for label, group in (("ffn", (0, 1, 2, 7, 8, 9)), ("w_in", (3,)), ("w_pool_out", (4,)), ("attn_out_and_out", (5, 6))):
        items = [(slots[k], view(big[k][1], big[k][4]), view(big[k][2], big[k][4]), view(big[k][3], big[k][4]))
                 for k in group]
        for k, res in zip(group, _adamw_sharded("adamw_" + label, items, transpose=big[group[0]][5])):
            big_out[big[k][0]] = tuple(view(r, big[k][4]) for r in res)

    small_grads = {
        "ffn1_norm": jnp.sum(dg1, axis=(0, 1)), "mix_norm": jnp.sum(dg2, axis=(0, 1)), "ffn2_norm": jnp.sum(dg3, axis=(0, 1)),
        "gate_bias": jnp.concatenate([jnp.sum(cs_gp, axis=(0, 1)), jnp.sum(cs_ga, axis=(0, 1))]),
        "pool_scale": dpool_scale, "q_norm": _fold_heads(dqg) * ATTN_SCALE, "k_norm": _fold_heads(dkg),
        "sinks": dsink_tile[0, :N_HEADS]}
    ((g_vec,),) = _comm_only("gather_small_grads", [_direct_gather_task([_pack_small_grads(small_grads, loss_local)])])
    given = {"ffn1_norm": (ffn1_norm, m_ffn1_norm, v_ffn1_norm), "mix_norm": (mix_norm, m_mix_norm, v_mix_norm),
             "ffn2_norm": (ffn2_norm, m_ffn2_norm, v_ffn2_norm), "gate_bias": (gate_bias, m_gate_bias, v_gate_bias),
             "pool_scale": (pool_scale, m_pool_scale, v_pool_scale), "q_norm": (q_norm, m_q_norm, v_q_norm),
             "k_norm": (k_norm, m_k_norm, v_k_norm), "sinks": (sinks, m_sinks, v_sinks)}
    params = [tuple(a.reshape(shape) for a in given[nm]) for nm, _, shape in SMALL_LAYOUT]
    params.append(tuple(a.reshape(-1, LANES) for a in (pool_w, m_pool_w, v_pool_w)))
    small_res, loss_row = _adamw_small("adamw_small", g_vec.reshape(N_DEV, SMALL_ROWS, LANES),
                                       g_pool_w.reshape(N_DEV, -1, LANES), params)
    small_out = {nm: tuple(r.reshape(given[nm][0].shape) for r in res)
                 for (nm, _, _), res in zip(SMALL_LAYOUT, small_res)}
    small_out["pool_w"] = tuple(r.reshape(pool_w.shape) for r in small_res[-1])
    loss = loss_row[0, 0]

    order = ["ffn1_norm", "ffn1_w_gate", "ffn1_w_up", "ffn1_w_down", "mix_norm", "w_in", "pool_w", "pool_scale",
             "w_pool_out", "q_norm", "k_norm", "sinks", "w_attn_out", "gate_bias", "w_out", "ffn2_norm",
             "ffn2_w_gate", "ffn2_w_up", "ffn2_w_down"]
    every = {**big_out, **small_out}
    outs = [loss, dx.reshape(x.shape)]
    for j in range(4):
        outs += [every[nm][j] for nm in order]
    return tuple(outs)
```

```python
import functools

import jax
import jax.numpy as jnp
from jax import lax
from jax.experimental import pallas as pl
from jax.experimental.pallas import tpu as pltpu

BF = jnp.bfloat16
F32 = jnp.float32

D_MODEL = 1024
D_FF = 2816
POOL_WIDTH = 512
POOL_GROUP = 128
N_POOL_GROUPS = 4
HEAD_DIM = 64
N_HEADS = 16
GQA_GROUP = 8
BLOCK = 128
ATTN_WIDTH = 1024
KV_WIDTH = 128
IN_WIDTH = 3840
RMS_EPS = 1e-6
N_DEV = 8
LANES = 128

COL_Q = POOL_WIDTH
COL_K = COL_Q + ATTN_WIDTH
COL_V = COL_K + KV_WIDTH
COL_GP = COL_V + KV_WIDTH
COL_GA = COL_GP + D_MODEL

ADAM_LR = 0.001
ADAM_B1 = 0.9
ADAM_B2 = 0.999
ADAM_EPS = 1e-08
ADAM_WD = 0.01
ADAM_STEP = 10

VMEM_LIMIT_V7X = 56 * 1024 * 1024
MESH = pl.DeviceIdType.MESH
ANY = pl.BlockSpec(memory_space=pl.ANY)


def _params(sem=None, collective_id=None):
    return pltpu.CompilerParams(dimension_semantics=sem, vmem_limit_bytes=VMEM_LIMIT_V7X, collective_id=collective_id)


COLLECTIVE_IDS = {frozenset(["sibling"]): 0, frozenset(["chips"]): 1, frozenset(["sibling", "chips"]): 2}


def _handshake(peer_kinds):
    x, y, c, chips = _place()
    peers = ([(x, y, 1 - c)] if "sibling" in peer_kinds else []) + ([(*chip, c) for chip in chips] if "chips" in peer_kinds else [])
    barrier = pltpu.get_barrier_semaphore()
    for peer in peers:
        pl.semaphore_signal(barrier, inc=1, device_id=peer, device_id_type=MESH)
    pl.semaphore_wait(barrier, len(peers))


_DIMS = {"nt": (((1,), (1,)), ((), ())), "nn": (((1,), (0,)), ((), ())), "tn": (((0,), (0,)), ((), ()))}


class _Task:
    def __init__(self, inputs, out_shapes, scratch, phases, peers):
        self.inputs, self.out_shapes, self.scratch = list(inputs), list(out_shapes), list(scratch)
        self.phases = list(phases)
        self.peers = frozenset(peers)


class _CommPlumbing:
    def __init__(self, tasks):
        self.tasks = list(tasks or [])
        self.args = [a for t in self.tasks for a in t.inputs]
        self.out_shapes = [o for t in self.tasks for o in t.out_shapes]
        self.scratch = [s for t in self.tasks for s in t.scratch]
        self.n_in, self.n_out = len(self.args), len(self.out_shapes)

    def peer_kinds(self, own=()):
        kinds = frozenset(own).union(*[t.peers for t in self.tasks])
        return None if "all" in kinds or not kinds else kinds

    def collective_id(self, own=()):
        kinds = self.peer_kinds(own)
        return None if kinds is None else COLLECTIVE_IDS[kinds]

    def handshake(self, first, own=()):
        kinds = self.peer_kinds(own)
        if kinds is not None:
            pl.when(first)(functools.partial(_handshake, kinds))

    def _slices(self, c_in, c_out, c_scr):
        i = o = s = 0
        for t in self.tasks:
            yield t, c_in[i:i + len(t.inputs)], c_out[o:o + len(t.out_shapes)], c_scr[s:s + len(t.scratch)]
            i, o, s = i + len(t.inputs), o + len(t.out_shapes), s + len(t.scratch)

    def run(self, step, steps, before, c_in, c_out, c_scr):
        for t, ins, outs, scr in self._slices(c_in, c_out, c_scr):
            for frac, fn in t.phases:
                if step is None:
                    fn(ins, outs, scr)
                elif before == (frac == 0):
                    at = 0 if frac == 0 else max(0, min(steps, -(-int(round(frac * steps * 64)) // 64)) - 1)
                    pl.when(step == at)(functools.partial(fn, ins, outs, scr))

    def split_outputs(self, flat):
        res, o = [], 0
        for t in self.tasks:
            res.append(list(flat[o:o + len(t.out_shapes)]))
            o += len(t.out_shapes)
        return res


def _comm_only(name, tasks):
    plumb = _CommPlumbing(tasks)

    def body(*refs):
        c_in, c_out = refs[:plumb.n_in], refs[plumb.n_in: plumb.n_in + plumb.n_out]
        c_scr = refs[plumb.n_in + plumb.n_out:]
        plumb.run(None, 1, True, c_in, c_out, c_scr)

    res = pl.pallas_call(
        body, name=name, in_specs=[ANY] * plumb.n_in, out_specs=[ANY] * plumb.n_out, out_shape=plumb.out_shapes,
        scratch_shapes=plumb.scratch, compiler_params=pltpu.CompilerParams(has_side_effects=True),
    )(*plumb.args)
    return plumb.split_outputs(res)


def _mm(name, terms, out_dtypes, *, tm, tn, tk, epilogue=None, extras=(), n_colsum=0, comm=None, cols_outer=False,
        out_placement=None):
    a0, b0, mode0, _ = terms[0]
    if mode0 == "nt":
        (M, K), N = a0.shape, b0.shape[0]
    elif mode0 == "nn":
        (M, K), N = a0.shape, b0.shape[1]
    else:
        (K, M), N = a0.shape, b0.shape[1]
    tm, tn, tk = min(tm, M), min(tn, N), min(tk, K)
    assert M % tm == 0 and N % tn == 0 and K % tk == 0, (name, M, N, K, tm, tn, tk)
    nI, nJ, nK = M // tm, N // tn, K // tk
    n_terms = len(terms)
    n_acc = max(t[3] for t in terms) + 1
    n_ex = len(extras)
    n_out = len(out_dtypes)
    if epilogue is None:
        epilogue = lambda accs, ex: ([accs[0]], [])
    plumb = _CommPlumbing(comm)
    n_scr = n_acc if nK > 1 else 0
    grid = (nJ, nI, nK) if cols_outer else (nI, nJ, nK)

    def body(*refs):
        n_in = 2 * n_terms + n_ex
        ab = refs[: 2 * n_terms]
        ex_refs = refs[2 * n_terms: n_in]
        c_in = refs[n_in: n_in + plumb.n_in]
        o0 = n_in + plumb.n_in
        out_refs = refs[o0: o0 + n_out]
        cs_refs = refs[o0 + n_out: o0 + n_out + n_colsum]
        c_out = refs[o0 + n_out + n_colsum: o0 + n_out + n_colsum + plumb.n_out]
        s0 = o0 + n_out + n_colsum + plumb.n_out
        acc_refs = refs[s0: s0 + n_scr]
        c_scr = refs[s0 + n_scr:]
        steps = grid[0] * grid[1] * nK
        if comm:
            step = (pl.program_id(0) * grid[1] + pl.program_id(1)) * nK + pl.program_id(2)
            plumb.handshake(step == 0)
            plumb.run(step, steps, True, c_in, c_out, c_scr)

        def products():
            accs = [None] * n_acc
            for t, (_, _, mode, ai) in enumerate(terms):
                p = lax.dot_general(ab[2 * t][...], ab[2 * t + 1][...], _DIMS[mode], preferred_element_type=F32)
                accs[ai] = p if accs[ai] is None else accs[ai] + p
            return accs

        def finish(accs):
            outs, colsums = epilogue(accs, [r[...] for r in ex_refs])
            for r, o in zip(out_refs, outs):
                r[...] = o.astype(r.dtype)
            for r, cs in zip(cs_refs, colsums):
                r[...] = jnp.sum(cs, axis=0, keepdims=True).reshape(r.shape)

        if nK == 1:
            finish(products())
        else:
            k = pl.program_id(2)
            accs = products()

            @pl.when(k == 0)
            def _():
                for r, a in zip(acc_refs, accs):
                    r[...] = a

            @pl.when(k > 0)
            def _():
                for r, a in zip(acc_refs, accs):
                    r[...] += a

            @pl.when(k == nK - 1)
            def _():
                finish([r[...] for r in acc_refs])

        if comm:
            plumb.run(step, steps, False, c_in, c_out, c_scr)

    def spec(block, index, fixed=False):
        imap = (lambda q, p, k: index(p, q, k)) if cols_outer else index
        return pl.BlockSpec(block, imap, pipeline_mode=pl.Buffered(1)) if fixed else pl.BlockSpec(block, imap)

    in_specs, args = [], []
    for a, b, mode, _ in terms:
        if mode == "nt":
            in_specs += [spec((tm, tk), lambda i, j, k: (i, k), nI * nK == 1),
                         spec((tn, tk), lambda i, j, k: (j, k), nJ * nK == 1)]
        elif mode == "nn":
            in_specs += [spec((tm, tk), lambda i, j, k: (i, k), nI * nK == 1),
                         spec((tk, tn), lambda i, j, k: (k, j), nJ * nK == 1)]
        else:
            in_specs += [spec((tk, tm), lambda i, j, k: (k, i), nI * nK == 1),
                         spec((tk, tn), lambda i, j, k: (k, j), nJ * nK == 1)]
        args += [a, b]
    for arr, kind, off in extras:
        if kind == "tile":
            in_specs.append(spec((tm, tn), functools.partial(lambda i, j, k, off: (i, j + off), off=off)))
        else:
            in_specs.append(spec((1, tn), functools.partial(lambda i, j, k, off: (0, j + off), off=off)))
        args.append(arr)
    placed = dict(out_placement or {})
    out_shape = [jax.ShapeDtypeStruct((M, placed.get(o, (N, 0))[0]), dt) for o, dt in enumerate(out_dtypes)]
    out_specs = [spec((tm, tn), functools.partial(lambda i, j, k, off: (i, j + off), off=placed.get(o, (N, 0))[1] // tn))
                 for o in range(n_out)]
    out_shape += [jax.ShapeDtypeStruct((nI, 1, N), F32) for _ in range(n_colsum)]
    out_specs += [spec((1, 1, tn), lambda i, j, k: (i, 0, j)) for _ in range(n_colsum)]
    scratch = [pltpu.VMEM((tm, tn), F32) for _ in range(n_scr)]
    args += plumb.args
    in_specs += [ANY] * plumb.n_in
    out_shape += plumb.out_shapes
    out_specs += [ANY] * plumb.n_out
    sem = ("arbitrary",) * 3 if comm else ("parallel", "parallel", "arbitrary")
    res = pl.pallas_call(
        body, name=name, grid=grid, in_specs=in_specs, out_specs=out_specs, out_shape=out_shape,
        scratch_shapes=scratch + plumb.scratch, compiler_params=_params(sem, plumb.collective_id()),
    )(*args)
    n_own = n_out + n_colsum
    return (list(res[:n_own]), plumb.split_outputs(res[n_own:])) if comm is not None else res


ROW_TILE = 512


def _rms_fwd(name, x, g, comm, weights, transposes):
    T, D = x.shape
    steps = T // ROW_TILE
    plumb = _CommPlumbing(comm)
    nw = len(weights)

    def body(x_ref, g_ref, *rest):
        w_refs, c_in = rest[:nw], rest[nw: nw + plumb.n_in]
        o_ref, shard_refs = rest[nw + plumb.n_in], rest[nw + plumb.n_in + 1: 2 * nw + plumb.n_in + 1]
        c_out = rest[2 * nw + plumb.n_in + 1: 2 * nw + plumb.n_in + 1 + plumb.n_out]
        c_scr = rest[2 * nw + plumb.n_in + 1 + plumb.n_out:]
        plumb.handshake(pl.program_id(0) == 0)
        plumb.run(pl.program_id(0), steps, True, c_in, c_out, c_scr)

        @pl.when(pl.program_id(0) == 0)
        def _():
            for w_ref, s_ref, tr in zip(w_refs, shard_refs, transposes):
                v = w_ref[...]
                s_ref[...] = (v.T if tr else v).astype(BF)

        xv = x_ref[...]
        r = lax.rsqrt(jnp.mean(xv * xv, axis=-1, keepdims=True) + RMS_EPS)
        o_ref[...] = (xv * r * g_ref[...]).astype(BF)
        plumb.run(pl.program_id(0), steps, False, c_in, c_out, c_scr)

    row = pl.BlockSpec((ROW_TILE, D), lambda i: (i, 0))
    whole = lambda shape: pl.BlockSpec(shape, lambda i: (0, 0), pipeline_mode=pl.Buffered(1))
    shard_shapes = [w.shape[::-1] if tr else w.shape for w, tr in zip(weights, transposes)]
    res = pl.pallas_call(
        body, name=name, grid=(steps,),
        in_specs=[row, pl.BlockSpec((1, D), lambda i: (0, 0))] + [whole(w.shape) for w in weights] + [ANY] * plumb.n_in,
        out_specs=[row] + [whole(s) for s in shard_shapes] + [ANY] * plumb.n_out,
        out_shape=[jax.ShapeDtypeStruct((T, D), BF)] + [jax.ShapeDtypeStruct(s, BF) for s in shard_shapes] + plumb.out_shapes,
        scratch_shapes=plumb.scratch, compiler_params=_params(("arbitrary",), plumb.collective_id()),
    )(x, g, *weights, *plumb.args)
    return res[0], list(res[1: nw + 1]), plumb.split_outputs(res[nw + 1:])


HEADNORM_TILE = 4096


def _half_sum_matrix():
    r = lax.broadcasted_iota(jnp.int32, (LANES, LANES), 0) // HEAD_DIM
    c = lax.broadcasted_iota(jnp.int32, (LANES, LANES), 1) // HEAD_DIM
    return (r == c).astype(BF)


def _head_mean(v, ones_blockdiag):
    hi = v.astype(BF)
    lo = (v - hi.astype(F32)).astype(BF)
    s = jnp.dot(hi, ones_blockdiag, preferred_element_type=F32) + jnp.dot(lo, ones_blockdiag, preferred_element_type=F32)
    return s * (1.0 / HEAD_DIM)


def _headnorm_fwd(name, proj, col0, width, g2):
    T = proj.shape[0]
    wide = min(width, GROUP_WIDTH)
    nb, off = width // wide, col0 // wide

    def body(x_ref, g_ref, b_ref, o_ref):
        for s in range(wide // LANES):
            lanes = slice(LANES * s, LANES * (s + 1))
            xv = x_ref[:, lanes].astype(F32)
            r = lax.rsqrt(_head_mean(xv * xv, b_ref[...]) + RMS_EPS)
            o_ref[:, lanes] = (xv * r * g_ref[...]).astype(BF)

    return pl.pallas_call(
        body, name=name, grid=(T // HEADNORM_TILE, nb),
        in_specs=[pl.BlockSpec((HEADNORM_TILE, wide), lambda i, j: (i, j + off)),
                  pl.BlockSpec((1, LANES), lambda i, j: (0, 0)), pl.BlockSpec((LANES, LANES), lambda i, j: (0, 0))],
        out_specs=pl.BlockSpec((HEADNORM_TILE, wide), lambda i, j: (i, j)),
        out_shape=jax.ShapeDtypeStruct((T, width), BF), compiler_params=_params(("parallel", "parallel")),
    )(proj, g2, _half_sum_matrix())


def _headnorm_bwd(name, dy, proj, col0, width, g2, into):
    T = proj.shape[0]
    wide = min(width, GROUP_WIDTH)
    nb, off = width // wide, col0 // wide

    def body(dy_ref, x_ref, g_ref, b_ref, into_ref, dx_ref, dg_ref):
        for s in range(wide // LANES):
            lanes = slice(LANES * s, LANES * (s + 1))
            xv = x_ref[:, lanes].astype(F32)
            dyv = dy_ref[:, lanes].astype(F32)
            r = lax.rsqrt(_head_mean(xv * xv, b_ref[...]) + RMS_EPS)
            xhat = xv * r
            dxhat = dyv * g_ref[...]
            dx_ref[:, lanes] = (r * (dxhat - xhat * _head_mean(dxhat * xhat, b_ref[...]))).astype(BF)
            dg_ref[0, :, lanes] = jnp.sum(dyv * xhat, axis=0, keepdims=True)

    return pl.pallas_call(
        body, name=name, grid=(T // HEADNORM_TILE, nb),
        in_specs=[pl.BlockSpec((HEADNORM_TILE, wide), lambda i, j: (i, j)),
                  pl.BlockSpec((HEADNORM_TILE, wide), lambda i, j: (i, j + off)),
                  pl.BlockSpec((1, LANES), lambda i, j: (0, 0)), pl.BlockSpec((LANES, LANES), lambda i, j: (0, 0)), ANY],
        out_specs=[pl.BlockSpec((HEADNORM_TILE, wide), lambda i, j: (i, j + off)),
                   pl.BlockSpec((1, 1, wide), lambda i, j: (i, 0, j))],
        out_shape=[jax.ShapeDtypeStruct(into.shape, BF), jax.ShapeDtypeStruct((T // HEADNORM_TILE, 1, width), F32)],
        input_output_aliases={4: 0}, compiler_params=_params(("parallel", "parallel")),
    )(dy, proj, g2, _half_sum_matrix(), into)


def _shift_down(v, k, row):
    return jnp.where(row >= k, pltpu.roll(v, k, axis=0), 0.0)


def _shift_up(v, k, row, T):
    return jnp.where(row < T - k, pltpu.roll(v, T - k, axis=0), 0.0)


def _by_group(g, vals):
    out = vals[-1]
    for i in range(len(vals) - 2, -1, -1):
        out = jnp.where(g == i, vals[i], out)
    return out


def _pool_fwd(name, proj, pool_w, pool_scale):
    T = proj.shape[0]

    def body(x_ref, w_ref, s_ref, pooled_ref, mixed_ref):
        g = pl.program_id(0)
        xv = x_ref[...].astype(F32)
        row = lax.broadcasted_iota(jnp.int32, (T, 1), 0)
        s2 = xv + _shift_down(xv, 1, row)
        s4 = s2 + _shift_down(s2, 2, row)
        s8 = s4 + _shift_down(s4, 4, row)
        s16 = s8 + _shift_down(s8, 8, row)
        wsum = _by_group(g, [s2, s4, s8, s16])
        count = jnp.minimum(row + 1, 2 << g).astype(F32)
        pooled = (wsum / count - xv).astype(BF)
        pooled_ref[...] = pooled
        mixed = jnp.dot(pooled, w_ref[0].astype(BF), preferred_element_type=F32) * s_ref[...]
        mixed_ref[...] = mixed.astype(BF)

    col = pl.BlockSpec((T, POOL_GROUP), lambda g: (0, g))
    return pl.pallas_call(
        body, name=name, grid=(N_POOL_GROUPS,),
        in_specs=[col, pl.BlockSpec((1, POOL_GROUP, POOL_GROUP), lambda g: (g, 0, 0)),
                  pl.BlockSpec((1, POOL_GROUP), lambda g: (0, g))],
        out_specs=[col, col],
        out_shape=[jax.ShapeDtypeStruct((T, POOL_WIDTH), BF), jax.ShapeDtypeStruct((T, POOL_WIDTH), BF)],
        compiler_params=_params(("parallel",)),
    )(proj, pool_w, pool_scale)


def _pool_bwd(name, dmixed, pooled, pool_w, pool_scale, into):
    T = dmixed.shape[0]

    def body(dm_ref, p_ref, w_ref, s_ref, into_ref, dx_ref, dw_ref, ds_ref):
        g = pl.program_id(0)
        dm = dm_ref[...].astype(F32)
        pooled = p_ref[...]
        w = w_ref[0].astype(BF)
        pre = jnp.dot(pooled, w, preferred_element_type=F32)
        ds_ref[...] = jnp.sum(dm * pre, axis=0, keepdims=True)
        dms = (dm * s_ref[...]).astype(BF)
        dw_ref[0] = lax.dot_general(pooled, dms, _DIMS["tn"], preferred_element_type=F32)
        dpooled = lax.dot_general(dms, w, _DIMS["nt"], preferred_element_type=F32)
        row = lax.broadcasted_iota(jnp.int32, (T, 1), 0)
        count = jnp.minimum(row + 1, 2 << g).astype(F32)
        z = dpooled / count
        l2 = z + _shift_up(z, 1, row, T)
        l4 = l2 + _shift_up(l2, 2, row, T)
        l8 = l4 + _shift_up(l4, 4, row, T)
        l16 = l8 + _shift_up(l8, 8, row, T)
        dx_ref[...] = (_by_group(g, [l2, l4, l8, l16]) - dpooled).astype(BF)

    col = pl.BlockSpec((T, POOL_GROUP), lambda g: (0, g))
    wspec = pl.BlockSpec((1, POOL_GROUP, POOL_GROUP), lambda g: (g, 0, 0))
    sspec = pl.BlockSpec((1, POOL_GROUP), lambda g: (0, g))
    return pl.pallas_call(
        body, name=name, grid=(N_POOL_GROUPS,), in_specs=[col, col, wspec, sspec, ANY], out_specs=[col, wspec, sspec],
        out_shape=[jax.ShapeDtypeStruct(into.shape, BF),
                   jax.ShapeDtypeStruct((N_POOL_GROUPS, POOL_GROUP, POOL_GROUP), F32),
                   jax.ShapeDtypeStruct((1, POOL_WIDTH), F32)],
        input_output_aliases={4: 0}, compiler_params=_params(("parallel",)),
    )(dmixed, pooled, pool_w, pool_scale, into)


ATTN_SCALE = HEAD_DIM ** -0.5
MASKED = float(jnp.finfo(jnp.float32).min)
KV_COL_BLOCK_V = COL_V // LANES
GROUP_WIDTH = GQA_GROUP * HEAD_DIM


def _dup_head(v, j):
    half = lax.broadcasted_iota(jnp.int32, (1, LANES), 1) // HEAD_DIM
    return jnp.where(half == j, v, pltpu.roll(v, HEAD_DIM, axis=1))


def _stack_heads(v, low):
    pieces = []
    for p in range(GROUP_WIDTH // LANES):
        vp = v[:, LANES * p: LANES * (p + 1)]
        pieces.append(jnp.where(low, vp, jnp.zeros_like(vp)))
        pieces.append(jnp.where(low, jnp.zeros_like(vp), vp))
    return jnp.concatenate(pieces, axis=0)


def _unstack_transposed(t, low):
    pairs = []
    for p in range(GROUP_WIDTH // LANES):
        even = t[:, BLOCK * (2 * p): BLOCK * (2 * p + 1)].T
        odd = t[:, BLOCK * (2 * p + 1): BLOCK * (2 * p + 2)].T
        pairs.append(jnp.where(low, even, odd))
    return pairs


STACKED = GQA_GROUP * BLOCK


def _band_bias():
    key = lax.broadcasted_iota(jnp.int32, (2, 2 * BLOCK, STACKED), 1)
    qry = lax.broadcasted_iota(jnp.int32, (2, 2 * BLOCK, STACKED), 2) % BLOCK
    first = lax.broadcasted_iota(jnp.int32, (2, 2 * BLOCK, STACKED), 0) == 0
    valid = (key > qry) & (key <= qry + BLOCK) & (jnp.logical_not(first) | (key >= BLOCK))
    return jnp.where(valid, 0.0, MASKED).astype(F32)


def _softmax_keys_on_sublanes(k2, q, bias, sink_ref, j):
    head_of_lane = lax.broadcasted_iota(jnp.int32, (1, STACKED), 1) // BLOCK
    sink = jnp.zeros((1, STACKED), F32)
    for h in range(GQA_GROUP):
        sink = jnp.where(head_of_lane == h, sink_ref[j * GQA_GROUP + h], sink)
    s = lax.dot_general(k2, q, _DIMS["nt"], preferred_element_type=F32) + bias
    m = jnp.maximum(jnp.max(s, axis=0, keepdims=True), sink)
    e = jnp.exp(s - m)
    e_sink = jnp.exp(sink - m)
    inv = 1.0 / (jnp.sum(e, axis=0, keepdims=True) + e_sink)
    return e * inv, e_sink * inv


def _attn_fwd(name, qn, kn, proj, sinks, comm=None):
    T = qn.shape[0]
    nb = T // BLOCK
    plumb = _CommPlumbing(comm)

    def body(sink_ref, bias_ref, q_ref, kp_ref, kc_ref, vp_ref, vc_ref, *rest):
        c_in, o_ref = rest[:plumb.n_in], rest[plumb.n_in]
        c_out, c_scr = rest[plumb.n_in + 1: plumb.n_in + 1 + plumb.n_out], rest[plumb.n_in + 1 + plumb.n_out:]
        m = pl.program_id(0)
        plumb.handshake(m == 0)
        plumb.run(m, nb // 2, True, c_in, c_out, c_scr)
        low = lax.broadcasted_iota(jnp.int32, (1, LANES), 1) < HEAD_DIM
        k_pair, v_pair = kc_ref[...], vc_ref[...]
        for b in range(2):
            rows = slice(BLOCK * b, BLOCK * (b + 1))
            kk = k_pair if b else jnp.concatenate([kp_ref[...], k_pair[0:BLOCK]], axis=0)
            vv = v_pair if b else jnp.concatenate([vp_ref[...], v_pair[0:BLOCK]], axis=0)
            bias = bias_ref[1] if b else bias_ref[jnp.minimum(m, 1)]
            for j in range(2):
                q = _stack_heads(q_ref[rows, GROUP_WIDTH * j: GROUP_WIDTH * (j + 1)], low)
                p, _ = _softmax_keys_on_sublanes(_dup_head(kk, j), q, bias, sink_ref, j)
                o_t = lax.dot_general(_dup_head(vv, j), p.astype(BF), _DIMS["tn"], preferred_element_type=F32)
                for pair, o in enumerate(_unstack_transposed(o_t, low)):
                    lanes = slice(GROUP_WIDTH * j + LANES * pair, GROUP_WIDTH * j + LANES * (pair + 1))
                    o_ref[rows, lanes] = o.astype(BF)
        plumb.run(m, nb // 2, False, c_in, c_out, c_scr)

    wide = pl.BlockSpec((2 * BLOCK, ATTN_WIDTH), lambda m: (m, 0))
    before = lambda m: jnp.maximum(2 * m - 1, 0)
    res = pl.pallas_call(
        body, name=name, grid=(nb // 2,),
        in_specs=[pl.BlockSpec(memory_space=pltpu.SMEM),
                  pl.BlockSpec((2, 2 * BLOCK, STACKED), lambda m: (0, 0, 0)), wide,
                  pl.BlockSpec((BLOCK, LANES), lambda m: (before(m), 0)),
                  pl.BlockSpec((2 * BLOCK, LANES), lambda m: (m, 0)),
                  pl.BlockSpec((BLOCK, LANES), lambda m: (before(m), KV_COL_BLOCK_V)),
                  pl.BlockSpec((2 * BLOCK, LANES), lambda m: (m, KV_COL_BLOCK_V))] + [ANY] * plumb.n_in,
        out_specs=[wide] + [ANY] * plumb.n_out,
        out_shape=[jax.ShapeDtypeStruct((T, ATTN_WIDTH), BF)] + plumb.out_shapes, scratch_shapes=plumb.scratch,
        compiler_params=_params(("arbitrary",) if comm else ("parallel",), plumb.collective_id()),
    )(sinks, _band_bias(), qn, kn, kn, proj, proj, *plumb.args)
    return (res[0], plumb.split_outputs(res[1:])) if comm is not None else res[0]


def _attn_bwd(name, dout, qn, kn, proj, sinks, comm):
    T = qn.shape[0]
    nb = T // BLOCK
    plumb = _CommPlumbing(comm)

    def body(sink_ref, bias_ref, do_ref, q_ref, kp_ref, kc_ref, vp_ref, vc_ref, *rest):
        c_in = rest[:plumb.n_in]
        dq_ref, k_own, k_before, v_own, v_before, dsink_ref = rest[plumb.n_in: plumb.n_in + 6]
        c_out, c_scr = rest[plumb.n_in + 6: plumb.n_in + 6 + plumb.n_out], rest[plumb.n_in + 6 + plumb.n_out:]
        m = pl.program_id(0)
        plumb.handshake(m == 0)
        plumb.run(m, nb // 2, True, c_in, c_out, c_scr)
        lane = lax.broadcasted_iota(jnp.int32, (1, LANES), 1)
        low = lane < HEAD_DIM

        @pl.when(m == 0)
        def _():
            dsink_ref[...] = jnp.zeros_like(dsink_ref)

        k_pair, v_pair = kc_ref[...], vc_ref[...]
        dsink = jnp.zeros((1, LANES), F32)
        for b in range(2):
            rows = slice(BLOCK * b, BLOCK * (b + 1))
            kk = k_pair if b else jnp.concatenate([kp_ref[...], k_pair[0:BLOCK]], axis=0)
            vv = v_pair if b else jnp.concatenate([vp_ref[...], v_pair[0:BLOCK]], axis=0)
            bias = bias_ref[1] if b else bias_ref[jnp.minimum(m, 1)]
            dk_tot = jnp.zeros((2 * BLOCK, LANES), F32)
            dv_tot = jnp.zeros((2 * BLOCK, LANES), F32)
            for j in range(2):
                k2 = _dup_head(kk, j)
                v2 = _dup_head(vv, j)
                q = _stack_heads(q_ref[rows, GROUP_WIDTH * j: GROUP_WIDTH * (j + 1)], low)
                do = _stack_heads(do_ref[rows, GROUP_WIDTH * j: GROUP_WIDTH * (j + 1)], low)
                p, psink = _softmax_keys_on_sublanes(k2, q, bias, sink_ref, j)
                dp =lax.dot_general(v2, do, _DIMS["nt"], preferred_element_type=F32)
                delta = jnp.sum(p * dp, axis=0, keepdims=True)
                ds = (p * (dp - delta)).astype(BF)
                dk2 = jnp.dot(ds, q, preferred_element_type=F32)
                dv2 = jnp.dot(p.astype(BF), do, preferred_element_type=F32)
                dq_t = lax.dot_general(k2, ds, _DIMS["tn"], preferred_element_type=F32)
                for pair, dq in enumerate(_unstack_transposed(dq_t, low)):
                    lanes = slice(GROUP_WIDTH * j + LANES * pair, GROUP_WIDTH * j + LANES * (pair + 1))
                    dq_ref[rows, lanes] = dq.astype(BF)
                mine = low if j == 0 else jnp.logical_not(low)
                dk_tot = dk_tot + jnp.where(mine, dk2 + pltpu.roll(dk2, HEAD_DIM, axis=1), 0.0)
                dv_tot = dv_tot + jnp.where(mine, dv2 + pltpu.roll(dv2, HEAD_DIM, axis=1), 0.0)
                sink_term = psink * delta
                for h in range(GQA_GROUP):
                    val = -jnp.sum(sink_term[:, BLOCK * h: BLOCK * (h + 1)], axis=1, keepdims=True)
                    dsink = dsink + jnp.where(lane == j * GQA_GROUP + h, val, 0.0)
            k_before[rows, :], k_own[rows, :] = dk_tot[0:BLOCK], dk_tot[BLOCK:]
            v_before[rows, :], v_own[rows, :] = dv_tot[0:BLOCK], dv_tot[BLOCK:]
        dsink_ref[0:1, :] += dsink
        plumb.run(m, nb // 2, False, c_in, c_out, c_scr)

    wide = pl.BlockSpec((2 * BLOCK, ATTN_WIDTH), lambda m: (m, 0))
    pair = pl.BlockSpec((2 * BLOCK, LANES), lambda m: (m, 0))
    before = lambda m: jnp.maximum(2 * m - 1, 0)
    res = pl.pallas_call(
        body, name=name, grid=(nb // 2,),
        in_specs=[pl.BlockSpec(memory_space=pltpu.SMEM),
                  pl.BlockSpec((2, 2 * BLOCK, STACKED), lambda m: (0, 0, 0)), wide, wide,
                  pl.BlockSpec((BLOCK, LANES), lambda m: (before(m), 0)), pair,
                  pl.BlockSpec((BLOCK, LANES), lambda m: (before(m), KV_COL_BLOCK_V)),
                  pl.BlockSpec((2 * BLOCK, LANES), lambda m: (m, KV_COL_BLOCK_V))] + [ANY] * plumb.n_in,
        out_specs=[wide, pair, pair, pair, pair, pl.BlockSpec((8, LANES), lambda m: (0, 0))] + [ANY] * plumb.n_out,
        out_shape=[jax.ShapeDtypeStruct((T, ATTN_WIDTH), BF)] + [jax.ShapeDtypeStruct((T, KV_WIDTH), F32)] * 4
        + [jax.ShapeDtypeStruct((8, LANES), F32)] + plumb.out_shapes,
        scratch_shapes=plumb.scratch, compiler_params=_params(("arbitrary",), plumb.collective_id()),
    )(sinks, _band_bias(), dout, qn, kn, kn, proj, proj, *plumb.args)
    return list(res[:6]), plumb.split_outputs(res[6:])


def _swiglu_fwd_epilogue(accs, ex):
    g, u = accs
    return [g, u, g * jax.nn.sigmoid(g) * u], []


def _swiglu_bwd_epilogue(accs, ex):
    (da,) = accs
    g, u = ex[0].astype(F32), ex[1].astype(F32)
    s = jax.nn.sigmoid(g)
    gs = g * s
    return [da * u * (s + gs - gs * s), da * gs], []


def _residual_norm_epilogue(scale):
    def epilogue(accs, ex):
        res, gain = ex
        h = res + scale * accs[0]
        r = lax.rsqrt(jnp.mean(h * h, axis=-1, keepdims=True) + RMS_EPS)
        return [h, h * r * gain], []
    return epilogue


def _rms_bwd_epilogue(accs, ex):
    (dn,) = accs
    xv, g, dres = ex
    r = lax.rsqrt(jnp.mean(xv * xv, axis=-1, keepdims=True) + RMS_EPS)
    xhat = xv * r
    dxhat = dn * g
    dx = dres + r * (dxhat - xhat * jnp.mean(dxhat * xhat, axis=-1, keepdims=True))
    return [dx, dx], [dn * xhat]


def _loss_epilogue(accs, ex):
    xv, target = ex
    d = xv + 0.5 * accs[0] - target
    dy = d * (1.0 / D_MODEL)
    return [dy, dy], [d * d]


def _merge_fwd_epilogue(accs, ex):
    (ba,) = accs
    bp, gp_pre, ga_pre, bias_p, bias_a = ex
    gp = jax.nn.sigmoid(gp_pre.astype(F32) + bias_p)
    ga = jax.nn.sigmoid(ga_pre.astype(F32) + bias_a)
    return [gp * bp.astype(F32) + ga * ba, ba], []


def _merge_bwd_epilogue(accs, ex):
    (dm,) = accs
    bp, ba, gp_pre, ga_pre, bias_p, bias_a = ex
    gp = jax.nn.sigmoid(gp_pre.astype(F32) + bias_p)
    ga = jax.nn.sigmoid(ga_pre.astype(F32) + bias_a)
    dbp, dba = dm * gp, dm * ga
    dgp = dbp * bp.astype(F32) * (1.0 - gp)
    dga = dba * ba.astype(F32) * (1.0 - ga)
    return [dbp, dba, dgp, dga], [dgp, dga]


def _prep(name, ws, transposes):
    n = len(ws)

    def body(*refs):
        for w_ref, o_ref, tr in zip(refs[:n], refs[n:], transposes):
            v = w_ref[...]
            o_ref[...] = (v.T if tr else v).astype(BF)

    shapes = [jax.ShapeDtypeStruct(w.shape[::-1] if tr else w.shape, BF) for w, tr in zip(ws, transposes)]
    return pl.pallas_call(body, name=name, out_shape=shapes, compiler_params=_params())(*ws)


def _adam_math(w, g, m, v):
    m = ADAM_B1 * m + (1.0 - ADAM_B1) * g
    v = ADAM_B2 * v + (1.0 - ADAM_B2) * jnp.square(g)
    m_hat = m / (1.0 - ADAM_B1 ** ADAM_STEP)
    v_hat = v / (1.0 - ADAM_B2 ** ADAM_STEP)
    delta = -ADAM_LR * (m_hat / (jnp.sqrt(v_hat) + ADAM_EPS) + ADAM_WD * w)
    return delta, m, v


def _adamw_sharded(name, items, transpose=False):
    n = len(items)

    def body(*refs):
        ins, outs = refs[:4 * n], refs[4 * n:]
        for k in range(n):
            s_ref, w_ref, m_ref, v_ref = ins[4 * k: 4 * k + 4]
            g = s_ref[0].astype(F32)
            for i in range(1, 4):
                g = g + s_ref[i].astype(F32)
            if transpose:
                g = g.T
            delta, mn, vn = _adam_math(w_ref[...], g, m_ref[...], v_ref[...])
            for o_ref, val in zip(outs[4 * k: 4 * k + 4], (g, delta, mn, vn)):
                o_ref[...] = val

    flat = [a for item in items for a in item]
    out_shape = [jax.ShapeDtypeStruct(item[1].shape, F32) for item in items for _ in range(4)]
    _, r, C = items[0][0].shape
    rows = r // 4
    if transpose or rows % 8:
        res = pl.pallas_call(body, name=name, out_shape=out_shape, compiler_params=_params())(*flat)
    else:
        tile = pl.BlockSpec((rows, C), lambda i: (i, 0))
        res = pl.pallas_call(
            body, name=name, grid=(4,), in_specs=[pl.BlockSpec((4, rows, C), lambda i: (0, i, 0)), tile, tile, tile] * n,
            out_specs=[tile] * (4 * n), out_shape=out_shape, compiler_params=_params(("parallel",)),
        )(*flat)
    return [tuple(res[4 * k: 4 * k + 4]) for k in range(n)]


SMALL_LAYOUT = (("ffn1_norm", 0, (8, LANES)), ("mix_norm", 8, (8, LANES)), ("ffn2_norm", 16, (8, LANES)),
                ("gate_bias", 24, (16, LANES)), ("pool_scale", 40, (4, LANES)), ("q_norm", 48, (1, HEAD_DIM)),
                ("k_norm", 56, (1, HEAD_DIM)), ("sinks", 64, (1, N_HEADS)))
LOSS_ROW = 72
SMALL_ROWS = 80


def _adamw_small(name, g_vec, g_pool_w, params):
    n = len(SMALL_LAYOUT) + 1

    def body(vec_ref, pw_ref, *refs):
        ins, outs = refs[:3 * n], refs[3 * n:]
        vec = vec_ref[0]
        pw = pw_ref[0]
        for i in range(1, N_DEV):
            vec = vec + vec_ref[i]
            pw = pw + pw_ref[i]
        grads = [vec[r0:r0 + shape[0], 0:shape[1]] for _, r0, shape in SMALL_LAYOUT] + [pw]
        for p, g in enumerate(grads):
            w_ref, m_ref, v_ref = ins[3 * p: 3 * p + 3]
            delta, mn, vn = _adam_math(w_ref[...], g, m_ref[...], v_ref[...])
            for o_ref, val in zip(outs[4 * p: 4 * p + 4], (g, delta, mn, vn)):
                o_ref[...] = val
        outs[4 * n][...] = vec[LOSS_ROW:LOSS_ROW + 1, :]

    flat = [a for wmv in params for a in wmv]
    out_shape = [jax.ShapeDtypeStruct(wmv[0].shape, F32) for wmv in params for _ in range(4)]
    out_shape.append(jax.ShapeDtypeStruct((1, LANES), F32))
    res = pl.pallas_call(body, name=name, out_shape=out_shape, compiler_params=_params())(g_vec, g_pool_w, *flat)
    return [tuple(res[4 * p: 4 * p + 4]) for p in range(n)], res[4 * n]


def _place():
    x, y, c = lax.axis_index("x"), lax.axis_index("y"), lax.axis_index("c")
    other_chips = [(1 - x, y), (x, 1 - y), (1 - x, 1 - y)]
    return x, y, c, other_chips


def _rows(ref, r, place, natural=False):
    px, py, pc = place
    b = 4 * px + 2 * py + pc if natural else 4 * pc + 2 * px + py
    return ref.at[pl.ds(pl.multiple_of(b * r, 8), r), :]


def _gather_task(shards, natural=(), forward_at=0.75):
    n = len(shards)
    rs = [s.shape[0] for s in shards]
    rows_of = lambda ref, k, place: _rows(ref, rs[k], place, k in natural)

    def copy(scr, outs, k, slot, block, to, src=None):
        rows = rows_of(outs[k], k, block)
        return pltpu.make_async_remote_copy(
            src_ref=rows if src is None else src, dst_ref=rows, send_sem=scr[0].at[7 * k + slot],
            recv_sem=scr[1].at[7 * k + slot], device_id=to, device_id_type=MESH)

    def first_sends(ins, outs, scr):
        x, y, c, chips = _place()
        me = (x, y, c)
        cps = [copy(scr, outs, k, 1 + j, me, (*chip, c), src=ins[k]) for j, chip in enumerate(chips) for k in range(n)]
        return cps + [copy(scr, outs, k, 0, me, (x, y, 1 - c), src=ins[k]) for k in range(n)]

    def passed_on(outs, scr):
        x, y, c, chips = _place()
        return [copy(scr, outs, k, 4 + j, (*chip, c), (x, y, 1 - c)) for j, chip in enumerate(chips) for k in range(n)]

    def local(ins, outs, scr):
        x, y, c, _ = _place()
        return [pltpu.make_async_copy(ins[k], rows_of(outs[k], k, (x, y, c)), scr[2].at[k]) for k in range(n)]

    def start(ins, outs, scr):
        for cp in local(ins, outs, scr) + first_sends(ins, outs, scr):
            cp.start()

    def forward(ins, outs, scr):
        x, y, c, chips = _place()
        for j, chip in enumerate(chips):
            for k in range(n):
                copy(scr, outs, k, 1 + j, (*chip, c), (x, y, c)).wait_recv()
                copy(scr, outs, k, 4 + j, (*chip, c), (x, y, 1 - c)).start()

    def finish(ins, outs, scr):
        x, y, c, chips = _place()
        for k in range(n):
            copy(scr, outs, k, 0, (x, y, 1 - c), (x, y, c)).wait_recv()
        for j, chip in enumerate(chips):
            for k in range(n):
                copy(scr, outs, k, 4 + j, (*chip, 1 - c), (x, y, c)).wait_recv()
        for cp in first_sends(ins, outs, scr) + passed_on(outs, scr):
            cp.wait_send()
        for cp in local(ins, outs, scr):
            cp.wait()

    out_shapes = [jax.ShapeDtypeStruct((N_DEV * s.shape[0], s.shape[1]), s.dtype) for s in shards]
    scratch = [pltpu.SemaphoreType.DMA((7 * n,)), pltpu.SemaphoreType.DMA((7 * n,)), pltpu.SemaphoreType.DMA((n,))]
    return _Task(shards, out_shapes, scratch, [(0, start), (forward_at, forward), (1.0, finish)], ("sibling", "chips"))


def _direct_gather_task(shards):
    n = len(shards)
    rs = [s.shape[0] for s in shards]

    def peers():
        x, y, c, _ = _place()
        flip = lambda v, bit: 1 - v if bit else v
        return (x, y, c), [(flip(x, (s >> 2) & 1), flip(y, (s >> 1) & 1), flip(c, s & 1)) for s in range(1, N_DEV)]

    def copies(ins, outs, scr):
        me, others = peers()
        local = [pltpu.make_async_copy(ins[k], _rows(outs[k], rs[k], me), scr[2].at[k]) for k in range(n)]
        sems = lambda k, s: dict(send_sem=scr[0].at[7 * k + s], recv_sem=scr[1].at[7 * k + s], device_id_type=MESH)
        sends = [pltpu.make_async_remote_copy(src_ref=ins[k], dst_ref=_rows(outs[k], rs[k], me), device_id=to, **sems(k, s))
                 for s, to in enumerate(others) for k in range(n)]
        recvs = [pltpu.make_async_remote_copy(src_ref=_rows(outs[k], rs[k], frm), dst_ref=_rows(outs[k], rs[k], frm),
                                              device_id=me, **sems(k, s))
                 for s, frm in enumerate(others) for k in range(n)]
        return local, sends, recvs

    def start(ins, outs, scr):
        local, sends, _ = copies(ins, outs, scr)
        for cp in local + sends:
            cp.start()

    def finish(ins, outs, scr):
        local, sends, recvs = copies(ins, outs, scr)
        for cp in recvs:
            cp.wait_recv()
        for cp in sends:
            cp.wait_send()
        for cp in local:
            cp.wait()

    out_shapes = [jax.ShapeDtypeStruct((N_DEV * s.shape[0], s.shape[1]), s.dtype) for s in shards]
    scratch = [pltpu.SemaphoreType.DMA((7 * n,)), pltpu.SemaphoreType.DMA((7 * n,)), pltpu.SemaphoreType.DMA((n,))]
    return _Task(shards, out_shapes, scratch, [(0, start), (1.0, finish)], ("all",))


def _chip_task(sums):
    n = len(sums)
    rs = [s.shape[0] // 4 for s in sums]

    def block(ref, k, chip_index):
        return ref.at[pl.ds(pl.multiple_of(chip_index * rs[k], 8), rs[k]), :]

    def copies(ins, outs, scr):
        send_sems, recv_sems, local_sems = scr
        x, y, c, chips = _place()
        here = 2 * x + y
        local = [pltpu.make_async_copy(block(ins[k], k, here), outs[k].at[here], local_sems.at[k]) for k in range(n)]
        remote = []
        for j, (px, py) in enumerate(chips):
            remote += [pltpu.make_async_remote_copy(
                src_ref=block(ins[k], k, 2 * px + py), dst_ref=outs[k].at[here],
                send_sem=send_sems.at[3 * k + j], recv_sem=recv_sems.at[3 * k + j],
                device_id=(px, py, c), device_id_type=MESH) for k in range(n)]
        return local, remote

    def start(ins, outs, scr):
        local, remote = copies(ins, outs, scr)
        for cp in local + remote:
            cp.start()

    def finish(ins, outs, scr):
        local, remote = copies(ins, outs, scr)
        for cp in remote:
            cp.wait()
        for cp in local:
            cp.wait()

    out_shapes = [jax.ShapeDtypeStruct((4, r, s.shape[1]), s.dtype) for r, s in zip(rs, sums)]
    scratch = [pltpu.SemaphoreType.DMA((3 * n,)), pltpu.SemaphoreType.DMA((3 * n,)), pltpu.SemaphoreType.DMA((n,))]
    return _Task(sums, out_shapes, scratch, [(0, start), (1.0, finish)], ("chips",))


def _dw_pair(name, a, b, scale, comm=None, blocks=1):
    T, M = a.shape
    N = b.shape[1]
    half = M // 2
    wide = half // blocks
    tk = min(2048, T)
    nK = T // tk
    plumb = _CommPlumbing(comm)

    def body(core_ref, *rest):
        a_refs, b_ref, rest = rest[:blocks], rest[blocks], rest[blocks + 1:]
        c_in = rest[:plumb.n_in]
        o_ref = rest[plumb.n_in]
        c_out = rest[plumb.n_in + 1: plumb.n_in + 1 + plumb.n_out]
        acc, stage, land, send_sem, recv_sem = rest[plumb.n_in + 1 + plumb.n_out: plumb.n_in + 6 + plumb.n_out]
        c_scr = rest[plumb.n_in + 6 + plumb.n_out:]
        i, k = pl.program_id(0), pl.program_id(1)
        x, y, c, _ = _place()
        push = pltpu.make_async_remote_copy(src_ref=stage, dst_ref=land, send_sem=send_sem, recv_sem=recv_sem,
                                            device_id=(x, y, 1 - c), device_id_type=MESH)
        plumb.handshake((i == 0) & (k == 0), own=("sibling",))
        if comm:
            plumb.run(i * nK + k, 2 * nK, True, c_in, c_out, c_scr)

        av = a_refs[0][...] if blocks == 1 else jnp.concatenate([r[...] for r in a_refs], axis=1)
        p = lax.dot_general(av, b_ref[...], _DIMS["tn"], preferred_element_type=F32)

        @pl.when(k == 0)
        def _():
            acc[...] = p

        @pl.when(k > 0)
        def _():
            acc[...] += p

        @pl.when((i == 0) & (k == nK - 1))
        def _():
            stage[...] = (scale * acc[...]).astype(BF)
            push.start()

        @pl.when((i == 1) & (k == nK - 1))
        def _():
            push.wait_recv()
            o_ref[...] = (scale * acc[...] + land[...].astype(F32)).astype(BF)
            push.wait_send()

        if comm:
            plumb.run(i * nK + k, 2 * nK, False, c_in, c_out, c_scr)

    grid_spec = pltpu.PrefetchScalarGridSpec(
        num_scalar_prefetch=1, grid=(2, nK),
        in_specs=[pl.BlockSpec((tk, wide), functools.partial(
            lambda i, k, core, j: (k, (2 * j if blocks > 1 else 0) + jnp.where(i == 0, 1 - core[0], core[0])), j=j))
            for j in range(blocks)] + [pl.BlockSpec((tk, N), lambda i, k, core: (k, 0))] + [ANY] * plumb.n_in,
        out_specs=[pl.BlockSpec((half, N), lambda i, k, core: (0, 0))] + [ANY] * plumb.n_out,
        scratch_shapes=[pltpu.VMEM((half, N), F32), pltpu.VMEM((half, N), BF), pltpu.VMEM((half, N), BF),
                        pltpu.SemaphoreType.DMA, pltpu.SemaphoreType.DMA] + plumb.scratch)
    core = lax.axis_index("c").astype(jnp.int32).reshape(1)
    res = pl.pallas_call(
        body, name=name, grid_spec=grid_spec,
        out_shape=[jax.ShapeDtypeStruct((half, N), BF)] + plumb.out_shapes,
        compiler_params=_params(("arbitrary", "arbitrary"), plumb.collective_id(own=("sibling",))),
    )(core, *([a] * blocks), b, *plumb.args)
    return (res[0], plumb.split_outputs(res[1:])) if comm else res[0]


def _pair_task(parts):
    n = len(parts)

    def copies(ins, outs, scr):
        x, y, c, _ = _place()
        return [pltpu.make_async_remote_copy(
            src_ref=ins[k].at[:, pl.ds(1 - c, 1)], dst_ref=outs[k], send_sem=scr[0].at[k], recv_sem=scr[1].at[k],
            device_id=(x, y, 1 - c), device_id_type=MESH) for k in range(n)]

    def start(ins, outs, scr):
        for cp in copies(ins, outs, scr):
            cp.start()

    def finish(ins, outs, scr):
        for cp in copies(ins, outs, scr):
            cp.wait()

    out_shapes = [jax.ShapeDtypeStruct((4, 1) + p.shape[2:], p.dtype) for p in parts]
    scratch = [pltpu.SemaphoreType.DMA((n,)), pltpu.SemaphoreType.DMA((n,))]
    return _Task(parts, out_shapes, scratch, [(0, start), (1.0, finish)], ("sibling",))


def _pair_sum(name, part, got, core):
    _, _, r, C = part.shape

    def body(core_ref, p_ref, g_ref, o_ref):
        o_ref[0] = (p_ref[0, 0].astype(F32) + g_ref[0, 0].astype(F32)).astype(o_ref.dtype)

    return pl.pallas_call(
        body, name=name,
        grid_spec=pltpu.PrefetchScalarGridSpec(
            num_scalar_prefetch=1, grid=(4,),
            in_specs=[pl.BlockSpec((1, 1, r, C), lambda i, core_ref: (i, core_ref[0], 0, 0)),
                      pl.BlockSpec((1, 1, r, C), lambda i, core_ref: (i, 0, 0, 0))],
            out_specs=pl.BlockSpec((1, r, C), lambda i, core_ref: (i, 0, 0))),
        out_shape=jax.ShapeDtypeStruct((4, r, C), part.dtype), compiler_params=_params(("parallel",)),
    )(core, part, got)


def _ffn_bwd(tag, dy, dyb, x, gain, wgT, wuT, wd, saved, earlier=None):
    n, g, u, a = saved
    half = lambda accs, ex: _swiglu_bwd_epilogue([0.5 * accs[0]], ex)
    act_args = dict(tm=1024, tn=1408, tk=D_MODEL, epilogue=half, extras=[(g, "tile", 0), (u, "tile", 0)], cols_outer=True)
    if earlier is None:
        sum_d = _dw_pair(tag + "_dw_down", a, dyb, 0.5)
        (dg, du), ((slots_d,),) = _mm(tag + "_d_act", [(dyb, wd, "nt", 0)], [BF, BF], comm=[_chip_task([sum_d])], **act_args)
        slots_e = None
        sum_g = _dw_pair(tag + "_dw_gate", dg, n, 1.0)
    else:
        sum_d, ((got,),) = _dw_pair(tag + "_dw_down", a, dyb, 0.5, comm=[_pair_task([earlier])])
        core = lax.axis_index("c").astype(jnp.int32).reshape(1)
        sum_e = _pair_sum(tag + "_pair_sum_earlier", earlier, got, core)
        sum_e = sum_e.reshape(4 * sum_e.shape[1], sum_e.shape[2])
        (dg, du), ((slots_e,),) = _mm(tag + "_d_act", [(dyb, wd, "nt", 0)], [BF, BF], comm=[_chip_task([sum_e])], **act_args)
        sum_g, ((slots_d,),) = _dw_pair(tag + "_dw_gate", dg, n, 1.0, comm=[_chip_task([sum_d])])
    sum_u, ((slots_g,),) = _dw_pair(tag + "_dw_up", du, n, 1.0, comm=[_chip_task([sum_g])])
    (dx, dxb, dgain), ((slots_u,),) = _mm(
        tag + "_d_norm", [(dg, wgT, "nn", 0), (du, wuT, "nn", 0)], [F32, BF], tm=512, tn=D_MODEL, tk=D_FF,
        epilogue=_rms_bwd_epilogue, extras=[(x, "tile", 0), (gain, "row", 0), (dy, "tile", 0)], n_colsum=1,
        comm=[_chip_task([sum_u])])
    return dx, dxb, dgain, slots_e, slots_g, slots_u, slots_d


def _tile_gain(g):
    return jnp.concatenate([g, g]).reshape(1, LANES)


def _fold_heads(partials):
    return jnp.sum(partials.reshape(-1, HEAD_DIM), axis=0)


def _pack_small_grads(grads, loss_local):
    pieces, row = [], 0
    for name, r0, _ in SMALL_LAYOUT + (("loss", LOSS_ROW, None),):
        v = (loss_local if name == "loss" else grads[name]).reshape(-1)
        rows = -(-v.size // LANES)
        block = jnp.pad(v, (0, rows * LANES - v.size)).reshape(rows, LANES)
        pieces += [jnp.zeros((r0 - row, LANES), F32)] * (r0 > row) + [block]
        row = r0 + rows
    pieces.append(jnp.zeros((SMALL_ROWS - row, LANES), F32))
    return jnp.concatenate(pieces, axis=0)


def kernel(x, ffn1_norm, ffn1_w_gate, ffn1_w_up, ffn1_w_down, mix_norm, w_in, pool_w, pool_scale, w_pool_out, q_norm, k_norm, sinks, w_attn_out, gate_bias, w_out, ffn2_norm, ffn2_w_gate, ffn2_w_up, ffn2_w_down, loss_target, m_ffn1_norm, m_ffn1_w_gate, m_ffn1_w_up, m_ffn1_w_down, m_mix_norm, m_w_in, m_pool_w, m_pool_scale, m_w_pool_out, m_q_norm, m_k_norm, m_sinks, m_w_attn_out, m_gate_bias, m_w_out, m_ffn2_norm, m_ffn2_w_gate, m_ffn2_w_up, m_ffn2_w_down, v_ffn1_norm, v_ffn1_w_gate, v_ffn1_w_up, v_ffn1_w_down, v_mix_norm, v_w_in, v_pool_w, v_pool_scale, v_w_pool_out, v_q_norm, v_k_norm, v_sinks, v_w_attn_out, v_gate_bias, v_w_out, v_ffn2_norm, v_ffn2_w_gate, v_ffn2_w_up, v_ffn2_w_down):
    T = x.shape[1]
    x2 = x.reshape(T, D_MODEL)
    target = loss_target.reshape(T, D_MODEL)

    big = [
        ("ffn1_w_gate", ffn1_w_gate, m_ffn1_w_gate, v_ffn1_w_gate, True, False),
        ("ffn1_w_up", ffn1_w_up, m_ffn1_w_up, v_ffn1_w_up, True, False),
        ("ffn1_w_down", ffn1_w_down, m_ffn1_w_down, v_ffn1_w_down, False, False),
        ("w_in", w_in, m_w_in, v_w_in, True, False),
        ("w_pool_out", w_pool_out, m_w_pool_out, v_w_pool_out, False, True),
        ("w_attn_out", w_attn_out, m_w_attn_out, v_w_attn_out, False, False),
        ("w_out", w_out, m_w_out, v_w_out, False, False),
        ("ffn2_w_gate", ffn2_w_gate, m_ffn2_w_gate, v_ffn2_w_gate, True, False),
        ("ffn2_w_up", ffn2_w_up, m_ffn2_w_up, v_ffn2_w_up, True, False),
        ("ffn2_w_down", ffn2_w_down, m_ffn2_w_down, v_ffn2_w_down, False, False),
    ]
    view = lambda a, tv: a.T if tv else a
    views = [view(w, tv) for _, w, _, _, tv, _ in big]
    in_kernel_t = [tk_ for *_, tk_ in big]
    first_shards = _prep("prep_ffn1_gate_up", views[0:2], in_kernel_t[0:2])
    g1 = ffn1_norm.reshape(1, D_MODEL)
    g2 = mix_norm.reshape(1, D_MODEL)
    g3 = ffn2_norm.reshape(1, D_MODEL)
    bias_row = gate_bias.reshape(1, 2 * D_MODEL)
    qg, kg = _tile_gain(q_norm) * ATTN_SCALE, _tile_gain(k_norm)
    scale_row = pool_scale.reshape(1, POOL_WIDTH)

    n1, later_shards, ((wg1T, wu1T),) = _rms_fwd(
        "ffn1_norm", x2, g1, [_gather_task(first_shards, forward_at=0.9)], views[2:], in_kernel_t[2:])
    shards = list(first_shards) + later_shards
    (gt1, up1, act1), ((wd1,), (w_inT,)) = _mm(
        "ffn1_gate_up", [(n1, wg1T, "nt", 0), (n1, wu1T, "nt", 1)], [BF, BF, BF], tm=1024, tn=1408, tk=D_MODEL,
        epilogue=_swiglu_fwd_epilogue, cols_outer=True,
        comm=[_gather_task(shards[2:3], forward_at=0.5), _gather_task(shards[3:4], natural=(0,), forward_at=0.9)])
    (h1, u), ((w_poT, w_ao, w_o),) = _mm(
        "ffn1_down", [(act1, wd1, "nn", 0)], [F32, BF], tm=512, tn=D_MODEL, tk=D_FF,
        epilogue=_residual_norm_epilogue(0.5), extras=[(x2, "tile", 0), (g2, "row", 0)],
        comm=[_gather_task(shards[4:7], natural=(0, 1, 2), forward_at=0.8)])
    saved1 = (n1, gt1, up1, act1)
    (proj,), ((wg2T,),) = _mm(
        "in_proj", [(u, w_inT, "nt", 0)], [BF], tm=1024, tn=1280, tk=D_MODEL, cols_outer=True,
        comm=[_gather_task(shards[7:8], forward_at=0.8)])
    pooled, mixed = _pool_fwd("pool_fwd", proj, pool_w, scale_row)
    qn = _headnorm_fwd("q_norm", proj, COL_Q, ATTN_WIDTH, qg)
    kn = _headnorm_fwd("k_norm", proj, COL_K, KV_WIDTH, kg)
    attn, ((wu2T,),) = _attn_fwd("attn_fwd", qn, kn, proj, sinks, comm=[_gather_task(shards[8:9], forward_at=0.8)])
    (bp,) = _mm("pool_out", [(mixed, w_poT, "nt", 0)], [BF], tm=1024, tn=D_MODEL, tk=POOL_WIDTH)
    gate_tn = 256
    gate_extras = [(proj, "tile", COL_GP // gate_tn), (proj, "tile", COL_GA // gate_tn),
                   (bias_row, "row", 0), (bias_row, "row", D_MODEL // gate_tn)]
    merged, ba = _mm("attn_out_merge", [(attn, w_ao, "nn", 0)], [BF, BF], tm=2048, tn=gate_tn, tk=ATTN_WIDTH,
                     epilogue=_merge_fwd_epilogue, extras=[(bp, "tile", 0)] + gate_extras)
    h2, n2 = _mm("mix_out", [(merged, w_o, "nn", 0)], [F32, BF], tm=1024, tn=D_MODEL, tk=D_MODEL,
                 epilogue=_residual_norm_epilogue(1.0), extras=[(h1, "tile", 0), (g3, "row", 0)])
    (gt2, up2, act2), ((wd2,),) = _mm(
        "ffn2_gate_up", [(n2, wg2T, "nt", 0), (n2, wu2T, "nt", 1)], [BF, BF, BF], tm=1024, tn=1408, tk=D_MODEL,
        epilogue=_swiglu_fwd_epilogue, cols_outer=True, comm=[_gather_task(shards[9:10], forward_at=0.8)])
    dy, dyb, sq = _mm("ffn2_down_loss", [(act2, wd2, "nn", 0)], [F32, BF], tm=512, tn=D_MODEL, tk=D_FF,
                      epilogue=_loss_epilogue, extras=[(h2, "tile", 0), (target, "tile", 0)], n_colsum=1)
    loss_local = 0.5 * jnp.sum(sq) / D_MODEL

    dh2, dh2b, dg3, _, slots_g2, slots_u2, slots_d2 = _ffn_bwd(
        "ffn2", dy, dyb, h2, g3, wg2T, wu2T, wd2, (n2, gt2, up2, act2))
    dbp, dba, dproj, dga, cs_gp, cs_ga = _mm(
        "mix_out_bwd", [(dh2b, w_o, "nt", 0)], [BF, BF, BF, BF], tm=2048, tn=gate_tn, tk=D_MODEL,
        epilogue=_merge_bwd_epilogue, extras=[(bp, "tile", 0), (ba, "tile", 0)] + gate_extras, n_colsum=2,
        out_placement={2: (IN_WIDTH, COL_GP)})
    sum_o = _dw_pair("dw_out", merged, dh2b, 1.0, blocks=4)
    (dmixed,) = _mm("pool_out_bwd", [(dbp, w_poT, "nn", 0)], [BF], tm=1024, tn=POOL_WIDTH, tk=D_MODEL)
    sum_po = _dw_pair("dw_pool_out", dbp, mixed, 1.0, blocks=4)
    (dattn,) = _mm("attn_out_bwd", [(dba, w_ao, "nt", 0)], [BF], tm=1024, tn=ATTN_WIDTH, tk=D_MODEL)
    sum_ao = _dw_pair("dw_attn_out", attn, dba, 1.0, blocks=4)
    (dqn, k_own, k_before, v_own, v_before, dsink_tile), ((slots_o, slots_po, slots_ao),) = _attn_bwd(
        "attn_bwd", dattn, qn, kn, proj, sinks, [_chip_task([sum_o, sum_po, sum_ao])])
    next_block = lambda a: jnp.concatenate([a[BLOCK:], jnp.zeros((BLOCK, KV_WIDTH), F32)], axis=0)
    dkn = (k_own + next_block(k_before)).astype(BF)
    dv = (v_own + next_block(v_before)).astype(BF)
    dproj, dqg = _headnorm_bwd("q_norm_bwd", dqn, proj, COL_Q, ATTN_WIDTH, qg, dproj)
    dproj, dkg = _headnorm_bwd("k_norm_bwd", dkn, proj, COL_K, KV_WIDTH, kg, dproj)
    dproj, dpool_w, dpool_scale = _pool_bwd("pool_bwd", dmixed, pooled, pool_w, scale_row, dproj)
    for piece, col in ((dv, COL_V), (dga, COL_GA)):
        dproj = lax.dynamic_update_slice(dproj, piece, (0, col))
    (dh1, dh1b, dg2), ((g_pool_w,),) = _mm(
        "in_proj_bwd", [(dproj, w_inT, "nn", 0)], [F32, BF], tm=512, tn=D_MODEL, tk=IN_WIDTH, epilogue=_rms_bwd_epilogue,
        extras=[(h1, "tile", 0), (g2, "row", 0), (dh2, "tile", 0)], n_colsum=1,
        comm=[_gather_task([dpool_w.reshape(-1, LANES)])])
    (dw_inT,) = _mm("dw_in", [(dproj, u, "tn", 0)], [BF], tm=1920, tn=D_MODEL, tk=2048)
    dx, _, dg1, slots_in, slots_g1, slots_u1, slots_d1 = _ffn_bwd(
        "ffn1", dh1, dh1b, x2, g1, wg1T, wu1T, wd1, saved1, dw_inT.reshape(4, 2, IN_WIDTH // N_DEV, D_MODEL))

    slots = [slots_g1, slots_u1, slots_d1, slots_in, slots_po, slots_ao, slots_o, slots_g2, slots_u2, slots_d2]
    big_out = {}
    for label, group in (("ffn", (0, 1, 2, 7, 8, 9)), ("w_in", (3,)), ("w_pool_out", (4,)), ("attn_out_and_out", (5, 6))):
        items = [(slots[k], view(big[k][1], big[k][4]), view(big[k][2], big[k][4]), view(big[k][3], big[k][4]))
                 for k in group]
        for k, res in zip(group, _adamw_sharded("adamw_" + label, items, transpose=big[group[0]][5])):
            big_out[big[k][0]] = tuple(view(r, big[k][4]) for r in res)

    small_grads = {
        "ffn1_norm": jnp.sum(dg1, axis=(0, 1)), "mix_norm": jnp.sum(dg2, axis=(0, 1)), "ffn2_norm": jnp.sum(dg3, axis=(0, 1)),
        "gate_bias": jnp.concatenate([jnp.sum(cs_gp, axis=(0, 1)), jnp.sum(cs_ga, axis=(0, 1))]),
        "pool_scale": dpool_scale, "q_norm": _fold_heads(dqg) * ATTN_SCALE, "k_norm": _fold_heads(dkg),
        "sinks": dsink_tile[0, :N_HEADS]}
    ((g_vec,),) = _comm_only("gather_small_grads", [_direct_gather_task([_pack_small_grads(small_grads, loss_local)])])
    given = {"ffn1_norm": (ffn1_norm, m_ffn1_norm, v_ffn1_norm), "mix_norm": (mix_norm, m_mix_norm, v_mix_norm),
             "ffn2_norm": (ffn2_norm, m_ffn2_norm, v_ffn2_norm), "gate_bias": (gate_bias, m_gate_bias, v_gate_bias),
             "pool_scale": (pool_scale, m_pool_scale, v_pool_scale), "q_norm": (q_norm, m_q_norm, v_q_norm),
             "k_norm": (k_norm, m_k_norm, v_k_norm), "sinks": (sinks, m_sinks, v_sinks)}
    params = [tuple(a.reshape(shape) for a in given[nm]) for nm, _, shape in SMALL_LAYOUT]
    params.append(tuple(a.reshape(-1, LANES) for a in (pool_w, m_pool_w, v_pool_w)))
    small_res, loss_row = _adamw_small("adamw_small", g_vec.reshape(N_DEV, SMALL_ROWS, LANES),
                                       g_pool_w.reshape(N_DEV, -1, LANES), params)
    small_out = {nm: tuple(r.reshape(given[nm][0].shape) for r in res)
                 for (nm, _, _), res in zip(SMALL_LAYOUT, small_res)}
    small_out["pool_w"] = tuple(r.reshape(pool_w.shape) for r in small_res[-1])
    loss = loss_row[0, 0]

    order = ["ffn1_norm", "ffn1_w_gate", "ffn1_w_up", "ffn1_w_down", "mix_norm", "w_in", "pool_w", "pool_scale",
             "w_pool_out", "q_norm", "k_norm", "sinks", "w_attn_out", "gate_bias", "w_out", "ffn2_norm",
             "ffn2_w_gate", "ffn2_w_up", "ffn2_w_down"]
    every = {**big_out, **small_out}
    outs = [loss, dx.reshape(x.shape)]
    for j in range(4):
        outs += [every[nm][j] for nm in order]
    return tuple(outs)
```

```python
import functools

import jax
import jax.numpy as jnp
from jax import lax
from jax.experimental import pallas as pl
from jax.experimental.pallas import tpu as pltpu

BF = jnp.bfloat16
F32 = jnp.float32

D_MODEL = 1024
D_FF = 2816
POOL_WIDTH = 512
POOL_GROUP = 128
N_POOL_GROUPS = 4
HEAD_DIM = 64
N_HEADS = 16
GQA_GROUP = 8
BLOCK = 128
ATTN_WIDTH = 1024
KV_WIDTH = 128
IN_WIDTH = 3840
RMS_EPS = 1e-6
N_DEV = 8
LANES = 128

COL_Q = POOL_WIDTH
COL_K = COL_Q + ATTN_WIDTH
COL_V = COL_K + KV_WIDTH
COL_GP = COL_V + KV_WIDTH
COL_GA = COL_GP + D_MODEL

ADAM_LR = 0.001
ADAM_B1 = 0.9
ADAM_B2 = 0.999
ADAM_EPS = 1e-08
ADAM_WD = 0.01
ADAM_STEP = 10

VMEM_LIMIT_V7X = 56 * 1024 * 1024
MESH = pl.DeviceIdType.MESH
ANY = pl.BlockSpec(memory_space=pl.ANY)


def _params(sem=None, collective_id=None):
    return pltpu.CompilerParams(dimension_semantics=sem, vmem_limit_bytes=VMEM_LIMIT_V7X, collective_id=collective_id)


COLLECTIVE_IDS = {frozenset(["sibling"]): 0, frozenset(["chips"]): 1, frozenset(["sibling", "chips"]): 2}


def _handshake(peer_kinds):
    x, y, c, chips = _place()
    peers = ([(x, y, 1 - c)] if "sibling" in peer_kinds else []) + ([(*chip, c) for chip in chips] if "chips" in peer_kinds else [])
    barrier = pltpu.get_barrier_semaphore()
    for peer in peers:
        pl.semaphore_signal(barrier, inc=1, device_id=peer, device_id_type=MESH)
    pl.semaphore_wait(barrier, len(peers))


_DIMS = {"nt": (((1,), (1,)), ((), ())), "nn": (((1,), (0,)), ((), ())), "tn": (((0,), (0,)), ((), ()))}


class _Task:
    def __init__(self, inputs, out_shapes, scratch, phases, peers):
        self.inputs, self.out_shapes, self.scratch = list(inputs), list(out_shapes), list(scratch)
        self.phases = list(phases)
        self.peers = frozenset(peers)


class _CommPlumbing:
    def __init__(self, tasks):
        self.tasks = list(tasks or [])
        self.args = [a for t in self.tasks for a in t.inputs]
        self.out_shapes = [o for t in self.tasks for o in t.out_shapes]
        self.scratch = [s for t in self.tasks for s in t.scratch]
        self.n_in, self.n_out = len(self.args), len(self.out_shapes)

    def peer_kinds(self, own=()):
        kinds = frozenset(own).union(*[t.peers for t in self.tasks])
        return None if "all" in kinds or not kinds else kinds

    def collective_id(self, own=()):
        kinds = self.peer_kinds(own)
        return None if kinds is None else COLLECTIVE_IDS[kinds]

    def handshake(self, first, own=()):
        kinds = self.peer_kinds(own)
        if kinds is not None:
            pl.when(first)(functools.partial(_handshake, kinds))

    def _slices(self, c_in, c_out, c_scr):
        i = o = s = 0
        for t in self.tasks:
            yield t, c_in[i:i + len(t.inputs)], c_out[o:o + len(t.out_shapes)], c_scr[s:s + len(t.scratch)]
            i, o, s = i + len(t.inputs), o + len(t.out_shapes), s + len(t.scratch)

    def run(self, step, steps, before, c_in, c_out, c_scr):
        for t, ins, outs, scr in self._slices(c_in, c_out, c_scr):
            for frac, fn in t.phases:
                if step is None:
                    fn(ins, outs, scr)
                elif before == (frac == 0):
                    at = 0 if frac == 0 else max(0, min(steps, -(-int(round(frac * steps * 64)) // 64)) - 1)
                    pl.when(step == at)(functools.partial(fn, ins, outs, scr))

    def split_outputs(self, flat):
        res, o = [], 0
        for t in self.tasks:
            res.append(list(flat[o:o + len(t.out_shapes)]))
            o += len(t.out_shapes)
        return res


def _comm_only(name, tasks):
    plumb = _CommPlumbing(tasks)

    def body(*refs):
        c_in, c_out = refs[:plumb.n_in], refs[plumb.n_in: plumb.n_in + plumb.n_out]
        c_scr = refs[plumb.n_in + plumb.n_out:]
        plumb.run(None, 1, True, c_in, c_out, c_scr)

    res = pl.pallas_call(
        body, name=name, in_specs=[ANY] * plumb.n_in, out_specs=[ANY] * plumb.n_out, out_shape=plumb.out_shapes,
        scratch_shapes=plumb.scratch, compiler_params=pltpu.CompilerParams(has_side_effects=True),
    )(*plumb.args)
    return plumb.split_outputs(res)


def _mm(name, terms, out_dtypes, *, tm, tn, tk, epilogue=None, extras=(), n_colsum=0, comm=None, cols_outer=False,
        out_placement=None):
    a0, b0, mode0, _ = terms[0]
    if mode0 == "nt":
        (M, K), N = a0.shape, b0.shape[0]
    elif mode0 == "nn":
        (M, K), N = a0.shape, b0.shape[1]
    else:
        (K, M), N = a0.shape, b0.shape[1]
    tm, tn, tk = min(tm, M), min(tn, N), min(tk, K)
    assert M % tm == 0 and N % tn == 0 and K % tk == 0, (name, M, N, K, tm, tn, tk)
    nI, nJ, nK = M // tm, N // tn, K // tk
    n_terms = len(terms)
    n_acc = max(t[3] for t in terms) + 1
    n_ex = len(extras)
    n_out = len(out_dtypes)
    if epilogue is None:
        epilogue = lambda accs, ex: ([accs[0]], [])
    plumb = _CommPlumbing(comm)
    n_scr = n_acc if nK > 1 else 0
    grid = (nJ, nI, nK) if cols_outer else (nI, nJ, nK)

    def body(*refs):
        n_in = 2 * n_terms + n_ex
        ab = refs[: 2 * n_terms]
        ex_refs = refs[2 * n_terms: n_in]
        c_in = refs[n_in: n_in + plumb.n_in]
        o0 = n_in + plumb.n_in
        out_refs = refs[o0: o0 + n_out]
        cs_refs = refs[o0 + n_out: o0 + n_out + n_colsum]
        c_out = refs[o0 + n_out + n_colsum: o0 + n_out + n_colsum + plumb.n_out]
        s0 = o0 + n_out + n_colsum + plumb.n_out
        acc_refs = refs[s0: s0 + n_scr]
        c_scr = refs[s0 + n_scr:]
        steps = grid[0] * grid[1] * nK
        if comm:
            step = (pl.program_id(0) * grid[1] + pl.program_id(1)) * nK + pl.program_id(2)
            plumb.handshake(step == 0)
            plumb.run(step, steps, True, c_in, c_out, c_scr)

        def products():
            accs = [None] * n_acc
            for t, (_, _, mode, ai) in enumerate(terms):
                p = lax.dot_general(ab[2 * t][...], ab[2 * t + 1][...], _DIMS[mode], preferred_element_type=F32)
                accs[ai] = p if accs[ai] is None else accs[ai] + p
            return accs

        def finish(accs):
            outs, colsums = epilogue(accs, [r[...] for r in ex_refs])
            for r, o in zip(out_refs, outs):
                r[...] = o.astype(r.dtype)
            for r, cs in zip(cs_refs, colsums):
                r[...] = jnp.sum(cs, axis=0, keepdims=True).reshape(r.shape)

        if nK == 1:
            finish(products())
        else:
            k = pl.program_id(2)
            accs = products()

            @pl.when(k == 0)
            def _():
                for r, a in zip(acc_refs, accs):
                    r[...] = a

            @pl.when(k > 0)
            def _():
                for r, a in zip(acc_refs, accs):
                    r[...] += a

            @pl.when(k == nK - 1)
            def _():
                finish([r[...] for r in acc_refs])

        if comm:
            plumb.run(step, steps, False, c_in, c_out, c_scr)

    def spec(block, index, fixed=False):
        imap = (lambda q, p, k: index(p, q, k)) if cols_outer else index
        return pl.BlockSpec(block, imap, pipeline_mode=pl.Buffered(1)) if fixed else pl.BlockSpec(block, imap)

    in_specs, args = [], []
    for a, b, mode, _ in terms:
        if mode == "nt":
            in_specs += [spec((tm, tk), lambda i, j, k: (i, k), nI * nK == 1),
                         spec((tn, tk), lambda i, j, k: (j, k), nJ * nK == 1)]
        elif mode == "nn":
            in_specs += [spec((tm, tk), lambda i, j, k: (i, k), nI * nK == 1),
                         spec((tk, tn), lambda i, j, k: (k, j), nJ * nK == 1)]
        else:
            in_specs += [spec((tk, tm), lambda i, j, k: (k, i), nI * nK == 1),
                         spec((tk, tn), lambda i, j, k: (k, j), nJ * nK == 1)]
        args += [a, b]
    for arr, kind, off in extras:
        if kind == "tile":
            in_specs.append(spec((tm, tn), functools.partial(lambda i, j, k, off: (i, j + off), off=off)))
        else:
            in_specs.append(spec((1, tn), functools.partial(lambda i, j, k, off: (0, j + off), off=off)))
        args.append(arr)
    placed = dict(out_placement or {})
    out_shape = [jax.ShapeDtypeStruct((M, placed.get(o, (N, 0))[0]), dt) for o, dt in enumerate(out_dtypes)]
    out_specs = [spec((tm, tn), functools.partial(lambda i, j, k, off: (i, j + off), off=placed.get(o, (N, 0))[1] // tn))
                 for o in range(n_out)]
    out_shape += [jax.ShapeDtypeStruct((nI, 1, N), F32) for _ in range(n_colsum)]
    out_specs += [spec((1, 1, tn), lambda i, j, k: (i, 0, j)) for _ in range(n_colsum)]
    scratch = [pltpu.VMEM((tm, tn), F32) for _ in range(n_scr)]
    args += plumb.args
    in_specs += [ANY] * plumb.n_in
    out_shape += plumb.out_shapes
    out_specs += [ANY] * plumb.n_out
    sem = ("arbitrary",) * 3 if comm else ("parallel", "parallel", "arbitrary")
    res = pl.pallas_call(
        body, name=name, grid=grid, in_specs=in_specs, out_specs=out_specs, out_shape=out_shape,
        scratch_shapes=scratch + plumb.scratch, compiler_params=_params(sem, plumb.collective_id()),
    )(*args)
    n_own = n_out + n_colsum
    return (list(res[:n_own]), plumb.split_outputs(res[n_own:])) if comm is not None else res


ROW_TILE = 512


def _rms_fwd(name, x, g, comm, weights, transposes):
    T, D = x.shape
    steps = T // ROW_TILE
    plumb = _CommPlumbing(comm)
    nw = len(weights)

    def body(x_ref, g_ref, *rest):
        w_refs, c_in = rest[:nw], rest[nw: nw + plumb.n_in]
        o_ref, shard_refs = rest[nw + plumb.n_in], rest[nw + plumb.n_in + 1: 2 * nw + plumb.n_in + 1]
        c_out = rest[2 * nw + plumb.n_in + 1: 2 * nw + plumb.n_in + 1 + plumb.n_out]
        c_scr = rest[2 * nw + plumb.n_in + 1 + plumb.n_out:]
        plumb.handshake(pl.program_id(0) == 0)
        plumb.run(pl.program_id(0), steps, True, c_in, c_out, c_scr)

        @pl.when(pl.program_id(0) == 0)
        def _():
            for w_ref, s_ref, tr in zip(w_refs, shard_refs, transposes):
                v = w_ref[...]
                s_ref[...] = (v.T if tr else v).astype(BF)

        xv = x_ref[...]
        r = lax.rsqrt(jnp.mean(xv * xv, axis=-1, keepdims=True) + RMS_EPS)
        o_ref[...] = (xv * r * g_ref[...]).astype(BF)
        plumb.run(pl.program_id(0), steps, False, c_in, c_out, c_scr)

    row = pl.BlockSpec((ROW_TILE, D), lambda i: (i, 0))
    whole = lambda shape: pl.BlockSpec(shape, lambda i: (0, 0), pipeline_mode=pl.Buffered(1))
    shard_shapes = [w.shape[::-1] if tr else w.shape for w, tr in zip(weights, transposes)]
    res = pl.pallas_call(
        body, name=name, grid=(steps,),
        in_specs=[row, pl.BlockSpec((1, D), lambda i: (0, 0))] + [whole(w.shape) for w in weights] + [ANY] * plumb.n_in,
        out_specs=[row] + [whole(s) for s in shard_shapes] + [ANY] * plumb.n_out,
        out_shape=[jax.ShapeDtypeStruct((T, D), BF)] + [jax.ShapeDtypeStruct(s, BF) for s in shard_shapes] + plumb.out_shapes,
        scratch_shapes=plumb.scratch, compiler_params=_params(("arbitrary",), plumb.collective_id()),
    )(x, g, *weights, *plumb.args)
    return res[0], list(res[1: nw + 1]), plumb.split_outputs(res[nw + 1:])


HEADNORM_TILE = 2048


def _half_sum_matrix():
    r = lax.broadcasted_iota(jnp.int32, (LANES, LANES), 0) // HEAD_DIM
    c = lax.broadcasted_iota(jnp.int32, (LANES, LANES), 1) // HEAD_DIM
    return (r == c).astype(BF)


def _head_mean(v, ones_blockdiag):
    hi = v.astype(BF)
    lo = (v - hi.astype(F32)).astype(BF)
    s = jnp.dot(hi, ones_blockdiag, preferred_element_type=F32) + jnp.dot(lo, ones_blockdiag, preferred_element_type=F32)
    return s * (1.0 / HEAD_DIM)


def _headnorm_fwd(name, proj, col0, width, g2):
    T = proj.shape[0]
    wide = min(width, GROUP_WIDTH)
    nb, off = width // wide, col0 // wide

    def body(x_ref, g_ref, b_ref, o_ref):
        for s in range(wide // LANES):
            lanes = slice(LANES * s, LANES * (s + 1))
            xv = x_ref[:, lanes].astype(F32)
            r = lax.rsqrt(_head_mean(xv * xv, b_ref[...]) + RMS_EPS)
            o_ref[:, lanes] = (xv * r * g_ref[...]).astype(BF)

    return pl.pallas_call(
        body, name=name, grid=(T // HEADNORM_TILE, nb),
        in_specs=[pl.BlockSpec((HEADNORM_TILE, wide), lambda i, j: (i, j + off)),
                  pl.BlockSpec((1, LANES), lambda i, j: (0, 0)), pl.BlockSpec((LANES, LANES), lambda i, j: (0, 0))],
        out_specs=pl.BlockSpec((HEADNORM_TILE, wide), lambda i, j: (i, j)),
        out_shape=jax.ShapeDtypeStruct((T, width), BF), compiler_params=_params(("parallel", "parallel")),
    )(proj, g2, _half_sum_matrix())


def _headnorm_bwd(name, dy, proj, col0, width, g2, into):
    T = proj.shape[0]
    wide = min(width, GROUP_WIDTH)
    nb, off = width // wide, col0 // wide

    def body(dy_ref, x_ref, g_ref, b_ref, into_ref, dx_ref, dg_ref):
        for s in range(wide // LANES):
            lanes = slice(LANES * s, LANES * (s + 1))
            xv = x_ref[:, lanes].astype(F32)
            dyv = dy_ref[:, lanes].astype(F32)
            r = lax.rsqrt(_head_mean(xv * xv, b_ref[...]) + RMS_EPS)
            xhat = xv * r
            dxhat = dyv * g_ref[...]
            dx_ref[:, lanes] = (r * (dxhat - xhat * _head_mean(dxhat * xhat, b_ref[...]))).astype(BF)
            dg_ref[0, :, lanes] = jnp.sum(dyv * xhat, axis=0, keepdims=True)

    return pl.pallas_call(
        body, name=name, grid=(T // HEADNORM_TILE, nb),
        in_specs=[pl.BlockSpec((HEADNORM_TILE, wide), lambda i, j: (i, j)),
                  pl.BlockSpec((HEADNORM_TILE, wide), lambda i, j: (i, j + off)),
                  pl.BlockSpec((1, LANES), lambda i, j: (0, 0)), pl.BlockSpec((LANES, LANES), lambda i, j: (0, 0)), ANY],
        out_specs=[pl.BlockSpec((HEADNORM_TILE, wide), lambda i, j: (i, j + off)),
                   pl.BlockSpec((1, 1, wide), lambda i, j: (i, 0, j))],
        out_shape=[jax.ShapeDtypeStruct(into.shape, BF), jax.ShapeDtypeStruct((T // HEADNORM_TILE, 1, width), F32)],
        input_output_aliases={4: 0}, compiler_params=_params(("parallel", "parallel")),
    )(dy, proj, g2, _half_sum_matrix(), into)


def _shift_down(v, k, row):
    return jnp.where(row >= k, pltpu.roll(v, k, axis=0), 0.0)


def _shift_up(v, k, row, T):
    return jnp.where(row < T - k, pltpu.roll(v, T - k, axis=0), 0.0)


def _by_group(g, vals):
    out = vals[-1]
    for i in range(len(vals) - 2, -1, -1):
        out = jnp.where(g == i, vals[i], out)
    return out


def _pool_fwd(name, proj, pool_w, pool_scale):
    T = proj.shape[0]

    def body(x_ref, w_ref, s_ref, pooled_ref, mixed_ref):
        g = pl.program_id(0)
        xv = x_ref[...].astype(F32)
        row = lax.broadcasted_iota(jnp.int32, (T, 1), 0)
        s2 = xv + _shift_down(xv, 1, row)
        s4 = s2 + _shift_down(s2, 2, row)
        s8 = s4 + _shift_down(s4, 4, row)
        s16 = s8 + _shift_down(s8, 8, row)
        wsum = _by_group(g, [s2, s4, s8, s16])
        count = jnp.minimum(row + 1, 2 << g).astype(F32)
        pooled = (wsum / count - xv).astype(BF)
        pooled_ref[...] = pooled
        mixed = jnp.dot(pooled, w_ref[0].astype(BF), preferred_element_type=F32) * s_ref[...]
        mixed_ref[...] = mixed.astype(BF)

    col = pl.BlockSpec((T, POOL_GROUP), lambda g: (0, g))
    return pl.pallas_call(
        body, name=name, grid=(N_POOL_GROUPS,),
        in_specs=[col, pl.BlockSpec((1, POOL_GROUP, POOL_GROUP), lambda g: (g, 0, 0)),
                  pl.BlockSpec((1, POOL_GROUP), lambda g: (0, g))],
        out_specs=[col, col],
        out_shape=[jax.ShapeDtypeStruct((T, POOL_WIDTH), BF), jax.ShapeDtypeStruct((T, POOL_WIDTH), BF)],
        compiler_params=_params(("parallel",)),
    )(proj, pool_w, pool_scale)


def _pool_bwd(name, dmixed, pooled, pool_w, pool_scale, into):
    T = dmixed.shape[0]

    def body(dm_ref, p_ref, w_ref, s_ref, into_ref, dx_ref, dw_ref, ds_ref):
        g = pl.program_id(0)
        dm = dm_ref[...].astype(F32)
        pooled = p_ref[...]
        w = w_ref[0].astype(BF)
        pre = jnp.dot(pooled, w, preferred_element_type=F32)
        ds_ref[...] = jnp.sum(dm * pre, axis=0, keepdims=True)
        dms = (dm * s_ref[...]).astype(BF)
        dw_ref[0] = lax.dot_general(pooled, dms, _DIMS["tn"], preferred_element_type=F32)
        dpooled = lax.dot_general(dms, w, _DIMS["nt"], preferred_element_type=F32)
        row = lax.broadcasted_iota(jnp.int32, (T, 1), 0)
        count = jnp.minimum(row + 1, 2 << g).astype(F32)
        z = dpooled / count
        l2 = z + _shift_up(z, 1, row, T)
        l4 = l2 + _shift_up(l2, 2, row, T)
        l8 = l4 + _shift_up(l4, 4, row, T)
        l16 = l8 + _shift_up(l8, 8, row, T)
        dx_ref[...] = (_by_group(g, [l2, l4, l8, l16]) - dpooled).astype(BF)

    col = pl.BlockSpec((T, POOL_GROUP), lambda g: (0, g))
    wspec = pl.BlockSpec((1, POOL_GROUP, POOL_GROUP), lambda g: (g, 0, 0))
    sspec = pl.BlockSpec((1, POOL_GROUP), lambda g: (0, g))
    return pl.pallas_call(
        body, name=name, grid=(N_POOL_GROUPS,), in_specs=[col, col, wspec, sspec, ANY], out_specs=[col, wspec, sspec],
        out_shape=[jax.ShapeDtypeStruct(into.shape, BF),
                   jax.ShapeDtypeStruct((N_POOL_GROUPS, POOL_GROUP, POOL_GROUP), F32),
                   jax.ShapeDtypeStruct((1, POOL_WIDTH), F32)],
        input_output_aliases={4: 0}, compiler_params=_params(("parallel",)),
    )(dmixed, pooled, pool_w, pool_scale, into)


ATTN_SCALE = HEAD_DIM ** -0.5
MASKED = float(jnp.finfo(jnp.float32).min)
KV_COL_BLOCK_V = COL_V // LANES
GROUP_WIDTH = GQA_GROUP * HEAD_DIM


def _dup_head(v, j):
    half = lax.broadcasted_iota(jnp.int32, (1, LANES), 1) // HEAD_DIM
    return jnp.where(half == j, v, pltpu.roll(v, HEAD_DIM, axis=1))


def _stack_heads(v, low):
    pieces = []
    for p in range(GROUP_WIDTH // LANES):
        vp = v[:, LANES * p: LANES * (p + 1)]
        pieces.append(jnp.where(low, vp, jnp.zeros_like(vp)))
        pieces.append(jnp.where(low, jnp.zeros_like(vp), vp))
    return jnp.concatenate(pieces, axis=0)


def _unstack_transposed(t, low):
    pairs = []
    for p in range(GROUP_WIDTH // LANES):
        even = t[:, BLOCK * (2 * p): BLOCK * (2 * p + 1)].T
        odd = t[:, BLOCK * (2 * p + 1): BLOCK * (2 * p + 2)].T
        pairs.append(jnp.where(low, even, odd))
    return pairs


STACKED = GQA_GROUP * BLOCK


def _band_bias():
    key = lax.broadcasted_iota(jnp.int32, (2, 2 * BLOCK, STACKED), 1)
    qry = lax.broadcasted_iota(jnp.int32, (2, 2 * BLOCK, STACKED), 2) % BLOCK
    first = lax.broadcasted_iota(jnp.int32, (2, 2 * BLOCK, STACKED), 0) == 0
    valid = (key > qry) & (key <= qry + BLOCK) & (jnp.logical_not(first) | (key >= BLOCK))
    return jnp.where(valid, 0.0, MASKED).astype(F32)


def _softmax_keys_on_sublanes(k2, q, bias, sink_ref, j):
    head_of_lane = lax.broadcasted_iota(jnp.int32, (1, STACKED), 1) // BLOCK
    sink = jnp.zeros((1, STACKED), F32)
    for h in range(GQA_GROUP):
        sink = jnp.where(head_of_lane == h, sink_ref[j * GQA_GROUP + h], sink)
    s = lax.dot_general(k2, q, _DIMS["nt"], preferred_element_type=F32) + bias
    m = jnp.maximum(jnp.max(s, axis=0, keepdims=True), sink)
    e = jnp.exp(s - m)
    e_sink = jnp.exp(sink - m)
    inv = 1.0 / (jnp.sum(e, axis=0, keepdims=True) + e_sink)
    return e * inv, e_sink * inv


def _attn_fwd(name, qn, kn, proj, sinks, bias, comm=None):
    T = qn.shape[0]
    nb = T // BLOCK
    plumb = _CommPlumbing(comm)

    def body(sink_ref, bias_ref, q_ref, kp_ref, kc_ref, vp_ref, vc_ref, *rest):
        c_in, o_ref = rest[:plumb.n_in], rest[plumb.n_in]
        c_out, c_scr = rest[plumb.n_in + 1: plumb.n_in + 1 + plumb.n_out], rest[plumb.n_in + 1 + plumb.n_out:]
        m = pl.program_id(0)
        plumb.handshake(m == 0)
        plumb.run(m, nb // 2, True, c_in, c_out, c_scr)
        low = lax.broadcasted_iota(jnp.int32, (1, LANES), 1) < HEAD_DIM
        k_pair, v_pair = kc_ref[...], vc_ref[...]
        for b in range(2):
            rows = slice(BLOCK * b, BLOCK * (b + 1))
            kk = k_pair if b else jnp.concatenate([kp_ref[...], k_pair[0:BLOCK]], axis=0)
            vv = v_pair if b else jnp.concatenate([vp_ref[...], v_pair[0:BLOCK]], axis=0)
            bias = bias_ref[1] if b else bias_ref[jnp.minimum(m, 1)]
            for j in range(2):
                q = _stack_heads(q_ref[rows, GROUP_WIDTH * j: GROUP_WIDTH * (j + 1)], low)
                p, _ = _softmax_keys_on_sublanes(_dup_head(kk, j), q, bias, sink_ref, j)
                o_t = lax.dot_general(_dup_head(vv, j), p.astype(BF), _DIMS["tn"], preferred_element_type=F32)
                for pair, o in enumerate(_unstack_transposed(o_t, low)):
                    lanes = slice(GROUP_WIDTH * j + LANES * pair, GROUP_WIDTH * j + LANES * (pair + 1))
                    o_ref[rows, lanes] = o.astype(BF)
        plumb.run(m, nb // 2, False, c_in, c_out, c_scr)

    wide = pl.BlockSpec((2 * BLOCK, ATTN_WIDTH), lambda m: (m, 0))
    before = lambda m: jnp.maximum(2 * m - 1, 0)
    res = pl.pallas_call(
        body, name=name, grid=(nb // 2,),
        in_specs=[pl.BlockSpec(memory_space=pltpu.SMEM),
                  pl.BlockSpec((2, 2 * BLOCK, STACKED), lambda m: (0, 0, 0)), wide,
                  pl.BlockSpec((BLOCK, LANES), lambda m: (before(m), 0)),
                  pl.BlockSpec((2 * BLOCK, LANES), lambda m: (m, 0)),
                  pl.BlockSpec((BLOCK, LANES), lambda m: (before(m), KV_COL_BLOCK_V)),
                  pl.BlockSpec((2 * BLOCK, LANES), lambda m: (m, KV_COL_BLOCK_V))] + [ANY] * plumb.n_in,
        out_specs=[wide] + [ANY] * plumb.n_out,
        out_shape=[jax.ShapeDtypeStruct((T, ATTN_WIDTH), BF)] + plumb.out_shapes, scratch_shapes=plumb.scratch,
        compiler_params=_params(("arbitrary",) if comm else ("parallel",), plumb.collective_id()),
    )(sinks, bias, qn, kn, kn, proj, proj, *plumb.args)
    return (res[0], plumb.split_outputs(res[1:])) if comm is not None else res[0]


def _attn_bwd(name, dout, qn, kn, proj, sinks, bias, comm):
    T = qn.shape[0]
    nb = T // BLOCK
    plumb = _CommPlumbing(comm)

    def body(sink_ref, bias_ref, do_ref, q_ref, kp_ref, kc_ref, vp_ref, vc_ref, *rest):
        c_in = rest[:plumb.n_in]
        dq_ref, k_own, k_before, v_own, v_before, dsink_ref = rest[plumb.n_in: plumb.n_in + 6]
        c_out, c_scr = rest[plumb.n_in + 6: plumb.n_in + 6 + plumb.n_out], rest[plumb.n_in + 6 + plumb.n_out:]
        m = pl.program_id(0)
        plumb.handshake(m == 0)
        plumb.run(m, nb // 2, True, c_in, c_out, c_scr)
        lane = lax.broadcasted_iota(jnp.int32, (1, LANES), 1)
        low = lane < HEAD_DIM

        @pl.when(m == 0)
        def _():
            dsink_ref[...] = jnp.zeros_like(dsink_ref)

        k_pair, v_pair = kc_ref[...], vc_ref[...]
        dsink = jnp.zeros((1, LANES), F32)
        for b in range(2):
            rows = slice(BLOCK * b, BLOCK * (b + 1))
            kk = k_pair if b else jnp.concatenate([kp_ref[...], k_pair[0:BLOCK]], axis=0)
            vv = v_pair if b else jnp.concatenate([vp_ref[...], v_pair[0:BLOCK]], axis=0)
            bias = bias_ref[1] if b else bias_ref[jnp.minimum(m, 1)]
            dk_tot = jnp.zeros((2 * BLOCK, LANES), F32)
            dv_tot = jnp.zeros((2 * BLOCK, LANES), F32)
            for j in range(2):
                k2 = _dup_head(kk, j)
                v2 = _dup_head(vv, j)
                q = _stack_heads(q_ref[rows, GROUP_WIDTH * j: GROUP_WIDTH * (j + 1)], low)
                do = _stack_heads(do_ref[rows, GROUP_WIDTH * j: GROUP_WIDTH * (j + 1)], low)
                p, psink = _softmax_keys_on_sublanes(k2, q, bias, sink_ref, j)
                dp =lax.dot_general(v2, do, _DIMS["nt"], preferred_element_type=F32)
                delta = jnp.sum(p * dp, axis=0, keepdims=True)
                ds = (p * (dp - delta)).astype(BF)
                dk2 = jnp.dot(ds, q, preferred_element_type=F32)
                dv2 = jnp.dot(p.astype(BF), do, preferred_element_type=F32)
                dq_t = lax.dot_general(k2, ds, _DIMS["tn"], preferred_element_type=F32)
                for pair, dq in enumerate(_unstack_transposed(dq_t, low)):
                    lanes = slice(GROUP_WIDTH * j + LANES * pair, GROUP_WIDTH * j + LANES * (pair + 1))
                    dq_ref[rows, lanes] = dq.astype(BF)
                mine = low if j == 0 else jnp.logical_not(low)
                dk_tot = dk_tot + jnp.where(mine, dk2 + pltpu.roll(dk2, HEAD_DIM, axis=1), 0.0)
                dv_tot = dv_tot + jnp.where(mine, dv2 + pltpu.roll(dv2, HEAD_DIM, axis=1), 0.0)
                sink_term = psink * delta
                for h in range(GQA_GROUP):
                    val = -jnp.sum(sink_term[:, BLOCK * h: BLOCK * (h + 1)], axis=1, keepdims=True)
                    dsink = dsink + jnp.where(lane == j * GQA_GROUP + h, val, 0.0)
            k_before[rows, :], k_own[rows, :] = dk_tot[0:BLOCK], dk_tot[BLOCK:]
            v_before[rows, :], v_own[rows, :] = dv_tot[0:BLOCK], dv_tot[BLOCK:]
        dsink_ref[0:1, :] += dsink
        plumb.run(m, nb // 2, False, c_in, c_out, c_scr)

    wide = pl.BlockSpec((2 * BLOCK, ATTN_WIDTH), lambda m: (m, 0))
    pair = pl.BlockSpec((2 * BLOCK, LANES), lambda m: (m, 0))
    before = lambda m: jnp.maximum(2 * m - 1, 0)
    res = pl.pallas_call(
        body, name=name, grid=(nb // 2,),
        in_specs=[pl.BlockSpec(memory_space=pltpu.SMEM),
                  pl.BlockSpec((2, 2 * BLOCK, STACKED), lambda m: (0, 0, 0)), wide, wide,
                  pl.BlockSpec((BLOCK, LANES), lambda m: (before(m), 0)), pair,
                  pl.BlockSpec((BLOCK, LANES), lambda m: (before(m), KV_COL_BLOCK_V)),
                  pl.BlockSpec((2 * BLOCK, LANES), lambda m: (m, KV_COL_BLOCK_V))] + [ANY] * plumb.n_in,
        out_specs=[wide, pair, pair, pair, pair, pl.BlockSpec((8, LANES), lambda m: (0, 0))] + [ANY] * plumb.n_out,
        out_shape=[jax.ShapeDtypeStruct((T, ATTN_WIDTH), BF)] + [jax.ShapeDtypeStruct((T, KV_WIDTH), F32)] * 4
        + [jax.ShapeDtypeStruct((8, LANES), F32)] + plumb.out_shapes,
        scratch_shapes=plumb.scratch, compiler_params=_params(("arbitrary",), plumb.collective_id()),
    )(sinks, bias, dout, qn, kn, kn, proj, proj, *plumb.args)
    return list(res[:6]), plumb.split_outputs(res[6:])


def _swiglu_fwd_epilogue(accs, ex):
    g, u = accs
    return [g, u, g * jax.nn.sigmoid(g) * u], []


def _swiglu_bwd_epilogue(accs, ex):
    (da,) = accs
    g, u = ex[0].astype(F32), ex[1].astype(F32)
    s = jax.nn.sigmoid(g)
    gs = g * s
    return [da * u * (s + gs - gs * s), da * gs], []


def _residual_norm_epilogue(scale):
    def epilogue(accs, ex):
        res, gain = ex
        h = res + scale * accs[0]
        r = lax.rsqrt(jnp.mean(h * h, axis=-1, keepdims=True) + RMS_EPS)
        return [h, h * r * gain], []
    return epilogue


def _rms_bwd_epilogue(accs, ex):
    (dn,) = accs
    xv, g, dres = ex
    r = lax.rsqrt(jnp.mean(xv * xv, axis=-1, keepdims=True) + RMS_EPS)
    xhat = xv * r
    dxhat = dn * g
    dx = dres + r * (dxhat - xhat * jnp.mean(dxhat * xhat, axis=-1, keepdims=True))
    return [dx, dx], [dn * xhat]


def _loss_epilogue(accs, ex):
    xv, target = ex
    d = xv + 0.5 * accs[0] - target
    dy = d * (1.0 / D_MODEL)
    return [dy, dy], [d * d]


def _merge_fwd_epilogue(accs, ex):
    (ba,) = accs
    bp, gp_pre, ga_pre, bias_p, bias_a = ex
    gp = jax.nn.sigmoid(gp_pre.astype(F32) + bias_p)
    ga = jax.nn.sigmoid(ga_pre.astype(F32) + bias_a)
    return [gp * bp.astype(F32) + ga * ba, ba], []


def _merge_bwd_epilogue(accs, ex):
    (dm,) = accs
    bp, ba, gp_pre, ga_pre, bias_p, bias_a = ex
    gp = jax.nn.sigmoid(gp_pre.astype(F32) + bias_p)
    ga = jax.nn.sigmoid(ga_pre.astype(F32) + bias_a)
    dbp, dba = dm * gp, dm * ga
    dgp = dbp * bp.astype(F32) * (1.0 - gp)
    dga = dba * ba.astype(F32) * (1.0 - ga)
    return [dbp, dba, dgp, dga], [dgp, dga]


def _prep(name, ws, transposes):
    n = len(ws)

    def body(*refs):
        for w_ref, o_ref, tr in zip(refs[:n], refs[n:], transposes):
            v = w_ref[...]
            o_ref[...] = (v.T if tr else v).astype(BF)

    shapes = [jax.ShapeDtypeStruct(w.shape[::-1] if tr else w.shape, BF) for w, tr in zip(ws, transposes)]
    return pl.pallas_call(body, name=name, out_shape=shapes, compiler_params=_params())(*ws)


def _adam_math(w, g, m, v):
    m = ADAM_B1 * m + (1.0 - ADAM_B1) * g
    v = ADAM_B2 * v + (1.0 - ADAM_B2) * jnp.square(g)
    m_hat = m / (1.0 - ADAM_B1 ** ADAM_STEP)
    v_hat = v / (1.0 - ADAM_B2 ** ADAM_STEP)
    delta = -ADAM_LR * (m_hat / (jnp.sqrt(v_hat) + ADAM_EPS) + ADAM_WD * w)
    return delta, m, v


def _adamw_sharded(name, items, transpose=False):
    n = len(items)

    def body(*refs):
        ins, outs = refs[:4 * n], refs[4 * n:]
        for k in range(n):
            s_ref, w_ref, m_ref, v_ref = ins[4 * k: 4 * k + 4]
            g = s_ref[0].astype(F32)
            for i in range(1, 4):
                g = g + s_ref[i].astype(F32)
            if transpose:
                g = g.T
            delta, mn, vn = _adam_math(w_ref[...], g, m_ref[...], v_ref[...])
            for o_ref, val in zip(outs[4 * k: 4 * k + 4], (g, delta, mn, vn)):
                o_ref[...] = val

    flat = [a for item in items for a in item]
    out_shape = [jax.ShapeDtypeStruct(item[1].shape, F32) for item in items for _ in range(4)]
    _, r, C = items[0][0].shape
    rows = r // 4
    if transpose or rows % 8:
        res = pl.pallas_call(body, name=name, out_shape=out_shape, compiler_params=_params())(*flat)
    else:
        tile = pl.BlockSpec((rows, C), lambda i: (i, 0))
        res = pl.pallas_call(
            body, name=name, grid=(4,), in_specs=[pl.BlockSpec((4, rows, C), lambda i: (0, i, 0)), tile, tile, tile] * n,
            out_specs=[tile] * (4 * n), out_shape=out_shape, compiler_params=_params(("parallel",)),
        )(*flat)
    return [tuple(res[4 * k: 4 * k + 4]) for k in range(n)]


SMALL_LAYOUT = (("ffn1_norm", 0, (8, LANES)), ("mix_norm", 8, (8, LANES)), ("ffn2_norm", 16, (8, LANES)),
                ("gate_bias", 24, (16, LANES)), ("pool_scale", 40, (4, LANES)), ("q_norm", 48, (1, HEAD_DIM)),
                ("k_norm", 56, (1, HEAD_DIM)), ("sinks", 64, (1, N_HEADS)))
LOSS_ROW = 72
SMALL_ROWS = 80


def _adamw_small(name, g_vec, g_pool_w, params):
    n = len(SMALL_LAYOUT) + 1

    def body(vec_ref, pw_ref, *refs):
        ins, outs = refs[:3 * n], refs[3 * n:]
        vec = vec_ref[0]
        pw = pw_ref[0]
        for i in range(1, N_DEV):
            vec = vec + vec_ref[i]
            pw = pw + pw_ref[i]
        grads = [vec[r0:r0 + shape[0], 0:shape[1]] for _, r0, shape in SMALL_LAYOUT] + [pw]
        for p, g in enumerate(grads):
            w_ref, m_ref, v_ref = ins[3 * p: 3 * p + 3]
            delta, mn, vn = _adam_math(w_ref[...], g, m_ref[...], v_ref[...])
            for o_ref, val in zip(outs[4 * p: 4 * p + 4], (g, delta, mn, vn)):
                o_ref[...] = val
        outs[4 * n][...] = vec[LOSS_ROW:LOSS_ROW + 1, :]

    flat = [a for wmv in params for a in wmv]
    out_shape = [jax.ShapeDtypeStruct(wmv[0].shape, F32) for wmv in params for _ in range(4)]
    out_shape.append(jax.ShapeDtypeStruct((1, LANES), F32))
    res = pl.pallas_call(body, name=name, out_shape=out_shape, compiler_params=_params())(g_vec, g_pool_w, *flat)
    return [tuple(res[4 * p: 4 * p + 4]) for p in range(n)], res[4 * n]


def _place():
    x, y, c = lax.axis_index("x"), lax.axis_index("y"), lax.axis_index("c")
    other_chips = [(1 - x, y), (x, 1 - y), (1 - x, 1 - y)]
    return x, y, c, other_chips


def _rows(ref, r, place, natural=False):
    px, py, pc = place
    b = 4 * px + 2 * py + pc if natural else 4 * pc + 2 * px + py
    return ref.at[pl.ds(pl.multiple_of(b * r, 8), r), :]


def _gather_task(shards, natural=(), forward_at=0.75):
    n = len(shards)
    rs = [s.shape[0] for s in shards]
    rows_of = lambda ref, k, place: _rows(ref, rs[k], place, k in natural)

    def copy(scr, outs, k, slot, block, to, src=None):
        rows = rows_of(outs[k], k, block)
        return pltpu.make_async_remote_copy(
            src_ref=rows if src is None else src, dst_ref=rows, send_sem=scr[0].at[7 * k + slot],
            recv_sem=scr[1].at[7 * k + slot], device_id=to, device_id_type=MESH)

    def first_sends(ins, outs, scr):
        x, y, c, chips = _place()
        me = (x, y, c)
        cps = [copy(scr, outs, k, 1 + j, me, (*chip, c), src=ins[k]) for j, chip in enumerate(chips) for k in range(n)]
        return cps + [copy(scr, outs, k, 0, me, (x, y, 1 - c), src=ins[k]) for k in range(n)]

    def passed_on(outs, scr):
        x, y, c, chips = _place()
        return [copy(scr, outs, k, 4 + j, (*chip, c), (x, y, 1 - c)) for j, chip in enumerate(chips) for k in range(n)]

    def local(ins, outs, scr):
        x, y, c, _ = _place()
        return [pltpu.make_async_copy(ins[k], rows_of(outs[k], k, (x, y, c)), scr[2].at[k]) for k in range(n)]

    def start(ins, outs, scr):
        for cp in local(ins, outs, scr) + first_sends(ins, outs, scr):
            cp.start()

    def forward(ins, outs, scr):
        x, y, c, chips = _place()
        for j, chip in enumerate(chips):
            for k in range(n):
                copy(scr, outs, k, 1 + j, (*chip, c), (x, y, c)).wait_recv()
                copy(scr, outs, k, 4 + j, (*chip, c), (x, y, 1 - c)).start()

    def finish(ins, outs, scr):
        x, y, c, chips = _place()
        for k in range(n):
            copy(scr, outs, k, 0, (x, y, 1 - c), (x, y, c)).wait_recv()
        for j, chip in enumerate(chips):
            for k in range(n):
                copy(scr, outs, k, 4 + j, (*chip, 1 - c), (x, y, c)).wait_recv()
        for cp in first_sends(ins, outs, scr) + passed_on(outs, scr):
            cp.wait_send()
        for cp in local(ins, outs, scr):
            cp.wait()

    out_shapes = [jax.ShapeDtypeStruct((N_DEV * s.shape[0], s.shape[1]), s.dtype) for s in shards]
    scratch = [pltpu.SemaphoreType.DMA((7 * n,)), pltpu.SemaphoreType.DMA((7 * n,)), pltpu.SemaphoreType.DMA((n,))]
    return _Task(shards, out_shapes, scratch, [(0, start), (forward_at, forward), (1.0, finish)], ("sibling", "chips"))


def _direct_gather_task(shards):
    n = len(shards)
    rs = [s.shape[0] for s in shards]

    def peers():
        x, y, c, _ = _place()
        flip = lambda v, bit: 1 - v if bit else v
        return (x, y, c), [(flip(x, (s >> 2) & 1), flip(y, (s >> 1) & 1), flip(c, s & 1)) for s in range(1, N_DEV)]

    def copies(ins, outs, scr):
        me, others = peers()
        local = [pltpu.make_async_copy(ins[k], _rows(outs[k], rs[k], me), scr[2].at[k]) for k in range(n)]
        sems = lambda k, s: dict(send_sem=scr[0].at[7 * k + s], recv_sem=scr[1].at[7 * k + s], device_id_type=MESH)
        sends = [pltpu.make_async_remote_copy(src_ref=ins[k], dst_ref=_rows(outs[k], rs[k], me), device_id=to, **sems(k, s))
                 for s, to in enumerate(others) for k in range(n)]
        recvs = [pltpu.make_async_remote_copy(src_ref=_rows(outs[k], rs[k], frm), dst_ref=_rows(outs[k], rs[k], frm),
                                              device_id=me, **sems(k, s))
                 for s, frm in enumerate(others) for k in range(n)]
        return local, sends, recvs

    def start(ins, outs, scr):
        local, sends, _ = copies(ins, outs, scr)
        for cp in local + sends:
            cp.start()

    def finish(ins, outs, scr):
        local, sends, recvs = copies(ins, outs, scr)
        for cp in recvs:
            cp.wait_recv()
        for cp in sends:
            cp.wait_send()
        for cp in local:
            cp.wait()

    out_shapes = [jax.ShapeDtypeStruct((N_DEV * s.shape[0], s.shape[1]), s.dtype) for s in shards]
    scratch = [pltpu.SemaphoreType.DMA((7 * n,)), pltpu.SemaphoreType.DMA((7 * n,)), pltpu.SemaphoreType.DMA((n,))]
    return _Task(shards, out_shapes, scratch, [(0, start), (1.0, finish)], ("all",))


def _chip_task(sums):
    n = len(sums)
    rs = [s.shape[0] // 4 for s in sums]

    def block(ref, k, chip_index):
        return ref.at[pl.ds(pl.multiple_of(chip_index * rs[k], 8), rs[k]), :]

    def copies(ins, outs, scr):
        send_sems, recv_sems, local_sems = scr
        x, y, c, chips = _place()
        here = 2 * x + y
        local = [pltpu.make_async_copy(block(ins[k], k, here), outs[k].at[here], local_sems.at[k]) for k in range(n)]
        remote = []
        for j, (px, py) in enumerate(chips):
            remote += [pltpu.make_async_remote_copy(
                src_ref=block(ins[k], k, 2 * px + py), dst_ref=outs[k].at[here],
                send_sem=send_sems.at[3 * k + j], recv_sem=recv_sems.at[3 * k + j],
                device_id=(px, py, c), device_id_type=MESH) for k in range(n)]
        return local, remote

    def start(ins, outs, scr):
        local, remote = copies(ins, outs, scr)
        for cp in local + remote:
            cp.start()

    def finish(ins, outs, scr):
        local, remote = copies(ins, outs, scr)
        for cp in remote:
            cp.wait()
        for cp in local:
            cp.wait()

    out_shapes = [jax.ShapeDtypeStruct((4, r, s.shape[1]), s.dtype) for r, s in zip(rs, sums)]
    scratch = [pltpu.SemaphoreType.DMA((3 * n,)), pltpu.SemaphoreType.DMA((3 * n,)), pltpu.SemaphoreType.DMA((n,))]
    return _Task(sums, out_shapes, scratch, [(0, start), (1.0, finish)], ("chips",))


def _dw_pair(name, a, b, scale, comm=None, blocks=1):
    T, M = a.shape
    N = b.shape[1]
    half = M // 2
    wide = half // blocks
    tk = min(2048, T)
    nK = T // tk
    plumb = _CommPlumbing(comm)

    def body(core_ref, *rest):
        a_refs, b_ref, rest = rest[:blocks], rest[blocks], rest[blocks + 1:]
        c_in = rest[:plumb.n_in]
        o_ref = rest[plumb.n_in]
        c_out = rest[plumb.n_in + 1: plumb.n_in + 1 + plumb.n_out]
        acc, stage, land, send_sem, recv_sem = rest[plumb.n_in + 1 + plumb.n_out: plumb.n_in + 6 + plumb.n_out]
        c_scr = rest[plumb.n_in + 6 + plumb.n_out:]
        i, k = pl.program_id(0), pl.program_id(1)
        x, y, c, _ = _place()
        push = pltpu.make_async_remote_copy(src_ref=stage, dst_ref=land, send_sem=send_sem, recv_sem=recv_sem,
                                            device_id=(x, y, 1 - c), device_id_type=MESH)
        plumb.handshake((i == 0) & (k == 0), own=("sibling",))
        if comm:
            plumb.run(i * nK + k, 2 * nK, True, c_in, c_out, c_scr)

        av = a_refs[0][...] if blocks == 1 else jnp.concatenate([r[...] for r in a_refs], axis=1)
        p = lax.dot_general(av, b_ref[...], _DIMS["tn"], preferred_element_type=F32)

        @pl.when(k == 0)
        def _():
            acc[...] = p

        @pl.when(k > 0)
        def _():
            acc[...] += p

        @pl.when((i == 0) & (k == nK - 1))
        def _():
            stage[...] = (scale * acc[...]).astype(BF)
            push.start()

        @pl.when((i == 1) & (k == nK - 1))
        def _():
            push.wait_recv()
            o_ref[...] = (scale * acc[...] + land[...].astype(F32)).astype(BF)
            push.wait_send()

        if comm:
            plumb.run(i * nK + k, 2 * nK, False, c_in, c_out, c_scr)

    grid_spec = pltpu.PrefetchScalarGridSpec(
        num_scalar_prefetch=1, grid=(2, nK),
        in_specs=[pl.BlockSpec((tk, wide), functools.partial(
            lambda i, k, core, j: (k, (2 * j if blocks > 1 else 0) + jnp.where(i == 0, 1 - core[0], core[0])), j=j))
            for j in range(blocks)] + [pl.BlockSpec((tk, N), lambda i, k, core: (k, 0))] + [ANY] * plumb.n_in,
        out_specs=[pl.BlockSpec((half, N), lambda i, k, core: (0, 0))] + [ANY] * plumb.n_out,
        scratch_shapes=[pltpu.VMEM((half, N), F32), pltpu.VMEM((half, N), BF), pltpu.VMEM((half, N), BF),
                        pltpu.SemaphoreType.DMA, pltpu.SemaphoreType.DMA] + plumb.scratch)
    core = lax.axis_index("c").astype(jnp.int32).reshape(1)
    res = pl.pallas_call(
        body, name=name, grid_spec=grid_spec,
        out_shape=[jax.ShapeDtypeStruct((half, N), BF)] + plumb.out_shapes,
        compiler_params=_params(("arbitrary", "arbitrary"), plumb.collective_id(own=("sibling",))),
    )(core, *([a] * blocks), b, *plumb.args)
    return (res[0], plumb.split_outputs(res[1:])) if comm else res[0]


def _pair_task(parts):
    n = len(parts)

    def copies(ins, outs, scr):
        x, y, c, _ = _place()
        return [pltpu.make_async_remote_copy(
            src_ref=ins[k].at[:, pl.ds(1 - c, 1)], dst_ref=outs[k], send_sem=scr[0].at[k], recv_sem=scr[1].at[k],
            device_id=(x, y, 1 - c), device_id_type=MESH) for k in range(n)]

    def start(ins, outs, scr):
        for cp in copies(ins, outs, scr):
            cp.start()

    def finish(ins, outs, scr):
        for cp in copies(ins, outs, scr):
            cp.wait()

    out_shapes = [jax.ShapeDtypeStruct((4, 1) + p.shape[2:], p.dtype) for p in parts]
    scratch = [pltpu.SemaphoreType.DMA((n,)), pltpu.SemaphoreType.DMA((n,))]
    return _Task(parts, out_shapes, scratch, [(0, start), (1.0, finish)], ("sibling",))


def _pair_sum(name, part, got, core):
    _, _, r, C = part.shape

    def body(core_ref, p_ref, g_ref, o_ref):
        o_ref[0] = (p_ref[0, 0].astype(F32) + g_ref[0, 0].astype(F32)).astype(o_ref.dtype)

    return pl.pallas_call(
        body, name=name,
        grid_spec=pltpu.PrefetchScalarGridSpec(
            num_scalar_prefetch=1, grid=(4,),
            in_specs=[pl.BlockSpec((1, 1, r, C), lambda i, core_ref: (i, core_ref[0], 0, 0)),
                      pl.BlockSpec((1, 1, r, C), lambda i, core_ref: (i, 0, 0, 0))],
            out_specs=pl.BlockSpec((1, r, C), lambda i, core_ref: (i, 0, 0))),
        out_shape=jax.ShapeDtypeStruct((4, r, C), part.dtype), compiler_params=_params(("parallel",)),
    )(core, part, got)


def _ffn_bwd(tag, dy, dyb, x, gain, wgT, wuT, wd, saved, earlier=None):
    n, g, u, a = saved
    half = lambda accs, ex: _swiglu_bwd_epilogue([0.5 * accs[0]], ex)
    act_args = dict(tm=1024, tn=1408, tk=D_MODEL, epilogue=half, extras=[(g, "tile", 0), (u, "tile", 0)], cols_outer=True)
    if earlier is None:
        sum_d = _dw_pair(tag + "_dw_down", a, dyb, 0.5)
        (dg, du), ((slots_d,),) = _mm(tag + "_d_act", [(dyb, wd, "nt", 0)], [BF, BF], comm=[_chip_task([sum_d])], **act_args)
        slots_e = None
        sum_g = _dw_pair(tag + "_dw_gate", dg, n, 1.0)
    else:
        sum_d, ((got,),) = _dw_pair(tag + "_dw_down", a, dyb, 0.5, comm=[_pair_task([earlier])])
        core = lax.axis_index("c").astype(jnp.int32).reshape(1)
        sum_e = _pair_sum(tag + "_pair_sum_earlier", earlier, got, core)
        sum_e = sum_e.reshape(4 * sum_e.shape[1], sum_e.shape[2])
        (dg, du), ((slots_e,),) = _mm(tag + "_d_act", [(dyb, wd, "nt", 0)], [BF, BF], comm=[_chip_task([sum_e])], **act_args)
        sum_g, ((slots_d,),) = _dw_pair(tag + "_dw_gate", dg, n, 1.0, comm=[_chip_task([sum_d])])
    norm_args = dict(tm=512, tn=D_MODEL, tk=D_FF, epilogue=_rms_bwd_epilogue, n_colsum=1,
                     extras=[(x, "tile", 0), (gain, "row", 0), (dy, "tile", 0)])
    norm_terms = [(dg, wgT, "nn", 0), (du, wuT, "nn", 0)]
    if earlier is None:
        up = _dw_pair(tag + "_dw_up", du, n, 1.0)
        (dx, dxb, dgain), ((slots_g,),) = _mm(tag + "_d_norm", norm_terms, [F32, BF], comm=[_chip_task([sum_g])], **norm_args)
    else:
        sum_u, ((slots_g,),) = _dw_pair(tag + "_dw_up", du, n, 1.0, comm=[_chip_task([sum_g])])
        (dx, dxb, dgain), ((up,),) = _mm(tag + "_d_norm", norm_terms, [F32, BF], comm=[_chip_task([sum_u])], **norm_args)
    return dx, dxb, dgain, slots_e, slots_g, up, slots_d


def _tile_gain(g):
    return jnp.concatenate([g, g]).reshape(1, LANES)


def _fold_heads(partials):
    return jnp.sum(partials.reshape(-1, HEAD_DIM), axis=0)


def _pack_small_grads(grads, loss_local):
    pieces, row = [], 0
    for name, r0, _ in SMALL_LAYOUT + (("loss", LOSS_ROW, None),):
        v = (loss_local if name == "loss" else grads[name]).reshape(-1)
        rows = -(-v.size // LANES)
        block = jnp.pad(v, (0, rows * LANES - v.size)).reshape(rows, LANES)
        pieces += [jnp.zeros((r0 - row, LANES), F32)] * (r0 > row) + [block]
        row = r0 + rows
    pieces.append(jnp.zeros((SMALL_ROWS - row, LANES), F32))
    return jnp.concatenate(pieces, axis=0)


def kernel(x, ffn1_norm, ffn1_w_gate, ffn1_w_up, ffn1_w_down, mix_norm, w_in, pool_w, pool_scale, w_pool_out, q_norm, k_norm, sinks, w_attn_out, gate_bias, w_out, ffn2_norm, ffn2_w_gate, ffn2_w_up, ffn2_w_down, loss_target, m_ffn1_norm, m_ffn1_w_gate, m_ffn1_w_up, m_ffn1_w_down, m_mix_norm, m_w_in, m_pool_w, m_pool_scale, m_w_pool_out, m_q_norm, m_k_norm, m_sinks, m_w_attn_out, m_gate_bias, m_w_out, m_ffn2_norm, m_ffn2_w_gate, m_ffn2_w_up, m_ffn2_w_down, v_ffn1_norm, v_ffn1_w_gate, v_ffn1_w_up, v_ffn1_w_down, v_mix_norm, v_w_in, v_pool_w, v_pool_scale, v_w_pool_out, v_q_norm, v_k_norm, v_sinks, v_w_attn_out, v_gate_bias, v_w_out, v_ffn2_norm, v_ffn2_w_gate, v_ffn2_w_up, v_ffn2_w_down):
    T = x.shape[1]
    x2 = x.reshape(T, D_MODEL)
    target = loss_target.reshape(T, D_MODEL)

    big = [
        ("ffn1_w_gate", ffn1_w_gate, m_ffn1_w_gate, v_ffn1_w_gate, True, False),
        ("ffn1_w_up", ffn1_w_up, m_ffn1_w_up, v_ffn1_w_up, True, False),
        ("ffn1_w_down", ffn1_w_down, m_ffn1_w_down, v_ffn1_w_down, False, False),
        ("w_in", w_in, m_w_in, v_w_in, True, False),
        ("w_pool_out", w_pool_out, m_w_pool_out, v_w_pool_out, False, True),
        ("w_attn_out", w_attn_out, m_w_attn_out, v_w_attn_out, False, False),
        ("w_out", w_out, m_w_out, v_w_out, False, False),
        ("ffn2_w_gate", ffn2_w_gate, m_ffn2_w_gate, v_ffn2_w_gate, True, False),
        ("ffn2_w_up", ffn2_w_up, m_ffn2_w_up, v_ffn2_w_up, True, False),
        ("ffn2_w_down", ffn2_w_down, m_ffn2_w_down, v_ffn2_w_down, False, False),
    ]
    view = lambda a, tv: a.T if tv else a
    views = [view(w, tv) for _, w, _, _, tv, _ in big]
    in_kernel_t = [tk_ for *_, tk_ in big]
    first_shards = _prep("prep_ffn1_gate_up", views[0:2], in_kernel_t[0:2])
    g1 = ffn1_norm.reshape(1, D_MODEL)
    g2 = mix_norm.reshape(1, D_MODEL)
    g3 = ffn2_norm.reshape(1, D_MODEL)
    bias_row = gate_bias.reshape(1, 2 * D_MODEL)
    qg, kg = _tile_gain(q_norm) * ATTN_SCALE, _tile_gain(k_norm)
    scale_row = pool_scale.reshape(1, POOL_WIDTH)
    band_bias = _band_bias()

    n1, later_shards, ((wg1T, wu1T),) = _rms_fwd(
        "ffn1_norm", x2, g1, [_gather_task(first_shards, forward_at=0.9)], views[2:], in_kernel_t[2:])
    shards = list(first_shards) + later_shards
    (gt1, up1, act1), ((wd1,), (w_inT,)) = _mm(
        "ffn1_gate_up", [(n1, wg1T, "nt", 0), (n1, wu1T, "nt", 1)], [BF, BF, BF], tm=1024, tn=1408, tk=D_MODEL,
        epilogue=_swiglu_fwd_epilogue, cols_outer=True,
        comm=[_gather_task(shards[2:3], forward_at=0.5), _gather_task(shards[3:4], natural=(0,), forward_at=0.9)])
    (h1, u), ((w_poT, w_ao, w_o),) = _mm(
        "ffn1_down", [(act1, wd1, "nn", 0)], [F32, BF], tm=512, tn=D_MODEL, tk=D_FF,
        epilogue=_residual_norm_epilogue(0.5), extras=[(x2, "tile", 0), (g2, "row", 0)],
        comm=[_gather_task(shards[4:7], natural=(0, 1, 2), forward_at=0.8)])
    saved1 = (n1, gt1, up1, act1)
    (proj,), ((wg2T,),) = _mm(
        "in_proj", [(u, w_inT, "nt", 0)], [BF], tm=1024, tn=1280, tk=D_MODEL, cols_outer=True,
        comm=[_gather_task(shards[7:8], forward_at=0.8)])
    pooled, mixed = _pool_fwd("pool_fwd", proj, pool_w, scale_row)
    qn = _headnorm_fwd("q_norm", proj, COL_Q, ATTN_WIDTH, qg)
    kn = _headnorm_fwd("k_norm", proj, COL_K, KV_WIDTH, kg)
    attn, ((wu2T,),) = _attn_fwd("attn_fwd", qn, kn, proj, sinks, band_bias, comm=[_gather_task(shards[8:9], forward_at=0.8)])
    (bp,) = _mm("pool_out", [(mixed, w_poT, "nt", 0)], [BF], tm=1024, tn=D_MODEL, tk=POOL_WIDTH)
    gate_tn = 256
    gate_extras = [(proj, "tile", COL_GP // gate_tn), (proj, "tile", COL_GA // gate_tn),
                   (bias_row, "row", 0), (bias_row, "row", D_MODEL // gate_tn)]
    merged, ba = _mm("attn_out_merge", [(attn, w_ao, "nn", 0)], [BF, BF], tm=2048, tn=gate_tn, tk=ATTN_WIDTH,
                     epilogue=_merge_fwd_epilogue, extras=[(bp, "tile", 0)] + gate_extras)
    h2, n2 = _mm("mix_out", [(merged, w_o, "nn", 0)], [F32, BF], tm=1024, tn=D_MODEL, tk=D_MODEL,
                 epilogue=_residual_norm_epilogue(1.0), extras=[(h1, "tile", 0), (g3, "row", 0)])
    (gt2, up2, act2), ((wd2,),) = _mm(
        "ffn2_gate_up", [(n2, wg2T, "nt", 0), (n2, wu2T, "nt", 1)], [BF, BF, BF], tm=1024, tn=1408, tk=D_MODEL,
        epilogue=_swiglu_fwd_epilogue, cols_outer=True, comm=[_gather_task(shards[9:10], forward_at=0.8)])
    dy, dyb, sq = _mm("ffn2_down_loss", [(act2, wd2, "nn", 0)], [F32, BF], tm=512, tn=D_MODEL, tk=D_FF,
                      epilogue=_loss_epilogue, extras=[(h2, "tile", 0), (target, "tile", 0)], n_colsum=1)
    loss_local = 0.5 * jnp.sum(sq) / D_MODEL

    dh2, dh2b, dg3, _, slots_g2, sum_u2, slots_d2 = _ffn_bwd(
        "ffn2", dy, dyb, h2, g3, wg2T, wu2T, wd2, (n2, gt2, up2, act2))
    (dbp, dba, dproj, dga, cs_gp, cs_ga), ((slots_u2,),) = _mm(
        "mix_out_bwd", [(dh2b, w_o, "nt", 0)], [BF, BF, BF, BF], tm=2048, tn=gate_tn, tk=D_MODEL,
        epilogue=_merge_bwd_epilogue, extras=[(bp, "tile", 0), (ba, "tile", 0)] + gate_extras, n_colsum=2,
        out_placement={2: (IN_WIDTH, COL_GP)}, comm=[_chip_task([sum_u2])])
    sum_o = _dw_pair("dw_out", merged, dh2b, 1.0, blocks=4)
    (dmixed,) = _mm("pool_out_bwd", [(dbp, w_poT, "nn", 0)], [BF], tm=1024, tn=POOL_WIDTH, tk=D_MODEL)
    sum_po = _dw_pair("dw_pool_out", dbp, mixed, 1.0, blocks=4)
    (dattn,) = _mm("attn_out_bwd", [(dba, w_ao, "nt", 0)], [BF], tm=1024, tn=ATTN_WIDTH, tk=D_MODEL)
    sum_ao = _dw_pair("dw_attn_out", attn, dba, 1.0, blocks=4)
    (dqn, k_own, k_before, v_own, v_before, dsink_tile), ((slots_o, slots_po, slots_ao),) = _attn_bwd(
        "attn_bwd", dattn, qn, kn, proj, sinks, band_bias, [_chip_task([sum_o, sum_po, sum_ao])])
    next_block = lambda a: jnp.concatenate([a[BLOCK:], jnp.zeros((BLOCK, KV_WIDTH), F32)], axis=0)
    dkn = (k_own + next_block(k_before)).astype(BF)
    dv = (v_own + next_block(v_before)).astype(BF)
    dproj, dqg = _headnorm_bwd("q_norm_bwd", dqn, proj, COL_Q, ATTN_WIDTH, qg, dproj)
    dproj, dkg = _headnorm_bwd("k_norm_bwd", dkn, proj, COL_K, KV_WIDTH, kg, dproj)
    dproj, dpool_w, dpool_scale = _pool_bwd("pool_bwd", dmixed, pooled, pool_w, scale_row, dproj)
    for piece, col in ((dv, COL_V), (dga, COL_GA)):
        dproj = lax.dynamic_update_slice(dproj, piece, (0, col))
    (dh1, dh1b, dg2), ((g_pool_w,),) = _mm(
        "in_proj_bwd", [(dproj, w_inT, "nn", 0)], [F32, BF], tm=512, tn=D_MODEL, tk=IN_WIDTH, epilogue=_rms_bwd_epilogue,
        extras=[(h1, "tile", 0), (g2, "row", 0), (dh2, "tile", 0)], n_colsum=1,
        comm=[_gather_task([dpool_w.reshape(-1, LANES)])])
    (dw_inT,) = _mm("dw_in", [(dproj, u, "tn", 0)], [BF], tm=1920, tn=D_MODEL, tk=2048)
    dx, _, dg1, slots_in, slots_g1, slots_u1, slots_d1 = _ffn_bwd(
        "ffn1", dh1, dh1b, x2, g1, wg1T, wu1T, wd1, saved1, dw_inT.reshape(4, 2, IN_WIDTH // N_DEV, D_MODEL))

    slots = [slots_g1, slots_u1, slots_d1, slots_in, slots_po, slots_ao, slots_o, slots_g2, slots_u2, slots_d2]
    big_out = {}
    for label, group in (("ffn", (0, 1, 2, 7, 8, 9)), ("w_in", (3,)), ("w_pool_out", (4,)), ("attn_out_and_out", (5, 6))):
        items = [(slots[k], view(big[k][1], big[k][4]), view(big[k][2], big[k][4]), view(big[k][3], big[k][4]))
                 for k in group]
        for k, res in zip(group, _adamw_sharded("adamw_" + label, items, transpose=big[group[0]][5])):
            big_out[big[k][0]] = tuple(view(r, big[k][4]) for r in res)

    small_grads = {
        "ffn1_norm": jnp.sum(dg1, axis=(0, 1)), "mix_norm": jnp.sum(dg2, axis=(0, 1)), "ffn2_norm": jnp.sum(dg3, axis=(0, 1)),
        "gate_bias": jnp.concatenate([jnp.sum(cs_gp, axis=(0, 1)), jnp.sum(cs_ga, axis=(0, 1))]),
        "pool_scale": dpool_scale, "q_norm": _fold_heads(dqg) * ATTN_SCALE, "k_norm": _fold_heads(dkg),
        "sinks": dsink_tile[0, :N_HEADS]}
    ((g_vec,),) = _comm_only("gather_small_grads", [_direct_gather_task([_pack_small_grads(small_grads, loss_local)])])
    given = {"ffn1_norm": (ffn1_norm, m_ffn1_norm, v_ffn1_norm), "mix_norm": (mix_norm, m_mix_norm, v_mix_norm),
             "ffn2_norm": (ffn2_norm, m_ffn2_norm, v_ffn2_norm), "gate_bias": (gate_bias, m_gate_bias, v_gate_bias),
             "pool_scale": (pool_scale, m_pool_scale, v_pool_scale), "q_norm": (q_norm, m_q_norm, v_q_norm),
             "k_norm": (k_norm, m_k_norm, v_k_norm), "sinks": (sinks, m_sinks, v_sinks)}
    params = [tuple(a.reshape(shape) for a in given[nm]) for nm, _, shape in SMALL_LAYOUT]
    params.append(tuple(a.reshape(-1, LANES) for a in (pool_w, m_pool_w, v_pool_w)))
    small_res, loss_row = _adamw_small("adamw_small", g_vec.reshape(N_DEV, SMALL_ROWS, LANES),
                                       g_pool_w.reshape(N_DEV, -1, LANES), params)
    small_out = {nm: tuple(r.reshape(given[nm][0].shape) for r in res)
                 for (nm, _, _), res in zip(SMALL_LAYOUT, small_res)}
    small_out["pool_w"] = tuple(r.reshape(pool_w.shape) for r in small_res[-1])
    loss = loss_row[0, 0]

    order = ["ffn1_norm", "ffn1_w_gate", "ffn1_w_up", "ffn1_w_down", "mix_norm", "w_in", "pool_w", "pool_scale",
             "w_pool_out", "q_norm", "k_norm", "sinks", "w_attn_out", "gate_bias", "w_out", "ffn2_norm",
             "ffn2_w_gate", "ffn2_w_up", "ffn2_w_down"]
    every = {**big_out, **small_out}
    outs = [loss, dx.reshape(x.shape)]
    for j in range(4):
        outs += [every[nm][j] for nm in order]
    return tuple(outs)
```

```python
import functools

import jax
import jax.numpy as jnp
from jax import lax
from jax.experimental import pallas as pl
from jax.experimental.pallas import tpu as pltpu

BF = jnp.bfloat16
F32 = jnp.float32

D_MODEL = 1024
D_FF = 2816
POOL_WIDTH = 512
POOL_GROUP = 128
N_POOL_GROUPS = 4
HEAD_DIM = 64
N_HEADS = 16
GQA_GROUP = 8
BLOCK = 128
ATTN_WIDTH = 1024
KV_WIDTH = 128
IN_WIDTH = 3840
RMS_EPS = 1e-6
N_DEV = 8
LANES = 128

COL_Q = POOL_WIDTH
COL_K = COL_Q + ATTN_WIDTH
COL_V = COL_K + KV_WIDTH
COL_GP = COL_V + KV_WIDTH
COL_GA = COL_GP + D_MODEL

ADAM_LR = 0.001
ADAM_B1 = 0.9
ADAM_B2 = 0.999
ADAM_EPS = 1e-08
ADAM_WD = 0.01
ADAM_STEP = 10

VMEM_LIMIT_V7X = 56 * 1024 * 1024
MESH = pl.DeviceIdType.MESH
ANY = pl.BlockSpec(memory_space=pl.ANY)


def _params(sem=None, collective_id=None):
    return pltpu.CompilerParams(dimension_semantics=sem, vmem_limit_bytes=VMEM_LIMIT_V7X, collective_id=collective_id)


COLLECTIVE_IDS = {frozenset(["sibling"]): 0, frozenset(["chips"]): 1, frozenset(["sibling", "chips"]): 2}


def _handshake(peer_kinds):
    x, y, c, chips = _place()
    peers = ([(x, y, 1 - c)] if "sibling" in peer_kinds else []) + ([(*chip, c) for chip in chips] if "chips" in peer_kinds else [])
    barrier = pltpu.get_barrier_semaphore()
    for peer in peers:
        pl.semaphore_signal(barrier, inc=1, device_id=peer, device_id_type=MESH)
    pl.semaphore_wait(barrier, len(peers))


_DIMS = {"nt": (((1,), (1,)), ((), ())), "nn": (((1,), (0,)), ((), ())), "tn": (((0,), (0,)), ((), ()))}


class _Task:
    def __init__(self, inputs, out_shapes, scratch, phases, peers):
        self.inputs, self.out_shapes, self.scratch = list(inputs), list(out_shapes), list(scratch)
        self.phases = list(phases)
        self.peers = frozenset(peers)


class _CommPlumbing:
    def __init__(self, tasks):
        self.tasks = list(tasks or [])
        self.args = [a for t in self.tasks for a in t.inputs]
        self.out_shapes = [o for t in self.tasks for o in t.out_shapes]
        self.scratch = [s for t in self.tasks for s in t.scratch]
        self.n_in, self.n_out = len(self.args), len(self.out_shapes)

    def peer_kinds(self, own=()):
        kinds = frozenset(own).union(*[t.peers for t in self.tasks])
        return None if "all" in kinds or not kinds else kinds

    def collective_id(self, own=()):
        kinds = self.peer_kinds(own)
        return None if kinds is None else COLLECTIVE_IDS[kinds]

    def handshake(self, first, own=()):
        kinds = self.peer_kinds(own)
        if kinds is not None:
            pl.when(first)(functools.partial(_handshake, kinds))

    def _slices(self, c_in, c_out, c_scr):
        i = o = s = 0
        for t in self.tasks:
            yield t, c_in[i:i + len(t.inputs)], c_out[o:o + len(t.out_shapes)], c_scr[s:s + len(t.scratch)]
            i, o, s = i + len(t.inputs), o + len(t.out_shapes), s + len(t.scratch)

    def run(self, step, steps, before, c_in, c_out, c_scr):
        for t, ins, outs, scr in self._slices(c_in, c_out, c_scr):
            for frac, fn in t.phases:
                if step is None:
                    fn(ins, outs, scr)
                elif before == (frac == 0):
                    at = 0 if frac == 0 else max(0, min(steps, -(-int(round(frac * steps * 64)) // 64)) - 1)
                    pl.when(step == at)(functools.partial(fn, ins, outs, scr))

    def split_outputs(self, flat):
        res, o = [], 0
        for t in self.tasks:
            res.append(list(flat[o:o + len(t.out_shapes)]))
            o += len(t.out_shapes)
        return res


def _comm_only(name, tasks):
    plumb = _CommPlumbing(tasks)

    def body(*refs):
        c_in, c_out = refs[:plumb.n_in], refs[plumb.n_in: plumb.n_in + plumb.n_out]
        c_scr = refs[plumb.n_in + plumb.n_out:]
        plumb.run(None, 1, True, c_in, c_out, c_scr)

    res = pl.pallas_call(
        body, name=name, in_specs=[ANY] * plumb.n_in, out_specs=[ANY] * plumb.n_out, out_shape=plumb.out_shapes,
        scratch_shapes=plumb.scratch, compiler_params=pltpu.CompilerParams(has_side_effects=True),
    )(*plumb.args)
    return plumb.split_outputs(res)


def _mm(name, terms, out_dtypes, *, tm, tn, tk, epilogue=None, extras=(), n_colsum=0, comm=None, cols_outer=False,
        out_placement=None):
    a0, b0, mode0, _ = terms[0]
    if mode0 == "nt":
        (M, K), N = a0.shape, b0.shape[0]
    elif mode0 == "nn":
        (M, K), N = a0.shape, b0.shape[1]
    else:
        (K, M), N = a0.shape, b0.shape[1]
    tm, tn, tk = min(tm, M), min(tn, N), min(tk, K)
    assert M % tm == 0 and N % tn == 0 and K % tk == 0, (name, M, N, K, tm, tn, tk)
    nI, nJ, nK = M // tm, N // tn, K // tk
    n_terms = len(terms)
    n_acc = max(t[3] for t in terms) + 1
    n_ex = len(extras)
    n_out = len(out_dtypes)
    if epilogue is None:
        epilogue = lambda accs, ex: ([accs[0]], [])
    plumb = _CommPlumbing(comm)
    n_scr = n_acc if nK > 1 else 0
    grid = (nJ, nI, nK) if cols_outer else (nI, nJ, nK)

    def body(*refs):
        n_in = 2 * n_terms + n_ex
        ab = refs[: 2 * n_terms]
        ex_refs = refs[2 * n_terms: n_in]
        c_in = refs[n_in: n_in + plumb.n_in]
        o0 = n_in + plumb.n_in
        out_refs = refs[o0: o0 + n_out]
        cs_refs = refs[o0 + n_out: o0 + n_out + n_colsum]
        c_out = refs[o0 + n_out + n_colsum: o0 + n_out + n_colsum + plumb.n_out]
        s0 = o0 + n_out + n_colsum + plumb.n_out
        acc_refs = refs[s0: s0 + n_scr]
        c_scr = refs[s0 + n_scr:]
        steps = grid[0] * grid[1] * nK
        if comm:
            step = (pl.program_id(0) * grid[1] + pl.program_id(1)) * nK + pl.program_id(2)
            plumb.handshake(step == 0)
            plumb.run(step, steps, True, c_in, c_out, c_scr)

        def products():
            accs = [None] * n_acc
            for t, (_, _, mode, ai) in enumerate(terms):
                p = lax.dot_general(ab[2 * t][...], ab[2 * t + 1][...], _DIMS[mode], preferred_element_type=F32)
                accs[ai] = p if accs[ai] is None else accs[ai] + p
            return accs

        def finish(accs):
            outs, colsums = epilogue(accs, [r[...] for r in ex_refs])
            for r, o in zip(out_refs, outs):
                r[...] = o.astype(r.dtype)
            for r, cs in zip(cs_refs, colsums):
                r[...] = jnp.sum(cs, axis=0, keepdims=True).reshape(r.shape)

        if nK == 1:
            finish(products())
        else:
            k = pl.program_id(2)
            accs = products()

            @pl.when(k == 0)
            def _():
                for r, a in zip(acc_refs, accs):
                    r[...] = a

            @pl.when(k > 0)
            def _():
                for r, a in zip(acc_refs, accs):
                    r[...] += a

            @pl.when(k == nK - 1)
            def _():
                finish([r[...] for r in acc_refs])

        if comm:
            plumb.run(step, steps, False, c_in, c_out, c_scr)

    def spec(block, index, fixed=False):
        imap = (lambda q, p, k: index(p, q, k)) if cols_outer else index
        return pl.BlockSpec(block, imap, pipeline_mode=pl.Buffered(1)) if fixed else pl.BlockSpec(block, imap)

    in_specs, args = [], []
    for a, b, mode, _ in terms:
        if mode == "nt":
            in_specs += [spec((tm, tk), lambda i, j, k: (i, k), nI * nK == 1),
                         spec((tn, tk), lambda i, j, k: (j, k), nJ * nK == 1)]
        elif mode == "nn":
            in_specs += [spec((tm, tk), lambda i, j, k: (i, k), nI * nK == 1),
                         spec((tk, tn), lambda i, j, k: (k, j), nJ * nK == 1)]
        else:
            in_specs += [spec((tk, tm), lambda i, j, k: (k, i), nI * nK == 1),
                         spec((tk, tn), lambda i, j, k: (k, j), nJ * nK == 1)]
        args += [a, b]
    for arr, kind, off in extras:
        if kind == "tile":
            in_specs.append(spec((tm, tn), functools.partial(lambda i, j, k, off: (i, j + off), off=off)))
        else:
            in_specs.append(spec((1, tn), functools.partial(lambda i, j, k, off: (0, j + off), off=off)))
        args.append(arr)
    placed = dict(out_placement or {})
    out_shape = [jax.ShapeDtypeStruct((M, placed.get(o, (N, 0))[0]), dt) for o, dt in enumerate(out_dtypes)]
    out_specs = [spec((tm, tn), functools.partial(lambda i, j, k, off: (i, j + off), off=placed.get(o, (N, 0))[1] // tn))
                 for o in range(n_out)]
    out_shape += [jax.ShapeDtypeStruct((nI, 1, N), F32) for _ in range(n_colsum)]
    out_specs += [spec((1, 1, tn), lambda i, j, k: (i, 0, j)) for _ in range(n_colsum)]
    scratch = [pltpu.VMEM((tm, tn), F32) for _ in range(n_scr)]
    args += plumb.args
    in_specs += [ANY] * plumb.n_in
    out_shape += plumb.out_shapes
    out_specs += [ANY] * plumb.n_out
    sem = ("arbitrary",) * 3 if comm else ("parallel", "parallel", "arbitrary")
    res = pl.pallas_call(
        body, name=name, grid=grid, in_specs=in_specs, out_specs=out_specs, out_shape=out_shape,
        scratch_shapes=scratch + plumb.scratch, compiler_params=_params(sem, plumb.collective_id()),
    )(*args)
    n_own = n_out + n_colsum
    return (list(res[:n_own]), plumb.split_outputs(res[n_own:])) if comm is not None else res


ROW_TILE = 512


def _rms_fwd(name, x, g, comm, weights, transposes):
    T, D = x.shape
    steps = T // ROW_TILE
    plumb = _CommPlumbing(comm)
    nw = len(weights)

    def body(x_ref, g_ref, *rest):
        w_refs, c_in = rest[:nw], rest[nw: nw + plumb.n_in]
        o_ref, shard_refs = rest[nw + plumb.n_in], rest[nw + plumb.n_in + 1: 2 * nw + plumb.n_in + 1]
        c_out = rest[2 * nw + plumb.n_in + 1: 2 * nw + plumb.n_in + 1 + plumb.n_out]
        c_scr = rest[2 * nw + plumb.n_in + 1 + plumb.n_out:]
        plumb.handshake(pl.program_id(0) == 0)
        plumb.run(pl.program_id(0), steps, True, c_in, c_out, c_scr)

        @pl.when(pl.program_id(0) == 0)
        def _():
            for w_ref, s_ref, tr in zip(w_refs, shard_refs, transposes):
                v = w_ref[...]
                s_ref[...] = (v.T if tr else v).astype(BF)

        xv = x_ref[...]
        r = lax.rsqrt(jnp.mean(xv * xv, axis=-1, keepdims=True) + RMS_EPS)
        o_ref[...] = (xv * r * g_ref[...]).astype(BF)
        plumb.run(pl.program_id(0), steps, False, c_in, c_out, c_scr)

    row = pl.BlockSpec((ROW_TILE, D), lambda i: (i, 0))
    whole = lambda shape: pl.BlockSpec(shape, lambda i: (0, 0), pipeline_mode=pl.Buffered(1))
    shard_shapes = [w.shape[::-1] if tr else w.shape for w, tr in zip(weights, transposes)]
    res = pl.pallas_call(
        body, name=name, grid=(steps,),
        in_specs=[row, pl.BlockSpec((1, D), lambda i: (0, 0))] + [whole(w.shape) for w in weights] + [ANY] * plumb.n_in,
        out_specs=[row] + [whole(s) for s in shard_shapes] + [ANY] * plumb.n_out,
        out_shape=[jax.ShapeDtypeStruct((T, D), BF)] + [jax.ShapeDtypeStruct(s, BF) for s in shard_shapes] + plumb.out_shapes,
        scratch_shapes=plumb.scratch, compiler_params=_params(("arbitrary",), plumb.collective_id()),
    )(x, g, *weights, *plumb.args)
    return res[0], list(res[1: nw + 1]), plumb.split_outputs(res[nw + 1:])


HEADNORM_TILE = 2048


def _half_sum_matrix():
    r = lax.broadcasted_iota(jnp.int32, (LANES, LANES), 0) // HEAD_DIM
    c = lax.broadcasted_iota(jnp.int32, (LANES, LANES), 1) // HEAD_DIM
    return (r == c).astype(BF)


def _head_mean(v, ones_blockdiag):
    hi = v.astype(BF)
    lo = (v - hi.astype(F32)).astype(BF)
    s = jnp.dot(hi, ones_blockdiag, preferred_element_type=F32) + jnp.dot(lo, ones_blockdiag, preferred_element_type=F32)
    return s * (1.0 / HEAD_DIM)


def _headnorm_bwd(name, dy, proj, col0, width, g2, into):
    T = proj.shape[0]
    wide = min(width, GROUP_WIDTH)
    nb, off = width // wide, col0 // wide

    def body(dy_ref, x_ref, g_ref, b_ref, into_ref, dx_ref, dg_ref):
        for s in range(wide // LANES):
            lanes = slice(LANES * s, LANES * (s + 1))
            xv = x_ref[:, lanes].astype(F32)
            dyv = dy_ref[:, lanes].astype(F32)
            r = lax.rsqrt(_head_mean(xv * xv, b_ref[...]) + RMS_EPS)
            xhat = xv * r
            dxhat = dyv * g_ref[...]
            dx_ref[:, lanes] = (r * (dxhat - xhat * _head_mean(dxhat * xhat, b_ref[...]))).astype(BF)
            dg_ref[0, :, lanes] = jnp.sum(dyv * xhat, axis=0, keepdims=True)

    return pl.pallas_call(
        body, name=name, grid=(T // HEADNORM_TILE, nb),
        in_specs=[pl.BlockSpec((HEADNORM_TILE, wide), lambda i, j: (i, j)),
                  pl.BlockSpec((HEADNORM_TILE, wide), lambda i, j: (i, j + off)),
                  pl.BlockSpec((1, LANES), lambda i, j: (0, 0)), pl.BlockSpec((LANES, LANES), lambda i, j: (0, 0)), ANY],
        out_specs=[pl.BlockSpec((HEADNORM_TILE, wide), lambda i, j: (i, j + off)),
                   pl.BlockSpec((1, 1, wide), lambda i, j: (i, 0, j))],
        out_shape=[jax.ShapeDtypeStruct(into.shape, BF), jax.ShapeDtypeStruct((T // HEADNORM_TILE, 1, width), F32)],
        input_output_aliases={4: 0}, compiler_params=_params(("parallel", "parallel")),
    )(dy, proj, g2, _half_sum_matrix(), into)


def _shift_down(v, k, row):
    return jnp.where(row >= k, pltpu.roll(v, k, axis=0), 0.0)


def _shift_up(v, k, row, T):
    return jnp.where(row < T - k, pltpu.roll(v, T - k, axis=0), 0.0)


def _by_group(g, vals):
    out = vals[-1]
    for i in range(len(vals) - 2, -1, -1):
        out = jnp.where(g == i, vals[i], out)
    return out


def _pool_fwd(name, proj, pool_w, pool_scale):
    T = proj.shape[0]

    def body(x_ref, w_ref, s_ref, pooled_ref, mixed_ref):
        g = pl.program_id(0)
        xv = x_ref[...].astype(F32)
        row = lax.broadcasted_iota(jnp.int32, (T, 1), 0)
        s2 = xv + _shift_down(xv, 1, row)
        s4 = s2 + _shift_down(s2, 2, row)
        s8 = s4 + _shift_down(s4, 4, row)
        s16 = s8 + _shift_down(s8, 8, row)
        wsum = _by_group(g, [s2, s4, s8, s16])
        count = jnp.minimum(row + 1, 2 << g).astype(F32)
        pooled = (wsum / count - xv).astype(BF)
        pooled_ref[...] = pooled
        mixed = jnp.dot(pooled, w_ref[0].astype(BF), preferred_element_type=F32) * s_ref[...]
        mixed_ref[...] = mixed.astype(BF)

    col = pl.BlockSpec((T, POOL_GROUP), lambda g: (0, g))
    return pl.pallas_call(
        body, name=name, grid=(N_POOL_GROUPS,),
        in_specs=[col, pl.BlockSpec((1, POOL_GROUP, POOL_GROUP), lambda g: (g, 0, 0)),
                  pl.BlockSpec((1, POOL_GROUP), lambda g: (0, g))],
        out_specs=[col, col],
        out_shape=[jax.ShapeDtypeStruct((T, POOL_WIDTH), BF), jax.ShapeDtypeStruct((T, POOL_WIDTH), BF)],
        compiler_params=_params(("parallel",)),
    )(proj, pool_w, pool_scale)


def _pool_bwd(name, dmixed, pooled, pool_w, pool_scale, into):
    T = dmixed.shape[0]

    def body(dm_ref, p_ref, w_ref, s_ref, into_ref, dx_ref, dw_ref, ds_ref):
        g = pl.program_id(0)
        dm = dm_ref[...].astype(F32)
        pooled = p_ref[...]
        w = w_ref[0].astype(BF)
        pre = jnp.dot(pooled, w, preferred_element_type=F32)
        ds_ref[...] = jnp.sum(dm * pre, axis=0, keepdims=True)
        dms = (dm * s_ref[...]).astype(BF)
        dw_ref[0] = lax.dot_general(pooled, dms, _DIMS["tn"], preferred_element_type=F32)
        dpooled = lax.dot_general(dms, w, _DIMS["nt"], preferred_element_type=F32)
        row = lax.broadcasted_iota(jnp.int32, (T, 1), 0)
        count = jnp.minimum(row + 1, 2 << g).astype(F32)
        z = dpooled / count
        l2 = z + _shift_up(z, 1, row, T)
        l4 = l2 + _shift_up(l2, 2, row, T)
        l8 = l4 + _shift_up(l4, 4, row, T)
        l16 = l8 + _shift_up(l8, 8, row, T)
        dx_ref[...] = (_by_group(g, [l2, l4, l8, l16]) - dpooled).astype(BF)

    col = pl.BlockSpec((T, POOL_GROUP), lambda g: (0, g))
    wspec = pl.BlockSpec((1, POOL_GROUP, POOL_GROUP), lambda g: (g, 0, 0))
    sspec = pl.BlockSpec((1, POOL_GROUP), lambda g: (0, g))
    return pl.pallas_call(
        body, name=name, grid=(N_POOL_GROUPS,), in_specs=[col, col, wspec, sspec, ANY], out_specs=[col, wspec, sspec],
        out_shape=[jax.ShapeDtypeStruct(into.shape, BF),
                   jax.ShapeDtypeStruct((N_POOL_GROUPS, POOL_GROUP, POOL_GROUP), F32),
                   jax.ShapeDtypeStruct((1, POOL_WIDTH), F32)],
        input_output_aliases={4: 0}, compiler_params=_params(("parallel",)),
    )(dmixed, pooled, pool_w, pool_scale, into)


ATTN_SCALE = HEAD_DIM ** -0.5
MASKED = float(jnp.finfo(jnp.float32).min)
KV_COL_BLOCK_V = COL_V // LANES
GROUP_WIDTH = GQA_GROUP * HEAD_DIM


def _dup_head(v, j):
    half = lax.broadcasted_iota(jnp.int32, (1, LANES), 1) // HEAD_DIM
    return jnp.where(half == j, v, pltpu.roll(v, HEAD_DIM, axis=1))


def _stack_heads(v, low):
    pieces = []
    for p in range(GROUP_WIDTH // LANES):
        vp = v[:, LANES * p: LANES * (p + 1)]
        pieces.append(jnp.where(low, vp, jnp.zeros_like(vp)))
        pieces.append(jnp.where(low, jnp.zeros_like(vp), vp))
    return jnp.concatenate(pieces, axis=0)


def _unstack_transposed(t, low):
    pairs = []
    for p in range(GROUP_WIDTH // LANES):
        even = t[:, BLOCK * (2 * p): BLOCK * (2 * p + 1)].T
        odd = t[:, BLOCK * (2 * p + 1): BLOCK * (2 * p + 2)].T
        pairs.append(jnp.where(low, even, odd))
    return pairs


STACKED = GQA_GROUP * BLOCK


def _band_bias():
    key = lax.broadcasted_iota(jnp.int32, (2, 2 * BLOCK, STACKED), 1)
    qry = lax.broadcasted_iota(jnp.int32, (2, 2 * BLOCK, STACKED), 2) % BLOCK
    first = lax.broadcasted_iota(jnp.int32, (2, 2 * BLOCK, STACKED), 0) == 0
    valid = (key > qry) & (key <= qry + BLOCK) & (jnp.logical_not(first) | (key >= BLOCK))
    return jnp.where(valid, 0.0, MASKED).astype(F32)


def _softmax_keys_on_sublanes(k2, q, bias, sink_ref, j):
    head_of_lane = lax.broadcasted_iota(jnp.int32, (1, STACKED), 1) // BLOCK
    sink = jnp.zeros((1, STACKED), F32)
    for h in range(GQA_GROUP):
        sink = jnp.where(head_of_lane == h, sink_ref[j * GQA_GROUP + h], sink)
    s = lax.dot_general(k2, q, _DIMS["nt"], preferred_element_type=F32) + bias
    m = jnp.maximum(jnp.max(s, axis=0, keepdims=True), sink)
    e = jnp.exp(s - m)
    e_sink = jnp.exp(sink - m)
    inv = 1.0 / (jnp.sum(e, axis=0, keepdims=True) + e_sink)
    return e * inv, e_sink * inv


def _attn_fwd(name, proj, qg, kg, sinks, bias, comm):
    T = proj.shape[0]
    nb = T // BLOCK
    plumb = _CommPlumbing(comm)

    def body(sink_ref, bias_ref, ones_ref, qg_ref, kg_ref, q0_ref, q1_ref, kp_ref, kc_ref, vp_ref, vc_ref, *rest):
        c_in, (o_ref, q_ref, kn_ref) = rest[:plumb.n_in], rest[plumb.n_in: plumb.n_in + 3]
        c_out, c_scr = rest[plumb.n_in + 3: plumb.n_in + 3 + plumb.n_out], rest[plumb.n_in + 3 + plumb.n_out:]
        m = pl.program_id(0)
        plumb.handshake(m == 0)
        plumb.run(m, nb // 2, True, c_in, c_out, c_scr)
        low = lax.broadcasted_iota(jnp.int32, (1, LANES), 1) < HEAD_DIM

        def head_norm(raw, gain):
            xv = raw.astype(F32)
            return (xv * lax.rsqrt(_head_mean(xv * xv, ones_ref[...]) + RMS_EPS) * gain).astype(BF)

        for half, raw_ref in enumerate((q0_ref, q1_ref)):
            for s in range(GROUP_WIDTH // LANES):
                q_ref[:, GROUP_WIDTH * half + LANES * s: GROUP_WIDTH * half + LANES * (s + 1)] = head_norm(
                    raw_ref[:, LANES * s: LANES * (s + 1)], qg_ref[...])
        k_pair, k_prev = head_norm(kc_ref[...], kg_ref[...]), head_norm(kp_ref[...], kg_ref[...])
        kn_ref[...] = k_pair
        v_pair = vc_ref[...]
        for b in range(2):
            rows = slice(BLOCK * b, BLOCK * (b + 1))
            kk = k_pair if b else jnp.concatenate([k_prev, k_pair[0:BLOCK]], axis=0)
            vv = v_pair if b else jnp.concatenate([vp_ref[...], v_pair[0:BLOCK]], axis=0)
            bias = bias_ref[1] if b else bias_ref[jnp.minimum(m, 1)]
            for j in range(2):
                q = _stack_heads(q_ref[rows, GROUP_WIDTH * j: GROUP_WIDTH * (j + 1)], low)
                p, _ = _softmax_keys_on_sublanes(_dup_head(kk, j), q, bias, sink_ref, j)
                o_t = lax.dot_general(_dup_head(vv, j), p.astype(BF), _DIMS["tn"], preferred_element_type=F32)
                for pair, o in enumerate(_unstack_transposed(o_t, low)):
                    lanes = slice(GROUP_WIDTH * j + LANES * pair, GROUP_WIDTH * j + LANES * (pair + 1))
                    o_ref[rows, lanes] = o.astype(BF)
        plumb.run(m, nb // 2, False, c_in, c_out, c_scr)

    wide = pl.BlockSpec((2 * BLOCK, ATTN_WIDTH), lambda m: (m, 0))
    before = lambda m: jnp.maximum(2 * m - 1, 0)
    gain = pl.BlockSpec((1, LANES), lambda m: (0, 0))
    q_block, k_block = COL_Q // GROUP_WIDTH, COL_K // LANES
    res = pl.pallas_call(
        body, name=name, grid=(nb // 2,),
        in_specs=[pl.BlockSpec(memory_space=pltpu.SMEM),
                  pl.BlockSpec((2, 2 * BLOCK, STACKED), lambda m: (0, 0, 0)),
                  pl.BlockSpec((LANES, LANES), lambda m: (0, 0)), gain, gain,
                  pl.BlockSpec((2 * BLOCK, GROUP_WIDTH), lambda m: (m, q_block)),
                  pl.BlockSpec((2 * BLOCK, GROUP_WIDTH), lambda m: (m, q_block + 1)),
                  pl.BlockSpec((BLOCK, LANES), lambda m: (before(m), k_block)),
                  pl.BlockSpec((2 * BLOCK, LANES), lambda m: (m, k_block)),
                  pl.BlockSpec((BLOCK, LANES), lambda m: (before(m), KV_COL_BLOCK_V)),
                  pl.BlockSpec((2 * BLOCK, LANES), lambda m: (m, KV_COL_BLOCK_V))] + [ANY] * plumb.n_in,
        out_specs=[wide, wide, pl.BlockSpec((2 * BLOCK, LANES), lambda m: (m, 0))] + [ANY] * plumb.n_out,
        out_shape=[jax.ShapeDtypeStruct((T, ATTN_WIDTH), BF), jax.ShapeDtypeStruct((T, ATTN_WIDTH), BF),
                   jax.ShapeDtypeStruct((T, KV_WIDTH), BF)] + plumb.out_shapes,
        scratch_shapes=plumb.scratch, compiler_params=_params(("arbitrary",), plumb.collective_id()),
    )(sinks, bias, _half_sum_matrix(), qg, kg, proj, proj, proj, proj, proj, proj, *plumb.args)
    return list(res[:3]), plumb.split_outputs(res[3:])


def _attn_bwd(name, dout, qn, kn, proj, sinks, bias, comm):
    T = qn.shape[0]
    nb = T // BLOCK
    plumb = _CommPlumbing(comm)

    def body(sink_ref, bias_ref, do_ref, q_ref, kp_ref, kc_ref, vp_ref, vc_ref, *rest):
        c_in = rest[:plumb.n_in]
        dq_ref, k_own, k_before, v_own, v_before, dsink_ref = rest[plumb.n_in: plumb.n_in + 6]
        c_out, c_scr = rest[plumb.n_in + 6: plumb.n_in + 6 + plumb.n_out], rest[plumb.n_in + 6 + plumb.n_out:]
        m = pl.program_id(0)
        plumb.handshake(m == 0)
        plumb.run(m, nb // 2, True, c_in, c_out, c_scr)
        lane = lax.broadcasted_iota(jnp.int32, (1, LANES), 1)
        low = lane < HEAD_DIM

        @pl.when(m == 0)
        def _():
            dsink_ref[...] = jnp.zeros_like(dsink_ref)

        k_pair, v_pair = kc_ref[...], vc_ref[...]
        dsink = jnp.zeros((1, LANES), F32)
        for b in range(2):
            rows = slice(BLOCK * b, BLOCK * (b + 1))
            kk = k_pair if b else jnp.concatenate([kp_ref[...], k_pair[0:BLOCK]], axis=0)
            vv = v_pair if b else jnp.concatenate([vp_ref[...], v_pair[0:BLOCK]], axis=0)
            bias = bias_ref[1] if b else bias_ref[jnp.minimum(m, 1)]
            dk_tot = jnp.zeros((2 * BLOCK, LANES), F32)
            dv_tot = jnp.zeros((2 * BLOCK, LANES), F32)
            for j in range(2):
                k2 = _dup_head(kk, j)
                v2 = _dup_head(vv, j)
                q = _stack_heads(q_ref[rows, GROUP_WIDTH * j: GROUP_WIDTH * (j + 1)], low)
                do = _stack_heads(do_ref[rows, GROUP_WIDTH * j: GROUP_WIDTH * (j + 1)], low)
                p, psink = _softmax_keys_on_sublanes(k2, q, bias, sink_ref, j)
                dp =lax.dot_general(v2, do, _DIMS["nt"], preferred_element_type=F32)
                delta = jnp.sum(p * dp, axis=0, keepdims=True)
                ds = (p * (dp - delta)).astype(BF)
                dk2 = jnp.dot(ds, q, preferred_element_type=F32)
                dv2 = jnp.dot(p.astype(BF), do, preferred_element_type=F32)
                dq_t = lax.dot_general(k2, ds, _DIMS["tn"], preferred_element_type=F32)
                for pair, dq in enumerate(_unstack_transposed(dq_t, low)):
                    lanes = slice(GROUP_WIDTH * j + LANES * pair, GROUP_WIDTH * j + LANES * (pair + 1))
                    dq_ref[rows, lanes] = dq.astype(BF)
                mine = low if j == 0 else jnp.logical_not(low)
                dk_tot = dk_tot + jnp.where(mine, dk2 + pltpu.roll(dk2, HEAD_DIM, axis=1), 0.0)
                dv_tot = dv_tot + jnp.where(mine, dv2 + pltpu.roll(dv2, HEAD_DIM, axis=1), 0.0)
                sink_term = psink * delta
                for h in range(GQA_GROUP):
                    val = -jnp.sum(sink_term[:, BLOCK * h: BLOCK * (h + 1)], axis=1, keepdims=True)
                    dsink = dsink + jnp.where(lane == j * GQA_GROUP + h, val, 0.0)
            k_before[rows, :], k_own[rows, :] = dk_tot[0:BLOCK], dk_tot[BLOCK:]
            v_before[rows, :], v_own[rows, :] = dv_tot[0:BLOCK], dv_tot[BLOCK:]
        dsink_ref[0:1, :] += dsink
        plumb.run(m, nb // 2, False, c_in, c_out, c_scr)

    wide = pl.BlockSpec((2 * BLOCK, ATTN_WIDTH), lambda m: (m, 0))
    pair = pl.BlockSpec((2 * BLOCK, LANES), lambda m: (m, 0))
    before = lambda m: jnp.maximum(2 * m - 1, 0)
    res = pl.pallas_call(
        body, name=name, grid=(nb // 2,),
        in_specs=[pl.BlockSpec(memory_space=pltpu.SMEM),
                  pl.BlockSpec((2, 2 * BLOCK, STACKED), lambda m: (0, 0, 0)), wide, wide,
                  pl.BlockSpec((BLOCK, LANES), lambda m: (before(m), 0)), pair,
                  pl.BlockSpec((BLOCK, LANES), lambda m: (before(m), KV_COL_BLOCK_V)),
                  pl.BlockSpec((2 * BLOCK, LANES), lambda m: (m, KV_COL_BLOCK_V))] + [ANY] * plumb.n_in,
        out_specs=[wide, pair, pair, pair, pair, pl.BlockSpec((8, LANES), lambda m: (0, 0))] + [ANY] * plumb.n_out,
        out_shape=[jax.ShapeDtypeStruct((T, ATTN_WIDTH), BF)] + [jax.ShapeDtypeStruct((T, KV_WIDTH), F32)] * 4
        + [jax.ShapeDtypeStruct((8, LANES), F32)] + plumb.out_shapes,
        scratch_shapes=plumb.scratch, compiler_params=_params(("arbitrary",), plumb.collective_id()),
    )(sinks, bias, dout, qn, kn, kn, proj, proj, *plumb.args)
    return list(res[:6]), plumb.split_outputs(res[6:])


def _swiglu_fwd_epilogue(accs, ex):
    g, u = accs
    return [g, u, g * jax.nn.sigmoid(g) * u], []


def _swiglu_bwd_epilogue(accs, ex):
    (da,) = accs
    g, u = ex[0].astype(F32), ex[1].astype(F32)
    s = jax.nn.sigmoid(g)
    gs = g * s
    return [da * u * (s + gs - gs * s), da * gs], []


def _residual_norm_epilogue(scale):
    def epilogue(accs, ex):
        res, gain = ex
        h = res + scale * accs[0]
        r = lax.rsqrt(jnp.mean(h * h, axis=-1, keepdims=True) + RMS_EPS)
        return [h, h * r * gain], []
    return epilogue


def _rms_bwd_epilogue(accs, ex):
    (dn,) = accs
    xv, g, dres = ex
    r = lax.rsqrt(jnp.mean(xv * xv, axis=-1, keepdims=True) + RMS_EPS)
    xhat = xv * r
    dxhat = dn * g
    dx = dres + r * (dxhat - xhat * jnp.mean(dxhat * xhat, axis=-1, keepdims=True))
    return [dx, dx], [dn * xhat]


def _loss_epilogue(accs, ex):
    xv, target = ex
    d = xv + 0.5 * accs[0] - target
    dy = d * (1.0 / D_MODEL)
    return [dy, dy], [d * d]


def _merge_fwd_epilogue(accs, ex):
    (ba,) = accs
    bp, gp_pre, ga_pre, bias_p, bias_a = ex
    gp = jax.nn.sigmoid(gp_pre.astype(F32) + bias_p)
    ga = jax.nn.sigmoid(ga_pre.astype(F32) + bias_a)
    return [gp * bp.astype(F32) + ga * ba, ba], []


def _merge_bwd_epilogue(accs, ex):
    (dm,) = accs
    bp, ba, gp_pre, ga_pre, bias_p, bias_a = ex
    gp = jax.nn.sigmoid(gp_pre.astype(F32) + bias_p)
    ga = jax.nn.sigmoid(ga_pre.astype(F32) + bias_a)
    dbp, dba = dm * gp, dm * ga
    dgp = dbp * bp.astype(F32) * (1.0 - gp)
    dga = dba * ba.astype(F32) * (1.0 - ga)
    return [dbp, dba, dgp, dga], [dgp, dga]


def _prep(name, ws, transposes):
    n = len(ws)

    def body(*refs):
        for w_ref, o_ref, tr in zip(refs[:n], refs[n:], transposes):
            v = w_ref[...]
            o_ref[...] = (v.T if tr else v).astype(BF)

    shapes = [jax.ShapeDtypeStruct(w.shape[::-1] if tr else w.shape, BF) for w, tr in zip(ws, transposes)]
    return pl.pallas_call(body, name=name, out_shape=shapes, compiler_params=_params())(*ws)


def _adam_math(w, g, m, v):
    m = ADAM_B1 * m + (1.0 - ADAM_B1) * g
    v = ADAM_B2 * v + (1.0 - ADAM_B2) * jnp.square(g)
    m_hat = m / (1.0 - ADAM_B1 ** ADAM_STEP)
    v_hat = v / (1.0 - ADAM_B2 ** ADAM_STEP)
    delta = -ADAM_LR * (m_hat / (jnp.sqrt(v_hat) + ADAM_EPS) + ADAM_WD * w)
    return delta, m, v


def _adamw_sharded(name, items, transpose=False):
    n = len(items)

    def body(*refs):
        ins, outs = refs[:4 * n], refs[4 * n:]
        for k in range(n):
            s_ref, w_ref, m_ref, v_ref = ins[4 * k: 4 * k + 4]
            g = s_ref[0].astype(F32)
            for i in range(1, 4):
                g = g + s_ref[i].astype(F32)
            if transpose:
                g = g.T
            delta, mn, vn = _adam_math(w_ref[...], g, m_ref[...], v_ref[...])
            for o_ref, val in zip(outs[4 * k: 4 * k + 4], (g, delta, mn, vn)):
                o_ref[...] = val

    flat = [a for item in items for a in item]
    out_shape = [jax.ShapeDtypeStruct(item[1].shape, F32) for item in items for _ in range(4)]
    _, r, C = items[0][0].shape
    rows = r // 4
    if transpose or rows % 8:
        res = pl.pallas_call(body, name=name, out_shape=out_shape, compiler_params=_params())(*flat)
    else:
        tile = pl.BlockSpec((rows, C), lambda i: (i, 0))
        res = pl.pallas_call(
            body, name=name, grid=(4,), in_specs=[pl.BlockSpec((4, rows, C), lambda i: (0, i, 0)), tile, tile, tile] * n,
            out_specs=[tile] * (4 * n), out_shape=out_shape, compiler_params=_params(("parallel",)),
        )(*flat)
    return [tuple(res[4 * k: 4 * k + 4]) for k in range(n)]


SMALL_LAYOUT = (("ffn1_norm", 0, (8, LANES)), ("mix_norm", 8, (8, LANES)), ("ffn2_norm", 16, (8, LANES)),
                ("gate_bias", 24, (16, LANES)), ("pool_scale", 40, (4, LANES)), ("q_norm", 48, (1, HEAD_DIM)),
                ("k_norm", 56, (1, HEAD_DIM)), ("sinks", 64, (1, N_HEADS)))
LOSS_ROW = 72
SMALL_ROWS = 80


def _adamw_small(name, g_vec, g_pool_w, params):
    n = len(SMALL_LAYOUT) + 1

    def body(vec_ref, pw_ref, *refs):
        ins, outs = refs[:3 * n], refs[3 * n:]
        vec = vec_ref[0]
        pw = pw_ref[0]
        for i in range(1, N_DEV):
            vec = vec + vec_ref[i]
            pw = pw + pw_ref[i]
        grads = [vec[r0:r0 + shape[0], 0:shape[1]] for _, r0, shape in SMALL_LAYOUT] + [pw]
        for p, g in enumerate(grads):
            w_ref, m_ref, v_ref = ins[3 * p: 3 * p + 3]
            delta, mn, vn = _adam_math(w_ref[...], g, m_ref[...], v_ref[...])
            for o_ref, val in zip(outs[4 * p: 4 * p + 4], (g, delta, mn, vn)):
                o_ref[...] = val
        outs[4 * n][...] = vec[LOSS_ROW:LOSS_ROW + 1, :]

    flat = [a for wmv in params for a in wmv]
    out_shape = [jax.ShapeDtypeStruct(wmv[0].shape, F32) for wmv in params for _ in range(4)]
    out_shape.append(jax.ShapeDtypeStruct((1, LANES), F32))
    res = pl.pallas_call(body, name=name, out_shape=out_shape, compiler_params=_params())(g_vec, g_pool_w, *flat)
    return [tuple(res[4 * p: 4 * p + 4]) for p in range(n)], res[4 * n]


def _place():
    x, y, c = lax.axis_index("x"), lax.axis_index("y"), lax.axis_index("c")
    other_chips = [(1 - x, y), (x, 1 - y), (1 - x, 1 - y)]
    return x, y, c, other_chips


def _rows(ref, r, place, natural=False):
    px, py, pc = place
    b = 4 * px + 2 * py + pc if natural else 4 * pc + 2 * px + py
    return ref.at[pl.ds(pl.multiple_of(b * r, 8), r), :]


def _gather_task(shards, natural=(), forward_at=0.75):
    n = len(shards)
    rs = [s.shape[0] for s in shards]
    rows_of = lambda ref, k, place: _rows(ref, rs[k], place, k in natural)

    def copy(scr, outs, k, slot, block, to, src=None):
        rows = rows_of(outs[k], k, block)
        return pltpu.make_async_remote_copy(
            src_ref=rows if src is None else src, dst_ref=rows, send_sem=scr[0].at[7 * k + slot],
            recv_sem=scr[1].at[7 * k + slot], device_id=to, device_id_type=MESH)

    def first_sends(ins, outs, scr):
        x, y, c, chips = _place()
        me = (x, y, c)
        cps = [copy(scr, outs, k, 1 + j, me, (*chip, c), src=ins[k]) for j, chip in enumerate(chips) for k in range(n)]
        return cps + [copy(scr, outs, k, 0, me, (x, y, 1 - c), src=ins[k]) for k in range(n)]

    def passed_on(outs, scr):
        x, y, c, chips = _place()
        return [copy(scr, outs, k, 4 + j, (*chip, c), (x, y, 1 - c)) for j, chip in enumerate(chips) for k in range(n)]

    def local(ins, outs, scr):
        x, y, c, _ = _place()
        return [pltpu.make_async_copy(ins[k], rows_of(outs[k], k, (x, y, c)), scr[2].at[k]) for k in range(n)]

    def start(ins, outs, scr):
        for cp in local(ins, outs, scr) + first_sends(ins, outs, scr):
            cp.start()

    def forward(ins, outs, scr):
        x, y, c, chips = _place()
        for j, chip in enumerate(chips):
            for k in range(n):
                copy(scr, outs, k, 1 + j, (*chip, c), (x, y, c)).wait_recv()
                copy(scr, outs, k, 4 + j, (*chip, c), (x, y, 1 - c)).start()

    def finish(ins, outs, scr):
        x, y, c, chips = _place()
        for k in range(n):
            copy(scr, outs, k, 0, (x, y, 1 - c), (x, y, c)).wait_recv()
        for j, chip in enumerate(chips):
            for k in range(n):
                copy(scr, outs, k, 4 + j, (*chip, 1 - c), (x, y, c)).wait_recv()
        for cp in first_sends(ins, outs, scr) + passed_on(outs, scr):
            cp.wait_send()
        for cp in local(ins, outs, scr):
            cp.wait()

    out_shapes = [jax.ShapeDtypeStruct((N_DEV * s.shape[0], s.shape[1]), s.dtype) for s in shards]
    scratch = [pltpu.SemaphoreType.DMA((7 * n,)), pltpu.SemaphoreType.DMA((7 * n,)), pltpu.SemaphoreType.DMA((n,))]
    return _Task(shards, out_shapes, scratch, [(0, start), (forward_at, forward), (1.0, finish)], ("sibling", "chips"))


def _direct_gather_task(shards):
    n = len(shards)
    rs = [s.shape[0] for s in shards]

    def peers():
        x, y, c, _ = _place()
        flip = lambda v, bit: 1 - v if bit else v
        return (x, y, c), [(flip(x, (s >> 2) & 1), flip(y, (s >> 1) & 1), flip(c, s & 1)) for s in range(1, N_DEV)]

    def copies(ins, outs, scr):
        me, others = peers()
        local = [pltpu.make_async_copy(ins[k], _rows(outs[k], rs[k], me), scr[2].at[k]) for k in range(n)]
        sems = lambda k, s: dict(send_sem=scr[0].at[7 * k + s], recv_sem=scr[1].at[7 * k + s], device_id_type=MESH)
        sends = [pltpu.make_async_remote_copy(src_ref=ins[k], dst_ref=_rows(outs[k], rs[k], me), device_id=to, **sems(k, s))
                 for s, to in enumerate(others) for k in range(n)]
        recvs = [pltpu.make_async_remote_copy(src_ref=_rows(outs[k], rs[k], frm), dst_ref=_rows(outs[k], rs[k], frm),
                                              device_id=me, **sems(k, s))
                 for s, frm in enumerate(others) for k in range(n)]
        return local, sends, recvs

    def start(ins, outs, scr):
        local, sends, _ = copies(ins, outs, scr)
        for cp in local + sends:
            cp.start()

    def finish(ins, outs, scr):
        local, sends, recvs = copies(ins, outs, scr)
        for cp in recvs:
            cp.wait_recv()
        for cp in sends:
            cp.wait_send()
        for cp in local:
            cp.wait()

    out_shapes = [jax.ShapeDtypeStruct((N_DEV * s.shape[0], s.shape[1]), s.dtype) for s in shards]
    scratch = [pltpu.SemaphoreType.DMA((7 * n,)), pltpu.SemaphoreType.DMA((7 * n,)), pltpu.SemaphoreType.DMA((n,))]
    return _Task(shards, out_shapes, scratch, [(0, start), (1.0, finish)], ("all",))


def _chip_task(sums):
    n = len(sums)
    rs = [s.shape[0] // 4 for s in sums]

    def block(ref, k, chip_index):
        return ref.at[pl.ds(pl.multiple_of(chip_index * rs[k], 8), rs[k]), :]

    def copies(ins, outs, scr):
        send_sems, recv_sems, local_sems = scr
        x, y, c, chips = _place()
        here = 2 * x + y
        local = [pltpu.make_async_copy(block(ins[k], k, here), outs[k].at[here], local_sems.at[k]) for k in range(n)]
        remote = []
        for j, (px, py) in enumerate(chips):
            remote += [pltpu.make_async_remote_copy(
                src_ref=block(ins[k], k, 2 * px + py), dst_ref=outs[k].at[here],
                send_sem=send_sems.at[3 * k + j], recv_sem=recv_sems.at[3 * k + j],
                device_id=(px, py, c), device_id_type=MESH) for k in range(n)]
        return local, remote

    def start(ins, outs, scr):
        local, remote = copies(ins, outs, scr)
        for cp in local + remote:
            cp.start()

    def finish(ins, outs, scr):
        local, remote = copies(ins, outs, scr)
        for cp in remote:
            cp.wait()
        for cp in local:
            cp.wait()

    out_shapes = [jax.ShapeDtypeStruct((4, r, s.shape[1]), s.dtype) for r, s in zip(rs, sums)]
    scratch = [pltpu.SemaphoreType.DMA((3 * n,)), pltpu.SemaphoreType.DMA((3 * n,)), pltpu.SemaphoreType.DMA((n,))]
    return _Task(sums, out_shapes, scratch, [(0, start), (1.0, finish)], ("chips",))


def _dw_pair(name, a, b, scale, comm=None, blocks=1):
    T, M = a.shape
    N = b.shape[1]
    half = M // 2
    wide = half // blocks
    tk = min(2048, T)
    nK = T // tk
    plumb = _CommPlumbing(comm)

    def body(core_ref, *rest):
        a_refs, b_ref, rest = rest[:blocks], rest[blocks], rest[blocks + 1:]
        c_in = rest[:plumb.n_in]
        o_ref = rest[plumb.n_in]
        c_out = rest[plumb.n_in + 1: plumb.n_in + 1 + plumb.n_out]
        acc, stage, land, send_sem, recv_sem = rest[plumb.n_in + 1 + plumb.n_out: plumb.n_in + 6 + plumb.n_out]
        c_scr = rest[plumb.n_in + 6 + plumb.n_out:]
        i, k = pl.program_id(0), pl.program_id(1)
        x, y, c, _ = _place()
        push = pltpu.make_async_remote_copy(src_ref=stage, dst_ref=land, send_sem=send_sem, recv_sem=recv_sem,
                                            device_id=(x, y, 1 - c), device_id_type=MESH)
        plumb.handshake((i == 0) & (k == 0), own=("sibling",))
        if comm:
            plumb.run(i * nK + k, 2 * nK, True, c_in, c_out, c_scr)

        av = a_refs[0][...] if blocks == 1 else jnp.concatenate([r[...] for r in a_refs], axis=1)
        p = lax.dot_general(av, b_ref[...], _DIMS["tn"], preferred_element_type=F32)

        @pl.when(k == 0)
        def _():
            acc[...] = p

        @pl.when(k > 0)
        def _():
            acc[...] += p

        @pl.when((i == 0) & (k == nK - 1))
        def _():
            stage[...] = (scale * acc[...]).astype(BF)
            push.start()

        @pl.when((i == 1) & (k == nK - 1))
        def _():
            push.wait_recv()
            o_ref[...] = (scale * acc[...] + land[...].astype(F32)).astype(BF)
            push.wait_send()

        if comm:
            plumb.run(i * nK + k, 2 * nK, False, c_in, c_out, c_scr)

    grid_spec = pltpu.PrefetchScalarGridSpec(
        num_scalar_prefetch=1, grid=(2, nK),
        in_specs=[pl.BlockSpec((tk, wide), functools.partial(
            lambda i, k, core, j: (k, (2 * j if blocks > 1 else 0) + jnp.where(i == 0, 1 - core[0], core[0])), j=j))
            for j in range(blocks)] + [pl.BlockSpec((tk, N), lambda i, k, core: (k, 0))] + [ANY] * plumb.n_in,
        out_specs=[pl.BlockSpec((half, N), lambda i, k, core: (0, 0))] + [ANY] * plumb.n_out,
        scratch_shapes=[pltpu.VMEM((half, N), F32), pltpu.VMEM((half, N), BF), pltpu.VMEM((half, N), BF),
                        pltpu.SemaphoreType.DMA, pltpu.SemaphoreType.DMA] + plumb.scratch)
    core = lax.axis_index("c").astype(jnp.int32).reshape(1)
    res = pl.pallas_call(
        body, name=name, grid_spec=grid_spec,
        out_shape=[jax.ShapeDtypeStruct((half, N), BF)] + plumb.out_shapes,
        compiler_params=_params(("arbitrary", "arbitrary"), plumb.collective_id(own=("sibling",))),
    )(core, *([a] * blocks), b, *plumb.args)
    return (res[0], plumb.split_outputs(res[1:])) if comm else res[0]


def _pair_task(parts):
    n = len(parts)

    def copies(ins, outs, scr):
        x, y, c, _ = _place()
        return [pltpu.make_async_remote_copy(
            src_ref=ins[k].at[:, pl.ds(1 - c, 1)], dst_ref=outs[k], send_sem=scr[0].at[k], recv_sem=scr[1].at[k],
            device_id=(x, y, 1 - c), device_id_type=MESH) for k in range(n)]

    def start(ins, outs, scr):
        for cp in copies(ins, outs, scr):
            cp.start()

    def finish(ins, outs, scr):
        for cp in copies(ins, outs, scr):
            cp.wait()

    out_shapes = [jax.ShapeDtypeStruct((4, 1) + p.shape[2:], p.dtype) for p in parts]
    scratch = [pltpu.SemaphoreType.DMA((n,)), pltpu.SemaphoreType.DMA((n,))]
    return _Task(parts, out_shapes, scratch, [(0, start), (1.0, finish)], ("sibling",))


def _pair_sum(name, part, got, core):
    _, _, r, C = part.shape

    def body(core_ref, p_ref, g_ref, o_ref):
        o_ref[0] = (p_ref[0, 0].astype(F32) + g_ref[0, 0].astype(F32)).astype(o_ref.dtype)

    return pl.pallas_call(
        body, name=name,
        grid_spec=pltpu.PrefetchScalarGridSpec(
            num_scalar_prefetch=1, grid=(4,),
            in_specs=[pl.BlockSpec((1, 1, r, C), lambda i, core_ref: (i, core_ref[0], 0, 0)),
                      pl.BlockSpec((1, 1, r, C), lambda i, core_ref: (i, 0, 0, 0))],
            out_specs=pl.BlockSpec((1, r, C), lambda i, core_ref: (i, 0, 0))),
        out_shape=jax.ShapeDtypeStruct((4, r, C), part.dtype), compiler_params=_params(("parallel",)),
    )(core, part, got)


def _ffn_bwd(tag, dy, dyb, x, gain, wgT, wuT, wd, saved, earlier=None):
    n, g, u, a = saved
    half = lambda accs, ex: _swiglu_bwd_epilogue([0.5 * accs[0]], ex)
    act_args = dict(tm=1024, tn=1408, tk=D_MODEL, epilogue=half, extras=[(g, "tile", 0), (u, "tile", 0)], cols_outer=True)
    if earlier is None:
        sum_d = _dw_pair(tag + "_dw_down", a, dyb, 0.5)
        (dg, du), ((slots_d,),) = _mm(tag + "_d_act", [(dyb, wd, "nt", 0)], [BF, BF], comm=[_chip_task([sum_d])], **act_args)
        slots_e = None
        sum_g = _dw_pair(tag + "_dw_gate", dg, n, 1.0)
    else:
        sum_d, ((got,),) = _dw_pair(tag + "_dw_down", a, dyb, 0.5, comm=[_pair_task([earlier])])
        core = lax.axis_index("c").astype(jnp.int32).reshape(1)
        sum_e = _pair_sum(tag + "_pair_sum_earlier", earlier, got, core)
        sum_e = sum_e.reshape(4 * sum_e.shape[1], sum_e.shape[2])
        (dg, du), ((slots_e,),) = _mm(tag + "_d_act", [(dyb, wd, "nt", 0)], [BF, BF], comm=[_chip_task([sum_e])], **act_args)
        sum_g, ((slots_d,),) = _dw_pair(tag + "_dw_gate", dg, n, 1.0, comm=[_chip_task([sum_d])])
    norm_args = dict(tm=512, tn=D_MODEL, tk=D_FF, epilogue=_rms_bwd_epilogue, n_colsum=1,
                     extras=[(x, "tile", 0), (gain, "row", 0), (dy, "tile", 0)])
    norm_terms = [(dg, wgT, "nn", 0), (du, wuT, "nn", 0)]
    if earlier is None:
        up = _dw_pair(tag + "_dw_up", du, n, 1.0)
        (dx, dxb, dgain), ((slots_g,),) = _mm(tag + "_d_norm", norm_terms, [F32, BF], comm=[_chip_task([sum_g])], **norm_args)
    else:
        sum_u, ((slots_g,),) = _dw_pair(tag + "_dw_up", du, n, 1.0, comm=[_chip_task([sum_g])])
        (dx, dxb, dgain), ((up,),) = _mm(tag + "_d_norm", norm_terms, [F32, BF], comm=[_chip_task([sum_u])], **norm_args)
    return dx, dxb, dgain, slots_e, slots_g, up, slots_d


def _tile_gain(g):
    return jnp.concatenate([g, g]).reshape(1, LANES)


def _fold_heads(partials):
    return jnp.sum(partials.reshape(-1, HEAD_DIM), axis=0)


def _pack_small_grads(grads, loss_local):
    pieces, row = [], 0
    for name, r0, _ in SMALL_LAYOUT + (("loss", LOSS_ROW, None),):
        v = (loss_local if name == "loss" else grads[name]).reshape(-1)
        rows = -(-v.size // LANES)
        block = jnp.pad(v, (0, rows * LANES - v.size)).reshape(rows, LANES)
        pieces += [jnp.zeros((r0 - row, LANES), F32)] * (r0 > row) + [block]
        row = r0 + rows
    pieces.append(jnp.zeros((SMALL_ROWS - row, LANES), F32))
    return jnp.concatenate(pieces, axis=0)


def kernel(x, ffn1_norm, ffn1_w_gate, ffn1_w_up, ffn1_w_down, mix_norm, w_in, pool_w, pool_scale, w_pool_out, q_norm, k_norm, sinks, w_attn_out, gate_bias, w_out, ffn2_norm, ffn2_w_gate, ffn2_w_up, ffn2_w_down, loss_target, m_ffn1_norm, m_ffn1_w_gate, m_ffn1_w_up, m_ffn1_w_down, m_mix_norm, m_w_in, m_pool_w, m_pool_scale, m_w_pool_out, m_q_norm, m_k_norm, m_sinks, m_w_attn_out, m_gate_bias, m_w_out, m_ffn2_norm, m_ffn2_w_gate, m_ffn2_w_up, m_ffn2_w_down, v_ffn1_norm, v_ffn1_w_gate, v_ffn1_w_up, v_ffn1_w_down, v_mix_norm, v_w_in, v_pool_w, v_pool_scale, v_w_pool_out, v_q_norm, v_k_norm, v_sinks, v_w_attn_out, v_gate_bias, v_w_out, v_ffn2_norm, v_ffn2_w_gate, v_ffn2_w_up, v_ffn2_w_down):
    T = x.shape[1]
    x2 = x.reshape(T, D_MODEL)
    target = loss_target.reshape(T, D_MODEL)

    big = [
        ("ffn1_w_gate", ffn1_w_gate, m_ffn1_w_gate, v_ffn1_w_gate, True, False),
        ("ffn1_w_up", ffn1_w_up, m_ffn1_w_up, v_ffn1_w_up, True, False),
        ("ffn1_w_down", ffn1_w_down, m_ffn1_w_down, v_ffn1_w_down, False, False),
        ("w_in", w_in, m_w_in, v_w_in, True, False),
        ("w_pool_out", w_pool_out, m_w_pool_out, v_w_pool_out, False, True),
        ("w_attn_out", w_attn_out, m_w_attn_out, v_w_attn_out, False, False),
        ("w_out", w_out, m_w_out, v_w_out, False, False),
        ("ffn2_w_gate", ffn2_w_gate, m_ffn2_w_gate, v_ffn2_w_gate, True, False),
        ("ffn2_w_up", ffn2_w_up, m_ffn2_w_up, v_ffn2_w_up, True, False),
        ("ffn2_w_down", ffn2_w_down, m_ffn2_w_down, v_ffn2_w_down, False, False),
    ]
    view = lambda a, tv: a.T if tv else a
    views = [view(w, tv) for _, w, _, _, tv, _ in big]
    in_kernel_t = [tk_ for *_, tk_ in big]
    first_shards = _prep("prep_ffn1_gate_up", views[0:2], in_kernel_t[0:2])
    g1 = ffn1_norm.reshape(1, D_MODEL)
    g2 = mix_norm.reshape(1, D_MODEL)
    g3 = ffn2_norm.reshape(1, D_MODEL)
    bias_row = gate_bias.reshape(1, 2 * D_MODEL)
    qg, kg = _tile_gain(q_norm) * ATTN_SCALE, _tile_gain(k_norm)
    scale_row = pool_scale.reshape(1, POOL_WIDTH)
    band_bias = _band_bias()

    n1, later_shards, ((wg1T, wu1T),) = _rms_fwd(
        "ffn1_norm", x2, g1, [_gather_task(first_shards, forward_at=0.9)], views[2:], in_kernel_t[2:])
    shards = list(first_shards) + later_shards
    (gt1, up1, act1), ((wd1,), (w_inT,)) = _mm(
        "ffn1_gate_up", [(n1, wg1T, "nt", 0), (n1, wu1T, "nt", 1)], [BF, BF, BF], tm=1024, tn=1408, tk=D_MODEL,
        epilogue=_swiglu_fwd_epilogue, cols_outer=True,
        comm=[_gather_task(shards[2:3], forward_at=0.5), _gather_task(shards[3:4], natural=(0,), forward_at=0.9)])
    (h1, u), ((w_poT, w_ao, w_o),) = _mm(
        "ffn1_down", [(act1, wd1, "nn", 0)], [F32, BF], tm=512, tn=D_MODEL, tk=D_FF,
        epilogue=_residual_norm_epilogue(0.5), extras=[(x2, "tile", 0), (g2, "row", 0)],
        comm=[_gather_task(shards[4:7], natural=(0, 1, 2), forward_at=0.8)])
    saved1 = (n1, gt1, up1, act1)
    (proj,), ((wg2T,),) = _mm(
        "in_proj", [(u, w_inT, "nt", 0)], [BF], tm=1024, tn=1280, tk=D_MODEL, cols_outer=True,
        comm=[_gather_task(shards[7:8], forward_at=0.8)])
    pooled, mixed = _pool_fwd("pool_fwd", proj, pool_w, scale_row)
    (attn, qn, kn), ((wu2T,),) = _attn_fwd("attn_fwd", proj, qg, kg, sinks, band_bias,
                                           [_gather_task(shards[8:9], forward_at=0.8)])
    (bp,) = _mm("pool_out", [(mixed, w_poT, "nt", 0)], [BF], tm=1024, tn=D_MODEL, tk=POOL_WIDTH)
    gate_tn = 256
    gate_extras = [(proj, "tile", COL_GP // gate_tn), (proj, "tile", COL_GA // gate_tn),
                   (bias_row, "row", 0), (bias_row, "row", D_MODEL // gate_tn)]
    merged, ba = _mm("attn_out_merge", [(attn, w_ao, "nn", 0)], [BF, BF], tm=2048, tn=gate_tn, tk=ATTN_WIDTH,
                     epilogue=_merge_fwd_epilogue, extras=[(bp, "tile", 0)] + gate_extras)
    h2, n2 = _mm("mix_out", [(merged, w_o, "nn", 0)], [F32, BF], tm=1024, tn=D_MODEL, tk=D_MODEL,
                 epilogue=_residual_norm_epilogue(1.0), extras=[(h1, "tile", 0), (g3, "row", 0)])
    (gt2, up2, act2), ((wd2,),) = _mm(
        "ffn2_gate_up", [(n2, wg2T, "nt", 0), (n2, wu2T, "nt", 1)], [BF, BF, BF], tm=1024, tn=1408, tk=D_MODEL,
        epilogue=_swiglu_fwd_epilogue, cols_outer=True, comm=[_gather_task(shards[9:10], forward_at=0.8)])
    dy, dyb, sq = _mm("ffn2_down_loss", [(act2, wd2, "nn", 0)], [F32, BF], tm=512, tn=D_MODEL, tk=D_FF,
                      epilogue=_loss_epilogue, extras=[(h2, "tile", 0), (target, "tile", 0)], n_colsum=1)
    loss_local = 0.5 * jnp.sum(sq) / D_MODEL

    dh2, dh2b, dg3, _, slots_g2, sum_u2, slots_d2 = _ffn_bwd(
        "ffn2", dy, dyb, h2, g3, wg2T, wu2T, wd2, (n2, gt2, up2, act2))
    (dbp, dba, dproj, dga, cs_gp, cs_ga), ((slots_u2,),) = _mm(
        "mix_out_bwd", [(dh2b, w_o, "nt", 0)], [BF, BF, BF, BF], tm=2048, tn=gate_tn, tk=D_MODEL,
        epilogue=_merge_bwd_epilogue, extras=[(bp, "tile", 0), (ba, "tile", 0)] + gate_extras, n_colsum=2,
        out_placement={2: (IN_WIDTH, COL_GP)}, comm=[_chip_task([sum_u2])])
    sum_o = _dw_pair("dw_out", merged, dh2b, 1.0, blocks=4)
    (dmixed,) = _mm("pool_out_bwd", [(dbp, w_poT, "nn", 0)], [BF], tm=1024, tn=POOL_WIDTH, tk=D_MODEL)
    sum_po = _dw_pair("dw_pool_out", dbp, mixed, 1.0, blocks=4)
    (dattn,) = _mm("attn_out_bwd", [(dba, w_ao, "nt", 0)], [BF], tm=1024, tn=ATTN_WIDTH, tk=D_MODEL)
    sum_ao = _dw_pair("dw_attn_out", attn, dba, 1.0, blocks=4)
    (dqn, k_own, k_before, v_own, v_before, dsink_tile), ((slots_o, slots_po, slots_ao),) = _attn_bwd(
        "attn_bwd", dattn, qn, kn, proj, sinks, band_bias, [_chip_task([sum_o, sum_po, sum_ao])])
    next_block = lambda a: jnp.concatenate([a[BLOCK:], jnp.zeros((BLOCK, KV_WIDTH), F32)], axis=0)
    dkn = (k_own + next_block(k_before)).astype(BF)
    dv = (v_own + next_block(v_before)).astype(BF)
    dproj, dqg = _headnorm_bwd("q_norm_bwd", dqn, proj, COL_Q, ATTN_WIDTH, qg, dproj)
    dproj, dkg = _headnorm_bwd("k_norm_bwd", dkn, proj, COL_K, KV_WIDTH, kg, dproj)
    dproj, dpool_w, dpool_scale = _pool_bwd("pool_bwd", dmixed, pooled, pool_w, scale_row, dproj)
    for piece, col in ((dv, COL_V), (dga, COL_GA)):
        dproj = lax.dynamic_update_slice(dproj, piece, (0, col))
    (dh1, dh1b, dg2), ((g_pool_w,),) = _mm(
        "in_proj_bwd", [(dproj, w_inT, "nn", 0)], [F32, BF], tm=512, tn=D_MODEL, tk=IN_WIDTH, epilogue=_rms_bwd_epilogue,
        extras=[(h1, "tile", 0), (g2, "row", 0), (dh2, "tile", 0)], n_colsum=1,
        comm=[_gather_task([dpool_w.reshape(-1, LANES)])])
    (dw_inT,) = _mm("dw_in", [(dproj, u, "tn", 0)], [BF], tm=1920, tn=D_MODEL, tk=2048)
    dx, _, dg1, slots_in, slots_g1, slots_u1, slots_d1 = _ffn_bwd(
        "ffn1", dh1, dh1b, x2, g1, wg1T, wu1T, wd1, saved1, dw_inT.reshape(4, 2, IN_WIDTH // N_DEV, D_MODEL))

    slots = [slots_g1, slots_u1, slots_d1, slots_in, slots_po, slots_ao, slots_o, slots_g2, slots_u2, slots_d2]
    big_out = {}
    for label, group in (("ffn", (0, 1, 2, 7, 8, 9)), ("w_in", (3,)), ("w_pool_out", (4,)), ("attn_out_and_out", (5, 6))):
        items = [(slots[k], view(big[k][1], big[k][4]), view(big[k][2], big[k][4]), view(big[k][3], big[k][4]))
                 for k in group]
        for k, res in zip(group, _adamw_sharded("adamw_" + label, items, transpose=big[group[0]][5])):
            big_out[big[k][0]] = tuple(view(r, big[k][4]) for r in res)

    small_grads = {
        "ffn1_norm": jnp.sum(dg1, axis=(0, 1)), "mix_norm": jnp.sum(dg2, axis=(0, 1)), "ffn2_norm": jnp.sum(dg3, axis=(0, 1)),
        "gate_bias": jnp.concatenate([jnp.sum(cs_gp, axis=(0, 1)), jnp.sum(cs_ga, axis=(0, 1))]),
        "pool_scale": dpool_scale, "q_norm": _fold_heads(dqg) * ATTN_SCALE, "k_norm": _fold_heads(dkg),
        "sinks": dsink_tile[0, :N_HEADS]}
    ((g_vec,),) = _comm_only("gather_small_grads", [_direct_gather_task([_pack_small_grads(small_grads, loss_local)])])
    given = {"ffn1_norm": (ffn1_norm, m_ffn1_norm, v_ffn1_norm), "mix_norm": (mix_norm, m_mix_norm, v_mix_norm),
             "ffn2_norm": (ffn2_norm, m_ffn2_norm, v_ffn2_norm), "gate_bias": (gate_bias, m_gate_bias, v_gate_bias),
             "pool_scale": (pool_scale, m_pool_scale, v_pool_scale), "q_norm": (q_norm, m_q_norm, v_q_norm),
             "k_norm": (k_norm, m_k_norm, v_k_norm), "sinks": (sinks, m_sinks, v_sinks)}
    params = [tuple(a.reshape(shape) for a in given[nm]) for nm, _, shape in SMALL_LAYOUT]
    params.append(tuple(a.reshape(-1, LANES) for a in (pool_w, m_pool_w, v_pool_w)))
    small_res, loss_row = _adamw_small("adamw_small", g_vec.reshape(N_DEV, SMALL_ROWS, LANES),
                                       g_pool_w.reshape(N_DEV, -1, LANES), params)
    small_out = {nm: tuple(r.reshape(given[nm][0].shape) for r in res)
                 for (nm, _, _), res in zip(SMALL_LAYOUT, small_res)}
    small_out["pool_w"] = tuple(r.reshape(pool_w.shape) for r in small_res[-1])
    loss = loss_row[0, 0]

    order = ["ffn1_norm", "ffn1_w_gate", "ffn1_w_up", "ffn1_w_down", "mix_norm", "w_in", "pool_w", "pool_scale",
             "w_pool_out", "q_norm", "k_norm", "sinks", "w_attn_out", "gate_bias", "w_out", "ffn2_norm",
             "ffn2_w_gate", "ffn2_w_up", "ffn2_w_down"]
    every = {**big_out, **small_out}
    outs = [loss, dx.reshape(x.shape)]
    for j in range(4):
        outs += [every[nm][j] for nm in order]
    return tuple(outs)
```

```python
import functools

import jax
import jax.numpy as jnp
from jax import lax
from jax.experimental import pallas as pl
from jax.experimental.pallas import tpu as pltpu

BF = jnp.bfloat16
F32 = jnp.float32

D_MODEL = 1024
D_FF = 2816
POOL_WIDTH = 512
POOL_GROUP = 128
N_POOL_GROUPS = 4
HEAD_DIM = 64
N_HEADS = 16
GQA_GROUP = 8
BLOCK = 128
ATTN_WIDTH = 1024
KV_WIDTH = 128
IN_WIDTH = 3840
RMS_EPS = 1e-6
N_DEV = 8
LANES = 128

COL_Q = POOL_WIDTH
COL_K = COL_Q + ATTN_WIDTH
COL_V = COL_K + KV_WIDTH
COL_GP = COL_V + KV_WIDTH
COL_GA = COL_GP + D_MODEL

ADAM_LR = 0.001
ADAM_B1 = 0.9
ADAM_B2 = 0.999
ADAM_EPS = 1e-08
ADAM_WD = 0.01
ADAM_STEP = 10

VMEM_LIMIT_V7X = 56 * 1024 * 1024
MESH = pl.DeviceIdType.MESH
ANY = pl.BlockSpec(memory_space=pl.ANY)


def _params(sem=None, collective_id=None):
    return pltpu.CompilerParams(dimension_semantics=sem, vmem_limit_bytes=VMEM_LIMIT_V7X, collective_id=collective_id)


COLLECTIVE_IDS = {frozenset(["sibling"]): 0, frozenset(["chips"]): 1, frozenset(["sibling", "chips"]): 2}


def _handshake(peer_kinds):
    x, y, c, chips = _place()
    peers = ([(x, y, 1 - c)] if "sibling" in peer_kinds else []) + ([(*chip, c) for chip in chips] if "chips" in peer_kinds else [])
    barrier = pltpu.get_barrier_semaphore()
    for peer in peers:
        pl.semaphore_signal(barrier, inc=1, device_id=peer, device_id_type=MESH)
    pl.semaphore_wait(barrier, len(peers))


_DIMS = {"nt": (((1,), (1,)), ((), ())), "nn": (((1,), (0,)), ((), ())), "tn": (((0,), (0,)), ((), ()))}


class _Task:
    def __init__(self, inputs, out_shapes, scratch, phases, peers):
        self.inputs, self.out_shapes, self.scratch = list(inputs), list(out_shapes), list(scratch)
        self.phases = list(phases)
        self.peers = frozenset(peers)


class _CommPlumbing:
    def __init__(self, tasks):
        self.tasks = list(tasks or [])
        self.args = [a for t in self.tasks for a in t.inputs]
        self.out_shapes = [o for t in self.tasks for o in t.out_shapes]
        self.scratch = [s for t in self.tasks for s in t.scratch]
        self.n_in, self.n_out = len(self.args), len(self.out_shapes)

    def peer_kinds(self, own=()):
        kinds = frozenset(own).union(*[t.peers for t in self.tasks])
        return None if "all" in kinds or not kinds else kinds

    def collective_id(self, own=()):
        kinds = self.peer_kinds(own)
        return None if kinds is None else COLLECTIVE_IDS[kinds]

    def handshake(self, first, own=()):
        kinds = self.peer_kinds(own)
        if kinds is not None:
            pl.when(first)(functools.partial(_handshake, kinds))

    def _slices(self, c_in, c_out, c_scr):
        i = o = s = 0
        for t in self.tasks:
            yield t, c_in[i:i + len(t.inputs)], c_out[o:o + len(t.out_shapes)], c_scr[s:s + len(t.scratch)]
            i, o, s = i + len(t.inputs), o + len(t.out_shapes), s + len(t.scratch)

    def run(self, step, steps, before, c_in, c_out, c_scr):
        for t, ins, outs, scr in self._slices(c_in, c_out, c_scr):
            for frac, fn in t.phases:
                if step is None:
                    fn(ins, outs, scr)
                elif before == (frac == 0):
                    at = 0 if frac == 0 else max(0, min(steps, -(-int(round(frac * steps * 64)) // 64)) - 1)
                    pl.when(step == at)(functools.partial(fn, ins, outs, scr))

    def split_outputs(self, flat):
        res, o = [], 0
        for t in self.tasks:
            res.append(list(flat[o:o + len(t.out_shapes)]))
            o += len(t.out_shapes)
        return res


def _comm_only(name, tasks):
    plumb = _CommPlumbing(tasks)

    def body(*refs):
        c_in, c_out = refs[:plumb.n_in], refs[plumb.n_in: plumb.n_in + plumb.n_out]
        c_scr = refs[plumb.n_in + plumb.n_out:]
        plumb.run(None, 1, True, c_in, c_out, c_scr)

    res = pl.pallas_call(
        body, name=name, in_specs=[ANY] * plumb.n_in, out_specs=[ANY] * plumb.n_out, out_shape=plumb.out_shapes,
        scratch_shapes=plumb.scratch, compiler_params=pltpu.CompilerParams(has_side_effects=True),
    )(*plumb.args)
    return plumb.split_outputs(res)


def _mm(name, terms, out_dtypes, *, tm, tn, tk, epilogue=None, extras=(), n_colsum=0, comm=None, cols_outer=False,
        out_placement=None):
    a0, b0, mode0, _ = terms[0]
    if mode0 == "nt":
        (M, K), N = a0.shape, b0.shape[0]
    elif mode0 == "nn":
        (M, K), N = a0.shape, b0.shape[1]
    else:
        (K, M), N = a0.shape, b0.shape[1]
    tm, tn, tk = min(tm, M), min(tn, N), min(tk, K)
    assert M % tm == 0 and N % tn == 0 and K % tk == 0, (name, M, N, K, tm, tn, tk)
    nI, nJ, nK = M // tm, N // tn, K // tk
    n_terms = len(terms)
    n_acc = max(t[3] for t in terms) + 1
    n_ex = len(extras)
    n_out = len(out_dtypes)
    if epilogue is None:
        epilogue = lambda accs, ex: ([accs[0]], [])
    plumb = _CommPlumbing(comm)
    n_scr = n_acc if nK > 1 else 0
    grid = (nJ, nI, nK) if cols_outer else (nI, nJ, nK)

    def body(*refs):
        n_in = 2 * n_terms + n_ex
        ab = refs[: 2 * n_terms]
        ex_refs = refs[2 * n_terms: n_in]
        c_in = refs[n_in: n_in + plumb.n_in]
        o0 = n_in + plumb.n_in
        out_refs = refs[o0: o0 + n_out]
        cs_refs = refs[o0 + n_out: o0 + n_out + n_colsum]
        c_out = refs[o0 + n_out + n_colsum: o0 + n_out + n_colsum + plumb.n_out]
        s0 = o0 + n_out + n_colsum + plumb.n_out
        acc_refs = refs[s0: s0 + n_scr]
        c_scr = refs[s0 + n_scr:]
        steps = grid[0] * grid[1] * nK
        if comm:
            step = (pl.program_id(0) * grid[1] + pl.program_id(1)) * nK + pl.program_id(2)
            plumb.handshake(step == 0)
            plumb.run(step, steps, True, c_in, c_out, c_scr)

        def products():
            accs = [None] * n_acc
            for t, (_, _, mode, ai) in enumerate(terms):
                p = lax.dot_general(ab[2 * t][...], ab[2 * t + 1][...], _DIMS[mode], preferred_element_type=F32)
                accs[ai] = p if accs[ai] is None else accs[ai] + p
            return accs

        def finish(accs):
            outs, colsums = epilogue(accs, [r[...] for r in ex_refs])
            for r, o in zip(out_refs, outs):
                r[...] = o.astype(r.dtype)
            for r, cs in zip(cs_refs, colsums):
                r[...] = jnp.sum(cs, axis=0, keepdims=True).reshape(r.shape)

        if nK == 1:
            finish(products())
        else:
            k = pl.program_id(2)
            accs = products()

            @pl.when(k == 0)
            def _():
                for r, a in zip(acc_refs, accs):
                    r[...] = a

            @pl.when(k > 0)
            def _():
                for r, a in zip(acc_refs, accs):
                    r[...] += a

            @pl.when(k == nK - 1)
            def _():
                finish([r[...] for r in acc_refs])

        if comm:
            plumb.run(step, steps, False, c_in, c_out, c_scr)

    def spec(block, index, fixed=False):
        imap = (lambda q, p, k: index(p, q, k)) if cols_outer else index
        return pl.BlockSpec(block, imap, pipeline_mode=pl.Buffered(1)) if fixed else pl.BlockSpec(block, imap)

    in_specs, args = [], []
    for a, b, mode, _ in terms:
        kt = tk if nK > 1 else (a.shape[0] if mode == "tn" else a.shape[1])
        if mode == "nt":
            in_specs += [spec((tm, kt), lambda i, j, k: (i, k), nI * nK == 1),
                         spec((tn, kt), lambda i, j, k: (j, k), nJ * nK == 1)]
        elif mode == "nn":
            in_specs += [spec((tm, kt), lambda i, j, k: (i, k), nI * nK == 1),
                         spec((kt, tn), lambda i, j, k: (k, j), nJ * nK == 1)]
        else:
            in_specs += [spec((kt, tm), lambda i, j, k: (k, i), nI * nK == 1),
                         spec((kt, tn), lambda i, j, k: (k, j), nJ * nK == 1)]
        args += [a, b]
    for arr, kind, off in extras:
        if kind == "tile":
            in_specs.append(spec((tm, tn), functools.partial(lambda i, j, k, off: (i, j + off), off=off)))
        else:
            in_specs.append(spec((1, tn), functools.partial(lambda i, j, k, off: (0, j + off), off=off)))
        args.append(arr)
    placed = dict(out_placement or {})
    out_shape = [jax.ShapeDtypeStruct((M, placed.get(o, (N, 0))[0]), dt) for o, dt in enumerate(out_dtypes)]
    out_specs = [spec((tm, tn), functools.partial(lambda i, j, k, off: (i, j + off), off=placed.get(o, (N, 0))[1] // tn))
                 for o in range(n_out)]
    out_shape += [jax.ShapeDtypeStruct((nI, 1, N), F32) for _ in range(n_colsum)]
    out_specs += [spec((1, 1, tn), lambda i, j, k: (i, 0, j)) for _ in range(n_colsum)]
    scratch = [pltpu.VMEM((tm, tn), F32) for _ in range(n_scr)]
    args += plumb.args
    in_specs += [ANY] * plumb.n_in
    out_shape += plumb.out_shapes
    out_specs += [ANY] * plumb.n_out
    sem = ("arbitrary",) * 3 if comm else ("parallel", "parallel", "arbitrary")
    res = pl.pallas_call(
        body, name=name, grid=grid, in_specs=in_specs, out_specs=out_specs, out_shape=out_shape,
        scratch_shapes=scratch + plumb.scratch, compiler_params=_params(sem, plumb.collective_id()),
    )(*args)
    n_own = n_out + n_colsum
    return (list(res[:n_own]), plumb.split_outputs(res[n_own:])) if comm is not None else res


ROW_TILE = 512


def _rms_fwd(name, x, g, comm, weights, transposes):
    T, D = x.shape
    steps = T // ROW_TILE
    plumb = _CommPlumbing(comm)
    nw = len(weights)

    def body(x_ref, g_ref, *rest):
        w_refs, c_in = rest[:nw], rest[nw: nw + plumb.n_in]
        o_ref, shard_refs = rest[nw + plumb.n_in], rest[nw + plumb.n_in + 1: 2 * nw + plumb.n_in + 1]
        c_out = rest[2 * nw + plumb.n_in + 1: 2 * nw + plumb.n_in + 1 + plumb.n_out]
        c_scr = rest[2 * nw + plumb.n_in + 1 + plumb.n_out:]
        plumb.handshake(pl.program_id(0) == 0)
        plumb.run(pl.program_id(0), steps, True, c_in, c_out, c_scr)

        @pl.when(pl.program_id(0) == 0)
        def _():
            for w_ref, s_ref, tr in zip(w_refs, shard_refs, transposes):
                v = w_ref[...]
                s_ref[...] = (v.T if tr else v).astype(BF)

        xv = x_ref[...]
        r = lax.rsqrt(jnp.mean(xv * xv, axis=-1, keepdims=True) + RMS_EPS)
        o_ref[...] = (xv * r * g_ref[...]).astype(BF)
        plumb.run(pl.program_id(0), steps, False, c_in, c_out, c_scr)

    row = pl.BlockSpec((ROW_TILE, D), lambda i: (i, 0))
    whole = lambda shape: pl.BlockSpec(shape, lambda i: (0, 0), pipeline_mode=pl.Buffered(1))
    shard_shapes = [w.shape[::-1] if tr else w.shape for w, tr in zip(weights, transposes)]
    res = pl.pallas_call(
        body, name=name, grid=(steps,),
        in_specs=[row, pl.BlockSpec((1, D), lambda i: (0, 0))] + [whole(w.shape) for w in weights] + [ANY] * plumb.n_in,
        out_specs=[row] + [whole(s) for s in shard_shapes] + [ANY] * plumb.n_out,
        out_shape=[jax.ShapeDtypeStruct((T, D), BF)] + [jax.ShapeDtypeStruct(s, BF) for s in shard_shapes] + plumb.out_shapes,
        scratch_shapes=plumb.scratch, compiler_params=_params(("arbitrary",), plumb.collective_id()),
    )(x, g, *weights, *plumb.args)
    return res[0], list(res[1: nw + 1]), plumb.split_outputs(res[nw + 1:])


HEADNORM_TILE = 2048


def _half_sum_matrix():
    r = lax.broadcasted_iota(jnp.int32, (LANES, LANES), 0) // HEAD_DIM
    c = lax.broadcasted_iota(jnp.int32, (LANES, LANES), 1) // HEAD_DIM
    return (r == c).astype(BF)


def _head_mean(v, ones_blockdiag):
    hi = v.astype(BF)
    lo = (v - hi.astype(F32)).astype(BF)
    s = jnp.dot(hi, ones_blockdiag, preferred_element_type=F32) + jnp.dot(lo, ones_blockdiag, preferred_element_type=F32)
    return s * (1.0 / HEAD_DIM)


def _headnorm_bwd(name, dy, proj, col0, width, g2, into):
    T = proj.shape[0]
    wide = min(width, GROUP_WIDTH)
    nb, off = width // wide, col0 // wide

    def body(dy_ref, x_ref, g_ref, b_ref, into_ref, dx_ref, dg_ref):
        for s in range(wide // LANES):
            lanes = slice(LANES * s, LANES * (s + 1))
            xv = x_ref[:, lanes].astype(F32)
            dyv = dy_ref[:, lanes].astype(F32)
            r = lax.rsqrt(_head_mean(xv * xv, b_ref[...]) + RMS_EPS)
            xhat = xv * r
            dxhat = dyv * g_ref[...]
            dx_ref[:, lanes] = (r * (dxhat - xhat * _head_mean(dxhat * xhat, b_ref[...]))).astype(BF)
            dg_ref[0, :, lanes] = jnp.sum(dyv * xhat, axis=0, keepdims=True)

    return pl.pallas_call(
        body, name=name, grid=(T // HEADNORM_TILE, nb),
        in_specs=[pl.BlockSpec((HEADNORM_TILE, wide), lambda i, j: (i, j)),
                  pl.BlockSpec((HEADNORM_TILE, wide), lambda i, j: (i, j + off)),
                  pl.BlockSpec((1, LANES), lambda i, j: (0, 0)), pl.BlockSpec((LANES, LANES), lambda i, j: (0, 0)), ANY],
        out_specs=[pl.BlockSpec((HEADNORM_TILE, wide), lambda i, j: (i, j + off)),
                   pl.BlockSpec((1, 1, wide), lambda i, j: (i, 0, j))],
        out_shape=[jax.ShapeDtypeStruct(into.shape, BF), jax.ShapeDtypeStruct((T // HEADNORM_TILE, 1, width), F32)],
        input_output_aliases={4: 0}, compiler_params=_params(("parallel", "parallel")),
    )(dy, proj, g2, _half_sum_matrix(), into)


def _shift_down(v, k, row):
    return jnp.where(row >= k, pltpu.roll(v, k, axis=0), 0.0)


def _shift_up(v, k, row, T):
    return jnp.where(row < T - k, pltpu.roll(v, T - k, axis=0), 0.0)


def _by_group(g, vals):
    out = vals[-1]
    for i in range(len(vals) - 2, -1, -1):
        out = jnp.where(g == i, vals[i], out)
    return out


def _pool_fwd(name, proj, pool_w, pool_scale):
    T = proj.shape[0]

    def body(x_ref, w_ref, s_ref, pooled_ref, mixed_ref):
        g = pl.program_id(0)
        xv = x_ref[...].astype(F32)
        row = lax.broadcasted_iota(jnp.int32, (T, 1), 0)
        s2 = xv + _shift_down(xv, 1, row)
        s4 = s2 + _shift_down(s2, 2, row)
        s8 = s4 + _shift_down(s4, 4, row)
        s16 = s8 + _shift_down(s8, 8, row)
        wsum = _by_group(g, [s2, s4, s8, s16])
        count = jnp.minimum(row + 1, 2 << g).astype(F32)
        pooled = (wsum / count - xv).astype(BF)
        pooled_ref[...] = pooled
        mixed = jnp.dot(pooled, w_ref[0].astype(BF), preferred_element_type=F32) * s_ref[...]
        mixed_ref[...] = mixed.astype(BF)

    col = pl.BlockSpec((T, POOL_GROUP), lambda g: (0, g))
    return pl.pallas_call(
        body, name=name, grid=(N_POOL_GROUPS,),
        in_specs=[col, pl.BlockSpec((1, POOL_GROUP, POOL_GROUP), lambda g: (g, 0, 0)),
                  pl.BlockSpec((1, POOL_GROUP), lambda g: (0, g))],
        out_specs=[col, col],
        out_shape=[jax.ShapeDtypeStruct((T, POOL_WIDTH), BF), jax.ShapeDtypeStruct((T, POOL_WIDTH), BF)],
        compiler_params=_params(("parallel",)),
    )(proj, pool_w, pool_scale)


def _pool_bwd(name, dmixed, pooled, pool_w, pool_scale, into):
    T = dmixed.shape[0]

    def body(dm_ref, p_ref, w_ref, s_ref, into_ref, dx_ref, dw_ref, ds_ref):
        g = pl.program_id(0)
        dm = dm_ref[...].astype(F32)
        pooled = p_ref[...]
        w = w_ref[0].astype(BF)
        pre = jnp.dot(pooled, w, preferred_element_type=F32)
        ds_ref[...] = jnp.sum(dm * pre, axis=0, keepdims=True)
        dms = (dm * s_ref[...]).astype(BF)
        dw_ref[0] = lax.dot_general(pooled, dms, _DIMS["tn"], preferred_element_type=F32)
        dpooled = lax.dot_general(dms, w, _DIMS["nt"], preferred_element_type=F32)
        row = lax.broadcasted_iota(jnp.int32, (T, 1), 0)
        count = jnp.minimum(row + 1, 2 << g).astype(F32)
        z = dpooled / count
        l2 = z + _shift_up(z, 1, row, T)
        l4 = l2 + _shift_up(l2, 2, row, T)
        l8 = l4 + _shift_up(l4, 4, row, T)
        l16 = l8 + _shift_up(l8, 8, row, T)
        dx_ref[...] = (_by_group(g, [l2, l4, l8, l16]) - dpooled).astype(BF)

    col = pl.BlockSpec((T, POOL_GROUP), lambda g: (0, g))
    wspec = pl.BlockSpec((1, POOL_GROUP, POOL_GROUP), lambda g: (g, 0, 0))
    sspec = pl.BlockSpec((1, POOL_GROUP), lambda g: (0, g))
    return pl.pallas_call(
        body, name=name, grid=(N_POOL_GROUPS,), in_specs=[col, col, wspec, sspec, ANY], out_specs=[col, wspec, sspec],
        out_shape=[jax.ShapeDtypeStruct(into.shape, BF),
                   jax.ShapeDtypeStruct((N_POOL_GROUPS, POOL_GROUP, POOL_GROUP), F32),
                   jax.ShapeDtypeStruct((1, POOL_WIDTH), F32)],
        input_output_aliases={4: 0}, compiler_params=_params(("parallel",)),
    )(dmixed, pooled, pool_w, pool_scale, into)


ATTN_SCALE = HEAD_DIM ** -0.5
MASKED = float(jnp.finfo(jnp.float32).min)
KV_COL_BLOCK_V = COL_V // LANES
GROUP_WIDTH = GQA_GROUP * HEAD_DIM


def _dup_head(v, j):
    half = lax.broadcasted_iota(jnp.int32, (1, LANES), 1) // HEAD_DIM
    return jnp.where(half == j, v, pltpu.roll(v, HEAD_DIM, axis=1))


def _stack_heads(v, low):
    pieces = []
    for p in range(GROUP_WIDTH // LANES):
        vp = v[:, LANES * p: LANES * (p + 1)]
        pieces.append(jnp.where(low, vp, jnp.zeros_like(vp)))
        pieces.append(jnp.where(low, jnp.zeros_like(vp), vp))
    return jnp.concatenate(pieces, axis=0)


def _unstack_transposed(t, low):
    pairs = []
    for p in range(GROUP_WIDTH // LANES):
        even = t[:, BLOCK * (2 * p): BLOCK * (2 * p + 1)].T
        odd = t[:, BLOCK * (2 * p + 1): BLOCK * (2 * p + 2)].T
        pairs.append(jnp.where(low, even, odd))
    return pairs


STACKED = GQA_GROUP * BLOCK


def _band_bias():
    key = lax.broadcasted_iota(jnp.int32, (2, 2 * BLOCK, STACKED), 1)
    qry = lax.broadcasted_iota(jnp.int32, (2, 2 * BLOCK, STACKED), 2) % BLOCK
    first = lax.broadcasted_iota(jnp.int32, (2, 2 * BLOCK, STACKED), 0) == 0
    valid = (key > qry) & (key <= qry + BLOCK) & (jnp.logical_not(first) | (key >= BLOCK))
    return jnp.where(valid, 0.0, MASKED).astype(F32)


def _softmax_keys_on_sublanes(k2, q, bias, sink_ref, j):
    head_of_lane = lax.broadcasted_iota(jnp.int32, (1, STACKED), 1) // BLOCK
    sink = jnp.zeros((1, STACKED), F32)
    for h in range(GQA_GROUP):
        sink = jnp.where(head_of_lane == h, sink_ref[j * GQA_GROUP + h], sink)
    s = lax.dot_general(k2, q, _DIMS["nt"], preferred_element_type=F32) + bias
    m = jnp.maximum(jnp.max(s, axis=0, keepdims=True), sink)
    e = jnp.exp(s - m)
    e_sink = jnp.exp(sink - m)
    inv = 1.0 / (jnp.sum(e, axis=0, keepdims=True) + e_sink)
    return e * inv, e_sink * inv


def _attn_fwd(name, proj, qg, kg, sinks, bias, comm):
    T = proj.shape[0]
    nb = T // BLOCK
    plumb = _CommPlumbing(comm)

    def body(sink_ref, bias_ref, ones_ref, qg_ref, kg_ref, q0_ref, q1_ref, kp_ref, kc_ref, vp_ref, vc_ref, *rest):
        c_in, (o_ref, q_ref, kn_ref) = rest[:plumb.n_in], rest[plumb.n_in: plumb.n_in + 3]
        c_out, c_scr = rest[plumb.n_in + 3: plumb.n_in + 3 + plumb.n_out], rest[plumb.n_in + 3 + plumb.n_out:]
        m = pl.program_id(0)
        plumb.handshake(m == 0)
        plumb.run(m, nb // 2, True, c_in, c_out, c_scr)
        low = lax.broadcasted_iota(jnp.int32, (1, LANES), 1) < HEAD_DIM

        def head_norm(raw, gain):
            xv = raw.astype(F32)
            return (xv * lax.rsqrt(_head_mean(xv * xv, ones_ref[...]) + RMS_EPS) * gain).astype(BF)

        for half, raw_ref in enumerate((q0_ref, q1_ref)):
            for s in range(GROUP_WIDTH // LANES):
                q_ref[:, GROUP_WIDTH * half + LANES * s: GROUP_WIDTH * half + LANES * (s + 1)] = head_norm(
                    raw_ref[:, LANES * s: LANES * (s + 1)], qg_ref[...])
        k_pair, k_prev = head_norm(kc_ref[...], kg_ref[...]), head_norm(kp_ref[...], kg_ref[...])
        kn_ref[...] = k_pair
        v_pair = vc_ref[...]
        for b in range(2):
            rows = slice(BLOCK * b, BLOCK * (b + 1))
            kk = k_pair if b else jnp.concatenate([k_prev, k_pair[0:BLOCK]], axis=0)
            vv = v_pair if b else jnp.concatenate([vp_ref[...], v_pair[0:BLOCK]], axis=0)
            bias = bias_ref[1] if b else bias_ref[jnp.minimum(m, 1)]
            for j in range(2):
                q = _stack_heads(q_ref[rows, GROUP_WIDTH * j: GROUP_WIDTH * (j + 1)], low)
                p, _ = _softmax_keys_on_sublanes(_dup_head(kk, j), q, bias, sink_ref, j)
                o_t = lax.dot_general(_dup_head(vv, j), p.astype(BF), _DIMS["tn"], preferred_element_type=F32)
                for pair, o in enumerate(_unstack_transposed(o_t, low)):
                    lanes = slice(GROUP_WIDTH * j + LANES * pair, GROUP_WIDTH * j + LANES * (pair + 1))
                    o_ref[rows, lanes] = o.astype(BF)
        plumb.run(m, nb // 2, False, c_in, c_out, c_scr)

    wide = pl.BlockSpec((2 * BLOCK, ATTN_WIDTH), lambda m: (m, 0))
    before = lambda m: jnp.maximum(2 * m - 1, 0)
    gain = pl.BlockSpec((1, LANES), lambda m: (0, 0))
    q_block, k_block = COL_Q // GROUP_WIDTH, COL_K // LANES
    res = pl.pallas_call(
        body, name=name, grid=(nb // 2,),
        in_specs=[pl.BlockSpec(memory_space=pltpu.SMEM),
                  pl.BlockSpec((2, 2 * BLOCK, STACKED), lambda m: (0, 0, 0)),
                  pl.BlockSpec((LANES, LANES), lambda m: (0, 0)), gain, gain,
                  pl.BlockSpec((2 * BLOCK, GROUP_WIDTH), lambda m: (m, q_block)),
                  pl.BlockSpec((2 * BLOCK, GROUP_WIDTH), lambda m: (m, q_block + 1)),
                  pl.BlockSpec((BLOCK, LANES), lambda m: (before(m), k_block)),
                  pl.BlockSpec((2 * BLOCK, LANES), lambda m: (m, k_block)),
                  pl.BlockSpec((BLOCK, LANES), lambda m: (before(m), KV_COL_BLOCK_V)),
                  pl.BlockSpec((2 * BLOCK, LANES), lambda m: (m, KV_COL_BLOCK_V))] + [ANY] * plumb.n_in,
        out_specs=[wide, wide, pl.BlockSpec((2 * BLOCK, LANES), lambda m: (m, 0))] + [ANY] * plumb.n_out,
        out_shape=[jax.ShapeDtypeStruct((T, ATTN_WIDTH), BF), jax.ShapeDtypeStruct((T, ATTN_WIDTH), BF),
                   jax.ShapeDtypeStruct((T, KV_WIDTH), BF)] + plumb.out_shapes,
        scratch_shapes=plumb.scratch, compiler_params=_params(("arbitrary",), plumb.collective_id()),
    )(sinks, bias, _half_sum_matrix(), qg, kg, proj, proj, proj, proj, proj, proj, *plumb.args)
    return list(res[:3]), plumb.split_outputs(res[3:])


def _attn_bwd(name, dout, qn, kn, proj, sinks, bias, comm):
    T = qn.shape[0]
    nb = T // BLOCK
    plumb = _CommPlumbing(comm)

    def body(sink_ref, bias_ref, do_ref, q_ref, kp_ref, kc_ref, vp_ref, vc_ref, *rest):
        c_in = rest[:plumb.n_in]
        dq_ref, k_own, k_before, v_own, v_before, dsink_ref = rest[plumb.n_in: plumb.n_in + 6]
        c_out, c_scr = rest[plumb.n_in + 6: plumb.n_in + 6 + plumb.n_out], rest[plumb.n_in + 6 + plumb.n_out:]
        m = pl.program_id(0)
        plumb.handshake(m == 0)
        plumb.run(m, nb // 2, True, c_in, c_out, c_scr)
        lane = lax.broadcasted_iota(jnp.int32, (1, LANES), 1)
        low = lane < HEAD_DIM

        @pl.when(m == 0)
        def _():
            dsink_ref[...] = jnp.zeros_like(dsink_ref)

        k_pair, v_pair = kc_ref[...], vc_ref[...]
        dsink = jnp.zeros((1, LANES), F32)
        for b in range(2):
            rows = slice(BLOCK * b, BLOCK * (b + 1))
            kk = k_pair if b else jnp.concatenate([kp_ref[...], k_pair[0:BLOCK]], axis=0)
            vv = v_pair if b else jnp.concatenate([vp_ref[...], v_pair[0:BLOCK]], axis=0)
            bias = bias_ref[1] if b else bias_ref[jnp.minimum(m, 1)]
            dk_tot = jnp.zeros((2 * BLOCK, LANES), F32)
            dv_tot = jnp.zeros((2 * BLOCK, LANES), F32)
            for j in range(2):
                k2 = _dup_head(kk, j)
                v2 = _dup_head(vv, j)
                q = _stack_heads(q_ref[rows, GROUP_WIDTH * j: GROUP_WIDTH * (j + 1)], low)
                do = _stack_heads(do_ref[rows, GROUP_WIDTH * j: GROUP_WIDTH * (j + 1)], low)
                p, psink = _softmax_keys_on_sublanes(k2, q, bias, sink_ref, j)
                dp =lax.dot_general(v2, do, _DIMS["nt"], preferred_element_type=F32)
                delta = jnp.sum(p * dp, axis=0, keepdims=True)
                ds = (p * (dp - delta)).astype(BF)
                dk2 = jnp.dot(ds, q, preferred_element_type=F32)
                dv2 = jnp.dot(p.astype(BF), do, preferred_element_type=F32)
                dq_t = lax.dot_general(k2, ds, _DIMS["tn"], preferred_element_type=F32)
                for pair, dq in enumerate(_unstack_transposed(dq_t, low)):
                    lanes = slice(GROUP_WIDTH * j + LANES * pair, GROUP_WIDTH * j + LANES * (pair + 1))
                    dq_ref[rows, lanes] = dq.astype(BF)
                mine = low if j == 0 else jnp.logical_not(low)
                dk_tot = dk_tot + jnp.where(mine, dk2 + pltpu.roll(dk2, HEAD_DIM, axis=1), 0.0)
                dv_tot = dv_tot + jnp.where(mine, dv2 + pltpu.roll(dv2, HEAD_DIM, axis=1), 0.0)
                sink_term = psink * delta
                for h in range(GQA_GROUP):
                    val = -jnp.sum(sink_term[:, BLOCK * h: BLOCK * (h + 1)], axis=1, keepdims=True)
                    dsink = dsink + jnp.where(lane == j * GQA_GROUP + h, val, 0.0)
            k_before[rows, :], k_own[rows, :] = dk_tot[0:BLOCK], dk_tot[BLOCK:]
            v_before[rows, :], v_own[rows, :] = dv_tot[0:BLOCK], dv_tot[BLOCK:]
        dsink_ref[0:1, :] += dsink
        plumb.run(m, nb // 2, False, c_in, c_out, c_scr)

    wide = pl.BlockSpec((2 * BLOCK, ATTN_WIDTH), lambda m: (m, 0))
    pair = pl.BlockSpec((2 * BLOCK, LANES), lambda m: (m, 0))
    before = lambda m: jnp.maximum(2 * m - 1, 0)
    res = pl.pallas_call(
        body, name=name, grid=(nb // 2,),
        in_specs=[pl.BlockSpec(memory_space=pltpu.SMEM),
                  pl.BlockSpec((2, 2 * BLOCK, STACKED), lambda m: (0, 0, 0)), wide, wide,
                  pl.BlockSpec((BLOCK, LANES), lambda m: (before(m), 0)), pair,
                  pl.BlockSpec((BLOCK, LANES), lambda m: (before(m), KV_COL_BLOCK_V)),
                  pl.BlockSpec((2 * BLOCK, LANES), lambda m: (m, KV_COL_BLOCK_V))] + [ANY] * plumb.n_in,
        out_specs=[wide, pair, pair, pair, pair, pl.BlockSpec((8, LANES), lambda m: (0, 0))] + [ANY] * plumb.n_out,
        out_shape=[jax.ShapeDtypeStruct((T, ATTN_WIDTH), BF)] + [jax.ShapeDtypeStruct((T, KV_WIDTH), F32)] * 4
        + [jax.ShapeDtypeStruct((8, LANES), F32)] + plumb.out_shapes,
        scratch_shapes=plumb.scratch, compiler_params=_params(("arbitrary",), plumb.collective_id()),
    )(sinks, bias, dout, qn, kn, kn, proj, proj, *plumb.args)
    return list(res[:6]), plumb.split_outputs(res[6:])


def _swiglu_fwd_epilogue(accs, ex):
    g, u = accs
    return [g, u, g * jax.nn.sigmoid(g) * u], []


def _swiglu_bwd_epilogue(accs, ex):
    (da,) = accs
    g, u = ex[0].astype(F32), ex[1].astype(F32)
    s = jax.nn.sigmoid(g)
    gs = g * s
    return [da * u * (s + gs - gs * s), da * gs], []


def _residual_norm_epilogue(scale):
    def epilogue(accs, ex):
        res, gain = ex
        h = res + scale * accs[0]
        r = lax.rsqrt(jnp.mean(h * h, axis=-1, keepdims=True) + RMS_EPS)
        return [h, h * r * gain], []
    return epilogue


def _rms_bwd_epilogue(accs, ex):
    (dn,) = accs
    xv, g, dres = ex
    r = lax.rsqrt(jnp.mean(xv * xv, axis=-1, keepdims=True) + RMS_EPS)
    xhat = xv * r
    dxhat = dn * g
    dx = dres + r * (dxhat - xhat * jnp.mean(dxhat * xhat, axis=-1, keepdims=True))
    return [dx, dx], [dn * xhat]


def _loss_epilogue(accs, ex):
    xv, target = ex
    d = xv + 0.5 * accs[0] - target
    dy = d * (1.0 / D_MODEL)
    return [dy, dy], [d * d]


def _merge_fwd_epilogue(accs, ex):
    ba, bp = accs
    gp_pre, ga_pre, bias_p, bias_a = ex
    gp = jax.nn.sigmoid(gp_pre.astype(F32) + bias_p)
    ga = jax.nn.sigmoid(ga_pre.astype(F32) + bias_a)
    return [gp * bp + ga * ba, ba, bp], []


def _merge_bwd_epilogue(accs, ex):
    (dm,) = accs
    bp, ba, gp_pre, ga_pre, bias_p, bias_a = ex
    gp = jax.nn.sigmoid(gp_pre.astype(F32) + bias_p)
    ga = jax.nn.sigmoid(ga_pre.astype(F32) + bias_a)
    dbp, dba = dm * gp, dm * ga
    dgp = dbp * bp.astype(F32) * (1.0 - gp)
    dga = dba * ba.astype(F32) * (1.0 - ga)
    return [dbp, dba, dgp, dga], [dgp, dga]


def _prep(name, ws, transposes):
    n = len(ws)

    def body(*refs):
        for w_ref, o_ref, tr in zip(refs[:n], refs[n:], transposes):
            v = w_ref[...]
            o_ref[...] = (v.T if tr else v).astype(BF)

    shapes = [jax.ShapeDtypeStruct(w.shape[::-1] if tr else w.shape, BF) for w, tr in zip(ws, transposes)]
    return pl.pallas_call(body, name=name, out_shape=shapes, compiler_params=_params())(*ws)


def _adam_math(w, g, m, v):
    m = ADAM_B1 * m + (1.0 - ADAM_B1) * g
    v = ADAM_B2 * v + (1.0 - ADAM_B2) * jnp.square(g)
    m_hat = m / (1.0 - ADAM_B1 ** ADAM_STEP)
    v_hat = v / (1.0 - ADAM_B2 ** ADAM_STEP)
    delta = -ADAM_LR * (m_hat / (jnp.sqrt(v_hat) + ADAM_EPS) + ADAM_WD * w)
    return delta, m, v


def _adamw_sharded(name, items, transpose=False):
    n = len(items)

    def body(*refs):
        ins, outs = refs[:4 * n], refs[4 * n:]
        for k in range(n):
            s_ref, w_ref, m_ref, v_ref = ins[4 * k: 4 * k + 4]
            g = s_ref[0].astype(F32)
            for i in range(1, 4):
                g = g + s_ref[i].astype(F32)
            if transpose:
                g = g.T
            delta, mn, vn = _adam_math(w_ref[...], g, m_ref[...], v_ref[...])
            for o_ref, val in zip(outs[4 * k: 4 * k + 4], (g, delta, mn, vn)):
                o_ref[...] = val

    flat = [a for item in items for a in item]
    out_shape = [jax.ShapeDtypeStruct(item[1].shape, F32) for item in items for _ in range(4)]
    _, r, C = items[0][0].shape
    rows = r // 4
    if transpose or rows % 8:
        res = pl.pallas_call(body, name=name, out_shape=out_shape, compiler_params=_params())(*flat)
    else:
        tile = pl.BlockSpec((rows, C), lambda i: (i, 0))
        res = pl.pallas_call(
            body, name=name, grid=(4,), in_specs=[pl.BlockSpec((4, rows, C), lambda i: (0, i, 0)), tile, tile, tile] * n,
            out_specs=[tile] * (4 * n), out_shape=out_shape, compiler_params=_params(("parallel",)),
        )(*flat)
    return [tuple(res[4 * k: 4 * k + 4]) for k in range(n)]


SMALL_LAYOUT = (("ffn1_norm", 0, (8, LANES)), ("mix_norm", 8, (8, LANES)), ("ffn2_norm", 16, (8, LANES)),
                ("gate_bias", 24, (16, LANES)), ("pool_scale", 40, (4, LANES)), ("q_norm", 48, (1, HEAD_DIM)),
                ("k_norm", 56, (1, HEAD_DIM)), ("sinks", 64, (1, N_HEADS)))
LOSS_ROW = 72
SMALL_ROWS = 80


def _adamw_small(name, g_vec, g_pool_w, params):
    n = len(SMALL_LAYOUT) + 1

    def body(vec_ref, pw_ref, *refs):
        ins, outs = refs[:3 * n], refs[3 * n:]
        vec = vec_ref[0]
        pw = pw_ref[0]
        for i in range(1, N_DEV):
            vec = vec + vec_ref[i]
            pw = pw + pw_ref[i]
        grads = [vec[r0:r0 + shape[0], 0:shape[1]] for _, r0, shape in SMALL_LAYOUT] + [pw]
        for p, g in enumerate(grads):
            w_ref, m_ref, v_ref = ins[3 * p: 3 * p + 3]
            delta, mn, vn = _adam_math(w_ref[...], g, m_ref[...], v_ref[...])
            for o_ref, val in zip(outs[4 * p: 4 * p + 4], (g, delta, mn, vn)):
                o_ref[...] = val
        outs[4 * n][...] = vec[LOSS_ROW:LOSS_ROW + 1, :]

    flat = [a for wmv in params for a in wmv]
    out_shape = [jax.ShapeDtypeStruct(wmv[0].shape, F32) for wmv in params for _ in range(4)]
    out_shape.append(jax.ShapeDtypeStruct((1, LANES), F32))
    res = pl.pallas_call(body, name=name, out_shape=out_shape, compiler_params=_params())(g_vec, g_pool_w, *flat)
    return [tuple(res[4 * p: 4 * p + 4]) for p in range(n)], res[4 * n]


def _place():
    x, y, c = lax.axis_index("x"), lax.axis_index("y"), lax.axis_index("c")
    other_chips = [(1 - x, y), (x, 1 - y), (1 - x, 1 - y)]
    return x, y, c, other_chips


def _rows(ref, r, place, natural=False):
    px, py, pc = place
    b = 4 * px + 2 * py + pc if natural else 4 * pc + 2 * px + py
    return ref.at[pl.ds(pl.multiple_of(b * r, 8), r), :]


def _gather_task(shards, natural=(), forward_at=0.75):
    n = len(shards)
    rs = [s.shape[0] for s in shards]
    rows_of = lambda ref, k, place: _rows(ref, rs[k], place, k in natural)

    def copy(scr, outs, k, slot, block, to, src=None):
        rows = rows_of(outs[k], k, block)
        return pltpu.make_async_remote_copy(
            src_ref=rows if src is None else src, dst_ref=rows, send_sem=scr[0].at[7 * k + slot],
            recv_sem=scr[1].at[7 * k + slot], device_id=to, device_id_type=MESH)

    def first_sends(ins, outs, scr):
        x, y, c, chips = _place()
        me = (x, y, c)
        cps = [copy(scr, outs, k, 1 + j, me, (*chip, c), src=ins[k]) for j, chip in enumerate(chips) for k in range(n)]
        return cps + [copy(scr, outs, k, 0, me, (x, y, 1 - c), src=ins[k]) for k in range(n)]

    def passed_on(outs, scr):
        x, y, c, chips = _place()
        return [copy(scr, outs, k, 4 + j, (*chip, c), (x, y, 1 - c)) for j, chip in enumerate(chips) for k in range(n)]

    def local(ins, outs, scr):
        x, y, c, _ = _place()
        return [pltpu.make_async_copy(ins[k], rows_of(outs[k], k, (x, y, c)), scr[2].at[k]) for k in range(n)]

    def start(ins, outs, scr):
        for cp in local(ins, outs, scr) + first_sends(ins, outs, scr):
            cp.start()

    def forward(ins, outs, scr):
        x, y, c, chips = _place()
        for j, chip in enumerate(chips):
            for k in range(n):
                copy(scr, outs, k, 1 + j, (*chip, c), (x, y, c)).wait_recv()
                copy(scr, outs, k, 4 + j, (*chip, c), (x, y, 1 - c)).start()

    def finish(ins, outs, scr):
        x, y, c, chips = _place()
        for k in range(n):
            copy(scr, outs, k, 0, (x, y, 1 - c), (x, y, c)).wait_recv()
        for j, chip in enumerate(chips):
            for k in range(n):
                copy(scr, outs, k, 4 + j, (*chip, 1 - c), (x, y, c)).wait_recv()
        for cp in first_sends(ins, outs, scr) + passed_on(outs, scr):
            cp.wait_send()
        for cp in local(ins, outs, scr):
            cp.wait()

    out_shapes = [jax.ShapeDtypeStruct((N_DEV * s.shape[0], s.shape[1]), s.dtype) for s in shards]
    scratch = [pltpu.SemaphoreType.DMA((7 * n,)), pltpu.SemaphoreType.DMA((7 * n,)), pltpu.SemaphoreType.DMA((n,))]
    return _Task(shards, out_shapes, scratch, [(0, start), (forward_at, forward), (1.0, finish)], ("sibling", "chips"))


def _direct_gather_task(shards):
    n = len(shards)
    rs = [s.shape[0] for s in shards]

    def peers():
        x, y, c, _ = _place()
        flip = lambda v, bit: 1 - v if bit else v
        return (x, y, c), [(flip(x, (s >> 2) & 1), flip(y, (s >> 1) & 1), flip(c, s & 1)) for s in range(1, N_DEV)]

    def copies(ins, outs, scr):
        me, others = peers()
        local = [pltpu.make_async_copy(ins[k], _rows(outs[k], rs[k], me), scr[2].at[k]) for k in range(n)]
        sems = lambda k, s: dict(send_sem=scr[0].at[7 * k + s], recv_sem=scr[1].at[7 * k + s], device_id_type=MESH)
        sends = [pltpu.make_async_remote_copy(src_ref=ins[k], dst_ref=_rows(outs[k], rs[k], me), device_id=to, **sems(k, s))
                 for s, to in enumerate(others) for k in range(n)]
        recvs = [pltpu.make_async_remote_copy(src_ref=_rows(outs[k], rs[k], frm), dst_ref=_rows(outs[k], rs[k], frm),
                                              device_id=me, **sems(k, s))
                 for s, frm in enumerate(others) for k in range(n)]
        return local, sends, recvs

    def start(ins, outs, scr):
        local, sends, _ = copies(ins, outs, scr)
        for cp in local + sends:
            cp.start()

    def finish(ins, outs, scr):
        local, sends, recvs = copies(ins, outs, scr)
        for cp in recvs:
            cp.wait_recv()
        for cp in sends:
            cp.wait_send()
        for cp in local:
            cp.wait()

    out_shapes = [jax.ShapeDtypeStruct((N_DEV * s.shape[0], s.shape[1]), s.dtype) for s in shards]
    scratch = [pltpu.SemaphoreType.DMA((7 * n,)), pltpu.SemaphoreType.DMA((7 * n,)), pltpu.SemaphoreType.DMA((n,))]
    return _Task(shards, out_shapes, scratch, [(0, start), (1.0, finish)], ("all",))


def _chip_task(sums):
    n = len(sums)
    rs = [s.shape[0] // 4 for s in sums]

    def block(ref, k, chip_index):
        return ref.at[pl.ds(pl.multiple_of(chip_index * rs[k], 8), rs[k]), :]

    def copies(ins, outs, scr):
        send_sems, recv_sems, local_sems = scr
        x, y, c, chips = _place()
        here = 2 * x + y
        local = [pltpu.make_async_copy(block(ins[k], k, here), outs[k].at[here], local_sems.at[k]) for k in range(n)]
        remote = []
        for j, (px, py) in enumerate(chips):
            remote += [pltpu.make_async_remote_copy(
                src_ref=block(ins[k], k, 2 * px + py), dst_ref=outs[k].at[here],
                send_sem=send_sems.at[3 * k + j], recv_sem=recv_sems.at[3 * k + j],
                device_id=(px, py, c), device_id_type=MESH) for k in range(n)]
        return local, remote

    def start(ins, outs, scr):
        local, remote = copies(ins, outs, scr)
        for cp in local + remote:
            cp.start()

    def finish(ins, outs, scr):
        local, remote = copies(ins, outs, scr)
        for cp in remote:
            cp.wait()
        for cp in local:
            cp.wait()

    out_shapes = [jax.ShapeDtypeStruct((4, r, s.shape[1]), s.dtype) for r, s in zip(rs, sums)]
    scratch = [pltpu.SemaphoreType.DMA((3 * n,)), pltpu.SemaphoreType.DMA((3 * n,)), pltpu.SemaphoreType.DMA((n,))]
    return _Task(sums, out_shapes, scratch, [(0, start), (1.0, finish)], ("chips",))


def _dw_pair(name, a, b, scale, comm=None, blocks=1):
    T, M = a.shape
    N = b.shape[1]
    half = M // 2
    wide = half // blocks
    tk = min(2048, T)
    nK = T // tk
    plumb = _CommPlumbing(comm)

    def body(core_ref, *rest):
        a_refs, b_ref, rest = rest[:blocks], rest[blocks], rest[blocks + 1:]
        c_in = rest[:plumb.n_in]
        o_ref = rest[plumb.n_in]
        c_out = rest[plumb.n_in + 1: plumb.n_in + 1 + plumb.n_out]
        acc, stage, land, send_sem, recv_sem = rest[plumb.n_in + 1 + plumb.n_out: plumb.n_in + 6 + plumb.n_out]
        c_scr = rest[plumb.n_in + 6 + plumb.n_out:]
        i, k = pl.program_id(0), pl.program_id(1)
        x, y, c, _ = _place()
        push = pltpu.make_async_remote_copy(src_ref=stage, dst_ref=land, send_sem=send_sem, recv_sem=recv_sem,
                                            device_id=(x, y, 1 - c), device_id_type=MESH)
        plumb.handshake((i == 0) & (k == 0), own=("sibling",))
        if comm:
            plumb.run(i * nK + k, 2 * nK, True, c_in, c_out, c_scr)

        av = a_refs[0][...] if blocks == 1 else jnp.concatenate([r[...] for r in a_refs], axis=1)
        p = lax.dot_general(av, b_ref[...], _DIMS["tn"], preferred_element_type=F32)

        @pl.when(k == 0)
        def _():
            acc[...] = p

        @pl.when(k > 0)
        def _():
            acc[...] += p

        @pl.when((i == 0) & (k == nK - 1))
        def _():
            stage[...] = (scale * acc[...]).astype(BF)
            push.start()

        @pl.when((i == 1) & (k == nK - 1))
        def _():
            push.wait_recv()
            o_ref[...] = (scale * acc[...] + land[...].astype(F32)).astype(BF)
            push.wait_send()

        if comm:
            plumb.run(i * nK + k, 2 * nK, False, c_in, c_out, c_scr)

    grid_spec = pltpu.PrefetchScalarGridSpec(
        num_scalar_prefetch=1, grid=(2, nK),
        in_specs=[pl.BlockSpec((tk, wide), functools.partial(
            lambda i, k, core, j: (k, (2 * j if blocks > 1 else 0) + jnp.where(i == 0, 1 - core[0], core[0])), j=j))
            for j in range(blocks)] + [pl.BlockSpec((tk, N), lambda i, k, core: (k, 0))] + [ANY] * plumb.n_in,
        out_specs=[pl.BlockSpec((half, N), lambda i, k, core: (0, 0))] + [ANY] * plumb.n_out,
        scratch_shapes=[pltpu.VMEM((half, N), F32), pltpu.VMEM((half, N), BF), pltpu.VMEM((half, N), BF),
                        pltpu.SemaphoreType.DMA, pltpu.SemaphoreType.DMA] + plumb.scratch)
    core = lax.axis_index("c").astype(jnp.int32).reshape(1)
    res = pl.pallas_call(
        body, name=name, grid_spec=grid_spec,
        out_shape=[jax.ShapeDtypeStruct((half, N), BF)] + plumb.out_shapes,
        compiler_params=_params(("arbitrary", "arbitrary"), plumb.collective_id(own=("sibling",))),
    )(core, *([a] * blocks), b, *plumb.args)
    return (res[0], plumb.split_outputs(res[1:])) if comm else res[0]


def _pair_task(parts):
    n = len(parts)

    def copies(ins, outs, scr):
        x, y, c, _ = _place()
        return [pltpu.make_async_remote_copy(
            src_ref=ins[k].at[:, pl.ds(1 - c, 1)], dst_ref=outs[k], send_sem=scr[0].at[k], recv_sem=scr[1].at[k],
            device_id=(x, y, 1 - c), device_id_type=MESH) for k in range(n)]

    def start(ins, outs, scr):
        for cp in copies(ins, outs, scr):
            cp.start()

    def finish(ins, outs, scr):
        for cp in copies(ins, outs, scr):
            cp.wait()

    out_shapes = [jax.ShapeDtypeStruct((4, 1) + p.shape[2:], p.dtype) for p in parts]
    scratch = [pltpu.SemaphoreType.DMA((n,)), pltpu.SemaphoreType.DMA((n,))]
    return _Task(parts, out_shapes, scratch, [(0, start), (1.0, finish)], ("sibling",))


def _pair_sum(name, part, got, core):
    _, _, r, C = part.shape

    def body(core_ref, p_ref, g_ref, o_ref):
        o_ref[0] = (p_ref[0, 0].astype(F32) + g_ref[0, 0].astype(F32)).astype(o_ref.dtype)

    return pl.pallas_call(
        body, name=name,
        grid_spec=pltpu.PrefetchScalarGridSpec(
            num_scalar_prefetch=1, grid=(4,),
            in_specs=[pl.BlockSpec((1, 1, r, C), lambda i, core_ref: (i, core_ref[0], 0, 0)),
                      pl.BlockSpec((1, 1, r, C), lambda i, core_ref: (i, 0, 0, 0))],
            out_specs=pl.BlockSpec((1, r, C), lambda i, core_ref: (i, 0, 0))),
        out_shape=jax.ShapeDtypeStruct((4, r, C), part.dtype), compiler_params=_params(("parallel",)),
    )(core, part, got)


def _ffn_bwd(tag, dy, dyb, x, gain, wgT, wuT, wd, saved, earlier=None):
    n, g, u, a = saved
    half = lambda accs, ex: _swiglu_bwd_epilogue([0.5 * accs[0]], ex)
    act_args = dict(tm=1024, tn=1408, tk=D_MODEL, epilogue=half, extras=[(g, "tile", 0), (u, "tile", 0)], cols_outer=True)
    if earlier is None:
        sum_d = _dw_pair(tag + "_dw_down", a, dyb, 0.5)
        (dg, du), ((slots_d,),) = _mm(tag + "_d_act", [(dyb, wd, "nt", 0)], [BF, BF], comm=[_chip_task([sum_d])], **act_args)
        slots_e = None
        sum_g = _dw_pair(tag + "_dw_gate", dg, n, 1.0)
    else:
        sum_d, ((got,),) = _dw_pair(tag + "_dw_down", a, dyb, 0.5, comm=[_pair_task([earlier])])
        core = lax.axis_index("c").astype(jnp.int32).reshape(1)
        sum_e = _pair_sum(tag + "_pair_sum_earlier", earlier, got, core)
        sum_e = sum_e.reshape(4 * sum_e.shape[1], sum_e.shape[2])
        (dg, du), ((slots_e,),) = _mm(tag + "_d_act", [(dyb, wd, "nt", 0)], [BF, BF], comm=[_chip_task([sum_e])], **act_args)
        sum_g, ((slots_d,),) = _dw_pair(tag + "_dw_gate", dg, n, 1.0, comm=[_chip_task([sum_d])])
    norm_args = dict(tm=512, tn=D_MODEL, tk=D_FF, epilogue=_rms_bwd_epilogue, n_colsum=1,
                     extras=[(x, "tile", 0), (gain, "row", 0), (dy, "tile", 0)])
    norm_terms = [(dg, wgT, "nn", 0), (du, wuT, "nn", 0)]
    if earlier is None:
        up = _dw_pair(tag + "_dw_up", du, n, 1.0)
        (dx, dxb, dgain), ((slots_g,),) = _mm(tag + "_d_norm", norm_terms, [F32, BF], comm=[_chip_task([sum_g])], **norm_args)
    else:
        sum_u, ((slots_g,),) = _dw_pair(tag + "_dw_up", du, n, 1.0, comm=[_chip_task([sum_g])])
        (dx, dxb, dgain), ((up,),) = _mm(tag + "_d_norm", norm_terms, [F32, BF], comm=[_chip_task([sum_u])], **norm_args)
    return dx, dxb, dgain, slots_e, slots_g, up, slots_d


def _tile_gain(g):
    return jnp.concatenate([g, g]).reshape(1, LANES)


def _fold_heads(partials):
    return jnp.sum(partials.reshape(-1, HEAD_DIM), axis=0)


def _pack_small_grads(grads, loss_local):
    pieces, row = [], 0
    for name, r0, _ in SMALL_LAYOUT + (("loss", LOSS_ROW, None),):
        v = (loss_local if name == "loss" else grads[name]).reshape(-1)
        rows = -(-v.size // LANES)
        block = jnp.pad(v, (0, rows * LANES - v.size)).reshape(rows, LANES)
        pieces += [jnp.zeros((r0 - row, LANES), F32)] * (r0 > row) + [block]
        row = r0 + rows
    pieces.append(jnp.zeros((SMALL_ROWS - row, LANES), F32))
    return jnp.concatenate(pieces, axis=0)


def kernel(x, ffn1_norm, ffn1_w_gate, ffn1_w_up, ffn1_w_down, mix_norm, w_in, pool_w, pool_scale, w_pool_out, q_norm, k_norm, sinks, w_attn_out, gate_bias, w_out, ffn2_norm, ffn2_w_gate, ffn2_w_up, ffn2_w_down, loss_target, m_ffn1_norm, m_ffn1_w_gate, m_ffn1_w_up, m_ffn1_w_down, m_mix_norm, m_w_in, m_pool_w, m_pool_scale, m_w_pool_out, m_q_norm, m_k_norm, m_sinks, m_w_attn_out, m_gate_bias, m_w_out, m_ffn2_norm, m_ffn2_w_gate, m_ffn2_w_up, m_ffn2_w_down, v_ffn1_norm, v_ffn1_w_gate, v_ffn1_w_up, v_ffn1_w_down, v_mix_norm, v_w_in, v_pool_w, v_pool_scale, v_w_pool_out, v_q_norm, v_k_norm, v_sinks, v_w_attn_out, v_gate_bias, v_w_out, v_ffn2_norm, v_ffn2_w_gate, v_ffn2_w_up, v_ffn2_w_down):
    T = x.shape[1]
    x2 = x.reshape(T, D_MODEL)
    target = loss_target.reshape(T, D_MODEL)

    big = [
        ("ffn1_w_gate", ffn1_w_gate, m_ffn1_w_gate, v_ffn1_w_gate, True, False),
        ("ffn1_w_up", ffn1_w_up, m_ffn1_w_up, v_ffn1_w_up, True, False),
        ("ffn1_w_down", ffn1_w_down, m_ffn1_w_down, v_ffn1_w_down, False, False),
        ("w_in", w_in, m_w_in, v_w_in, True, False),
        ("w_pool_out", w_pool_out, m_w_pool_out, v_w_pool_out, False, True),
        ("w_attn_out", w_attn_out, m_w_attn_out, v_w_attn_out, False, False),
        ("w_out", w_out, m_w_out, v_w_out, False, False),
        ("ffn2_w_gate", ffn2_w_gate, m_ffn2_w_gate, v_ffn2_w_gate, True, False),
        ("ffn2_w_up", ffn2_w_up, m_ffn2_w_up, v_ffn2_w_up, True, False),
        ("ffn2_w_down", ffn2_w_down, m_ffn2_w_down, v_ffn2_w_down, False, False),
    ]
    view = lambda a, tv: a.T if tv else a
    views = [view(w, tv) for _, w, _, _, tv, _ in big]
    in_kernel_t = [tk_ for *_, tk_ in big]
    first_shards = _prep("prep_ffn1_gate_up", views[0:2], in_kernel_t[0:2])
    g1 = ffn1_norm.reshape(1, D_MODEL)
    g2 = mix_norm.reshape(1, D_MODEL)
    g3 = ffn2_norm.reshape(1, D_MODEL)
    bias_row = gate_bias.reshape(1, 2 * D_MODEL)
    qg, kg = _tile_gain(q_norm) * ATTN_SCALE, _tile_gain(k_norm)
    scale_row = pool_scale.reshape(1, POOL_WIDTH)
    band_bias = _band_bias()

    n1, later_shards, ((wg1T, wu1T),) = _rms_fwd(
        "ffn1_norm", x2, g1, [_gather_task(first_shards, forward_at=0.9)], views[2:], in_kernel_t[2:])
    shards = list(first_shards) + later_shards
    (gt1, up1, act1), ((wd1,), (w_inT,)) = _mm(
        "ffn1_gate_up", [(n1, wg1T, "nt", 0), (n1, wu1T, "nt", 1)], [BF, BF, BF], tm=1024, tn=1408, tk=D_MODEL,
        epilogue=_swiglu_fwd_epilogue, cols_outer=True,
        comm=[_gather_task(shards[2:3], forward_at=0.5), _gather_task(shards[3:4], natural=(0,), forward_at=0.9)])
    (h1, u), ((w_poT, w_ao, w_o),) = _mm(
        "ffn1_down", [(act1, wd1, "nn", 0)], [F32, BF], tm=512, tn=D_MODEL, tk=D_FF,
        epilogue=_residual_norm_epilogue(0.5), extras=[(x2, "tile", 0), (g2, "row", 0)],
        comm=[_gather_task(shards[4:7], natural=(0, 1, 2), forward_at=0.8)])
    saved1 = (n1, gt1, up1, act1)
    (proj,), ((wg2T,),) = _mm(
        "in_proj", [(u, w_inT, "nt", 0)], [BF], tm=1024, tn=1280, tk=D_MODEL, cols_outer=True,
        comm=[_gather_task(shards[7:8], forward_at=0.8)])
    pooled, mixed = _pool_fwd("pool_fwd", proj, pool_w, scale_row)
    (attn, qn, kn), ((wu2T,),) = _attn_fwd("attn_fwd", proj, qg, kg, sinks, band_bias,
                                           [_gather_task(shards[8:9], forward_at=0.8)])
    gate_tn = 256
    gate_extras = [(proj, "tile", COL_GP // gate_tn), (proj, "tile", COL_GA // gate_tn),
                   (bias_row, "row", 0), (bias_row, "row", D_MODEL // gate_tn)]
    merged, ba, bp = _mm("branch_out_merge", [(attn, w_ao, "nn", 0), (mixed, w_poT, "nt", 1)], [BF, BF, BF],
                         tm=2048, tn=gate_tn, tk=ATTN_WIDTH, epilogue=_merge_fwd_epilogue, extras=gate_extras)
    h2, n2 = _mm("mix_out", [(merged, w_o, "nn", 0)], [F32, BF], tm=1024, tn=D_MODEL, tk=D_MODEL,
                 epilogue=_residual_norm_epilogue(1.0), extras=[(h1, "tile", 0), (g3, "row", 0)])
    (gt2, up2, act2), ((wd2,),) = _mm(
        "ffn2_gate_up", [(n2, wg2T, "nt", 0), (n2, wu2T, "nt", 1)], [BF, BF, BF], tm=1024, tn=1408, tk=D_MODEL,
        epilogue=_swiglu_fwd_epilogue, cols_outer=True, comm=[_gather_task(shards[9:10], forward_at=0.8)])
    dy, dyb, sq = _mm("ffn2_down_loss", [(act2, wd2, "nn", 0)], [F32, BF], tm=512, tn=D_MODEL, tk=D_FF,
                      epilogue=_loss_epilogue, extras=[(h2, "tile", 0), (target, "tile", 0)], n_colsum=1)
    loss_local = 0.5 * jnp.sum(sq) / D_MODEL

    dh2, dh2b, dg3, _, slots_g2, sum_u2, slots_d2 = _ffn_bwd(
        "ffn2", dy, dyb, h2, g3, wg2T, wu2T, wd2, (n2, gt2, up2, act2))
    (dbp, dba, dproj, dga, cs_gp, cs_ga), ((slots_u2,),) = _mm(
        "mix_out_bwd", [(dh2b, w_o, "nt", 0)], [BF, BF, BF, BF], tm=2048, tn=gate_tn, tk=D_MODEL,
        epilogue=_merge_bwd_epilogue, extras=[(bp, "tile", 0), (ba, "tile", 0)] + gate_extras, n_colsum=2,
        out_placement={2: (IN_WIDTH, COL_GP)}, comm=[_chip_task([sum_u2])])
    sum_o = _dw_pair("dw_out", merged, dh2b, 1.0, blocks=4)
    (dmixed,) = _mm("pool_out_bwd", [(dbp, w_poT, "nn", 0)], [BF], tm=1024, tn=POOL_WIDTH, tk=D_MODEL)
    sum_po = _dw_pair("dw_pool_out", dbp, mixed, 1.0, blocks=4)
    (dattn,) = _mm("attn_out_bwd", [(dba, w_ao, "nt", 0)], [BF], tm=1024, tn=ATTN_WIDTH, tk=D_MODEL)
    sum_ao = _dw_pair("dw_attn_out", attn, dba, 1.0, blocks=4)
    (dqn, k_own, k_before, v_own, v_before, dsink_tile), ((slots_o, slots_po, slots_ao),) = _attn_bwd(
        "attn_bwd", dattn, qn, kn, proj, sinks, band_bias, [_chip_task([sum_o, sum_po, sum_ao])])
    next_block = lambda a: jnp.concatenate([a[BLOCK:], jnp.zeros((BLOCK, KV_WIDTH), F32)], axis=0)
    dkn = (k_own + next_block(k_before)).astype(BF)
    dv = (v_own + next_block(v_before)).astype(BF)
    dproj, dqg = _headnorm_bwd("q_norm_bwd", dqn, proj, COL_Q, ATTN_WIDTH, qg, dproj)
    dproj, dkg = _headnorm_bwd("k_norm_bwd", dkn, proj, COL_K, KV_WIDTH, kg, dproj)
    dproj, dpool_w, dpool_scale = _pool_bwd("pool_bwd", dmixed, pooled, pool_w, scale_row, dproj)
    for piece, col in ((dv, COL_V), (dga, COL_GA)):
        dproj = lax.dynamic_update_slice(dproj, piece, (0, col))
    (dh1, dh1b, dg2), ((g_pool_w,),) = _mm(
        "in_proj_bwd", [(dproj, w_inT, "nn", 0)], [F32, BF], tm=512, tn=D_MODEL, tk=IN_WIDTH, epilogue=_rms_bwd_epilogue,
        extras=[(h1, "tile", 0), (g2, "row", 0), (dh2, "tile", 0)], n_colsum=1,
        comm=[_gather_task([dpool_w.reshape(-1, LANES)])])
    (dw_inT,) = _mm("dw_in", [(dproj, u, "tn", 0)], [BF], tm=1920, tn=D_MODEL, tk=2048)
    dx, _, dg1, slots_in, slots_g1, slots_u1, slots_d1 = _ffn_bwd(
        "ffn1", dh1, dh1b, x2, g1, wg1T, wu1T, wd1, saved1, dw_inT.reshape(4, 2, IN_WIDTH // N_DEV, D_MODEL))

    slots = [slots_g1, slots_u1, slots_d1, slots_in, slots_po, slots_ao, slots_o, slots_g2, slots_u2, slots_d2]
    big_out = {}
    for label, group in (("ffn", (0, 1, 2, 7, 8, 9)), ("w_in", (3,)), ("w_pool_out", (4,)), ("attn_out_and_out", (5, 6))):
        items = [(slots[k], view(big[k][1], big[k][4]), view(big[k][2], big[k][4]), view(big[k][3], big[k][4]))
                 for k in group]
        for k, res in zip(group, _adamw_sharded("adamw_" + label, items, transpose=big[group[0]][5])):
            big_out[big[k][0]] = tuple(view(r, big[k][4]) for r in res)

    small_grads = {
        "ffn1_norm": jnp.sum(dg1, axis=(0, 1)), "mix_norm": jnp.sum(dg2, axis=(0, 1)), "ffn2_norm": jnp.sum(dg3, axis=(0, 1)),
        "gate_bias": jnp.concatenate([jnp.sum(cs_gp, axis=(0, 1)), jnp.sum(cs_ga, axis=(0, 1))]),
        "pool_scale": dpool_scale, "q_norm": _fold_heads(dqg) * ATTN_SCALE, "k_norm": _fold_heads(dkg),
        "sinks": dsink_tile[0, :N_HEADS]}
    ((g_vec,),) = _comm_only("gather_small_grads", [_direct_gather_task([_pack_small_grads(small_grads, loss_local)])])
    given = {"ffn1_norm": (ffn1_norm, m_ffn1_norm, v_ffn1_norm), "mix_norm": (mix_norm, m_mix_norm, v_mix_norm),
             "ffn2_norm": (ffn2_norm, m_ffn2_norm, v_ffn2_norm), "gate_bias": (gate_bias, m_gate_bias, v_gate_bias),
             "pool_scale": (pool_scale, m_pool_scale, v_pool_scale), "q_norm": (q_norm, m_q_norm, v_q_norm),
             "k_norm": (k_norm, m_k_norm, v_k_norm), "sinks": (sinks, m_sinks, v_sinks)}
    params = [tuple(a.reshape(shape) for a in given[nm]) for nm, _, shape in SMALL_LAYOUT]
    params.append(tuple(a.reshape(-1, LANES) for a in (pool_w, m_pool_w, v_pool_w)))
    small_res, loss_row = _adamw_small("adamw_small", g_vec.reshape(N_DEV, SMALL_ROWS, LANES),
                                       g_pool_w.reshape(N_DEV, -1, LANES), params)
    small_out = {nm: tuple(r.reshape(given[nm][0].shape) for r in res)
                 for (nm, _, _), res in zip(SMALL_LAYOUT, small_res)}
    small_out["pool_w"] = tuple(r.reshape(pool_w.shape) for r in small_res[-1])
    loss = loss_row[0, 0]

    order = ["ffn1_norm", "ffn1_w_gate", "ffn1_w_up", "ffn1_w_down", "mix_norm", "w_in", "pool_w", "pool_scale",
             "w_pool_out", "q_norm", "k_norm", "sinks", "w_attn_out", "gate_bias", "w_out", "ffn2_norm",
             "ffn2_w_gate", "ffn2_w_up", "ffn2_w_down"]
    every = {**big_out, **small_out}
    outs = [loss, dx.reshape(x.shape)]
    for j in range(4):
        outs += [every[nm][j] for nm in order]
    return tuple(outs)
```

```python
import functools

import jax
import jax.numpy as jnp
from jax import lax
from jax.experimental import pallas as pl
from jax.experimental.pallas import tpu as pltpu

BF = jnp.bfloat16
F32 = jnp.float32

D_MODEL = 1024
D_FF = 2816
POOL_WIDTH = 512
POOL_GROUP = 128
N_POOL_GROUPS = 4
HEAD_DIM = 64
N_HEADS = 16
GQA_GROUP = 8
BLOCK = 128
ATTN_WIDTH = 1024
KV_WIDTH = 128
IN_WIDTH = 3840
RMS_EPS = 1e-6
N_DEV = 8
LANES = 128

COL_Q = POOL_WIDTH
COL_K = COL_Q + ATTN_WIDTH
COL_V = COL_K + KV_WIDTH
COL_GP = COL_V + KV_WIDTH
COL_GA = COL_GP + D_MODEL

ADAM_LR = 0.001
ADAM_B1 = 0.9
ADAM_B2 = 0.999
ADAM_EPS = 1e-08
ADAM_WD = 0.01
ADAM_STEP = 10

VMEM_LIMIT_V7X = 56 * 1024 * 1024
MESH = pl.DeviceIdType.MESH
ANY = pl.BlockSpec(memory_space=pl.ANY)


def _params(sem=None, collective_id=None):
    return pltpu.CompilerParams(dimension_semantics=sem, vmem_limit_bytes=VMEM_LIMIT_V7X, collective_id=collective_id)


COLLECTIVE_IDS = {frozenset(["sibling"]): 0, frozenset(["chips"]): 1, frozenset(["sibling", "chips"]): 2}


def _handshake(peer_kinds):
    x, y, c, chips = _place()
    peers = ([(x, y, 1 - c)] if "sibling" in peer_kinds else []) + ([(*chip, c) for chip in chips] if "chips" in peer_kinds else [])
    barrier = pltpu.get_barrier_semaphore()
    for peer in peers:
        pl.semaphore_signal(barrier, inc=1, device_id=peer, device_id_type=MESH)
    pl.semaphore_wait(barrier, len(peers))


_DIMS = {"nt": (((1,), (1,)), ((), ())), "nn": (((1,), (0,)), ((), ())), "tn": (((0,), (0,)), ((), ()))}


class _Task:
    def __init__(self, inputs, out_shapes, scratch, phases, peers):
        self.inputs, self.out_shapes, self.scratch = list(inputs), list(out_shapes), list(scratch)
        self.phases = list(phases)
        self.peers = frozenset(peers)


class _CommPlumbing:
    def __init__(self, tasks):
        self.tasks = list(tasks or [])
        self.args = [a for t in self.tasks for a in t.inputs]
        self.out_shapes = [o for t in self.tasks for o in t.out_shapes]
        self.scratch = [s for t in self.tasks for s in t.scratch]
        self.n_in, self.n_out = len(self.args), len(self.out_shapes)

    def peer_kinds(self, own=()):
        kinds = frozenset(own).union(*[t.peers for t in self.tasks])
        return None if "all" in kinds or not kinds else kinds

    def collective_id(self, own=()):
        kinds = self.peer_kinds(own)
        return None if kinds is None else COLLECTIVE_IDS[kinds]

    def handshake(self, first, own=()):
        kinds = self.peer_kinds(own)
        if kinds is not None:
            pl.when(first)(functools.partial(_handshake, kinds))

    def _slices(self, c_in, c_out, c_scr):
        i = o = s = 0
        for t in self.tasks:
            yield t, c_in[i:i + len(t.inputs)], c_out[o:o + len(t.out_shapes)], c_scr[s:s + len(t.scratch)]
            i, o, s = i + len(t.inputs), o + len(t.out_shapes), s + len(t.scratch)

    def run(self, step, steps, before, c_in, c_out, c_scr):
        for t, ins, outs, scr in self._slices(c_in, c_out, c_scr):
            for frac, fn in t.phases:
                if step is None:
                    fn(ins, outs, scr)
                elif before == (frac == 0):
                    at = 0 if frac == 0 else max(0, min(steps, -(-int(round(frac * steps * 64)) // 64)) - 1)
                    pl.when(step == at)(functools.partial(fn, ins, outs, scr))

    def split_outputs(self, flat):
        res, o = [], 0
        for t in self.tasks:
            res.append(list(flat[o:o + len(t.out_shapes)]))
            o += len(t.out_shapes)
        return res


def _comm_only(name, tasks):
    plumb = _CommPlumbing(tasks)

    def body(*refs):
        c_in, c_out = refs[:plumb.n_in], refs[plumb.n_in: plumb.n_in + plumb.n_out]
        c_scr = refs[plumb.n_in + plumb.n_out:]
        plumb.run(None, 1, True, c_in, c_out, c_scr)

    res = pl.pallas_call(
        body, name=name, in_specs=[ANY] * plumb.n_in, out_specs=[ANY] * plumb.n_out, out_shape=plumb.out_shapes,
        scratch_shapes=plumb.scratch, compiler_params=pltpu.CompilerParams(has_side_effects=True),
    )(*plumb.args)
    return plumb.split_outputs(res)


def _mm(name, terms, out_dtypes, *, tm, tn, tk, epilogue=None, extras=(), n_colsum=0, comm=None, cols_outer=False,
        out_placement=None):
    a0, b0, mode0, _ = terms[0]
    if mode0 == "nt":
        (M, K), N = a0.shape, b0.shape[0]
    elif mode0 == "nn":
        (M, K), N = a0.shape, b0.shape[1]
    else:
        (K, M), N = a0.shape, b0.shape[1]
    tm, tn, tk = min(tm, M), min(tn, N), min(tk, K)
    assert M % tm == 0 and N % tn == 0 and K % tk == 0, (name, M, N, K, tm, tn, tk)
    nI, nJ, nK = M // tm, N // tn, K // tk
    n_terms = len(terms)
    n_acc = max(t[3] for t in terms) + 1
    n_ex = len(extras)
    n_out = len(out_dtypes)
    if epilogue is None:
        epilogue = lambda accs, ex: ([accs[0]], [])
    plumb = _CommPlumbing(comm)
    n_scr = n_acc if nK > 1 else 0
    grid = (nJ, nI, nK) if cols_outer else (nI, nJ, nK)

    def body(*refs):
        n_in = 2 * n_terms + n_ex
        ab = refs[: 2 * n_terms]
        ex_refs = refs[2 * n_terms: n_in]
        c_in = refs[n_in: n_in + plumb.n_in]
        o0 = n_in + plumb.n_in
        out_refs = refs[o0: o0 + n_out]
        cs_refs = refs[o0 + n_out: o0 + n_out + n_colsum]
        c_out = refs[o0 + n_out + n_colsum: o0 + n_out + n_colsum + plumb.n_out]
        s0 = o0 + n_out + n_colsum + plumb.n_out
        acc_refs = refs[s0: s0 + n_scr]
        c_scr = refs[s0 + n_scr:]
        steps = grid[0] * grid[1] * nK
        if comm:
            step = (pl.program_id(0) * grid[1] + pl.program_id(1)) * nK + pl.program_id(2)
            plumb.handshake(step == 0)
            plumb.run(step, steps, True, c_in, c_out, c_scr)

        def products():
            accs = [None] * n_acc
            for t, (_, _, mode, ai) in enumerate(terms):
                p = lax.dot_general(ab[2 * t][...], ab[2 * t + 1][...], _DIMS[mode], preferred_element_type=F32)
                accs[ai] = p if accs[ai] is None else accs[ai] + p
            return accs

        def finish(accs):
            outs, colsums = epilogue(accs, [r[...] for r in ex_refs])
            for r, o in zip(out_refs, outs):
                r[...] = o.astype(r.dtype)
            for r, cs in zip(cs_refs, colsums):
                r[...] = jnp.sum(cs, axis=0, keepdims=True).reshape(r.shape)

        if nK == 1:
            finish(products())
        else:
            k = pl.program_id(2)
            accs = products()

            @pl.when(k == 0)
            def _():
                for r, a in zip(acc_refs, accs):
                    r[...] = a

            @pl.when(k > 0)
            def _():
                for r, a in zip(acc_refs, accs):
                    r[...] += a

            @pl.when(k == nK - 1)
            def _():
                finish([r[...] for r in acc_refs])

        if comm:
            plumb.run(step, steps, False, c_in, c_out, c_scr)

    def spec(block, index, fixed=False):
        imap = (lambda q, p, k: index(p, q, k)) if cols_outer else index
        return pl.BlockSpec(block, imap, pipeline_mode=pl.Buffered(1)) if fixed else pl.BlockSpec(block, imap)

    in_specs, args = [], []
    for a, b, mode, _ in terms:
        kt = tk if nK > 1 else (a.shape[0] if mode == "tn" else a.shape[1])
        if mode == "nt":
            in_specs += [spec((tm, kt), lambda i, j, k: (i, k), nI * nK == 1),
                         spec((tn, kt), lambda i, j, k: (j, k), nJ * nK == 1)]
        elif mode == "nn":
            in_specs += [spec((tm, kt), lambda i, j, k: (i, k), nI * nK == 1),
                         spec((kt, tn), lambda i, j, k: (k, j), nJ * nK == 1)]
        else:
            in_specs += [spec((kt, tm), lambda i, j, k: (k, i), nI * nK == 1),
                         spec((kt, tn), lambda i, j, k: (k, j), nJ * nK == 1)]
        args += [a, b]
    for arr, kind, off in extras:
        if kind == "tile":
            in_specs.append(spec((tm, tn), functools.partial(lambda i, j, k, off: (i, j + off), off=off)))
        else:
            in_specs.append(spec((1, tn), functools.partial(lambda i, j, k, off: (0, j + off), off=off)))
        args.append(arr)
    placed = dict(out_placement or {})
    out_shape = [jax.ShapeDtypeStruct((M, placed.get(o, (N, 0))[0]), dt) for o, dt in enumerate(out_dtypes)]
    out_specs = [spec((tm, tn), functools.partial(lambda i, j, k, off: (i, j + off), off=placed.get(o, (N, 0))[1] // tn))
                 for o in range(n_out)]
    out_shape += [jax.ShapeDtypeStruct((nI, 1, N), F32) for _ in range(n_colsum)]
    out_specs += [spec((1, 1, tn), lambda i, j, k: (i, 0, j)) for _ in range(n_colsum)]
    scratch = [pltpu.VMEM((tm, tn), F32) for _ in range(n_scr)]
    args += plumb.args
    in_specs += [ANY] * plumb.n_in
    out_shape += plumb.out_shapes
    out_specs += [ANY] * plumb.n_out
    sem = ("arbitrary",) * 3 if comm else ("parallel", "parallel", "arbitrary")
    res = pl.pallas_call(
        body, name=name, grid=grid, in_specs=in_specs, out_specs=out_specs, out_shape=out_shape,
        scratch_shapes=scratch + plumb.scratch, compiler_params=_params(sem, plumb.collective_id()),
    )(*args)
    n_own = n_out + n_colsum
    return (list(res[:n_own]), plumb.split_outputs(res[n_own:])) if comm is not None else res


ROW_TILE = 512


def _rms_fwd(name, x, g, comm, weights, transposes):
    T, D = x.shape
    steps = T // ROW_TILE
    plumb = _CommPlumbing(comm)
    nw = len(weights)

    def body(x_ref, g_ref, *rest):
        w_refs, c_in = rest[:nw], rest[nw: nw + plumb.n_in]
        o_ref, shard_refs = rest[nw + plumb.n_in], rest[nw + plumb.n_in + 1: 2 * nw + plumb.n_in + 1]
        c_out = rest[2 * nw + plumb.n_in + 1: 2 * nw + plumb.n_in + 1 + plumb.n_out]
        c_scr = rest[2 * nw + plumb.n_in + 1 + plumb.n_out:]
        plumb.handshake(pl.program_id(0) == 0)
        plumb.run(pl.program_id(0), steps, True, c_in, c_out, c_scr)

        @pl.when(pl.program_id(0) == 0)
        def _():
            for w_ref, s_ref, tr in zip(w_refs, shard_refs, transposes):
                v = w_ref[...]
                s_ref[...] = (v.T if tr else v).astype(BF)

        xv = x_ref[...]
        r = lax.rsqrt(jnp.mean(xv * xv, axis=-1, keepdims=True) + RMS_EPS)
        o_ref[...] = (xv * r * g_ref[...]).astype(BF)
        plumb.run(pl.program_id(0), steps, False, c_in, c_out, c_scr)

    row = pl.BlockSpec((ROW_TILE, D), lambda i: (i, 0))
    whole = lambda shape: pl.BlockSpec(shape, lambda i: (0, 0), pipeline_mode=pl.Buffered(1))
    shard_shapes = [w.shape[::-1] if tr else w.shape for w, tr in zip(weights, transposes)]
    res = pl.pallas_call(
        body, name=name, grid=(steps,),
        in_specs=[row, pl.BlockSpec((1, D), lambda i: (0, 0))] + [whole(w.shape) for w in weights] + [ANY] * plumb.n_in,
        out_specs=[row] + [whole(s) for s in shard_shapes] + [ANY] * plumb.n_out,
        out_shape=[jax.ShapeDtypeStruct((T, D), BF)] + [jax.ShapeDtypeStruct(s, BF) for s in shard_shapes] + plumb.out_shapes,
        scratch_shapes=plumb.scratch, compiler_params=_params(("arbitrary",), plumb.collective_id()),
    )(x, g, *weights, *plumb.args)
    return res[0], list(res[1: nw + 1]), plumb.split_outputs(res[nw + 1:])


HEADNORM_TILE = 2048


def _half_sum_matrix():
    r = lax.broadcasted_iota(jnp.int32, (LANES, LANES), 0) // HEAD_DIM
    c = lax.broadcasted_iota(jnp.int32, (LANES, LANES), 1) // HEAD_DIM
    return (r == c).astype(BF)


def _head_mean(v, ones_blockdiag):
    hi = v.astype(BF)
    lo = (v - hi.astype(F32)).astype(BF)
    s = jnp.dot(hi, ones_blockdiag, preferred_element_type=F32) + jnp.dot(lo, ones_blockdiag, preferred_element_type=F32)
    return s * (1.0 / HEAD_DIM)


def _headnorm_bwd(name, dy, proj, col0, width, g2, into):
    T = proj.shape[0]
    wide = min(width, GROUP_WIDTH)
    nb, off = width // wide, col0 // wide

    def body(dy_ref, x_ref, g_ref, b_ref, into_ref, dx_ref, dg_ref):
        for s in range(wide // LANES):
            lanes = slice(LANES * s, LANES * (s + 1))
            xv = x_ref[:, lanes].astype(F32)
            dyv = dy_ref[:, lanes].astype(F32)
            r = lax.rsqrt(_head_mean(xv * xv, b_ref[...]) + RMS_EPS)
            xhat = xv * r
            dxhat = dyv * g_ref[...]
            dx_ref[:, lanes] = (r * (dxhat - xhat * _head_mean(dxhat * xhat, b_ref[...]))).astype(BF)
            dg_ref[0, :, lanes] = jnp.sum(dyv * xhat, axis=0, keepdims=True)

    return pl.pallas_call(
        body, name=name, grid=(T // HEADNORM_TILE, nb),
        in_specs=[pl.BlockSpec((HEADNORM_TILE, wide), lambda i, j: (i, j)),
                  pl.BlockSpec((HEADNORM_TILE, wide), lambda i, j: (i, j + off)),
                  pl.BlockSpec((1, LANES), lambda i, j: (0, 0)), pl.BlockSpec((LANES, LANES), lambda i, j: (0, 0)), ANY],
        out_specs=[pl.BlockSpec((HEADNORM_TILE, wide), lambda i, j: (i, j + off)),
                   pl.BlockSpec((1, 1, wide), lambda i, j: (i, 0, j))],
        out_shape=[jax.ShapeDtypeStruct(into.shape, BF), jax.ShapeDtypeStruct((T // HEADNORM_TILE, 1, width), F32)],
        input_output_aliases={4: 0}, compiler_params=_params(("parallel", "parallel")),
    )(dy, proj, g2, _half_sum_matrix(), into)


def _shift_down(v, k, row):
    return jnp.where(row >= k, pltpu.roll(v, k, axis=0), 0.0)


def _shift_up(v, k, row, T):
    return jnp.where(row < T - k, pltpu.roll(v, T - k, axis=0), 0.0)


def _by_group(g, vals):
    out = vals[-1]
    for i in range(len(vals) - 2, -1, -1):
        out = jnp.where(g == i, vals[i], out)
    return out


def _pool_fwd(name, proj, pool_w, pool_scale):
    T = proj.shape[0]

    def body(x_ref, w_ref, s_ref, pooled_ref, mixed_ref):
        g = pl.program_id(0)
        xv = x_ref[...].astype(F32)
        row = lax.broadcasted_iota(jnp.int32, (T, 1), 0)
        s2 = xv + _shift_down(xv, 1, row)
        s4 = s2 + _shift_down(s2, 2, row)
        s8 = s4 + _shift_down(s4, 4, row)
        s16 = s8 + _shift_down(s8, 8, row)
        wsum = _by_group(g, [s2, s4, s8, s16])
        count = jnp.minimum(row + 1, 2 << g).astype(F32)
        pooled = (wsum / count - xv).astype(BF)
        pooled_ref[...] = pooled
        mixed = jnp.dot(pooled, w_ref[0].astype(BF), preferred_element_type=F32) * s_ref[...]
        mixed_ref[...] = mixed.astype(BF)

    col = pl.BlockSpec((T, POOL_GROUP), lambda g: (0, g))
    return pl.pallas_call(
        body, name=name, grid=(N_POOL_GROUPS,),
        in_specs=[col, pl.BlockSpec((1, POOL_GROUP, POOL_GROUP), lambda g: (g, 0, 0)),
                  pl.BlockSpec((1, POOL_GROUP), lambda g: (0, g))],
        out_specs=[col, col],
        out_shape=[jax.ShapeDtypeStruct((T, POOL_WIDTH), BF), jax.ShapeDtypeStruct((T, POOL_WIDTH), BF)],
        compiler_params=_params(("parallel",)),
    )(proj, pool_w, pool_scale)


def _pool_bwd(name, dbranch, w_poT, pooled, pool_w, pool_scale, into):
    T = dbranch.shape[0]

    def body(db_ref, wo_ref, p_ref, w_ref, s_ref, into_ref, dx_ref, dw_ref, ds_ref):
        g = pl.program_id(0)
        dm = jnp.dot(db_ref[...], wo_ref[...], preferred_element_type=F32)
        pooled = p_ref[...]
        w = w_ref[0].astype(BF)
        pre = jnp.dot(pooled, w, preferred_element_type=F32)
        ds_ref[...] = jnp.sum(dm * pre, axis=0, keepdims=True)
        dms = (dm * s_ref[...]).astype(BF)
        dw_ref[0] = lax.dot_general(pooled, dms, _DIMS["tn"], preferred_element_type=F32)
        dpooled = lax.dot_general(dms, w, _DIMS["nt"], preferred_element_type=F32)
        row = lax.broadcasted_iota(jnp.int32, (T, 1), 0)
        count = jnp.minimum(row + 1, 2 << g).astype(F32)
        z = dpooled / count
        l2 = z + _shift_up(z, 1, row, T)
        l4 = l2 + _shift_up(l2, 2, row, T)
        l8 = l4 + _shift_up(l4, 4, row, T)
        l16 = l8 + _shift_up(l8, 8, row, T)
        dx_ref[...] = (_by_group(g, [l2, l4, l8, l16]) - dpooled).astype(BF)

    col = pl.BlockSpec((T, POOL_GROUP), lambda g: (0, g))
    wspec = pl.BlockSpec((1, POOL_GROUP, POOL_GROUP), lambda g: (g, 0, 0))
    sspec = pl.BlockSpec((1, POOL_GROUP), lambda g: (0, g))
    whole = pl.BlockSpec(dbranch.shape, lambda g: (0, 0), pipeline_mode=pl.Buffered(1))
    return pl.pallas_call(
        body, name=name, grid=(N_POOL_GROUPS,),
        in_specs=[whole, pl.BlockSpec((w_poT.shape[0], POOL_GROUP), lambda g: (0, g)), col, wspec, sspec, ANY],
        out_specs=[col, wspec, sspec],
        out_shape=[jax.ShapeDtypeStruct(into.shape, BF),
                   jax.ShapeDtypeStruct((N_POOL_GROUPS, POOL_GROUP, POOL_GROUP), F32),
                   jax.ShapeDtypeStruct((1, POOL_WIDTH), F32)],
        input_output_aliases={5: 0}, compiler_params=_params(("parallel",)),
    )(dbranch, w_poT, pooled, pool_w, pool_scale, into)


ATTN_SCALE = HEAD_DIM ** -0.5
MASKED = float(jnp.finfo(jnp.float32).min)
KV_COL_BLOCK_V = COL_V // LANES
GROUP_WIDTH = GQA_GROUP * HEAD_DIM


def _dup_head(v, j):
    half = lax.broadcasted_iota(jnp.int32, (1, LANES), 1) // HEAD_DIM
    return jnp.where(half == j, v, pltpu.roll(v, HEAD_DIM, axis=1))


def _stack_heads(v, low):
    pieces = []
    for p in range(GROUP_WIDTH // LANES):
        vp = v[:, LANES * p: LANES * (p + 1)]
        pieces.append(jnp.where(low, vp, jnp.zeros_like(vp)))
        pieces.append(jnp.where(low, jnp.zeros_like(vp), vp))
    return jnp.concatenate(pieces, axis=0)


def _unstack_transposed(t, low):
    pairs = []
    for p in range(GROUP_WIDTH // LANES):
        even = t[:, BLOCK * (2 * p): BLOCK * (2 * p + 1)].T
        odd = t[:, BLOCK * (2 * p + 1): BLOCK * (2 * p + 2)].T
        pairs.append(jnp.where(low, even, odd))
    return pairs


STACKED = GQA_GROUP * BLOCK


def _band_bias():
    key = lax.broadcasted_iota(jnp.int32, (2, 2 * BLOCK, STACKED), 1)
    qry = lax.broadcasted_iota(jnp.int32, (2, 2 * BLOCK, STACKED), 2) % BLOCK
    first = lax.broadcasted_iota(jnp.int32, (2, 2 * BLOCK, STACKED), 0) == 0
    valid = (key > qry) & (key <= qry + BLOCK) & (jnp.logical_not(first) | (key >= BLOCK))
    return jnp.where(valid, 0.0, MASKED).astype(F32)


def _softmax_keys_on_sublanes(k2, q, bias, sink_ref, j):
    head_of_lane = lax.broadcasted_iota(jnp.int32, (1, STACKED), 1) // BLOCK
    sink = jnp.zeros((1, STACKED), F32)
    for h in range(GQA_GROUP):
        sink = jnp.where(head_of_lane == h, sink_ref[j * GQA_GROUP + h], sink)
    s = lax.dot_general(k2, q, _DIMS["nt"], preferred_element_type=F32) + bias
    m = jnp.maximum(jnp.max(s, axis=0, keepdims=True), sink)
    e = jnp.exp(s - m)
    e_sink = jnp.exp(sink - m)
    inv = 1.0 / (jnp.sum(e, axis=0, keepdims=True) + e_sink)
    return e * inv, e_sink * inv


def _attn_fwd(name, proj, qg, kg, sinks, bias, comm):
    T = proj.shape[0]
    nb = T // BLOCK
    plumb = _CommPlumbing(comm)

    def body(sink_ref, bias_ref, ones_ref, qg_ref, kg_ref, q0_ref, q1_ref, kp_ref, kc_ref, vp_ref, vc_ref, *rest):
        c_in, (o_ref, q_ref, kn_ref) = rest[:plumb.n_in], rest[plumb.n_in: plumb.n_in + 3]
        c_out, c_scr = rest[plumb.n_in + 3: plumb.n_in + 3 + plumb.n_out], rest[plumb.n_in + 3 + plumb.n_out:]
        m = pl.program_id(0)
        plumb.handshake(m == 0)
        plumb.run(m, nb // 2, True, c_in, c_out, c_scr)
        low = lax.broadcasted_iota(jnp.int32, (1, LANES), 1) < HEAD_DIM

        def head_norm(raw, gain):
            xv = raw.astype(F32)
            return (xv * lax.rsqrt(_head_mean(xv * xv, ones_ref[...]) + RMS_EPS) * gain).astype(BF)

        for half, raw_ref in enumerate((q0_ref, q1_ref)):
            for s in range(GROUP_WIDTH // LANES):
                q_ref[:, GROUP_WIDTH * half + LANES * s: GROUP_WIDTH * half + LANES * (s + 1)] = head_norm(
                    raw_ref[:, LANES * s: LANES * (s + 1)], qg_ref[...])
        k_pair, k_prev = head_norm(kc_ref[...], kg_ref[...]), head_norm(kp_ref[...], kg_ref[...])
        kn_ref[...] = k_pair
        v_pair = vc_ref[...]
        for b in range(2):
            rows = slice(BLOCK * b, BLOCK * (b + 1))
            kk = k_pair if b else jnp.concatenate([k_prev, k_pair[0:BLOCK]], axis=0)
            vv = v_pair if b else jnp.concatenate([vp_ref[...], v_pair[0:BLOCK]], axis=0)
            bias = bias_ref[1] if b else bias_ref[jnp.minimum(m, 1)]
            for j in range(2):
                q = _stack_heads(q_ref[rows, GROUP_WIDTH * j: GROUP_WIDTH * (j + 1)], low)
                p, _ = _softmax_keys_on_sublanes(_dup_head(kk, j), q, bias, sink_ref, j)
                o_t = lax.dot_general(_dup_head(vv, j), p.astype(BF), _DIMS["tn"], preferred_element_type=F32)
                for pair, o in enumerate(_unstack_transposed(o_t, low)):
                    lanes = slice(GROUP_WIDTH * j + LANES * pair, GROUP_WIDTH * j + LANES * (pair + 1))
                    o_ref[rows, lanes] = o.astype(BF)
        plumb.run(m, nb // 2, False, c_in, c_out, c_scr)

    wide = pl.BlockSpec((2 * BLOCK, ATTN_WIDTH), lambda m: (m, 0))
    before = lambda m: jnp.maximum(2 * m - 1, 0)
    gain = pl.BlockSpec((1, LANES), lambda m: (0, 0))
    q_block, k_block = COL_Q // GROUP_WIDTH, COL_K // LANES
    res = pl.pallas_call(
        body, name=name, grid=(nb // 2,),
        in_specs=[pl.BlockSpec(memory_space=pltpu.SMEM),
                  pl.BlockSpec((2, 2 * BLOCK, STACKED), lambda m: (0, 0, 0)),
                  pl.BlockSpec((LANES, LANES), lambda m: (0, 0)), gain, gain,
                  pl.BlockSpec((2 * BLOCK, GROUP_WIDTH), lambda m: (m, q_block)),
                  pl.BlockSpec((2 * BLOCK, GROUP_WIDTH), lambda m: (m, q_block + 1)),
                  pl.BlockSpec((BLOCK, LANES), lambda m: (before(m), k_block)),
                  pl.BlockSpec((2 * BLOCK, LANES), lambda m: (m, k_block)),
                  pl.BlockSpec((BLOCK, LANES), lambda m: (before(m), KV_COL_BLOCK_V)),
                  pl.BlockSpec((2 * BLOCK, LANES), lambda m: (m, KV_COL_BLOCK_V))] + [ANY] * plumb.n_in,
        out_specs=[wide, wide, pl.BlockSpec((2 * BLOCK, LANES), lambda m: (m, 0))] + [ANY] * plumb.n_out,
        out_shape=[jax.ShapeDtypeStruct((T, ATTN_WIDTH), BF), jax.ShapeDtypeStruct((T, ATTN_WIDTH), BF),
                   jax.ShapeDtypeStruct((T, KV_WIDTH), BF)] + plumb.out_shapes,
        scratch_shapes=plumb.scratch, compiler_params=_params(("arbitrary",), plumb.collective_id()),
    )(sinks, bias, _half_sum_matrix(), qg, kg, proj, proj, proj, proj, proj, proj, *plumb.args)
    return list(res[:3]), plumb.split_outputs(res[3:])


def _attn_bwd(name, dout, qn, kn, proj, sinks, bias, comm):
    T = qn.shape[0]
    nb = T // BLOCK
    plumb = _CommPlumbing(comm)

    def body(sink_ref, bias_ref, do_ref, q_ref, kp_ref, kc_ref, vp_ref, vc_ref, *rest):
        c_in = rest[:plumb.n_in]
        dq_ref, k_own, k_before, v_own, v_before, dsink_ref = rest[plumb.n_in: plumb.n_in + 6]
        c_out, c_scr = rest[plumb.n_in + 6: plumb.n_in + 6 + plumb.n_out], rest[plumb.n_in + 6 + plumb.n_out:]
        m = pl.program_id(0)
        plumb.handshake(m == 0)
        plumb.run(m, nb // 2, True, c_in, c_out, c_scr)
        lane = lax.broadcasted_iota(jnp.int32, (1, LANES), 1)
        low = lane < HEAD_DIM

        @pl.when(m == 0)
        def _():
            dsink_ref[...] = jnp.zeros_like(dsink_ref)

        k_pair, v_pair = kc_ref[...], vc_ref[...]
        dsink = jnp.zeros((1, LANES), F32)
        for b in range(2):
            rows = slice(BLOCK * b, BLOCK * (b + 1))
            kk = k_pair if b else jnp.concatenate([kp_ref[...], k_pair[0:BLOCK]], axis=0)
            vv = v_pair if b else jnp.concatenate([vp_ref[...], v_pair[0:BLOCK]], axis=0)
            bias = bias_ref[1] if b else bias_ref[jnp.minimum(m, 1)]
            dk_tot = jnp.zeros((2 * BLOCK, LANES), F32)
            dv_tot = jnp.zeros((2 * BLOCK, LANES), F32)
            for j in range(2):
                k2 = _dup_head(kk, j)
                v2 = _dup_head(vv, j)
                q = _stack_heads(q_ref[rows, GROUP_WIDTH * j: GROUP_WIDTH * (j + 1)], low)
                do = _stack_heads(do_ref[rows, GROUP_WIDTH * j: GROUP_WIDTH * (j + 1)], low)
                p, psink = _softmax_keys_on_sublanes(k2, q, bias, sink_ref, j)
                dp =lax.dot_general(v2, do, _DIMS["nt"], preferred_element_type=F32)
                delta = jnp.sum(p * dp, axis=0, keepdims=True)
                ds = (p * (dp - delta)).astype(BF)
                dk2 = jnp.dot(ds, q, preferred_element_type=F32)
                dv2 = jnp.dot(p.astype(BF), do, preferred_element_type=F32)
                dq_t = lax.dot_general(k2, ds, _DIMS["tn"], preferred_element_type=F32)
                for pair, dq in enumerate(_unstack_transposed(dq_t, low)):
                    lanes = slice(GROUP_WIDTH * j + LANES * pair, GROUP_WIDTH * j + LANES * (pair + 1))
                    dq_ref[rows, lanes] = dq.astype(BF)
                mine = low if j == 0 else jnp.logical_not(low)
                dk_tot = dk_tot + jnp.where(mine, dk2 + pltpu.roll(dk2, HEAD_DIM, axis=1), 0.0)
                dv_tot = dv_tot + jnp.where(mine, dv2 + pltpu.roll(dv2, HEAD_DIM, axis=1), 0.0)
                sink_term = psink * delta
                for h in range(GQA_GROUP):
                    val = -jnp.sum(sink_term[:, BLOCK * h: BLOCK * (h + 1)], axis=1, keepdims=True)
                    dsink = dsink + jnp.where(lane == j * GQA_GROUP + h, val, 0.0)
            k_before[rows, :], k_own[rows, :] = dk_tot[0:BLOCK], dk_tot[BLOCK:]
            v_before[rows, :], v_own[rows, :] = dv_tot[0:BLOCK], dv_tot[BLOCK:]
        dsink_ref[0:1, :] += dsink
        plumb.run(m, nb // 2, False, c_in, c_out, c_scr)

    wide = pl.BlockSpec((2 * BLOCK, ATTN_WIDTH), lambda m: (m, 0))
    pair = pl.BlockSpec((2 * BLOCK, LANES), lambda m: (m, 0))
    before = lambda m: jnp.maximum(2 * m - 1, 0)
    res = pl.pallas_call(
        body, name=name, grid=(nb // 2,),
        in_specs=[pl.BlockSpec(memory_space=pltpu.SMEM),
                  pl.BlockSpec((2, 2 * BLOCK, STACKED), lambda m: (0, 0, 0)), wide, wide,
                  pl.BlockSpec((BLOCK, LANES), lambda m: (before(m), 0)), pair,
                  pl.BlockSpec((BLOCK, LANES), lambda m: (before(m), KV_COL_BLOCK_V)),
                  pl.BlockSpec((2 * BLOCK, LANES), lambda m: (m, KV_COL_BLOCK_V))] + [ANY] * plumb.n_in,
        out_specs=[wide, pair, pair, pair, pair, pl.BlockSpec((8, LANES), lambda m: (0, 0))] + [ANY] * plumb.n_out,
        out_shape=[jax.ShapeDtypeStruct((T, ATTN_WIDTH), BF)] + [jax.ShapeDtypeStruct((T, KV_WIDTH), F32)] * 4
        + [jax.ShapeDtypeStruct((8, LANES), F32)] + plumb.out_shapes,
        scratch_shapes=plumb.scratch, compiler_params=_params(("arbitrary",), plumb.collective_id()),
    )(sinks, bias, dout, qn, kn, kn, proj, proj, *plumb.args)
    return list(res[:6]), plumb.split_outputs(res[6:])


def _swiglu_fwd_epilogue(accs, ex):
    g, u = accs
    return [g, u, g * jax.nn.sigmoid(g) * u], []


def _swiglu_bwd_epilogue(accs, ex):
    (da,) = accs
    g, u = ex[0].astype(F32), ex[1].astype(F32)
    s = jax.nn.sigmoid(g)
    gs = g * s
    return [da * u * (s + gs - gs * s), da * gs], []


def _residual_norm_epilogue(scale):
    def epilogue(accs, ex):
        res, gain = ex
        h = res + scale * accs[0]
        r = lax.rsqrt(jnp.mean(h * h, axis=-1, keepdims=True) + RMS_EPS)
        return [h, h * r * gain], []
    return epilogue


def _rms_bwd_epilogue(accs, ex):
    (dn,) = accs
    xv, g, dres = ex
    r = lax.rsqrt(jnp.mean(xv * xv, axis=-1, keepdims=True) + RMS_EPS)
    xhat = xv * r
    dxhat = dn * g
    dx = dres + r * (dxhat - xhat * jnp.mean(dxhat * xhat, axis=-1, keepdims=True))
    return [dx, dx], [dn * xhat]


def _loss_epilogue(accs, ex):
    xv, target = ex
    d = xv + 0.5 * accs[0] - target
    dy = d * (1.0 / D_MODEL)
    return [dy, dy], [d * d]


def _merge_fwd_epilogue(accs, ex):
    ba, bp = accs
    gp_pre, ga_pre, bias_p, bias_a = ex
    gp = jax.nn.sigmoid(gp_pre.astype(F32) + bias_p)
    ga = jax.nn.sigmoid(ga_pre.astype(F32) + bias_a)
    return [gp * bp + ga * ba, ba, bp], []


def _merge_bwd_epilogue(accs, ex):
    (dm,) = accs
    bp, ba, gp_pre, ga_pre, bias_p, bias_a = ex
    gp = jax.nn.sigmoid(gp_pre.astype(F32) + bias_p)
    ga = jax.nn.sigmoid(ga_pre.astype(F32) + bias_a)
    dbp, dba = dm * gp, dm * ga
    dgp = dbp * bp.astype(F32) * (1.0 - gp)
    dga = dba * ba.astype(F32) * (1.0 - ga)
    return [dbp, dba, dgp, dga], [dgp, dga]


def _prep(name, ws, transposes):
    n = len(ws)

    def body(*refs):
        for w_ref, o_ref, tr in zip(refs[:n], refs[n:], transposes):
            v = w_ref[...]
            o_ref[...] = (v.T if tr else v).astype(BF)

    shapes = [jax.ShapeDtypeStruct(w.shape[::-1] if tr else w.shape, BF) for w, tr in zip(ws, transposes)]
    return pl.pallas_call(body, name=name, out_shape=shapes, compiler_params=_params())(*ws)


def _adam_math(w, g, m, v):
    m = ADAM_B1 * m + (1.0 - ADAM_B1) * g
    v = ADAM_B2 * v + (1.0 - ADAM_B2) * jnp.square(g)
    m_hat = m / (1.0 - ADAM_B1 ** ADAM_STEP)
    v_hat = v / (1.0 - ADAM_B2 ** ADAM_STEP)
    delta = -ADAM_LR * (m_hat / (jnp.sqrt(v_hat) + ADAM_EPS) + ADAM_WD * w)
    return delta, m, v


def _adamw_sharded(name, items, transpose=False):
    n = len(items)

    def body(*refs):
        ins, outs = refs[:4 * n], refs[4 * n:]
        for k in range(n):
            s_ref, w_ref, m_ref, v_ref = ins[4 * k: 4 * k + 4]
            g = s_ref[0].astype(F32)
            for i in range(1, 4):
                g = g + s_ref[i].astype(F32)
            if transpose:
                g = g.T
            delta, mn, vn = _adam_math(w_ref[...], g, m_ref[...], v_ref[...])
            for o_ref, val in zip(outs[4 * k: 4 * k + 4], (g, delta, mn, vn)):
                o_ref[...] = val

    flat = [a for item in items for a in item]
    out_shape = [jax.ShapeDtypeStruct(item[1].shape, F32) for item in items for _ in range(4)]
    _, r, C = items[0][0].shape
    rows = r // 4
    if transpose or rows % 8:
        res = pl.pallas_call(body, name=name, out_shape=out_shape, compiler_params=_params())(*flat)
    else:
        tile = pl.BlockSpec((rows, C), lambda i: (i, 0))
        res = pl.pallas_call(
            body, name=name, grid=(4,), in_specs=[pl.BlockSpec((4, rows, C), lambda i: (0, i, 0)), tile, tile, tile] * n,
            out_specs=[tile] * (4 * n), out_shape=out_shape, compiler_params=_params(("parallel",)),
        )(*flat)
    return [tuple(res[4 * k: 4 * k + 4]) for k in range(n)]


SMALL_LAYOUT = (("ffn1_norm", 0, (8, LANES)), ("mix_norm", 8, (8, LANES)), ("ffn2_norm", 16, (8, LANES)),
                ("gate_bias", 24, (16, LANES)), ("pool_scale", 40, (4, LANES)), ("q_norm", 48, (1, HEAD_DIM)),
                ("k_norm", 56, (1, HEAD_DIM)), ("sinks", 64, (1, N_HEADS)))
LOSS_ROW = 72
SMALL_ROWS = 80


def _adamw_small(name, g_vec, g_pool_w, params):
    n = len(SMALL_LAYOUT) + 1

    def body(vec_ref, pw_ref, *refs):
        ins, outs = refs[:3 * n], refs[3 * n:]
        vec = vec_ref[0]
        pw = pw_ref[0]
        for i in range(1, N_DEV):
            vec = vec + vec_ref[i]
            pw = pw + pw_ref[i]
        grads = [vec[r0:r0 + shape[0], 0:shape[1]] for _, r0, shape in SMALL_LAYOUT] + [pw]
        for p, g in enumerate(grads):
            w_ref, m_ref, v_ref = ins[3 * p: 3 * p + 3]
            delta, mn, vn = _adam_math(w_ref[...], g, m_ref[...], v_ref[...])
            for o_ref, val in zip(outs[4 * p: 4 * p + 4], (g, delta, mn, vn)):
                o_ref[...] = val
        outs[4 * n][...] = vec[LOSS_ROW:LOSS_ROW + 1, :]

    flat = [a for wmv in params for a in wmv]
    out_shape = [jax.ShapeDtypeStruct(wmv[0].shape, F32) for wmv in params for _ in range(4)]
    out_shape.append(jax.ShapeDtypeStruct((1, LANES), F32))
    res = pl.pallas_call(body, name=name, out_shape=out_shape, compiler_params=_params())(g_vec, g_pool_w, *flat)
    return [tuple(res[4 * p: 4 * p + 4]) for p in range(n)], res[4 * n]


def _place():
    x, y, c = lax.axis_index("x"), lax.axis_index("y"), lax.axis_index("c")
    other_chips = [(1 - x, y), (x, 1 - y), (1 - x, 1 - y)]
    return x, y, c, other_chips


def _rows(ref, r, place, natural=False):
    px, py, pc = place
    b = 4 * px + 2 * py + pc if natural else 4 * pc + 2 * px + py
    return ref.at[pl.ds(pl.multiple_of(b * r, 8), r), :]


def _gather_task(shards, natural=(), forward_at=0.75):
    n = len(shards)
    rs = [s.shape[0] for s in shards]
    rows_of = lambda ref, k, place: _rows(ref, rs[k], place, k in natural)

    def copy(scr, outs, k, slot, block, to, src=None):
        rows = rows_of(outs[k], k, block)
        return pltpu.make_async_remote_copy(
            src_ref=rows if src is None else src, dst_ref=rows, send_sem=scr[0].at[7 * k + slot],
            recv_sem=scr[1].at[7 * k + slot], device_id=to, device_id_type=MESH)

    def first_sends(ins, outs, scr):
        x, y, c, chips = _place()
        me = (x, y, c)
        cps = [copy(scr, outs, k, 1 + j, me, (*chip, c), src=ins[k]) for j, chip in enumerate(chips) for k in range(n)]
        return cps + [copy(scr, outs, k, 0, me, (x, y, 1 - c), src=ins[k]) for k in range(n)]

    def passed_on(outs, scr):
        x, y, c, chips = _place()
        return [copy(scr, outs, k, 4 + j, (*chip, c), (x, y, 1 - c)) for j, chip in enumerate(chips) for k in range(n)]

    def local(ins, outs, scr):
        x, y, c, _ = _place()
        return [pltpu.make_async_copy(ins[k], rows_of(outs[k], k, (x, y, c)), scr[2].at[k]) for k in range(n)]

    def start(ins, outs, scr):
        for cp in local(ins, outs, scr) + first_sends(ins, outs, scr):
            cp.start()

    def forward(ins, outs, scr):
        x, y, c, chips = _place()
        for j, chip in enumerate(chips):
            for k in range(n):
                copy(scr, outs, k, 1 + j, (*chip, c), (x, y, c)).wait_recv()
                copy(scr, outs, k, 4 + j, (*chip, c), (x, y, 1 - c)).start()

    def finish(ins, outs, scr):
        x, y, c, chips = _place()
        for k in range(n):
            copy(scr, outs, k, 0, (x, y, 1 - c), (x, y, c)).wait_recv()
        for j, chip in enumerate(chips):
            for k in range(n):
                copy(scr, outs, k, 4 + j, (*chip, 1 - c), (x, y, c)).wait_recv()
        for cp in first_sends(ins, outs, scr) + passed_on(outs, scr):
            cp.wait_send()
        for cp in local(ins, outs, scr):
            cp.wait()

    out_shapes = [jax.ShapeDtypeStruct((N_DEV * s.shape[0], s.shape[1]), s.dtype) for s in shards]
    scratch = [pltpu.SemaphoreType.DMA((7 * n,)), pltpu.SemaphoreType.DMA((7 * n,)), pltpu.SemaphoreType.DMA((n,))]
    return _Task(shards, out_shapes, scratch, [(0, start), (forward_at, forward), (1.0, finish)], ("sibling", "chips"))


def _direct_gather_task(shards):
    n = len(shards)
    rs = [s.shape[0] for s in shards]

    def peers():
        x, y, c, _ = _place()
        flip = lambda v, bit: 1 - v if bit else v
        return (x, y, c), [(flip(x, (s >> 2) & 1), flip(y, (s >> 1) & 1), flip(c, s & 1)) for s in range(1, N_DEV)]

    def copies(ins, outs, scr):
        me, others = peers()
        local = [pltpu.make_async_copy(ins[k], _rows(outs[k], rs[k], me), scr[2].at[k]) for k in range(n)]
        sems = lambda k, s: dict(send_sem=scr[0].at[7 * k + s], recv_sem=scr[1].at[7 * k + s], device_id_type=MESH)
        sends = [pltpu.make_async_remote_copy(src_ref=ins[k], dst_ref=_rows(outs[k], rs[k], me), device_id=to, **sems(k, s))
                 for s, to in enumerate(others) for k in range(n)]
        recvs = [pltpu.make_async_remote_copy(src_ref=_rows(outs[k], rs[k], frm), dst_ref=_rows(outs[k], rs[k], frm),
                                              device_id=me, **sems(k, s))
                 for s, frm in enumerate(others) for k in range(n)]
        return local, sends, recvs

    def start(ins, outs, scr):
        local, sends, _ = copies(ins, outs, scr)
        for cp in local + sends:
            cp.start()

    def finish(ins, outs, scr):
        local, sends, recvs = copies(ins, outs, scr)
        for cp in recvs:
            cp.wait_recv()
        for cp in sends:
            cp.wait_send()
        for cp in local:
            cp.wait()

    out_shapes = [jax.ShapeDtypeStruct((N_DEV * s.shape[0], s.shape[1]), s.dtype) for s in shards]
    scratch = [pltpu.SemaphoreType.DMA((7 * n,)), pltpu.SemaphoreType.DMA((7 * n,)), pltpu.SemaphoreType.DMA((n,))]
    return _Task(shards, out_shapes, scratch, [(0, start), (1.0, finish)], ("all",))


def _chip_task(sums):
    n = len(sums)
    rs = [s.shape[0] // 4 for s in sums]

    def block(ref, k, chip_index):
        return ref.at[pl.ds(pl.multiple_of(chip_index * rs[k], 8), rs[k]), :]

    def copies(ins, outs, scr):
        send_sems, recv_sems, local_sems = scr
        x, y, c, chips = _place()
        here = 2 * x + y
        local = [pltpu.make_async_copy(block(ins[k], k, here), outs[k].at[here], local_sems.at[k]) for k in range(n)]
        remote = []
        for j, (px, py) in enumerate(chips):
            remote += [pltpu.make_async_remote_copy(
                src_ref=block(ins[k], k, 2 * px + py), dst_ref=outs[k].at[here],
                send_sem=send_sems.at[3 * k + j], recv_sem=recv_sems.at[3 * k + j],
                device_id=(px, py, c), device_id_type=MESH) for k in range(n)]
        return local, remote

    def start(ins, outs, scr):
        local, remote = copies(ins, outs, scr)
        for cp in local + remote:
            cp.start()

    def finish(ins, outs, scr):
        local, remote = copies(ins, outs, scr)
        for cp in remote:
            cp.wait()
        for cp in local:
            cp.wait()

    out_shapes = [jax.ShapeDtypeStruct((4, r, s.shape[1]), s.dtype) for r, s in zip(rs, sums)]
    scratch = [pltpu.SemaphoreType.DMA((3 * n,)), pltpu.SemaphoreType.DMA((3 * n,)), pltpu.SemaphoreType.DMA((n,))]
    return _Task(sums, out_shapes, scratch, [(0, start), (1.0, finish)], ("chips",))


def _dw_pair(name, a, b, scale, comm=None, blocks=1):
    T, M = a.shape
    N = b.shape[1]
    half = M // 2
    wide = half // blocks
    tk = min(2048, T)
    nK = T // tk
    plumb = _CommPlumbing(comm)

    def body(core_ref, *rest):
        a_refs, b_ref, rest = rest[:blocks], rest[blocks], rest[blocks + 1:]
        c_in = rest[:plumb.n_in]
        o_ref = rest[plumb.n_in]
        c_out = rest[plumb.n_in + 1: plumb.n_in + 1 + plumb.n_out]
        acc, stage, land, send_sem, recv_sem = rest[plumb.n_in + 1 + plumb.n_out: plumb.n_in + 6 + plumb.n_out]
        c_scr = rest[plumb.n_in + 6 + plumb.n_out:]
        i, k = pl.program_id(0), pl.program_id(1)
        x, y, c, _ = _place()
        push = pltpu.make_async_remote_copy(src_ref=stage, dst_ref=land, send_sem=send_sem, recv_sem=recv_sem,
                                            device_id=(x, y, 1 - c), device_id_type=MESH)
        plumb.handshake((i == 0) & (k == 0), own=("sibling",))
        if comm:
            plumb.run(i * nK + k, 2 * nK, True, c_in, c_out, c_scr)

        av = a_refs[0][...] if blocks == 1 else jnp.concatenate([r[...] for r in a_refs], axis=1)
        p = lax.dot_general(av, b_ref[...], _DIMS["tn"], preferred_element_type=F32)

        @pl.when(k == 0)
        def _():
            acc[...] = p

        @pl.when(k > 0)
        def _():
            acc[...] += p

        @pl.when((i == 0) & (k == nK - 1))
        def _():
            stage[...] = (scale * acc[...]).astype(BF)
            push.start()

        @pl.when((i == 1) & (k == nK - 1))
        def _():
            push.wait_recv()
            o_ref[...] = (scale * acc[...] + land[...].astype(F32)).astype(BF)
            push.wait_send()

        if comm:
            plumb.run(i * nK + k, 2 * nK, False, c_in, c_out, c_scr)

    grid_spec = pltpu.PrefetchScalarGridSpec(
        num_scalar_prefetch=1, grid=(2, nK),
        in_specs=[pl.BlockSpec((tk, wide), functools.partial(
            lambda i, k, core, j: (k, (2 * j if blocks > 1 else 0) + jnp.where(i == 0, 1 - core[0], core[0])), j=j))
            for j in range(blocks)] + [pl.BlockSpec((tk, N), lambda i, k, core: (k, 0))] + [ANY] * plumb.n_in,
        out_specs=[pl.BlockSpec((half, N), lambda i, k, core: (0, 0))] + [ANY] * plumb.n_out,
        scratch_shapes=[pltpu.VMEM((half, N), F32), pltpu.VMEM((half, N), BF), pltpu.VMEM((half, N), BF),
                        pltpu.SemaphoreType.DMA, pltpu.SemaphoreType.DMA] + plumb.scratch)
    core = lax.axis_index("c").astype(jnp.int32).reshape(1)
    res = pl.pallas_call(
        body, name=name, grid_spec=grid_spec,
        out_shape=[jax.ShapeDtypeStruct((half, N), BF)] + plumb.out_shapes,
        compiler_params=_params(("arbitrary", "arbitrary"), plumb.collective_id(own=("sibling",))),
    )(core, *([a] * blocks), b, *plumb.args)
    return (res[0], plumb.split_outputs(res[1:])) if comm else res[0]


def _pair_task(parts):
    n = len(parts)

    def copies(ins, outs, scr):
        x, y, c, _ = _place()
        return [pltpu.make_async_remote_copy(
            src_ref=ins[k].at[:, pl.ds(1 - c, 1)], dst_ref=outs[k], send_sem=scr[0].at[k], recv_sem=scr[1].at[k],
            device_id=(x, y, 1 - c), device_id_type=MESH) for k in range(n)]

    def start(ins, outs, scr):
        for cp in copies(ins, outs, scr):
            cp.start()

    def finish(ins, outs, scr):
        for cp in copies(ins, outs, scr):
            cp.wait()

    out_shapes = [jax.ShapeDtypeStruct((4, 1) + p.shape[2:], p.dtype) for p in parts]
    scratch = [pltpu.SemaphoreType.DMA((n,)), pltpu.SemaphoreType.DMA((n,))]
    return _Task(parts, out_shapes, scratch, [(0, start), (1.0, finish)], ("sibling",))


def _pair_sum(name, part, got, core):
    _, _, r, C = part.shape

    def body(core_ref, p_ref, g_ref, o_ref):
        o_ref[0] = (p_ref[0, 0].astype(F32) + g_ref[0, 0].astype(F32)).astype(o_ref.dtype)

    return pl.pallas_call(
        body, name=name,
        grid_spec=pltpu.PrefetchScalarGridSpec(
            num_scalar_prefetch=1, grid=(4,),
            in_specs=[pl.BlockSpec((1, 1, r, C), lambda i, core_ref: (i, core_ref[0], 0, 0)),
                      pl.BlockSpec((1, 1, r, C), lambda i, core_ref: (i, 0, 0, 0))],
            out_specs=pl.BlockSpec((1, r, C), lambda i, core_ref: (i, 0, 0))),
        out_shape=jax.ShapeDtypeStruct((4, r, C), part.dtype), compiler_params=_params(("parallel",)),
    )(core, part, got)


def _ffn_bwd(tag, dy, dyb, x, gain, wgT, wuT, wd, saved, earlier=None):
    n, g, u, a = saved
    half = lambda accs, ex: _swiglu_bwd_epilogue([0.5 * accs[0]], ex)
    act_args = dict(tm=1024, tn=1408, tk=D_MODEL, epilogue=half, extras=[(g, "tile", 0), (u, "tile", 0)], cols_outer=True)
    if earlier is None:
        sum_d = _dw_pair(tag + "_dw_down", a, dyb, 0.5)
        (dg, du), ((slots_d,),) = _mm(tag + "_d_act", [(dyb, wd, "nt", 0)], [BF, BF], comm=[_chip_task([sum_d])], **act_args)
        slots_e = None
        sum_g = _dw_pair(tag + "_dw_gate", dg, n, 1.0)
    else:
        sum_d, ((got,),) = _dw_pair(tag + "_dw_down", a, dyb, 0.5, comm=[_pair_task([earlier])])
        core = lax.axis_index("c").astype(jnp.int32).reshape(1)
        sum_e = _pair_sum(tag + "_pair_sum_earlier", earlier, got, core)
        sum_e = sum_e.reshape(4 * sum_e.shape[1], sum_e.shape[2])
        (dg, du), ((slots_e,),) = _mm(tag + "_d_act", [(dyb, wd, "nt", 0)], [BF, BF], comm=[_chip_task([sum_e])], **act_args)
        sum_g, ((slots_d,),) = _dw_pair(tag + "_dw_gate", dg, n, 1.0, comm=[_chip_task([sum_d])])
    norm_args = dict(tm=512, tn=D_MODEL, tk=D_FF, epilogue=_rms_bwd_epilogue, n_colsum=1,
                     extras=[(x, "tile", 0), (gain, "row", 0), (dy, "tile", 0)])
    norm_terms = [(dg, wgT, "nn", 0), (du, wuT, "nn", 0)]
    if earlier is None:
        up = _dw_pair(tag + "_dw_up", du, n, 1.0)
        (dx, dxb, dgain), ((slots_g,),) = _mm(tag + "_d_norm", norm_terms, [F32, BF], comm=[_chip_task([sum_g])], **norm_args)
    else:
        sum_u, ((slots_g,),) = _dw_pair(tag + "_dw_up", du, n, 1.0, comm=[_chip_task([sum_g])])
        (dx, dxb, dgain), ((up,),) = _mm(tag + "_d_norm", norm_terms, [F32, BF], comm=[_chip_task([sum_u])], **norm_args)
    return dx, dxb, dgain, slots_e, slots_g, up, slots_d


def _tile_gain(g):
    return jnp.concatenate([g, g]).reshape(1, LANES)


def _fold_heads(partials):
    return jnp.sum(partials.reshape(-1, HEAD_DIM), axis=0)


def _pack_small_grads(grads, loss_local):
    pieces, row = [], 0
    for name, r0, _ in SMALL_LAYOUT + (("loss", LOSS_ROW, None),):
        v = (loss_local if name == "loss" else grads[name]).reshape(-1)
        rows = -(-v.size // LANES)
        block = jnp.pad(v, (0, rows * LANES - v.size)).reshape(rows, LANES)
        pieces += [jnp.zeros((r0 - row, LANES), F32)] * (r0 > row) + [block]
        row = r0 + rows
    pieces.append(jnp.zeros((SMALL_ROWS - row, LANES), F32))
    return jnp.concatenate(pieces, axis=0)


def kernel(x, ffn1_norm, ffn1_w_gate, ffn1_w_up, ffn1_w_down, mix_norm, w_in, pool_w, pool_scale, w_pool_out, q_norm, k_norm, sinks, w_attn_out, gate_bias, w_out, ffn2_norm, ffn2_w_gate, ffn2_w_up, ffn2_w_down, loss_target, m_ffn1_norm, m_ffn1_w_gate, m_ffn1_w_up, m_ffn1_w_down, m_mix_norm, m_w_in, m_pool_w, m_pool_scale, m_w_pool_out, m_q_norm, m_k_norm, m_sinks, m_w_attn_out, m_gate_bias, m_w_out, m_ffn2_norm, m_ffn2_w_gate, m_ffn2_w_up, m_ffn2_w_down, v_ffn1_norm, v_ffn1_w_gate, v_ffn1_w_up, v_ffn1_w_down, v_mix_norm, v_w_in, v_pool_w, v_pool_scale, v_w_pool_out, v_q_norm, v_k_norm, v_sinks, v_w_attn_out, v_gate_bias, v_w_out, v_ffn2_norm, v_ffn2_w_gate, v_ffn2_w_up, v_ffn2_w_down):
    T = x.shape[1]
    x2 = x.reshape(T, D_MODEL)
    target = loss_target.reshape(T, D_MODEL)

    big = [
        ("ffn1_w_gate", ffn1_w_gate, m_ffn1_w_gate, v_ffn1_w_gate, True, False),
        ("ffn1_w_up", ffn1_w_up, m_ffn1_w_up, v_ffn1_w_up, True, False),
        ("ffn1_w_down", ffn1_w_down, m_ffn1_w_down, v_ffn1_w_down, False, False),
        ("w_in", w_in, m_w_in, v_w_in, True, False),
        ("w_pool_out", w_pool_out, m_w_pool_out, v_w_pool_out, False, True),
        ("w_attn_out", w_attn_out, m_w_attn_out, v_w_attn_out, False, False),
        ("w_out", w_out, m_w_out, v_w_out, False, False),
        ("ffn2_w_gate", ffn2_w_gate, m_ffn2_w_gate, v_ffn2_w_gate, True, False),
        ("ffn2_w_up", ffn2_w_up, m_ffn2_w_up, v_ffn2_w_up, True, False),
        ("ffn2_w_down", ffn2_w_down, m_ffn2_w_down, v_ffn2_w_down, False, False),
    ]
    view = lambda a, tv: a.T if tv else a
    views = [view(w, tv) for _, w, _, _, tv, _ in big]
    in_kernel_t = [tk_ for *_, tk_ in big]
    first_shards = _prep("prep_ffn1_gate_up", views[0:2], in_kernel_t[0:2])
    g1 = ffn1_norm.reshape(1, D_MODEL)
    g2 = mix_norm.reshape(1, D_MODEL)
    g3 = ffn2_norm.reshape(1, D_MODEL)
    bias_row = gate_bias.reshape(1, 2 * D_MODEL)
    qg, kg = _tile_gain(q_norm) * ATTN_SCALE, _tile_gain(k_norm)
    scale_row = pool_scale.reshape(1, POOL_WIDTH)
    band_bias = _band_bias()

    n1, later_shards, ((wg1T, wu1T),) = _rms_fwd(
        "ffn1_norm", x2, g1, [_gather_task(first_shards, forward_at=0.9)], views[2:], in_kernel_t[2:])
    shards = list(first_shards) + later_shards
    (gt1, up1, act1), ((wd1,), (w_inT,)) = _mm(
        "ffn1_gate_up", [(n1, wg1T, "nt", 0), (n1, wu1T, "nt", 1)], [BF, BF, BF], tm=1024, tn=1408, tk=D_MODEL,
        epilogue=_swiglu_fwd_epilogue, cols_outer=True,
        comm=[_gather_task(shards[2:3], forward_at=0.5), _gather_task(shards[3:4], natural=(0,), forward_at=0.9)])
    (h1, u), ((w_poT, w_ao, w_o),) = _mm(
        "ffn1_down", [(act1, wd1, "nn", 0)], [F32, BF], tm=512, tn=D_MODEL, tk=D_FF,
        epilogue=_residual_norm_epilogue(0.5), extras=[(x2, "tile", 0), (g2, "row", 0)],
        comm=[_gather_task(shards[4:7], natural=(0, 1, 2), forward_at=0.8)])
    saved1 = (n1, gt1, up1, act1)
    (proj,), ((wg2T,),) = _mm(
        "in_proj", [(u, w_inT, "nt", 0)], [BF], tm=1024, tn=1280, tk=D_MODEL, cols_outer=True,
        comm=[_gather_task(shards[7:8], forward_at=0.8)])
    pooled, mixed = _pool_fwd("pool_fwd", proj, pool_w, scale_row)
    (attn, qn, kn), ((wu2T,),) = _attn_fwd("attn_fwd", proj, qg, kg, sinks, band_bias,
                                           [_gather_task(shards[8:9], forward_at=0.8)])
    gate_tn = 256
    gate_extras = [(proj, "tile", COL_GP // gate_tn), (proj, "tile", COL_GA // gate_tn),
                   (bias_row, "row", 0), (bias_row, "row", D_MODEL // gate_tn)]
    merged, ba, bp = _mm("branch_out_merge", [(attn, w_ao, "nn", 0), (mixed, w_poT, "nt", 1)], [BF, BF, BF],
                         tm=2048, tn=gate_tn, tk=ATTN_WIDTH, epilogue=_merge_fwd_epilogue, extras=gate_extras)
    h2, n2 = _mm("mix_out", [(merged, w_o, "nn", 0)], [F32, BF], tm=1024, tn=D_MODEL, tk=D_MODEL,
                 epilogue=_residual_norm_epilogue(1.0), extras=[(h1, "tile", 0), (g3, "row", 0)])
    (gt2, up2, act2), ((wd2,),) = _mm(
        "ffn2_gate_up", [(n2, wg2T, "nt", 0), (n2, wu2T, "nt", 1)], [BF, BF, BF], tm=1024, tn=1408, tk=D_MODEL,
        epilogue=_swiglu_fwd_epilogue, cols_outer=True, comm=[_gather_task(shards[9:10], forward_at=0.8)])
    dy, dyb, sq = _mm("ffn2_down_loss", [(act2, wd2, "nn", 0)], [F32, BF], tm=512, tn=D_MODEL, tk=D_FF,
                      epilogue=_loss_epilogue, extras=[(h2, "tile", 0), (target, "tile", 0)], n_colsum=1)
    loss_local = 0.5 * jnp.sum(sq) / D_MODEL

    dh2, dh2b, dg3, _, slots_g2, sum_u2, slots_d2 = _ffn_bwd(
        "ffn2", dy, dyb, h2, g3, wg2T, wu2T, wd2, (n2, gt2, up2, act2))
    (dbp, dba, dproj, dga, cs_gp, cs_ga), ((slots_u2,),) = _mm(
        "mix_out_bwd", [(dh2b, w_o, "nt", 0)], [BF, BF, BF, BF], tm=2048, tn=gate_tn, tk=D_MODEL,
        epilogue=_merge_bwd_epilogue, extras=[(bp, "tile", 0), (ba, "tile", 0)] + gate_extras, n_colsum=2,
        out_placement={2: (IN_WIDTH, COL_GP)}, comm=[_chip_task([sum_u2])])
    sum_o = _dw_pair("dw_out", merged, dh2b, 1.0, blocks=4)
    sum_po = _dw_pair("dw_pool_out", dbp, mixed, 1.0, blocks=4)
    (dattn,) = _mm("attn_out_bwd", [(dba, w_ao, "nt", 0)], [BF], tm=1024, tn=ATTN_WIDTH, tk=D_MODEL)
    sum_ao = _dw_pair("dw_attn_out", attn, dba, 1.0, blocks=4)
    (dqn, k_own, k_before, v_own, v_before, dsink_tile), ((slots_o, slots_po, slots_ao),) = _attn_bwd(
        "attn_bwd", dattn, qn, kn, proj, sinks, band_bias, [_chip_task([sum_o, sum_po, sum_ao])])
    next_block = lambda a: jnp.concatenate([a[BLOCK:], jnp.zeros((BLOCK, KV_WIDTH), F32)], axis=0)
    dkn = (k_own + next_block(k_before)).astype(BF)
    dv = (v_own + next_block(v_before)).astype(BF)
    dproj, dqg = _headnorm_bwd("q_norm_bwd", dqn, proj, COL_Q, ATTN_WIDTH, qg, dproj)
    dproj, dkg = _headnorm_bwd("k_norm_bwd", dkn, proj, COL_K, KV_WIDTH, kg, dproj)
    dproj, dpool_w, dpool_scale = _pool_bwd("pool_bwd", dbp, w_poT, pooled, pool_w, scale_row, dproj)
    for piece, col in ((dv, COL_V), (dga, COL_GA)):
        dproj = lax.dynamic_update_slice(dproj, piece, (0, col))
    (dh1, dh1b, dg2), ((g_pool_w,),) = _mm(
        "in_proj_bwd", [(dproj, w_inT, "nn", 0)], [F32, BF], tm=512, tn=D_MODEL, tk=IN_WIDTH, epilogue=_rms_bwd_epilogue,
        extras=[(h1, "tile", 0), (g2, "row", 0), (dh2, "tile", 0)], n_colsum=1,
        comm=[_gather_task([dpool_w.reshape(-1, LANES)])])
    (dw_inT,) = _mm("dw_in", [(dproj, u, "tn", 0)], [BF], tm=1920, tn=D_MODEL, tk=2048)
    dx, _, dg1, slots_in, slots_g1, slots_u1, slots_d1 = _ffn_bwd(
        "ffn1", dh1, dh1b, x2, g1, wg1T, wu1T, wd1, saved1, dw_inT.reshape(4, 2, IN_WIDTH // N_DEV, D_MODEL))

    slots = [slots_g1, slots_u1, slots_d1, slots_in, slots_po, slots_ao, slots_o, slots_g2, slots_u2, slots_d2]
    big_out = {}
    for label, group in (("ffn", (0, 1, 2, 7, 8, 9)), ("w_in", (3,)), ("w_pool_out", (4,)), ("attn_out_and_out", (5, 6))):
        items = [(slots[k], view(big[k][1], big[k][4]), view(big[k][2], big[k][4]), view(big[k][3], big[k][4]))
                 for k in group]
        for k, res in zip(group, _adamw_sharded("adamw_" + label, items, transpose=big[group[0]][5])):
            big_out[big[k][0]] = tuple(view(r, big[k][4]) for r in res)

    small_grads = {
        "ffn1_norm": jnp.sum(dg1, axis=(0, 1)), "mix_norm": jnp.sum(dg2, axis=(0, 1)), "ffn2_norm": jnp.sum(dg3, axis=(0, 1)),
        "gate_bias": jnp.concatenate([jnp.sum(cs_gp, axis=(0, 1)), jnp.sum(cs_ga, axis=(0, 1))]),
        "pool_scale": dpool_scale, "q_norm": _fold_heads(dqg) * ATTN_SCALE, "k_norm": _fold_heads(dkg),
        "sinks": dsink_tile[0, :N_HEADS]}
    ((g_vec,),) = _comm_only("gather_small_grads", [_direct_gather_task([_pack_small_grads(small_grads, loss_local)])])
    given = {"ffn1_norm": (ffn1_norm, m_ffn1_norm, v_ffn1_norm), "mix_norm": (mix_norm, m_mix_norm, v_mix_norm),
             "ffn2_norm": (ffn2_norm, m_ffn2_norm, v_ffn2_norm), "gate_bias": (gate_bias, m_gate_bias, v_gate_bias),
             "pool_scale": (pool_scale, m_pool_scale, v_pool_scale), "q_norm": (q_norm, m_q_norm, v_q_norm),
             "k_norm": (k_norm, m_k_norm, v_k_norm), "sinks": (sinks, m_sinks, v_sinks)}
    params = [tuple(a.reshape(shape) for a in given[nm]) for nm, _, shape in SMALL_LAYOUT]
    params.append(tuple(a.reshape(-1, LANES) for a in (pool_w, m_pool_w, v_pool_w)))
    small_res, loss_row = _adamw_small("adamw_small", g_vec.reshape(N_DEV, SMALL_ROWS, LANES),
                                       g_pool_w.reshape(N_DEV, -1, LANES), params)
    small_out = {nm: tuple(r.reshape(given[nm][0].shape) for r in res)
                 for (nm, _, _), res in zip(SMALL_LAYOUT, small_res)}
    small_out["pool_w"] = tuple(r.reshape(pool_w.shape) for r in small_res[-1])
    loss = loss_row[0, 0]

    order = ["ffn1_norm", "ffn1_w_gate", "ffn1_w_up", "ffn1_w_down", "mix_norm", "w_in", "pool_w", "pool_scale",
             "w_pool_out", "q_norm", "k_norm", "sinks", "w_attn_out", "gate_bias", "w_out", "ffn2_norm",
             "ffn2_w_gate", "ffn2_w_up", "ffn2_w_down"]
    every = {**big_out, **small_out}
    outs = [loss, dx.reshape(x.shape)]
    for j in range(4):
        outs += [every[nm][j] for nm in order]
    return tuple(outs)
```

```python
import functools

import jax
import jax.numpy as jnp
from jax import lax
from jax.experimental import pallas as pl
from jax.experimental.pallas import tpu as pltpu

BF = jnp.bfloat16
F32 = jnp.float32

D_MODEL = 1024
D_FF = 2816
POOL_WIDTH = 512
POOL_GROUP = 128
N_POOL_GROUPS = 4
HEAD_DIM = 64
N_HEADS = 16
GQA_GROUP = 8
BLOCK = 128
ATTN_WIDTH = 1024
KV_WIDTH = 128
IN_WIDTH = 3840
RMS_EPS = 1e-6
N_DEV = 8
LANES = 128

COL_Q = POOL_WIDTH
COL_K = COL_Q + ATTN_WIDTH
COL_V = COL_K + KV_WIDTH
COL_GP = COL_V + KV_WIDTH
COL_GA = COL_GP + D_MODEL

ADAM_LR = 0.001
ADAM_B1 = 0.9
ADAM_B2 = 0.999
ADAM_EPS = 1e-08
ADAM_WD = 0.01
ADAM_STEP = 10

VMEM_LIMIT_V7X = 56 * 1024 * 1024
MESH = pl.DeviceIdType.MESH
ANY = pl.BlockSpec(memory_space=pl.ANY)


def _params(sem=None, collective_id=None):
    return pltpu.CompilerParams(dimension_semantics=sem, vmem_limit_bytes=VMEM_LIMIT_V7X, collective_id=collective_id)


COLLECTIVE_IDS = {frozenset(["sibling"]): 0, frozenset(["chips"]): 1, frozenset(["sibling", "chips"]): 2}


def _handshake(peer_kinds):
    x, y, c, chips = _place()
    peers = ([(x, y, 1 - c)] if "sibling" in peer_kinds else []) + ([(*chip, c) for chip in chips] if "chips" in peer_kinds else [])
    barrier = pltpu.get_barrier_semaphore()
    for peer in peers:
        pl.semaphore_signal(barrier, inc=1, device_id=peer, device_id_type=MESH)
    pl.semaphore_wait(barrier, len(peers))


_DIMS = {"nt": (((1,), (1,)), ((), ())), "nn": (((1,), (0,)), ((), ())), "tn": (((0,), (0,)), ((), ()))}


class _Task:
    def __init__(self, inputs, out_shapes, scratch, phases, peers):
        self.inputs, self.out_shapes, self.scratch = list(inputs), list(out_shapes), list(scratch)
        self.phases = list(phases)
        self.peers = frozenset(peers)


class _CommPlumbing:
    def __init__(self, tasks):
        self.tasks = list(tasks or [])
        self.args = [a for t in self.tasks for a in t.inputs]
        self.out_shapes = [o for t in self.tasks for o in t.out_shapes]
        self.scratch = [s for t in self.tasks for s in t.scratch]
        self.n_in, self.n_out = len(self.args), len(self.out_shapes)

    def peer_kinds(self, own=()):
        kinds = frozenset(own).union(*[t.peers for t in self.tasks])
        return None if "all" in kinds or not kinds else kinds

    def collective_id(self, own=()):
        kinds = self.peer_kinds(own)
        return None if kinds is None else COLLECTIVE_IDS[kinds]

    def handshake(self, first, own=()):
        kinds = self.peer_kinds(own)
        if kinds is not None:
            pl.when(first)(functools.partial(_handshake, kinds))

    def _slices(self, c_in, c_out, c_scr):
        i = o = s = 0
        for t in self.tasks:
            yield t, c_in[i:i + len(t.inputs)], c_out[o:o + len(t.out_shapes)], c_scr[s:s + len(t.scratch)]
            i, o, s = i + len(t.inputs), o + len(t.out_shapes), s + len(t.scratch)

    def run(self, step, steps, before, c_in, c_out, c_scr):
        for t, ins, outs, scr in self._slices(c_in, c_out, c_scr):
            for frac, fn in t.phases:
                if step is None:
                    fn(ins, outs, scr)
                elif before == (frac == 0):
                    at = 0 if frac == 0 else max(0, min(steps, -(-int(round(frac * steps * 64)) // 64)) - 1)
                    pl.when(step == at)(functools.partial(fn, ins, outs, scr))

    def split_outputs(self, flat):
        res, o = [], 0
        for t in self.tasks:
            res.append(list(flat[o:o + len(t.out_shapes)]))
            o += len(t.out_shapes)
        return res


def _comm_only(name, tasks):
    plumb = _CommPlumbing(tasks)

    def body(*refs):
        c_in, c_out = refs[:plumb.n_in], refs[plumb.n_in: plumb.n_in + plumb.n_out]
        c_scr = refs[plumb.n_in + plumb.n_out:]
        plumb.run(None, 1, True, c_in, c_out, c_scr)

    res = pl.pallas_call(
        body, name=name, in_specs=[ANY] * plumb.n_in, out_specs=[ANY] * plumb.n_out, out_shape=plumb.out_shapes,
        scratch_shapes=plumb.scratch, compiler_params=pltpu.CompilerParams(has_side_effects=True),
    )(*plumb.args)
    return plumb.split_outputs(res)


def _mm(name, terms, out_dtypes, *, tm, tn, tk, epilogue=None, extras=(), n_colsum=0, comm=None, cols_outer=False,
        out_placement=None):
    a0, b0, mode0, _ = terms[0]
    if mode0 == "nt":
        (M, K), N = a0.shape, b0.shape[0]
    elif mode0 == "nn":
        (M, K), N = a0.shape, b0.shape[1]
    else:
        (K, M), N = a0.shape, b0.shape[1]
    tm, tn, tk = min(tm, M), min(tn, N), min(tk, K)
    assert M % tm == 0 and N % tn == 0 and K % tk == 0, (name, M, N, K, tm, tn, tk)
    nI, nJ, nK = M // tm, N // tn, K // tk
    n_terms = len(terms)
    n_acc = max(t[3] for t in terms) + 1
    n_ex = len(extras)
    n_out = len(out_dtypes)
    if epilogue is None:
        epilogue = lambda accs, ex: ([accs[0]], [])
    plumb = _CommPlumbing(comm)
    n_scr = n_acc if nK > 1 else 0
    grid = (nJ, nI, nK) if cols_outer else (nI, nJ, nK)

    def body(*refs):
        n_in = 2 * n_terms + n_ex
        ab = refs[: 2 * n_terms]
        ex_refs = refs[2 * n_terms: n_in]
        c_in = refs[n_in: n_in + plumb.n_in]
        o0 = n_in + plumb.n_in
        out_refs = refs[o0: o0 + n_out]
        cs_refs = refs[o0 + n_out: o0 + n_out + n_colsum]
        c_out = refs[o0 + n_out + n_colsum: o0 + n_out + n_colsum + plumb.n_out]
        s0 = o0 + n_out + n_colsum + plumb.n_out
        acc_refs = refs[s0: s0 + n_scr]
        c_scr = refs[s0 + n_scr:]
        steps = grid[0] * grid[1] * nK
        if comm:
            step = (pl.program_id(0) * grid[1] + pl.program_id(1)) * nK + pl.program_id(2)
            plumb.handshake(step == 0)
            plumb.run(step, steps, True, c_in, c_out, c_scr)

        def products():
            accs = [None] * n_acc
            for t, (_, _, mode, ai) in enumerate(terms):
                p = lax.dot_general(ab[2 * t][...], ab[2 * t + 1][...], _DIMS[mode], preferred_element_type=F32)
                accs[ai] = p if accs[ai] is None else accs[ai] + p
            return accs

        def finish(accs):
            outs, colsums = epilogue(accs, [r[...] for r in ex_refs])
            for r, o in zip(out_refs, outs):
                r[...] = o.astype(r.dtype)
            for r, cs in zip(cs_refs, colsums):
                r[...] = jnp.sum(cs, axis=0, keepdims=True).reshape(r.shape)

        if nK == 1:
            finish(products())
        else:
            k = pl.program_id(2)
            accs = products()

            @pl.when(k == 0)
            def _():
                for r, a in zip(acc_refs, accs):
                    r[...] = a

            @pl.when(k > 0)
            def _():
                for r, a in zip(acc_refs, accs):
                    r[...] += a

            @pl.when(k == nK - 1)
            def _():
                finish([r[...] for r in acc_refs])

        if comm:
            plumb.run(step, steps, False, c_in, c_out, c_scr)

    def spec(block, index, fixed=False):
        imap = (lambda q, p, k: index(p, q, k)) if cols_outer else index
        return pl.BlockSpec(block, imap, pipeline_mode=pl.Buffered(1)) if fixed else pl.BlockSpec(block, imap)

    in_specs, args = [], []
    for a, b, mode, _ in terms:
        kt = tk if nK > 1 else (a.shape[0] if mode == "tn" else a.shape[1])
        if mode == "nt":
            in_specs += [spec((tm, kt), lambda i, j, k: (i, k), nI * nK == 1),
                         spec((tn, kt), lambda i, j, k: (j, k), nJ * nK == 1)]
        elif mode == "nn":
            in_specs += [spec((tm, kt), lambda i, j, k: (i, k), nI * nK == 1),
                         spec((kt, tn), lambda i, j, k: (k, j), nJ * nK == 1)]
        else:
            in_specs += [spec((kt, tm), lambda i, j, k: (k, i), nI * nK == 1),
                         spec((kt, tn), lambda i, j, k: (k, j), nJ * nK == 1)]
        args += [a, b]
    for arr, kind, off in extras:
        if kind == "tile":
            in_specs.append(spec((tm, tn), functools.partial(lambda i, j, k, off: (i, j + off), off=off)))
        else:
            in_specs.append(spec((1, tn), functools.partial(lambda i, j, k, off: (0, j + off), off=off)))
        args.append(arr)
    placed = dict(out_placement or {})
    out_shape = [jax.ShapeDtypeStruct((M, placed.get(o, (N, 0))[0]), dt) for o, dt in enumerate(out_dtypes)]
    out_specs = [spec((tm, tn), functools.partial(lambda i, j, k, off: (i, j + off), off=placed.get(o, (N, 0))[1] // tn))
                 for o in range(n_out)]
    out_shape += [jax.ShapeDtypeStruct((nI, 1, N), F32) for _ in range(n_colsum)]
    out_specs += [spec((1, 1, tn), lambda i, j, k: (i, 0, j)) for _ in range(n_colsum)]
    scratch = [pltpu.VMEM((tm, tn), F32) for _ in range(n_scr)]
    args += plumb.args
    in_specs += [ANY] * plumb.n_in
    out_shape += plumb.out_shapes
    out_specs += [ANY] * plumb.n_out
    sem = ("arbitrary",) * 3 if comm else ("parallel", "parallel", "arbitrary")
    res = pl.pallas_call(
        body, name=name, grid=grid, in_specs=in_specs, out_specs=out_specs, out_shape=out_shape,
        scratch_shapes=scratch + plumb.scratch, compiler_params=_params(sem, plumb.collective_id()),
    )(*args)
    n_own = n_out + n_colsum
    return (list(res[:n_own]), plumb.split_outputs(res[n_own:])) if comm is not None else res


ROW_TILE = 512


def _rms_fwd(name, x, g, comm, weights, transposes):
    T, D = x.shape
    steps = T // ROW_TILE
    plumb = _CommPlumbing(comm)
    nw = len(weights)

    def body(x_ref, g_ref, *rest):
        w_refs, c_in = rest[:nw], rest[nw: nw + plumb.n_in]
        o_ref, shard_refs = rest[nw + plumb.n_in], rest[nw + plumb.n_in + 1: 2 * nw + plumb.n_in + 1]
        c_out = rest[2 * nw + plumb.n_in + 1: 2 * nw + plumb.n_in + 1 + plumb.n_out]
        c_scr = rest[2 * nw + plumb.n_in + 1 + plumb.n_out:]
        plumb.handshake(pl.program_id(0) == 0)
        plumb.run(pl.program_id(0), steps, True, c_in, c_out, c_scr)

        @pl.when(pl.program_id(0) == 0)
        def _():
            for w_ref, s_ref, tr in zip(w_refs, shard_refs, transposes):
                v = w_ref[...]
                s_ref[...] = (v.T if tr else v).astype(BF)

        xv = x_ref[...]
        r = lax.rsqrt(jnp.mean(xv * xv, axis=-1, keepdims=True) + RMS_EPS)
        o_ref[...] = (xv * r * g_ref[...]).astype(BF)
        plumb.run(pl.program_id(0), steps, False, c_in, c_out, c_scr)

    row = pl.BlockSpec((ROW_TILE, D), lambda i: (i, 0))
    whole = lambda shape: pl.BlockSpec(shape, lambda i: (0, 0), pipeline_mode=pl.Buffered(1))
    shard_shapes = [w.shape[::-1] if tr else w.shape for w, tr in zip(weights, transposes)]
    res = pl.pallas_call(
        body, name=name, grid=(steps,),
        in_specs=[row, pl.BlockSpec((1, D), lambda i: (0, 0))] + [whole(w.shape) for w in weights] + [ANY] * plumb.n_in,
        out_specs=[row] + [whole(s) for s in shard_shapes] + [ANY] * plumb.n_out,
        out_shape=[jax.ShapeDtypeStruct((T, D), BF)] + [jax.ShapeDtypeStruct(s, BF) for s in shard_shapes] + plumb.out_shapes,
        scratch_shapes=plumb.scratch, compiler_params=_params(("arbitrary",), plumb.collective_id()),
    )(x, g, *weights, *plumb.args)
    return res[0], list(res[1: nw + 1]), plumb.split_outputs(res[nw + 1:])


HEADNORM_TILE = 2048


def _half_sum_matrix():
    r = lax.broadcasted_iota(jnp.int32, (LANES, LANES), 0) // HEAD_DIM
    c = lax.broadcasted_iota(jnp.int32, (LANES, LANES), 1) // HEAD_DIM
    return (r == c).astype(BF)


def _head_mean(v, ones_blockdiag):
    hi = v.astype(BF)
    lo = (v - hi.astype(F32)).astype(BF)
    s = jnp.dot(hi, ones_blockdiag, preferred_element_type=F32) + jnp.dot(lo, ones_blockdiag, preferred_element_type=F32)
    return s * (1.0 / HEAD_DIM)


def _headnorm_bwd(name, dy, proj, col0, width, g2, into):
    T = proj.shape[0]
    wide = min(width, GROUP_WIDTH)
    nb, off = width // wide, col0 // wide

    def body(dy_ref, x_ref, g_ref, b_ref, into_ref, dx_ref, dg_ref):
        for s in range(wide // LANES):
            lanes = slice(LANES * s, LANES * (s + 1))
            xv = x_ref[:, lanes].astype(F32)
            dyv = dy_ref[:, lanes].astype(F32)
            r = lax.rsqrt(_head_mean(xv * xv, b_ref[...]) + RMS_EPS)
            xhat = xv * r
            dxhat = dyv * g_ref[...]
            dx_ref[:, lanes] = (r * (dxhat - xhat * _head_mean(dxhat * xhat, b_ref[...]))).astype(BF)
            dg_ref[0, :, lanes] = jnp.sum(dyv * xhat, axis=0, keepdims=True)

    return pl.pallas_call(
        body, name=name, grid=(T // HEADNORM_TILE, nb),
        in_specs=[pl.BlockSpec((HEADNORM_TILE, wide), lambda i, j: (i, j)),
                  pl.BlockSpec((HEADNORM_TILE, wide), lambda i, j: (i, j + off)),
                  pl.BlockSpec((1, LANES), lambda i, j: (0, 0)), pl.BlockSpec((LANES, LANES), lambda i, j: (0, 0)), ANY],
        out_specs=[pl.BlockSpec((HEADNORM_TILE, wide), lambda i, j: (i, j + off)),
                   pl.BlockSpec((1, 1, wide), lambda i, j: (i, 0, j))],
        out_shape=[jax.ShapeDtypeStruct(into.shape, BF), jax.ShapeDtypeStruct((T // HEADNORM_TILE, 1, width), F32)],
        input_output_aliases={4: 0}, compiler_params=_params(("parallel", "parallel")),
    )(dy, proj, g2, _half_sum_matrix(), into)


def _shift_down(v, k, row):
    return jnp.where(row >= k, pltpu.roll(v, k, axis=0), 0.0)


def _shift_up(v, k, row, T):
    return jnp.where(row < T - k, pltpu.roll(v, T - k, axis=0), 0.0)


def _by_group(g, vals):
    out = vals[-1]
    for i in range(len(vals) - 2, -1, -1):
        out = jnp.where(g == i, vals[i], out)
    return out


def _pool_fwd(name, proj, pool_w, pool_scale):
    T = proj.shape[0]

    def body(x_ref, w_ref, s_ref, pooled_ref, mixed_ref):
        g = pl.program_id(0)
        xv = x_ref[...].astype(F32)
        row = lax.broadcasted_iota(jnp.int32, (T, 1), 0)
        s2 = xv + _shift_down(xv, 1, row)
        s4 = s2 + _shift_down(s2, 2, row)
        s8 = s4 + _shift_down(s4, 4, row)
        s16 = s8 + _shift_down(s8, 8, row)
        wsum = _by_group(g, [s2, s4, s8, s16])
        count = jnp.minimum(row + 1, 2 << g).astype(F32)
        pooled = (wsum / count - xv).astype(BF)
        pooled_ref[...] = pooled
        mixed = jnp.dot(pooled, w_ref[0].astype(BF), preferred_element_type=F32) * s_ref[...]
        mixed_ref[...] = mixed.astype(BF)

    col = pl.BlockSpec((T, POOL_GROUP), lambda g: (0, g))
    return pl.pallas_call(
        body, name=name, grid=(N_POOL_GROUPS,),
        in_specs=[col, pl.BlockSpec((1, POOL_GROUP, POOL_GROUP), lambda g: (g, 0, 0)),
                  pl.BlockSpec((1, POOL_GROUP), lambda g: (0, g))],
        out_specs=[col, col],
        out_shape=[jax.ShapeDtypeStruct((T, POOL_WIDTH), BF), jax.ShapeDtypeStruct((T, POOL_WIDTH), BF)],
        compiler_params=_params(("parallel",)),
    )(proj, pool_w, pool_scale)


def _pool_bwd(name, dmixed, pooled, pool_w, pool_scale, into):
    T = dmixed.shape[0]

    def body(dm_ref, p_ref, w_ref, s_ref, into_ref, dx_ref, dw_ref, ds_ref):
        g = pl.program_id(0)
        dm = dm_ref[...].astype(F32)
        pooled = p_ref[...]
        w = w_ref[0].astype(BF)
        pre = jnp.dot(pooled, w, preferred_element_type=F32)
        ds_ref[...] = jnp.sum(dm * pre, axis=0, keepdims=True)
        dms = (dm * s_ref[...]).astype(BF)
        dw_ref[0] = lax.dot_general(pooled, dms, _DIMS["tn"], preferred_element_type=F32)
        dpooled = lax.dot_general(dms, w, _DIMS["nt"], preferred_element_type=F32)
        row = lax.broadcasted_iota(jnp.int32, (T, 1), 0)
        count = jnp.minimum(row + 1, 2 << g).astype(F32)
        z = dpooled / count
        l2 = z + _shift_up(z, 1, row, T)
        l4 = l2 + _shift_up(l2, 2, row, T)
        l8 = l4 + _shift_up(l4, 4, row, T)
        l16 = l8 + _shift_up(l8, 8, row, T)
        dx_ref[...] = (_by_group(g, [l2, l4, l8, l16]) - dpooled).astype(BF)

    col = pl.BlockSpec((T, POOL_GROUP), lambda g: (0, g))
    wspec = pl.BlockSpec((1, POOL_GROUP, POOL_GROUP), lambda g: (g, 0, 0))
    sspec = pl.BlockSpec((1, POOL_GROUP), lambda g: (0, g))
    return pl.pallas_call(
        body, name=name, grid=(N_POOL_GROUPS,), in_specs=[col, col, wspec, sspec, ANY], out_specs=[col, wspec, sspec],
        out_shape=[jax.ShapeDtypeStruct(into.shape, BF),
                   jax.ShapeDtypeStruct((N_POOL_GROUPS, POOL_GROUP, POOL_GROUP), F32),
                   jax.ShapeDtypeStruct((1, POOL_WIDTH), F32)],
        input_output_aliases={4: 0}, compiler_params=_params(("parallel",)),
    )(dmixed, pooled, pool_w, pool_scale, into)


ATTN_SCALE = HEAD_DIM ** -0.5
MASKED = float(jnp.finfo(jnp.float32).min)
KV_COL_BLOCK_V = COL_V // LANES
GROUP_WIDTH = GQA_GROUP * HEAD_DIM


def _dup_head(v, j):
    half = lax.broadcasted_iota(jnp.int32, (1, LANES), 1) // HEAD_DIM
    return jnp.where(half == j, v, pltpu.roll(v, HEAD_DIM, axis=1))


def _stack_heads(v, low):
    pieces = []
    for p in range(GROUP_WIDTH // LANES):
        vp = v[:, LANES * p: LANES * (p + 1)]
        pieces.append(jnp.where(low, vp, jnp.zeros_like(vp)))
        pieces.append(jnp.where(low, jnp.zeros_like(vp), vp))
    return jnp.concatenate(pieces, axis=0)


def _unstack_transposed(t, low):
    pairs = []
    for p in range(GROUP_WIDTH // LANES):
        even = t[:, BLOCK * (2 * p): BLOCK * (2 * p + 1)].T
        odd = t[:, BLOCK * (2 * p + 1): BLOCK * (2 * p + 2)].T
        pairs.append(jnp.where(low, even, odd))
    return pairs


STACKED = GQA_GROUP * BLOCK


def _band_bias():
    key = lax.broadcasted_iota(jnp.int32, (2, 2 * BLOCK, STACKED), 1)
    qry = lax.broadcasted_iota(jnp.int32, (2, 2 * BLOCK, STACKED), 2) % BLOCK
    first = lax.broadcasted_iota(jnp.int32, (2, 2 * BLOCK, STACKED), 0) == 0
    valid = (key > qry) & (key <= qry + BLOCK) & (jnp.logical_not(first) | (key >= BLOCK))
    return jnp.where(valid, 0.0, MASKED).astype(F32)


def _softmax_keys_on_sublanes(k2, q, bias, sink_ref, j):
    head_of_lane = lax.broadcasted_iota(jnp.int32, (1, STACKED), 1) // BLOCK
    sink = jnp.zeros((1, STACKED), F32)
    for h in range(GQA_GROUP):
        sink = jnp.where(head_of_lane == h, sink_ref[j * GQA_GROUP + h], sink)
    s = lax.dot_general(k2, q, _DIMS["nt"], preferred_element_type=F32) + bias
    m = jnp.maximum(jnp.max(s, axis=0, keepdims=True), sink)
    e = jnp.exp(s - m)
    e_sink = jnp.exp(sink - m)
    inv = 1.0 / (jnp.sum(e, axis=0, keepdims=True) + e_sink)
    return e * inv, e_sink * inv


def _attn_fwd(name, proj, qg, kg, sinks, bias, comm):
    T = proj.shape[0]
    nb = T // BLOCK
    plumb = _CommPlumbing(comm)

    def body(sink_ref, bias_ref, ones_ref, qg_ref, kg_ref, q0_ref, q1_ref, kp_ref, kc_ref, vp_ref, vc_ref, *rest):
        c_in, (o_ref, q_ref, kn_ref) = rest[:plumb.n_in], rest[plumb.n_in: plumb.n_in + 3]
        c_out, c_scr = rest[plumb.n_in + 3: plumb.n_in + 3 + plumb.n_out], rest[plumb.n_in + 3 + plumb.n_out:]
        m = pl.program_id(0)
        plumb.handshake(m == 0)
        plumb.run(m, nb // 2, True, c_in, c_out, c_scr)
        low = lax.broadcasted_iota(jnp.int32, (1, LANES), 1) < HEAD_DIM

        def head_norm(raw, gain):
            xv = raw.astype(F32)
            return (xv * lax.rsqrt(_head_mean(xv * xv, ones_ref[...]) + RMS_EPS) * gain).astype(BF)

        for half, raw_ref in enumerate((q0_ref, q1_ref)):
            for s in range(GROUP_WIDTH // LANES):
                q_ref[:, GROUP_WIDTH * half + LANES * s: GROUP_WIDTH * half + LANES * (s + 1)] = head_norm(
                    raw_ref[:, LANES * s: LANES * (s + 1)], qg_ref[...])
        k_pair, k_prev = head_norm(kc_ref[...], kg_ref[...]), head_norm(kp_ref[...], kg_ref[...])
        kn_ref[...] = k_pair
        v_pair = vc_ref[...]
        for b in range(2):
            rows = slice(BLOCK * b, BLOCK * (b + 1))
            kk = k_pair if b else jnp.concatenate([k_prev, k_pair[0:BLOCK]], axis=0)
            vv = v_pair if b else jnp.concatenate([vp_ref[...], v_pair[0:BLOCK]], axis=0)
            bias = bias_ref[1] if b else bias_ref[jnp.minimum(m, 1)]
            for j in range(2):
                q = _stack_heads(q_ref[rows, GROUP_WIDTH * j: GROUP_WIDTH * (j + 1)], low)
                p, _ = _softmax_keys_on_sublanes(_dup_head(kk, j), q, bias, sink_ref, j)
                o_t = lax.dot_general(_dup_head(vv, j), p.astype(BF), _DIMS["tn"], preferred_element_type=F32)
                for pair, o in enumerate(_unstack_transposed(o_t, low)):
                    lanes = slice(GROUP_WIDTH * j + LANES * pair, GROUP_WIDTH * j + LANES * (pair + 1))
                    o_ref[rows, lanes] = o.astype(BF)
        plumb.run(m, nb // 2, False, c_in, c_out, c_scr)

    wide = pl.BlockSpec((2 * BLOCK, ATTN_WIDTH), lambda m: (m, 0))
    before = lambda m: jnp.maximum(2 * m - 1, 0)
    gain = pl.BlockSpec((1, LANES), lambda m: (0, 0))
    q_block, k_block = COL_Q // GROUP_WIDTH, COL_K // LANES
    res = pl.pallas_call(
        body, name=name, grid=(nb // 2,),
        in_specs=[pl.BlockSpec(memory_space=pltpu.SMEM),
                  pl.BlockSpec((2, 2 * BLOCK, STACKED), lambda m: (0, 0, 0)),
                  pl.BlockSpec((LANES, LANES), lambda m: (0, 0)), gain, gain,
                  pl.BlockSpec((2 * BLOCK, GROUP_WIDTH), lambda m: (m, q_block)),
                  pl.BlockSpec((2 * BLOCK, GROUP_WIDTH), lambda m: (m, q_block + 1)),
                  pl.BlockSpec((BLOCK, LANES), lambda m: (before(m), k_block)),
                  pl.BlockSpec((2 * BLOCK, LANES), lambda m: (m, k_block)),
                  pl.BlockSpec((BLOCK, LANES), lambda m: (before(m), KV_COL_BLOCK_V)),
                  pl.BlockSpec((2 * BLOCK, LANES), lambda m: (m, KV_COL_BLOCK_V))] + [ANY] * plumb.n_in,
        out_specs=[wide, wide, pl.BlockSpec((2 * BLOCK, LANES), lambda m: (m, 0))] + [ANY] * plumb.n_out,
        out_shape=[jax.ShapeDtypeStruct((T, ATTN_WIDTH), BF), jax.ShapeDtypeStruct((T, ATTN_WIDTH), BF),
                   jax.ShapeDtypeStruct((T, KV_WIDTH), BF)] + plumb.out_shapes,
        scratch_shapes=plumb.scratch, compiler_params=_params(("arbitrary",), plumb.collective_id()),
    )(sinks, bias, _half_sum_matrix(), qg, kg, proj, proj, proj, proj, proj, proj, *plumb.args)
    return list(res[:3]), plumb.split_outputs(res[3:])


def _attn_bwd(name, dout, qn, kn, proj, sinks, bias, comm):
    T = qn.shape[0]
    nb = T // BLOCK
    plumb = _CommPlumbing(comm)

    def body(sink_ref, bias_ref, do_ref, q_ref, kp_ref, kc_ref, vp_ref, vc_ref, *rest):
        c_in = rest[:plumb.n_in]
        dq_ref, k_own, k_before, v_own, v_before, dsink_ref = rest[plumb.n_in: plumb.n_in + 6]
        c_out, c_scr = rest[plumb.n_in + 6: plumb.n_in + 6 + plumb.n_out], rest[plumb.n_in + 6 + plumb.n_out:]
        m = pl.program_id(0)
        plumb.handshake(m == 0)
        plumb.run(m, nb // 2, True, c_in, c_out, c_scr)
        lane = lax.broadcasted_iota(jnp.int32, (1, LANES), 1)
        low = lane < HEAD_DIM

        @pl.when(m == 0)
        def _():
            dsink_ref[...] = jnp.zeros_like(dsink_ref)

        k_pair, v_pair = kc_ref[...], vc_ref[...]
        dsink = jnp.zeros((1, LANES), F32)
        for b in range(2):
            rows = slice(BLOCK * b, BLOCK * (b + 1))
            kk = k_pair if b else jnp.concatenate([kp_ref[...], k_pair[0:BLOCK]], axis=0)
            vv = v_pair if b else jnp.concatenate([vp_ref[...], v_pair[0:BLOCK]], axis=0)
            bias = bias_ref[1] if b else bias_ref[jnp.minimum(m, 1)]
            dk_tot = jnp.zeros((2 * BLOCK, LANES), F32)
            dv_tot = jnp.zeros((2 * BLOCK, LANES), F32)
            for j in range(2):
                k2 = _dup_head(kk, j)
                v2 = _dup_head(vv, j)
                q = _stack_heads(q_ref[rows, GROUP_WIDTH * j: GROUP_WIDTH * (j + 1)], low)
                do = _stack_heads(do_ref[rows, GROUP_WIDTH * j: GROUP_WIDTH * (j + 1)], low)
                p, psink = _softmax_keys_on_sublanes(k2, q, bias, sink_ref, j)
                dp =lax.dot_general(v2, do, _DIMS["nt"], preferred_element_type=F32)
                delta = jnp.sum(p * dp, axis=0, keepdims=True)
                ds = (p * (dp - delta)).astype(BF)
                dk2 = jnp.dot(ds, q, preferred_element_type=F32)
                dv2 = jnp.dot(p.astype(BF), do, preferred_element_type=F32)
                dq_t = lax.dot_general(k2, ds, _DIMS["tn"], preferred_element_type=F32)
                for pair, dq in enumerate(_unstack_transposed(dq_t, low)):
                    lanes = slice(GROUP_WIDTH * j + LANES * pair, GROUP_WIDTH * j + LANES * (pair + 1))
                    dq_ref[rows, lanes] = dq.astype(BF)
                mine = low if j == 0 else jnp.logical_not(low)
                dk_tot = dk_tot + jnp.where(mine, dk2 + pltpu.roll(dk2, HEAD_DIM, axis=1), 0.0)
                dv_tot = dv_tot + jnp.where(mine, dv2 + pltpu.roll(dv2, HEAD_DIM, axis=1), 0.0)
                sink_term = psink * delta
                for h in range(GQA_GROUP):
                    val = -jnp.sum(sink_term[:, BLOCK * h: BLOCK * (h + 1)], axis=1, keepdims=True)
                    dsink = dsink + jnp.where(lane == j * GQA_GROUP + h, val, 0.0)
            k_before[rows, :], k_own[rows, :] = dk_tot[0:BLOCK], dk_tot[BLOCK:]
            v_before[rows, :], v_own[rows, :] = dv_tot[0:BLOCK], dv_tot[BLOCK:]
        dsink_ref[0:1, :] += dsink
        plumb.run(m, nb // 2, False, c_in, c_out, c_scr)

    wide = pl.BlockSpec((2 * BLOCK, ATTN_WIDTH), lambda m: (m, 0))
    pair = pl.BlockSpec((2 * BLOCK, LANES), lambda m: (m, 0))
    before = lambda m: jnp.maximum(2 * m - 1, 0)
    res = pl.pallas_call(
        body, name=name, grid=(nb // 2,),
        in_specs=[pl.BlockSpec(memory_space=pltpu.SMEM),
                  pl.BlockSpec((2, 2 * BLOCK, STACKED), lambda m: (0, 0, 0)), wide, wide,
                  pl.BlockSpec((BLOCK, LANES), lambda m: (before(m), 0)), pair,
                  pl.BlockSpec((BLOCK, LANES), lambda m: (before(m), KV_COL_BLOCK_V)),
                  pl.BlockSpec((2 * BLOCK, LANES), lambda m: (m, KV_COL_BLOCK_V))] + [ANY] * plumb.n_in,
        out_specs=[wide, pair, pair, pair, pair, pl.BlockSpec((8, LANES), lambda m: (0, 0))] + [ANY] * plumb.n_out,
        out_shape=[jax.ShapeDtypeStruct((T, ATTN_WIDTH), BF)] + [jax.ShapeDtypeStruct((T, KV_WIDTH), F32)] * 4
        + [jax.ShapeDtypeStruct((8, LANES), F32)] + plumb.out_shapes,
        scratch_shapes=plumb.scratch, compiler_params=_params(("arbitrary",), plumb.collective_id()),
    )(sinks, bias, dout, qn, kn, kn, proj, proj, *plumb.args)
    return list(res[:6]), plumb.split_outputs(res[6:])


def _swiglu_fwd_epilogue(accs, ex):
    g, u = accs
    return [g, u, g * jax.nn.sigmoid(g) * u], []


def _swiglu_bwd_epilogue(accs, ex):
    (da,) = accs
    g, u = ex[0].astype(F32), ex[1].astype(F32)
    s = jax.nn.sigmoid(g)
    gs = g * s
    return [da * u * (s + gs - gs * s), da * gs], []


def _residual_norm_epilogue(scale):
    def epilogue(accs, ex):
        res, gain = ex
        h = res + scale * accs[0]
        r = lax.rsqrt(jnp.mean(h * h, axis=-1, keepdims=True) + RMS_EPS)
        return [h, h * r * gain], []
    return epilogue


def _rms_bwd_epilogue(accs, ex):
    (dn,) = accs
    xv, g, dres = ex
    r = lax.rsqrt(jnp.mean(xv * xv, axis=-1, keepdims=True) + RMS_EPS)
    xhat = xv * r
    dxhat = dn * g
    dx = dres + r * (dxhat - xhat * jnp.mean(dxhat * xhat, axis=-1, keepdims=True))
    return [dx, dx], [dn * xhat]


def _loss_epilogue(accs, ex):
    xv, target = ex
    d = xv + 0.5 * accs[0] - target
    dy = d * (1.0 / D_MODEL)
    return [dy, dy], [d * d]


def _merge_fwd_epilogue(accs, ex):
    ba, bp = accs
    gp_pre, ga_pre, bias_p, bias_a = ex
    gp = jax.nn.sigmoid(gp_pre.astype(F32) + bias_p)
    ga = jax.nn.sigmoid(ga_pre.astype(F32) + bias_a)
    return [gp * bp + ga * ba, ba, bp], []


def _merge_bwd_epilogue(accs, ex):
    (dm,) = accs
    bp, ba, gp_pre, ga_pre, bias_p, bias_a = ex
    gp = jax.nn.sigmoid(gp_pre.astype(F32) + bias_p)
    ga = jax.nn.sigmoid(ga_pre.astype(F32) + bias_a)
    dbp, dba = dm * gp, dm * ga
    dgp = dbp * bp.astype(F32) * (1.0 - gp)
    dga = dba * ba.astype(F32) * (1.0 - ga)
    return [dbp, dba, dgp, dga], [dgp, dga]


def _prep(name, ws, transposes):
    n = len(ws)

    def body(*refs):
        for w_ref, o_ref, tr in zip(refs[:n], refs[n:], transposes):
            v = w_ref[...]
            o_ref[...] = (v.T if tr else v).astype(BF)

    shapes = [jax.ShapeDtypeStruct(w.shape[::-1] if tr else w.shape, BF) for w, tr in zip(ws, transposes)]
    return pl.pallas_call(body, name=name, out_shape=shapes, compiler_params=_params())(*ws)


def _adam_math(w, g, m, v):
    m = ADAM_B1 * m + (1.0 - ADAM_B1) * g
    v = ADAM_B2 * v + (1.0 - ADAM_B2) * jnp.square(g)
    m_hat = m / (1.0 - ADAM_B1 ** ADAM_STEP)
    v_hat = v / (1.0 - ADAM_B2 ** ADAM_STEP)
    delta = -ADAM_LR * (m_hat / (jnp.sqrt(v_hat) + ADAM_EPS) + ADAM_WD * w)
    return delta, m, v


def _adamw_sharded(name, items, transpose=False):
    n = len(items)

    def body(*refs):
        ins, outs = refs[:4 * n], refs[4 * n:]
        for k in range(n):
            s_ref, w_ref, m_ref, v_ref = ins[4 * k: 4 * k + 4]
            g = s_ref[0].astype(F32)
            for i in range(1, 4):
                g = g + s_ref[i].astype(F32)
            if transpose:
                g = g.T
            delta, mn, vn = _adam_math(w_ref[...], g, m_ref[...], v_ref[...])
            for o_ref, val in zip(outs[4 * k: 4 * k + 4], (g, delta, mn, vn)):
                o_ref[...] = val

    flat = [a for item in items for a in item]
    out_shape = [jax.ShapeDtypeStruct(item[1].shape, F32) for item in items for _ in range(4)]
    _, r, C = items[0][0].shape
    rows = r // 4
    if transpose or rows % 8:
        res = pl.pallas_call(body, name=name, out_shape=out_shape, compiler_params=_params())(*flat)
    else:
        tile = pl.BlockSpec((rows, C), lambda i: (i, 0))
        res = pl.pallas_call(
            body, name=name, grid=(4,), in_specs=[pl.BlockSpec((4, rows, C), lambda i: (0, i, 0)), tile, tile, tile] * n,
            out_specs=[tile] * (4 * n), out_shape=out_shape, compiler_params=_params(("parallel",)),
        )(*flat)
    return [tuple(res[4 * k: 4 * k + 4]) for k in range(n)]


SMALL_LAYOUT = (("ffn1_norm", 0, (8, LANES)), ("mix_norm", 8, (8, LANES)), ("ffn2_norm", 16, (8, LANES)),
                ("gate_bias", 24, (16, LANES)), ("pool_scale", 40, (4, LANES)), ("q_norm", 48, (1, HEAD_DIM)),
                ("k_norm", 56, (1, HEAD_DIM)), ("sinks", 64, (1, N_HEADS)))
LOSS_ROW = 72
SMALL_ROWS = 80


def _adamw_small(name, g_vec, g_pool_w, params):
    n = len(SMALL_LAYOUT) + 1

    def body(vec_ref, pw_ref, *refs):
        ins, outs = refs[:3 * n], refs[3 * n:]
        vec = vec_ref[0]
        pw = pw_ref[0]
        for i in range(1, N_DEV):
            vec = vec + vec_ref[i]
            pw = pw + pw_ref[i]
        grads = [vec[r0:r0 + shape[0], 0:shape[1]] for _, r0, shape in SMALL_LAYOUT] + [pw]
        for p, g in enumerate(grads):
            w_ref, m_ref, v_ref = ins[3 * p: 3 * p + 3]
            delta, mn, vn = _adam_math(w_ref[...], g, m_ref[...], v_ref[...])
            for o_ref, val in zip(outs[4 * p: 4 * p + 4], (g, delta, mn, vn)):
                o_ref[...] = val
        outs[4 * n][...] = vec[LOSS_ROW:LOSS_ROW + 1, :]

    flat = [a for wmv in params for a in wmv]
    out_shape = [jax.ShapeDtypeStruct(wmv[0].shape, F32) for wmv in params for _ in range(4)]
    out_shape.append(jax.ShapeDtypeStruct((1, LANES), F32))
    res = pl.pallas_call(body, name=name, out_shape=out_shape, compiler_params=_params())(g_vec, g_pool_w, *flat)
    return [tuple(res[4 * p: 4 * p + 4]) for p in range(n)], res[4 * n]


def _place():
    x, y, c = lax.axis_index("x"), lax.axis_index("y"), lax.axis_index("c")
    other_chips = [(1 - x, y), (x, 1 - y), (1 - x, 1 - y)]
    return x, y, c, other_chips


def _rows(ref, r, place, natural=False):
    px, py, pc = place
    b = 4 * px + 2 * py + pc if natural else 4 * pc + 2 * px + py
    return ref.at[pl.ds(pl.multiple_of(b * r, 8), r), :]


def _gather_task(shards, natural=(), forward_at=0.75):
    n = len(shards)
    rs = [s.shape[0] for s in shards]
    rows_of = lambda ref, k, place: _rows(ref, rs[k], place, k in natural)

    def copy(scr, outs, k, slot, block, to, src=None):
        rows = rows_of(outs[k], k, block)
        return pltpu.make_async_remote_copy(
            src_ref=rows if src is None else src, dst_ref=rows, send_sem=scr[0].at[7 * k + slot],
            recv_sem=scr[1].at[7 * k + slot], device_id=to, device_id_type=MESH)

    def first_sends(ins, outs, scr):
        x, y, c, chips = _place()
        me = (x, y, c)
        cps = [copy(scr, outs, k, 1 + j, me, (*chip, c), src=ins[k]) for j, chip in enumerate(chips) for k in range(n)]
        return cps + [copy(scr, outs, k, 0, me, (x, y, 1 - c), src=ins[k]) for k in range(n)]

    def passed_on(outs, scr):
        x, y, c, chips = _place()
        return [copy(scr, outs, k, 4 + j, (*chip, c), (x, y, 1 - c)) for j, chip in enumerate(chips) for k in range(n)]

    def local(ins, outs, scr):
        x, y, c, _ = _place()
        return [pltpu.make_async_copy(ins[k], rows_of(outs[k], k, (x, y, c)), scr[2].at[k]) for k in range(n)]

    def start(ins, outs, scr):
        for cp in local(ins, outs, scr) + first_sends(ins, outs, scr):
            cp.start()

    def forward(ins, outs, scr):
        x, y, c, chips = _place()
        for j, chip in enumerate(chips):
            for k in range(n):
                copy(scr, outs, k, 1 + j, (*chip, c), (x, y, c)).wait_recv()
                copy(scr, outs, k, 4 + j, (*chip, c), (x, y, 1 - c)).start()

    def finish(ins, outs, scr):
        x, y, c, chips = _place()
        for k in range(n):
            copy(scr, outs, k, 0, (x, y, 1 - c), (x, y, c)).wait_recv()
        for j, chip in enumerate(chips):
            for k in range(n):
                copy(scr, outs, k, 4 + j, (*chip, 1 - c), (x, y, c)).wait_recv()
        for cp in first_sends(ins, outs, scr) + passed_on(outs, scr):
            cp.wait_send()
        for cp in local(ins, outs, scr):
            cp.wait()

    out_shapes = [jax.ShapeDtypeStruct((N_DEV * s.shape[0], s.shape[1]), s.dtype) for s in shards]
    scratch = [pltpu.SemaphoreType.DMA((7 * n,)), pltpu.SemaphoreType.DMA((7 * n,)), pltpu.SemaphoreType.DMA((n,))]
    return _Task(shards, out_shapes, scratch, [(0, start), (forward_at, forward), (1.0, finish)], ("sibling", "chips"))


def _direct_gather_task(shards):
    n = len(shards)
    rs = [s.shape[0] for s in shards]

    def peers():
        x, y, c, _ = _place()
        flip = lambda v, bit: 1 - v if bit else v
        return (x, y, c), [(flip(x, (s >> 2) & 1), flip(y, (s >> 1) & 1), flip(c, s & 1)) for s in range(1, N_DEV)]

    def copies(ins, outs, scr):
        me, others = peers()
        local = [pltpu.make_async_copy(ins[k], _rows(outs[k], rs[k], me), scr[2].at[k]) for k in range(n)]
        sems = lambda k, s: dict(send_sem=scr[0].at[7 * k + s], recv_sem=scr[1].at[7 * k + s], device_id_type=MESH)
        sends = [pltpu.make_async_remote_copy(src_ref=ins[k], dst_ref=_rows(outs[k], rs[k], me), device_id=to, **sems(k, s))
                 for s, to in enumerate(others) for k in range(n)]
        recvs = [pltpu.make_async_remote_copy(src_ref=_rows(outs[k], rs[k], frm), dst_ref=_rows(outs[k], rs[k], frm),
                                              device_id=me, **sems(k, s))
                 for s, frm in enumerate(others) for k in range(n)]
        return local, sends, recvs

    def start(ins, outs, scr):
        local, sends, _ = copies(ins, outs, scr)
        for cp in local + sends:
            cp.start()

    def finish(ins, outs, scr):
        local, sends, recvs = copies(ins, outs, scr)
        for cp in recvs:
            cp.wait_recv()
        for cp in sends:
            cp.wait_send()
        for cp in local:
            cp.wait()

    out_shapes = [jax.ShapeDtypeStruct((N_DEV * s.shape[0], s.shape[1]), s.dtype) for s in shards]
    scratch = [pltpu.SemaphoreType.DMA((7 * n,)), pltpu.SemaphoreType.DMA((7 * n,)), pltpu.SemaphoreType.DMA((n,))]
    return _Task(shards, out_shapes, scratch, [(0, start), (1.0, finish)], ("all",))


def _chip_task(sums):
    n = len(sums)
    rs = [s.shape[0] // 4 for s in sums]

    def block(ref, k, chip_index):
        return ref.at[pl.ds(pl.multiple_of(chip_index * rs[k], 8), rs[k]), :]

    def copies(ins, outs, scr):
        send_sems, recv_sems, local_sems = scr
        x, y, c, chips = _place()
        here = 2 * x + y
        local = [pltpu.make_async_copy(block(ins[k], k, here), outs[k].at[here], local_sems.at[k]) for k in range(n)]
        remote = []
        for j, (px, py) in enumerate(chips):
            remote += [pltpu.make_async_remote_copy(
                src_ref=block(ins[k], k, 2 * px + py), dst_ref=outs[k].at[here],
                send_sem=send_sems.at[3 * k + j], recv_sem=recv_sems.at[3 * k + j],
                device_id=(px, py, c), device_id_type=MESH) for k in range(n)]
        return local, remote

    def start(ins, outs, scr):
        local, remote = copies(ins, outs, scr)
        for cp in local + remote:
            cp.start()

    def finish(ins, outs, scr):
        local, remote = copies(ins, outs, scr)
        for cp in remote:
            cp.wait()
        for cp in local:
            cp.wait()

    out_shapes = [jax.ShapeDtypeStruct((4, r, s.shape[1]), s.dtype) for r, s in zip(rs, sums)]
    scratch = [pltpu.SemaphoreType.DMA((3 * n,)), pltpu.SemaphoreType.DMA((3 * n,)), pltpu.SemaphoreType.DMA((n,))]
    return _Task(sums, out_shapes, scratch, [(0, start), (1.0, finish)], ("chips",))


def _dw_pair(name, a, b, scale, comm=None, blocks=1):
    T, M = a.shape
    N = b.shape[1]
    half = M // 2
    wide = half // blocks
    tk = min(2048, T)
    nK = T // tk
    plumb = _CommPlumbing(comm)

    def body(core_ref, *rest):
        a_refs, b_ref, rest = rest[:blocks], rest[blocks], rest[blocks + 1:]
        c_in = rest[:plumb.n_in]
        o_ref = rest[plumb.n_in]
        c_out = rest[plumb.n_in + 1: plumb.n_in + 1 + plumb.n_out]
        acc, stage, land, send_sem, recv_sem = rest[plumb.n_in + 1 + plumb.n_out: plumb.n_in + 6 + plumb.n_out]
        c_scr = rest[plumb.n_in + 6 + plumb.n_out:]
        i, k = pl.program_id(0), pl.program_id(1)
        x, y, c, _ = _place()
        push = pltpu.make_async_remote_copy(src_ref=stage, dst_ref=land, send_sem=send_sem, recv_sem=recv_sem,
                                            device_id=(x, y, 1 - c), device_id_type=MESH)
        plumb.handshake((i == 0) & (k == 0), own=("sibling",))
        if comm:
            plumb.run(i * nK + k, 2 * nK, True, c_in, c_out, c_scr)

        av = a_refs[0][...] if blocks == 1 else jnp.concatenate([r[...] for r in a_refs], axis=1)
        p = lax.dot_general(av, b_ref[...], _DIMS["tn"], preferred_element_type=F32)

        @pl.when(k == 0)
        def _():
            acc[...] = p

        @pl.when(k > 0)
        def _():
            acc[...] += p

        @pl.when((i == 0) & (k == nK - 1))
        def _():
            stage[...] = (scale * acc[...]).astype(BF)
            push.start()

        @pl.when((i == 1) & (k == nK - 1))
        def _():
            push.wait_recv()
            o_ref[...] = (scale * acc[...] + land[...].astype(F32)).astype(BF)
            push.wait_send()

        if comm:
            plumb.run(i * nK + k, 2 * nK, False, c_in, c_out, c_scr)

    grid_spec = pltpu.PrefetchScalarGridSpec(
        num_scalar_prefetch=1, grid=(2, nK),
        in_specs=[pl.BlockSpec((tk, wide), functools.partial(
            lambda i, k, core, j: (k, (2 * j if blocks > 1 else 0) + jnp.where(i == 0, 1 - core[0], core[0])), j=j))
            for j in range(blocks)] + [pl.BlockSpec((tk, N), lambda i, k, core: (k, 0))] + [ANY] * plumb.n_in,
        out_specs=[pl.BlockSpec((half, N), lambda i, k, core: (0, 0))] + [ANY] * plumb.n_out,
        scratch_shapes=[pltpu.VMEM((half, N), F32), pltpu.VMEM((half, N), BF), pltpu.VMEM((half, N), BF),
                        pltpu.SemaphoreType.DMA, pltpu.SemaphoreType.DMA] + plumb.scratch)
    core = lax.axis_index("c").astype(jnp.int32).reshape(1)
    res = pl.pallas_call(
        body, name=name, grid_spec=grid_spec,
        out_shape=[jax.ShapeDtypeStruct((half, N), BF)] + plumb.out_shapes,
        compiler_params=_params(("arbitrary", "arbitrary"), plumb.collective_id(own=("sibling",))),
    )(core, *([a] * blocks), b, *plumb.args)
    return (res[0], plumb.split_outputs(res[1:])) if comm else res[0]


def _pair_task(parts):
    n = len(parts)

    def copies(ins, outs, scr):
        x, y, c, _ = _place()
        return [pltpu.make_async_remote_copy(
            src_ref=ins[k].at[:, pl.ds(1 - c, 1)], dst_ref=outs[k], send_sem=scr[0].at[k], recv_sem=scr[1].at[k],
            device_id=(x, y, 1 - c), device_id_type=MESH) for k in range(n)]

    def start(ins, outs, scr):
        for cp in copies(ins, outs, scr):
            cp.start()

    def finish(ins, outs, scr):
        for cp in copies(ins, outs, scr):
            cp.wait()

    out_shapes = [jax.ShapeDtypeStruct((4, 1) + p.shape[2:], p.dtype) for p in parts]
    scratch = [pltpu.SemaphoreType.DMA((n,)), pltpu.SemaphoreType.DMA((n,))]
    return _Task(parts, out_shapes, scratch, [(0, start), (1.0, finish)], ("sibling",))


def _pair_sum(name, part, got, core):
    _, _, r, C = part.shape

    def body(core_ref, p_ref, g_ref, o_ref):
        o_ref[0] = (p_ref[0, 0].astype(F32) + g_ref[0, 0].astype(F32)).astype(o_ref.dtype)

    return pl.pallas_call(
        body, name=name,
        grid_spec=pltpu.PrefetchScalarGridSpec(
            num_scalar_prefetch=1, grid=(4,),
            in_specs=[pl.BlockSpec((1, 1, r, C), lambda i, core_ref: (i, core_ref[0], 0, 0)),
                      pl.BlockSpec((1, 1, r, C), lambda i, core_ref: (i, 0, 0, 0))],
            out_specs=pl.BlockSpec((1, r, C), lambda i, core_ref: (i, 0, 0))),
        out_shape=jax.ShapeDtypeStruct((4, r, C), part.dtype), compiler_params=_params(("parallel",)),
    )(core, part, got)


def _ffn_bwd(tag, dy, dyb, x, gain, wgT, wuT, wd, saved, earlier=None):
    n, g, u, a = saved
    half = lambda accs, ex: _swiglu_bwd_epilogue([0.5 * accs[0]], ex)
    act_args = dict(tm=1024, tn=1408, tk=D_MODEL, epilogue=half, extras=[(g, "tile", 0), (u, "tile", 0)], cols_outer=True)
    if earlier is None:
        sum_d = _dw_pair(tag + "_dw_down", a, dyb, 0.5)
        (dg, du), ((slots_d,),) = _mm(tag + "_d_act", [(dyb, wd, "nt", 0)], [BF, BF], comm=[_chip_task([sum_d])], **act_args)
        slots_e = None
        sum_g = _dw_pair(tag + "_dw_gate", dg, n, 1.0)
    else:
        sum_d, ((got,),) = _dw_pair(tag + "_dw_down", a, dyb, 0.5, comm=[_pair_task([earlier])])
        core = lax.axis_index("c").astype(jnp.int32).reshape(1)
        sum_e = _pair_sum(tag + "_pair_sum_earlier", earlier, got, core)
        sum_e = sum_e.reshape(4 * sum_e.shape[1], sum_e.shape[2])
        (dg, du), ((slots_e,),) = _mm(tag + "_d_act", [(dyb, wd, "nt", 0)], [BF, BF], comm=[_chip_task([sum_e])], **act_args)
        sum_g, ((slots_d,),) = _dw_pair(tag + "_dw_gate", dg, n, 1.0, comm=[_chip_task([sum_d])])
    norm_args = dict(tm=512, tn=D_MODEL, tk=D_FF, epilogue=_rms_bwd_epilogue, n_colsum=1,
                     extras=[(x, "tile", 0), (gain, "row", 0), (dy, "tile", 0)])
    norm_terms = [(dg, wgT, "nn", 0), (du, wuT, "nn", 0)]
    if earlier is None:
        up = _dw_pair(tag + "_dw_up", du, n, 1.0)
        (dx, dxb, dgain), ((slots_g,),) = _mm(tag + "_d_norm", norm_terms, [F32, BF], comm=[_chip_task([sum_g])], **norm_args)
    else:
        sum_u, ((slots_g,),) = _dw_pair(tag + "_dw_up", du, n, 1.0, comm=[_chip_task([sum_g])])
        dx, dxb, dgain = _mm(tag + "_d_norm", norm_terms, [F32, BF], **norm_args)
        up = sum_u
    return dx, dxb, dgain, slots_e, slots_g, up, slots_d


def _tile_gain(g):
    return jnp.concatenate([g, g]).reshape(1, LANES)


def _fold_heads(partials):
    return jnp.sum(partials.reshape(-1, HEAD_DIM), axis=0)


def _pack_small_grads(grads, loss_local):
    pieces, row = [], 0
    for name, r0, _ in SMALL_LAYOUT + (("loss", LOSS_ROW, None),):
        v = (loss_local if name == "loss" else grads[name]).reshape(-1)
        rows = -(-v.size // LANES)
        block = jnp.pad(v, (0, rows * LANES - v.size)).reshape(rows, LANES)
        pieces += [jnp.zeros((r0 - row, LANES), F32)] * (r0 > row) + [block]
        row = r0 + rows
    pieces.append(jnp.zeros((SMALL_ROWS - row, LANES), F32))
    return jnp.concatenate(pieces, axis=0)


def kernel(x, ffn1_norm, ffn1_w_gate, ffn1_w_up, ffn1_w_down, mix_norm, w_in, pool_w, pool_scale, w_pool_out, q_norm, k_norm, sinks, w_attn_out, gate_bias, w_out, ffn2_norm, ffn2_w_gate, ffn2_w_up, ffn2_w_down, loss_target, m_ffn1_norm, m_ffn1_w_gate, m_ffn1_w_up, m_ffn1_w_down, m_mix_norm, m_w_in, m_pool_w, m_pool_scale, m_w_pool_out, m_q_norm, m_k_norm, m_sinks, m_w_attn_out, m_gate_bias, m_w_out, m_ffn2_norm, m_ffn2_w_gate, m_ffn2_w_up, m_ffn2_w_down, v_ffn1_norm, v_ffn1_w_gate, v_ffn1_w_up, v_ffn1_w_down, v_mix_norm, v_w_in, v_pool_w, v_pool_scale, v_w_pool_out, v_q_norm, v_k_norm, v_sinks, v_w_attn_out, v_gate_bias, v_w_out, v_ffn2_norm, v_ffn2_w_gate, v_ffn2_w_up, v_ffn2_w_down):
    T = x.shape[1]
    x2 = x.reshape(T, D_MODEL)
    target = loss_target.reshape(T, D_MODEL)

    big = [
        ("ffn1_w_gate", ffn1_w_gate, m_ffn1_w_gate, v_ffn1_w_gate, True, False),
        ("ffn1_w_up", ffn1_w_up, m_ffn1_w_up, v_ffn1_w_up, True, False),
        ("ffn1_w_down", ffn1_w_down, m_ffn1_w_down, v_ffn1_w_down, False, False),
        ("w_in", w_in, m_w_in, v_w_in, True, False),
        ("w_pool_out", w_pool_out, m_w_pool_out, v_w_pool_out, False, True),
        ("w_attn_out", w_attn_out, m_w_attn_out, v_w_attn_out, False, False),
        ("w_out", w_out, m_w_out, v_w_out, False, False),
        ("ffn2_w_gate", ffn2_w_gate, m_ffn2_w_gate, v_ffn2_w_gate, True, False),
        ("ffn2_w_up", ffn2_w_up, m_ffn2_w_up, v_ffn2_w_up, True, False),
        ("ffn2_w_down", ffn2_w_down, m_ffn2_w_down, v_ffn2_w_down, False, False),
    ]
    view = lambda a, tv: a.T if tv else a
    views = [view(w, tv) for _, w, _, _, tv, _ in big]
    in_kernel_t = [tk_ for *_, tk_ in big]
    first_shards = _prep("prep_ffn1_gate_up", views[0:2], in_kernel_t[0:2])
    g1 = ffn1_norm.reshape(1, D_MODEL)
    g2 = mix_norm.reshape(1, D_MODEL)
    g3 = ffn2_norm.reshape(1, D_MODEL)
    bias_row = gate_bias.reshape(1, 2 * D_MODEL)
    qg, kg = _tile_gain(q_norm) * ATTN_SCALE, _tile_gain(k_norm)
    scale_row = pool_scale.reshape(1, POOL_WIDTH)
    band_bias = _band_bias()

    n1, later_shards, ((wg1T, wu1T),) = _rms_fwd(
        "ffn1_norm", x2, g1, [_gather_task(first_shards, forward_at=0.9)], views[2:], in_kernel_t[2:])
    shards = list(first_shards) + later_shards
    (gt1, up1, act1), ((wd1,), (w_inT,)) = _mm(
        "ffn1_gate_up", [(n1, wg1T, "nt", 0), (n1, wu1T, "nt", 1)], [BF, BF, BF], tm=1024, tn=1408, tk=D_MODEL,
        epilogue=_swiglu_fwd_epilogue, cols_outer=True,
        comm=[_gather_task(shards[2:3], forward_at=0.5), _gather_task(shards[3:4], natural=(0,), forward_at=0.9)])
    (h1, u), ((w_poT, w_ao, w_o),) = _mm(
        "ffn1_down", [(act1, wd1, "nn", 0)], [F32, BF], tm=512, tn=D_MODEL, tk=D_FF,
        epilogue=_residual_norm_epilogue(0.5), extras=[(x2, "tile", 0), (g2, "row", 0)],
        comm=[_gather_task(shards[4:7], natural=(0, 1, 2), forward_at=0.8)])
    saved1 = (n1, gt1, up1, act1)
    (proj,), ((wg2T,),) = _mm(
        "in_proj", [(u, w_inT, "nt", 0)], [BF], tm=1024, tn=1280, tk=D_MODEL, cols_outer=True,
        comm=[_gather_task(shards[7:8], forward_at=0.8)])
    pooled, mixed = _pool_fwd("pool_fwd", proj, pool_w, scale_row)
    (attn, qn, kn), ((wu2T,),) = _attn_fwd("attn_fwd", proj, qg, kg, sinks, band_bias,
                                           [_gather_task(shards[8:9], forward_at=0.8)])
    gate_tn = 256
    gate_extras = [(proj, "tile", COL_GP // gate_tn), (proj, "tile", COL_GA // gate_tn),
                   (bias_row, "row", 0), (bias_row, "row", D_MODEL // gate_tn)]
    merged, ba, bp = _mm("branch_out_merge", [(attn, w_ao, "nn", 0), (mixed, w_poT, "nt", 1)], [BF, BF, BF],
                         tm=2048, tn=gate_tn, tk=ATTN_WIDTH, epilogue=_merge_fwd_epilogue, extras=gate_extras)
    h2, n2 = _mm("mix_out", [(merged, w_o, "nn", 0)], [F32, BF], tm=1024, tn=D_MODEL, tk=D_MODEL,
                 epilogue=_residual_norm_epilogue(1.0), extras=[(h1, "tile", 0), (g3, "row", 0)])
    (gt2, up2, act2), ((wd2,),) = _mm(
        "ffn2_gate_up", [(n2, wg2T, "nt", 0), (n2, wu2T, "nt", 1)], [BF, BF, BF], tm=1024, tn=1408, tk=D_MODEL,
        epilogue=_swiglu_fwd_epilogue, cols_outer=True, comm=[_gather_task(shards[9:10], forward_at=0.8)])
    dy, dyb, sq = _mm("ffn2_down_loss", [(act2, wd2, "nn", 0)], [F32, BF], tm=512, tn=D_MODEL, tk=D_FF,
                      epilogue=_loss_epilogue, extras=[(h2, "tile", 0), (target, "tile", 0)], n_colsum=1)
    loss_local = 0.5 * jnp.sum(sq) / D_MODEL

    dh2, dh2b, dg3, _, slots_g2, sum_u2, slots_d2 = _ffn_bwd(
        "ffn2", dy, dyb, h2, g3, wg2T, wu2T, wd2, (n2, gt2, up2, act2))
    (dbp, dba, dproj, dga, cs_gp, cs_ga), ((slots_u2,),) = _mm(
        "mix_out_bwd", [(dh2b, w_o, "nt", 0)], [BF, BF, BF, BF], tm=2048, tn=gate_tn, tk=D_MODEL,
        epilogue=_merge_bwd_epilogue, extras=[(bp, "tile", 0), (ba, "tile", 0)] + gate_extras, n_colsum=2,
        out_placement={2: (IN_WIDTH, COL_GP)}, comm=[_chip_task([sum_u2])])
    sum_o = _dw_pair("dw_out", merged, dh2b, 1.0, blocks=4)
    (dmixed,) = _mm("pool_out_bwd", [(dbp, w_poT, "nn", 0)], [BF], tm=1024, tn=POOL_WIDTH, tk=D_MODEL)
    sum_po = _dw_pair("dw_pool_out", dbp, mixed, 1.0, blocks=4)
    (dattn,) = _mm("attn_out_bwd", [(dba, w_ao, "nt", 0)], [BF], tm=1024, tn=ATTN_WIDTH, tk=D_MODEL)
    sum_ao = _dw_pair("dw_attn_out", attn, dba, 1.0, blocks=4)
    (dqn, k_own, k_before, v_own, v_before, dsink_tile), ((slots_o, slots_po, slots_ao),) = _attn_bwd(
        "attn_bwd", dattn, qn, kn, proj, sinks, band_bias, [_chip_task([sum_o, sum_po, sum_ao])])
    next_block = lambda a: jnp.concatenate([a[BLOCK:], jnp.zeros((BLOCK, KV_WIDTH), F32)], axis=0)
    dkn = (k_own + next_block(k_before)).astype(BF)
    dv = (v_own + next_block(v_before)).astype(BF)
    dproj, dqg = _headnorm_bwd("q_norm_bwd", dqn, proj, COL_Q, ATTN_WIDTH, qg, dproj)
    dproj, dkg = _headnorm_bwd("k_norm_bwd", dkn, proj, COL_K, KV_WIDTH, kg, dproj)
    dproj, dpool_w, dpool_scale = _pool_bwd("pool_bwd", dmixed, pooled, pool_w, scale_row, dproj)
    for piece, col in ((dv, COL_V), (dga, COL_GA)):
        dproj = lax.dynamic_update_slice(dproj, piece, (0, col))
    (dh1, dh1b, dg2), ((g_pool_w,),) = _mm(
        "in_proj_bwd", [(dproj, w_inT, "nn", 0)], [F32, BF], tm=512, tn=D_MODEL, tk=IN_WIDTH, epilogue=_rms_bwd_epilogue,
        extras=[(h1, "tile", 0), (g2, "row", 0), (dh2, "tile", 0)], n_colsum=1,
        comm=[_gather_task([dpool_w.reshape(-1, LANES)])])
    (dw_inT,) = _mm("dw_in", [(dproj, u, "tn", 0)], [BF], tm=1920, tn=D_MODEL, tk=2048)
    dx, _, dg1, slots_in, slots_g1, sum_u1, slots_d1 = _ffn_bwd(
        "ffn1", dh1, dh1b, x2, g1, wg1T, wu1T, wd1, saved1, dw_inT.reshape(4, 2, IN_WIDTH // N_DEV, D_MODEL))

    small_grads = {
        "ffn1_norm": jnp.sum(dg1, axis=(0, 1)), "mix_norm": jnp.sum(dg2, axis=(0, 1)), "ffn2_norm": jnp.sum(dg3, axis=(0, 1)),
        "gate_bias": jnp.concatenate([jnp.sum(cs_gp, axis=(0, 1)), jnp.sum(cs_ga, axis=(0, 1))]),
        "pool_scale": dpool_scale, "q_norm": _fold_heads(dqg) * ATTN_SCALE, "k_norm": _fold_heads(dkg),
        "sinks": dsink_tile[0, :N_HEADS]}
    (g_vec,), (slots_u1,) = _comm_only(
        "gather_small_grads",
        [_direct_gather_task([_pack_small_grads(small_grads, loss_local)]), _chip_task([sum_u1])])

    slots = [slots_g1, slots_u1, slots_d1, slots_in, slots_po, slots_ao, slots_o, slots_g2, slots_u2, slots_d2]
    big_out = {}
    for label, group in (("ffn", (0, 1, 2, 7, 8, 9)), ("w_in", (3,)), ("w_pool_out", (4,)), ("attn_out_and_out", (5, 6))):
        items = [(slots[k], view(big[k][1], big[k][4]), view(big[k][2], big[k][4]), view(big[k][3], big[k][4]))
                 for k in group]
        for k, res in zip(group, _adamw_sharded("adamw_" + label, items, transpose=big[group[0]][5])):
            big_out[big[k][0]] = tuple(view(r, big[k][4]) for r in res)

    given = {"ffn1_norm": (ffn1_norm, m_ffn1_norm, v_ffn1_norm), "mix_norm": (mix_norm, m_mix_norm, v_mix_norm),
             "ffn2_norm": (ffn2_norm, m_ffn2_norm, v_ffn2_norm), "gate_bias": (gate_bias, m_gate_bias, v_gate_bias),
             "pool_scale": (pool_scale, m_pool_scale, v_pool_scale), "q_norm": (q_norm, m_q_norm, v_q_norm),
             "k_norm": (k_norm, m_k_norm, v_k_norm), "sinks": (sinks, m_sinks, v_sinks)}
    params = [tuple(a.reshape(shape) for a in given[nm]) for nm, _, shape in SMALL_LAYOUT]
    params.append(tuple(a.reshape(-1, LANES) for a in (pool_w, m_pool_w, v_pool_w)))
    small_res, loss_row = _adamw_small("adamw_small", g_vec.reshape(N_DEV, SMALL_ROWS, LANES),
                                       g_pool_w.reshape(N_DEV, -1, LANES), params)
    small_out = {nm: tuple(r.reshape(given[nm][0].shape) for r in res)
                 for (nm, _, _), res in zip(SMALL_LAYOUT, small_res)}
    small_out["pool_w"] = tuple(r.reshape(pool_w.shape) for r in small_res[-1])
    loss = loss_row[0, 0]

    order = ["ffn1_norm", "ffn1_w_gate", "ffn1_w_up", "ffn1_w_down", "mix_norm", "w_in", "pool_w", "pool_scale",
             "w_pool_out", "q_norm", "k_norm", "sinks", "w_attn_out", "gate_bias", "w_out", "ffn2_norm",
             "ffn2_w_gate", "ffn2_w_up", "ffn2_w_down"]
    every = {**big_out, **small_out}
    outs = [loss, dx.reshape(x.shape)]
    for j in range(4):
        outs += [every[nm][j] for nm in order]
    return tuple(outs)
```

```python
import functools

import jax
import jax.numpy as jnp
from jax import lax
from jax.experimental import pallas as pl
from jax.experimental.pallas import tpu as pltpu

BF = jnp.bfloat16
F32 = jnp.float32

D_MODEL = 1024
D_FF = 2816
POOL_WIDTH = 512
POOL_GROUP = 128
N_POOL_GROUPS = 4
HEAD_DIM = 64
N_HEADS = 16
GQA_GROUP = 8
BLOCK = 128
ATTN_WIDTH = 1024
KV_WIDTH = 128
IN_WIDTH = 3840
RMS_EPS = 1e-6
N_DEV = 8
LANES = 128

COL_Q = POOL_WIDTH
COL_K = COL_Q + ATTN_WIDTH
COL_V = COL_K + KV_WIDTH
COL_GP = COL_V + KV_WIDTH
COL_GA = COL_GP + D_MODEL

ADAM_LR = 0.001
ADAM_B1 = 0.9
ADAM_B2 = 0.999
ADAM_EPS = 1e-08
ADAM_WD = 0.01
ADAM_STEP = 10

VMEM_LIMIT_V7X = 56 * 1024 * 1024
MESH = pl.DeviceIdType.MESH
ANY = pl.BlockSpec(memory_space=pl.ANY)


def _params(sem=None, collective_id=None):
    return pltpu.CompilerParams(dimension_semantics=sem, vmem_limit_bytes=VMEM_LIMIT_V7X, collective_id=collective_id)


COLLECTIVE_IDS = {frozenset(["sibling"]): 0, frozenset(["chips"]): 1, frozenset(["sibling", "chips"]): 2}


def _handshake(peer_kinds):
    x, y, c, chips = _place()
    peers = ([(x, y, 1 - c)] if "sibling" in peer_kinds else []) + ([(*chip, c) for chip in chips] if "chips" in peer_kinds else [])
    barrier = pltpu.get_barrier_semaphore()
    for peer in peers:
        pl.semaphore_signal(barrier, inc=1, device_id=peer, device_id_type=MESH)
    pl.semaphore_wait(barrier, len(peers))


_DIMS = {"nt": (((1,), (1,)), ((), ())), "nn": (((1,), (0,)), ((), ())), "tn": (((0,), (0,)), ((), ()))}


class _Task:
    def __init__(self, inputs, out_shapes, scratch, phases, peers):
        self.inputs, self.out_shapes, self.scratch = list(inputs), list(out_shapes), list(scratch)
        self.phases = list(phases)
        self.peers = frozenset(peers)


class _CommPlumbing:
    def __init__(self, tasks):
        self.tasks = list(tasks or [])
        self.args = [a for t in self.tasks for a in t.inputs]
        self.out_shapes = [o for t in self.tasks for o in t.out_shapes]
        self.scratch = [s for t in self.tasks for s in t.scratch]
        self.n_in, self.n_out = len(self.args), len(self.out_shapes)

    def peer_kinds(self, own=()):
        kinds = frozenset(own).union(*[t.peers for t in self.tasks])
        return None if "all" in kinds or not kinds else kinds

    def collective_id(self, own=()):
        kinds = self.peer_kinds(own)
        return None if kinds is None else COLLECTIVE_IDS[kinds]

    def handshake(self, first, own=()):
        kinds = self.peer_kinds(own)
        if kinds is not None:
            pl.when(first)(functools.partial(_handshake, kinds))

    def _slices(self, c_in, c_out, c_scr):
        i = o = s = 0
        for t in self.tasks:
            yield t, c_in[i:i + len(t.inputs)], c_out[o:o + len(t.out_shapes)], c_scr[s:s + len(t.scratch)]
            i, o, s = i + len(t.inputs), o + len(t.out_shapes), s + len(t.scratch)

    def run(self, step, steps, before, c_in, c_out, c_scr):
        for t, ins, outs, scr in self._slices(c_in, c_out, c_scr):
            for frac, fn in t.phases:
                if step is None:
                    fn(ins, outs, scr)
                elif before == (frac == 0):
                    at = 0 if frac == 0 else max(0, min(steps, -(-int(round(frac * steps * 64)) // 64)) - 1)
                    pl.when(step == at)(functools.partial(fn, ins, outs, scr))

    def split_outputs(self, flat):
        res, o = [], 0
        for t in self.tasks:
            res.append(list(flat[o:o + len(t.out_shapes)]))
            o += len(t.out_shapes)
        return res


def _comm_only(name, tasks):
    plumb = _CommPlumbing(tasks)

    def body(*refs):
        c_in, c_out = refs[:plumb.n_in], refs[plumb.n_in: plumb.n_in + plumb.n_out]
        c_scr = refs[plumb.n_in + plumb.n_out:]
        plumb.run(None, 1, True, c_in, c_out, c_scr)

    res = pl.pallas_call(
        body, name=name, in_specs=[ANY] * plumb.n_in, out_specs=[ANY] * plumb.n_out, out_shape=plumb.out_shapes,
        scratch_shapes=plumb.scratch, compiler_params=pltpu.CompilerParams(has_side_effects=True),
    )(*plumb.args)
    return plumb.split_outputs(res)


def _mm(name, terms, out_dtypes, *, tm, tn, tk, epilogue=None, extras=(), n_colsum=0, comm=None, cols_outer=False,
        out_placement=None):
    a0, b0, mode0, _ = terms[0]
    if mode0 == "nt":
        (M, K), N = a0.shape, b0.shape[0]
    elif mode0 == "nn":
        (M, K), N = a0.shape, b0.shape[1]
    else:
        (K, M), N = a0.shape, b0.shape[1]
    tm, tn, tk = min(tm, M), min(tn, N), min(tk, K)
    assert M % tm == 0 and N % tn == 0 and K % tk == 0, (name, M, N, K, tm, tn, tk)
    nI, nJ, nK = M // tm, N // tn, K // tk
    n_terms = len(terms)
    n_acc = max(t[3] for t in terms) + 1
    n_ex = len(extras)
    n_out = len(out_dtypes)
    if epilogue is None:
        epilogue = lambda accs, ex: ([accs[0]], [])
    plumb = _CommPlumbing(comm)
    n_scr = n_acc if nK > 1 else 0
    grid = (nJ, nI, nK) if cols_outer else (nI, nJ, nK)

    def body(*refs):
        n_in = 2 * n_terms + n_ex
        ab = refs[: 2 * n_terms]
        ex_refs = refs[2 * n_terms: n_in]
        c_in = refs[n_in: n_in + plumb.n_in]
        o0 = n_in + plumb.n_in
        out_refs = refs[o0: o0 + n_out]
        cs_refs = refs[o0 + n_out: o0 + n_out + n_colsum]
        c_out = refs[o0 + n_out + n_colsum: o0 + n_out + n_colsum + plumb.n_out]
        s0 = o0 + n_out + n_colsum + plumb.n_out
        acc_refs = refs[s0: s0 + n_scr]
        c_scr = refs[s0 + n_scr:]
        steps = grid[0] * grid[1] * nK
        if comm:
            step = (pl.program_id(0) * grid[1] + pl.program_id(1)) * nK + pl.program_id(2)
            plumb.handshake(step == 0)
            plumb.run(step, steps, True, c_in, c_out, c_scr)

        def products():
            accs = [None] * n_acc
            for t, (_, _, mode, ai) in enumerate(terms):
                p = lax.dot_general(ab[2 * t][...], ab[2 * t + 1][...], _DIMS[mode], preferred_element_type=F32)
                accs[ai] = p if accs[ai] is None else accs[ai] + p
            return accs

        def finish(accs):
            outs, colsums = epilogue(accs, [r[...] for r in ex_refs])
            for r, o in zip(out_refs, outs):
                r[...] = o.astype(r.dtype)
            for r, cs in zip(cs_refs, colsums):
                r[...] = jnp.sum(cs, axis=0, keepdims=True).reshape(r.shape)

        if nK == 1:
            finish(products())
        else:
            k = pl.program_id(2)
            accs = products()

            @pl.when(k == 0)
            def _():
                for r, a in zip(acc_refs, accs):
                    r[...] = a

            @pl.when(k > 0)
            def _():
                for r, a in zip(acc_refs, accs):
                    r[...] += a

            @pl.when(k == nK - 1)
            def _():
                finish([r[...] for r in acc_refs])

        if comm:
            plumb.run(step, steps, False, c_in, c_out, c_scr)

    def spec(block, index, fixed=False):
        imap = (lambda q, p, k: index(p, q, k)) if cols_outer else index
        return pl.BlockSpec(block, imap, pipeline_mode=pl.Buffered(1)) if fixed else pl.BlockSpec(block, imap)

    in_specs, args = [], []
    for a, b, mode, _ in terms:
        kt = tk if nK > 1 else (a.shape[0] if mode == "tn" else a.shape[1])
        if mode == "nt":
            in_specs += [spec((tm, kt), lambda i, j, k: (i, k), nI * nK == 1),
                         spec((tn, kt), lambda i, j, k: (j, k), nJ * nK == 1)]
        elif mode == "nn":
            in_specs += [spec((tm, kt), lambda i, j, k: (i, k), nI * nK == 1),
                         spec((kt, tn), lambda i, j, k: (k, j), nJ * nK == 1)]
        else:
            in_specs += [spec((kt, tm), lambda i, j, k: (k, i), nI * nK == 1),
                         spec((kt, tn), lambda i, j, k: (k, j), nJ * nK == 1)]
        args += [a, b]
    for arr, kind, off in extras:
        if kind == "tile":
            in_specs.append(spec((tm, tn), functools.partial(lambda i, j, k, off: (i, j + off), off=off)))
        else:
            in_specs.append(spec((1, tn), functools.partial(lambda i, j, k, off: (0, j + off), off=off)))
        args.append(arr)
    placed = dict(out_placement or {})
    out_shape = [jax.ShapeDtypeStruct((M, placed.get(o, (N, 0))[0]), dt) for o, dt in enumerate(out_dtypes)]
    out_specs = [spec((tm, tn), functools.partial(lambda i, j, k, off: (i, j + off), off=placed.get(o, (N, 0))[1] // tn))
                 for o in range(n_out)]
    out_shape += [jax.ShapeDtypeStruct((nI, 1, N), F32) for _ in range(n_colsum)]
    out_specs += [spec((1, 1, tn), lambda i, j, k: (i, 0, j)) for _ in range(n_colsum)]
    scratch = [pltpu.VMEM((tm, tn), F32) for _ in range(n_scr)]
    args += plumb.args
    in_specs += [ANY] * plumb.n_in
    out_shape += plumb.out_shapes
    out_specs += [ANY] * plumb.n_out
    sem = ("arbitrary",) * 3 if comm else ("parallel", "parallel", "arbitrary")
    res = pl.pallas_call(
        body, name=name, grid=grid, in_specs=in_specs, out_specs=out_specs, out_shape=out_shape,
        scratch_shapes=scratch + plumb.scratch, compiler_params=_params(sem, plumb.collective_id()),
    )(*args)
    n_own = n_out + n_colsum
    return (list(res[:n_own]), plumb.split_outputs(res[n_own:])) if comm is not None else res


ROW_TILE = 512


def _rms_fwd(name, x, g, comm, weights, transposes):
    T, D = x.shape
    steps = T // ROW_TILE
    plumb = _CommPlumbing(comm)
    nw = len(weights)

    def body(x_ref, g_ref, *rest):
        w_refs, c_in = rest[:nw], rest[nw: nw + plumb.n_in]
        o_ref, shard_refs = rest[nw + plumb.n_in], rest[nw + plumb.n_in + 1: 2 * nw + plumb.n_in + 1]
        c_out = rest[2 * nw + plumb.n_in + 1: 2 * nw + plumb.n_in + 1 + plumb.n_out]
        c_scr = rest[2 * nw + plumb.n_in + 1 + plumb.n_out:]
        plumb.handshake(pl.program_id(0) == 0)
        plumb.run(pl.program_id(0), steps, True, c_in, c_out, c_scr)

        @pl.when(pl.program_id(0) == 0)
        def _():
            for w_ref, s_ref, tr in zip(w_refs, shard_refs, transposes):
                v = w_ref[...]
                s_ref[...] = (v.T if tr else v).astype(BF)

        xv = x_ref[...]
        r = lax.rsqrt(jnp.mean(xv * xv, axis=-1, keepdims=True) + RMS_EPS)
        o_ref[...] = (xv * r * g_ref[...]).astype(BF)
        plumb.run(pl.program_id(0), steps, False, c_in, c_out, c_scr)

    row = pl.BlockSpec((ROW_TILE, D), lambda i: (i, 0))
    whole = lambda shape: pl.BlockSpec(shape, lambda i: (0, 0), pipeline_mode=pl.Buffered(1))
    shard_shapes = [w.shape[::-1] if tr else w.shape for w, tr in zip(weights, transposes)]
    res = pl.pallas_call(
        body, name=name, grid=(steps,),
        in_specs=[row, pl.BlockSpec((1, D), lambda i: (0, 0))] + [whole(w.shape) for w in weights] + [ANY] * plumb.n_in,
        out_specs=[row] + [whole(s) for s in shard_shapes] + [ANY] * plumb.n_out,
        out_shape=[jax.ShapeDtypeStruct((T, D), BF)] + [jax.ShapeDtypeStruct(s, BF) for s in shard_shapes] + plumb.out_shapes,
        scratch_shapes=plumb.scratch, compiler_params=_params(("arbitrary",), plumb.collective_id()),
    )(x, g, *weights, *plumb.args)
    return res[0], list(res[1: nw + 1]), plumb.split_outputs(res[nw + 1:])


HEADNORM_TILE = 2048


def _half_sum_matrix():
    r = lax.broadcasted_iota(jnp.int32, (LANES, LANES), 0) // HEAD_DIM
    c = lax.broadcasted_iota(jnp.int32, (LANES, LANES), 1) // HEAD_DIM
    return (r == c).astype(BF)


def _head_mean(v, ones_blockdiag):
    hi = v.astype(BF)
    lo = (v - hi.astype(F32)).astype(BF)
    s = jnp.dot(hi, ones_blockdiag, preferred_element_type=F32) + jnp.dot(lo, ones_blockdiag, preferred_element_type=F32)
    return s * (1.0 / HEAD_DIM)


def _headnorm_bwd(name, dy, proj, col0, width, g2, into):
    T = proj.shape[0]
    wide = min(width, GROUP_WIDTH)
    nb, off = width // wide, col0 // wide

    def body(dy_ref, x_ref, g_ref, b_ref, into_ref, dx_ref, dg_ref):
        for s in range(wide // LANES):
            lanes = slice(LANES * s, LANES * (s + 1))
            xv = x_ref[:, lanes].astype(F32)
            dyv = dy_ref[:, lanes].astype(F32)
            r = lax.rsqrt(_head_mean(xv * xv, b_ref[...]) + RMS_EPS)
            xhat = xv * r
            dxhat = dyv * g_ref[...]
            dx_ref[:, lanes] = (r * (dxhat - xhat * _head_mean(dxhat * xhat, b_ref[...]))).astype(BF)
            dg_ref[0, :, lanes] = jnp.sum(dyv * xhat, axis=0, keepdims=True)

    return pl.pallas_call(
        body, name=name, grid=(T // HEADNORM_TILE, nb),
        in_specs=[pl.BlockSpec((HEADNORM_TILE, wide), lambda i, j: (i, j)),
                  pl.BlockSpec((HEADNORM_TILE, wide), lambda i, j: (i, j + off)),
                  pl.BlockSpec((1, LANES), lambda i, j: (0, 0)), pl.BlockSpec((LANES, LANES), lambda i, j: (0, 0)), ANY],
        out_specs=[pl.BlockSpec((HEADNORM_TILE, wide), lambda i, j: (i, j + off)),
                   pl.BlockSpec((1, 1, wide), lambda i, j: (i, 0, j))],
        out_shape=[jax.ShapeDtypeStruct(into.shape, BF), jax.ShapeDtypeStruct((T // HEADNORM_TILE, 1, width), F32)],
        input_output_aliases={4: 0}, compiler_params=_params(("parallel", "parallel")),
    )(dy, proj, g2, _half_sum_matrix(), into)


def _shift_down(v, k, row):
    return jnp.where(row >= k, pltpu.roll(v, k, axis=0), 0.0)


def _shift_up(v, k, row, T):
    return jnp.where(row < T - k, pltpu.roll(v, T - k, axis=0), 0.0)


def _by_group(g, vals):
    out = vals[-1]
    for i in range(len(vals) - 2, -1, -1):
        out = jnp.where(g == i, vals[i], out)
    return out


def _pool_fwd(name, proj, pool_w, pool_scale):
    T = proj.shape[0]

    def body(x_ref, w_ref, s_ref, pooled_ref, mixed_ref):
        g = pl.program_id(0)
        xv = x_ref[...].astype(F32)
        row = lax.broadcasted_iota(jnp.int32, (T, 1), 0)
        s2 = xv + _shift_down(xv, 1, row)
        s4 = s2 + _shift_down(s2, 2, row)
        s8 = s4 + _shift_down(s4, 4, row)
        s16 = s8 + _shift_down(s8, 8, row)
        wsum = _by_group(g, [s2, s4, s8, s16])
        count = jnp.minimum(row + 1, 2 << g).astype(F32)
        pooled = (wsum / count - xv).astype(BF)
        pooled_ref[...] = pooled
        mixed = jnp.dot(pooled, w_ref[0].astype(BF), preferred_element_type=F32) * s_ref[...]
        mixed_ref[...] = mixed.astype(BF)

    col = pl.BlockSpec((T, POOL_GROUP), lambda g: (0, g))
    return pl.pallas_call(
        body, name=name, grid=(N_POOL_GROUPS,),
        in_specs=[col, pl.BlockSpec((1, POOL_GROUP, POOL_GROUP), lambda g: (g, 0, 0)),
                  pl.BlockSpec((1, POOL_GROUP), lambda g: (0, g))],
        out_specs=[col, col],
        out_shape=[jax.ShapeDtypeStruct((T, POOL_WIDTH), BF), jax.ShapeDtypeStruct((T, POOL_WIDTH), BF)],
        compiler_params=_params(("parallel",)),
    )(proj, pool_w, pool_scale)


def _pool_bwd(name, dmixed, pooled, pool_w, pool_scale, into):
    T = dmixed.shape[0]

    def body(dm_ref, p_ref, w_ref, s_ref, into_ref, dx_ref, dw_ref, ds_ref):
        g = pl.program_id(0)
        dm = dm_ref[...].astype(F32)
        pooled = p_ref[...]
        w = w_ref[0].astype(BF)
        pre = jnp.dot(pooled, w, preferred_element_type=F32)
        ds_ref[...] = jnp.sum(dm * pre, axis=0, keepdims=True)
        dms = (dm * s_ref[...]).astype(BF)
        dw_ref[0] = lax.dot_general(pooled, dms, _DIMS["tn"], preferred_element_type=F32)
        dpooled = lax.dot_general(dms, w, _DIMS["nt"], preferred_element_type=F32)
        row = lax.broadcasted_iota(jnp.int32, (T, 1), 0)
        count = jnp.minimum(row + 1, 2 << g).astype(F32)
        z = dpooled / count
        l2 = z + _shift_up(z, 1, row, T)
        l4 = l2 + _shift_up(l2, 2, row, T)
        l8 = l4 + _shift_up(l4, 4, row, T)
        l16 = l8 + _shift_up(l8, 8, row, T)
        dx_ref[...] = (_by_group(g, [l2, l4, l8, l16]) - dpooled).astype(BF)

    col = pl.BlockSpec((T, POOL_GROUP), lambda g: (0, g))
    wspec = pl.BlockSpec((1, POOL_GROUP, POOL_GROUP), lambda g: (g, 0, 0))
    sspec = pl.BlockSpec((1, POOL_GROUP), lambda g: (0, g))
    return pl.pallas_call(
        body, name=name, grid=(N_POOL_GROUPS,), in_specs=[col, col, wspec, sspec, ANY], out_specs=[col, wspec, sspec],
        out_shape=[jax.ShapeDtypeStruct(into.shape, BF),
                   jax.ShapeDtypeStruct((N_POOL_GROUPS, POOL_GROUP, POOL_GROUP), F32),
                   jax.ShapeDtypeStruct((1, POOL_WIDTH), F32)],
        input_output_aliases={4: 0}, compiler_params=_params(("parallel",)),
    )(dmixed, pooled, pool_w, pool_scale, into)


ATTN_SCALE = HEAD_DIM ** -0.5
MASKED = float(jnp.finfo(jnp.float32).min)
KV_COL_BLOCK_V = COL_V // LANES
GROUP_WIDTH = GQA_GROUP * HEAD_DIM


def _dup_head(v, j):
    half = lax.broadcasted_iota(jnp.int32, (1, LANES), 1) // HEAD_DIM
    return jnp.where(half == j, v, pltpu.roll(v, HEAD_DIM, axis=1))


def _stack_heads(v, low):
    pieces = []
    for p in range(GROUP_WIDTH // LANES):
        vp = v[:, LANES * p: LANES * (p + 1)]
        pieces.append(jnp.where(low, vp, jnp.zeros_like(vp)))
        pieces.append(jnp.where(low, jnp.zeros_like(vp), vp))
    return jnp.concatenate(pieces, axis=0)


def _unstack_transposed(t, low):
    pairs = []
    for p in range(GROUP_WIDTH // LANES):
        even = t[:, BLOCK * (2 * p): BLOCK * (2 * p + 1)].T
        odd = t[:, BLOCK * (2 * p + 1): BLOCK * (2 * p + 2)].T
        pairs.append(jnp.where(low, even, odd))
    return pairs


STACKED = GQA_GROUP * BLOCK


def _band_bias():
    key = lax.broadcasted_iota(jnp.int32, (2, 2 * BLOCK, STACKED), 1)
    qry = lax.broadcasted_iota(jnp.int32, (2, 2 * BLOCK, STACKED), 2) % BLOCK
    first = lax.broadcasted_iota(jnp.int32, (2, 2 * BLOCK, STACKED), 0) == 0
    valid = (key > qry) & (key <= qry + BLOCK) & (jnp.logical_not(first) | (key >= BLOCK))
    return jnp.where(valid, 0.0, MASKED).astype(F32)


def _softmax_keys_on_sublanes(k2, q, bias, sink_ref, j):
    head_of_lane = lax.broadcasted_iota(jnp.int32, (1, STACKED), 1) // BLOCK
    sink = jnp.zeros((1, STACKED), F32)
    for h in range(GQA_GROUP):
        sink = jnp.where(head_of_lane == h, sink_ref[j * GQA_GROUP + h], sink)
    s = lax.dot_general(k2, q, _DIMS["nt"], preferred_element_type=F32) + bias
    m = jnp.maximum(jnp.max(s, axis=0, keepdims=True), sink)
    e = jnp.exp(s - m)
    e_sink = jnp.exp(sink - m)
    inv = 1.0 / (jnp.sum(e, axis=0, keepdims=True) + e_sink)
    return e * inv, e_sink * inv


def _attn_fwd(name, proj, qg, kg, sinks, bias, comm):
    T = proj.shape[0]
    nb = T // BLOCK
    plumb = _CommPlumbing(comm)

    def body(sink_ref, bias_ref, ones_ref, qg_ref, kg_ref, q0_ref, q1_ref, kp_ref, kc_ref, vp_ref, vc_ref, *rest):
        c_in, (o_ref, q_ref, kn_ref) = rest[:plumb.n_in], rest[plumb.n_in: plumb.n_in + 3]
        c_out, c_scr = rest[plumb.n_in + 3: plumb.n_in + 3 + plumb.n_out], rest[plumb.n_in + 3 + plumb.n_out:]
        m = pl.program_id(0)
        plumb.handshake(m == 0)
        plumb.run(m, nb // 2, True, c_in, c_out, c_scr)
        low = lax.broadcasted_iota(jnp.int32, (1, LANES), 1) < HEAD_DIM

        def head_norm(raw, gain):
            xv = raw.astype(F32)
            return (xv * lax.rsqrt(_head_mean(xv * xv, ones_ref[...]) + RMS_EPS) * gain).astype(BF)

        for half, raw_ref in enumerate((q0_ref, q1_ref)):
            for s in range(GROUP_WIDTH // LANES):
                q_ref[:, GROUP_WIDTH * half + LANES * s: GROUP_WIDTH * half + LANES * (s + 1)] = head_norm(
                    raw_ref[:, LANES * s: LANES * (s + 1)], qg_ref[...])
        k_pair, k_prev = head_norm(kc_ref[...], kg_ref[...]), head_norm(kp_ref[...], kg_ref[...])
        kn_ref[...] = k_pair
        v_pair = vc_ref[...]
        for b in range(2):
            rows = slice(BLOCK * b, BLOCK * (b + 1))
            kk = k_pair if b else jnp.concatenate([k_prev, k_pair[0:BLOCK]], axis=0)
            vv = v_pair if b else jnp.concatenate([vp_ref[...], v_pair[0:BLOCK]], axis=0)
            bias = bias_ref[1] if b else bias_ref[jnp.minimum(m, 1)]
            for j in range(2):
                q = _stack_heads(q_ref[rows, GROUP_WIDTH * j: GROUP_WIDTH * (j + 1)], low)
                p, _ = _softmax_keys_on_sublanes(_dup_head(kk, j), q, bias, sink_ref, j)
                o_t = lax.dot_general(_dup_head(vv, j), p.astype(BF), _DIMS["tn"], preferred_element_type=F32)
                for pair, o in enumerate(_unstack_transposed(o_t, low)):
                    lanes = slice(GROUP_WIDTH * j + LANES * pair, GROUP_WIDTH * j + LANES * (pair + 1))
                    o_ref[rows, lanes] = o.astype(BF)
        plumb.run(m, nb // 2, False, c_in, c_out, c_scr)

    wide = pl.BlockSpec((2 * BLOCK, ATTN_WIDTH), lambda m: (m, 0))
    before = lambda m: jnp.maximum(2 * m - 1, 0)
    gain = pl.BlockSpec((1, LANES), lambda m: (0, 0))
    q_block, k_block = COL_Q // GROUP_WIDTH, COL_K // LANES
    res = pl.pallas_call(
        body, name=name, grid=(nb // 2,),
        in_specs=[pl.BlockSpec(memory_space=pltpu.SMEM),
                  pl.BlockSpec((2, 2 * BLOCK, STACKED), lambda m: (0, 0, 0)),
                  pl.BlockSpec((LANES, LANES), lambda m: (0, 0)), gain, gain,
                  pl.BlockSpec((2 * BLOCK, GROUP_WIDTH), lambda m: (m, q_block)),
                  pl.BlockSpec((2 * BLOCK, GROUP_WIDTH), lambda m: (m, q_block + 1)),
                  pl.BlockSpec((BLOCK, LANES), lambda m: (before(m), k_block)),
                  pl.BlockSpec((2 * BLOCK, LANES), lambda m: (m, k_block)),
                  pl.BlockSpec((BLOCK, LANES), lambda m: (before(m), KV_COL_BLOCK_V)),
                  pl.BlockSpec((2 * BLOCK, LANES), lambda m: (m, KV_COL_BLOCK_V))] + [ANY] * plumb.n_in,
        out_specs=[wide, wide, pl.BlockSpec((2 * BLOCK, LANES), lambda m: (m, 0))] + [ANY] * plumb.n_out,
        out_shape=[jax.ShapeDtypeStruct((T, ATTN_WIDTH), BF), jax.ShapeDtypeStruct((T, ATTN_WIDTH), BF),
                   jax.ShapeDtypeStruct((T, KV_WIDTH), BF)] + plumb.out_shapes,
        scratch_shapes=plumb.scratch, compiler_params=_params(("arbitrary",), plumb.collective_id()),
    )(sinks, bias, _half_sum_matrix(), qg, kg, proj, proj, proj, proj, proj, proj, *plumb.args)
    return list(res[:3]), plumb.split_outputs(res[3:])


def _attn_bwd(name, dout, qn, kn, proj, sinks, bias, comm):
    T = qn.shape[0]
    nb = T // BLOCK
    plumb = _CommPlumbing(comm)

    def body(sink_ref, bias_ref, do_ref, q_ref, kp_ref, kc_ref, vp_ref, vc_ref, *rest):
        c_in = rest[:plumb.n_in]
        dq_ref, k_own, k_before, v_own, v_before, dsink_ref = rest[plumb.n_in: plumb.n_in + 6]
        c_out, c_scr = rest[plumb.n_in + 6: plumb.n_in + 6 + plumb.n_out], rest[plumb.n_in + 6 + plumb.n_out:]
        m = pl.program_id(0)
        plumb.handshake(m == 0)
        plumb.run(m, nb // 2, True, c_in, c_out, c_scr)
        lane = lax.broadcasted_iota(jnp.int32, (1, LANES), 1)
        low = lane < HEAD_DIM

        @pl.when(m == 0)
        def _():
            dsink_ref[...] = jnp.zeros_like(dsink_ref)

        k_pair, v_pair = kc_ref[...], vc_ref[...]
        dsink = jnp.zeros((1, LANES), F32)
        for b in range(2):
            rows = slice(BLOCK * b, BLOCK * (b + 1))
            kk = k_pair if b else jnp.concatenate([kp_ref[...], k_pair[0:BLOCK]], axis=0)
            vv = v_pair if b else jnp.concatenate([vp_ref[...], v_pair[0:BLOCK]], axis=0)
            bias = bias_ref[1] if b else bias_ref[jnp.minimum(m, 1)]
            dk_tot = jnp.zeros((2 * BLOCK, LANES), F32)
            dv_tot = jnp.zeros((2 * BLOCK, LANES), F32)
            for j in range(2):
                k2 = _dup_head(kk, j)
                v2 = _dup_head(vv, j)
                q = _stack_heads(q_ref[rows, GROUP_WIDTH * j: GROUP_WIDTH * (j + 1)], low)
                do = _stack_heads(do_ref[rows, GROUP_WIDTH * j: GROUP_WIDTH * (j + 1)], low)
                p, psink = _softmax_keys_on_sublanes(k2, q, bias, sink_ref, j)
                dp =lax.dot_general(v2, do, _DIMS["nt"], preferred_element_type=F32)
                delta = jnp.sum(p * dp, axis=0, keepdims=True)
                ds = (p * (dp - delta)).astype(BF)
                dk2 = jnp.dot(ds, q, preferred_element_type=F32)
                dv2 = jnp.dot(p.astype(BF), do, preferred_element_type=F32)
                dq_t = lax.dot_general(k2, ds, _DIMS["tn"], preferred_element_type=F32)
                for pair, dq in enumerate(_unstack_transposed(dq_t, low)):
                    lanes = slice(GROUP_WIDTH * j + LANES * pair, GROUP_WIDTH * j + LANES * (pair + 1))
                    dq_ref[rows, lanes] = dq.astype(BF)
                mine = low if j == 0 else jnp.logical_not(low)
                dk_tot = dk_tot + jnp.where(mine, dk2 + pltpu.roll(dk2, HEAD_DIM, axis=1), 0.0)
                dv_tot = dv_tot + jnp.where(mine, dv2 + pltpu.roll(dv2, HEAD_DIM, axis=1), 0.0)
                sink_term = psink * delta
                for h in range(GQA_GROUP):
                    val = -jnp.sum(sink_term[:, BLOCK * h: BLOCK * (h + 1)], axis=1, keepdims=True)
                    dsink = dsink + jnp.where(lane == j * GQA_GROUP + h, val, 0.0)
            k_before[rows, :], k_own[rows, :] = dk_tot[0:BLOCK], dk_tot[BLOCK:]
            v_before[rows, :], v_own[rows, :] = dv_tot[0:BLOCK], dv_tot[BLOCK:]
        dsink_ref[0:1, :] += dsink
        plumb.run(m, nb // 2, False, c_in, c_out, c_scr)

    wide = pl.BlockSpec((2 * BLOCK, ATTN_WIDTH), lambda m: (m, 0))
    pair = pl.BlockSpec((2 * BLOCK, LANES), lambda m: (m, 0))
    before = lambda m: jnp.maximum(2 * m - 1, 0)
    res = pl.pallas_call(
        body, name=name, grid=(nb // 2,),
        in_specs=[pl.BlockSpec(memory_space=pltpu.SMEM),
                  pl.BlockSpec((2, 2 * BLOCK, STACKED), lambda m: (0, 0, 0)), wide, wide,
                  pl.BlockSpec((BLOCK, LANES), lambda m: (before(m), 0)), pair,
                  pl.BlockSpec((BLOCK, LANES), lambda m: (before(m), KV_COL_BLOCK_V)),
                  pl.BlockSpec((2 * BLOCK, LANES), lambda m: (m, KV_COL_BLOCK_V))] + [ANY] * plumb.n_in,
        out_specs=[wide, pair, pair, pair, pair, pl.BlockSpec((8, LANES), lambda m: (0, 0))] + [ANY] * plumb.n_out,
        out_shape=[jax.ShapeDtypeStruct((T, ATTN_WIDTH), BF)] + [jax.ShapeDtypeStruct((T, KV_WIDTH), F32)] * 4
        + [jax.ShapeDtypeStruct((8, LANES), F32)] + plumb.out_shapes,
        scratch_shapes=plumb.scratch, compiler_params=_params(("arbitrary",), plumb.collective_id()),
    )(sinks, bias, dout, qn, kn, kn, proj, proj, *plumb.args)
    return list(res[:6]), plumb.split_outputs(res[6:])


def _swiglu_fwd_epilogue(accs, ex):
    g, u = accs
    return [g, u, g * jax.nn.sigmoid(g) * u], []


def _swiglu_bwd_epilogue(accs, ex):
    (da,) = accs
    g, u = ex[0].astype(F32), ex[1].astype(F32)
    s = jax.nn.sigmoid(g)
    gs = g * s
    return [da * u * (s + gs - gs * s), da * gs], []


def _residual_norm_epilogue(scale):
    def epilogue(accs, ex):
        res, gain = ex
        h = res + scale * accs[0]
        r = lax.rsqrt(jnp.mean(h * h, axis=-1, keepdims=True) + RMS_EPS)
        return [h, h * r * gain], []
    return epilogue


def _rms_bwd_epilogue(accs, ex):
    (dn,) = accs
    xv, g, dres = ex
    r = lax.rsqrt(jnp.mean(xv * xv, axis=-1, keepdims=True) + RMS_EPS)
    xhat = xv * r
    dxhat = dn * g
    dx = dres + r * (dxhat - xhat * jnp.mean(dxhat * xhat, axis=-1, keepdims=True))
    return [dx, dx], [dn * xhat]


def _loss_epilogue(accs, ex):
    xv, target = ex
    d = xv + 0.5 * accs[0] - target
    dy = d * (1.0 / D_MODEL)
    return [dy, dy], [d * d]


def _merge_fwd_epilogue(accs, ex):
    ba, bp = accs
    gp_pre, ga_pre, bias_p, bias_a = ex
    gp = jax.nn.sigmoid(gp_pre.astype(F32) + bias_p)
    ga = jax.nn.sigmoid(ga_pre.astype(F32) + bias_a)
    return [gp * bp + ga * ba, ba, bp], []


def _merge_bwd_epilogue(accs, ex):
    (dm,) = accs
    bp, ba, gp_pre, ga_pre, bias_p, bias_a = ex
    gp = jax.nn.sigmoid(gp_pre.astype(F32) + bias_p)
    ga = jax.nn.sigmoid(ga_pre.astype(F32) + bias_a)
    dbp, dba = dm * gp, dm * ga
    dgp = dbp * bp.astype(F32) * (1.0 - gp)
    dga = dba * ba.astype(F32) * (1.0 - ga)
    return [dbp, dba, dgp, dga], [dgp, dga]


def _prep(name, ws, transposes):
    n = len(ws)

    def body(*refs):
        for w_ref, o_ref, tr in zip(refs[:n], refs[n:], transposes):
            v = w_ref[...]
            o_ref[...] = (v.T if tr else v).astype(BF)

    shapes = [jax.ShapeDtypeStruct(w.shape[::-1] if tr else w.shape, BF) for w, tr in zip(ws, transposes)]
    return pl.pallas_call(body, name=name, out_shape=shapes, compiler_params=_params())(*ws)


def _adam_math(w, g, m, v):
    m = ADAM_B1 * m + (1.0 - ADAM_B1) * g
    v = ADAM_B2 * v + (1.0 - ADAM_B2) * jnp.square(g)
    m_hat = m / (1.0 - ADAM_B1 ** ADAM_STEP)
    v_hat = v / (1.0 - ADAM_B2 ** ADAM_STEP)
    delta = -ADAM_LR * (m_hat / (jnp.sqrt(v_hat) + ADAM_EPS) + ADAM_WD * w)
    return delta, m, v


def _adamw_sharded(name, items, transpose=False, comm=None):
    n = len(items)
    plumb = _CommPlumbing(comm)

    def body(*refs):
        ins, c_in = refs[:4 * n], refs[4 * n: 4 * n + plumb.n_in]
        o0 = 4 * n + plumb.n_in
        outs, c_out, c_scr = refs[o0: o0 + 4 * n], refs[o0 + 4 * n: o0 + 4 * n + plumb.n_out], refs[o0 + 4 * n + plumb.n_out:]
        if comm:
            plumb.handshake(pl.program_id(0) == 0)
            plumb.run(pl.program_id(0), 4, True, c_in, c_out, c_scr)
        for k in range(n):
            s_ref, w_ref, m_ref, v_ref = ins[4 * k: 4 * k + 4]
            g = s_ref[0].astype(F32)
            for i in range(1, 4):
                g = g + s_ref[i].astype(F32)
            if transpose:
                g = g.T
            delta, mn, vn = _adam_math(w_ref[...], g, m_ref[...], v_ref[...])
            for o_ref, val in zip(outs[4 * k: 4 * k + 4], (g, delta, mn, vn)):
                o_ref[...] = val
        if comm:
            plumb.run(pl.program_id(0), 4, False, c_in, c_out, c_scr)

    flat = [a for item in items for a in item]
    out_shape = [jax.ShapeDtypeStruct(item[1].shape, F32) for item in items for _ in range(4)]
    _, r, C = items[0][0].shape
    rows = r // 4
    if transpose or rows % 8:
        assert not comm, name
        res = pl.pallas_call(body, name=name, out_shape=out_shape, compiler_params=_params())(*flat)
    else:
        tile = pl.BlockSpec((rows, C), lambda i: (i, 0))
        res = pl.pallas_call(
            body, name=name, grid=(4,),
            in_specs=[pl.BlockSpec((4, rows, C), lambda i: (0, i, 0)), tile, tile, tile] * n + [ANY] * plumb.n_in,
            out_specs=[tile] * (4 * n) + [ANY] * plumb.n_out, out_shape=out_shape + plumb.out_shapes,
            scratch_shapes=plumb.scratch,
            compiler_params=_params(("arbitrary",) if comm else ("parallel",), plumb.collective_id()),
        )(*flat, *plumb.args)
    own = [tuple(res[4 * k: 4 * k + 4]) for k in range(n)]
    return (own, plumb.split_outputs(res[4 * n:])) if comm else own


SMALL_LAYOUT = (("ffn1_norm", 0, (8, LANES)), ("mix_norm", 8, (8, LANES)), ("ffn2_norm", 16, (8, LANES)),
                ("gate_bias", 24, (16, LANES)), ("pool_scale", 40, (4, LANES)), ("q_norm", 48, (1, HEAD_DIM)),
                ("k_norm", 56, (1, HEAD_DIM)), ("sinks", 64, (1, N_HEADS)))
LOSS_ROW = 72
SMALL_ROWS = 80


def _adamw_small(name, g_vec, g_pool_w, params):
    n = len(SMALL_LAYOUT) + 1

    def body(vec_ref, pw_ref, *refs):
        ins, outs = refs[:3 * n], refs[3 * n:]
        vec = vec_ref[0]
        pw = pw_ref[0]
        for i in range(1, N_DEV):
            vec = vec + vec_ref[i]
            pw = pw + pw_ref[i]
        grads = [vec[r0:r0 + shape[0], 0:shape[1]] for _, r0, shape in SMALL_LAYOUT] + [pw]
        for p, g in enumerate(grads):
            w_ref, m_ref, v_ref = ins[3 * p: 3 * p + 3]
            delta, mn, vn = _adam_math(w_ref[...], g, m_ref[...], v_ref[...])
            for o_ref, val in zip(outs[4 * p: 4 * p + 4], (g, delta, mn, vn)):
                o_ref[...] = val
        outs[4 * n][...] = vec[LOSS_ROW:LOSS_ROW + 1, :]

    flat = [a for wmv in params for a in wmv]
    out_shape = [jax.ShapeDtypeStruct(wmv[0].shape, F32) for wmv in params for _ in range(4)]
    out_shape.append(jax.ShapeDtypeStruct((1, LANES), F32))
    res = pl.pallas_call(body, name=name, out_shape=out_shape, compiler_params=_params())(g_vec, g_pool_w, *flat)
    return [tuple(res[4 * p: 4 * p + 4]) for p in range(n)], res[4 * n]


def _place():
    x, y, c = lax.axis_index("x"), lax.axis_index("y"), lax.axis_index("c")
    other_chips = [(1 - x, y), (x, 1 - y), (1 - x, 1 - y)]
    return x, y, c, other_chips


def _rows(ref, r, place, natural=False):
    px, py, pc = place
    b = 4 * px + 2 * py + pc if natural else 4 * pc + 2 * px + py
    return ref.at[pl.ds(pl.multiple_of(b * r, 8), r), :]


def _gather_task(shards, natural=(), forward_at=0.75):
    n = len(shards)
    rs = [s.shape[0] for s in shards]
    rows_of = lambda ref, k, place: _rows(ref, rs[k], place, k in natural)

    def copy(scr, outs, k, slot, block, to, src=None):
        rows = rows_of(outs[k], k, block)
        return pltpu.make_async_remote_copy(
            src_ref=rows if src is None else src, dst_ref=rows, send_sem=scr[0].at[7 * k + slot],
            recv_sem=scr[1].at[7 * k + slot], device_id=to, device_id_type=MESH)

    def first_sends(ins, outs, scr):
        x, y, c, chips = _place()
        me = (x, y, c)
        cps = [copy(scr, outs, k, 1 + j, me, (*chip, c), src=ins[k]) for j, chip in enumerate(chips) for k in range(n)]
        return cps + [copy(scr, outs, k, 0, me, (x, y, 1 - c), src=ins[k]) for k in range(n)]

    def passed_on(outs, scr):
        x, y, c, chips = _place()
        return [copy(scr, outs, k, 4 + j, (*chip, c), (x, y, 1 - c)) for j, chip in enumerate(chips) for k in range(n)]

    def local(ins, outs, scr):
        x, y, c, _ = _place()
        return [pltpu.make_async_copy(ins[k], rows_of(outs[k], k, (x, y, c)), scr[2].at[k]) for k in range(n)]

    def start(ins, outs, scr):
        for cp in local(ins, outs, scr) + first_sends(ins, outs, scr):
            cp.start()

    def forward(ins, outs, scr):
        x, y, c, chips = _place()
        for j, chip in enumerate(chips):
            for k in range(n):
                copy(scr, outs, k, 1 + j, (*chip, c), (x, y, c)).wait_recv()
                copy(scr, outs, k, 4 + j, (*chip, c), (x, y, 1 - c)).start()

    def finish(ins, outs, scr):
        x, y, c, chips = _place()
        for k in range(n):
            copy(scr, outs, k, 0, (x, y, 1 - c), (x, y, c)).wait_recv()
        for j, chip in enumerate(chips):
            for k in range(n):
                copy(scr, outs, k, 4 + j, (*chip, 1 - c), (x, y, c)).wait_recv()
        for cp in first_sends(ins, outs, scr) + passed_on(outs, scr):
            cp.wait_send()
        for cp in local(ins, outs, scr):
            cp.wait()

    out_shapes = [jax.ShapeDtypeStruct((N_DEV * s.shape[0], s.shape[1]), s.dtype) for s in shards]
    scratch = [pltpu.SemaphoreType.DMA((7 * n,)), pltpu.SemaphoreType.DMA((7 * n,)), pltpu.SemaphoreType.DMA((n,))]
    return _Task(shards, out_shapes, scratch, [(0, start), (forward_at, forward), (1.0, finish)], ("sibling", "chips"))


def _direct_gather_task(shards):
    n = len(shards)
    rs = [s.shape[0] for s in shards]

    def peers():
        x, y, c, _ = _place()
        flip = lambda v, bit: 1 - v if bit else v
        return (x, y, c), [(flip(x, (s >> 2) & 1), flip(y, (s >> 1) & 1), flip(c, s & 1)) for s in range(1, N_DEV)]

    def copies(ins, outs, scr):
        me, others = peers()
        local = [pltpu.make_async_copy(ins[k], _rows(outs[k], rs[k], me), scr[2].at[k]) for k in range(n)]
        sems = lambda k, s: dict(send_sem=scr[0].at[7 * k + s], recv_sem=scr[1].at[7 * k + s], device_id_type=MESH)
        sends = [pltpu.make_async_remote_copy(src_ref=ins[k], dst_ref=_rows(outs[k], rs[k], me), device_id=to, **sems(k, s))
                 for s, to in enumerate(others) for k in range(n)]
        recvs = [pltpu.make_async_remote_copy(src_ref=_rows(outs[k], rs[k], frm), dst_ref=_rows(outs[k], rs[k], frm),
                                              device_id=me, **sems(k, s))
                 for s, frm in enumerate(others) for k in range(n)]
        return local, sends, recvs

    def start(ins, outs, scr):
        local, sends, _ = copies(ins, outs, scr)
        for cp in local + sends:
            cp.start()

    def finish(ins, outs, scr):
        local, sends, recvs = copies(ins, outs, scr)
        for cp in recvs:
            cp.wait_recv()
        for cp in sends:
            cp.wait_send()
        for cp in local:
            cp.wait()

    out_shapes = [jax.ShapeDtypeStruct((N_DEV * s.shape[0], s.shape[1]), s.dtype) for s in shards]
    scratch = [pltpu.SemaphoreType.DMA((7 * n,)), pltpu.SemaphoreType.DMA((7 * n,)), pltpu.SemaphoreType.DMA((n,))]
    return _Task(shards, out_shapes, scratch, [(0, start), (1.0, finish)], ("all",))


def _chip_task(sums):
    n = len(sums)
    rs = [s.shape[0] // 4 for s in sums]

    def block(ref, k, chip_index):
        return ref.at[pl.ds(pl.multiple_of(chip_index * rs[k], 8), rs[k]), :]

    def copies(ins, outs, scr):
        send_sems, recv_sems, local_sems = scr
        x, y, c, chips = _place()
        here = 2 * x + y
        local = [pltpu.make_async_copy(block(ins[k], k, here), outs[k].at[here], local_sems.at[k]) for k in range(n)]
        remote = []
        for j, (px, py) in enumerate(chips):
            remote += [pltpu.make_async_remote_copy(
                src_ref=block(ins[k], k, 2 * px + py), dst_ref=outs[k].at[here],
                send_sem=send_sems.at[3 * k + j], recv_sem=recv_sems.at[3 * k + j],
                device_id=(px, py, c), device_id_type=MESH) for k in range(n)]
        return local, remote

    def start(ins, outs, scr):
        local, remote = copies(ins, outs, scr)
        for cp in local + remote:
            cp.start()

    def finish(ins, outs, scr):
        local, remote = copies(ins, outs, scr)
        for cp in remote:
            cp.wait()
        for cp in local:
            cp.wait()

    out_shapes = [jax.ShapeDtypeStruct((4, r, s.shape[1]), s.dtype) for r, s in zip(rs, sums)]
    scratch = [pltpu.SemaphoreType.DMA((3 * n,)), pltpu.SemaphoreType.DMA((3 * n,)), pltpu.SemaphoreType.DMA((n,))]
    return _Task(sums, out_shapes, scratch, [(0, start), (1.0, finish)], ("chips",))


def _dw_pair(name, a, b, scale, comm=None, blocks=1):
    T, M = a.shape
    N = b.shape[1]
    half = M // 2
    wide = half // blocks
    tk = min(2048, T)
    nK = T // tk
    plumb = _CommPlumbing(comm)

    def body(core_ref, *rest):
        a_refs, b_ref, rest = rest[:blocks], rest[blocks], rest[blocks + 1:]
        c_in = rest[:plumb.n_in]
        o_ref = rest[plumb.n_in]
        c_out = rest[plumb.n_in + 1: plumb.n_in + 1 + plumb.n_out]
        acc, stage, land, send_sem, recv_sem = rest[plumb.n_in + 1 + plumb.n_out: plumb.n_in + 6 + plumb.n_out]
        c_scr = rest[plumb.n_in + 6 + plumb.n_out:]
        i, k = pl.program_id(0), pl.program_id(1)
        x, y, c, _ = _place()
        push = pltpu.make_async_remote_copy(src_ref=stage, dst_ref=land, send_sem=send_sem, recv_sem=recv_sem,
                                            device_id=(x, y, 1 - c), device_id_type=MESH)
        plumb.handshake((i == 0) & (k == 0), own=("sibling",))
        if comm:
            plumb.run(i * nK + k, 2 * nK, True, c_in, c_out, c_scr)

        av = a_refs[0][...] if blocks == 1 else jnp.concatenate([r[...] for r in a_refs], axis=1)
        p = lax.dot_general(av, b_ref[...], _DIMS["tn"], preferred_element_type=F32)

        @pl.when(k == 0)
        def _():
            acc[...] = p

        @pl.when(k > 0)
        def _():
            acc[...] += p

        @pl.when((i == 0) & (k == nK - 1))
        def _():
            stage[...] = (scale * acc[...]).astype(BF)
            push.start()

        @pl.when((i == 1) & (k == nK - 1))
        def _():
            push.wait_recv()
            o_ref[...] = (scale * acc[...] + land[...].astype(F32)).astype(BF)
            push.wait_send()

        if comm:
            plumb.run(i * nK + k, 2 * nK, False, c_in, c_out, c_scr)

    grid_spec = pltpu.PrefetchScalarGridSpec(
        num_scalar_prefetch=1, grid=(2, nK),
        in_specs=[pl.BlockSpec((tk, wide), functools.partial(
            lambda i, k, core, j: (k, (2 * j if blocks > 1 else 0) + jnp.where(i == 0, 1 - core[0], core[0])), j=j))
            for j in range(blocks)] + [pl.BlockSpec((tk, N), lambda i, k, core: (k, 0))] + [ANY] * plumb.n_in,
        out_specs=[pl.BlockSpec((half, N), lambda i, k, core: (0, 0))] + [ANY] * plumb.n_out,
        scratch_shapes=[pltpu.VMEM((half, N), F32), pltpu.VMEM((half, N), BF), pltpu.VMEM((half, N), BF),
                        pltpu.SemaphoreType.DMA, pltpu.SemaphoreType.DMA] + plumb.scratch)
    core = lax.axis_index("c").astype(jnp.int32).reshape(1)
    res = pl.pallas_call(
        body, name=name, grid_spec=grid_spec,
        out_shape=[jax.ShapeDtypeStruct((half, N), BF)] + plumb.out_shapes,
        compiler_params=_params(("arbitrary", "arbitrary"), plumb.collective_id(own=("sibling",))),
    )(core, *([a] * blocks), b, *plumb.args)
    return (res[0], plumb.split_outputs(res[1:])) if comm else res[0]


def _pair_task(parts):
    n = len(parts)

    def copies(ins, outs, scr):
        x, y, c, _ = _place()
        return [pltpu.make_async_remote_copy(
            src_ref=ins[k].at[:, pl.ds(1 - c, 1)], dst_ref=outs[k], send_sem=scr[0].at[k], recv_sem=scr[1].at[k],
            device_id=(x, y, 1 - c), device_id_type=MESH) for k in range(n)]

    def start(ins, outs, scr):
        for cp in copies(ins, outs, scr):
            cp.start()

    def finish(ins, outs, scr):
        for cp in copies(ins, outs, scr):
            cp.wait()

    out_shapes = [jax.ShapeDtypeStruct((4, 1) + p.shape[2:], p.dtype) for p in parts]
    scratch = [pltpu.SemaphoreType.DMA((n,)), pltpu.SemaphoreType.DMA((n,))]
    return _Task(parts, out_shapes, scratch, [(0, start), (1.0, finish)], ("sibling",))


def _pair_sum(name, part, got, core):
    _, _, r, C = part.shape

    def body(core_ref, p_ref, g_ref, o_ref):
        o_ref[0] = (p_ref[0, 0].astype(F32) + g_ref[0, 0].astype(F32)).astype(o_ref.dtype)

    return pl.pallas_call(
        body, name=name,
        grid_spec=pltpu.PrefetchScalarGridSpec(
            num_scalar_prefetch=1, grid=(4,),
            in_specs=[pl.BlockSpec((1, 1, r, C), lambda i, core_ref: (i, core_ref[0], 0, 0)),
                      pl.BlockSpec((1, 1, r, C), lambda i, core_ref: (i, 0, 0, 0))],
            out_specs=pl.BlockSpec((1, r, C), lambda i, core_ref: (i, 0, 0))),
        out_shape=jax.ShapeDtypeStruct((4, r, C), part.dtype), compiler_params=_params(("parallel",)),
    )(core, part, got)


def _ffn_bwd(tag, dy, dyb, x, gain, wgT, wuT, wd, saved, earlier=None):
    n, g, u, a = saved
    half = lambda accs, ex: _swiglu_bwd_epilogue([0.5 * accs[0]], ex)
    act_args = dict(tm=1024, tn=1408, tk=D_MODEL, epilogue=half, extras=[(g, "tile", 0), (u, "tile", 0)], cols_outer=True)
    if earlier is None:
        sum_d = _dw_pair(tag + "_dw_down", a, dyb, 0.5)
        (dg, du), ((slots_d,),) = _mm(tag + "_d_act", [(dyb, wd, "nt", 0)], [BF, BF], comm=[_chip_task([sum_d])], **act_args)
        slots_e = None
        sum_g = _dw_pair(tag + "_dw_gate", dg, n, 1.0)
    else:
        sum_d, ((got,),) = _dw_pair(tag + "_dw_down", a, dyb, 0.5, comm=[_pair_task([earlier])])
        core = lax.axis_index("c").astype(jnp.int32).reshape(1)
        sum_e = _pair_sum(tag + "_pair_sum_earlier", earlier, got, core)
        sum_e = sum_e.reshape(4 * sum_e.shape[1], sum_e.shape[2])
        (dg, du), ((slots_e,),) = _mm(tag + "_d_act", [(dyb, wd, "nt", 0)], [BF, BF], comm=[_chip_task([sum_e])], **act_args)
        sum_g, ((slots_d,),) = _dw_pair(tag + "_dw_gate", dg, n, 1.0, comm=[_chip_task([sum_d])])
    norm_args = dict(tm=512, tn=D_MODEL, tk=D_FF, epilogue=_rms_bwd_epilogue, n_colsum=1,
                     extras=[(x, "tile", 0), (gain, "row", 0), (dy, "tile", 0)])
    norm_terms = [(dg, wgT, "nn", 0), (du, wuT, "nn", 0)]
    if earlier is None:
        up = _dw_pair(tag + "_dw_up", du, n, 1.0)
        (dx, dxb, dgain), ((slots_g,),) = _mm(tag + "_d_norm", norm_terms, [F32, BF], comm=[_chip_task([sum_g])], **norm_args)
    else:
        sum_u, ((slots_g,),) = _dw_pair(tag + "_dw_up", du, n, 1.0, comm=[_chip_task([sum_g])])
        dx, dxb, dgain = _mm(tag + "_d_norm", norm_terms, [F32, BF], **norm_args)
        up = sum_u
    return dx, dxb, dgain, slots_e, slots_g, up, slots_d


def _tile_gain(g):
    return jnp.concatenate([g, g]).reshape(1, LANES)


def _fold_heads(partials):
    return jnp.sum(partials.reshape(-1, HEAD_DIM), axis=0)


def _pack_small_grads(grads, loss_local):
    pieces, row = [], 0
    for name, r0, _ in SMALL_LAYOUT + (("loss", LOSS_ROW, None),):
        v = (loss_local if name == "loss" else grads[name]).reshape(-1)
        rows = -(-v.size // LANES)
        block = jnp.pad(v, (0, rows * LANES - v.size)).reshape(rows, LANES)
        pieces += [jnp.zeros((r0 - row, LANES), F32)] * (r0 > row) + [block]
        row = r0 + rows
    pieces.append(jnp.zeros((SMALL_ROWS - row, LANES), F32))
    return jnp.concatenate(pieces, axis=0)


def kernel(x, ffn1_norm, ffn1_w_gate, ffn1_w_up, ffn1_w_down, mix_norm, w_in, pool_w, pool_scale, w_pool_out, q_norm, k_norm, sinks, w_attn_out, gate_bias, w_out, ffn2_norm, ffn2_w_gate, ffn2_w_up, ffn2_w_down, loss_target, m_ffn1_norm, m_ffn1_w_gate, m_ffn1_w_up, m_ffn1_w_down, m_mix_norm, m_w_in, m_pool_w, m_pool_scale, m_w_pool_out, m_q_norm, m_k_norm, m_sinks, m_w_attn_out, m_gate_bias, m_w_out, m_ffn2_norm, m_ffn2_w_gate, m_ffn2_w_up, m_ffn2_w_down, v_ffn1_norm, v_ffn1_w_gate, v_ffn1_w_up, v_ffn1_w_down, v_mix_norm, v_w_in, v_pool_w, v_pool_scale, v_w_pool_out, v_q_norm, v_k_norm, v_sinks, v_w_attn_out, v_gate_bias, v_w_out, v_ffn2_norm, v_ffn2_w_gate, v_ffn2_w_up, v_ffn2_w_down):
    T = x.shape[1]
    x2 = x.reshape(T, D_MODEL)
    target = loss_target.reshape(T, D_MODEL)

    big = [
        ("ffn1_w_gate", ffn1_w_gate, m_ffn1_w_gate, v_ffn1_w_gate, True, False),
        ("ffn1_w_up", ffn1_w_up, m_ffn1_w_up, v_ffn1_w_up, True, False),
        ("ffn1_w_down", ffn1_w_down, m_ffn1_w_down, v_ffn1_w_down, False, False),
        ("w_in", w_in, m_w_in, v_w_in, True, False),
        ("w_pool_out", w_pool_out, m_w_pool_out, v_w_pool_out, False, True),
        ("w_attn_out", w_attn_out, m_w_attn_out, v_w_attn_out, False, False),
        ("w_out", w_out, m_w_out, v_w_out, False, False),
        ("ffn2_w_gate", ffn2_w_gate, m_ffn2_w_gate, v_ffn2_w_gate, True, False),
        ("ffn2_w_up", ffn2_w_up, m_ffn2_w_up, v_ffn2_w_up, True, False),
        ("ffn2_w_down", ffn2_w_down, m_ffn2_w_down, v_ffn2_w_down, False, False),
    ]
    view = lambda a, tv: a.T if tv else a
    views = [view(w, tv) for _, w, _, _, tv, _ in big]
    in_kernel_t = [tk_ for *_, tk_ in big]
    first_shards = _prep("prep_ffn1_gate_up", views[0:2], in_kernel_t[0:2])
    g1 = ffn1_norm.reshape(1, D_MODEL)
    g2 = mix_norm.reshape(1, D_MODEL)
    g3 = ffn2_norm.reshape(1, D_MODEL)
    bias_row = gate_bias.reshape(1, 2 * D_MODEL)
    qg, kg = _tile_gain(q_norm) * ATTN_SCALE, _tile_gain(k_norm)
    scale_row = pool_scale.reshape(1, POOL_WIDTH)
    band_bias = _band_bias()

    n1, later_shards, ((wg1T, wu1T),) = _rms_fwd(
        "ffn1_norm", x2, g1, [_gather_task(first_shards, forward_at=0.9)], views[2:], in_kernel_t[2:])
    shards = list(first_shards) + later_shards
    (gt1, up1, act1), ((wd1,), (w_inT,)) = _mm(
        "ffn1_gate_up", [(n1, wg1T, "nt", 0), (n1, wu1T, "nt", 1)], [BF, BF, BF], tm=1024, tn=1408, tk=D_MODEL,
        epilogue=_swiglu_fwd_epilogue, cols_outer=True,
        comm=[_gather_task(shards[2:3], forward_at=0.5), _gather_task(shards[3:4], natural=(0,), forward_at=0.9)])
    (h1, u), ((w_poT, w_ao, w_o),) = _mm(
        "ffn1_down", [(act1, wd1, "nn", 0)], [F32, BF], tm=512, tn=D_MODEL, tk=D_FF,
        epilogue=_residual_norm_epilogue(0.5), extras=[(x2, "tile", 0), (g2, "row", 0)],
        comm=[_gather_task(shards[4:7], natural=(0, 1, 2), forward_at=0.8)])
    saved1 = (n1, gt1, up1, act1)
    (proj,), ((wg2T,),) = _mm(
        "in_proj", [(u, w_inT, "nt", 0)], [BF], tm=1024, tn=1280, tk=D_MODEL, cols_outer=True,
        comm=[_gather_task(shards[7:8], forward_at=0.8)])
    pooled, mixed = _pool_fwd("pool_fwd", proj, pool_w, scale_row)
    (attn, qn, kn), ((wu2T,),) = _attn_fwd("attn_fwd", proj, qg, kg, sinks, band_bias,
                                           [_gather_task(shards[8:9], forward_at=0.8)])
    gate_tn = 256
    gate_extras = [(proj, "tile", COL_GP // gate_tn), (proj, "tile", COL_GA // gate_tn),
                   (bias_row, "row", 0), (bias_row, "row", D_MODEL // gate_tn)]
    merged, ba, bp = _mm("branch_out_merge", [(attn, w_ao, "nn", 0), (mixed, w_poT, "nt", 1)], [BF, BF, BF],
                         tm=2048, tn=gate_tn, tk=ATTN_WIDTH, epilogue=_merge_fwd_epilogue, extras=gate_extras)
    h2, n2 = _mm("mix_out", [(merged, w_o, "nn", 0)], [F32, BF], tm=1024, tn=D_MODEL, tk=D_MODEL,
                 epilogue=_residual_norm_epilogue(1.0), extras=[(h1, "tile", 0), (g3, "row", 0)])
    (gt2, up2, act2), ((wd2,),) = _mm(
        "ffn2_gate_up", [(n2, wg2T, "nt", 0), (n2, wu2T, "nt", 1)], [BF, BF, BF], tm=1024, tn=1408, tk=D_MODEL,
        epilogue=_swiglu_fwd_epilogue, cols_outer=True, comm=[_gather_task(shards[9:10], forward_at=0.8)])
    dy, dyb, sq = _mm("ffn2_down_loss", [(act2, wd2, "nn", 0)], [F32, BF], tm=512, tn=D_MODEL, tk=D_FF,
                      epilogue=_loss_epilogue, extras=[(h2, "tile", 0), (target, "tile", 0)], n_colsum=1)
    loss_local = 0.5 * jnp.sum(sq) / D_MODEL

    dh2, dh2b, dg3, _, slots_g2, sum_u2, slots_d2 = _ffn_bwd(
        "ffn2", dy, dyb, h2, g3, wg2T, wu2T, wd2, (n2, gt2, up2, act2))
    (dbp, dba, dproj, dga, cs_gp, cs_ga), ((slots_u2,),) = _mm(
        "mix_out_bwd", [(dh2b, w_o, "nt", 0)], [BF, BF, BF, BF], tm=2048, tn=gate_tn, tk=D_MODEL,
        epilogue=_merge_bwd_epilogue, extras=[(bp, "tile", 0), (ba, "tile", 0)] + gate_extras, n_colsum=2,
        out_placement={2: (IN_WIDTH, COL_GP)}, comm=[_chip_task([sum_u2])])
    sum_o = _dw_pair("dw_out", merged, dh2b, 1.0, blocks=4)
    (dmixed,) = _mm("pool_out_bwd", [(dbp, w_poT, "nn", 0)], [BF], tm=1024, tn=POOL_WIDTH, tk=D_MODEL)
    sum_po = _dw_pair("dw_pool_out", dbp, mixed, 1.0, blocks=4)
    (dattn,) = _mm("attn_out_bwd", [(dba, w_ao, "nt", 0)], [BF], tm=1024, tn=ATTN_WIDTH, tk=D_MODEL)
    sum_ao = _dw_pair("dw_attn_out", attn, dba, 1.0, blocks=4)
    (dqn, k_own, k_before, v_own, v_before, dsink_tile), ((slots_o, slots_po, slots_ao),) = _attn_bwd(
        "attn_bwd", dattn, qn, kn, proj, sinks, band_bias, [_chip_task([sum_o, sum_po, sum_ao])])
    next_block = lambda a: jnp.concatenate([a[BLOCK:], jnp.zeros((BLOCK, KV_WIDTH), F32)], axis=0)
    dkn = (k_own + next_block(k_before)).astype(BF)
    dv = (v_own + next_block(v_before)).astype(BF)
    dproj, dqg = _headnorm_bwd("q_norm_bwd", dqn, proj, COL_Q, ATTN_WIDTH, qg, dproj)
    dproj, dkg = _headnorm_bwd("k_norm_bwd", dkn, proj, COL_K, KV_WIDTH, kg, dproj)
    dproj, dpool_w, dpool_scale = _pool_bwd("pool_bwd", dmixed, pooled, pool_w, scale_row, dproj)
    for piece, col in ((dv, COL_V), (dga, COL_GA)):
        dproj = lax.dynamic_update_slice(dproj, piece, (0, col))
    (dh1, dh1b, dg2), ((g_pool_w,),) = _mm(
        "in_proj_bwd", [(dproj, w_inT, "nn", 0)], [F32, BF], tm=512, tn=D_MODEL, tk=IN_WIDTH, epilogue=_rms_bwd_epilogue,
        extras=[(h1, "tile", 0), (g2, "row", 0), (dh2, "tile", 0)], n_colsum=1,
        comm=[_gather_task([dpool_w.reshape(-1, LANES)])])
    (dw_inT,) = _mm("dw_in", [(dproj, u, "tn", 0)], [BF], tm=1920, tn=D_MODEL, tk=2048)
    dx, _, dg1, slots_in, slots_g1, sum_u1, slots_d1 = _ffn_bwd(
        "ffn1", dh1, dh1b, x2, g1, wg1T, wu1T, wd1, saved1, dw_inT.reshape(4, 2, IN_WIDTH // N_DEV, D_MODEL))

    slots = [slots_g1, None, slots_d1, slots_in, slots_po, slots_ao, slots_o, slots_g2, slots_u2, slots_d2]
    big_out = {}
    for label, group in (("ffn", (0, 2, 7, 8, 9)), ("w_in", (3,)), ("w_pool_out", (4,)), ("attn_out_and_out", (5, 6)),
                         ("ffn1_up", (1,))):
        items = [(slots[k], view(big[k][1], big[k][4]), view(big[k][2], big[k][4]), view(big[k][3], big[k][4]))
                 for k in group]
        if label == "ffn":
            results, ((slots[1],),) = _adamw_sharded("adamw_" + label, items, transpose=big[group[0]][5],
                                                     comm=[_chip_task([sum_u1])])
        else:
            results = _adamw_sharded("adamw_" + label, items, transpose=big[group[0]][5])
        for k, res in zip(group, results):
            big_out[big[k][0]] = tuple(view(r, big[k][4]) for r in res)

    small_grads = {
        "ffn1_norm": jnp.sum(dg1, axis=(0, 1)), "mix_norm": jnp.sum(dg2, axis=(0, 1)), "ffn2_norm": jnp.sum(dg3, axis=(0, 1)),
        "gate_bias": jnp.concatenate([jnp.sum(cs_gp, axis=(0, 1)), jnp.sum(cs_ga, axis=(0, 1))]),
        "pool_scale": dpool_scale, "q_norm": _fold_heads(dqg) * ATTN_SCALE, "k_norm": _fold_heads(dkg),
        "sinks": dsink_tile[0, :N_HEADS]}
    ((g_vec,),) = _comm_only("gather_small_grads", [_direct_gather_task([_pack_small_grads(small_grads, loss_local)])])
    given = {"ffn1_norm": (ffn1_norm, m_ffn1_norm, v_ffn1_norm), "mix_norm": (mix_norm, m_mix_norm, v_mix_norm),
             "ffn2_norm": (ffn2_norm, m_ffn2_norm, v_ffn2_norm), "gate_bias": (gate_bias, m_gate_bias, v_gate_bias),
             "pool_scale": (pool_scale, m_pool_scale, v_pool_scale), "q_norm": (q_norm, m_q_norm, v_q_norm),
             "k_norm": (k_norm, m_k_norm, v_k_norm), "sinks": (sinks, m_sinks, v_sinks)}
    params = [tuple(a.reshape(shape) for a in given[nm]) for nm, _, shape in SMALL_LAYOUT]
    params.append(tuple(a.reshape(-1, LANES) for a in (pool_w, m_pool_w, v_pool_w)))
    small_res, loss_row = _adamw_small("adamw_small", g_vec.reshape(N_DEV, SMALL_ROWS, LANES),
                                       g_pool_w.reshape(N_DEV, -1, LANES), params)
    small_out = {nm: tuple(r.reshape(given[nm][0].shape) for r in res)
                 for (nm, _, _), res in zip(SMALL_LAYOUT, small_res)}
    small_out["pool_w"] = tuple(r.reshape(pool_w.shape) for r in small_res[-1])
    loss = loss_row[0, 0]

    order = ["ffn1_norm", "ffn1_w_gate", "ffn1_w_up", "ffn1_w_down", "mix_norm", "w_in", "pool_w", "pool_scale",
             "w_pool_out", "q_norm", "k_norm", "sinks", "w_attn_out", "gate_bias", "w_out", "ffn2_norm",
             "ffn2_w_gate", "ffn2_w_up", "ffn2_w_down"]
    every = {**big_out, **small_out}
    outs = [loss, dx.reshape(x.shape)]
    for j in range(4):
        outs += [every[nm][j] for nm in order]
    return tuple(outs)
```
